```python
import jax, jax.numpy as jnp
from jax import lax
import numpy as np

D_MODEL = 1024
BATCH = 8
SEQ = 2048
DEPTH = 2

POOL_WINDOWS = (2, 4, 8, 16)
POOL_GROUPS = len(POOL_WINDOWS)
POOL_GROUP_DIM = D_MODEL // 8
POOL_WIDTH = POOL_GROUPS * POOL_GROUP_DIM
HGRN_HEAD_DIM = 128
HGRN_HEADS = D_MODEL // HGRN_HEAD_DIM
HGRN_WIDTH = HGRN_HEADS * HGRN_HEAD_DIM
CHUNK = 64
NORM_EPS = 1e-6
IN_SIZES = (POOL_WIDTH, POOL_WIDTH, HGRN_WIDTH, HGRN_WIDTH, HGRN_WIDTH, HGRN_WIDTH, D_MODEL, D_MODEL)
IN_WIDTH = sum(IN_SIZES)

kernel_name = "hybrid_pool_hgrn2_gated_block"


def rms_norm(x, g):
    xf = x.astype(jnp.float32)
    y = xf * lax.rsqrt(jnp.mean(xf * xf, axis=-1, keepdims=True) + NORM_EPS)
    return (y * g.astype(jnp.float32)).astype(x.dtype)


def multiscale_pool(u):
    b, s, _ = u.shape
    uf = u.astype(jnp.float32)
    csum = lax.cumsum(uf, axis=1)
    pos = jnp.arange(s, dtype=jnp.float32) + 1.0
    outs = []
    for gi, w in enumerate(POOL_WINDOWS):
        sl = slice(gi * POOL_GROUP_DIM, (gi + 1) * POOL_GROUP_DIM)
        cg = csum[:, :, sl]
        prev = jnp.pad(cg, ((0, 0), (w, 0), (0, 0)))[:, :s]
        count = jnp.minimum(pos, float(w))[None, :, None]
        outs.append((cg - prev) / count - uf[:, :, sl])
    return jnp.stack(outs, axis=2)


def _hgrn2_chunk_step(state, inp):
    q, k, v, logf = inp
    cum = jnp.cumsum(logf, axis=2)
    o_inter = jnp.einsum('bhtk,bhkv->bhtv', q * jnp.exp(cum), state)
    c = q.shape[2]
    causal = jnp.tril(jnp.ones((c, c), dtype=bool))[:, :, None]
    diff = cum[:, :, :, None, :] - cum[:, :, None, :, :]
    decay = jnp.where(causal, jnp.exp(jnp.minimum(diff, 0.0)), 0.0)
    scores = jnp.sum(q[:, :, :, None, :] * k[:, :, None, :, :] * decay, axis=-1)
    o = o_inter + jnp.einsum('bhts,bhsv->bhtv', scores, v)
    last = cum[:, :, -1:, :]
    new_state = (jnp.exp(last[:, :, 0, :])[..., None] * state
                 + jnp.einsum('bhsk,bhsv->bhkv', k * jnp.exp(last - cum), v))
    return new_state, o


def hgrn2(q, k, v, logf):
    b, s, h, dk = q.shape
    n = s // CHUNK

    def to_chunks(t):
        return t.reshape(b, n, CHUNK, h, t.shape[-1]).transpose(1, 0, 3, 2, 4)

    state0 = jnp.zeros((b, h, dk, v.shape[-1]), jnp.float32)
    _, o = lax.scan(_hgrn2_chunk_step, state0,
                    (to_chunks(q), to_chunks(k), to_chunks(v), to_chunks(logf)))
    return o.transpose(1, 0, 3, 2, 4).reshape(b, s, h, v.shape[-1])


def _fwd_setup_inputs(seed: int = 0) -> dict:
    key = jax.random.key(seed)
    ks = jax.random.split(key, 16)
    L, D = DEPTH, D_MODEL
    nrm = jax.random.normal
    return {
        "x": nrm(ks[0], (BATCH, SEQ, D), jnp.float32),
        "c": nrm(ks[1], (BATCH, D), jnp.float32),
        "w_ada": nrm(ks[2], (L, D, 3 * D), jnp.float32) * (0.5 * D ** -0.5),
        "b_ada": nrm(ks[3], (L, 3 * D), jnp.float32) * 0.02,
        "g_pre": 1.0 + 0.02 * nrm(ks[4], (L, D), jnp.float32),
        "g_post": 1.0 + 0.02 * nrm(ks[5], (L, D), jnp.float32),
        "w_in": nrm(ks[6], (L, D, IN_WIDTH), jnp.float32) * D ** -0.5,
        "pool_w": nrm(ks[7], (L, POOL_GROUPS, POOL_GROUP_DIM, POOL_GROUP_DIM), jnp.float32) * POOL_GROUP_DIM ** -0.5,
        "pool_scale": 1.0 + 0.02 * nrm(ks[8], (L, POOL_WIDTH), jnp.float32),
        "lb_logits": nrm(ks[9], (L, HGRN_WIDTH), jnp.float32),
        "hgrn_norm_g": 1.0 + 0.02 * nrm(ks[10], (L, HGRN_HEAD_DIM), jnp.float32),
        "w_pool_o": nrm(ks[11], (L, POOL_WIDTH, D), jnp.float32) * POOL_WIDTH ** -0.5,
        "w_hgrn_o": nrm(ks[12], (L, HGRN_WIDTH, D), jnp.float32) * HGRN_WIDTH ** -0.5,
        "w_out": nrm(ks[13], (L, D, D), jnp.float32) * D ** -0.5,
    }


def _fwd_reference(x, c, w_ada, b_ada, g_pre, g_post, w_in, pool_w, pool_scale, lb_logits,
              hgrn_norm_g, w_pool_o, w_hgrn_o, w_out):
    b, s, d = x.shape
    p = jax.nn.softmax(lb_logits.astype(jnp.float32), axis=0)
    lower_bounds = jnp.cumsum(p, axis=0) - p[0:1]
    split_idx = np.cumsum(IN_SIZES)[:-1].tolist()
    c_act = jax.nn.silu(c)
    for l in range(DEPTH):
        ada = c_act @ w_ada[l] + b_ada[l]
        shift, scale, gate = jnp.split(ada[:, None, :], 3, axis=-1)
        h = rms_norm(x, g_pre[l]) * (1.0 + scale) + shift
        (pv, pg, hq, hf, hi, hg, mg_pool, mg_hgrn) = jnp.split(h @ w_in[l], split_idx, axis=-1)

        pooled = multiscale_pool(pv)
        pooled = jnp.einsum('bsgc,gcd->bsgd', pooled, pool_w[l].astype(jnp.float32))
        pooled = pooled.reshape(b, s, POOL_WIDTH) * pool_scale[l]
        branch_a = (pooled.astype(x.dtype) * jax.nn.silu(pg)) @ w_pool_o[l]

        shp = (b, s, HGRN_HEADS, HGRN_HEAD_DIM)
        lb = jnp.clip(lower_bounds[l], 0.0, 1.0).reshape(HGRN_HEADS, HGRN_HEAD_DIM)
        zf = hf.astype(jnp.float32).reshape(shp)
        f = lb + (1.0 - lb) * jax.nn.sigmoid(zf)
        logf = jnp.log(jnp.maximum(f, 1e-30))
        k = 1.0 - f
        q = jax.nn.silu(hq.astype(jnp.float32)).reshape(shp)
        v = hi.astype(jnp.float32).reshape(shp)
        o = hgrn2(q, k, v, logf)
        o = rms_norm(o, hgrn_norm_g[l]).astype(x.dtype).reshape(b, s, HGRN_WIDTH)
        branch_b = (o * jax.nn.silu(hg)) @ w_hgrn_o[l]

        merged = jax.nn.sigmoid(mg_pool) * branch_a + jax.nn.sigmoid(mg_hgrn) * branch_b
        y = merged @ w_out[l]
        x = x + gate * rms_norm(y, g_post[l])
    return x


import jax as _jax
import jax.numpy as _jnp

TWIN_FORMAT = 'train_step'
FWD_PARAMS = ['x', 'c', 'w_ada', 'b_ada', 'g_pre', 'g_post', 'w_in', 'pool_w', 'pool_scale', 'lb_logits', 'hgrn_norm_g', 'w_pool_o', 'w_hgrn_o', 'w_out']
TWIN_WEIGHTS = ['w_ada', 'b_ada', 'g_pre', 'g_post', 'w_in', 'pool_w', 'pool_scale', 'lb_logits', 'hgrn_norm_g', 'w_pool_o', 'w_hgrn_o', 'w_out']
TWIN_DIFF_INPUT = 'x'
TWIN_INPUTS = ['x', 'c', 'w_ada', 'b_ada', 'g_pre', 'g_post', 'w_in', 'pool_w', 'pool_scale', 'lb_logits', 'hgrn_norm_g', 'w_pool_o', 'w_hgrn_o', 'w_out', 'loss_target', 'm_w_ada', 'm_b_ada', 'm_g_pre', 'm_g_post', 'm_w_in', 'm_pool_w', 'm_pool_scale', 'm_lb_logits', 'm_hgrn_norm_g', 'm_w_pool_o', 'm_w_hgrn_o', 'm_w_out', 'v_w_ada', 'v_b_ada', 'v_g_pre', 'v_g_post', 'v_w_in', 'v_pool_w', 'v_pool_scale', 'v_lb_logits', 'v_hgrn_norm_g', 'v_w_pool_o', 'v_w_hgrn_o', 'v_w_out']
TWIN_OUTPUTS = ['loss', 'grad_x', 'grad_w_ada', 'grad_b_ada', 'grad_g_pre', 'grad_g_post', 'grad_w_in', 'grad_pool_w', 'grad_pool_scale', 'grad_lb_logits', 'grad_hgrn_norm_g', 'grad_w_pool_o', 'grad_w_hgrn_o', 'grad_w_out', 'delta_w_ada', 'delta_b_ada', 'delta_g_pre', 'delta_g_post', 'delta_w_in', 'delta_pool_w', 'delta_pool_scale', 'delta_lb_logits', 'delta_hgrn_norm_g', 'delta_w_pool_o', 'delta_w_hgrn_o', 'delta_w_out', 'new_m_w_ada', 'new_m_b_ada', 'new_m_g_pre', 'new_m_g_post', 'new_m_w_in', 'new_m_pool_w', 'new_m_pool_scale', 'new_m_lb_logits', 'new_m_hgrn_norm_g', 'new_m_w_pool_o', 'new_m_w_hgrn_o', 'new_m_w_out', 'new_v_w_ada', 'new_v_b_ada', 'new_v_g_pre', 'new_v_g_post', 'new_v_w_in', 'new_v_pool_w', 'new_v_pool_scale', 'new_v_lb_logits', 'new_v_hgrn_norm_g', 'new_v_w_pool_o', 'new_v_w_hgrn_o', 'new_v_w_out']
TWIN_LEAF_KINDS = {'loss': 'loss', 'grad_x': 'grad_x', 'grad_w_ada': 'grad_w', 'grad_b_ada': 'grad_w', 'grad_g_pre': 'grad_w', 'grad_g_post': 'grad_w', 'grad_w_in': 'grad_w', 'grad_pool_w': 'grad_w', 'grad_pool_scale': 'grad_w', 'grad_lb_logits': 'grad_w', 'grad_hgrn_norm_g': 'grad_w', 'grad_w_pool_o': 'grad_w', 'grad_w_hgrn_o': 'grad_w', 'grad_w_out': 'grad_w', 'delta_w_ada': 'delta_w', 'delta_b_ada': 'delta_w', 'delta_g_pre': 'delta_w', 'delta_g_post': 'delta_w', 'delta_w_in': 'delta_w', 'delta_pool_w': 'delta_w', 'delta_pool_scale': 'delta_w', 'delta_lb_logits': 'delta_w', 'delta_hgrn_norm_g': 'delta_w', 'delta_w_pool_o': 'delta_w', 'delta_w_hgrn_o': 'delta_w', 'delta_w_out': 'delta_w', 'new_m_w_ada': 'new_m', 'new_m_b_ada': 'new_m', 'new_m_g_pre': 'new_m', 'new_m_g_post': 'new_m', 'new_m_w_in': 'new_m', 'new_m_pool_w': 'new_m', 'new_m_pool_scale': 'new_m', 'new_m_lb_logits': 'new_m', 'new_m_hgrn_norm_g': 'new_m', 'new_m_w_pool_o': 'new_m', 'new_m_w_hgrn_o': 'new_m', 'new_m_w_out': 'new_m', 'new_v_w_ada': 'new_v', 'new_v_b_ada': 'new_v', 'new_v_g_pre': 'new_v', 'new_v_g_post': 'new_v', 'new_v_w_in': 'new_v', 'new_v_pool_w': 'new_v', 'new_v_pool_scale': 'new_v', 'new_v_lb_logits': 'new_v', 'new_v_hgrn_norm_g': 'new_v', 'new_v_w_pool_o': 'new_v', 'new_v_w_hgrn_o': 'new_v', 'new_v_w_out': 'new_v'}


def _forward(args):
    return _fwd_reference(*[args[k] for k in FWD_PARAMS])


def _output_shape():
    out = _jax.eval_shape(lambda: _forward(_fwd_setup_inputs(0)))
    return out.shape, out.dtype

N_MICROBATCH = 1
ADAM_LR = 0.001
ADAM_B1 = 0.9
ADAM_B2 = 0.999
ADAM_EPS = 1e-08
ADAM_WD = 0.01
ADAM_STEP = 10
PER_EXAMPLE_BATCH_AXIS = {'x': 0, 'c': 0, 'loss_target': 0}
SHARED_INPUTS = []
_WEIGHT_DTYPES = {'w_ada': _jnp.float32, 'b_ada': _jnp.float32, 'g_pre': _jnp.float32, 'g_post': _jnp.float32, 'w_in': _jnp.float32, 'pool_w': _jnp.float32, 'pool_scale': _jnp.float32, 'lb_logits': _jnp.float32, 'hgrn_norm_g': _jnp.float32, 'w_pool_o': _jnp.float32, 'w_hgrn_o': _jnp.float32, 'w_out': _jnp.float32}
MOMENT_SCALE = {'w_ada': 8.832286e-01, 'b_ada': 1.617589e+00, 'g_pre': 8.872146e-02, 'g_post': 1.885165e+00, 'w_in': 3.919362e-02, 'pool_w': 6.214815e-02, 'pool_scale': 6.387186e-02, 'lb_logits': 2.964279e-03, 'hgrn_norm_g': 2.049185e-01, 'w_pool_o': 4.652122e-02, 'w_hgrn_o': 5.829345e-02, 'w_out': 7.803613e-02}


def _to_microbatches(a, axis):
    t = _jnp.moveaxis(a, axis, 0)
    t = t.reshape((N_MICROBATCH, t.shape[0] // N_MICROBATCH) + t.shape[1:])
    return _jnp.moveaxis(t, 1, axis + 1)


def setup_inputs(seed: int = 0) -> dict:
    inp = _fwd_setup_inputs(seed)
    key = _jax.random.fold_in(_jax.random.key(seed), 7919)
    shape, _ = _output_shape()
    out = dict(inp)
    out["loss_target"] = _jax.random.normal(_jax.random.fold_in(key, 0), shape, _jnp.float32)
    for i, name in enumerate(TWIN_WEIGHTS):
        w = inp[name].astype(_jnp.float32)
        if MOMENT_SCALE is None:
            s = _jnp.sqrt(_jnp.mean(_jnp.square(w)) + 1e-30)
        else:
            s = MOMENT_SCALE[name]
        km, kv = _jax.random.split(_jax.random.fold_in(key, i + 1))
        out[name] = w
        out["m_" + name] = s * _jax.random.normal(km, w.shape, _jnp.float32)
        out["v_" + name] = (s * s) * _jax.random.uniform(kv, w.shape, _jnp.float32, 0.5, 1.5)
    if N_MICROBATCH > 1:
        for name, axis in PER_EXAMPLE_BATCH_AXIS.items():
            out[name] = _to_microbatches(out[name], axis)
    return {'x': out['x'], 'c': out['c'], 'w_ada': out['w_ada'], 'b_ada': out['b_ada'], 'g_pre': out['g_pre'], 'g_post': out['g_post'], 'w_in': out['w_in'], 'pool_w': out['pool_w'], 'pool_scale': out['pool_scale'], 'lb_logits': out['lb_logits'], 'hgrn_norm_g': out['hgrn_norm_g'], 'w_pool_o': out['w_pool_o'], 'w_hgrn_o': out['w_hgrn_o'], 'w_out': out['w_out'], 'loss_target': out['loss_target'], 'm_w_ada': out['m_w_ada'], 'm_b_ada': out['m_b_ada'], 'm_g_pre': out['m_g_pre'], 'm_g_post': out['m_g_post'], 'm_w_in': out['m_w_in'], 'm_pool_w': out['m_pool_w'], 'm_pool_scale': out['m_pool_scale'], 'm_lb_logits': out['m_lb_logits'], 'm_hgrn_norm_g': out['m_hgrn_norm_g'], 'm_w_pool_o': out['m_w_pool_o'], 'm_w_hgrn_o': out['m_w_hgrn_o'], 'm_w_out': out['m_w_out'], 'v_w_ada': out['v_w_ada'], 'v_b_ada': out['v_b_ada'], 'v_g_pre': out['v_g_pre'], 'v_g_post': out['v_g_post'], 'v_w_in': out['v_w_in'], 'v_pool_w': out['v_pool_w'], 'v_pool_scale': out['v_pool_scale'], 'v_lb_logits': out['v_lb_logits'], 'v_hgrn_norm_g': out['v_hgrn_norm_g'], 'v_w_pool_o': out['v_w_pool_o'], 'v_w_hgrn_o': out['v_w_hgrn_o'], 'v_w_out': out['v_w_out']}


def _loss(weights, diff, rest, loss_target):
    with _jax.named_scope("forward"):
        args = {**rest, TWIN_DIFF_INPUT: diff, **{k: w.astype(_WEIGHT_DTYPES[k]) for k, w in weights.items()}}
        y = _forward(args)
    with _jax.named_scope("loss_head"):
        err = _jnp.square(y.astype(_jnp.float32) - loss_target)
        return 0.5 * _jnp.sum(_jnp.mean(err, axis=-1)) if err.ndim else 0.5 * err


def _adamw(w, g, m, v):
    m = ADAM_B1 * m + (1.0 - ADAM_B1) * g
    v = ADAM_B2 * v + (1.0 - ADAM_B2) * _jnp.square(g)
    m_hat = m / (1.0 - ADAM_B1 ** ADAM_STEP)
    v_hat = v / (1.0 - ADAM_B2 ** ADAM_STEP)
    delta = -ADAM_LR * (m_hat / (_jnp.sqrt(v_hat) + ADAM_EPS) + ADAM_WD * w)
    return delta, m, v


def reference(x, c, w_ada, b_ada, g_pre, g_post, w_in, pool_w, pool_scale, lb_logits, hgrn_norm_g, w_pool_o, w_hgrn_o, w_out, loss_target, m_w_ada, m_b_ada, m_g_pre, m_g_post, m_w_in, m_pool_w, m_pool_scale, m_lb_logits, m_hgrn_norm_g, m_w_pool_o, m_w_hgrn_o, m_w_out, v_w_ada, v_b_ada, v_g_pre, v_g_post, v_w_in, v_pool_w, v_pool_scale, v_lb_logits, v_hgrn_norm_g, v_w_pool_o, v_w_hgrn_o, v_w_out):
    given = dict(x=x, c=c, w_ada=w_ada, b_ada=b_ada, g_pre=g_pre, g_post=g_post, w_in=w_in, pool_w=pool_w, pool_scale=pool_scale, lb_logits=lb_logits, hgrn_norm_g=hgrn_norm_g, w_pool_o=w_pool_o, w_hgrn_o=w_hgrn_o, w_out=w_out, loss_target=loss_target, m_w_ada=m_w_ada, m_b_ada=m_b_ada, m_g_pre=m_g_pre, m_g_post=m_g_post, m_w_in=m_w_in, m_pool_w=m_pool_w, m_pool_scale=m_pool_scale, m_lb_logits=m_lb_logits, m_hgrn_norm_g=m_hgrn_norm_g, m_w_pool_o=m_w_pool_o, m_w_hgrn_o=m_w_hgrn_o, m_w_out=m_w_out, v_w_ada=v_w_ada, v_b_ada=v_b_ada, v_g_pre=v_g_pre, v_g_post=v_g_post, v_w_in=v_w_in, v_pool_w=v_pool_w, v_pool_scale=v_pool_scale, v_lb_logits=v_lb_logits, v_hgrn_norm_g=v_hgrn_norm_g, v_w_pool_o=v_w_pool_o, v_w_hgrn_o=v_w_hgrn_o, v_w_out=v_w_out)
    weights = {n: given[n] for n in TWIN_WEIGHTS}
    shared = {n: given[n] for n in SHARED_INPUTS}
    per_example = {n: given[n] for n in ['x', 'c']}
    grad_fn = _jax.value_and_grad(_loss, argnums=(0, 1))

    def one_microbatch(ex, loss_target):
        ex = dict(ex)
        diff = ex.pop(TWIN_DIFF_INPUT)
        return grad_fn(weights, diff, {**shared, **ex}, loss_target)

    if N_MICROBATCH == 1:
        loss, (grad_w, grad_x) = one_microbatch(per_example, given["loss_target"])
    else:
        def body(carry, xs):
            loss_sum, grad_sum = carry
            l_k, (gw_k, gx_k) = one_microbatch(xs[0], xs[1])
            with _jax.named_scope("update"):
                return (loss_sum + l_k, _jax.tree.map(_jnp.add, grad_sum, gw_k)), gx_k

        init = (_jnp.zeros((), _jnp.float32), _jax.tree.map(_jnp.zeros_like, weights))
        (loss, grad_w), grad_x = _jax.lax.scan(body, init, (per_example, given["loss_target"]))
    with _jax.named_scope("update"):
        delta_w, new_m, new_v = {}, {}, {}
        for n in TWIN_WEIGHTS:
            delta_w[n], new_m[n], new_v[n] = _adamw(weights[n], grad_w[n], given["m_" + n], given["v_" + n])
    return (loss, grad_x, *[grad_w[n] for n in TWIN_WEIGHTS], *[delta_w[n] for n in TWIN_WEIGHTS],
            *[new_m[n] for n in TWIN_WEIGHTS], *[new_v[n] for n in TWIN_WEIGHTS])
```

```python
import jax
import jax.numpy as jnp
from jax import lax
from jax.experimental import pallas as pl
from jax.experimental.pallas import tpu as pltpu

F32 = jnp.float32
MXU_DTYPE = jnp.bfloat16
WIRE_DTYPE = jnp.bfloat16

N_DEV = 8
DEPTH = 2
D_MODEL = 1024
HEADS = 8
HEAD_DIM = 128
POOL_GROUPS = 4
GROUP_DIM = 128
POOL_WIDTH = POOL_GROUPS * GROUP_DIM
IN_WIDTH = 7168
CHUNK = 64
SUB = 16
N_SUB = CHUNK // SUB
EXP_CLAMP = 80.0
NORM_EPS = 1e-6
LOG_FLOOR = 1e-30
ADA_COLS = 3 * D_MODEL // N_DEV
IN_COLS = IN_WIDTH // N_DEV
COL_HQ, COL_HF, COL_HI, COL_HG, COL_MGP, COL_MGH = 1, 2, 3, 4, 5, 6

ADAM_LR = 0.001
ADAM_B1 = 0.9
ADAM_B2 = 0.999
ADAM_EPS = 1e-08
ADAM_WD = 0.01
ADAM_STEP = 10

VMEM_LIMIT = 48 * 1024 * 1024
MESH_ID = pl.DeviceIdType.MESH
HIGHEST = lax.Precision.HIGHEST

_SMALL_ROWS = (("b_ada", 48), ("g_pre", 16), ("g_post", 16), ("pool_w", 1024), ("pool_scale", 8),
               ("lb_logits", 16), ("hgrn_norm_g", 2))
SMALL_ROWS_PAD = 1136
LB_ROW0 = 48 + 16 + 16 + 1024 + 8


def _params(**kw):
    return pltpu.CompilerParams(vmem_limit_bytes=VMEM_LIMIT, **kw)


def _sigmoid(v):
    return 1.0 / (1.0 + jnp.exp(-v))


def _dsilu(v, s):
    return s * (1.0 + v * (1.0 - s))


def _dot(a, b):
    return jnp.dot(a.astype(MXU_DTYPE), b.astype(MXU_DTYPE), preferred_element_type=F32)


def _dot_nt(a, b):
    return lax.dot_general(a.astype(MXU_DTYPE), b.astype(MXU_DTYPE), (((1,), (1,)), ((), ())),
                           preferred_element_type=F32)


def _dot_tn(a, b):
    return lax.dot_general(a.astype(MXU_DTYPE), b.astype(MXU_DTYPE), (((0,), (0,)), ((), ())),
                           preferred_element_type=F32)


def _my_position():
    mx, my, mc = lax.axis_index("x"), lax.axis_index("y"), lax.axis_index("c")
    return mx, my, mc, 4 * mx + 2 * my + mc


def _peer(mx, my, mc, k):
    px = 1 - mx if (k >> 2) & 1 else mx
    py = 1 - my if (k >> 1) & 1 else my
    pc = 1 - mc if k & 1 else mc
    return (px, py, pc), 4 * px + 2 * py + pc


def _allgather_small(v, name):
    rows, cols = v.shape

    def body(v_ref, out_ref, send_sems, recv_sems):
        mx, my, mc, me = _my_position()
        out_ref[me] = v_ref[...]
        copies = []
        for k in range(1, N_DEV):
            peer, _ = _peer(mx, my, mc, k)
            cp = pltpu.make_async_remote_copy(
                src_ref=v_ref, dst_ref=out_ref.at[me],
                send_sem=send_sems.at[k - 1], recv_sem=recv_sems.at[k - 1],
                device_id=peer, device_id_type=MESH_ID)
            cp.start()
            copies.append(cp)
        for cp in copies:
            cp.wait()

    return pl.pallas_call(
        body, name=name,
        out_shape=jax.ShapeDtypeStruct((N_DEV, rows, cols), v.dtype),
        in_specs=[pl.BlockSpec(memory_space=pltpu.VMEM)],
        out_specs=pl.BlockSpec(memory_space=pltpu.VMEM),
        scratch_shapes=[pltpu.SemaphoreType.DMA((N_DEV - 1,)), pltpu.SemaphoreType.DMA((N_DEV - 1,))],
        compiler_params=_params(),
    )(v)


def _allgather_weights(shards):
    n = len(shards)

    def body(*refs):
        ins, outs = refs[:n], refs[n:2 * n]
        send_sems, recv_sems, local_sems = refs[2 * n:]
        mx, my, mc, me = _my_position()
        copies = []
        for w in range(n):
            for l in range(DEPTH):
                cp = pltpu.make_async_copy(ins[w].at[l], outs[w].at[l, me], local_sems.at[w, l])
                cp.start()
                copies.append(cp)
        for k in range(1, N_DEV):
            peer, _ = _peer(mx, my, mc, k)
            for w in range(n):
                for l in range(DEPTH):
                    cp = pltpu.make_async_remote_copy(
                        src_ref=ins[w].at[l], dst_ref=outs[w].at[l, me],
                        send_sem=send_sems.at[w, l, k - 1], recv_sem=recv_sems.at[w, l, k - 1],
                        device_id=peer, device_id_type=MESH_ID)
                    cp.start()
                    copies.append(cp)
        for cp in copies:
            cp.wait()

    hbm = pl.BlockSpec(memory_space=pl.ANY)
    return pl.pallas_call(
        body, name="allgather_weights",
        out_shape=[jax.ShapeDtypeStruct((DEPTH, N_DEV) + s.shape[1:], s.dtype) for s in shards],
        in_specs=[hbm] * n, out_specs=[hbm] * n,
        scratch_shapes=[pltpu.SemaphoreType.DMA((n, DEPTH, N_DEV - 1)),
                        pltpu.SemaphoreType.DMA((n, DEPTH, N_DEV - 1)),
                        pltpu.SemaphoreType.DMA((n, DEPTH))],
        compiler_params=_params(),
    )(*shards)


def _scatter_grads(parts):
    n = len(parts)
    flat = [p for per_layer in parts for p in per_layer]

    def body(*refs):
        ins = [refs[w * DEPTH:(w + 1) * DEPTH] for w in range(n)]
        outs = refs[n * DEPTH:n * DEPTH + n]
        send_sems, recv_sems, local_sems = refs[n * DEPTH + n:]
        mx, my, mc, me = _my_position()
        copies = []
        for w in range(n):
            for l in range(DEPTH):
                cp = pltpu.make_async_copy(ins[w][l].at[me], outs[w].at[l, me], local_sems.at[w, l])
                cp.start()
                copies.append(cp)
        for k in range(1, N_DEV):
            peer, pid = _peer(mx, my, mc, k)
            for w in range(n):
                for l in range(DEPTH):
                    cp = pltpu.make_async_remote_copy(
                        src_ref=ins[w][l].at[pid], dst_ref=outs[w].at[l, me],
                        send_sem=send_sems.at[w, l, k - 1], recv_sem=recv_sems.at[w, l, k - 1],
                        device_id=peer, device_id_type=MESH_ID)
                    cp.start()
                    copies.append(cp)
        for cp in copies:
            cp.wait()

    hbm = pl.BlockSpec(memory_space=pl.ANY)
    return pl.pallas_call(
        body, name="scatter_grads",
        out_shape=[jax.ShapeDtypeStruct((DEPTH,) + per_layer[0].shape, per_layer[0].dtype) for per_layer in parts],
        in_specs=[hbm] * (n * DEPTH), out_specs=[hbm] * n,
        scratch_shapes=[pltpu.SemaphoreType.DMA((n, DEPTH, N_DEV - 1)),
                        pltpu.SemaphoreType.DMA((n, DEPTH, N_DEV - 1)),
                        pltpu.SemaphoreType.DMA((n, DEPTH))],
        compiler_params=_params(),
    )(*flat)


def _ada_fwd(c_all, w_ada, b_cols):
    def body(c_ref, w_ref, b_ref, out_ref):
        cv = c_ref[...]
        ca = cv * _sigmoid(cv)
        for l in range(DEPTH):
            out_ref[l] = jnp.dot(ca, w_ref[l], precision=HIGHEST, preferred_element_type=F32) + b_ref[l:l + 1, :]

    return pl.pallas_call(
        body, name="ada_fwd",
        out_shape=jax.ShapeDtypeStruct((DEPTH, N_DEV, ADA_COLS), F32),
        compiler_params=_params(),
    )(c_all, w_ada, b_cols)


def _ada_bwd(c_all, d_cols):
    def body(c_ref, d_ref, out_ref):
        cv = c_ref[...]
        ca = cv * _sigmoid(cv)
        for l in range(DEPTH):
            out_ref[l] = lax.dot_general(ca, d_ref[l], (((0,), (0,)), ((), ())), precision=HIGHEST,
                                         preferred_element_type=F32)

    return pl.pallas_call(
        body, name="ada_bwd",
        out_shape=jax.ShapeDtypeStruct((DEPTH, D_MODEL, ADA_COLS), F32),
        compiler_params=_params(),
    )(c_all, d_cols)


def _lower_bounds(logits):
    m = jnp.maximum(logits[0:1], logits[1:2])
    e0, e1 = jnp.exp(logits[0:1] - m), jnp.exp(logits[1:2] - m)
    den = e0 + e1
    p0, p1 = e0 / den, e1 / den
    low0 = p0 - p0
    low1 = (p0 + p1) - p0
    return (p0, p1), (low0, low1)


def _lb_fwd(lb_logits):
    def body(lg_ref, out_ref):
        _, (low0, low1) = _lower_bounds(lg_ref[...])
        out_ref[0:1, :] = jnp.clip(low0, 0.0, 1.0)
        out_ref[1:2, :] = jnp.clip(low1, 0.0, 1.0)

    return pl.pallas_call(body, name="lb_fwd", out_shape=jax.ShapeDtypeStruct(lb_logits.shape, F32),
                          compiler_params=_params())(lb_logits)


def _row_spec(cols=D_MODEL):
    return pl.BlockSpec((1, cols), lambda *_: (0, 0))


def _prenorm_fwd(x, g, shift, scale, tm, name):
    seq = x.shape[0]

    def body(x_ref, g_ref, sh_ref, sc_ref, h_ref):
        xv = x_ref[...]
        rs = lax.rsqrt(jnp.mean(xv * xv, axis=-1, keepdims=True) + NORM_EPS)
        h = (xv * rs * g_ref[...]) * (1.0 + sc_ref[...]) + sh_ref[...]
        h_ref[...] = h.astype(h_ref.dtype)

    tile = pl.BlockSpec((tm, D_MODEL), lambda i: (i, 0))
    return pl.pallas_call(
        body, name=name, grid=(seq // tm,),
        in_specs=[tile, _row_spec(), _row_spec(), _row_spec()], out_specs=tile,
        out_shape=jax.ShapeDtypeStruct((seq, D_MODEL), MXU_DTYPE),
        compiler_params=_params(dimension_semantics=("parallel",)),
    )(x, g, shift, scale)


def _in_proj(h, win_g, l, tm, name):
    seq = h.shape[0]

    def body(h_ref, w_ref, z_ref):
        z_ref[...] = jnp.dot(h_ref[...], w_ref[...], preferred_element_type=F32)

    return pl.pallas_call(
        body, name=name, grid=(N_DEV, seq // tm),
        in_specs=[pl.BlockSpec((tm, D_MODEL), lambda j, i: (i, 0)),
                  pl.BlockSpec((None, None, D_MODEL, IN_COLS), lambda j, i: (l, j, 0, 0))],
        out_specs=pl.BlockSpec((tm, IN_COLS), lambda j, i: (i, j)),
        out_shape=jax.ShapeDtypeStruct((seq, IN_WIDTH), F32),
        compiler_params=_params(dimension_semantics=("parallel", "parallel")),
    )(h, win_g)


def _shift_down(v, j, pos):
    return jnp.where(pos >= j, pltpu.roll(v, j, 0), 0.0)


def _shift_up(v, j, pos, seq):
    return jnp.where(pos < seq - j, pltpu.roll(v, seq - j, 0), 0.0)


def _select_window(g, candidates):
    out = candidates[-1]
    for i in range(len(candidates) - 2, -1, -1):
        out = jnp.where(g == i, candidates[i], out)
    return out


def _pool_mean_minus_token(u, g, pos):
    sums, acc = [], u
    for j in (1, 2, 4, 8):
        acc = acc + _shift_down(acc, j, pos)
        sums.append(acc)
    wsum = _select_window(g, sums)
    width = jnp.left_shift(2, g).astype(F32)
    count = jnp.minimum(pos.astype(F32) + 1.0, width)
    return wsum / count - u, count


def _pool_fwd(z, pool_w_l, pool_scale_l, name):
    seq = z.shape[0]

    def body(pv_ref, pg_ref, w_ref, sc_ref, out_ref):
        g = pl.program_id(0)
        pos = lax.broadcasted_iota(jnp.int32, (seq, GROUP_DIM), 0)
        pm, _ = _pool_mean_minus_token(pv_ref[...], g, pos)
        lin = _dot(pm, w_ref[...]) * sc_ref[...]
        pg = pg_ref[...]
        out_ref[...] = (lin * (pg * _sigmoid(pg))).astype(out_ref.dtype)

    return pl.pallas_call(
        body, name=name, grid=(POOL_GROUPS,),
        in_specs=[pl.BlockSpec((seq, GROUP_DIM), lambda g: (0, g)),
                  pl.BlockSpec((seq, GROUP_DIM), lambda g: (0, POOL_GROUPS + g)),
                  pl.BlockSpec((None, GROUP_DIM, GROUP_DIM), lambda g: (g, 0, 0)),
                  pl.BlockSpec((1, GROUP_DIM), lambda g: (0, g))],
        out_specs=pl.BlockSpec((seq, GROUP_DIM), lambda g: (0, g)),
        out_shape=jax.ShapeDtypeStruct((seq, POOL_WIDTH), MXU_DTYPE),
        compiler_params=_params(dimension_semantics=("parallel",)),
    )(z, z, pool_w_l, pool_scale_l)


def _chunk_masks():
    row = lax.broadcasted_iota(jnp.int32, (CHUNK, CHUNK), 0)
    col = lax.broadcasted_iota(jnp.int32, (CHUNK, CHUNK), 1)
    causal = row >= col
    tri = causal.astype(F32)
    before_sub = (col < (row // SUB) * SUB).astype(F32)
    return causal, tri, before_sub


def _gates(zf, lb):
    sg = _sigmoid(zf)
    f = lb + (1.0 - lb) * sg
    logf = jnp.log(jnp.maximum(f, LOG_FLOOR))
    return sg, f, logf


def _intra_blocks(q_h, k_h, cum_h, base_h, causal):
    rel = cum_h - base_h
    out = []
    for i in range(N_SUB):
        rows = slice(i * SUB, (i + 1) * SUB)
        e_q = jnp.exp(rel[rows])
        base_i = jnp.concatenate([base_h[rows]] * N_SUB, axis=0)
        e_k = jnp.exp(jnp.minimum(base_i - cum_h, EXP_CLAMP))
        q_t = q_h[rows] * e_q
        k_t = k_h * e_k
        a_i = jnp.where(causal[rows], _dot_nt(q_t, k_t), 0.0)
        out.append((q_t, k_t, e_q, e_k, a_i))
    return out


def _hgrn_fwd(z, lb_l, gn_l, name):
    seq = z.shape[0]
    n_chunks = seq // CHUNK

    def body(hq_ref, hf_ref, hi_ref, hg_ref, lb_ref, gn_ref, o_ref, bin_ref, st_ref, state):
        @pl.when(pl.program_id(0) == 0)
        def _():
            state[...] = jnp.zeros_like(state)

        causal, tri, before_sub = _chunk_masks()
        _, f, logf = _gates(hf_ref[...], lb_ref[...])
        kk = 1.0 - f
        hq = hq_ref[...]
        q = hq * _sigmoid(hq)
        cum = jnp.dot(tri, logf, precision=HIGHEST, preferred_element_type=F32)
        base = jnp.dot(before_sub, logf, precision=HIGHEST, preferred_element_type=F32)
        st_ref[0] = state[...]
        for h in range(HEADS):
            sl = slice(h * HEAD_DIM, (h + 1) * HEAD_DIM)
            q_h, k_h, cum_h = q[:, sl], kk[:, sl], cum[:, sl]
            v_h = hi_ref[:, sl]
            st_h = state[h]
            blocks = _intra_blocks(q_h, k_h, cum_h, base[:, sl], causal)
            a = jnp.concatenate([b[4] for b in blocks], axis=0)
            o_h = _dot_nt(q_h * jnp.exp(cum_h), st_h) + _dot(a, v_h)
            last = jnp.sum(logf[:, sl], axis=0, keepdims=True)
            state[h] = st_h * jnp.exp(last) + _dot_tn(v_h, k_h * jnp.exp(last - cum_h))
            rs = lax.rsqrt(jnp.mean(o_h * o_h, axis=-1, keepdims=True) + NORM_EPS)
            hg = hg_ref[:, sl]
            o_ref[:, sl] = o_h
            bin_ref[:, sl] = ((o_h * rs * gn_ref[...]) * (hg * _sigmoid(hg))).astype(bin_ref.dtype)

    def col(block):
        return pl.BlockSpec((CHUNK, D_MODEL), lambda c: (c, block))

    tile = pl.BlockSpec((CHUNK, D_MODEL), lambda c: (c, 0))
    return pl.pallas_call(
        body, name=name, grid=(n_chunks,),
        in_specs=[col(COL_HQ), col(COL_HF), col(COL_HI), col(COL_HG), _row_spec(), _row_spec(HEAD_DIM)],
        out_specs=[tile, tile, pl.BlockSpec((1, HEADS, HEAD_DIM, HEAD_DIM), lambda c: (c, 0, 0, 0))],
        out_shape=[jax.ShapeDtypeStruct((seq, D_MODEL), F32),
                   jax.ShapeDtypeStruct((seq, D_MODEL), MXU_DTYPE),
                   jax.ShapeDtypeStruct((n_chunks, HEADS, HEAD_DIM, HEAD_DIM), F32)],
        scratch_shapes=[pltpu.VMEM((HEADS, HEAD_DIM, HEAD_DIM), F32)],
        compiler_params=_params(dimension_semantics=("arbitrary",)),
    )(z, z, z, z, lb_l, gn_l)


def _rms_parts(y):
    rs = lax.rsqrt(jnp.mean(y * y, axis=-1, keepdims=True) + NORM_EPS)
    return rs, y * rs


def _merge_fwd(a_in, b_in, z, x, wpo_g, who_g, wout_g, gate, g_post, l, tm, name):
    seq = x.shape[0]

    def body(a_ref, b_ref, mgp_ref, mgh_ref, x_ref, wpo_ref, who_ref, wout_ref, gate_ref, gp_ref,
             ba_ref, bb_ref, mer_ref, y_ref, xn_ref):
        a = a_ref[...]
        ba = jnp.concatenate([_dot(a, wpo_ref[j]) for j in range(N_DEV)], axis=1)
        bb = _dot(b_ref[...], who_ref[...])
        merged = _sigmoid(mgp_ref[...]) * ba + _sigmoid(mgh_ref[...]) * bb
        y = _dot(merged, wout_ref[...])
        _, yn = _rms_parts(y)
        ba_ref[...] = ba.astype(ba_ref.dtype)
        bb_ref[...] = bb.astype(bb_ref.dtype)
        mer_ref[...] = merged.astype(mer_ref.dtype)
        y_ref[...] = y
        xn_ref[...] = x_ref[...] + gate_ref[...] * (yn * gp_ref[...])

    def tile(cols=D_MODEL, block=0):
        return pl.BlockSpec((tm, cols), lambda i: (i, block))

    full = pl.BlockSpec((None, D_MODEL, D_MODEL), lambda i: (l, 0, 0))
    act = jax.ShapeDtypeStruct((seq, D_MODEL), MXU_DTYPE)
    f32 = jax.ShapeDtypeStruct((seq, D_MODEL), F32)
    return pl.pallas_call(
        body, name=name, grid=(seq // tm,),
        in_specs=[tile(POOL_WIDTH), tile(), tile(block=COL_MGP), tile(block=COL_MGH), tile(),
                  pl.BlockSpec((None, N_DEV, POOL_WIDTH, GROUP_DIM), lambda i: (l, 0, 0, 0)),
                  full, full, _row_spec(), _row_spec()],
        out_specs=[tile(), tile(), tile(), tile(), tile()],
        out_shape=[act, act, act, f32, f32],
        compiler_params=_params(dimension_semantics=("parallel",)),
    )(a_in, b_in, z, z, x, wpo_g, who_g, wout_g, gate, g_post)


def _loss_grad(x_out, target, tm):
    seq = x_out.shape[0]

    def body(x_ref, t_ref, loss_ref, dx_ref):
        @pl.when(pl.program_id(0) == 0)
        def _():
            loss_ref[...] = jnp.zeros_like(loss_ref)

        err = x_ref[...] - t_ref[...]
        per_token = jnp.mean(err * err, axis=-1, keepdims=True)
        loss_ref[...] += 0.5 * jnp.sum(per_token, axis=0, keepdims=True)
        dx_ref[...] = err * (1.0 / D_MODEL)

    tile = pl.BlockSpec((tm, D_MODEL), lambda i: (i, 0))
    return pl.pallas_call(
        body, name="loss_grad", grid=(seq // tm,),
        in_specs=[tile, tile],
        out_specs=[pl.BlockSpec((1, 1), lambda i: (0, 0)), tile],
        out_shape=[jax.ShapeDtypeStruct((1, 1), F32), jax.ShapeDtypeStruct((seq, D_MODEL), F32)],
        compiler_params=_params(dimension_semantics=("arbitrary",)),
    )(x_out, target)


def _merge_bwd(dx, y, ba, bb, z, wpo_g, who_g, wout_g, gate, g_post, l, tm, name):
    seq = dx.shape[0]

    def body(dx_ref, y_ref, ba_ref, bb_ref, mgp_ref, mgh_ref, wpo_ref, who_ref, wout_ref, gate_ref, gp_ref,
             dy_ref, dba_ref, dbb_ref, da_ref, db_ref, dmg_ref, acc_ref):
        @pl.when(pl.program_id(0) == 0)
        def _():
            acc_ref[...] = jnp.zeros_like(acc_ref)

        dxv = dx_ref[...]
        rs, yn = _rms_parts(y_ref[...])
        acc_ref[0:1, :] += jnp.sum(dxv * yn * gp_ref[...], axis=0, keepdims=True)
        acc_ref[1:2, :] += jnp.sum(dxv * gate_ref[...] * yn, axis=0, keepdims=True)
        dyn = dxv * (gate_ref[...] * gp_ref[...])
        dy = rs * (dyn - yn * jnp.mean(dyn * yn, axis=-1, keepdims=True))
        dmerged = _dot_nt(dy, wout_ref[...])
        sp, sh = _sigmoid(mgp_ref[...]), _sigmoid(mgh_ref[...])
        dba, dbb = sp * dmerged, sh * dmerged
        dmg_ref[:, 0:D_MODEL] = (dmerged * ba_ref[...].astype(F32) * sp * (1.0 - sp)).astype(dmg_ref.dtype)
        dmg_ref[:, D_MODEL:2 * D_MODEL] = (dmerged * bb_ref[...].astype(F32) * sh * (1.0 - sh)).astype(dmg_ref.dtype)
        da = _dot_nt(dba[:, 0:GROUP_DIM], wpo_ref[0])
        for j in range(1, N_DEV):
            da += _dot_nt(dba[:, j * GROUP_DIM:(j + 1) * GROUP_DIM], wpo_ref[j])
        dy_ref[...] = dy.astype(dy_ref.dtype)
        dba_ref[...] = dba.astype(dba_ref.dtype)
        dbb_ref[...] = dbb.astype(dbb_ref.dtype)
        da_ref[...] = da
        db_ref[...] = _dot_nt(dbb, who_ref[...])

    def tile(cols=D_MODEL, block=0):
        return pl.BlockSpec((tm, cols), lambda i: (i, block))

    full = pl.BlockSpec((None, D_MODEL, D_MODEL), lambda i: (l, 0, 0))
    act = jax.ShapeDtypeStruct((seq, D_MODEL), MXU_DTYPE)
    return pl.pallas_call(
        body, name=name, grid=(seq // tm,),
        in_specs=[tile(), tile(), tile(), tile(), tile(block=COL_MGP), tile(block=COL_MGH),
                  pl.BlockSpec((None, N_DEV, POOL_WIDTH, GROUP_DIM), lambda i: (l, 0, 0, 0)),
                  full, full, _row_spec(), _row_spec()],
        out_specs=[tile(), tile(), tile(), tile(POOL_WIDTH), tile(), tile(2 * D_MODEL),
                   pl.BlockSpec((8, D_MODEL), lambda i: (0, 0))],
        out_shape=[act, act, act, jax.ShapeDtypeStruct((seq, POOL_WIDTH), F32),
                   jax.ShapeDtypeStruct((seq, D_MODEL), F32),
                   jax.ShapeDtypeStruct((seq, 2 * D_MODEL), MXU_DTYPE),
                   jax.ShapeDtypeStruct((8, D_MODEL), F32)],
        compiler_params=_params(dimension_semantics=("arbitrary",)),
    )(dx, y, ba, bb, z, z, wpo_g, who_g, wout_g, gate, g_post)


def _grad_tn(a, b, tn, dev_major, name):
    seq, ka = a.shape
    n = b.shape[1]

    def body(a_ref, b_ref, out_ref):
        out_ref[...] = _dot_tn(a_ref[...], b_ref[...]).astype(out_ref.dtype)

    if dev_major:
        out_spec = pl.BlockSpec((None, ka, tn), lambda j: (j, 0, 0))
        out_shape = jax.ShapeDtypeStruct((n // tn, ka, tn), WIRE_DTYPE)
    else:
        out_spec = pl.BlockSpec((ka, tn), lambda j: (0, j))
        out_shape = jax.ShapeDtypeStruct((ka, n), WIRE_DTYPE)
    return pl.pallas_call(
        body, name=name, grid=(n // tn,),
        in_specs=[pl.BlockSpec((seq, ka), lambda j: (0, 0)), pl.BlockSpec((seq, tn), lambda j: (0, j))],
        out_specs=out_spec, out_shape=out_shape,
        compiler_params=_params(dimension_semantics=("parallel",)),
    )(a, b)


def _hgrn_bwd(db_in, z, o, states, lb_l, gn_l, name):
    seq = z.shape[0]
    n_chunks = seq // CHUNK

    def body(db_ref, hq_ref, hf_ref, hi_ref, hg_ref, o_ref, st_ref, lb_ref, gn_ref,
             dz_ref, dlb_ref, dgn_ref, dstate, dq_buf, dk_buf):
        @pl.when(pl.program_id(0) == 0)
        def _():
            dstate[...] = jnp.zeros_like(dstate)
            dlb_ref[...] = jnp.zeros_like(dlb_ref)
            dgn_ref[...] = jnp.zeros_like(dgn_ref)

        causal, tri, before_sub = _chunk_masks()
        lb = lb_ref[...]
        sg, f, logf = _gates(hf_ref[...], lb)
        kk = 1.0 - f
        hq = hq_ref[...]
        sq = _sigmoid(hq)
        q = hq * sq
        cum = jnp.dot(tri, logf, precision=HIGHEST, preferred_element_type=F32)
        base = jnp.dot(before_sub, logf, precision=HIGHEST, preferred_element_type=F32)
        gn = gn_ref[...]
        dgn = jnp.zeros((1, HEAD_DIM), F32)
        dlast = []
        for h in range(HEADS):
            sl = slice(h * HEAD_DIM, (h + 1) * HEAD_DIM)
            q_h, k_h, cum_h = q[:, sl], kk[:, sl], cum[:, sl]
            v_h = hi_ref[:, sl]
            st_h = st_ref[0, h]
            dst_h = dstate[h]
            rs, ohat = _rms_parts(o_ref[:, sl])
            hg = hg_ref[:, sl]
            shg = _sigmoid(hg)
            d_bin = db_ref[:, sl]
            don = d_bin * (hg * shg)
            dgn += jnp.sum(don * ohat, axis=0, keepdims=True)
            dohat = don * gn
            do = rs * (dohat - ohat * jnp.mean(dohat * ohat, axis=-1, keepdims=True))
            dz_ref[:, 3 * D_MODEL + h * HEAD_DIM:3 * D_MODEL + (h + 1) * HEAD_DIM] = (
                d_bin * (ohat * gn) * _dsilu(hg, shg)).astype(dz_ref.dtype)
            last = jnp.sum(logf[:, sl], axis=0, keepdims=True)
            g_in = jnp.exp(cum_h)
            d_out = jnp.exp(last - cum_h)
            q_bar, k_bar = q_h * g_in, k_h * d_out
            blocks = _intra_blocks(q_h, k_h, cum_h, base[:, sl], causal)
            a = jnp.concatenate([b[4] for b in blocks], axis=0)
            da = jnp.where(causal, _dot_nt(do, v_h), 0.0)
            dv = _dot_tn(a, do) + _dot_nt(k_bar, dst_h)
            dq_parts = []
            dk_bar = _dot(v_h, dst_h)
            dk = dk_bar * d_out
            dlast.append(jnp.sum(k_bar * dk_bar, axis=0, keepdims=True)
                         + jnp.exp(last) * jnp.sum(st_h * dst_h, axis=0, keepdims=True))
            for i, (q_t, k_t, e_q, e_k, _) in enumerate(blocks):
                da_i = da[i * SUB:(i + 1) * SUB]
                dq_parts.append(jnp.dot(da_i, k_t, precision=HIGHEST, preferred_element_type=F32) * e_q)
                dk += lax.dot_general(da_i, q_t, (((0,), (0,)), ((), ())), precision=HIGHEST,
                                      preferred_element_type=F32) * e_k
            dq = _dot(do, st_h) * g_in + jnp.concatenate(dq_parts, axis=0)
            dstate[h] = dst_h * jnp.exp(last) + _dot_tn(do, q_bar)
            dq_buf[:, sl] = dq
            dk_buf[:, sl] = dk
            dz_ref[:, 2 * D_MODEL + h * HEAD_DIM:2 * D_MODEL + (h + 1) * HEAD_DIM] = dv.astype(dz_ref.dtype)
        dgn_ref[...] += dgn
        dq_all, dk_all = dq_buf[...], dk_buf[...]
        dg = q * dq_all - kk * dk_all
        dlogf = lax.dot_general(tri, dg, (((0,), (0,)), ((), ())), precision=HIGHEST,
                                preferred_element_type=F32) + jnp.concatenate(dlast, axis=1)
        df = jnp.where(f > LOG_FLOOR, dlogf / f, 0.0) - dk_all
        dlb_ref[...] += jnp.sum(df * (1.0 - sg), axis=0, keepdims=True)
        dz_ref[:, 0:D_MODEL] = (dq_all * _dsilu(hq, sq)).astype(dz_ref.dtype)
        dz_ref[:, D_MODEL:2 * D_MODEL] = (df * (1.0 - lb) * sg * (1.0 - sg)).astype(dz_ref.dtype)

    last_chunk = n_chunks - 1

    def col(block):
        return pl.BlockSpec((CHUNK, D_MODEL), lambda c: (last_chunk - c, block))

    return pl.pallas_call(
        body, name=name, grid=(n_chunks,),
        in_specs=[col(0), col(COL_HQ), col(COL_HF), col(COL_HI), col(COL_HG), col(0),
                  pl.BlockSpec((1, HEADS, HEAD_DIM, HEAD_DIM), lambda c: (last_chunk - c, 0, 0, 0)),
                  _row_spec(), _row_spec(HEAD_DIM)],
        out_specs=[pl.BlockSpec((CHUNK, 4 * D_MODEL), lambda c: (last_chunk - c, 0)),
                   _row_spec(), _row_spec(HEAD_DIM)],
        out_shape=[jax.ShapeDtypeStruct((seq, 4 * D_MODEL), MXU_DTYPE),
                   jax.ShapeDtypeStruct((1, D_MODEL), F32), jax.ShapeDtypeStruct((1, HEAD_DIM), F32)],
        scratch_shapes=[pltpu.VMEM((HEADS, HEAD_DIM, HEAD_DIM), F32),
                        pltpu.VMEM((CHUNK, D_MODEL), F32), pltpu.VMEM((CHUNK, D_MODEL), F32)],
        compiler_params=_params(dimension_semantics=("arbitrary",)),
    )(db_in, z, z, z, z, o, states, lb_l, gn_l)


def _pool_bwd(da_in, z, pool_w_l, pool_scale_l, name):
    seq = z.shape[0]

    def body(da_ref, pv_ref, pg_ref, w_ref, sc_ref, dpv_ref, dpg_ref, dw_ref, dsc_ref):
        g = pl.program_id(0)
        pos = lax.broadcasted_iota(jnp.int32, (seq, GROUP_DIM), 0)
        pm, count = _pool_mean_minus_token(pv_ref[...], g, pos)
        lin0 = _dot(pm, w_ref[...])
        pg = pg_ref[...]
        spg = _sigmoid(pg)
        da = da_ref[...]
        dlin = da * (pg * spg)
        dpg_ref[...] = (da * (lin0 * sc_ref[...]) * _dsilu(pg, spg)).astype(dpg_ref.dtype)
        dsc_ref[...] = jnp.sum(dlin * lin0, axis=0, keepdims=True)
        dl0 = dlin * sc_ref[...]
        dw_ref[...] = _dot_tn(pm, dl0)
        dpm = _dot_nt(dl0, w_ref[...])
        sums, acc = [], dpm / count
        for j in (1, 2, 4, 8):
            acc = acc + _shift_up(acc, j, pos, seq)
            sums.append(acc)
        dpv_ref[...] = (_select_window(g, sums) - dpm).astype(dpv_ref.dtype)

    grp = pl.BlockSpec((seq, GROUP_DIM), lambda g: (0, g))
    return pl.pallas_call(
        body, name=name, grid=(POOL_GROUPS,),
        in_specs=[grp, grp, pl.BlockSpec((seq, GROUP_DIM), lambda g: (0, POOL_GROUPS + g)),
                  pl.BlockSpec((None, GROUP_DIM, GROUP_DIM), lambda g: (g, 0, 0)),
                  pl.BlockSpec((1, GROUP_DIM), lambda g: (0, g))],
        out_specs=[grp, grp, pl.BlockSpec((None, GROUP_DIM, GROUP_DIM), lambda g: (g, 0, 0)),
                   pl.BlockSpec((1, GROUP_DIM), lambda g: (0, g))],
        out_shape=[jax.ShapeDtypeStruct((seq, POOL_WIDTH), MXU_DTYPE),
                   jax.ShapeDtypeStruct((seq, POOL_WIDTH), MXU_DTYPE),
                   jax.ShapeDtypeStruct((POOL_GROUPS, GROUP_DIM, GROUP_DIM), F32),
                   jax.ShapeDtypeStruct((1, POOL_WIDTH), F32)],
        compiler_params=_params(dimension_semantics=("parallel",)),
    )(da_in, z, z, pool_w_l, pool_scale_l)


def _in_proj_dw(h, dz, name):
    seq = h.shape[0]

    def body(h_ref, dz_ref, out_ref):
        out_ref[...] = lax.dot_general(h_ref[...], dz_ref[...], (((0,), (0,)), ((), ())),
                                       preferred_element_type=F32).astype(out_ref.dtype)

    return pl.pallas_call(
        body, name=name, grid=(N_DEV,),
        in_specs=[pl.BlockSpec((seq, D_MODEL), lambda j: (0, 0)), pl.BlockSpec((seq, IN_COLS), lambda j: (0, j))],
        out_specs=pl.BlockSpec((None, D_MODEL, IN_COLS), lambda j: (j, 0, 0)),
        out_shape=jax.ShapeDtypeStruct((N_DEV, D_MODEL, IN_COLS), WIRE_DTYPE),
        compiler_params=_params(dimension_semantics=("parallel",)),
    )(h, dz)


def _in_proj_dh(dz, win_g, l, tm, name):
    seq = dz.shape[0]

    def body(dz_ref, w_ref, dh_ref):
        @pl.when(pl.program_id(1) == 0)
        def _():
            dh_ref[...] = jnp.zeros_like(dh_ref)

        dh_ref[...] += lax.dot_general(dz_ref[...], w_ref[...], (((1,), (1,)), ((), ())),
                                       preferred_element_type=F32)

    return pl.pallas_call(
        body, name=name, grid=(seq // tm, N_DEV),
        in_specs=[pl.BlockSpec((tm, IN_COLS), lambda i, j: (i, j)),
                  pl.BlockSpec((None, None, D_MODEL, IN_COLS), lambda i, j: (l, j, 0, 0))],
        out_specs=pl.BlockSpec((tm, D_MODEL), lambda i, j: (i, 0)),
        out_shape=jax.ShapeDtypeStruct((seq, D_MODEL), F32),
        compiler_params=_params(dimension_semantics=("parallel", "arbitrary")),
    )(dz, win_g)


def _prenorm_bwd(x, dh, dx_res, g, scale, tm, name):
    seq = x.shape[0]

    def body(x_ref, dh_ref, dxr_ref, g_ref, sc_ref, dx_ref, acc_ref):
        @pl.when(pl.program_id(0) == 0)
        def _():
            acc_ref[...] = jnp.zeros_like(acc_ref)

        rs, xn = _rms_parts(x_ref[...])
        dh = dh_ref[...]
        acc_ref[0:1, :] += jnp.sum(dh, axis=0, keepdims=True)
        acc_ref[1:2, :] += jnp.sum(dh * (xn * g_ref[...]), axis=0, keepdims=True)
        dhn = dh * (1.0 + sc_ref[...])
        acc_ref[2:3, :] += jnp.sum(dhn * xn, axis=0, keepdims=True)
        dxn = dhn * g_ref[...]
        dx_ref[...] = rs * (dxn - xn * jnp.mean(dxn * xn, axis=-1, keepdims=True)) + dxr_ref[...]

    tile = pl.BlockSpec((tm, D_MODEL), lambda i: (i, 0))
    return pl.pallas_call(
        body, name=name, grid=(seq // tm,),
        in_specs=[tile, tile, tile, _row_spec(), _row_spec()],
        out_specs=[tile, pl.BlockSpec((8, D_MODEL), lambda i: (0, 0))],
        out_shape=[jax.ShapeDtypeStruct((seq, D_MODEL), F32), jax.ShapeDtypeStruct((8, D_MODEL), F32)],
        compiler_params=_params(dimension_semantics=("arbitrary",)),
    )(x, dh, dx_res, g, scale)


def _adamw_math(w, g, m, v):
    m = ADAM_B1 * m + (1.0 - ADAM_B1) * g
    v = ADAM_B2 * v + (1.0 - ADAM_B2) * (g * g)
    m_hat = m / (1.0 - ADAM_B1 ** ADAM_STEP)
    v_hat = v / (1.0 - ADAM_B2 ** ADAM_STEP)
    delta = -ADAM_LR * (m_hat / (jnp.sqrt(v_hat) + ADAM_EPS) + ADAM_WD * w)
    return delta, m, v


def _adamw_sharded(w, m, v, contrib, tr, name):
    depth, rows, cols = w.shape
    n_parts = contrib.shape[1]

    def body(w_ref, m_ref, v_ref, c_ref, g_ref, d_ref, mo_ref, vo_ref):
        g = c_ref[0].astype(F32)
        for p in range(1, n_parts):
            g += c_ref[p].astype(F32)
        delta, mn, vn = _adamw_math(w_ref[...], g, m_ref[...], v_ref[...])
        g_ref[...] = g
        d_ref[...] = delta
        mo_ref[...] = mn
        vo_ref[...] = vn

    tile = pl.BlockSpec((None, tr, cols), lambda l, i: (l, i, 0))
    shape = jax.ShapeDtypeStruct(w.shape, F32)
    return pl.pallas_call(
        body, name=name, grid=(depth, rows // tr),
        in_specs=[tile, tile, tile, pl.BlockSpec((None, n_parts, tr, cols), lambda l, i: (l, 0, i, 0))],
        out_specs=[tile] * 4, out_shape=[shape] * 4,
        compiler_params=_params(dimension_semantics=("parallel", "parallel")),
    )(w, m, v, contrib)


def _adamw_small(w_pack, m_pack, v_pack, g_all):
    def body(w_ref, m_ref, v_ref, ga_ref, g_ref, d_ref, mo_ref, vo_ref):
        g = ga_ref[0]
        for d in range(1, N_DEV):
            g += ga_ref[d]
        w = w_ref[...]
        r0, r1, r2 = LB_ROW0, LB_ROW0 + 8, LB_ROW0 + 16
        lg0, lg1 = w[r0:r1], w[r1:r2]
        mx = jnp.maximum(lg0, lg1)
        e0, e1 = jnp.exp(lg0 - mx), jnp.exp(lg1 - mx)
        p0, p1 = e0 / (e0 + e1), e1 / (e0 + e1)
        low = ((p0 - p0), (p0 + p1) - p0)
        dlow = [g_rows * jnp.where((lo > 0.0) & (lo < 1.0), 1.0, jnp.where((lo == 0.0) | (lo == 1.0), 0.5, 0.0))
                for g_rows, lo in ((g[r0:r1], low[0]), (g[r1:r2], low[1]))]
        dp0 = (dlow[0] + dlow[1]) - (dlow[0] + dlow[1])
        dp1 = dlow[1]
        inner = p0 * dp0 + p1 * dp1
        g = jnp.concatenate([g[:r0], p0 * (dp0 - inner), p1 * (dp1 - inner), g[r2:]], axis=0)
        delta, mn, vn = _adamw_math(w, g, m_ref[...], v_ref[...])
        g_ref[...] = g
        d_ref[...] = delta
        mo_ref[...] = mn
        vo_ref[...] = vn

    shape = jax.ShapeDtypeStruct(w_pack.shape, F32)
    return pl.pallas_call(body, name="adamw_small", out_shape=[shape] * 4, compiler_params=_params())(
        w_pack, m_pack, v_pack, g_all)


def _pack_small(parts):
    rows = [parts[name].reshape(n, 128) for name, n in _SMALL_ROWS]
    used = sum(n for _, n in _SMALL_ROWS)
    rows.append(jnp.zeros((SMALL_ROWS_PAD - used, 128), F32))
    return jnp.concatenate(rows, axis=0)


def _unpack_small(pack, shapes):
    out, r = {}, 0
    for name, n in _SMALL_ROWS:
        out[name] = pack[r:r + n].reshape(shapes[name])
        r += n
    return out


def kernel(x, c, w_ada, b_ada, g_pre, g_post, w_in, pool_w, pool_scale, lb_logits, hgrn_norm_g, w_pool_o, w_hgrn_o, w_out, loss_target, m_w_ada, m_b_ada, m_g_pre, m_g_post, m_w_in, m_pool_w, m_pool_scale, m_lb_logits, m_hgrn_norm_g, m_w_pool_o, m_w_hgrn_o, m_w_out, v_w_ada, v_b_ada, v_g_pre, v_g_post, v_w_in, v_pool_w, v_pool_scale, v_lb_logits, v_hgrn_norm_g, v_w_pool_o, v_w_hgrn_o, v_w_out):
    seq = x.shape[1]
    tm = min(512, seq)
    tm_merge = min(256, seq)
    _, _, _, me = _my_position()

    win_g, wpo_g, who_g, wout_g = _allgather_weights(
        [w_in.astype(WIRE_DTYPE), w_pool_o.astype(WIRE_DTYPE), w_hgrn_o.astype(WIRE_DTYPE), w_out.astype(WIRE_DTYPE)])
    who_g = who_g.reshape(DEPTH, D_MODEL, D_MODEL)
    wout_g = wout_g.reshape(DEPTH, D_MODEL, D_MODEL)

    c_all = _allgather_small(c, "allgather_c").reshape(N_DEV, D_MODEL)
    b_cols = lax.dynamic_slice_in_dim(b_ada, me * ADA_COLS, ADA_COLS, axis=1)
    ada_part = _ada_fwd(c_all, w_ada, b_cols)
    ada_all = _allgather_small(ada_part.reshape(DEPTH * N_DEV, ADA_COLS), "allgather_ada")
    ada_all = ada_all.reshape(N_DEV, DEPTH, N_DEV, ADA_COLS)
    ada = lax.dynamic_index_in_dim(ada_all, me, axis=2, keepdims=False)
    ada = jnp.transpose(ada, (1, 0, 2)).reshape(DEPTH, 3 * D_MODEL)
    shift = [ada[l:l + 1, 0:D_MODEL] for l in range(DEPTH)]
    scale = [ada[l:l + 1, D_MODEL:2 * D_MODEL] for l in range(DEPTH)]
    gate = [ada[l:l + 1, 2 * D_MODEL:] for l in range(DEPTH)]
    lb = _lb_fwd(lb_logits)

    xs, saved = [x[0]], []
    for l in range(DEPTH):
        h = _prenorm_fwd(xs[l], g_pre[l:l + 1], shift[l], scale[l], tm, f"prenorm_fwd_{l}")
        z = _in_proj(h, win_g, l, tm, f"in_proj_{l}")
        a_in = _pool_fwd(z, pool_w[l], pool_scale[l:l + 1], f"pool_fwd_{l}")
        o, b_in, states = _hgrn_fwd(z, lb[l:l + 1], hgrn_norm_g[l:l + 1], f"hgrn_fwd_{l}")
        ba, bb, merged, y, x_next = _merge_fwd(a_in, b_in, z, xs[l], wpo_g, who_g, wout_g, gate[l], g_post[l:l + 1],
                                               l, tm_merge, f"merge_fwd_{l}")
        xs.append(x_next)
        saved.append((h, z, a_in, o, b_in, states, ba, bb, merged, y))

    loss_part, dx = _loss_grad(xs[DEPTH], loss_target[0], tm)
    loss = lax.psum(loss_part[0, 0], ("x", "y", "c"))

    g_win, g_wpo, g_who, g_wout = [None] * DEPTH, [None] * DEPTH, [None] * DEPTH, [None] * DEPTH
    d_ada, small = [None] * DEPTH, [None] * DEPTH
    for l in reversed(range(DEPTH)):
        h, z, a_in, o, b_in, states, ba, bb, merged, y = saved[l]
        dy, dba, dbb, da_in, db_in, dmg, acc_post = _merge_bwd(
            dx, y, ba, bb, z, wpo_g, who_g, wout_g, gate[l], g_post[l:l + 1], l, tm_merge, f"merge_bwd_{l}")
        g_wout[l] = _grad_tn(merged, dy, 512, False, f"grad_w_out_{l}").reshape(N_DEV, HEAD_DIM, D_MODEL)
        g_who[l] = _grad_tn(b_in, dbb, 512, False, f"grad_w_hgrn_o_{l}").reshape(N_DEV, HEAD_DIM, D_MODEL)
        g_wpo[l] = _grad_tn(a_in, dba, GROUP_DIM, True, f"grad_w_pool_o_{l}")
        dzh, dlb, dgn = _hgrn_bwd(db_in, z, o, states, lb[l:l + 1], hgrn_norm_g[l:l + 1], f"hgrn_bwd_{l}")
        dpv, dpg, dpw, dps = _pool_bwd(da_in, z, pool_w[l], pool_scale[l:l + 1], f"pool_bwd_{l}")
        dz = jnp.concatenate([dpv, dpg, dzh, dmg], axis=1)
        g_win[l] = _in_proj_dw(h, dz, f"grad_w_in_{l}")
        dh = _in_proj_dh(dz, win_g, l, tm, f"in_proj_dh_{l}")
        dx, acc_pre = _prenorm_bwd(xs[l], dh, dx, g_pre[l:l + 1], scale[l], tm, f"prenorm_bwd_{l}")
        d_ada[l] = jnp.concatenate([acc_pre[0], acc_pre[1], acc_post[0]])
        small[l] = dict(g_pre=acc_pre[2], g_post=acc_post[1], pool_w=dpw, pool_scale=dps[0], lb_logits=dlb[0],
                        hgrn_norm_g=dgn[0])
    grad_x = dx[None]

    small_names = [name for name, _ in _SMALL_ROWS]
    g_parts = {name: jnp.stack([small[l][name] for l in range(DEPTH)]) for name in small_names if name != "b_ada"}
    g_parts["b_ada"] = jnp.stack(d_ada)
    g_all = _allgather_small(_pack_small(g_parts), "allgather_small_grads")
    weights = dict(b_ada=b_ada, g_pre=g_pre, g_post=g_post, pool_w=pool_w, pool_scale=pool_scale,
                   lb_logits=lb_logits, hgrn_norm_g=hgrn_norm_g)
    m_small = dict(b_ada=m_b_ada, g_pre=m_g_pre, g_post=m_g_post, pool_w=m_pool_w, pool_scale=m_pool_scale,
                   lb_logits=m_lb_logits, hgrn_norm_g=m_hgrn_norm_g)
    v_small = dict(b_ada=v_b_ada, g_pre=v_g_pre, g_post=v_g_post, pool_w=v_pool_w, pool_scale=v_pool_scale,
                   lb_logits=v_lb_logits, hgrn_norm_g=v_hgrn_norm_g)
    shapes = {name: weights[name].shape for name in small_names}
    small_out = [_unpack_small(p, shapes) for p in
                 _adamw_small(_pack_small(weights), _pack_small(m_small), _pack_small(v_small), g_all)]

    d_ada_all = g_all[:, 0:48, :].reshape(N_DEV, DEPTH, 3 * D_MODEL)
    d_cols = jnp.transpose(lax.dynamic_slice_in_dim(d_ada_all, me * ADA_COLS, ADA_COLS, axis=2), (1, 0, 2))
    g_w_ada = _ada_bwd(c_all, d_cols)
    ada_out = _adamw_sharded(w_ada, m_w_ada, v_w_ada, g_w_ada[:, None], 256, "adamw_w_ada")

    r_win, r_wpo, r_who, r_wout = _scatter_grads([g_win, g_wpo, g_who, g_wout])
    win_out = _adamw_sharded(w_in, m_w_in, v_w_in, r_win, 256, "adamw_w_in")
    wpo_out = _adamw_sharded(w_pool_o, m_w_pool_o, v_w_pool_o, r_wpo, POOL_WIDTH, "adamw_w_pool_o")
    who_out = _adamw_sharded(w_hgrn_o, m_w_hgrn_o, v_w_hgrn_o, r_who, HEAD_DIM, "adamw_w_hgrn_o")
    wout_out = _adamw_sharded(w_out, m_w_out, v_w_out, r_wout, HEAD_DIM, "adamw_w_out")

    def leaf(kind):
        s = small_out[kind]
        return (ada_out[kind], s["b_ada"], s["g_pre"], s["g_post"], win_out[kind], s["pool_w"], s["pool_scale"],
                s["lb_logits"], s["hgrn_norm_g"], wpo_out[kind], who_out[kind], wout_out[kind])

    return (loss, grad_x) + leaf(0) + leaf(1) + leaf(2) + leaf(3)
```

```python
import jax
import jax.numpy as jnp
from jax import lax
from jax.experimental import pallas as pl
from jax.experimental.pallas import tpu as pltpu

F32 = jnp.float32
MXU_DTYPE = jnp.bfloat16
WIRE_DTYPE = jnp.bfloat16

N_DEV = 8
DEPTH = 2
D_MODEL = 1024
HEADS = 8
HEAD_DIM = 128
POOL_GROUPS = 4
GROUP_DIM = 128
POOL_WIDTH = POOL_GROUPS * GROUP_DIM
IN_WIDTH = 7168
CHUNK = 64
SUB = 16
N_SUB = CHUNK // SUB
EXP_CLAMP = 80.0
NORM_EPS = 1e-6
LOG_FLOOR = 1e-30
ADA_COLS = 3 * D_MODEL // N_DEV
IN_COLS = IN_WIDTH // N_DEV
COL_HQ, COL_HF, COL_HI, COL_HG, COL_MGP, COL_MGH = 1, 2, 3, 4, 5, 6

ADAM_LR = 0.001
ADAM_B1 = 0.9
ADAM_B2 = 0.999
ADAM_EPS = 1e-08
ADAM_WD = 0.01
ADAM_STEP = 10

VMEM_LIMIT = 48 * 1024 * 1024
MESH_ID = pl.DeviceIdType.MESH
HIGHEST = lax.Precision.HIGHEST

_SMALL_ROWS = (("b_ada", 48), ("g_pre", 16), ("g_post", 16), ("pool_w", 1024), ("pool_scale", 8),
               ("lb_logits", 16), ("hgrn_norm_g", 2))
SMALL_ROWS_PAD = 1136
LB_ROW0 = 48 + 16 + 16 + 1024 + 8


def _params(**kw):
    return pltpu.CompilerParams(vmem_limit_bytes=VMEM_LIMIT, **kw)


def _sigmoid(v):
    return 1.0 / (1.0 + jnp.exp(-v))


def _dsilu(v, s):
    return s * (1.0 + v * (1.0 - s))


def _dot(a, b):
    return jnp.dot(a.astype(MXU_DTYPE), b.astype(MXU_DTYPE), preferred_element_type=F32)


def _dot_nt(a, b):
    return lax.dot_general(a.astype(MXU_DTYPE), b.astype(MXU_DTYPE), (((1,), (1,)), ((), ())),
                           preferred_element_type=F32)


def _dot_tn(a, b):
    return lax.dot_general(a.astype(MXU_DTYPE), b.astype(MXU_DTYPE), (((0,), (0,)), ((), ())),
                           preferred_element_type=F32)


def _my_position():
    mx, my, mc = lax.axis_index("x"), lax.axis_index("y"), lax.axis_index("c")
    return mx, my, mc, 4 * mx + 2 * my + mc


def _peer(mx, my, mc, k):
    px = 1 - mx if (k >> 2) & 1 else mx
    py = 1 - my if (k >> 1) & 1 else my
    pc = 1 - mc if k & 1 else mc
    return (px, py, pc), 4 * px + 2 * py + pc


def _allgather_small(v, name):
    rows, cols = v.shape

    def body(v_ref, out_ref, send_sems, recv_sems):
        mx, my, mc, me = _my_position()
        out_ref[me] = v_ref[...]
        copies = []
        for k in range(1, N_DEV):
            peer, _ = _peer(mx, my, mc, k)
            cp = pltpu.make_async_remote_copy(
                src_ref=v_ref, dst_ref=out_ref.at[me],
                send_sem=send_sems.at[k - 1], recv_sem=recv_sems.at[k - 1],
                device_id=peer, device_id_type=MESH_ID)
            cp.start()
            copies.append(cp)
        for cp in copies:
            cp.wait()

    return pl.pallas_call(
        body, name=name,
        out_shape=jax.ShapeDtypeStruct((N_DEV, rows, cols), v.dtype),
        in_specs=[pl.BlockSpec(memory_space=pltpu.VMEM)],
        out_specs=pl.BlockSpec(memory_space=pltpu.VMEM),
        scratch_shapes=[pltpu.SemaphoreType.DMA((N_DEV - 1,)), pltpu.SemaphoreType.DMA((N_DEV - 1,))],
        compiler_params=_params(),
    )(v)


class _Stream:
    def __init__(self, n, plan):
        self.n, self.plan = n, plan


def _comm_call(name, bufs, start=(), wait=(), after=None, local=None):
    names = list(bufs)

    def body(*refs):
        it = iter(refs)
        buf_refs = {n: next(it) for n in names}
        wait_sems = [(next(it), next(it)) for _ in wait]
        if after is not None:
            next(it)
        start_sems = [(next(it), next(it)) for _ in start]
        for _ in names:
            next(it)
        token = next(it)
        pos = _my_position()

        def descriptors(stream, sems):
            return [pltpu.make_async_remote_copy(src_ref=src, dst_ref=dst, send_sem=sems[0].at[k], recv_sem=sems[1].at[k],
                                                 device_id=dev, device_id_type=MESH_ID)
                    for k, (src, dst, dev) in enumerate(stream.plan(buf_refs, pos))]

        for (stream, _), sems in zip(wait, wait_sems):
            for cp in descriptors(stream, sems):
                cp.wait_send()
                cp.wait_recv()
        for stream, sems in zip(start, start_sems):
            for cp in descriptors(stream, sems):
                cp.start()
        if local is not None:
            local_sems = next(it)
            copies = [pltpu.make_async_copy(src, dst, local_sems.at[k])
                      for k, (src, dst) in enumerate(local[1](buf_refs, pos))]
            for cp in copies:
                cp.start()
            for cp in copies:
                cp.wait()
        token[...] = jnp.zeros_like(token)

    hbm = pl.BlockSpec(memory_space=pltpu.HBM)
    sem = pl.BlockSpec(memory_space=pltpu.SEMAPHORE)
    operands = [pltpu.with_memory_space_constraint(bufs[n], pltpu.HBM) for n in names]
    in_specs = [hbm] * len(names)
    for _, (send_sems, recv_sems) in wait:
        operands += [send_sems, recv_sems]
        in_specs += [sem, sem]
    if after is not None:
        operands.append(after)
        in_specs.append(pl.BlockSpec(memory_space=pl.ANY))
    out_shape, out_specs = [], []
    for stream in start:
        out_shape += [pltpu.SemaphoreType.DMA((stream.n,)), pltpu.SemaphoreType.DMA((stream.n,))]
        out_specs += [sem, sem]
    n_sem_out = len(out_shape)
    out_shape += [pltpu.HBM(bufs[n].shape, bufs[n].dtype) for n in names]
    out_specs += [hbm] * len(names)
    out_shape.append(jax.ShapeDtypeStruct((8, 128), F32))
    out_specs.append(pl.BlockSpec(memory_space=pltpu.VMEM))
    outs = pl.pallas_call(
        body, name=name, out_shape=out_shape, in_specs=in_specs, out_specs=out_specs,
        input_output_aliases={i: n_sem_out + i for i in range(len(names))},
        scratch_shapes=[pltpu.SemaphoreType.DMA((local[0],))] if local is not None else [],
        compiler_params=pltpu.CompilerParams(has_side_effects=pltpu.SideEffectType.DATAFLOW_SIDE_EFFECTING),
    )(*operands)
    sems = [(outs[2 * i], outs[2 * i + 1]) for i in range(len(start))]
    return dict(zip(names, outs[n_sem_out:n_sem_out + len(names)])), sems, outs[-1]


def _other_chips(pos):
    mx, my, _, _ = pos
    return [(1 - mx if i & 2 else mx, 1 - my if i & 1 else my) for i in (1, 2, 3)]


def _dev_index(px, py, pc):
    return 4 * px + 2 * py + pc


def _gather_streams(keys):
    def to_chips(refs, pos):
        _, _, mc, me = pos
        return [(refs["s_" + k], refs["g_" + k].at[me], (cx, cy, mc)) for k in keys for cx, cy in _other_chips(pos)]

    def to_sibling(refs, pos):
        mx, my, mc, me = pos
        return [(refs["s_" + k], refs["g_" + k].at[me], (mx, my, 1 - mc)) for k in keys]

    def pass_on(refs, pos):
        mx, my, mc, _ = pos
        out = []
        for k in keys:
            for cx, cy in _other_chips(pos):
                slot = refs["g_" + k].at[_dev_index(cx, cy, mc)]
                out.append((slot, slot, (mx, my, 1 - mc)))
        return out

    return _Stream(3 * len(keys), to_chips), _Stream(len(keys), to_sibling), _Stream(3 * len(keys), pass_on)


def _scatter_streams(keys):
    def pair(refs, pos):
        mx, my, mc, _ = pos
        sib = (mx, my, 1 - mc)
        out = []
        for k in keys:
            for i, (cx, cy) in enumerate(_other_chips(pos)):
                out.append((refs["g_" + k].at[_dev_index(cx, cy, 1 - mc)], refs["st_" + k].at[i], sib))
            out.append((refs["g_" + k].at[_dev_index(mx, my, 1 - mc)], refs["st_" + k].at[3], sib))
        return out

    def chips(refs, pos):
        mc = pos[2]
        return [(refs["ps_" + k].at[i], refs["ld_" + k].at[i], (cx, cy, mc))
                for k in keys for i, (cx, cy) in enumerate(_other_chips(pos))]

    return _Stream(4 * len(keys), pair), _Stream(3 * len(keys), chips)


def _pair_sum(g, st, idx, tr, name):
    _, rows, cols = g.shape

    def body(idx_ref, g_ref, st_ref, out_ref):
        out_ref[...] = (g_ref[...].astype(F32) + st_ref[...].astype(F32)).astype(out_ref.dtype)

    return pl.pallas_call(
        body, name=name,
        grid_spec=pltpu.PrefetchScalarGridSpec(
            num_scalar_prefetch=1, grid=(4, rows // tr),
            in_specs=[pl.BlockSpec((None, tr, cols), lambda j, i, idx_ref: (idx_ref[j], i, 0)),
                      pl.BlockSpec((None, tr, cols), lambda j, i, idx_ref: (j, i, 0))],
            out_specs=pl.BlockSpec((None, tr, cols), lambda j, i, idx_ref: (j, i, 0))),
        out_shape=jax.ShapeDtypeStruct((4, rows, cols), WIRE_DTYPE),
        compiler_params=_params(dimension_semantics=("parallel", "parallel")),
    )(idx, g, st)


def _ada_fwd(c_all, w_ada, b_cols):
    def body(c_ref, w_ref, b_ref, out_ref):
        cv = c_ref[...]
        ca = cv * _sigmoid(cv)
        for l in range(DEPTH):
            out_ref[l] = jnp.dot(ca, w_ref[l], precision=HIGHEST, preferred_element_type=F32) + b_ref[l:l + 1, :]

    return pl.pallas_call(
        body, name="ada_fwd",
        out_shape=jax.ShapeDtypeStruct((DEPTH, N_DEV, ADA_COLS), F32),
        compiler_params=_params(),
    )(c_all, w_ada, b_cols)


def _ada_bwd(c_all, d_cols):
    def body(c_ref, d_ref, out_ref):
        cv = c_ref[...]
        ca = cv * _sigmoid(cv)
        for l in range(DEPTH):
            out_ref[l] = lax.dot_general(ca, d_ref[l], (((0,), (0,)), ((), ())), precision=HIGHEST,
                                         preferred_element_type=F32)

    return pl.pallas_call(
        body, name="ada_bwd",
        out_shape=jax.ShapeDtypeStruct((DEPTH, D_MODEL, ADA_COLS), F32),
        compiler_params=_params(),
    )(c_all, d_cols)


def _lower_bounds(logits):
    m = jnp.maximum(logits[0:1], logits[1:2])
    e0, e1 = jnp.exp(logits[0:1] - m), jnp.exp(logits[1:2] - m)
    den = e0 + e1
    p0, p1 = e0 / den, e1 / den
    low0 = p0 - p0
    low1 = (p0 + p1) - p0
    return (p0, p1), (low0, low1)


def _lb_fwd(lb_logits):
    def body(lg_ref, out_ref):
        _, (low0, low1) = _lower_bounds(lg_ref[...])
        out_ref[0:1, :] = jnp.clip(low0, 0.0, 1.0)
        out_ref[1:2, :] = jnp.clip(low1, 0.0, 1.0)

    return pl.pallas_call(body, name="lb_fwd", out_shape=jax.ShapeDtypeStruct(lb_logits.shape, F32),
                          compiler_params=_params())(lb_logits)


def _row_spec(cols=D_MODEL):
    return pl.BlockSpec((1, cols), lambda *_: (0, 0))


def _prenorm_fwd(x, g, shift, scale, tm, name):
    seq = x.shape[0]

    def body(x_ref, g_ref, sh_ref, sc_ref, h_ref):
        xv = x_ref[...]
        rs = lax.rsqrt(jnp.mean(xv * xv, axis=-1, keepdims=True) + NORM_EPS)
        h = (xv * rs * g_ref[...]) * (1.0 + sc_ref[...]) + sh_ref[...]
        h_ref[...] = h.astype(h_ref.dtype)

    tile = pl.BlockSpec((tm, D_MODEL), lambda i: (i, 0))
    return pl.pallas_call(
        body, name=name, grid=(seq // tm,),
        in_specs=[tile, _row_spec(), _row_spec(), _row_spec()], out_specs=tile,
        out_shape=jax.ShapeDtypeStruct((seq, D_MODEL), MXU_DTYPE),
        compiler_params=_params(dimension_semantics=("parallel",)),
    )(x, g, shift, scale)


def _in_proj(h, win_g, tm, name):
    seq = h.shape[0]

    def body(h_ref, w_ref, z_ref):
        z_ref[...] = jnp.dot(h_ref[...], w_ref[...], preferred_element_type=F32)

    return pl.pallas_call(
        body, name=name, grid=(N_DEV, seq // tm),
        in_specs=[pl.BlockSpec((tm, D_MODEL), lambda j, i: (i, 0)),
                  pl.BlockSpec((None, D_MODEL, IN_COLS), lambda j, i: (j, 0, 0))],
        out_specs=pl.BlockSpec((tm, IN_COLS), lambda j, i: (i, j)),
        out_shape=jax.ShapeDtypeStruct((seq, IN_WIDTH), F32),
        compiler_params=_params(dimension_semantics=("parallel", "parallel")),
    )(h, win_g)


def _shift_down(v, j, pos):
    return jnp.where(pos >= j, pltpu.roll(v, j, 0), 0.0)


def _shift_up(v, j, pos, seq):
    return jnp.where(pos < seq - j, pltpu.roll(v, seq - j, 0), 0.0)


def _select_window(g, candidates):
    out = candidates[-1]
    for i in range(len(candidates) - 2, -1, -1):
        out = jnp.where(g == i, candidates[i], out)
    return out


def _pool_mean_minus_token(u, g, pos):
    sums, acc = [], u
    for j in (1, 2, 4, 8):
        acc = acc + _shift_down(acc, j, pos)
        sums.append(acc)
    wsum = _select_window(g, sums)
    width = jnp.left_shift(2, g).astype(F32)
    count = jnp.minimum(pos.astype(F32) + 1.0, width)
    return wsum / count - u, count


def _pool_fwd(z, pool_w_l, pool_scale_l, name):
    seq = z.shape[0]

    def body(pv_ref, pg_ref, w_ref, sc_ref, out_ref):
        g = pl.program_id(0)
        pos = lax.broadcasted_iota(jnp.int32, (seq, GROUP_DIM), 0)
        pm, _ = _pool_mean_minus_token(pv_ref[...], g, pos)
        lin = _dot(pm, w_ref[...]) * sc_ref[...]
        pg = pg_ref[...]
        out_ref[...] = (lin * (pg * _sigmoid(pg))).astype(out_ref.dtype)

    return pl.pallas_call(
        body, name=name, grid=(POOL_GROUPS,),
        in_specs=[pl.BlockSpec((seq, GROUP_DIM), lambda g: (0, g)),
                  pl.BlockSpec((seq, GROUP_DIM), lambda g: (0, POOL_GROUPS + g)),
                  pl.BlockSpec((None, GROUP_DIM, GROUP_DIM), lambda g: (g, 0, 0)),
                  pl.BlockSpec((1, GROUP_DIM), lambda g: (0, g))],
        out_specs=pl.BlockSpec((seq, GROUP_DIM), lambda g: (0, g)),
        out_shape=jax.ShapeDtypeStruct((seq, POOL_WIDTH), MXU_DTYPE),
        compiler_params=_params(dimension_semantics=("parallel",)),
    )(z, z, pool_w_l, pool_scale_l)


def _chunk_masks():
    row = lax.broadcasted_iota(jnp.int32, (CHUNK, CHUNK), 0)
    col = lax.broadcasted_iota(jnp.int32, (CHUNK, CHUNK), 1)
    causal = row >= col
    tri = causal.astype(F32)
    before_sub = (col < (row // SUB) * SUB).astype(F32)
    return causal, tri, before_sub


def _gates(zf, lb):
    sg = _sigmoid(zf)
    f = lb + (1.0 - lb) * sg
    logf = jnp.log(jnp.maximum(f, LOG_FLOOR))
    return sg, f, logf


def _intra_blocks(q_h, k_h, cum_h, base_h, causal):
    rel = cum_h - base_h
    out = []
    for i in range(N_SUB):
        rows = slice(i * SUB, (i + 1) * SUB)
        e_q = jnp.exp(rel[rows])
        base_i = jnp.concatenate([base_h[rows]] * N_SUB, axis=0)
        e_k = jnp.exp(jnp.minimum(base_i - cum_h, EXP_CLAMP))
        q_t = q_h[rows] * e_q
        k_t = k_h * e_k
        a_i = jnp.where(causal[rows], _dot_nt(q_t, k_t), 0.0)
        out.append((q_t, k_t, e_q, e_k, a_i))
    return out


def _hgrn_fwd(z, lb_l, gn_l, name):
    seq = z.shape[0]
    n_chunks = seq // CHUNK

    def body(hq_ref, hf_ref, hi_ref, hg_ref, lb_ref, gn_ref, o_ref, bin_ref, st_ref, state):
        @pl.when(pl.program_id(0) == 0)
        def _():
            state[...] = jnp.zeros_like(state)

        causal, tri, before_sub = _chunk_masks()
        _, f, logf = _gates(hf_ref[...], lb_ref[...])
        kk = 1.0 - f
        hq = hq_ref[...]
        q = hq * _sigmoid(hq)
        cum = jnp.dot(tri, logf, precision=HIGHEST, preferred_element_type=F32)
        base = jnp.dot(before_sub, logf, precision=HIGHEST, preferred_element_type=F32)
        st_ref[0] = state[...]
        for h in range(HEADS):
            sl = slice(h * HEAD_DIM, (h + 1) * HEAD_DIM)
            q_h, k_h, cum_h = q[:, sl], kk[:, sl], cum[:, sl]
            v_h = hi_ref[:, sl]
            st_h = state[h]
            blocks = _intra_blocks(q_h, k_h, cum_h, base[:, sl], causal)
            a = jnp.concatenate([b[4] for b in blocks], axis=0)
            o_h = _dot_nt(q_h * jnp.exp(cum_h), st_h) + _dot(a, v_h)
            last = jnp.sum(logf[:, sl], axis=0, keepdims=True)
            state[h] = st_h * jnp.exp(last) + _dot_tn(v_h, k_h * jnp.exp(last - cum_h))
            rs = lax.rsqrt(jnp.mean(o_h * o_h, axis=-1, keepdims=True) + NORM_EPS)
            hg = hg_ref[:, sl]
            o_ref[:, sl] = o_h
            bin_ref[:, sl] = ((o_h * rs * gn_ref[...]) * (hg * _sigmoid(hg))).astype(bin_ref.dtype)

    def col(block):
        return pl.BlockSpec((CHUNK, D_MODEL), lambda c: (c, block))

    tile = pl.BlockSpec((CHUNK, D_MODEL), lambda c: (c, 0))
    return pl.pallas_call(
        body, name=name, grid=(n_chunks,),
        in_specs=[col(COL_HQ), col(COL_HF), col(COL_HI), col(COL_HG), _row_spec(), _row_spec(HEAD_DIM)],
        out_specs=[tile, tile, pl.BlockSpec((1, HEADS, HEAD_DIM, HEAD_DIM), lambda c: (c, 0, 0, 0))],
        out_shape=[jax.ShapeDtypeStruct((seq, D_MODEL), F32),
                   jax.ShapeDtypeStruct((seq, D_MODEL), MXU_DTYPE),
                   jax.ShapeDtypeStruct((n_chunks, HEADS, HEAD_DIM, HEAD_DIM), F32)],
        scratch_shapes=[pltpu.VMEM((HEADS, HEAD_DIM, HEAD_DIM), F32)],
        compiler_params=_params(dimension_semantics=("arbitrary",)),
    )(z, z, z, z, lb_l, gn_l)


def _rms_parts(y):
    rs = lax.rsqrt(jnp.mean(y * y, axis=-1, keepdims=True) + NORM_EPS)
    return rs, y * rs


def _merge_fwd(a_in, b_in, z, x, wpo_g, who_g, wout_g, gate, g_post, tm, name):
    seq = x.shape[0]

    def body(a_ref, b_ref, mgp_ref, mgh_ref, x_ref, wpo_ref, who_ref, wout_ref, gate_ref, gp_ref,
             ba_ref, bb_ref, mer_ref, y_ref, xn_ref):
        a = a_ref[...]
        ba = jnp.concatenate([_dot(a, wpo_ref[j]) for j in range(N_DEV)], axis=1)
        bb = _dot(b_ref[...], who_ref[...])
        merged = _sigmoid(mgp_ref[...]) * ba + _sigmoid(mgh_ref[...]) * bb
        y = _dot(merged, wout_ref[...])
        _, yn = _rms_parts(y)
        ba_ref[...] = ba.astype(ba_ref.dtype)
        bb_ref[...] = bb.astype(bb_ref.dtype)
        mer_ref[...] = merged.astype(mer_ref.dtype)
        y_ref[...] = y
        xn_ref[...] = x_ref[...] + gate_ref[...] * (yn * gp_ref[...])

    def tile(cols=D_MODEL, block=0):
        return pl.BlockSpec((tm, cols), lambda i: (i, block))

    full = pl.BlockSpec((D_MODEL, D_MODEL), lambda i: (0, 0))
    act = jax.ShapeDtypeStruct((seq, D_MODEL), MXU_DTYPE)
    f32 = jax.ShapeDtypeStruct((seq, D_MODEL), F32)
    return pl.pallas_call(
        body, name=name, grid=(seq // tm,),
        in_specs=[tile(POOL_WIDTH), tile(), tile(block=COL_MGP), tile(block=COL_MGH), tile(),
                  pl.BlockSpec((N_DEV, POOL_WIDTH, GROUP_DIM), lambda i: (0, 0, 0)),
                  full, full, _row_spec(), _row_spec()],
        out_specs=[tile(), tile(), tile(), tile(), tile()],
        out_shape=[act, act, act, f32, f32],
        compiler_params=_params(dimension_semantics=("parallel",)),
    )(a_in, b_in, z, z, x, wpo_g, who_g, wout_g, gate, g_post)


def _loss_grad(x_out, target, tm):
    seq = x_out.shape[0]

    def body(x_ref, t_ref, loss_ref, dx_ref):
        @pl.when(pl.program_id(0) == 0)
        def _():
            loss_ref[...] = jnp.zeros_like(loss_ref)

        err = x_ref[...] - t_ref[...]
        per_token = jnp.mean(err * err, axis=-1, keepdims=True)
        loss_ref[...] += 0.5 * jnp.sum(per_token, axis=0, keepdims=True)
        dx_ref[...] = err * (1.0 / D_MODEL)

    tile = pl.BlockSpec((tm, D_MODEL), lambda i: (i, 0))
    return pl.pallas_call(
        body, name="loss_grad", grid=(seq // tm,),
        in_specs=[tile, tile],
        out_specs=[pl.BlockSpec((1, 1), lambda i: (0, 0)), tile],
        out_shape=[jax.ShapeDtypeStruct((1, 1), F32), jax.ShapeDtypeStruct((seq, D_MODEL), F32)],
        compiler_params=_params(dimension_semantics=("arbitrary",)),
    )(x_out, target)


def _merge_bwd(dx, y, ba, bb, z, wpo_g, who_g, wout_g, gate, g_post, tm, name):
    seq = dx.shape[0]

    def body(dx_ref, y_ref, ba_ref, bb_ref, mgp_ref, mgh_ref, wpo_ref, who_ref, wout_ref, gate_ref, gp_ref,
             dy_ref, dba_ref, dbb_ref, da_ref, db_ref, dmg_ref, acc_ref):
        @pl.when(pl.program_id(0) == 0)
        def _():
            acc_ref[...] = jnp.zeros_like(acc_ref)

        dxv = dx_ref[...]
        rs, yn = _rms_parts(y_ref[...])
        acc_ref[0:1, :] += jnp.sum(dxv * yn * gp_ref[...], axis=0, keepdims=True)
        acc_ref[1:2, :] += jnp.sum(dxv * gate_ref[...] * yn, axis=0, keepdims=True)
        dyn = dxv * (gate_ref[...] * gp_ref[...])
        dy = rs * (dyn - yn * jnp.mean(dyn * yn, axis=-1, keepdims=True))
        dmerged = _dot_nt(dy, wout_ref[...])
        sp, sh = _sigmoid(mgp_ref[...]), _sigmoid(mgh_ref[...])
        dba, dbb = sp * dmerged, sh * dmerged
        dmg_ref[:, 0:D_MODEL] = (dmerged * ba_ref[...].astype(F32) * sp * (1.0 - sp)).astype(dmg_ref.dtype)
        dmg_ref[:, D_MODEL:2 * D_MODEL] = (dmerged * bb_ref[...].astype(F32) * sh * (1.0 - sh)).astype(dmg_ref.dtype)
        da = _dot_nt(dba[:, 0:GROUP_DIM], wpo_ref[0])
        for j in range(1, N_DEV):
            da += _dot_nt(dba[:, j * GROUP_DIM:(j + 1) * GROUP_DIM], wpo_ref[j])
        dy_ref[...] = dy.astype(dy_ref.dtype)
        dba_ref[...] = dba.astype(dba_ref.dtype)
        dbb_ref[...] = dbb.astype(dbb_ref.dtype)
        da_ref[...] = da
        db_ref[...] = _dot_nt(dbb, who_ref[...])

    def tile(cols=D_MODEL, block=0):
        return pl.BlockSpec((tm, cols), lambda i: (i, block))

    full = pl.BlockSpec((D_MODEL, D_MODEL), lambda i: (0, 0))
    act = jax.ShapeDtypeStruct((seq, D_MODEL), MXU_DTYPE)
    return pl.pallas_call(
        body, name=name, grid=(seq // tm,),
        in_specs=[tile(), tile(), tile(), tile(), tile(block=COL_MGP), tile(block=COL_MGH),
                  pl.BlockSpec((N_DEV, POOL_WIDTH, GROUP_DIM), lambda i: (0, 0, 0)),
                  full, full, _row_spec(), _row_spec()],
        out_specs=[tile(), tile(), tile(), tile(POOL_WIDTH), tile(), tile(2 * D_MODEL),
                   pl.BlockSpec((8, D_MODEL), lambda i: (0, 0))],
        out_shape=[act, act, act, jax.ShapeDtypeStruct((seq, POOL_WIDTH), F32),
                   jax.ShapeDtypeStruct((seq, D_MODEL), F32),
                   jax.ShapeDtypeStruct((seq, 2 * D_MODEL), MXU_DTYPE),
                   jax.ShapeDtypeStruct((8, D_MODEL), F32)],
        compiler_params=_params(dimension_semantics=("arbitrary",)),
    )(dx, y, ba, bb, z, z, wpo_g, who_g, wout_g, gate, g_post)


def _grad_tn(a, b, tn, dev_major, name):
    seq, ka = a.shape
    n = b.shape[1]

    def body(a_ref, b_ref, out_ref):
        out_ref[...] = _dot_tn(a_ref[...], b_ref[...]).astype(out_ref.dtype)

    if dev_major:
        out_spec = pl.BlockSpec((None, ka, tn), lambda j: (j, 0, 0))
        out_shape = jax.ShapeDtypeStruct((n // tn, ka, tn), WIRE_DTYPE)
    else:
        out_spec = pl.BlockSpec((ka, tn), lambda j: (0, j))
        out_shape = jax.ShapeDtypeStruct((ka, n), WIRE_DTYPE)
    return pl.pallas_call(
        body, name=name, grid=(n // tn,),
        in_specs=[pl.BlockSpec((seq, ka), lambda j: (0, 0)), pl.BlockSpec((seq, tn), lambda j: (0, j))],
        out_specs=out_spec, out_shape=out_shape,
        compiler_params=_params(dimension_semantics=("parallel",)),
    )(a, b)


def _hgrn_bwd(db_in, z, o, states, lb_l, gn_l, name):
    seq = z.shape[0]
    n_chunks = seq // CHUNK

    def body(db_ref, hq_ref, hf_ref, hi_ref, hg_ref, o_ref, st_ref, lb_ref, gn_ref,
             dz_ref, dlb_ref, dgn_ref, dstate, dq_buf, dk_buf):
        @pl.when(pl.program_id(0) == 0)
        def _():
            dstate[...] = jnp.zeros_like(dstate)
            dlb_ref[...] = jnp.zeros_like(dlb_ref)
            dgn_ref[...] = jnp.zeros_like(dgn_ref)

        causal, tri, before_sub = _chunk_masks()
        lb = lb_ref[...]
        sg, f, logf = _gates(hf_ref[...], lb)
        kk = 1.0 - f
        hq = hq_ref[...]
        sq = _sigmoid(hq)
        q = hq * sq
        cum = jnp.dot(tri, logf, precision=HIGHEST, preferred_element_type=F32)
        base = jnp.dot(before_sub, logf, precision=HIGHEST, preferred_element_type=F32)
        gn = gn_ref[...]
        dgn = jnp.zeros((1, HEAD_DIM), F32)
        dlast = []
        for h in range(HEADS):
            sl = slice(h * HEAD_DIM, (h + 1) * HEAD_DIM)
            q_h, k_h, cum_h = q[:, sl], kk[:, sl], cum[:, sl]
            v_h = hi_ref[:, sl]
            st_h = st_ref[0, h]
            dst_h = dstate[h]
            rs, ohat = _rms_parts(o_ref[:, sl])
            hg = hg_ref[:, sl]
            shg = _sigmoid(hg)
            d_bin = db_ref[:, sl]
            don = d_bin * (hg * shg)
            dgn += jnp.sum(don * ohat, axis=0, keepdims=True)
            dohat = don * gn
            do = rs * (dohat - ohat * jnp.mean(dohat * ohat, axis=-1, keepdims=True))
            dz_ref[:, 3 * D_MODEL + h * HEAD_DIM:3 * D_MODEL + (h + 1) * HEAD_DIM] = (
                d_bin * (ohat * gn) * _dsilu(hg, shg)).astype(dz_ref.dtype)
            last = jnp.sum(logf[:, sl], axis=0, keepdims=True)
            g_in = jnp.exp(cum_h)
            d_out = jnp.exp(last - cum_h)
            q_bar, k_bar = q_h * g_in, k_h * d_out
            blocks = _intra_blocks(q_h, k_h, cum_h, base[:, sl], causal)
            a = jnp.concatenate([b[4] for b in blocks], axis=0)
            da = jnp.where(causal, _dot_nt(do, v_h), 0.0)
            dv = _dot_tn(a, do) + _dot_nt(k_bar, dst_h)
            dq_parts = []
            dk_bar = _dot(v_h, dst_h)
            dk = dk_bar * d_out
            dlast.append(jnp.sum(k_bar * dk_bar, axis=0, keepdims=True)
                         + jnp.exp(last) * jnp.sum(st_h * dst_h, axis=0, keepdims=True))
            for i, (q_t, k_t, e_q, e_k, _) in enumerate(blocks):
                da_i = da[i * SUB:(i + 1) * SUB]
                dq_parts.append(jnp.dot(da_i, k_t, precision=HIGHEST, preferred_element_type=F32) * e_q)
                dk += lax.dot_general(da_i, q_t, (((0,), (0,)), ((), ())), precision=HIGHEST,
                                      preferred_element_type=F32) * e_k
            dq = _dot(do, st_h) * g_in + jnp.concatenate(dq_parts, axis=0)
            dstate[h] = dst_h * jnp.exp(last) + _dot_tn(do, q_bar)
            dq_buf[:, sl] = dq
            dk_buf[:, sl] = dk
            dz_ref[:, 2 * D_MODEL + h * HEAD_DIM:2 * D_MODEL + (h + 1) * HEAD_DIM] = dv.astype(dz_ref.dtype)
        dgn_ref[...] += dgn
        dq_all, dk_all = dq_buf[...], dk_buf[...]
        dg = q * dq_all - kk * dk_all
        dlogf = lax.dot_general(tri, dg, (((0,), (0,)), ((), ())), precision=HIGHEST,
                                preferred_element_type=F32) + jnp.concatenate(dlast, axis=1)
        df = jnp.where(f > LOG_FLOOR, dlogf / f, 0.0) - dk_all
        dlb_ref[...] += jnp.sum(df * (1.0 - sg), axis=0, keepdims=True)
        dz_ref[:, 0:D_MODEL] = (dq_all * _dsilu(hq, sq)).astype(dz_ref.dtype)
        dz_ref[:, D_MODEL:2 * D_MODEL] = (df * (1.0 - lb) * sg * (1.0 - sg)).astype(dz_ref.dtype)

    last_chunk = n_chunks - 1

    def col(block):
        return pl.BlockSpec((CHUNK, D_MODEL), lambda c: (last_chunk - c, block))

    return pl.pallas_call(
        body, name=name, grid=(n_chunks,),
        in_specs=[col(0), col(COL_HQ), col(COL_HF), col(COL_HI), col(COL_HG), col(0),
                  pl.BlockSpec((1, HEADS, HEAD_DIM, HEAD_DIM), lambda c: (last_chunk - c, 0, 0, 0)),
                  _row_spec(), _row_spec(HEAD_DIM)],
        out_specs=[pl.BlockSpec((CHUNK, 4 * D_MODEL), lambda c: (last_chunk - c, 0)),
                   _row_spec(), _row_spec(HEAD_DIM)],
        out_shape=[jax.ShapeDtypeStruct((seq, 4 * D_MODEL), MXU_DTYPE),
                   jax.ShapeDtypeStruct((1, D_MODEL), F32), jax.ShapeDtypeStruct((1, HEAD_DIM), F32)],
        scratch_shapes=[pltpu.VMEM((HEADS, HEAD_DIM, HEAD_DIM), F32),
                        pltpu.VMEM((CHUNK, D_MODEL), F32), pltpu.VMEM((CHUNK, D_MODEL), F32)],
        compiler_params=_params(dimension_semantics=("arbitrary",)),
    )(db_in, z, z, z, z, o, states, lb_l, gn_l)


def _pool_bwd(da_in, z, pool_w_l, pool_scale_l, name):
    seq = z.shape[0]

    def body(da_ref, pv_ref, pg_ref, w_ref, sc_ref, dpv_ref, dpg_ref, dw_ref, dsc_ref):
        g = pl.program_id(0)
        pos = lax.broadcasted_iota(jnp.int32, (seq, GROUP_DIM), 0)
        pm, count = _pool_mean_minus_token(pv_ref[...], g, pos)
        lin0 = _dot(pm, w_ref[...])
        pg = pg_ref[...]
        spg = _sigmoid(pg)
        da = da_ref[...]
        dlin = da * (pg * spg)
        dpg_ref[...] = (da * (lin0 * sc_ref[...]) * _dsilu(pg, spg)).astype(dpg_ref.dtype)
        dsc_ref[...] = jnp.sum(dlin * lin0, axis=0, keepdims=True)
        dl0 = dlin * sc_ref[...]
        dw_ref[...] = _dot_tn(pm, dl0)
        dpm = _dot_nt(dl0, w_ref[...])
        sums, acc = [], dpm / count
        for j in (1, 2, 4, 8):
            acc = acc + _shift_up(acc, j, pos, seq)
            sums.append(acc)
        dpv_ref[...] = (_select_window(g, sums) - dpm).astype(dpv_ref.dtype)

    grp = pl.BlockSpec((seq, GROUP_DIM), lambda g: (0, g))
    return pl.pallas_call(
        body, name=name, grid=(POOL_GROUPS,),
        in_specs=[grp, grp, pl.BlockSpec((seq, GROUP_DIM), lambda g: (0, POOL_GROUPS + g)),
                  pl.BlockSpec((None, GROUP_DIM, GROUP_DIM), lambda g: (g, 0, 0)),
                  pl.BlockSpec((1, GROUP_DIM), lambda g: (0, g))],
        out_specs=[grp, grp, pl.BlockSpec((None, GROUP_DIM, GROUP_DIM), lambda g: (g, 0, 0)),
                   pl.BlockSpec((1, GROUP_DIM), lambda g: (0, g))],
        out_shape=[jax.ShapeDtypeStruct((seq, POOL_WIDTH), MXU_DTYPE),
                   jax.ShapeDtypeStruct((seq, POOL_WIDTH), MXU_DTYPE),
                   jax.ShapeDtypeStruct((POOL_GROUPS, GROUP_DIM, GROUP_DIM), F32),
                   jax.ShapeDtypeStruct((1, POOL_WIDTH), F32)],
        compiler_params=_params(dimension_semantics=("parallel",)),
    )(da_in, z, z, pool_w_l, pool_scale_l)


def _in_proj_dw(h, dz, name):
    seq = h.shape[0]

    def body(h_ref, dz_ref, out_ref):
        out_ref[...] = lax.dot_general(h_ref[...], dz_ref[...], (((0,), (0,)), ((), ())),
                                       preferred_element_type=F32).astype(out_ref.dtype)

    return pl.pallas_call(
        body, name=name, grid=(N_DEV,),
        in_specs=[pl.BlockSpec((seq, D_MODEL), lambda j: (0, 0)), pl.BlockSpec((seq, IN_COLS), lambda j: (0, j))],
        out_specs=pl.BlockSpec((None, D_MODEL, IN_COLS), lambda j: (j, 0, 0)),
        out_shape=jax.ShapeDtypeStruct((N_DEV, D_MODEL, IN_COLS), WIRE_DTYPE),
        compiler_params=_params(dimension_semantics=("parallel",)),
    )(h, dz)


def _in_proj_dh(dz, win_g, tm, name):
    seq = dz.shape[0]

    def body(dz_ref, w_ref, dh_ref):
        @pl.when(pl.program_id(1) == 0)
        def _():
            dh_ref[...] = jnp.zeros_like(dh_ref)

        dh_ref[...] += lax.dot_general(dz_ref[...], w_ref[...], (((1,), (1,)), ((), ())),
                                       preferred_element_type=F32)

    return pl.pallas_call(
        body, name=name, grid=(seq // tm, N_DEV),
        in_specs=[pl.BlockSpec((tm, IN_COLS), lambda i, j: (i, j)),
                  pl.BlockSpec((None, D_MODEL, IN_COLS), lambda i, j: (j, 0, 0))],
        out_specs=pl.BlockSpec((tm, D_MODEL), lambda i, j: (i, 0)),
        out_shape=jax.ShapeDtypeStruct((seq, D_MODEL), F32),
        compiler_params=_params(dimension_semantics=("parallel", "arbitrary")),
    )(dz, win_g)


def _prenorm_bwd(x, dh, dx_res, g, scale, tm, name):
    seq = x.shape[0]

    def body(x_ref, dh_ref, dxr_ref, g_ref, sc_ref, dx_ref, acc_ref):
        @pl.when(pl.program_id(0) == 0)
        def _():
            acc_ref[...] = jnp.zeros_like(acc_ref)

        rs, xn = _rms_parts(x_ref[...])
        dh = dh_ref[...]
        acc_ref[0:1, :] += jnp.sum(dh, axis=0, keepdims=True)
        acc_ref[1:2, :] += jnp.sum(dh * (xn * g_ref[...]), axis=0, keepdims=True)
        dhn = dh * (1.0 + sc_ref[...])
        acc_ref[2:3, :] += jnp.sum(dhn * xn, axis=0, keepdims=True)
        dxn = dhn * g_ref[...]
        dx_ref[...] = rs * (dxn - xn * jnp.mean(dxn * xn, axis=-1, keepdims=True)) + dxr_ref[...]

    tile = pl.BlockSpec((tm, D_MODEL), lambda i: (i, 0))
    return pl.pallas_call(
        body, name=name, grid=(seq // tm,),
        in_specs=[tile, tile, tile, _row_spec(), _row_spec()],
        out_specs=[tile, pl.BlockSpec((8, D_MODEL), lambda i: (0, 0))],
        out_shape=[jax.ShapeDtypeStruct((seq, D_MODEL), F32), jax.ShapeDtypeStruct((8, D_MODEL), F32)],
        compiler_params=_params(dimension_semantics=("arbitrary",)),
    )(x, dh, dx_res, g, scale)


def _adamw_math(w, g, m, v):
    m = ADAM_B1 * m + (1.0 - ADAM_B1) * g
    v = ADAM_B2 * v + (1.0 - ADAM_B2) * (g * g)
    m_hat = m / (1.0 - ADAM_B1 ** ADAM_STEP)
    v_hat = v / (1.0 - ADAM_B2 ** ADAM_STEP)
    delta = -ADAM_LR * (m_hat / (jnp.sqrt(v_hat) + ADAM_EPS) + ADAM_WD * w)
    return delta, m, v


def _adamw_sharded(w, m, v, contrib, tr, name):
    depth, rows, cols = w.shape
    n_parts = contrib.shape[1]

    def body(w_ref, m_ref, v_ref, c_ref, g_ref, d_ref, mo_ref, vo_ref):
        g = c_ref[0].astype(F32)
        for p in range(1, n_parts):
            g += c_ref[p].astype(F32)
        delta, mn, vn = _adamw_math(w_ref[...], g, m_ref[...], v_ref[...])
        g_ref[...] = g
        d_ref[...] = delta
        mo_ref[...] = mn
        vo_ref[...] = vn

    tile = pl.BlockSpec((None, tr, cols), lambda l, i: (l, i, 0))
    shape = jax.ShapeDtypeStruct(w.shape, F32)
    return pl.pallas_call(
        body, name=name, grid=(depth, rows // tr),
        in_specs=[tile, tile, tile, pl.BlockSpec((None, n_parts, tr, cols), lambda l, i: (l, 0, i, 0))],
        out_specs=[tile] * 4, out_shape=[shape] * 4,
        compiler_params=_params(dimension_semantics=("parallel", "parallel")),
    )(w, m, v, contrib)


def _adamw_layer(w, m, v, contribs, l, tr, name, prev=None):
    _, rows, cols = w.shape
    n = len(contribs)

    def body(*refs):
        w_ref, m_ref, v_ref = refs[:3]
        c_refs = refs[3:3 + n]
        g_ref, d_ref, mo_ref, vo_ref = refs[-4:]
        g = c_refs[0][...].astype(F32)
        for c_ref in c_refs[1:]:
            g += c_ref[...].astype(F32)
        delta, mn, vn = _adamw_math(w_ref[...], g, m_ref[...], v_ref[...])
        g_ref[...] = g
        d_ref[...] = delta
        mo_ref[...] = mn
        vo_ref[...] = vn

    tile = pl.BlockSpec((None, tr, cols), lambda i: (l, i, 0))
    in_specs = [tile, tile, tile] + [pl.BlockSpec((None, tr, cols), lambda i, s=slot: (s, i, 0)) for _, slot in contribs]
    operands = [w, m, v] + [arr for arr, _ in contribs]
    aliases = {}
    if prev is not None:
        aliases = {len(operands) + k: k for k in range(4)}
        in_specs += [pl.BlockSpec(memory_space=pl.ANY)] * 4
        operands += list(prev)
    shape = jax.ShapeDtypeStruct(w.shape, F32)
    return pl.pallas_call(
        body, name=name, grid=(rows // tr,), in_specs=in_specs, out_specs=[tile] * 4, out_shape=[shape] * 4,
        input_output_aliases=aliases,
        compiler_params=_params(dimension_semantics=("parallel",)),
    )(*operands)


def _adamw_small(w_pack, m_pack, v_pack, g_all):
    def body(w_ref, m_ref, v_ref, ga_ref, g_ref, d_ref, mo_ref, vo_ref):
        g = ga_ref[0]
        for d in range(1, N_DEV):
            g += ga_ref[d]
        w = w_ref[...]
        r0, r1, r2 = LB_ROW0, LB_ROW0 + 8, LB_ROW0 + 16
        lg0, lg1 = w[r0:r1], w[r1:r2]
        mx = jnp.maximum(lg0, lg1)
        e0, e1 = jnp.exp(lg0 - mx), jnp.exp(lg1 - mx)
        p0, p1 = e0 / (e0 + e1), e1 / (e0 + e1)
        low = ((p0 - p0), (p0 + p1) - p0)
        dlow = [g_rows * jnp.where((lo > 0.0) & (lo < 1.0), 1.0, jnp.where((lo == 0.0) | (lo == 1.0), 0.5, 0.0))
                for g_rows, lo in ((g[r0:r1], low[0]), (g[r1:r2], low[1]))]
        dp0 = (dlow[0] + dlow[1]) - (dlow[0] + dlow[1])
        dp1 = dlow[1]
        inner = p0 * dp0 + p1 * dp1
        g = jnp.concatenate([g[:r0], p0 * (dp0 - inner), p1 * (dp1 - inner), g[r2:]], axis=0)
        delta, mn, vn = _adamw_math(w, g, m_ref[...], v_ref[...])
        g_ref[...] = g
        d_ref[...] = delta
        mo_ref[...] = mn
        vo_ref[...] = vn

    shape = jax.ShapeDtypeStruct(w_pack.shape, F32)
    return pl.pallas_call(body, name="adamw_small", out_shape=[shape] * 4, compiler_params=_params())(
        w_pack, m_pack, v_pack, g_all)


def _pack_small(parts):
    rows = [parts[name].reshape(n, 128) for name, n in _SMALL_ROWS]
    used = sum(n for _, n in _SMALL_ROWS)
    rows.append(jnp.zeros((SMALL_ROWS_PAD - used, 128), F32))
    return jnp.concatenate(rows, axis=0)


def _unpack_small(pack, shapes):
    out, r = {}, 0
    for name, n in _SMALL_ROWS:
        out[name] = pack[r:r + n].reshape(shapes[name])
        r += n
    return out


def kernel(x, c, w_ada, b_ada, g_pre, g_post, w_in, pool_w, pool_scale, lb_logits, hgrn_norm_g, w_pool_o, w_hgrn_o, w_out, loss_target, m_w_ada, m_b_ada, m_g_pre, m_g_post, m_w_in, m_pool_w, m_pool_scale, m_lb_logits, m_hgrn_norm_g, m_w_pool_o, m_w_hgrn_o, m_w_out, v_w_ada, v_b_ada, v_g_pre, v_g_post, v_w_in, v_pool_w, v_pool_scale, v_lb_logits, v_hgrn_norm_g, v_w_pool_o, v_w_hgrn_o, v_w_out):
    seq = x.shape[1]
    tm = min(512, seq)
    tm_merge = min(256, seq)
    pos = _my_position()
    me = pos[3]

    big = dict(win=w_in, wpo=w_pool_o, who=w_hgrn_o, wout=w_out)
    units = [["win0"], ["wpo0", "who0", "wout0"], ["win1", "wpo1", "who1", "wout1"]]
    wbuf = {}
    for wname, arr in big.items():
        for l in range(DEPTH):
            wbuf[f"s_{wname}{l}"] = arr[l].astype(WIRE_DTYPE)
            wbuf[f"g_{wname}{l}"] = lax.empty((N_DEV,) + arr.shape[1:], WIRE_DTYPE)
    g_streams = [_gather_streams(keys) for keys in units]
    all_keys = [k for keys in units for k in keys]

    def own_shards(refs, p):
        return [(refs["s_" + k], refs["g_" + k].at[p[3]]) for k in all_keys]

    wbuf, g_sems, token = _comm_call(
        "gather_start", wbuf, start=[s for to_chips, to_sibling, _ in g_streams for s in (to_chips, to_sibling)],
        local=(len(all_keys), own_shards))

    def gather_finish(u, after):
        to_chips, to_sibling, pass_on = g_streams[u]
        sub = {p + k: wbuf[p + k] for k in units[u] for p in ("s_", "g_")}
        sub, (f_sems,), _ = _comm_call(f"gather_pass_{u}", sub, start=[pass_on], wait=[(to_chips, g_sems[2 * u])],
                                       after=after)
        sub, _, _ = _comm_call(f"gather_done_{u}", sub, wait=[(to_sibling, g_sems[2 * u + 1]), (pass_on, f_sems)])
        return {k: sub["g_" + k] for k in units[u]}

    c_all = _allgather_small(c + token[0:1, 0:1], "allgather_c").reshape(N_DEV, D_MODEL)
    b_cols = lax.dynamic_slice_in_dim(b_ada, me * ADA_COLS, ADA_COLS, axis=1)
    ada_part = _ada_fwd(c_all, w_ada, b_cols)
    ada_all = _allgather_small(ada_part.reshape(DEPTH * N_DEV, ADA_COLS), "allgather_ada")
    ada_all = ada_all.reshape(N_DEV, DEPTH, N_DEV, ADA_COLS)
    ada = lax.dynamic_index_in_dim(ada_all, me, axis=2, keepdims=False)
    ada = jnp.transpose(ada, (1, 0, 2)).reshape(DEPTH, 3 * D_MODEL)
    shift = [ada[l:l + 1, 0:D_MODEL] for l in range(DEPTH)]
    scale = [ada[l:l + 1, D_MODEL:2 * D_MODEL] for l in range(DEPTH)]
    gate = [ada[l:l + 1, 2 * D_MODEL:] for l in range(DEPTH)]
    lb = _lb_fwd(lb_logits)

    gw = {}
    xs, saved = [x[0]], []
    for l in range(DEPTH):
        h = _prenorm_fwd(xs[l], g_pre[l:l + 1], shift[l], scale[l], tm, f"prenorm_fwd_{l}")
        if l == 0:
            gw.update(gather_finish(0, h))
        z = _in_proj(h, gw[f"win{l}"], tm, f"in_proj_{l}")
        a_in = _pool_fwd(z, pool_w[l], pool_scale[l:l + 1], f"pool_fwd_{l}")
        if l == 0:
            gw.update(gather_finish(1, a_in))
        o, b_in, states = _hgrn_fwd(z, lb[l:l + 1], hgrn_norm_g[l:l + 1], f"hgrn_fwd_{l}")
        if l == 0:
            gw.update(gather_finish(2, b_in))
        who_l = gw[f"who{l}"].reshape(D_MODEL, D_MODEL)
        wout_l = gw[f"wout{l}"].reshape(D_MODEL, D_MODEL)
        ba, bb, merged, y, x_next = _merge_fwd(a_in, b_in, z, xs[l], gw[f"wpo{l}"], who_l, wout_l, gate[l],
                                               g_post[l:l + 1], tm_merge, f"merge_fwd_{l}")
        xs.append(x_next)
        saved.append((h, z, a_in, o, b_in, states, ba, bb, merged, y, who_l, wout_l))

    loss_part, dx = _loss_grad(xs[DEPTH], loss_target[0], tm)
    loss = lax.psum(loss_part[0, 0], ("x", "y", "c"))

    chips = _other_chips(pos)
    pair_idx = jnp.stack([_dev_index(cx, cy, pos[2]) for cx, cy in chips] + [me]).astype(jnp.int32)
    pair_rows = dict(win=256, wpo=POOL_WIDTH, who=HEAD_DIM, wout=HEAD_DIM)

    def scatter_pair_start(u, grads):
        keys = list(grads)
        pair, to_chips = _scatter_streams(keys)
        bufs = {}
        for k in keys:
            bufs["g_" + k] = grads[k]
            bufs["st_" + k] = lax.empty((4,) + grads[k].shape[1:], WIRE_DTYPE)
        bufs, (sems,), _ = _comm_call(f"scatter_pair_start_{u}", bufs, start=[pair])
        return dict(u=u, keys=keys, pair=pair, to_chips=to_chips, bufs=bufs, sems=sems)

    def scatter_chips_start(st, after):
        u, keys = st["u"], st["keys"]
        bufs, _, _ = _comm_call(f"scatter_pair_done_{u}", st["bufs"], wait=[(st["pair"], st["sems"])], after=after)
        bufs2 = {}
        for k in keys:
            bufs2["ps_" + k] = _pair_sum(bufs["g_" + k], bufs["st_" + k], pair_idx, pair_rows[k[:-1]], f"pair_sum_{k}")
            bufs2["ld_" + k] = lax.empty((3,) + bufs["g_" + k].shape[1:], WIRE_DTYPE)
        bufs2, (sems,), _ = _comm_call(f"scatter_chips_start_{u}", bufs2, start=[st["to_chips"]])
        st.update(bufs=bufs2, sems=sems)

    def scatter_finish(st, after):
        bufs, _, _ = _comm_call(f"scatter_chips_done_{st['u']}", st["bufs"], wait=[(st["to_chips"], st["sems"])],
                                after=after)
        return {k: [(bufs["ps_" + k], 3), (bufs["ld_" + k], 0), (bufs["ld_" + k], 1), (bufs["ld_" + k], 2)]
                for k in st["keys"]}

    d_ada, small, scat = [None] * DEPTH, [None] * DEPTH, {}
    for l in reversed(range(DEPTH)):
        h, z, a_in, o, b_in, states, ba, bb, merged, y, who_l, wout_l = saved[l]
        dy, dba, dbb, da_in, db_in, dmg, acc_post = _merge_bwd(
            dx, y, ba, bb, z, gw[f"wpo{l}"], who_l, wout_l, gate[l], g_post[l:l + 1], tm_merge, f"merge_bwd_{l}")
        g_small = {
            f"wout{l}": _grad_tn(merged, dy, 512, False, f"grad_w_out_{l}").reshape(N_DEV, HEAD_DIM, D_MODEL),
            f"who{l}": _grad_tn(b_in, dbb, 512, False, f"grad_w_hgrn_o_{l}").reshape(N_DEV, HEAD_DIM, D_MODEL),
            f"wpo{l}": _grad_tn(a_in, dba, GROUP_DIM, True, f"grad_w_pool_o_{l}")}
        scat[f"small{l}"] = scatter_pair_start(f"small{l}", g_small)
        dzh, dlb, dgn = _hgrn_bwd(db_in, z, o, states, lb[l:l + 1], hgrn_norm_g[l:l + 1], f"hgrn_bwd_{l}")
        scatter_chips_start(scat[f"small{l}"], dzh)
        dpv, dpg, dpw, dps = _pool_bwd(da_in, z, pool_w[l], pool_scale[l:l + 1], f"pool_bwd_{l}")
        dz = jnp.concatenate([dpv, dpg, dzh, dmg], axis=1)
        scat[f"win{l}"] = scatter_pair_start(f"win{l}", {f"win{l}": _in_proj_dw(h, dz, f"grad_w_in_{l}")})
        dh = _in_proj_dh(dz, gw[f"win{l}"], tm, f"in_proj_dh_{l}")
        scatter_chips_start(scat[f"win{l}"], dh)
        dx, acc_pre = _prenorm_bwd(xs[l], dh, dx, g_pre[l:l + 1], scale[l], tm, f"prenorm_bwd_{l}")
        d_ada[l] = jnp.concatenate([acc_pre[0], acc_pre[1], acc_post[0]])
        small[l] = dict(g_pre=acc_pre[2], g_post=acc_post[1], pool_w=dpw, pool_scale=dps[0], lb_logits=dlb[0],
                        hgrn_norm_g=dgn[0])
    grad_x = dx[None]

    small_names = [name for name, _ in _SMALL_ROWS]
    g_parts = {name: jnp.stack([small[l][name] for l in range(DEPTH)]) for name in small_names if name != "b_ada"}
    g_parts["b_ada"] = jnp.stack(d_ada)
    g_all = _allgather_small(_pack_small(g_parts), "allgather_small_grads")
    weights = dict(b_ada=b_ada, g_pre=g_pre, g_post=g_post, pool_w=pool_w, pool_scale=pool_scale,
                   lb_logits=lb_logits, hgrn_norm_g=hgrn_norm_g)
    m_small = dict(b_ada=m_b_ada, g_pre=m_g_pre, g_post=m_g_post, pool_w=m_pool_w, pool_scale=m_pool_scale,
                   lb_logits=m_lb_logits, hgrn_norm_g=m_hgrn_norm_g)
    v_small = dict(b_ada=v_b_ada, g_pre=v_g_pre, g_post=v_g_post, pool_w=v_pool_w, pool_scale=v_pool_scale,
                   lb_logits=v_lb_logits, hgrn_norm_g=v_hgrn_norm_g)
    shapes = {name: weights[name].shape for name in small_names}
    small_out = [_unpack_small(p, shapes) for p in
                 _adamw_small(_pack_small(weights), _pack_small(m_small), _pack_small(v_small), g_all)]

    d_ada_all = g_all[:, 0:48, :].reshape(N_DEV, DEPTH, 3 * D_MODEL)
    d_cols = jnp.transpose(lax.dynamic_slice_in_dim(d_ada_all, me * ADA_COLS, ADA_COLS, axis=2), (1, 0, 2))
    g_w_ada = _ada_bwd(c_all, d_cols)
    ada_out = _adamw_sharded(w_ada, m_w_ada, v_w_ada, g_w_ada[:, None], 256, "adamw_w_ada")

    moments = dict(win=(m_w_in, v_w_in), wpo=(m_w_pool_o, v_w_pool_o), who=(m_w_hgrn_o, v_w_hgrn_o),
                   wout=(m_w_out, v_w_out))
    big_out, after = {}, ada_out[0]
    for unit in ("small1", "win1", "small0", "win0"):
        for k, contribs in scatter_finish(scat[unit], after).items():
            wname, l = k[:-1], int(k[-1])
            big_out[wname] = _adamw_layer(big[wname], moments[wname][0], moments[wname][1], contribs, l,
                                          pair_rows[wname], f"adamw_{k}", prev=big_out.get(wname))
            after = big_out[wname][0]

    def leaf(kind):
        s = small_out[kind]
        return (ada_out[kind], s["b_ada"], s["g_pre"], s["g_post"], big_out["win"][kind], s["pool_w"], s["pool_scale"],
                s["lb_logits"], s["hgrn_norm_g"], big_out["wpo"][kind], big_out["who"][kind], big_out["wout"][kind])

    return (loss, grad_x) + leaf(0) + leaf(1) + leaf(2) + leaf(3)
```

```python
import jax
import jax.numpy as jnp
from jax import lax
from jax.experimental import pallas as pl
from jax.experimental.pallas import tpu as pltpu

F32 = jnp.float32
MXU_DTYPE = jnp.bfloat16
WIRE_DTYPE = jnp.bfloat16

N_DEV = 8
DEPTH = 2
D_MODEL = 1024
HEADS = 8
HEAD_DIM = 128
POOL_GROUPS = 4
GROUP_DIM = 128
POOL_WIDTH = POOL_GROUPS * GROUP_DIM
IN_WIDTH = 7168
CHUNK = 64
SUB = 16
N_SUB = CHUNK // SUB
EXP_CLAMP = 80.0
NORM_EPS = 1e-6
LOG_FLOOR = 1e-30
ADA_COLS = 3 * D_MODEL // N_DEV
IN_COLS = IN_WIDTH // N_DEV
COL_HQ, COL_HF, COL_HI, COL_HG, COL_MGP, COL_MGH = 1, 2, 3, 4, 5, 6

ADAM_LR = 0.001
ADAM_B1 = 0.9
ADAM_B2 = 0.999
ADAM_EPS = 1e-08
ADAM_WD = 0.01
ADAM_STEP = 10

VMEM_LIMIT = 48 * 1024 * 1024
MESH_ID = pl.DeviceIdType.MESH
HIGHEST = lax.Precision.HIGHEST

_SMALL_ROWS = (("b_ada", 48), ("g_pre", 16), ("g_post", 16), ("pool_w", 1024), ("pool_scale", 8),
               ("lb_logits", 16), ("hgrn_norm_g", 2))
SMALL_ROWS_PAD = 1136
LB_ROW0 = 48 + 16 + 16 + 1024 + 8


def _params(**kw):
    return pltpu.CompilerParams(vmem_limit_bytes=VMEM_LIMIT, **kw)


def _sigmoid(v):
    return 1.0 / (1.0 + jnp.exp(-v))


def _dsilu(v, s):
    return s * (1.0 + v * (1.0 - s))


def _dot(a, b):
    return jnp.dot(a.astype(MXU_DTYPE), b.astype(MXU_DTYPE), preferred_element_type=F32)


def _dot_nt(a, b):
    return lax.dot_general(a.astype(MXU_DTYPE), b.astype(MXU_DTYPE), (((1,), (1,)), ((), ())),
                           preferred_element_type=F32)


def _dot_tn(a, b):
    return lax.dot_general(a.astype(MXU_DTYPE), b.astype(MXU_DTYPE), (((0,), (0,)), ((), ())),
                           preferred_element_type=F32)


def _pallas_after(body, n_in, after, *, in_specs, **kw):
    if after is None:
        return pl.pallas_call(body, in_specs=in_specs, **kw)

    def tied(*refs):
        body(*refs[:n_in], *refs[n_in + 1:])

    call = pl.pallas_call(tied, in_specs=list(in_specs) + [pl.BlockSpec(memory_space=pl.ANY)], **kw)
    return lambda *operands: call(*operands, after)


def _my_position():
    mx, my, mc = lax.axis_index("x"), lax.axis_index("y"), lax.axis_index("c")
    return mx, my, mc, 4 * mx + 2 * my + mc


def _peer(mx, my, mc, k):
    px = 1 - mx if (k >> 2) & 1 else mx
    py = 1 - my if (k >> 1) & 1 else my
    pc = 1 - mc if k & 1 else mc
    return (px, py, pc), 4 * px + 2 * py + pc


def _allgather_small(v, name):
    rows, cols = v.shape

    def body(v_ref, out_ref, send_sems, recv_sems):
        mx, my, mc, me = _my_position()
        out_ref[me] = v_ref[...]
        copies = []
        for k in range(1, N_DEV):
            peer, _ = _peer(mx, my, mc, k)
            cp = pltpu.make_async_remote_copy(
                src_ref=v_ref, dst_ref=out_ref.at[me],
                send_sem=send_sems.at[k - 1], recv_sem=recv_sems.at[k - 1],
                device_id=peer, device_id_type=MESH_ID)
            cp.start()
            copies.append(cp)
        for cp in copies:
            cp.wait()

    return pl.pallas_call(
        body, name=name,
        out_shape=jax.ShapeDtypeStruct((N_DEV, rows, cols), v.dtype),
        in_specs=[pl.BlockSpec(memory_space=pltpu.VMEM)],
        out_specs=pl.BlockSpec(memory_space=pltpu.VMEM),
        scratch_shapes=[pltpu.SemaphoreType.DMA((N_DEV - 1,)), pltpu.SemaphoreType.DMA((N_DEV - 1,))],
        compiler_params=_params(),
    )(v)


class _Stream:
    def __init__(self, n, plan):
        self.n, self.plan = n, plan


def _comm_call(name, bufs, start=(), wait=(), after=None, local=None):
    names = list(bufs)

    def body(*refs):
        it = iter(refs)
        buf_refs = {n: next(it) for n in names}
        wait_sems = [(next(it), next(it)) for _ in wait]
        if after is not None:
            next(it)
        start_sems = [(next(it), next(it)) for _ in start]
        for _ in names:
            next(it)
        token = next(it)
        pos = _my_position()

        def descriptors(stream, sems):
            return [pltpu.make_async_remote_copy(src_ref=src, dst_ref=dst, send_sem=sems[0].at[k], recv_sem=sems[1].at[k],
                                                 device_id=dev, device_id_type=MESH_ID)
                    for k, (src, dst, dev) in enumerate(stream.plan(buf_refs, pos))]

        if local is not None:
            local_sems = next(it)
            copies = [pltpu.make_async_copy(src, dst, local_sems.at[k])
                      for k, (src, dst) in enumerate(local[1](buf_refs, pos))]
            for cp in copies:
                cp.start()
            for cp in copies:
                cp.wait()
        for (stream, _), sems in zip(wait, wait_sems):
            for cp in descriptors(stream, sems):
                cp.wait_send()
                cp.wait_recv()
        for stream, sems in zip(start, start_sems):
            for cp in descriptors(stream, sems):
                cp.start()
        token[...] = jnp.zeros_like(token)

    hbm = pl.BlockSpec(memory_space=pltpu.HBM)
    sem = pl.BlockSpec(memory_space=pltpu.SEMAPHORE)
    operands = [pltpu.with_memory_space_constraint(bufs[n], pltpu.HBM) for n in names]
    in_specs = [hbm] * len(names)
    for _, (send_sems, recv_sems) in wait:
        operands += [send_sems, recv_sems]
        in_specs += [sem, sem]
    if after is not None:
        operands.append(after)
        in_specs.append(pl.BlockSpec(memory_space=pl.ANY))
    out_shape, out_specs = [], []
    for stream in start:
        out_shape += [pltpu.SemaphoreType.DMA((stream.n,)), pltpu.SemaphoreType.DMA((stream.n,))]
        out_specs += [sem, sem]
    n_sem_out = len(out_shape)
    out_shape += [pltpu.HBM(bufs[n].shape, bufs[n].dtype) for n in names]
    out_specs += [hbm] * len(names)
    out_shape.append(jax.ShapeDtypeStruct((8, 128), F32))
    out_specs.append(pl.BlockSpec(memory_space=pltpu.VMEM))
    outs = pl.pallas_call(
        body, name=name, out_shape=out_shape, in_specs=in_specs, out_specs=out_specs,
        input_output_aliases={i: n_sem_out + i for i in range(len(names))},
        scratch_shapes=[pltpu.SemaphoreType.DMA((local[0],))] if local is not None else [],
        compiler_params=pltpu.CompilerParams(has_side_effects=pltpu.SideEffectType.DATAFLOW_SIDE_EFFECTING),
    )(*operands)
    sems = [(outs[2 * i], outs[2 * i + 1]) for i in range(len(start))]
    return dict(zip(names, outs[n_sem_out:n_sem_out + len(names)])), sems, outs[-1]


def _other_chips(pos):
    mx, my, _, _ = pos
    return [(1 - mx if i & 2 else mx, 1 - my if i & 1 else my) for i in (1, 2, 3)]


def _dev_index(px, py, pc):
    return 4 * px + 2 * py + pc


def _gather_streams(keys):
    def to_chips(refs, pos):
        _, _, mc, me = pos
        return [(refs["s_" + k], refs["g_" + k].at[me], (cx, cy, mc)) for k in keys for cx, cy in _other_chips(pos)]

    def to_sibling(refs, pos):
        mx, my, mc, me = pos
        return [(refs["s_" + k], refs["g_" + k].at[me], (mx, my, 1 - mc)) for k in keys]

    def pass_on(refs, pos):
        mx, my, mc, _ = pos
        out = []
        for k in keys:
            for cx, cy in _other_chips(pos):
                slot = refs["g_" + k].at[_dev_index(cx, cy, mc)]
                out.append((slot, slot, (mx, my, 1 - mc)))
        return out

    return _Stream(3 * len(keys), to_chips), _Stream(len(keys), to_sibling), _Stream(3 * len(keys), pass_on)


def _scatter_streams(keys):
    def pair(refs, pos):
        mx, my, mc, _ = pos
        sib = (mx, my, 1 - mc)
        out = []
        for k in keys:
            for i, (cx, cy) in enumerate(_other_chips(pos)):
                out.append((refs["g_" + k].at[_dev_index(cx, cy, 1 - mc)], refs["st_" + k].at[i], sib))
            out.append((refs["g_" + k].at[_dev_index(mx, my, 1 - mc)], refs["st_" + k].at[3], sib))
        return out

    def chips(refs, pos):
        mc = pos[2]
        return [(refs["ps_" + k].at[i], refs["ld_" + k].at[i], (cx, cy, mc))
                for k in keys for i, (cx, cy) in enumerate(_other_chips(pos))]

    return _Stream(4 * len(keys), pair), _Stream(3 * len(keys), chips)


def _pair_sum(g, st, idx, tr, name):
    _, rows, cols = g.shape

    def body(idx_ref, g_ref, st_ref, out_ref):
        out_ref[...] = (g_ref[...].astype(F32) + st_ref[...].astype(F32)).astype(out_ref.dtype)

    return pl.pallas_call(
        body, name=name,
        grid_spec=pltpu.PrefetchScalarGridSpec(
            num_scalar_prefetch=1, grid=(4, rows // tr),
            in_specs=[pl.BlockSpec((None, tr, cols), lambda j, i, idx_ref: (idx_ref[j], i, 0)),
                      pl.BlockSpec((None, tr, cols), lambda j, i, idx_ref: (j, i, 0))],
            out_specs=pl.BlockSpec((None, tr, cols), lambda j, i, idx_ref: (j, i, 0))),
        out_shape=jax.ShapeDtypeStruct((4, rows, cols), WIRE_DTYPE),
        compiler_params=_params(dimension_semantics=("parallel", "parallel")),
    )(idx, g, st)


def _ada_fwd(c_all, w_ada, b_cols):
    def body(c_ref, w_ref, b_ref, out_ref):
        cv = c_ref[...]
        ca = cv * _sigmoid(cv)
        for l in range(DEPTH):
            out_ref[l] = jnp.dot(ca, w_ref[l], precision=HIGHEST, preferred_element_type=F32) + b_ref[l:l + 1, :]

    return pl.pallas_call(
        body, name="ada_fwd",
        out_shape=jax.ShapeDtypeStruct((DEPTH, N_DEV, ADA_COLS), F32),
        compiler_params=_params(),
    )(c_all, w_ada, b_cols)


def _ada_bwd(c_all, d_cols):
    def body(c_ref, d_ref, out_ref):
        cv = c_ref[...]
        ca = cv * _sigmoid(cv)
        for l in range(DEPTH):
            out_ref[l] = lax.dot_general(ca, d_ref[l], (((0,), (0,)), ((), ())), precision=HIGHEST,
                                         preferred_element_type=F32)

    return pl.pallas_call(
        body, name="ada_bwd",
        out_shape=jax.ShapeDtypeStruct((DEPTH, D_MODEL, ADA_COLS), F32),
        compiler_params=_params(),
    )(c_all, d_cols)


def _lower_bounds(logits):
    m = jnp.maximum(logits[0:1], logits[1:2])
    e0, e1 = jnp.exp(logits[0:1] - m), jnp.exp(logits[1:2] - m)
    den = e0 + e1
    p0, p1 = e0 / den, e1 / den
    low0 = p0 - p0
    low1 = (p0 + p1) - p0
    return (p0, p1), (low0, low1)


def _lb_fwd(lb_logits):
    def body(lg_ref, out_ref):
        _, (low0, low1) = _lower_bounds(lg_ref[...])
        out_ref[0:1, :] = jnp.clip(low0, 0.0, 1.0)
        out_ref[1:2, :] = jnp.clip(low1, 0.0, 1.0)

    return pl.pallas_call(body, name="lb_fwd", out_shape=jax.ShapeDtypeStruct(lb_logits.shape, F32),
                          compiler_params=_params())(lb_logits)


def _row_spec(cols=D_MODEL):
    return pl.BlockSpec((1, cols), lambda *_: (0, 0))


def _prenorm_fwd(x, g, shift, scale, tm, name, after=None):
    seq = x.shape[0]

    def body(x_ref, g_ref, sh_ref, sc_ref, h_ref):
        xv = x_ref[...]
        rs = lax.rsqrt(jnp.mean(xv * xv, axis=-1, keepdims=True) + NORM_EPS)
        h = (xv * rs * g_ref[...]) * (1.0 + sc_ref[...]) + sh_ref[...]
        h_ref[...] = h.astype(h_ref.dtype)

    tile = pl.BlockSpec((tm, D_MODEL), lambda i: (i, 0))
    return _pallas_after(
        body, 4, after, name=name, grid=(seq // tm,),
        in_specs=[tile, _row_spec(), _row_spec(), _row_spec()], out_specs=tile,
        out_shape=jax.ShapeDtypeStruct((seq, D_MODEL), MXU_DTYPE),
        compiler_params=_params(dimension_semantics=("parallel",)),
    )(x, g, shift, scale)


def _in_proj(h, win_g, tm, name):
    seq = h.shape[0]

    def body(h_ref, w_ref, z_ref):
        z_ref[...] = jnp.dot(h_ref[...], w_ref[...], preferred_element_type=F32)

    return pl.pallas_call(
        body, name=name, grid=(N_DEV, seq // tm),
        in_specs=[pl.BlockSpec((tm, D_MODEL), lambda j, i: (i, 0)),
                  pl.BlockSpec((None, D_MODEL, IN_COLS), lambda j, i: (j, 0, 0))],
        out_specs=pl.BlockSpec((tm, IN_COLS), lambda j, i: (i, j)),
        out_shape=jax.ShapeDtypeStruct((seq, IN_WIDTH), F32),
        compiler_params=_params(dimension_semantics=("parallel", "parallel")),
    )(h, win_g)


def _shift_down(v, j, pos):
    return jnp.where(pos >= j, pltpu.roll(v, j, 0), 0.0)


def _shift_up(v, j, pos, seq):
    return jnp.where(pos < seq - j, pltpu.roll(v, seq - j, 0), 0.0)


def _select_window(g, candidates):
    out = candidates[-1]
    for i in range(len(candidates) - 2, -1, -1):
        out = jnp.where(g == i, candidates[i], out)
    return out


def _pool_mean_minus_token(u, g, pos):
    sums, acc = [], u
    for j in (1, 2, 4, 8):
        acc = acc + _shift_down(acc, j, pos)
        sums.append(acc)
    wsum = _select_window(g, sums)
    width = jnp.left_shift(2, g).astype(F32)
    count = jnp.minimum(pos.astype(F32) + 1.0, width)
    return wsum / count - u, count


def _pool_fwd(z, pool_w_l, pool_scale_l, name):
    seq = z.shape[0]

    def body(pv_ref, pg_ref, w_ref, sc_ref, out_ref):
        g = pl.program_id(0)
        pos = lax.broadcasted_iota(jnp.int32, (seq, GROUP_DIM), 0)
        pm, _ = _pool_mean_minus_token(pv_ref[...], g, pos)
        lin = _dot(pm, w_ref[...]) * sc_ref[...]
        pg = pg_ref[...]
        out_ref[...] = (lin * (pg * _sigmoid(pg))).astype(out_ref.dtype)

    return pl.pallas_call(
        body, name=name, grid=(POOL_GROUPS,),
        in_specs=[pl.BlockSpec((seq, GROUP_DIM), lambda g: (0, g)),
                  pl.BlockSpec((seq, GROUP_DIM), lambda g: (0, POOL_GROUPS + g)),
                  pl.BlockSpec((None, GROUP_DIM, GROUP_DIM), lambda g: (g, 0, 0)),
                  pl.BlockSpec((1, GROUP_DIM), lambda g: (0, g))],
        out_specs=pl.BlockSpec((seq, GROUP_DIM), lambda g: (0, g)),
        out_shape=jax.ShapeDtypeStruct((seq, POOL_WIDTH), MXU_DTYPE),
        compiler_params=_params(dimension_semantics=("parallel",)),
    )(z, z, pool_w_l, pool_scale_l)


def _chunk_masks():
    row = lax.broadcasted_iota(jnp.int32, (CHUNK, CHUNK), 0)
    col = lax.broadcasted_iota(jnp.int32, (CHUNK, CHUNK), 1)
    causal = row >= col
    tri = causal.astype(F32)
    before_sub = (col < (row // SUB) * SUB).astype(F32)
    return causal, tri, before_sub


def _gates(zf, lb):
    sg = _sigmoid(zf)
    f = lb + (1.0 - lb) * sg
    logf = jnp.log(jnp.maximum(f, LOG_FLOOR))
    return sg, f, logf


def _intra_blocks(q_h, k_h, cum_h, base_h, causal):
    rel = cum_h - base_h
    out = []
    for i in range(N_SUB):
        rows = slice(i * SUB, (i + 1) * SUB)
        e_q = jnp.exp(rel[rows])
        base_i = jnp.concatenate([base_h[rows]] * N_SUB, axis=0)
        e_k = jnp.exp(jnp.minimum(base_i - cum_h, EXP_CLAMP))
        q_t = q_h[rows] * e_q
        k_t = k_h * e_k
        a_i = jnp.where(causal[rows], _dot_nt(q_t, k_t), 0.0)
        out.append((q_t, k_t, e_q, e_k, a_i))
    return out


def _hgrn_fwd(z, lb_l, gn_l, name):
    seq = z.shape[0]
    n_chunks = seq // CHUNK

    def body(hq_ref, hf_ref, hi_ref, hg_ref, lb_ref, gn_ref, o_ref, bin_ref, st_ref, state):
        @pl.when(pl.program_id(0) == 0)
        def _():
            state[...] = jnp.zeros_like(state)

        causal, tri, before_sub = _chunk_masks()
        _, f, logf = _gates(hf_ref[...], lb_ref[...])
        kk = 1.0 - f
        hq = hq_ref[...]
        q = hq * _sigmoid(hq)
        cum = jnp.dot(tri, logf, precision=HIGHEST, preferred_element_type=F32)
        base = jnp.dot(before_sub, logf, precision=HIGHEST, preferred_element_type=F32)
        st_ref[0] = state[...]
        for h in range(HEADS):
            sl = slice(h * HEAD_DIM, (h + 1) * HEAD_DIM)
            q_h, k_h, cum_h = q[:, sl], kk[:, sl], cum[:, sl]
            v_h = hi_ref[:, sl]
            st_h = state[h]
            blocks = _intra_blocks(q_h, k_h, cum_h, base[:, sl], causal)
            a = jnp.concatenate([b[4] for b in blocks], axis=0)
            o_h = _dot_nt(q_h * jnp.exp(cum_h), st_h) + _dot(a, v_h)
            last = jnp.sum(logf[:, sl], axis=0, keepdims=True)
            state[h] = st_h * jnp.exp(last) + _dot_tn(v_h, k_h * jnp.exp(last - cum_h))
            rs = lax.rsqrt(jnp.mean(o_h * o_h, axis=-1, keepdims=True) + NORM_EPS)
            hg = hg_ref[:, sl]
            o_ref[:, sl] = o_h
            bin_ref[:, sl] = ((o_h * rs * gn_ref[...]) * (hg * _sigmoid(hg))).astype(bin_ref.dtype)

    def col(block):
        return pl.BlockSpec((CHUNK, D_MODEL), lambda c: (c, block))

    tile = pl.BlockSpec((CHUNK, D_MODEL), lambda c: (c, 0))
    return pl.pallas_call(
        body, name=name, grid=(n_chunks,),
        in_specs=[col(COL_HQ), col(COL_HF), col(COL_HI), col(COL_HG), _row_spec(), _row_spec(HEAD_DIM)],
        out_specs=[tile, tile, pl.BlockSpec((1, HEADS, HEAD_DIM, HEAD_DIM), lambda c: (c, 0, 0, 0))],
        out_shape=[jax.ShapeDtypeStruct((seq, D_MODEL), F32),
                   jax.ShapeDtypeStruct((seq, D_MODEL), MXU_DTYPE),
                   jax.ShapeDtypeStruct((n_chunks, HEADS, HEAD_DIM, HEAD_DIM), F32)],
        scratch_shapes=[pltpu.VMEM((HEADS, HEAD_DIM, HEAD_DIM), F32)],
        compiler_params=_params(dimension_semantics=("arbitrary",)),
    )(z, z, z, z, lb_l, gn_l)


def _rms_parts(y):
    rs = lax.rsqrt(jnp.mean(y * y, axis=-1, keepdims=True) + NORM_EPS)
    return rs, y * rs


def _merge_fwd(a_in, b_in, z, x, wpo_g, who_g, wout_g, gate, g_post, tm, name):
    seq = x.shape[0]

    def body(a_ref, b_ref, mgp_ref, mgh_ref, x_ref, wpo_ref, who_ref, wout_ref, gate_ref, gp_ref,
             ba_ref, bb_ref, mer_ref, y_ref, xn_ref):
        a = a_ref[...]
        ba = jnp.concatenate([_dot(a, wpo_ref[j]) for j in range(N_DEV)], axis=1)
        bb = _dot(b_ref[...], who_ref[...])
        merged = _sigmoid(mgp_ref[...]) * ba + _sigmoid(mgh_ref[...]) * bb
        y = _dot(merged, wout_ref[...])
        _, yn = _rms_parts(y)
        ba_ref[...] = ba.astype(ba_ref.dtype)
        bb_ref[...] = bb.astype(bb_ref.dtype)
        mer_ref[...] = merged.astype(mer_ref.dtype)
        y_ref[...] = y
        xn_ref[...] = x_ref[...] + gate_ref[...] * (yn * gp_ref[...])

    def tile(cols=D_MODEL, block=0):
        return pl.BlockSpec((tm, cols), lambda i: (i, block))

    full = pl.BlockSpec((D_MODEL, D_MODEL), lambda i: (0, 0))
    act = jax.ShapeDtypeStruct((seq, D_MODEL), MXU_DTYPE)
    f32 = jax.ShapeDtypeStruct((seq, D_MODEL), F32)
    return pl.pallas_call(
        body, name=name, grid=(seq // tm,),
        in_specs=[tile(POOL_WIDTH), tile(), tile(block=COL_MGP), tile(block=COL_MGH), tile(),
                  pl.BlockSpec((N_DEV, POOL_WIDTH, GROUP_DIM), lambda i: (0, 0, 0)),
                  full, full, _row_spec(), _row_spec()],
        out_specs=[tile(), tile(), tile(), tile(), tile()],
        out_shape=[act, act, act, f32, f32],
        compiler_params=_params(dimension_semantics=("parallel",)),
    )(a_in, b_in, z, z, x, wpo_g, who_g, wout_g, gate, g_post)


def _loss_grad(x_out, target, tm):
    seq = x_out.shape[0]

    def body(x_ref, t_ref, loss_ref, dx_ref):
        @pl.when(pl.program_id(0) == 0)
        def _():
            loss_ref[...] = jnp.zeros_like(loss_ref)

        err = x_ref[...] - t_ref[...]
        per_token = jnp.mean(err * err, axis=-1, keepdims=True)
        loss_ref[...] += 0.5 * jnp.sum(per_token, axis=0, keepdims=True)
        dx_ref[...] = err * (1.0 / D_MODEL)

    tile = pl.BlockSpec((tm, D_MODEL), lambda i: (i, 0))
    return pl.pallas_call(
        body, name="loss_grad", grid=(seq // tm,),
        in_specs=[tile, tile],
        out_specs=[pl.BlockSpec((1, 1), lambda i: (0, 0)), tile],
        out_shape=[jax.ShapeDtypeStruct((1, 1), F32), jax.ShapeDtypeStruct((seq, D_MODEL), F32)],
        compiler_params=_params(dimension_semantics=("arbitrary",)),
    )(x_out, target)


def _merge_bwd(dx, y, ba, bb, z, wpo_g, who_g, wout_g, gate, g_post, tm, name):
    seq = dx.shape[0]

    def body(dx_ref, y_ref, ba_ref, bb_ref, mgp_ref, mgh_ref, wpo_ref, who_ref, wout_ref, gate_ref, gp_ref,
             dy_ref, dba_ref, dbb_ref, da_ref, db_ref, dmg_ref, acc_ref):
        @pl.when(pl.program_id(0) == 0)
        def _():
            acc_ref[...] = jnp.zeros_like(acc_ref)

        dxv = dx_ref[...]
        rs, yn = _rms_parts(y_ref[...])
        acc_ref[0:1, :] += jnp.sum(dxv * yn * gp_ref[...], axis=0, keepdims=True)
        acc_ref[1:2, :] += jnp.sum(dxv * gate_ref[...] * yn, axis=0, keepdims=True)
        dyn = dxv * (gate_ref[...] * gp_ref[...])
        dy = rs * (dyn - yn * jnp.mean(dyn * yn, axis=-1, keepdims=True))
        dmerged = _dot_nt(dy, wout_ref[...])
        sp, sh = _sigmoid(mgp_ref[...]), _sigmoid(mgh_ref[...])
        dba, dbb = sp * dmerged, sh * dmerged
        dmg_ref[:, 0:D_MODEL] = (dmerged * ba_ref[...].astype(F32) * sp * (1.0 - sp)).astype(dmg_ref.dtype)
        dmg_ref[:, D_MODEL:2 * D_MODEL] = (dmerged * bb_ref[...].astype(F32) * sh * (1.0 - sh)).astype(dmg_ref.dtype)
        da = _dot_nt(dba[:, 0:GROUP_DIM], wpo_ref[0])
        for j in range(1, N_DEV):
            da += _dot_nt(dba[:, j * GROUP_DIM:(j + 1) * GROUP_DIM], wpo_ref[j])
        dy_ref[...] = dy.astype(dy_ref.dtype)
        dba_ref[...] = dba.astype(dba_ref.dtype)
        dbb_ref[...] = dbb.astype(dbb_ref.dtype)
        da_ref[...] = da
        db_ref[...] = _dot_nt(dbb, who_ref[...])

    def tile(cols=D_MODEL, block=0):
        return pl.BlockSpec((tm, cols), lambda i: (i, block))

    full = pl.BlockSpec((D_MODEL, D_MODEL), lambda i: (0, 0))
    act = jax.ShapeDtypeStruct((seq, D_MODEL), MXU_DTYPE)
    return pl.pallas_call(
        body, name=name, grid=(seq // tm,),
        in_specs=[tile(), tile(), tile(), tile(), tile(block=COL_MGP), tile(block=COL_MGH),
                  pl.BlockSpec((N_DEV, POOL_WIDTH, GROUP_DIM), lambda i: (0, 0, 0)),
                  full, full, _row_spec(), _row_spec()],
        out_specs=[tile(), tile(), tile(), tile(POOL_WIDTH), tile(), tile(2 * D_MODEL),
                   pl.BlockSpec((8, D_MODEL), lambda i: (0, 0))],
        out_shape=[act, act, act, jax.ShapeDtypeStruct((seq, POOL_WIDTH), F32),
                   jax.ShapeDtypeStruct((seq, D_MODEL), F32),
                   jax.ShapeDtypeStruct((seq, 2 * D_MODEL), MXU_DTYPE),
                   jax.ShapeDtypeStruct((8, D_MODEL), F32)],
        compiler_params=_params(dimension_semantics=("arbitrary",)),
    )(dx, y, ba, bb, z, z, wpo_g, who_g, wout_g, gate, g_post)


def _grad_tn(a, b, tn, dev_major, name):
    seq, ka = a.shape
    n = b.shape[1]

    def body(a_ref, b_ref, out_ref):
        out_ref[...] = _dot_tn(a_ref[...], b_ref[...]).astype(out_ref.dtype)

    if dev_major:
        out_spec = pl.BlockSpec((None, ka, tn), lambda j: (j, 0, 0))
        out_shape = jax.ShapeDtypeStruct((n // tn, ka, tn), WIRE_DTYPE)
    else:
        out_spec = pl.BlockSpec((ka, tn), lambda j: (0, j))
        out_shape = jax.ShapeDtypeStruct((ka, n), WIRE_DTYPE)
    return pl.pallas_call(
        body, name=name, grid=(n // tn,),
        in_specs=[pl.BlockSpec((seq, ka), lambda j: (0, 0)), pl.BlockSpec((seq, tn), lambda j: (0, j))],
        out_specs=out_spec, out_shape=out_shape,
        compiler_params=_params(dimension_semantics=("parallel",)),
    )(a, b)


def _hgrn_bwd(db_in, z, o, states, lb_l, gn_l, name, after=None):
    seq = z.shape[0]
    n_chunks = seq // CHUNK

    def body(db_ref, hq_ref, hf_ref, hi_ref, hg_ref, o_ref, st_ref, lb_ref, gn_ref,
             dz_ref, dlb_ref, dgn_ref, dstate, dq_buf, dk_buf):
        @pl.when(pl.program_id(0) == 0)
        def _():
            dstate[...] = jnp.zeros_like(dstate)
            dlb_ref[...] = jnp.zeros_like(dlb_ref)
            dgn_ref[...] = jnp.zeros_like(dgn_ref)

        causal, tri, before_sub = _chunk_masks()
        lb = lb_ref[...]
        sg, f, logf = _gates(hf_ref[...], lb)
        kk = 1.0 - f
        hq = hq_ref[...]
        sq = _sigmoid(hq)
        q = hq * sq
        cum = jnp.dot(tri, logf, precision=HIGHEST, preferred_element_type=F32)
        base = jnp.dot(before_sub, logf, precision=HIGHEST, preferred_element_type=F32)
        gn = gn_ref[...]
        dgn = jnp.zeros((1, HEAD_DIM), F32)
        dlast = []
        for h in range(HEADS):
            sl = slice(h * HEAD_DIM, (h + 1) * HEAD_DIM)
            q_h, k_h, cum_h = q[:, sl], kk[:, sl], cum[:, sl]
            v_h = hi_ref[:, sl]
            st_h = st_ref[0, h]
            dst_h = dstate[h]
            rs, ohat = _rms_parts(o_ref[:, sl])
            hg = hg_ref[:, sl]
            shg = _sigmoid(hg)
            d_bin = db_ref[:, sl]
            don = d_bin * (hg * shg)
            dgn += jnp.sum(don * ohat, axis=0, keepdims=True)
            dohat = don * gn
            do = rs * (dohat - ohat * jnp.mean(dohat * ohat, axis=-1, keepdims=True))
            dz_ref[:, 3 * D_MODEL + h * HEAD_DIM:3 * D_MODEL + (h + 1) * HEAD_DIM] = (
                d_bin * (ohat * gn) * _dsilu(hg, shg)).astype(dz_ref.dtype)
            last = jnp.sum(logf[:, sl], axis=0, keepdims=True)
            g_in = jnp.exp(cum_h)
            d_out = jnp.exp(last - cum_h)
            q_bar, k_bar = q_h * g_in, k_h * d_out
            blocks = _intra_blocks(q_h, k_h, cum_h, base[:, sl], causal)
            a = jnp.concatenate([b[4] for b in blocks], axis=0)
            da = jnp.where(causal, _dot_nt(do, v_h), 0.0)
            dv = _dot_tn(a, do) + _dot_nt(k_bar, dst_h)
            dq_parts = []
            dk_bar = _dot(v_h, dst_h)
            dk = dk_bar * d_out
            dlast.append(jnp.sum(k_bar * dk_bar, axis=0, keepdims=True)
                         + jnp.exp(last) * jnp.sum(st_h * dst_h, axis=0, keepdims=True))
            for i, (q_t, k_t, e_q, e_k, _) in enumerate(blocks):
                da_i = da[i * SUB:(i + 1) * SUB]
                dq_parts.append(jnp.dot(da_i, k_t, precision=HIGHEST, preferred_element_type=F32) * e_q)
                dk += lax.dot_general(da_i, q_t, (((0,), (0,)), ((), ())), precision=HIGHEST,
                                      preferred_element_type=F32) * e_k
            dq = _dot(do, st_h) * g_in + jnp.concatenate(dq_parts, axis=0)
            dstate[h] = dst_h * jnp.exp(last) + _dot_tn(do, q_bar)
            dq_buf[:, sl] = dq
            dk_buf[:, sl] = dk
            dz_ref[:, 2 * D_MODEL + h * HEAD_DIM:2 * D_MODEL + (h + 1) * HEAD_DIM] = dv.astype(dz_ref.dtype)
        dgn_ref[...] += dgn
        dq_all, dk_all = dq_buf[...], dk_buf[...]
        dg = q * dq_all - kk * dk_all
        dlogf = lax.dot_general(tri, dg, (((0,), (0,)), ((), ())), precision=HIGHEST,
                                preferred_element_type=F32) + jnp.concatenate(dlast, axis=1)
        df = jnp.where(f > LOG_FLOOR, dlogf / f, 0.0) - dk_all
        dlb_ref[...] += jnp.sum(df * (1.0 - sg), axis=0, keepdims=True)
        dz_ref[:, 0:D_MODEL] = (dq_all * _dsilu(hq, sq)).astype(dz_ref.dtype)
        dz_ref[:, D_MODEL:2 * D_MODEL] = (df * (1.0 - lb) * sg * (1.0 - sg)).astype(dz_ref.dtype)

    last_chunk = n_chunks - 1

    def col(block):
        return pl.BlockSpec((CHUNK, D_MODEL), lambda c: (last_chunk - c, block))

    return _pallas_after(
        body, 9, after, name=name, grid=(n_chunks,),
        in_specs=[col(0), col(COL_HQ), col(COL_HF), col(COL_HI), col(COL_HG), col(0),
                  pl.BlockSpec((1, HEADS, HEAD_DIM, HEAD_DIM), lambda c: (last_chunk - c, 0, 0, 0)),
                  _row_spec(), _row_spec(HEAD_DIM)],
        out_specs=[pl.BlockSpec((CHUNK, 4 * D_MODEL), lambda c: (last_chunk - c, 0)),
                   _row_spec(), _row_spec(HEAD_DIM)],
        out_shape=[jax.ShapeDtypeStruct((seq, 4 * D_MODEL), MXU_DTYPE),
                   jax.ShapeDtypeStruct((1, D_MODEL), F32), jax.ShapeDtypeStruct((1, HEAD_DIM), F32)],
        scratch_shapes=[pltpu.VMEM((HEADS, HEAD_DIM, HEAD_DIM), F32),
                        pltpu.VMEM((CHUNK, D_MODEL), F32), pltpu.VMEM((CHUNK, D_MODEL), F32)],
        compiler_params=_params(dimension_semantics=("arbitrary",)),
    )(db_in, z, z, z, z, o, states, lb_l, gn_l)


def _pool_bwd(da_in, z, pool_w_l, pool_scale_l, name, after=None):
    seq = z.shape[0]

    def body(da_ref, pv_ref, pg_ref, w_ref, sc_ref, dpv_ref, dpg_ref, dw_ref, dsc_ref):
        g = pl.program_id(0)
        pos = lax.broadcasted_iota(jnp.int32, (seq, GROUP_DIM), 0)
        pm, count = _pool_mean_minus_token(pv_ref[...], g, pos)
        lin0 = _dot(pm, w_ref[...])
        pg = pg_ref[...]
        spg = _sigmoid(pg)
        da = da_ref[...]
        dlin = da * (pg * spg)
        dpg_ref[...] = (da * (lin0 * sc_ref[...]) * _dsilu(pg, spg)).astype(dpg_ref.dtype)
        dsc_ref[...] = jnp.sum(dlin * lin0, axis=0, keepdims=True)
        dl0 = dlin * sc_ref[...]
        dw_ref[...] = _dot_tn(pm, dl0)
        dpm = _dot_nt(dl0, w_ref[...])
        sums, acc = [], dpm / count
        for j in (1, 2, 4, 8):
            acc = acc + _shift_up(acc, j, pos, seq)
            sums.append(acc)
        dpv_ref[...] = (_select_window(g, sums) - dpm).astype(dpv_ref.dtype)

    grp = pl.BlockSpec((seq, GROUP_DIM), lambda g: (0, g))
    return _pallas_after(
        body, 5, after, name=name, grid=(POOL_GROUPS,),
        in_specs=[grp, grp, pl.BlockSpec((seq, GROUP_DIM), lambda g: (0, POOL_GROUPS + g)),
                  pl.BlockSpec((None, GROUP_DIM, GROUP_DIM), lambda g: (g, 0, 0)),
                  pl.BlockSpec((1, GROUP_DIM), lambda g: (0, g))],
        out_specs=[grp, grp, pl.BlockSpec((None, GROUP_DIM, GROUP_DIM), lambda g: (g, 0, 0)),
                   pl.BlockSpec((1, GROUP_DIM), lambda g: (0, g))],
        out_shape=[jax.ShapeDtypeStruct((seq, POOL_WIDTH), MXU_DTYPE),
                   jax.ShapeDtypeStruct((seq, POOL_WIDTH), MXU_DTYPE),
                   jax.ShapeDtypeStruct((POOL_GROUPS, GROUP_DIM, GROUP_DIM), F32),
                   jax.ShapeDtypeStruct((1, POOL_WIDTH), F32)],
        compiler_params=_params(dimension_semantics=("parallel",)),
    )(da_in, z, z, pool_w_l, pool_scale_l)


def _in_proj_dw(h, dz, name):
    seq = h.shape[0]

    def body(h_ref, dz_ref, out_ref):
        out_ref[...] = lax.dot_general(h_ref[...], dz_ref[...], (((0,), (0,)), ((), ())),
                                       preferred_element_type=F32).astype(out_ref.dtype)

    return pl.pallas_call(
        body, name=name, grid=(N_DEV,),
        in_specs=[pl.BlockSpec((seq, D_MODEL), lambda j: (0, 0)), pl.BlockSpec((seq, IN_COLS), lambda j: (0, j))],
        out_specs=pl.BlockSpec((None, D_MODEL, IN_COLS), lambda j: (j, 0, 0)),
        out_shape=jax.ShapeDtypeStruct((N_DEV, D_MODEL, IN_COLS), WIRE_DTYPE),
        compiler_params=_params(dimension_semantics=("parallel",)),
    )(h, dz)


def _in_proj_dh(dz, win_g, tm, name, after=None):
    seq = dz.shape[0]

    def body(dz_ref, w_ref, dh_ref):
        @pl.when(pl.program_id(1) == 0)
        def _():
            dh_ref[...] = jnp.zeros_like(dh_ref)

        dh_ref[...] += lax.dot_general(dz_ref[...], w_ref[...], (((1,), (1,)), ((), ())),
                                       preferred_element_type=F32)

    return _pallas_after(
        body, 2, after, name=name, grid=(seq // tm, N_DEV),
        in_specs=[pl.BlockSpec((tm, IN_COLS), lambda i, j: (i, j)),
                  pl.BlockSpec((None, D_MODEL, IN_COLS), lambda i, j: (j, 0, 0))],
        out_specs=pl.BlockSpec((tm, D_MODEL), lambda i, j: (i, 0)),
        out_shape=jax.ShapeDtypeStruct((seq, D_MODEL), F32),
        compiler_params=_params(dimension_semantics=("parallel", "arbitrary")),
    )(dz, win_g)


def _prenorm_bwd(x, dh, dx_res, g, scale, tm, name, after=None):
    seq = x.shape[0]

    def body(x_ref, dh_ref, dxr_ref, g_ref, sc_ref, dx_ref, acc_ref):
        @pl.when(pl.program_id(0) == 0)
        def _():
            acc_ref[...] = jnp.zeros_like(acc_ref)

        rs, xn = _rms_parts(x_ref[...])
        dh = dh_ref[...]
        acc_ref[0:1, :] += jnp.sum(dh, axis=0, keepdims=True)
        acc_ref[1:2, :] += jnp.sum(dh * (xn * g_ref[...]), axis=0, keepdims=True)
        dhn = dh * (1.0 + sc_ref[...])
        acc_ref[2:3, :] += jnp.sum(dhn * xn, axis=0, keepdims=True)
        dxn = dhn * g_ref[...]
        dx_ref[...] = rs * (dxn - xn * jnp.mean(dxn * xn, axis=-1, keepdims=True)) + dxr_ref[...]

    tile = pl.BlockSpec((tm, D_MODEL), lambda i: (i, 0))
    return _pallas_after(
        body, 5, after, name=name, grid=(seq // tm,),
        in_specs=[tile, tile, tile, _row_spec(), _row_spec()],
        out_specs=[tile, pl.BlockSpec((8, D_MODEL), lambda i: (0, 0))],
        out_shape=[jax.ShapeDtypeStruct((seq, D_MODEL), F32), jax.ShapeDtypeStruct((8, D_MODEL), F32)],
        compiler_params=_params(dimension_semantics=("arbitrary",)),
    )(x, dh, dx_res, g, scale)


def _adamw_math(w, g, m, v):
    m = ADAM_B1 * m + (1.0 - ADAM_B1) * g
    v = ADAM_B2 * v + (1.0 - ADAM_B2) * (g * g)
    m_hat = m / (1.0 - ADAM_B1 ** ADAM_STEP)
    v_hat = v / (1.0 - ADAM_B2 ** ADAM_STEP)
    delta = -ADAM_LR * (m_hat / (jnp.sqrt(v_hat) + ADAM_EPS) + ADAM_WD * w)
    return delta, m, v


def _adamw_sharded(w, m, v, contrib, tr, name):
    depth, rows, cols = w.shape
    n_parts = contrib.shape[1]

    def body(w_ref, m_ref, v_ref, c_ref, g_ref, d_ref, mo_ref, vo_ref):
        g = c_ref[0].astype(F32)
        for p in range(1, n_parts):
            g += c_ref[p].astype(F32)
        delta, mn, vn = _adamw_math(w_ref[...], g, m_ref[...], v_ref[...])
        g_ref[...] = g
        d_ref[...] = delta
        mo_ref[...] = mn
        vo_ref[...] = vn

    tile = pl.BlockSpec((None, tr, cols), lambda l, i: (l, i, 0))
    shape = jax.ShapeDtypeStruct(w.shape, F32)
    return pl.pallas_call(
        body, name=name, grid=(depth, rows // tr),
        in_specs=[tile, tile, tile, pl.BlockSpec((None, n_parts, tr, cols), lambda l, i: (l, 0, i, 0))],
        out_specs=[tile] * 4, out_shape=[shape] * 4,
        compiler_params=_params(dimension_semantics=("parallel", "parallel")),
    )(w, m, v, contrib)


def _adamw_layer(w, m, v, contribs, l, tr, name, prev=None):
    _, rows, cols = w.shape
    n = len(contribs)

    def body(*refs):
        w_ref, m_ref, v_ref = refs[:3]
        c_refs = refs[3:3 + n]
        g_ref, d_ref, mo_ref, vo_ref = refs[-4:]
        g = c_refs[0][...].astype(F32)
        for c_ref in c_refs[1:]:
            g += c_ref[...].astype(F32)
        delta, mn, vn = _adamw_math(w_ref[...], g, m_ref[...], v_ref[...])
        g_ref[...] = g
        d_ref[...] = delta
        mo_ref[...] = mn
        vo_ref[...] = vn

    tile = pl.BlockSpec((None, tr, cols), lambda i: (l, i, 0))
    in_specs = [tile, tile, tile] + [pl.BlockSpec((None, tr, cols), lambda i, s=slot: (s, i, 0)) for _, slot in contribs]
    operands = [w, m, v] + [arr for arr, _ in contribs]
    aliases = {}
    if prev is not None:
        aliases = {len(operands) + k: k for k in range(4)}
        in_specs += [pl.BlockSpec(memory_space=pl.ANY)] * 4
        operands += list(prev)
    shape = jax.ShapeDtypeStruct(w.shape, F32)
    return pl.pallas_call(
        body, name=name, grid=(rows // tr,), in_specs=in_specs, out_specs=[tile] * 4, out_shape=[shape] * 4,
        input_output_aliases=aliases,
        compiler_params=_params(dimension_semantics=("parallel",)),
    )(*operands)


def _adamw_small(w_pack, m_pack, v_pack, g_all):
    def body(w_ref, m_ref, v_ref, ga_ref, g_ref, d_ref, mo_ref, vo_ref):
        g = ga_ref[0]
        for d in range(1, N_DEV):
            g += ga_ref[d]
        w = w_ref[...]
        r0, r1, r2 = LB_ROW0, LB_ROW0 + 8, LB_ROW0 + 16
        lg0, lg1 = w[r0:r1], w[r1:r2]
        mx = jnp.maximum(lg0, lg1)
        e0, e1 = jnp.exp(lg0 - mx), jnp.exp(lg1 - mx)
        p0, p1 = e0 / (e0 + e1), e1 / (e0 + e1)
        low = ((p0 - p0), (p0 + p1) - p0)
        dlow = [g_rows * jnp.where((lo > 0.0) & (lo < 1.0), 1.0, jnp.where((lo == 0.0) | (lo == 1.0), 0.5, 0.0))
                for g_rows, lo in ((g[r0:r1], low[0]), (g[r1:r2], low[1]))]
        dp0 = (dlow[0] + dlow[1]) - (dlow[0] + dlow[1])
        dp1 = dlow[1]
        inner = p0 * dp0 + p1 * dp1
        g = jnp.concatenate([g[:r0], p0 * (dp0 - inner), p1 * (dp1 - inner), g[r2:]], axis=0)
        delta, mn, vn = _adamw_math(w, g, m_ref[...], v_ref[...])
        g_ref[...] = g
        d_ref[...] = delta
        mo_ref[...] = mn
        vo_ref[...] = vn

    shape = jax.ShapeDtypeStruct(w_pack.shape, F32)
    return pl.pallas_call(body, name="adamw_small", out_shape=[shape] * 4, compiler_params=_params())(
        w_pack, m_pack, v_pack, g_all)


def _pack_small(parts):
    rows = [parts[name].reshape(n, 128) for name, n in _SMALL_ROWS]
    used = sum(n for _, n in _SMALL_ROWS)
    rows.append(jnp.zeros((SMALL_ROWS_PAD - used, 128), F32))
    return jnp.concatenate(rows, axis=0)


def _unpack_small(pack, shapes):
    out, r = {}, 0
    for name, n in _SMALL_ROWS:
        out[name] = pack[r:r + n].reshape(shapes[name])
        r += n
    return out


def kernel(x, c, w_ada, b_ada, g_pre, g_post, w_in, pool_w, pool_scale, lb_logits, hgrn_norm_g, w_pool_o, w_hgrn_o, w_out, loss_target, m_w_ada, m_b_ada, m_g_pre, m_g_post, m_w_in, m_pool_w, m_pool_scale, m_lb_logits, m_hgrn_norm_g, m_w_pool_o, m_w_hgrn_o, m_w_out, v_w_ada, v_b_ada, v_g_pre, v_g_post, v_w_in, v_pool_w, v_pool_scale, v_lb_logits, v_hgrn_norm_g, v_w_pool_o, v_w_hgrn_o, v_w_out):
    seq = x.shape[1]
    tm = min(512, seq)
    tm_merge = min(256, seq)
    pos = _my_position()
    me = pos[3]

    c_all = _allgather_small(c, "allgather_c").reshape(N_DEV, D_MODEL)
    b_cols = lax.dynamic_slice_in_dim(b_ada, me * ADA_COLS, ADA_COLS, axis=1)
    ada_part = _ada_fwd(c_all, w_ada, b_cols)
    ada_all = _allgather_small(ada_part.reshape(DEPTH * N_DEV, ADA_COLS), "allgather_ada")
    ada = lax.dynamic_index_in_dim(ada_all.reshape(N_DEV, DEPTH, N_DEV, ADA_COLS), me, axis=2, keepdims=False)
    ada = jnp.transpose(ada, (1, 0, 2)).reshape(DEPTH, 3 * D_MODEL)
    shift = [ada[l:l + 1, 0:D_MODEL] for l in range(DEPTH)]
    scale = [ada[l:l + 1, D_MODEL:2 * D_MODEL] for l in range(DEPTH)]
    gate = [ada[l:l + 1, 2 * D_MODEL:] for l in range(DEPTH)]

    big = dict(win=w_in, wpo=w_pool_o, who=w_hgrn_o, wout=w_out)
    units = [["win0"], ["wpo0", "who0", "wout0"], ["win1", "wpo1", "who1", "wout1"]]
    wbuf = {}
    for wname, arr in big.items():
        for l in range(DEPTH):
            wbuf[f"s_{wname}{l}"] = arr[l].astype(WIRE_DTYPE)
            wbuf[f"g_{wname}{l}"] = lax.empty((N_DEV,) + arr.shape[1:], WIRE_DTYPE)
    g_streams = [_gather_streams(keys) for keys in units]
    all_keys = [k for keys in units for k in keys]

    def own_shards(refs, p):
        return [(refs["s_" + k], refs["g_" + k].at[p[3]]) for k in all_keys]

    wbuf, g_sems, token = _comm_call(
        "gather_start", wbuf, start=[s for to_chips, to_sibling, _ in g_streams for s in (to_chips, to_sibling)],
        local=(len(all_keys), own_shards), after=ada_all)

    def gather_finish(u, after):
        to_chips, to_sibling, pass_on = g_streams[u]
        sub = {p + k: wbuf[p + k] for k in units[u] for p in ("s_", "g_")}
        sub, (f_sems,), _ = _comm_call(f"gather_pass_{u}", sub, start=[pass_on], wait=[(to_chips, g_sems[2 * u])],
                                       after=after)
        sub, _, _ = _comm_call(f"gather_done_{u}", sub, wait=[(to_sibling, g_sems[2 * u + 1]), (pass_on, f_sems)])
        return {k: sub["g_" + k] for k in units[u]}

    lb = _lb_fwd(lb_logits)

    gw = {}
    xs, saved = [x[0]], []
    for l in range(DEPTH):
        h = _prenorm_fwd(xs[l], g_pre[l:l + 1], shift[l], scale[l], tm, f"prenorm_fwd_{l}",
                         after=token if l == 0 else None)
        if l == 0:
            gw.update(gather_finish(0, h))
        z = _in_proj(h, gw[f"win{l}"], tm, f"in_proj_{l}")
        a_in = _pool_fwd(z, pool_w[l], pool_scale[l:l + 1], f"pool_fwd_{l}")
        if l == 0:
            gw.update(gather_finish(1, a_in))
        o, b_in, states = _hgrn_fwd(z, lb[l:l + 1], hgrn_norm_g[l:l + 1], f"hgrn_fwd_{l}")
        if l == 0:
            gw.update(gather_finish(2, b_in))
        who_l = gw[f"who{l}"].reshape(D_MODEL, D_MODEL)
        wout_l = gw[f"wout{l}"].reshape(D_MODEL, D_MODEL)
        ba, bb, merged, y, x_next = _merge_fwd(a_in, b_in, z, xs[l], gw[f"wpo{l}"], who_l, wout_l, gate[l],
                                               g_post[l:l + 1], tm_merge, f"merge_fwd_{l}")
        xs.append(x_next)
        saved.append((h, z, a_in, o, b_in, states, ba, bb, merged, y, who_l, wout_l))

    loss_part, dx = _loss_grad(xs[DEPTH], loss_target[0], tm)
    loss = lax.psum(loss_part[0, 0], ("x", "y", "c"))

    chips = _other_chips(pos)
    pair_idx = jnp.stack([_dev_index(cx, cy, pos[2]) for cx, cy in chips] + [me]).astype(jnp.int32)
    pair_rows = dict(win=256, wpo=POOL_WIDTH, who=HEAD_DIM, wout=HEAD_DIM)

    def scatter_pair_start(u, grads):
        keys = list(grads)
        pair, to_chips = _scatter_streams(keys)
        bufs = {}
        for k in keys:
            bufs["g_" + k] = grads[k]
            bufs["st_" + k] = lax.empty((4,) + grads[k].shape[1:], WIRE_DTYPE)
        bufs, (sems,), token = _comm_call(f"scatter_pair_start_{u}", bufs, start=[pair])
        return dict(u=u, keys=keys, pair=pair, to_chips=to_chips, bufs=bufs, sems=sems, token=token)

    def scatter_chips_start(st, after):
        u, keys = st["u"], st["keys"]
        bufs, _, _ = _comm_call(f"scatter_pair_done_{u}", st["bufs"], wait=[(st["pair"], st["sems"])], after=after)
        bufs2 = {}
        for k in keys:
            bufs2["ps_" + k] = _pair_sum(bufs["g_" + k], bufs["st_" + k], pair_idx, bufs["g_" + k].shape[1],
                                         f"pair_sum_{k}")
            bufs2["ld_" + k] = lax.empty((3,) + bufs["g_" + k].shape[1:], WIRE_DTYPE)
        bufs2, (sems,), token = _comm_call(f"scatter_chips_start_{u}", bufs2, start=[st["to_chips"]])
        st.update(bufs=bufs2, sems=sems, token=token)

    def scatter_finish(st, after):
        bufs, _, _ = _comm_call(f"scatter_chips_done_{st['u']}", st["bufs"], wait=[(st["to_chips"], st["sems"])],
                                after=after)
        return {k: [(bufs["ps_" + k], 3), (bufs["ld_" + k], 0), (bufs["ld_" + k], 1), (bufs["ld_" + k], 2)]
                for k in st["keys"]}

    d_ada, small, scat = [None] * DEPTH, [None] * DEPTH, {}
    for l in reversed(range(DEPTH)):
        h, z, a_in, o, b_in, states, ba, bb, merged, y, who_l, wout_l = saved[l]
        dy, dba, dbb, da_in, db_in, dmg, acc_post = _merge_bwd(
            dx, y, ba, bb, z, gw[f"wpo{l}"], who_l, wout_l, gate[l], g_post[l:l + 1], tm_merge, f"merge_bwd_{l}")
        g_small = {
            f"wout{l}": _grad_tn(merged, dy, 512, False, f"grad_w_out_{l}").reshape(N_DEV, HEAD_DIM, D_MODEL),
            f"who{l}": _grad_tn(b_in, dbb, 512, False, f"grad_w_hgrn_o_{l}").reshape(N_DEV, HEAD_DIM, D_MODEL),
            f"wpo{l}": _grad_tn(a_in, dba, GROUP_DIM, True, f"grad_w_pool_o_{l}")}
        st_small = scat[f"small{l}"] = scatter_pair_start(f"small{l}", g_small)
        dzh, dlb, dgn = _hgrn_bwd(db_in, z, o, states, lb[l:l + 1], hgrn_norm_g[l:l + 1], f"hgrn_bwd_{l}",
                                  after=st_small["token"])
        scatter_chips_start(st_small, dzh)
        dpv, dpg, dpw, dps = _pool_bwd(da_in, z, pool_w[l], pool_scale[l:l + 1], f"pool_bwd_{l}",
                                       after=st_small["token"])
        dz = jnp.concatenate([dpv, dpg, dzh, dmg], axis=1)
        st_win = scat[f"win{l}"] = scatter_pair_start(f"win{l}", {f"win{l}": _in_proj_dw(h, dz, f"grad_w_in_{l}")})
        dh = _in_proj_dh(dz, gw[f"win{l}"], tm, f"in_proj_dh_{l}", after=st_win["token"])
        scatter_chips_start(st_win, dh)
        dx, acc_pre = _prenorm_bwd(xs[l], dh, dx, g_pre[l:l + 1], scale[l], tm, f"prenorm_bwd_{l}",
                                   after=st_win["token"])
        d_ada[l] = jnp.concatenate([acc_pre[0], acc_pre[1], acc_post[0]])
        small[l] = dict(g_pre=acc_pre[2], g_post=acc_post[1], pool_w=dpw, pool_scale=dps[0], lb_logits=dlb[0],
                        hgrn_norm_g=dgn[0])
    grad_x = dx[None]

    small_names = [name for name, _ in _SMALL_ROWS]
    g_parts = {name: jnp.stack([small[l][name] for l in range(DEPTH)]) for name in small_names if name != "b_ada"}
    g_parts["b_ada"] = jnp.stack(d_ada)
    g_all = _allgather_small(_pack_small(g_parts), "allgather_small_grads")
    weights = dict(b_ada=b_ada, g_pre=g_pre, g_post=g_post, pool_w=pool_w, pool_scale=pool_scale,
                   lb_logits=lb_logits, hgrn_norm_g=hgrn_norm_g)
    m_small = dict(b_ada=m_b_ada, g_pre=m_g_pre, g_post=m_g_post, pool_w=m_pool_w, pool_scale=m_pool_scale,
                   lb_logits=m_lb_logits, hgrn_norm_g=m_hgrn_norm_g)
    v_small = dict(b_ada=v_b_ada, g_pre=v_g_pre, g_post=v_g_post, pool_w=v_pool_w, pool_scale=v_pool_scale,
                   lb_logits=v_lb_logits, hgrn_norm_g=v_hgrn_norm_g)
    shapes = {name: weights[name].shape for name in small_names}
    small_out = [_unpack_small(p, shapes) for p in
                 _adamw_small(_pack_small(weights), _pack_small(m_small), _pack_small(v_small), g_all)]

    d_ada_all = g_all[:, 0:48, :].reshape(N_DEV, DEPTH, 3 * D_MODEL)
    d_cols = jnp.transpose(lax.dynamic_slice_in_dim(d_ada_all, me * ADA_COLS, ADA_COLS, axis=2), (1, 0, 2))
    g_w_ada = _ada_bwd(c_all, d_cols)
    ada_out = _adamw_sharded(w_ada, m_w_ada, v_w_ada, g_w_ada[:, None], 256, "adamw_w_ada")

    moments = dict(win=(m_w_in, v_w_in), wpo=(m_w_pool_o, v_w_pool_o), who=(m_w_hgrn_o, v_w_hgrn_o),
                   wout=(m_w_out, v_w_out))
    big_out, after = {}, ada_out[0]
    for unit in ("small1", "win1", "small0", "win0"):
        for k, contribs in scatter_finish(scat[unit], after).items():
            wname, l = k[:-1], int(k[-1])
            big_out[wname] = _adamw_layer(big[wname], moments[wname][0], moments[wname][1], contribs, l,
                                          pair_rows[wname], f"adamw_{k}", prev=big_out.get(wname))
            after = big_out[wname][0]

    def leaf(kind):
        s = small_out[kind]
        return (ada_out[kind], s["b_ada"], s["g_pre"], s["g_post"], big_out["win"][kind], s["pool_w"], s["pool_scale"],
                s["lb_logits"], s["hgrn_norm_g"], big_out["wpo"][kind], big_out["who"][kind], big_out["wout"][kind])

    return (loss, grad_x) + leaf(0) + leaf(1) + leaf(2) + leaf(3)
```

```python
import jax
import jax.numpy as jnp
from jax import lax
from jax.experimental import pallas as pl
from jax.experimental.pallas import tpu as pltpu

F32 = jnp.float32
MXU_DTYPE = jnp.bfloat16
WIRE_DTYPE = jnp.bfloat16

N_DEV = 8
DEPTH = 2
D_MODEL = 1024
HEADS = 8
HEAD_DIM = 128
POOL_GROUPS = 4
GROUP_DIM = 128
POOL_WIDTH = POOL_GROUPS * GROUP_DIM
IN_WIDTH = 7168
CHUNK = 64
SUB = 16
N_SUB = CHUNK // SUB
EXP_CLAMP = 80.0
NORM_EPS = 1e-6
LOG_FLOOR = 1e-30
ADA_COLS = 3 * D_MODEL // N_DEV
IN_COLS = IN_WIDTH // N_DEV
COL_HQ, COL_HF, COL_HI, COL_HG, COL_MGP, COL_MGH = 1, 2, 3, 4, 5, 6

ADAM_LR = 0.001
ADAM_B1 = 0.9
ADAM_B2 = 0.999
ADAM_EPS = 1e-08
ADAM_WD = 0.01
ADAM_STEP = 10

VMEM_LIMIT = 48 * 1024 * 1024
MESH_ID = pl.DeviceIdType.MESH
HIGHEST = lax.Precision.HIGHEST

_SMALL_ROWS = (("b_ada", 0, 24), ("g_pre", 0, 8), ("b_ada", 1, 24), ("g_pre", 1, 8), ("g_post", None, 16),
               ("pool_w", None, 1024), ("pool_scale", None, 8), ("lb_logits", None, 16), ("hgrn_norm_g", None, 2))
SMALL_LATE_ROWS = 32
SMALL_ROWS_PAD = 1136
LB_ROW0 = 32 + 32 + 16 + 1024 + 8


def _params(**kw):
    return pltpu.CompilerParams(vmem_limit_bytes=VMEM_LIMIT, **kw)


def _sigmoid(v):
    return 1.0 / (1.0 + jnp.exp(-v))


def _dsilu(v, s):
    return s * (1.0 + v * (1.0 - s))


def _dot(a, b):
    return jnp.dot(a.astype(MXU_DTYPE), b.astype(MXU_DTYPE), preferred_element_type=F32)


def _dot_nt(a, b):
    return lax.dot_general(a.astype(MXU_DTYPE), b.astype(MXU_DTYPE), (((1,), (1,)), ((), ())),
                           preferred_element_type=F32)


def _dot_tn(a, b):
    return lax.dot_general(a.astype(MXU_DTYPE), b.astype(MXU_DTYPE), (((0,), (0,)), ((), ())),
                           preferred_element_type=F32)


def _pallas_after(body, n_in, after, *, in_specs, **kw):
    if after is None:
        return pl.pallas_call(body, in_specs=in_specs, **kw)

    def tied(*refs):
        body(*refs[:n_in], *refs[n_in + 1:])

    call = pl.pallas_call(tied, in_specs=list(in_specs) + [pl.BlockSpec(memory_space=pl.ANY)], **kw)
    return lambda *operands: call(*operands, after)


def _my_position():
    mx, my, mc = lax.axis_index("x"), lax.axis_index("y"), lax.axis_index("c")
    return mx, my, mc, 4 * mx + 2 * my + mc


def _peer(mx, my, mc, k):
    px = 1 - mx if (k >> 2) & 1 else mx
    py = 1 - my if (k >> 1) & 1 else my
    pc = 1 - mc if k & 1 else mc
    return (px, py, pc), 4 * px + 2 * py + pc


def _allgather_small(v, name, after=None):
    rows, cols = v.shape

    def body(v_ref, out_ref, send_sems, recv_sems):
        mx, my, mc, me = _my_position()
        out_ref[me] = v_ref[...]
        copies = []
        for k in range(1, N_DEV):
            peer, _ = _peer(mx, my, mc, k)
            cp = pltpu.make_async_remote_copy(
                src_ref=v_ref, dst_ref=out_ref.at[me],
                send_sem=send_sems.at[k - 1], recv_sem=recv_sems.at[k - 1],
                device_id=peer, device_id_type=MESH_ID)
            cp.start()
            copies.append(cp)
        for cp in copies:
            cp.wait()

    return _pallas_after(
        body, 1, after, name=name,
        out_shape=jax.ShapeDtypeStruct((N_DEV, rows, cols), v.dtype),
        in_specs=[pl.BlockSpec(memory_space=pltpu.VMEM)],
        out_specs=pl.BlockSpec(memory_space=pltpu.VMEM),
        scratch_shapes=[pltpu.SemaphoreType.DMA((N_DEV - 1,)), pltpu.SemaphoreType.DMA((N_DEV - 1,))],
        compiler_params=_params(),
    )(v)


class _Stream:
    def __init__(self, n, plan):
        self.n, self.plan = n, plan


def _comm_call(name, bufs, start=(), wait=(), after=None, local=None):
    names = list(bufs)

    def body(*refs):
        it = iter(refs)
        buf_refs = {n: next(it) for n in names}
        wait_sems = [(next(it), next(it)) for _ in wait]
        if after is not None:
            next(it)
        start_sems = [(next(it), next(it)) for _ in start]
        for _ in names:
            next(it)
        token = next(it)
        pos = _my_position()

        def descriptors(stream, sems):
            return [pltpu.make_async_remote_copy(src_ref=src, dst_ref=dst, send_sem=sems[0].at[k], recv_sem=sems[1].at[k],
                                                 device_id=dev, device_id_type=MESH_ID)
                    for k, (src, dst, dev) in enumerate(stream.plan(buf_refs, pos))]

        if local is not None:
            local_sems = next(it)
            copies = [pltpu.make_async_copy(src, dst, local_sems.at[k])
                      for k, (src, dst) in enumerate(local[1](buf_refs, pos))]
            for cp in copies:
                cp.start()
            for cp in copies:
                cp.wait()
        for (stream, _), sems in zip(wait, wait_sems):
            for cp in descriptors(stream, sems):
                cp.wait_send()
                cp.wait_recv()
        for stream, sems in zip(start, start_sems):
            for cp in descriptors(stream, sems):
                cp.start()
        token[...] = jnp.zeros_like(token)

    hbm = pl.BlockSpec(memory_space=pltpu.HBM)
    sem = pl.BlockSpec(memory_space=pltpu.SEMAPHORE)
    operands = [pltpu.with_memory_space_constraint(bufs[n], pltpu.HBM) for n in names]
    in_specs = [hbm] * len(names)
    for _, (send_sems, recv_sems) in wait:
        operands += [send_sems, recv_sems]
        in_specs += [sem, sem]
    if after is not None:
        operands.append(after)
        in_specs.append(pl.BlockSpec(memory_space=pl.ANY))
    out_shape, out_specs = [], []
    for stream in start:
        out_shape += [pltpu.SemaphoreType.DMA((stream.n,)), pltpu.SemaphoreType.DMA((stream.n,))]
        out_specs += [sem, sem]
    n_sem_out = len(out_shape)
    out_shape += [pltpu.HBM(bufs[n].shape, bufs[n].dtype) for n in names]
    out_specs += [hbm] * len(names)
    out_shape.append(jax.ShapeDtypeStruct((8, 128), F32))
    out_specs.append(pl.BlockSpec(memory_space=pltpu.VMEM))
    outs = pl.pallas_call(
        body, name=name, out_shape=out_shape, in_specs=in_specs, out_specs=out_specs,
        input_output_aliases={i: n_sem_out + i for i in range(len(names))},
        scratch_shapes=[pltpu.SemaphoreType.DMA((local[0],))] if local is not None else [],
        compiler_params=pltpu.CompilerParams(has_side_effects=pltpu.SideEffectType.DATAFLOW_SIDE_EFFECTING),
    )(*operands)
    sems = [(outs[2 * i], outs[2 * i + 1]) for i in range(len(start))]
    return dict(zip(names, outs[n_sem_out:n_sem_out + len(names)])), sems, outs[-1]


def _other_chips(pos):
    mx, my, _, _ = pos
    return [(1 - mx if i & 2 else mx, 1 - my if i & 1 else my) for i in (1, 2, 3)]


def _dev_index(px, py, pc):
    return 4 * px + 2 * py + pc


def _gather_streams(keys):
    def to_chips(refs, pos):
        _, _, mc, me = pos
        return [(refs["s_" + k], refs["g_" + k].at[me], (cx, cy, mc)) for k in keys for cx, cy in _other_chips(pos)]

    def to_sibling(refs, pos):
        mx, my, mc, me = pos
        return [(refs["s_" + k], refs["g_" + k].at[me], (mx, my, 1 - mc)) for k in keys]

    def pass_on(refs, pos):
        mx, my, mc, _ = pos
        out = []
        for k in keys:
            for cx, cy in _other_chips(pos):
                slot = refs["g_" + k].at[_dev_index(cx, cy, mc)]
                out.append((slot, slot, (mx, my, 1 - mc)))
        return out

    return _Stream(3 * len(keys), to_chips), _Stream(len(keys), to_sibling), _Stream(3 * len(keys), pass_on)


def _direct_gather_stream(key):
    def plan(refs, pos):
        mx, my, mc, me = pos
        return [(refs["s_" + key], refs["g_" + key].at[me], _peer(mx, my, mc, k)[0]) for k in range(1, N_DEV)]

    return _Stream(N_DEV - 1, plan)


def _scatter_streams(keys):
    def pair(refs, pos):
        mx, my, mc, _ = pos
        sib = (mx, my, 1 - mc)
        out = []
        for k in keys:
            for i, (cx, cy) in enumerate(_other_chips(pos)):
                out.append((refs["g_" + k].at[_dev_index(cx, cy, 1 - mc)], refs["st_" + k].at[i], sib))
            out.append((refs["g_" + k].at[_dev_index(mx, my, 1 - mc)], refs["st_" + k].at[3], sib))
        return out

    def chips(refs, pos):
        mc = pos[2]
        return [(refs["ps_" + k].at[i], refs["ld_" + k].at[i], (cx, cy, mc))
                for k in keys for i, (cx, cy) in enumerate(_other_chips(pos))]

    return _Stream(4 * len(keys), pair), _Stream(3 * len(keys), chips)


def _pair_sum(g, st, idx, tr, name):
    _, rows, cols = g.shape

    def body(idx_ref, g_ref, st_ref, out_ref):
        out_ref[...] = (g_ref[...].astype(F32) + st_ref[...].astype(F32)).astype(out_ref.dtype)

    return pl.pallas_call(
        body, name=name,
        grid_spec=pltpu.PrefetchScalarGridSpec(
            num_scalar_prefetch=1, grid=(4, rows // tr),
            in_specs=[pl.BlockSpec((None, tr, cols), lambda j, i, idx_ref: (idx_ref[j], i, 0)),
                      pl.BlockSpec((None, tr, cols), lambda j, i, idx_ref: (j, i, 0))],
            out_specs=pl.BlockSpec((None, tr, cols), lambda j, i, idx_ref: (j, i, 0))),
        out_shape=jax.ShapeDtypeStruct((4, rows, cols), WIRE_DTYPE),
        compiler_params=_params(dimension_semantics=("parallel", "parallel")),
    )(idx, g, st)


def _ada_fwd(c_all, w_ada, b_cols):
    def body(c_ref, w_ref, b_ref, out_ref):
        cv = c_ref[...]
        ca = cv * _sigmoid(cv)
        for l in range(DEPTH):
            out_ref[l] = jnp.dot(ca, w_ref[l], precision=HIGHEST, preferred_element_type=F32) + b_ref[l:l + 1, :]

    return pl.pallas_call(
        body, name="ada_fwd",
        out_shape=jax.ShapeDtypeStruct((DEPTH, N_DEV, ADA_COLS), F32),
        compiler_params=_params(),
    )(c_all, w_ada, b_cols)


def _ada_bwd(c_all, d_cols):
    def body(c_ref, d_ref, out_ref):
        cv = c_ref[...]
        ca = cv * _sigmoid(cv)
        for l in range(DEPTH):
            out_ref[l] = lax.dot_general(ca, d_ref[l], (((0,), (0,)), ((), ())), precision=HIGHEST,
                                         preferred_element_type=F32)

    return pl.pallas_call(
        body, name="ada_bwd",
        out_shape=jax.ShapeDtypeStruct((DEPTH, D_MODEL, ADA_COLS), F32),
        compiler_params=_params(),
    )(c_all, d_cols)


def _lower_bounds(logits):
    m = jnp.maximum(logits[0:1], logits[1:2])
    e0, e1 = jnp.exp(logits[0:1] - m), jnp.exp(logits[1:2] - m)
    den = e0 + e1
    p0, p1 = e0 / den, e1 / den
    low0 = p0 - p0
    low1 = (p0 + p1) - p0
    return (p0, p1), (low0, low1)


def _lb_fwd(lb_logits):
    def body(lg_ref, out_ref):
        _, (low0, low1) = _lower_bounds(lg_ref[...])
        out_ref[0:1, :] = jnp.clip(low0, 0.0, 1.0)
        out_ref[1:2, :] = jnp.clip(low1, 0.0, 1.0)

    return pl.pallas_call(body, name="lb_fwd", out_shape=jax.ShapeDtypeStruct(lb_logits.shape, F32),
                          compiler_params=_params())(lb_logits)


def _row_spec(cols=D_MODEL):
    return pl.BlockSpec((1, cols), lambda *_: (0, 0))


def _prenorm_fwd(x, g, shift, scale, tm, name, after=None):
    seq = x.shape[0]

    def body(x_ref, g_ref, sh_ref, sc_ref, h_ref):
        xv = x_ref[...]
        rs = lax.rsqrt(jnp.mean(xv * xv, axis=-1, keepdims=True) + NORM_EPS)
        h = (xv * rs * g_ref[...]) * (1.0 + sc_ref[...]) + sh_ref[...]
        h_ref[...] = h.astype(h_ref.dtype)

    tile = pl.BlockSpec((tm, D_MODEL), lambda i: (i, 0))
    return _pallas_after(
        body, 4, after, name=name, grid=(seq // tm,),
        in_specs=[tile, _row_spec(), _row_spec(), _row_spec()], out_specs=tile,
        out_shape=jax.ShapeDtypeStruct((seq, D_MODEL), MXU_DTYPE),
        compiler_params=_params(dimension_semantics=("parallel",)),
    )(x, g, shift, scale)


def _in_proj(h, win_g, tm, name, after=None):
    seq = h.shape[0]

    def body(h_ref, w_ref, z_ref):
        z_ref[...] = jnp.dot(h_ref[...], w_ref[...], preferred_element_type=F32)

    return _pallas_after(
        body, 2, after, name=name, grid=(N_DEV, seq // tm),
        in_specs=[pl.BlockSpec((tm, D_MODEL), lambda j, i: (i, 0)),
                  pl.BlockSpec((None, D_MODEL, IN_COLS), lambda j, i: (j, 0, 0))],
        out_specs=pl.BlockSpec((tm, IN_COLS), lambda j, i: (i, j)),
        out_shape=jax.ShapeDtypeStruct((seq, IN_WIDTH), F32),
        compiler_params=_params(dimension_semantics=("parallel", "parallel")),
    )(h, win_g)


def _shift_down(v, j, pos):
    return jnp.where(pos >= j, pltpu.roll(v, j, 0), 0.0)


def _shift_up(v, j, pos, seq):
    return jnp.where(pos < seq - j, pltpu.roll(v, seq - j, 0), 0.0)


def _select_window(g, candidates):
    out = candidates[-1]
    for i in range(len(candidates) - 2, -1, -1):
        out = jnp.where(g == i, candidates[i], out)
    return out


def _pool_mean_minus_token(u, g, pos):
    sums, acc = [], u
    for j in (1, 2, 4, 8):
        acc = acc + _shift_down(acc, j, pos)
        sums.append(acc)
    wsum = _select_window(g, sums)
    width = jnp.left_shift(2, g).astype(F32)
    count = jnp.minimum(pos.astype(F32) + 1.0, width)
    return wsum / count - u, count


def _pool_fwd(z, pool_w_l, pool_scale_l, name, after=None):
    seq = z.shape[0]

    def body(pv_ref, pg_ref, w_ref, sc_ref, out_ref):
        g = pl.program_id(0)
        pos = lax.broadcasted_iota(jnp.int32, (seq, GROUP_DIM), 0)
        pm, _ = _pool_mean_minus_token(pv_ref[...], g, pos)
        lin = _dot(pm, w_ref[...]) * sc_ref[...]
        pg = pg_ref[...]
        out_ref[...] = (lin * (pg * _sigmoid(pg))).astype(out_ref.dtype)

    return _pallas_after(
        body, 4, after, name=name, grid=(POOL_GROUPS,),
        in_specs=[pl.BlockSpec((seq, GROUP_DIM), lambda g: (0, g)),
                  pl.BlockSpec((seq, GROUP_DIM), lambda g: (0, POOL_GROUPS + g)),
                  pl.BlockSpec((None, GROUP_DIM, GROUP_DIM), lambda g: (g, 0, 0)),
                  pl.BlockSpec((1, GROUP_DIM), lambda g: (0, g))],
        out_specs=pl.BlockSpec((seq, GROUP_DIM), lambda g: (0, g)),
        out_shape=jax.ShapeDtypeStruct((seq, POOL_WIDTH), MXU_DTYPE),
        compiler_params=_params(dimension_semantics=("parallel",)),
    )(z, z, pool_w_l, pool_scale_l)


def _chunk_masks():
    row = lax.broadcasted_iota(jnp.int32, (CHUNK, CHUNK), 0)
    col = lax.broadcasted_iota(jnp.int32, (CHUNK, CHUNK), 1)
    causal = row >= col
    tri = causal.astype(F32)
    before_sub = (col < (row // SUB) * SUB).astype(F32)
    return causal, tri, before_sub


def _gates(zf, lb):
    sg = _sigmoid(zf)
    f = lb + (1.0 - lb) * sg
    logf = jnp.log(jnp.maximum(f, LOG_FLOOR))
    return sg, f, logf


def _intra_blocks(q_h, k_h, cum_h, base_h, causal):
    rel = cum_h - base_h
    out = []
    for i in range(N_SUB):
        rows = slice(i * SUB, (i + 1) * SUB)
        e_q = jnp.exp(rel[rows])
        base_i = jnp.concatenate([base_h[rows]] * N_SUB, axis=0)
        e_k = jnp.exp(jnp.minimum(base_i - cum_h, EXP_CLAMP))
        q_t = q_h[rows] * e_q
        k_t = k_h * e_k
        a_i = jnp.where(causal[rows], _dot_nt(q_t, k_t), 0.0)
        out.append((q_t, k_t, e_q, e_k, a_i))
    return out


def _hgrn_fwd(z, lb_l, gn_l, name, after=None):
    seq = z.shape[0]
    n_chunks = seq // CHUNK

    def body(hq_ref, hf_ref, hi_ref, hg_ref, lb_ref, gn_ref, o_ref, bin_ref, st_ref, state):
        @pl.when(pl.program_id(0) == 0)
        def _():
            state[...] = jnp.zeros_like(state)

        causal, tri, before_sub = _chunk_masks()
        _, f, logf = _gates(hf_ref[...], lb_ref[...])
        kk = 1.0 - f
        hq = hq_ref[...]
        q = hq * _sigmoid(hq)
        cum = jnp.dot(tri, logf, precision=HIGHEST, preferred_element_type=F32)
        base = jnp.dot(before_sub, logf, precision=HIGHEST, preferred_element_type=F32)
        st_ref[0] = state[...]
        for h in range(HEADS):
            sl = slice(h * HEAD_DIM, (h + 1) * HEAD_DIM)
            q_h, k_h, cum_h = q[:, sl], kk[:, sl], cum[:, sl]
            v_h = hi_ref[:, sl]
            st_h = state[h]
            blocks = _intra_blocks(q_h, k_h, cum_h, base[:, sl], causal)
            a = jnp.concatenate([b[4] for b in blocks], axis=0)
            o_h = _dot_nt(q_h * jnp.exp(cum_h), st_h) + _dot(a, v_h)
            last = jnp.sum(logf[:, sl], axis=0, keepdims=True)
            state[h] = st_h * jnp.exp(last) + _dot_tn(v_h, k_h * jnp.exp(last - cum_h))
            rs = lax.rsqrt(jnp.mean(o_h * o_h, axis=-1, keepdims=True) + NORM_EPS)
            hg = hg_ref[:, sl]
            o_ref[:, sl] = o_h
            bin_ref[:, sl] = ((o_h * rs * gn_ref[...]) * (hg * _sigmoid(hg))).astype(bin_ref.dtype)

    def col(block):
        return pl.BlockSpec((CHUNK, D_MODEL), lambda c: (c, block))

    tile = pl.BlockSpec((CHUNK, D_MODEL), lambda c: (c, 0))
    return _pallas_after(
        body, 6, after, name=name, grid=(n_chunks,),
        in_specs=[col(COL_HQ), col(COL_HF), col(COL_HI), col(COL_HG), _row_spec(), _row_spec(HEAD_DIM)],
        out_specs=[tile, tile, pl.BlockSpec((1, HEADS, HEAD_DIM, HEAD_DIM), lambda c: (c, 0, 0, 0))],
        out_shape=[jax.ShapeDtypeStruct((seq, D_MODEL), F32),
                   jax.ShapeDtypeStruct((seq, D_MODEL), MXU_DTYPE),
                   jax.ShapeDtypeStruct((n_chunks, HEADS, HEAD_DIM, HEAD_DIM), F32)],
        scratch_shapes=[pltpu.VMEM((HEADS, HEAD_DIM, HEAD_DIM), F32)],
        compiler_params=_params(dimension_semantics=("arbitrary",)),
    )(z, z, z, z, lb_l, gn_l)


def _rms_parts(y):
    rs = lax.rsqrt(jnp.mean(y * y, axis=-1, keepdims=True) + NORM_EPS)
    return rs, y * rs


def _merge_fwd(a_in, b_in, z, x, wpo_g, who_g, wout_g, gate, g_post, tm, name):
    seq = x.shape[0]

    def body(a_ref, b_ref, mgp_ref, mgh_ref, x_ref, wpo_ref, who_ref, wout_ref, gate_ref, gp_ref,
             ba_ref, bb_ref, mer_ref, y_ref, xn_ref):
        a = a_ref[...]
        ba = jnp.concatenate([_dot(a, wpo_ref[j]) for j in range(N_DEV)], axis=1)
        bb = _dot(b_ref[...], who_ref[...])
        merged = _sigmoid(mgp_ref[...]) * ba + _sigmoid(mgh_ref[...]) * bb
        y = _dot(merged, wout_ref[...])
        _, yn = _rms_parts(y)
        ba_ref[...] = ba.astype(ba_ref.dtype)
        bb_ref[...] = bb.astype(bb_ref.dtype)
        mer_ref[...] = merged.astype(mer_ref.dtype)
        y_ref[...] = y
        xn_ref[...] = x_ref[...] + gate_ref[...] * (yn * gp_ref[...])

    def tile(cols=D_MODEL, block=0):
        return pl.BlockSpec((tm, cols), lambda i: (i, block))

    full = pl.BlockSpec((D_MODEL, D_MODEL), lambda i: (0, 0))
    act = jax.ShapeDtypeStruct((seq, D_MODEL), MXU_DTYPE)
    f32 = jax.ShapeDtypeStruct((seq, D_MODEL), F32)
    return pl.pallas_call(
        body, name=name, grid=(seq // tm,),
        in_specs=[tile(POOL_WIDTH), tile(), tile(block=COL_MGP), tile(block=COL_MGH), tile(),
                  pl.BlockSpec((N_DEV, POOL_WIDTH, GROUP_DIM), lambda i: (0, 0, 0)),
                  full, full, _row_spec(), _row_spec()],
        out_specs=[tile(), tile(), tile(), tile(), tile()],
        out_shape=[act, act, act, f32, f32],
        compiler_params=_params(dimension_semantics=("parallel",)),
    )(a_in, b_in, z, z, x, wpo_g, who_g, wout_g, gate, g_post)


def _loss_grad(x_out, target, tm):
    seq = x_out.shape[0]

    def body(x_ref, t_ref, loss_ref, dx_ref):
        @pl.when(pl.program_id(0) == 0)
        def _():
            loss_ref[...] = jnp.zeros_like(loss_ref)

        err = x_ref[...] - t_ref[...]
        per_token = jnp.mean(err * err, axis=-1, keepdims=True)
        loss_ref[...] += 0.5 * jnp.sum(per_token, axis=0, keepdims=True)
        dx_ref[...] = err * (1.0 / D_MODEL)

    tile = pl.BlockSpec((tm, D_MODEL), lambda i: (i, 0))
    return pl.pallas_call(
        body, name="loss_grad", grid=(seq // tm,),
        in_specs=[tile, tile],
        out_specs=[pl.BlockSpec((1, 1), lambda i: (0, 0)), tile],
        out_shape=[jax.ShapeDtypeStruct((1, 1), F32), jax.ShapeDtypeStruct((seq, D_MODEL), F32)],
        compiler_params=_params(dimension_semantics=("arbitrary",)),
    )(x_out, target)


def _merge_bwd(dx, y, ba, bb, z, wpo_g, who_g, wout_g, gate, g_post, tm, name):
    seq = dx.shape[0]

    def body(dx_ref, y_ref, ba_ref, bb_ref, mgp_ref, mgh_ref, wpo_ref, who_ref, wout_ref, gate_ref, gp_ref,
             dy_ref, dba_ref, dbb_ref, da_ref, db_ref, dmg_ref, acc_ref):
        @pl.when(pl.program_id(0) == 0)
        def _():
            acc_ref[...] = jnp.zeros_like(acc_ref)

        dxv = dx_ref[...]
        rs, yn = _rms_parts(y_ref[...])
        acc_ref[0:1, :] += jnp.sum(dxv * yn * gp_ref[...], axis=0, keepdims=True)
        acc_ref[1:2, :] += jnp.sum(dxv * gate_ref[...] * yn, axis=0, keepdims=True)
        dyn = dxv * (gate_ref[...] * gp_ref[...])
        dy = rs * (dyn - yn * jnp.mean(dyn * yn, axis=-1, keepdims=True))
        dmerged = _dot_nt(dy, wout_ref[...])
        sp, sh = _sigmoid(mgp_ref[...]), _sigmoid(mgh_ref[...])
        dba, dbb = sp * dmerged, sh * dmerged
        dmg_ref[:, 0:D_MODEL] = (dmerged * ba_ref[...].astype(F32) * sp * (1.0 - sp)).astype(dmg_ref.dtype)
        dmg_ref[:, D_MODEL:2 * D_MODEL] = (dmerged * bb_ref[...].astype(F32) * sh * (1.0 - sh)).astype(dmg_ref.dtype)
        da = _dot_nt(dba[:, 0:GROUP_DIM], wpo_ref[0])
        for j in range(1, N_DEV):
            da += _dot_nt(dba[:, j * GROUP_DIM:(j + 1) * GROUP_DIM], wpo_ref[j])
        dy_ref[...] = dy.astype(dy_ref.dtype)
        dba_ref[...] = dba.astype(dba_ref.dtype)
        dbb_ref[...] = dbb.astype(dbb_ref.dtype)
        da_ref[...] = da
        db_ref[...] = _dot_nt(dbb, who_ref[...])

    def tile(cols=D_MODEL, block=0):
        return pl.BlockSpec((tm, cols), lambda i: (i, block))

    full = pl.BlockSpec((D_MODEL, D_MODEL), lambda i: (0, 0))
    act = jax.ShapeDtypeStruct((seq, D_MODEL), MXU_DTYPE)
    return pl.pallas_call(
        body, name=name, grid=(seq // tm,),
        in_specs=[tile(), tile(), tile(), tile(), tile(block=COL_MGP), tile(block=COL_MGH),
                  pl.BlockSpec((N_DEV, POOL_WIDTH, GROUP_DIM), lambda i: (0, 0, 0)),
                  full, full, _row_spec(), _row_spec()],
        out_specs=[tile(), tile(), tile(), tile(POOL_WIDTH), tile(), tile(2 * D_MODEL),
                   pl.BlockSpec((8, D_MODEL), lambda i: (0, 0))],
        out_shape=[act, act, act, jax.ShapeDtypeStruct((seq, POOL_WIDTH), F32),
                   jax.ShapeDtypeStruct((seq, D_MODEL), F32),
                   jax.ShapeDtypeStruct((seq, 2 * D_MODEL), MXU_DTYPE),
                   jax.ShapeDtypeStruct((8, D_MODEL), F32)],
        compiler_params=_params(dimension_semantics=("arbitrary",)),
    )(dx, y, ba, bb, z, z, wpo_g, who_g, wout_g, gate, g_post)


def _grad_tn(a, b, tn, dev_major, name):
    seq, ka = a.shape
    n = b.shape[1]

    def body(a_ref, b_ref, out_ref):
        out_ref[...] = _dot_tn(a_ref[...], b_ref[...]).astype(out_ref.dtype)

    if dev_major:
        out_spec = pl.BlockSpec((None, ka, tn), lambda j: (j, 0, 0))
        out_shape = jax.ShapeDtypeStruct((n // tn, ka, tn), WIRE_DTYPE)
    else:
        out_spec = pl.BlockSpec((ka, tn), lambda j: (0, j))
        out_shape = jax.ShapeDtypeStruct((ka, n), WIRE_DTYPE)
    return pl.pallas_call(
        body, name=name, grid=(n // tn,),
        in_specs=[pl.BlockSpec((seq, ka), lambda j: (0, 0)), pl.BlockSpec((seq, tn), lambda j: (0, j))],
        out_specs=out_spec, out_shape=out_shape,
        compiler_params=_params(dimension_semantics=("parallel",)),
    )(a, b)


def _hgrn_bwd(db_in, z, o, states, lb_l, gn_l, name, after=None):
    seq = z.shape[0]
    n_chunks = seq // CHUNK

    def body(db_ref, hq_ref, hf_ref, hi_ref, hg_ref, o_ref, st_ref, lb_ref, gn_ref,
             dz_ref, dlb_ref, dgn_ref, dstate, dq_buf, dk_buf):
        @pl.when(pl.program_id(0) == 0)
        def _():
            dstate[...] = jnp.zeros_like(dstate)
            dlb_ref[...] = jnp.zeros_like(dlb_ref)
            dgn_ref[...] = jnp.zeros_like(dgn_ref)

        causal, tri, before_sub = _chunk_masks()
        lb = lb_ref[...]
        sg, f, logf = _gates(hf_ref[...], lb)
        kk = 1.0 - f
        hq = hq_ref[...]
        sq = _sigmoid(hq)
        q = hq * sq
        cum = jnp.dot(tri, logf, precision=HIGHEST, preferred_element_type=F32)
        base = jnp.dot(before_sub, logf, precision=HIGHEST, preferred_element_type=F32)
        gn = gn_ref[...]
        dgn = jnp.zeros((1, HEAD_DIM), F32)
        dlast = []
        for h in range(HEADS):
            sl = slice(h * HEAD_DIM, (h + 1) * HEAD_DIM)
            q_h, k_h, cum_h = q[:, sl], kk[:, sl], cum[:, sl]
            v_h = hi_ref[:, sl]
            st_h = st_ref[0, h]
            dst_h = dstate[h]
            rs, ohat = _rms_parts(o_ref[:, sl])
            hg = hg_ref[:, sl]
            shg = _sigmoid(hg)
            d_bin = db_ref[:, sl]
            don = d_bin * (hg * shg)
            dgn += jnp.sum(don * ohat, axis=0, keepdims=True)
            dohat = don * gn
            do = rs * (dohat - ohat * jnp.mean(dohat * ohat, axis=-1, keepdims=True))
            dz_ref[:, 3 * D_MODEL + h * HEAD_DIM:3 * D_MODEL + (h + 1) * HEAD_DIM] = (
                d_bin * (ohat * gn) * _dsilu(hg, shg)).astype(dz_ref.dtype)
            last = jnp.sum(logf[:, sl], axis=0, keepdims=True)
            g_in = jnp.exp(cum_h)
            d_out = jnp.exp(last - cum_h)
            q_bar, k_bar = q_h * g_in, k_h * d_out
            blocks = _intra_blocks(q_h, k_h, cum_h, base[:, sl], causal)
            a = jnp.concatenate([b[4] for b in blocks], axis=0)
            da = jnp.where(causal, _dot_nt(do, v_h), 0.0)
            dv = _dot_tn(a, do) + _dot_nt(k_bar, dst_h)
            dq_parts = []
            dk_bar = _dot(v_h, dst_h)
            dk = dk_bar * d_out
            dlast.append(jnp.sum(k_bar * dk_bar, axis=0, keepdims=True)
                         + jnp.exp(last) * jnp.sum(st_h * dst_h, axis=0, keepdims=True))
            for i, (q_t, k_t, e_q, e_k, _) in enumerate(blocks):
                da_i = da[i * SUB:(i + 1) * SUB]
                dq_parts.append(jnp.dot(da_i, k_t, precision=HIGHEST, preferred_element_type=F32) * e_q)
                dk += lax.dot_general(da_i, q_t, (((0,), (0,)), ((), ())), precision=HIGHEST,
                                      preferred_element_type=F32) * e_k
            dq = _dot(do, st_h) * g_in + jnp.concatenate(dq_parts, axis=0)
            dstate[h] = dst_h * jnp.exp(last) + _dot_tn(do, q_bar)
            dq_buf[:, sl] = dq
            dk_buf[:, sl] = dk
            dz_ref[:, 2 * D_MODEL + h * HEAD_DIM:2 * D_MODEL + (h + 1) * HEAD_DIM] = dv.astype(dz_ref.dtype)
        dgn_ref[...] += dgn
        dq_all, dk_all = dq_buf[...], dk_buf[...]
        dg = q * dq_all - kk * dk_all
        dlogf = lax.dot_general(tri, dg, (((0,), (0,)), ((), ())), precision=HIGHEST,
                                preferred_element_type=F32) + jnp.concatenate(dlast, axis=1)
        df = jnp.where(f > LOG_FLOOR, dlogf / f, 0.0) - dk_all
        dlb_ref[...] += jnp.sum(df * (1.0 - sg), axis=0, keepdims=True)
        dz_ref[:, 0:D_MODEL] = (dq_all * _dsilu(hq, sq)).astype(dz_ref.dtype)
        dz_ref[:, D_MODEL:2 * D_MODEL] = (df * (1.0 - lb) * sg * (1.0 - sg)).astype(dz_ref.dtype)

    last_chunk = n_chunks - 1

    def col(block):
        return pl.BlockSpec((CHUNK, D_MODEL), lambda c: (last_chunk - c, block))

    return _pallas_after(
        body, 9, after, name=name, grid=(n_chunks,),
        in_specs=[col(0), col(COL_HQ), col(COL_HF), col(COL_HI), col(COL_HG), col(0),
                  pl.BlockSpec((1, HEADS, HEAD_DIM, HEAD_DIM), lambda c: (last_chunk - c, 0, 0, 0)),
                  _row_spec(), _row_spec(HEAD_DIM)],
        out_specs=[pl.BlockSpec((CHUNK, 4 * D_MODEL), lambda c: (last_chunk - c, 0)),
                   _row_spec(), _row_spec(HEAD_DIM)],
        out_shape=[jax.ShapeDtypeStruct((seq, 4 * D_MODEL), MXU_DTYPE),
                   jax.ShapeDtypeStruct((1, D_MODEL), F32), jax.ShapeDtypeStruct((1, HEAD_DIM), F32)],
        scratch_shapes=[pltpu.VMEM((HEADS, HEAD_DIM, HEAD_DIM), F32),
                        pltpu.VMEM((CHUNK, D_MODEL), F32), pltpu.VMEM((CHUNK, D_MODEL), F32)],
        compiler_params=_params(dimension_semantics=("arbitrary",)),
    )(db_in, z, z, z, z, o, states, lb_l, gn_l)


def _pool_bwd(da_in, z, pool_w_l, pool_scale_l, name, after=None):
    seq = z.shape[0]

    def body(da_ref, pv_ref, pg_ref, w_ref, sc_ref, dpv_ref, dpg_ref, dw_ref, dsc_ref):
        g = pl.program_id(0)
        pos = lax.broadcasted_iota(jnp.int32, (seq, GROUP_DIM), 0)
        pm, count = _pool_mean_minus_token(pv_ref[...], g, pos)
        lin0 = _dot(pm, w_ref[...])
        pg = pg_ref[...]
        spg = _sigmoid(pg)
        da = da_ref[...]
        dlin = da * (pg * spg)
        dpg_ref[...] = (da * (lin0 * sc_ref[...]) * _dsilu(pg, spg)).astype(dpg_ref.dtype)
        dsc_ref[...] = jnp.sum(dlin * lin0, axis=0, keepdims=True)
        dl0 = dlin * sc_ref[...]
        dw_ref[...] = _dot_tn(pm, dl0)
        dpm = _dot_nt(dl0, w_ref[...])
        sums, acc = [], dpm / count
        for j in (1, 2, 4, 8):
            acc = acc + _shift_up(acc, j, pos, seq)
            sums.append(acc)
        dpv_ref[...] = (_select_window(g, sums) - dpm).astype(dpv_ref.dtype)

    grp = pl.BlockSpec((seq, GROUP_DIM), lambda g: (0, g))
    return _pallas_after(
        body, 5, after, name=name, grid=(POOL_GROUPS,),
        in_specs=[grp, grp, pl.BlockSpec((seq, GROUP_DIM), lambda g: (0, POOL_GROUPS + g)),
                  pl.BlockSpec((None, GROUP_DIM, GROUP_DIM), lambda g: (g, 0, 0)),
                  pl.BlockSpec((1, GROUP_DIM), lambda g: (0, g))],
        out_specs=[grp, grp, pl.BlockSpec((None, GROUP_DIM, GROUP_DIM), lambda g: (g, 0, 0)),
                   pl.BlockSpec((1, GROUP_DIM), lambda g: (0, g))],
        out_shape=[jax.ShapeDtypeStruct((seq, POOL_WIDTH), MXU_DTYPE),
                   jax.ShapeDtypeStruct((seq, POOL_WIDTH), MXU_DTYPE),
                   jax.ShapeDtypeStruct((POOL_GROUPS, GROUP_DIM, GROUP_DIM), F32),
                   jax.ShapeDtypeStruct((1, POOL_WIDTH), F32)],
        compiler_params=_params(dimension_semantics=("parallel",)),
    )(da_in, z, z, pool_w_l, pool_scale_l)


def _in_proj_dw(h, dz, name, after=None):
    seq = h.shape[0]

    def body(h_ref, dz_ref, out_ref):
        out_ref[...] = lax.dot_general(h_ref[...], dz_ref[...], (((0,), (0,)), ((), ())),
                                       preferred_element_type=F32).astype(out_ref.dtype)

    return _pallas_after(
        body, 2, after, name=name, grid=(N_DEV,),
        in_specs=[pl.BlockSpec((seq, D_MODEL), lambda j: (0, 0)), pl.BlockSpec((seq, IN_COLS), lambda j: (0, j))],
        out_specs=pl.BlockSpec((None, D_MODEL, IN_COLS), lambda j: (j, 0, 0)),
        out_shape=jax.ShapeDtypeStruct((N_DEV, D_MODEL, IN_COLS), WIRE_DTYPE),
        compiler_params=_params(dimension_semantics=("parallel",)),
    )(h, dz)


def _in_proj_dh(dz, win_g, tm, name, after=None):
    seq = dz.shape[0]

    def body(dz_ref, w_ref, dh_ref):
        @pl.when(pl.program_id(1) == 0)
        def _():
            dh_ref[...] = jnp.zeros_like(dh_ref)

        dh_ref[...] += lax.dot_general(dz_ref[...], w_ref[...], (((1,), (1,)), ((), ())),
                                       preferred_element_type=F32)

    return _pallas_after(
        body, 2, after, name=name, grid=(seq // tm, N_DEV),
        in_specs=[pl.BlockSpec((tm, IN_COLS), lambda i, j: (i, j)),
                  pl.BlockSpec((None, D_MODEL, IN_COLS), lambda i, j: (j, 0, 0))],
        out_specs=pl.BlockSpec((tm, D_MODEL), lambda i, j: (i, 0)),
        out_shape=jax.ShapeDtypeStruct((seq, D_MODEL), F32),
        compiler_params=_params(dimension_semantics=("parallel", "arbitrary")),
    )(dz, win_g)


def _prenorm_bwd(x, dh, dx_res, g, scale, tm, name, after=None):
    seq = x.shape[0]

    def body(x_ref, dh_ref, dxr_ref, g_ref, sc_ref, dx_ref, acc_ref):
        @pl.when(pl.program_id(0) == 0)
        def _():
            acc_ref[...] = jnp.zeros_like(acc_ref)

        rs, xn = _rms_parts(x_ref[...])
        dh = dh_ref[...]
        acc_ref[0:1, :] += jnp.sum(dh, axis=0, keepdims=True)
        acc_ref[1:2, :] += jnp.sum(dh * (xn * g_ref[...]), axis=0, keepdims=True)
        dhn = dh * (1.0 + sc_ref[...])
        acc_ref[2:3, :] += jnp.sum(dhn * xn, axis=0, keepdims=True)
        dxn = dhn * g_ref[...]
        dx_ref[...] = rs * (dxn - xn * jnp.mean(dxn * xn, axis=-1, keepdims=True)) + dxr_ref[...]

    tile = pl.BlockSpec((tm, D_MODEL), lambda i: (i, 0))
    return _pallas_after(
        body, 5, after, name=name, grid=(seq // tm,),
        in_specs=[tile, tile, tile, _row_spec(), _row_spec()],
        out_specs=[tile, pl.BlockSpec((8, D_MODEL), lambda i: (0, 0))],
        out_shape=[jax.ShapeDtypeStruct((seq, D_MODEL), F32), jax.ShapeDtypeStruct((8, D_MODEL), F32)],
        compiler_params=_params(dimension_semantics=("arbitrary",)),
    )(x, dh, dx_res, g, scale)


def _adamw_math(w, g, m, v):
    m = ADAM_B1 * m + (1.0 - ADAM_B1) * g
    v = ADAM_B2 * v + (1.0 - ADAM_B2) * (g * g)
    m_hat = m / (1.0 - ADAM_B1 ** ADAM_STEP)
    v_hat = v / (1.0 - ADAM_B2 ** ADAM_STEP)
    delta = -ADAM_LR * (m_hat / (jnp.sqrt(v_hat) + ADAM_EPS) + ADAM_WD * w)
    return delta, m, v


def _adamw_sharded(w, m, v, contrib, tr, name):
    depth, rows, cols = w.shape
    n_parts = contrib.shape[1]

    def body(w_ref, m_ref, v_ref, c_ref, g_ref, d_ref, mo_ref, vo_ref):
        g = c_ref[0].astype(F32)
        for p in range(1, n_parts):
            g += c_ref[p].astype(F32)
        delta, mn, vn = _adamw_math(w_ref[...], g, m_ref[...], v_ref[...])
        g_ref[...] = g
        d_ref[...] = delta
        mo_ref[...] = mn
        vo_ref[...] = vn

    tile = pl.BlockSpec((None, tr, cols), lambda l, i: (l, i, 0))
    shape = jax.ShapeDtypeStruct(w.shape, F32)
    return pl.pallas_call(
        body, name=name, grid=(depth, rows // tr),
        in_specs=[tile, tile, tile, pl.BlockSpec((None, n_parts, tr, cols), lambda l, i: (l, 0, i, 0))],
        out_specs=[tile] * 4, out_shape=[shape] * 4,
        compiler_params=_params(dimension_semantics=("parallel", "parallel")),
    )(w, m, v, contrib)


def _adamw_layer(w, m, v, contribs, l, tr, name, prev=None):
    _, rows, cols = w.shape
    n = len(contribs)

    def body(*refs):
        w_ref, m_ref, v_ref = refs[:3]
        c_refs = refs[3:3 + n]
        g_ref, d_ref, mo_ref, vo_ref = refs[-4:]
        g = c_refs[0][...].astype(F32)
        for c_ref in c_refs[1:]:
            g += c_ref[...].astype(F32)
        delta, mn, vn = _adamw_math(w_ref[...], g, m_ref[...], v_ref[...])
        g_ref[...] = g
        d_ref[...] = delta
        mo_ref[...] = mn
        vo_ref[...] = vn

    tile = pl.BlockSpec((None, tr, cols), lambda i: (l, i, 0))
    in_specs = [tile, tile, tile] + [pl.BlockSpec((None, tr, cols), lambda i, s=slot: (s, i, 0)) for _, slot in contribs]
    operands = [w, m, v] + [arr for arr, _ in contribs]
    aliases = {}
    if prev is not None:
        aliases = {len(operands) + k: k for k in range(4)}
        in_specs += [pl.BlockSpec(memory_space=pl.ANY)] * 4
        operands += list(prev)
    shape = jax.ShapeDtypeStruct(w.shape, F32)
    return pl.pallas_call(
        body, name=name, grid=(rows // tr,), in_specs=in_specs, out_specs=[tile] * 4, out_shape=[shape] * 4,
        input_output_aliases=aliases,
        compiler_params=_params(dimension_semantics=("parallel",)),
    )(*operands)


def _adamw_small(w_pack, m_pack, v_pack, g_late, g_early):
    def body(w_ref, m_ref, v_ref, gl_ref, ge_ref, g_ref, d_ref, mo_ref, vo_ref):
        g_l, g_e = gl_ref[0], ge_ref[0]
        for d in range(1, N_DEV):
            g_l += gl_ref[d]
            g_e += ge_ref[d]
        g = jnp.concatenate([g_l, g_e], axis=0)
        w = w_ref[...]
        r0, r1, r2 = LB_ROW0, LB_ROW0 + 8, LB_ROW0 + 16
        lg0, lg1 = w[r0:r1], w[r1:r2]
        mx = jnp.maximum(lg0, lg1)
        e0, e1 = jnp.exp(lg0 - mx), jnp.exp(lg1 - mx)
        p0, p1 = e0 / (e0 + e1), e1 / (e0 + e1)
        low = ((p0 - p0), (p0 + p1) - p0)
        dlow = [g_rows * jnp.where((lo > 0.0) & (lo < 1.0), 1.0, jnp.where((lo == 0.0) | (lo == 1.0), 0.5, 0.0))
                for g_rows, lo in ((g[r0:r1], low[0]), (g[r1:r2], low[1]))]
        dp0 = (dlow[0] + dlow[1]) - (dlow[0] + dlow[1])
        dp1 = dlow[1]
        inner = p0 * dp0 + p1 * dp1
        g = jnp.concatenate([g[:r0], p0 * (dp0 - inner), p1 * (dp1 - inner), g[r2:]], axis=0)
        delta, mn, vn = _adamw_math(w, g, m_ref[...], v_ref[...])
        g_ref[...] = g
        d_ref[...] = delta
        mo_ref[...] = mn
        vo_ref[...] = vn

    shape = jax.ShapeDtypeStruct(w_pack.shape, F32)
    return pl.pallas_call(body, name="adamw_small", out_shape=[shape] * 4, compiler_params=_params())(
        w_pack, m_pack, v_pack, g_late, g_early)


def _pack_small(parts, first=0, last=len(_SMALL_ROWS)):
    rows = [(parts[name] if l is None else parts[name][l]).reshape(n, 128) for name, l, n in _SMALL_ROWS[first:last]]
    if last == len(_SMALL_ROWS):
        rows.append(jnp.zeros((SMALL_ROWS_PAD - sum(n for _, _, n in _SMALL_ROWS), 128), F32))
    return jnp.concatenate(rows, axis=0)


def _unpack_small(pack, shapes):
    pieces, r = {}, 0
    for name, l, n in _SMALL_ROWS:
        pieces.setdefault(name, []).append(pack[r:r + n])
        r += n
    return {name: jnp.concatenate(p, axis=0).reshape(shapes[name]) for name, p in pieces.items()}


def kernel(x, c, w_ada, b_ada, g_pre, g_post, w_in, pool_w, pool_scale, lb_logits, hgrn_norm_g, w_pool_o, w_hgrn_o, w_out, loss_target, m_w_ada, m_b_ada, m_g_pre, m_g_post, m_w_in, m_pool_w, m_pool_scale, m_lb_logits, m_hgrn_norm_g, m_w_pool_o, m_w_hgrn_o, m_w_out, v_w_ada, v_b_ada, v_g_pre, v_g_post, v_w_in, v_pool_w, v_pool_scale, v_lb_logits, v_hgrn_norm_g, v_w_pool_o, v_w_hgrn_o, v_w_out):
    seq = x.shape[1]
    tm = min(512, seq)
    tm_merge = min(256, seq)
    pos = _my_position()
    me = pos[3]

    c_all = _allgather_small(c, "allgather_c").reshape(N_DEV, D_MODEL)
    b_cols = lax.dynamic_slice_in_dim(b_ada, me * ADA_COLS, ADA_COLS, axis=1)
    ada_part = _ada_fwd(c_all, w_ada, b_cols)
    ada_all = _allgather_small(ada_part.reshape(DEPTH * N_DEV, ADA_COLS), "allgather_ada")
    ada = lax.dynamic_index_in_dim(ada_all.reshape(N_DEV, DEPTH, N_DEV, ADA_COLS), me, axis=2, keepdims=False)
    ada = jnp.transpose(ada, (1, 0, 2)).reshape(DEPTH, 3 * D_MODEL)
    shift = [ada[l:l + 1, 0:D_MODEL] for l in range(DEPTH)]
    scale = [ada[l:l + 1, D_MODEL:2 * D_MODEL] for l in range(DEPTH)]
    gate = [ada[l:l + 1, 2 * D_MODEL:] for l in range(DEPTH)]

    big = dict(win=w_in, wpo=w_pool_o, who=w_hgrn_o, wout=w_out)
    units = [["win0"], ["wpo0", "who0", "wout0"], ["win1", "wpo1", "who1", "wout1"]]
    g_streams = [_gather_streams(keys) for keys in units]
    g_state = [None] * len(units)

    def gather_start(u, after):
        bufs = {}
        for k in units[u]:
            arr = big[k[:-1]]
            bufs["s_" + k] = arr[int(k[-1])].astype(WIRE_DTYPE)
            bufs["g_" + k] = lax.empty((N_DEV,) + arr.shape[1:], WIRE_DTYPE)

        def own_shards(refs, p):
            return [(refs["s_" + k], refs["g_" + k].at[p[3]]) for k in units[u]]

        bufs, sems, token = _comm_call(f"gather_start_{u}", bufs, start=list(g_streams[u][:2]),
                                       local=(len(units[u]), own_shards), after=after)
        g_state[u] = dict(bufs=bufs, sems=sems)
        return token

    def gather_pass(u, after):
        st = g_state[u]
        to_chips, _, pass_on = g_streams[u]
        st["bufs"], (st["pass_sems"],), _ = _comm_call(f"gather_pass_{u}", st["bufs"], start=[pass_on],
                                                       wait=[(to_chips, st["sems"][0])], after=after)

    def gather_done(u, after=None):
        st = g_state[u]
        _, to_sibling, pass_on = g_streams[u]
        bufs, _, _ = _comm_call(f"gather_done_{u}", st["bufs"], after=after,
                                wait=[(to_sibling, st["sems"][1]), (pass_on, st["pass_sems"])])
        return {k: bufs["g_" + k] for k in units[u]}

    token = gather_start(0, ada_all)

    lb = _lb_fwd(lb_logits)

    gw = {}
    xs, saved = [x[0]], []
    for l in range(DEPTH):
        h = _prenorm_fwd(xs[l], g_pre[l:l + 1], shift[l], scale[l], tm, f"prenorm_fwd_{l}",
                         after=token if l == 0 else None)
        token = None
        if l == 0:
            gather_pass(0, h)
            gw.update(gather_done(0))
            token = gather_start(1, gw["win0"])
        else:
            gather_pass(2, h)
            gw.update(gather_done(2))
        z = _in_proj(h, gw[f"win{l}"], tm, f"in_proj_{l}", after=token)
        if l == 0:
            gather_pass(1, z)
            token = gather_start(2, g_state[1]["bufs"]["g_wpo0"])
        a_in = _pool_fwd(z, pool_w[l], pool_scale[l:l + 1], f"pool_fwd_{l}", after=token)
        o, b_in, states = _hgrn_fwd(z, lb[l:l + 1], hgrn_norm_g[l:l + 1], f"hgrn_fwd_{l}", after=token)
        if l == 0:
            gw.update(gather_done(1, b_in))
        who_l = gw[f"who{l}"].reshape(D_MODEL, D_MODEL)
        wout_l = gw[f"wout{l}"].reshape(D_MODEL, D_MODEL)
        ba, bb, merged, y, x_next = _merge_fwd(a_in, b_in, z, xs[l], gw[f"wpo{l}"], who_l, wout_l, gate[l],
                                               g_post[l:l + 1], tm_merge, f"merge_fwd_{l}")
        xs.append(x_next)
        saved.append((h, z, a_in, o, b_in, states, ba, bb, merged, y, who_l, wout_l))

    loss_part, dx = _loss_grad(xs[DEPTH], loss_target[0], tm)
    loss = lax.psum(loss_part[0, 0], ("x", "y", "c"))

    chips = _other_chips(pos)
    pair_idx = jnp.stack([_dev_index(cx, cy, pos[2]) for cx, cy in chips] + [me]).astype(jnp.int32)
    pair_rows = dict(win=256, wpo=POOL_WIDTH, who=HEAD_DIM, wout=HEAD_DIM)

    def scatter_pair_start(u, grads):
        keys = list(grads)
        pair, to_chips = _scatter_streams(keys)
        bufs = {}
        for k in keys:
            bufs["g_" + k] = grads[k]
            bufs["st_" + k] = lax.empty((4,) + grads[k].shape[1:], WIRE_DTYPE)
        bufs, (sems,), token = _comm_call(f"scatter_pair_start_{u}", bufs, start=[pair])
        return dict(u=u, keys=keys, pair=pair, to_chips=to_chips, bufs=bufs, sems=sems, token=token)

    def scatter_chips_start(st, after):
        u, keys = st["u"], st["keys"]
        bufs, _, _ = _comm_call(f"scatter_pair_done_{u}", st["bufs"], wait=[(st["pair"], st["sems"])], after=after)
        bufs2 = {}
        for k in keys:
            bufs2["ps_" + k] = _pair_sum(bufs["g_" + k], bufs["st_" + k], pair_idx, bufs["g_" + k].shape[1],
                                         f"pair_sum_{k}")
            bufs2["ld_" + k] = lax.empty((3,) + bufs["g_" + k].shape[1:], WIRE_DTYPE)
        bufs2, (sems,), token = _comm_call(f"scatter_chips_start_{u}", bufs2, start=[st["to_chips"]])
        st.update(bufs=bufs2, sems=sems, token=token)

    def scatter_finish(st, after):
        bufs, _, _ = _comm_call(f"scatter_chips_done_{st['u']}", st["bufs"], wait=[(st["to_chips"], st["sems"])],
                                after=after)
        return {k: [(bufs["ps_" + k], 3), (bufs["ld_" + k], 0), (bufs["ld_" + k], 1), (bufs["ld_" + k], 2)]
                for k in st["keys"]}

    d_ada, small, scat = [None] * DEPTH, [None] * DEPTH, {}
    for l in reversed(range(DEPTH)):
        h, z, a_in, o, b_in, states, ba, bb, merged, y, who_l, wout_l = saved[l]
        dy, dba, dbb, da_in, db_in, dmg, acc_post = _merge_bwd(
            dx, y, ba, bb, z, gw[f"wpo{l}"], who_l, wout_l, gate[l], g_post[l:l + 1], tm_merge, f"merge_bwd_{l}")
        g_small = {
            f"wout{l}": _grad_tn(merged, dy, 512, False, f"grad_w_out_{l}").reshape(N_DEV, HEAD_DIM, D_MODEL),
            f"who{l}": _grad_tn(b_in, dbb, 512, False, f"grad_w_hgrn_o_{l}").reshape(N_DEV, HEAD_DIM, D_MODEL),
            f"wpo{l}": _grad_tn(a_in, dba, GROUP_DIM, True, f"grad_w_pool_o_{l}")}
        st_small = scat[f"small{l}"] = scatter_pair_start(f"small{l}", g_small)
        dzh, dlb, dgn = _hgrn_bwd(db_in, z, o, states, lb[l:l + 1], hgrn_norm_g[l:l + 1], f"hgrn_bwd_{l}",
                                  after=st_small["token"])
        scatter_chips_start(st_small, dzh)
        dpv, dpg, dpw, dps = _pool_bwd(da_in, z, pool_w[l], pool_scale[l:l + 1], f"pool_bwd_{l}",
                                       after=st_small["token"])
        dz = jnp.concatenate([dpv, dpg, dzh, dmg], axis=1)
        small[l] = dict(g_post=acc_post[1], pool_w=dpw, pool_scale=dps[0], lb_logits=dlb[0], hgrn_norm_g=dgn[0])
        token = None
        if l == 0:
            parts = {name: jnp.stack([small[0][name], small[1][name]]) for name in small[0]}
            parts.update(b_ada=[None, d_ada[1]], g_pre=[None, small[1]["g_pre"]])
            sg_stream = _direct_gather_stream("sg")
            early = _pack_small(parts, 2)
            sg_bufs, (sg_sems,), token = _comm_call(
                "small_grads_start", dict(s_sg=early, g_sg=lax.empty((N_DEV,) + early.shape, F32)),
                start=[sg_stream], local=(1, lambda refs, p: [(refs["s_sg"], refs["g_sg"].at[p[3]])]))
        st_win = scat[f"win{l}"] = scatter_pair_start(
            f"win{l}", {f"win{l}": _in_proj_dw(h, dz, f"grad_w_in_{l}", after=token)})
        dh = _in_proj_dh(dz, gw[f"win{l}"], tm, f"in_proj_dh_{l}", after=st_win["token"])
        scatter_chips_start(st_win, dh)
        dx, acc_pre = _prenorm_bwd(xs[l], dh, dx, g_pre[l:l + 1], scale[l], tm, f"prenorm_bwd_{l}",
                                   after=st_win["token"])
        d_ada[l] = jnp.concatenate([acc_pre[0], acc_pre[1], acc_post[0]])
        small[l]["g_pre"] = acc_pre[2]
    grad_x = dx[None]

    moments = dict(win=(m_w_in, v_w_in), wpo=(m_w_pool_o, v_w_pool_o), who=(m_w_hgrn_o, v_w_hgrn_o),
                   wout=(m_w_out, v_w_out))
    big_out = {}

    def finish_unit(unit, after):
        for k, contribs in scatter_finish(scat[unit], after).items():
            wname, l = k[:-1], int(k[-1])
            big_out[wname] = _adamw_layer(big[wname], moments[wname][0], moments[wname][1], contribs, l,
                                          pair_rows[wname], f"adamw_{k}", prev=big_out.get(wname))
            after = big_out[wname][0]
        return after

    after = dx
    for unit in ("small1", "win1", "small0"):
        after = finish_unit(unit, after)

    parts = dict(b_ada=[d_ada[0]], g_pre=[small[0]["g_pre"]])
    g_late = _allgather_small(_pack_small(parts, 0, 2), "allgather_late_grads", after=after)
    sg_bufs, _, _ = _comm_call("small_grads_done", sg_bufs, wait=[(sg_stream, sg_sems)], after=g_late)
    g_early = sg_bufs["g_sg"]
    small_names = list(dict.fromkeys(name for name, _, _ in _SMALL_ROWS))
    weights = dict(b_ada=b_ada, g_pre=g_pre, g_post=g_post, pool_w=pool_w, pool_scale=pool_scale,
                   lb_logits=lb_logits, hgrn_norm_g=hgrn_norm_g)
    m_small = dict(b_ada=m_b_ada, g_pre=m_g_pre, g_post=m_g_post, pool_w=m_pool_w, pool_scale=m_pool_scale,
                   lb_logits=m_lb_logits, hgrn_norm_g=m_hgrn_norm_g)
    v_small = dict(b_ada=v_b_ada, g_pre=v_g_pre, g_post=v_g_post, pool_w=v_pool_w, pool_scale=v_pool_scale,
                   lb_logits=v_lb_logits, hgrn_norm_g=v_hgrn_norm_g)
    shapes = {name: weights[name].shape for name in small_names}
    small_out = [_unpack_small(p, shapes) for p in
                 _adamw_small(_pack_small(weights), _pack_small(m_small), _pack_small(v_small), g_late, g_early)]

    d_ada_all = jnp.stack([g_late[:, 0:24, :].reshape(N_DEV, 3 * D_MODEL),
                           g_early[:, 0:24, :].reshape(N_DEV, 3 * D_MODEL)], axis=1)
    d_cols = jnp.transpose(lax.dynamic_slice_in_dim(d_ada_all, me * ADA_COLS, ADA_COLS, axis=2), (1, 0, 2))
    g_w_ada = _ada_bwd(c_all, d_cols)
    ada_out = _adamw_sharded(w_ada, m_w_ada, v_w_ada, g_w_ada[:, None], 256, "adamw_w_ada")
    finish_unit("win0", ada_out[0])

    def leaf(kind):
        s = small_out[kind]
        return (ada_out[kind], s["b_ada"], s["g_pre"], s["g_post"], big_out["win"][kind], s["pool_w"], s["pool_scale"],
                s["lb_logits"], s["hgrn_norm_g"], big_out["wpo"][kind], big_out["who"][kind], big_out["wout"][kind])

    return (loss, grad_x) + leaf(0) + leaf(1) + leaf(2) + leaf(3)
```

```python
import jax
import jax.numpy as jnp
from jax import lax
from jax.experimental import pallas as pl
from jax.experimental.pallas import tpu as pltpu

F32 = jnp.float32
MXU_DTYPE = jnp.bfloat16
WIRE_DTYPE = jnp.bfloat16

N_DEV = 8
DEPTH = 2
D_MODEL = 1024
HEADS = 8
HEAD_DIM = 128
POOL_GROUPS = 4
GROUP_DIM = 128
POOL_WIDTH = POOL_GROUPS * GROUP_DIM
IN_WIDTH = 7168
CHUNK = 64
SUB = 16
N_SUB = CHUNK // SUB
EXP_CLAMP = 80.0
NORM_EPS = 1e-6
LOG_FLOOR = 1e-30
ADA_COLS = 3 * D_MODEL // N_DEV
IN_COLS = IN_WIDTH // N_DEV
COL_HQ, COL_HF, COL_HI, COL_HG, COL_MGP, COL_MGH = 1, 2, 3, 4, 5, 6

ADAM_LR = 0.001
ADAM_B1 = 0.9
ADAM_B2 = 0.999
ADAM_EPS = 1e-08
ADAM_WD = 0.01
ADAM_STEP = 10

VMEM_LIMIT = 48 * 1024 * 1024
MESH_ID = pl.DeviceIdType.MESH
HIGHEST = lax.Precision.HIGHEST

_SMALL_ROWS = (("b_ada", 0, 24), ("g_pre", 0, 8), ("b_ada", 1, 24), ("g_pre", 1, 8), ("g_post", None, 16),
               ("pool_w", None, 1024), ("pool_scale", None, 8), ("lb_logits", None, 16), ("hgrn_norm_g", None, 2))
SMALL_LATE_ROWS = 32
SMALL_ROWS_PAD = 1136
LB_ROW0 = 32 + 32 + 16 + 1024 + 8


def _params(**kw):
    return pltpu.CompilerParams(vmem_limit_bytes=VMEM_LIMIT, **kw)


def _sigmoid(v):
    return 1.0 / (1.0 + jnp.exp(-v))


def _dsilu(v, s):
    return s * (1.0 + v * (1.0 - s))


def _dot(a, b):
    return jnp.dot(a.astype(MXU_DTYPE), b.astype(MXU_DTYPE), preferred_element_type=F32)


def _dot_nt(a, b):
    return lax.dot_general(a.astype(MXU_DTYPE), b.astype(MXU_DTYPE), (((1,), (1,)), ((), ())),
                           preferred_element_type=F32)


def _dot_tn(a, b):
    return lax.dot_general(a.astype(MXU_DTYPE), b.astype(MXU_DTYPE), (((0,), (0,)), ((), ())),
                           preferred_element_type=F32)


def _pallas_after(body, n_in, after, *, in_specs, **kw):
    if after is None:
        return pl.pallas_call(body, in_specs=in_specs, **kw)

    def tied(*refs):
        body(*refs[:n_in], *refs[n_in + 1:])

    call = pl.pallas_call(tied, in_specs=list(in_specs) + [pl.BlockSpec(memory_space=pl.ANY)], **kw)
    return lambda *operands: call(*operands, after)


def _my_position():
    mx, my, mc = lax.axis_index("x"), lax.axis_index("y"), lax.axis_index("c")
    return mx, my, mc, 4 * mx + 2 * my + mc


def _peer(mx, my, mc, k):
    px = 1 - mx if (k >> 2) & 1 else mx
    py = 1 - my if (k >> 1) & 1 else my
    pc = 1 - mc if k & 1 else mc
    return (px, py, pc), 4 * px + 2 * py + pc


def _allgather_small(v, name, after=None):
    rows, cols = v.shape

    def body(v_ref, out_ref, send_sems, recv_sems):
        mx, my, mc, me = _my_position()
        out_ref[me] = v_ref[...]
        copies = []
        for k in range(1, N_DEV):
            peer, _ = _peer(mx, my, mc, k)
            cp = pltpu.make_async_remote_copy(
                src_ref=v_ref, dst_ref=out_ref.at[me],
                send_sem=send_sems.at[k - 1], recv_sem=recv_sems.at[k - 1],
                device_id=peer, device_id_type=MESH_ID)
            cp.start()
            copies.append(cp)
        for cp in copies:
            cp.wait()

    return _pallas_after(
        body, 1, after, name=name,
        out_shape=jax.ShapeDtypeStruct((N_DEV, rows, cols), v.dtype),
        in_specs=[pl.BlockSpec(memory_space=pltpu.VMEM)],
        out_specs=pl.BlockSpec(memory_space=pltpu.VMEM),
        scratch_shapes=[pltpu.SemaphoreType.DMA((N_DEV - 1,)), pltpu.SemaphoreType.DMA((N_DEV - 1,))],
        compiler_params=_params(),
    )(v)


class _Stream:
    def __init__(self, n, plan):
        self.n, self.plan = n, plan


def _comm_call(name, bufs, start=(), wait=(), after=None):
    names = list(bufs)

    def body(*refs):
        it = iter(refs)
        buf_refs = {n: next(it) for n in names}
        wait_sems = [(next(it), next(it)) for _ in wait]
        if after is not None:
            next(it)
        start_sems = [(next(it), next(it)) for _ in start]
        for _ in names:
            next(it)
        token = next(it)
        pos = _my_position()

        def descriptors(stream, sems):
            return [pltpu.make_async_remote_copy(src_ref=src, dst_ref=dst, send_sem=sems[0].at[k], recv_sem=sems[1].at[k],
                                                 device_id=dev, device_id_type=MESH_ID)
                    for k, (src, dst, dev) in enumerate(stream.plan(buf_refs, pos))]

        for (stream, _), sems in zip(wait, wait_sems):
            for cp in descriptors(stream, sems):
                cp.wait_send()
                cp.wait_recv()
        for stream, sems in zip(start, start_sems):
            for cp in descriptors(stream, sems):
                cp.start()
        token[...] = jnp.zeros_like(token)

    hbm = pl.BlockSpec(memory_space=pltpu.HBM)
    sem = pl.BlockSpec(memory_space=pltpu.SEMAPHORE)
    operands = [pltpu.with_memory_space_constraint(bufs[n], pltpu.HBM) for n in names]
    in_specs = [hbm] * len(names)
    for _, (send_sems, recv_sems) in wait:
        operands += [send_sems, recv_sems]
        in_specs += [sem, sem]
    if after is not None:
        operands.append(after)
        in_specs.append(pl.BlockSpec(memory_space=pl.ANY))
    out_shape, out_specs = [], []
    for stream in start:
        out_shape += [pltpu.SemaphoreType.DMA((stream.n,)), pltpu.SemaphoreType.DMA((stream.n,))]
        out_specs += [sem, sem]
    n_sem_out = len(out_shape)
    out_shape += [pltpu.HBM(bufs[n].shape, bufs[n].dtype) for n in names]
    out_specs += [hbm] * len(names)
    out_shape.append(jax.ShapeDtypeStruct((8, 128), F32))
    out_specs.append(pl.BlockSpec(memory_space=pltpu.VMEM))
    outs = pl.pallas_call(
        body, name=name, out_shape=out_shape, in_specs=in_specs, out_specs=out_specs,
        input_output_aliases={i: n_sem_out + i for i in range(len(names))},
        compiler_params=pltpu.CompilerParams(has_side_effects=pltpu.SideEffectType.DATAFLOW_SIDE_EFFECTING),
    )(*operands)
    sems = [(outs[2 * i], outs[2 * i + 1]) for i in range(len(start))]
    return dict(zip(names, outs[n_sem_out:n_sem_out + len(names)])), sems, outs[-1]


def _with_own_slot(block, me):
    return lax.dynamic_update_index_in_dim(lax.empty((N_DEV,) + block.shape, block.dtype), block, me, 0)


def _other_chips(pos):
    mx, my, _, _ = pos
    return [(1 - mx if i & 2 else mx, 1 - my if i & 1 else my) for i in (1, 2, 3)]


def _dev_index(px, py, pc):
    return 4 * px + 2 * py + pc


def _gather_streams(keys):
    def to_chips(refs, pos):
        _, _, mc, me = pos
        return [(refs["s_" + k], refs["g_" + k].at[me], (cx, cy, mc)) for k in keys for cx, cy in _other_chips(pos)]

    def to_sibling(refs, pos):
        mx, my, mc, me = pos
        return [(refs["s_" + k], refs["g_" + k].at[me], (mx, my, 1 - mc)) for k in keys]

    def pass_on(refs, pos):
        mx, my, mc, _ = pos
        out = []
        for k in keys:
            for cx, cy in _other_chips(pos):
                slot = refs["g_" + k].at[_dev_index(cx, cy, mc)]
                out.append((slot, slot, (mx, my, 1 - mc)))
        return out

    return _Stream(3 * len(keys), to_chips), _Stream(len(keys), to_sibling), _Stream(3 * len(keys), pass_on)


def _direct_gather_stream(key):
    def plan(refs, pos):
        mx, my, mc, me = pos
        return [(refs["s_" + key], refs["g_" + key].at[me], _peer(mx, my, mc, k)[0]) for k in range(1, N_DEV)]

    return _Stream(N_DEV - 1, plan)


def _scatter_streams(keys):
    def pair(refs, pos):
        mx, my, mc, _ = pos
        sib = (mx, my, 1 - mc)
        out = []
        for k in keys:
            for i, (cx, cy) in enumerate(_other_chips(pos)):
                out.append((refs["g_" + k].at[_dev_index(cx, cy, 1 - mc)], refs["st_" + k].at[i], sib))
            out.append((refs["g_" + k].at[_dev_index(mx, my, 1 - mc)], refs["st_" + k].at[3], sib))
        return out

    def chips(refs, pos):
        mc = pos[2]
        return [(refs["ps_" + k].at[i], refs["ld_" + k].at[i], (cx, cy, mc))
                for k in keys for i, (cx, cy) in enumerate(_other_chips(pos))]

    return _Stream(4 * len(keys), pair), _Stream(3 * len(keys), chips)


def _pair_sum(g, st, idx, tr, name):
    _, rows, cols = g.shape

    def body(idx_ref, g_ref, st_ref, out_ref):
        out_ref[...] = (g_ref[...].astype(F32) + st_ref[...].astype(F32)).astype(out_ref.dtype)

    return pl.pallas_call(
        body, name=name,
        grid_spec=pltpu.PrefetchScalarGridSpec(
            num_scalar_prefetch=1, grid=(4, rows // tr),
            in_specs=[pl.BlockSpec((None, tr, cols), lambda j, i, idx_ref: (idx_ref[j], i, 0)),
                      pl.BlockSpec((None, tr, cols), lambda j, i, idx_ref: (j, i, 0))],
            out_specs=pl.BlockSpec((None, tr, cols), lambda j, i, idx_ref: (j, i, 0))),
        out_shape=jax.ShapeDtypeStruct((4, rows, cols), WIRE_DTYPE),
        compiler_params=_params(dimension_semantics=("parallel", "parallel")),
    )(idx, g, st)


def _ada_fwd(c_all, w_ada, b_cols):
    def body(c_ref, w_ref, b_ref, out_ref):
        cv = c_ref[...]
        ca = cv * _sigmoid(cv)
        for l in range(DEPTH):
            out_ref[l] = jnp.dot(ca, w_ref[l], precision=HIGHEST, preferred_element_type=F32) + b_ref[l:l + 1, :]

    return pl.pallas_call(
        body, name="ada_fwd",
        out_shape=jax.ShapeDtypeStruct((DEPTH, N_DEV, ADA_COLS), F32),
        compiler_params=_params(),
    )(c_all, w_ada, b_cols)


def _ada_bwd(c_all, d_cols):
    def body(c_ref, d_ref, out_ref):
        cv = c_ref[...]
        ca = cv * _sigmoid(cv)
        for l in range(DEPTH):
            out_ref[l] = lax.dot_general(ca, d_ref[l], (((0,), (0,)), ((), ())), precision=HIGHEST,
                                         preferred_element_type=F32)

    return pl.pallas_call(
        body, name="ada_bwd",
        out_shape=jax.ShapeDtypeStruct((DEPTH, D_MODEL, ADA_COLS), F32),
        compiler_params=_params(),
    )(c_all, d_cols)


def _lower_bounds(logits):
    m = jnp.maximum(logits[0:1], logits[1:2])
    e0, e1 = jnp.exp(logits[0:1] - m), jnp.exp(logits[1:2] - m)
    den = e0 + e1
    p0, p1 = e0 / den, e1 / den
    low0 = p0 - p0
    low1 = (p0 + p1) - p0
    return (p0, p1), (low0, low1)


def _lb_fwd(lb_logits):
    def body(lg_ref, out_ref):
        _, (low0, low1) = _lower_bounds(lg_ref[...])
        out_ref[0:1, :] = jnp.clip(low0, 0.0, 1.0)
        out_ref[1:2, :] = jnp.clip(low1, 0.0, 1.0)

    return pl.pallas_call(body, name="lb_fwd", out_shape=jax.ShapeDtypeStruct(lb_logits.shape, F32),
                          compiler_params=_params())(lb_logits)


def _row_spec(cols=D_MODEL):
    return pl.BlockSpec((1, cols), lambda *_: (0, 0))


def _prenorm_fwd(x, g, shift, scale, tm, name, after=None):
    seq = x.shape[0]

    def body(x_ref, g_ref, sh_ref, sc_ref, h_ref):
        xv = x_ref[...]
        rs = lax.rsqrt(jnp.mean(xv * xv, axis=-1, keepdims=True) + NORM_EPS)
        h = (xv * rs * g_ref[...]) * (1.0 + sc_ref[...]) + sh_ref[...]
        h_ref[...] = h.astype(h_ref.dtype)

    tile = pl.BlockSpec((tm, D_MODEL), lambda i: (i, 0))
    return _pallas_after(
        body, 4, after, name=name, grid=(seq // tm,),
        in_specs=[tile, _row_spec(), _row_spec(), _row_spec()], out_specs=tile,
        out_shape=jax.ShapeDtypeStruct((seq, D_MODEL), MXU_DTYPE),
        compiler_params=_params(dimension_semantics=("parallel",)),
    )(x, g, shift, scale)


def _in_proj(h, win_g, tm, name, after=None):
    seq = h.shape[0]

    def body(h_ref, w_ref, z_ref):
        z_ref[...] = jnp.dot(h_ref[...], w_ref[...], preferred_element_type=F32)

    return _pallas_after(
        body, 2, after, name=name, grid=(N_DEV, seq // tm),
        in_specs=[pl.BlockSpec((tm, D_MODEL), lambda j, i: (i, 0)),
                  pl.BlockSpec((None, D_MODEL, IN_COLS), lambda j, i: (j, 0, 0))],
        out_specs=pl.BlockSpec((tm, IN_COLS), lambda j, i: (i, j)),
        out_shape=jax.ShapeDtypeStruct((seq, IN_WIDTH), F32),
        compiler_params=_params(dimension_semantics=("parallel", "parallel")),
    )(h, win_g)


def _shift_down(v, j, pos):
    return jnp.where(pos >= j, pltpu.roll(v, j, 0), 0.0)


def _shift_up(v, j, pos, seq):
    return jnp.where(pos < seq - j, pltpu.roll(v, seq - j, 0), 0.0)


def _select_window(g, candidates):
    out = candidates[-1]
    for i in range(len(candidates) - 2, -1, -1):
        out = jnp.where(g == i, candidates[i], out)
    return out


def _pool_mean_minus_token(u, g, pos):
    sums, acc = [], u
    for j in (1, 2, 4, 8):
        acc = acc + _shift_down(acc, j, pos)
        sums.append(acc)
    wsum = _select_window(g, sums)
    width = jnp.left_shift(2, g).astype(F32)
    count = jnp.minimum(pos.astype(F32) + 1.0, width)
    return wsum / count - u, count


def _pool_fwd(z, pool_w_l, pool_scale_l, name, after=None):
    seq = z.shape[0]

    def body(pv_ref, pg_ref, w_ref, sc_ref, out_ref):
        g = pl.program_id(0)
        pos = lax.broadcasted_iota(jnp.int32, (seq, GROUP_DIM), 0)
        pm, _ = _pool_mean_minus_token(pv_ref[...], g, pos)
        lin = _dot(pm, w_ref[...]) * sc_ref[...]
        pg = pg_ref[...]
        out_ref[...] = (lin * (pg * _sigmoid(pg))).astype(out_ref.dtype)

    return _pallas_after(
        body, 4, after, name=name, grid=(POOL_GROUPS,),
        in_specs=[pl.BlockSpec((seq, GROUP_DIM), lambda g: (0, g)),
                  pl.BlockSpec((seq, GROUP_DIM), lambda g: (0, POOL_GROUPS + g)),
                  pl.BlockSpec((None, GROUP_DIM, GROUP_DIM), lambda g: (g, 0, 0)),
                  pl.BlockSpec((1, GROUP_DIM), lambda g: (0, g))],
        out_specs=pl.BlockSpec((seq, GROUP_DIM), lambda g: (0, g)),
        out_shape=jax.ShapeDtypeStruct((seq, POOL_WIDTH), MXU_DTYPE),
        compiler_params=_params(dimension_semantics=("parallel",)),
    )(z, z, pool_w_l, pool_scale_l)


def _chunk_masks():
    row = lax.broadcasted_iota(jnp.int32, (CHUNK, CHUNK), 0)
    col = lax.broadcasted_iota(jnp.int32, (CHUNK, CHUNK), 1)
    causal = row >= col
    tri = causal.astype(F32)
    before_sub = (col < (row // SUB) * SUB).astype(F32)
    return causal, tri, before_sub


def _gates(zf, lb):
    sg = _sigmoid(zf)
    f = lb + (1.0 - lb) * sg
    logf = jnp.log(jnp.maximum(f, LOG_FLOOR))
    return sg, f, logf


def _intra_blocks(q_h, k_h, cum_h, base_h, causal):
    rel = cum_h - base_h
    out = []
    for i in range(N_SUB):
        rows = slice(i * SUB, (i + 1) * SUB)
        e_q = jnp.exp(rel[rows])
        base_i = jnp.concatenate([base_h[rows]] * N_SUB, axis=0)
        e_k = jnp.exp(jnp.minimum(base_i - cum_h, EXP_CLAMP))
        q_t = q_h[rows] * e_q
        k_t = k_h * e_k
        a_i = jnp.where(causal[rows], _dot_nt(q_t, k_t), 0.0)
        out.append((q_t, k_t, e_q, e_k, a_i))
    return out


def _hgrn_fwd(z, lb_l, gn_l, name, after=None):
    seq = z.shape[0]
    n_chunks = seq // CHUNK

    def body(hq_ref, hf_ref, hi_ref, hg_ref, lb_ref, gn_ref, o_ref, bin_ref, st_ref, state):
        @pl.when(pl.program_id(0) == 0)
        def _():
            state[...] = jnp.zeros_like(state)

        causal, tri, before_sub = _chunk_masks()
        _, f, logf = _gates(hf_ref[...], lb_ref[...])
        kk = 1.0 - f
        hq = hq_ref[...]
        q = hq * _sigmoid(hq)
        cum = jnp.dot(tri, logf, precision=HIGHEST, preferred_element_type=F32)
        base = jnp.dot(before_sub, logf, precision=HIGHEST, preferred_element_type=F32)
        st_ref[0] = state[...]
        for h in range(HEADS):
            sl = slice(h * HEAD_DIM, (h + 1) * HEAD_DIM)
            q_h, k_h, cum_h = q[:, sl], kk[:, sl], cum[:, sl]
            v_h = hi_ref[:, sl]
            st_h = state[h]
            blocks = _intra_blocks(q_h, k_h, cum_h, base[:, sl], causal)
            a = jnp.concatenate([b[4] for b in blocks], axis=0)
            o_h = _dot_nt(q_h * jnp.exp(cum_h), st_h) + _dot(a, v_h)
            last = jnp.sum(logf[:, sl], axis=0, keepdims=True)
            state[h] = st_h * jnp.exp(last) + _dot_tn(v_h, k_h * jnp.exp(last - cum_h))
            rs = lax.rsqrt(jnp.mean(o_h * o_h, axis=-1, keepdims=True) + NORM_EPS)
            hg = hg_ref[:, sl]
            o_ref[:, sl] = o_h
            bin_ref[:, sl] = ((o_h * rs * gn_ref[...]) * (hg * _sigmoid(hg))).astype(bin_ref.dtype)

    def col(block):
        return pl.BlockSpec((CHUNK, D_MODEL), lambda c: (c, block))

    tile = pl.BlockSpec((CHUNK, D_MODEL), lambda c: (c, 0))
    return _pallas_after(
        body, 6, after, name=name, grid=(n_chunks,),
        in_specs=[col(COL_HQ), col(COL_HF), col(COL_HI), col(COL_HG), _row_spec(), _row_spec(HEAD_DIM)],
        out_specs=[tile, tile, pl.BlockSpec((1, HEADS, HEAD_DIM, HEAD_DIM), lambda c: (c, 0, 0, 0))],
        out_shape=[jax.ShapeDtypeStruct((seq, D_MODEL), F32),
                   jax.ShapeDtypeStruct((seq, D_MODEL), MXU_DTYPE),
                   jax.ShapeDtypeStruct((n_chunks, HEADS, HEAD_DIM, HEAD_DIM), F32)],
        scratch_shapes=[pltpu.VMEM((HEADS, HEAD_DIM, HEAD_DIM), F32)],
        compiler_params=_params(dimension_semantics=("arbitrary",)),
    )(z, z, z, z, lb_l, gn_l)


def _rms_parts(y):
    rs = lax.rsqrt(jnp.mean(y * y, axis=-1, keepdims=True) + NORM_EPS)
    return rs, y * rs


def _merge_fwd(a_in, b_in, z, x, wpo_g, who_g, wout_g, gate, g_post, tm, name):
    seq = x.shape[0]

    def body(a_ref, b_ref, mgp_ref, mgh_ref, x_ref, wpo_ref, who_ref, wout_ref, gate_ref, gp_ref,
             ba_ref, bb_ref, mer_ref, y_ref, xn_ref):
        a = a_ref[...]
        ba = jnp.concatenate([_dot(a, wpo_ref[j]) for j in range(N_DEV)], axis=1)
        bb = _dot(b_ref[...], who_ref[...])
        merged = _sigmoid(mgp_ref[...]) * ba + _sigmoid(mgh_ref[...]) * bb
        y = _dot(merged, wout_ref[...])
        _, yn = _rms_parts(y)
        ba_ref[...] = ba.astype(ba_ref.dtype)
        bb_ref[...] = bb.astype(bb_ref.dtype)
        mer_ref[...] = merged.astype(mer_ref.dtype)
        y_ref[...] = y
        xn_ref[...] = x_ref[...] + gate_ref[...] * (yn * gp_ref[...])

    def tile(cols=D_MODEL, block=0):
        return pl.BlockSpec((tm, cols), lambda i: (i, block))

    full = pl.BlockSpec((D_MODEL, D_MODEL), lambda i: (0, 0))
    act = jax.ShapeDtypeStruct((seq, D_MODEL), MXU_DTYPE)
    f32 = jax.ShapeDtypeStruct((seq, D_MODEL), F32)
    return pl.pallas_call(
        body, name=name, grid=(seq // tm,),
        in_specs=[tile(POOL_WIDTH), tile(), tile(block=COL_MGP), tile(block=COL_MGH), tile(),
                  pl.BlockSpec((N_DEV, POOL_WIDTH, GROUP_DIM), lambda i: (0, 0, 0)),
                  full, full, _row_spec(), _row_spec()],
        out_specs=[tile(), tile(), tile(), tile(), tile()],
        out_shape=[act, act, act, f32, f32],
        compiler_params=_params(dimension_semantics=("parallel",)),
    )(a_in, b_in, z, z, x, wpo_g, who_g, wout_g, gate, g_post)


def _loss_grad(x_out, target, tm):
    seq = x_out.shape[0]

    def body(x_ref, t_ref, loss_ref, dx_ref):
        @pl.when(pl.program_id(0) == 0)
        def _():
            loss_ref[...] = jnp.zeros_like(loss_ref)

        err = x_ref[...] - t_ref[...]
        per_token = jnp.mean(err * err, axis=-1, keepdims=True)
        loss_ref[...] += 0.5 * jnp.sum(per_token, axis=0, keepdims=True)
        dx_ref[...] = err * (1.0 / D_MODEL)

    tile = pl.BlockSpec((tm, D_MODEL), lambda i: (i, 0))
    return pl.pallas_call(
        body, name="loss_grad", grid=(seq // tm,),
        in_specs=[tile, tile],
        out_specs=[pl.BlockSpec((1, 1), lambda i: (0, 0)), tile],
        out_shape=[jax.ShapeDtypeStruct((1, 1), F32), jax.ShapeDtypeStruct((seq, D_MODEL), F32)],
        compiler_params=_params(dimension_semantics=("arbitrary",)),
    )(x_out, target)


def _merge_bwd(dx, y, ba, bb, z, wpo_g, who_g, wout_g, gate, g_post, tm, name):
    seq = dx.shape[0]

    def body(dx_ref, y_ref, ba_ref, bb_ref, mgp_ref, mgh_ref, wpo_ref, who_ref, wout_ref, gate_ref, gp_ref,
             dy_ref, dba_ref, dbb_ref, da_ref, db_ref, dmg_ref, acc_ref):
        @pl.when(pl.program_id(0) == 0)
        def _():
            acc_ref[...] = jnp.zeros_like(acc_ref)

        dxv = dx_ref[...]
        rs, yn = _rms_parts(y_ref[...])
        acc_ref[0:1, :] += jnp.sum(dxv * yn * gp_ref[...], axis=0, keepdims=True)
        acc_ref[1:2, :] += jnp.sum(dxv * gate_ref[...] * yn, axis=0, keepdims=True)
        dyn = dxv * (gate_ref[...] * gp_ref[...])
        dy = rs * (dyn - yn * jnp.mean(dyn * yn, axis=-1, keepdims=True))
        dmerged = _dot_nt(dy, wout_ref[...])
        sp, sh = _sigmoid(mgp_ref[...]), _sigmoid(mgh_ref[...])
        dba, dbb = sp * dmerged, sh * dmerged
        dmg_ref[:, 0:D_MODEL] = (dmerged * ba_ref[...].astype(F32) * sp * (1.0 - sp)).astype(dmg_ref.dtype)
        dmg_ref[:, D_MODEL:2 * D_MODEL] = (dmerged * bb_ref[...].astype(F32) * sh * (1.0 - sh)).astype(dmg_ref.dtype)
        da = _dot_nt(dba[:, 0:GROUP_DIM], wpo_ref[0])
        for j in range(1, N_DEV):
            da += _dot_nt(dba[:, j * GROUP_DIM:(j + 1) * GROUP_DIM], wpo_ref[j])
        dy_ref[...] = dy.astype(dy_ref.dtype)
        dba_ref[...] = dba.astype(dba_ref.dtype)
        dbb_ref[...] = dbb.astype(dbb_ref.dtype)
        da_ref[...] = da
        db_ref[...] = _dot_nt(dbb, who_ref[...])

    def tile(cols=D_MODEL, block=0):
        return pl.BlockSpec((tm, cols), lambda i: (i, block))

    full = pl.BlockSpec((D_MODEL, D_MODEL), lambda i: (0, 0))
    act = jax.ShapeDtypeStruct((seq, D_MODEL), MXU_DTYPE)
    return pl.pallas_call(
        body, name=name, grid=(seq // tm,),
        in_specs=[tile(), tile(), tile(), tile(), tile(block=COL_MGP), tile(block=COL_MGH),
                  pl.BlockSpec((N_DEV, POOL_WIDTH, GROUP_DIM), lambda i: (0, 0, 0)),
                  full, full, _row_spec(), _row_spec()],
        out_specs=[tile(), tile(), tile(), tile(POOL_WIDTH), tile(), tile(2 * D_MODEL),
                   pl.BlockSpec((8, D_MODEL), lambda i: (0, 0))],
        out_shape=[act, act, act, jax.ShapeDtypeStruct((seq, POOL_WIDTH), F32),
                   jax.ShapeDtypeStruct((seq, D_MODEL), F32),
                   jax.ShapeDtypeStruct((seq, 2 * D_MODEL), MXU_DTYPE),
                   jax.ShapeDtypeStruct((8, D_MODEL), F32)],
        compiler_params=_params(dimension_semantics=("arbitrary",)),
    )(dx, y, ba, bb, z, z, wpo_g, who_g, wout_g, gate, g_post)


def _grad_tn(a, b, tn, dev_major, name):
    seq, ka = a.shape
    n = b.shape[1]

    def body(a_ref, b_ref, out_ref):
        out_ref[...] = _dot_tn(a_ref[...], b_ref[...]).astype(out_ref.dtype)

    if dev_major:
        out_spec = pl.BlockSpec((None, ka, tn), lambda j: (j, 0, 0))
        out_shape = jax.ShapeDtypeStruct((n // tn, ka, tn), WIRE_DTYPE)
    else:
        out_spec = pl.BlockSpec((ka, tn), lambda j: (0, j))
        out_shape = jax.ShapeDtypeStruct((ka, n), WIRE_DTYPE)
    return pl.pallas_call(
        body, name=name, grid=(n // tn,),
        in_specs=[pl.BlockSpec((seq, ka), lambda j: (0, 0)), pl.BlockSpec((seq, tn), lambda j: (0, j))],
        out_specs=out_spec, out_shape=out_shape,
        compiler_params=_params(dimension_semantics=("parallel",)),
    )(a, b)


def _hgrn_bwd(db_in, z, o, states, lb_l, gn_l, name, after=None):
    seq = z.shape[0]
    n_chunks = seq // CHUNK

    def body(db_ref, hq_ref, hf_ref, hi_ref, hg_ref, o_ref, st_ref, lb_ref, gn_ref,
             dz_ref, dlb_ref, dgn_ref, dstate, dq_buf, dk_buf):
        @pl.when(pl.program_id(0) == 0)
        def _():
            dstate[...] = jnp.zeros_like(dstate)
            dlb_ref[...] = jnp.zeros_like(dlb_ref)
            dgn_ref[...] = jnp.zeros_like(dgn_ref)

        causal, tri, before_sub = _chunk_masks()
        lb = lb_ref[...]
        sg, f, logf = _gates(hf_ref[...], lb)
        kk = 1.0 - f
        hq = hq_ref[...]
        sq = _sigmoid(hq)
        q = hq * sq
        cum = jnp.dot(tri, logf, precision=HIGHEST, preferred_element_type=F32)
        base = jnp.dot(before_sub, logf, precision=HIGHEST, preferred_element_type=F32)
        gn = gn_ref[...]
        dgn = jnp.zeros((1, HEAD_DIM), F32)
        dlast = []
        for h in range(HEADS):
            sl = slice(h * HEAD_DIM, (h + 1) * HEAD_DIM)
            q_h, k_h, cum_h = q[:, sl], kk[:, sl], cum[:, sl]
            v_h = hi_ref[:, sl]
            st_h = st_ref[0, h]
            dst_h = dstate[h]
            rs, ohat = _rms_parts(o_ref[:, sl])
            hg = hg_ref[:, sl]
            shg = _sigmoid(hg)
            d_bin = db_ref[:, sl]
            don = d_bin * (hg * shg)
            dgn += jnp.sum(don * ohat, axis=0, keepdims=True)
            dohat = don * gn
            do = rs * (dohat - ohat * jnp.mean(dohat * ohat, axis=-1, keepdims=True))
            dz_ref[:, 3 * D_MODEL + h * HEAD_DIM:3 * D_MODEL + (h + 1) * HEAD_DIM] = (
                d_bin * (ohat * gn) * _dsilu(hg, shg)).astype(dz_ref.dtype)
            last = jnp.sum(logf[:, sl], axis=0, keepdims=True)
            g_in = jnp.exp(cum_h)
            d_out = jnp.exp(last - cum_h)
            q_bar, k_bar = q_h * g_in, k_h * d_out
            blocks = _intra_blocks(q_h, k_h, cum_h, base[:, sl], causal)
            a = jnp.concatenate([b[4] for b in blocks], axis=0)
            da = jnp.where(causal, _dot_nt(do, v_h), 0.0)
            dv = _dot_tn(a, do) + _dot_nt(k_bar, dst_h)
            dq_parts = []
            dk_bar = _dot(v_h, dst_h)
            dk = dk_bar * d_out
            dlast.append(jnp.sum(k_bar * dk_bar, axis=0, keepdims=True)
                         + jnp.exp(last) * jnp.sum(st_h * dst_h, axis=0, keepdims=True))
            for i, (q_t, k_t, e_q, e_k, _) in enumerate(blocks):
                da_i = da[i * SUB:(i + 1) * SUB]
                dq_parts.append(jnp.dot(da_i, k_t, precision=HIGHEST, preferred_element_type=F32) * e_q)
                dk += lax.dot_general(da_i, q_t, (((0,), (0,)), ((), ())), precision=HIGHEST,
                                      preferred_element_type=F32) * e_k
            dq = _dot(do, st_h) * g_in + jnp.concatenate(dq_parts, axis=0)
            dstate[h] = dst_h * jnp.exp(last) + _dot_tn(do, q_bar)
            dq_buf[:, sl] = dq
            dk_buf[:, sl] = dk
            dz_ref[:, 2 * D_MODEL + h * HEAD_DIM:2 * D_MODEL + (h + 1) * HEAD_DIM] = dv.astype(dz_ref.dtype)
        dgn_ref[...] += dgn
        dq_all, dk_all = dq_buf[...], dk_buf[...]
        dg = q * dq_all - kk * dk_all
        dlogf = lax.dot_general(tri, dg, (((0,), (0,)), ((), ())), precision=HIGHEST,
                                preferred_element_type=F32) + jnp.concatenate(dlast, axis=1)
        df = jnp.where(f > LOG_FLOOR, dlogf / f, 0.0) - dk_all
        dlb_ref[...] += jnp.sum(df * (1.0 - sg), axis=0, keepdims=True)
        dz_ref[:, 0:D_MODEL] = (dq_all * _dsilu(hq, sq)).astype(dz_ref.dtype)
        dz_ref[:, D_MODEL:2 * D_MODEL] = (df * (1.0 - lb) * sg * (1.0 - sg)).astype(dz_ref.dtype)

    last_chunk = n_chunks - 1

    def col(block):
        return pl.BlockSpec((CHUNK, D_MODEL), lambda c: (last_chunk - c, block))

    return _pallas_after(
        body, 9, after, name=name, grid=(n_chunks,),
        in_specs=[col(0), col(COL_HQ), col(COL_HF), col(COL_HI), col(COL_HG), col(0),
                  pl.BlockSpec((1, HEADS, HEAD_DIM, HEAD_DIM), lambda c: (last_chunk - c, 0, 0, 0)),
                  _row_spec(), _row_spec(HEAD_DIM)],
        out_specs=[pl.BlockSpec((CHUNK, 4 * D_MODEL), lambda c: (last_chunk - c, 0)),
                   _row_spec(), _row_spec(HEAD_DIM)],
        out_shape=[jax.ShapeDtypeStruct((seq, 4 * D_MODEL), MXU_DTYPE),
                   jax.ShapeDtypeStruct((1, D_MODEL), F32), jax.ShapeDtypeStruct((1, HEAD_DIM), F32)],
        scratch_shapes=[pltpu.VMEM((HEADS, HEAD_DIM, HEAD_DIM), F32),
                        pltpu.VMEM((CHUNK, D_MODEL), F32), pltpu.VMEM((CHUNK, D_MODEL), F32)],
        compiler_params=_params(dimension_semantics=("arbitrary",)),
    )(db_in, z, z, z, z, o, states, lb_l, gn_l)


def _pool_bwd(da_in, z, pool_w_l, pool_scale_l, name, after=None):
    seq = z.shape[0]

    def body(da_ref, pv_ref, pg_ref, w_ref, sc_ref, dpv_ref, dpg_ref, dw_ref, dsc_ref):
        g = pl.program_id(0)
        pos = lax.broadcasted_iota(jnp.int32, (seq, GROUP_DIM), 0)
        pm, count = _pool_mean_minus_token(pv_ref[...], g, pos)
        lin0 = _dot(pm, w_ref[...])
        pg = pg_ref[...]
        spg = _sigmoid(pg)
        da = da_ref[...]
        dlin = da * (pg * spg)
        dpg_ref[...] = (da * (lin0 * sc_ref[...]) * _dsilu(pg, spg)).astype(dpg_ref.dtype)
        dsc_ref[...] = jnp.sum(dlin * lin0, axis=0, keepdims=True)
        dl0 = dlin * sc_ref[...]
        dw_ref[...] = _dot_tn(pm, dl0)
        dpm = _dot_nt(dl0, w_ref[...])
        sums, acc = [], dpm / count
        for j in (1, 2, 4, 8):
            acc = acc + _shift_up(acc, j, pos, seq)
            sums.append(acc)
        dpv_ref[...] = (_select_window(g, sums) - dpm).astype(dpv_ref.dtype)

    grp = pl.BlockSpec((seq, GROUP_DIM), lambda g: (0, g))
    return _pallas_after(
        body, 5, after, name=name, grid=(POOL_GROUPS,),
        in_specs=[grp, grp, pl.BlockSpec((seq, GROUP_DIM), lambda g: (0, POOL_GROUPS + g)),
                  pl.BlockSpec((None, GROUP_DIM, GROUP_DIM), lambda g: (g, 0, 0)),
                  pl.BlockSpec((1, GROUP_DIM), lambda g: (0, g))],
        out_specs=[grp, grp, pl.BlockSpec((None, GROUP_DIM, GROUP_DIM), lambda g: (g, 0, 0)),
                   pl.BlockSpec((1, GROUP_DIM), lambda g: (0, g))],
        out_shape=[jax.ShapeDtypeStruct((seq, POOL_WIDTH), MXU_DTYPE),
                   jax.ShapeDtypeStruct((seq, POOL_WIDTH), MXU_DTYPE),
                   jax.ShapeDtypeStruct((POOL_GROUPS, GROUP_DIM, GROUP_DIM), F32),
                   jax.ShapeDtypeStruct((1, POOL_WIDTH), F32)],
        compiler_params=_params(dimension_semantics=("parallel",)),
    )(da_in, z, z, pool_w_l, pool_scale_l)


def _in_proj_dw(h, dz, name, after=None):
    seq = h.shape[0]

    def body(h_ref, dz_ref, out_ref):
        out_ref[...] = lax.dot_general(h_ref[...], dz_ref[...], (((0,), (0,)), ((), ())),
                                       preferred_element_type=F32).astype(out_ref.dtype)

    return _pallas_after(
        body, 2, after, name=name, grid=(N_DEV,),
        in_specs=[pl.BlockSpec((seq, D_MODEL), lambda j: (0, 0)), pl.BlockSpec((seq, IN_COLS), lambda j: (0, j))],
        out_specs=pl.BlockSpec((None, D_MODEL, IN_COLS), lambda j: (j, 0, 0)),
        out_shape=jax.ShapeDtypeStruct((N_DEV, D_MODEL, IN_COLS), WIRE_DTYPE),
        compiler_params=_params(dimension_semantics=("parallel",)),
    )(h, dz)


def _in_proj_dh(dz, win_g, tm, name, after=None):
    seq = dz.shape[0]

    def body(dz_ref, w_ref, dh_ref):
        @pl.when(pl.program_id(1) == 0)
        def _():
            dh_ref[...] = jnp.zeros_like(dh_ref)

        dh_ref[...] += lax.dot_general(dz_ref[...], w_ref[...], (((1,), (1,)), ((), ())),
                                       preferred_element_type=F32)

    return _pallas_after(
        body, 2, after, name=name, grid=(seq // tm, N_DEV),
        in_specs=[pl.BlockSpec((tm, IN_COLS), lambda i, j: (i, j)),
                  pl.BlockSpec((None, D_MODEL, IN_COLS), lambda i, j: (j, 0, 0))],
        out_specs=pl.BlockSpec((tm, D_MODEL), lambda i, j: (i, 0)),
        out_shape=jax.ShapeDtypeStruct((seq, D_MODEL), F32),
        compiler_params=_params(dimension_semantics=("parallel", "arbitrary")),
    )(dz, win_g)


def _prenorm_bwd(x, dh, dx_res, g, scale, tm, name, after=None):
    seq = x.shape[0]

    def body(x_ref, dh_ref, dxr_ref, g_ref, sc_ref, dx_ref, acc_ref):
        @pl.when(pl.program_id(0) == 0)
        def _():
            acc_ref[...] = jnp.zeros_like(acc_ref)

        rs, xn = _rms_parts(x_ref[...])
        dh = dh_ref[...]
        acc_ref[0:1, :] += jnp.sum(dh, axis=0, keepdims=True)
        acc_ref[1:2, :] += jnp.sum(dh * (xn * g_ref[...]), axis=0, keepdims=True)
        dhn = dh * (1.0 + sc_ref[...])
        acc_ref[2:3, :] += jnp.sum(dhn * xn, axis=0, keepdims=True)
        dxn = dhn * g_ref[...]
        dx_ref[...] = rs * (dxn - xn * jnp.mean(dxn * xn, axis=-1, keepdims=True)) + dxr_ref[...]

    tile = pl.BlockSpec((tm, D_MODEL), lambda i: (i, 0))
    return _pallas_after(
        body, 5, after, name=name, grid=(seq // tm,),
        in_specs=[tile, tile, tile, _row_spec(), _row_spec()],
        out_specs=[tile, pl.BlockSpec((8, D_MODEL), lambda i: (0, 0))],
        out_shape=[jax.ShapeDtypeStruct((seq, D_MODEL), F32), jax.ShapeDtypeStruct((8, D_MODEL), F32)],
        compiler_params=_params(dimension_semantics=("arbitrary",)),
    )(x, dh, dx_res, g, scale)


def _adamw_math(w, g, m, v):
    m = ADAM_B1 * m + (1.0 - ADAM_B1) * g
    v = ADAM_B2 * v + (1.0 - ADAM_B2) * (g * g)
    m_hat = m / (1.0 - ADAM_B1 ** ADAM_STEP)
    v_hat = v / (1.0 - ADAM_B2 ** ADAM_STEP)
    delta = -ADAM_LR * (m_hat / (jnp.sqrt(v_hat) + ADAM_EPS) + ADAM_WD * w)
    return delta, m, v


def _adamw_sharded(w, m, v, contrib, tr, name):
    depth, rows, cols = w.shape
    n_parts = contrib.shape[1]

    def body(w_ref, m_ref, v_ref, c_ref, g_ref, d_ref, mo_ref, vo_ref):
        g = c_ref[0].astype(F32)
        for p in range(1, n_parts):
            g += c_ref[p].astype(F32)
        delta, mn, vn = _adamw_math(w_ref[...], g, m_ref[...], v_ref[...])
        g_ref[...] = g
        d_ref[...] = delta
        mo_ref[...] = mn
        vo_ref[...] = vn

    tile = pl.BlockSpec((None, tr, cols), lambda l, i: (l, i, 0))
    shape = jax.ShapeDtypeStruct(w.shape, F32)
    return pl.pallas_call(
        body, name=name, grid=(depth, rows // tr),
        in_specs=[tile, tile, tile, pl.BlockSpec((None, n_parts, tr, cols), lambda l, i: (l, 0, i, 0))],
        out_specs=[tile] * 4, out_shape=[shape] * 4,
        compiler_params=_params(dimension_semantics=("parallel", "parallel")),
    )(w, m, v, contrib)


def _adamw_layer(w, m, v, contribs, l, tr, name, prev=None):
    _, rows, cols = w.shape
    n = len(contribs)

    def body(*refs):
        w_ref, m_ref, v_ref = refs[:3]
        c_refs = refs[3:3 + n]
        g_ref, d_ref, mo_ref, vo_ref = refs[-4:]
        g = c_refs[0][...].astype(F32)
        for c_ref in c_refs[1:]:
            g += c_ref[...].astype(F32)
        delta, mn, vn = _adamw_math(w_ref[...], g, m_ref[...], v_ref[...])
        g_ref[...] = g
        d_ref[...] = delta
        mo_ref[...] = mn
        vo_ref[...] = vn

    tile = pl.BlockSpec((None, tr, cols), lambda i: (l, i, 0))
    in_specs = [tile, tile, tile] + [pl.BlockSpec((None, tr, cols), lambda i, s=slot: (s, i, 0)) for _, slot in contribs]
    operands = [w, m, v] + [arr for arr, _ in contribs]
    aliases = {}
    if prev is not None:
        aliases = {len(operands) + k: k for k in range(4)}
        in_specs += [pl.BlockSpec(memory_space=pl.ANY)] * 4
        operands += list(prev)
    shape = jax.ShapeDtypeStruct(w.shape, F32)
    return pl.pallas_call(
        body, name=name, grid=(rows // tr,), in_specs=in_specs, out_specs=[tile] * 4, out_shape=[shape] * 4,
        input_output_aliases=aliases,
        compiler_params=_params(dimension_semantics=("parallel",)),
    )(*operands)


def _adamw_small(w_pack, m_pack, v_pack, g_late, g_early):
    def body(w_ref, m_ref, v_ref, gl_ref, ge_ref, g_ref, d_ref, mo_ref, vo_ref):
        g_l, g_e = gl_ref[0], ge_ref[0]
        for d in range(1, N_DEV):
            g_l += gl_ref[d]
            g_e += ge_ref[d]
        g = jnp.concatenate([g_l, g_e], axis=0)
        w = w_ref[...]
        r0, r1, r2 = LB_ROW0, LB_ROW0 + 8, LB_ROW0 + 16
        lg0, lg1 = w[r0:r1], w[r1:r2]
        mx = jnp.maximum(lg0, lg1)
        e0, e1 = jnp.exp(lg0 - mx), jnp.exp(lg1 - mx)
        p0, p1 = e0 / (e0 + e1), e1 / (e0 + e1)
        low = ((p0 - p0), (p0 + p1) - p0)
        dlow = [g_rows * jnp.where((lo > 0.0) & (lo < 1.0), 1.0, jnp.where((lo == 0.0) | (lo == 1.0), 0.5, 0.0))
                for g_rows, lo in ((g[r0:r1], low[0]), (g[r1:r2], low[1]))]
        dp0 = (dlow[0] + dlow[1]) - (dlow[0] + dlow[1])
        dp1 = dlow[1]
        inner = p0 * dp0 + p1 * dp1
        g = jnp.concatenate([g[:r0], p0 * (dp0 - inner), p1 * (dp1 - inner), g[r2:]], axis=0)
        delta, mn, vn = _adamw_math(w, g, m_ref[...], v_ref[...])
        g_ref[...] = g
        d_ref[...] = delta
        mo_ref[...] = mn
        vo_ref[...] = vn

    shape = jax.ShapeDtypeStruct(w_pack.shape, F32)
    return pl.pallas_call(body, name="adamw_small", out_shape=[shape] * 4, compiler_params=_params())(
        w_pack, m_pack, v_pack, g_late, g_early)


def _pack_small(parts, first=0, last=len(_SMALL_ROWS)):
    rows = [(parts[name] if l is None else parts[name][l]).reshape(n, 128) for name, l, n in _SMALL_ROWS[first:last]]
    if last == len(_SMALL_ROWS):
        rows.append(jnp.zeros((SMALL_ROWS_PAD - sum(n for _, _, n in _SMALL_ROWS), 128), F32))
    return jnp.concatenate(rows, axis=0)


def _unpack_small(pack, shapes):
    pieces, r = {}, 0
    for name, l, n in _SMALL_ROWS:
        pieces.setdefault(name, []).append(pack[r:r + n])
        r += n
    return {name: jnp.concatenate(p, axis=0).reshape(shapes[name]) for name, p in pieces.items()}


def kernel(x, c, w_ada, b_ada, g_pre, g_post, w_in, pool_w, pool_scale, lb_logits, hgrn_norm_g, w_pool_o, w_hgrn_o, w_out, loss_target, m_w_ada, m_b_ada, m_g_pre, m_g_post, m_w_in, m_pool_w, m_pool_scale, m_lb_logits, m_hgrn_norm_g, m_w_pool_o, m_w_hgrn_o, m_w_out, v_w_ada, v_b_ada, v_g_pre, v_g_post, v_w_in, v_pool_w, v_pool_scale, v_lb_logits, v_hgrn_norm_g, v_w_pool_o, v_w_hgrn_o, v_w_out):
    seq = x.shape[1]
    tm = min(512, seq)
    tm_merge = min(256, seq)
    pos = _my_position()
    me = pos[3]

    c_all = _allgather_small(c, "allgather_c").reshape(N_DEV, D_MODEL)
    b_cols = lax.dynamic_slice_in_dim(b_ada, me * ADA_COLS, ADA_COLS, axis=1)
    ada_part = _ada_fwd(c_all, w_ada, b_cols)
    ada_all = _allgather_small(ada_part.reshape(DEPTH * N_DEV, ADA_COLS), "allgather_ada")
    ada = lax.dynamic_index_in_dim(ada_all.reshape(N_DEV, DEPTH, N_DEV, ADA_COLS), me, axis=2, keepdims=False)
    ada = jnp.transpose(ada, (1, 0, 2)).reshape(DEPTH, 3 * D_MODEL)
    shift = [ada[l:l + 1, 0:D_MODEL] for l in range(DEPTH)]
    scale = [ada[l:l + 1, D_MODEL:2 * D_MODEL] for l in range(DEPTH)]
    gate = [ada[l:l + 1, 2 * D_MODEL:] for l in range(DEPTH)]

    big = dict(win=w_in, wpo=w_pool_o, who=w_hgrn_o, wout=w_out)
    units = [["win0"], ["wpo0", "who0", "wout0"], ["win1", "wpo1", "who1", "wout1"]]
    g_streams = [_gather_streams(keys) for keys in units]
    g_state = [None] * len(units)

    def gather_start(u, after):
        bufs = {}
        for k in units[u]:
            arr = big[k[:-1]]
            bufs["s_" + k] = arr[int(k[-1])].astype(WIRE_DTYPE)
            bufs["g_" + k] = _with_own_slot(bufs["s_" + k], me)
        bufs, sems, token = _comm_call(f"gather_start_{u}", bufs, start=list(g_streams[u][:2]), after=after)
        g_state[u] = dict(bufs=bufs, sems=sems)
        return token

    def gather_pass(u, after):
        st = g_state[u]
        to_chips, _, pass_on = g_streams[u]
        st["bufs"], (st["pass_sems"],), _ = _comm_call(f"gather_pass_{u}", st["bufs"], start=[pass_on],
                                                       wait=[(to_chips, st["sems"][0])], after=after)

    def gather_done(u, after=None):
        st = g_state[u]
        _, to_sibling, pass_on = g_streams[u]
        bufs, _, _ = _comm_call(f"gather_done_{u}", st["bufs"], after=after,
                                wait=[(to_sibling, st["sems"][1]), (pass_on, st["pass_sems"])])
        return {k: bufs["g_" + k] for k in units[u]}

    token = gather_start(0, ada_all)

    lb = _lb_fwd(lb_logits)

    gw = {}
    xs, saved = [x[0]], []
    for l in range(DEPTH):
        h = _prenorm_fwd(xs[l], g_pre[l:l + 1], shift[l], scale[l], tm, f"prenorm_fwd_{l}",
                         after=token if l == 0 else None)
        token = None
        if l == 0:
            gather_pass(0, h)
            gw.update(gather_done(0))
            token = gather_start(1, gw["win0"])
        else:
            gather_pass(2, h)
            gw.update(gather_done(2))
        z = _in_proj(h, gw[f"win{l}"], tm, f"in_proj_{l}", after=token)
        if l == 0:
            gather_pass(1, z)
            token = gather_start(2, g_state[1]["bufs"]["g_wpo0"])
        a_in = _pool_fwd(z, pool_w[l], pool_scale[l:l + 1], f"pool_fwd_{l}", after=token)
        o, b_in, states = _hgrn_fwd(z, lb[l:l + 1], hgrn_norm_g[l:l + 1], f"hgrn_fwd_{l}", after=token)
        if l == 0:
            gw.update(gather_done(1, b_in))
        who_l = gw[f"who{l}"].reshape(D_MODEL, D_MODEL)
        wout_l = gw[f"wout{l}"].reshape(D_MODEL, D_MODEL)
        ba, bb, merged, y, x_next = _merge_fwd(a_in, b_in, z, xs[l], gw[f"wpo{l}"], who_l, wout_l, gate[l],
                                               g_post[l:l + 1], tm_merge, f"merge_fwd_{l}")
        xs.append(x_next)
        saved.append((h, z, a_in, o, b_in, states, ba, bb, merged, y, who_l, wout_l))

    loss_part, dx = _loss_grad(xs[DEPTH], loss_target[0], tm)
    loss = lax.psum(loss_part[0, 0], ("x", "y", "c"))

    chips = _other_chips(pos)
    pair_idx = jnp.stack([_dev_index(cx, cy, pos[2]) for cx, cy in chips] + [me]).astype(jnp.int32)
    pair_rows = dict(win=256, wpo=POOL_WIDTH, who=HEAD_DIM, wout=HEAD_DIM)

    def scatter_pair_start(u, grads):
        keys = list(grads)
        pair, to_chips = _scatter_streams(keys)
        bufs = {}
        for k in keys:
            bufs["g_" + k] = grads[k]
            bufs["st_" + k] = lax.empty((4,) + grads[k].shape[1:], WIRE_DTYPE)
        bufs, (sems,), token = _comm_call(f"scatter_pair_start_{u}", bufs, start=[pair])
        return dict(u=u, keys=keys, pair=pair, to_chips=to_chips, bufs=bufs, sems=sems, token=token)

    def scatter_chips_start(st, after):
        u, keys = st["u"], st["keys"]
        bufs, _, _ = _comm_call(f"scatter_pair_done_{u}", st["bufs"], wait=[(st["pair"], st["sems"])], after=after)
        bufs2 = {}
        for k in keys:
            bufs2["ps_" + k] = _pair_sum(bufs["g_" + k], bufs["st_" + k], pair_idx, bufs["g_" + k].shape[1],
                                         f"pair_sum_{k}")
            bufs2["ld_" + k] = lax.empty((3,) + bufs["g_" + k].shape[1:], WIRE_DTYPE)
        bufs2, (sems,), token = _comm_call(f"scatter_chips_start_{u}", bufs2, start=[st["to_chips"]])
        st.update(bufs=bufs2, sems=sems, token=token)

    def scatter_finish(st, after):
        bufs, _, _ = _comm_call(f"scatter_chips_done_{st['u']}", st["bufs"], wait=[(st["to_chips"], st["sems"])],
                                after=after)
        return {k: [(bufs["ps_" + k], 3), (bufs["ld_" + k], 0), (bufs["ld_" + k], 1), (bufs["ld_" + k], 2)]
                for k in st["keys"]}

    d_ada, small, scat = [None] * DEPTH, [None] * DEPTH, {}
    for l in reversed(range(DEPTH)):
        h, z, a_in, o, b_in, states, ba, bb, merged, y, who_l, wout_l = saved[l]
        dy, dba, dbb, da_in, db_in, dmg, acc_post = _merge_bwd(
            dx, y, ba, bb, z, gw[f"wpo{l}"], who_l, wout_l, gate[l], g_post[l:l + 1], tm_merge, f"merge_bwd_{l}")
        g_small = {
            f"wout{l}": _grad_tn(merged, dy, 512, False, f"grad_w_out_{l}").reshape(N_DEV, HEAD_DIM, D_MODEL),
            f"who{l}": _grad_tn(b_in, dbb, 512, False, f"grad_w_hgrn_o_{l}").reshape(N_DEV, HEAD_DIM, D_MODEL),
            f"wpo{l}": _grad_tn(a_in, dba, GROUP_DIM, True, f"grad_w_pool_o_{l}")}
        st_small = scat[f"small{l}"] = scatter_pair_start(f"small{l}", g_small)
        dzh, dlb, dgn = _hgrn_bwd(db_in, z, o, states, lb[l:l + 1], hgrn_norm_g[l:l + 1], f"hgrn_bwd_{l}",
                                  after=st_small["token"])
        scatter_chips_start(st_small, dzh)
        dpv, dpg, dpw, dps = _pool_bwd(da_in, z, pool_w[l], pool_scale[l:l + 1], f"pool_bwd_{l}",
                                       after=st_small["token"])
        dz = jnp.concatenate([dpv, dpg, dzh, dmg], axis=1)
        small[l] = dict(g_post=acc_post[1], pool_w=dpw, pool_scale=dps[0], lb_logits=dlb[0], hgrn_norm_g=dgn[0])
        token = None
        if l == 0:
            parts = {name: jnp.stack([small[0][name], small[1][name]]) for name in small[0]}
            parts.update(b_ada=[None, d_ada[1]], g_pre=[None, small[1]["g_pre"]])
            sg_stream = _direct_gather_stream("sg")
            early = _pack_small(parts, 2)
            sg_bufs, (sg_sems,), token = _comm_call(
                "small_grads_start", dict(s_sg=early, g_sg=_with_own_slot(early, me)), start=[sg_stream])
        st_win = scat[f"win{l}"] = scatter_pair_start(
            f"win{l}", {f"win{l}": _in_proj_dw(h, dz, f"grad_w_in_{l}", after=token)})
        dh = _in_proj_dh(dz, gw[f"win{l}"], tm, f"in_proj_dh_{l}", after=st_win["token"])
        scatter_chips_start(st_win, dh)
        dx, acc_pre = _prenorm_bwd(xs[l], dh, dx, g_pre[l:l + 1], scale[l], tm, f"prenorm_bwd_{l}",
                                   after=st_win["token"])
        d_ada[l] = jnp.concatenate([acc_pre[0], acc_pre[1], acc_post[0]])
        small[l]["g_pre"] = acc_pre[2]
    grad_x = dx[None]

    moments = dict(win=(m_w_in, v_w_in), wpo=(m_w_pool_o, v_w_pool_o), who=(m_w_hgrn_o, v_w_hgrn_o),
                   wout=(m_w_out, v_w_out))
    big_out = {}

    def finish_unit(unit, after):
        for k, contribs in scatter_finish(scat[unit], after).items():
            wname, l = k[:-1], int(k[-1])
            big_out[wname] = _adamw_layer(big[wname], moments[wname][0], moments[wname][1], contribs, l,
                                          pair_rows[wname], f"adamw_{k}", prev=big_out.get(wname))
            after = big_out[wname][0]
        return after

    after = dx
    for unit in ("small1", "win1", "small0"):
        after = finish_unit(unit, after)

    parts = dict(b_ada=[d_ada[0]], g_pre=[small[0]["g_pre"]])
    g_late = _allgather_small(_pack_small(parts, 0, 2), "allgather_late_grads", after=after)
    sg_bufs, _, _ = _comm_call("small_grads_done", sg_bufs, wait=[(sg_stream, sg_sems)], after=g_late)
    g_early = sg_bufs["g_sg"]
    small_names = list(dict.fromkeys(name for name, _, _ in _SMALL_ROWS))
    weights = dict(b_ada=b_ada, g_pre=g_pre, g_post=g_post, pool_w=pool_w, pool_scale=pool_scale,
                   lb_logits=lb_logits, hgrn_norm_g=hgrn_norm_g)
    m_small = dict(b_ada=m_b_ada, g_pre=m_g_pre, g_post=m_g_post, pool_w=m_pool_w, pool_scale=m_pool_scale,
                   lb_logits=m_lb_logits, hgrn_norm_g=m_hgrn_norm_g)
    v_small = dict(b_ada=v_b_ada, g_pre=v_g_pre, g_post=v_g_post, pool_w=v_pool_w, pool_scale=v_pool_scale,
                   lb_logits=v_lb_logits, hgrn_norm_g=v_hgrn_norm_g)
    shapes = {name: weights[name].shape for name in small_names}
    small_out = [_unpack_small(p, shapes) for p in
                 _adamw_small(_pack_small(weights), _pack_small(m_small), _pack_small(v_small), g_late, g_early)]

    d_ada_all = jnp.stack([g_late[:, 0:24, :].reshape(N_DEV, 3 * D_MODEL),
                           g_early[:, 0:24, :].reshape(N_DEV, 3 * D_MODEL)], axis=1)
    d_cols = jnp.transpose(lax.dynamic_slice_in_dim(d_ada_all, me * ADA_COLS, ADA_COLS, axis=2), (1, 0, 2))
    g_w_ada = _ada_bwd(c_all, d_cols)
    ada_out = _adamw_sharded(w_ada, m_w_ada, v_w_ada, g_w_ada[:, None], 256, "adamw_w_ada")
    finish_unit("win0", ada_out[0])

    def leaf(kind):
        s = small_out[kind]
        return (ada_out[kind], s["b_ada"], s["g_pre"], s["g_post"], big_out["win"][kind], s["pool_w"], s["pool_scale"],
                s["lb_logits"], s["hgrn_norm_g"], big_out["wpo"][kind], big_out["who"][kind], big_out["wout"][kind])

    return (loss, grad_x) + leaf(0) + leaf(1) + leaf(2) + leaf(3)
```

```python
import jax
import jax.numpy as jnp
from jax import lax
from jax.experimental import pallas as pl
from jax.experimental.pallas import tpu as pltpu

F32 = jnp.float32
MXU_DTYPE = jnp.bfloat16
WIRE_DTYPE = jnp.bfloat16

N_DEV = 8
DEPTH = 2
D_MODEL = 1024
HEADS = 8
HEAD_DIM = 128
POOL_GROUPS = 4
GROUP_DIM = 128
POOL_WIDTH = POOL_GROUPS * GROUP_DIM
IN_WIDTH = 7168
CHUNK = 64
SUB = 16
N_SUB = CHUNK // SUB
EXP_CLAMP = 80.0
NORM_EPS = 1e-6
LOG_FLOOR = 1e-30
ADA_COLS = 3 * D_MODEL // N_DEV
IN_COLS = IN_WIDTH // N_DEV
COL_HQ, COL_HF, COL_HI, COL_HG, COL_MGP, COL_MGH = 1, 2, 3, 4, 5, 6

ADAM_LR = 0.001
ADAM_B1 = 0.9
ADAM_B2 = 0.999
ADAM_EPS = 1e-08
ADAM_WD = 0.01
ADAM_STEP = 10

VMEM_LIMIT = 48 * 1024 * 1024
MESH_ID = pl.DeviceIdType.MESH
HIGHEST = lax.Precision.HIGHEST

_SMALL_ROWS = (("b_ada", 0, 24), ("g_pre", 0, 8), ("b_ada", 1, 24), ("g_pre", 1, 8), ("g_post", None, 16),
               ("pool_w", None, 1024), ("pool_scale", None, 8), ("lb_logits", None, 16), ("hgrn_norm_g", None, 2))
SMALL_LATE_ROWS = 32
SMALL_ROWS_PAD = 1136
LB_ROW0 = 32 + 32 + 16 + 1024 + 8


def _params(**kw):
    return pltpu.CompilerParams(vmem_limit_bytes=VMEM_LIMIT, **kw)


def _sigmoid(v):
    return 1.0 / (1.0 + jnp.exp(-v))


def _dsilu(v, s):
    return s * (1.0 + v * (1.0 - s))


def _dot(a, b):
    return jnp.dot(a.astype(MXU_DTYPE), b.astype(MXU_DTYPE), preferred_element_type=F32)


def _dot_nt(a, b):
    return lax.dot_general(a.astype(MXU_DTYPE), b.astype(MXU_DTYPE), (((1,), (1,)), ((), ())),
                           preferred_element_type=F32)


def _dot_tn(a, b):
    return lax.dot_general(a.astype(MXU_DTYPE), b.astype(MXU_DTYPE), (((0,), (0,)), ((), ())),
                           preferred_element_type=F32)


def _pallas_after(body, n_in, after, *, in_specs, **kw):
    if after is None:
        return pl.pallas_call(body, in_specs=in_specs, **kw)

    def tied(*refs):
        body(*refs[:n_in], *refs[n_in + 1:])

    call = pl.pallas_call(tied, in_specs=list(in_specs) + [pl.BlockSpec(memory_space=pl.ANY)], **kw)
    return lambda *operands: call(*operands, after)


def _my_position():
    mx, my, mc = lax.axis_index("x"), lax.axis_index("y"), lax.axis_index("c")
    return mx, my, mc, 4 * mx + 2 * my + mc


def _peer(mx, my, mc, k):
    px = 1 - mx if (k >> 2) & 1 else mx
    py = 1 - my if (k >> 1) & 1 else my
    pc = 1 - mc if k & 1 else mc
    return (px, py, pc), 4 * px + 2 * py + pc


def _allgather_small(v, name, after=None):
    rows, cols = v.shape

    def body(v_ref, out_ref, send_sems, recv_sems):
        mx, my, mc, me = _my_position()
        out_ref[me] = v_ref[...]
        copies = []
        for k in range(1, N_DEV):
            peer, _ = _peer(mx, my, mc, k)
            cp = pltpu.make_async_remote_copy(
                src_ref=v_ref, dst_ref=out_ref.at[me],
                send_sem=send_sems.at[k - 1], recv_sem=recv_sems.at[k - 1],
                device_id=peer, device_id_type=MESH_ID)
            cp.start()
            copies.append(cp)
        for cp in copies:
            cp.wait()

    return _pallas_after(
        body, 1, after, name=name,
        out_shape=jax.ShapeDtypeStruct((N_DEV, rows, cols), v.dtype),
        in_specs=[pl.BlockSpec(memory_space=pltpu.VMEM)],
        out_specs=pl.BlockSpec(memory_space=pltpu.VMEM),
        scratch_shapes=[pltpu.SemaphoreType.DMA((N_DEV - 1,)), pltpu.SemaphoreType.DMA((N_DEV - 1,))],
        compiler_params=_params(),
    )(v)


class _Stream:
    def __init__(self, n, plan):
        self.n, self.plan = n, plan


def _comm_call(name, bufs, start=(), wait=(), after=None):
    names = list(bufs)

    def body(*refs):
        it = iter(refs)
        buf_refs = {n: next(it) for n in names}
        wait_sems = [(next(it), next(it)) for _ in wait]
        if after is not None:
            next(it)
        start_sems = [(next(it), next(it)) for _ in start]
        for _ in names:
            next(it)
        token = next(it)
        pos = _my_position()

        def descriptors(stream, sems):
            return [pltpu.make_async_remote_copy(src_ref=src, dst_ref=dst, send_sem=sems[0].at[k], recv_sem=sems[1].at[k],
                                                 device_id=dev, device_id_type=MESH_ID)
                    for k, (src, dst, dev) in enumerate(stream.plan(buf_refs, pos))]

        for (stream, _), sems in zip(wait, wait_sems):
            for cp in descriptors(stream, sems):
                cp.wait_send()
                cp.wait_recv()
        for stream, sems in zip(start, start_sems):
            for cp in descriptors(stream, sems):
                cp.start()
        token[...] = jnp.zeros_like(token)

    hbm = pl.BlockSpec(memory_space=pltpu.HBM)
    sem = pl.BlockSpec(memory_space=pltpu.SEMAPHORE)
    operands = [pltpu.with_memory_space_constraint(bufs[n], pltpu.HBM) for n in names]
    in_specs = [hbm] * len(names)
    for _, (send_sems, recv_sems) in wait:
        operands += [send_sems, recv_sems]
        in_specs += [sem, sem]
    if after is not None:
        operands.append(after)
        in_specs.append(pl.BlockSpec(memory_space=pl.ANY))
    out_shape, out_specs = [], []
    for stream in start:
        out_shape += [pltpu.SemaphoreType.DMA((stream.n,)), pltpu.SemaphoreType.DMA((stream.n,))]
        out_specs += [sem, sem]
    n_sem_out = len(out_shape)
    out_shape += [pltpu.HBM(bufs[n].shape, bufs[n].dtype) for n in names]
    out_specs += [hbm] * len(names)
    out_shape.append(jax.ShapeDtypeStruct((8, 128), F32))
    out_specs.append(pl.BlockSpec(memory_space=pltpu.VMEM))
    outs = pl.pallas_call(
        body, name=name, out_shape=out_shape, in_specs=in_specs, out_specs=out_specs,
        input_output_aliases={i: n_sem_out + i for i in range(len(names))},
        compiler_params=pltpu.CompilerParams(has_side_effects=pltpu.SideEffectType.DATAFLOW_SIDE_EFFECTING),
    )(*operands)
    sems = [(outs[2 * i], outs[2 * i + 1]) for i in range(len(start))]
    return dict(zip(names, outs[n_sem_out:n_sem_out + len(names)])), sems, outs[-1]


def _with_own_slot(block, me):
    return lax.dynamic_update_index_in_dim(lax.empty((N_DEV,) + block.shape, block.dtype), block, me, 0)


def _other_chips(pos):
    mx, my, _, _ = pos
    return [(1 - mx if i & 2 else mx, 1 - my if i & 1 else my) for i in (1, 2, 3)]


def _dev_index(px, py, pc):
    return 4 * px + 2 * py + pc


def _gather_streams(keys):
    def to_chips(refs, pos):
        _, _, mc, me = pos
        return [(refs["s_" + k], refs["g_" + k].at[me], (cx, cy, mc)) for k in keys for cx, cy in _other_chips(pos)]

    def to_sibling(refs, pos):
        mx, my, mc, me = pos
        return [(refs["s_" + k], refs["g_" + k].at[me], (mx, my, 1 - mc)) for k in keys]

    def pass_on(refs, pos):
        mx, my, mc, _ = pos
        out = []
        for k in keys:
            for cx, cy in _other_chips(pos):
                slot = refs["g_" + k].at[_dev_index(cx, cy, mc)]
                out.append((slot, slot, (mx, my, 1 - mc)))
        return out

    return _Stream(3 * len(keys), to_chips), _Stream(len(keys), to_sibling), _Stream(3 * len(keys), pass_on)


def _direct_gather_stream(key):
    def plan(refs, pos):
        mx, my, mc, me = pos
        return [(refs["s_" + key], refs["g_" + key].at[me], _peer(mx, my, mc, k)[0]) for k in range(1, N_DEV)]

    return _Stream(N_DEV - 1, plan)


def _scatter_streams(keys):
    def pair(refs, pos):
        mx, my, mc, _ = pos
        sib = (mx, my, 1 - mc)
        out = []
        for k in keys:
            for i, (cx, cy) in enumerate(_other_chips(pos)):
                out.append((refs["g_" + k].at[_dev_index(cx, cy, 1 - mc)], refs["st_" + k].at[i], sib))
            out.append((refs["g_" + k].at[_dev_index(mx, my, 1 - mc)], refs["st_" + k].at[3], sib))
        return out

    def chips(refs, pos):
        mc = pos[2]
        return [(refs["ps_" + k].at[i], refs["ld_" + k].at[i], (cx, cy, mc))
                for k in keys for i, (cx, cy) in enumerate(_other_chips(pos))]

    return _Stream(4 * len(keys), pair), _Stream(3 * len(keys), chips)


def _pair_sum(g, st, idx, tr, name):
    _, rows, cols = g.shape

    def body(idx_ref, g_ref, st_ref, out_ref):
        out_ref[...] = (g_ref[...].astype(F32) + st_ref[...].astype(F32)).astype(out_ref.dtype)

    return pl.pallas_call(
        body, name=name,
        grid_spec=pltpu.PrefetchScalarGridSpec(
            num_scalar_prefetch=1, grid=(4, rows // tr),
            in_specs=[pl.BlockSpec((None, tr, cols), lambda j, i, idx_ref: (idx_ref[j], i, 0)),
                      pl.BlockSpec((None, tr, cols), lambda j, i, idx_ref: (j, i, 0))],
            out_specs=pl.BlockSpec((None, tr, cols), lambda j, i, idx_ref: (j, i, 0))),
        out_shape=jax.ShapeDtypeStruct((4, rows, cols), WIRE_DTYPE),
        compiler_params=_params(dimension_semantics=("parallel", "parallel")),
    )(idx, g, st)


def _ada_fwd(c_all, w_ada, b_cols):
    def body(c_ref, w_ref, b_ref, out_ref):
        cv = c_ref[...]
        ca = cv * _sigmoid(cv)
        for l in range(DEPTH):
            out_ref[l] = jnp.dot(ca, w_ref[l], precision=HIGHEST, preferred_element_type=F32) + b_ref[l:l + 1, :]

    return pl.pallas_call(
        body, name="ada_fwd",
        out_shape=jax.ShapeDtypeStruct((DEPTH, N_DEV, ADA_COLS), F32),
        compiler_params=_params(),
    )(c_all, w_ada, b_cols)


def _ada_bwd(c_all, d_cols):
    def body(c_ref, d_ref, out_ref):
        cv = c_ref[...]
        ca = cv * _sigmoid(cv)
        for l in range(DEPTH):
            out_ref[l] = lax.dot_general(ca, d_ref[l], (((0,), (0,)), ((), ())), precision=HIGHEST,
                                         preferred_element_type=F32)

    return pl.pallas_call(
        body, name="ada_bwd",
        out_shape=jax.ShapeDtypeStruct((DEPTH, D_MODEL, ADA_COLS), F32),
        compiler_params=_params(),
    )(c_all, d_cols)


def _lower_bounds(logits):
    m = jnp.maximum(logits[0:1], logits[1:2])
    e0, e1 = jnp.exp(logits[0:1] - m), jnp.exp(logits[1:2] - m)
    den = e0 + e1
    p0, p1 = e0 / den, e1 / den
    low0 = p0 - p0
    low1 = (p0 + p1) - p0
    return (p0, p1), (low0, low1)


def _lb_fwd(lb_logits):
    def body(lg_ref, out_ref):
        _, (low0, low1) = _lower_bounds(lg_ref[...])
        out_ref[0:1, :] = jnp.clip(low0, 0.0, 1.0)
        out_ref[1:2, :] = jnp.clip(low1, 0.0, 1.0)

    return pl.pallas_call(body, name="lb_fwd", out_shape=jax.ShapeDtypeStruct(lb_logits.shape, F32),
                          compiler_params=_params())(lb_logits)


def _row_spec(cols=D_MODEL):
    return pl.BlockSpec((1, cols), lambda *_: (0, 0))


def _prenorm_fwd(x, g, shift, scale, tm, name, after=None):
    seq = x.shape[0]

    def body(x_ref, g_ref, sh_ref, sc_ref, h_ref):
        xv = x_ref[...]
        rs = lax.rsqrt(jnp.mean(xv * xv, axis=-1, keepdims=True) + NORM_EPS)
        h = (xv * rs * g_ref[...]) * (1.0 + sc_ref[...]) + sh_ref[...]
        h_ref[...] = h.astype(h_ref.dtype)

    tile = pl.BlockSpec((tm, D_MODEL), lambda i: (i, 0))
    return _pallas_after(
        body, 4, after, name=name, grid=(seq // tm,),
        in_specs=[tile, _row_spec(), _row_spec(), _row_spec()], out_specs=tile,
        out_shape=jax.ShapeDtypeStruct((seq, D_MODEL), MXU_DTYPE),
        compiler_params=_params(dimension_semantics=("parallel",)),
    )(x, g, shift, scale)


def _in_proj(h, win_g, tm, name, after=None):
    seq = h.shape[0]

    def body(h_ref, w_ref, z_ref):
        z_ref[...] = jnp.dot(h_ref[...], w_ref[...], preferred_element_type=F32)

    return _pallas_after(
        body, 2, after, name=name, grid=(N_DEV, seq // tm),
        in_specs=[pl.BlockSpec((tm, D_MODEL), lambda j, i: (i, 0)),
                  pl.BlockSpec((None, D_MODEL, IN_COLS), lambda j, i: (j, 0, 0))],
        out_specs=pl.BlockSpec((tm, IN_COLS), lambda j, i: (i, j)),
        out_shape=jax.ShapeDtypeStruct((seq, IN_WIDTH), F32),
        compiler_params=_params(dimension_semantics=("parallel", "parallel")),
    )(h, win_g)


def _shift_down(v, j, pos):
    return jnp.where(pos >= j, pltpu.roll(v, j, 0), 0.0)


def _shift_up(v, j, pos, seq):
    return jnp.where(pos < seq - j, pltpu.roll(v, seq - j, 0), 0.0)


def _select_window(g, candidates):
    out = candidates[-1]
    for i in range(len(candidates) - 2, -1, -1):
        out = jnp.where(g == i, candidates[i], out)
    return out


def _pool_mean_minus_token(u, g, pos):
    sums, acc = [], u
    for j in (1, 2, 4, 8):
        acc = acc + _shift_down(acc, j, pos)
        sums.append(acc)
    wsum = _select_window(g, sums)
    width = jnp.left_shift(2, g).astype(F32)
    count = jnp.minimum(pos.astype(F32) + 1.0, width)
    return wsum / count - u, count


def _pool_fwd(z, pool_w_l, pool_scale_l, name, after=None):
    seq = z.shape[0]

    def body(pv_ref, pg_ref, w_ref, sc_ref, out_ref):
        g = pl.program_id(0)
        pos = lax.broadcasted_iota(jnp.int32, (seq, GROUP_DIM), 0)
        pm, _ = _pool_mean_minus_token(pv_ref[...], g, pos)
        lin = _dot(pm, w_ref[...]) * sc_ref[...]
        pg = pg_ref[...]
        out_ref[...] = (lin * (pg * _sigmoid(pg))).astype(out_ref.dtype)

    return _pallas_after(
        body, 4, after, name=name, grid=(POOL_GROUPS,),
        in_specs=[pl.BlockSpec((seq, GROUP_DIM), lambda g: (0, g)),
                  pl.BlockSpec((seq, GROUP_DIM), lambda g: (0, POOL_GROUPS + g)),
                  pl.BlockSpec((None, GROUP_DIM, GROUP_DIM), lambda g: (g, 0, 0)),
                  pl.BlockSpec((1, GROUP_DIM), lambda g: (0, g))],
        out_specs=pl.BlockSpec((seq, GROUP_DIM), lambda g: (0, g)),
        out_shape=jax.ShapeDtypeStruct((seq, POOL_WIDTH), MXU_DTYPE),
        compiler_params=_params(dimension_semantics=("parallel",)),
    )(z, z, pool_w_l, pool_scale_l)


def _chunk_masks():
    row = lax.broadcasted_iota(jnp.int32, (CHUNK, CHUNK), 0)
    col = lax.broadcasted_iota(jnp.int32, (CHUNK, CHUNK), 1)
    causal = row >= col
    before_sub = col < (row // SUB) * SUB
    suffix = row <= col
    return causal, before_sub, suffix


def _masked_sums(masks, v):
    lhs = jnp.concatenate([m.astype(jnp.bfloat16) for m in masks], axis=0)
    hi = v.astype(jnp.bfloat16)
    rest = v - hi.astype(F32)
    mid = rest.astype(jnp.bfloat16)
    lo = (rest - mid.astype(F32)).astype(jnp.bfloat16)
    out = jnp.dot(lhs, hi, preferred_element_type=F32)
    out += jnp.dot(lhs, mid, preferred_element_type=F32)
    out += jnp.dot(lhs, lo, preferred_element_type=F32)
    return [out[i * CHUNK:(i + 1) * CHUNK] for i in range(len(masks))]


def _gates(zf, lb):
    sg = _sigmoid(zf)
    f = lb + (1.0 - lb) * sg
    logf = jnp.log(jnp.maximum(f, LOG_FLOOR))
    return sg, f, logf


def _intra_blocks(q_h, k_h, cum_h, base_h, causal):
    rel = cum_h - base_h
    out = []
    for i in range(N_SUB):
        rows = slice(i * SUB, (i + 1) * SUB)
        e_q = jnp.exp(rel[rows])
        base_i = jnp.concatenate([base_h[rows]] * N_SUB, axis=0)
        e_k = jnp.exp(jnp.minimum(base_i - cum_h, EXP_CLAMP))
        q_t = (q_h[rows] * e_q).astype(MXU_DTYPE)
        k_t = (k_h * e_k).astype(MXU_DTYPE)
        a_i = jnp.where(causal[rows], _dot_nt(q_t, k_t), 0.0)
        out.append((q_t, k_t, e_q, e_k, a_i))
    return out


def _hgrn_fwd(z, lb_l, gn_l, name, after=None):
    seq = z.shape[0]
    n_chunks = seq // CHUNK

    def body(hq_ref, hf_ref, hi_ref, hg_ref, lb_ref, gn_ref, o_ref, bin_ref, st_ref, state):
        @pl.when(pl.program_id(0) == 0)
        def _():
            state[...] = jnp.zeros_like(state)

        causal, before_sub, _ = _chunk_masks()
        _, f, logf = _gates(hf_ref[...], lb_ref[...])
        kk = 1.0 - f
        hq = hq_ref[...]
        q = hq * _sigmoid(hq)
        cum, base = _masked_sums([causal, before_sub], logf)
        st_ref[0] = state[...]
        for h in range(HEADS):
            sl = slice(h * HEAD_DIM, (h + 1) * HEAD_DIM)
            q_h, k_h, cum_h = q[:, sl], kk[:, sl], cum[:, sl]
            v_h = hi_ref[:, sl]
            st_h = state[h]
            blocks = _intra_blocks(q_h, k_h, cum_h, base[:, sl], causal)
            a = jnp.concatenate([b[4] for b in blocks], axis=0)
            o_h = _dot_nt(q_h * jnp.exp(cum_h), st_h) + _dot(a, v_h)
            last = jnp.sum(logf[:, sl], axis=0, keepdims=True)
            state[h] = st_h * jnp.exp(last) + _dot_tn(v_h, k_h * jnp.exp(last - cum_h))
            rs = lax.rsqrt(jnp.mean(o_h * o_h, axis=-1, keepdims=True) + NORM_EPS)
            hg = hg_ref[:, sl]
            o_ref[:, sl] = o_h
            bin_ref[:, sl] = ((o_h * rs * gn_ref[...]) * (hg * _sigmoid(hg))).astype(bin_ref.dtype)

    def col(block):
        return pl.BlockSpec((CHUNK, D_MODEL), lambda c: (c, block))

    tile = pl.BlockSpec((CHUNK, D_MODEL), lambda c: (c, 0))
    return _pallas_after(
        body, 6, after, name=name, grid=(n_chunks,),
        in_specs=[col(COL_HQ), col(COL_HF), col(COL_HI), col(COL_HG), _row_spec(), _row_spec(HEAD_DIM)],
        out_specs=[tile, tile, pl.BlockSpec((1, HEADS, HEAD_DIM, HEAD_DIM), lambda c: (c, 0, 0, 0))],
        out_shape=[jax.ShapeDtypeStruct((seq, D_MODEL), F32),
                   jax.ShapeDtypeStruct((seq, D_MODEL), MXU_DTYPE),
                   jax.ShapeDtypeStruct((n_chunks, HEADS, HEAD_DIM, HEAD_DIM), F32)],
        scratch_shapes=[pltpu.VMEM((HEADS, HEAD_DIM, HEAD_DIM), F32)],
        compiler_params=_params(dimension_semantics=("arbitrary",)),
    )(z, z, z, z, lb_l, gn_l)


def _rms_parts(y):
    rs = lax.rsqrt(jnp.mean(y * y, axis=-1, keepdims=True) + NORM_EPS)
    return rs, y * rs


def _merge_fwd(a_in, b_in, z, x, wpo_g, who_g, wout_g, gate, g_post, tm, name):
    seq = x.shape[0]

    def body(a_ref, b_ref, mgp_ref, mgh_ref, x_ref, wpo_ref, who_ref, wout_ref, gate_ref, gp_ref,
             ba_ref, bb_ref, mer_ref, y_ref, xn_ref):
        a = a_ref[...]
        ba = jnp.concatenate([_dot(a, wpo_ref[j]) for j in range(N_DEV)], axis=1)
        bb = _dot(b_ref[...], who_ref[...])
        merged = _sigmoid(mgp_ref[...]) * ba + _sigmoid(mgh_ref[...]) * bb
        y = _dot(merged, wout_ref[...])
        _, yn = _rms_parts(y)
        ba_ref[...] = ba.astype(ba_ref.dtype)
        bb_ref[...] = bb.astype(bb_ref.dtype)
        mer_ref[...] = merged.astype(mer_ref.dtype)
        y_ref[...] = y
        xn_ref[...] = x_ref[...] + gate_ref[...] * (yn * gp_ref[...])

    def tile(cols=D_MODEL, block=0):
        return pl.BlockSpec((tm, cols), lambda i: (i, block))

    full = pl.BlockSpec((D_MODEL, D_MODEL), lambda i: (0, 0))
    act = jax.ShapeDtypeStruct((seq, D_MODEL), MXU_DTYPE)
    f32 = jax.ShapeDtypeStruct((seq, D_MODEL), F32)
    return pl.pallas_call(
        body, name=name, grid=(seq // tm,),
        in_specs=[tile(POOL_WIDTH), tile(), tile(block=COL_MGP), tile(block=COL_MGH), tile(),
                  pl.BlockSpec((N_DEV, POOL_WIDTH, GROUP_DIM), lambda i: (0, 0, 0)),
                  full, full, _row_spec(), _row_spec()],
        out_specs=[tile(), tile(), tile(), tile(), tile()],
        out_shape=[act, act, act, f32, f32],
        compiler_params=_params(dimension_semantics=("parallel",)),
    )(a_in, b_in, z, z, x, wpo_g, who_g, wout_g, gate, g_post)


def _loss_grad(x_out, target, tm):
    seq = x_out.shape[0]

    def body(x_ref, t_ref, loss_ref, dx_ref):
        @pl.when(pl.program_id(0) == 0)
        def _():
            loss_ref[...] = jnp.zeros_like(loss_ref)

        err = x_ref[...] - t_ref[...]
        per_token = jnp.mean(err * err, axis=-1, keepdims=True)
        loss_ref[...] += 0.5 * jnp.sum(per_token, axis=0, keepdims=True)
        dx_ref[...] = err * (1.0 / D_MODEL)

    tile = pl.BlockSpec((tm, D_MODEL), lambda i: (i, 0))
    return pl.pallas_call(
        body, name="loss_grad", grid=(seq // tm,),
        in_specs=[tile, tile],
        out_specs=[pl.BlockSpec((1, 1), lambda i: (0, 0)), tile],
        out_shape=[jax.ShapeDtypeStruct((1, 1), F32), jax.ShapeDtypeStruct((seq, D_MODEL), F32)],
        compiler_params=_params(dimension_semantics=("arbitrary",)),
    )(x_out, target)


def _merge_bwd(dx, y, ba, bb, z, wpo_g, who_g, wout_g, gate, g_post, tm, name):
    seq = dx.shape[0]

    def body(dx_ref, y_ref, ba_ref, bb_ref, mgp_ref, mgh_ref, wpo_ref, who_ref, wout_ref, gate_ref, gp_ref,
             dy_ref, dba_ref, dbb_ref, da_ref, db_ref, dmg_ref, acc_ref):
        @pl.when(pl.program_id(0) == 0)
        def _():
            acc_ref[...] = jnp.zeros_like(acc_ref)

        dxv = dx_ref[...]
        rs, yn = _rms_parts(y_ref[...])
        acc_ref[0:1, :] += jnp.sum(dxv * yn * gp_ref[...], axis=0, keepdims=True)
        acc_ref[1:2, :] += jnp.sum(dxv * gate_ref[...] * yn, axis=0, keepdims=True)
        dyn = dxv * (gate_ref[...] * gp_ref[...])
        dy = rs * (dyn - yn * jnp.mean(dyn * yn, axis=-1, keepdims=True))
        dmerged = _dot_nt(dy, wout_ref[...])
        sp, sh = _sigmoid(mgp_ref[...]), _sigmoid(mgh_ref[...])
        dba, dbb = sp * dmerged, sh * dmerged
        dmg_ref[:, 0:D_MODEL] = (dmerged * ba_ref[...].astype(F32) * sp * (1.0 - sp)).astype(dmg_ref.dtype)
        dmg_ref[:, D_MODEL:2 * D_MODEL] = (dmerged * bb_ref[...].astype(F32) * sh * (1.0 - sh)).astype(dmg_ref.dtype)
        da = _dot_nt(dba[:, 0:GROUP_DIM], wpo_ref[0])
        for j in range(1, N_DEV):
            da += _dot_nt(dba[:, j * GROUP_DIM:(j + 1) * GROUP_DIM], wpo_ref[j])
        dy_ref[...] = dy.astype(dy_ref.dtype)
        dba_ref[...] = dba.astype(dba_ref.dtype)
        dbb_ref[...] = dbb.astype(dbb_ref.dtype)
        da_ref[...] = da
        db_ref[...] = _dot_nt(dbb, who_ref[...])

    def tile(cols=D_MODEL, block=0):
        return pl.BlockSpec((tm, cols), lambda i: (i, block))

    full = pl.BlockSpec((D_MODEL, D_MODEL), lambda i: (0, 0))
    act = jax.ShapeDtypeStruct((seq, D_MODEL), MXU_DTYPE)
    return pl.pallas_call(
        body, name=name, grid=(seq // tm,),
        in_specs=[tile(), tile(), tile(), tile(), tile(block=COL_MGP), tile(block=COL_MGH),
                  pl.BlockSpec((N_DEV, POOL_WIDTH, GROUP_DIM), lambda i: (0, 0, 0)),
                  full, full, _row_spec(), _row_spec()],
        out_specs=[tile(), tile(), tile(), tile(POOL_WIDTH), tile(), tile(2 * D_MODEL),
                   pl.BlockSpec((8, D_MODEL), lambda i: (0, 0))],
        out_shape=[act, act, act, jax.ShapeDtypeStruct((seq, POOL_WIDTH), F32),
                   jax.ShapeDtypeStruct((seq, D_MODEL), F32),
                   jax.ShapeDtypeStruct((seq, 2 * D_MODEL), MXU_DTYPE),
                   jax.ShapeDtypeStruct((8, D_MODEL), F32)],
        compiler_params=_params(dimension_semantics=("arbitrary",)),
    )(dx, y, ba, bb, z, z, wpo_g, who_g, wout_g, gate, g_post)


def _grad_tn(a, b, tn, dev_major, name):
    seq, ka = a.shape
    n = b.shape[1]

    def body(a_ref, b_ref, out_ref):
        out_ref[...] = _dot_tn(a_ref[...], b_ref[...]).astype(out_ref.dtype)

    if dev_major:
        out_spec = pl.BlockSpec((None, ka, tn), lambda j: (j, 0, 0))
        out_shape = jax.ShapeDtypeStruct((n // tn, ka, tn), WIRE_DTYPE)
    else:
        out_spec = pl.BlockSpec((ka, tn), lambda j: (0, j))
        out_shape = jax.ShapeDtypeStruct((ka, n), WIRE_DTYPE)
    return pl.pallas_call(
        body, name=name, grid=(n // tn,),
        in_specs=[pl.BlockSpec((seq, ka), lambda j: (0, 0)), pl.BlockSpec((seq, tn), lambda j: (0, j))],
        out_specs=out_spec, out_shape=out_shape,
        compiler_params=_params(dimension_semantics=("parallel",)),
    )(a, b)


def _hgrn_bwd(db_in, z, o, states, lb_l, gn_l, name, after=None):
    seq = z.shape[0]
    n_chunks = seq // CHUNK

    def body(db_ref, hq_ref, hf_ref, hi_ref, hg_ref, o_ref, st_ref, lb_ref, gn_ref,
             dz_ref, dlb_ref, dgn_ref, dstate, dq_buf, dk_buf, dg_buf):
        @pl.when(pl.program_id(0) == 0)
        def _():
            dstate[...] = jnp.zeros_like(dstate)
            dlb_ref[...] = jnp.zeros_like(dlb_ref)
            dgn_ref[...] = jnp.zeros_like(dgn_ref)

        causal, before_sub, suffix = _chunk_masks()
        lb = lb_ref[...]
        sg, f, logf = _gates(hf_ref[...], lb)
        kk = 1.0 - f
        hq = hq_ref[...]
        sq = _sigmoid(hq)
        q = hq * sq
        cum, base = _masked_sums([causal, before_sub], logf)
        gn = gn_ref[...]
        dgn = jnp.zeros((1, HEAD_DIM), F32)
        dlast = []
        for h in range(HEADS):
            sl = slice(h * HEAD_DIM, (h + 1) * HEAD_DIM)
            q_h, k_h, cum_h = q[:, sl], kk[:, sl], cum[:, sl]
            v_h = hi_ref[:, sl]
            st_h = st_ref[0, h]
            dst_h = dstate[h]
            rs, ohat = _rms_parts(o_ref[:, sl])
            hg = hg_ref[:, sl]
            shg = _sigmoid(hg)
            d_bin = db_ref[:, sl]
            don = d_bin * (hg * shg)
            dgn += jnp.sum(don * ohat, axis=0, keepdims=True)
            dohat = don * gn
            do = rs * (dohat - ohat * jnp.mean(dohat * ohat, axis=-1, keepdims=True))
            dz_ref[:, 3 * D_MODEL + h * HEAD_DIM:3 * D_MODEL + (h + 1) * HEAD_DIM] = (
                d_bin * (ohat * gn) * _dsilu(hg, shg)).astype(dz_ref.dtype)
            last = jnp.sum(logf[:, sl], axis=0, keepdims=True)
            g_in = jnp.exp(cum_h)
            d_out = jnp.exp(last - cum_h)
            q_bar, k_bar = q_h * g_in, k_h * d_out
            blocks = _intra_blocks(q_h, k_h, cum_h, base[:, sl], causal)
            a = jnp.concatenate([b[4] for b in blocks], axis=0)
            da = jnp.where(causal, _dot_nt(do, v_h), 0.0)
            dv = _dot_tn(a, do) + _dot_nt(k_bar, dst_h)
            dq_bar, dk_bar = _dot(do, st_h), _dot(v_h, dst_h)
            dk = dk_bar * d_out
            dq_parts, dg_parts = [], []
            dg_k = k_bar * dk_bar
            dlast.append(jnp.sum(k_bar * dk_bar, axis=0, keepdims=True)
                         + jnp.exp(last) * jnp.sum(st_h * dst_h, axis=0, keepdims=True))
            for i, (q_t, k_t, e_q, e_k, _) in enumerate(blocks):
                da_i = da[i * SUB:(i + 1) * SUB].astype(MXU_DTYPE)
                dq_t = _dot(da_i, k_t)
                dk_t = _dot_tn(da_i, q_t)
                dq_parts.append(dq_t * e_q)
                dk += dk_t * e_k
                dg_parts.append(q_t.astype(F32) * dq_t)
                dg_k += k_t.astype(F32) * dk_t
            dq = dq_bar * g_in + jnp.concatenate(dq_parts, axis=0)
            dg_buf[:, sl] = q_bar * dq_bar + jnp.concatenate(dg_parts, axis=0) - dg_k
            dstate[h] = dst_h * jnp.exp(last) + _dot_tn(do, q_bar)
            dq_buf[:, sl] = dq
            dk_buf[:, sl] = dk
            dz_ref[:, 2 * D_MODEL + h * HEAD_DIM:2 * D_MODEL + (h + 1) * HEAD_DIM] = dv.astype(dz_ref.dtype)
        dgn_ref[...] += dgn
        dq_all, dk_all = dq_buf[...], dk_buf[...]
        dlogf = _masked_sums([suffix], dg_buf[...])[0] + jnp.concatenate(dlast, axis=1)
        df = jnp.where(f > LOG_FLOOR, dlogf / f, 0.0) - dk_all
        dlb_ref[...] += jnp.sum(df * (1.0 - sg), axis=0, keepdims=True)
        dz_ref[:, 0:D_MODEL] = (dq_all * _dsilu(hq, sq)).astype(dz_ref.dtype)
        dz_ref[:, D_MODEL:2 * D_MODEL] = (df * (1.0 - lb) * sg * (1.0 - sg)).astype(dz_ref.dtype)

    last_chunk = n_chunks - 1

    def col(block):
        return pl.BlockSpec((CHUNK, D_MODEL), lambda c: (last_chunk - c, block))

    return _pallas_after(
        body, 9, after, name=name, grid=(n_chunks,),
        in_specs=[col(0), col(COL_HQ), col(COL_HF), col(COL_HI), col(COL_HG), col(0),
                  pl.BlockSpec((1, HEADS, HEAD_DIM, HEAD_DIM), lambda c: (last_chunk - c, 0, 0, 0)),
                  _row_spec(), _row_spec(HEAD_DIM)],
        out_specs=[pl.BlockSpec((CHUNK, 4 * D_MODEL), lambda c: (last_chunk - c, 0)),
                   _row_spec(), _row_spec(HEAD_DIM)],
        out_shape=[jax.ShapeDtypeStruct((seq, 4 * D_MODEL), MXU_DTYPE),
                   jax.ShapeDtypeStruct((1, D_MODEL), F32), jax.ShapeDtypeStruct((1, HEAD_DIM), F32)],
        scratch_shapes=[pltpu.VMEM((HEADS, HEAD_DIM, HEAD_DIM), F32)] + [pltpu.VMEM((CHUNK, D_MODEL), F32)] * 3,
        compiler_params=_params(dimension_semantics=("arbitrary",)),
    )(db_in, z, z, z, z, o, states, lb_l, gn_l)


def _pool_bwd(da_in, z, pool_w_l, pool_scale_l, name, after=None):
    seq = z.shape[0]

    def body(da_ref, pv_ref, pg_ref, w_ref, sc_ref, dpv_ref, dpg_ref, dw_ref, dsc_ref):
        g = pl.program_id(0)
        pos = lax.broadcasted_iota(jnp.int32, (seq, GROUP_DIM), 0)
        pm, count = _pool_mean_minus_token(pv_ref[...], g, pos)
        lin0 = _dot(pm, w_ref[...])
        pg = pg_ref[...]
        spg = _sigmoid(pg)
        da = da_ref[...]
        dlin = da * (pg * spg)
        dpg_ref[...] = (da * (lin0 * sc_ref[...]) * _dsilu(pg, spg)).astype(dpg_ref.dtype)
        dsc_ref[...] = jnp.sum(dlin * lin0, axis=0, keepdims=True)
        dl0 = dlin * sc_ref[...]
        dw_ref[...] = _dot_tn(pm, dl0)
        dpm = _dot_nt(dl0, w_ref[...])
        sums, acc = [], dpm / count
        for j in (1, 2, 4, 8):
            acc = acc + _shift_up(acc, j, pos, seq)
            sums.append(acc)
        dpv_ref[...] = (_select_window(g, sums) - dpm).astype(dpv_ref.dtype)

    grp = pl.BlockSpec((seq, GROUP_DIM), lambda g: (0, g))
    return _pallas_after(
        body, 5, after, name=name, grid=(POOL_GROUPS,),
        in_specs=[grp, grp, pl.BlockSpec((seq, GROUP_DIM), lambda g: (0, POOL_GROUPS + g)),
                  pl.BlockSpec((None, GROUP_DIM, GROUP_DIM), lambda g: (g, 0, 0)),
                  pl.BlockSpec((1, GROUP_DIM), lambda g: (0, g))],
        out_specs=[grp, grp, pl.BlockSpec((None, GROUP_DIM, GROUP_DIM), lambda g: (g, 0, 0)),
                   pl.BlockSpec((1, GROUP_DIM), lambda g: (0, g))],
        out_shape=[jax.ShapeDtypeStruct((seq, POOL_WIDTH), MXU_DTYPE),
                   jax.ShapeDtypeStruct((seq, POOL_WIDTH), MXU_DTYPE),
                   jax.ShapeDtypeStruct((POOL_GROUPS, GROUP_DIM, GROUP_DIM), F32),
                   jax.ShapeDtypeStruct((1, POOL_WIDTH), F32)],
        compiler_params=_params(dimension_semantics=("parallel",)),
    )(da_in, z, z, pool_w_l, pool_scale_l)


def _in_proj_dw(h, dz, name, after=None):
    seq = h.shape[0]

    def body(h_ref, dz_ref, out_ref):
        out_ref[...] = lax.dot_general(h_ref[...], dz_ref[...], (((0,), (0,)), ((), ())),
                                       preferred_element_type=F32).astype(out_ref.dtype)

    return _pallas_after(
        body, 2, after, name=name, grid=(N_DEV,),
        in_specs=[pl.BlockSpec((seq, D_MODEL), lambda j: (0, 0)), pl.BlockSpec((seq, IN_COLS), lambda j: (0, j))],
        out_specs=pl.BlockSpec((None, D_MODEL, IN_COLS), lambda j: (j, 0, 0)),
        out_shape=jax.ShapeDtypeStruct((N_DEV, D_MODEL, IN_COLS), WIRE_DTYPE),
        compiler_params=_params(dimension_semantics=("parallel",)),
    )(h, dz)


def _in_proj_dh(dz, win_g, tm, name, after=None):
    seq = dz.shape[0]

    def body(dz_ref, w_ref, dh_ref):
        @pl.when(pl.program_id(1) == 0)
        def _():
            dh_ref[...] = jnp.zeros_like(dh_ref)

        dh_ref[...] += lax.dot_general(dz_ref[...], w_ref[...], (((1,), (1,)), ((), ())),
                                       preferred_element_type=F32)

    return _pallas_after(
        body, 2, after, name=name, grid=(seq // tm, N_DEV),
        in_specs=[pl.BlockSpec((tm, IN_COLS), lambda i, j: (i, j)),
                  pl.BlockSpec((None, D_MODEL, IN_COLS), lambda i, j: (j, 0, 0))],
        out_specs=pl.BlockSpec((tm, D_MODEL), lambda i, j: (i, 0)),
        out_shape=jax.ShapeDtypeStruct((seq, D_MODEL), F32),
        compiler_params=_params(dimension_semantics=("parallel", "arbitrary")),
    )(dz, win_g)


def _prenorm_bwd(x, dh, dx_res, g, scale, tm, name, after=None):
    seq = x.shape[0]

    def body(x_ref, dh_ref, dxr_ref, g_ref, sc_ref, dx_ref, acc_ref):
        @pl.when(pl.program_id(0) == 0)
        def _():
            acc_ref[...] = jnp.zeros_like(acc_ref)

        rs, xn = _rms_parts(x_ref[...])
        dh = dh_ref[...]
        acc_ref[0:1, :] += jnp.sum(dh, axis=0, keepdims=True)
        acc_ref[1:2, :] += jnp.sum(dh * (xn * g_ref[...]), axis=0, keepdims=True)
        dhn = dh * (1.0 + sc_ref[...])
        acc_ref[2:3, :] += jnp.sum(dhn * xn, axis=0, keepdims=True)
        dxn = dhn * g_ref[...]
        dx_ref[...] = rs * (dxn - xn * jnp.mean(dxn * xn, axis=-1, keepdims=True)) + dxr_ref[...]

    tile = pl.BlockSpec((tm, D_MODEL), lambda i: (i, 0))
    return _pallas_after(
        body, 5, after, name=name, grid=(seq // tm,),
        in_specs=[tile, tile, tile, _row_spec(), _row_spec()],
        out_specs=[tile, pl.BlockSpec((8, D_MODEL), lambda i: (0, 0))],
        out_shape=[jax.ShapeDtypeStruct((seq, D_MODEL), F32), jax.ShapeDtypeStruct((8, D_MODEL), F32)],
        compiler_params=_params(dimension_semantics=("arbitrary",)),
    )(x, dh, dx_res, g, scale)


def _adamw_math(w, g, m, v):
    m = ADAM_B1 * m + (1.0 - ADAM_B1) * g
    v = ADAM_B2 * v + (1.0 - ADAM_B2) * (g * g)
    m_hat = m / (1.0 - ADAM_B1 ** ADAM_STEP)
    v_hat = v / (1.0 - ADAM_B2 ** ADAM_STEP)
    delta = -ADAM_LR * (m_hat / (jnp.sqrt(v_hat) + ADAM_EPS) + ADAM_WD * w)
    return delta, m, v


def _adamw_sharded(w, m, v, contrib, tr, name):
    depth, rows, cols = w.shape
    n_parts = contrib.shape[1]

    def body(w_ref, m_ref, v_ref, c_ref, g_ref, d_ref, mo_ref, vo_ref):
        g = c_ref[0].astype(F32)
        for p in range(1, n_parts):
            g += c_ref[p].astype(F32)
        delta, mn, vn = _adamw_math(w_ref[...], g, m_ref[...], v_ref[...])
        g_ref[...] = g
        d_ref[...] = delta
        mo_ref[...] = mn
        vo_ref[...] = vn

    tile = pl.BlockSpec((None, tr, cols), lambda l, i: (l, i, 0))
    shape = jax.ShapeDtypeStruct(w.shape, F32)
    return pl.pallas_call(
        body, name=name, grid=(depth, rows // tr),
        in_specs=[tile, tile, tile, pl.BlockSpec((None, n_parts, tr, cols), lambda l, i: (l, 0, i, 0))],
        out_specs=[tile] * 4, out_shape=[shape] * 4,
        compiler_params=_params(dimension_semantics=("parallel", "parallel")),
    )(w, m, v, contrib)


def _adamw_layer(w, m, v, contribs, l, tr, name, prev=None):
    _, rows, cols = w.shape
    n = len(contribs)

    def body(*refs):
        w_ref, m_ref, v_ref = refs[:3]
        c_refs = refs[3:3 + n]
        g_ref, d_ref, mo_ref, vo_ref = refs[-4:]
        g = c_refs[0][...].astype(F32)
        for c_ref in c_refs[1:]:
            g += c_ref[...].astype(F32)
        delta, mn, vn = _adamw_math(w_ref[...], g, m_ref[...], v_ref[...])
        g_ref[...] = g
        d_ref[...] = delta
        mo_ref[...] = mn
        vo_ref[...] = vn

    tile = pl.BlockSpec((None, tr, cols), lambda i: (l, i, 0))
    in_specs = [tile, tile, tile] + [pl.BlockSpec((None, tr, cols), lambda i, s=slot: (s, i, 0)) for _, slot in contribs]
    operands = [w, m, v] + [arr for arr, _ in contribs]
    aliases = {}
    if prev is not None:
        aliases = {len(operands) + k: k for k in range(4)}
        in_specs += [pl.BlockSpec(memory_space=pl.ANY)] * 4
        operands += list(prev)
    shape = jax.ShapeDtypeStruct(w.shape, F32)
    return pl.pallas_call(
        body, name=name, grid=(rows // tr,), in_specs=in_specs, out_specs=[tile] * 4, out_shape=[shape] * 4,
        input_output_aliases=aliases,
        compiler_params=_params(dimension_semantics=("parallel",)),
    )(*operands)


def _adamw_small(w_pack, m_pack, v_pack, g_late, g_early):
    def body(w_ref, m_ref, v_ref, gl_ref, ge_ref, g_ref, d_ref, mo_ref, vo_ref):
        g_l, g_e = gl_ref[0], ge_ref[0]
        for d in range(1, N_DEV):
            g_l += gl_ref[d]
            g_e += ge_ref[d]
        g = jnp.concatenate([g_l, g_e], axis=0)
        w = w_ref[...]
        r0, r1, r2 = LB_ROW0, LB_ROW0 + 8, LB_ROW0 + 16
        lg0, lg1 = w[r0:r1], w[r1:r2]
        mx = jnp.maximum(lg0, lg1)
        e0, e1 = jnp.exp(lg0 - mx), jnp.exp(lg1 - mx)
        p0, p1 = e0 / (e0 + e1), e1 / (e0 + e1)
        low = ((p0 - p0), (p0 + p1) - p0)
        dlow = [g_rows * jnp.where((lo > 0.0) & (lo < 1.0), 1.0, jnp.where((lo == 0.0) | (lo == 1.0), 0.5, 0.0))
                for g_rows, lo in ((g[r0:r1], low[0]), (g[r1:r2], low[1]))]
        dp0 = (dlow[0] + dlow[1]) - (dlow[0] + dlow[1])
        dp1 = dlow[1]
        inner = p0 * dp0 + p1 * dp1
        g = jnp.concatenate([g[:r0], p0 * (dp0 - inner), p1 * (dp1 - inner), g[r2:]], axis=0)
        delta, mn, vn = _adamw_math(w, g, m_ref[...], v_ref[...])
        g_ref[...] = g
        d_ref[...] = delta
        mo_ref[...] = mn
        vo_ref[...] = vn

    shape = jax.ShapeDtypeStruct(w_pack.shape, F32)
    return pl.pallas_call(body, name="adamw_small", out_shape=[shape] * 4, compiler_params=_params())(
        w_pack, m_pack, v_pack, g_late, g_early)


def _pack_small(parts, first=0, last=len(_SMALL_ROWS)):
    rows = [(parts[name] if l is None else parts[name][l]).reshape(n, 128) for name, l, n in _SMALL_ROWS[first:last]]
    if last == len(_SMALL_ROWS):
        rows.append(jnp.zeros((SMALL_ROWS_PAD - sum(n for _, _, n in _SMALL_ROWS), 128), F32))
    return jnp.concatenate(rows, axis=0)


def _unpack_small(pack, shapes):
    pieces, r = {}, 0
    for name, l, n in _SMALL_ROWS:
        pieces.setdefault(name, []).append(pack[r:r + n])
        r += n
    return {name: jnp.concatenate(p, axis=0).reshape(shapes[name]) for name, p in pieces.items()}


def kernel(x, c, w_ada, b_ada, g_pre, g_post, w_in, pool_w, pool_scale, lb_logits, hgrn_norm_g, w_pool_o, w_hgrn_o, w_out, loss_target, m_w_ada, m_b_ada, m_g_pre, m_g_post, m_w_in, m_pool_w, m_pool_scale, m_lb_logits, m_hgrn_norm_g, m_w_pool_o, m_w_hgrn_o, m_w_out, v_w_ada, v_b_ada, v_g_pre, v_g_post, v_w_in, v_pool_w, v_pool_scale, v_lb_logits, v_hgrn_norm_g, v_w_pool_o, v_w_hgrn_o, v_w_out):
    seq = x.shape[1]
    tm = min(512, seq)
    tm_merge = min(256, seq)
    pos = _my_position()
    me = pos[3]

    c_all = _allgather_small(c, "allgather_c").reshape(N_DEV, D_MODEL)
    b_cols = lax.dynamic_slice_in_dim(b_ada, me * ADA_COLS, ADA_COLS, axis=1)
    ada_part = _ada_fwd(c_all, w_ada, b_cols)
    ada_all = _allgather_small(ada_part.reshape(DEPTH * N_DEV, ADA_COLS), "allgather_ada")
    ada = lax.dynamic_index_in_dim(ada_all.reshape(N_DEV, DEPTH, N_DEV, ADA_COLS), me, axis=2, keepdims=False)
    ada = jnp.transpose(ada, (1, 0, 2)).reshape(DEPTH, 3 * D_MODEL)
    shift = [ada[l:l + 1, 0:D_MODEL] for l in range(DEPTH)]
    scale = [ada[l:l + 1, D_MODEL:2 * D_MODEL] for l in range(DEPTH)]
    gate = [ada[l:l + 1, 2 * D_MODEL:] for l in range(DEPTH)]

    big = dict(win=w_in, wpo=w_pool_o, who=w_hgrn_o, wout=w_out)
    units = [["win0"], ["wpo0", "who0", "wout0"], ["win1", "wpo1", "who1", "wout1"]]
    g_streams = [_gather_streams(keys) for keys in units]
    g_state = [None] * len(units)

    def gather_start(u, after):
        bufs = {}
        for k in units[u]:
            arr = big[k[:-1]]
            bufs["s_" + k] = arr[int(k[-1])].astype(WIRE_DTYPE)
            bufs["g_" + k] = _with_own_slot(bufs["s_" + k], me)
        bufs, sems, token = _comm_call(f"gather_start_{u}", bufs, start=list(g_streams[u][:2]), after=after)
        g_state[u] = dict(bufs=bufs, sems=sems)
        return token

    def gather_pass(u, after):
        st = g_state[u]
        to_chips, _, pass_on = g_streams[u]
        st["bufs"], (st["pass_sems"],), _ = _comm_call(f"gather_pass_{u}", st["bufs"], start=[pass_on],
                                                       wait=[(to_chips, st["sems"][0])], after=after)

    def gather_done(u, after=None):
        st = g_state[u]
        _, to_sibling, pass_on = g_streams[u]
        bufs, _, _ = _comm_call(f"gather_done_{u}", st["bufs"], after=after,
                                wait=[(to_sibling, st["sems"][1]), (pass_on, st["pass_sems"])])
        return {k: bufs["g_" + k] for k in units[u]}

    token = gather_start(0, ada_all)

    lb = _lb_fwd(lb_logits)

    gw = {}
    xs, saved = [x[0]], []
    for l in range(DEPTH):
        h = _prenorm_fwd(xs[l], g_pre[l:l + 1], shift[l], scale[l], tm, f"prenorm_fwd_{l}",
                         after=token if l == 0 else None)
        token = None
        if l == 0:
            gather_pass(0, h)
            gw.update(gather_done(0))
            token = gather_start(1, gw["win0"])
        else:
            gather_pass(2, h)
            gw.update(gather_done(2))
        z = _in_proj(h, gw[f"win{l}"], seq, f"in_proj_{l}", after=token)
        if l == 0:
            gather_pass(1, z)
            token = gather_start(2, g_state[1]["bufs"]["g_wpo0"])
        a_in = _pool_fwd(z, pool_w[l], pool_scale[l:l + 1], f"pool_fwd_{l}", after=token)
        o, b_in, states = _hgrn_fwd(z, lb[l:l + 1], hgrn_norm_g[l:l + 1], f"hgrn_fwd_{l}", after=token)
        if l == 0:
            gw.update(gather_done(1, b_in))
        who_l = gw[f"who{l}"].reshape(D_MODEL, D_MODEL)
        wout_l = gw[f"wout{l}"].reshape(D_MODEL, D_MODEL)
        ba, bb, merged, y, x_next = _merge_fwd(a_in, b_in, z, xs[l], gw[f"wpo{l}"], who_l, wout_l, gate[l],
                                               g_post[l:l + 1], tm_merge, f"merge_fwd_{l}")
        xs.append(x_next)
        saved.append((h, z, a_in, o, b_in, states, ba, bb, merged, y, who_l, wout_l))

    loss_part, dx = _loss_grad(xs[DEPTH], loss_target[0], tm)
    loss = lax.psum(loss_part[0, 0], ("x", "y", "c"))

    chips = _other_chips(pos)
    pair_idx = jnp.stack([_dev_index(cx, cy, pos[2]) for cx, cy in chips] + [me]).astype(jnp.int32)
    pair_rows = dict(win=256, wpo=POOL_WIDTH, who=HEAD_DIM, wout=HEAD_DIM)

    def scatter_pair_start(u, grads):
        keys = list(grads)
        pair, to_chips = _scatter_streams(keys)
        bufs = {}
        for k in keys:
            bufs["g_" + k] = grads[k]
            bufs["st_" + k] = lax.empty((4,) + grads[k].shape[1:], WIRE_DTYPE)
        bufs, (sems,), token = _comm_call(f"scatter_pair_start_{u}", bufs, start=[pair])
        return dict(u=u, keys=keys, pair=pair, to_chips=to_chips, bufs=bufs, sems=sems, token=token)

    def scatter_chips_start(st, after):
        u, keys = st["u"], st["keys"]
        bufs, _, _ = _comm_call(f"scatter_pair_done_{u}", st["bufs"], wait=[(st["pair"], st["sems"])], after=after)
        bufs2 = {}
        for k in keys:
            bufs2["ps_" + k] = _pair_sum(bufs["g_" + k], bufs["st_" + k], pair_idx, bufs["g_" + k].shape[1],
                                         f"pair_sum_{k}")
            bufs2["ld_" + k] = lax.empty((3,) + bufs["g_" + k].shape[1:], WIRE_DTYPE)
        bufs2, (sems,), token = _comm_call(f"scatter_chips_start_{u}", bufs2, start=[st["to_chips"]])
        st.update(bufs=bufs2, sems=sems, token=token)

    def scatter_finish(st, after):
        bufs, _, _ = _comm_call(f"scatter_chips_done_{st['u']}", st["bufs"], wait=[(st["to_chips"], st["sems"])],
                                after=after)
        return {k: [(bufs["ps_" + k], 3), (bufs["ld_" + k], 0), (bufs["ld_" + k], 1), (bufs["ld_" + k], 2)]
                for k in st["keys"]}

    d_ada, small, scat = [None] * DEPTH, [None] * DEPTH, {}
    for l in reversed(range(DEPTH)):
        h, z, a_in, o, b_in, states, ba, bb, merged, y, who_l, wout_l = saved[l]
        dy, dba, dbb, da_in, db_in, dmg, acc_post = _merge_bwd(
            dx, y, ba, bb, z, gw[f"wpo{l}"], who_l, wout_l, gate[l], g_post[l:l + 1], tm_merge, f"merge_bwd_{l}")
        g_small = {
            f"wout{l}": _grad_tn(merged, dy, 512, False, f"grad_w_out_{l}").reshape(N_DEV, HEAD_DIM, D_MODEL),
            f"who{l}": _grad_tn(b_in, dbb, 512, False, f"grad_w_hgrn_o_{l}").reshape(N_DEV, HEAD_DIM, D_MODEL),
            f"wpo{l}": _grad_tn(a_in, dba, GROUP_DIM, True, f"grad_w_pool_o_{l}")}
        st_small = scat[f"small{l}"] = scatter_pair_start(f"small{l}", g_small)
        dzh, dlb, dgn = _hgrn_bwd(db_in, z, o, states, lb[l:l + 1], hgrn_norm_g[l:l + 1], f"hgrn_bwd_{l}",
                                  after=st_small["token"])
        scatter_chips_start(st_small, dzh)
        dpv, dpg, dpw, dps = _pool_bwd(da_in, z, pool_w[l], pool_scale[l:l + 1], f"pool_bwd_{l}",
                                       after=st_small["token"])
        dz = jnp.concatenate([dpv, dpg, dzh, dmg], axis=1)
        small[l] = dict(g_post=acc_post[1], pool_w=dpw, pool_scale=dps[0], lb_logits=dlb[0], hgrn_norm_g=dgn[0])
        token = None
        if l == 0:
            parts = {name: jnp.stack([small[0][name], small[1][name]]) for name in small[0]}
            parts.update(b_ada=[None, d_ada[1]], g_pre=[None, small[1]["g_pre"]])
            sg_stream = _direct_gather_stream("sg")
            early = _pack_small(parts, 2)
            sg_bufs, (sg_sems,), token = _comm_call(
                "small_grads_start", dict(s_sg=early, g_sg=_with_own_slot(early, me)), start=[sg_stream])
        st_win = scat[f"win{l}"] = scatter_pair_start(
            f"win{l}", {f"win{l}": _in_proj_dw(h, dz, f"grad_w_in_{l}", after=token)})
        dh = _in_proj_dh(dz, gw[f"win{l}"], seq, f"in_proj_dh_{l}", after=st_win["token"])
        scatter_chips_start(st_win, dh)
        dx, acc_pre = _prenorm_bwd(xs[l], dh, dx, g_pre[l:l + 1], scale[l], tm, f"prenorm_bwd_{l}",
                                   after=st_win["token"])
        d_ada[l] = jnp.concatenate([acc_pre[0], acc_pre[1], acc_post[0]])
        small[l]["g_pre"] = acc_pre[2]
    grad_x = dx[None]

    moments = dict(win=(m_w_in, v_w_in), wpo=(m_w_pool_o, v_w_pool_o), who=(m_w_hgrn_o, v_w_hgrn_o),
                   wout=(m_w_out, v_w_out))
    big_out = {}

    def finish_unit(unit, after):
        for k, contribs in scatter_finish(scat[unit], after).items():
            wname, l = k[:-1], int(k[-1])
            big_out[wname] = _adamw_layer(big[wname], moments[wname][0], moments[wname][1], contribs, l,
                                          pair_rows[wname], f"adamw_{k}", prev=big_out.get(wname))
            after = big_out[wname][0]
        return after

    after = dx
    for unit in ("small1", "win1", "small0"):
        after = finish_unit(unit, after)

    parts = dict(b_ada=[d_ada[0]], g_pre=[small[0]["g_pre"]])
    g_late = _allgather_small(_pack_small(parts, 0, 2), "allgather_late_grads", after=after)
    sg_bufs, _, _ = _comm_call("small_grads_done", sg_bufs, wait=[(sg_stream, sg_sems)], after=g_late)
    g_early = sg_bufs["g_sg"]
    small_names = list(dict.fromkeys(name for name, _, _ in _SMALL_ROWS))
    weights = dict(b_ada=b_ada, g_pre=g_pre, g_post=g_post, pool_w=pool_w, pool_scale=pool_scale,
                   lb_logits=lb_logits, hgrn_norm_g=hgrn_norm_g)
    m_small = dict(b_ada=m_b_ada, g_pre=m_g_pre, g_post=m_g_post, pool_w=m_pool_w, pool_scale=m_pool_scale,
                   lb_logits=m_lb_logits, hgrn_norm_g=m_hgrn_norm_g)
    v_small = dict(b_ada=v_b_ada, g_pre=v_g_pre, g_post=v_g_post, pool_w=v_pool_w, pool_scale=v_pool_scale,
                   lb_logits=v_lb_logits, hgrn_norm_g=v_hgrn_norm_g)
    shapes = {name: weights[name].shape for name in small_names}
    small_out = [_unpack_small(p, shapes) for p in
                 _adamw_small(_pack_small(weights), _pack_small(m_small), _pack_small(v_small), g_late, g_early)]

    d_ada_all = jnp.stack([g_late[:, 0:24, :].reshape(N_DEV, 3 * D_MODEL),
                           g_early[:, 0:24, :].reshape(N_DEV, 3 * D_MODEL)], axis=1)
    d_cols = jnp.transpose(lax.dynamic_slice_in_dim(d_ada_all, me * ADA_COLS, ADA_COLS, axis=2), (1, 0, 2))
    g_w_ada = _ada_bwd(c_all, d_cols)
    ada_out = _adamw_sharded(w_ada, m_w_ada, v_w_ada, g_w_ada[:, None], 256, "adamw_w_ada")
    finish_unit("win0", ada_out[0])

    def leaf(kind):
        s = small_out[kind]
        return (ada_out[kind], s["b_ada"], s["g_pre"], s["g_post"], big_out["win"][kind], s["pool_w"], s["pool_scale"],
                s["lb_logits"], s["hgrn_norm_g"], big_out["wpo"][kind], big_out["who"][kind], big_out["wout"][kind])

    return (loss, grad_x) + leaf(0) + leaf(1) + leaf(2) + leaf(3)
```

```python
import jax
import jax.numpy as jnp
from jax import lax
from jax.experimental import pallas as pl
from jax.experimental.pallas import tpu as pltpu

F32 = jnp.float32
MXU_DTYPE = jnp.bfloat16
WIRE_DTYPE = jnp.bfloat16

N_DEV = 8
DEPTH = 2
D_MODEL = 1024
HEADS = 8
HEAD_DIM = 128
POOL_GROUPS = 4
GROUP_DIM = 128
POOL_WIDTH = POOL_GROUPS * GROUP_DIM
IN_WIDTH = 7168
CHUNK = 64
SUB = 16
N_SUB = CHUNK // SUB
EXP_CLAMP = 80.0
NORM_EPS = 1e-6
LOG_FLOOR = 1e-30
ADA_COLS = 3 * D_MODEL // N_DEV
IN_COLS = IN_WIDTH // N_DEV
COL_HQ, COL_HF, COL_HI, COL_HG, COL_MGP, COL_MGH = 1, 2, 3, 4, 5, 6

ADAM_LR = 0.001
ADAM_B1 = 0.9
ADAM_B2 = 0.999
ADAM_EPS = 1e-08
ADAM_WD = 0.01
ADAM_STEP = 10

VMEM_LIMIT = 48 * 1024 * 1024
MESH_ID = pl.DeviceIdType.MESH
HIGHEST = lax.Precision.HIGHEST

_SMALL_ROWS = (("b_ada", 0, 24), ("g_pre", 0, 8), ("b_ada", 1, 24), ("g_pre", 1, 8), ("g_post", None, 16),
               ("pool_w", None, 1024), ("pool_scale", None, 8), ("lb_logits", None, 16), ("hgrn_norm_g", None, 2))
SMALL_LATE_ROWS = 32
SMALL_ROWS_PAD = 1136
LB_ROW0 = 32 + 32 + 16 + 1024 + 8


def _params(**kw):
    return pltpu.CompilerParams(vmem_limit_bytes=VMEM_LIMIT, **kw)


def _sigmoid(v):
    return 1.0 / (1.0 + jnp.exp(-v))


def _dsilu(v, s):
    return s * (1.0 + v * (1.0 - s))


def _dot(a, b):
    return jnp.dot(a.astype(MXU_DTYPE), b.astype(MXU_DTYPE), preferred_element_type=F32)


def _dot_nt(a, b):
    return lax.dot_general(a.astype(MXU_DTYPE), b.astype(MXU_DTYPE), (((1,), (1,)), ((), ())),
                           preferred_element_type=F32)


def _dot_tn(a, b):
    return lax.dot_general(a.astype(MXU_DTYPE), b.astype(MXU_DTYPE), (((0,), (0,)), ((), ())),
                           preferred_element_type=F32)


def _pallas_after(body, n_in, after, *, in_specs, **kw):
    if after is None:
        return pl.pallas_call(body, in_specs=in_specs, **kw)

    def tied(*refs):
        body(*refs[:n_in], *refs[n_in + 1:])

    call = pl.pallas_call(tied, in_specs=list(in_specs) + [pl.BlockSpec(memory_space=pl.ANY)], **kw)
    return lambda *operands: call(*operands, after)


def _my_position():
    mx, my, mc = lax.axis_index("x"), lax.axis_index("y"), lax.axis_index("c")
    return mx, my, mc, 4 * mx + 2 * my + mc


def _peer(mx, my, mc, k):
    px = 1 - mx if (k >> 2) & 1 else mx
    py = 1 - my if (k >> 1) & 1 else my
    pc = 1 - mc if k & 1 else mc
    return (px, py, pc), 4 * px + 2 * py + pc


def _allgather_small(v, name, after=None):
    rows, cols = v.shape

    def body(v_ref, out_ref, send_sems, recv_sems):
        mx, my, mc, me = _my_position()
        out_ref[me] = v_ref[...]
        copies = []
        for k in range(1, N_DEV):
            peer, _ = _peer(mx, my, mc, k)
            cp = pltpu.make_async_remote_copy(
                src_ref=v_ref, dst_ref=out_ref.at[me],
                send_sem=send_sems.at[k - 1], recv_sem=recv_sems.at[k - 1],
                device_id=peer, device_id_type=MESH_ID)
            cp.start()
            copies.append(cp)
        for cp in copies:
            cp.wait()

    return _pallas_after(
        body, 1, after, name=name,
        out_shape=jax.ShapeDtypeStruct((N_DEV, rows, cols), v.dtype),
        in_specs=[pl.BlockSpec(memory_space=pltpu.VMEM)],
        out_specs=pl.BlockSpec(memory_space=pltpu.VMEM),
        scratch_shapes=[pltpu.SemaphoreType.DMA((N_DEV - 1,)), pltpu.SemaphoreType.DMA((N_DEV - 1,))],
        compiler_params=_params(),
    )(v)


class _Stream:
    def __init__(self, n, plan):
        self.n, self.plan = n, plan


def _comm_call(name, bufs, start=(), wait=(), after=None):
    names = list(bufs)

    def body(*refs):
        it = iter(refs)
        buf_refs = {n: next(it) for n in names}
        wait_sems = [(next(it), next(it)) for _ in wait]
        if after is not None:
            next(it)
        start_sems = [(next(it), next(it)) for _ in start]
        for _ in names:
            next(it)
        token = next(it)
        pos = _my_position()

        def descriptors(stream, sems):
            return [pltpu.make_async_remote_copy(src_ref=src, dst_ref=dst, send_sem=sems[0].at[k], recv_sem=sems[1].at[k],
                                                 device_id=dev, device_id_type=MESH_ID)
                    for k, (src, dst, dev) in enumerate(stream.plan(buf_refs, pos))]

        for (stream, _), sems in zip(wait, wait_sems):
            for cp in descriptors(stream, sems):
                cp.wait_send()
                cp.wait_recv()
        for stream, sems in zip(start, start_sems):
            for cp in descriptors(stream, sems):
                cp.start()
        token[...] = jnp.zeros_like(token)

    hbm = pl.BlockSpec(memory_space=pltpu.HBM)
    sem = pl.BlockSpec(memory_space=pltpu.SEMAPHORE)
    operands = [pltpu.with_memory_space_constraint(bufs[n], pltpu.HBM) for n in names]
    in_specs = [hbm] * len(names)
    for _, (send_sems, recv_sems) in wait:
        operands += [send_sems, recv_sems]
        in_specs += [sem, sem]
    if after is not None:
        operands.append(after)
        in_specs.append(pl.BlockSpec(memory_space=pl.ANY))
    out_shape, out_specs = [], []
    for stream in start:
        out_shape += [pltpu.SemaphoreType.DMA((stream.n,)), pltpu.SemaphoreType.DMA((stream.n,))]
        out_specs += [sem, sem]
    n_sem_out = len(out_shape)
    out_shape += [pltpu.HBM(bufs[n].shape, bufs[n].dtype) for n in names]
    out_specs += [hbm] * len(names)
    out_shape.append(jax.ShapeDtypeStruct((8, 128), F32))
    out_specs.append(pl.BlockSpec(memory_space=pltpu.VMEM))
    outs = pl.pallas_call(
        body, name=name, out_shape=out_shape, in_specs=in_specs, out_specs=out_specs,
        input_output_aliases={i: n_sem_out + i for i in range(len(names))},
        compiler_params=pltpu.CompilerParams(has_side_effects=pltpu.SideEffectType.DATAFLOW_SIDE_EFFECTING),
    )(*operands)
    sems = [(outs[2 * i], outs[2 * i + 1]) for i in range(len(start))]
    return dict(zip(names, outs[n_sem_out:n_sem_out + len(names)])), sems, outs[-1]


def _with_own_slot(block, me):
    return lax.dynamic_update_index_in_dim(lax.empty((N_DEV,) + block.shape, block.dtype), block, me, 0)


def _other_chips(pos):
    mx, my, _, _ = pos
    return [(1 - mx if i & 2 else mx, 1 - my if i & 1 else my) for i in (1, 2, 3)]


def _dev_index(px, py, pc):
    return 4 * px + 2 * py + pc


def _gather_streams(keys):
    def to_chips(refs, pos):
        _, _, mc, me = pos
        return [(refs["s_" + k], refs["g_" + k].at[me], (cx, cy, mc)) for k in keys for cx, cy in _other_chips(pos)]

    def to_sibling(refs, pos):
        mx, my, mc, me = pos
        return [(refs["s_" + k], refs["g_" + k].at[me], (mx, my, 1 - mc)) for k in keys]

    def pass_on(refs, pos):
        mx, my, mc, _ = pos
        out = []
        for k in keys:
            for cx, cy in _other_chips(pos):
                slot = refs["g_" + k].at[_dev_index(cx, cy, mc)]
                out.append((slot, slot, (mx, my, 1 - mc)))
        return out

    return _Stream(3 * len(keys), to_chips), _Stream(len(keys), to_sibling), _Stream(3 * len(keys), pass_on)


def _direct_gather_stream(key):
    def plan(refs, pos):
        mx, my, mc, me = pos
        return [(refs["s_" + key], refs["g_" + key].at[me], _peer(mx, my, mc, k)[0]) for k in range(1, N_DEV)]

    return _Stream(N_DEV - 1, plan)


def _scatter_streams(keys):
    def pair(refs, pos):
        mx, my, mc, _ = pos
        sib = (mx, my, 1 - mc)
        out = []
        for k in keys:
            for i, (cx, cy) in enumerate(_other_chips(pos)):
                out.append((refs["g_" + k].at[_dev_index(cx, cy, 1 - mc)], refs["st_" + k].at[i], sib))
            out.append((refs["g_" + k].at[_dev_index(mx, my, 1 - mc)], refs["st_" + k].at[3], sib))
        return out

    def chips(refs, pos):
        mc = pos[2]
        return [(refs["ps_" + k].at[i], refs["ld_" + k].at[i], (cx, cy, mc))
                for k in keys for i, (cx, cy) in enumerate(_other_chips(pos))]

    return _Stream(4 * len(keys), pair), _Stream(3 * len(keys), chips)


def _pair_sum(g, st, idx, tr, name):
    _, rows, cols = g.shape

    def body(idx_ref, g_ref, st_ref, out_ref):
        out_ref[...] = (g_ref[...].astype(F32) + st_ref[...].astype(F32)).astype(out_ref.dtype)

    return pl.pallas_call(
        body, name=name,
        grid_spec=pltpu.PrefetchScalarGridSpec(
            num_scalar_prefetch=1, grid=(4, rows // tr),
            in_specs=[pl.BlockSpec((None, tr, cols), lambda j, i, idx_ref: (idx_ref[j], i, 0)),
                      pl.BlockSpec((None, tr, cols), lambda j, i, idx_ref: (j, i, 0))],
            out_specs=pl.BlockSpec((None, tr, cols), lambda j, i, idx_ref: (j, i, 0))),
        out_shape=jax.ShapeDtypeStruct((4, rows, cols), WIRE_DTYPE),
        compiler_params=_params(dimension_semantics=("parallel", "parallel")),
    )(idx, g, st)


def _ada_fwd(c_all, w_ada, b_cols):
    def body(c_ref, w_ref, b_ref, out_ref):
        cv = c_ref[...]
        ca = cv * _sigmoid(cv)
        for l in range(DEPTH):
            out_ref[l] = jnp.dot(ca, w_ref[l], precision=HIGHEST, preferred_element_type=F32) + b_ref[l:l + 1, :]

    return pl.pallas_call(
        body, name="ada_fwd",
        out_shape=jax.ShapeDtypeStruct((DEPTH, N_DEV, ADA_COLS), F32),
        compiler_params=_params(),
    )(c_all, w_ada, b_cols)


def _ada_bwd(c_all, d_cols):
    def body(c_ref, d_ref, out_ref):
        cv = c_ref[...]
        ca = cv * _sigmoid(cv)
        for l in range(DEPTH):
            out_ref[l] = lax.dot_general(ca, d_ref[l], (((0,), (0,)), ((), ())), precision=HIGHEST,
                                         preferred_element_type=F32)

    return pl.pallas_call(
        body, name="ada_bwd",
        out_shape=jax.ShapeDtypeStruct((DEPTH, D_MODEL, ADA_COLS), F32),
        compiler_params=_params(),
    )(c_all, d_cols)


def _lower_bounds(logits):
    m = jnp.maximum(logits[0:1], logits[1:2])
    e0, e1 = jnp.exp(logits[0:1] - m), jnp.exp(logits[1:2] - m)
    den = e0 + e1
    p0, p1 = e0 / den, e1 / den
    low0 = p0 - p0
    low1 = (p0 + p1) - p0
    return (p0, p1), (low0, low1)


def _lb_fwd(lb_logits):
    def body(lg_ref, out_ref):
        _, (low0, low1) = _lower_bounds(lg_ref[...])
        out_ref[0:1, :] = jnp.clip(low0, 0.0, 1.0)
        out_ref[1:2, :] = jnp.clip(low1, 0.0, 1.0)

    return pl.pallas_call(body, name="lb_fwd", out_shape=jax.ShapeDtypeStruct(lb_logits.shape, F32),
                          compiler_params=_params())(lb_logits)


def _row_spec(cols=D_MODEL):
    return pl.BlockSpec((1, cols), lambda *_: (0, 0))


def _prenorm_fwd(x, g, shift, scale, tm, name, after=None):
    seq = x.shape[0]

    def body(x_ref, g_ref, sh_ref, sc_ref, h_ref):
        xv = x_ref[...]
        rs = lax.rsqrt(jnp.mean(xv * xv, axis=-1, keepdims=True) + NORM_EPS)
        h = (xv * rs * g_ref[...]) * (1.0 + sc_ref[...]) + sh_ref[...]
        h_ref[...] = h.astype(h_ref.dtype)

    tile = pl.BlockSpec((tm, D_MODEL), lambda i: (i, 0))
    return _pallas_after(
        body, 4, after, name=name, grid=(seq // tm,),
        in_specs=[tile, _row_spec(), _row_spec(), _row_spec()], out_specs=tile,
        out_shape=jax.ShapeDtypeStruct((seq, D_MODEL), MXU_DTYPE),
        compiler_params=_params(dimension_semantics=("parallel",)),
    )(x, g, shift, scale)


def _in_proj(h, win_g, tm, name, after=None):
    seq = h.shape[0]

    def body(h_ref, w_ref, z_ref):
        z_ref[...] = jnp.dot(h_ref[...], w_ref[...], preferred_element_type=F32)

    return _pallas_after(
        body, 2, after, name=name, grid=(N_DEV, seq // tm),
        in_specs=[pl.BlockSpec((tm, D_MODEL), lambda j, i: (i, 0)),
                  pl.BlockSpec((None, D_MODEL, IN_COLS), lambda j, i: (j, 0, 0))],
        out_specs=pl.BlockSpec((tm, IN_COLS), lambda j, i: (i, j)),
        out_shape=jax.ShapeDtypeStruct((seq, IN_WIDTH), F32),
        compiler_params=_params(dimension_semantics=("parallel", "parallel")),
    )(h, win_g)


def _shift_down(v, j, pos):
    return jnp.where(pos >= j, pltpu.roll(v, j, 0), 0.0)


def _shift_up(v, j, pos, seq):
    return jnp.where(pos < seq - j, pltpu.roll(v, seq - j, 0), 0.0)


def _select_window(g, candidates):
    out = candidates[-1]
    for i in range(len(candidates) - 2, -1, -1):
        out = jnp.where(g == i, candidates[i], out)
    return out


def _pool_mean_minus_token(u, g, pos):
    sums, acc = [], u
    for j in (1, 2, 4, 8):
        acc = acc + _shift_down(acc, j, pos)
        sums.append(acc)
    wsum = _select_window(g, sums)
    width = jnp.left_shift(2, g).astype(F32)
    count = jnp.minimum(pos.astype(F32) + 1.0, width)
    return wsum / count - u, count


def _pool_fwd(z, pool_w_l, pool_scale_l, name, after=None):
    seq = z.shape[0]

    def body(pv_ref, pg_ref, w_ref, sc_ref, out_ref):
        g = pl.program_id(0)
        pos = lax.broadcasted_iota(jnp.int32, (seq, GROUP_DIM), 0)
        pm, _ = _pool_mean_minus_token(pv_ref[...], g, pos)
        lin = _dot(pm, w_ref[...]) * sc_ref[...]
        pg = pg_ref[...]
        out_ref[...] = (lin * (pg * _sigmoid(pg))).astype(out_ref.dtype)

    return _pallas_after(
        body, 4, after, name=name, grid=(POOL_GROUPS,),
        in_specs=[pl.BlockSpec((seq, GROUP_DIM), lambda g: (0, g)),
                  pl.BlockSpec((seq, GROUP_DIM), lambda g: (0, POOL_GROUPS + g)),
                  pl.BlockSpec((None, GROUP_DIM, GROUP_DIM), lambda g: (g, 0, 0)),
                  pl.BlockSpec((1, GROUP_DIM), lambda g: (0, g))],
        out_specs=pl.BlockSpec((seq, GROUP_DIM), lambda g: (0, g)),
        out_shape=jax.ShapeDtypeStruct((seq, POOL_WIDTH), MXU_DTYPE),
        compiler_params=_params(dimension_semantics=("parallel",)),
    )(z, z, pool_w_l, pool_scale_l)


def _chunk_masks():
    row = lax.broadcasted_iota(jnp.int32, (CHUNK, CHUNK), 0)
    col = lax.broadcasted_iota(jnp.int32, (CHUNK, CHUNK), 1)
    causal = row >= col
    before_sub = col < (row // SUB) * SUB
    suffix = row <= col
    return causal, before_sub, suffix


def _masked_sums(masks, v):
    lhs = jnp.concatenate([m.astype(jnp.bfloat16) for m in masks], axis=0)
    hi = v.astype(jnp.bfloat16)
    rest = v - hi.astype(F32)
    mid = rest.astype(jnp.bfloat16)
    lo = (rest - mid.astype(F32)).astype(jnp.bfloat16)
    out = jnp.dot(lhs, hi, preferred_element_type=F32)
    out += jnp.dot(lhs, mid, preferred_element_type=F32)
    out += jnp.dot(lhs, lo, preferred_element_type=F32)
    return [out[i * CHUNK:(i + 1) * CHUNK] for i in range(len(masks))]


def _gates(zf, lb):
    sg = _sigmoid(zf)
    f = lb + (1.0 - lb) * sg
    logf = jnp.log(jnp.maximum(f, LOG_FLOOR))
    return sg, f, logf


def _intra_blocks(q_h, k_h, cum_h, base_h, causal):
    rel = cum_h - base_h
    out = []
    for i in range(N_SUB):
        rows = slice(i * SUB, (i + 1) * SUB)
        e_q = jnp.exp(rel[rows])
        base_i = jnp.concatenate([base_h[rows]] * N_SUB, axis=0)
        e_k = jnp.exp(jnp.minimum(base_i - cum_h, EXP_CLAMP))
        q_t = (q_h[rows] * e_q).astype(MXU_DTYPE)
        k_t = (k_h * e_k).astype(MXU_DTYPE)
        a_i = jnp.where(causal[rows], _dot_nt(q_t, k_t), 0.0)
        out.append((q_t, k_t, e_q, e_k, a_i))
    return out


def _hgrn_fwd(z, lb_l, gn_l, name, after=None):
    seq = z.shape[0]
    n_chunks = seq // CHUNK

    def body(hq_ref, hf_ref, hi_ref, hg_ref, lb_ref, gn_ref, o_ref, bin_ref, st_ref, state):
        @pl.when(pl.program_id(0) == 0)
        def _():
            state[...] = jnp.zeros_like(state)

        causal, before_sub, _ = _chunk_masks()
        _, f, logf = _gates(hf_ref[...], lb_ref[...])
        kk = 1.0 - f
        hq = hq_ref[...]
        q = hq * _sigmoid(hq)
        cum, base = _masked_sums([causal, before_sub], logf)
        st_ref[0] = state[...]
        for h in range(HEADS):
            sl = slice(h * HEAD_DIM, (h + 1) * HEAD_DIM)
            q_h, k_h, cum_h = q[:, sl], kk[:, sl], cum[:, sl]
            v_h = hi_ref[:, sl]
            st_h = state[h]
            blocks = _intra_blocks(q_h, k_h, cum_h, base[:, sl], causal)
            a = jnp.concatenate([b[4] for b in blocks], axis=0)
            o_h = _dot_nt(q_h * jnp.exp(cum_h), st_h) + _dot(a, v_h)
            last = jnp.sum(logf[:, sl], axis=0, keepdims=True)
            state[h] = st_h * jnp.exp(last) + _dot_tn(v_h, k_h * jnp.exp(last - cum_h))
            rs = lax.rsqrt(jnp.mean(o_h * o_h, axis=-1, keepdims=True) + NORM_EPS)
            hg = hg_ref[:, sl]
            o_ref[:, sl] = o_h
            bin_ref[:, sl] = ((o_h * rs * gn_ref[...]) * (hg * _sigmoid(hg))).astype(bin_ref.dtype)

    def col(block):
        return pl.BlockSpec((CHUNK, D_MODEL), lambda c: (c, block))

    tile = pl.BlockSpec((CHUNK, D_MODEL), lambda c: (c, 0))
    return _pallas_after(
        body, 6, after, name=name, grid=(n_chunks,),
        in_specs=[col(COL_HQ), col(COL_HF), col(COL_HI), col(COL_HG), _row_spec(), _row_spec(HEAD_DIM)],
        out_specs=[tile, tile, pl.BlockSpec((1, HEADS, HEAD_DIM, HEAD_DIM), lambda c: (c, 0, 0, 0))],
        out_shape=[jax.ShapeDtypeStruct((seq, D_MODEL), F32),
                   jax.ShapeDtypeStruct((seq, D_MODEL), MXU_DTYPE),
                   jax.ShapeDtypeStruct((n_chunks, HEADS, HEAD_DIM, HEAD_DIM), F32)],
        scratch_shapes=[pltpu.VMEM((HEADS, HEAD_DIM, HEAD_DIM), F32)],
        compiler_params=_params(dimension_semantics=("arbitrary",)),
    )(z, z, z, z, lb_l, gn_l)


def _rms_parts(y):
    rs = lax.rsqrt(jnp.mean(y * y, axis=-1, keepdims=True) + NORM_EPS)
    return rs, y * rs


def _merge_fwd(a_in, b_in, z, x, wpo_g, who_g, wout_g, gate, g_post, tm, name):
    seq = x.shape[0]

    def body(a_ref, b_ref, mgp_ref, mgh_ref, x_ref, wpo_ref, who_ref, wout_ref, gate_ref, gp_ref,
             ba_ref, bb_ref, mer_ref, y_ref, xn_ref):
        a = a_ref[...]
        ba = jnp.concatenate([_dot(a, wpo_ref[j]) for j in range(N_DEV)], axis=1)
        bb = _dot(b_ref[...], who_ref[...])
        merged = _sigmoid(mgp_ref[...]) * ba + _sigmoid(mgh_ref[...]) * bb
        y = _dot(merged, wout_ref[...])
        _, yn = _rms_parts(y)
        ba_ref[...] = ba.astype(ba_ref.dtype)
        bb_ref[...] = bb.astype(bb_ref.dtype)
        mer_ref[...] = merged.astype(mer_ref.dtype)
        y_ref[...] = y
        xn_ref[...] = x_ref[...] + gate_ref[...] * (yn * gp_ref[...])

    def tile(cols=D_MODEL, block=0):
        return pl.BlockSpec((tm, cols), lambda i: (i, block))

    full = pl.BlockSpec((D_MODEL, D_MODEL), lambda i: (0, 0))
    act = jax.ShapeDtypeStruct((seq, D_MODEL), MXU_DTYPE)
    f32 = jax.ShapeDtypeStruct((seq, D_MODEL), F32)
    return pl.pallas_call(
        body, name=name, grid=(seq // tm,),
        in_specs=[tile(POOL_WIDTH), tile(), tile(block=COL_MGP), tile(block=COL_MGH), tile(),
                  pl.BlockSpec((N_DEV, POOL_WIDTH, GROUP_DIM), lambda i: (0, 0, 0)),
                  full, full, _row_spec(), _row_spec()],
        out_specs=[tile(), tile(), tile(), tile(), tile()],
        out_shape=[act, act, act, f32, f32],
        compiler_params=_params(dimension_semantics=("parallel",)),
    )(a_in, b_in, z, z, x, wpo_g, who_g, wout_g, gate, g_post)


def _loss_grad(x_out, target, tm):
    seq = x_out.shape[0]

    def body(x_ref, t_ref, loss_ref, dx_ref):
        @pl.when(pl.program_id(0) == 0)
        def _():
            loss_ref[...] = jnp.zeros_like(loss_ref)

        err = x_ref[...] - t_ref[...]
        per_token = jnp.mean(err * err, axis=-1, keepdims=True)
        loss_ref[...] += 0.5 * jnp.sum(per_token, axis=0, keepdims=True)
        dx_ref[...] = err * (1.0 / D_MODEL)

    tile = pl.BlockSpec((tm, D_MODEL), lambda i: (i, 0))
    return pl.pallas_call(
        body, name="loss_grad", grid=(seq // tm,),
        in_specs=[tile, tile],
        out_specs=[pl.BlockSpec((1, 1), lambda i: (0, 0)), tile],
        out_shape=[jax.ShapeDtypeStruct((1, 1), F32), jax.ShapeDtypeStruct((seq, D_MODEL), F32)],
        compiler_params=_params(dimension_semantics=("arbitrary",)),
    )(x_out, target)


def _merge_bwd(dx, y, ba, bb, z, wpo_g, who_g, wout_g, gate, g_post, tm, name):
    seq = dx.shape[0]

    def body(dx_ref, y_ref, ba_ref, bb_ref, mgp_ref, mgh_ref, wpo_ref, who_ref, wout_ref, gate_ref, gp_ref,
             dy_ref, dba_ref, dbb_ref, da_ref, db_ref, dmg_ref, acc_ref):
        @pl.when(pl.program_id(0) == 0)
        def _():
            acc_ref[...] = jnp.zeros_like(acc_ref)

        dxv = dx_ref[...]
        rs, yn = _rms_parts(y_ref[...])
        acc_ref[0:1, :] += jnp.sum(dxv * yn * gp_ref[...], axis=0, keepdims=True)
        acc_ref[1:2, :] += jnp.sum(dxv * gate_ref[...] * yn, axis=0, keepdims=True)
        dyn = dxv * (gate_ref[...] * gp_ref[...])
        dy = rs * (dyn - yn * jnp.mean(dyn * yn, axis=-1, keepdims=True))
        dmerged = _dot_nt(dy, wout_ref[...])
        sp, sh = _sigmoid(mgp_ref[...]), _sigmoid(mgh_ref[...])
        dba, dbb = sp * dmerged, sh * dmerged
        dmg_ref[:, 0:D_MODEL] = (dmerged * ba_ref[...].astype(F32) * sp * (1.0 - sp)).astype(dmg_ref.dtype)
        dmg_ref[:, D_MODEL:2 * D_MODEL] = (dmerged * bb_ref[...].astype(F32) * sh * (1.0 - sh)).astype(dmg_ref.dtype)
        da = _dot_nt(dba[:, 0:GROUP_DIM], wpo_ref[0])
        for j in range(1, N_DEV):
            da += _dot_nt(dba[:, j * GROUP_DIM:(j + 1) * GROUP_DIM], wpo_ref[j])
        dy_ref[...] = dy.astype(dy_ref.dtype)
        dba_ref[...] = dba.astype(dba_ref.dtype)
        dbb_ref[...] = dbb.astype(dbb_ref.dtype)
        da_ref[...] = da
        db_ref[...] = _dot_nt(dbb, who_ref[...])

    def tile(cols=D_MODEL, block=0):
        return pl.BlockSpec((tm, cols), lambda i: (i, block))

    full = pl.BlockSpec((D_MODEL, D_MODEL), lambda i: (0, 0))
    act = jax.ShapeDtypeStruct((seq, D_MODEL), MXU_DTYPE)
    return pl.pallas_call(
        body, name=name, grid=(seq // tm,),
        in_specs=[tile(), tile(), tile(), tile(), tile(block=COL_MGP), tile(block=COL_MGH),
                  pl.BlockSpec((N_DEV, POOL_WIDTH, GROUP_DIM), lambda i: (0, 0, 0)),
                  full, full, _row_spec(), _row_spec()],
        out_specs=[tile(), tile(), tile(), tile(POOL_WIDTH), tile(), tile(2 * D_MODEL),
                   pl.BlockSpec((8, D_MODEL), lambda i: (0, 0))],
        out_shape=[act, act, act, jax.ShapeDtypeStruct((seq, POOL_WIDTH), F32),
                   jax.ShapeDtypeStruct((seq, D_MODEL), F32),
                   jax.ShapeDtypeStruct((seq, 2 * D_MODEL), MXU_DTYPE),
                   jax.ShapeDtypeStruct((8, D_MODEL), F32)],
        compiler_params=_params(dimension_semantics=("arbitrary",)),
    )(dx, y, ba, bb, z, z, wpo_g, who_g, wout_g, gate, g_post)


def _grad_tn(a, b, tn, dev_major, name):
    seq, ka = a.shape
    n = b.shape[1]

    def body(a_ref, b_ref, out_ref):
        out_ref[...] = _dot_tn(a_ref[...], b_ref[...]).astype(out_ref.dtype)

    if dev_major:
        out_spec = pl.BlockSpec((None, ka, tn), lambda j: (j, 0, 0))
        out_shape = jax.ShapeDtypeStruct((n // tn, ka, tn), WIRE_DTYPE)
    else:
        out_spec = pl.BlockSpec((ka, tn), lambda j: (0, j))
        out_shape = jax.ShapeDtypeStruct((ka, n), WIRE_DTYPE)
    return pl.pallas_call(
        body, name=name, grid=(n // tn,),
        in_specs=[pl.BlockSpec((seq, ka), lambda j: (0, 0)), pl.BlockSpec((seq, tn), lambda j: (0, j))],
        out_specs=out_spec, out_shape=out_shape,
        compiler_params=_params(dimension_semantics=("parallel",)),
    )(a, b)


def _hgrn_bwd(db_in, z, o, states, lb_l, gn_l, name, after=None):
    seq = z.shape[0]
    n_chunks = seq // CHUNK

    def body(db_ref, hq_ref, hf_ref, hi_ref, hg_ref, o_ref, st_ref, lb_ref, gn_ref,
             dz_ref, dlb_ref, dgn_ref, dstate, dq_buf, dk_buf, dg_buf):
        @pl.when(pl.program_id(0) == 0)
        def _():
            dstate[...] = jnp.zeros_like(dstate)
            dlb_ref[...] = jnp.zeros_like(dlb_ref)
            dgn_ref[...] = jnp.zeros_like(dgn_ref)

        causal, before_sub, suffix = _chunk_masks()
        lb = lb_ref[...]
        sg, f, logf = _gates(hf_ref[...], lb)
        kk = 1.0 - f
        hq = hq_ref[...]
        sq = _sigmoid(hq)
        q = hq * sq
        cum, base = _masked_sums([causal, before_sub], logf)
        gn = gn_ref[...]
        dgn = jnp.zeros((1, HEAD_DIM), F32)
        dlast = []
        for h in range(HEADS):
            sl = slice(h * HEAD_DIM, (h + 1) * HEAD_DIM)
            q_h, k_h, cum_h = q[:, sl], kk[:, sl], cum[:, sl]
            v_h = hi_ref[:, sl]
            st_h = st_ref[0, h]
            dst_h = dstate[h]
            rs, ohat = _rms_parts(o_ref[:, sl])
            hg = hg_ref[:, sl]
            shg = _sigmoid(hg)
            d_bin = db_ref[:, sl]
            don = d_bin * (hg * shg)
            dgn += jnp.sum(don * ohat, axis=0, keepdims=True)
            dohat = don * gn
            do = rs * (dohat - ohat * jnp.mean(dohat * ohat, axis=-1, keepdims=True))
            dz_ref[:, 3 * D_MODEL + h * HEAD_DIM:3 * D_MODEL + (h + 1) * HEAD_DIM] = (
                d_bin * (ohat * gn) * _dsilu(hg, shg)).astype(dz_ref.dtype)
            last = jnp.sum(logf[:, sl], axis=0, keepdims=True)
            g_in = jnp.exp(cum_h)
            d_out = jnp.exp(last - cum_h)
            q_bar, k_bar = q_h * g_in, k_h * d_out
            blocks = _intra_blocks(q_h, k_h, cum_h, base[:, sl], causal)
            a = jnp.concatenate([b[4] for b in blocks], axis=0)
            da = jnp.where(causal, _dot_nt(do, v_h), 0.0)
            dv = _dot_tn(a, do) + _dot_nt(k_bar, dst_h)
            dq_bar, dk_bar = _dot(do, st_h), _dot(v_h, dst_h)
            dk = dk_bar * d_out
            dq_parts, dg_parts = [], []
            dg_k = k_bar * dk_bar
            dlast.append(jnp.sum(k_bar * dk_bar, axis=0, keepdims=True)
                         + jnp.exp(last) * jnp.sum(st_h * dst_h, axis=0, keepdims=True))
            for i, (q_t, k_t, e_q, e_k, _) in enumerate(blocks):
                da_i = da[i * SUB:(i + 1) * SUB].astype(MXU_DTYPE)
                dq_t = _dot(da_i, k_t)
                dk_t = _dot_tn(da_i, q_t)
                dq_parts.append(dq_t * e_q)
                dk += dk_t * e_k
                dg_parts.append(q_t.astype(F32) * dq_t)
                dg_k += k_t.astype(F32) * dk_t
            dq = dq_bar * g_in + jnp.concatenate(dq_parts, axis=0)
            dg_buf[:, sl] = q_bar * dq_bar + jnp.concatenate(dg_parts, axis=0) - dg_k
            dstate[h] = dst_h * jnp.exp(last) + _dot_tn(do, q_bar)
            dq_buf[:, sl] = dq
            dk_buf[:, sl] = dk
            dz_ref[:, 2 * D_MODEL + h * HEAD_DIM:2 * D_MODEL + (h + 1) * HEAD_DIM] = dv.astype(dz_ref.dtype)
        dgn_ref[...] += dgn
        dq_all, dk_all = dq_buf[...], dk_buf[...]
        dlogf = _masked_sums([suffix], dg_buf[...])[0] + jnp.concatenate(dlast, axis=1)
        df = jnp.where(f > LOG_FLOOR, dlogf / f, 0.0) - dk_all
        dlb_ref[...] += jnp.sum(df * (1.0 - sg), axis=0, keepdims=True)
        dz_ref[:, 0:D_MODEL] = (dq_all * _dsilu(hq, sq)).astype(dz_ref.dtype)
        dz_ref[:, D_MODEL:2 * D_MODEL] = (df * (1.0 - lb) * sg * (1.0 - sg)).astype(dz_ref.dtype)

    last_chunk = n_chunks - 1

    def col(block):
        return pl.BlockSpec((CHUNK, D_MODEL), lambda c: (last_chunk - c, block))

    return _pallas_after(
        body, 9, after, name=name, grid=(n_chunks,),
        in_specs=[col(0), col(COL_HQ), col(COL_HF), col(COL_HI), col(COL_HG), col(0),
                  pl.BlockSpec((1, HEADS, HEAD_DIM, HEAD_DIM), lambda c: (last_chunk - c, 0, 0, 0)),
                  _row_spec(), _row_spec(HEAD_DIM)],
        out_specs=[pl.BlockSpec((CHUNK, 4 * D_MODEL), lambda c: (last_chunk - c, 0)),
                   _row_spec(), _row_spec(HEAD_DIM)],
        out_shape=[jax.ShapeDtypeStruct((seq, 4 * D_MODEL), MXU_DTYPE),
                   jax.ShapeDtypeStruct((1, D_MODEL), F32), jax.ShapeDtypeStruct((1, HEAD_DIM), F32)],
        scratch_shapes=[pltpu.VMEM((HEADS, HEAD_DIM, HEAD_DIM), F32)] + [pltpu.VMEM((CHUNK, D_MODEL), F32)] * 3,
        compiler_params=_params(dimension_semantics=("arbitrary",)),
    )(db_in, z, z, z, z, o, states, lb_l, gn_l)


def _pool_bwd(da_in, z, pool_w_l, pool_scale_l, name, after=None):
    seq = z.shape[0]

    def body(da_ref, pv_ref, pg_ref, w_ref, sc_ref, dpv_ref, dpg_ref, dw_ref, dsc_ref):
        g = pl.program_id(0)
        pos = lax.broadcasted_iota(jnp.int32, (seq, GROUP_DIM), 0)
        pm, count = _pool_mean_minus_token(pv_ref[...], g, pos)
        lin0 = _dot(pm, w_ref[...])
        pg = pg_ref[...]
        spg = _sigmoid(pg)
        da = da_ref[...]
        dlin = da * (pg * spg)
        dpg_ref[...] = (da * (lin0 * sc_ref[...]) * _dsilu(pg, spg)).astype(dpg_ref.dtype)
        dsc_ref[...] = jnp.sum(dlin * lin0, axis=0, keepdims=True)
        dl0 = dlin * sc_ref[...]
        dw_ref[...] = _dot_tn(pm, dl0)
        dpm = _dot_nt(dl0, w_ref[...])
        sums, acc = [], dpm / count
        for j in (1, 2, 4, 8):
            acc = acc + _shift_up(acc, j, pos, seq)
            sums.append(acc)
        dpv_ref[...] = (_select_window(g, sums) - dpm).astype(dpv_ref.dtype)

    grp = pl.BlockSpec((seq, GROUP_DIM), lambda g: (0, g))
    return _pallas_after(
        body, 5, after, name=name, grid=(POOL_GROUPS,),
        in_specs=[grp, grp, pl.BlockSpec((seq, GROUP_DIM), lambda g: (0, POOL_GROUPS + g)),
                  pl.BlockSpec((None, GROUP_DIM, GROUP_DIM), lambda g: (g, 0, 0)),
                  pl.BlockSpec((1, GROUP_DIM), lambda g: (0, g))],
        out_specs=[grp, grp, pl.BlockSpec((None, GROUP_DIM, GROUP_DIM), lambda g: (g, 0, 0)),
                   pl.BlockSpec((1, GROUP_DIM), lambda g: (0, g))],
        out_shape=[jax.ShapeDtypeStruct((seq, POOL_WIDTH), MXU_DTYPE),
                   jax.ShapeDtypeStruct((seq, POOL_WIDTH), MXU_DTYPE),
                   jax.ShapeDtypeStruct((POOL_GROUPS, GROUP_DIM, GROUP_DIM), F32),
                   jax.ShapeDtypeStruct((1, POOL_WIDTH), F32)],
        compiler_params=_params(dimension_semantics=("parallel",)),
    )(da_in, z, z, pool_w_l, pool_scale_l)


def _in_proj_dw(h, dz, name, after=None):
    seq = h.shape[0]

    def body(h_ref, dz_ref, out_ref):
        out_ref[...] = lax.dot_general(h_ref[...], dz_ref[...], (((0,), (0,)), ((), ())),
                                       preferred_element_type=F32).astype(out_ref.dtype)

    return _pallas_after(
        body, 2, after, name=name, grid=(N_DEV,),
        in_specs=[pl.BlockSpec((seq, D_MODEL), lambda j: (0, 0)), pl.BlockSpec((seq, IN_COLS), lambda j: (0, j))],
        out_specs=pl.BlockSpec((None, D_MODEL, IN_COLS), lambda j: (j, 0, 0)),
        out_shape=jax.ShapeDtypeStruct((N_DEV, D_MODEL, IN_COLS), WIRE_DTYPE),
        compiler_params=_params(dimension_semantics=("parallel",)),
    )(h, dz)


def _in_proj_dh(dz, win_g, tm, name, after=None):
    seq = dz.shape[0]

    def body(dz_ref, w_ref, dh_ref):
        @pl.when(pl.program_id(1) == 0)
        def _():
            dh_ref[...] = jnp.zeros_like(dh_ref)

        dh_ref[...] += lax.dot_general(dz_ref[...], w_ref[...], (((1,), (1,)), ((), ())),
                                       preferred_element_type=F32)

    return _pallas_after(
        body, 2, after, name=name, grid=(seq // tm, N_DEV),
        in_specs=[pl.BlockSpec((tm, IN_COLS), lambda i, j: (i, j)),
                  pl.BlockSpec((None, D_MODEL, IN_COLS), lambda i, j: (j, 0, 0))],
        out_specs=pl.BlockSpec((tm, D_MODEL), lambda i, j: (i, 0)),
        out_shape=jax.ShapeDtypeStruct((seq, D_MODEL), F32),
        compiler_params=_params(dimension_semantics=("parallel", "arbitrary")),
    )(dz, win_g)


def _prenorm_bwd(x, dh, dx_res, g, scale, tm, name, after=None):
    seq = x.shape[0]

    def body(x_ref, dh_ref, dxr_ref, g_ref, sc_ref, dx_ref, acc_ref):
        @pl.when(pl.program_id(0) == 0)
        def _():
            acc_ref[...] = jnp.zeros_like(acc_ref)

        rs, xn = _rms_parts(x_ref[...])
        dh = dh_ref[...]
        acc_ref[0:1, :] += jnp.sum(dh, axis=0, keepdims=True)
        acc_ref[1:2, :] += jnp.sum(dh * (xn * g_ref[...]), axis=0, keepdims=True)
        dhn = dh * (1.0 + sc_ref[...])
        acc_ref[2:3, :] += jnp.sum(dhn * xn, axis=0, keepdims=True)
        dxn = dhn * g_ref[...]
        dx_ref[...] = rs * (dxn - xn * jnp.mean(dxn * xn, axis=-1, keepdims=True)) + dxr_ref[...]

    tile = pl.BlockSpec((tm, D_MODEL), lambda i: (i, 0))
    return _pallas_after(
        body, 5, after, name=name, grid=(seq // tm,),
        in_specs=[tile, tile, tile, _row_spec(), _row_spec()],
        out_specs=[tile, pl.BlockSpec((8, D_MODEL), lambda i: (0, 0))],
        out_shape=[jax.ShapeDtypeStruct((seq, D_MODEL), F32), jax.ShapeDtypeStruct((8, D_MODEL), F32)],
        compiler_params=_params(dimension_semantics=("arbitrary",)),
    )(x, dh, dx_res, g, scale)


def _adamw_math(w, g, m, v):
    m = ADAM_B1 * m + (1.0 - ADAM_B1) * g
    v = ADAM_B2 * v + (1.0 - ADAM_B2) * (g * g)
    m_hat = m / (1.0 - ADAM_B1 ** ADAM_STEP)
    v_hat = v / (1.0 - ADAM_B2 ** ADAM_STEP)
    delta = -ADAM_LR * (m_hat / (jnp.sqrt(v_hat) + ADAM_EPS) + ADAM_WD * w)
    return delta, m, v


def _adamw_sharded(w, m, v, contrib, tr, name):
    depth, rows, cols = w.shape
    n_parts = contrib.shape[1]

    def body(w_ref, m_ref, v_ref, c_ref, g_ref, d_ref, mo_ref, vo_ref):
        g = c_ref[0].astype(F32)
        for p in range(1, n_parts):
            g += c_ref[p].astype(F32)
        delta, mn, vn = _adamw_math(w_ref[...], g, m_ref[...], v_ref[...])
        g_ref[...] = g
        d_ref[...] = delta
        mo_ref[...] = mn
        vo_ref[...] = vn

    tile = pl.BlockSpec((None, tr, cols), lambda l, i: (l, i, 0))
    shape = jax.ShapeDtypeStruct(w.shape, F32)
    return pl.pallas_call(
        body, name=name, grid=(depth, rows // tr),
        in_specs=[tile, tile, tile, pl.BlockSpec((None, n_parts, tr, cols), lambda l, i: (l, 0, i, 0))],
        out_specs=[tile] * 4, out_shape=[shape] * 4,
        compiler_params=_params(dimension_semantics=("parallel", "parallel")),
    )(w, m, v, contrib)


def _adamw_layer(w, m, v, contribs, l, tr, name, prev=None):
    _, rows, cols = w.shape
    n = len(contribs)

    def body(*refs):
        w_ref, m_ref, v_ref = refs[:3]
        c_refs = refs[3:3 + n]
        g_ref, d_ref, mo_ref, vo_ref = refs[-4:]
        g = c_refs[0][...].astype(F32)
        for c_ref in c_refs[1:]:
            g += c_ref[...].astype(F32)
        delta, mn, vn = _adamw_math(w_ref[...], g, m_ref[...], v_ref[...])
        g_ref[...] = g
        d_ref[...] = delta
        mo_ref[...] = mn
        vo_ref[...] = vn

    tile = pl.BlockSpec((None, tr, cols), lambda i: (l, i, 0))
    in_specs = [tile, tile, tile] + [pl.BlockSpec((None, tr, cols), lambda i, s=slot: (s, i, 0)) for _, slot in contribs]
    operands = [w, m, v] + [arr for arr, _ in contribs]
    aliases = {}
    if prev is not None:
        aliases = {len(operands) + k: k for k in range(4)}
        in_specs += [pl.BlockSpec(memory_space=pl.ANY)] * 4
        operands += list(prev)
    shape = jax.ShapeDtypeStruct(w.shape, F32)
    return pl.pallas_call(
        body, name=name, grid=(rows // tr,), in_specs=in_specs, out_specs=[tile] * 4, out_shape=[shape] * 4,
        input_output_aliases=aliases,
        compiler_params=_params(dimension_semantics=("parallel",)),
    )(*operands)


def _adamw_small(w_pack, m_pack, v_pack, g_late, g_early, shapes):
    pieces, r = {}, 0
    for name, _, n in _SMALL_ROWS:
        pieces.setdefault(name, []).append((r, n))
        r += n
    names = list(pieces)

    def body(w_ref, m_ref, v_ref, gl_ref, ge_ref, *rest):
        outs, packs = rest[:4 * len(names)], rest[4 * len(names):]
        g_l, g_e = gl_ref[0][0:SMALL_LATE_ROWS], ge_ref[0]
        for d in range(1, N_DEV):
            g_l += gl_ref[d][0:SMALL_LATE_ROWS]
            g_e += ge_ref[d]
        g = jnp.concatenate([g_l, g_e], axis=0)
        w = w_ref[...]
        r0, r1, r2 = LB_ROW0, LB_ROW0 + 8, LB_ROW0 + 16
        lg0, lg1 = w[r0:r1], w[r1:r2]
        mx = jnp.maximum(lg0, lg1)
        e0, e1 = jnp.exp(lg0 - mx), jnp.exp(lg1 - mx)
        p0, p1 = e0 / (e0 + e1), e1 / (e0 + e1)
        low = ((p0 - p0), (p0 + p1) - p0)
        dlow = [g_rows * jnp.where((lo > 0.0) & (lo < 1.0), 1.0, jnp.where((lo == 0.0) | (lo == 1.0), 0.5, 0.0))
                for g_rows, lo in ((g[r0:r1], low[0]), (g[r1:r2], low[1]))]
        dp0 = (dlow[0] + dlow[1]) - (dlow[0] + dlow[1])
        dp1 = dlow[1]
        inner = p0 * dp0 + p1 * dp1
        g = jnp.concatenate([g[:r0], p0 * (dp0 - inner), p1 * (dp1 - inner), g[r2:]], axis=0)
        delta, mn, vn = _adamw_math(w, g, m_ref[...], v_ref[...])
        for kind, val in enumerate((g, delta, mn, vn)):
            packs[kind][...] = val
            for j, name in enumerate(names):
                at = 0
                for start, n in pieces[name]:
                    outs[kind * len(names) + j][at:at + n, :] = packs[kind][start:start + n, :]
                    at += n

    rows = {name: sum(n for _, n in pieces[name]) for name in names}
    outs = pl.pallas_call(
        body, name="adamw_small",
        out_shape=[jax.ShapeDtypeStruct((rows[name], 128), F32) for _ in range(4) for name in names],
        scratch_shapes=[pltpu.VMEM(w_pack.shape, F32)] * 4, compiler_params=_params(),
    )(w_pack, m_pack, v_pack, g_late, g_early)
    return [{name: outs[kind * len(names) + j].reshape(shapes[name]) for j, name in enumerate(names)}
            for kind in range(4)]


def _pack_small(parts, first=0, last=len(_SMALL_ROWS)):
    rows = [(parts[name] if l is None else parts[name][l]).reshape(n, 128) for name, l, n in _SMALL_ROWS[first:last]]
    if last == len(_SMALL_ROWS):
        rows.append(jnp.zeros((SMALL_ROWS_PAD - sum(n for _, _, n in _SMALL_ROWS), 128), F32))
    return jnp.concatenate(rows, axis=0)


def kernel(x, c, w_ada, b_ada, g_pre, g_post, w_in, pool_w, pool_scale, lb_logits, hgrn_norm_g, w_pool_o, w_hgrn_o, w_out, loss_target, m_w_ada, m_b_ada, m_g_pre, m_g_post, m_w_in, m_pool_w, m_pool_scale, m_lb_logits, m_hgrn_norm_g, m_w_pool_o, m_w_hgrn_o, m_w_out, v_w_ada, v_b_ada, v_g_pre, v_g_post, v_w_in, v_pool_w, v_pool_scale, v_lb_logits, v_hgrn_norm_g, v_w_pool_o, v_w_hgrn_o, v_w_out):
    seq = x.shape[1]
    tm = min(512, seq)
    tm_merge = min(256, seq)
    pos = _my_position()
    me = pos[3]

    c_all = _allgather_small(c, "allgather_c").reshape(N_DEV, D_MODEL)
    b_cols = lax.dynamic_slice_in_dim(b_ada, me * ADA_COLS, ADA_COLS, axis=1)
    ada_part = _ada_fwd(c_all, w_ada, b_cols)
    ada_all = _allgather_small(ada_part.reshape(DEPTH * N_DEV, ADA_COLS), "allgather_ada")
    ada = lax.dynamic_index_in_dim(ada_all.reshape(N_DEV, DEPTH, N_DEV, ADA_COLS), me, axis=2, keepdims=False)
    ada = jnp.transpose(ada, (1, 0, 2)).reshape(DEPTH, 3 * D_MODEL)
    shift = [ada[l:l + 1, 0:D_MODEL] for l in range(DEPTH)]
    scale = [ada[l:l + 1, D_MODEL:2 * D_MODEL] for l in range(DEPTH)]
    gate = [ada[l:l + 1, 2 * D_MODEL:] for l in range(DEPTH)]

    big = dict(win=w_in, wpo=w_pool_o, who=w_hgrn_o, wout=w_out)
    units = [["win0"], ["wpo0", "who0", "wout0"], ["win1", "wpo1", "who1", "wout1"]]
    g_streams = [_gather_streams(keys) for keys in units]
    g_state = [None] * len(units)

    def gather_start(u, after):
        bufs = {}
        for k in units[u]:
            arr = big[k[:-1]]
            bufs["s_" + k] = arr[int(k[-1])].astype(WIRE_DTYPE)
            bufs["g_" + k] = _with_own_slot(bufs["s_" + k], me)
        bufs, sems, token = _comm_call(f"gather_start_{u}", bufs, start=list(g_streams[u][:2]), after=after)
        g_state[u] = dict(bufs=bufs, sems=sems)
        return token

    def gather_pass(u, after):
        st = g_state[u]
        to_chips, _, pass_on = g_streams[u]
        st["bufs"], (st["pass_sems"],), _ = _comm_call(f"gather_pass_{u}", st["bufs"], start=[pass_on],
                                                       wait=[(to_chips, st["sems"][0])], after=after)

    def gather_done(u, after=None):
        st = g_state[u]
        _, to_sibling, pass_on = g_streams[u]
        bufs, _, _ = _comm_call(f"gather_done_{u}", st["bufs"], after=after,
                                wait=[(to_sibling, st["sems"][1]), (pass_on, st["pass_sems"])])
        return {k: bufs["g_" + k] for k in units[u]}

    token = gather_start(0, ada_all)

    lb = _lb_fwd(lb_logits)

    gw = {}
    xs, saved = [x[0]], []
    for l in range(DEPTH):
        h = _prenorm_fwd(xs[l], g_pre[l:l + 1], shift[l], scale[l], tm, f"prenorm_fwd_{l}",
                         after=token if l == 0 else None)
        token = None
        if l == 0:
            gather_pass(0, h)
            gw.update(gather_done(0))
            token = gather_start(1, gw["win0"])
        else:
            gather_pass(2, h)
            gw.update(gather_done(2))
        z = _in_proj(h, gw[f"win{l}"], seq, f"in_proj_{l}", after=token)
        if l == 0:
            gather_pass(1, z)
            token = gather_start(2, g_state[1]["bufs"]["g_wpo0"])
        a_in = _pool_fwd(z, pool_w[l], pool_scale[l:l + 1], f"pool_fwd_{l}", after=token)
        o, b_in, states = _hgrn_fwd(z, lb[l:l + 1], hgrn_norm_g[l:l + 1], f"hgrn_fwd_{l}", after=token)
        if l == 0:
            gw.update(gather_done(1, b_in))
        who_l = gw[f"who{l}"].reshape(D_MODEL, D_MODEL)
        wout_l = gw[f"wout{l}"].reshape(D_MODEL, D_MODEL)
        ba, bb, merged, y, x_next = _merge_fwd(a_in, b_in, z, xs[l], gw[f"wpo{l}"], who_l, wout_l, gate[l],
                                               g_post[l:l + 1], tm_merge, f"merge_fwd_{l}")
        xs.append(x_next)
        saved.append((h, z, a_in, o, b_in, states, ba, bb, merged, y, who_l, wout_l))

    loss_part, dx = _loss_grad(xs[DEPTH], loss_target[0], tm)

    chips = _other_chips(pos)
    pair_idx = jnp.stack([_dev_index(cx, cy, pos[2]) for cx, cy in chips] + [me]).astype(jnp.int32)
    pair_rows = dict(win=256, wpo=POOL_WIDTH, who=HEAD_DIM, wout=HEAD_DIM)

    def scatter_pair_start(u, grads):
        keys = list(grads)
        pair, to_chips = _scatter_streams(keys)
        bufs = {}
        for k in keys:
            bufs["g_" + k] = grads[k]
            bufs["st_" + k] = lax.empty((4,) + grads[k].shape[1:], WIRE_DTYPE)
        bufs, (sems,), token = _comm_call(f"scatter_pair_start_{u}", bufs, start=[pair])
        return dict(u=u, keys=keys, pair=pair, to_chips=to_chips, bufs=bufs, sems=sems, token=token)

    def scatter_pair_finish(st, after):
        u, keys = st["u"], st["keys"]
        bufs, _, _ = _comm_call(f"scatter_pair_done_{u}", st["bufs"], wait=[(st["pair"], st["sems"])], after=after)
        bufs2 = {}
        for k in keys:
            bufs2["ps_" + k] = _pair_sum(bufs["g_" + k], bufs["st_" + k], pair_idx, bufs["g_" + k].shape[1],
                                         f"pair_sum_{k}")
            bufs2["ld_" + k] = lax.empty((3,) + bufs["g_" + k].shape[1:], WIRE_DTYPE)
        st.update(bufs=bufs2)

    def scatter_chips_start(st, after=None):
        bufs2, (sems,), token = _comm_call(f"scatter_chips_start_{st['u']}", st["bufs"], start=[st["to_chips"]],
                                           after=after)
        st.update(bufs=bufs2, sems=sems, token=token)

    def scatter_finish(st, after):
        bufs, _, _ = _comm_call(f"scatter_chips_done_{st['u']}", st["bufs"], wait=[(st["to_chips"], st["sems"])],
                                after=after)
        return {k: [(bufs["ps_" + k], 3), (bufs["ld_" + k], 0), (bufs["ld_" + k], 1), (bufs["ld_" + k], 2)]
                for k in st["keys"]}

    d_ada, small, scat = [None] * DEPTH, [None] * DEPTH, {}
    for l in reversed(range(DEPTH)):
        h, z, a_in, o, b_in, states, ba, bb, merged, y, who_l, wout_l = saved[l]
        dy, dba, dbb, da_in, db_in, dmg, acc_post = _merge_bwd(
            dx, y, ba, bb, z, gw[f"wpo{l}"], who_l, wout_l, gate[l], g_post[l:l + 1], tm_merge, f"merge_bwd_{l}")
        g_small = {
            f"wout{l}": _grad_tn(merged, dy, 512, False, f"grad_w_out_{l}").reshape(N_DEV, HEAD_DIM, D_MODEL),
            f"who{l}": _grad_tn(b_in, dbb, 512, False, f"grad_w_hgrn_o_{l}").reshape(N_DEV, HEAD_DIM, D_MODEL),
            f"wpo{l}": _grad_tn(a_in, dba, GROUP_DIM, True, f"grad_w_pool_o_{l}")}
        st_small = scat[f"small{l}"] = scatter_pair_start(f"small{l}", g_small)
        dzh, dlb, dgn = _hgrn_bwd(db_in, z, o, states, lb[l:l + 1], hgrn_norm_g[l:l + 1], f"hgrn_bwd_{l}",
                                  after=st_small["token"])
        scatter_pair_finish(st_small, dzh)
        scatter_chips_start(st_small)
        dpv, dpg, dpw, dps = _pool_bwd(da_in, z, pool_w[l], pool_scale[l:l + 1], f"pool_bwd_{l}",
                                       after=st_small["token"])
        dz = jnp.concatenate([dpv, dpg, dzh, dmg], axis=1)
        small[l] = dict(g_post=acc_post[1], pool_w=dpw, pool_scale=dps[0], lb_logits=dlb[0], hgrn_norm_g=dgn[0])
        token = None
        if l == 0:
            parts = {name: jnp.stack([small[0][name], small[1][name]]) for name in small[0]}
            parts.update(b_ada=[None, d_ada[1]], g_pre=[None, small[1]["g_pre"]])
            sg_stream = _direct_gather_stream("sg")
            early = _pack_small(parts, 2)
            sg_bufs, (sg_sems,), token = _comm_call(
                "small_grads_start", dict(s_sg=early, g_sg=_with_own_slot(early, me)), start=[sg_stream])
        st_win = scat[f"win{l}"] = scatter_pair_start(
            f"win{l}", {f"win{l}": _in_proj_dw(h, dz, f"grad_w_in_{l}", after=token)})
        dh = _in_proj_dh(dz, gw[f"win{l}"], seq, f"in_proj_dh_{l}", after=st_win["token"])
        scatter_pair_finish(st_win, dh)
        if l > 0:
            scatter_chips_start(st_win)
        dx, acc_pre = _prenorm_bwd(xs[l], dh, dx, g_pre[l:l + 1], scale[l], tm, f"prenorm_bwd_{l}",
                                   after=st_win["token"])
        d_ada[l] = jnp.concatenate([acc_pre[0], acc_pre[1], acc_post[0]])
        small[l]["g_pre"] = acc_pre[2]
    grad_x = dx[None]

    parts = dict(b_ada=[d_ada[0]], g_pre=[small[0]["g_pre"]])
    late = jnp.concatenate([_pack_small(parts, 0, 2), jnp.broadcast_to(loss_part, (8, 128))], axis=0)
    lg_stream = _direct_gather_stream("lg")
    lg_bufs, (lg_sems,), token = _comm_call(
        "late_grads_start", dict(s_lg=late, g_lg=_with_own_slot(late, me)), start=[lg_stream])
    scatter_chips_start(scat["win0"], after=token)

    moments = dict(win=(m_w_in, v_w_in), wpo=(m_w_pool_o, v_w_pool_o), who=(m_w_hgrn_o, v_w_hgrn_o),
                   wout=(m_w_out, v_w_out))
    big_out = {}

    def finish_unit(unit, after):
        for k, contribs in scatter_finish(scat[unit], after).items():
            wname, l = k[:-1], int(k[-1])
            big_out[wname] = _adamw_layer(big[wname], moments[wname][0], moments[wname][1], contribs, l,
                                          pair_rows[wname], f"adamw_{k}", prev=big_out.get(wname))
            after = big_out[wname][0]
        return after

    after = scat["win0"]["token"]
    for unit in ("small1", "win1", "small0"):
        after = finish_unit(unit, after)

    lg_bufs, _, _ = _comm_call("late_grads_done", lg_bufs, wait=[(lg_stream, lg_sems)], after=after)
    g_late = lg_bufs["g_lg"]
    loss = jnp.sum(g_late[:, SMALL_LATE_ROWS, 0])
    sg_bufs, _, _ = _comm_call("small_grads_done", sg_bufs, wait=[(sg_stream, sg_sems)], after=g_late)
    g_early = sg_bufs["g_sg"]
    small_names = list(dict.fromkeys(name for name, _, _ in _SMALL_ROWS))
    weights = dict(b_ada=b_ada, g_pre=g_pre, g_post=g_post, pool_w=pool_w, pool_scale=pool_scale,
                   lb_logits=lb_logits, hgrn_norm_g=hgrn_norm_g)
    m_small = dict(b_ada=m_b_ada, g_pre=m_g_pre, g_post=m_g_post, pool_w=m_pool_w, pool_scale=m_pool_scale,
                   lb_logits=m_lb_logits, hgrn_norm_g=m_hgrn_norm_g)
    v_small = dict(b_ada=v_b_ada, g_pre=v_g_pre, g_post=v_g_post, pool_w=v_pool_w, pool_scale=v_pool_scale,
                   lb_logits=v_lb_logits, hgrn_norm_g=v_hgrn_norm_g)
    shapes = {name: weights[name].shape for name in small_names}
    small_out = _adamw_small(_pack_small(weights), _pack_small(m_small), _pack_small(v_small), g_late, g_early,
                             shapes)

    d_ada_all = jnp.stack([g_late[:, 0:24, :].reshape(N_DEV, 3 * D_MODEL),
                           g_early[:, 0:24, :].reshape(N_DEV, 3 * D_MODEL)], axis=1)
    d_cols = jnp.transpose(lax.dynamic_slice_in_dim(d_ada_all, me * ADA_COLS, ADA_COLS, axis=2), (1, 0, 2))
    g_w_ada = _ada_bwd(c_all, d_cols)
    ada_out = _adamw_sharded(w_ada, m_w_ada, v_w_ada, g_w_ada[:, None], 256, "adamw_w_ada")
    finish_unit("win0", ada_out[0])

    def leaf(kind):
        s = small_out[kind]
        return (ada_out[kind], s["b_ada"], s["g_pre"], s["g_post"], big_out["win"][kind], s["pool_w"], s["pool_scale"],
                s["lb_logits"], s["hgrn_norm_g"], big_out["wpo"][kind], big_out["who"][kind], big_out["wout"][kind])

    return (loss, grad_x) + leaf(0) + leaf(1) + leaf(2) + leaf(3)
```

```python
import jax
import jax.numpy as jnp
from jax import lax
from jax.experimental import pallas as pl
from jax.experimental.pallas import tpu as pltpu

F32 = jnp.float32
MXU_DTYPE = jnp.bfloat16
WIRE_DTYPE = jnp.bfloat16

N_DEV = 8
DEPTH = 2
D_MODEL = 1024
HEADS = 8
HEAD_DIM = 128
POOL_GROUPS = 4
GROUP_DIM = 128
POOL_WIDTH = POOL_GROUPS * GROUP_DIM
IN_WIDTH = 7168
CHUNK = 64
SUB = 16
N_SUB = CHUNK // SUB
EXP_CLAMP = 80.0
NORM_EPS = 1e-6
LOG_FLOOR = 1e-30
ADA_COLS = 3 * D_MODEL // N_DEV
IN_COLS = IN_WIDTH // N_DEV
COL_HQ, COL_HF, COL_HI, COL_HG, COL_MGP, COL_MGH = 1, 2, 3, 4, 5, 6

ADAM_LR = 0.001
ADAM_B1 = 0.9
ADAM_B2 = 0.999
ADAM_EPS = 1e-08
ADAM_WD = 0.01
ADAM_STEP = 10

VMEM_LIMIT = 48 * 1024 * 1024
MESH_ID = pl.DeviceIdType.MESH
HIGHEST = lax.Precision.HIGHEST

_SMALL_ROWS = (("b_ada", 0, 24), ("g_pre", 0, 8), ("b_ada", 1, 24), ("g_pre", 1, 8), ("g_post", None, 16),
               ("pool_w", None, 1024), ("pool_scale", None, 8), ("lb_logits", None, 16), ("hgrn_norm_g", None, 2))
SMALL_LATE_ROWS = 32
SMALL_ROWS_PAD = 1136
LB_ROW0 = 32 + 32 + 16 + 1024 + 8


def _params(**kw):
    return pltpu.CompilerParams(vmem_limit_bytes=VMEM_LIMIT, **kw)


def _sigmoid(v):
    return 1.0 / (1.0 + jnp.exp(-v))


def _dsilu(v, s):
    return s * (1.0 + v * (1.0 - s))


def _dot(a, b):
    return jnp.dot(a.astype(MXU_DTYPE), b.astype(MXU_DTYPE), preferred_element_type=F32)


def _dot_nt(a, b):
    return lax.dot_general(a.astype(MXU_DTYPE), b.astype(MXU_DTYPE), (((1,), (1,)), ((), ())),
                           preferred_element_type=F32)


def _dot_tn(a, b):
    return lax.dot_general(a.astype(MXU_DTYPE), b.astype(MXU_DTYPE), (((0,), (0,)), ((), ())),
                           preferred_element_type=F32)


def _pallas_after(body, n_in, after, *, in_specs, **kw):
    if after is None:
        return pl.pallas_call(body, in_specs=in_specs, **kw)

    def tied(*refs):
        body(*refs[:n_in], *refs[n_in + 1:])

    call = pl.pallas_call(tied, in_specs=list(in_specs) + [pl.BlockSpec(memory_space=pl.ANY)], **kw)
    return lambda *operands: call(*operands, after)


def _my_position():
    mx, my, mc = lax.axis_index("x"), lax.axis_index("y"), lax.axis_index("c")
    return mx, my, mc, 4 * mx + 2 * my + mc


def _peer(mx, my, mc, k):
    px = 1 - mx if (k >> 2) & 1 else mx
    py = 1 - my if (k >> 1) & 1 else my
    pc = 1 - mc if k & 1 else mc
    return (px, py, pc), 4 * px + 2 * py + pc


def _allgather_small(v, name, after=None):
    rows, cols = v.shape

    def body(v_ref, out_ref, send_sems, recv_sems):
        mx, my, mc, me = _my_position()
        out_ref[me] = v_ref[...]
        copies = []
        for k in range(1, N_DEV):
            peer, _ = _peer(mx, my, mc, k)
            cp = pltpu.make_async_remote_copy(
                src_ref=v_ref, dst_ref=out_ref.at[me],
                send_sem=send_sems.at[k - 1], recv_sem=recv_sems.at[k - 1],
                device_id=peer, device_id_type=MESH_ID)
            cp.start()
            copies.append(cp)
        for cp in copies:
            cp.wait()

    return _pallas_after(
        body, 1, after, name=name,
        out_shape=jax.ShapeDtypeStruct((N_DEV, rows, cols), v.dtype),
        in_specs=[pl.BlockSpec(memory_space=pltpu.VMEM)],
        out_specs=pl.BlockSpec(memory_space=pltpu.VMEM),
        scratch_shapes=[pltpu.SemaphoreType.DMA((N_DEV - 1,)), pltpu.SemaphoreType.DMA((N_DEV - 1,))],
        compiler_params=_params(),
    )(v)


class _Stream:
    def __init__(self, n, plan):
        self.n, self.plan = n, plan


def _comm_call(name, bufs, start=(), wait=(), after=None):
    names = list(bufs)

    def body(*refs):
        it = iter(refs)
        buf_refs = {n: next(it) for n in names}
        wait_sems = [(next(it), next(it)) for _ in wait]
        if after is not None:
            next(it)
        start_sems = [(next(it), next(it)) for _ in start]
        for _ in names:
            next(it)
        token = next(it)
        pos = _my_position()

        def descriptors(stream, sems):
            return [pltpu.make_async_remote_copy(src_ref=src, dst_ref=dst, send_sem=sems[0].at[k], recv_sem=sems[1].at[k],
                                                 device_id=dev, device_id_type=MESH_ID)
                    for k, (src, dst, dev) in enumerate(stream.plan(buf_refs, pos))]

        for (stream, _), sems in zip(wait, wait_sems):
            for cp in descriptors(stream, sems):
                cp.wait_send()
                cp.wait_recv()
        for stream, sems in zip(start, start_sems):
            for cp in descriptors(stream, sems):
                cp.start()
        token[...] = jnp.zeros_like(token)

    hbm = pl.BlockSpec(memory_space=pltpu.HBM)
    sem = pl.BlockSpec(memory_space=pltpu.SEMAPHORE)
    operands = [pltpu.with_memory_space_constraint(bufs[n], pltpu.HBM) for n in names]
    in_specs = [hbm] * len(names)
    for _, (send_sems, recv_sems) in wait:
        operands += [send_sems, recv_sems]
        in_specs += [sem, sem]
    if after is not None:
        operands.append(after)
        in_specs.append(pl.BlockSpec(memory_space=pl.ANY))
    out_shape, out_specs = [], []
    for stream in start:
        out_shape += [pltpu.SemaphoreType.DMA((stream.n,)), pltpu.SemaphoreType.DMA((stream.n,))]
        out_specs += [sem, sem]
    n_sem_out = len(out_shape)
    out_shape += [pltpu.HBM(bufs[n].shape, bufs[n].dtype) for n in names]
    out_specs += [hbm] * len(names)
    out_shape.append(jax.ShapeDtypeStruct((8, 128), F32))
    out_specs.append(pl.BlockSpec(memory_space=pltpu.VMEM))
    outs = pl.pallas_call(
        body, name=name, out_shape=out_shape, in_specs=in_specs, out_specs=out_specs,
        input_output_aliases={i: n_sem_out + i for i in range(len(names))},
        compiler_params=pltpu.CompilerParams(has_side_effects=pltpu.SideEffectType.DATAFLOW_SIDE_EFFECTING),
    )(*operands)
    sems = [(outs[2 * i], outs[2 * i + 1]) for i in range(len(start))]
    return dict(zip(names, outs[n_sem_out:n_sem_out + len(names)])), sems, outs[-1]


def _with_own_slot(block, me):
    return lax.dynamic_update_index_in_dim(lax.empty((N_DEV,) + block.shape, block.dtype), block, me, 0)


def _other_chips(pos):
    mx, my, _, _ = pos
    return [(1 - mx if i & 2 else mx, 1 - my if i & 1 else my) for i in (1, 2, 3)]


def _dev_index(px, py, pc):
    return 4 * px + 2 * py + pc


def _gather_streams(keys):
    def to_chips(refs, pos):
        _, _, mc, me = pos
        return [(refs["s_" + k], refs["g_" + k].at[me], (cx, cy, mc)) for k in keys for cx, cy in _other_chips(pos)]

    def to_sibling(refs, pos):
        mx, my, mc, me = pos
        return [(refs["s_" + k], refs["g_" + k].at[me], (mx, my, 1 - mc)) for k in keys]

    def pass_on(refs, pos):
        mx, my, mc, _ = pos
        out = []
        for k in keys:
            for cx, cy in _other_chips(pos):
                slot = refs["g_" + k].at[_dev_index(cx, cy, mc)]
                out.append((slot, slot, (mx, my, 1 - mc)))
        return out

    return _Stream(3 * len(keys), to_chips), _Stream(len(keys), to_sibling), _Stream(3 * len(keys), pass_on)


def _direct_gather_stream(key):
    def plan(refs, pos):
        mx, my, mc, me = pos
        return [(refs["s_" + key], refs["g_" + key].at[me], _peer(mx, my, mc, k)[0]) for k in range(1, N_DEV)]

    return _Stream(N_DEV - 1, plan)


def _scatter_streams(keys):
    def pair(refs, pos):
        mx, my, mc, _ = pos
        sib = (mx, my, 1 - mc)
        out = []
        for k in keys:
            for i, (cx, cy) in enumerate(_other_chips(pos)):
                out.append((refs["g_" + k].at[_dev_index(cx, cy, 1 - mc)], refs["st_" + k].at[i], sib))
            out.append((refs["g_" + k].at[_dev_index(mx, my, 1 - mc)], refs["st_" + k].at[3], sib))
        return out

    def chips(refs, pos):
        mc = pos[2]
        return [(refs["ps_" + k].at[i], refs["ld_" + k].at[i], (cx, cy, mc))
                for k in keys for i, (cx, cy) in enumerate(_other_chips(pos))]

    return _Stream(4 * len(keys), pair), _Stream(3 * len(keys), chips)


def _pair_sum(g, st, idx, tr, name):
    _, rows, cols = g.shape

    def body(idx_ref, g_ref, st_ref, out_ref):
        out_ref[...] = (g_ref[...].astype(F32) + st_ref[...].astype(F32)).astype(out_ref.dtype)

    return pl.pallas_call(
        body, name=name,
        grid_spec=pltpu.PrefetchScalarGridSpec(
            num_scalar_prefetch=1, grid=(4, rows // tr),
            in_specs=[pl.BlockSpec((None, tr, cols), lambda j, i, idx_ref: (idx_ref[j], i, 0)),
                      pl.BlockSpec((None, tr, cols), lambda j, i, idx_ref: (j, i, 0))],
            out_specs=pl.BlockSpec((None, tr, cols), lambda j, i, idx_ref: (j, i, 0))),
        out_shape=jax.ShapeDtypeStruct((4, rows, cols), WIRE_DTYPE),
        compiler_params=_params(dimension_semantics=("parallel", "parallel")),
    )(idx, g, st)


def _ada_fwd(c_all, w_ada, b_cols):
    def body(c_ref, w_ref, b_ref, out_ref):
        cv = c_ref[...]
        ca = cv * _sigmoid(cv)
        for l in range(DEPTH):
            out_ref[l] = jnp.dot(ca, w_ref[l], precision=HIGHEST, preferred_element_type=F32) + b_ref[l:l + 1, :]

    return pl.pallas_call(
        body, name="ada_fwd",
        out_shape=jax.ShapeDtypeStruct((DEPTH, N_DEV, ADA_COLS), F32),
        compiler_params=_params(),
    )(c_all, w_ada, b_cols)


def _ada_bwd(c_all, d_cols):
    def body(c_ref, d_ref, out_ref):
        cv = c_ref[...]
        ca = cv * _sigmoid(cv)
        for l in range(DEPTH):
            out_ref[l] = lax.dot_general(ca, d_ref[l], (((0,), (0,)), ((), ())), precision=HIGHEST,
                                         preferred_element_type=F32)

    return pl.pallas_call(
        body, name="ada_bwd",
        out_shape=jax.ShapeDtypeStruct((DEPTH, D_MODEL, ADA_COLS), F32),
        compiler_params=_params(),
    )(c_all, d_cols)


def _lower_bounds(logits):
    m = jnp.maximum(logits[0:1], logits[1:2])
    e0, e1 = jnp.exp(logits[0:1] - m), jnp.exp(logits[1:2] - m)
    den = e0 + e1
    p0, p1 = e0 / den, e1 / den
    low0 = p0 - p0
    low1 = (p0 + p1) - p0
    return (p0, p1), (low0, low1)


def _lb_fwd(lb_logits):
    def body(lg_ref, out_ref):
        _, (low0, low1) = _lower_bounds(lg_ref[...])
        out_ref[0:1, :] = jnp.clip(low0, 0.0, 1.0)
        out_ref[1:2, :] = jnp.clip(low1, 0.0, 1.0)

    return pl.pallas_call(body, name="lb_fwd", out_shape=jax.ShapeDtypeStruct(lb_logits.shape, F32),
                          compiler_params=_params())(lb_logits)


def _row_spec(cols=D_MODEL):
    return pl.BlockSpec((1, cols), lambda *_: (0, 0))


def _prenorm_fwd(x, g, shift, scale, tm, name, after=None):
    seq = x.shape[0]

    def body(x_ref, g_ref, sh_ref, sc_ref, h_ref):
        xv = x_ref[...]
        rs = lax.rsqrt(jnp.mean(xv * xv, axis=-1, keepdims=True) + NORM_EPS)
        h = (xv * rs * g_ref[...]) * (1.0 + sc_ref[...]) + sh_ref[...]
        h_ref[...] = h.astype(h_ref.dtype)

    tile = pl.BlockSpec((tm, D_MODEL), lambda i: (i, 0))
    return _pallas_after(
        body, 4, after, name=name, grid=(seq // tm,),
        in_specs=[tile, _row_spec(), _row_spec(), _row_spec()], out_specs=tile,
        out_shape=jax.ShapeDtypeStruct((seq, D_MODEL), MXU_DTYPE),
        compiler_params=_params(dimension_semantics=("parallel",)),
    )(x, g, shift, scale)


def _in_proj(h, win_g, tm, name, after=None):
    seq = h.shape[0]

    def body(h_ref, w_ref, z_ref):
        z_ref[...] = jnp.dot(h_ref[...], w_ref[...], preferred_element_type=F32)

    return _pallas_after(
        body, 2, after, name=name, grid=(N_DEV, seq // tm),
        in_specs=[pl.BlockSpec((tm, D_MODEL), lambda j, i: (i, 0)),
                  pl.BlockSpec((None, D_MODEL, IN_COLS), lambda j, i: (j, 0, 0))],
        out_specs=pl.BlockSpec((tm, IN_COLS), lambda j, i: (i, j)),
        out_shape=jax.ShapeDtypeStruct((seq, IN_WIDTH), F32),
        compiler_params=_params(dimension_semantics=("parallel", "parallel")),
    )(h, win_g)


def _shift_down(v, j, pos):
    return jnp.where(pos >= j, pltpu.roll(v, j, 0), 0.0)


def _shift_up(v, j, pos, seq):
    return jnp.where(pos < seq - j, pltpu.roll(v, seq - j, 0), 0.0)


def _select_window(g, candidates):
    out = candidates[-1]
    for i in range(len(candidates) - 2, -1, -1):
        out = jnp.where(g == i, candidates[i], out)
    return out


def _pool_mean_minus_token(u, g, pos):
    sums, acc = [], u
    for j in (1, 2, 4, 8):
        acc = acc + _shift_down(acc, j, pos)
        sums.append(acc)
    wsum = _select_window(g, sums)
    width = jnp.left_shift(2, g).astype(F32)
    count = jnp.minimum(pos.astype(F32) + 1.0, width)
    return wsum / count - u, count


def _pool_fwd(z, pool_w_l, pool_scale_l, name, after=None):
    seq = z.shape[0]

    def body(pv_ref, pg_ref, w_ref, sc_ref, out_ref):
        g = pl.program_id(0)
        pos = lax.broadcasted_iota(jnp.int32, (seq, GROUP_DIM), 0)
        pm, _ = _pool_mean_minus_token(pv_ref[...], g, pos)
        lin = _dot(pm, w_ref[...]) * sc_ref[...]
        pg = pg_ref[...]
        out_ref[...] = (lin * (pg * _sigmoid(pg))).astype(out_ref.dtype)

    return _pallas_after(
        body, 4, after, name=name, grid=(POOL_GROUPS,),
        in_specs=[pl.BlockSpec((seq, GROUP_DIM), lambda g: (0, g)),
                  pl.BlockSpec((seq, GROUP_DIM), lambda g: (0, POOL_GROUPS + g)),
                  pl.BlockSpec((None, GROUP_DIM, GROUP_DIM), lambda g: (g, 0, 0)),
                  pl.BlockSpec((1, GROUP_DIM), lambda g: (0, g))],
        out_specs=pl.BlockSpec((seq, GROUP_DIM), lambda g: (0, g)),
        out_shape=jax.ShapeDtypeStruct((seq, POOL_WIDTH), MXU_DTYPE),
        compiler_params=_params(dimension_semantics=("parallel",)),
    )(z, z, pool_w_l, pool_scale_l)


def _chunk_masks():
    row = lax.broadcasted_iota(jnp.int32, (CHUNK, CHUNK), 0)
    col = lax.broadcasted_iota(jnp.int32, (CHUNK, CHUNK), 1)
    causal = row >= col
    before_sub = col < (row // SUB) * SUB
    suffix = row <= col
    return causal, before_sub, suffix


def _masked_sums(masks, v):
    lhs = jnp.concatenate([m.astype(jnp.bfloat16) for m in masks], axis=0)
    hi = v.astype(jnp.bfloat16)
    rest = v - hi.astype(F32)
    mid = rest.astype(jnp.bfloat16)
    lo = (rest - mid.astype(F32)).astype(jnp.bfloat16)
    out = jnp.dot(lhs, hi, preferred_element_type=F32)
    out += jnp.dot(lhs, mid, preferred_element_type=F32)
    out += jnp.dot(lhs, lo, preferred_element_type=F32)
    return [out[i * CHUNK:(i + 1) * CHUNK] for i in range(len(masks))]


def _gates(zf, lb):
    sg = _sigmoid(zf)
    f = lb + (1.0 - lb) * sg
    logf = jnp.log(jnp.maximum(f, LOG_FLOOR))
    return sg, f, logf


def _intra_blocks(q_h, k_h, cum_h, base_h, causal):
    rel = cum_h - base_h
    out = []
    for i in range(N_SUB):
        rows = slice(i * SUB, (i + 1) * SUB)
        e_q = jnp.exp(rel[rows])
        base_i = jnp.concatenate([base_h[rows]] * N_SUB, axis=0)
        e_k = jnp.exp(jnp.minimum(base_i - cum_h, EXP_CLAMP))
        q_t = (q_h[rows] * e_q).astype(MXU_DTYPE)
        k_t = (k_h * e_k).astype(MXU_DTYPE)
        a_i = jnp.where(causal[rows], _dot_nt(q_t, k_t), 0.0)
        out.append((q_t, k_t, e_q, e_k, a_i))
    return out


def _hgrn_fwd(z, lb_l, gn_l, name, after=None):
    seq = z.shape[0]
    n_chunks = seq // CHUNK

    def body(hq_ref, hf_ref, hi_ref, hg_ref, lb_ref, gn_ref, o_ref, bin_ref, st_ref, state):
        @pl.when(pl.program_id(0) == 0)
        def _():
            state[...] = jnp.zeros_like(state)

        causal, before_sub, _ = _chunk_masks()
        _, f, logf = _gates(hf_ref[...], lb_ref[...])
        kk = 1.0 - f
        hq = hq_ref[...]
        q = hq * _sigmoid(hq)
        cum, base = _masked_sums([causal, before_sub], logf)
        st_ref[0] = state[...]
        for h in range(HEADS):
            sl = slice(h * HEAD_DIM, (h + 1) * HEAD_DIM)
            q_h, k_h, cum_h = q[:, sl], kk[:, sl], cum[:, sl]
            v_h = hi_ref[:, sl]
            st_h = state[h]
            blocks = _intra_blocks(q_h, k_h, cum_h, base[:, sl], causal)
            a = jnp.concatenate([b[4] for b in blocks], axis=0)
            o_h = _dot_nt(q_h * jnp.exp(cum_h), st_h) + _dot(a, v_h)
            last = jnp.sum(logf[:, sl], axis=0, keepdims=True)
            state[h] = st_h * jnp.exp(last) + _dot_tn(v_h, k_h * jnp.exp(last - cum_h))
            rs = lax.rsqrt(jnp.mean(o_h * o_h, axis=-1, keepdims=True) + NORM_EPS)
            hg = hg_ref[:, sl]
            o_ref[:, sl] = o_h
            bin_ref[:, sl] = ((o_h * rs * gn_ref[...]) * (hg * _sigmoid(hg))).astype(bin_ref.dtype)

    def col(block):
        return pl.BlockSpec((CHUNK, D_MODEL), lambda c: (c, block))

    tile = pl.BlockSpec((CHUNK, D_MODEL), lambda c: (c, 0))
    return _pallas_after(
        body, 6, after, name=name, grid=(n_chunks,),
        in_specs=[col(COL_HQ), col(COL_HF), col(COL_HI), col(COL_HG), _row_spec(), _row_spec(HEAD_DIM)],
        out_specs=[tile, tile, pl.BlockSpec((1, HEADS, HEAD_DIM, HEAD_DIM), lambda c: (c, 0, 0, 0))],
        out_shape=[jax.ShapeDtypeStruct((seq, D_MODEL), F32),
                   jax.ShapeDtypeStruct((seq, D_MODEL), MXU_DTYPE),
                   jax.ShapeDtypeStruct((n_chunks, HEADS, HEAD_DIM, HEAD_DIM), F32)],
        scratch_shapes=[pltpu.VMEM((HEADS, HEAD_DIM, HEAD_DIM), F32)],
        compiler_params=_params(dimension_semantics=("arbitrary",)),
    )(z, z, z, z, lb_l, gn_l)


def _rms_parts(y):
    rs = lax.rsqrt(jnp.mean(y * y, axis=-1, keepdims=True) + NORM_EPS)
    return rs, y * rs


def _merge_fwd(a_in, b_in, z, x, wpo_g, who_g, wout_g, gate, g_post, tm, name):
    seq = x.shape[0]

    def body(a_ref, b_ref, mgp_ref, mgh_ref, x_ref, wpo_ref, who_ref, wout_ref, gate_ref, gp_ref,
             ba_ref, bb_ref, mer_ref, y_ref, xn_ref):
        a = a_ref[...]
        ba = jnp.concatenate([_dot(a, wpo_ref[j]) for j in range(N_DEV)], axis=1)
        bb = _dot(b_ref[...], who_ref[...])
        merged = _sigmoid(mgp_ref[...]) * ba + _sigmoid(mgh_ref[...]) * bb
        y = _dot(merged, wout_ref[...])
        _, yn = _rms_parts(y)
        ba_ref[...] = ba.astype(ba_ref.dtype)
        bb_ref[...] = bb.astype(bb_ref.dtype)
        mer_ref[...] = merged.astype(mer_ref.dtype)
        y_ref[...] = y
        xn_ref[...] = x_ref[...] + gate_ref[...] * (yn * gp_ref[...])

    def tile(cols=D_MODEL, block=0):
        return pl.BlockSpec((tm, cols), lambda i: (i, block))

    full = pl.BlockSpec((D_MODEL, D_MODEL), lambda i: (0, 0))
    act = jax.ShapeDtypeStruct((seq, D_MODEL), MXU_DTYPE)
    f32 = jax.ShapeDtypeStruct((seq, D_MODEL), F32)
    return pl.pallas_call(
        body, name=name, grid=(seq // tm,),
        in_specs=[tile(POOL_WIDTH), tile(), tile(block=COL_MGP), tile(block=COL_MGH), tile(),
                  pl.BlockSpec((N_DEV, POOL_WIDTH, GROUP_DIM), lambda i: (0, 0, 0)),
                  full, full, _row_spec(), _row_spec()],
        out_specs=[tile(), tile(), tile(), tile(), tile()],
        out_shape=[act, act, act, f32, f32],
        compiler_params=_params(dimension_semantics=("parallel",)),
    )(a_in, b_in, z, z, x, wpo_g, who_g, wout_g, gate, g_post)


def _loss_grad(x_out, target, tm):
    seq = x_out.shape[0]

    def body(x_ref, t_ref, loss_ref, dx_ref):
        @pl.when(pl.program_id(0) == 0)
        def _():
            loss_ref[...] = jnp.zeros_like(loss_ref)

        err = x_ref[...] - t_ref[...]
        per_token = jnp.mean(err * err, axis=-1, keepdims=True)
        loss_ref[...] += 0.5 * jnp.sum(per_token, axis=0, keepdims=True)
        dx_ref[...] = err * (1.0 / D_MODEL)

    tile = pl.BlockSpec((tm, D_MODEL), lambda i: (i, 0))
    return pl.pallas_call(
        body, name="loss_grad", grid=(seq // tm,),
        in_specs=[tile, tile],
        out_specs=[pl.BlockSpec((1, 1), lambda i: (0, 0)), tile],
        out_shape=[jax.ShapeDtypeStruct((1, 1), F32), jax.ShapeDtypeStruct((seq, D_MODEL), F32)],
        compiler_params=_params(dimension_semantics=("arbitrary",)),
    )(x_out, target)


def _merge_bwd(dx, y, ba, bb, z, wpo_g, who_g, wout_g, gate, g_post, tm, name):
    seq = dx.shape[0]

    def body(dx_ref, y_ref, ba_ref, bb_ref, mgp_ref, mgh_ref, wpo_ref, who_ref, wout_ref, gate_ref, gp_ref,
             dy_ref, dba_ref, dbb_ref, da_ref, db_ref, dmg_ref, acc_ref):
        @pl.when(pl.program_id(0) == 0)
        def _():
            acc_ref[...] = jnp.zeros_like(acc_ref)

        dxv = dx_ref[...]
        rs, yn = _rms_parts(y_ref[...])
        acc_ref[0:1, :] += jnp.sum(dxv * yn * gp_ref[...], axis=0, keepdims=True)
        acc_ref[1:2, :] += jnp.sum(dxv * gate_ref[...] * yn, axis=0, keepdims=True)
        dyn = dxv * (gate_ref[...] * gp_ref[...])
        dy = rs * (dyn - yn * jnp.mean(dyn * yn, axis=-1, keepdims=True))
        dmerged = _dot_nt(dy, wout_ref[...])
        sp, sh = _sigmoid(mgp_ref[...]), _sigmoid(mgh_ref[...])
        dba, dbb = sp * dmerged, sh * dmerged
        dmg_ref[:, 0:D_MODEL] = (dmerged * ba_ref[...].astype(F32) * sp * (1.0 - sp)).astype(dmg_ref.dtype)
        dmg_ref[:, D_MODEL:2 * D_MODEL] = (dmerged * bb_ref[...].astype(F32) * sh * (1.0 - sh)).astype(dmg_ref.dtype)
        da = _dot_nt(dba[:, 0:GROUP_DIM], wpo_ref[0])
        for j in range(1, N_DEV):
            da += _dot_nt(dba[:, j * GROUP_DIM:(j + 1) * GROUP_DIM], wpo_ref[j])
        dy_ref[...] = dy.astype(dy_ref.dtype)
        dba_ref[...] = dba.astype(dba_ref.dtype)
        dbb_ref[...] = dbb.astype(dbb_ref.dtype)
        da_ref[...] = da
        db_ref[...] = _dot_nt(dbb, who_ref[...])

    def tile(cols=D_MODEL, block=0):
        return pl.BlockSpec((tm, cols), lambda i: (i, block))

    full = pl.BlockSpec((D_MODEL, D_MODEL), lambda i: (0, 0))
    act = jax.ShapeDtypeStruct((seq, D_MODEL), MXU_DTYPE)
    return pl.pallas_call(
        body, name=name, grid=(seq // tm,),
        in_specs=[tile(), tile(), tile(), tile(), tile(block=COL_MGP), tile(block=COL_MGH),
                  pl.BlockSpec((N_DEV, POOL_WIDTH, GROUP_DIM), lambda i: (0, 0, 0)),
                  full, full, _row_spec(), _row_spec()],
        out_specs=[tile(), tile(), tile(), tile(POOL_WIDTH), tile(), tile(2 * D_MODEL),
                   pl.BlockSpec((8, D_MODEL), lambda i: (0, 0))],
        out_shape=[act, act, act, jax.ShapeDtypeStruct((seq, POOL_WIDTH), F32),
                   jax.ShapeDtypeStruct((seq, D_MODEL), F32),
                   jax.ShapeDtypeStruct((seq, 2 * D_MODEL), MXU_DTYPE),
                   jax.ShapeDtypeStruct((8, D_MODEL), F32)],
        compiler_params=_params(dimension_semantics=("arbitrary",)),
    )(dx, y, ba, bb, z, z, wpo_g, who_g, wout_g, gate, g_post)


def _grad_tn(a, b, tn, dev_major, name):
    seq, ka = a.shape
    n = b.shape[1]

    def body(a_ref, b_ref, out_ref):
        out_ref[...] = _dot_tn(a_ref[...], b_ref[...]).astype(out_ref.dtype)

    if dev_major:
        out_spec = pl.BlockSpec((None, ka, tn), lambda j: (j, 0, 0))
        out_shape = jax.ShapeDtypeStruct((n // tn, ka, tn), WIRE_DTYPE)
    else:
        out_spec = pl.BlockSpec((ka, tn), lambda j: (0, j))
        out_shape = jax.ShapeDtypeStruct((ka, n), WIRE_DTYPE)
    return pl.pallas_call(
        body, name=name, grid=(n // tn,),
        in_specs=[pl.BlockSpec((seq, ka), lambda j: (0, 0)), pl.BlockSpec((seq, tn), lambda j: (0, j))],
        out_specs=out_spec, out_shape=out_shape,
        compiler_params=_params(dimension_semantics=("parallel",)),
    )(a, b)


def _hgrn_bwd(db_in, z, o, states, lb_l, gn_l, name, after=None):
    seq = z.shape[0]
    n_chunks = seq // CHUNK

    def body(db_ref, hq_ref, hf_ref, hi_ref, hg_ref, o_ref, st_ref, lb_ref, gn_ref,
             dz_ref, dlb_ref, dgn_ref, dstate, dq_buf, dk_buf, dg_buf):
        @pl.when(pl.program_id(0) == 0)
        def _():
            dstate[...] = jnp.zeros_like(dstate)
            dlb_ref[...] = jnp.zeros_like(dlb_ref)
            dgn_ref[...] = jnp.zeros_like(dgn_ref)

        causal, before_sub, suffix = _chunk_masks()
        lb = lb_ref[...]
        sg, f, logf = _gates(hf_ref[...], lb)
        kk = 1.0 - f
        hq = hq_ref[...]
        sq = _sigmoid(hq)
        q = hq * sq
        cum, base = _masked_sums([causal, before_sub], logf)
        gn = gn_ref[...]
        dgn = jnp.zeros((1, HEAD_DIM), F32)
        dlast = []
        for h in range(HEADS):
            sl = slice(h * HEAD_DIM, (h + 1) * HEAD_DIM)
            q_h, k_h, cum_h = q[:, sl], kk[:, sl], cum[:, sl]
            v_h = hi_ref[:, sl]
            st_h = st_ref[0, h]
            dst_h = dstate[h]
            rs, ohat = _rms_parts(o_ref[:, sl])
            hg = hg_ref[:, sl]
            shg = _sigmoid(hg)
            d_bin = db_ref[:, sl]
            don = d_bin * (hg * shg)
            dgn += jnp.sum(don * ohat, axis=0, keepdims=True)
            dohat = don * gn
            do = rs * (dohat - ohat * jnp.mean(dohat * ohat, axis=-1, keepdims=True))
            dz_ref[:, 3 * D_MODEL + h * HEAD_DIM:3 * D_MODEL + (h + 1) * HEAD_DIM] = (
                d_bin * (ohat * gn) * _dsilu(hg, shg)).astype(dz_ref.dtype)
            last = jnp.sum(logf[:, sl], axis=0, keepdims=True)
            g_in = jnp.exp(cum_h)
            d_out = jnp.exp(last - cum_h)
            q_bar, k_bar = q_h * g_in, k_h * d_out
            blocks = _intra_blocks(q_h, k_h, cum_h, base[:, sl], causal)
            a = jnp.concatenate([b[4] for b in blocks], axis=0)
            da = jnp.where(causal, _dot_nt(do, v_h), 0.0)
            dv = _dot_tn(a, do) + _dot_nt(k_bar, dst_h)
            dq_bar, dk_bar = _dot(do, st_h), _dot(v_h, dst_h)
            dk = dk_bar * d_out
            dq_parts, dg_parts = [], []
            dg_k = k_bar * dk_bar
            dlast.append(jnp.sum(k_bar * dk_bar, axis=0, keepdims=True)
                         + jnp.exp(last) * jnp.sum(st_h * dst_h, axis=0, keepdims=True))
            for i, (q_t, k_t, e_q, e_k, _) in enumerate(blocks):
                da_i = da[i * SUB:(i + 1) * SUB].astype(MXU_DTYPE)
                dq_t = _dot(da_i, k_t)
                dk_t = _dot_tn(da_i, q_t)
                dq_parts.append(dq_t * e_q)
                dk += dk_t * e_k
                dg_parts.append(q_t.astype(F32) * dq_t)
                dg_k += k_t.astype(F32) * dk_t
            dq = dq_bar * g_in + jnp.concatenate(dq_parts, axis=0)
            dg_buf[:, sl] = q_bar * dq_bar + jnp.concatenate(dg_parts, axis=0) - dg_k
            dstate[h] = dst_h * jnp.exp(last) + _dot_tn(do, q_bar)
            dq_buf[:, sl] = dq
            dk_buf[:, sl] = dk
            dz_ref[:, 2 * D_MODEL + h * HEAD_DIM:2 * D_MODEL + (h + 1) * HEAD_DIM] = dv.astype(dz_ref.dtype)
        dgn_ref[...] += dgn
        dq_all, dk_all = dq_buf[...], dk_buf[...]
        dlogf = _masked_sums([suffix], dg_buf[...])[0] + jnp.concatenate(dlast, axis=1)
        df = jnp.where(f > LOG_FLOOR, dlogf / f, 0.0) - dk_all
        dlb_ref[...] += jnp.sum(df * (1.0 - sg), axis=0, keepdims=True)
        dz_ref[:, 0:D_MODEL] = (dq_all * _dsilu(hq, sq)).astype(dz_ref.dtype)
        dz_ref[:, D_MODEL:2 * D_MODEL] = (df * (1.0 - lb) * sg * (1.0 - sg)).astype(dz_ref.dtype)

    last_chunk = n_chunks - 1

    def col(block):
        return pl.BlockSpec((CHUNK, D_MODEL), lambda c: (last_chunk - c, block))

    return _pallas_after(
        body, 9, after, name=name, grid=(n_chunks,),
        in_specs=[col(0), col(COL_HQ), col(COL_HF), col(COL_HI), col(COL_HG), col(0),
                  pl.BlockSpec((1, HEADS, HEAD_DIM, HEAD_DIM), lambda c: (last_chunk - c, 0, 0, 0)),
                  _row_spec(), _row_spec(HEAD_DIM)],
        out_specs=[pl.BlockSpec((CHUNK, 4 * D_MODEL), lambda c: (last_chunk - c, 0)),
                   _row_spec(), _row_spec(HEAD_DIM)],
        out_shape=[jax.ShapeDtypeStruct((seq, 4 * D_MODEL), MXU_DTYPE),
                   jax.ShapeDtypeStruct((1, D_MODEL), F32), jax.ShapeDtypeStruct((1, HEAD_DIM), F32)],
        scratch_shapes=[pltpu.VMEM((HEADS, HEAD_DIM, HEAD_DIM), F32)] + [pltpu.VMEM((CHUNK, D_MODEL), F32)] * 3,
        compiler_params=_params(dimension_semantics=("arbitrary",)),
    )(db_in, z, z, z, z, o, states, lb_l, gn_l)


def _pool_bwd(da_in, z, pool_w_l, pool_scale_l, name, after=None):
    seq = z.shape[0]

    def body(da_ref, pv_ref, pg_ref, w_ref, sc_ref, dpv_ref, dpg_ref, dw_ref, dsc_ref):
        g = pl.program_id(0)
        pos = lax.broadcasted_iota(jnp.int32, (seq, GROUP_DIM), 0)
        pm, count = _pool_mean_minus_token(pv_ref[...], g, pos)
        lin0 = _dot(pm, w_ref[...])
        pg = pg_ref[...]
        spg = _sigmoid(pg)
        da = da_ref[...]
        dlin = da * (pg * spg)
        dpg_ref[...] = (da * (lin0 * sc_ref[...]) * _dsilu(pg, spg)).astype(dpg_ref.dtype)
        dsc_ref[...] = jnp.sum(dlin * lin0, axis=0, keepdims=True)
        dl0 = dlin * sc_ref[...]
        dw_ref[...] = _dot_tn(pm, dl0)
        dpm = _dot_nt(dl0, w_ref[...])
        sums, acc = [], dpm / count
        for j in (1, 2, 4, 8):
            acc = acc + _shift_up(acc, j, pos, seq)
            sums.append(acc)
        dpv_ref[...] = (_select_window(g, sums) - dpm).astype(dpv_ref.dtype)

    grp = pl.BlockSpec((seq, GROUP_DIM), lambda g: (0, g))
    return _pallas_after(
        body, 5, after, name=name, grid=(POOL_GROUPS,),
        in_specs=[grp, grp, pl.BlockSpec((seq, GROUP_DIM), lambda g: (0, POOL_GROUPS + g)),
                  pl.BlockSpec((None, GROUP_DIM, GROUP_DIM), lambda g: (g, 0, 0)),
                  pl.BlockSpec((1, GROUP_DIM), lambda g: (0, g))],
        out_specs=[grp, grp, pl.BlockSpec((None, GROUP_DIM, GROUP_DIM), lambda g: (g, 0, 0)),
                   pl.BlockSpec((1, GROUP_DIM), lambda g: (0, g))],
        out_shape=[jax.ShapeDtypeStruct((seq, POOL_WIDTH), MXU_DTYPE),
                   jax.ShapeDtypeStruct((seq, POOL_WIDTH), MXU_DTYPE),
                   jax.ShapeDtypeStruct((POOL_GROUPS, GROUP_DIM, GROUP_DIM), F32),
                   jax.ShapeDtypeStruct((1, POOL_WIDTH), F32)],
        compiler_params=_params(dimension_semantics=("parallel",)),
    )(da_in, z, z, pool_w_l, pool_scale_l)


def _in_proj_dw(h, dz, name, after=None):
    seq = h.shape[0]

    def body(h_ref, dz_ref, out_ref):
        out_ref[...] = lax.dot_general(h_ref[...], dz_ref[...], (((0,), (0,)), ((), ())),
                                       preferred_element_type=F32).astype(out_ref.dtype)

    return _pallas_after(
        body, 2, after, name=name, grid=(N_DEV,),
        in_specs=[pl.BlockSpec((seq, D_MODEL), lambda j: (0, 0)), pl.BlockSpec((seq, IN_COLS), lambda j: (0, j))],
        out_specs=pl.BlockSpec((None, D_MODEL, IN_COLS), lambda j: (j, 0, 0)),
        out_shape=jax.ShapeDtypeStruct((N_DEV, D_MODEL, IN_COLS), WIRE_DTYPE),
        compiler_params=_params(dimension_semantics=("parallel",)),
    )(h, dz)


def _in_proj_dh(dz, win_g, tm, name, after=None):
    seq = dz.shape[0]

    def body(dz_ref, w_ref, dh_ref):
        @pl.when(pl.program_id(1) == 0)
        def _():
            dh_ref[...] = jnp.zeros_like(dh_ref)

        dh_ref[...] += lax.dot_general(dz_ref[...], w_ref[...], (((1,), (1,)), ((), ())),
                                       preferred_element_type=F32)

    return _pallas_after(
        body, 2, after, name=name, grid=(seq // tm, N_DEV),
        in_specs=[pl.BlockSpec((tm, IN_COLS), lambda i, j: (i, j)),
                  pl.BlockSpec((None, D_MODEL, IN_COLS), lambda i, j: (j, 0, 0))],
        out_specs=pl.BlockSpec((tm, D_MODEL), lambda i, j: (i, 0)),
        out_shape=jax.ShapeDtypeStruct((seq, D_MODEL), F32),
        compiler_params=_params(dimension_semantics=("parallel", "arbitrary")),
    )(dz, win_g)


def _prenorm_bwd(x, dh, dx_res, g, scale, tm, name, after=None):
    seq = x.shape[0]

    def body(x_ref, dh_ref, dxr_ref, g_ref, sc_ref, dx_ref, acc_ref):
        @pl.when(pl.program_id(0) == 0)
        def _():
            acc_ref[...] = jnp.zeros_like(acc_ref)

        rs, xn = _rms_parts(x_ref[...])
        dh = dh_ref[...]
        acc_ref[0:1, :] += jnp.sum(dh, axis=0, keepdims=True)
        acc_ref[1:2, :] += jnp.sum(dh * (xn * g_ref[...]), axis=0, keepdims=True)
        dhn = dh * (1.0 + sc_ref[...])
        acc_ref[2:3, :] += jnp.sum(dhn * xn, axis=0, keepdims=True)
        dxn = dhn * g_ref[...]
        dx_ref[...] = rs * (dxn - xn * jnp.mean(dxn * xn, axis=-1, keepdims=True)) + dxr_ref[...]

    tile = pl.BlockSpec((tm, D_MODEL), lambda i: (i, 0))
    return _pallas_after(
        body, 5, after, name=name, grid=(seq // tm,),
        in_specs=[tile, tile, tile, _row_spec(), _row_spec()],
        out_specs=[tile, pl.BlockSpec((8, D_MODEL), lambda i: (0, 0))],
        out_shape=[jax.ShapeDtypeStruct((seq, D_MODEL), F32), jax.ShapeDtypeStruct((8, D_MODEL), F32)],
        compiler_params=_params(dimension_semantics=("arbitrary",)),
    )(x, dh, dx_res, g, scale)


def _adamw_math(w, g, m, v):
    m = ADAM_B1 * m + (1.0 - ADAM_B1) * g
    v = ADAM_B2 * v + (1.0 - ADAM_B2) * (g * g)
    m_hat = m / (1.0 - ADAM_B1 ** ADAM_STEP)
    v_hat = v / (1.0 - ADAM_B2 ** ADAM_STEP)
    delta = -ADAM_LR * (m_hat / (jnp.sqrt(v_hat) + ADAM_EPS) + ADAM_WD * w)
    return delta, m, v


def _adamw_sharded(w, m, v, contrib, tr, name):
    depth, rows, cols = w.shape
    n_parts = contrib.shape[1]

    def body(w_ref, m_ref, v_ref, c_ref, g_ref, d_ref, mo_ref, vo_ref):
        g = c_ref[0].astype(F32)
        for p in range(1, n_parts):
            g += c_ref[p].astype(F32)
        delta, mn, vn = _adamw_math(w_ref[...], g, m_ref[...], v_ref[...])
        g_ref[...] = g
        d_ref[...] = delta
        mo_ref[...] = mn
        vo_ref[...] = vn

    tile = pl.BlockSpec((None, tr, cols), lambda l, i: (l, i, 0))
    shape = jax.ShapeDtypeStruct(w.shape, F32)
    return pl.pallas_call(
        body, name=name, grid=(depth, rows // tr),
        in_specs=[tile, tile, tile, pl.BlockSpec((None, n_parts, tr, cols), lambda l, i: (l, 0, i, 0))],
        out_specs=[tile] * 4, out_shape=[shape] * 4,
        compiler_params=_params(dimension_semantics=("parallel", "parallel")),
    )(w, m, v, contrib)


def _adamw_layer(w, m, v, contribs, l, tr, name, prev=None):
    _, rows, cols = w.shape
    n = len(contribs)

    def body(*refs):
        w_ref, m_ref, v_ref = refs[:3]
        c_refs = refs[3:3 + n]
        g_ref, d_ref, mo_ref, vo_ref = refs[-4:]
        g = c_refs[0][...].astype(F32)
        for c_ref in c_refs[1:]:
            g += c_ref[...].astype(F32)
        delta, mn, vn = _adamw_math(w_ref[...], g, m_ref[...], v_ref[...])
        g_ref[...] = g
        d_ref[...] = delta
        mo_ref[...] = mn
        vo_ref[...] = vn

    tile = pl.BlockSpec((None, tr, cols), lambda i: (l, i, 0))
    in_specs = [tile, tile, tile] + [pl.BlockSpec((None, tr, cols), lambda i, s=slot: (s, i, 0)) for _, slot in contribs]
    operands = [w, m, v] + [arr for arr, _ in contribs]
    aliases = {}
    if prev is not None:
        aliases = {len(operands) + k: k for k in range(4)}
        in_specs += [pl.BlockSpec(memory_space=pl.ANY)] * 4
        operands += list(prev)
    shape = jax.ShapeDtypeStruct(w.shape, F32)
    return pl.pallas_call(
        body, name=name, grid=(rows // tr,), in_specs=in_specs, out_specs=[tile] * 4, out_shape=[shape] * 4,
        input_output_aliases=aliases,
        compiler_params=_params(dimension_semantics=("parallel",)),
    )(*operands)


def _adamw_small(w_pack, m_pack, v_pack, g_late, g_early, shapes):
    pieces, r = {}, 0
    for name, _, n in _SMALL_ROWS:
        pieces.setdefault(name, []).append((r, n))
        r += n
    names = list(pieces)

    def body(w_ref, m_ref, v_ref, gl_ref, ge_ref, *rest):
        outs, packs = rest[:4 * len(names)], rest[4 * len(names):]
        g_l, g_e = gl_ref[0][0:SMALL_LATE_ROWS], ge_ref[0]
        for d in range(1, N_DEV):
            g_l += gl_ref[d][0:SMALL_LATE_ROWS]
            g_e += ge_ref[d]
        g = jnp.concatenate([g_l, g_e], axis=0)
        w = w_ref[...]
        r0, r1, r2 = LB_ROW0, LB_ROW0 + 8, LB_ROW0 + 16
        lg0, lg1 = w[r0:r1], w[r1:r2]
        mx = jnp.maximum(lg0, lg1)
        e0, e1 = jnp.exp(lg0 - mx), jnp.exp(lg1 - mx)
        p0, p1 = e0 / (e0 + e1), e1 / (e0 + e1)
        low = ((p0 - p0), (p0 + p1) - p0)
        dlow = [g_rows * jnp.where((lo > 0.0) & (lo < 1.0), 1.0, jnp.where((lo == 0.0) | (lo == 1.0), 0.5, 0.0))
                for g_rows, lo in ((g[r0:r1], low[0]), (g[r1:r2], low[1]))]
        dp0 = (dlow[0] + dlow[1]) - (dlow[0] + dlow[1])
        dp1 = dlow[1]
        inner = p0 * dp0 + p1 * dp1
        g = jnp.concatenate([g[:r0], p0 * (dp0 - inner), p1 * (dp1 - inner), g[r2:]], axis=0)
        delta, mn, vn = _adamw_math(w, g, m_ref[...], v_ref[...])
        for kind, val in enumerate((g, delta, mn, vn)):
            packs[kind][...] = val
            for j, name in enumerate(names):
                at = 0
                for start, n in pieces[name]:
                    outs[kind * len(names) + j][at:at + n, :] = packs[kind][start:start + n, :]
                    at += n

    rows = {name: sum(n for _, n in pieces[name]) for name in names}
    outs = pl.pallas_call(
        body, name="adamw_small",
        out_shape=[jax.ShapeDtypeStruct((rows[name], 128), F32) for _ in range(4) for name in names],
        scratch_shapes=[pltpu.VMEM(w_pack.shape, F32)] * 4, compiler_params=_params(),
    )(w_pack, m_pack, v_pack, g_late, g_early)
    return [{name: outs[kind * len(names) + j].reshape(shapes[name]) for j, name in enumerate(names)}
            for kind in range(4)]


def _pack_small(parts, first=0, last=len(_SMALL_ROWS)):
    rows = [(parts[name] if l is None else parts[name][l]).reshape(n, 128) for name, l, n in _SMALL_ROWS[first:last]]
    if last == len(_SMALL_ROWS):
        rows.append(jnp.zeros((SMALL_ROWS_PAD - sum(n for _, _, n in _SMALL_ROWS), 128), F32))
    return jnp.concatenate(rows, axis=0)


def kernel(x, c, w_ada, b_ada, g_pre, g_post, w_in, pool_w, pool_scale, lb_logits, hgrn_norm_g, w_pool_o, w_hgrn_o, w_out, loss_target, m_w_ada, m_b_ada, m_g_pre, m_g_post, m_w_in, m_pool_w, m_pool_scale, m_lb_logits, m_hgrn_norm_g, m_w_pool_o, m_w_hgrn_o, m_w_out, v_w_ada, v_b_ada, v_g_pre, v_g_post, v_w_in, v_pool_w, v_pool_scale, v_lb_logits, v_hgrn_norm_g, v_w_pool_o, v_w_hgrn_o, v_w_out):
    seq = x.shape[1]
    tm = min(512, seq)
    tm_merge = min(256, seq)
    pos = _my_position()
    me = pos[3]

    c_all = _allgather_small(c, "allgather_c").reshape(N_DEV, D_MODEL)
    b_cols = lax.dynamic_slice_in_dim(b_ada, me * ADA_COLS, ADA_COLS, axis=1)
    ada_part = _ada_fwd(c_all, w_ada, b_cols)
    ada_all = _allgather_small(ada_part.reshape(DEPTH * N_DEV, ADA_COLS), "allgather_ada")
    ada = lax.dynamic_index_in_dim(ada_all.reshape(N_DEV, DEPTH, N_DEV, ADA_COLS), me, axis=2, keepdims=False)
    ada = jnp.transpose(ada, (1, 0, 2)).reshape(DEPTH, 3 * D_MODEL)
    shift = [ada[l:l + 1, 0:D_MODEL] for l in range(DEPTH)]
    scale = [ada[l:l + 1, D_MODEL:2 * D_MODEL] for l in range(DEPTH)]
    gate = [ada[l:l + 1, 2 * D_MODEL:] for l in range(DEPTH)]

    big = dict(win=w_in, wpo=w_pool_o, who=w_hgrn_o, wout=w_out)
    units = [["win0"], ["wpo0", "who0", "wout0"], ["win1", "wpo1", "who1", "wout1"]]
    g_streams = [_gather_streams(keys) for keys in units]
    g_state = [None] * len(units)

    def gather_start(u, after):
        bufs = {}
        for k in units[u]:
            arr = big[k[:-1]]
            bufs["s_" + k] = arr[int(k[-1])].astype(WIRE_DTYPE)
            bufs["g_" + k] = _with_own_slot(bufs["s_" + k], me)
        bufs, sems, token = _comm_call(f"gather_start_{u}", bufs, start=list(g_streams[u][:2]), after=after)
        g_state[u] = dict(bufs=bufs, sems=sems)
        return token

    def gather_pass(u, after):
        st = g_state[u]
        to_chips, _, pass_on = g_streams[u]
        st["bufs"], (st["pass_sems"],), _ = _comm_call(f"gather_pass_{u}", st["bufs"], start=[pass_on],
                                                       wait=[(to_chips, st["sems"][0])], after=after)

    def gather_done(u, after=None):
        st = g_state[u]
        _, to_sibling, pass_on = g_streams[u]
        bufs, _, _ = _comm_call(f"gather_done_{u}", st["bufs"], after=after,
                                wait=[(to_sibling, st["sems"][1]), (pass_on, st["pass_sems"])])
        return {k: bufs["g_" + k] for k in units[u]}

    token = gather_start(0, ada_all)

    lb = _lb_fwd(lb_logits)

    gw = {}
    xs, saved = [x[0]], []
    for l in range(DEPTH):
        h = _prenorm_fwd(xs[l], g_pre[l:l + 1], shift[l], scale[l], tm, f"prenorm_fwd_{l}",
                         after=token if l == 0 else None)
        token = None
        if l == 0:
            gather_pass(0, h)
            gw.update(gather_done(0))
            token = gather_start(1, gw["win0"])
        else:
            gather_pass(2, h)
            gw.update(gather_done(2))
        z = _in_proj(h, gw[f"win{l}"], seq, f"in_proj_{l}", after=token)
        if l == 0:
            gather_pass(1, z)
            token = gather_start(2, g_state[1]["bufs"]["g_wpo0"])
        a_in = _pool_fwd(z, pool_w[l], pool_scale[l:l + 1], f"pool_fwd_{l}", after=token)
        o, b_in, states = _hgrn_fwd(z, lb[l:l + 1], hgrn_norm_g[l:l + 1], f"hgrn_fwd_{l}", after=token)
        if l == 0:
            gw.update(gather_done(1, b_in))
        who_l = gw[f"who{l}"].reshape(D_MODEL, D_MODEL)
        wout_l = gw[f"wout{l}"].reshape(D_MODEL, D_MODEL)
        ba, bb, merged, y, x_next = _merge_fwd(a_in, b_in, z, xs[l], gw[f"wpo{l}"], who_l, wout_l, gate[l],
                                               g_post[l:l + 1], tm_merge, f"merge_fwd_{l}")
        xs.append(x_next)
        saved.append((h, z, a_in, o, b_in, states, ba, bb, merged, y, who_l, wout_l))

    loss_part, dx = _loss_grad(xs[DEPTH], loss_target[0], tm)

    chips = _other_chips(pos)
    pair_idx = jnp.stack([_dev_index(cx, cy, pos[2]) for cx, cy in chips] + [me]).astype(jnp.int32)
    pair_rows = dict(win=256, wpo=POOL_WIDTH, who=HEAD_DIM, wout=HEAD_DIM)

    def scatter_pair_start(u, grads):
        keys = list(grads)
        pair, to_chips = _scatter_streams(keys)
        bufs = {}
        for k in keys:
            bufs["g_" + k] = grads[k]
            bufs["st_" + k] = lax.empty((4,) + grads[k].shape[1:], WIRE_DTYPE)
        bufs, (sems,), token = _comm_call(f"scatter_pair_start_{u}", bufs, start=[pair])
        return dict(u=u, keys=keys, pair=pair, to_chips=to_chips, bufs=bufs, sems=sems, token=token)

    def scatter_pair_finish(st, after):
        u, keys = st["u"], st["keys"]
        bufs, _, _ = _comm_call(f"scatter_pair_done_{u}", st["bufs"], wait=[(st["pair"], st["sems"])], after=after)
        bufs2 = {}
        for k in keys:
            bufs2["ps_" + k] = _pair_sum(bufs["g_" + k], bufs["st_" + k], pair_idx, bufs["g_" + k].shape[1],
                                         f"pair_sum_{k}")
            bufs2["ld_" + k] = lax.empty((3,) + bufs["g_" + k].shape[1:], WIRE_DTYPE)
        st.update(bufs=bufs2)

    def scatter_chips_start(st, after=None):
        bufs2, (sems,), token = _comm_call(f"scatter_chips_start_{st['u']}", st["bufs"], start=[st["to_chips"]],
                                           after=after)
        st.update(bufs=bufs2, sems=sems, token=token)

    def scatter_finish(st, after):
        bufs, _, _ = _comm_call(f"scatter_chips_done_{st['u']}", st["bufs"], wait=[(st["to_chips"], st["sems"])],
                                after=after)
        return {k: [(bufs["ps_" + k], 3), (bufs["ld_" + k], 0), (bufs["ld_" + k], 1), (bufs["ld_" + k], 2)]
                for k in st["keys"]}

    moments = dict(win=(m_w_in, v_w_in), wpo=(m_w_pool_o, v_w_pool_o), who=(m_w_hgrn_o, v_w_hgrn_o),
                   wout=(m_w_out, v_w_out))
    big_out = {}

    def finish_unit(unit, after):
        for k, contribs in scatter_finish(scat[unit], after).items():
            wname, l = k[:-1], int(k[-1])
            big_out[wname] = _adamw_layer(big[wname], moments[wname][0], moments[wname][1], contribs, l,
                                          pair_rows[wname], f"adamw_{k}", prev=big_out.get(wname))
            after = big_out[wname][0]
        return after

    d_ada, small, scat = [None] * DEPTH, [None] * DEPTH, {}
    for l in reversed(range(DEPTH)):
        h, z, a_in, o, b_in, states, ba, bb, merged, y, who_l, wout_l = saved[l]
        dy, dba, dbb, da_in, db_in, dmg, acc_post = _merge_bwd(
            dx, y, ba, bb, z, gw[f"wpo{l}"], who_l, wout_l, gate[l], g_post[l:l + 1], tm_merge, f"merge_bwd_{l}")
        g_small = {
            f"wout{l}": _grad_tn(merged, dy, 512, False, f"grad_w_out_{l}").reshape(N_DEV, HEAD_DIM, D_MODEL),
            f"who{l}": _grad_tn(b_in, dbb, 512, False, f"grad_w_hgrn_o_{l}").reshape(N_DEV, HEAD_DIM, D_MODEL),
            f"wpo{l}": _grad_tn(a_in, dba, GROUP_DIM, True, f"grad_w_pool_o_{l}")}
        st_small = scat[f"small{l}"] = scatter_pair_start(f"small{l}", g_small)
        dzh, dlb, dgn = _hgrn_bwd(db_in, z, o, states, lb[l:l + 1], hgrn_norm_g[l:l + 1], f"hgrn_bwd_{l}",
                                  after=st_small["token"])
        scatter_pair_finish(st_small, dzh)
        scatter_chips_start(st_small)
        dpv, dpg, dpw, dps = _pool_bwd(da_in, z, pool_w[l], pool_scale[l:l + 1], f"pool_bwd_{l}",
                                       after=st_small["token"])
        dz = jnp.concatenate([dpv, dpg, dzh, dmg], axis=1)
        small[l] = dict(g_post=acc_post[1], pool_w=dpw, pool_scale=dps[0], lb_logits=dlb[0], hgrn_norm_g=dgn[0])
        token = None
        if l == 0:
            parts = {name: jnp.stack([small[0][name], small[1][name]]) for name in small[0]}
            parts.update(b_ada=[None, d_ada[1]], g_pre=[None, small[1]["g_pre"]])
            sg_stream = _direct_gather_stream("sg")
            early = _pack_small(parts, 2)
            sg_bufs, (sg_sems,), token = _comm_call(
                "small_grads_start", dict(s_sg=early, g_sg=_with_own_slot(early, me)), start=[sg_stream])
        st_win = scat[f"win{l}"] = scatter_pair_start(
            f"win{l}", {f"win{l}": _in_proj_dw(h, dz, f"grad_w_in_{l}", after=token)})
        if l > 0:
            dh = _in_proj_dh(dz, gw[f"win{l}"], seq, f"in_proj_dh_{l}", after=st_win["token"])
            scatter_pair_finish(st_win, dh)
            scatter_chips_start(st_win)
        else:
            after = finish_unit("small1", st_win["token"])
            scatter_pair_finish(st_win, after)
            scatter_chips_start(st_win)
            after = st_win["token"]
            for unit in ("win1", "small0"):
                after = finish_unit(unit, after)
            dh = _in_proj_dh(dz, gw[f"win{l}"], seq, f"in_proj_dh_{l}", after=after)
        dx, acc_pre = _prenorm_bwd(xs[l], dh, dx, g_pre[l:l + 1], scale[l], tm, f"prenorm_bwd_{l}",
                                   after=st_win["token"])
        d_ada[l] = jnp.concatenate([acc_pre[0], acc_pre[1], acc_post[0]])
        small[l]["g_pre"] = acc_pre[2]
    grad_x = dx[None]

    parts = dict(b_ada=[d_ada[0]], g_pre=[small[0]["g_pre"]])
    late = jnp.concatenate([_pack_small(parts, 0, 2), jnp.broadcast_to(loss_part, (8, 128))], axis=0)
    g_late = _allgather_small(late, "allgather_late_grads")
    loss = jnp.sum(g_late[:, SMALL_LATE_ROWS, 0])
    sg_bufs, _, _ = _comm_call("small_grads_done", sg_bufs, wait=[(sg_stream, sg_sems)], after=g_late)
    g_early = sg_bufs["g_sg"]
    small_names = list(dict.fromkeys(name for name, _, _ in _SMALL_ROWS))
    weights = dict(b_ada=b_ada, g_pre=g_pre, g_post=g_post, pool_w=pool_w, pool_scale=pool_scale,
                   lb_logits=lb_logits, hgrn_norm_g=hgrn_norm_g)
    m_small = dict(b_ada=m_b_ada, g_pre=m_g_pre, g_post=m_g_post, pool_w=m_pool_w, pool_scale=m_pool_scale,
                   lb_logits=m_lb_logits, hgrn_norm_g=m_hgrn_norm_g)
    v_small = dict(b_ada=v_b_ada, g_pre=v_g_pre, g_post=v_g_post, pool_w=v_pool_w, pool_scale=v_pool_scale,
                   lb_logits=v_lb_logits, hgrn_norm_g=v_hgrn_norm_g)
    shapes = {name: weights[name].shape for name in small_names}
    small_out = _adamw_small(_pack_small(weights), _pack_small(m_small), _pack_small(v_small), g_late, g_early,
                             shapes)

    d_ada_all = jnp.stack([g_late[:, 0:24, :].reshape(N_DEV, 3 * D_MODEL),
                           g_early[:, 0:24, :].reshape(N_DEV, 3 * D_MODEL)], axis=1)
    d_cols = jnp.transpose(lax.dynamic_slice_in_dim(d_ada_all, me * ADA_COLS, ADA_COLS, axis=2), (1, 0, 2))
    g_w_ada = _ada_bwd(c_all, d_cols)
    ada_out = _adamw_sharded(w_ada, m_w_ada, v_w_ada, g_w_ada[:, None], 256, "adamw_w_ada")
    finish_unit("win0", ada_out[1][0, 0:8, 0:128] + small_out[1]["pool_scale"][0:1, 0:128])

    def leaf(kind):
        s = small_out[kind]
        return (ada_out[kind], s["b_ada"], s["g_pre"], s["g_post"], big_out["win"][kind], s["pool_w"], s["pool_scale"],
                s["lb_logits"], s["hgrn_norm_g"], big_out["wpo"][kind], big_out["who"][kind], big_out["wout"][kind])

    return (loss, grad_x) + leaf(0) + leaf(1) + leaf(2) + leaf(3)
```

```python
import jax
import jax.numpy as jnp
from jax import lax
from jax.experimental import pallas as pl
from jax.experimental.pallas import tpu as pltpu

F32 = jnp.float32
MXU_DTYPE = jnp.bfloat16
WIRE_DTYPE = jnp.bfloat16

N_DEV = 8
DEPTH = 2
D_MODEL = 1024
HEADS = 8
HEAD_DIM = 128
POOL_GROUPS = 4
GROUP_DIM = 128
POOL_WIDTH = POOL_GROUPS * GROUP_DIM
IN_WIDTH = 7168
CHUNK = 64
SUB = 16
N_SUB = CHUNK // SUB
EXP_CLAMP = 80.0
NORM_EPS = 1e-6
LOG_FLOOR = 1e-30
ADA_COLS = 3 * D_MODEL // N_DEV
IN_COLS = IN_WIDTH // N_DEV
COL_HQ, COL_HF, COL_HI, COL_HG, COL_MGP, COL_MGH = 1, 2, 3, 4, 5, 6

ADAM_LR = 0.001
ADAM_B1 = 0.9
ADAM_B2 = 0.999
ADAM_EPS = 1e-08
ADAM_WD = 0.01
ADAM_STEP = 10

VMEM_LIMIT = 48 * 1024 * 1024
MESH_ID = pl.DeviceIdType.MESH
HIGHEST = lax.Precision.HIGHEST

_SMALL_ROWS = (("b_ada", 0, 24), ("g_pre", 0, 8), ("b_ada", 1, 24), ("g_pre", 1, 8), ("g_post", None, 16),
               ("pool_w", None, 1024), ("pool_scale", None, 8), ("lb_logits", None, 16), ("hgrn_norm_g", None, 2))
SMALL_LATE_ROWS = 32
SMALL_ROWS_PAD = 1136
LB_ROW0 = 32 + 32 + 16 + 1024 + 8


def _params(**kw):
    return pltpu.CompilerParams(vmem_limit_bytes=VMEM_LIMIT, **kw)


def _sigmoid(v):
    return 1.0 / (1.0 + jnp.exp(-v))


def _dsilu(v, s):
    return s * (1.0 + v * (1.0 - s))


def _dot(a, b):
    return jnp.dot(a.astype(MXU_DTYPE), b.astype(MXU_DTYPE), preferred_element_type=F32)


def _dot_nt(a, b):
    return lax.dot_general(a.astype(MXU_DTYPE), b.astype(MXU_DTYPE), (((1,), (1,)), ((), ())),
                           preferred_element_type=F32)


def _dot_tn(a, b):
    return lax.dot_general(a.astype(MXU_DTYPE), b.astype(MXU_DTYPE), (((0,), (0,)), ((), ())),
                           preferred_element_type=F32)


def _pallas_after(body, n_in, after, *, in_specs, **kw):
    if after is None:
        return pl.pallas_call(body, in_specs=in_specs, **kw)

    def tied(*refs):
        body(*refs[:n_in], *refs[n_in + 1:])

    call = pl.pallas_call(tied, in_specs=list(in_specs) + [pl.BlockSpec(memory_space=pl.ANY)], **kw)
    return lambda *operands: call(*operands, after)


def _my_position():
    mx, my, mc = lax.axis_index("x"), lax.axis_index("y"), lax.axis_index("c")
    return mx, my, mc, 4 * mx + 2 * my + mc


def _peer(mx, my, mc, k):
    px = 1 - mx if (k >> 2) & 1 else mx
    py = 1 - my if (k >> 1) & 1 else my
    pc = 1 - mc if k & 1 else mc
    return (px, py, pc), 4 * px + 2 * py + pc


def _allgather_small(v, name, after=None):
    rows, cols = v.shape

    def body(v_ref, out_ref, send_sems, recv_sems):
        mx, my, mc, me = _my_position()
        out_ref[me] = v_ref[...]
        copies = []
        for k in range(1, N_DEV):
            peer, _ = _peer(mx, my, mc, k)
            cp = pltpu.make_async_remote_copy(
                src_ref=v_ref, dst_ref=out_ref.at[me],
                send_sem=send_sems.at[k - 1], recv_sem=recv_sems.at[k - 1],
                device_id=peer, device_id_type=MESH_ID)
            cp.start()
            copies.append(cp)
        for cp in copies:
            cp.wait()

    return _pallas_after(
        body, 1, after, name=name,
        out_shape=jax.ShapeDtypeStruct((N_DEV, rows, cols), v.dtype),
        in_specs=[pl.BlockSpec(memory_space=pltpu.VMEM)],
        out_specs=pl.BlockSpec(memory_space=pltpu.VMEM),
        scratch_shapes=[pltpu.SemaphoreType.DMA((N_DEV - 1,)), pltpu.SemaphoreType.DMA((N_DEV - 1,))],
        compiler_params=_params(),
    )(v)


class _Stream:
    def __init__(self, n, plan):
        self.n, self.plan = n, plan


def _comm_call(name, bufs, start=(), wait=(), after=None):
    names = list(bufs)

    def body(*refs):
        it = iter(refs)
        buf_refs = {n: next(it) for n in names}
        wait_sems = [(next(it), next(it)) for _ in wait]
        if after is not None:
            next(it)
        start_sems = [(next(it), next(it)) for _ in start]
        for _ in names:
            next(it)
        token = next(it)
        pos = _my_position()

        def descriptors(stream, sems):
            return [pltpu.make_async_remote_copy(src_ref=src, dst_ref=dst, send_sem=sems[0].at[k], recv_sem=sems[1].at[k],
                                                 device_id=dev, device_id_type=MESH_ID)
                    for k, (src, dst, dev) in enumerate(stream.plan(buf_refs, pos))]

        for (stream, _), sems in zip(wait, wait_sems):
            for cp in descriptors(stream, sems):
                cp.wait_send()
                cp.wait_recv()
        for stream, sems in zip(start, start_sems):
            for cp in descriptors(stream, sems):
                cp.start()
        token[...] = jnp.zeros_like(token)

    hbm = pl.BlockSpec(memory_space=pltpu.HBM)
    sem = pl.BlockSpec(memory_space=pltpu.SEMAPHORE)
    operands = [pltpu.with_memory_space_constraint(bufs[n], pltpu.HBM) for n in names]
    in_specs = [hbm] * len(names)
    for _, (send_sems, recv_sems) in wait:
        operands += [send_sems, recv_sems]
        in_specs += [sem, sem]
    if after is not None:
        operands.append(after)
        in_specs.append(pl.BlockSpec(memory_space=pl.ANY))
    out_shape, out_specs = [], []
    for stream in start:
        out_shape += [pltpu.SemaphoreType.DMA((stream.n,)), pltpu.SemaphoreType.DMA((stream.n,))]
        out_specs += [sem, sem]
    n_sem_out = len(out_shape)
    out_shape += [pltpu.HBM(bufs[n].shape, bufs[n].dtype) for n in names]
    out_specs += [hbm] * len(names)
    out_shape.append(jax.ShapeDtypeStruct((8, 128), F32))
    out_specs.append(pl.BlockSpec(memory_space=pltpu.VMEM))
    outs = pl.pallas_call(
        body, name=name, out_shape=out_shape, in_specs=in_specs, out_specs=out_specs,
        input_output_aliases={i: n_sem_out + i for i in range(len(names))},
        compiler_params=pltpu.CompilerParams(has_side_effects=pltpu.SideEffectType.DATAFLOW_SIDE_EFFECTING),
    )(*operands)
    sems = [(outs[2 * i], outs[2 * i + 1]) for i in range(len(start))]
    return dict(zip(names, outs[n_sem_out:n_sem_out + len(names)])), sems, outs[-1]


def _with_own_slot(block, me):
    return lax.dynamic_update_index_in_dim(lax.empty((N_DEV,) + block.shape, block.dtype), block, me, 0)


def _other_chips(pos):
    mx, my, _, _ = pos
    return [(1 - mx if i & 2 else mx, 1 - my if i & 1 else my) for i in (1, 2, 3)]


def _dev_index(px, py, pc):
    return 4 * px + 2 * py + pc


def _gather_streams(keys):
    def to_chips(refs, pos):
        _, _, mc, me = pos
        return [(refs["s_" + k], refs["g_" + k].at[me], (cx, cy, mc)) for k in keys for cx, cy in _other_chips(pos)]

    def to_sibling(refs, pos):
        mx, my, mc, me = pos
        return [(refs["s_" + k], refs["g_" + k].at[me], (mx, my, 1 - mc)) for k in keys]

    def pass_on(refs, pos):
        mx, my, mc, _ = pos
        out = []
        for k in keys:
            for cx, cy in _other_chips(pos):
                slot = refs["g_" + k].at[_dev_index(cx, cy, mc)]
                out.append((slot, slot, (mx, my, 1 - mc)))
        return out

    return _Stream(3 * len(keys), to_chips), _Stream(len(keys), to_sibling), _Stream(3 * len(keys), pass_on)


def _direct_gather_stream(key):
    def plan(refs, pos):
        mx, my, mc, me = pos
        return [(refs["s_" + key], refs["g_" + key].at[me], _peer(mx, my, mc, k)[0]) for k in range(1, N_DEV)]

    return _Stream(N_DEV - 1, plan)


def _scatter_streams(keys):
    def pair(refs, pos):
        mx, my, mc, _ = pos
        sib = (mx, my, 1 - mc)
        out = []
        for k in keys:
            for i, (cx, cy) in enumerate(_other_chips(pos)):
                out.append((refs["g_" + k].at[_dev_index(cx, cy, 1 - mc)], refs["st_" + k].at[i], sib))
            out.append((refs["g_" + k].at[_dev_index(mx, my, 1 - mc)], refs["st_" + k].at[3], sib))
        return out

    def chips(refs, pos):
        mc = pos[2]
        return [(refs["ps_" + k].at[i], refs["ld_" + k].at[i], (cx, cy, mc))
                for k in keys for i, (cx, cy) in enumerate(_other_chips(pos))]

    return _Stream(4 * len(keys), pair), _Stream(3 * len(keys), chips)


def _pair_sum(g, st, idx, tr, name):
    _, rows, cols = g.shape

    def body(idx_ref, g_ref, st_ref, out_ref):
        out_ref[...] = (g_ref[...].astype(F32) + st_ref[...].astype(F32)).astype(out_ref.dtype)

    return pl.pallas_call(
        body, name=name,
        grid_spec=pltpu.PrefetchScalarGridSpec(
            num_scalar_prefetch=1, grid=(4, rows // tr),
            in_specs=[pl.BlockSpec((None, tr, cols), lambda j, i, idx_ref: (idx_ref[j], i, 0)),
                      pl.BlockSpec((None, tr, cols), lambda j, i, idx_ref: (j, i, 0))],
            out_specs=pl.BlockSpec((None, tr, cols), lambda j, i, idx_ref: (j, i, 0))),
        out_shape=jax.ShapeDtypeStruct((4, rows, cols), WIRE_DTYPE),
        compiler_params=_params(dimension_semantics=("parallel", "parallel")),
    )(idx, g, st)


def _ada_fwd(c_all, w_ada, b_cols):
    def body(c_ref, w_ref, b_ref, out_ref):
        cv = c_ref[...]
        ca = cv * _sigmoid(cv)
        for l in range(DEPTH):
            out_ref[l] = jnp.dot(ca, w_ref[l], precision=HIGHEST, preferred_element_type=F32) + b_ref[l:l + 1, :]

    return pl.pallas_call(
        body, name="ada_fwd",
        out_shape=jax.ShapeDtypeStruct((DEPTH, N_DEV, ADA_COLS), F32),
        compiler_params=_params(),
    )(c_all, w_ada, b_cols)


def _ada_bwd(c_all, d_cols):
    def body(c_ref, d_ref, out_ref):
        cv = c_ref[...]
        ca = cv * _sigmoid(cv)
        for l in range(DEPTH):
            out_ref[l] = lax.dot_general(ca, d_ref[l], (((0,), (0,)), ((), ())), precision=HIGHEST,
                                         preferred_element_type=F32)

    return pl.pallas_call(
        body, name="ada_bwd",
        out_shape=jax.ShapeDtypeStruct((DEPTH, D_MODEL, ADA_COLS), F32),
        compiler_params=_params(),
    )(c_all, d_cols)


def _lower_bounds(logits):
    m = jnp.maximum(logits[0:1], logits[1:2])
    e0, e1 = jnp.exp(logits[0:1] - m), jnp.exp(logits[1:2] - m)
    den = e0 + e1
    p0, p1 = e0 / den, e1 / den
    low0 = p0 - p0
    low1 = (p0 + p1) - p0
    return (p0, p1), (low0, low1)


def _lb_fwd(lb_logits):
    def body(lg_ref, out_ref):
        _, (low0, low1) = _lower_bounds(lg_ref[...])
        out_ref[0:1, :] = jnp.clip(low0, 0.0, 1.0)
        out_ref[1:2, :] = jnp.clip(low1, 0.0, 1.0)

    return pl.pallas_call(body, name="lb_fwd", out_shape=jax.ShapeDtypeStruct(lb_logits.shape, F32),
                          compiler_params=_params())(lb_logits)


def _row_spec(cols=D_MODEL):
    return pl.BlockSpec((1, cols), lambda *_: (0, 0))


def _prenorm_fwd(x, g, shift, scale, tm, name, after=None):
    seq = x.shape[0]

    def body(x_ref, g_ref, sh_ref, sc_ref, h_ref):
        xv = x_ref[...]
        rs = lax.rsqrt(jnp.mean(xv * xv, axis=-1, keepdims=True) + NORM_EPS)
        h = (xv * rs * g_ref[...]) * (1.0 + sc_ref[...]) + sh_ref[...]
        h_ref[...] = h.astype(h_ref.dtype)

    tile = pl.BlockSpec((tm, D_MODEL), lambda i: (i, 0))
    return _pallas_after(
        body, 4, after, name=name, grid=(seq // tm,),
        in_specs=[tile, _row_spec(), _row_spec(), _row_spec()], out_specs=tile,
        out_shape=jax.ShapeDtypeStruct((seq, D_MODEL), MXU_DTYPE),
        compiler_params=_params(dimension_semantics=("parallel",)),
    )(x, g, shift, scale)


def _in_proj(h, win_g, tm, name, after=None):
    seq = h.shape[0]

    def body(h_ref, w_ref, z_ref):
        z_ref[...] = jnp.dot(h_ref[...], w_ref[...], preferred_element_type=F32)

    return _pallas_after(
        body, 2, after, name=name, grid=(N_DEV, seq // tm),
        in_specs=[pl.BlockSpec((tm, D_MODEL), lambda j, i: (i, 0)),
                  pl.BlockSpec((None, D_MODEL, IN_COLS), lambda j, i: (j, 0, 0))],
        out_specs=pl.BlockSpec((tm, IN_COLS), lambda j, i: (i, j)),
        out_shape=jax.ShapeDtypeStruct((seq, IN_WIDTH), F32),
        compiler_params=_params(dimension_semantics=("parallel", "parallel")),
    )(h, win_g)


def _shift_down(v, j, pos):
    return jnp.where(pos >= j, pltpu.roll(v, j, 0), 0.0)


def _shift_up(v, j, pos, seq):
    return jnp.where(pos < seq - j, pltpu.roll(v, seq - j, 0), 0.0)


def _select_window(g, candidates):
    out = candidates[-1]
    for i in range(len(candidates) - 2, -1, -1):
        out = jnp.where(g == i, candidates[i], out)
    return out


def _pool_mean_minus_token(u, g, pos):
    sums, acc = [], u
    for j in (1, 2, 4, 8):
        acc = acc + _shift_down(acc, j, pos)
        sums.append(acc)
    wsum = _select_window(g, sums)
    width = jnp.left_shift(2, g).astype(F32)
    count = jnp.minimum(pos.astype(F32) + 1.0, width)
    return wsum / count - u, count


def _pool_fwd(z, pool_w_l, pool_scale_l, name, after=None):
    seq = z.shape[0]

    def body(pv_ref, pg_ref, w_ref, sc_ref, out_ref):
        g = pl.program_id(0)
        pos = lax.broadcasted_iota(jnp.int32, (seq, GROUP_DIM), 0)
        pm, _ = _pool_mean_minus_token(pv_ref[...], g, pos)
        lin = _dot(pm, w_ref[...]) * sc_ref[...]
        pg = pg_ref[...]
        out_ref[...] = (lin * (pg * _sigmoid(pg))).astype(out_ref.dtype)

    return _pallas_after(
        body, 4, after, name=name, grid=(POOL_GROUPS,),
        in_specs=[pl.BlockSpec((seq, GROUP_DIM), lambda g: (0, g)),
                  pl.BlockSpec((seq, GROUP_DIM), lambda g: (0, POOL_GROUPS + g)),
                  pl.BlockSpec((None, GROUP_DIM, GROUP_DIM), lambda g: (g, 0, 0)),
                  pl.BlockSpec((1, GROUP_DIM), lambda g: (0, g))],
        out_specs=pl.BlockSpec((seq, GROUP_DIM), lambda g: (0, g)),
        out_shape=jax.ShapeDtypeStruct((seq, POOL_WIDTH), MXU_DTYPE),
        compiler_params=_params(dimension_semantics=("parallel",)),
    )(z, z, pool_w_l, pool_scale_l)


def _chunk_masks():
    row = lax.broadcasted_iota(jnp.int32, (CHUNK, CHUNK), 0)
    col = lax.broadcasted_iota(jnp.int32, (CHUNK, CHUNK), 1)
    causal = row >= col
    before_sub = col < (row // SUB) * SUB
    suffix = row <= col
    return causal, before_sub, suffix


def _masked_sums(masks, v):
    lhs = jnp.concatenate([m.astype(jnp.bfloat16) for m in masks], axis=0)
    hi = v.astype(jnp.bfloat16)
    rest = v - hi.astype(F32)
    mid = rest.astype(jnp.bfloat16)
    lo = (rest - mid.astype(F32)).astype(jnp.bfloat16)
    out = jnp.dot(lhs, hi, preferred_element_type=F32)
    out += jnp.dot(lhs, mid, preferred_element_type=F32)
    out += jnp.dot(lhs, lo, preferred_element_type=F32)
    return [out[i * CHUNK:(i + 1) * CHUNK] for i in range(len(masks))]


def _gates(zf, lb):
    sg = _sigmoid(zf)
    f = lb + (1.0 - lb) * sg
    logf = jnp.log(jnp.maximum(f, LOG_FLOOR))
    return sg, f, logf


def _intra_blocks(q_h, k_h, cum_h, base_h, causal):
    rel = cum_h - base_h
    out = []
    for i in range(N_SUB):
        rows = slice(i * SUB, (i + 1) * SUB)
        e_q = jnp.exp(rel[rows])
        base_i = jnp.concatenate([base_h[rows]] * N_SUB, axis=0)
        e_k = jnp.exp(jnp.minimum(base_i - cum_h, EXP_CLAMP))
        q_t = (q_h[rows] * e_q).astype(MXU_DTYPE)
        k_t = (k_h * e_k).astype(MXU_DTYPE)
        a_i = jnp.where(causal[rows], _dot_nt(q_t, k_t), 0.0)
        out.append((q_t, k_t, e_q, e_k, a_i))
    return out


def _hgrn_fwd(z, lb_l, gn_l, name, after=None):
    seq = z.shape[0]
    n_chunks = seq // CHUNK

    def body(hq_ref, hf_ref, hi_ref, hg_ref, lb_ref, gn_ref, o_ref, bin_ref, st_ref, state):
        @pl.when(pl.program_id(0) == 0)
        def _():
            state[...] = jnp.zeros_like(state)

        causal, before_sub, _ = _chunk_masks()
        _, f, logf = _gates(hf_ref[...], lb_ref[...])
        kk = 1.0 - f
        hq = hq_ref[...]
        q = hq * _sigmoid(hq)
        cum, base = _masked_sums([causal, before_sub], logf)
        st_ref[0] = state[...]
        for h in range(HEADS):
            sl = slice(h * HEAD_DIM, (h + 1) * HEAD_DIM)
            q_h, k_h, cum_h = q[:, sl], kk[:, sl], cum[:, sl]
            v_h = hi_ref[:, sl]
            st_h = state[h]
            blocks = _intra_blocks(q_h, k_h, cum_h, base[:, sl], causal)
            a = jnp.concatenate([b[4] for b in blocks], axis=0)
            o_h = _dot_nt(q_h * jnp.exp(cum_h), st_h) + _dot(a, v_h)
            last = jnp.sum(logf[:, sl], axis=0, keepdims=True)
            state[h] = st_h * jnp.exp(last) + _dot_tn(v_h, k_h * jnp.exp(last - cum_h))
            rs = lax.rsqrt(jnp.mean(o_h * o_h, axis=-1, keepdims=True) + NORM_EPS)
            hg = hg_ref[:, sl]
            o_ref[:, sl] = o_h
            bin_ref[:, sl] = ((o_h * rs * gn_ref[...]) * (hg * _sigmoid(hg))).astype(bin_ref.dtype)

    def col(block):
        return pl.BlockSpec((CHUNK, D_MODEL), lambda c: (c, block))

    tile = pl.BlockSpec((CHUNK, D_MODEL), lambda c: (c, 0))
    return _pallas_after(
        body, 6, after, name=name, grid=(n_chunks,),
        in_specs=[col(COL_HQ), col(COL_HF), col(COL_HI), col(COL_HG), _row_spec(), _row_spec(HEAD_DIM)],
        out_specs=[tile, tile, pl.BlockSpec((1, HEADS, HEAD_DIM, HEAD_DIM), lambda c: (c, 0, 0, 0))],
        out_shape=[jax.ShapeDtypeStruct((seq, D_MODEL), F32),
                   jax.ShapeDtypeStruct((seq, D_MODEL), MXU_DTYPE),
                   jax.ShapeDtypeStruct((n_chunks, HEADS, HEAD_DIM, HEAD_DIM), F32)],
        scratch_shapes=[pltpu.VMEM((HEADS, HEAD_DIM, HEAD_DIM), F32)],
        compiler_params=_params(dimension_semantics=("arbitrary",)),
    )(z, z, z, z, lb_l, gn_l)


def _rms_parts(y):
    rs = lax.rsqrt(jnp.mean(y * y, axis=-1, keepdims=True) + NORM_EPS)
    return rs, y * rs


def _merge_fwd(a_in, b_in, z, x, wpo_g, who_g, wout_g, gate, g_post, tm, name):
    seq = x.shape[0]

    def body(a_ref, b_ref, mgp_ref, mgh_ref, x_ref, wpo_ref, who_ref, wout_ref, gate_ref, gp_ref,
             ba_ref, bb_ref, mer_ref, y_ref, xn_ref):
        a = a_ref[...]
        ba = jnp.concatenate([_dot(a, wpo_ref[j]) for j in range(N_DEV)], axis=1)
        bb = _dot(b_ref[...], who_ref[...])
        merged = _sigmoid(mgp_ref[...]) * ba + _sigmoid(mgh_ref[...]) * bb
        y = _dot(merged, wout_ref[...])
        _, yn = _rms_parts(y)
        ba_ref[...] = ba.astype(ba_ref.dtype)
        bb_ref[...] = bb.astype(bb_ref.dtype)
        mer_ref[...] = merged.astype(mer_ref.dtype)
        y_ref[...] = y
        xn_ref[...] = x_ref[...] + gate_ref[...] * (yn * gp_ref[...])

    def tile(cols=D_MODEL, block=0):
        return pl.BlockSpec((tm, cols), lambda i: (i, block))

    full = pl.BlockSpec((D_MODEL, D_MODEL), lambda i: (0, 0))
    act = jax.ShapeDtypeStruct((seq, D_MODEL), MXU_DTYPE)
    f32 = jax.ShapeDtypeStruct((seq, D_MODEL), F32)
    return pl.pallas_call(
        body, name=name, grid=(seq // tm,),
        in_specs=[tile(POOL_WIDTH), tile(), tile(block=COL_MGP), tile(block=COL_MGH), tile(),
                  pl.BlockSpec((N_DEV, POOL_WIDTH, GROUP_DIM), lambda i: (0, 0, 0)),
                  full, full, _row_spec(), _row_spec()],
        out_specs=[tile(), tile(), tile(), tile(), tile()],
        out_shape=[act, act, act, f32, f32],
        compiler_params=_params(dimension_semantics=("parallel",)),
    )(a_in, b_in, z, z, x, wpo_g, who_g, wout_g, gate, g_post)


def _loss_grad(x_out, target, tm):
    seq = x_out.shape[0]

    def body(x_ref, t_ref, loss_ref, dx_ref):
        @pl.when(pl.program_id(0) == 0)
        def _():
            loss_ref[...] = jnp.zeros_like(loss_ref)

        err = x_ref[...] - t_ref[...]
        per_token = jnp.mean(err * err, axis=-1, keepdims=True)
        loss_ref[...] += 0.5 * jnp.sum(per_token, axis=0, keepdims=True)
        dx_ref[...] = err * (1.0 / D_MODEL)

    tile = pl.BlockSpec((tm, D_MODEL), lambda i: (i, 0))
    return pl.pallas_call(
        body, name="loss_grad", grid=(seq // tm,),
        in_specs=[tile, tile],
        out_specs=[pl.BlockSpec((1, 1), lambda i: (0, 0)), tile],
        out_shape=[jax.ShapeDtypeStruct((1, 1), F32), jax.ShapeDtypeStruct((seq, D_MODEL), F32)],
        compiler_params=_params(dimension_semantics=("arbitrary",)),
    )(x_out, target)


def _stage_copy(stage, sems, dst, slot, step, where):
    rows, cols = where(step)
    return pltpu.make_async_copy(stage.at[slot], dst.at[rows, cols], sems.at[slot])


def _stage_begin(stage, sems, dst, step, where):
    slot = step % 2

    @pl.when(step >= 2)
    def _():
        _stage_copy(stage, sems, dst, slot, step - 2, where).wait()

    return slot


def _stage_end(stage, sems, dst, step, n_steps, where):
    slot = step % 2
    _stage_copy(stage, sems, dst, slot, step, where).start()

    @pl.when(step == n_steps - 1)
    def _():
        _stage_copy(stage, sems, dst, slot, step, where).wait()
        if n_steps > 1:
            _stage_copy(stage, sems, dst, 1 - slot, step - 1, where).wait()


def _merge_bwd(dx, y, ba, bb, z, wpo_g, who_g, wout_g, gate, g_post, dz, tm, name):
    seq = dx.shape[0]
    n_steps = seq // tm

    def body(dx_ref, y_ref, ba_ref, bb_ref, mgp_ref, mgh_ref, wpo_ref, who_ref, wout_ref, gate_ref, gp_ref, _,
             dy_ref, dba_ref, dbb_ref, da_ref, db_ref, dz_ref, acc_ref, stage, sems):
        step = pl.program_id(0)

        @pl.when(step == 0)
        def _():
            acc_ref[...] = jnp.zeros_like(acc_ref)

        def where(t):
            return pl.ds(t * tm, tm), pl.ds(COL_MGP * D_MODEL, 2 * D_MODEL)

        dmg_ref = stage.at[_stage_begin(stage, sems, dz_ref, step, where)]

        dxv = dx_ref[...]
        rs, yn = _rms_parts(y_ref[...])
        acc_ref[0:1, :] += jnp.sum(dxv * yn * gp_ref[...], axis=0, keepdims=True)
        acc_ref[1:2, :] += jnp.sum(dxv * gate_ref[...] * yn, axis=0, keepdims=True)
        dyn = dxv * (gate_ref[...] * gp_ref[...])
        dy = rs * (dyn - yn * jnp.mean(dyn * yn, axis=-1, keepdims=True))
        dmerged = _dot_nt(dy, wout_ref[...])
        sp, sh = _sigmoid(mgp_ref[...]), _sigmoid(mgh_ref[...])
        dba, dbb = sp * dmerged, sh * dmerged
        dmg_ref[:, 0:D_MODEL] = (dmerged * ba_ref[...].astype(F32) * sp * (1.0 - sp)).astype(dmg_ref.dtype)
        dmg_ref[:, D_MODEL:2 * D_MODEL] = (dmerged * bb_ref[...].astype(F32) * sh * (1.0 - sh)).astype(dmg_ref.dtype)
        da = _dot_nt(dba[:, 0:GROUP_DIM], wpo_ref[0])
        for j in range(1, N_DEV):
            da += _dot_nt(dba[:, j * GROUP_DIM:(j + 1) * GROUP_DIM], wpo_ref[j])
        dy_ref[...] = dy.astype(dy_ref.dtype)
        dba_ref[...] = dba.astype(dba_ref.dtype)
        dbb_ref[...] = dbb.astype(dbb_ref.dtype)
        da_ref[...] = da
        db_ref[...] = _dot_nt(dbb, who_ref[...])
        _stage_end(stage, sems, dz_ref, step, n_steps, where)

    def tile(cols=D_MODEL, block=0):
        return pl.BlockSpec((tm, cols), lambda i: (i, block))

    full = pl.BlockSpec((D_MODEL, D_MODEL), lambda i: (0, 0))
    hbm = pl.BlockSpec(memory_space=pl.ANY)
    act = jax.ShapeDtypeStruct((seq, D_MODEL), MXU_DTYPE)
    return pl.pallas_call(
        body, name=name, grid=(n_steps,),
        in_specs=[tile(), tile(), tile(), tile(), tile(block=COL_MGP), tile(block=COL_MGH),
                  pl.BlockSpec((N_DEV, POOL_WIDTH, GROUP_DIM), lambda i: (0, 0, 0)),
                  full, full, _row_spec(), _row_spec(), hbm],
        out_specs=[tile(), tile(), tile(), tile(POOL_WIDTH), tile(), hbm,
                   pl.BlockSpec((8, D_MODEL), lambda i: (0, 0))],
        out_shape=[act, act, act, jax.ShapeDtypeStruct((seq, POOL_WIDTH), F32),
                   jax.ShapeDtypeStruct((seq, D_MODEL), F32),
                   jax.ShapeDtypeStruct(dz.shape, dz.dtype),
                   jax.ShapeDtypeStruct((8, D_MODEL), F32)],
        input_output_aliases={11: 5},
        scratch_shapes=[pltpu.VMEM((2, tm, 2 * D_MODEL), MXU_DTYPE), pltpu.SemaphoreType.DMA((2,))],
        compiler_params=_params(dimension_semantics=("arbitrary",)),
    )(dx, y, ba, bb, z, z, wpo_g, who_g, wout_g, gate, g_post, dz)


def _grad_tn(a, b, tn, dev_major, name):
    seq, ka = a.shape
    n = b.shape[1]

    def body(a_ref, b_ref, out_ref):
        out_ref[...] = _dot_tn(a_ref[...], b_ref[...]).astype(out_ref.dtype)

    if dev_major:
        out_spec = pl.BlockSpec((None, ka, tn), lambda j: (j, 0, 0))
        out_shape = jax.ShapeDtypeStruct((n // tn, ka, tn), WIRE_DTYPE)
    else:
        out_spec = pl.BlockSpec((ka, tn), lambda j: (0, j))
        out_shape = jax.ShapeDtypeStruct((ka, n), WIRE_DTYPE)
    return pl.pallas_call(
        body, name=name, grid=(n // tn,),
        in_specs=[pl.BlockSpec((seq, ka), lambda j: (0, 0)), pl.BlockSpec((seq, tn), lambda j: (0, j))],
        out_specs=out_spec, out_shape=out_shape,
        compiler_params=_params(dimension_semantics=("parallel",)),
    )(a, b)


def _hgrn_bwd(db_in, z, o, states, lb_l, gn_l, dz, name, after=None):
    seq = z.shape[0]
    n_chunks = seq // CHUNK

    last_chunk = n_chunks - 1

    def body(db_ref, hq_ref, hf_ref, hi_ref, hg_ref, o_ref, st_ref, lb_ref, gn_ref, _,
             dz_hbm, dlb_ref, dgn_ref, dstate, dq_buf, dk_buf, dg_buf, stage, sems):
        step = pl.program_id(0)

        @pl.when(step == 0)
        def _():
            dstate[...] = jnp.zeros_like(dstate)
            dlb_ref[...] = jnp.zeros_like(dlb_ref)
            dgn_ref[...] = jnp.zeros_like(dgn_ref)

        def where(t):
            return pl.ds((last_chunk - t) * CHUNK, CHUNK), pl.ds(COL_HQ * D_MODEL, 4 * D_MODEL)

        dz_ref = stage.at[_stage_begin(stage, sems, dz_hbm, step, where)]

        causal, before_sub, suffix = _chunk_masks()
        lb = lb_ref[...]
        sg, f, logf = _gates(hf_ref[...], lb)
        kk = 1.0 - f
        hq = hq_ref[...]
        sq = _sigmoid(hq)
        q = hq * sq
        cum, base = _masked_sums([causal, before_sub], logf)
        gn = gn_ref[...]
        dgn = jnp.zeros((1, HEAD_DIM), F32)
        dlast = []
        for h in range(HEADS):
            sl = slice(h * HEAD_DIM, (h + 1) * HEAD_DIM)
            q_h, k_h, cum_h = q[:, sl], kk[:, sl], cum[:, sl]
            v_h = hi_ref[:, sl]
            st_h = st_ref[0, h]
            dst_h = dstate[h]
            rs, ohat = _rms_parts(o_ref[:, sl])
            hg = hg_ref[:, sl]
            shg = _sigmoid(hg)
            d_bin = db_ref[:, sl]
            don = d_bin * (hg * shg)
            dgn += jnp.sum(don * ohat, axis=0, keepdims=True)
            dohat = don * gn
            do = rs * (dohat - ohat * jnp.mean(dohat * ohat, axis=-1, keepdims=True))
            dz_ref[:, 3 * D_MODEL + h * HEAD_DIM:3 * D_MODEL + (h + 1) * HEAD_DIM] = (
                d_bin * (ohat * gn) * _dsilu(hg, shg)).astype(dz_ref.dtype)
            last = jnp.sum(logf[:, sl], axis=0, keepdims=True)
            g_in = jnp.exp(cum_h)
            d_out = jnp.exp(last - cum_h)
            q_bar, k_bar = q_h * g_in, k_h * d_out
            blocks = _intra_blocks(q_h, k_h, cum_h, base[:, sl], causal)
            a = jnp.concatenate([b[4] for b in blocks], axis=0)
            da = jnp.where(causal, _dot_nt(do, v_h), 0.0)
            dv = _dot_tn(a, do) + _dot_nt(k_bar, dst_h)
            dq_bar, dk_bar = _dot(do, st_h), _dot(v_h, dst_h)
            dk = dk_bar * d_out
            dq_parts, dg_parts = [], []
            dg_k = k_bar * dk_bar
            dlast.append(jnp.sum(k_bar * dk_bar, axis=0, keepdims=True)
                         + jnp.exp(last) * jnp.sum(st_h * dst_h, axis=0, keepdims=True))
            for i, (q_t, k_t, e_q, e_k, _) in enumerate(blocks):
                da_i = da[i * SUB:(i + 1) * SUB].astype(MXU_DTYPE)
                dq_t = _dot(da_i, k_t)
                dk_t = _dot_tn(da_i, q_t)
                dq_parts.append(dq_t * e_q)
                dk += dk_t * e_k
                dg_parts.append(q_t.astype(F32) * dq_t)
                dg_k += k_t.astype(F32) * dk_t
            dq = dq_bar * g_in + jnp.concatenate(dq_parts, axis=0)
            dg_buf[:, sl] = q_bar * dq_bar + jnp.concatenate(dg_parts, axis=0) - dg_k
            dstate[h] = dst_h * jnp.exp(last) + _dot_tn(do, q_bar)
            dq_buf[:, sl] = dq
            dk_buf[:, sl] = dk
            dz_ref[:, 2 * D_MODEL + h * HEAD_DIM:2 * D_MODEL + (h + 1) * HEAD_DIM] = dv.astype(dz_ref.dtype)
        dgn_ref[...] += dgn
        dq_all, dk_all = dq_buf[...], dk_buf[...]
        dlogf = _masked_sums([suffix], dg_buf[...])[0] + jnp.concatenate(dlast, axis=1)
        df = jnp.where(f > LOG_FLOOR, dlogf / f, 0.0) - dk_all
        dlb_ref[...] += jnp.sum(df * (1.0 - sg), axis=0, keepdims=True)
        dz_ref[:, 0:D_MODEL] = (dq_all * _dsilu(hq, sq)).astype(dz_ref.dtype)
        dz_ref[:, D_MODEL:2 * D_MODEL] = (df * (1.0 - lb) * sg * (1.0 - sg)).astype(dz_ref.dtype)
        _stage_end(stage, sems, dz_hbm, step, n_chunks, where)

    def col(block):
        return pl.BlockSpec((CHUNK, D_MODEL), lambda c: (last_chunk - c, block))

    hbm = pl.BlockSpec(memory_space=pl.ANY)
    return _pallas_after(
        body, 10, after, name=name, grid=(n_chunks,),
        in_specs=[col(0), col(COL_HQ), col(COL_HF), col(COL_HI), col(COL_HG), col(0),
                  pl.BlockSpec((1, HEADS, HEAD_DIM, HEAD_DIM), lambda c: (last_chunk - c, 0, 0, 0)),
                  _row_spec(), _row_spec(HEAD_DIM), hbm],
        out_specs=[hbm, _row_spec(), _row_spec(HEAD_DIM)],
        out_shape=[jax.ShapeDtypeStruct(dz.shape, dz.dtype),
                   jax.ShapeDtypeStruct((1, D_MODEL), F32), jax.ShapeDtypeStruct((1, HEAD_DIM), F32)],
        input_output_aliases={9: 0},
        scratch_shapes=[pltpu.VMEM((HEADS, HEAD_DIM, HEAD_DIM), F32)] + [pltpu.VMEM((CHUNK, D_MODEL), F32)] * 3
        + [pltpu.VMEM((2, CHUNK, 4 * D_MODEL), MXU_DTYPE), pltpu.SemaphoreType.DMA((2,))],
        compiler_params=_params(dimension_semantics=("arbitrary",)),
    )(db_in, z, z, z, z, o, states, lb_l, gn_l, dz)


def _pool_bwd(da_in, z, pool_w_l, pool_scale_l, dz, name, after=None):
    seq = z.shape[0]

    def body(da_ref, pv_ref, pg_ref, w_ref, sc_ref, _, dz_hbm, dw_ref, dsc_ref, stage_pv, stage_pg, sems_pv, sems_pg):
        g = pl.program_id(0)

        def where_pv(t):
            return pl.ds(0, seq), pl.ds(pl.multiple_of(t * GROUP_DIM, GROUP_DIM), GROUP_DIM)

        def where_pg(t):
            return pl.ds(0, seq), pl.ds(pl.multiple_of(POOL_WIDTH + t * GROUP_DIM, GROUP_DIM), GROUP_DIM)

        dpv_ref = stage_pv.at[_stage_begin(stage_pv, sems_pv, dz_hbm, g, where_pv)]
        dpg_ref = stage_pg.at[_stage_begin(stage_pg, sems_pg, dz_hbm, g, where_pg)]
        pos = lax.broadcasted_iota(jnp.int32, (seq, GROUP_DIM), 0)
        pm, count = _pool_mean_minus_token(pv_ref[...], g, pos)
        lin0 = _dot(pm, w_ref[...])
        pg = pg_ref[...]
        spg = _sigmoid(pg)
        da = da_ref[...]
        dlin = da * (pg * spg)
        dpg_ref[...] = (da * (lin0 * sc_ref[...]) * _dsilu(pg, spg)).astype(dpg_ref.dtype)
        dsc_ref[...] = jnp.sum(dlin * lin0, axis=0, keepdims=True)
        dl0 = dlin * sc_ref[...]
        dw_ref[...] = _dot_tn(pm, dl0)
        dpm = _dot_nt(dl0, w_ref[...])
        sums, acc = [], dpm / count
        for j in (1, 2, 4, 8):
            acc = acc + _shift_up(acc, j, pos, seq)
            sums.append(acc)
        dpv_ref[...] = (_select_window(g, sums) - dpm).astype(dpv_ref.dtype)
        _stage_end(stage_pv, sems_pv, dz_hbm, g, POOL_GROUPS, where_pv)
        _stage_end(stage_pg, sems_pg, dz_hbm, g, POOL_GROUPS, where_pg)

    grp = pl.BlockSpec((seq, GROUP_DIM), lambda g: (0, g))
    hbm = pl.BlockSpec(memory_space=pl.ANY)
    stage = pltpu.VMEM((2, seq, GROUP_DIM), MXU_DTYPE)
    return _pallas_after(
        body, 6, after, name=name, grid=(POOL_GROUPS,),
        in_specs=[grp, grp, pl.BlockSpec((seq, GROUP_DIM), lambda g: (0, POOL_GROUPS + g)),
                  pl.BlockSpec((None, GROUP_DIM, GROUP_DIM), lambda g: (g, 0, 0)),
                  pl.BlockSpec((1, GROUP_DIM), lambda g: (0, g)), hbm],
        out_specs=[hbm, pl.BlockSpec((None, GROUP_DIM, GROUP_DIM), lambda g: (g, 0, 0)),
                   pl.BlockSpec((1, GROUP_DIM), lambda g: (0, g))],
        out_shape=[jax.ShapeDtypeStruct(dz.shape, dz.dtype),
                   jax.ShapeDtypeStruct((POOL_GROUPS, GROUP_DIM, GROUP_DIM), F32),
                   jax.ShapeDtypeStruct((1, POOL_WIDTH), F32)],
        input_output_aliases={5: 0},
        scratch_shapes=[stage, stage, pltpu.SemaphoreType.DMA((2,)), pltpu.SemaphoreType.DMA((2,))],
        compiler_params=_params(dimension_semantics=("arbitrary",)),
    )(da_in, z, z, pool_w_l, pool_scale_l, dz)


def _in_proj_dw(h, dz, name, after=None):
    seq = h.shape[0]

    def body(h_ref, dz_ref, out_ref):
        out_ref[...] = lax.dot_general(h_ref[...], dz_ref[...], (((0,), (0,)), ((), ())),
                                       preferred_element_type=F32).astype(out_ref.dtype)

    return _pallas_after(
        body, 2, after, name=name, grid=(N_DEV,),
        in_specs=[pl.BlockSpec((seq, D_MODEL), lambda j: (0, 0)), pl.BlockSpec((seq, IN_COLS), lambda j: (0, j))],
        out_specs=pl.BlockSpec((None, D_MODEL, IN_COLS), lambda j: (j, 0, 0)),
        out_shape=jax.ShapeDtypeStruct((N_DEV, D_MODEL, IN_COLS), WIRE_DTYPE),
        compiler_params=_params(dimension_semantics=("parallel",)),
    )(h, dz)


def _in_proj_dh(dz, win_g, tm, name, after=None):
    seq = dz.shape[0]

    def body(dz_ref, w_ref, dh_ref):
        @pl.when(pl.program_id(1) == 0)
        def _():
            dh_ref[...] = jnp.zeros_like(dh_ref)

        dh_ref[...] += lax.dot_general(dz_ref[...], w_ref[...], (((1,), (1,)), ((), ())),
                                       preferred_element_type=F32)

    return _pallas_after(
        body, 2, after, name=name, grid=(seq // tm, N_DEV),
        in_specs=[pl.BlockSpec((tm, IN_COLS), lambda i, j: (i, j)),
                  pl.BlockSpec((None, D_MODEL, IN_COLS), lambda i, j: (j, 0, 0))],
        out_specs=pl.BlockSpec((tm, D_MODEL), lambda i, j: (i, 0)),
        out_shape=jax.ShapeDtypeStruct((seq, D_MODEL), F32),
        compiler_params=_params(dimension_semantics=("parallel", "arbitrary")),
    )(dz, win_g)


def _prenorm_bwd(x, dh, dx_res, g, scale, tm, name, after=None):
    seq = x.shape[0]

    def body(x_ref, dh_ref, dxr_ref, g_ref, sc_ref, dx_ref, acc_ref):
        @pl.when(pl.program_id(0) == 0)
        def _():
            acc_ref[...] = jnp.zeros_like(acc_ref)

        rs, xn = _rms_parts(x_ref[...])
        dh = dh_ref[...]
        acc_ref[0:1, :] += jnp.sum(dh, axis=0, keepdims=True)
        acc_ref[1:2, :] += jnp.sum(dh * (xn * g_ref[...]), axis=0, keepdims=True)
        dhn = dh * (1.0 + sc_ref[...])
        acc_ref[2:3, :] += jnp.sum(dhn * xn, axis=0, keepdims=True)
        dxn = dhn * g_ref[...]
        dx_ref[...] = rs * (dxn - xn * jnp.mean(dxn * xn, axis=-1, keepdims=True)) + dxr_ref[...]

    tile = pl.BlockSpec((tm, D_MODEL), lambda i: (i, 0))
    return _pallas_after(
        body, 5, after, name=name, grid=(seq // tm,),
        in_specs=[tile, tile, tile, _row_spec(), _row_spec()],
        out_specs=[tile, pl.BlockSpec((8, D_MODEL), lambda i: (0, 0))],
        out_shape=[jax.ShapeDtypeStruct((seq, D_MODEL), F32), jax.ShapeDtypeStruct((8, D_MODEL), F32)],
        compiler_params=_params(dimension_semantics=("arbitrary",)),
    )(x, dh, dx_res, g, scale)


def _adamw_math(w, g, m, v):
    m = ADAM_B1 * m + (1.0 - ADAM_B1) * g
    v = ADAM_B2 * v + (1.0 - ADAM_B2) * (g * g)
    m_hat = m / (1.0 - ADAM_B1 ** ADAM_STEP)
    v_hat = v / (1.0 - ADAM_B2 ** ADAM_STEP)
    delta = -ADAM_LR * (m_hat / (jnp.sqrt(v_hat) + ADAM_EPS) + ADAM_WD * w)
    return delta, m, v


def _adamw_sharded(w, m, v, contrib, tr, name):
    depth, rows, cols = w.shape
    n_parts = contrib.shape[1]

    def body(w_ref, m_ref, v_ref, c_ref, g_ref, d_ref, mo_ref, vo_ref):
        g = c_ref[0].astype(F32)
        for p in range(1, n_parts):
            g += c_ref[p].astype(F32)
        delta, mn, vn = _adamw_math(w_ref[...], g, m_ref[...], v_ref[...])
        g_ref[...] = g
        d_ref[...] = delta
        mo_ref[...] = mn
        vo_ref[...] = vn

    tile = pl.BlockSpec((None, tr, cols), lambda l, i: (l, i, 0))
    shape = jax.ShapeDtypeStruct(w.shape, F32)
    return pl.pallas_call(
        body, name=name, grid=(depth, rows // tr),
        in_specs=[tile, tile, tile, pl.BlockSpec((None, n_parts, tr, cols), lambda l, i: (l, 0, i, 0))],
        out_specs=[tile] * 4, out_shape=[shape] * 4,
        compiler_params=_params(dimension_semantics=("parallel", "parallel")),
    )(w, m, v, contrib)


def _adamw_layer(w, m, v, contribs, l, tr, name, prev=None):
    _, rows, cols = w.shape
    n = len(contribs)

    def body(*refs):
        w_ref, m_ref, v_ref = refs[:3]
        c_refs = refs[3:3 + n]
        g_ref, d_ref, mo_ref, vo_ref = refs[-4:]
        g = c_refs[0][...].astype(F32)
        for c_ref in c_refs[1:]:
            g += c_ref[...].astype(F32)
        delta, mn, vn = _adamw_math(w_ref[...], g, m_ref[...], v_ref[...])
        g_ref[...] = g
        d_ref[...] = delta
        mo_ref[...] = mn
        vo_ref[...] = vn

    tile = pl.BlockSpec((None, tr, cols), lambda i: (l, i, 0))
    in_specs = [tile, tile, tile] + [pl.BlockSpec((None, tr, cols), lambda i, s=slot: (s, i, 0)) for _, slot in contribs]
    operands = [w, m, v] + [arr for arr, _ in contribs]
    aliases = {}
    if prev is not None:
        aliases = {len(operands) + k: k for k in range(4)}
        in_specs += [pl.BlockSpec(memory_space=pl.ANY)] * 4
        operands += list(prev)
    shape = jax.ShapeDtypeStruct(w.shape, F32)
    return pl.pallas_call(
        body, name=name, grid=(rows // tr,), in_specs=in_specs, out_specs=[tile] * 4, out_shape=[shape] * 4,
        input_output_aliases=aliases,
        compiler_params=_params(dimension_semantics=("parallel",)),
    )(*operands)


def _adamw_small(w_pack, m_pack, v_pack, g_late, g_early, shapes):
    pieces, r = {}, 0
    for name, _, n in _SMALL_ROWS:
        pieces.setdefault(name, []).append((r, n))
        r += n
    names = list(pieces)

    def body(w_ref, m_ref, v_ref, gl_ref, ge_ref, *rest):
        outs, packs = rest[:4 * len(names)], rest[4 * len(names):]
        g_l, g_e = gl_ref[0][0:SMALL_LATE_ROWS], ge_ref[0]
        for d in range(1, N_DEV):
            g_l += gl_ref[d][0:SMALL_LATE_ROWS]
            g_e += ge_ref[d]
        g = jnp.concatenate([g_l, g_e], axis=0)
        w = w_ref[...]
        r0, r1, r2 = LB_ROW0, LB_ROW0 + 8, LB_ROW0 + 16
        lg0, lg1 = w[r0:r1], w[r1:r2]
        mx = jnp.maximum(lg0, lg1)
        e0, e1 = jnp.exp(lg0 - mx), jnp.exp(lg1 - mx)
        p0, p1 = e0 / (e0 + e1), e1 / (e0 + e1)
        low = ((p0 - p0), (p0 + p1) - p0)
        dlow = [g_rows * jnp.where((lo > 0.0) & (lo < 1.0), 1.0, jnp.where((lo == 0.0) | (lo == 1.0), 0.5, 0.0))
                for g_rows, lo in ((g[r0:r1], low[0]), (g[r1:r2], low[1]))]
        dp0 = (dlow[0] + dlow[1]) - (dlow[0] + dlow[1])
        dp1 = dlow[1]
        inner = p0 * dp0 + p1 * dp1
        g = jnp.concatenate([g[:r0], p0 * (dp0 - inner), p1 * (dp1 - inner), g[r2:]], axis=0)
        delta, mn, vn = _adamw_math(w, g, m_ref[...], v_ref[...])
        for kind, val in enumerate((g, delta, mn, vn)):
            packs[kind][...] = val
            for j, name in enumerate(names):
                at = 0
                for start, n in pieces[name]:
                    outs[kind * len(names) + j][at:at + n, :] = packs[kind][start:start + n, :]
                    at += n

    rows = {name: sum(n for _, n in pieces[name]) for name in names}
    outs = pl.pallas_call(
        body, name="adamw_small",
        out_shape=[jax.ShapeDtypeStruct((rows[name], 128), F32) for _ in range(4) for name in names],
        scratch_shapes=[pltpu.VMEM(w_pack.shape, F32)] * 4, compiler_params=_params(),
    )(w_pack, m_pack, v_pack, g_late, g_early)
    return [{name: outs[kind * len(names) + j].reshape(shapes[name]) for j, name in enumerate(names)}
            for kind in range(4)]


def _pack_small(parts, first=0, last=len(_SMALL_ROWS)):
    rows = [(parts[name] if l is None else parts[name][l]).reshape(n, 128) for name, l, n in _SMALL_ROWS[first:last]]
    if last == len(_SMALL_ROWS):
        rows.append(jnp.zeros((SMALL_ROWS_PAD - sum(n for _, _, n in _SMALL_ROWS), 128), F32))
    return jnp.concatenate(rows, axis=0)


def kernel(x, c, w_ada, b_ada, g_pre, g_post, w_in, pool_w, pool_scale, lb_logits, hgrn_norm_g, w_pool_o, w_hgrn_o, w_out, loss_target, m_w_ada, m_b_ada, m_g_pre, m_g_post, m_w_in, m_pool_w, m_pool_scale, m_lb_logits, m_hgrn_norm_g, m_w_pool_o, m_w_hgrn_o, m_w_out, v_w_ada, v_b_ada, v_g_pre, v_g_post, v_w_in, v_pool_w, v_pool_scale, v_lb_logits, v_hgrn_norm_g, v_w_pool_o, v_w_hgrn_o, v_w_out):
    seq = x.shape[1]
    tm = min(512, seq)
    tm_merge = min(256, seq)
    pos = _my_position()
    me = pos[3]

    c_all = _allgather_small(c, "allgather_c").reshape(N_DEV, D_MODEL)
    b_cols = lax.dynamic_slice_in_dim(b_ada, me * ADA_COLS, ADA_COLS, axis=1)
    ada_part = _ada_fwd(c_all, w_ada, b_cols)
    ada_all = _allgather_small(ada_part.reshape(DEPTH * N_DEV, ADA_COLS), "allgather_ada")
    ada = lax.dynamic_index_in_dim(ada_all.reshape(N_DEV, DEPTH, N_DEV, ADA_COLS), me, axis=2, keepdims=False)
    ada = jnp.transpose(ada, (1, 0, 2)).reshape(DEPTH, 3 * D_MODEL)
    shift = [ada[l:l + 1, 0:D_MODEL] for l in range(DEPTH)]
    scale = [ada[l:l + 1, D_MODEL:2 * D_MODEL] for l in range(DEPTH)]
    gate = [ada[l:l + 1, 2 * D_MODEL:] for l in range(DEPTH)]

    big = dict(win=w_in, wpo=w_pool_o, who=w_hgrn_o, wout=w_out)
    units = [["win0"], ["wpo0", "who0", "wout0"], ["win1", "wpo1", "who1", "wout1"]]
    g_streams = [_gather_streams(keys) for keys in units]
    g_state = [None] * len(units)

    def gather_start(u, after):
        bufs = {}
        for k in units[u]:
            arr = big[k[:-1]]
            bufs["s_" + k] = arr[int(k[-1])].astype(WIRE_DTYPE)
            bufs["g_" + k] = _with_own_slot(bufs["s_" + k], me)
        bufs, sems, token = _comm_call(f"gather_start_{u}", bufs, start=list(g_streams[u][:2]), after=after)
        g_state[u] = dict(bufs=bufs, sems=sems)
        return token

    def gather_pass(u, after):
        st = g_state[u]
        to_chips, _, pass_on = g_streams[u]
        st["bufs"], (st["pass_sems"],), _ = _comm_call(f"gather_pass_{u}", st["bufs"], start=[pass_on],
                                                       wait=[(to_chips, st["sems"][0])], after=after)

    def gather_done(u, after=None):
        st = g_state[u]
        _, to_sibling, pass_on = g_streams[u]
        bufs, _, _ = _comm_call(f"gather_done_{u}", st["bufs"], after=after,
                                wait=[(to_sibling, st["sems"][1]), (pass_on, st["pass_sems"])])
        return {k: bufs["g_" + k] for k in units[u]}

    token = gather_start(0, ada_all)

    lb = _lb_fwd(lb_logits)

    gw = {}
    xs, saved = [x[0]], []
    for l in range(DEPTH):
        h = _prenorm_fwd(xs[l], g_pre[l:l + 1], shift[l], scale[l], tm, f"prenorm_fwd_{l}",
                         after=token if l == 0 else None)
        token = None
        if l == 0:
            gather_pass(0, h)
            gw.update(gather_done(0))
            token = gather_start(1, gw["win0"])
        else:
            gather_pass(2, h)
            gw.update(gather_done(2))
        z = _in_proj(h, gw[f"win{l}"], seq, f"in_proj_{l}", after=token)
        if l == 0:
            gather_pass(1, z)
            token = gather_start(2, g_state[1]["bufs"]["g_wpo0"])
        a_in = _pool_fwd(z, pool_w[l], pool_scale[l:l + 1], f"pool_fwd_{l}", after=token)
        o, b_in, states = _hgrn_fwd(z, lb[l:l + 1], hgrn_norm_g[l:l + 1], f"hgrn_fwd_{l}", after=token)
        if l == 0:
            gw.update(gather_done(1, b_in))
        who_l = gw[f"who{l}"].reshape(D_MODEL, D_MODEL)
        wout_l = gw[f"wout{l}"].reshape(D_MODEL, D_MODEL)
        ba, bb, merged, y, x_next = _merge_fwd(a_in, b_in, z, xs[l], gw[f"wpo{l}"], who_l, wout_l, gate[l],
                                               g_post[l:l + 1], tm_merge, f"merge_fwd_{l}")
        xs.append(x_next)
        saved.append((h, z, a_in, o, b_in, states, ba, bb, merged, y, who_l, wout_l))

    loss_part, dx = _loss_grad(xs[DEPTH], loss_target[0], tm)

    chips = _other_chips(pos)
    pair_idx = jnp.stack([_dev_index(cx, cy, pos[2]) for cx, cy in chips] + [me]).astype(jnp.int32)
    pair_rows = dict(win=256, wpo=POOL_WIDTH, who=HEAD_DIM, wout=HEAD_DIM)

    def scatter_pair_start(u, grads):
        keys = list(grads)
        pair, to_chips = _scatter_streams(keys)
        bufs = {}
        for k in keys:
            bufs["g_" + k] = grads[k]
            bufs["st_" + k] = lax.empty((4,) + grads[k].shape[1:], WIRE_DTYPE)
        bufs, (sems,), token = _comm_call(f"scatter_pair_start_{u}", bufs, start=[pair])
        return dict(u=u, keys=keys, pair=pair, to_chips=to_chips, bufs=bufs, sems=sems, token=token)

    def scatter_pair_finish(st, after):
        u, keys = st["u"], st["keys"]
        bufs, _, _ = _comm_call(f"scatter_pair_done_{u}", st["bufs"], wait=[(st["pair"], st["sems"])], after=after)
        bufs2 = {}
        for k in keys:
            bufs2["ps_" + k] = _pair_sum(bufs["g_" + k], bufs["st_" + k], pair_idx, bufs["g_" + k].shape[1],
                                         f"pair_sum_{k}")
            bufs2["ld_" + k] = lax.empty((3,) + bufs["g_" + k].shape[1:], WIRE_DTYPE)
        st.update(bufs=bufs2)

    def scatter_chips_start(st, after=None):
        bufs2, (sems,), token = _comm_call(f"scatter_chips_start_{st['u']}", st["bufs"], start=[st["to_chips"]],
                                           after=after)
        st.update(bufs=bufs2, sems=sems, token=token)

    def scatter_finish(st, after):
        bufs, _, _ = _comm_call(f"scatter_chips_done_{st['u']}", st["bufs"], wait=[(st["to_chips"], st["sems"])],
                                after=after)
        return {k: [(bufs["ps_" + k], 3), (bufs["ld_" + k], 0), (bufs["ld_" + k], 1), (bufs["ld_" + k], 2)]
                for k in st["keys"]}

    moments = dict(win=(m_w_in, v_w_in), wpo=(m_w_pool_o, v_w_pool_o), who=(m_w_hgrn_o, v_w_hgrn_o),
                   wout=(m_w_out, v_w_out))
    big_out = {}

    def finish_unit(unit, after):
        for k, contribs in scatter_finish(scat[unit], after).items():
            wname, l = k[:-1], int(k[-1])
            big_out[wname] = _adamw_layer(big[wname], moments[wname][0], moments[wname][1], contribs, l,
                                          pair_rows[wname], f"adamw_{k}", prev=big_out.get(wname))
            after = big_out[wname][0]
        return after

    d_ada, small, scat = [None] * DEPTH, [None] * DEPTH, {}
    for l in reversed(range(DEPTH)):
        h, z, a_in, o, b_in, states, ba, bb, merged, y, who_l, wout_l = saved[l]
        dy, dba, dbb, da_in, db_in, dz, acc_post = _merge_bwd(
            dx, y, ba, bb, z, gw[f"wpo{l}"], who_l, wout_l, gate[l], g_post[l:l + 1],
            lax.empty((seq, IN_WIDTH), MXU_DTYPE), tm_merge, f"merge_bwd_{l}")
        g_small = {
            f"wout{l}": _grad_tn(merged, dy, 512, False, f"grad_w_out_{l}").reshape(N_DEV, HEAD_DIM, D_MODEL),
            f"who{l}": _grad_tn(b_in, dbb, 512, False, f"grad_w_hgrn_o_{l}").reshape(N_DEV, HEAD_DIM, D_MODEL),
            f"wpo{l}": _grad_tn(a_in, dba, GROUP_DIM, True, f"grad_w_pool_o_{l}")}
        st_small = scat[f"small{l}"] = scatter_pair_start(f"small{l}", g_small)
        dz, dlb, dgn = _hgrn_bwd(db_in, z, o, states, lb[l:l + 1], hgrn_norm_g[l:l + 1], dz, f"hgrn_bwd_{l}",
                                 after=st_small["token"])
        scatter_pair_finish(st_small, dlb)
        scatter_chips_start(st_small)
        dz, dpw, dps = _pool_bwd(da_in, z, pool_w[l], pool_scale[l:l + 1], dz, f"pool_bwd_{l}",
                                 after=st_small["token"])
        small[l] = dict(g_post=acc_post[1], pool_w=dpw, pool_scale=dps[0], lb_logits=dlb[0], hgrn_norm_g=dgn[0])
        token = None
        if l == 0:
            parts = {name: jnp.stack([small[0][name], small[1][name]]) for name in small[0]}
            parts.update(b_ada=[None, d_ada[1]], g_pre=[None, small[1]["g_pre"]])
            sg_stream = _direct_gather_stream("sg")
            early = _pack_small(parts, 2)
            sg_bufs, (sg_sems,), token = _comm_call(
                "small_grads_start", dict(s_sg=early, g_sg=_with_own_slot(early, me)), start=[sg_stream])
        st_win = scat[f"win{l}"] = scatter_pair_start(
            f"win{l}", {f"win{l}": _in_proj_dw(h, dz, f"grad_w_in_{l}", after=token)})
        if l > 0:
            dh = _in_proj_dh(dz, gw[f"win{l}"], seq, f"in_proj_dh_{l}", after=st_win["token"])
            scatter_pair_finish(st_win, dh)
            scatter_chips_start(st_win)
        else:
            after = finish_unit("small1", st_win["token"])
            scatter_pair_finish(st_win, after)
            scatter_chips_start(st_win)
            after = st_win["token"]
            for unit in ("win1", "small0"):
                after = finish_unit(unit, after)
            dh = _in_proj_dh(dz, gw[f"win{l}"], seq, f"in_proj_dh_{l}", after=after)
        dx, acc_pre = _prenorm_bwd(xs[l], dh, dx, g_pre[l:l + 1], scale[l], tm, f"prenorm_bwd_{l}",
                                   after=st_win["token"])
        d_ada[l] = jnp.concatenate([acc_pre[0], acc_pre[1], acc_post[0]])
        small[l]["g_pre"] = acc_pre[2]
    grad_x = dx[None]

    parts = dict(b_ada=[d_ada[0]], g_pre=[small[0]["g_pre"]])
    late = jnp.concatenate([_pack_small(parts, 0, 2), jnp.broadcast_to(loss_part, (8, 128))], axis=0)
    g_late = _allgather_small(late, "allgather_late_grads")
    loss = jnp.sum(g_late[:, SMALL_LATE_ROWS, 0])
    sg_bufs, _, _ = _comm_call("small_grads_done", sg_bufs, wait=[(sg_stream, sg_sems)], after=g_late)
    g_early = sg_bufs["g_sg"]
    small_names = list(dict.fromkeys(name for name, _, _ in _SMALL_ROWS))
    weights = dict(b_ada=b_ada, g_pre=g_pre, g_post=g_post, pool_w=pool_w, pool_scale=pool_scale,
                   lb_logits=lb_logits, hgrn_norm_g=hgrn_norm_g)
    m_small = dict(b_ada=m_b_ada, g_pre=m_g_pre, g_post=m_g_post, pool_w=m_pool_w, pool_scale=m_pool_scale,
                   lb_logits=m_lb_logits, hgrn_norm_g=m_hgrn_norm_g)
    v_small = dict(b_ada=v_b_ada, g_pre=v_g_pre, g_post=v_g_post, pool_w=v_pool_w, pool_scale=v_pool_scale,
                   lb_logits=v_lb_logits, hgrn_norm_g=v_hgrn_norm_g)
    shapes = {name: weights[name].shape for name in small_names}
    small_out = _adamw_small(_pack_small(weights), _pack_small(m_small), _pack_small(v_small), g_late, g_early,
                             shapes)

    d_ada_all = jnp.stack([g_late[:, 0:24, :].reshape(N_DEV, 3 * D_MODEL),
                           g_early[:, 0:24, :].reshape(N_DEV, 3 * D_MODEL)], axis=1)
    d_cols = jnp.transpose(lax.dynamic_slice_in_dim(d_ada_all, me * ADA_COLS, ADA_COLS, axis=2), (1, 0, 2))
    g_w_ada = _ada_bwd(c_all, d_cols)
    ada_out = _adamw_sharded(w_ada, m_w_ada, v_w_ada, g_w_ada[:, None], 256, "adamw_w_ada")
    finish_unit("win0", ada_out[1][0, 0:8, 0:128] + small_out[1]["pool_scale"][0:1, 0:128])

    def leaf(kind):
        s = small_out[kind]
        return (ada_out[kind], s["b_ada"], s["g_pre"], s["g_post"], big_out["win"][kind], s["pool_w"], s["pool_scale"],
                s["lb_logits"], s["hgrn_norm_g"], big_out["wpo"][kind], big_out["who"][kind], big_out["wout"][kind])

    return (loss, grad_x) + leaf(0) + leaf(1) + leaf(2) + leaf(3)
```

```python
import jax
import jax.numpy as jnp
from jax import lax
from jax.experimental import pallas as pl
from jax.experimental.pallas import tpu as pltpu

F32 = jnp.float32
MXU_DTYPE = jnp.bfloat16
WIRE_DTYPE = jnp.bfloat16

N_DEV = 8
DEPTH = 2
D_MODEL = 1024
HEADS = 8
HEAD_DIM = 128
POOL_GROUPS = 4
GROUP_DIM = 128
POOL_WIDTH = POOL_GROUPS * GROUP_DIM
IN_WIDTH = 7168
CHUNK = 64
SUB = 16
N_SUB = CHUNK // SUB
EXP_CLAMP = 80.0
NORM_EPS = 1e-6
LOG_FLOOR = 1e-30
ADA_COLS = 3 * D_MODEL // N_DEV
IN_COLS = IN_WIDTH // N_DEV
COL_HQ, COL_HF, COL_HI, COL_HG, COL_MGP, COL_MGH = 1, 2, 3, 4, 5, 6

ADAM_LR = 0.001
ADAM_B1 = 0.9
ADAM_B2 = 0.999
ADAM_EPS = 1e-08
ADAM_WD = 0.01
ADAM_STEP = 10

VMEM_LIMIT = 48 * 1024 * 1024
MESH_ID = pl.DeviceIdType.MESH
HIGHEST = lax.Precision.HIGHEST

_SMALL_ROWS = (("b_ada", 0, 24), ("g_pre", 0, 8), ("b_ada", 1, 24), ("g_pre", 1, 8), ("g_post", None, 16),
               ("pool_w", None, 1024), ("pool_scale", None, 8), ("lb_logits", None, 16), ("hgrn_norm_g", None, 2))
SMALL_LATE_ROWS = 32
SMALL_ROWS_PAD = 1136
LB_ROW0 = 32 + 32 + 16 + 1024 + 8


def _params(**kw):
    return pltpu.CompilerParams(vmem_limit_bytes=VMEM_LIMIT, **kw)


def _sigmoid(v):
    return 1.0 / (1.0 + jnp.exp(-v))


def _dsilu(v, s):
    return s * (1.0 + v * (1.0 - s))


def _dot(a, b):
    return jnp.dot(a.astype(MXU_DTYPE), b.astype(MXU_DTYPE), preferred_element_type=F32)


def _dot_nt(a, b):
    return lax.dot_general(a.astype(MXU_DTYPE), b.astype(MXU_DTYPE), (((1,), (1,)), ((), ())),
                           preferred_element_type=F32)


def _dot_tn(a, b):
    return lax.dot_general(a.astype(MXU_DTYPE), b.astype(MXU_DTYPE), (((0,), (0,)), ((), ())),
                           preferred_element_type=F32)


def _pallas_after(body, n_in, after, *, in_specs, **kw):
    if after is None:
        return pl.pallas_call(body, in_specs=in_specs, **kw)

    def tied(*refs):
        body(*refs[:n_in], *refs[n_in + 1:])

    call = pl.pallas_call(tied, in_specs=list(in_specs) + [pl.BlockSpec(memory_space=pl.ANY)], **kw)
    return lambda *operands: call(*operands, after)


def _my_position():
    mx, my, mc = lax.axis_index("x"), lax.axis_index("y"), lax.axis_index("c")
    return mx, my, mc, 4 * mx + 2 * my + mc


def _peer(mx, my, mc, k):
    px = 1 - mx if (k >> 2) & 1 else mx
    py = 1 - my if (k >> 1) & 1 else my
    pc = 1 - mc if k & 1 else mc
    return (px, py, pc), 4 * px + 2 * py + pc


def _allgather_small(v, name, after=None):
    rows, cols = v.shape

    def body(v_ref, out_ref, send_sems, recv_sems):
        mx, my, mc, me = _my_position()
        out_ref[me] = v_ref[...]
        copies = []
        for k in range(1, N_DEV):
            peer, _ = _peer(mx, my, mc, k)
            cp = pltpu.make_async_remote_copy(
                src_ref=v_ref, dst_ref=out_ref.at[me],
                send_sem=send_sems.at[k - 1], recv_sem=recv_sems.at[k - 1],
                device_id=peer, device_id_type=MESH_ID)
            cp.start()
            copies.append(cp)
        for cp in copies:
            cp.wait()

    return _pallas_after(
        body, 1, after, name=name,
        out_shape=jax.ShapeDtypeStruct((N_DEV, rows, cols), v.dtype),
        in_specs=[pl.BlockSpec(memory_space=pltpu.VMEM)],
        out_specs=pl.BlockSpec(memory_space=pltpu.VMEM),
        scratch_shapes=[pltpu.SemaphoreType.DMA((N_DEV - 1,)), pltpu.SemaphoreType.DMA((N_DEV - 1,))],
        compiler_params=_params(),
    )(v)


class _Stream:
    def __init__(self, n, plan):
        self.n, self.plan = n, plan


def _comm_call(name, bufs, start=(), wait=(), after=None):
    names = list(bufs)

    def body(*refs):
        it = iter(refs)
        buf_refs = {n: next(it) for n in names}
        wait_sems = [(next(it), next(it)) for _ in wait]
        if after is not None:
            next(it)
        start_sems = [(next(it), next(it)) for _ in start]
        for _ in names:
            next(it)
        token = next(it)
        pos = _my_position()

        def descriptors(stream, sems):
            return [pltpu.make_async_remote_copy(src_ref=src, dst_ref=dst, send_sem=sems[0].at[k], recv_sem=sems[1].at[k],
                                                 device_id=dev, device_id_type=MESH_ID)
                    for k, (src, dst, dev) in enumerate(stream.plan(buf_refs, pos))]

        for (stream, _), sems in zip(wait, wait_sems):
            for cp in descriptors(stream, sems):
                cp.wait_send()
                cp.wait_recv()
        for stream, sems in zip(start, start_sems):
            for cp in descriptors(stream, sems):
                cp.start()
        token[...] = jnp.zeros_like(token)

    hbm = pl.BlockSpec(memory_space=pltpu.HBM)
    sem = pl.BlockSpec(memory_space=pltpu.SEMAPHORE)
    operands = [pltpu.with_memory_space_constraint(bufs[n], pltpu.HBM) for n in names]
    in_specs = [hbm] * len(names)
    for _, (send_sems, recv_sems) in wait:
        operands += [send_sems, recv_sems]
        in_specs += [sem, sem]
    if after is not None:
        operands.append(after)
        in_specs.append(pl.BlockSpec(memory_space=pl.ANY))
    out_shape, out_specs = [], []
    for stream in start:
        out_shape += [pltpu.SemaphoreType.DMA((stream.n,)), pltpu.SemaphoreType.DMA((stream.n,))]
        out_specs += [sem, sem]
    n_sem_out = len(out_shape)
    out_shape += [pltpu.HBM(bufs[n].shape, bufs[n].dtype) for n in names]
    out_specs += [hbm] * len(names)
    out_shape.append(jax.ShapeDtypeStruct((8, 128), F32))
    out_specs.append(pl.BlockSpec(memory_space=pltpu.VMEM))
    outs = pl.pallas_call(
        body, name=name, out_shape=out_shape, in_specs=in_specs, out_specs=out_specs,
        input_output_aliases={i: n_sem_out + i for i in range(len(names))},
        compiler_params=pltpu.CompilerParams(has_side_effects=pltpu.SideEffectType.DATAFLOW_SIDE_EFFECTING),
    )(*operands)
    sems = [(outs[2 * i], outs[2 * i + 1]) for i in range(len(start))]
    return dict(zip(names, outs[n_sem_out:n_sem_out + len(names)])), sems, outs[-1]


def _with_own_slot(block, me):
    return lax.dynamic_update_index_in_dim(lax.empty((N_DEV,) + block.shape, block.dtype), block, me, 0)


def _other_chips(pos):
    mx, my, _, _ = pos
    return [(1 - mx if i & 2 else mx, 1 - my if i & 1 else my) for i in (1, 2, 3)]


def _dev_index(px, py, pc):
    return 4 * px + 2 * py + pc


def _gather_streams(keys):
    def to_chips(refs, pos):
        _, _, mc, me = pos
        return [(refs["s_" + k], refs["g_" + k].at[me], (cx, cy, mc)) for k in keys for cx, cy in _other_chips(pos)]

    def to_sibling(refs, pos):
        mx, my, mc, me = pos
        return [(refs["s_" + k], refs["g_" + k].at[me], (mx, my, 1 - mc)) for k in keys]

    def pass_on(refs, pos):
        mx, my, mc, _ = pos
        out = []
        for k in keys:
            for cx, cy in _other_chips(pos):
                slot = refs["g_" + k].at[_dev_index(cx, cy, mc)]
                out.append((slot, slot, (mx, my, 1 - mc)))
        return out

    return _Stream(3 * len(keys), to_chips), _Stream(len(keys), to_sibling), _Stream(3 * len(keys), pass_on)


def _direct_gather_stream(key):
    def plan(refs, pos):
        mx, my, mc, me = pos
        return [(refs["s_" + key], refs["g_" + key].at[me], _peer(mx, my, mc, k)[0]) for k in range(1, N_DEV)]

    return _Stream(N_DEV - 1, plan)


def _scatter_streams(keys):
    def pair(refs, pos):
        mx, my, mc, _ = pos
        sib = (mx, my, 1 - mc)
        out = []
        for k in keys:
            for i, (cx, cy) in enumerate(_other_chips(pos)):
                out.append((refs["g_" + k].at[_dev_index(cx, cy, 1 - mc)], refs["st_" + k].at[i], sib))
            out.append((refs["g_" + k].at[_dev_index(mx, my, 1 - mc)], refs["st_" + k].at[3], sib))
        return out

    def chips(refs, pos):
        mc = pos[2]
        return [(refs["ps_" + k].at[i], refs["ld_" + k].at[i], (cx, cy, mc))
                for k in keys for i, (cx, cy) in enumerate(_other_chips(pos))]

    return _Stream(4 * len(keys), pair), _Stream(3 * len(keys), chips)


def _pair_sum(g, st, idx, tr, name):
    _, rows, cols = g.shape

    def body(idx_ref, g_ref, st_ref, out_ref):
        out_ref[...] = (g_ref[...].astype(F32) + st_ref[...].astype(F32)).astype(out_ref.dtype)

    return pl.pallas_call(
        body, name=name,
        grid_spec=pltpu.PrefetchScalarGridSpec(
            num_scalar_prefetch=1, grid=(4, rows // tr),
            in_specs=[pl.BlockSpec((None, tr, cols), lambda j, i, idx_ref: (idx_ref[j], i, 0)),
                      pl.BlockSpec((None, tr, cols), lambda j, i, idx_ref: (j, i, 0))],
            out_specs=pl.BlockSpec((None, tr, cols), lambda j, i, idx_ref: (j, i, 0))),
        out_shape=jax.ShapeDtypeStruct((4, rows, cols), WIRE_DTYPE),
        compiler_params=_params(dimension_semantics=("parallel", "parallel")),
    )(idx, g, st)


def _ada_fwd(c_all, w_ada, b_cols):
    def body(c_ref, w_ref, b_ref, out_ref):
        cv = c_ref[...]
        ca = cv * _sigmoid(cv)
        for l in range(DEPTH):
            out_ref[l] = jnp.dot(ca, w_ref[l], precision=HIGHEST, preferred_element_type=F32) + b_ref[l:l + 1, :]

    return pl.pallas_call(
        body, name="ada_fwd",
        out_shape=jax.ShapeDtypeStruct((DEPTH, N_DEV, ADA_COLS), F32),
        compiler_params=_params(),
    )(c_all, w_ada, b_cols)


def _ada_bwd(c_all, d_cols):
    def body(c_ref, d_ref, out_ref):
        cv = c_ref[...]
        ca = cv * _sigmoid(cv)
        for l in range(DEPTH):
            out_ref[l] = lax.dot_general(ca, d_ref[l], (((0,), (0,)), ((), ())), precision=HIGHEST,
                                         preferred_element_type=F32)

    return pl.pallas_call(
        body, name="ada_bwd",
        out_shape=jax.ShapeDtypeStruct((DEPTH, D_MODEL, ADA_COLS), F32),
        compiler_params=_params(),
    )(c_all, d_cols)


def _lower_bounds(logits):
    m = jnp.maximum(logits[0:1], logits[1:2])
    e0, e1 = jnp.exp(logits[0:1] - m), jnp.exp(logits[1:2] - m)
    den = e0 + e1
    p0, p1 = e0 / den, e1 / den
    low0 = p0 - p0
    low1 = (p0 + p1) - p0
    return (p0, p1), (low0, low1)


def _lb_fwd(lb_logits):
    def body(lg_ref, out_ref):
        _, (low0, low1) = _lower_bounds(lg_ref[...])
        out_ref[0:1, :] = jnp.clip(low0, 0.0, 1.0)
        out_ref[1:2, :] = jnp.clip(low1, 0.0, 1.0)

    return pl.pallas_call(body, name="lb_fwd", out_shape=jax.ShapeDtypeStruct(lb_logits.shape, F32),
                          compiler_params=_params())(lb_logits)


def _row_spec(cols=D_MODEL):
    return pl.BlockSpec((1, cols), lambda *_: (0, 0))


def _prenorm_fwd(x, g, shift, scale, tm, name, after=None):
    seq = x.shape[0]

    def body(x_ref, g_ref, sh_ref, sc_ref, h_ref):
        xv = x_ref[...]
        rs = lax.rsqrt(jnp.mean(xv * xv, axis=-1, keepdims=True) + NORM_EPS)
        h = (xv * rs * g_ref[...]) * (1.0 + sc_ref[...]) + sh_ref[...]
        h_ref[...] = h.astype(h_ref.dtype)

    tile = pl.BlockSpec((tm, D_MODEL), lambda i: (i, 0))
    return _pallas_after(
        body, 4, after, name=name, grid=(seq // tm,),
        in_specs=[tile, _row_spec(), _row_spec(), _row_spec()], out_specs=tile,
        out_shape=jax.ShapeDtypeStruct((seq, D_MODEL), MXU_DTYPE),
        compiler_params=_params(dimension_semantics=("parallel",)),
    )(x, g, shift, scale)


def _in_proj(h, win_g, tm, name, after=None):
    seq = h.shape[0]

    def body(h_ref, w_ref, z_ref):
        z_ref[...] = jnp.dot(h_ref[...], w_ref[...], preferred_element_type=F32)

    return _pallas_after(
        body, 2, after, name=name, grid=(N_DEV, seq // tm),
        in_specs=[pl.BlockSpec((tm, D_MODEL), lambda j, i: (i, 0)),
                  pl.BlockSpec((None, D_MODEL, IN_COLS), lambda j, i: (j, 0, 0))],
        out_specs=pl.BlockSpec((tm, IN_COLS), lambda j, i: (i, j)),
        out_shape=jax.ShapeDtypeStruct((seq, IN_WIDTH), F32),
        compiler_params=_params(dimension_semantics=("parallel", "parallel")),
    )(h, win_g)


def _shift_down(v, j, pos):
    return jnp.where(pos >= j, pltpu.roll(v, j, 0), 0.0)


def _shift_up(v, j, pos, seq):
    return jnp.where(pos < seq - j, pltpu.roll(v, seq - j, 0), 0.0)


def _select_window(g, candidates):
    out = candidates[-1]
    for i in range(len(candidates) - 2, -1, -1):
        out = jnp.where(g == i, candidates[i], out)
    return out


def _pool_mean_minus_token(u, g, pos):
    sums, acc = [], u
    for j in (1, 2, 4, 8):
        acc = acc + _shift_down(acc, j, pos)
        sums.append(acc)
    wsum = _select_window(g, sums)
    width = jnp.left_shift(2, g).astype(F32)
    count = jnp.minimum(pos.astype(F32) + 1.0, width)
    return wsum / count - u, count


def _pool_fwd(z, pool_w_l, pool_scale_l, name, after=None):
    seq = z.shape[0]

    def body(pv_ref, pg_ref, w_ref, sc_ref, out_ref):
        g = pl.program_id(0)
        pos = lax.broadcasted_iota(jnp.int32, (seq, GROUP_DIM), 0)
        pm, _ = _pool_mean_minus_token(pv_ref[...], g, pos)
        lin = _dot(pm, w_ref[...]) * sc_ref[...]
        pg = pg_ref[...]
        out_ref[...] = (lin * (pg * _sigmoid(pg))).astype(out_ref.dtype)

    return _pallas_after(
        body, 4, after, name=name, grid=(POOL_GROUPS,),
        in_specs=[pl.BlockSpec((seq, GROUP_DIM), lambda g: (0, g)),
                  pl.BlockSpec((seq, GROUP_DIM), lambda g: (0, POOL_GROUPS + g)),
                  pl.BlockSpec((None, GROUP_DIM, GROUP_DIM), lambda g: (g, 0, 0)),
                  pl.BlockSpec((1, GROUP_DIM), lambda g: (0, g))],
        out_specs=pl.BlockSpec((seq, GROUP_DIM), lambda g: (0, g)),
        out_shape=jax.ShapeDtypeStruct((seq, POOL_WIDTH), MXU_DTYPE),
        compiler_params=_params(dimension_semantics=("parallel",)),
    )(z, z, pool_w_l, pool_scale_l)


def _chunk_masks():
    row = lax.broadcasted_iota(jnp.int32, (CHUNK, CHUNK), 0)
    col = lax.broadcasted_iota(jnp.int32, (CHUNK, CHUNK), 1)
    causal = row >= col
    before_sub = col < (row // SUB) * SUB
    suffix = row <= col
    return causal, before_sub, suffix


def _masked_sums(masks, v):
    lhs = jnp.concatenate([m.astype(jnp.bfloat16) for m in masks], axis=0)
    hi = v.astype(jnp.bfloat16)
    rest = v - hi.astype(F32)
    mid = rest.astype(jnp.bfloat16)
    lo = (rest - mid.astype(F32)).astype(jnp.bfloat16)
    out = jnp.dot(lhs, hi, preferred_element_type=F32)
    out += jnp.dot(lhs, mid, preferred_element_type=F32)
    out += jnp.dot(lhs, lo, preferred_element_type=F32)
    return [out[i * CHUNK:(i + 1) * CHUNK] for i in range(len(masks))]


def _gates(zf, lb):
    sg = _sigmoid(zf)
    f = lb + (1.0 - lb) * sg
    logf = jnp.log(jnp.maximum(f, LOG_FLOOR))
    return sg, f, logf


def _intra_blocks(q_h, k_h, cum_h, base_h, causal):
    rel = cum_h - base_h
    out = []
    for i in range(N_SUB):
        rows = slice(i * SUB, (i + 1) * SUB)
        e_q = jnp.exp(rel[rows])
        base_i = jnp.concatenate([base_h[rows]] * N_SUB, axis=0)
        e_k = jnp.exp(jnp.minimum(base_i - cum_h, EXP_CLAMP))
        q_t = (q_h[rows] * e_q).astype(MXU_DTYPE)
        k_t = (k_h * e_k).astype(MXU_DTYPE)
        a_i = jnp.where(causal[rows], _dot_nt(q_t, k_t), 0.0)
        out.append((q_t, k_t, e_q, e_k, a_i))
    return out


def _hgrn_fwd(z, lb_l, gn_l, name, after=None):
    seq = z.shape[0]
    n_chunks = seq // CHUNK

    def body(hq_ref, hf_ref, hi_ref, hg_ref, lb_ref, gn_ref, o_ref, bin_ref, st_ref, state):
        @pl.when(pl.program_id(0) == 0)
        def _():
            state[...] = jnp.zeros_like(state)

        causal, before_sub, _ = _chunk_masks()
        _, f, logf = _gates(hf_ref[...], lb_ref[...])
        kk = 1.0 - f
        hq = hq_ref[...]
        q = hq * _sigmoid(hq)
        cum, base = _masked_sums([causal, before_sub], logf)
        st_ref[0] = state[...]
        for h in range(HEADS):
            sl = slice(h * HEAD_DIM, (h + 1) * HEAD_DIM)
            q_h, k_h, cum_h = q[:, sl], kk[:, sl], cum[:, sl]
            v_h = hi_ref[:, sl]
            st_h = state[h]
            blocks = _intra_blocks(q_h, k_h, cum_h, base[:, sl], causal)
            a = jnp.concatenate([b[4] for b in blocks], axis=0)
            o_h = _dot_nt(q_h * jnp.exp(cum_h), st_h) + _dot(a, v_h)
            last = jnp.sum(logf[:, sl], axis=0, keepdims=True)
            state[h] = st_h * jnp.exp(last) + _dot_tn(v_h, k_h * jnp.exp(last - cum_h))
            rs = lax.rsqrt(jnp.mean(o_h * o_h, axis=-1, keepdims=True) + NORM_EPS)
            hg = hg_ref[:, sl]
            o_ref[:, sl] = o_h
            bin_ref[:, sl] = ((o_h * rs * gn_ref[...]) * (hg * _sigmoid(hg))).astype(bin_ref.dtype)

    def col(block):
        return pl.BlockSpec((CHUNK, D_MODEL), lambda c: (c, block))

    tile = pl.BlockSpec((CHUNK, D_MODEL), lambda c: (c, 0))
    return _pallas_after(
        body, 6, after, name=name, grid=(n_chunks,),
        in_specs=[col(COL_HQ), col(COL_HF), col(COL_HI), col(COL_HG), _row_spec(), _row_spec(HEAD_DIM)],
        out_specs=[tile, tile, pl.BlockSpec((1, HEADS, HEAD_DIM, HEAD_DIM), lambda c: (c, 0, 0, 0))],
        out_shape=[jax.ShapeDtypeStruct((seq, D_MODEL), F32),
                   jax.ShapeDtypeStruct((seq, D_MODEL), MXU_DTYPE),
                   jax.ShapeDtypeStruct((n_chunks, HEADS, HEAD_DIM, HEAD_DIM), F32)],
        scratch_shapes=[pltpu.VMEM((HEADS, HEAD_DIM, HEAD_DIM), F32)],
        compiler_params=_params(dimension_semantics=("arbitrary",)),
    )(z, z, z, z, lb_l, gn_l)


def _rms_parts(y):
    rs = lax.rsqrt(jnp.mean(y * y, axis=-1, keepdims=True) + NORM_EPS)
    return rs, y * rs


def _merge_fwd(a_in, b_in, z, x, wpo_g, who_g, wout_g, gate, g_post, tm, name):
    seq = x.shape[0]

    def body(a_ref, b_ref, mgp_ref, mgh_ref, x_ref, wpo_ref, who_ref, wout_ref, gate_ref, gp_ref,
             ba_ref, bb_ref, mer_ref, y_ref, xn_ref):
        a = a_ref[...]
        ba = jnp.concatenate([_dot(a, wpo_ref[j]) for j in range(N_DEV)], axis=1)
        bb = _dot(b_ref[...], who_ref[...])
        merged = _sigmoid(mgp_ref[...]) * ba + _sigmoid(mgh_ref[...]) * bb
        y = _dot(merged, wout_ref[...])
        _, yn = _rms_parts(y)
        ba_ref[...] = ba.astype(ba_ref.dtype)
        bb_ref[...] = bb.astype(bb_ref.dtype)
        mer_ref[...] = merged.astype(mer_ref.dtype)
        y_ref[...] = y
        xn_ref[...] = x_ref[...] + gate_ref[...] * (yn * gp_ref[...])

    def tile(cols=D_MODEL, block=0):
        return pl.BlockSpec((tm, cols), lambda i: (i, block))

    full = pl.BlockSpec((D_MODEL, D_MODEL), lambda i: (0, 0))
    act = jax.ShapeDtypeStruct((seq, D_MODEL), MXU_DTYPE)
    f32 = jax.ShapeDtypeStruct((seq, D_MODEL), F32)
    return pl.pallas_call(
        body, name=name, grid=(seq // tm,),
        in_specs=[tile(POOL_WIDTH), tile(), tile(block=COL_MGP), tile(block=COL_MGH), tile(),
                  pl.BlockSpec((N_DEV, POOL_WIDTH, GROUP_DIM), lambda i: (0, 0, 0)),
                  full, full, _row_spec(), _row_spec()],
        out_specs=[tile(), tile(), tile(), tile(), tile()],
        out_shape=[act, act, act, f32, f32],
        compiler_params=_params(dimension_semantics=("parallel",)),
    )(a_in, b_in, z, z, x, wpo_g, who_g, wout_g, gate, g_post)


def _loss_grad(x_out, target, tm):
    seq = x_out.shape[0]

    def body(x_ref, t_ref, loss_ref, dx_ref):
        @pl.when(pl.program_id(0) == 0)
        def _():
            loss_ref[...] = jnp.zeros_like(loss_ref)

        err = x_ref[...] - t_ref[...]
        per_token = jnp.mean(err * err, axis=-1, keepdims=True)
        loss_ref[...] += 0.5 * jnp.sum(per_token, axis=0, keepdims=True)
        dx_ref[...] = err * (1.0 / D_MODEL)

    tile = pl.BlockSpec((tm, D_MODEL), lambda i: (i, 0))
    return pl.pallas_call(
        body, name="loss_grad", grid=(seq // tm,),
        in_specs=[tile, tile],
        out_specs=[pl.BlockSpec((1, 1), lambda i: (0, 0)), tile],
        out_shape=[jax.ShapeDtypeStruct((1, 1), F32), jax.ShapeDtypeStruct((seq, D_MODEL), F32)],
        compiler_params=_params(dimension_semantics=("arbitrary",)),
    )(x_out, target)


def _stage_copy(stage, sems, dst, slot, step, where):
    rows, cols = where(step)
    return pltpu.make_async_copy(stage.at[slot], dst.at[rows, cols], sems.at[slot])


def _stage_begin(stage, sems, dst, step, where):
    slot = step % 2

    @pl.when(step >= 2)
    def _():
        _stage_copy(stage, sems, dst, slot, step - 2, where).wait()

    return slot


def _stage_end(stage, sems, dst, step, n_steps, where):
    slot = step % 2
    _stage_copy(stage, sems, dst, slot, step, where).start()

    @pl.when(step == n_steps - 1)
    def _():
        _stage_copy(stage, sems, dst, slot, step, where).wait()
        if n_steps > 1:
            _stage_copy(stage, sems, dst, 1 - slot, step - 1, where).wait()


def _merge_bwd(dx, y, ba, bb, z, wpo_g, who_g, wout_g, gate, g_post, dz, tm, name):
    seq = dx.shape[0]
    n_steps = seq // tm

    def body(dx_ref, y_ref, ba_ref, bb_ref, mgp_ref, mgh_ref, wpo_ref, who_ref, wout_ref, gate_ref, gp_ref, _,
             dy_ref, dba_ref, dbb_ref, da_ref, db_ref, dz_ref, acc_ref, stage, sems):
        step = pl.program_id(0)

        @pl.when(step == 0)
        def _():
            acc_ref[...] = jnp.zeros_like(acc_ref)

        def where(t):
            return pl.ds(t * tm, tm), pl.ds(COL_MGP * D_MODEL, 2 * D_MODEL)

        dmg_ref = stage.at[_stage_begin(stage, sems, dz_ref, step, where)]

        dxv = dx_ref[...]
        rs, yn = _rms_parts(y_ref[...])
        acc_ref[0:1, :] += jnp.sum(dxv * yn * gp_ref[...], axis=0, keepdims=True)
        acc_ref[1:2, :] += jnp.sum(dxv * gate_ref[...] * yn, axis=0, keepdims=True)
        dyn = dxv * (gate_ref[...] * gp_ref[...])
        dy = rs * (dyn - yn * jnp.mean(dyn * yn, axis=-1, keepdims=True))
        dmerged = _dot_nt(dy, wout_ref[...])
        sp, sh = _sigmoid(mgp_ref[...]), _sigmoid(mgh_ref[...])
        dba, dbb = sp * dmerged, sh * dmerged
        dmg_ref[:, 0:D_MODEL] = (dmerged * ba_ref[...].astype(F32) * sp * (1.0 - sp)).astype(dmg_ref.dtype)
        dmg_ref[:, D_MODEL:2 * D_MODEL] = (dmerged * bb_ref[...].astype(F32) * sh * (1.0 - sh)).astype(dmg_ref.dtype)
        da = _dot_nt(dba[:, 0:GROUP_DIM], wpo_ref[0])
        for j in range(1, N_DEV):
            da += _dot_nt(dba[:, j * GROUP_DIM:(j + 1) * GROUP_DIM], wpo_ref[j])
        dy_ref[...] = dy.astype(dy_ref.dtype)
        dba_ref[...] = dba.astype(dba_ref.dtype)
        dbb_ref[...] = dbb.astype(dbb_ref.dtype)
        da_ref[...] = da
        db_ref[...] = _dot_nt(dbb, who_ref[...])
        _stage_end(stage, sems, dz_ref, step, n_steps, where)

    def tile(cols=D_MODEL, block=0):
        return pl.BlockSpec((tm, cols), lambda i: (i, block))

    full = pl.BlockSpec((D_MODEL, D_MODEL), lambda i: (0, 0))
    hbm = pl.BlockSpec(memory_space=pl.ANY)
    act = jax.ShapeDtypeStruct((seq, D_MODEL), MXU_DTYPE)
    return pl.pallas_call(
        body, name=name, grid=(n_steps,),
        in_specs=[tile(), tile(), tile(), tile(), tile(block=COL_MGP), tile(block=COL_MGH),
                  pl.BlockSpec((N_DEV, POOL_WIDTH, GROUP_DIM), lambda i: (0, 0, 0)),
                  full, full, _row_spec(), _row_spec(), hbm],
        out_specs=[tile(), tile(), tile(), tile(POOL_WIDTH), tile(), hbm,
                   pl.BlockSpec((8, D_MODEL), lambda i: (0, 0))],
        out_shape=[act, act, act, jax.ShapeDtypeStruct((seq, POOL_WIDTH), F32),
                   jax.ShapeDtypeStruct((seq, D_MODEL), F32),
                   jax.ShapeDtypeStruct(dz.shape, dz.dtype),
                   jax.ShapeDtypeStruct((8, D_MODEL), F32)],
        input_output_aliases={11: 5},
        scratch_shapes=[pltpu.VMEM((2, tm, 2 * D_MODEL), MXU_DTYPE), pltpu.SemaphoreType.DMA((2,))],
        compiler_params=_params(dimension_semantics=("arbitrary",)),
    )(dx, y, ba, bb, z, z, wpo_g, who_g, wout_g, gate, g_post, dz)


def _grad_tn(a, b, tn, dev_major, name):
    seq, ka = a.shape
    n = b.shape[1]

    def body(a_ref, b_ref, out_ref):
        out_ref[...] = _dot_tn(a_ref[...], b_ref[...]).astype(out_ref.dtype)

    if dev_major:
        out_spec = pl.BlockSpec((None, ka, tn), lambda j: (j, 0, 0))
        out_shape = jax.ShapeDtypeStruct((n // tn, ka, tn), WIRE_DTYPE)
    else:
        out_spec = pl.BlockSpec((ka, tn), lambda j: (0, j))
        out_shape = jax.ShapeDtypeStruct((ka, n), WIRE_DTYPE)
    return pl.pallas_call(
        body, name=name, grid=(n // tn,),
        in_specs=[pl.BlockSpec((seq, ka), lambda j: (0, 0)), pl.BlockSpec((seq, tn), lambda j: (0, j))],
        out_specs=out_spec, out_shape=out_shape,
        compiler_params=_params(dimension_semantics=("parallel",)),
    )(a, b)


def _hgrn_bwd(db_in, z, o, states, lb_l, gn_l, dz, name, after=None):
    seq = z.shape[0]
    n_chunks = seq // CHUNK

    last_chunk = n_chunks - 1

    def body(db_ref, hq_ref, hf_ref, hi_ref, hg_ref, o_ref, st_ref, lb_ref, gn_ref, _,
             dz_hbm, dlb_ref, dgn_ref, dstate, dq_buf, dk_buf, dg_buf, stage, sems):
        step = pl.program_id(0)

        @pl.when(step == 0)
        def _():
            dstate[...] = jnp.zeros_like(dstate)
            dlb_ref[...] = jnp.zeros_like(dlb_ref)
            dgn_ref[...] = jnp.zeros_like(dgn_ref)

        def where(t):
            return pl.ds((last_chunk - t) * CHUNK, CHUNK), pl.ds(COL_HQ * D_MODEL, 4 * D_MODEL)

        dz_ref = stage.at[_stage_begin(stage, sems, dz_hbm, step, where)]

        causal, before_sub, suffix = _chunk_masks()
        lb = lb_ref[...]
        sg, f, logf = _gates(hf_ref[...], lb)
        kk = 1.0 - f
        hq = hq_ref[...]
        sq = _sigmoid(hq)
        q = hq * sq
        cum, base = _masked_sums([causal, before_sub], logf)
        gn = gn_ref[...]
        dgn = jnp.zeros((1, HEAD_DIM), F32)
        dlast = []
        for h in range(HEADS):
            sl = slice(h * HEAD_DIM, (h + 1) * HEAD_DIM)
            q_h, k_h, cum_h = q[:, sl], kk[:, sl], cum[:, sl]
            v_h = hi_ref[:, sl]
            st_h = st_ref[0, h]
            dst_h = dstate[h]
            rs, ohat = _rms_parts(o_ref[:, sl])
            hg = hg_ref[:, sl]
            shg = _sigmoid(hg)
            d_bin = db_ref[:, sl]
            don = d_bin * (hg * shg)
            dgn += jnp.sum(don * ohat, axis=0, keepdims=True)
            dohat = don * gn
            do = rs * (dohat - ohat * jnp.mean(dohat * ohat, axis=-1, keepdims=True))
            dz_ref[:, 3 * D_MODEL + h * HEAD_DIM:3 * D_MODEL + (h + 1) * HEAD_DIM] = (
                d_bin * (ohat * gn) * _dsilu(hg, shg)).astype(dz_ref.dtype)
            last = jnp.sum(logf[:, sl], axis=0, keepdims=True)
            g_in = jnp.exp(cum_h)
            d_out = jnp.exp(last - cum_h)
            q_bar, k_bar = q_h * g_in, k_h * d_out
            blocks = _intra_blocks(q_h, k_h, cum_h, base[:, sl], causal)
            a = jnp.concatenate([b[4] for b in blocks], axis=0)
            da = jnp.where(causal, _dot_nt(do, v_h), 0.0)
            dv = _dot_tn(a, do) + _dot_nt(k_bar, dst_h)
            dq_bar, dk_bar = _dot(do, st_h), _dot(v_h, dst_h)
            dk = dk_bar * d_out
            dq_parts, dg_parts = [], []
            dg_k = k_bar * dk_bar
            dlast.append(jnp.sum(k_bar * dk_bar, axis=0, keepdims=True)
                         + jnp.exp(last) * jnp.sum(st_h * dst_h, axis=0, keepdims=True))
            for i, (q_t, k_t, e_q, e_k, _) in enumerate(blocks):
                da_i = da[i * SUB:(i + 1) * SUB].astype(MXU_DTYPE)
                dq_t = _dot(da_i, k_t)
                dk_t = _dot_tn(da_i, q_t)
                dq_parts.append(dq_t * e_q)
                dk += dk_t * e_k
                dg_parts.append(q_t.astype(F32) * dq_t)
                dg_k += k_t.astype(F32) * dk_t
            dq = dq_bar * g_in + jnp.concatenate(dq_parts, axis=0)
            dg_buf[:, sl] = q_bar * dq_bar + jnp.concatenate(dg_parts, axis=0) - dg_k
            dstate[h] = dst_h * jnp.exp(last) + _dot_tn(do, q_bar)
            dq_buf[:, sl] = dq
            dk_buf[:, sl] = dk
            dz_ref[:, 2 * D_MODEL + h * HEAD_DIM:2 * D_MODEL + (h + 1) * HEAD_DIM] = dv.astype(dz_ref.dtype)
        dgn_ref[...] += dgn
        dq_all, dk_all = dq_buf[...], dk_buf[...]
        dlogf = _masked_sums([suffix], dg_buf[...])[0] + jnp.concatenate(dlast, axis=1)
        df = jnp.where(f > LOG_FLOOR, dlogf / f, 0.0) - dk_all
        dlb_ref[...] += jnp.sum(df * (1.0 - sg), axis=0, keepdims=True)
        dz_ref[:, 0:D_MODEL] = (dq_all * _dsilu(hq, sq)).astype(dz_ref.dtype)
        dz_ref[:, D_MODEL:2 * D_MODEL] = (df * (1.0 - lb) * sg * (1.0 - sg)).astype(dz_ref.dtype)
        _stage_end(stage, sems, dz_hbm, step, n_chunks, where)

    def col(block):
        return pl.BlockSpec((CHUNK, D_MODEL), lambda c: (last_chunk - c, block))

    hbm = pl.BlockSpec(memory_space=pl.ANY)
    return _pallas_after(
        body, 10, after, name=name, grid=(n_chunks,),
        in_specs=[col(0), col(COL_HQ), col(COL_HF), col(COL_HI), col(COL_HG), col(0),
                  pl.BlockSpec((1, HEADS, HEAD_DIM, HEAD_DIM), lambda c: (last_chunk - c, 0, 0, 0)),
                  _row_spec(), _row_spec(HEAD_DIM), hbm],
        out_specs=[hbm, _row_spec(), _row_spec(HEAD_DIM)],
        out_shape=[jax.ShapeDtypeStruct(dz.shape, dz.dtype),
                   jax.ShapeDtypeStruct((1, D_MODEL), F32), jax.ShapeDtypeStruct((1, HEAD_DIM), F32)],
        input_output_aliases={9: 0},
        scratch_shapes=[pltpu.VMEM((HEADS, HEAD_DIM, HEAD_DIM), F32)] + [pltpu.VMEM((CHUNK, D_MODEL), F32)] * 3
        + [pltpu.VMEM((2, CHUNK, 4 * D_MODEL), MXU_DTYPE), pltpu.SemaphoreType.DMA((2,))],
        compiler_params=_params(dimension_semantics=("arbitrary",)),
    )(db_in, z, z, z, z, o, states, lb_l, gn_l, dz)


def _pool_bwd(da_in, z, pool_w_l, pool_scale_l, dz, name, after=None):
    seq = z.shape[0]

    def body(da_ref, pv_ref, pg_ref, w_ref, sc_ref, _, dz_hbm, dw_ref, dsc_ref, stage_pv, stage_pg, sems_pv, sems_pg):
        g = pl.program_id(0)

        def where_pv(t):
            return pl.ds(0, seq), pl.ds(pl.multiple_of(t * GROUP_DIM, GROUP_DIM), GROUP_DIM)

        def where_pg(t):
            return pl.ds(0, seq), pl.ds(pl.multiple_of(POOL_WIDTH + t * GROUP_DIM, GROUP_DIM), GROUP_DIM)

        dpv_ref = stage_pv.at[_stage_begin(stage_pv, sems_pv, dz_hbm, g, where_pv)]
        dpg_ref = stage_pg.at[_stage_begin(stage_pg, sems_pg, dz_hbm, g, where_pg)]
        pos = lax.broadcasted_iota(jnp.int32, (seq, GROUP_DIM), 0)
        pm, count = _pool_mean_minus_token(pv_ref[...], g, pos)
        lin0 = _dot(pm, w_ref[...])
        pg = pg_ref[...]
        spg = _sigmoid(pg)
        da = da_ref[...]
        dlin = da * (pg * spg)
        dpg_ref[...] = (da * (lin0 * sc_ref[...]) * _dsilu(pg, spg)).astype(dpg_ref.dtype)
        dsc_ref[...] = jnp.sum(dlin * lin0, axis=0, keepdims=True)
        dl0 = dlin * sc_ref[...]
        dw_ref[...] = _dot_tn(pm, dl0)
        dpm = _dot_nt(dl0, w_ref[...])
        sums, acc = [], dpm / count
        for j in (1, 2, 4, 8):
            acc = acc + _shift_up(acc, j, pos, seq)
            sums.append(acc)
        dpv_ref[...] = (_select_window(g, sums) - dpm).astype(dpv_ref.dtype)
        _stage_end(stage_pv, sems_pv, dz_hbm, g, POOL_GROUPS, where_pv)
        _stage_end(stage_pg, sems_pg, dz_hbm, g, POOL_GROUPS, where_pg)

    grp = pl.BlockSpec((seq, GROUP_DIM), lambda g: (0, g))
    hbm = pl.BlockSpec(memory_space=pl.ANY)
    stage = pltpu.VMEM((2, seq, GROUP_DIM), MXU_DTYPE)
    return _pallas_after(
        body, 6, after, name=name, grid=(POOL_GROUPS,),
        in_specs=[grp, grp, pl.BlockSpec((seq, GROUP_DIM), lambda g: (0, POOL_GROUPS + g)),
                  pl.BlockSpec((None, GROUP_DIM, GROUP_DIM), lambda g: (g, 0, 0)),
                  pl.BlockSpec((1, GROUP_DIM), lambda g: (0, g)), hbm],
        out_specs=[hbm, pl.BlockSpec((None, GROUP_DIM, GROUP_DIM), lambda g: (g, 0, 0)),
                   pl.BlockSpec((1, GROUP_DIM), lambda g: (0, g))],
        out_shape=[jax.ShapeDtypeStruct(dz.shape, dz.dtype),
                   jax.ShapeDtypeStruct((POOL_GROUPS, GROUP_DIM, GROUP_DIM), F32),
                   jax.ShapeDtypeStruct((1, POOL_WIDTH), F32)],
        input_output_aliases={5: 0},
        scratch_shapes=[stage, stage, pltpu.SemaphoreType.DMA((2,)), pltpu.SemaphoreType.DMA((2,))],
        compiler_params=_params(dimension_semantics=("arbitrary",)),
    )(da_in, z, z, pool_w_l, pool_scale_l, dz)


def _in_proj_dw(h, dz, name, after=None):
    seq = h.shape[0]

    def body(h_ref, dz_ref, out_ref):
        out_ref[...] = lax.dot_general(h_ref[...], dz_ref[...], (((0,), (0,)), ((), ())),
                                       preferred_element_type=F32).astype(out_ref.dtype)

    return _pallas_after(
        body, 2, after, name=name, grid=(N_DEV,),
        in_specs=[pl.BlockSpec((seq, D_MODEL), lambda j: (0, 0)), pl.BlockSpec((seq, IN_COLS), lambda j: (0, j))],
        out_specs=pl.BlockSpec((None, D_MODEL, IN_COLS), lambda j: (j, 0, 0)),
        out_shape=jax.ShapeDtypeStruct((N_DEV, D_MODEL, IN_COLS), WIRE_DTYPE),
        compiler_params=_params(dimension_semantics=("parallel",)),
    )(h, dz)


def _in_proj_dh(dz, win_g, tm, name, after=None):
    seq = dz.shape[0]

    def body(dz_ref, w_ref, dh_ref):
        @pl.when(pl.program_id(1) == 0)
        def _():
            dh_ref[...] = jnp.zeros_like(dh_ref)

        dh_ref[...] += lax.dot_general(dz_ref[...], w_ref[...], (((1,), (1,)), ((), ())),
                                       preferred_element_type=F32)

    return _pallas_after(
        body, 2, after, name=name, grid=(seq // tm, N_DEV),
        in_specs=[pl.BlockSpec((tm, IN_COLS), lambda i, j: (i, j)),
                  pl.BlockSpec((None, D_MODEL, IN_COLS), lambda i, j: (j, 0, 0))],
        out_specs=pl.BlockSpec((tm, D_MODEL), lambda i, j: (i, 0)),
        out_shape=jax.ShapeDtypeStruct((seq, D_MODEL), F32),
        compiler_params=_params(dimension_semantics=("parallel", "arbitrary")),
    )(dz, win_g)


def _prenorm_bwd(x, dh, dx_res, g, scale, tm, name, after=None):
    seq = x.shape[0]

    def body(x_ref, dh_ref, dxr_ref, g_ref, sc_ref, dx_ref, acc_ref):
        @pl.when(pl.program_id(0) == 0)
        def _():
            acc_ref[...] = jnp.zeros_like(acc_ref)

        rs, xn = _rms_parts(x_ref[...])
        dh = dh_ref[...]
        acc_ref[0:1, :] += jnp.sum(dh, axis=0, keepdims=True)
        acc_ref[1:2, :] += jnp.sum(dh * (xn * g_ref[...]), axis=0, keepdims=True)
        dhn = dh * (1.0 + sc_ref[...])
        acc_ref[2:3, :] += jnp.sum(dhn * xn, axis=0, keepdims=True)
        dxn = dhn * g_ref[...]
        dx_ref[...] = rs * (dxn - xn * jnp.mean(dxn * xn, axis=-1, keepdims=True)) + dxr_ref[...]

    tile = pl.BlockSpec((tm, D_MODEL), lambda i: (i, 0))
    return _pallas_after(
        body, 5, after, name=name, grid=(seq // tm,),
        in_specs=[tile, tile, tile, _row_spec(), _row_spec()],
        out_specs=[tile, pl.BlockSpec((8, D_MODEL), lambda i: (0, 0))],
        out_shape=[jax.ShapeDtypeStruct((seq, D_MODEL), F32), jax.ShapeDtypeStruct((8, D_MODEL), F32)],
        compiler_params=_params(dimension_semantics=("arbitrary",)),
    )(x, dh, dx_res, g, scale)


def _adamw_math(w, g, m, v):
    m = ADAM_B1 * m + (1.0 - ADAM_B1) * g
    v = ADAM_B2 * v + (1.0 - ADAM_B2) * (g * g)
    m_hat = m / (1.0 - ADAM_B1 ** ADAM_STEP)
    v_hat = v / (1.0 - ADAM_B2 ** ADAM_STEP)
    delta = -ADAM_LR * (m_hat / (jnp.sqrt(v_hat) + ADAM_EPS) + ADAM_WD * w)
    return delta, m, v


def _adamw_sharded(w, m, v, contrib, tr, name):
    depth, rows, cols = w.shape
    n_parts = contrib.shape[1]

    def body(w_ref, m_ref, v_ref, c_ref, g_ref, d_ref, mo_ref, vo_ref):
        g = c_ref[0].astype(F32)
        for p in range(1, n_parts):
            g += c_ref[p].astype(F32)
        delta, mn, vn = _adamw_math(w_ref[...], g, m_ref[...], v_ref[...])
        g_ref[...] = g
        d_ref[...] = delta
        mo_ref[...] = mn
        vo_ref[...] = vn

    tile = pl.BlockSpec((None, tr, cols), lambda l, i: (l, i, 0))
    shape = jax.ShapeDtypeStruct(w.shape, F32)
    return pl.pallas_call(
        body, name=name, grid=(depth, rows // tr),
        in_specs=[tile, tile, tile, pl.BlockSpec((None, n_parts, tr, cols), lambda l, i: (l, 0, i, 0))],
        out_specs=[tile] * 4, out_shape=[shape] * 4,
        compiler_params=_params(dimension_semantics=("parallel", "parallel")),
    )(w, m, v, contrib)


def _adamw_layer(w, m, v, contribs, l, tr, name, prev=None):
    _, rows, cols = w.shape
    n = len(contribs)

    def body(*refs):
        w_ref, m_ref, v_ref = refs[:3]
        c_refs = refs[3:3 + n]
        g_ref, d_ref, mo_ref, vo_ref = refs[-4:]
        g = c_refs[0][...].astype(F32)
        for c_ref in c_refs[1:]:
            g += c_ref[...].astype(F32)
        delta, mn, vn = _adamw_math(w_ref[...], g, m_ref[...], v_ref[...])
        g_ref[...] = g
        d_ref[...] = delta
        mo_ref[...] = mn
        vo_ref[...] = vn

    tile = pl.BlockSpec((None, tr, cols), lambda i: (l, i, 0))
    in_specs = [tile, tile, tile] + [pl.BlockSpec((None, tr, cols), lambda i, s=slot: (s, i, 0)) for _, slot in contribs]
    operands = [w, m, v] + [arr for arr, _ in contribs]
    aliases = {}
    if prev is not None:
        aliases = {len(operands) + k: k for k in range(4)}
        in_specs += [pl.BlockSpec(memory_space=pl.ANY)] * 4
        operands += list(prev)
    shape = jax.ShapeDtypeStruct(w.shape, F32)
    return pl.pallas_call(
        body, name=name, grid=(rows // tr,), in_specs=in_specs, out_specs=[tile] * 4, out_shape=[shape] * 4,
        input_output_aliases=aliases,
        compiler_params=_params(dimension_semantics=("parallel",)),
    )(*operands)


def _adamw_small(w_pack, m_pack, v_pack, g_late, g_early, shapes):
    pieces, r = {}, 0
    for name, _, n in _SMALL_ROWS:
        pieces.setdefault(name, []).append((r, n))
        r += n
    names = list(pieces)

    def body(w_ref, m_ref, v_ref, gl_ref, ge_ref, *rest):
        outs, packs = rest[:4 * len(names)], rest[4 * len(names):]
        g_l, g_e = gl_ref[0][0:SMALL_LATE_ROWS], ge_ref[0]
        for d in range(1, N_DEV):
            g_l += gl_ref[d][0:SMALL_LATE_ROWS]
            g_e += ge_ref[d]
        g = jnp.concatenate([g_l, g_e], axis=0)
        w = w_ref[...]
        r0, r1, r2 = LB_ROW0, LB_ROW0 + 8, LB_ROW0 + 16
        lg0, lg1 = w[r0:r1], w[r1:r2]
        mx = jnp.maximum(lg0, lg1)
        e0, e1 = jnp.exp(lg0 - mx), jnp.exp(lg1 - mx)
        p0, p1 = e0 / (e0 + e1), e1 / (e0 + e1)
        low = ((p0 - p0), (p0 + p1) - p0)
        dlow = [g_rows * jnp.where((lo > 0.0) & (lo < 1.0), 1.0, jnp.where((lo == 0.0) | (lo == 1.0), 0.5, 0.0))
                for g_rows, lo in ((g[r0:r1], low[0]), (g[r1:r2], low[1]))]
        dp0 = (dlow[0] + dlow[1]) - (dlow[0] + dlow[1])
        dp1 = dlow[1]
        inner = p0 * dp0 + p1 * dp1
        g = jnp.concatenate([g[:r0], p0 * (dp0 - inner), p1 * (dp1 - inner), g[r2:]], axis=0)
        delta, mn, vn = _adamw_math(w, g, m_ref[...], v_ref[...])
        for kind, val in enumerate((g, delta, mn, vn)):
            packs[kind][...] = val
            for j, name in enumerate(names):
                at = 0
                for start, n in pieces[name]:
                    outs[kind * len(names) + j][at:at + n, :] = packs[kind][start:start + n, :]
                    at += n

    rows = {name: sum(n for _, n in pieces[name]) for name in names}
    outs = pl.pallas_call(
        body, name="adamw_small",
        out_shape=[jax.ShapeDtypeStruct((rows[name], 128), F32) for _ in range(4) for name in names],
        scratch_shapes=[pltpu.VMEM(w_pack.shape, F32)] * 4, compiler_params=_params(),
    )(w_pack, m_pack, v_pack, g_late, g_early)
    return [{name: outs[kind * len(names) + j].reshape(shapes[name]) for j, name in enumerate(names)}
            for kind in range(4)]


def _pack_small(parts, first=0, last=len(_SMALL_ROWS)):
    rows = [(parts[name] if l is None else parts[name][l]).reshape(n, 128) for name, l, n in _SMALL_ROWS[first:last]]
    if last == len(_SMALL_ROWS):
        rows.append(jnp.zeros((SMALL_ROWS_PAD - sum(n for _, _, n in _SMALL_ROWS), 128), F32))
    return jnp.concatenate(rows, axis=0)


def kernel(x, c, w_ada, b_ada, g_pre, g_post, w_in, pool_w, pool_scale, lb_logits, hgrn_norm_g, w_pool_o, w_hgrn_o, w_out, loss_target, m_w_ada, m_b_ada, m_g_pre, m_g_post, m_w_in, m_pool_w, m_pool_scale, m_lb_logits, m_hgrn_norm_g, m_w_pool_o, m_w_hgrn_o, m_w_out, v_w_ada, v_b_ada, v_g_pre, v_g_post, v_w_in, v_pool_w, v_pool_scale, v_lb_logits, v_hgrn_norm_g, v_w_pool_o, v_w_hgrn_o, v_w_out):
    seq = x.shape[1]
    tm = min(512, seq)
    tm_merge = min(256, seq)
    pos = _my_position()
    me = pos[3]

    c_all = _allgather_small(c, "allgather_c").reshape(N_DEV, D_MODEL)
    b_cols = lax.dynamic_slice_in_dim(b_ada, me * ADA_COLS, ADA_COLS, axis=1)
    ada_part = _ada_fwd(c_all, w_ada, b_cols)
    ada_all = _allgather_small(ada_part.reshape(DEPTH * N_DEV, ADA_COLS), "allgather_ada")
    ada = lax.dynamic_index_in_dim(ada_all.reshape(N_DEV, DEPTH, N_DEV, ADA_COLS), me, axis=2, keepdims=False)
    ada = jnp.transpose(ada, (1, 0, 2)).reshape(DEPTH, 3 * D_MODEL)
    shift = [ada[l:l + 1, 0:D_MODEL] for l in range(DEPTH)]
    scale = [ada[l:l + 1, D_MODEL:2 * D_MODEL] for l in range(DEPTH)]
    gate = [ada[l:l + 1, 2 * D_MODEL:] for l in range(DEPTH)]

    big = dict(win=w_in, wpo=w_pool_o, who=w_hgrn_o, wout=w_out)
    units = [["win0"], ["wpo0", "who0", "wout0"], ["win1", "wpo1", "who1", "wout1"]]
    g_streams = [_gather_streams(keys) for keys in units]
    g_state = [None] * len(units)

    def gather_start(u, after):
        bufs = {}
        for k in units[u]:
            arr = big[k[:-1]]
            bufs["s_" + k] = arr[int(k[-1])].astype(WIRE_DTYPE)
            bufs["g_" + k] = _with_own_slot(bufs["s_" + k], me)
        bufs, sems, token = _comm_call(f"gather_start_{u}", bufs, start=list(g_streams[u][:2]), after=after)
        g_state[u] = dict(bufs=bufs, sems=sems)
        return token

    def gather_pass(u, after):
        st = g_state[u]
        to_chips, _, pass_on = g_streams[u]
        st["bufs"], (st["pass_sems"],), _ = _comm_call(f"gather_pass_{u}", st["bufs"], start=[pass_on],
                                                       wait=[(to_chips, st["sems"][0])], after=after)

    def gather_done(u, after=None):
        st = g_state[u]
        _, to_sibling, pass_on = g_streams[u]
        bufs, _, _ = _comm_call(f"gather_done_{u}", st["bufs"], after=after,
                                wait=[(to_sibling, st["sems"][1]), (pass_on, st["pass_sems"])])
        return {k: bufs["g_" + k] for k in units[u]}

    token = gather_start(0, ada_all)

    lb = _lb_fwd(lb_logits)

    gw = {}
    xs, saved = [x[0]], []
    for l in range(DEPTH):
        h = _prenorm_fwd(xs[l], g_pre[l:l + 1], shift[l], scale[l], tm, f"prenorm_fwd_{l}",
                         after=token if l == 0 else None)
        token = None
        if l == 0:
            gather_pass(0, h)
            gw.update(gather_done(0))
            token = gather_start(1, gw["win0"])
        else:
            gather_pass(2, h)
            gw.update(gather_done(2))
        z = _in_proj(h, gw[f"win{l}"], seq, f"in_proj_{l}", after=token)
        if l == 0:
            gather_pass(1, z)
            token = gather_start(2, g_state[1]["bufs"]["g_wpo0"])
        a_in = _pool_fwd(z, pool_w[l], pool_scale[l:l + 1], f"pool_fwd_{l}", after=token)
        o, b_in, states = _hgrn_fwd(z, lb[l:l + 1], hgrn_norm_g[l:l + 1], f"hgrn_fwd_{l}", after=token)
        if l == 0:
            gw.update(gather_done(1, b_in))
        who_l = gw[f"who{l}"].reshape(D_MODEL, D_MODEL)
        wout_l = gw[f"wout{l}"].reshape(D_MODEL, D_MODEL)
        ba, bb, merged, y, x_next = _merge_fwd(a_in, b_in, z, xs[l], gw[f"wpo{l}"], who_l, wout_l, gate[l],
                                               g_post[l:l + 1], tm_merge, f"merge_fwd_{l}")
        xs.append(x_next)
        saved.append((h, z, a_in, o, b_in, states, ba, bb, merged, y, who_l, wout_l))

    loss_part, dx = _loss_grad(xs[DEPTH], loss_target[0], tm)

    chips = _other_chips(pos)
    pair_idx = jnp.stack([_dev_index(cx, cy, pos[2]) for cx, cy in chips] + [me]).astype(jnp.int32)
    pair_rows = dict(win=256, wpo=POOL_WIDTH, who=HEAD_DIM, wout=HEAD_DIM)

    def scatter_pair_start(u, grads):
        keys = list(grads)
        pair, to_chips = _scatter_streams(keys)
        bufs = {}
        for k in keys:
            bufs["g_" + k] = grads[k]
            bufs["st_" + k] = lax.empty((4,) + grads[k].shape[1:], WIRE_DTYPE)
        bufs, (sems,), token = _comm_call(f"scatter_pair_start_{u}", bufs, start=[pair])
        return dict(u=u, keys=keys, pair=pair, to_chips=to_chips, bufs=bufs, sems=sems, token=token)

    def scatter_pair_finish(st, after):
        u, keys = st["u"], st["keys"]
        bufs, _, _ = _comm_call(f"scatter_pair_done_{u}", st["bufs"], wait=[(st["pair"], st["sems"])], after=after)
        bufs2 = {}
        for k in keys:
            bufs2["ps_" + k] = _pair_sum(bufs["g_" + k], bufs["st_" + k], pair_idx, bufs["g_" + k].shape[1],
                                         f"pair_sum_{k}")
            bufs2["ld_" + k] = lax.empty((3,) + bufs["g_" + k].shape[1:], WIRE_DTYPE)
        st.update(bufs=bufs2)

    def scatter_chips_start(st, after=None):
        bufs2, (sems,), token = _comm_call(f"scatter_chips_start_{st['u']}", st["bufs"], start=[st["to_chips"]],
                                           after=after)
        st.update(bufs=bufs2, sems=sems, token=token)

    def scatter_finish(st, after):
        bufs, _, _ = _comm_call(f"scatter_chips_done_{st['u']}", st["bufs"], wait=[(st["to_chips"], st["sems"])],
                                after=after)
        return {k: [(bufs["ps_" + k], 3), (bufs["ld_" + k], 0), (bufs["ld_" + k], 1), (bufs["ld_" + k], 2)]
                for k in st["keys"]}

    moments = dict(win=(m_w_in, v_w_in), wpo=(m_w_pool_o, v_w_pool_o), who=(m_w_hgrn_o, v_w_hgrn_o),
                   wout=(m_w_out, v_w_out))
    big_out = {}

    def finish_unit(unit, after):
        for k, contribs in scatter_finish(scat[unit], after).items():
            wname, l = k[:-1], int(k[-1])
            big_out[wname] = _adamw_layer(big[wname], moments[wname][0], moments[wname][1], contribs, l,
                                          pair_rows[wname], f"adamw_{k}", prev=big_out.get(wname))
            after = big_out[wname][0]
        return after

    d_ada, small, scat = [None] * DEPTH, [None] * DEPTH, {}
    for l in reversed(range(DEPTH)):
        h, z, a_in, o, b_in, states, ba, bb, merged, y, who_l, wout_l = saved[l]
        dy, dba, dbb, da_in, db_in, dz, acc_post = _merge_bwd(
            dx, y, ba, bb, z, gw[f"wpo{l}"], who_l, wout_l, gate[l], g_post[l:l + 1],
            lax.empty((seq, IN_WIDTH), MXU_DTYPE), tm_merge, f"merge_bwd_{l}")
        g_small = {
            f"wout{l}": _grad_tn(merged, dy, 512, False, f"grad_w_out_{l}").reshape(N_DEV, HEAD_DIM, D_MODEL),
            f"who{l}": _grad_tn(b_in, dbb, 512, False, f"grad_w_hgrn_o_{l}").reshape(N_DEV, HEAD_DIM, D_MODEL),
            f"wpo{l}": _grad_tn(a_in, dba, GROUP_DIM, True, f"grad_w_pool_o_{l}")}
        st_small = scat["small0"] = scatter_pair_start("small0", g_small) if l == 0 else None
        dz, dlb, dgn = _hgrn_bwd(db_in, z, o, states, lb[l:l + 1], hgrn_norm_g[l:l + 1], dz, f"hgrn_bwd_{l}",
                                 after=st_small and st_small["token"])
        if l == 0:
            scatter_pair_finish(st_small, dlb)
            scatter_chips_start(st_small)
        dz, dpw, dps = _pool_bwd(da_in, z, pool_w[l], pool_scale[l:l + 1], dz, f"pool_bwd_{l}",
                                 after=st_small and st_small["token"])
        small[l] = dict(g_post=acc_post[1], pool_w=dpw, pool_scale=dps[0], lb_logits=dlb[0], hgrn_norm_g=dgn[0])
        token = None
        if l == 0:
            parts = {name: jnp.stack([small[0][name], small[1][name]]) for name in small[0]}
            parts.update(b_ada=[None, d_ada[1]], g_pre=[None, small[1]["g_pre"]])
            sg_stream = _direct_gather_stream("sg")
            early = _pack_small(parts, 2)
            sg_bufs, (sg_sems,), token = _comm_call(
                "small_grads_start", dict(s_sg=early, g_sg=_with_own_slot(early, me)), start=[sg_stream])
        g_win = {f"win{l}": _in_proj_dw(h, dz, f"grad_w_in_{l}", after=token)}
        st_win = scat[f"win{l}"] = scatter_pair_start(f"win{l}", g_win if l == 0 else {**g_small, **g_win})
        if l > 0:
            dh = _in_proj_dh(dz, gw[f"win{l}"], seq, f"in_proj_dh_{l}", after=st_win["token"])
            scatter_pair_finish(st_win, dh)
            scatter_chips_start(st_win)
        else:
            scatter_pair_finish(st_win, st_win["token"])
            scatter_chips_start(st_win)
            after = st_win["token"]
            for unit in ("win1", "small0"):
                after = finish_unit(unit, after)
            dh = _in_proj_dh(dz, gw[f"win{l}"], seq, f"in_proj_dh_{l}", after=after)
        dx, acc_pre = _prenorm_bwd(xs[l], dh, dx, g_pre[l:l + 1], scale[l], tm, f"prenorm_bwd_{l}",
                                   after=st_win["token"])
        d_ada[l] = jnp.concatenate([acc_pre[0], acc_pre[1], acc_post[0]])
        small[l]["g_pre"] = acc_pre[2]
    grad_x = dx[None]

    parts = dict(b_ada=[d_ada[0]], g_pre=[small[0]["g_pre"]])
    late = jnp.concatenate([_pack_small(parts, 0, 2), jnp.broadcast_to(loss_part, (8, 128))], axis=0)
    g_late = _allgather_small(late, "allgather_late_grads")
    loss = jnp.sum(g_late[:, SMALL_LATE_ROWS, 0])
    sg_bufs, _, _ = _comm_call("small_grads_done", sg_bufs, wait=[(sg_stream, sg_sems)], after=g_late)
    g_early = sg_bufs["g_sg"]
    small_names = list(dict.fromkeys(name for name, _, _ in _SMALL_ROWS))
    weights = dict(b_ada=b_ada, g_pre=g_pre, g_post=g_post, pool_w=pool_w, pool_scale=pool_scale,
                   lb_logits=lb_logits, hgrn_norm_g=hgrn_norm_g)
    m_small = dict(b_ada=m_b_ada, g_pre=m_g_pre, g_post=m_g_post, pool_w=m_pool_w, pool_scale=m_pool_scale,
                   lb_logits=m_lb_logits, hgrn_norm_g=m_hgrn_norm_g)
    v_small = dict(b_ada=v_b_ada, g_pre=v_g_pre, g_post=v_g_post, pool_w=v_pool_w, pool_scale=v_pool_scale,
                   lb_logits=v_lb_logits, hgrn_norm_g=v_hgrn_norm_g)
    shapes = {name: weights[name].shape for name in small_names}
    small_out = _adamw_small(_pack_small(weights), _pack_small(m_small), _pack_small(v_small), g_late, g_early,
                             shapes)

    d_ada_all = jnp.stack([g_late[:, 0:24, :].reshape(N_DEV, 3 * D_MODEL),
                           g_early[:, 0:24, :].reshape(N_DEV, 3 * D_MODEL)], axis=1)
    d_cols = jnp.transpose(lax.dynamic_slice_in_dim(d_ada_all, me * ADA_COLS, ADA_COLS, axis=2), (1, 0, 2))
    g_w_ada = _ada_bwd(c_all, d_cols)
    ada_out = _adamw_sharded(w_ada, m_w_ada, v_w_ada, g_w_ada[:, None], 256, "adamw_w_ada")
    finish_unit("win0", ada_out[1][0, 0:8, 0:128] + small_out[1]["pool_scale"][0:1, 0:128])

    def leaf(kind):
        s = small_out[kind]
        return (ada_out[kind], s["b_ada"], s["g_pre"], s["g_post"], big_out["win"][kind], s["pool_w"], s["pool_scale"],
                s["lb_logits"], s["hgrn_norm_g"], big_out["wpo"][kind], big_out["who"][kind], big_out["wout"][kind])

    return (loss, grad_x) + leaf(0) + leaf(1) + leaf(2) + leaf(3)
```

```python
import jax
import jax.numpy as jnp
from jax import lax
from jax.experimental import pallas as pl
from jax.experimental.pallas import tpu as pltpu

F32 = jnp.float32
MXU_DTYPE = jnp.bfloat16
WIRE_DTYPE = jnp.bfloat16

N_DEV = 8
DEPTH = 2
D_MODEL = 1024
HEADS = 8
HEAD_DIM = 128
POOL_GROUPS = 4
GROUP_DIM = 128
POOL_WIDTH = POOL_GROUPS * GROUP_DIM
IN_WIDTH = 7168
CHUNK = 64
SUB = 16
N_SUB = CHUNK // SUB
EXP_CLAMP = 80.0
NORM_EPS = 1e-6
LOG_FLOOR = 1e-30
ADA_COLS = 3 * D_MODEL // N_DEV
IN_COLS = IN_WIDTH // N_DEV
COL_HQ, COL_HF, COL_HI, COL_HG, COL_MGP, COL_MGH = 1, 2, 3, 4, 5, 6

ADAM_LR = 0.001
ADAM_B1 = 0.9
ADAM_B2 = 0.999
ADAM_EPS = 1e-08
ADAM_WD = 0.01
ADAM_STEP = 10

VMEM_LIMIT = 48 * 1024 * 1024
MESH_ID = pl.DeviceIdType.MESH
HIGHEST = lax.Precision.HIGHEST

_SMALL_ROWS = (("b_ada", 0, 24), ("g_pre", 0, 8), ("b_ada", 1, 24), ("g_pre", 1, 8), ("g_post", None, 16),
               ("pool_w", None, 1024), ("pool_scale", None, 8), ("lb_logits", None, 16), ("hgrn_norm_g", None, 2))
SMALL_LATE_ROWS = 32
SMALL_ROWS_PAD = 1136
LB_ROW0 = 32 + 32 + 16 + 1024 + 8


def _params(**kw):
    return pltpu.CompilerParams(vmem_limit_bytes=VMEM_LIMIT, **kw)


def _sigmoid(v):
    return 1.0 / (1.0 + jnp.exp(-v))


def _dsilu(v, s):
    return s * (1.0 + v * (1.0 - s))


def _dot(a, b):
    return jnp.dot(a.astype(MXU_DTYPE), b.astype(MXU_DTYPE), preferred_element_type=F32)


def _dot_nt(a, b):
    return lax.dot_general(a.astype(MXU_DTYPE), b.astype(MXU_DTYPE), (((1,), (1,)), ((), ())),
                           preferred_element_type=F32)


def _dot_tn(a, b):
    return lax.dot_general(a.astype(MXU_DTYPE), b.astype(MXU_DTYPE), (((0,), (0,)), ((), ())),
                           preferred_element_type=F32)


def _pallas_after(body, n_in, after, *, in_specs, **kw):
    if after is None:
        return pl.pallas_call(body, in_specs=in_specs, **kw)

    def tied(*refs):
        body(*refs[:n_in], *refs[n_in + 1:])

    call = pl.pallas_call(tied, in_specs=list(in_specs) + [pl.BlockSpec(memory_space=pl.ANY)], **kw)
    return lambda *operands: call(*operands, after)


def _my_position():
    mx, my, mc = lax.axis_index("x"), lax.axis_index("y"), lax.axis_index("c")
    return mx, my, mc, 4 * mx + 2 * my + mc


def _peer(mx, my, mc, k):
    px = 1 - mx if (k >> 2) & 1 else mx
    py = 1 - my if (k >> 1) & 1 else my
    pc = 1 - mc if k & 1 else mc
    return (px, py, pc), 4 * px + 2 * py + pc


def _allgather_small(v, name, after=None):
    rows, cols = v.shape

    def body(v_ref, out_ref, send_sems, recv_sems):
        mx, my, mc, me = _my_position()
        out_ref[me] = v_ref[...]
        copies = []
        for k in range(1, N_DEV):
            peer, _ = _peer(mx, my, mc, k)
            cp = pltpu.make_async_remote_copy(
                src_ref=v_ref, dst_ref=out_ref.at[me],
                send_sem=send_sems.at[k - 1], recv_sem=recv_sems.at[k - 1],
                device_id=peer, device_id_type=MESH_ID)
            cp.start()
            copies.append(cp)
        for cp in copies:
            cp.wait()

    return _pallas_after(
        body, 1, after, name=name,
        out_shape=jax.ShapeDtypeStruct((N_DEV, rows, cols), v.dtype),
        in_specs=[pl.BlockSpec(memory_space=pltpu.VMEM)],
        out_specs=pl.BlockSpec(memory_space=pltpu.VMEM),
        scratch_shapes=[pltpu.SemaphoreType.DMA((N_DEV - 1,)), pltpu.SemaphoreType.DMA((N_DEV - 1,))],
        compiler_params=_params(),
    )(v)


class _Stream:
    def __init__(self, n, plan):
        self.n, self.plan = n, plan


def _comm_call(name, bufs, start=(), wait=(), after=None):
    names = list(bufs)

    def body(*refs):
        it = iter(refs)
        buf_refs = {n: next(it) for n in names}
        wait_sems = [(next(it), next(it)) for _ in wait]
        if after is not None:
            next(it)
        start_sems = [(next(it), next(it)) for _ in start]
        for _ in names:
            next(it)
        token = next(it)
        pos = _my_position()

        def descriptors(stream, sems):
            return [pltpu.make_async_remote_copy(src_ref=src, dst_ref=dst, send_sem=sems[0].at[k], recv_sem=sems[1].at[k],
                                                 device_id=dev, device_id_type=MESH_ID)
                    for k, (src, dst, dev) in enumerate(stream.plan(buf_refs, pos))]

        for (stream, _), sems in zip(wait, wait_sems):
            for cp in descriptors(stream, sems):
                cp.wait_send()
                cp.wait_recv()
        for stream, sems in zip(start, start_sems):
            for cp in descriptors(stream, sems):
                cp.start()
        token[...] = jnp.zeros_like(token)

    hbm = pl.BlockSpec(memory_space=pltpu.HBM)
    sem = pl.BlockSpec(memory_space=pltpu.SEMAPHORE)
    operands = [pltpu.with_memory_space_constraint(bufs[n], pltpu.HBM) for n in names]
    in_specs = [hbm] * len(names)
    for _, (send_sems, recv_sems) in wait:
        operands += [send_sems, recv_sems]
        in_specs += [sem, sem]
    if after is not None:
        operands.append(after)
        in_specs.append(pl.BlockSpec(memory_space=pl.ANY))
    out_shape, out_specs = [], []
    for stream in start:
        out_shape += [pltpu.SemaphoreType.DMA((stream.n,)), pltpu.SemaphoreType.DMA((stream.n,))]
        out_specs += [sem, sem]
    n_sem_out = len(out_shape)
    out_shape += [pltpu.HBM(bufs[n].shape, bufs[n].dtype) for n in names]
    out_specs += [hbm] * len(names)
    out_shape.append(jax.ShapeDtypeStruct((8, 128), F32))
    out_specs.append(pl.BlockSpec(memory_space=pltpu.VMEM))
    outs = pl.pallas_call(
        body, name=name, out_shape=out_shape, in_specs=in_specs, out_specs=out_specs,
        input_output_aliases={i: n_sem_out + i for i in range(len(names))},
        compiler_params=pltpu.CompilerParams(has_side_effects=pltpu.SideEffectType.DATAFLOW_SIDE_EFFECTING),
    )(*operands)
    sems = [(outs[2 * i], outs[2 * i + 1]) for i in range(len(start))]
    return dict(zip(names, outs[n_sem_out:n_sem_out + len(names)])), sems, outs[-1]


def _with_own_slot(block, me):
    return lax.dynamic_update_index_in_dim(lax.empty((N_DEV,) + block.shape, block.dtype), block, me, 0)


def _other_chips(pos):
    mx, my, _, _ = pos
    return [(1 - mx if i & 2 else mx, 1 - my if i & 1 else my) for i in (1, 2, 3)]


def _dev_index(px, py, pc):
    return 4 * px + 2 * py + pc


def _gather_streams(keys):
    def to_chips(refs, pos):
        _, _, mc, me = pos
        return [(refs["s_" + k], refs["g_" + k].at[me], (cx, cy, mc)) for k in keys for cx, cy in _other_chips(pos)]

    def to_sibling(refs, pos):
        mx, my, mc, me = pos
        return [(refs["s_" + k], refs["g_" + k].at[me], (mx, my, 1 - mc)) for k in keys]

    def pass_on(refs, pos):
        mx, my, mc, _ = pos
        out = []
        for k in keys:
            for cx, cy in _other_chips(pos):
                slot = refs["g_" + k].at[_dev_index(cx, cy, mc)]
                out.append((slot, slot, (mx, my, 1 - mc)))
        return out

    return _Stream(3 * len(keys), to_chips), _Stream(len(keys), to_sibling), _Stream(3 * len(keys), pass_on)


def _direct_gather_stream(key):
    def plan(refs, pos):
        mx, my, mc, me = pos
        return [(refs["s_" + key], refs["g_" + key].at[me], _peer(mx, my, mc, k)[0]) for k in range(1, N_DEV)]

    return _Stream(N_DEV - 1, plan)


def _scatter_streams(keys):
    def pair(refs, pos):
        mx, my, mc, _ = pos
        sib = (mx, my, 1 - mc)
        out = []
        for k in keys:
            for i, (cx, cy) in enumerate(_other_chips(pos)):
                out.append((refs["g_" + k].at[_dev_index(cx, cy, 1 - mc)], refs["st_" + k].at[i], sib))
            out.append((refs["g_" + k].at[_dev_index(mx, my, 1 - mc)], refs["st_" + k].at[3], sib))
        return out

    def chips(refs, pos):
        mc = pos[2]
        return [(refs["ps_" + k].at[i], refs["ld_" + k].at[i], (cx, cy, mc))
                for k in keys for i, (cx, cy) in enumerate(_other_chips(pos))]

    return _Stream(4 * len(keys), pair), _Stream(3 * len(keys), chips)


def _pair_sum(g, st, idx, tr, name):
    _, rows, cols = g.shape

    def body(idx_ref, g_ref, st_ref, out_ref):
        out_ref[...] = (g_ref[...].astype(F32) + st_ref[...].astype(F32)).astype(out_ref.dtype)

    return pl.pallas_call(
        body, name=name,
        grid_spec=pltpu.PrefetchScalarGridSpec(
            num_scalar_prefetch=1, grid=(4, rows // tr),
            in_specs=[pl.BlockSpec((None, tr, cols), lambda j, i, idx_ref: (idx_ref[j], i, 0)),
                      pl.BlockSpec((None, tr, cols), lambda j, i, idx_ref: (j, i, 0))],
            out_specs=pl.BlockSpec((None, tr, cols), lambda j, i, idx_ref: (j, i, 0))),
        out_shape=jax.ShapeDtypeStruct((4, rows, cols), WIRE_DTYPE),
        compiler_params=_params(dimension_semantics=("parallel", "parallel")),
    )(idx, g, st)


def _ada_fwd(c_all, w_ada, b_cols):
    def body(c_ref, w_ref, b_ref, out_ref):
        cv = c_ref[...]
        ca = cv * _sigmoid(cv)
        for l in range(DEPTH):
            out_ref[l] = jnp.dot(ca, w_ref[l], precision=HIGHEST, preferred_element_type=F32) + b_ref[l:l + 1, :]

    return pl.pallas_call(
        body, name="ada_fwd",
        out_shape=jax.ShapeDtypeStruct((DEPTH, N_DEV, ADA_COLS), F32),
        compiler_params=_params(),
    )(c_all, w_ada, b_cols)


def _ada_bwd(c_all, d_cols):
    def body(c_ref, d_ref, out_ref):
        cv = c_ref[...]
        ca = cv * _sigmoid(cv)
        for l in range(DEPTH):
            out_ref[l] = lax.dot_general(ca, d_ref[l], (((0,), (0,)), ((), ())), precision=HIGHEST,
                                         preferred_element_type=F32)

    return pl.pallas_call(
        body, name="ada_bwd",
        out_shape=jax.ShapeDtypeStruct((DEPTH, D_MODEL, ADA_COLS), F32),
        compiler_params=_params(),
    )(c_all, d_cols)


def _lower_bounds(logits):
    m = jnp.maximum(logits[0:1], logits[1:2])
    e0, e1 = jnp.exp(logits[0:1] - m), jnp.exp(logits[1:2] - m)
    den = e0 + e1
    p0, p1 = e0 / den, e1 / den
    low0 = p0 - p0
    low1 = (p0 + p1) - p0
    return (p0, p1), (low0, low1)


def _lb_fwd(lb_logits):
    def body(lg_ref, out_ref):
        _, (low0, low1) = _lower_bounds(lg_ref[...])
        out_ref[0:1, :] = jnp.clip(low0, 0.0, 1.0)
        out_ref[1:2, :] = jnp.clip(low1, 0.0, 1.0)

    return pl.pallas_call(body, name="lb_fwd", out_shape=jax.ShapeDtypeStruct(lb_logits.shape, F32),
                          compiler_params=_params())(lb_logits)


def _row_spec(cols=D_MODEL):
    return pl.BlockSpec((1, cols), lambda *_: (0, 0))


def _prenorm_fwd(x, g, shift, scale, tm, name, after=None):
    seq = x.shape[0]

    def body(x_ref, g_ref, sh_ref, sc_ref, h_ref):
        xv = x_ref[...]
        rs = lax.rsqrt(jnp.mean(xv * xv, axis=-1, keepdims=True) + NORM_EPS)
        h = (xv * rs * g_ref[...]) * (1.0 + sc_ref[...]) + sh_ref[...]
        h_ref[...] = h.astype(h_ref.dtype)

    tile = pl.BlockSpec((tm, D_MODEL), lambda i: (i, 0))
    return _pallas_after(
        body, 4, after, name=name, grid=(seq // tm,),
        in_specs=[tile, _row_spec(), _row_spec(), _row_spec()], out_specs=tile,
        out_shape=jax.ShapeDtypeStruct((seq, D_MODEL), MXU_DTYPE),
        compiler_params=_params(dimension_semantics=("parallel",)),
    )(x, g, shift, scale)


def _in_proj(h, win_g, tm, name, after=None):
    seq = h.shape[0]

    def body(h_ref, w_ref, z_ref):
        z_ref[...] = jnp.dot(h_ref[...], w_ref[...], preferred_element_type=F32)

    return _pallas_after(
        body, 2, after, name=name, grid=(N_DEV, seq // tm),
        in_specs=[pl.BlockSpec((tm, D_MODEL), lambda j, i: (i, 0)),
                  pl.BlockSpec((None, D_MODEL, IN_COLS), lambda j, i: (j, 0, 0))],
        out_specs=pl.BlockSpec((tm, IN_COLS), lambda j, i: (i, j)),
        out_shape=jax.ShapeDtypeStruct((seq, IN_WIDTH), F32),
        compiler_params=_params(dimension_semantics=("parallel", "parallel")),
    )(h, win_g)


def _shift_down(v, j, pos):
    return jnp.where(pos >= j, pltpu.roll(v, j, 0), 0.0)


def _shift_up(v, j, pos, seq):
    return jnp.where(pos < seq - j, pltpu.roll(v, seq - j, 0), 0.0)


def _select_window(g, candidates):
    out = candidates[-1]
    for i in range(len(candidates) - 2, -1, -1):
        out = jnp.where(g == i, candidates[i], out)
    return out


def _pool_mean_minus_token(u, g, pos):
    sums, acc = [], u
    for j in (1, 2, 4, 8):
        acc = acc + _shift_down(acc, j, pos)
        sums.append(acc)
    wsum = _select_window(g, sums)
    width = jnp.left_shift(2, g).astype(F32)
    count = jnp.minimum(pos.astype(F32) + 1.0, width)
    return wsum / count - u, count


def _pool_fwd(z, pool_w_l, pool_scale_l, name, after=None):
    seq = z.shape[0]

    def body(pv_ref, pg_ref, w_ref, sc_ref, out_ref):
        g = pl.program_id(0)
        pos = lax.broadcasted_iota(jnp.int32, (seq, GROUP_DIM), 0)
        pm, _ = _pool_mean_minus_token(pv_ref[...], g, pos)
        lin = _dot(pm, w_ref[...]) * sc_ref[...]
        pg = pg_ref[...]
        out_ref[...] = (lin * (pg * _sigmoid(pg))).astype(out_ref.dtype)

    return _pallas_after(
        body, 4, after, name=name, grid=(POOL_GROUPS,),
        in_specs=[pl.BlockSpec((seq, GROUP_DIM), lambda g: (0, g)),
                  pl.BlockSpec((seq, GROUP_DIM), lambda g: (0, POOL_GROUPS + g)),
                  pl.BlockSpec((None, GROUP_DIM, GROUP_DIM), lambda g: (g, 0, 0)),
                  pl.BlockSpec((1, GROUP_DIM), lambda g: (0, g))],
        out_specs=pl.BlockSpec((seq, GROUP_DIM), lambda g: (0, g)),
        out_shape=jax.ShapeDtypeStruct((seq, POOL_WIDTH), MXU_DTYPE),
        compiler_params=_params(dimension_semantics=("parallel",)),
    )(z, z, pool_w_l, pool_scale_l)


def _chunk_masks():
    row = lax.broadcasted_iota(jnp.int32, (CHUNK, CHUNK), 0)
    col = lax.broadcasted_iota(jnp.int32, (CHUNK, CHUNK), 1)
    causal = row >= col
    before_sub = col < (row // SUB) * SUB
    suffix = row <= col
    return causal, before_sub, suffix


def _masked_sums(masks, v):
    lhs = jnp.concatenate([m.astype(jnp.bfloat16) for m in masks], axis=0)
    hi = v.astype(jnp.bfloat16)
    rest = v - hi.astype(F32)
    mid = rest.astype(jnp.bfloat16)
    lo = (rest - mid.astype(F32)).astype(jnp.bfloat16)
    out = jnp.dot(lhs, hi, preferred_element_type=F32)
    out += jnp.dot(lhs, mid, preferred_element_type=F32)
    out += jnp.dot(lhs, lo, preferred_element_type=F32)
    return [out[i * CHUNK:(i + 1) * CHUNK] for i in range(len(masks))]


def _gates(zf, lb):
    sg = _sigmoid(zf)
    f = lb + (1.0 - lb) * sg
    logf = jnp.log(jnp.maximum(f, LOG_FLOOR))
    return sg, f, logf


def _intra_blocks(q_h, k_h, cum_h, base_h, causal):
    rel = cum_h - base_h
    out = []
    for i in range(N_SUB):
        rows = slice(i * SUB, (i + 1) * SUB)
        e_q = jnp.exp(rel[rows])
        base_i = jnp.concatenate([base_h[rows]] * N_SUB, axis=0)
        e_k = jnp.exp(jnp.minimum(base_i - cum_h, EXP_CLAMP))
        q_t = (q_h[rows] * e_q).astype(MXU_DTYPE)
        k_t = (k_h * e_k).astype(MXU_DTYPE)
        a_i = jnp.where(causal[rows], _dot_nt(q_t, k_t), 0.0)
        out.append((q_t, k_t, e_q, e_k, a_i))
    return out


def _hgrn_fwd(z, lb_l, gn_l, name, after=None):
    seq = z.shape[0]
    n_chunks = seq // CHUNK

    def body(hq_ref, hf_ref, hi_ref, hg_ref, lb_ref, gn_ref, o_ref, bin_ref, st_ref, state):
        @pl.when(pl.program_id(0) == 0)
        def _():
            state[...] = jnp.zeros_like(state)

        causal, before_sub, _ = _chunk_masks()
        _, f, logf = _gates(hf_ref[...], lb_ref[...])
        kk = 1.0 - f
        hq = hq_ref[...]
        q = hq * _sigmoid(hq)
        cum, base = _masked_sums([causal, before_sub], logf)
        st_ref[0] = state[...]
        for h in range(HEADS):
            sl = slice(h * HEAD_DIM, (h + 1) * HEAD_DIM)
            q_h, k_h, cum_h = q[:, sl], kk[:, sl], cum[:, sl]
            v_h = hi_ref[:, sl]
            st_h = state[h]
            blocks = _intra_blocks(q_h, k_h, cum_h, base[:, sl], causal)
            a = jnp.concatenate([b[4] for b in blocks], axis=0)
            o_h = _dot_nt(q_h * jnp.exp(cum_h), st_h) + _dot(a, v_h)
            last = jnp.sum(logf[:, sl], axis=0, keepdims=True)
            state[h] = st_h * jnp.exp(last) + _dot_tn(v_h, k_h * jnp.exp(last - cum_h))
            rs = lax.rsqrt(jnp.mean(o_h * o_h, axis=-1, keepdims=True) + NORM_EPS)
            hg = hg_ref[:, sl]
            o_ref[:, sl] = o_h
            bin_ref[:, sl] = ((o_h * rs * gn_ref[...]) * (hg * _sigmoid(hg))).astype(bin_ref.dtype)

    def col(block):
        return pl.BlockSpec((CHUNK, D_MODEL), lambda c: (c, block))

    tile = pl.BlockSpec((CHUNK, D_MODEL), lambda c: (c, 0))
    return _pallas_after(
        body, 6, after, name=name, grid=(n_chunks,),
        in_specs=[col(COL_HQ), col(COL_HF), col(COL_HI), col(COL_HG), _row_spec(), _row_spec(HEAD_DIM)],
        out_specs=[tile, tile, pl.BlockSpec((1, HEADS, HEAD_DIM, HEAD_DIM), lambda c: (c, 0, 0, 0))],
        out_shape=[jax.ShapeDtypeStruct((seq, D_MODEL), F32),
                   jax.ShapeDtypeStruct((seq, D_MODEL), MXU_DTYPE),
                   jax.ShapeDtypeStruct((n_chunks, HEADS, HEAD_DIM, HEAD_DIM), F32)],
        scratch_shapes=[pltpu.VMEM((HEADS, HEAD_DIM, HEAD_DIM), F32)],
        compiler_params=_params(dimension_semantics=("arbitrary",)),
    )(z, z, z, z, lb_l, gn_l)


def _rms_parts(y):
    rs = lax.rsqrt(jnp.mean(y * y, axis=-1, keepdims=True) + NORM_EPS)
    return rs, y * rs


def _merge_fwd(a_in, b_in, z, x, wpo_g, who_g, wout_g, gate, g_post, tm, name):
    seq = x.shape[0]

    def body(a_ref, b_ref, mgp_ref, mgh_ref, x_ref, wpo_ref, who_ref, wout_ref, gate_ref, gp_ref,
             ba_ref, bb_ref, mer_ref, y_ref, xn_ref):
        a = a_ref[...]
        ba = jnp.concatenate([_dot(a, wpo_ref[j]) for j in range(N_DEV)], axis=1)
        bb = _dot(b_ref[...], who_ref[...])
        merged = _sigmoid(mgp_ref[...]) * ba + _sigmoid(mgh_ref[...]) * bb
        y = _dot(merged, wout_ref[...])
        _, yn = _rms_parts(y)
        ba_ref[...] = ba.astype(ba_ref.dtype)
        bb_ref[...] = bb.astype(bb_ref.dtype)
        mer_ref[...] = merged.astype(mer_ref.dtype)
        y_ref[...] = y
        xn_ref[...] = x_ref[...] + gate_ref[...] * (yn * gp_ref[...])

    def tile(cols=D_MODEL, block=0):
        return pl.BlockSpec((tm, cols), lambda i: (i, block))

    full = pl.BlockSpec((D_MODEL, D_MODEL), lambda i: (0, 0))
    act = jax.ShapeDtypeStruct((seq, D_MODEL), MXU_DTYPE)
    f32 = jax.ShapeDtypeStruct((seq, D_MODEL), F32)
    return pl.pallas_call(
        body, name=name, grid=(seq // tm,),
        in_specs=[tile(POOL_WIDTH), tile(), tile(block=COL_MGP), tile(block=COL_MGH), tile(),
                  pl.BlockSpec((N_DEV, POOL_WIDTH, GROUP_DIM), lambda i: (0, 0, 0)),
                  full, full, _row_spec(), _row_spec()],
        out_specs=[tile(), tile(), tile(), tile(), tile()],
        out_shape=[act, act, act, f32, f32],
        compiler_params=_params(dimension_semantics=("parallel",)),
    )(a_in, b_in, z, z, x, wpo_g, who_g, wout_g, gate, g_post)


def _loss_grad(x_out, target, tm):
    seq = x_out.shape[0]

    def body(x_ref, t_ref, loss_ref, dx_ref):
        @pl.when(pl.program_id(0) == 0)
        def _():
            loss_ref[...] = jnp.zeros_like(loss_ref)

        err = x_ref[...] - t_ref[...]
        per_token = jnp.mean(err * err, axis=-1, keepdims=True)
        loss_ref[...] += 0.5 * jnp.sum(per_token, axis=0, keepdims=True)
        dx_ref[...] = err * (1.0 / D_MODEL)

    tile = pl.BlockSpec((tm, D_MODEL), lambda i: (i, 0))
    return pl.pallas_call(
        body, name="loss_grad", grid=(seq // tm,),
        in_specs=[tile, tile],
        out_specs=[pl.BlockSpec((1, 1), lambda i: (0, 0)), tile],
        out_shape=[jax.ShapeDtypeStruct((1, 1), F32), jax.ShapeDtypeStruct((seq, D_MODEL), F32)],
        compiler_params=_params(dimension_semantics=("arbitrary",)),
    )(x_out, target)


def _stage_copy(stage, sems, dst, slot, step, where):
    rows, cols = where(step)
    return pltpu.make_async_copy(stage.at[slot], dst.at[rows, cols], sems.at[slot])


def _stage_begin(stage, sems, dst, step, where):
    slot = step % 2

    @pl.when(step >= 2)
    def _():
        _stage_copy(stage, sems, dst, slot, step - 2, where).wait()

    return slot


def _stage_end(stage, sems, dst, step, n_steps, where):
    slot = step % 2
    _stage_copy(stage, sems, dst, slot, step, where).start()

    @pl.when(step == n_steps - 1)
    def _():
        _stage_copy(stage, sems, dst, slot, step, where).wait()
        if n_steps > 1:
            _stage_copy(stage, sems, dst, 1 - slot, step - 1, where).wait()


def _merge_bwd(dx, y, ba, bb, z, wpo_g, who_g, wout_g, gate, g_post, dz, tm, name):
    seq = dx.shape[0]
    n_steps = seq // tm

    def body(dx_ref, y_ref, ba_ref, bb_ref, mgp_ref, mgh_ref, wpo_ref, who_ref, wout_ref, gate_ref, gp_ref, _,
             dy_ref, dba_ref, dbb_ref, da_ref, db_ref, dz_ref, acc_ref, stage, sems):
        step = pl.program_id(0)

        @pl.when(step == 0)
        def _():
            acc_ref[...] = jnp.zeros_like(acc_ref)

        def where(t):
            return pl.ds(t * tm, tm), pl.ds(COL_MGP * D_MODEL, 2 * D_MODEL)

        dmg_ref = stage.at[_stage_begin(stage, sems, dz_ref, step, where)]

        dxv = dx_ref[...]
        rs, yn = _rms_parts(y_ref[...])
        acc_ref[0:1, :] += jnp.sum(dxv * yn * gp_ref[...], axis=0, keepdims=True)
        acc_ref[1:2, :] += jnp.sum(dxv * gate_ref[...] * yn, axis=0, keepdims=True)
        dyn = dxv * (gate_ref[...] * gp_ref[...])
        dy = rs * (dyn - yn * jnp.mean(dyn * yn, axis=-1, keepdims=True))
        dmerged = _dot_nt(dy, wout_ref[...])
        sp, sh = _sigmoid(mgp_ref[...]), _sigmoid(mgh_ref[...])
        dba, dbb = sp * dmerged, sh * dmerged
        dmg_ref[:, 0:D_MODEL] = (dmerged * ba_ref[...].astype(F32) * sp * (1.0 - sp)).astype(dmg_ref.dtype)
        dmg_ref[:, D_MODEL:2 * D_MODEL] = (dmerged * bb_ref[...].astype(F32) * sh * (1.0 - sh)).astype(dmg_ref.dtype)
        da = _dot_nt(dba[:, 0:GROUP_DIM], wpo_ref[0])
        for j in range(1, N_DEV):
            da += _dot_nt(dba[:, j * GROUP_DIM:(j + 1) * GROUP_DIM], wpo_ref[j])
        dy_ref[...] = dy.astype(dy_ref.dtype)
        dba_ref[...] = dba.astype(dba_ref.dtype)
        dbb_ref[...] = dbb.astype(dbb_ref.dtype)
        da_ref[...] = da
        db_ref[...] = _dot_nt(dbb, who_ref[...])
        _stage_end(stage, sems, dz_ref, step, n_steps, where)

    def tile(cols=D_MODEL, block=0):
        return pl.BlockSpec((tm, cols), lambda i: (i, block))

    full = pl.BlockSpec((D_MODEL, D_MODEL), lambda i: (0, 0))
    hbm = pl.BlockSpec(memory_space=pl.ANY)
    act = jax.ShapeDtypeStruct((seq, D_MODEL), MXU_DTYPE)
    return pl.pallas_call(
        body, name=name, grid=(n_steps,),
        in_specs=[tile(), tile(), tile(), tile(), tile(block=COL_MGP), tile(block=COL_MGH),
                  pl.BlockSpec((N_DEV, POOL_WIDTH, GROUP_DIM), lambda i: (0, 0, 0)),
                  full, full, _row_spec(), _row_spec(), hbm],
        out_specs=[tile(), tile(), tile(), tile(POOL_WIDTH), tile(), hbm,
                   pl.BlockSpec((8, D_MODEL), lambda i: (0, 0))],
        out_shape=[act, act, act, jax.ShapeDtypeStruct((seq, POOL_WIDTH), F32),
                   jax.ShapeDtypeStruct((seq, D_MODEL), F32),
                   jax.ShapeDtypeStruct(dz.shape, dz.dtype),
                   jax.ShapeDtypeStruct((8, D_MODEL), F32)],
        input_output_aliases={11: 5},
        scratch_shapes=[pltpu.VMEM((2, tm, 2 * D_MODEL), MXU_DTYPE), pltpu.SemaphoreType.DMA((2,))],
        compiler_params=_params(dimension_semantics=("arbitrary",)),
    )(dx, y, ba, bb, z, z, wpo_g, who_g, wout_g, gate, g_post, dz)


def _grad_out_weights(merged, dy, b_in, dbb, a_in, dba, name):
    seq = merged.shape[0]
    tn = D_MODEL // 2
    per_step = tn // GROUP_DIM

    def body(mer_ref, dy_ref, b_ref, dbb_ref, a_ref, dba_ref, gout_ref, gho_ref, gpo_ref):
        gout_ref[...] = _dot_tn(mer_ref[...], dy_ref[...]).astype(gout_ref.dtype)
        gho_ref[...] = _dot_tn(b_ref[...], dbb_ref[...]).astype(gho_ref.dtype)
        g_po = _dot_tn(a_ref[...], dba_ref[...])
        for j in range(per_step):
            gpo_ref[j] = g_po[:, j * GROUP_DIM:(j + 1) * GROUP_DIM].astype(gpo_ref.dtype)

    def whole(cols):
        return pl.BlockSpec((seq, cols), lambda j: (0, 0))

    cols = pl.BlockSpec((seq, tn), lambda j: (0, j))
    return pl.pallas_call(
        body, name=name, grid=(D_MODEL // tn,),
        in_specs=[whole(D_MODEL), cols, whole(D_MODEL), cols, whole(POOL_WIDTH), cols],
        out_specs=[pl.BlockSpec((D_MODEL, tn), lambda j: (0, j)), pl.BlockSpec((D_MODEL, tn), lambda j: (0, j)),
                   pl.BlockSpec((per_step, POOL_WIDTH, GROUP_DIM), lambda j: (j, 0, 0))],
        out_shape=[jax.ShapeDtypeStruct((D_MODEL, D_MODEL), WIRE_DTYPE),
                   jax.ShapeDtypeStruct((D_MODEL, D_MODEL), WIRE_DTYPE),
                   jax.ShapeDtypeStruct((N_DEV, POOL_WIDTH, GROUP_DIM), WIRE_DTYPE)],
        compiler_params=_params(dimension_semantics=("parallel",)),
    )(merged, dy, b_in, dbb, a_in, dba)


def _hgrn_bwd(db_in, z, o, states, lb_l, gn_l, dz, name, after=None):
    seq = z.shape[0]
    n_chunks = seq // CHUNK

    last_chunk = n_chunks - 1

    def body(db_ref, hq_ref, hf_ref, hi_ref, hg_ref, o_ref, st_ref, lb_ref, gn_ref, _,
             dz_hbm, dlb_ref, dgn_ref, dstate, dq_buf, dk_buf, dg_buf, stage, sems):
        step = pl.program_id(0)

        @pl.when(step == 0)
        def _():
            dstate[...] = jnp.zeros_like(dstate)
            dlb_ref[...] = jnp.zeros_like(dlb_ref)
            dgn_ref[...] = jnp.zeros_like(dgn_ref)

        def where(t):
            return pl.ds((last_chunk - t) * CHUNK, CHUNK), pl.ds(COL_HQ * D_MODEL, 4 * D_MODEL)

        dz_ref = stage.at[_stage_begin(stage, sems, dz_hbm, step, where)]

        causal, before_sub, suffix = _chunk_masks()
        lb = lb_ref[...]
        sg, f, logf = _gates(hf_ref[...], lb)
        kk = 1.0 - f
        hq = hq_ref[...]
        sq = _sigmoid(hq)
        q = hq * sq
        cum, base = _masked_sums([causal, before_sub], logf)
        gn = gn_ref[...]
        dgn = jnp.zeros((1, HEAD_DIM), F32)
        dlast = []
        for h in range(HEADS):
            sl = slice(h * HEAD_DIM, (h + 1) * HEAD_DIM)
            q_h, k_h, cum_h = q[:, sl], kk[:, sl], cum[:, sl]
            v_h = hi_ref[:, sl]
            st_h = st_ref[0, h]
            dst_h = dstate[h]
            rs, ohat = _rms_parts(o_ref[:, sl])
            hg = hg_ref[:, sl]
            shg = _sigmoid(hg)
            d_bin = db_ref[:, sl]
            don = d_bin * (hg * shg)
            dgn += jnp.sum(don * ohat, axis=0, keepdims=True)
            dohat = don * gn
            do = rs * (dohat - ohat * jnp.mean(dohat * ohat, axis=-1, keepdims=True))
            dz_ref[:, 3 * D_MODEL + h * HEAD_DIM:3 * D_MODEL + (h + 1) * HEAD_DIM] = (
                d_bin * (ohat * gn) * _dsilu(hg, shg)).astype(dz_ref.dtype)
            last = jnp.sum(logf[:, sl], axis=0, keepdims=True)
            g_in = jnp.exp(cum_h)
            d_out = jnp.exp(last - cum_h)
            q_bar, k_bar = q_h * g_in, k_h * d_out
            blocks = _intra_blocks(q_h, k_h, cum_h, base[:, sl], causal)
            a = jnp.concatenate([b[4] for b in blocks], axis=0)
            da = jnp.where(causal, _dot_nt(do, v_h), 0.0)
            dv = _dot_tn(a, do) + _dot_nt(k_bar, dst_h)
            dq_bar, dk_bar = _dot(do, st_h), _dot(v_h, dst_h)
            dk = dk_bar * d_out
            dq_parts, dg_parts = [], []
            dg_k = k_bar * dk_bar
            dlast.append(jnp.sum(k_bar * dk_bar, axis=0, keepdims=True)
                         + jnp.exp(last) * jnp.sum(st_h * dst_h, axis=0, keepdims=True))
            for i, (q_t, k_t, e_q, e_k, _) in enumerate(blocks):
                da_i = da[i * SUB:(i + 1) * SUB].astype(MXU_DTYPE)
                dq_t = _dot(da_i, k_t)
                dk_t = _dot_tn(da_i, q_t)
                dq_parts.append(dq_t * e_q)
                dk += dk_t * e_k
                dg_parts.append(q_t.astype(F32) * dq_t)
                dg_k += k_t.astype(F32) * dk_t
            dq = dq_bar * g_in + jnp.concatenate(dq_parts, axis=0)
            dg_buf[:, sl] = q_bar * dq_bar + jnp.concatenate(dg_parts, axis=0) - dg_k
            dstate[h] = dst_h * jnp.exp(last) + _dot_tn(do, q_bar)
            dq_buf[:, sl] = dq
            dk_buf[:, sl] = dk
            dz_ref[:, 2 * D_MODEL + h * HEAD_DIM:2 * D_MODEL + (h + 1) * HEAD_DIM] = dv.astype(dz_ref.dtype)
        dgn_ref[...] += dgn
        dq_all, dk_all = dq_buf[...], dk_buf[...]
        dlogf = _masked_sums([suffix], dg_buf[...])[0] + jnp.concatenate(dlast, axis=1)
        df = jnp.where(f > LOG_FLOOR, dlogf / f, 0.0) - dk_all
        dlb_ref[...] += jnp.sum(df * (1.0 - sg), axis=0, keepdims=True)
        dz_ref[:, 0:D_MODEL] = (dq_all * _dsilu(hq, sq)).astype(dz_ref.dtype)
        dz_ref[:, D_MODEL:2 * D_MODEL] = (df * (1.0 - lb) * sg * (1.0 - sg)).astype(dz_ref.dtype)
        _stage_end(stage, sems, dz_hbm, step, n_chunks, where)

    def col(block):
        return pl.BlockSpec((CHUNK, D_MODEL), lambda c: (last_chunk - c, block))

    hbm = pl.BlockSpec(memory_space=pl.ANY)
    return _pallas_after(
        body, 10, after, name=name, grid=(n_chunks,),
        in_specs=[col(0), col(COL_HQ), col(COL_HF), col(COL_HI), col(COL_HG), col(0),
                  pl.BlockSpec((1, HEADS, HEAD_DIM, HEAD_DIM), lambda c: (last_chunk - c, 0, 0, 0)),
                  _row_spec(), _row_spec(HEAD_DIM), hbm],
        out_specs=[hbm, _row_spec(), _row_spec(HEAD_DIM)],
        out_shape=[jax.ShapeDtypeStruct(dz.shape, dz.dtype),
                   jax.ShapeDtypeStruct((1, D_MODEL), F32), jax.ShapeDtypeStruct((1, HEAD_DIM), F32)],
        input_output_aliases={9: 0},
        scratch_shapes=[pltpu.VMEM((HEADS, HEAD_DIM, HEAD_DIM), F32)] + [pltpu.VMEM((CHUNK, D_MODEL), F32)] * 3
        + [pltpu.VMEM((2, CHUNK, 4 * D_MODEL), MXU_DTYPE), pltpu.SemaphoreType.DMA((2,))],
        compiler_params=_params(dimension_semantics=("arbitrary",)),
    )(db_in, z, z, z, z, o, states, lb_l, gn_l, dz)


def _pool_bwd(da_in, z, pool_w_l, pool_scale_l, dz, name, after=None):
    seq = z.shape[0]

    def body(da_ref, pv_ref, pg_ref, w_ref, sc_ref, _, dz_hbm, dw_ref, dsc_ref, stage_pv, stage_pg, sems_pv, sems_pg):
        g = pl.program_id(0)

        def where_pv(t):
            return pl.ds(0, seq), pl.ds(pl.multiple_of(t * GROUP_DIM, GROUP_DIM), GROUP_DIM)

        def where_pg(t):
            return pl.ds(0, seq), pl.ds(pl.multiple_of(POOL_WIDTH + t * GROUP_DIM, GROUP_DIM), GROUP_DIM)

        dpv_ref = stage_pv.at[_stage_begin(stage_pv, sems_pv, dz_hbm, g, where_pv)]
        dpg_ref = stage_pg.at[_stage_begin(stage_pg, sems_pg, dz_hbm, g, where_pg)]
        pos = lax.broadcasted_iota(jnp.int32, (seq, GROUP_DIM), 0)
        pm, count = _pool_mean_minus_token(pv_ref[...], g, pos)
        lin0 = _dot(pm, w_ref[...])
        pg = pg_ref[...]
        spg = _sigmoid(pg)
        da = da_ref[...]
        dlin = da * (pg * spg)
        dpg_ref[...] = (da * (lin0 * sc_ref[...]) * _dsilu(pg, spg)).astype(dpg_ref.dtype)
        dsc_ref[...] = jnp.sum(dlin * lin0, axis=0, keepdims=True)
        dl0 = dlin * sc_ref[...]
        dw_ref[...] = _dot_tn(pm, dl0)
        dpm = _dot_nt(dl0, w_ref[...])
        sums, acc = [], dpm / count
        for j in (1, 2, 4, 8):
            acc = acc + _shift_up(acc, j, pos, seq)
            sums.append(acc)
        dpv_ref[...] = (_select_window(g, sums) - dpm).astype(dpv_ref.dtype)
        _stage_end(stage_pv, sems_pv, dz_hbm, g, POOL_GROUPS, where_pv)
        _stage_end(stage_pg, sems_pg, dz_hbm, g, POOL_GROUPS, where_pg)

    grp = pl.BlockSpec((seq, GROUP_DIM), lambda g: (0, g))
    hbm = pl.BlockSpec(memory_space=pl.ANY)
    stage = pltpu.VMEM((2, seq, GROUP_DIM), MXU_DTYPE)
    return _pallas_after(
        body, 6, after, name=name, grid=(POOL_GROUPS,),
        in_specs=[grp, grp, pl.BlockSpec((seq, GROUP_DIM), lambda g: (0, POOL_GROUPS + g)),
                  pl.BlockSpec((None, GROUP_DIM, GROUP_DIM), lambda g: (g, 0, 0)),
                  pl.BlockSpec((1, GROUP_DIM), lambda g: (0, g)), hbm],
        out_specs=[hbm, pl.BlockSpec((None, GROUP_DIM, GROUP_DIM), lambda g: (g, 0, 0)),
                   pl.BlockSpec((1, GROUP_DIM), lambda g: (0, g))],
        out_shape=[jax.ShapeDtypeStruct(dz.shape, dz.dtype),
                   jax.ShapeDtypeStruct((POOL_GROUPS, GROUP_DIM, GROUP_DIM), F32),
                   jax.ShapeDtypeStruct((1, POOL_WIDTH), F32)],
        input_output_aliases={5: 0},
        scratch_shapes=[stage, stage, pltpu.SemaphoreType.DMA((2,)), pltpu.SemaphoreType.DMA((2,))],
        compiler_params=_params(dimension_semantics=("arbitrary",)),
    )(da_in, z, z, pool_w_l, pool_scale_l, dz)


def _in_proj_dw(h, dz, name, after=None):
    seq = h.shape[0]

    def body(h_ref, dz_ref, out_ref):
        out_ref[...] = lax.dot_general(h_ref[...], dz_ref[...], (((0,), (0,)), ((), ())),
                                       preferred_element_type=F32).astype(out_ref.dtype)

    return _pallas_after(
        body, 2, after, name=name, grid=(N_DEV,),
        in_specs=[pl.BlockSpec((seq, D_MODEL), lambda j: (0, 0)), pl.BlockSpec((seq, IN_COLS), lambda j: (0, j))],
        out_specs=pl.BlockSpec((None, D_MODEL, IN_COLS), lambda j: (j, 0, 0)),
        out_shape=jax.ShapeDtypeStruct((N_DEV, D_MODEL, IN_COLS), WIRE_DTYPE),
        compiler_params=_params(dimension_semantics=("parallel",)),
    )(h, dz)


def _in_proj_dh(dz, win_g, tm, name, after=None):
    seq = dz.shape[0]

    def body(dz_ref, w_ref, dh_ref):
        @pl.when(pl.program_id(1) == 0)
        def _():
            dh_ref[...] = jnp.zeros_like(dh_ref)

        dh_ref[...] += lax.dot_general(dz_ref[...], w_ref[...], (((1,), (1,)), ((), ())),
                                       preferred_element_type=F32)

    return _pallas_after(
        body, 2, after, name=name, grid=(seq // tm, N_DEV),
        in_specs=[pl.BlockSpec((tm, IN_COLS), lambda i, j: (i, j)),
                  pl.BlockSpec((None, D_MODEL, IN_COLS), lambda i, j: (j, 0, 0))],
        out_specs=pl.BlockSpec((tm, D_MODEL), lambda i, j: (i, 0)),
        out_shape=jax.ShapeDtypeStruct((seq, D_MODEL), F32),
        compiler_params=_params(dimension_semantics=("parallel", "arbitrary")),
    )(dz, win_g)


def _prenorm_bwd(x, dh, dx_res, g, scale, tm, name, after=None):
    seq = x.shape[0]

    def body(x_ref, dh_ref, dxr_ref, g_ref, sc_ref, dx_ref, acc_ref):
        @pl.when(pl.program_id(0) == 0)
        def _():
            acc_ref[...] = jnp.zeros_like(acc_ref)

        rs, xn = _rms_parts(x_ref[...])
        dh = dh_ref[...]
        acc_ref[0:1, :] += jnp.sum(dh, axis=0, keepdims=True)
        acc_ref[1:2, :] += jnp.sum(dh * (xn * g_ref[...]), axis=0, keepdims=True)
        dhn = dh * (1.0 + sc_ref[...])
        acc_ref[2:3, :] += jnp.sum(dhn * xn, axis=0, keepdims=True)
        dxn = dhn * g_ref[...]
        dx_ref[...] = rs * (dxn - xn * jnp.mean(dxn * xn, axis=-1, keepdims=True)) + dxr_ref[...]

    tile = pl.BlockSpec((tm, D_MODEL), lambda i: (i, 0))
    return _pallas_after(
        body, 5, after, name=name, grid=(seq // tm,),
        in_specs=[tile, tile, tile, _row_spec(), _row_spec()],
        out_specs=[tile, pl.BlockSpec((8, D_MODEL), lambda i: (0, 0))],
        out_shape=[jax.ShapeDtypeStruct((seq, D_MODEL), F32), jax.ShapeDtypeStruct((8, D_MODEL), F32)],
        compiler_params=_params(dimension_semantics=("arbitrary",)),
    )(x, dh, dx_res, g, scale)


def _adamw_math(w, g, m, v):
    m = ADAM_B1 * m + (1.0 - ADAM_B1) * g
    v = ADAM_B2 * v + (1.0 - ADAM_B2) * (g * g)
    m_hat = m / (1.0 - ADAM_B1 ** ADAM_STEP)
    v_hat = v / (1.0 - ADAM_B2 ** ADAM_STEP)
    delta = -ADAM_LR * (m_hat / (jnp.sqrt(v_hat) + ADAM_EPS) + ADAM_WD * w)
    return delta, m, v


def _adamw_sharded(w, m, v, contrib, tr, name):
    depth, rows, cols = w.shape
    n_parts = contrib.shape[1]

    def body(w_ref, m_ref, v_ref, c_ref, g_ref, d_ref, mo_ref, vo_ref):
        g = c_ref[0].astype(F32)
        for p in range(1, n_parts):
            g += c_ref[p].astype(F32)
        delta, mn, vn = _adamw_math(w_ref[...], g, m_ref[...], v_ref[...])
        g_ref[...] = g
        d_ref[...] = delta
        mo_ref[...] = mn
        vo_ref[...] = vn

    tile = pl.BlockSpec((None, tr, cols), lambda l, i: (l, i, 0))
    shape = jax.ShapeDtypeStruct(w.shape, F32)
    return pl.pallas_call(
        body, name=name, grid=(depth, rows // tr),
        in_specs=[tile, tile, tile, pl.BlockSpec((None, n_parts, tr, cols), lambda l, i: (l, 0, i, 0))],
        out_specs=[tile] * 4, out_shape=[shape] * 4,
        compiler_params=_params(dimension_semantics=("parallel", "parallel")),
    )(w, m, v, contrib)


def _adamw_layer(w, m, v, contribs, l, tr, name, prev=None):
    _, rows, cols = w.shape
    n = len(contribs)

    def body(*refs):
        w_ref, m_ref, v_ref = refs[:3]
        c_refs = refs[3:3 + n]
        g_ref, d_ref, mo_ref, vo_ref = refs[-4:]
        g = c_refs[0][...].astype(F32)
        for c_ref in c_refs[1:]:
            g += c_ref[...].astype(F32)
        delta, mn, vn = _adamw_math(w_ref[...], g, m_ref[...], v_ref[...])
        g_ref[...] = g
        d_ref[...] = delta
        mo_ref[...] = mn
        vo_ref[...] = vn

    tile = pl.BlockSpec((None, tr, cols), lambda i: (l, i, 0))
    in_specs = [tile, tile, tile] + [pl.BlockSpec((None, tr, cols), lambda i, s=slot: (s, i, 0)) for _, slot in contribs]
    operands = [w, m, v] + [arr for arr, _ in contribs]
    aliases = {}
    if prev is not None:
        aliases = {len(operands) + k: k for k in range(4)}
        in_specs += [pl.BlockSpec(memory_space=pl.ANY)] * 4
        operands += list(prev)
    shape = jax.ShapeDtypeStruct(w.shape, F32)
    return pl.pallas_call(
        body, name=name, grid=(rows // tr,), in_specs=in_specs, out_specs=[tile] * 4, out_shape=[shape] * 4,
        input_output_aliases=aliases,
        compiler_params=_params(dimension_semantics=("parallel",)),
    )(*operands)


def _adamw_small(w_pack, m_pack, v_pack, g_late, g_early, shapes):
    pieces, r = {}, 0
    for name, _, n in _SMALL_ROWS:
        pieces.setdefault(name, []).append((r, n))
        r += n
    names = list(pieces)

    def body(w_ref, m_ref, v_ref, gl_ref, ge_ref, *rest):
        outs, packs = rest[:4 * len(names)], rest[4 * len(names):]
        g_l, g_e = gl_ref[0][0:SMALL_LATE_ROWS], ge_ref[0]
        for d in range(1, N_DEV):
            g_l += gl_ref[d][0:SMALL_LATE_ROWS]
            g_e += ge_ref[d]
        g = jnp.concatenate([g_l, g_e], axis=0)
        w = w_ref[...]
        r0, r1, r2 = LB_ROW0, LB_ROW0 + 8, LB_ROW0 + 16
        lg0, lg1 = w[r0:r1], w[r1:r2]
        mx = jnp.maximum(lg0, lg1)
        e0, e1 = jnp.exp(lg0 - mx), jnp.exp(lg1 - mx)
        p0, p1 = e0 / (e0 + e1), e1 / (e0 + e1)
        low = ((p0 - p0), (p0 + p1) - p0)
        dlow = [g_rows * jnp.where((lo > 0.0) & (lo < 1.0), 1.0, jnp.where((lo == 0.0) | (lo == 1.0), 0.5, 0.0))
                for g_rows, lo in ((g[r0:r1], low[0]), (g[r1:r2], low[1]))]
        dp0 = (dlow[0] + dlow[1]) - (dlow[0] + dlow[1])
        dp1 = dlow[1]
        inner = p0 * dp0 + p1 * dp1
        g = jnp.concatenate([g[:r0], p0 * (dp0 - inner), p1 * (dp1 - inner), g[r2:]], axis=0)
        delta, mn, vn = _adamw_math(w, g, m_ref[...], v_ref[...])
        for kind, val in enumerate((g, delta, mn, vn)):
            packs[kind][...] = val
            for j, name in enumerate(names):
                out, at = outs[kind * len(names) + j], 0
                for start, n in pieces[name]:
                    if name in flat:
                        for r in range(n):
                            layer, c = divmod(at + r, flat[name])
                            out[layer:layer + 1, c * 128:(c + 1) * 128] = packs[kind][start + r:start + r + 1, :]
                    else:
                        out[at:at + n, :] = packs[kind][start:start + n, :]
                    at += n

    rows = {name: sum(n for _, n in pieces[name]) for name in names}
    flat = {name: rows[name] // DEPTH for name in names if len(shapes[name]) == 2}
    outs = pl.pallas_call(
        body, name="adamw_small",
        out_shape=[jax.ShapeDtypeStruct(shapes[name] if name in flat else (rows[name], 128), F32)
                   for _ in range(4) for name in names],
        scratch_shapes=[pltpu.VMEM(w_pack.shape, F32)] * 4, compiler_params=_params(),
    )(w_pack, m_pack, v_pack, g_late, g_early)
    return [{name: outs[kind * len(names) + j].reshape(shapes[name]) for j, name in enumerate(names)}
            for kind in range(4)]


def _pack_small(parts, first=0, last=len(_SMALL_ROWS)):
    rows = [(parts[name] if l is None else parts[name][l]).reshape(n, 128) for name, l, n in _SMALL_ROWS[first:last]]
    if last == len(_SMALL_ROWS):
        rows.append(jnp.zeros((SMALL_ROWS_PAD - sum(n for _, _, n in _SMALL_ROWS), 128), F32))
    return jnp.concatenate(rows, axis=0)


def kernel(x, c, w_ada, b_ada, g_pre, g_post, w_in, pool_w, pool_scale, lb_logits, hgrn_norm_g, w_pool_o, w_hgrn_o, w_out, loss_target, m_w_ada, m_b_ada, m_g_pre, m_g_post, m_w_in, m_pool_w, m_pool_scale, m_lb_logits, m_hgrn_norm_g, m_w_pool_o, m_w_hgrn_o, m_w_out, v_w_ada, v_b_ada, v_g_pre, v_g_post, v_w_in, v_pool_w, v_pool_scale, v_lb_logits, v_hgrn_norm_g, v_w_pool_o, v_w_hgrn_o, v_w_out):
    seq = x.shape[1]
    tm = min(512, seq)
    tm_merge = min(256, seq)
    pos = _my_position()
    me = pos[3]

    c_all = _allgather_small(c, "allgather_c").reshape(N_DEV, D_MODEL)
    b_cols = lax.dynamic_slice_in_dim(b_ada, me * ADA_COLS, ADA_COLS, axis=1)
    ada_part = _ada_fwd(c_all, w_ada, b_cols)
    ada_all = _allgather_small(ada_part.reshape(DEPTH * N_DEV, ADA_COLS), "allgather_ada")
    ada = lax.dynamic_index_in_dim(ada_all.reshape(N_DEV, DEPTH, N_DEV, ADA_COLS), me, axis=2, keepdims=False)
    ada = jnp.transpose(ada, (1, 0, 2)).reshape(DEPTH, 3 * D_MODEL)
    shift = [ada[l:l + 1, 0:D_MODEL] for l in range(DEPTH)]
    scale = [ada[l:l + 1, D_MODEL:2 * D_MODEL] for l in range(DEPTH)]
    gate = [ada[l:l + 1, 2 * D_MODEL:] for l in range(DEPTH)]

    big = dict(win=w_in, wpo=w_pool_o, who=w_hgrn_o, wout=w_out)
    units = [["win0"], ["wpo0", "who0", "wout0"], ["win1", "wpo1", "who1", "wout1"]]
    g_streams = [_gather_streams(keys) for keys in units]
    g_state = [None] * len(units)

    def gather_start(u, after):
        bufs = {}
        for k in units[u]:
            arr = big[k[:-1]]
            bufs["s_" + k] = arr[int(k[-1])].astype(WIRE_DTYPE)
            bufs["g_" + k] = _with_own_slot(bufs["s_" + k], me)
        bufs, sems, token = _comm_call(f"gather_start_{u}", bufs, start=list(g_streams[u][:2]), after=after)
        g_state[u] = dict(bufs=bufs, sems=sems)
        return token

    def gather_pass(u, after):
        st = g_state[u]
        to_chips, _, pass_on = g_streams[u]
        st["bufs"], (st["pass_sems"],), _ = _comm_call(f"gather_pass_{u}", st["bufs"], start=[pass_on],
                                                       wait=[(to_chips, st["sems"][0])], after=after)

    def gather_done(u, after=None):
        st = g_state[u]
        _, to_sibling, pass_on = g_streams[u]
        bufs, _, _ = _comm_call(f"gather_done_{u}", st["bufs"], after=after,
                                wait=[(to_sibling, st["sems"][1]), (pass_on, st["pass_sems"])])
        return {k: bufs["g_" + k] for k in units[u]}

    token = gather_start(0, ada_all)

    lb = _lb_fwd(lb_logits)

    gw = {}
    xs, saved = [x[0]], []
    for l in range(DEPTH):
        h = _prenorm_fwd(xs[l], g_pre[l:l + 1], shift[l], scale[l], tm, f"prenorm_fwd_{l}",
                         after=token if l == 0 else None)
        token = None
        if l == 0:
            gather_pass(0, h)
            gw.update(gather_done(0))
            token = gather_start(1, gw["win0"])
        else:
            gather_pass(2, h)
            gw.update(gather_done(2))
        z = _in_proj(h, gw[f"win{l}"], seq, f"in_proj_{l}", after=token)
        if l == 0:
            gather_pass(1, z)
            token = gather_start(2, g_state[1]["bufs"]["g_wpo0"])
        a_in = _pool_fwd(z, pool_w[l], pool_scale[l:l + 1], f"pool_fwd_{l}", after=token)
        o, b_in, states = _hgrn_fwd(z, lb[l:l + 1], hgrn_norm_g[l:l + 1], f"hgrn_fwd_{l}", after=token)
        if l == 0:
            gw.update(gather_done(1, b_in))
        who_l = gw[f"who{l}"].reshape(D_MODEL, D_MODEL)
        wout_l = gw[f"wout{l}"].reshape(D_MODEL, D_MODEL)
        ba, bb, merged, y, x_next = _merge_fwd(a_in, b_in, z, xs[l], gw[f"wpo{l}"], who_l, wout_l, gate[l],
                                               g_post[l:l + 1], tm_merge, f"merge_fwd_{l}")
        xs.append(x_next)
        saved.append((h, z, a_in, o, b_in, states, ba, bb, merged, y, who_l, wout_l))

    loss_part, dx = _loss_grad(xs[DEPTH], loss_target[0], tm)

    chips = _other_chips(pos)
    pair_idx = jnp.stack([_dev_index(cx, cy, pos[2]) for cx, cy in chips] + [me]).astype(jnp.int32)
    pair_rows = dict(win=256, wpo=POOL_WIDTH, who=HEAD_DIM, wout=HEAD_DIM)

    def scatter_pair_start(u, grads):
        keys = list(grads)
        pair, to_chips = _scatter_streams(keys)
        bufs = {}
        for k in keys:
            bufs["g_" + k] = grads[k]
            bufs["st_" + k] = lax.empty((4,) + grads[k].shape[1:], WIRE_DTYPE)
        bufs, (sems,), token = _comm_call(f"scatter_pair_start_{u}", bufs, start=[pair])
        return dict(u=u, keys=keys, pair=pair, to_chips=to_chips, bufs=bufs, sems=sems, token=token)

    def scatter_pair_finish(st, after):
        u, keys = st["u"], st["keys"]
        bufs, _, _ = _comm_call(f"scatter_pair_done_{u}", st["bufs"], wait=[(st["pair"], st["sems"])], after=after)
        bufs2 = {}
        for k in keys:
            bufs2["ps_" + k] = _pair_sum(bufs["g_" + k], bufs["st_" + k], pair_idx, bufs["g_" + k].shape[1],
                                         f"pair_sum_{k}")
            bufs2["ld_" + k] = lax.empty((3,) + bufs["g_" + k].shape[1:], WIRE_DTYPE)
        st.update(bufs=bufs2)

    def scatter_chips_start(st, after=None):
        bufs2, (sems,), token = _comm_call(f"scatter_chips_start_{st['u']}", st["bufs"], start=[st["to_chips"]],
                                           after=after)
        st.update(bufs=bufs2, sems=sems, token=token)

    def scatter_finish(st, after):
        bufs, _, _ = _comm_call(f"scatter_chips_done_{st['u']}", st["bufs"], wait=[(st["to_chips"], st["sems"])],
                                after=after)
        return {k: [(bufs["ps_" + k], 3), (bufs["ld_" + k], 0), (bufs["ld_" + k], 1), (bufs["ld_" + k], 2)]
                for k in st["keys"]}

    moments = dict(win=(m_w_in, v_w_in), wpo=(m_w_pool_o, v_w_pool_o), who=(m_w_hgrn_o, v_w_hgrn_o),
                   wout=(m_w_out, v_w_out))
    big_out = {}

    def finish_unit(unit, after):
        for k, contribs in scatter_finish(scat[unit], after).items():
            wname, l = k[:-1], int(k[-1])
            big_out[wname] = _adamw_layer(big[wname], moments[wname][0], moments[wname][1], contribs, l,
                                          pair_rows[wname], f"adamw_{k}", prev=big_out.get(wname))
            after = big_out[wname][0]
        return after

    d_ada, small, scat = [None] * DEPTH, [None] * DEPTH, {}
    for l in reversed(range(DEPTH)):
        h, z, a_in, o, b_in, states, ba, bb, merged, y, who_l, wout_l = saved[l]
        dy, dba, dbb, da_in, db_in, dz, acc_post = _merge_bwd(
            dx, y, ba, bb, z, gw[f"wpo{l}"], who_l, wout_l, gate[l], g_post[l:l + 1],
            lax.empty((seq, IN_WIDTH), MXU_DTYPE), tm_merge, f"merge_bwd_{l}")
        g_out, g_ho, g_po = _grad_out_weights(merged, dy, b_in, dbb, a_in, dba, f"grad_out_weights_{l}")
        g_small = {f"wout{l}": g_out.reshape(N_DEV, HEAD_DIM, D_MODEL),
                   f"who{l}": g_ho.reshape(N_DEV, HEAD_DIM, D_MODEL), f"wpo{l}": g_po}
        st_small = scat["small0"] = scatter_pair_start("small0", g_small) if l == 0 else None
        dz, dlb, dgn = _hgrn_bwd(db_in, z, o, states, lb[l:l + 1], hgrn_norm_g[l:l + 1], dz, f"hgrn_bwd_{l}",
                                 after=st_small and st_small["token"])
        if l == 0:
            scatter_pair_finish(st_small, dlb)
            scatter_chips_start(st_small)
        dz, dpw, dps = _pool_bwd(da_in, z, pool_w[l], pool_scale[l:l + 1], dz, f"pool_bwd_{l}",
                                 after=st_small and st_small["token"])
        small[l] = dict(g_post=acc_post[1], pool_w=dpw, pool_scale=dps[0], lb_logits=dlb[0], hgrn_norm_g=dgn[0])
        token = None
        if l == 0:
            parts = {name: jnp.stack([small[0][name], small[1][name]]) for name in small[0]}
            parts.update(b_ada=[None, d_ada[1]], g_pre=[None, small[1]["g_pre"]])
            sg_stream = _direct_gather_stream("sg")
            early = _pack_small(parts, 2)
            sg_bufs, (sg_sems,), token = _comm_call(
                "small_grads_start", dict(s_sg=early, g_sg=_with_own_slot(early, me)), start=[sg_stream])
        g_win = {f"win{l}": _in_proj_dw(h, dz, f"grad_w_in_{l}", after=token)}
        st_win = scat[f"win{l}"] = scatter_pair_start(f"win{l}", g_win if l == 0 else {**g_small, **g_win})
        if l > 0:
            dh = _in_proj_dh(dz, gw[f"win{l}"], seq, f"in_proj_dh_{l}", after=st_win["token"])
            scatter_pair_finish(st_win, dh)
            scatter_chips_start(st_win)
        else:
            scatter_pair_finish(st_win, st_win["token"])
            scatter_chips_start(st_win)
            after = st_win["token"]
            for unit in ("win1", "small0"):
                after = finish_unit(unit, after)
            dh = _in_proj_dh(dz, gw[f"win{l}"], seq, f"in_proj_dh_{l}", after=after)
        dx, acc_pre = _prenorm_bwd(xs[l], dh, dx, g_pre[l:l + 1], scale[l], tm, f"prenorm_bwd_{l}",
                                   after=st_win["token"])
        d_ada[l] = jnp.concatenate([acc_pre[0], acc_pre[1], acc_post[0]])
        small[l]["g_pre"] = acc_pre[2]
    grad_x = dx[None]

    parts = dict(b_ada=[d_ada[0]], g_pre=[small[0]["g_pre"]])
    late = jnp.concatenate([_pack_small(parts, 0, 2), jnp.broadcast_to(loss_part, (8, 128))], axis=0)
    g_late = _allgather_small(late, "allgather_late_grads")
    loss = jnp.sum(g_late[:, SMALL_LATE_ROWS, 0])
    sg_bufs, _, _ = _comm_call("small_grads_done", sg_bufs, wait=[(sg_stream, sg_sems)], after=g_late)
    g_early = sg_bufs["g_sg"]
    small_names = list(dict.fromkeys(name for name, _, _ in _SMALL_ROWS))
    weights = dict(b_ada=b_ada, g_pre=g_pre, g_post=g_post, pool_w=pool_w, pool_scale=pool_scale,
                   lb_logits=lb_logits, hgrn_norm_g=hgrn_norm_g)
    m_small = dict(b_ada=m_b_ada, g_pre=m_g_pre, g_post=m_g_post, pool_w=m_pool_w, pool_scale=m_pool_scale,
                   lb_logits=m_lb_logits, hgrn_norm_g=m_hgrn_norm_g)
    v_small = dict(b_ada=v_b_ada, g_pre=v_g_pre, g_post=v_g_post, pool_w=v_pool_w, pool_scale=v_pool_scale,
                   lb_logits=v_lb_logits, hgrn_norm_g=v_hgrn_norm_g)
    shapes = {name: weights[name].shape for name in small_names}
    small_out = _adamw_small(_pack_small(weights), _pack_small(m_small), _pack_small(v_small), g_late, g_early,
                             shapes)

    d_ada_all = jnp.stack([g_late[:, 0:24, :].reshape(N_DEV, 3 * D_MODEL),
                           g_early[:, 0:24, :].reshape(N_DEV, 3 * D_MODEL)], axis=1)
    d_cols = jnp.transpose(lax.dynamic_slice_in_dim(d_ada_all, me * ADA_COLS, ADA_COLS, axis=2), (1, 0, 2))
    g_w_ada = _ada_bwd(c_all, d_cols)
    ada_out = _adamw_sharded(w_ada, m_w_ada, v_w_ada, g_w_ada[:, None], 256, "adamw_w_ada")
    finish_unit("win0", ada_out[1][0, 0:8, 0:128] + small_out[1]["pool_scale"][0:1, 0:128])

    def leaf(kind):
        s = small_out[kind]
        return (ada_out[kind], s["b_ada"], s["g_pre"], s["g_post"], big_out["win"][kind], s["pool_w"], s["pool_scale"],
                s["lb_logits"], s["hgrn_norm_g"], big_out["wpo"][kind], big_out["who"][kind], big_out["wout"][kind])

    return (loss, grad_x) + leaf(0) + leaf(1) + leaf(2) + leaf(3)
```

```python
import jax
import jax.numpy as jnp
from jax import lax
from jax.experimental import pallas as pl
from jax.experimental.pallas import tpu as pltpu

F32 = jnp.float32
MXU_DTYPE = jnp.bfloat16
WIRE_DTYPE = jnp.bfloat16

N_DEV = 8
DEPTH = 2
D_MODEL = 1024
HEADS = 8
HEAD_DIM = 128
POOL_GROUPS = 4
GROUP_DIM = 128
POOL_WIDTH = POOL_GROUPS * GROUP_DIM
IN_WIDTH = 7168
CHUNK = 64
SUB = 16
N_SUB = CHUNK // SUB
FWD_STEP_CHUNKS = 8
BWD_STEP_CHUNKS = 4
EXP_CLAMP = 80.0
NORM_EPS = 1e-6
LOG_FLOOR = 1e-30
ADA_COLS = 3 * D_MODEL // N_DEV
IN_COLS = IN_WIDTH // N_DEV
COL_HQ, COL_HF, COL_HI, COL_HG, COL_MGP, COL_MGH = 1, 2, 3, 4, 5, 6

ADAM_LR = 0.001
ADAM_B1 = 0.9
ADAM_B2 = 0.999
ADAM_EPS = 1e-08
ADAM_WD = 0.01
ADAM_STEP = 10

VMEM_LIMIT = 48 * 1024 * 1024
MESH_ID = pl.DeviceIdType.MESH
HIGHEST = lax.Precision.HIGHEST

_SMALL_ROWS = (("b_ada", 0, 24), ("g_pre", 0, 8), ("b_ada", 1, 24), ("g_pre", 1, 8), ("g_post", None, 16),
               ("pool_w", None, 1024), ("pool_scale", None, 8), ("lb_logits", None, 16), ("hgrn_norm_g", None, 2))
SMALL_LATE_ROWS = 32
SMALL_ROWS_PAD = 1136
LB_ROW0 = 32 + 32 + 16 + 1024 + 8


def _params(**kw):
    return pltpu.CompilerParams(vmem_limit_bytes=VMEM_LIMIT, **kw)


def _sigmoid(v):
    return 1.0 / (1.0 + jnp.exp(-v))


def _dsilu(v, s):
    return s * (1.0 + v * (1.0 - s))


def _dot(a, b):
    return jnp.dot(a.astype(MXU_DTYPE), b.astype(MXU_DTYPE), preferred_element_type=F32)


def _dot_nt(a, b):
    return lax.dot_general(a.astype(MXU_DTYPE), b.astype(MXU_DTYPE), (((1,), (1,)), ((), ())),
                           preferred_element_type=F32)


def _dot_tn(a, b):
    return lax.dot_general(a.astype(MXU_DTYPE), b.astype(MXU_DTYPE), (((0,), (0,)), ((), ())),
                           preferred_element_type=F32)


def _pallas_after(body, n_in, after, *, in_specs, **kw):
    if after is None:
        return pl.pallas_call(body, in_specs=in_specs, **kw)

    def tied(*refs):
        body(*refs[:n_in], *refs[n_in + 1:])

    call = pl.pallas_call(tied, in_specs=list(in_specs) + [pl.BlockSpec(memory_space=pl.ANY)], **kw)
    return lambda *operands: call(*operands, after)


def _my_position():
    mx, my, mc = lax.axis_index("x"), lax.axis_index("y"), lax.axis_index("c")
    return mx, my, mc, 4 * mx + 2 * my + mc


def _peer(mx, my, mc, k):
    px = 1 - mx if (k >> 2) & 1 else mx
    py = 1 - my if (k >> 1) & 1 else my
    pc = 1 - mc if k & 1 else mc
    return (px, py, pc), 4 * px + 2 * py + pc


def _allgather_small(v, name, after=None):
    rows, cols = v.shape

    def body(v_ref, out_ref, send_sems, recv_sems):
        mx, my, mc, me = _my_position()
        out_ref[me] = v_ref[...]
        copies = []
        for k in range(1, N_DEV):
            peer, _ = _peer(mx, my, mc, k)
            cp = pltpu.make_async_remote_copy(
                src_ref=v_ref, dst_ref=out_ref.at[me],
                send_sem=send_sems.at[k - 1], recv_sem=recv_sems.at[k - 1],
                device_id=peer, device_id_type=MESH_ID)
            cp.start()
            copies.append(cp)
        for cp in copies:
            cp.wait()

    return _pallas_after(
        body, 1, after, name=name,
        out_shape=jax.ShapeDtypeStruct((N_DEV, rows, cols), v.dtype),
        in_specs=[pl.BlockSpec(memory_space=pltpu.VMEM)],
        out_specs=pl.BlockSpec(memory_space=pltpu.VMEM),
        scratch_shapes=[pltpu.SemaphoreType.DMA((N_DEV - 1,)), pltpu.SemaphoreType.DMA((N_DEV - 1,))],
        compiler_params=_params(),
    )(v)


class _Stream:
    def __init__(self, n, plan):
        self.n, self.plan = n, plan


def _comm_call(name, bufs, start=(), wait=(), after=None):
    names = list(bufs)

    def body(*refs):
        it = iter(refs)
        buf_refs = {n: next(it) for n in names}
        wait_sems = [(next(it), next(it)) for _ in wait]
        if after is not None:
            next(it)
        start_sems = [(next(it), next(it)) for _ in start]
        for _ in names:
            next(it)
        token = next(it)
        pos = _my_position()

        def descriptors(stream, sems):
            return [pltpu.make_async_remote_copy(src_ref=src, dst_ref=dst, send_sem=sems[0].at[k], recv_sem=sems[1].at[k],
                                                 device_id=dev, device_id_type=MESH_ID)
                    for k, (src, dst, dev) in enumerate(stream.plan(buf_refs, pos))]

        for (stream, _), sems in zip(wait, wait_sems):
            for cp in descriptors(stream, sems):
                cp.wait_send()
                cp.wait_recv()
        for stream, sems in zip(start, start_sems):
            for cp in descriptors(stream, sems):
                cp.start()
        token[...] = jnp.zeros_like(token)

    hbm = pl.BlockSpec(memory_space=pltpu.HBM)
    sem = pl.BlockSpec(memory_space=pltpu.SEMAPHORE)
    operands = [pltpu.with_memory_space_constraint(bufs[n], pltpu.HBM) for n in names]
    in_specs = [hbm] * len(names)
    for _, (send_sems, recv_sems) in wait:
        operands += [send_sems, recv_sems]
        in_specs += [sem, sem]
    if after is not None:
        operands.append(after)
        in_specs.append(pl.BlockSpec(memory_space=pl.ANY))
    out_shape, out_specs = [], []
    for stream in start:
        out_shape += [pltpu.SemaphoreType.DMA((stream.n,)), pltpu.SemaphoreType.DMA((stream.n,))]
        out_specs += [sem, sem]
    n_sem_out = len(out_shape)
    out_shape += [pltpu.HBM(bufs[n].shape, bufs[n].dtype) for n in names]
    out_specs += [hbm] * len(names)
    out_shape.append(jax.ShapeDtypeStruct((8, 128), F32))
    out_specs.append(pl.BlockSpec(memory_space=pltpu.VMEM))
    outs = pl.pallas_call(
        body, name=name, out_shape=out_shape, in_specs=in_specs, out_specs=out_specs,
        input_output_aliases={i: n_sem_out + i for i in range(len(names))},
        compiler_params=pltpu.CompilerParams(has_side_effects=pltpu.SideEffectType.DATAFLOW_SIDE_EFFECTING),
    )(*operands)
    sems = [(outs[2 * i], outs[2 * i + 1]) for i in range(len(start))]
    return dict(zip(names, outs[n_sem_out:n_sem_out + len(names)])), sems, outs[-1]


def _with_own_slot(block, me):
    return lax.dynamic_update_index_in_dim(lax.empty((N_DEV,) + block.shape, block.dtype), block, me, 0)


def _other_chips(pos):
    mx, my, _, _ = pos
    return [(1 - mx if i & 2 else mx, 1 - my if i & 1 else my) for i in (1, 2, 3)]


def _dev_index(px, py, pc):
    return 4 * px + 2 * py + pc


def _gather_streams(keys):
    def to_chips(refs, pos):
        _, _, mc, me = pos
        return [(refs["s_" + k], refs["g_" + k].at[me], (cx, cy, mc)) for k in keys for cx, cy in _other_chips(pos)]

    def to_sibling(refs, pos):
        mx, my, mc, me = pos
        return [(refs["s_" + k], refs["g_" + k].at[me], (mx, my, 1 - mc)) for k in keys]

    def pass_on(refs, pos):
        mx, my, mc, _ = pos
        out = []
        for k in keys:
            for cx, cy in _other_chips(pos):
                slot = refs["g_" + k].at[_dev_index(cx, cy, mc)]
                out.append((slot, slot, (mx, my, 1 - mc)))
        return out

    return _Stream(3 * len(keys), to_chips), _Stream(len(keys), to_sibling), _Stream(3 * len(keys), pass_on)


def _direct_gather_stream(key):
    def plan(refs, pos):
        mx, my, mc, me = pos
        return [(refs["s_" + key], refs["g_" + key].at[me], _peer(mx, my, mc, k)[0]) for k in range(1, N_DEV)]

    return _Stream(N_DEV - 1, plan)


def _scatter_streams(keys):
    def pair(refs, pos):
        mx, my, mc, _ = pos
        sib = (mx, my, 1 - mc)
        out = []
        for k in keys:
            for i, (cx, cy) in enumerate(_other_chips(pos)):
                out.append((refs["g_" + k].at[_dev_index(cx, cy, 1 - mc)], refs["st_" + k].at[i], sib))
            out.append((refs["g_" + k].at[_dev_index(mx, my, 1 - mc)], refs["st_" + k].at[3], sib))
        return out

    def chips(refs, pos):
        mc = pos[2]
        return [(refs["ps_" + k].at[i], refs["ld_" + k].at[i], (cx, cy, mc))
                for k in keys for i, (cx, cy) in enumerate(_other_chips(pos))]

    return _Stream(4 * len(keys), pair), _Stream(3 * len(keys), chips)


def _pair_sum(g, st, idx, tr, name):
    _, rows, cols = g.shape

    def body(idx_ref, g_ref, st_ref, out_ref):
        out_ref[...] = (g_ref[...].astype(F32) + st_ref[...].astype(F32)).astype(out_ref.dtype)

    return pl.pallas_call(
        body, name=name,
        grid_spec=pltpu.PrefetchScalarGridSpec(
            num_scalar_prefetch=1, grid=(4, rows // tr),
            in_specs=[pl.BlockSpec((None, tr, cols), lambda j, i, idx_ref: (idx_ref[j], i, 0)),
                      pl.BlockSpec((None, tr, cols), lambda j, i, idx_ref: (j, i, 0))],
            out_specs=pl.BlockSpec((None, tr, cols), lambda j, i, idx_ref: (j, i, 0))),
        out_shape=jax.ShapeDtypeStruct((4, rows, cols), WIRE_DTYPE),
        compiler_params=_params(dimension_semantics=("parallel", "parallel")),
    )(idx, g, st)


def _ada_fwd(c_all, w_ada, b_cols):
    def body(c_ref, w_ref, b_ref, out_ref):
        cv = c_ref[...]
        ca = cv * _sigmoid(cv)
        for l in range(DEPTH):
            out_ref[l] = jnp.dot(ca, w_ref[l], precision=HIGHEST, preferred_element_type=F32) + b_ref[l:l + 1, :]

    return pl.pallas_call(
        body, name="ada_fwd",
        out_shape=jax.ShapeDtypeStruct((DEPTH, N_DEV, ADA_COLS), F32),
        compiler_params=_params(),
    )(c_all, w_ada, b_cols)


def _ada_bwd(c_all, d_cols):
    def body(c_ref, d_ref, out_ref):
        cv = c_ref[...]
        ca = cv * _sigmoid(cv)
        for l in range(DEPTH):
            out_ref[l] = lax.dot_general(ca, d_ref[l], (((0,), (0,)), ((), ())), precision=HIGHEST,
                                         preferred_element_type=F32)

    return pl.pallas_call(
        body, name="ada_bwd",
        out_shape=jax.ShapeDtypeStruct((DEPTH, D_MODEL, ADA_COLS), F32),
        compiler_params=_params(),
    )(c_all, d_cols)


def _lower_bounds(logits):
    m = jnp.maximum(logits[0:1], logits[1:2])
    e0, e1 = jnp.exp(logits[0:1] - m), jnp.exp(logits[1:2] - m)
    den = e0 + e1
    p0, p1 = e0 / den, e1 / den
    low0 = p0 - p0
    low1 = (p0 + p1) - p0
    return (p0, p1), (low0, low1)


def _lb_fwd(lb_logits):
    def body(lg_ref, out_ref):
        _, (low0, low1) = _lower_bounds(lg_ref[...])
        out_ref[0:1, :] = jnp.clip(low0, 0.0, 1.0)
        out_ref[1:2, :] = jnp.clip(low1, 0.0, 1.0)

    return pl.pallas_call(body, name="lb_fwd", out_shape=jax.ShapeDtypeStruct(lb_logits.shape, F32),
                          compiler_params=_params())(lb_logits)


def _row_spec(cols=D_MODEL):
    return pl.BlockSpec((1, cols), lambda *_: (0, 0))


def _prenorm_fwd(x, g, shift, scale, tm, name, after=None):
    seq = x.shape[0]

    def body(x_ref, g_ref, sh_ref, sc_ref, h_ref):
        xv = x_ref[...]
        rs = lax.rsqrt(jnp.mean(xv * xv, axis=-1, keepdims=True) + NORM_EPS)
        h = (xv * rs * g_ref[...]) * (1.0 + sc_ref[...]) + sh_ref[...]
        h_ref[...] = h.astype(h_ref.dtype)

    tile = pl.BlockSpec((tm, D_MODEL), lambda i: (i, 0))
    return _pallas_after(
        body, 4, after, name=name, grid=(seq // tm,),
        in_specs=[tile, _row_spec(), _row_spec(), _row_spec()], out_specs=tile,
        out_shape=jax.ShapeDtypeStruct((seq, D_MODEL), MXU_DTYPE),
        compiler_params=_params(dimension_semantics=("parallel",)),
    )(x, g, shift, scale)


def _in_proj(h, win_g, tm, name, after=None):
    seq = h.shape[0]

    def body(h_ref, w_ref, z_ref):
        z_ref[...] = jnp.dot(h_ref[...], w_ref[...], preferred_element_type=F32)

    return _pallas_after(
        body, 2, after, name=name, grid=(N_DEV, seq // tm),
        in_specs=[pl.BlockSpec((tm, D_MODEL), lambda j, i: (i, 0)),
                  pl.BlockSpec((None, D_MODEL, IN_COLS), lambda j, i: (j, 0, 0))],
        out_specs=pl.BlockSpec((tm, IN_COLS), lambda j, i: (i, j)),
        out_shape=jax.ShapeDtypeStruct((seq, IN_WIDTH), F32),
        compiler_params=_params(dimension_semantics=("parallel", "parallel")),
    )(h, win_g)


def _shift_down(v, j, pos):
    return jnp.where(pos >= j, pltpu.roll(v, j, 0), 0.0)


def _shift_up(v, j, pos, seq):
    return jnp.where(pos < seq - j, pltpu.roll(v, seq - j, 0), 0.0)


def _select_window(g, candidates):
    out = candidates[-1]
    for i in range(len(candidates) - 2, -1, -1):
        out = jnp.where(g == i, candidates[i], out)
    return out


def _pool_mean_minus_token(u, g, pos):
    sums, acc = [], u
    for j in (1, 2, 4, 8):
        acc = acc + _shift_down(acc, j, pos)
        sums.append(acc)
    wsum = _select_window(g, sums)
    width = jnp.left_shift(2, g).astype(F32)
    count = jnp.minimum(pos.astype(F32) + 1.0, width)
    return wsum / count - u, count


def _pool_fwd(z, pool_w_l, pool_scale_l, name, after=None):
    seq = z.shape[0]

    def body(pv_ref, pg_ref, w_ref, sc_ref, out_ref):
        g = pl.program_id(0)
        pos = lax.broadcasted_iota(jnp.int32, (seq, GROUP_DIM), 0)
        pm, _ = _pool_mean_minus_token(pv_ref[...], g, pos)
        lin = _dot(pm, w_ref[...]) * sc_ref[...]
        pg = pg_ref[...]
        out_ref[...] = (lin * (pg * _sigmoid(pg))).astype(out_ref.dtype)

    return _pallas_after(
        body, 4, after, name=name, grid=(POOL_GROUPS,),
        in_specs=[pl.BlockSpec((seq, GROUP_DIM), lambda g: (0, g)),
                  pl.BlockSpec((seq, GROUP_DIM), lambda g: (0, POOL_GROUPS + g)),
                  pl.BlockSpec((None, GROUP_DIM, GROUP_DIM), lambda g: (g, 0, 0)),
                  pl.BlockSpec((1, GROUP_DIM), lambda g: (0, g))],
        out_specs=pl.BlockSpec((seq, GROUP_DIM), lambda g: (0, g)),
        out_shape=jax.ShapeDtypeStruct((seq, POOL_WIDTH), MXU_DTYPE),
        compiler_params=_params(dimension_semantics=("parallel",)),
    )(z, z, pool_w_l, pool_scale_l)


def _chunk_masks():
    row = lax.broadcasted_iota(jnp.int32, (CHUNK, CHUNK), 0)
    col = lax.broadcasted_iota(jnp.int32, (CHUNK, CHUNK), 1)
    causal = row >= col
    before_sub = col < (row // SUB) * SUB
    suffix = row <= col
    return causal, before_sub, suffix


def _masked_sums(masks, v):
    lhs = jnp.concatenate([m.astype(jnp.bfloat16) for m in masks], axis=0)
    hi = v.astype(jnp.bfloat16)
    rest = v - hi.astype(F32)
    mid = rest.astype(jnp.bfloat16)
    lo = (rest - mid.astype(F32)).astype(jnp.bfloat16)
    out = jnp.dot(lhs, hi, preferred_element_type=F32)
    out += jnp.dot(lhs, mid, preferred_element_type=F32)
    out += jnp.dot(lhs, lo, preferred_element_type=F32)
    return [out[i * CHUNK:(i + 1) * CHUNK] for i in range(len(masks))]


def _gates(zf, lb):
    sg = _sigmoid(zf)
    f = lb + (1.0 - lb) * sg
    logf = jnp.log(jnp.maximum(f, LOG_FLOOR))
    return sg, f, logf


def _intra_blocks(q_h, k_h, cum_h, base_h, causal):
    rel = cum_h - base_h
    out = []
    for i in range(N_SUB):
        rows = slice(i * SUB, (i + 1) * SUB)
        e_q = jnp.exp(rel[rows])
        base_i = jnp.concatenate([base_h[rows]] * N_SUB, axis=0)
        e_k = jnp.exp(jnp.minimum(base_i - cum_h, EXP_CLAMP))
        q_t = (q_h[rows] * e_q).astype(MXU_DTYPE)
        k_t = (k_h * e_k).astype(MXU_DTYPE)
        a_i = jnp.where(causal[rows], _dot_nt(q_t, k_t), 0.0)
        out.append((q_t, k_t, e_q, e_k, a_i))
    return out


def _hgrn_fwd(z, lb_l, gn_l, name, after=None):
    seq = z.shape[0]
    n_chunks = seq // CHUNK
    per_step = min(FWD_STEP_CHUNKS, n_chunks)
    rows_per_step = per_step * CHUNK

    def body(hq_ref, hf_ref, hi_ref, hg_ref, lb_ref, gn_ref, o_ref, bin_ref, st_ref, state):
        @pl.when(pl.program_id(0) == 0)
        def _():
            state[...] = jnp.zeros_like(state)

        causal, before_sub, _ = _chunk_masks()
        for cc in range(per_step):
            rows = slice(cc * CHUNK, (cc + 1) * CHUNK)
            _, f, logf = _gates(hf_ref[rows, :], lb_ref[...])
            kk = 1.0 - f
            hq = hq_ref[rows, :]
            q = hq * _sigmoid(hq)
            cum, base = _masked_sums([causal, before_sub], logf)
            st_ref[cc] = state[...]
            for h in range(HEADS):
                sl = slice(h * HEAD_DIM, (h + 1) * HEAD_DIM)
                q_h, k_h, cum_h = q[:, sl], kk[:, sl], cum[:, sl]
                v_h = hi_ref[rows, sl]
                st_h = state[h]
                blocks = _intra_blocks(q_h, k_h, cum_h, base[:, sl], causal)
                a = jnp.concatenate([b[4] for b in blocks], axis=0)
                o_h = _dot_nt(q_h * jnp.exp(cum_h), st_h) + _dot(a, v_h)
                last = jnp.sum(logf[:, sl], axis=0, keepdims=True)
                state[h] = st_h * jnp.exp(last) + _dot_tn(v_h, k_h * jnp.exp(last - cum_h))
                rs = lax.rsqrt(jnp.mean(o_h * o_h, axis=-1, keepdims=True) + NORM_EPS)
                hg = hg_ref[rows, sl]
                o_ref[rows, sl] = o_h
                bin_ref[rows, sl] = ((o_h * rs * gn_ref[...]) * (hg * _sigmoid(hg))).astype(bin_ref.dtype)

    def col(block):
        return pl.BlockSpec((rows_per_step, D_MODEL), lambda c: (c, block))

    tile = pl.BlockSpec((rows_per_step, D_MODEL), lambda c: (c, 0))
    return _pallas_after(
        body, 6, after, name=name, grid=(n_chunks // per_step,),
        in_specs=[col(COL_HQ), col(COL_HF), col(COL_HI), col(COL_HG), _row_spec(), _row_spec(HEAD_DIM)],
        out_specs=[tile, tile, pl.BlockSpec((per_step, HEADS, HEAD_DIM, HEAD_DIM), lambda c: (c, 0, 0, 0))],
        out_shape=[jax.ShapeDtypeStruct((seq, D_MODEL), F32),
                   jax.ShapeDtypeStruct((seq, D_MODEL), MXU_DTYPE),
                   jax.ShapeDtypeStruct((n_chunks, HEADS, HEAD_DIM, HEAD_DIM), F32)],
        scratch_shapes=[pltpu.VMEM((HEADS, HEAD_DIM, HEAD_DIM), F32)],
        compiler_params=_params(dimension_semantics=("arbitrary",)),
    )(z, z, z, z, lb_l, gn_l)


def _rms_parts(y):
    rs = lax.rsqrt(jnp.mean(y * y, axis=-1, keepdims=True) + NORM_EPS)
    return rs, y * rs


def _merge_fwd(a_in, b_in, z, x, wpo_g, who_g, wout_g, gate, g_post, tm, name):
    seq = x.shape[0]

    def body(a_ref, b_ref, mgp_ref, mgh_ref, x_ref, wpo_ref, who_ref, wout_ref, gate_ref, gp_ref,
             ba_ref, bb_ref, mer_ref, y_ref, xn_ref):
        a = a_ref[...]
        ba = jnp.concatenate([_dot(a, wpo_ref[j]) for j in range(N_DEV)], axis=1)
        bb = _dot(b_ref[...], who_ref[...])
        merged = _sigmoid(mgp_ref[...]) * ba + _sigmoid(mgh_ref[...]) * bb
        y = _dot(merged, wout_ref[...])
        _, yn = _rms_parts(y)
        ba_ref[...] = ba.astype(ba_ref.dtype)
        bb_ref[...] = bb.astype(bb_ref.dtype)
        mer_ref[...] = merged.astype(mer_ref.dtype)
        y_ref[...] = y
        xn_ref[...] = x_ref[...] + gate_ref[...] * (yn * gp_ref[...])

    def tile(cols=D_MODEL, block=0):
        return pl.BlockSpec((tm, cols), lambda i: (i, block))

    full = pl.BlockSpec((D_MODEL, D_MODEL), lambda i: (0, 0))
    act = jax.ShapeDtypeStruct((seq, D_MODEL), MXU_DTYPE)
    f32 = jax.ShapeDtypeStruct((seq, D_MODEL), F32)
    return pl.pallas_call(
        body, name=name, grid=(seq // tm,),
        in_specs=[tile(POOL_WIDTH), tile(), tile(block=COL_MGP), tile(block=COL_MGH), tile(),
                  pl.BlockSpec((N_DEV, POOL_WIDTH, GROUP_DIM), lambda i: (0, 0, 0)),
                  full, full, _row_spec(), _row_spec()],
        out_specs=[tile(), tile(), tile(), tile(), tile()],
        out_shape=[act, act, act, f32, f32],
        compiler_params=_params(dimension_semantics=("parallel",)),
    )(a_in, b_in, z, z, x, wpo_g, who_g, wout_g, gate, g_post)


def _loss_grad(x_out, target, tm):
    seq = x_out.shape[0]

    def body(x_ref, t_ref, loss_ref, dx_ref):
        @pl.when(pl.program_id(0) == 0)
        def _():
            loss_ref[...] = jnp.zeros_like(loss_ref)

        err = x_ref[...] - t_ref[...]
        per_token = jnp.mean(err * err, axis=-1, keepdims=True)
        loss_ref[...] += 0.5 * jnp.sum(per_token, axis=0, keepdims=True)
        dx_ref[...] = err * (1.0 / D_MODEL)

    tile = pl.BlockSpec((tm, D_MODEL), lambda i: (i, 0))
    return pl.pallas_call(
        body, name="loss_grad", grid=(seq // tm,),
        in_specs=[tile, tile],
        out_specs=[pl.BlockSpec((1, 1), lambda i: (0, 0)), tile],
        out_shape=[jax.ShapeDtypeStruct((1, 1), F32), jax.ShapeDtypeStruct((seq, D_MODEL), F32)],
        compiler_params=_params(dimension_semantics=("arbitrary",)),
    )(x_out, target)


def _stage_copy(stage, sems, dst, slot, step, where):
    rows, cols = where(step)
    return pltpu.make_async_copy(stage.at[slot], dst.at[rows, cols], sems.at[slot])


def _stage_begin(stage, sems, dst, step, where):
    slot = step % 2

    @pl.when(step >= 2)
    def _():
        _stage_copy(stage, sems, dst, slot, step - 2, where).wait()

    return slot


def _stage_end(stage, sems, dst, step, n_steps, where):
    slot = step % 2
    _stage_copy(stage, sems, dst, slot, step, where).start()

    @pl.when(step == n_steps - 1)
    def _():
        _stage_copy(stage, sems, dst, slot, step, where).wait()
        if n_steps > 1:
            _stage_copy(stage, sems, dst, 1 - slot, step - 1, where).wait()


def _merge_bwd(dx, y, ba, bb, z, wpo_g, who_g, wout_g, gate, g_post, dz, tm, name):
    seq = dx.shape[0]
    n_steps = seq // tm

    def body(dx_ref, y_ref, ba_ref, bb_ref, mgp_ref, mgh_ref, wpo_ref, who_ref, wout_ref, gate_ref, gp_ref, _,
             dy_ref, dba_ref, dbb_ref, da_ref, db_ref, dz_ref, acc_ref, stage, sems):
        step = pl.program_id(0)

        @pl.when(step == 0)
        def _():
            acc_ref[...] = jnp.zeros_like(acc_ref)

        def where(t):
            return pl.ds(t * tm, tm), pl.ds(COL_MGP * D_MODEL, 2 * D_MODEL)

        dmg_ref = stage.at[_stage_begin(stage, sems, dz_ref, step, where)]

        dxv = dx_ref[...]
        rs, yn = _rms_parts(y_ref[...])
        acc_ref[0:1, :] += jnp.sum(dxv * yn * gp_ref[...], axis=0, keepdims=True)
        acc_ref[1:2, :] += jnp.sum(dxv * gate_ref[...] * yn, axis=0, keepdims=True)
        dyn = dxv * (gate_ref[...] * gp_ref[...])
        dy = rs * (dyn - yn * jnp.mean(dyn * yn, axis=-1, keepdims=True))
        dmerged = _dot_nt(dy, wout_ref[...])
        sp, sh = _sigmoid(mgp_ref[...]), _sigmoid(mgh_ref[...])
        dba, dbb = sp * dmerged, sh * dmerged
        dmg_ref[:, 0:D_MODEL] = (dmerged * ba_ref[...].astype(F32) * sp * (1.0 - sp)).astype(dmg_ref.dtype)
        dmg_ref[:, D_MODEL:2 * D_MODEL] = (dmerged * bb_ref[...].astype(F32) * sh * (1.0 - sh)).astype(dmg_ref.dtype)
        da = _dot_nt(dba[:, 0:GROUP_DIM], wpo_ref[0])
        for j in range(1, N_DEV):
            da += _dot_nt(dba[:, j * GROUP_DIM:(j + 1) * GROUP_DIM], wpo_ref[j])
        dy_ref[...] = dy.astype(dy_ref.dtype)
        dba_ref[...] = dba.astype(dba_ref.dtype)
        dbb_ref[...] = dbb.astype(dbb_ref.dtype)
        da_ref[...] = da
        db_ref[...] = _dot_nt(dbb, who_ref[...])
        _stage_end(stage, sems, dz_ref, step, n_steps, where)

    def tile(cols=D_MODEL, block=0):
        return pl.BlockSpec((tm, cols), lambda i: (i, block))

    full = pl.BlockSpec((D_MODEL, D_MODEL), lambda i: (0, 0))
    hbm = pl.BlockSpec(memory_space=pl.ANY)
    act = jax.ShapeDtypeStruct((seq, D_MODEL), MXU_DTYPE)
    return pl.pallas_call(
        body, name=name, grid=(n_steps,),
        in_specs=[tile(), tile(), tile(), tile(), tile(block=COL_MGP), tile(block=COL_MGH),
                  pl.BlockSpec((N_DEV, POOL_WIDTH, GROUP_DIM), lambda i: (0, 0, 0)),
                  full, full, _row_spec(), _row_spec(), hbm],
        out_specs=[tile(), tile(), tile(), tile(POOL_WIDTH), tile(), hbm,
                   pl.BlockSpec((8, D_MODEL), lambda i: (0, 0))],
        out_shape=[act, act, act, jax.ShapeDtypeStruct((seq, POOL_WIDTH), F32),
                   jax.ShapeDtypeStruct((seq, D_MODEL), F32),
                   jax.ShapeDtypeStruct(dz.shape, dz.dtype),
                   jax.ShapeDtypeStruct((8, D_MODEL), F32)],
        input_output_aliases={11: 5},
        scratch_shapes=[pltpu.VMEM((2, tm, 2 * D_MODEL), MXU_DTYPE), pltpu.SemaphoreType.DMA((2,))],
        compiler_params=_params(dimension_semantics=("arbitrary",)),
    )(dx, y, ba, bb, z, z, wpo_g, who_g, wout_g, gate, g_post, dz)


def _grad_out_weights(merged, dy, b_in, dbb, a_in, dba, name):
    seq = merged.shape[0]
    tn = D_MODEL // 2
    per_step = tn // GROUP_DIM

    def body(mer_ref, dy_ref, b_ref, dbb_ref, a_ref, dba_ref, gout_ref, gho_ref, gpo_ref):
        gout_ref[...] = _dot_tn(mer_ref[...], dy_ref[...]).astype(gout_ref.dtype)
        gho_ref[...] = _dot_tn(b_ref[...], dbb_ref[...]).astype(gho_ref.dtype)
        g_po = _dot_tn(a_ref[...], dba_ref[...])
        for j in range(per_step):
            gpo_ref[j] = g_po[:, j * GROUP_DIM:(j + 1) * GROUP_DIM].astype(gpo_ref.dtype)

    def whole(cols):
        return pl.BlockSpec((seq, cols), lambda j: (0, 0))

    cols = pl.BlockSpec((seq, tn), lambda j: (0, j))
    return pl.pallas_call(
        body, name=name, grid=(D_MODEL // tn,),
        in_specs=[whole(D_MODEL), cols, whole(D_MODEL), cols, whole(POOL_WIDTH), cols],
        out_specs=[pl.BlockSpec((D_MODEL, tn), lambda j: (0, j)), pl.BlockSpec((D_MODEL, tn), lambda j: (0, j)),
                   pl.BlockSpec((per_step, POOL_WIDTH, GROUP_DIM), lambda j: (j, 0, 0))],
        out_shape=[jax.ShapeDtypeStruct((D_MODEL, D_MODEL), WIRE_DTYPE),
                   jax.ShapeDtypeStruct((D_MODEL, D_MODEL), WIRE_DTYPE),
                   jax.ShapeDtypeStruct((N_DEV, POOL_WIDTH, GROUP_DIM), WIRE_DTYPE)],
        compiler_params=_params(dimension_semantics=("parallel",)),
    )(merged, dy, b_in, dbb, a_in, dba)


def _hgrn_bwd(db_in, z, o, states, lb_l, gn_l, dz, name, after=None):
    seq = z.shape[0]
    per_step = min(BWD_STEP_CHUNKS, seq // CHUNK)
    rows_per_step = per_step * CHUNK
    n_steps = seq // rows_per_step
    last_step = n_steps - 1

    def body(db_ref, hq_ref, hf_ref, hi_ref, hg_ref, o_ref, st_ref, lb_ref, gn_ref, _,
             dz_hbm, dlb_ref, dgn_ref, dstate, dq_buf, dk_buf, dg_buf, stage, sems):
        step = pl.program_id(0)

        @pl.when(step == 0)
        def _():
            dstate[...] = jnp.zeros_like(dstate)
            dlb_ref[...] = jnp.zeros_like(dlb_ref)
            dgn_ref[...] = jnp.zeros_like(dgn_ref)

        def one_chunk(cc, *args):
            one_chunk_body((db_ref, hq_ref, hf_ref, hi_ref, hg_ref, o_ref, st_ref, dlb_ref, dgn_ref, dstate,
                            dq_buf, dk_buf, dg_buf), cc, *args)

        def where(t):
            return pl.ds((last_step - t) * rows_per_step, rows_per_step), pl.ds(COL_HQ * D_MODEL, 4 * D_MODEL)

        dz_step = stage.at[_stage_begin(stage, sems, dz_hbm, step, where)]
        causal, before_sub, suffix = _chunk_masks()
        lb = lb_ref[...]
        gn = gn_ref[...]
        for cc in reversed(range(per_step)):
            one_chunk(cc, dz_step, causal, before_sub, suffix, lb, gn)
        _stage_end(stage, sems, dz_hbm, step, n_steps, where)

    def one_chunk_body(refs, cc, dz_step, causal, before_sub, suffix, lb, gn):
        (db_ref, hq_ref, hf_ref, hi_ref, hg_ref, o_ref, st_ref, dlb_ref, dgn_ref, dstate, dq_buf, dk_buf, dg_buf) = refs
        rows = slice(cc * CHUNK, (cc + 1) * CHUNK)
        dz_ref = dz_step.at[rows, :]
        dq_buf, dk_buf, dg_buf = dq_buf.at[cc], dk_buf.at[cc], dg_buf.at[cc]
        sg, f, logf = _gates(hf_ref[rows, :], lb)
        kk = 1.0 - f
        hq = hq_ref[rows, :]
        sq = _sigmoid(hq)
        q = hq * sq
        cum, base = _masked_sums([causal, before_sub], logf)
        dgn = jnp.zeros((1, HEAD_DIM), F32)
        dlast = []
        for h in range(HEADS):
            sl = slice(h * HEAD_DIM, (h + 1) * HEAD_DIM)
            q_h, k_h, cum_h = q[:, sl], kk[:, sl], cum[:, sl]
            v_h = hi_ref[rows, sl]
            st_h = st_ref[cc, h]
            dst_h = dstate[h]
            rs, ohat = _rms_parts(o_ref[rows, sl])
            hg = hg_ref[rows, sl]
            shg = _sigmoid(hg)
            d_bin = db_ref[rows, sl]
            don = d_bin * (hg * shg)
            dgn += jnp.sum(don * ohat, axis=0, keepdims=True)
            dohat = don * gn
            do = rs * (dohat - ohat * jnp.mean(dohat * ohat, axis=-1, keepdims=True))
            dz_ref[:, 3 * D_MODEL + h * HEAD_DIM:3 * D_MODEL + (h + 1) * HEAD_DIM] = (
                d_bin * (ohat * gn) * _dsilu(hg, shg)).astype(dz_ref.dtype)
            last = jnp.sum(logf[:, sl], axis=0, keepdims=True)
            g_in = jnp.exp(cum_h)
            d_out = jnp.exp(last - cum_h)
            q_bar, k_bar = q_h * g_in, k_h * d_out
            blocks = _intra_blocks(q_h, k_h, cum_h, base[:, sl], causal)
            a = jnp.concatenate([b[4] for b in blocks], axis=0)
            da = jnp.where(causal, _dot_nt(do, v_h), 0.0)
            dv = _dot_tn(a, do) + _dot_nt(k_bar, dst_h)
            dq_bar, dk_bar = _dot(do, st_h), _dot(v_h, dst_h)
            dk = dk_bar * d_out
            dq_parts, dg_parts = [], []
            dg_k = k_bar * dk_bar
            dlast.append(jnp.sum(k_bar * dk_bar, axis=0, keepdims=True)
                         + jnp.exp(last) * jnp.sum(st_h * dst_h, axis=0, keepdims=True))
            for i, (q_t, k_t, e_q, e_k, _) in enumerate(blocks):
                da_i = da[i * SUB:(i + 1) * SUB].astype(MXU_DTYPE)
                dq_t = _dot(da_i, k_t)
                dk_t = _dot_tn(da_i, q_t)
                dq_parts.append(dq_t * e_q)
                dk += dk_t * e_k
                dg_parts.append(q_t.astype(F32) * dq_t)
                dg_k += k_t.astype(F32) * dk_t
            dq = dq_bar * g_in + jnp.concatenate(dq_parts, axis=0)
            dg_buf[:, sl] = q_bar * dq_bar + jnp.concatenate(dg_parts, axis=0) - dg_k
            dstate[h] = dst_h * jnp.exp(last) + _dot_tn(do, q_bar)
            dq_buf[:, sl] = dq
            dk_buf[:, sl] = dk
            dz_ref[:, 2 * D_MODEL + h * HEAD_DIM:2 * D_MODEL + (h + 1) * HEAD_DIM] = dv.astype(dz_ref.dtype)
        dgn_ref[...] += dgn
        dq_all, dk_all = dq_buf[...], dk_buf[...]
        dlogf = _masked_sums([suffix], dg_buf[...])[0] + jnp.concatenate(dlast, axis=1)
        df = jnp.where(f > LOG_FLOOR, dlogf / f, 0.0) - dk_all
        dlb_ref[...] += jnp.sum(df * (1.0 - sg), axis=0, keepdims=True)
        dz_ref[:, 0:D_MODEL] = (dq_all * _dsilu(hq, sq)).astype(dz_ref.dtype)
        dz_ref[:, D_MODEL:2 * D_MODEL] = (df * (1.0 - lb) * sg * (1.0 - sg)).astype(dz_ref.dtype)

    def col(block):
        return pl.BlockSpec((rows_per_step, D_MODEL), lambda c: (last_step - c, block))

    hbm = pl.BlockSpec(memory_space=pl.ANY)
    return _pallas_after(
        body, 10, after, name=name, grid=(n_steps,),
        in_specs=[col(0), col(COL_HQ), col(COL_HF), col(COL_HI), col(COL_HG), col(0),
                  pl.BlockSpec((per_step, HEADS, HEAD_DIM, HEAD_DIM), lambda c: (last_step - c, 0, 0, 0)),
                  _row_spec(), _row_spec(HEAD_DIM), hbm],
        out_specs=[hbm, _row_spec(), _row_spec(HEAD_DIM)],
        out_shape=[jax.ShapeDtypeStruct(dz.shape, dz.dtype),
                   jax.ShapeDtypeStruct((1, D_MODEL), F32), jax.ShapeDtypeStruct((1, HEAD_DIM), F32)],
        input_output_aliases={9: 0},
        scratch_shapes=[pltpu.VMEM((HEADS, HEAD_DIM, HEAD_DIM), F32)]
        + [pltpu.VMEM((per_step, CHUNK, D_MODEL), F32)] * 3
        + [pltpu.VMEM((2, rows_per_step, 4 * D_MODEL), MXU_DTYPE), pltpu.SemaphoreType.DMA((2,))],
        compiler_params=_params(dimension_semantics=("arbitrary",)),
    )(db_in, z, z, z, z, o, states, lb_l, gn_l, dz)


def _pool_bwd(da_in, z, pool_w_l, pool_scale_l, dz, name, after=None):
    seq = z.shape[0]

    def body(da_ref, pv_ref, pg_ref, w_ref, sc_ref, _, dz_hbm, dw_ref, dsc_ref, stage_pv, stage_pg, sems_pv, sems_pg):
        g = pl.program_id(0)

        def where_pv(t):
            return pl.ds(0, seq), pl.ds(pl.multiple_of(t * GROUP_DIM, GROUP_DIM), GROUP_DIM)

        def where_pg(t):
            return pl.ds(0, seq), pl.ds(pl.multiple_of(POOL_WIDTH + t * GROUP_DIM, GROUP_DIM), GROUP_DIM)

        dpv_ref = stage_pv.at[_stage_begin(stage_pv, sems_pv, dz_hbm, g, where_pv)]
        dpg_ref = stage_pg.at[_stage_begin(stage_pg, sems_pg, dz_hbm, g, where_pg)]
        pos = lax.broadcasted_iota(jnp.int32, (seq, GROUP_DIM), 0)
        pm, count = _pool_mean_minus_token(pv_ref[...], g, pos)
        lin0 = _dot(pm, w_ref[...])
        pg = pg_ref[...]
        spg = _sigmoid(pg)
        da = da_ref[...]
        dlin = da * (pg * spg)
        dpg_ref[...] = (da * (lin0 * sc_ref[...]) * _dsilu(pg, spg)).astype(dpg_ref.dtype)
        dsc_ref[...] = jnp.sum(dlin * lin0, axis=0, keepdims=True)
        dl0 = dlin * sc_ref[...]
        dw_ref[...] = _dot_tn(pm, dl0)
        dpm = _dot_nt(dl0, w_ref[...])
        sums, acc = [], dpm / count
        for j in (1, 2, 4, 8):
            acc = acc + _shift_up(acc, j, pos, seq)
            sums.append(acc)
        dpv_ref[...] = (_select_window(g, sums) - dpm).astype(dpv_ref.dtype)
        _stage_end(stage_pv, sems_pv, dz_hbm, g, POOL_GROUPS, where_pv)
        _stage_end(stage_pg, sems_pg, dz_hbm, g, POOL_GROUPS, where_pg)

    grp = pl.BlockSpec((seq, GROUP_DIM), lambda g: (0, g))
    hbm = pl.BlockSpec(memory_space=pl.ANY)
    stage = pltpu.VMEM((2, seq, GROUP_DIM), MXU_DTYPE)
    return _pallas_after(
        body, 6, after, name=name, grid=(POOL_GROUPS,),
        in_specs=[grp, grp, pl.BlockSpec((seq, GROUP_DIM), lambda g: (0, POOL_GROUPS + g)),
                  pl.BlockSpec((None, GROUP_DIM, GROUP_DIM), lambda g: (g, 0, 0)),
                  pl.BlockSpec((1, GROUP_DIM), lambda g: (0, g)), hbm],
        out_specs=[hbm, pl.BlockSpec((None, GROUP_DIM, GROUP_DIM), lambda g: (g, 0, 0)),
                   pl.BlockSpec((1, GROUP_DIM), lambda g: (0, g))],
        out_shape=[jax.ShapeDtypeStruct(dz.shape, dz.dtype),
                   jax.ShapeDtypeStruct((POOL_GROUPS, GROUP_DIM, GROUP_DIM), F32),
                   jax.ShapeDtypeStruct((1, POOL_WIDTH), F32)],
        input_output_aliases={5: 0},
        scratch_shapes=[stage, stage, pltpu.SemaphoreType.DMA((2,)), pltpu.SemaphoreType.DMA((2,))],
        compiler_params=_params(dimension_semantics=("arbitrary",)),
    )(da_in, z, z, pool_w_l, pool_scale_l, dz)


def _in_proj_dw(h, dz, name, after=None):
    seq = h.shape[0]

    def body(h_ref, dz_ref, out_ref):
        out_ref[...] = lax.dot_general(h_ref[...], dz_ref[...], (((0,), (0,)), ((), ())),
                                       preferred_element_type=F32).astype(out_ref.dtype)

    return _pallas_after(
        body, 2, after, name=name, grid=(N_DEV,),
        in_specs=[pl.BlockSpec((seq, D_MODEL), lambda j: (0, 0)), pl.BlockSpec((seq, IN_COLS), lambda j: (0, j))],
        out_specs=pl.BlockSpec((None, D_MODEL, IN_COLS), lambda j: (j, 0, 0)),
        out_shape=jax.ShapeDtypeStruct((N_DEV, D_MODEL, IN_COLS), WIRE_DTYPE),
        compiler_params=_params(dimension_semantics=("parallel",)),
    )(h, dz)


def _in_proj_dh(dz, win_g, tm, name, after=None):
    seq = dz.shape[0]

    def body(dz_ref, w_ref, dh_ref):
        @pl.when(pl.program_id(1) == 0)
        def _():
            dh_ref[...] = jnp.zeros_like(dh_ref)

        dh_ref[...] += lax.dot_general(dz_ref[...], w_ref[...], (((1,), (1,)), ((), ())),
                                       preferred_element_type=F32)

    return _pallas_after(
        body, 2, after, name=name, grid=(seq // tm, N_DEV),
        in_specs=[pl.BlockSpec((tm, IN_COLS), lambda i, j: (i, j)),
                  pl.BlockSpec((None, D_MODEL, IN_COLS), lambda i, j: (j, 0, 0))],
        out_specs=pl.BlockSpec((tm, D_MODEL), lambda i, j: (i, 0)),
        out_shape=jax.ShapeDtypeStruct((seq, D_MODEL), F32),
        compiler_params=_params(dimension_semantics=("parallel", "arbitrary")),
    )(dz, win_g)


def _prenorm_bwd(x, dh, dx_res, g, scale, tm, name, after=None):
    seq = x.shape[0]

    def body(x_ref, dh_ref, dxr_ref, g_ref, sc_ref, dx_ref, acc_ref):
        @pl.when(pl.program_id(0) == 0)
        def _():
            acc_ref[...] = jnp.zeros_like(acc_ref)

        rs, xn = _rms_parts(x_ref[...])
        dh = dh_ref[...]
        acc_ref[0:1, :] += jnp.sum(dh, axis=0, keepdims=True)
        acc_ref[1:2, :] += jnp.sum(dh * (xn * g_ref[...]), axis=0, keepdims=True)
        dhn = dh * (1.0 + sc_ref[...])
        acc_ref[2:3, :] += jnp.sum(dhn * xn, axis=0, keepdims=True)
        dxn = dhn * g_ref[...]
        dx_ref[...] = rs * (dxn - xn * jnp.mean(dxn * xn, axis=-1, keepdims=True)) + dxr_ref[...]

    tile = pl.BlockSpec((tm, D_MODEL), lambda i: (i, 0))
    return _pallas_after(
        body, 5, after, name=name, grid=(seq // tm,),
        in_specs=[tile, tile, tile, _row_spec(), _row_spec()],
        out_specs=[tile, pl.BlockSpec((8, D_MODEL), lambda i: (0, 0))],
        out_shape=[jax.ShapeDtypeStruct((seq, D_MODEL), F32), jax.ShapeDtypeStruct((8, D_MODEL), F32)],
        compiler_params=_params(dimension_semantics=("arbitrary",)),
    )(x, dh, dx_res, g, scale)


def _adamw_math(w, g, m, v):
    m = ADAM_B1 * m + (1.0 - ADAM_B1) * g
    v = ADAM_B2 * v + (1.0 - ADAM_B2) * (g * g)
    m_hat = m / (1.0 - ADAM_B1 ** ADAM_STEP)
    v_hat = v / (1.0 - ADAM_B2 ** ADAM_STEP)
    delta = -ADAM_LR * (m_hat / (jnp.sqrt(v_hat) + ADAM_EPS) + ADAM_WD * w)
    return delta, m, v


def _adamw_sharded(w, m, v, contrib, tr, name):
    depth, rows, cols = w.shape
    n_parts = contrib.shape[1]

    def body(w_ref, m_ref, v_ref, c_ref, g_ref, d_ref, mo_ref, vo_ref):
        g = c_ref[0].astype(F32)
        for p in range(1, n_parts):
            g += c_ref[p].astype(F32)
        delta, mn, vn = _adamw_math(w_ref[...], g, m_ref[...], v_ref[...])
        g_ref[...] = g
        d_ref[...] = delta
        mo_ref[...] = mn
        vo_ref[...] = vn

    tile = pl.BlockSpec((None, tr, cols), lambda l, i: (l, i, 0))
    shape = jax.ShapeDtypeStruct(w.shape, F32)
    return pl.pallas_call(
        body, name=name, grid=(depth, rows // tr),
        in_specs=[tile, tile, tile, pl.BlockSpec((None, n_parts, tr, cols), lambda l, i: (l, 0, i, 0))],
        out_specs=[tile] * 4, out_shape=[shape] * 4,
        compiler_params=_params(dimension_semantics=("parallel", "parallel")),
    )(w, m, v, contrib)


def _adamw_layer(w, m, v, contribs, l, tr, name, prev=None):
    _, rows, cols = w.shape
    n = len(contribs)

    def body(*refs):
        w_ref, m_ref, v_ref = refs[:3]
        c_refs = refs[3:3 + n]
        g_ref, d_ref, mo_ref, vo_ref = refs[-4:]
        g = c_refs[0][...].astype(F32)
        for c_ref in c_refs[1:]:
            g += c_ref[...].astype(F32)
        delta, mn, vn = _adamw_math(w_ref[...], g, m_ref[...], v_ref[...])
        g_ref[...] = g
        d_ref[...] = delta
        mo_ref[...] = mn
        vo_ref[...] = vn

    tile = pl.BlockSpec((None, tr, cols), lambda i: (l, i, 0))
    in_specs = [tile, tile, tile] + [pl.BlockSpec((None, tr, cols), lambda i, s=slot: (s, i, 0)) for _, slot in contribs]
    operands = [w, m, v] + [arr for arr, _ in contribs]
    aliases = {}
    if prev is not None:
        aliases = {len(operands) + k: k for k in range(4)}
        in_specs += [pl.BlockSpec(memory_space=pl.ANY)] * 4
        operands += list(prev)
    shape = jax.ShapeDtypeStruct(w.shape, F32)
    return pl.pallas_call(
        body, name=name, grid=(rows // tr,), in_specs=in_specs, out_specs=[tile] * 4, out_shape=[shape] * 4,
        input_output_aliases=aliases,
        compiler_params=_params(dimension_semantics=("parallel",)),
    )(*operands)


def _adamw_small(w_pack, m_pack, v_pack, g_late, g_early, shapes):
    pieces, r = {}, 0
    for name, _, n in _SMALL_ROWS:
        pieces.setdefault(name, []).append((r, n))
        r += n
    names = list(pieces)

    def body(w_ref, m_ref, v_ref, gl_ref, ge_ref, *rest):
        outs, packs = rest[:4 * len(names)], rest[4 * len(names):]
        g_l, g_e = gl_ref[0][0:SMALL_LATE_ROWS], ge_ref[0]
        for d in range(1, N_DEV):
            g_l += gl_ref[d][0:SMALL_LATE_ROWS]
            g_e += ge_ref[d]
        g = jnp.concatenate([g_l, g_e], axis=0)
        w = w_ref[...]
        r0, r1, r2 = LB_ROW0, LB_ROW0 + 8, LB_ROW0 + 16
        lg0, lg1 = w[r0:r1], w[r1:r2]
        mx = jnp.maximum(lg0, lg1)
        e0, e1 = jnp.exp(lg0 - mx), jnp.exp(lg1 - mx)
        p0, p1 = e0 / (e0 + e1), e1 / (e0 + e1)
        low = ((p0 - p0), (p0 + p1) - p0)
        dlow = [g_rows * jnp.where((lo > 0.0) & (lo < 1.0), 1.0, jnp.where((lo == 0.0) | (lo == 1.0), 0.5, 0.0))
                for g_rows, lo in ((g[r0:r1], low[0]), (g[r1:r2], low[1]))]
        dp0 = (dlow[0] + dlow[1]) - (dlow[0] + dlow[1])
        dp1 = dlow[1]
        inner = p0 * dp0 + p1 * dp1
        g = jnp.concatenate([g[:r0], p0 * (dp0 - inner), p1 * (dp1 - inner), g[r2:]], axis=0)
        delta, mn, vn = _adamw_math(w, g, m_ref[...], v_ref[...])
        for kind, val in enumerate((g, delta, mn, vn)):
            packs[kind][...] = val
            for j, name in enumerate(names):
                out, at = outs[kind * len(names) + j], 0
                for start, n in pieces[name]:
                    if name in flat:
                        for r in range(n):
                            layer, c = divmod(at + r, flat[name])
                            out[layer:layer + 1, c * 128:(c + 1) * 128] = packs[kind][start + r:start + r + 1, :]
                    else:
                        out[at:at + n, :] = packs[kind][start:start + n, :]
                    at += n

    rows = {name: sum(n for _, n in pieces[name]) for name in names}
    flat = {name: rows[name] // DEPTH for name in names if len(shapes[name]) == 2}
    outs = pl.pallas_call(
        body, name="adamw_small",
        out_shape=[jax.ShapeDtypeStruct(shapes[name] if name in flat else (rows[name], 128), F32)
                   for _ in range(4) for name in names],
        scratch_shapes=[pltpu.VMEM(w_pack.shape, F32)] * 4, compiler_params=_params(),
    )(w_pack, m_pack, v_pack, g_late, g_early)
    return [{name: outs[kind * len(names) + j].reshape(shapes[name]) for j, name in enumerate(names)}
            for kind in range(4)]


def _pack_small(parts, first=0, last=len(_SMALL_ROWS)):
    rows = [(parts[name] if l is None else parts[name][l]).reshape(n, 128) for name, l, n in _SMALL_ROWS[first:last]]
    if last == len(_SMALL_ROWS):
        rows.append(jnp.zeros((SMALL_ROWS_PAD - sum(n for _, _, n in _SMALL_ROWS), 128), F32))
    return jnp.concatenate(rows, axis=0)


def kernel(x, c, w_ada, b_ada, g_pre, g_post, w_in, pool_w, pool_scale, lb_logits, hgrn_norm_g, w_pool_o, w_hgrn_o, w_out, loss_target, m_w_ada, m_b_ada, m_g_pre, m_g_post, m_w_in, m_pool_w, m_pool_scale, m_lb_logits, m_hgrn_norm_g, m_w_pool_o, m_w_hgrn_o, m_w_out, v_w_ada, v_b_ada, v_g_pre, v_g_post, v_w_in, v_pool_w, v_pool_scale, v_lb_logits, v_hgrn_norm_g, v_w_pool_o, v_w_hgrn_o, v_w_out):
    seq = x.shape[1]
    tm = min(512, seq)
    tm_merge = min(256, seq)
    pos = _my_position()
    me = pos[3]

    c_all = _allgather_small(c, "allgather_c").reshape(N_DEV, D_MODEL)
    b_cols = lax.dynamic_slice_in_dim(b_ada, me * ADA_COLS, ADA_COLS, axis=1)
    ada_part = _ada_fwd(c_all, w_ada, b_cols)
    ada_all = _allgather_small(ada_part.reshape(DEPTH * N_DEV, ADA_COLS), "allgather_ada")
    ada = lax.dynamic_index_in_dim(ada_all.reshape(N_DEV, DEPTH, N_DEV, ADA_COLS), me, axis=2, keepdims=False)
    ada = jnp.transpose(ada, (1, 0, 2)).reshape(DEPTH, 3 * D_MODEL)
    shift = [ada[l:l + 1, 0:D_MODEL] for l in range(DEPTH)]
    scale = [ada[l:l + 1, D_MODEL:2 * D_MODEL] for l in range(DEPTH)]
    gate = [ada[l:l + 1, 2 * D_MODEL:] for l in range(DEPTH)]

    big = dict(win=w_in, wpo=w_pool_o, who=w_hgrn_o, wout=w_out)
    units = [["win0"], ["wpo0", "who0", "wout0"], ["win1", "wpo1", "who1", "wout1"]]
    g_streams = [_gather_streams(keys) for keys in units]
    g_state = [None] * len(units)

    def gather_start(u, after):
        bufs = {}
        for k in units[u]:
            arr = big[k[:-1]]
            bufs["s_" + k] = arr[int(k[-1])].astype(WIRE_DTYPE)
            bufs["g_" + k] = _with_own_slot(bufs["s_" + k], me)
        bufs, sems, token = _comm_call(f"gather_start_{u}", bufs, start=list(g_streams[u][:2]), after=after)
        g_state[u] = dict(bufs=bufs, sems=sems)
        return token

    def gather_pass(u, after):
        st = g_state[u]
        to_chips, _, pass_on = g_streams[u]
        st["bufs"], (st["pass_sems"],), _ = _comm_call(f"gather_pass_{u}", st["bufs"], start=[pass_on],
                                                       wait=[(to_chips, st["sems"][0])], after=after)

    def gather_done(u, after=None):
        st = g_state[u]
        _, to_sibling, pass_on = g_streams[u]
        bufs, _, _ = _comm_call(f"gather_done_{u}", st["bufs"], after=after,
                                wait=[(to_sibling, st["sems"][1]), (pass_on, st["pass_sems"])])
        return {k: bufs["g_" + k] for k in units[u]}

    token = gather_start(0, ada_all)

    lb = _lb_fwd(lb_logits)

    gw = {}
    xs, saved = [x[0]], []
    for l in range(DEPTH):
        h = _prenorm_fwd(xs[l], g_pre[l:l + 1], shift[l], scale[l], tm, f"prenorm_fwd_{l}",
                         after=token if l == 0 else None)
        token = None
        if l == 0:
            gather_pass(0, h)
            gw.update(gather_done(0))
            token = gather_start(1, gw["win0"])
        else:
            gather_pass(2, h)
            gw.update(gather_done(2))
        z = _in_proj(h, gw[f"win{l}"], seq, f"in_proj_{l}", after=token)
        if l == 0:
            gather_pass(1, z)
            token = gather_start(2, g_state[1]["bufs"]["g_wpo0"])
        a_in = _pool_fwd(z, pool_w[l], pool_scale[l:l + 1], f"pool_fwd_{l}", after=token)
        o, b_in, states = _hgrn_fwd(z, lb[l:l + 1], hgrn_norm_g[l:l + 1], f"hgrn_fwd_{l}", after=token)
        if l == 0:
            gw.update(gather_done(1, b_in))
        who_l = gw[f"who{l}"].reshape(D_MODEL, D_MODEL)
        wout_l = gw[f"wout{l}"].reshape(D_MODEL, D_MODEL)
        ba, bb, merged, y, x_next = _merge_fwd(a_in, b_in, z, xs[l], gw[f"wpo{l}"], who_l, wout_l, gate[l],
                                               g_post[l:l + 1], tm_merge, f"merge_fwd_{l}")
        xs.append(x_next)
        saved.append((h, z, a_in, o, b_in, states, ba, bb, merged, y, who_l, wout_l))

    loss_part, dx = _loss_grad(xs[DEPTH], loss_target[0], tm)

    chips = _other_chips(pos)
    pair_idx = jnp.stack([_dev_index(cx, cy, pos[2]) for cx, cy in chips] + [me]).astype(jnp.int32)
    pair_rows = dict(win=256, wpo=POOL_WIDTH, who=HEAD_DIM, wout=HEAD_DIM)

    def scatter_pair_start(u, grads):
        keys = list(grads)
        pair, to_chips = _scatter_streams(keys)
        bufs = {}
        for k in keys:
            bufs["g_" + k] = grads[k]
            bufs["st_" + k] = lax.empty((4,) + grads[k].shape[1:], WIRE_DTYPE)
        bufs, (sems,), token = _comm_call(f"scatter_pair_start_{u}", bufs, start=[pair])
        return dict(u=u, keys=keys, pair=pair, to_chips=to_chips, bufs=bufs, sems=sems, token=token)

    def scatter_pair_finish(st, after):
        u, keys = st["u"], st["keys"]
        bufs, _, _ = _comm_call(f"scatter_pair_done_{u}", st["bufs"], wait=[(st["pair"], st["sems"])], after=after)
        bufs2 = {}
        for k in keys:
            bufs2["ps_" + k] = _pair_sum(bufs["g_" + k], bufs["st_" + k], pair_idx, bufs["g_" + k].shape[1],
                                         f"pair_sum_{k}")
            bufs2["ld_" + k] = lax.empty((3,) + bufs["g_" + k].shape[1:], WIRE_DTYPE)
        st.update(bufs=bufs2)

    def scatter_chips_start(st, after=None):
        bufs2, (sems,), token = _comm_call(f"scatter_chips_start_{st['u']}", st["bufs"], start=[st["to_chips"]],
                                           after=after)
        st.update(bufs=bufs2, sems=sems, token=token)

    def scatter_finish(st, after):
        bufs, _, _ = _comm_call(f"scatter_chips_done_{st['u']}", st["bufs"], wait=[(st["to_chips"], st["sems"])],
                                after=after)
        return {k: [(bufs["ps_" + k], 3), (bufs["ld_" + k], 0), (bufs["ld_" + k], 1), (bufs["ld_" + k], 2)]
                for k in st["keys"]}

    moments = dict(win=(m_w_in, v_w_in), wpo=(m_w_pool_o, v_w_pool_o), who=(m_w_hgrn_o, v_w_hgrn_o),
                   wout=(m_w_out, v_w_out))
    big_out = {}

    def finish_unit(unit, after):
        for k, contribs in scatter_finish(scat[unit], after).items():
            wname, l = k[:-1], int(k[-1])
            big_out[wname] = _adamw_layer(big[wname], moments[wname][0], moments[wname][1], contribs, l,
                                          pair_rows[wname], f"adamw_{k}", prev=big_out.get(wname))
            after = big_out[wname][0]
        return after

    d_ada, small, scat = [None] * DEPTH, [None] * DEPTH, {}
    for l in reversed(range(DEPTH)):
        h, z, a_in, o, b_in, states, ba, bb, merged, y, who_l, wout_l = saved[l]
        dy, dba, dbb, da_in, db_in, dz, acc_post = _merge_bwd(
            dx, y, ba, bb, z, gw[f"wpo{l}"], who_l, wout_l, gate[l], g_post[l:l + 1],
            lax.empty((seq, IN_WIDTH), MXU_DTYPE), tm_merge, f"merge_bwd_{l}")
        g_out, g_ho, g_po = _grad_out_weights(merged, dy, b_in, dbb, a_in, dba, f"grad_out_weights_{l}")
        g_small = {f"wout{l}": g_out.reshape(N_DEV, HEAD_DIM, D_MODEL),
                   f"who{l}": g_ho.reshape(N_DEV, HEAD_DIM, D_MODEL), f"wpo{l}": g_po}
        st_small = scat["small0"] = scatter_pair_start("small0", g_small) if l == 0 else None
        dz, dlb, dgn = _hgrn_bwd(db_in, z, o, states, lb[l:l + 1], hgrn_norm_g[l:l + 1], dz, f"hgrn_bwd_{l}",
                                 after=st_small and st_small["token"])
        if l == 0:
            scatter_pair_finish(st_small, dlb)
            scatter_chips_start(st_small)
        dz, dpw, dps = _pool_bwd(da_in, z, pool_w[l], pool_scale[l:l + 1], dz, f"pool_bwd_{l}",
                                 after=st_small and st_small["token"])
        small[l] = dict(g_post=acc_post[1], pool_w=dpw, pool_scale=dps[0], lb_logits=dlb[0], hgrn_norm_g=dgn[0])
        token = None
        if l == 0:
            parts = {name: jnp.stack([small[0][name], small[1][name]]) for name in small[0]}
            parts.update(b_ada=[None, d_ada[1]], g_pre=[None, small[1]["g_pre"]])
            sg_stream = _direct_gather_stream("sg")
            early = _pack_small(parts, 2)
            sg_bufs, (sg_sems,), token = _comm_call(
                "small_grads_start", dict(s_sg=early, g_sg=_with_own_slot(early, me)), start=[sg_stream])
        g_win = {f"win{l}": _in_proj_dw(h, dz, f"grad_w_in_{l}", after=token)}
        st_win = scat[f"win{l}"] = scatter_pair_start(f"win{l}", g_win if l == 0 else {**g_small, **g_win})
        if l > 0:
            dh = _in_proj_dh(dz, gw[f"win{l}"], seq, f"in_proj_dh_{l}", after=st_win["token"])
            scatter_pair_finish(st_win, dh)
            scatter_chips_start(st_win)
        else:
            scatter_pair_finish(st_win, st_win["token"])
            scatter_chips_start(st_win)
            after = st_win["token"]
            for unit in ("win1", "small0"):
                after = finish_unit(unit, after)
            dh = _in_proj_dh(dz, gw[f"win{l}"], seq, f"in_proj_dh_{l}", after=after)
        dx, acc_pre = _prenorm_bwd(xs[l], dh, dx, g_pre[l:l + 1], scale[l], tm, f"prenorm_bwd_{l}",
                                   after=st_win["token"])
        d_ada[l] = jnp.concatenate([acc_pre[0], acc_pre[1], acc_post[0]])
        small[l]["g_pre"] = acc_pre[2]
    grad_x = dx[None]

    parts = dict(b_ada=[d_ada[0]], g_pre=[small[0]["g_pre"]])
    late = jnp.concatenate([_pack_small(parts, 0, 2), jnp.broadcast_to(loss_part, (8, 128))], axis=0)
    g_late = _allgather_small(late, "allgather_late_grads")
    loss = jnp.sum(g_late[:, SMALL_LATE_ROWS, 0])
    sg_bufs, _, _ = _comm_call("small_grads_done", sg_bufs, wait=[(sg_stream, sg_sems)], after=g_late)
    g_early = sg_bufs["g_sg"]
    small_names = list(dict.fromkeys(name for name, _, _ in _SMALL_ROWS))
    weights = dict(b_ada=b_ada, g_pre=g_pre, g_post=g_post, pool_w=pool_w, pool_scale=pool_scale,
                   lb_logits=lb_logits, hgrn_norm_g=hgrn_norm_g)
    m_small = dict(b_ada=m_b_ada, g_pre=m_g_pre, g_post=m_g_post, pool_w=m_pool_w, pool_scale=m_pool_scale,
                   lb_logits=m_lb_logits, hgrn_norm_g=m_hgrn_norm_g)
    v_small = dict(b_ada=v_b_ada, g_pre=v_g_pre, g_post=v_g_post, pool_w=v_pool_w, pool_scale=v_pool_scale,
                   lb_logits=v_lb_logits, hgrn_norm_g=v_hgrn_norm_g)
    shapes = {name: weights[name].shape for name in small_names}
    small_out = _adamw_small(_pack_small(weights), _pack_small(m_small), _pack_small(v_small), g_late, g_early,
                             shapes)

    d_ada_all = jnp.stack([g_late[:, 0:24, :].reshape(N_DEV, 3 * D_MODEL),
                           g_early[:, 0:24, :].reshape(N_DEV, 3 * D_MODEL)], axis=1)
    d_cols = jnp.transpose(lax.dynamic_slice_in_dim(d_ada_all, me * ADA_COLS, ADA_COLS, axis=2), (1, 0, 2))
    g_w_ada = _ada_bwd(c_all, d_cols)
    ada_out = _adamw_sharded(w_ada, m_w_ada, v_w_ada, g_w_ada[:, None], 256, "adamw_w_ada")
    finish_unit("win0", ada_out[1][0, 0:8, 0:128] + small_out[1]["pool_scale"][0:1, 0:128])

    def leaf(kind):
        s = small_out[kind]
        return (ada_out[kind], s["b_ada"], s["g_pre"], s["g_post"], big_out["win"][kind], s["pool_w"], s["pool_scale"],
                s["lb_logits"], s["hgrn_norm_g"], big_out["wpo"][kind], big_out["who"][kind], big_out["wout"][kind])

    return (loss, grad_x) + leaf(0) + leaf(1) + leaf(2) + leaf(3)
```

```python
import jax
import jax.numpy as jnp
from jax import lax
from jax.experimental import pallas as pl
from jax.experimental.pallas import tpu as pltpu

F32 = jnp.float32
MXU_DTYPE = jnp.bfloat16
WIRE_DTYPE = jnp.bfloat16

N_DEV = 8
DEPTH = 2
D_MODEL = 1024
HEADS = 8
HEAD_DIM = 128
POOL_GROUPS = 4
GROUP_DIM = 128
POOL_WIDTH = POOL_GROUPS * GROUP_DIM
IN_WIDTH = 7168
CHUNK = 64
SUB = 16
N_SUB = CHUNK // SUB
FWD_STEP_CHUNKS = 8
BWD_STEP_CHUNKS = 4
EXP_CLAMP = 80.0
NORM_EPS = 1e-6
LOG_FLOOR = 1e-30
ADA_COLS = 3 * D_MODEL // N_DEV
IN_COLS = IN_WIDTH // N_DEV
COL_HQ, COL_HF, COL_HI, COL_HG, COL_MGP, COL_MGH = 1, 2, 3, 4, 5, 6

ADAM_LR = 0.001
ADAM_B1 = 0.9
ADAM_B2 = 0.999
ADAM_EPS = 1e-08
ADAM_WD = 0.01
ADAM_STEP = 10

VMEM_LIMIT = 48 * 1024 * 1024
MESH_ID = pl.DeviceIdType.MESH
HIGHEST = lax.Precision.HIGHEST

_SMALL_ROWS = (("b_ada", 0, 24), ("g_pre", 0, 8), ("b_ada", 1, 24), ("g_pre", 1, 8), ("g_post", None, 16),
               ("pool_w", None, 1024), ("pool_scale", None, 8), ("lb_logits", None, 16), ("hgrn_norm_g", None, 2))
SMALL_LATE_ROWS = 32
SMALL_ROWS_PAD = 1136
LB_ROW0 = 32 + 32 + 16 + 1024 + 8


def _params(**kw):
    return pltpu.CompilerParams(vmem_limit_bytes=VMEM_LIMIT, **kw)


def _sigmoid(v):
    return 1.0 / (1.0 + jnp.exp(-v))


def _dsilu(v, s):
    return s * (1.0 + v * (1.0 - s))


def _dot(a, b):
    return jnp.dot(a.astype(MXU_DTYPE), b.astype(MXU_DTYPE), preferred_element_type=F32)


def _dot_nt(a, b):
    return lax.dot_general(a.astype(MXU_DTYPE), b.astype(MXU_DTYPE), (((1,), (1,)), ((), ())),
                           preferred_element_type=F32)


def _dot_tn(a, b):
    return lax.dot_general(a.astype(MXU_DTYPE), b.astype(MXU_DTYPE), (((0,), (0,)), ((), ())),
                           preferred_element_type=F32)


def _pallas_after(body, n_in, after, *, in_specs, **kw):
    if after is None:
        return pl.pallas_call(body, in_specs=in_specs, **kw)

    def tied(*refs):
        body(*refs[:n_in], *refs[n_in + 1:])

    call = pl.pallas_call(tied, in_specs=list(in_specs) + [pl.BlockSpec(memory_space=pl.ANY)], **kw)
    return lambda *operands: call(*operands, after)


def _my_position():
    mx, my, mc = lax.axis_index("x"), lax.axis_index("y"), lax.axis_index("c")
    return mx, my, mc, 4 * mx + 2 * my + mc


def _peer(mx, my, mc, k):
    px = 1 - mx if (k >> 2) & 1 else mx
    py = 1 - my if (k >> 1) & 1 else my
    pc = 1 - mc if k & 1 else mc
    return (px, py, pc), 4 * px + 2 * py + pc


def _allgather_small(v, name, after=None):
    rows, cols = v.shape

    def body(v_ref, out_ref, send_sems, recv_sems):
        mx, my, mc, me = _my_position()
        out_ref[me] = v_ref[...]
        copies = []
        for k in range(1, N_DEV):
            peer, _ = _peer(mx, my, mc, k)
            cp = pltpu.make_async_remote_copy(
                src_ref=v_ref, dst_ref=out_ref.at[me],
                send_sem=send_sems.at[k - 1], recv_sem=recv_sems.at[k - 1],
                device_id=peer, device_id_type=MESH_ID)
            cp.start()
            copies.append(cp)
        for cp in copies:
            cp.wait()

    return _pallas_after(
        body, 1, after, name=name,
        out_shape=jax.ShapeDtypeStruct((N_DEV, rows, cols), v.dtype),
        in_specs=[pl.BlockSpec(memory_space=pltpu.VMEM)],
        out_specs=pl.BlockSpec(memory_space=pltpu.VMEM),
        scratch_shapes=[pltpu.SemaphoreType.DMA((N_DEV - 1,)), pltpu.SemaphoreType.DMA((N_DEV - 1,))],
        compiler_params=_params(),
    )(v)


class _Stream:
    def __init__(self, n, plan):
        self.n, self.plan = n, plan


def _comm_call(name, bufs, start=(), wait=(), after=None):
    names = list(bufs)

    def body(*refs):
        it = iter(refs)
        buf_refs = {n: next(it) for n in names}
        wait_sems = [(next(it), next(it)) for _ in wait]
        if after is not None:
            next(it)
        start_sems = [(next(it), next(it)) for _ in start]
        for _ in names:
            next(it)
        token = next(it)
        pos = _my_position()

        def descriptors(stream, sems):
            return [pltpu.make_async_remote_copy(src_ref=src, dst_ref=dst, send_sem=sems[0].at[k], recv_sem=sems[1].at[k],
                                                 device_id=dev, device_id_type=MESH_ID)
                    for k, (src, dst, dev) in enumerate(stream.plan(buf_refs, pos))]

        for (stream, _), sems in zip(wait, wait_sems):
            for cp in descriptors(stream, sems):
                cp.wait_send()
                cp.wait_recv()
        for stream, sems in zip(start, start_sems):
            for cp in descriptors(stream, sems):
                cp.start()
        token[...] = jnp.zeros_like(token)

    hbm = pl.BlockSpec(memory_space=pltpu.HBM)
    sem = pl.BlockSpec(memory_space=pltpu.SEMAPHORE)
    operands = [pltpu.with_memory_space_constraint(bufs[n], pltpu.HBM) for n in names]
    in_specs = [hbm] * len(names)
    for _, (send_sems, recv_sems) in wait:
        operands += [send_sems, recv_sems]
        in_specs += [sem, sem]
    if after is not None:
        operands.append(after)
        in_specs.append(pl.BlockSpec(memory_space=pl.ANY))
    out_shape, out_specs = [], []
    for stream in start:
        out_shape += [pltpu.SemaphoreType.DMA((stream.n,)), pltpu.SemaphoreType.DMA((stream.n,))]
        out_specs += [sem, sem]
    n_sem_out = len(out_shape)
    out_shape += [pltpu.HBM(bufs[n].shape, bufs[n].dtype) for n in names]
    out_specs += [hbm] * len(names)
    out_shape.append(jax.ShapeDtypeStruct((8, 128), F32))
    out_specs.append(pl.BlockSpec(memory_space=pltpu.VMEM))
    outs = pl.pallas_call(
        body, name=name, out_shape=out_shape, in_specs=in_specs, out_specs=out_specs,
        input_output_aliases={i: n_sem_out + i for i in range(len(names))},
        compiler_params=pltpu.CompilerParams(has_side_effects=pltpu.SideEffectType.DATAFLOW_SIDE_EFFECTING),
    )(*operands)
    sems = [(outs[2 * i], outs[2 * i + 1]) for i in range(len(start))]
    return dict(zip(names, outs[n_sem_out:n_sem_out + len(names)])), sems, outs[-1]


def _with_own_slot(block, me):
    return lax.dynamic_update_index_in_dim(lax.empty((N_DEV,) + block.shape, block.dtype), block, me, 0)


def _other_chips(pos):
    mx, my, _, _ = pos
    return [(1 - mx if i & 2 else mx, 1 - my if i & 1 else my) for i in (1, 2, 3)]


def _dev_index(px, py, pc):
    return 4 * px + 2 * py + pc


def _gather_streams(keys):
    def to_chips(refs, pos):
        _, _, mc, me = pos
        return [(refs["s_" + k], refs["g_" + k].at[me], (cx, cy, mc)) for k in keys for cx, cy in _other_chips(pos)]

    def to_sibling(refs, pos):
        mx, my, mc, me = pos
        return [(refs["s_" + k], refs["g_" + k].at[me], (mx, my, 1 - mc)) for k in keys]

    def pass_on(refs, pos):
        mx, my, mc, _ = pos
        out = []
        for k in keys:
            for cx, cy in _other_chips(pos):
                slot = refs["g_" + k].at[_dev_index(cx, cy, mc)]
                out.append((slot, slot, (mx, my, 1 - mc)))
        return out

    return _Stream(3 * len(keys), to_chips), _Stream(len(keys), to_sibling), _Stream(3 * len(keys), pass_on)


def _direct_gather_stream(key):
    def plan(refs, pos):
        mx, my, mc, me = pos
        return [(refs["s_" + key], refs["g_" + key].at[me], _peer(mx, my, mc, k)[0]) for k in range(1, N_DEV)]

    return _Stream(N_DEV - 1, plan)


def _scatter_streams(keys):
    def pair(refs, pos):
        mx, my, mc, _ = pos
        sib = (mx, my, 1 - mc)
        out = []
        for k in keys:
            for i, (cx, cy) in enumerate(_other_chips(pos)):
                out.append((refs["g_" + k].at[_dev_index(cx, cy, 1 - mc)], refs["st_" + k].at[i], sib))
            out.append((refs["g_" + k].at[_dev_index(mx, my, 1 - mc)], refs["st_" + k].at[3], sib))
        return out

    def chips(refs, pos):
        mc = pos[2]
        return [(refs["ps_" + k].at[i], refs["ld_" + k].at[i], (cx, cy, mc))
                for k in keys for i, (cx, cy) in enumerate(_other_chips(pos))]

    return _Stream(4 * len(keys), pair), _Stream(3 * len(keys), chips)


def _pair_sum(g, st, idx, tr, name):
    _, rows, cols = g.shape

    def body(idx_ref, g_ref, st_ref, out_ref):
        out_ref[...] = (g_ref[...].astype(F32) + st_ref[...].astype(F32)).astype(out_ref.dtype)

    return pl.pallas_call(
        body, name=name,
        grid_spec=pltpu.PrefetchScalarGridSpec(
            num_scalar_prefetch=1, grid=(4, rows // tr),
            in_specs=[pl.BlockSpec((None, tr, cols), lambda j, i, idx_ref: (idx_ref[j], i, 0)),
                      pl.BlockSpec((None, tr, cols), lambda j, i, idx_ref: (j, i, 0))],
            out_specs=pl.BlockSpec((None, tr, cols), lambda j, i, idx_ref: (j, i, 0))),
        out_shape=jax.ShapeDtypeStruct((4, rows, cols), WIRE_DTYPE),
        compiler_params=_params(dimension_semantics=("parallel", "parallel")),
    )(idx, g, st)


def _ada_fwd(c_all, w_ada, b_cols):
    def body(c_ref, w_ref, b_ref, out_ref):
        cv = c_ref[...]
        ca = cv * _sigmoid(cv)
        for l in range(DEPTH):
            out_ref[l] = jnp.dot(ca, w_ref[l], precision=HIGHEST, preferred_element_type=F32) + b_ref[l:l + 1, :]

    return pl.pallas_call(
        body, name="ada_fwd",
        out_shape=jax.ShapeDtypeStruct((DEPTH, N_DEV, ADA_COLS), F32),
        compiler_params=_params(),
    )(c_all, w_ada, b_cols)


def _ada_bwd(c_all, d_cols):
    def body(c_ref, d_ref, out_ref):
        cv = c_ref[...]
        ca = cv * _sigmoid(cv)
        for l in range(DEPTH):
            out_ref[l] = lax.dot_general(ca, d_ref[l], (((0,), (0,)), ((), ())), precision=HIGHEST,
                                         preferred_element_type=F32)

    return pl.pallas_call(
        body, name="ada_bwd",
        out_shape=jax.ShapeDtypeStruct((DEPTH, D_MODEL, ADA_COLS), F32),
        compiler_params=_params(),
    )(c_all, d_cols)


def _lower_bounds(logits):
    m = jnp.maximum(logits[0:1], logits[1:2])
    e0, e1 = jnp.exp(logits[0:1] - m), jnp.exp(logits[1:2] - m)
    den = e0 + e1
    p0, p1 = e0 / den, e1 / den
    low0 = p0 - p0
    low1 = (p0 + p1) - p0
    return (p0, p1), (low0, low1)


def _lb_fwd(lb_logits):
    def body(lg_ref, out_ref):
        _, (low0, low1) = _lower_bounds(lg_ref[...])
        out_ref[0:1, :] = jnp.clip(low0, 0.0, 1.0)
        out_ref[1:2, :] = jnp.clip(low1, 0.0, 1.0)

    return pl.pallas_call(body, name="lb_fwd", out_shape=jax.ShapeDtypeStruct(lb_logits.shape, F32),
                          compiler_params=_params())(lb_logits)


def _row_spec(cols=D_MODEL):
    return pl.BlockSpec((1, cols), lambda *_: (0, 0))


def _prenorm_fwd(x, g, shift, scale, tm, name, after=None):
    seq = x.shape[0]

    def body(x_ref, g_ref, sh_ref, sc_ref, h_ref):
        xv = x_ref[...]
        rs = lax.rsqrt(jnp.mean(xv * xv, axis=-1, keepdims=True) + NORM_EPS)
        h = (xv * rs * g_ref[...]) * (1.0 + sc_ref[...]) + sh_ref[...]
        h_ref[...] = h.astype(h_ref.dtype)

    tile = pl.BlockSpec((tm, D_MODEL), lambda i: (i, 0))
    return _pallas_after(
        body, 4, after, name=name, grid=(seq // tm,),
        in_specs=[tile, _row_spec(), _row_spec(), _row_spec()], out_specs=tile,
        out_shape=jax.ShapeDtypeStruct((seq, D_MODEL), MXU_DTYPE),
        compiler_params=_params(dimension_semantics=("parallel",)),
    )(x, g, shift, scale)


def _in_proj(h, win_g, tm, name, after=None):
    seq = h.shape[0]

    def body(h_ref, w_ref, z_ref, w_pair):
        @pl.when(pl.program_id(1) == 0)
        def _():
            w_pair[...] = jnp.concatenate([w_ref[0], w_ref[1]], axis=1)

        z_ref[...] = jnp.dot(h_ref[...], w_pair[...], preferred_element_type=F32)

    return _pallas_after(
        body, 2, after, name=name, grid=(N_DEV // 2, seq // tm),
        in_specs=[pl.BlockSpec((tm, D_MODEL), lambda j, i: (i, 0)),
                  pl.BlockSpec((2, D_MODEL, IN_COLS), lambda j, i: (j, 0, 0))],
        out_specs=pl.BlockSpec((tm, 2 * IN_COLS), lambda j, i: (i, j)),
        out_shape=jax.ShapeDtypeStruct((seq, IN_WIDTH), F32),
        scratch_shapes=[pltpu.VMEM((D_MODEL, 2 * IN_COLS), MXU_DTYPE)],
        compiler_params=_params(dimension_semantics=("parallel", "arbitrary")),
    )(h, win_g)


def _shift_down(v, j, pos):
    return jnp.where(pos >= j, pltpu.roll(v, j, 0), 0.0)


def _shift_up(v, j, pos, seq):
    return jnp.where(pos < seq - j, pltpu.roll(v, seq - j, 0), 0.0)


def _select_window(g, candidates):
    out = candidates[-1]
    for i in range(len(candidates) - 2, -1, -1):
        out = jnp.where(g == i, candidates[i], out)
    return out


def _pool_mean_minus_token(u, g, pos):
    sums, acc = [], u
    for j in (1, 2, 4, 8):
        acc = acc + _shift_down(acc, j, pos)
        sums.append(acc)
    wsum = _select_window(g, sums)
    width = jnp.left_shift(2, g).astype(F32)
    count = jnp.minimum(pos.astype(F32) + 1.0, width)
    return wsum / count - u, count


def _pool_fwd(z, pool_w_l, pool_scale_l, name, after=None):
    seq = z.shape[0]

    def body(pv_ref, pg_ref, w_ref, sc_ref, out_ref):
        g = pl.program_id(0)
        pos = lax.broadcasted_iota(jnp.int32, (seq, GROUP_DIM), 0)
        pm, _ = _pool_mean_minus_token(pv_ref[...], g, pos)
        lin = _dot(pm, w_ref[...]) * sc_ref[...]
        pg = pg_ref[...]
        out_ref[...] = (lin * (pg * _sigmoid(pg))).astype(out_ref.dtype)

    return _pallas_after(
        body, 4, after, name=name, grid=(POOL_GROUPS,),
        in_specs=[pl.BlockSpec((seq, GROUP_DIM), lambda g: (0, g)),
                  pl.BlockSpec((seq, GROUP_DIM), lambda g: (0, POOL_GROUPS + g)),
                  pl.BlockSpec((None, GROUP_DIM, GROUP_DIM), lambda g: (g, 0, 0)),
                  pl.BlockSpec((1, GROUP_DIM), lambda g: (0, g))],
        out_specs=pl.BlockSpec((seq, GROUP_DIM), lambda g: (0, g)),
        out_shape=jax.ShapeDtypeStruct((seq, POOL_WIDTH), MXU_DTYPE),
        compiler_params=_params(dimension_semantics=("parallel",)),
    )(z, z, pool_w_l, pool_scale_l)


def _chunk_masks():
    row = lax.broadcasted_iota(jnp.int32, (CHUNK, CHUNK), 0)
    col = lax.broadcasted_iota(jnp.int32, (CHUNK, CHUNK), 1)
    causal = row >= col
    before_sub = col < (row // SUB) * SUB
    suffix = row <= col
    return causal, before_sub, suffix


def _masked_sums(masks, v):
    lhs = jnp.concatenate([m.astype(jnp.bfloat16) for m in masks], axis=0)
    hi = v.astype(jnp.bfloat16)
    rest = v - hi.astype(F32)
    mid = rest.astype(jnp.bfloat16)
    lo = (rest - mid.astype(F32)).astype(jnp.bfloat16)
    out = jnp.dot(lhs, hi, preferred_element_type=F32)
    out += jnp.dot(lhs, mid, preferred_element_type=F32)
    out += jnp.dot(lhs, lo, preferred_element_type=F32)
    return [out[i * CHUNK:(i + 1) * CHUNK] for i in range(len(masks))]


def _gates(zf, lb):
    sg = _sigmoid(zf)
    f = lb + (1.0 - lb) * sg
    logf = jnp.log(jnp.maximum(f, LOG_FLOOR))
    return sg, f, logf


def _intra_blocks(q_h, k_h, cum_h, base_h, causal):
    rel = cum_h - base_h
    out = []
    for i in range(N_SUB):
        rows = slice(i * SUB, (i + 1) * SUB)
        e_q = jnp.exp(rel[rows])
        base_i = jnp.concatenate([base_h[rows]] * N_SUB, axis=0)
        e_k = jnp.exp(jnp.minimum(base_i - cum_h, EXP_CLAMP))
        q_t = (q_h[rows] * e_q).astype(MXU_DTYPE)
        k_t = (k_h * e_k).astype(MXU_DTYPE)
        a_i = jnp.where(causal[rows], _dot_nt(q_t, k_t), 0.0)
        out.append((q_t, k_t, e_q, e_k, a_i))
    return out


def _hgrn_fwd(z, lb_l, gn_l, name, after=None):
    seq = z.shape[0]
    n_chunks = seq // CHUNK
    per_step = min(FWD_STEP_CHUNKS, n_chunks)
    rows_per_step = per_step * CHUNK

    def body(hq_ref, hf_ref, hi_ref, hg_ref, lb_ref, gn_ref, o_ref, bin_ref, st_ref, state):
        @pl.when(pl.program_id(0) == 0)
        def _():
            state[...] = jnp.zeros_like(state)

        causal, before_sub, _ = _chunk_masks()
        for cc in range(per_step):
            rows = slice(cc * CHUNK, (cc + 1) * CHUNK)
            _, f, logf = _gates(hf_ref[rows, :], lb_ref[...])
            kk = 1.0 - f
            hq = hq_ref[rows, :]
            q = hq * _sigmoid(hq)
            cum, base = _masked_sums([causal, before_sub], logf)
            st_ref[cc] = state[...]
            for h in range(HEADS):
                sl = slice(h * HEAD_DIM, (h + 1) * HEAD_DIM)
                q_h, k_h, cum_h = q[:, sl], kk[:, sl], cum[:, sl]
                v_h = hi_ref[rows, sl]
                st_h = state[h]
                blocks = _intra_blocks(q_h, k_h, cum_h, base[:, sl], causal)
                a = jnp.concatenate([b[4] for b in blocks], axis=0)
                o_h = _dot_nt(q_h * jnp.exp(cum_h), st_h) + _dot(a, v_h)
                last = jnp.sum(logf[:, sl], axis=0, keepdims=True)
                state[h] = st_h * jnp.exp(last) + _dot_tn(v_h, k_h * jnp.exp(last - cum_h))
                rs = lax.rsqrt(jnp.mean(o_h * o_h, axis=-1, keepdims=True) + NORM_EPS)
                hg = hg_ref[rows, sl]
                o_ref[rows, sl] = o_h
                bin_ref[rows, sl] = ((o_h * rs * gn_ref[...]) * (hg * _sigmoid(hg))).astype(bin_ref.dtype)

    def col(block):
        return pl.BlockSpec((rows_per_step, D_MODEL), lambda c: (c, block))

    tile = pl.BlockSpec((rows_per_step, D_MODEL), lambda c: (c, 0))
    return _pallas_after(
        body, 6, after, name=name, grid=(n_chunks // per_step,),
        in_specs=[col(COL_HQ), col(COL_HF), col(COL_HI), col(COL_HG), _row_spec(), _row_spec(HEAD_DIM)],
        out_specs=[tile, tile, pl.BlockSpec((per_step, HEADS, HEAD_DIM, HEAD_DIM), lambda c: (c, 0, 0, 0))],
        out_shape=[jax.ShapeDtypeStruct((seq, D_MODEL), F32),
                   jax.ShapeDtypeStruct((seq, D_MODEL), MXU_DTYPE),
                   jax.ShapeDtypeStruct((n_chunks, HEADS, HEAD_DIM, HEAD_DIM), F32)],
        scratch_shapes=[pltpu.VMEM((HEADS, HEAD_DIM, HEAD_DIM), F32)],
        compiler_params=_params(dimension_semantics=("arbitrary",)),
    )(z, z, z, z, lb_l, gn_l)


def _rms_parts(y):
    rs = lax.rsqrt(jnp.mean(y * y, axis=-1, keepdims=True) + NORM_EPS)
    return rs, y * rs


def _merge_fwd(a_in, b_in, z, x, wpo_g, who_g, wout_g, gate, g_post, tm, name):
    seq = x.shape[0]

    def body(a_ref, b_ref, mgp_ref, mgh_ref, x_ref, wpo_ref, who_ref, wout_ref, gate_ref, gp_ref,
             ba_ref, bb_ref, mer_ref, y_ref, xn_ref):
        a = a_ref[...]
        ba = jnp.concatenate([_dot(a, wpo_ref[j]) for j in range(N_DEV)], axis=1)
        bb = _dot(b_ref[...], who_ref[...])
        merged = _sigmoid(mgp_ref[...]) * ba + _sigmoid(mgh_ref[...]) * bb
        y = _dot(merged, wout_ref[...])
        _, yn = _rms_parts(y)
        ba_ref[...] = ba.astype(ba_ref.dtype)
        bb_ref[...] = bb.astype(bb_ref.dtype)
        mer_ref[...] = merged.astype(mer_ref.dtype)
        y_ref[...] = y
        xn_ref[...] = x_ref[...] + gate_ref[...] * (yn * gp_ref[...])

    def tile(cols=D_MODEL, block=0):
        return pl.BlockSpec((tm, cols), lambda i: (i, block))

    full = pl.BlockSpec((D_MODEL, D_MODEL), lambda i: (0, 0))
    act = jax.ShapeDtypeStruct((seq, D_MODEL), MXU_DTYPE)
    f32 = jax.ShapeDtypeStruct((seq, D_MODEL), F32)
    return pl.pallas_call(
        body, name=name, grid=(seq // tm,),
        in_specs=[tile(POOL_WIDTH), tile(), tile(block=COL_MGP), tile(block=COL_MGH), tile(),
                  pl.BlockSpec((N_DEV, POOL_WIDTH, GROUP_DIM), lambda i: (0, 0, 0)),
                  full, full, _row_spec(), _row_spec()],
        out_specs=[tile(), tile(), tile(), tile(), tile()],
        out_shape=[act, act, act, f32, f32],
        compiler_params=_params(dimension_semantics=("parallel",)),
    )(a_in, b_in, z, z, x, wpo_g, who_g, wout_g, gate, g_post)


def _loss_grad(x_out, target, tm):
    seq = x_out.shape[0]

    def body(x_ref, t_ref, loss_ref, dx_ref):
        @pl.when(pl.program_id(0) == 0)
        def _():
            loss_ref[...] = jnp.zeros_like(loss_ref)

        err = x_ref[...] - t_ref[...]
        per_token = jnp.mean(err * err, axis=-1, keepdims=True)
        loss_ref[...] += 0.5 * jnp.sum(per_token, axis=0, keepdims=True)
        dx_ref[...] = err * (1.0 / D_MODEL)

    tile = pl.BlockSpec((tm, D_MODEL), lambda i: (i, 0))
    return pl.pallas_call(
        body, name="loss_grad", grid=(seq // tm,),
        in_specs=[tile, tile],
        out_specs=[pl.BlockSpec((1, 1), lambda i: (0, 0)), tile],
        out_shape=[jax.ShapeDtypeStruct((1, 1), F32), jax.ShapeDtypeStruct((seq, D_MODEL), F32)],
        compiler_params=_params(dimension_semantics=("arbitrary",)),
    )(x_out, target)


def _stage_copy(stage, sems, dst, slot, step, where):
    rows, cols = where(step)
    return pltpu.make_async_copy(stage.at[slot], dst.at[rows, cols], sems.at[slot])


def _stage_begin(stage, sems, dst, step, where):
    slot = step % 2

    @pl.when(step >= 2)
    def _():
        _stage_copy(stage, sems, dst, slot, step - 2, where).wait()

    return slot


def _stage_end(stage, sems, dst, step, n_steps, where):
    slot = step % 2
    _stage_copy(stage, sems, dst, slot, step, where).start()

    @pl.when(step == n_steps - 1)
    def _():
        _stage_copy(stage, sems, dst, slot, step, where).wait()
        if n_steps > 1:
            _stage_copy(stage, sems, dst, 1 - slot, step - 1, where).wait()


def _merge_bwd(dx, y, ba, bb, z, wpo_g, who_g, wout_g, gate, g_post, dz, tm, name):
    seq = dx.shape[0]
    n_steps = seq // tm

    def body(dx_ref, y_ref, ba_ref, bb_ref, mgp_ref, mgh_ref, wpo_ref, who_ref, wout_ref, gate_ref, gp_ref, _,
             dy_ref, dba_ref, dbb_ref, da_ref, db_ref, dz_ref, acc_ref, stage, sems):
        step = pl.program_id(0)

        @pl.when(step == 0)
        def _():
            acc_ref[...] = jnp.zeros_like(acc_ref)

        def where(t):
            return pl.ds(t * tm, tm), pl.ds(COL_MGP * D_MODEL, 2 * D_MODEL)

        dmg_ref = stage.at[_stage_begin(stage, sems, dz_ref, step, where)]

        dxv = dx_ref[...]
        rs, yn = _rms_parts(y_ref[...])
        acc_ref[0:1, :] += jnp.sum(dxv * yn * gp_ref[...], axis=0, keepdims=True)
        acc_ref[1:2, :] += jnp.sum(dxv * gate_ref[...] * yn, axis=0, keepdims=True)
        dyn = dxv * (gate_ref[...] * gp_ref[...])
        dy = rs * (dyn - yn * jnp.mean(dyn * yn, axis=-1, keepdims=True))
        dmerged = _dot_nt(dy, wout_ref[...])
        sp, sh = _sigmoid(mgp_ref[...]), _sigmoid(mgh_ref[...])
        dba, dbb = sp * dmerged, sh * dmerged
        dmg_ref[:, 0:D_MODEL] = (dmerged * ba_ref[...].astype(F32) * sp * (1.0 - sp)).astype(dmg_ref.dtype)
        dmg_ref[:, D_MODEL:2 * D_MODEL] = (dmerged * bb_ref[...].astype(F32) * sh * (1.0 - sh)).astype(dmg_ref.dtype)
        da = _dot_nt(dba[:, 0:GROUP_DIM], wpo_ref[0])
        for j in range(1, N_DEV):
            da += _dot_nt(dba[:, j * GROUP_DIM:(j + 1) * GROUP_DIM], wpo_ref[j])
        dy_ref[...] = dy.astype(dy_ref.dtype)
        dba_ref[...] = dba.astype(dba_ref.dtype)
        dbb_ref[...] = dbb.astype(dbb_ref.dtype)
        da_ref[...] = da
        db_ref[...] = _dot_nt(dbb, who_ref[...])
        _stage_end(stage, sems, dz_ref, step, n_steps, where)

    def tile(cols=D_MODEL, block=0):
        return pl.BlockSpec((tm, cols), lambda i: (i, block))

    full = pl.BlockSpec((D_MODEL, D_MODEL), lambda i: (0, 0))
    hbm = pl.BlockSpec(memory_space=pl.ANY)
    act = jax.ShapeDtypeStruct((seq, D_MODEL), MXU_DTYPE)
    return pl.pallas_call(
        body, name=name, grid=(n_steps,),
        in_specs=[tile(), tile(), tile(), tile(), tile(block=COL_MGP), tile(block=COL_MGH),
                  pl.BlockSpec((N_DEV, POOL_WIDTH, GROUP_DIM), lambda i: (0, 0, 0)),
                  full, full, _row_spec(), _row_spec(), hbm],
        out_specs=[tile(), tile(), tile(), tile(POOL_WIDTH), tile(), hbm,
                   pl.BlockSpec((8, D_MODEL), lambda i: (0, 0))],
        out_shape=[act, act, act, jax.ShapeDtypeStruct((seq, POOL_WIDTH), F32),
                   jax.ShapeDtypeStruct((seq, D_MODEL), F32),
                   jax.ShapeDtypeStruct(dz.shape, dz.dtype),
                   jax.ShapeDtypeStruct((8, D_MODEL), F32)],
        input_output_aliases={11: 5},
        scratch_shapes=[pltpu.VMEM((2, tm, 2 * D_MODEL), MXU_DTYPE), pltpu.SemaphoreType.DMA((2,))],
        compiler_params=_params(dimension_semantics=("arbitrary",)),
    )(dx, y, ba, bb, z, z, wpo_g, who_g, wout_g, gate, g_post, dz)


def _grad_out_weights(merged, dy, b_in, dbb, a_in, dba, name):
    seq = merged.shape[0]
    tn = D_MODEL // 2
    per_step = tn // GROUP_DIM

    def body(mer_ref, dy_ref, b_ref, dbb_ref, a_ref, dba_ref, gout_ref, gho_ref, gpo_ref):
        gout_ref[...] = _dot_tn(mer_ref[...], dy_ref[...]).astype(gout_ref.dtype)
        gho_ref[...] = _dot_tn(b_ref[...], dbb_ref[...]).astype(gho_ref.dtype)
        g_po = _dot_tn(a_ref[...], dba_ref[...])
        for j in range(per_step):
            gpo_ref[j] = g_po[:, j * GROUP_DIM:(j + 1) * GROUP_DIM].astype(gpo_ref.dtype)

    def whole(cols):
        return pl.BlockSpec((seq, cols), lambda j: (0, 0))

    cols = pl.BlockSpec((seq, tn), lambda j: (0, j))
    return pl.pallas_call(
        body, name=name, grid=(D_MODEL // tn,),
        in_specs=[whole(D_MODEL), cols, whole(D_MODEL), cols, whole(POOL_WIDTH), cols],
        out_specs=[pl.BlockSpec((D_MODEL, tn), lambda j: (0, j)), pl.BlockSpec((D_MODEL, tn), lambda j: (0, j)),
                   pl.BlockSpec((per_step, POOL_WIDTH, GROUP_DIM), lambda j: (j, 0, 0))],
        out_shape=[jax.ShapeDtypeStruct((D_MODEL, D_MODEL), WIRE_DTYPE),
                   jax.ShapeDtypeStruct((D_MODEL, D_MODEL), WIRE_DTYPE),
                   jax.ShapeDtypeStruct((N_DEV, POOL_WIDTH, GROUP_DIM), WIRE_DTYPE)],
        compiler_params=_params(dimension_semantics=("parallel",)),
    )(merged, dy, b_in, dbb, a_in, dba)


def _hgrn_bwd(db_in, z, o, states, lb_l, gn_l, dz, name, after=None):
    seq = z.shape[0]
    per_step = min(BWD_STEP_CHUNKS, seq // CHUNK)
    rows_per_step = per_step * CHUNK
    n_steps = seq // rows_per_step
    last_step = n_steps - 1

    def body(db_ref, hq_ref, hf_ref, hi_ref, hg_ref, o_ref, st_ref, lb_ref, gn_ref, _,
             dz_hbm, dlb_ref, dgn_ref, dstate, dq_buf, dk_buf, dg_buf, stage, sems):
        step = pl.program_id(0)

        @pl.when(step == 0)
        def _():
            dstate[...] = jnp.zeros_like(dstate)
            dlb_ref[...] = jnp.zeros_like(dlb_ref)
            dgn_ref[...] = jnp.zeros_like(dgn_ref)

        def one_chunk(cc, *args):
            one_chunk_body((db_ref, hq_ref, hf_ref, hi_ref, hg_ref, o_ref, st_ref, dlb_ref, dgn_ref, dstate,
                            dq_buf, dk_buf, dg_buf), cc, *args)

        def where(t):
            return pl.ds((last_step - t) * rows_per_step, rows_per_step), pl.ds(COL_HQ * D_MODEL, 4 * D_MODEL)

        dz_step = stage.at[_stage_begin(stage, sems, dz_hbm, step, where)]
        causal, before_sub, suffix = _chunk_masks()
        lb = lb_ref[...]
        gn = gn_ref[...]
        for cc in reversed(range(per_step)):
            one_chunk(cc, dz_step, causal, before_sub, suffix, lb, gn)
        _stage_end(stage, sems, dz_hbm, step, n_steps, where)

    def one_chunk_body(refs, cc, dz_step, causal, before_sub, suffix, lb, gn):
        (db_ref, hq_ref, hf_ref, hi_ref, hg_ref, o_ref, st_ref, dlb_ref, dgn_ref, dstate, dq_buf, dk_buf, dg_buf) = refs
        rows = slice(cc * CHUNK, (cc + 1) * CHUNK)
        dz_ref = dz_step.at[rows, :]
        dq_buf, dk_buf, dg_buf = dq_buf.at[cc], dk_buf.at[cc], dg_buf.at[cc]
        sg, f, logf = _gates(hf_ref[rows, :], lb)
        kk = 1.0 - f
        hq = hq_ref[rows, :]
        sq = _sigmoid(hq)
        q = hq * sq
        cum, base = _masked_sums([causal, before_sub], logf)
        dgn = jnp.zeros((1, HEAD_DIM), F32)
        dlast = []
        for h in range(HEADS):
            sl = slice(h * HEAD_DIM, (h + 1) * HEAD_DIM)
            q_h, k_h, cum_h = q[:, sl], kk[:, sl], cum[:, sl]
            v_h = hi_ref[rows, sl]
            st_h = st_ref[cc, h]
            dst_h = dstate[h]
            rs, ohat = _rms_parts(o_ref[rows, sl])
            hg = hg_ref[rows, sl]
            shg = _sigmoid(hg)
            d_bin = db_ref[rows, sl]
            don = d_bin * (hg * shg)
            dgn += jnp.sum(don * ohat, axis=0, keepdims=True)
            dohat = don * gn
            do = rs * (dohat - ohat * jnp.mean(dohat * ohat, axis=-1, keepdims=True))
            dz_ref[:, 3 * D_MODEL + h * HEAD_DIM:3 * D_MODEL + (h + 1) * HEAD_DIM] = (
                d_bin * (ohat * gn) * _dsilu(hg, shg)).astype(dz_ref.dtype)
            last = jnp.sum(logf[:, sl], axis=0, keepdims=True)
            g_in = jnp.exp(cum_h)
            d_out = jnp.exp(last - cum_h)
            q_bar, k_bar = q_h * g_in, k_h * d_out
            blocks = _intra_blocks(q_h, k_h, cum_h, base[:, sl], causal)
            a = jnp.concatenate([b[4] for b in blocks], axis=0)
            da = jnp.where(causal, _dot_nt(do, v_h), 0.0)
            dv = _dot_tn(a, do) + _dot_nt(k_bar, dst_h)
            dq_bar, dk_bar = _dot(do, st_h), _dot(v_h, dst_h)
            dk = dk_bar * d_out
            dq_parts, dg_parts = [], []
            dg_k = k_bar * dk_bar
            dlast.append(jnp.sum(k_bar * dk_bar, axis=0, keepdims=True)
                         + jnp.exp(last) * jnp.sum(st_h * dst_h, axis=0, keepdims=True))
            for i, (q_t, k_t, e_q, e_k, _) in enumerate(blocks):
                da_i = da[i * SUB:(i + 1) * SUB].astype(MXU_DTYPE)
                dq_t = _dot(da_i, k_t)
                dk_t = _dot_tn(da_i, q_t)
                dq_parts.append(dq_t * e_q)
                dk += dk_t * e_k
                dg_parts.append(q_t.astype(F32) * dq_t)
                dg_k += k_t.astype(F32) * dk_t
            dq = dq_bar * g_in + jnp.concatenate(dq_parts, axis=0)
            dg_buf[:, sl] = q_bar * dq_bar + jnp.concatenate(dg_parts, axis=0) - dg_k
            dstate[h] = dst_h * jnp.exp(last) + _dot_tn(do, q_bar)
            dq_buf[:, sl] = dq
            dk_buf[:, sl] = dk
            dz_ref[:, 2 * D_MODEL + h * HEAD_DIM:2 * D_MODEL + (h + 1) * HEAD_DIM] = dv.astype(dz_ref.dtype)
        dgn_ref[...] += dgn
        dq_all, dk_all = dq_buf[...], dk_buf[...]
        dlogf = _masked_sums([suffix], dg_buf[...])[0] + jnp.concatenate(dlast, axis=1)
        df = jnp.where(f > LOG_FLOOR, dlogf / f, 0.0) - dk_all
        dlb_ref[...] += jnp.sum(df * (1.0 - sg), axis=0, keepdims=True)
        dz_ref[:, 0:D_MODEL] = (dq_all * _dsilu(hq, sq)).astype(dz_ref.dtype)
        dz_ref[:, D_MODEL:2 * D_MODEL] = (df * (1.0 - lb) * sg * (1.0 - sg)).astype(dz_ref.dtype)

    def col(block):
        return pl.BlockSpec((rows_per_step, D_MODEL), lambda c: (last_step - c, block))

    hbm = pl.BlockSpec(memory_space=pl.ANY)
    return _pallas_after(
        body, 10, after, name=name, grid=(n_steps,),
        in_specs=[col(0), col(COL_HQ), col(COL_HF), col(COL_HI), col(COL_HG), col(0),
                  pl.BlockSpec((per_step, HEADS, HEAD_DIM, HEAD_DIM), lambda c: (last_step - c, 0, 0, 0)),
                  _row_spec(), _row_spec(HEAD_DIM), hbm],
        out_specs=[hbm, _row_spec(), _row_spec(HEAD_DIM)],
        out_shape=[jax.ShapeDtypeStruct(dz.shape, dz.dtype),
                   jax.ShapeDtypeStruct((1, D_MODEL), F32), jax.ShapeDtypeStruct((1, HEAD_DIM), F32)],
        input_output_aliases={9: 0},
        scratch_shapes=[pltpu.VMEM((HEADS, HEAD_DIM, HEAD_DIM), F32)]
        + [pltpu.VMEM((per_step, CHUNK, D_MODEL), F32)] * 3
        + [pltpu.VMEM((2, rows_per_step, 4 * D_MODEL), MXU_DTYPE), pltpu.SemaphoreType.DMA((2,))],
        compiler_params=_params(dimension_semantics=("arbitrary",)),
    )(db_in, z, z, z, z, o, states, lb_l, gn_l, dz)


def _pool_bwd(da_in, z, pool_w_l, pool_scale_l, dz, name, after=None):
    seq = z.shape[0]

    def body(da_ref, pv_ref, pg_ref, w_ref, sc_ref, _, dz_hbm, dw_ref, dsc_ref, stage_pv, stage_pg, sems_pv, sems_pg):
        g = pl.program_id(0)

        def where_pv(t):
            return pl.ds(0, seq), pl.ds(pl.multiple_of(t * GROUP_DIM, GROUP_DIM), GROUP_DIM)

        def where_pg(t):
            return pl.ds(0, seq), pl.ds(pl.multiple_of(POOL_WIDTH + t * GROUP_DIM, GROUP_DIM), GROUP_DIM)

        dpv_ref = stage_pv.at[_stage_begin(stage_pv, sems_pv, dz_hbm, g, where_pv)]
        dpg_ref = stage_pg.at[_stage_begin(stage_pg, sems_pg, dz_hbm, g, where_pg)]
        pos = lax.broadcasted_iota(jnp.int32, (seq, GROUP_DIM), 0)
        pm, count = _pool_mean_minus_token(pv_ref[...], g, pos)
        lin0 = _dot(pm, w_ref[...])
        pg = pg_ref[...]
        spg = _sigmoid(pg)
        da = da_ref[...]
        dlin = da * (pg * spg)
        dpg_ref[...] = (da * (lin0 * sc_ref[...]) * _dsilu(pg, spg)).astype(dpg_ref.dtype)
        dsc_ref[...] = jnp.sum(dlin * lin0, axis=0, keepdims=True)
        dl0 = dlin * sc_ref[...]
        dw_ref[...] = _dot_tn(pm, dl0)
        dpm = _dot_nt(dl0, w_ref[...])
        sums, acc = [], dpm / count
        for j in (1, 2, 4, 8):
            acc = acc + _shift_up(acc, j, pos, seq)
            sums.append(acc)
        dpv_ref[...] = (_select_window(g, sums) - dpm).astype(dpv_ref.dtype)
        _stage_end(stage_pv, sems_pv, dz_hbm, g, POOL_GROUPS, where_pv)
        _stage_end(stage_pg, sems_pg, dz_hbm, g, POOL_GROUPS, where_pg)

    grp = pl.BlockSpec((seq, GROUP_DIM), lambda g: (0, g))
    hbm = pl.BlockSpec(memory_space=pl.ANY)
    stage = pltpu.VMEM((2, seq, GROUP_DIM), MXU_DTYPE)
    return _pallas_after(
        body, 6, after, name=name, grid=(POOL_GROUPS,),
        in_specs=[grp, grp, pl.BlockSpec((seq, GROUP_DIM), lambda g: (0, POOL_GROUPS + g)),
                  pl.BlockSpec((None, GROUP_DIM, GROUP_DIM), lambda g: (g, 0, 0)),
                  pl.BlockSpec((1, GROUP_DIM), lambda g: (0, g)), hbm],
        out_specs=[hbm, pl.BlockSpec((None, GROUP_DIM, GROUP_DIM), lambda g: (g, 0, 0)),
                   pl.BlockSpec((1, GROUP_DIM), lambda g: (0, g))],
        out_shape=[jax.ShapeDtypeStruct(dz.shape, dz.dtype),
                   jax.ShapeDtypeStruct((POOL_GROUPS, GROUP_DIM, GROUP_DIM), F32),
                   jax.ShapeDtypeStruct((1, POOL_WIDTH), F32)],
        input_output_aliases={5: 0},
        scratch_shapes=[stage, stage, pltpu.SemaphoreType.DMA((2,)), pltpu.SemaphoreType.DMA((2,))],
        compiler_params=_params(dimension_semantics=("arbitrary",)),
    )(da_in, z, z, pool_w_l, pool_scale_l, dz)


def _in_proj_dw(h, dz, name, after=None):
    seq = h.shape[0]

    def body(h_ref, dz_ref, out_ref):
        pair = lax.dot_general(h_ref[...], dz_ref[...], (((0,), (0,)), ((), ())), preferred_element_type=F32)
        out_ref[0] = pair[:, 0:IN_COLS].astype(out_ref.dtype)
        out_ref[1] = pair[:, IN_COLS:].astype(out_ref.dtype)

    return _pallas_after(
        body, 2, after, name=name, grid=(N_DEV // 2,),
        in_specs=[pl.BlockSpec((seq, D_MODEL), lambda j: (0, 0)),
                  pl.BlockSpec((seq, 2 * IN_COLS), lambda j: (0, j))],
        out_specs=pl.BlockSpec((2, D_MODEL, IN_COLS), lambda j: (j, 0, 0)),
        out_shape=jax.ShapeDtypeStruct((N_DEV, D_MODEL, IN_COLS), WIRE_DTYPE),
        compiler_params=_params(dimension_semantics=("parallel",)),
    )(h, dz)


def _in_proj_dh(dz, win_g, tm, name, after=None):
    seq = dz.shape[0]

    def body(dz_ref, w_ref, dh_ref):
        @pl.when(pl.program_id(1) == 0)
        def _():
            dh_ref[...] = jnp.zeros_like(dh_ref)

        w_pair = jnp.concatenate([w_ref[0], w_ref[1]], axis=1)
        dh_ref[...] += lax.dot_general(dz_ref[...], w_pair, (((1,), (1,)), ((), ())), preferred_element_type=F32)

    return _pallas_after(
        body, 2, after, name=name, grid=(seq // tm, N_DEV // 2),
        in_specs=[pl.BlockSpec((tm, 2 * IN_COLS), lambda i, j: (i, j)),
                  pl.BlockSpec((2, D_MODEL, IN_COLS), lambda i, j: (j, 0, 0))],
        out_specs=pl.BlockSpec((tm, D_MODEL), lambda i, j: (i, 0)),
        out_shape=jax.ShapeDtypeStruct((seq, D_MODEL), F32),
        compiler_params=_params(dimension_semantics=("parallel", "arbitrary")),
    )(dz, win_g)


def _prenorm_bwd(x, dh, dx_res, g, scale, tm, name, after=None):
    seq = x.shape[0]

    def body(x_ref, dh_ref, dxr_ref, g_ref, sc_ref, dx_ref, acc_ref):
        @pl.when(pl.program_id(0) == 0)
        def _():
            acc_ref[...] = jnp.zeros_like(acc_ref)

        rs, xn = _rms_parts(x_ref[...])
        dh = dh_ref[...]
        acc_ref[0:1, :] += jnp.sum(dh, axis=0, keepdims=True)
        acc_ref[1:2, :] += jnp.sum(dh * (xn * g_ref[...]), axis=0, keepdims=True)
        dhn = dh * (1.0 + sc_ref[...])
        acc_ref[2:3, :] += jnp.sum(dhn * xn, axis=0, keepdims=True)
        dxn = dhn * g_ref[...]
        dx_ref[...] = rs * (dxn - xn * jnp.mean(dxn * xn, axis=-1, keepdims=True)) + dxr_ref[...]

    tile = pl.BlockSpec((tm, D_MODEL), lambda i: (i, 0))
    return _pallas_after(
        body, 5, after, name=name, grid=(seq // tm,),
        in_specs=[tile, tile, tile, _row_spec(), _row_spec()],
        out_specs=[tile, pl.BlockSpec((8, D_MODEL), lambda i: (0, 0))],
        out_shape=[jax.ShapeDtypeStruct((seq, D_MODEL), F32), jax.ShapeDtypeStruct((8, D_MODEL), F32)],
        compiler_params=_params(dimension_semantics=("arbitrary",)),
    )(x, dh, dx_res, g, scale)


def _adamw_math(w, g, m, v):
    m = ADAM_B1 * m + (1.0 - ADAM_B1) * g
    v = ADAM_B2 * v + (1.0 - ADAM_B2) * (g * g)
    m_hat = m / (1.0 - ADAM_B1 ** ADAM_STEP)
    v_hat = v / (1.0 - ADAM_B2 ** ADAM_STEP)
    delta = -ADAM_LR * (m_hat / (jnp.sqrt(v_hat) + ADAM_EPS) + ADAM_WD * w)
    return delta, m, v


def _adamw_sharded(w, m, v, contrib, tr, name):
    depth, rows, cols = w.shape
    n_parts = contrib.shape[1]

    def body(w_ref, m_ref, v_ref, c_ref, g_ref, d_ref, mo_ref, vo_ref):
        g = c_ref[0].astype(F32)
        for p in range(1, n_parts):
            g += c_ref[p].astype(F32)
        delta, mn, vn = _adamw_math(w_ref[...], g, m_ref[...], v_ref[...])
        g_ref[...] = g
        d_ref[...] = delta
        mo_ref[...] = mn
        vo_ref[...] = vn

    tile = pl.BlockSpec((None, tr, cols), lambda l, i: (l, i, 0))
    shape = jax.ShapeDtypeStruct(w.shape, F32)
    return pl.pallas_call(
        body, name=name, grid=(depth, rows // tr),
        in_specs=[tile, tile, tile, pl.BlockSpec((None, n_parts, tr, cols), lambda l, i: (l, 0, i, 0))],
        out_specs=[tile] * 4, out_shape=[shape] * 4,
        compiler_params=_params(dimension_semantics=("parallel", "parallel")),
    )(w, m, v, contrib)


def _adamw_layer(w, m, v, contribs, l, tr, name, prev=None):
    _, rows, cols = w.shape
    n = len(contribs)

    def body(*refs):
        w_ref, m_ref, v_ref = refs[:3]
        c_refs = refs[3:3 + n]
        g_ref, d_ref, mo_ref, vo_ref = refs[-4:]
        g = c_refs[0][...].astype(F32)
        for c_ref in c_refs[1:]:
            g += c_ref[...].astype(F32)
        delta, mn, vn = _adamw_math(w_ref[...], g, m_ref[...], v_ref[...])
        g_ref[...] = g
        d_ref[...] = delta
        mo_ref[...] = mn
        vo_ref[...] = vn

    tile = pl.BlockSpec((None, tr, cols), lambda i: (l, i, 0))
    in_specs = [tile, tile, tile] + [pl.BlockSpec((None, tr, cols), lambda i, s=slot: (s, i, 0)) for _, slot in contribs]
    operands = [w, m, v] + [arr for arr, _ in contribs]
    aliases = {}
    if prev is not None:
        aliases = {len(operands) + k: k for k in range(4)}
        in_specs += [pl.BlockSpec(memory_space=pl.ANY)] * 4
        operands += list(prev)
    shape = jax.ShapeDtypeStruct(w.shape, F32)
    return pl.pallas_call(
        body, name=name, grid=(rows // tr,), in_specs=in_specs, out_specs=[tile] * 4, out_shape=[shape] * 4,
        input_output_aliases=aliases,
        compiler_params=_params(dimension_semantics=("parallel",)),
    )(*operands)


def _adamw_small(w_pack, m_pack, v_pack, g_late, g_early, shapes):
    pieces, r = {}, 0
    for name, _, n in _SMALL_ROWS:
        pieces.setdefault(name, []).append((r, n))
        r += n
    names = list(pieces)

    def body(w_ref, m_ref, v_ref, gl_ref, ge_ref, *rest):
        outs, packs = rest[:4 * len(names)], rest[4 * len(names):]
        g_l, g_e = gl_ref[0][0:SMALL_LATE_ROWS], ge_ref[0]
        for d in range(1, N_DEV):
            g_l += gl_ref[d][0:SMALL_LATE_ROWS]
            g_e += ge_ref[d]
        g = jnp.concatenate([g_l, g_e], axis=0)
        w = w_ref[...]
        r0, r1, r2 = LB_ROW0, LB_ROW0 + 8, LB_ROW0 + 16
        lg0, lg1 = w[r0:r1], w[r1:r2]
        mx = jnp.maximum(lg0, lg1)
        e0, e1 = jnp.exp(lg0 - mx), jnp.exp(lg1 - mx)
        p0, p1 = e0 / (e0 + e1), e1 / (e0 + e1)
        low = ((p0 - p0), (p0 + p1) - p0)
        dlow = [g_rows * jnp.where((lo > 0.0) & (lo < 1.0), 1.0, jnp.where((lo == 0.0) | (lo == 1.0), 0.5, 0.0))
                for g_rows, lo in ((g[r0:r1], low[0]), (g[r1:r2], low[1]))]
        dp0 = (dlow[0] + dlow[1]) - (dlow[0] + dlow[1])
        dp1 = dlow[1]
        inner = p0 * dp0 + p1 * dp1
        g = jnp.concatenate([g[:r0], p0 * (dp0 - inner), p1 * (dp1 - inner), g[r2:]], axis=0)
        delta, mn, vn = _adamw_math(w, g, m_ref[...], v_ref[...])
        for kind, val in enumerate((g, delta, mn, vn)):
            packs[kind][...] = val
            for j, name in enumerate(names):
                out, at = outs[kind * len(names) + j], 0
                for start, n in pieces[name]:
                    if name in flat:
                        for r in range(n):
                            layer, c = divmod(at + r, flat[name])
                            out[layer:layer + 1, c * 128:(c + 1) * 128] = packs[kind][start + r:start + r + 1, :]
                    else:
                        out[at:at + n, :] = packs[kind][start:start + n, :]
                    at += n

    rows = {name: sum(n for _, n in pieces[name]) for name in names}
    flat = {name: rows[name] // DEPTH for name in names if len(shapes[name]) == 2}
    outs = pl.pallas_call(
        body, name="adamw_small",
        out_shape=[jax.ShapeDtypeStruct(shapes[name] if name in flat else (rows[name], 128), F32)
                   for _ in range(4) for name in names],
        scratch_shapes=[pltpu.VMEM(w_pack.shape, F32)] * 4, compiler_params=_params(),
    )(w_pack, m_pack, v_pack, g_late, g_early)
    return [{name: outs[kind * len(names) + j].reshape(shapes[name]) for j, name in enumerate(names)}
            for kind in range(4)]


def _pack_small(parts, first=0, last=len(_SMALL_ROWS)):
    rows = [(parts[name] if l is None else parts[name][l]).reshape(n, 128) for name, l, n in _SMALL_ROWS[first:last]]
    if last == len(_SMALL_ROWS):
        rows.append(jnp.zeros((SMALL_ROWS_PAD - sum(n for _, _, n in _SMALL_ROWS), 128), F32))
    return jnp.concatenate(rows, axis=0)


def kernel(x, c, w_ada, b_ada, g_pre, g_post, w_in, pool_w, pool_scale, lb_logits, hgrn_norm_g, w_pool_o, w_hgrn_o, w_out, loss_target, m_w_ada, m_b_ada, m_g_pre, m_g_post, m_w_in, m_pool_w, m_pool_scale, m_lb_logits, m_hgrn_norm_g, m_w_pool_o, m_w_hgrn_o, m_w_out, v_w_ada, v_b_ada, v_g_pre, v_g_post, v_w_in, v_pool_w, v_pool_scale, v_lb_logits, v_hgrn_norm_g, v_w_pool_o, v_w_hgrn_o, v_w_out):
    seq = x.shape[1]
    tm = min(512, seq)
    tm_merge = min(256, seq)
    pos = _my_position()
    me = pos[3]

    c_all = _allgather_small(c, "allgather_c").reshape(N_DEV, D_MODEL)
    b_cols = lax.dynamic_slice_in_dim(b_ada, me * ADA_COLS, ADA_COLS, axis=1)
    ada_part = _ada_fwd(c_all, w_ada, b_cols)
    ada_all = _allgather_small(ada_part.reshape(DEPTH * N_DEV, ADA_COLS), "allgather_ada")
    ada = lax.dynamic_index_in_dim(ada_all.reshape(N_DEV, DEPTH, N_DEV, ADA_COLS), me, axis=2, keepdims=False)
    ada = jnp.transpose(ada, (1, 0, 2)).reshape(DEPTH, 3 * D_MODEL)
    shift = [ada[l:l + 1, 0:D_MODEL] for l in range(DEPTH)]
    scale = [ada[l:l + 1, D_MODEL:2 * D_MODEL] for l in range(DEPTH)]
    gate = [ada[l:l + 1, 2 * D_MODEL:] for l in range(DEPTH)]

    big = dict(win=w_in, wpo=w_pool_o, who=w_hgrn_o, wout=w_out)
    units = [["win0"], ["wpo0", "who0", "wout0"], ["win1", "wpo1", "who1", "wout1"]]
    g_streams = [_gather_streams(keys) for keys in units]
    g_state = [None] * len(units)

    def gather_start(u, after):
        bufs = {}
        for k in units[u]:
            arr = big[k[:-1]]
            bufs["s_" + k] = arr[int(k[-1])].astype(WIRE_DTYPE)
            bufs["g_" + k] = _with_own_slot(bufs["s_" + k], me)
        bufs, sems, token = _comm_call(f"gather_start_{u}", bufs, start=list(g_streams[u][:2]), after=after)
        g_state[u] = dict(bufs=bufs, sems=sems)
        return token

    def gather_pass(u, after):
        st = g_state[u]
        to_chips, _, pass_on = g_streams[u]
        st["bufs"], (st["pass_sems"],), _ = _comm_call(f"gather_pass_{u}", st["bufs"], start=[pass_on],
                                                       wait=[(to_chips, st["sems"][0])], after=after)

    def gather_done(u, after=None):
        st = g_state[u]
        _, to_sibling, pass_on = g_streams[u]
        bufs, _, _ = _comm_call(f"gather_done_{u}", st["bufs"], after=after,
                                wait=[(to_sibling, st["sems"][1]), (pass_on, st["pass_sems"])])
        return {k: bufs["g_" + k] for k in units[u]}

    token = gather_start(0, ada_all)

    lb = _lb_fwd(lb_logits)

    gw = {}
    xs, saved = [x[0]], []
    for l in range(DEPTH):
        h = _prenorm_fwd(xs[l], g_pre[l:l + 1], shift[l], scale[l], tm, f"prenorm_fwd_{l}",
                         after=token if l == 0 else None)
        token = None
        if l == 0:
            gather_pass(0, h)
            gw.update(gather_done(0))
            token = gather_start(1, gw["win0"])
        else:
            gather_pass(2, h)
            gw.update(gather_done(2))
        z = _in_proj(h, gw[f"win{l}"], min(1024, seq), f"in_proj_{l}", after=token)
        if l == 0:
            gather_pass(1, z)
            token = gather_start(2, g_state[1]["bufs"]["g_wpo0"])
        a_in = _pool_fwd(z, pool_w[l], pool_scale[l:l + 1], f"pool_fwd_{l}", after=token)
        o, b_in, states = _hgrn_fwd(z, lb[l:l + 1], hgrn_norm_g[l:l + 1], f"hgrn_fwd_{l}", after=token)
        if l == 0:
            gw.update(gather_done(1, b_in))
        who_l = gw[f"who{l}"].reshape(D_MODEL, D_MODEL)
        wout_l = gw[f"wout{l}"].reshape(D_MODEL, D_MODEL)
        ba, bb, merged, y, x_next = _merge_fwd(a_in, b_in, z, xs[l], gw[f"wpo{l}"], who_l, wout_l, gate[l],
                                               g_post[l:l + 1], tm_merge, f"merge_fwd_{l}")
        xs.append(x_next)
        saved.append((h, z, a_in, o, b_in, states, ba, bb, merged, y, who_l, wout_l))

    loss_part, dx = _loss_grad(xs[DEPTH], loss_target[0], tm)

    chips = _other_chips(pos)
    pair_idx = jnp.stack([_dev_index(cx, cy, pos[2]) for cx, cy in chips] + [me]).astype(jnp.int32)
    pair_rows = dict(win=256, wpo=POOL_WIDTH, who=HEAD_DIM, wout=HEAD_DIM)

    def scatter_pair_start(u, grads):
        keys = list(grads)
        pair, to_chips = _scatter_streams(keys)
        bufs = {}
        for k in keys:
            bufs["g_" + k] = grads[k]
            bufs["st_" + k] = lax.empty((4,) + grads[k].shape[1:], WIRE_DTYPE)
        bufs, (sems,), token = _comm_call(f"scatter_pair_start_{u}", bufs, start=[pair])
        return dict(u=u, keys=keys, pair=pair, to_chips=to_chips, bufs=bufs, sems=sems, token=token)

    def scatter_pair_finish(st, after):
        u, keys = st["u"], st["keys"]
        bufs, _, _ = _comm_call(f"scatter_pair_done_{u}", st["bufs"], wait=[(st["pair"], st["sems"])], after=after)
        bufs2 = {}
        for k in keys:
            bufs2["ps_" + k] = _pair_sum(bufs["g_" + k], bufs["st_" + k], pair_idx, bufs["g_" + k].shape[1],
                                         f"pair_sum_{k}")
            bufs2["ld_" + k] = lax.empty((3,) + bufs["g_" + k].shape[1:], WIRE_DTYPE)
        st.update(bufs=bufs2)

    def scatter_chips_start(st, after=None):
        bufs2, (sems,), token = _comm_call(f"scatter_chips_start_{st['u']}", st["bufs"], start=[st["to_chips"]],
                                           after=after)
        st.update(bufs=bufs2, sems=sems, token=token)

    def scatter_finish(st, after):
        bufs, _, _ = _comm_call(f"scatter_chips_done_{st['u']}", st["bufs"], wait=[(st["to_chips"], st["sems"])],
                                after=after)
        return {k: [(bufs["ps_" + k], 3), (bufs["ld_" + k], 0), (bufs["ld_" + k], 1), (bufs["ld_" + k], 2)]
                for k in st["keys"]}

    moments = dict(win=(m_w_in, v_w_in), wpo=(m_w_pool_o, v_w_pool_o), who=(m_w_hgrn_o, v_w_hgrn_o),
                   wout=(m_w_out, v_w_out))
    big_out = {}

    def finish_unit(unit, after):
        for k, contribs in scatter_finish(scat[unit], after).items():
            wname, l = k[:-1], int(k[-1])
            big_out[wname] = _adamw_layer(big[wname], moments[wname][0], moments[wname][1], contribs, l,
                                          pair_rows[wname], f"adamw_{k}", prev=big_out.get(wname))
            after = big_out[wname][0]
        return after

    d_ada, small, scat = [None] * DEPTH, [None] * DEPTH, {}
    for l in reversed(range(DEPTH)):
        h, z, a_in, o, b_in, states, ba, bb, merged, y, who_l, wout_l = saved[l]
        dy, dba, dbb, da_in, db_in, dz, acc_post = _merge_bwd(
            dx, y, ba, bb, z, gw[f"wpo{l}"], who_l, wout_l, gate[l], g_post[l:l + 1],
            lax.empty((seq, IN_WIDTH), MXU_DTYPE), tm_merge, f"merge_bwd_{l}")
        g_out, g_ho, g_po = _grad_out_weights(merged, dy, b_in, dbb, a_in, dba, f"grad_out_weights_{l}")
        g_small = {f"wout{l}": g_out.reshape(N_DEV, HEAD_DIM, D_MODEL),
                   f"who{l}": g_ho.reshape(N_DEV, HEAD_DIM, D_MODEL), f"wpo{l}": g_po}
        st_small = scat["small0"] = scatter_pair_start("small0", g_small) if l == 0 else None
        dz, dlb, dgn = _hgrn_bwd(db_in, z, o, states, lb[l:l + 1], hgrn_norm_g[l:l + 1], dz, f"hgrn_bwd_{l}",
                                 after=st_small and st_small["token"])
        if l == 0:
            scatter_pair_finish(st_small, dlb)
            scatter_chips_start(st_small)
        dz, dpw, dps = _pool_bwd(da_in, z, pool_w[l], pool_scale[l:l + 1], dz, f"pool_bwd_{l}",
                                 after=st_small and st_small["token"])
        small[l] = dict(g_post=acc_post[1], pool_w=dpw, pool_scale=dps[0], lb_logits=dlb[0], hgrn_norm_g=dgn[0])
        token = None
        if l == 0:
            parts = {name: jnp.stack([small[0][name], small[1][name]]) for name in small[0]}
            parts.update(b_ada=[None, d_ada[1]], g_pre=[None, small[1]["g_pre"]])
            sg_stream = _direct_gather_stream("sg")
            early = _pack_small(parts, 2)
            sg_bufs, (sg_sems,), token = _comm_call(
                "small_grads_start", dict(s_sg=early, g_sg=_with_own_slot(early, me)), start=[sg_stream])
        g_win = {f"win{l}": _in_proj_dw(h, dz, f"grad_w_in_{l}", after=token)}
        st_win = scat[f"win{l}"] = scatter_pair_start(f"win{l}", g_win if l == 0 else {**g_small, **g_win})
        if l > 0:
            dh = _in_proj_dh(dz, gw[f"win{l}"], seq, f"in_proj_dh_{l}", after=st_win["token"])
            scatter_pair_finish(st_win, dh)
            scatter_chips_start(st_win)
        else:
            scatter_pair_finish(st_win, st_win["token"])
            scatter_chips_start(st_win)
            after = st_win["token"]
            for unit in ("win1", "small0"):
                after = finish_unit(unit, after)
            dh = _in_proj_dh(dz, gw[f"win{l}"], seq, f"in_proj_dh_{l}", after=after)
        dx, acc_pre = _prenorm_bwd(xs[l], dh, dx, g_pre[l:l + 1], scale[l], tm, f"prenorm_bwd_{l}",
                                   after=st_win["token"])
        d_ada[l] = jnp.concatenate([acc_pre[0], acc_pre[1], acc_post[0]])
        small[l]["g_pre"] = acc_pre[2]
    grad_x = dx[None]

    parts = dict(b_ada=[d_ada[0]], g_pre=[small[0]["g_pre"]])
    late = jnp.concatenate([_pack_small(parts, 0, 2), jnp.broadcast_to(loss_part, (8, 128))], axis=0)
    g_late = _allgather_small(late, "allgather_late_grads")
    loss = jnp.sum(g_late[:, SMALL_LATE_ROWS, 0])
    sg_bufs, _, _ = _comm_call("small_grads_done", sg_bufs, wait=[(sg_stream, sg_sems)], after=g_late)
    g_early = sg_bufs["g_sg"]
    small_names = list(dict.fromkeys(name for name, _, _ in _SMALL_ROWS))
    weights = dict(b_ada=b_ada, g_pre=g_pre, g_post=g_post, pool_w=pool_w, pool_scale=pool_scale,
                   lb_logits=lb_logits, hgrn_norm_g=hgrn_norm_g)
    m_small = dict(b_ada=m_b_ada, g_pre=m_g_pre, g_post=m_g_post, pool_w=m_pool_w, pool_scale=m_pool_scale,
                   lb_logits=m_lb_logits, hgrn_norm_g=m_hgrn_norm_g)
    v_small = dict(b_ada=v_b_ada, g_pre=v_g_pre, g_post=v_g_post, pool_w=v_pool_w, pool_scale=v_pool_scale,
                   lb_logits=v_lb_logits, hgrn_norm_g=v_hgrn_norm_g)
    shapes = {name: weights[name].shape for name in small_names}
    small_out = _adamw_small(_pack_small(weights), _pack_small(m_small), _pack_small(v_small), g_late, g_early,
                             shapes)

    d_ada_all = jnp.stack([g_late[:, 0:24, :].reshape(N_DEV, 3 * D_MODEL),
                           g_early[:, 0:24, :].reshape(N_DEV, 3 * D_MODEL)], axis=1)
    d_cols = jnp.transpose(lax.dynamic_slice_in_dim(d_ada_all, me * ADA_COLS, ADA_COLS, axis=2), (1, 0, 2))
    g_w_ada = _ada_bwd(c_all, d_cols)
    ada_out = _adamw_sharded(w_ada, m_w_ada, v_w_ada, g_w_ada[:, None], 256, "adamw_w_ada")
    finish_unit("win0", ada_out[1][0, 0:8, 0:128] + small_out[1]["pool_scale"][0:1, 0:128])

    def leaf(kind):
        s = small_out[kind]
        return (ada_out[kind], s["b_ada"], s["g_pre"], s["g_post"], big_out["win"][kind], s["pool_w"], s["pool_scale"],
                s["lb_logits"], s["hgrn_norm_g"], big_out["wpo"][kind], big_out["who"][kind], big_out["wout"][kind])

    return (loss, grad_x) + leaf(0) + leaf(1) + leaf(2) + leaf(3)
```

```python
import jax
import jax.numpy as jnp
from jax import lax
from jax.experimental import pallas as pl
from jax.experimental.pallas import tpu as pltpu

F32 = jnp.float32
MXU_DTYPE = jnp.bfloat16
WIRE_DTYPE = jnp.bfloat16

N_DEV = 8
DEPTH = 2
D_MODEL = 1024
HEADS = 8
HEAD_DIM = 128
POOL_GROUPS = 4
GROUP_DIM = 128
POOL_WIDTH = POOL_GROUPS * GROUP_DIM
IN_WIDTH = 7168
CHUNK = 64
SUB = 16
N_SUB = CHUNK // SUB
FWD_STEP_CHUNKS = 8
BWD_STEP_CHUNKS = 4
EXP_CLAMP = 80.0
NORM_EPS = 1e-6
LOG_FLOOR = 1e-30
ADA_COLS = 3 * D_MODEL // N_DEV
IN_COLS = IN_WIDTH // N_DEV
COL_HQ, COL_HF, COL_HI, COL_HG, COL_MGP, COL_MGH = 1, 2, 3, 4, 5, 6

ADAM_LR = 0.001
ADAM_B1 = 0.9
ADAM_B2 = 0.999
ADAM_EPS = 1e-08
ADAM_WD = 0.01
ADAM_STEP = 10

VMEM_LIMIT = 48 * 1024 * 1024
MESH_ID = pl.DeviceIdType.MESH
HIGHEST = lax.Precision.HIGHEST

_SMALL_ROWS = (("b_ada", 0, 24), ("g_pre", 0, 8), ("b_ada", 1, 24), ("g_pre", 1, 8), ("g_post", None, 16),
               ("pool_w", None, 1024), ("pool_scale", None, 8), ("lb_logits", None, 16), ("hgrn_norm_g", None, 2))
SMALL_LATE_ROWS = 32
SMALL_ROWS_PAD = 1136
LB_ROW0 = 32 + 32 + 16 + 1024 + 8


def _params(**kw):
    return pltpu.CompilerParams(vmem_limit_bytes=VMEM_LIMIT, **kw)


def _sigmoid(v):
    return 1.0 / (1.0 + jnp.exp(-v))


def _dsilu(v, s):
    return s * (1.0 + v * (1.0 - s))


def _dot(a, b):
    return jnp.dot(a.astype(MXU_DTYPE), b.astype(MXU_DTYPE), preferred_element_type=F32)


def _dot_nt(a, b):
    return lax.dot_general(a.astype(MXU_DTYPE), b.astype(MXU_DTYPE), (((1,), (1,)), ((), ())),
                           preferred_element_type=F32)


def _dot_tn(a, b):
    return lax.dot_general(a.astype(MXU_DTYPE), b.astype(MXU_DTYPE), (((0,), (0,)), ((), ())),
                           preferred_element_type=F32)


def _pallas_after(body, n_in, after, *, in_specs, **kw):
    if after is None:
        return pl.pallas_call(body, in_specs=in_specs, **kw)

    def tied(*refs):
        body(*refs[:n_in], *refs[n_in + 1:])

    call = pl.pallas_call(tied, in_specs=list(in_specs) + [pl.BlockSpec(memory_space=pl.ANY)], **kw)
    return lambda *operands: call(*operands, after)


def _my_position():
    mx, my, mc = lax.axis_index("x"), lax.axis_index("y"), lax.axis_index("c")
    return mx, my, mc, 4 * mx + 2 * my + mc


def _peer(mx, my, mc, k):
    px = 1 - mx if (k >> 2) & 1 else mx
    py = 1 - my if (k >> 1) & 1 else my
    pc = 1 - mc if k & 1 else mc
    return (px, py, pc), 4 * px + 2 * py + pc


def _allgather_small(v, name, after=None):
    rows, cols = v.shape

    def body(v_ref, out_ref, send_sems, recv_sems):
        mx, my, mc, me = _my_position()
        out_ref[me] = v_ref[...]
        copies = []
        for k in range(1, N_DEV):
            peer, _ = _peer(mx, my, mc, k)
            cp = pltpu.make_async_remote_copy(
                src_ref=v_ref, dst_ref=out_ref.at[me],
                send_sem=send_sems.at[k - 1], recv_sem=recv_sems.at[k - 1],
                device_id=peer, device_id_type=MESH_ID)
            cp.start()
            copies.append(cp)
        for cp in copies:
            cp.wait()

    return _pallas_after(
        body, 1, after, name=name,
        out_shape=jax.ShapeDtypeStruct((N_DEV, rows, cols), v.dtype),
        in_specs=[pl.BlockSpec(memory_space=pltpu.VMEM)],
        out_specs=pl.BlockSpec(memory_space=pltpu.VMEM),
        scratch_shapes=[pltpu.SemaphoreType.DMA((N_DEV - 1,)), pltpu.SemaphoreType.DMA((N_DEV - 1,))],
        compiler_params=_params(),
    )(v)


class _Stream:
    def __init__(self, n, plan):
        self.n, self.plan = n, plan


def _comm_call(name, bufs, start=(), wait=(), after=None):
    names = list(bufs)

    def body(*refs):
        it = iter(refs)
        buf_refs = {n: next(it) for n in names}
        wait_sems = [(next(it), next(it)) for _ in wait]
        if after is not None:
            next(it)
        start_sems = [(next(it), next(it)) for _ in start]
        for _ in names:
            next(it)
        token = next(it)
        pos = _my_position()

        def descriptors(stream, sems):
            return [pltpu.make_async_remote_copy(src_ref=src, dst_ref=dst, send_sem=sems[0].at[k], recv_sem=sems[1].at[k],
                                                 device_id=dev, device_id_type=MESH_ID)
                    for k, (src, dst, dev) in enumerate(stream.plan(buf_refs, pos))]

        for (stream, _), sems in zip(wait, wait_sems):
            for cp in descriptors(stream, sems):
                cp.wait_send()
                cp.wait_recv()
        for stream, sems in zip(start, start_sems):
            for cp in descriptors(stream, sems):
                cp.start()
        token[...] = jnp.zeros_like(token)

    hbm = pl.BlockSpec(memory_space=pltpu.HBM)
    sem = pl.BlockSpec(memory_space=pltpu.SEMAPHORE)
    operands = [pltpu.with_memory_space_constraint(bufs[n], pltpu.HBM) for n in names]
    in_specs = [hbm] * len(names)
    for _, (send_sems, recv_sems) in wait:
        operands += [send_sems, recv_sems]
        in_specs += [sem, sem]
    if after is not None:
        operands.append(after)
        in_specs.append(pl.BlockSpec(memory_space=pl.ANY))
    out_shape, out_specs = [], []
    for stream in start:
        out_shape += [pltpu.SemaphoreType.DMA((stream.n,)), pltpu.SemaphoreType.DMA((stream.n,))]
        out_specs += [sem, sem]
    n_sem_out = len(out_shape)
    out_shape += [pltpu.HBM(bufs[n].shape, bufs[n].dtype) for n in names]
    out_specs += [hbm] * len(names)
    out_shape.append(jax.ShapeDtypeStruct((8, 128), F32))
    out_specs.append(pl.BlockSpec(memory_space=pltpu.VMEM))
    outs = pl.pallas_call(
        body, name=name, out_shape=out_shape, in_specs=in_specs, out_specs=out_specs,
        input_output_aliases={i: n_sem_out + i for i in range(len(names))},
        compiler_params=pltpu.CompilerParams(has_side_effects=pltpu.SideEffectType.DATAFLOW_SIDE_EFFECTING),
    )(*operands)
    sems = [(outs[2 * i], outs[2 * i + 1]) for i in range(len(start))]
    return dict(zip(names, outs[n_sem_out:n_sem_out + len(names)])), sems, outs[-1]


def _with_own_slot(block, me):
    return lax.dynamic_update_index_in_dim(lax.empty((N_DEV,) + block.shape, block.dtype), block, me, 0)


def _other_chips(pos):
    mx, my, _, _ = pos
    return [(1 - mx if i & 2 else mx, 1 - my if i & 1 else my) for i in (1, 2, 3)]


def _dev_index(px, py, pc):
    return 4 * px + 2 * py + pc


def _gather_streams(keys):
    def to_chips(refs, pos):
        _, _, mc, me = pos
        return [(refs["s_" + k], refs["g_" + k].at[me], (cx, cy, mc)) for k in keys for cx, cy in _other_chips(pos)]

    def to_sibling(refs, pos):
        mx, my, mc, me = pos
        return [(refs["s_" + k], refs["g_" + k].at[me], (mx, my, 1 - mc)) for k in keys]

    def pass_on(refs, pos):
        mx, my, mc, _ = pos
        out = []
        for k in keys:
            for cx, cy in _other_chips(pos):
                slot = refs["g_" + k].at[_dev_index(cx, cy, mc)]
                out.append((slot, slot, (mx, my, 1 - mc)))
        return out

    return _Stream(3 * len(keys), to_chips), _Stream(len(keys), to_sibling), _Stream(3 * len(keys), pass_on)


def _direct_gather_stream(key):
    def plan(refs, pos):
        mx, my, mc, me = pos
        return [(refs["s_" + key], refs["g_" + key].at[me], _peer(mx, my, mc, k)[0]) for k in range(1, N_DEV)]

    return _Stream(N_DEV - 1, plan)


def _scatter_streams(keys):
    def pair(refs, pos):
        mx, my, mc, _ = pos
        sib = (mx, my, 1 - mc)
        out = []
        for k in keys:
            for i, (cx, cy) in enumerate(_other_chips(pos)):
                out.append((refs["g_" + k].at[_dev_index(cx, cy, 1 - mc)], refs["st_" + k].at[i], sib))
            out.append((refs["g_" + k].at[_dev_index(mx, my, 1 - mc)], refs["st_" + k].at[3], sib))
        return out

    def chips(refs, pos):
        mc = pos[2]
        return [(refs["ps_" + k].at[i], refs["ld_" + k].at[i], (cx, cy, mc))
                for k in keys for i, (cx, cy) in enumerate(_other_chips(pos))]

    return _Stream(4 * len(keys), pair), _Stream(3 * len(keys), chips)


def _pair_sum(g, st, idx, tr, name):
    _, rows, cols = g.shape

    def body(idx_ref, g_ref, st_ref, out_ref):
        out_ref[...] = (g_ref[...].astype(F32) + st_ref[...].astype(F32)).astype(out_ref.dtype)

    return pl.pallas_call(
        body, name=name,
        grid_spec=pltpu.PrefetchScalarGridSpec(
            num_scalar_prefetch=1, grid=(4, rows // tr),
            in_specs=[pl.BlockSpec((None, tr, cols), lambda j, i, idx_ref: (idx_ref[j], i, 0)),
                      pl.BlockSpec((None, tr, cols), lambda j, i, idx_ref: (j, i, 0))],
            out_specs=pl.BlockSpec((None, tr, cols), lambda j, i, idx_ref: (j, i, 0))),
        out_shape=jax.ShapeDtypeStruct((4, rows, cols), WIRE_DTYPE),
        compiler_params=_params(dimension_semantics=("parallel", "parallel")),
    )(idx, g, st)


def _ada_fwd(c_all, w_ada, b_cols):
    def body(c_ref, w_ref, b_ref, out_ref):
        cv = c_ref[...]
        ca = cv * _sigmoid(cv)
        for l in range(DEPTH):
            out_ref[l] = jnp.dot(ca, w_ref[l], precision=HIGHEST, preferred_element_type=F32) + b_ref[l:l + 1, :]

    return pl.pallas_call(
        body, name="ada_fwd",
        out_shape=jax.ShapeDtypeStruct((DEPTH, N_DEV, ADA_COLS), F32),
        compiler_params=_params(),
    )(c_all, w_ada, b_cols)


def _ada_bwd(c_all, d_cols):
    def body(c_ref, d_ref, out_ref):
        cv = c_ref[...]
        ca = cv * _sigmoid(cv)
        for l in range(DEPTH):
            out_ref[l] = lax.dot_general(ca, d_ref[l], (((0,), (0,)), ((), ())), precision=HIGHEST,
                                         preferred_element_type=F32)

    return pl.pallas_call(
        body, name="ada_bwd",
        out_shape=jax.ShapeDtypeStruct((DEPTH, D_MODEL, ADA_COLS), F32),
        compiler_params=_params(),
    )(c_all, d_cols)


def _lower_bounds(logits):
    m = jnp.maximum(logits[0:1], logits[1:2])
    e0, e1 = jnp.exp(logits[0:1] - m), jnp.exp(logits[1:2] - m)
    den = e0 + e1
    p0, p1 = e0 / den, e1 / den
    low0 = p0 - p0
    low1 = (p0 + p1) - p0
    return (p0, p1), (low0, low1)


def _lb_fwd(lb_logits):
    def body(lg_ref, out_ref):
        _, (low0, low1) = _lower_bounds(lg_ref[...])
        out_ref[0:1, :] = jnp.clip(low0, 0.0, 1.0)
        out_ref[1:2, :] = jnp.clip(low1, 0.0, 1.0)

    return pl.pallas_call(body, name="lb_fwd", out_shape=jax.ShapeDtypeStruct(lb_logits.shape, F32),
                          compiler_params=_params())(lb_logits)


def _row_spec(cols=D_MODEL):
    return pl.BlockSpec((1, cols), lambda *_: (0, 0))


def _prenorm_fwd(x, g, shift, scale, tm, name, after=None):
    seq = x.shape[0]

    def body(x_ref, g_ref, sh_ref, sc_ref, h_ref):
        xv = x_ref[...]
        rs = lax.rsqrt(jnp.mean(xv * xv, axis=-1, keepdims=True) + NORM_EPS)
        h = (xv * rs * g_ref[...]) * (1.0 + sc_ref[...]) + sh_ref[...]
        h_ref[...] = h.astype(h_ref.dtype)

    tile = pl.BlockSpec((tm, D_MODEL), lambda i: (i, 0))
    return _pallas_after(
        body, 4, after, name=name, grid=(seq // tm,),
        in_specs=[tile, _row_spec(), _row_spec(), _row_spec()], out_specs=tile,
        out_shape=jax.ShapeDtypeStruct((seq, D_MODEL), MXU_DTYPE),
        compiler_params=_params(dimension_semantics=("parallel",)),
    )(x, g, shift, scale)


def _in_proj(h, win_g, tm, name, after=None):
    seq = h.shape[0]

    def body(h_ref, w_ref, z_ref, w_pair):
        @pl.when(pl.program_id(1) == 0)
        def _():
            w_pair[...] = jnp.concatenate([w_ref[0], w_ref[1]], axis=1)

        z_ref[...] = jnp.dot(h_ref[...], w_pair[...], preferred_element_type=F32)

    return _pallas_after(
        body, 2, after, name=name, grid=(N_DEV // 2, seq // tm),
        in_specs=[pl.BlockSpec((tm, D_MODEL), lambda j, i: (i, 0)),
                  pl.BlockSpec((2, D_MODEL, IN_COLS), lambda j, i: (j, 0, 0))],
        out_specs=pl.BlockSpec((tm, 2 * IN_COLS), lambda j, i: (i, j)),
        out_shape=jax.ShapeDtypeStruct((seq, IN_WIDTH), F32),
        scratch_shapes=[pltpu.VMEM((D_MODEL, 2 * IN_COLS), MXU_DTYPE)],
        compiler_params=_params(dimension_semantics=("parallel", "arbitrary")),
    )(h, win_g)


def _shift_down(v, j, pos):
    return jnp.where(pos >= j, pltpu.roll(v, j, 0), 0.0)


def _shift_up(v, j, pos, seq):
    return jnp.where(pos < seq - j, pltpu.roll(v, seq - j, 0), 0.0)


def _select_window(g, candidates):
    out = candidates[-1]
    for i in range(len(candidates) - 2, -1, -1):
        out = jnp.where(g == i, candidates[i], out)
    return out


def _pool_mean_minus_token(u, g, pos):
    sums, acc = [], u
    for j in (1, 2, 4, 8):
        acc = acc + _shift_down(acc, j, pos)
        sums.append(acc)
    wsum = _select_window(g, sums)
    width = jnp.left_shift(2, g).astype(F32)
    count = jnp.minimum(pos.astype(F32) + 1.0, width)
    return wsum / count - u, count


def _pool_fwd(z, pool_w_l, pool_scale_l, name, after=None):
    seq = z.shape[0]

    def body(pv_ref, pg_ref, w_ref, sc_ref, out_ref):
        g = pl.program_id(0)
        pos = lax.broadcasted_iota(jnp.int32, (seq, GROUP_DIM), 0)
        pm, _ = _pool_mean_minus_token(pv_ref[...], g, pos)
        lin = _dot(pm, w_ref[...]) * sc_ref[...]
        pg = pg_ref[...]
        out_ref[...] = (lin * (pg * _sigmoid(pg))).astype(out_ref.dtype)

    return _pallas_after(
        body, 4, after, name=name, grid=(POOL_GROUPS,),
        in_specs=[pl.BlockSpec((seq, GROUP_DIM), lambda g: (0, g)),
                  pl.BlockSpec((seq, GROUP_DIM), lambda g: (0, POOL_GROUPS + g)),
                  pl.BlockSpec((None, GROUP_DIM, GROUP_DIM), lambda g: (g, 0, 0)),
                  pl.BlockSpec((1, GROUP_DIM), lambda g: (0, g))],
        out_specs=pl.BlockSpec((seq, GROUP_DIM), lambda g: (0, g)),
        out_shape=jax.ShapeDtypeStruct((seq, POOL_WIDTH), MXU_DTYPE),
        compiler_params=_params(dimension_semantics=("parallel",)),
    )(z, z, pool_w_l, pool_scale_l)


def _chunk_masks():
    row = lax.broadcasted_iota(jnp.int32, (CHUNK, CHUNK), 0)
    col = lax.broadcasted_iota(jnp.int32, (CHUNK, CHUNK), 1)
    causal = row >= col
    before_sub = col < (row // SUB) * SUB
    suffix = row <= col
    return causal, before_sub, suffix


def _masked_sums(masks, v):
    lhs = jnp.concatenate([m.astype(jnp.bfloat16) for m in masks], axis=0)
    hi = v.astype(jnp.bfloat16)
    rest = v - hi.astype(F32)
    mid = rest.astype(jnp.bfloat16)
    lo = (rest - mid.astype(F32)).astype(jnp.bfloat16)
    out = jnp.dot(lhs, hi, preferred_element_type=F32)
    out += jnp.dot(lhs, mid, preferred_element_type=F32)
    out += jnp.dot(lhs, lo, preferred_element_type=F32)
    return [out[i * CHUNK:(i + 1) * CHUNK] for i in range(len(masks))]


def _gates(zf, lb):
    sg = _sigmoid(zf)
    f = lb + (1.0 - lb) * sg
    logf = jnp.log(jnp.maximum(f, LOG_FLOOR))
    return sg, f, logf


def _intra_blocks(q_h, k_h, cum_h, base_h, causal):
    rel = cum_h - base_h
    out = []
    for i in range(N_SUB):
        rows = slice(i * SUB, (i + 1) * SUB)
        e_q = jnp.exp(rel[rows])
        base_i = jnp.concatenate([base_h[rows]] * N_SUB, axis=0)
        e_k = jnp.exp(jnp.minimum(base_i - cum_h, EXP_CLAMP))
        q_t = (q_h[rows] * e_q).astype(MXU_DTYPE)
        k_t = (k_h * e_k).astype(MXU_DTYPE)
        a_i = jnp.where(causal[rows], _dot_nt(q_t, k_t), 0.0)
        out.append((q_t, k_t, e_q, e_k, a_i))
    return out


def _hgrn_fwd(z, lb_l, gn_l, name, after=None):
    seq = z.shape[0]
    n_chunks = seq // CHUNK
    per_step = min(FWD_STEP_CHUNKS, n_chunks)
    rows_per_step = per_step * CHUNK

    def body(hq_ref, hf_ref, hi_ref, hg_ref, lb_ref, gn_ref, o_ref, bin_ref, st_ref, state):
        @pl.when(pl.program_id(0) == 0)
        def _():
            state[...] = jnp.zeros_like(state)

        causal, before_sub, _ = _chunk_masks()
        for cc in range(per_step):
            rows = slice(cc * CHUNK, (cc + 1) * CHUNK)
            _, f, logf = _gates(hf_ref[rows, :], lb_ref[...])
            kk = 1.0 - f
            hq = hq_ref[rows, :]
            q = hq * _sigmoid(hq)
            cum, base = _masked_sums([causal, before_sub], logf)
            st_ref[cc] = state[...]
            for h in range(HEADS):
                sl = slice(h * HEAD_DIM, (h + 1) * HEAD_DIM)
                q_h, k_h, cum_h = q[:, sl], kk[:, sl], cum[:, sl]
                v_h = hi_ref[rows, sl]
                st_h = state[h]
                blocks = _intra_blocks(q_h, k_h, cum_h, base[:, sl], causal)
                a = jnp.concatenate([b[4] for b in blocks], axis=0)
                o_h = _dot_nt(q_h * jnp.exp(cum_h), st_h) + _dot(a, v_h)
                last = jnp.sum(logf[:, sl], axis=0, keepdims=True)
                state[h] = st_h * jnp.exp(last) + _dot_tn(v_h, k_h * jnp.exp(last - cum_h))
                rs = lax.rsqrt(jnp.mean(o_h * o_h, axis=-1, keepdims=True) + NORM_EPS)
                hg = hg_ref[rows, sl]
                o_ref[rows, sl] = o_h
                bin_ref[rows, sl] = ((o_h * rs * gn_ref[...]) * (hg * _sigmoid(hg))).astype(bin_ref.dtype)

    def col(block):
        return pl.BlockSpec((rows_per_step, D_MODEL), lambda c: (c, block))

    tile = pl.BlockSpec((rows_per_step, D_MODEL), lambda c: (c, 0))
    return _pallas_after(
        body, 6, after, name=name, grid=(n_chunks // per_step,),
        in_specs=[col(COL_HQ), col(COL_HF), col(COL_HI), col(COL_HG), _row_spec(), _row_spec(HEAD_DIM)],
        out_specs=[tile, tile, pl.BlockSpec((per_step, HEADS, HEAD_DIM, HEAD_DIM), lambda c: (c, 0, 0, 0))],
        out_shape=[jax.ShapeDtypeStruct((seq, D_MODEL), F32),
                   jax.ShapeDtypeStruct((seq, D_MODEL), MXU_DTYPE),
                   jax.ShapeDtypeStruct((n_chunks, HEADS, HEAD_DIM, HEAD_DIM), F32)],
        scratch_shapes=[pltpu.VMEM((HEADS, HEAD_DIM, HEAD_DIM), F32)],
        compiler_params=_params(dimension_semantics=("arbitrary",)),
    )(z, z, z, z, lb_l, gn_l)


def _rms_parts(y):
    rs = lax.rsqrt(jnp.mean(y * y, axis=-1, keepdims=True) + NORM_EPS)
    return rs, y * rs


def _merge_fwd(a_in, b_in, z, x, wpo_g, who_g, wout_g, gate, g_post, tm, name):
    seq = x.shape[0]

    def body(a_ref, b_ref, mgp_ref, mgh_ref, x_ref, wpo_ref, who_ref, wout_ref, gate_ref, gp_ref,
             ba_ref, bb_ref, mer_ref, y_ref, xn_ref):
        a = a_ref[...]
        ba = jnp.concatenate([_dot(a, wpo_ref[j]) for j in range(N_DEV)], axis=1)
        bb = _dot(b_ref[...], who_ref[...])
        merged = _sigmoid(mgp_ref[...]) * ba + _sigmoid(mgh_ref[...]) * bb
        y = _dot(merged, wout_ref[...])
        _, yn = _rms_parts(y)
        ba_ref[...] = ba.astype(ba_ref.dtype)
        bb_ref[...] = bb.astype(bb_ref.dtype)
        mer_ref[...] = merged.astype(mer_ref.dtype)
        y_ref[...] = y.astype(y_ref.dtype)
        xn_ref[...] = x_ref[...] + gate_ref[...] * (yn * gp_ref[...])

    def tile(cols=D_MODEL, block=0):
        return pl.BlockSpec((tm, cols), lambda i: (i, block))

    full = pl.BlockSpec((D_MODEL, D_MODEL), lambda i: (0, 0))
    act = jax.ShapeDtypeStruct((seq, D_MODEL), MXU_DTYPE)
    f32 = jax.ShapeDtypeStruct((seq, D_MODEL), F32)
    return pl.pallas_call(
        body, name=name, grid=(seq // tm,),
        in_specs=[tile(POOL_WIDTH), tile(), tile(block=COL_MGP), tile(block=COL_MGH), tile(),
                  pl.BlockSpec((N_DEV, POOL_WIDTH, GROUP_DIM), lambda i: (0, 0, 0)),
                  full, full, _row_spec(), _row_spec()],
        out_specs=[tile(), tile(), tile(), tile(), tile()],
        out_shape=[act, act, act, act, f32],
        compiler_params=_params(dimension_semantics=("parallel",)),
    )(a_in, b_in, z, z, x, wpo_g, who_g, wout_g, gate, g_post)


def _loss_grad(x_out, target, tm):
    seq = x_out.shape[0]

    def body(x_ref, t_ref, loss_ref, dx_ref):
        @pl.when(pl.program_id(0) == 0)
        def _():
            loss_ref[...] = jnp.zeros_like(loss_ref)

        err = x_ref[...] - t_ref[...]
        per_token = jnp.mean(err * err, axis=-1, keepdims=True)
        loss_ref[...] += 0.5 * jnp.sum(per_token, axis=0, keepdims=True)
        dx_ref[...] = err * (1.0 / D_MODEL)

    tile = pl.BlockSpec((tm, D_MODEL), lambda i: (i, 0))
    return pl.pallas_call(
        body, name="loss_grad", grid=(seq // tm,),
        in_specs=[tile, tile],
        out_specs=[pl.BlockSpec((1, 1), lambda i: (0, 0)), tile],
        out_shape=[jax.ShapeDtypeStruct((1, 1), F32), jax.ShapeDtypeStruct((seq, D_MODEL), F32)],
        compiler_params=_params(dimension_semantics=("arbitrary",)),
    )(x_out, target)


def _stage_copy(stage, sems, dst, slot, step, where):
    rows, cols = where(step)
    return pltpu.make_async_copy(stage.at[slot], dst.at[rows, cols], sems.at[slot])


def _stage_begin(stage, sems, dst, step, where):
    slot = step % 2

    @pl.when(step >= 2)
    def _():
        _stage_copy(stage, sems, dst, slot, step - 2, where).wait()

    return slot


def _stage_end(stage, sems, dst, step, n_steps, where):
    slot = step % 2
    _stage_copy(stage, sems, dst, slot, step, where).start()

    @pl.when(step == n_steps - 1)
    def _():
        _stage_copy(stage, sems, dst, slot, step, where).wait()
        if n_steps > 1:
            _stage_copy(stage, sems, dst, 1 - slot, step - 1, where).wait()


def _merge_bwd(dx, y, ba, bb, z, wpo_g, who_g, wout_g, gate, g_post, dz, tm, name):
    seq = dx.shape[0]
    n_steps = seq // tm

    def body(dx_ref, y_ref, ba_ref, bb_ref, mgp_ref, mgh_ref, wpo_ref, who_ref, wout_ref, gate_ref, gp_ref, _,
             dy_ref, dba_ref, dbb_ref, da_ref, db_ref, dz_ref, acc_ref, stage, sems):
        step = pl.program_id(0)

        @pl.when(step == 0)
        def _():
            acc_ref[...] = jnp.zeros_like(acc_ref)

        def where(t):
            return pl.ds(t * tm, tm), pl.ds(COL_MGP * D_MODEL, 2 * D_MODEL)

        dmg_ref = stage.at[_stage_begin(stage, sems, dz_ref, step, where)]

        dxv = dx_ref[...]
        rs, yn = _rms_parts(y_ref[...].astype(F32))
        acc_ref[0:1, :] += jnp.sum(dxv * yn * gp_ref[...], axis=0, keepdims=True)
        acc_ref[1:2, :] += jnp.sum(dxv * gate_ref[...] * yn, axis=0, keepdims=True)
        dyn = dxv * (gate_ref[...] * gp_ref[...])
        dy = rs * (dyn - yn * jnp.mean(dyn * yn, axis=-1, keepdims=True))
        dmerged = _dot_nt(dy, wout_ref[...])
        sp, sh = _sigmoid(mgp_ref[...]), _sigmoid(mgh_ref[...])
        dba, dbb = sp * dmerged, sh * dmerged
        dmg_ref[:, 0:D_MODEL] = (dmerged * ba_ref[...].astype(F32) * sp * (1.0 - sp)).astype(dmg_ref.dtype)
        dmg_ref[:, D_MODEL:2 * D_MODEL] = (dmerged * bb_ref[...].astype(F32) * sh * (1.0 - sh)).astype(dmg_ref.dtype)
        da = _dot_nt(dba[:, 0:GROUP_DIM], wpo_ref[0])
        for j in range(1, N_DEV):
            da += _dot_nt(dba[:, j * GROUP_DIM:(j + 1) * GROUP_DIM], wpo_ref[j])
        dy_ref[...] = dy.astype(dy_ref.dtype)
        dba_ref[...] = dba.astype(dba_ref.dtype)
        dbb_ref[...] = dbb.astype(dbb_ref.dtype)
        da_ref[...] = da.astype(da_ref.dtype)
        db_ref[...] = _dot_nt(dbb, who_ref[...]).astype(db_ref.dtype)
        _stage_end(stage, sems, dz_ref, step, n_steps, where)

    def tile(cols=D_MODEL, block=0):
        return pl.BlockSpec((tm, cols), lambda i: (i, block))

    full = pl.BlockSpec((D_MODEL, D_MODEL), lambda i: (0, 0))
    hbm = pl.BlockSpec(memory_space=pl.ANY)
    act = jax.ShapeDtypeStruct((seq, D_MODEL), MXU_DTYPE)
    return pl.pallas_call(
        body, name=name, grid=(n_steps,),
        in_specs=[tile(), tile(), tile(), tile(), tile(block=COL_MGP), tile(block=COL_MGH),
                  pl.BlockSpec((N_DEV, POOL_WIDTH, GROUP_DIM), lambda i: (0, 0, 0)),
                  full, full, _row_spec(), _row_spec(), hbm],
        out_specs=[tile(), tile(), tile(), tile(POOL_WIDTH), tile(), hbm,
                   pl.BlockSpec((8, D_MODEL), lambda i: (0, 0))],
        out_shape=[act, act, act, jax.ShapeDtypeStruct((seq, POOL_WIDTH), MXU_DTYPE), act,
                   jax.ShapeDtypeStruct(dz.shape, dz.dtype),
                   jax.ShapeDtypeStruct((8, D_MODEL), F32)],
        input_output_aliases={11: 5},
        scratch_shapes=[pltpu.VMEM((2, tm, 2 * D_MODEL), MXU_DTYPE), pltpu.SemaphoreType.DMA((2,))],
        compiler_params=_params(dimension_semantics=("arbitrary",)),
    )(dx, y, ba, bb, z, z, wpo_g, who_g, wout_g, gate, g_post, dz)


def _grad_out_weights(merged, dy, b_in, dbb, a_in, dba, name):
    seq = merged.shape[0]
    tn = D_MODEL // 2
    per_step = tn // GROUP_DIM

    def body(mer_ref, dy_ref, b_ref, dbb_ref, a_ref, dba_ref, gout_ref, gho_ref, gpo_ref):
        gout_ref[...] = _dot_tn(mer_ref[...], dy_ref[...]).astype(gout_ref.dtype)
        gho_ref[...] = _dot_tn(b_ref[...], dbb_ref[...]).astype(gho_ref.dtype)
        g_po = _dot_tn(a_ref[...], dba_ref[...])
        for j in range(per_step):
            gpo_ref[j] = g_po[:, j * GROUP_DIM:(j + 1) * GROUP_DIM].astype(gpo_ref.dtype)

    def whole(cols):
        return pl.BlockSpec((seq, cols), lambda j: (0, 0))

    cols = pl.BlockSpec((seq, tn), lambda j: (0, j))
    return pl.pallas_call(
        body, name=name, grid=(D_MODEL // tn,),
        in_specs=[whole(D_MODEL), cols, whole(D_MODEL), cols, whole(POOL_WIDTH), cols],
        out_specs=[pl.BlockSpec((D_MODEL, tn), lambda j: (0, j)), pl.BlockSpec((D_MODEL, tn), lambda j: (0, j)),
                   pl.BlockSpec((per_step, POOL_WIDTH, GROUP_DIM), lambda j: (j, 0, 0))],
        out_shape=[jax.ShapeDtypeStruct((D_MODEL, D_MODEL), WIRE_DTYPE),
                   jax.ShapeDtypeStruct((D_MODEL, D_MODEL), WIRE_DTYPE),
                   jax.ShapeDtypeStruct((N_DEV, POOL_WIDTH, GROUP_DIM), WIRE_DTYPE)],
        compiler_params=_params(dimension_semantics=("parallel",)),
    )(merged, dy, b_in, dbb, a_in, dba)


def _hgrn_bwd(db_in, z, o, states, lb_l, gn_l, dz, name, after=None):
    seq = z.shape[0]
    per_step = min(BWD_STEP_CHUNKS, seq // CHUNK)
    rows_per_step = per_step * CHUNK
    n_steps = seq // rows_per_step
    last_step = n_steps - 1

    def body(db_ref, hq_ref, hf_ref, hi_ref, hg_ref, o_ref, st_ref, lb_ref, gn_ref, _,
             dz_hbm, dlb_ref, dgn_ref, dstate, dq_buf, dk_buf, dg_buf, stage, sems):
        step = pl.program_id(0)

        @pl.when(step == 0)
        def _():
            dstate[...] = jnp.zeros_like(dstate)
            dlb_ref[...] = jnp.zeros_like(dlb_ref)
            dgn_ref[...] = jnp.zeros_like(dgn_ref)

        def one_chunk(cc, *args):
            one_chunk_body((db_ref, hq_ref, hf_ref, hi_ref, hg_ref, o_ref, st_ref, dlb_ref, dgn_ref, dstate,
                            dq_buf, dk_buf, dg_buf), cc, *args)

        def where(t):
            return pl.ds((last_step - t) * rows_per_step, rows_per_step), pl.ds(COL_HQ * D_MODEL, 4 * D_MODEL)

        dz_step = stage.at[_stage_begin(stage, sems, dz_hbm, step, where)]
        causal, before_sub, suffix = _chunk_masks()
        lb = lb_ref[...]
        gn = gn_ref[...]
        for cc in reversed(range(per_step)):
            one_chunk(cc, dz_step, causal, before_sub, suffix, lb, gn)
        _stage_end(stage, sems, dz_hbm, step, n_steps, where)

    def one_chunk_body(refs, cc, dz_step, causal, before_sub, suffix, lb, gn):
        (db_ref, hq_ref, hf_ref, hi_ref, hg_ref, o_ref, st_ref, dlb_ref, dgn_ref, dstate, dq_buf, dk_buf, dg_buf) = refs
        rows = slice(cc * CHUNK, (cc + 1) * CHUNK)
        dz_ref = dz_step.at[rows, :]
        dq_buf, dk_buf, dg_buf = dq_buf.at[cc], dk_buf.at[cc], dg_buf.at[cc]
        sg, f, logf = _gates(hf_ref[rows, :], lb)
        kk = 1.0 - f
        hq = hq_ref[rows, :]
        sq = _sigmoid(hq)
        q = hq * sq
        cum, base = _masked_sums([causal, before_sub], logf)
        dgn = jnp.zeros((1, HEAD_DIM), F32)
        dlast = []
        for h in range(HEADS):
            sl = slice(h * HEAD_DIM, (h + 1) * HEAD_DIM)
            q_h, k_h, cum_h = q[:, sl], kk[:, sl], cum[:, sl]
            v_h = hi_ref[rows, sl]
            st_h = st_ref[cc, h]
            dst_h = dstate[h]
            rs, ohat = _rms_parts(o_ref[rows, sl])
            hg = hg_ref[rows, sl]
            shg = _sigmoid(hg)
            d_bin = db_ref[rows, sl].astype(F32)
            don = d_bin * (hg * shg)
            dgn += jnp.sum(don * ohat, axis=0, keepdims=True)
            dohat = don * gn
            do = rs * (dohat - ohat * jnp.mean(dohat * ohat, axis=-1, keepdims=True))
            dz_ref[:, 3 * D_MODEL + h * HEAD_DIM:3 * D_MODEL + (h + 1) * HEAD_DIM] = (
                d_bin * (ohat * gn) * _dsilu(hg, shg)).astype(dz_ref.dtype)
            last = jnp.sum(logf[:, sl], axis=0, keepdims=True)
            g_in = jnp.exp(cum_h)
            d_out = jnp.exp(last - cum_h)
            q_bar, k_bar = q_h * g_in, k_h * d_out
            blocks = _intra_blocks(q_h, k_h, cum_h, base[:, sl], causal)
            a = jnp.concatenate([b[4] for b in blocks], axis=0)
            da = jnp.where(causal, _dot_nt(do, v_h), 0.0)
            dv = _dot_tn(a, do) + _dot_nt(k_bar, dst_h)
            dq_bar, dk_bar = _dot(do, st_h), _dot(v_h, dst_h)
            dk = dk_bar * d_out
            dq_parts, dg_parts = [], []
            dg_k = k_bar * dk_bar
            dlast.append(jnp.sum(k_bar * dk_bar, axis=0, keepdims=True)
                         + jnp.exp(last) * jnp.sum(st_h * dst_h, axis=0, keepdims=True))
            for i, (q_t, k_t, e_q, e_k, _) in enumerate(blocks):
                da_i = da[i * SUB:(i + 1) * SUB].astype(MXU_DTYPE)
                dq_t = _dot(da_i, k_t)
                dk_t = _dot_tn(da_i, q_t)
                dq_parts.append(dq_t * e_q)
                dk += dk_t * e_k
                dg_parts.append(q_t.astype(F32) * dq_t)
                dg_k += k_t.astype(F32) * dk_t
            dq = dq_bar * g_in + jnp.concatenate(dq_parts, axis=0)
            dg_buf[:, sl] = q_bar * dq_bar + jnp.concatenate(dg_parts, axis=0) - dg_k
            dstate[h] = dst_h * jnp.exp(last) + _dot_tn(do, q_bar)
            dq_buf[:, sl] = dq
            dk_buf[:, sl] = dk
            dz_ref[:, 2 * D_MODEL + h * HEAD_DIM:2 * D_MODEL + (h + 1) * HEAD_DIM] = dv.astype(dz_ref.dtype)
        dgn_ref[...] += dgn
        dq_all, dk_all = dq_buf[...], dk_buf[...]
        dlogf = _masked_sums([suffix], dg_buf[...])[0] + jnp.concatenate(dlast, axis=1)
        df = jnp.where(f > LOG_FLOOR, dlogf / f, 0.0) - dk_all
        dlb_ref[...] += jnp.sum(df * (1.0 - sg), axis=0, keepdims=True)
        dz_ref[:, 0:D_MODEL] = (dq_all * _dsilu(hq, sq)).astype(dz_ref.dtype)
        dz_ref[:, D_MODEL:2 * D_MODEL] = (df * (1.0 - lb) * sg * (1.0 - sg)).astype(dz_ref.dtype)

    def col(block):
        return pl.BlockSpec((rows_per_step, D_MODEL), lambda c: (last_step - c, block))

    hbm = pl.BlockSpec(memory_space=pl.ANY)
    return _pallas_after(
        body, 10, after, name=name, grid=(n_steps,),
        in_specs=[col(0), col(COL_HQ), col(COL_HF), col(COL_HI), col(COL_HG), col(0),
                  pl.BlockSpec((per_step, HEADS, HEAD_DIM, HEAD_DIM), lambda c: (last_step - c, 0, 0, 0)),
                  _row_spec(), _row_spec(HEAD_DIM), hbm],
        out_specs=[hbm, _row_spec(), _row_spec(HEAD_DIM)],
        out_shape=[jax.ShapeDtypeStruct(dz.shape, dz.dtype),
                   jax.ShapeDtypeStruct((1, D_MODEL), F32), jax.ShapeDtypeStruct((1, HEAD_DIM), F32)],
        input_output_aliases={9: 0},
        scratch_shapes=[pltpu.VMEM((HEADS, HEAD_DIM, HEAD_DIM), F32)]
        + [pltpu.VMEM((per_step, CHUNK, D_MODEL), F32)] * 3
        + [pltpu.VMEM((2, rows_per_step, 4 * D_MODEL), MXU_DTYPE), pltpu.SemaphoreType.DMA((2,))],
        compiler_params=_params(dimension_semantics=("arbitrary",)),
    )(db_in, z, z, z, z, o, states, lb_l, gn_l, dz)


def _pool_bwd(da_in, z, pool_w_l, pool_scale_l, dz, name, after=None):
    seq = z.shape[0]

    def body(da_ref, pv_ref, pg_ref, w_ref, sc_ref, _, dz_hbm, dw_ref, dsc_ref, stage_pv, stage_pg, sems_pv, sems_pg):
        g = pl.program_id(0)

        def where_pv(t):
            return pl.ds(0, seq), pl.ds(pl.multiple_of(t * GROUP_DIM, GROUP_DIM), GROUP_DIM)

        def where_pg(t):
            return pl.ds(0, seq), pl.ds(pl.multiple_of(POOL_WIDTH + t * GROUP_DIM, GROUP_DIM), GROUP_DIM)

        dpv_ref = stage_pv.at[_stage_begin(stage_pv, sems_pv, dz_hbm, g, where_pv)]
        dpg_ref = stage_pg.at[_stage_begin(stage_pg, sems_pg, dz_hbm, g, where_pg)]
        pos = lax.broadcasted_iota(jnp.int32, (seq, GROUP_DIM), 0)
        pm, count = _pool_mean_minus_token(pv_ref[...], g, pos)
        lin0 = _dot(pm, w_ref[...])
        pg = pg_ref[...]
        spg = _sigmoid(pg)
        da = da_ref[...].astype(F32)
        dlin = da * (pg * spg)
        dpg_ref[...] = (da * (lin0 * sc_ref[...]) * _dsilu(pg, spg)).astype(dpg_ref.dtype)
        dsc_ref[...] = jnp.sum(dlin * lin0, axis=0, keepdims=True)
        dl0 = dlin * sc_ref[...]
        dw_ref[...] = _dot_tn(pm, dl0)
        dpm = _dot_nt(dl0, w_ref[...])
        sums, acc = [], dpm / count
        for j in (1, 2, 4, 8):
            acc = acc + _shift_up(acc, j, pos, seq)
            sums.append(acc)
        dpv_ref[...] = (_select_window(g, sums) - dpm).astype(dpv_ref.dtype)
        _stage_end(stage_pv, sems_pv, dz_hbm, g, POOL_GROUPS, where_pv)
        _stage_end(stage_pg, sems_pg, dz_hbm, g, POOL_GROUPS, where_pg)

    grp = pl.BlockSpec((seq, GROUP_DIM), lambda g: (0, g))
    hbm = pl.BlockSpec(memory_space=pl.ANY)
    stage = pltpu.VMEM((2, seq, GROUP_DIM), MXU_DTYPE)
    return _pallas_after(
        body, 6, after, name=name, grid=(POOL_GROUPS,),
        in_specs=[grp, grp, pl.BlockSpec((seq, GROUP_DIM), lambda g: (0, POOL_GROUPS + g)),
                  pl.BlockSpec((None, GROUP_DIM, GROUP_DIM), lambda g: (g, 0, 0)),
                  pl.BlockSpec((1, GROUP_DIM), lambda g: (0, g)), hbm],
        out_specs=[hbm, pl.BlockSpec((None, GROUP_DIM, GROUP_DIM), lambda g: (g, 0, 0)),
                   pl.BlockSpec((1, GROUP_DIM), lambda g: (0, g))],
        out_shape=[jax.ShapeDtypeStruct(dz.shape, dz.dtype),
                   jax.ShapeDtypeStruct((POOL_GROUPS, GROUP_DIM, GROUP_DIM), F32),
                   jax.ShapeDtypeStruct((1, POOL_WIDTH), F32)],
        input_output_aliases={5: 0},
        scratch_shapes=[stage, stage, pltpu.SemaphoreType.DMA((2,)), pltpu.SemaphoreType.DMA((2,))],
        compiler_params=_params(dimension_semantics=("arbitrary",)),
    )(da_in, z, z, pool_w_l, pool_scale_l, dz)


def _in_proj_dw(h, dz, name, after=None):
    seq = h.shape[0]

    def body(h_ref, dz_ref, out_ref):
        pair = lax.dot_general(h_ref[...], dz_ref[...], (((0,), (0,)), ((), ())), preferred_element_type=F32)
        out_ref[0] = pair[:, 0:IN_COLS].astype(out_ref.dtype)
        out_ref[1] = pair[:, IN_COLS:].astype(out_ref.dtype)

    return _pallas_after(
        body, 2, after, name=name, grid=(N_DEV // 2,),
        in_specs=[pl.BlockSpec((seq, D_MODEL), lambda j: (0, 0)),
                  pl.BlockSpec((seq, 2 * IN_COLS), lambda j: (0, j))],
        out_specs=pl.BlockSpec((2, D_MODEL, IN_COLS), lambda j: (j, 0, 0)),
        out_shape=jax.ShapeDtypeStruct((N_DEV, D_MODEL, IN_COLS), WIRE_DTYPE),
        compiler_params=_params(dimension_semantics=("parallel",)),
    )(h, dz)


def _in_proj_dh(dz, win_g, tm, name, after=None):
    seq = dz.shape[0]

    def body(dz_ref, w_ref, dh_ref):
        @pl.when(pl.program_id(1) == 0)
        def _():
            dh_ref[...] = jnp.zeros_like(dh_ref)

        w_pair = jnp.concatenate([w_ref[0], w_ref[1]], axis=1)
        dh_ref[...] += lax.dot_general(dz_ref[...], w_pair, (((1,), (1,)), ((), ())), preferred_element_type=F32)

    return _pallas_after(
        body, 2, after, name=name, grid=(seq // tm, N_DEV // 2),
        in_specs=[pl.BlockSpec((tm, 2 * IN_COLS), lambda i, j: (i, j)),
                  pl.BlockSpec((2, D_MODEL, IN_COLS), lambda i, j: (j, 0, 0))],
        out_specs=pl.BlockSpec((tm, D_MODEL), lambda i, j: (i, 0)),
        out_shape=jax.ShapeDtypeStruct((seq, D_MODEL), F32),
        compiler_params=_params(dimension_semantics=("parallel", "arbitrary")),
    )(dz, win_g)


def _prenorm_bwd(x, dh, dx_res, g, scale, tm, name, after=None):
    seq = x.shape[0]

    def body(x_ref, dh_ref, dxr_ref, g_ref, sc_ref, dx_ref, acc_ref):
        @pl.when(pl.program_id(0) == 0)
        def _():
            acc_ref[...] = jnp.zeros_like(acc_ref)

        rs, xn = _rms_parts(x_ref[...])
        dh = dh_ref[...]
        acc_ref[0:1, :] += jnp.sum(dh, axis=0, keepdims=True)
        acc_ref[1:2, :] += jnp.sum(dh * (xn * g_ref[...]), axis=0, keepdims=True)
        dhn = dh * (1.0 + sc_ref[...])
        acc_ref[2:3, :] += jnp.sum(dhn * xn, axis=0, keepdims=True)
        dxn = dhn * g_ref[...]
        dx_ref[...] = rs * (dxn - xn * jnp.mean(dxn * xn, axis=-1, keepdims=True)) + dxr_ref[...]

    tile = pl.BlockSpec((tm, D_MODEL), lambda i: (i, 0))
    return _pallas_after(
        body, 5, after, name=name, grid=(seq // tm,),
        in_specs=[tile, tile, tile, _row_spec(), _row_spec()],
        out_specs=[tile, pl.BlockSpec((8, D_MODEL), lambda i: (0, 0))],
        out_shape=[jax.ShapeDtypeStruct((seq, D_MODEL), F32), jax.ShapeDtypeStruct((8, D_MODEL), F32)],
        compiler_params=_params(dimension_semantics=("arbitrary",)),
    )(x, dh, dx_res, g, scale)


def _adamw_math(w, g, m, v):
    m = ADAM_B1 * m + (1.0 - ADAM_B1) * g
    v = ADAM_B2 * v + (1.0 - ADAM_B2) * (g * g)
    m_hat = m / (1.0 - ADAM_B1 ** ADAM_STEP)
    v_hat = v / (1.0 - ADAM_B2 ** ADAM_STEP)
    delta = -ADAM_LR * (m_hat / (jnp.sqrt(v_hat) + ADAM_EPS) + ADAM_WD * w)
    return delta, m, v


def _adamw_sharded(w, m, v, contrib, tr, name):
    depth, rows, cols = w.shape
    n_parts = contrib.shape[1]

    def body(w_ref, m_ref, v_ref, c_ref, g_ref, d_ref, mo_ref, vo_ref):
        g = c_ref[0].astype(F32)
        for p in range(1, n_parts):
            g += c_ref[p].astype(F32)
        delta, mn, vn = _adamw_math(w_ref[...], g, m_ref[...], v_ref[...])
        g_ref[...] = g
        d_ref[...] = delta
        mo_ref[...] = mn
        vo_ref[...] = vn

    tile = pl.BlockSpec((None, tr, cols), lambda l, i: (l, i, 0))
    shape = jax.ShapeDtypeStruct(w.shape, F32)
    return pl.pallas_call(
        body, name=name, grid=(depth, rows // tr),
        in_specs=[tile, tile, tile, pl.BlockSpec((None, n_parts, tr, cols), lambda l, i: (l, 0, i, 0))],
        out_specs=[tile] * 4, out_shape=[shape] * 4,
        compiler_params=_params(dimension_semantics=("parallel", "parallel")),
    )(w, m, v, contrib)


def _adamw_layer(w, m, v, contribs, l, tr, name, prev=None):
    _, rows, cols = w.shape
    n = len(contribs)

    def body(*refs):
        w_ref, m_ref, v_ref = refs[:3]
        c_refs = refs[3:3 + n]
        g_ref, d_ref, mo_ref, vo_ref = refs[-4:]
        g = c_refs[0][...].astype(F32)
        for c_ref in c_refs[1:]:
            g += c_ref[...].astype(F32)
        delta, mn, vn = _adamw_math(w_ref[...], g, m_ref[...], v_ref[...])
        g_ref[...] = g
        d_ref[...] = delta
        mo_ref[...] = mn
        vo_ref[...] = vn

    tile = pl.BlockSpec((None, tr, cols), lambda i: (l, i, 0))
    in_specs = [tile, tile, tile] + [pl.BlockSpec((None, tr, cols), lambda i, s=slot: (s, i, 0)) for _, slot in contribs]
    operands = [w, m, v] + [arr for arr, _ in contribs]
    aliases = {}
    if prev is not None:
        aliases = {len(operands) + k: k for k in range(4)}
        in_specs += [pl.BlockSpec(memory_space=pl.ANY)] * 4
        operands += list(prev)
    shape = jax.ShapeDtypeStruct(w.shape, F32)
    return pl.pallas_call(
        body, name=name, grid=(rows // tr,), in_specs=in_specs, out_specs=[tile] * 4, out_shape=[shape] * 4,
        input_output_aliases=aliases,
        compiler_params=_params(dimension_semantics=("parallel",)),
    )(*operands)


def _adamw_small(w_pack, m_pack, v_pack, g_late, g_early, shapes):
    pieces, r = {}, 0
    for name, _, n in _SMALL_ROWS:
        pieces.setdefault(name, []).append((r, n))
        r += n
    names = list(pieces)

    def body(w_ref, m_ref, v_ref, gl_ref, ge_ref, *rest):
        outs, packs = rest[:4 * len(names)], rest[4 * len(names):]
        g_l, g_e = gl_ref[0][0:SMALL_LATE_ROWS], ge_ref[0]
        for d in range(1, N_DEV):
            g_l += gl_ref[d][0:SMALL_LATE_ROWS]
            g_e += ge_ref[d]
        g = jnp.concatenate([g_l, g_e], axis=0)
        w = w_ref[...]
        r0, r1, r2 = LB_ROW0, LB_ROW0 + 8, LB_ROW0 + 16
        lg0, lg1 = w[r0:r1], w[r1:r2]
        mx = jnp.maximum(lg0, lg1)
        e0, e1 = jnp.exp(lg0 - mx), jnp.exp(lg1 - mx)
        p0, p1 = e0 / (e0 + e1), e1 / (e0 + e1)
        low = ((p0 - p0), (p0 + p1) - p0)
        dlow = [g_rows * jnp.where((lo > 0.0) & (lo < 1.0), 1.0, jnp.where((lo == 0.0) | (lo == 1.0), 0.5, 0.0))
                for g_rows, lo in ((g[r0:r1], low[0]), (g[r1:r2], low[1]))]
        dp0 = (dlow[0] + dlow[1]) - (dlow[0] + dlow[1])
        dp1 = dlow[1]
        inner = p0 * dp0 + p1 * dp1
        g = jnp.concatenate([g[:r0], p0 * (dp0 - inner), p1 * (dp1 - inner), g[r2:]], axis=0)
        delta, mn, vn = _adamw_math(w, g, m_ref[...], v_ref[...])
        for kind, val in enumerate((g, delta, mn, vn)):
            packs[kind][...] = val
            for j, name in enumerate(names):
                out, at = outs[kind * len(names) + j], 0
                for start, n in pieces[name]:
                    if name in flat:
                        for r in range(n):
                            layer, c = divmod(at + r, flat[name])
                            out[layer:layer + 1, c * 128:(c + 1) * 128] = packs[kind][start + r:start + r + 1, :]
                    else:
                        out[at:at + n, :] = packs[kind][start:start + n, :]
                    at += n

    rows = {name: sum(n for _, n in pieces[name]) for name in names}
    flat = {name: rows[name] // DEPTH for name in names if len(shapes[name]) == 2}
    outs = pl.pallas_call(
        body, name="adamw_small",
        out_shape=[jax.ShapeDtypeStruct(shapes[name] if name in flat else (rows[name], 128), F32)
                   for _ in range(4) for name in names],
        scratch_shapes=[pltpu.VMEM(w_pack.shape, F32)] * 4, compiler_params=_params(),
    )(w_pack, m_pack, v_pack, g_late, g_early)
    return [{name: outs[kind * len(names) + j].reshape(shapes[name]) for j, name in enumerate(names)}
            for kind in range(4)]


def _pack_small(parts, first=0, last=len(_SMALL_ROWS)):
    rows = [(parts[name] if l is None else parts[name][l]).reshape(n, 128) for name, l, n in _SMALL_ROWS[first:last]]
    if last == len(_SMALL_ROWS):
        rows.append(jnp.zeros((SMALL_ROWS_PAD - sum(n for _, _, n in _SMALL_ROWS), 128), F32))
    return jnp.concatenate(rows, axis=0)


def kernel(x, c, w_ada, b_ada, g_pre, g_post, w_in, pool_w, pool_scale, lb_logits, hgrn_norm_g, w_pool_o, w_hgrn_o, w_out, loss_target, m_w_ada, m_b_ada, m_g_pre, m_g_post, m_w_in, m_pool_w, m_pool_scale, m_lb_logits, m_hgrn_norm_g, m_w_pool_o, m_w_hgrn_o, m_w_out, v_w_ada, v_b_ada, v_g_pre, v_g_post, v_w_in, v_pool_w, v_pool_scale, v_lb_logits, v_hgrn_norm_g, v_w_pool_o, v_w_hgrn_o, v_w_out):
    seq = x.shape[1]
    tm = min(512, seq)
    tm_merge = min(256, seq)
    pos = _my_position()
    me = pos[3]

    c_all = _allgather_small(c, "allgather_c").reshape(N_DEV, D_MODEL)
    b_cols = lax.dynamic_slice_in_dim(b_ada, me * ADA_COLS, ADA_COLS, axis=1)
    ada_part = _ada_fwd(c_all, w_ada, b_cols)
    ada_all = _allgather_small(ada_part.reshape(DEPTH * N_DEV, ADA_COLS), "allgather_ada")
    ada = lax.dynamic_index_in_dim(ada_all.reshape(N_DEV, DEPTH, N_DEV, ADA_COLS), me, axis=2, keepdims=False)
    ada = jnp.transpose(ada, (1, 0, 2)).reshape(DEPTH, 3 * D_MODEL)
    shift = [ada[l:l + 1, 0:D_MODEL] for l in range(DEPTH)]
    scale = [ada[l:l + 1, D_MODEL:2 * D_MODEL] for l in range(DEPTH)]
    gate = [ada[l:l + 1, 2 * D_MODEL:] for l in range(DEPTH)]

    big = dict(win=w_in, wpo=w_pool_o, who=w_hgrn_o, wout=w_out)
    units = [["win0"], ["wpo0", "who0", "wout0"], ["win1", "wpo1", "who1", "wout1"]]
    g_streams = [_gather_streams(keys) for keys in units]
    g_state = [None] * len(units)

    def gather_start(u, after):
        bufs = {}
        for k in units[u]:
            arr = big[k[:-1]]
            bufs["s_" + k] = arr[int(k[-1])].astype(WIRE_DTYPE)
            bufs["g_" + k] = _with_own_slot(bufs["s_" + k], me)
        bufs, sems, token = _comm_call(f"gather_start_{u}", bufs, start=list(g_streams[u][:2]), after=after)
        g_state[u] = dict(bufs=bufs, sems=sems)
        return token

    def gather_pass(u, after):
        st = g_state[u]
        to_chips, _, pass_on = g_streams[u]
        st["bufs"], (st["pass_sems"],), _ = _comm_call(f"gather_pass_{u}", st["bufs"], start=[pass_on],
                                                       wait=[(to_chips, st["sems"][0])], after=after)

    def gather_done(u, after=None):
        st = g_state[u]
        _, to_sibling, pass_on = g_streams[u]
        bufs, _, _ = _comm_call(f"gather_done_{u}", st["bufs"], after=after,
                                wait=[(to_sibling, st["sems"][1]), (pass_on, st["pass_sems"])])
        return {k: bufs["g_" + k] for k in units[u]}

    token = gather_start(0, ada_all)

    lb = _lb_fwd(lb_logits)

    gw = {}
    xs, saved = [x[0]], []
    for l in range(DEPTH):
        h = _prenorm_fwd(xs[l], g_pre[l:l + 1], shift[l], scale[l], tm, f"prenorm_fwd_{l}",
                         after=token if l == 0 else None)
        token = None
        if l == 0:
            gather_pass(0, h)
            gw.update(gather_done(0))
            token = gather_start(1, gw["win0"])
        else:
            gather_pass(2, h)
            gw.update(gather_done(2))
        z = _in_proj(h, gw[f"win{l}"], min(1024, seq), f"in_proj_{l}", after=token)
        if l == 0:
            gather_pass(1, z)
            token = gather_start(2, g_state[1]["bufs"]["g_wpo0"])
        a_in = _pool_fwd(z, pool_w[l], pool_scale[l:l + 1], f"pool_fwd_{l}", after=token)
        o, b_in, states = _hgrn_fwd(z, lb[l:l + 1], hgrn_norm_g[l:l + 1], f"hgrn_fwd_{l}", after=token)
        if l == 0:
            gw.update(gather_done(1, b_in))
        who_l = gw[f"who{l}"].reshape(D_MODEL, D_MODEL)
        wout_l = gw[f"wout{l}"].reshape(D_MODEL, D_MODEL)
        ba, bb, merged, y, x_next = _merge_fwd(a_in, b_in, z, xs[l], gw[f"wpo{l}"], who_l, wout_l, gate[l],
                                               g_post[l:l + 1], tm_merge, f"merge_fwd_{l}")
        xs.append(x_next)
        saved.append((h, z, a_in, o, b_in, states, ba, bb, merged, y, who_l, wout_l))

    loss_part, dx = _loss_grad(xs[DEPTH], loss_target[0], tm)

    chips = _other_chips(pos)
    pair_idx = jnp.stack([_dev_index(cx, cy, pos[2]) for cx, cy in chips] + [me]).astype(jnp.int32)
    pair_rows = dict(win=256, wpo=POOL_WIDTH, who=HEAD_DIM, wout=HEAD_DIM)

    def scatter_pair_start(u, grads):
        keys = list(grads)
        pair, to_chips = _scatter_streams(keys)
        bufs = {}
        for k in keys:
            bufs["g_" + k] = grads[k]
            bufs["st_" + k] = lax.empty((4,) + grads[k].shape[1:], WIRE_DTYPE)
        bufs, (sems,), token = _comm_call(f"scatter_pair_start_{u}", bufs, start=[pair])
        return dict(u=u, keys=keys, pair=pair, to_chips=to_chips, bufs=bufs, sems=sems, token=token)

    def scatter_pair_finish(st, after):
        u, keys = st["u"], st["keys"]
        bufs, _, _ = _comm_call(f"scatter_pair_done_{u}", st["bufs"], wait=[(st["pair"], st["sems"])], after=after)
        bufs2 = {}
        for k in keys:
            bufs2["ps_" + k] = _pair_sum(bufs["g_" + k], bufs["st_" + k], pair_idx, bufs["g_" + k].shape[1],
                                         f"pair_sum_{k}")
            bufs2["ld_" + k] = lax.empty((3,) + bufs["g_" + k].shape[1:], WIRE_DTYPE)
        st.update(bufs=bufs2)

    def scatter_chips_start(st, after=None):
        bufs2, (sems,), token = _comm_call(f"scatter_chips_start_{st['u']}", st["bufs"], start=[st["to_chips"]],
                                           after=after)
        st.update(bufs=bufs2, sems=sems, token=token)

    def scatter_finish(st, after):
        bufs, _, _ = _comm_call(f"scatter_chips_done_{st['u']}", st["bufs"], wait=[(st["to_chips"], st["sems"])],
                                after=after)
        return {k: [(bufs["ps_" + k], 3), (bufs["ld_" + k], 0), (bufs["ld_" + k], 1), (bufs["ld_" + k], 2)]
                for k in st["keys"]}

    moments = dict(win=(m_w_in, v_w_in), wpo=(m_w_pool_o, v_w_pool_o), who=(m_w_hgrn_o, v_w_hgrn_o),
                   wout=(m_w_out, v_w_out))
    big_out = {}

    def finish_unit(unit, after):
        for k, contribs in scatter_finish(scat[unit], after).items():
            wname, l = k[:-1], int(k[-1])
            big_out[wname] = _adamw_layer(big[wname], moments[wname][0], moments[wname][1], contribs, l,
                                          pair_rows[wname], f"adamw_{k}", prev=big_out.get(wname))
            after = big_out[wname][0]
        return after

    d_ada, small, scat = [None] * DEPTH, [None] * DEPTH, {}
    for l in reversed(range(DEPTH)):
        h, z, a_in, o, b_in, states, ba, bb, merged, y, who_l, wout_l = saved[l]
        dy, dba, dbb, da_in, db_in, dz, acc_post = _merge_bwd(
            dx, y, ba, bb, z, gw[f"wpo{l}"], who_l, wout_l, gate[l], g_post[l:l + 1],
            lax.empty((seq, IN_WIDTH), MXU_DTYPE), tm_merge, f"merge_bwd_{l}")
        g_out, g_ho, g_po = _grad_out_weights(merged, dy, b_in, dbb, a_in, dba, f"grad_out_weights_{l}")
        g_small = {f"wout{l}": g_out.reshape(N_DEV, HEAD_DIM, D_MODEL),
                   f"who{l}": g_ho.reshape(N_DEV, HEAD_DIM, D_MODEL), f"wpo{l}": g_po}
        st_small = scat["small0"] = scatter_pair_start("small0", g_small) if l == 0 else None
        dz, dlb, dgn = _hgrn_bwd(db_in, z, o, states, lb[l:l + 1], hgrn_norm_g[l:l + 1], dz, f"hgrn_bwd_{l}",
                                 after=st_small and st_small["token"])
        if l == 0:
            scatter_pair_finish(st_small, dlb)
            scatter_chips_start(st_small)
        dz, dpw, dps = _pool_bwd(da_in, z, pool_w[l], pool_scale[l:l + 1], dz, f"pool_bwd_{l}",
                                 after=st_small and st_small["token"])
        small[l] = dict(g_post=acc_post[1], pool_w=dpw, pool_scale=dps[0], lb_logits=dlb[0], hgrn_norm_g=dgn[0])
        token = None
        if l == 0:
            parts = {name: jnp.stack([small[0][name], small[1][name]]) for name in small[0]}
            parts.update(b_ada=[None, d_ada[1]], g_pre=[None, small[1]["g_pre"]])
            sg_stream = _direct_gather_stream("sg")
            early = _pack_small(parts, 2)
            sg_bufs, (sg_sems,), token = _comm_call(
                "small_grads_start", dict(s_sg=early, g_sg=_with_own_slot(early, me)), start=[sg_stream])
        g_win = {f"win{l}": _in_proj_dw(h, dz, f"grad_w_in_{l}", after=token)}
        st_win = scat[f"win{l}"] = scatter_pair_start(f"win{l}", g_win if l == 0 else {**g_small, **g_win})
        if l > 0:
            dh = _in_proj_dh(dz, gw[f"win{l}"], seq, f"in_proj_dh_{l}", after=st_win["token"])
            scatter_pair_finish(st_win, dh)
            scatter_chips_start(st_win)
        else:
            scatter_pair_finish(st_win, st_win["token"])
            scatter_chips_start(st_win)
            after = st_win["token"]
            for unit in ("win1", "small0"):
                after = finish_unit(unit, after)
            dh = _in_proj_dh(dz, gw[f"win{l}"], seq, f"in_proj_dh_{l}", after=after)
        dx, acc_pre = _prenorm_bwd(xs[l], dh, dx, g_pre[l:l + 1], scale[l], tm, f"prenorm_bwd_{l}",
                                   after=st_win["token"])
        d_ada[l] = jnp.concatenate([acc_pre[0], acc_pre[1], acc_post[0]])
        small[l]["g_pre"] = acc_pre[2]
    grad_x = dx[None]

    parts = dict(b_ada=[d_ada[0]], g_pre=[small[0]["g_pre"]])
    late = jnp.concatenate([_pack_small(parts, 0, 2), jnp.broadcast_to(loss_part, (8, 128))], axis=0)
    g_late = _allgather_small(late, "allgather_late_grads")
    loss = jnp.sum(g_late[:, SMALL_LATE_ROWS, 0])
    sg_bufs, _, _ = _comm_call("small_grads_done", sg_bufs, wait=[(sg_stream, sg_sems)], after=g_late)
    g_early = sg_bufs["g_sg"]
    small_names = list(dict.fromkeys(name for name, _, _ in _SMALL_ROWS))
    weights = dict(b_ada=b_ada, g_pre=g_pre, g_post=g_post, pool_w=pool_w, pool_scale=pool_scale,
                   lb_logits=lb_logits, hgrn_norm_g=hgrn_norm_g)
    m_small = dict(b_ada=m_b_ada, g_pre=m_g_pre, g_post=m_g_post, pool_w=m_pool_w, pool_scale=m_pool_scale,
                   lb_logits=m_lb_logits, hgrn_norm_g=m_hgrn_norm_g)
    v_small = dict(b_ada=v_b_ada, g_pre=v_g_pre, g_post=v_g_post, pool_w=v_pool_w, pool_scale=v_pool_scale,
                   lb_logits=v_lb_logits, hgrn_norm_g=v_hgrn_norm_g)
    shapes = {name: weights[name].shape for name in small_names}
    small_out = _adamw_small(_pack_small(weights), _pack_small(m_small), _pack_small(v_small), g_late, g_early,
                             shapes)

    d_ada_all = jnp.stack([g_late[:, 0:24, :].reshape(N_DEV, 3 * D_MODEL),
                           g_early[:, 0:24, :].reshape(N_DEV, 3 * D_MODEL)], axis=1)
    d_cols = jnp.transpose(lax.dynamic_slice_in_dim(d_ada_all, me * ADA_COLS, ADA_COLS, axis=2), (1, 0, 2))
    g_w_ada = _ada_bwd(c_all, d_cols)
    ada_out = _adamw_sharded(w_ada, m_w_ada, v_w_ada, g_w_ada[:, None], 256, "adamw_w_ada")
    finish_unit("win0", ada_out[1][0, 0:8, 0:128] + small_out[1]["pool_scale"][0:1, 0:128])

    def leaf(kind):
        s = small_out[kind]
        return (ada_out[kind], s["b_ada"], s["g_pre"], s["g_post"], big_out["win"][kind], s["pool_w"], s["pool_scale"],
                s["lb_logits"], s["hgrn_norm_g"], big_out["wpo"][kind], big_out["who"][kind], big_out["wout"][kind])

    return (loss, grad_x) + leaf(0) + leaf(1) + leaf(2) + leaf(3)
```

```python
import jax
import jax.numpy as jnp
from jax import lax
from jax.experimental import pallas as pl
from jax.experimental.pallas import tpu as pltpu

F32 = jnp.float32
MXU_DTYPE = jnp.bfloat16
WIRE_DTYPE = jnp.bfloat16

N_DEV = 8
DEPTH = 2
D_MODEL = 1024
HEADS = 8
HEAD_DIM = 128
POOL_GROUPS = 4
GROUP_DIM = 128
POOL_WIDTH = POOL_GROUPS * GROUP_DIM
IN_WIDTH = 7168
CHUNK = 64
SUB = 16
N_SUB = CHUNK // SUB
FWD_STEP_CHUNKS = 8
BWD_STEP_CHUNKS = 4
EXP_CLAMP = 80.0
NORM_EPS = 1e-6
LOG_FLOOR = 1e-30
ADA_COLS = 3 * D_MODEL // N_DEV
IN_COLS = IN_WIDTH // N_DEV
COL_HQ, COL_HF, COL_HI, COL_HG, COL_MGP, COL_MGH = 1, 2, 3, 4, 5, 6

ADAM_LR = 0.001
ADAM_B1 = 0.9
ADAM_B2 = 0.999
ADAM_EPS = 1e-08
ADAM_WD = 0.01
ADAM_STEP = 10

VMEM_LIMIT = 48 * 1024 * 1024
MESH_ID = pl.DeviceIdType.MESH
HIGHEST = lax.Precision.HIGHEST

_SMALL_ROWS = (("b_ada", 0, 24), ("g_pre", 0, 8), ("b_ada", 1, 24), ("g_pre", 1, 8), ("g_post", None, 16),
               ("pool_w", None, 1024), ("pool_scale", None, 8), ("lb_logits", None, 16), ("hgrn_norm_g", None, 2))
SMALL_LATE_ROWS = 32
SMALL_ROWS_PAD = 1136
LB_ROW0 = 32 + 32 + 16 + 1024 + 8


def _params(**kw):
    return pltpu.CompilerParams(vmem_limit_bytes=VMEM_LIMIT, **kw)


def _sigmoid(v):
    return 1.0 / (1.0 + jnp.exp(-v))


def _dsilu(v, s):
    return s * (1.0 + v * (1.0 - s))


def _dot(a, b):
    return jnp.dot(a.astype(MXU_DTYPE), b.astype(MXU_DTYPE), preferred_element_type=F32)


def _dot_nt(a, b):
    return lax.dot_general(a.astype(MXU_DTYPE), b.astype(MXU_DTYPE), (((1,), (1,)), ((), ())),
                           preferred_element_type=F32)


def _dot_tn(a, b):
    return lax.dot_general(a.astype(MXU_DTYPE), b.astype(MXU_DTYPE), (((0,), (0,)), ((), ())),
                           preferred_element_type=F32)


def _pallas_after(body, n_in, after, *, in_specs, **kw):
    if after is None:
        return pl.pallas_call(body, in_specs=in_specs, **kw)

    def tied(*refs):
        body(*refs[:n_in], *refs[n_in + 1:])

    call = pl.pallas_call(tied, in_specs=list(in_specs) + [pl.BlockSpec(memory_space=pl.ANY)], **kw)
    return lambda *operands: call(*operands, after)


def _my_position():
    mx, my, mc = lax.axis_index("x"), lax.axis_index("y"), lax.axis_index("c")
    return mx, my, mc, 4 * mx + 2 * my + mc


def _peer(mx, my, mc, k):
    px = 1 - mx if (k >> 2) & 1 else mx
    py = 1 - my if (k >> 1) & 1 else my
    pc = 1 - mc if k & 1 else mc
    return (px, py, pc), 4 * px + 2 * py + pc


def _allgather_small(v, name, after=None):
    rows, cols = v.shape

    def body(v_ref, out_ref, send_sems, recv_sems):
        mx, my, mc, me = _my_position()
        out_ref[me] = v_ref[...]
        copies = []
        for k in range(1, N_DEV):
            peer, _ = _peer(mx, my, mc, k)
            cp = pltpu.make_async_remote_copy(
                src_ref=v_ref, dst_ref=out_ref.at[me],
                send_sem=send_sems.at[k - 1], recv_sem=recv_sems.at[k - 1],
                device_id=peer, device_id_type=MESH_ID)
            cp.start()
            copies.append(cp)
        for cp in copies:
            cp.wait()

    return _pallas_after(
        body, 1, after, name=name,
        out_shape=jax.ShapeDtypeStruct((N_DEV, rows, cols), v.dtype),
        in_specs=[pl.BlockSpec(memory_space=pltpu.VMEM)],
        out_specs=pl.BlockSpec(memory_space=pltpu.VMEM),
        scratch_shapes=[pltpu.SemaphoreType.DMA((N_DEV - 1,)), pltpu.SemaphoreType.DMA((N_DEV - 1,))],
        compiler_params=_params(),
    )(v)


class _Stream:
    def __init__(self, n, plan):
        self.n, self.plan = n, plan


def _comm_call(name, bufs, start=(), wait=(), after=None):
    names = list(bufs)

    def body(*refs):
        it = iter(refs)
        buf_refs = {n: next(it) for n in names}
        wait_sems = [(next(it), next(it)) for _ in wait]
        if after is not None:
            next(it)
        start_sems = [(next(it), next(it)) for _ in start]
        for _ in names:
            next(it)
        token = next(it)
        pos = _my_position()

        def descriptors(stream, sems):
            return [pltpu.make_async_remote_copy(src_ref=src, dst_ref=dst, send_sem=sems[0].at[k], recv_sem=sems[1].at[k],
                                                 device_id=dev, device_id_type=MESH_ID)
                    for k, (src, dst, dev) in enumerate(stream.plan(buf_refs, pos))]

        for (stream, _), sems in zip(wait, wait_sems):
            for cp in descriptors(stream, sems):
                cp.wait_send()
                cp.wait_recv()
        for stream, sems in zip(start, start_sems):
            for cp in descriptors(stream, sems):
                cp.start()
        token[...] = jnp.zeros_like(token)

    hbm = pl.BlockSpec(memory_space=pltpu.HBM)
    sem = pl.BlockSpec(memory_space=pltpu.SEMAPHORE)
    operands = [pltpu.with_memory_space_constraint(bufs[n], pltpu.HBM) for n in names]
    in_specs = [hbm] * len(names)
    for _, (send_sems, recv_sems) in wait:
        operands += [send_sems, recv_sems]
        in_specs += [sem, sem]
    if after is not None:
        operands.append(after)
        in_specs.append(pl.BlockSpec(memory_space=pl.ANY))
    out_shape, out_specs = [], []
    for stream in start:
        out_shape += [pltpu.SemaphoreType.DMA((stream.n,)), pltpu.SemaphoreType.DMA((stream.n,))]
        out_specs += [sem, sem]
    n_sem_out = len(out_shape)
    out_shape += [pltpu.HBM(bufs[n].shape, bufs[n].dtype) for n in names]
    out_specs += [hbm] * len(names)
    out_shape.append(jax.ShapeDtypeStruct((8, 128), F32))
    out_specs.append(pl.BlockSpec(memory_space=pltpu.VMEM))
    outs = pl.pallas_call(
        body, name=name, out_shape=out_shape, in_specs=in_specs, out_specs=out_specs,
        input_output_aliases={i: n_sem_out + i for i in range(len(names))},
        compiler_params=pltpu.CompilerParams(has_side_effects=pltpu.SideEffectType.DATAFLOW_SIDE_EFFECTING),
    )(*operands)
    sems = [(outs[2 * i], outs[2 * i + 1]) for i in range(len(start))]
    return dict(zip(names, outs[n_sem_out:n_sem_out + len(names)])), sems, outs[-1]


def _with_own_slot(block, me):
    return lax.dynamic_update_index_in_dim(lax.empty((N_DEV,) + block.shape, block.dtype), block, me, 0)


def _other_chips(pos):
    mx, my, _, _ = pos
    return [(1 - mx if i & 2 else mx, 1 - my if i & 1 else my) for i in (1, 2, 3)]


def _dev_index(px, py, pc):
    return 4 * px + 2 * py + pc


def _gather_streams(keys):
    def to_chips(refs, pos):
        _, _, mc, me = pos
        return [(refs["s_" + k], refs["g_" + k].at[me], (cx, cy, mc)) for k in keys for cx, cy in _other_chips(pos)]

    def to_sibling(refs, pos):
        mx, my, mc, me = pos
        return [(refs["s_" + k], refs["g_" + k].at[me], (mx, my, 1 - mc)) for k in keys]

    def pass_on(refs, pos):
        mx, my, mc, _ = pos
        out = []
        for k in keys:
            for cx, cy in _other_chips(pos):
                slot = refs["g_" + k].at[_dev_index(cx, cy, mc)]
                out.append((slot, slot, (mx, my, 1 - mc)))
        return out

    return _Stream(3 * len(keys), to_chips), _Stream(len(keys), to_sibling), _Stream(3 * len(keys), pass_on)


def _direct_gather_stream(key):
    def plan(refs, pos):
        mx, my, mc, me = pos
        return [(refs["s_" + key], refs["g_" + key].at[me], _peer(mx, my, mc, k)[0]) for k in range(1, N_DEV)]

    return _Stream(N_DEV - 1, plan)


def _scatter_streams(keys):
    def pair(refs, pos):
        mx, my, mc, _ = pos
        sib = (mx, my, 1 - mc)
        out = []
        for k in keys:
            for i, (cx, cy) in enumerate(_other_chips(pos)):
                out.append((refs["g_" + k].at[_dev_index(cx, cy, 1 - mc)], refs["st_" + k].at[i], sib))
            out.append((refs["g_" + k].at[_dev_index(mx, my, 1 - mc)], refs["st_" + k].at[3], sib))
        return out

    def chips(refs, pos):
        mc = pos[2]
        return [(refs["ps_" + k].at[i], refs["ld_" + k].at[i], (cx, cy, mc))
                for k in keys for i, (cx, cy) in enumerate(_other_chips(pos))]

    return _Stream(4 * len(keys), pair), _Stream(3 * len(keys), chips)


def _pair_sum(g, st, idx, tr, name):
    _, rows, cols = g.shape

    def body(idx_ref, g_ref, st_ref, out_ref):
        out_ref[...] = (g_ref[...].astype(F32) + st_ref[...].astype(F32)).astype(out_ref.dtype)

    return pl.pallas_call(
        body, name=name,
        grid_spec=pltpu.PrefetchScalarGridSpec(
            num_scalar_prefetch=1, grid=(4, rows // tr),
            in_specs=[pl.BlockSpec((None, tr, cols), lambda j, i, idx_ref: (idx_ref[j], i, 0)),
                      pl.BlockSpec((None, tr, cols), lambda j, i, idx_ref: (j, i, 0))],
            out_specs=pl.BlockSpec((None, tr, cols), lambda j, i, idx_ref: (j, i, 0))),
        out_shape=jax.ShapeDtypeStruct((4, rows, cols), WIRE_DTYPE),
        compiler_params=_params(dimension_semantics=("parallel", "parallel")),
    )(idx, g, st)


def _ada_fwd(c_all, w_ada, b_cols):
    def body(c_ref, w_ref, b_ref, out_ref):
        cv = c_ref[...]
        ca = cv * _sigmoid(cv)
        for l in range(DEPTH):
            out_ref[l] = jnp.dot(ca, w_ref[l], precision=HIGHEST, preferred_element_type=F32) + b_ref[l:l + 1, :]

    return pl.pallas_call(
        body, name="ada_fwd",
        out_shape=jax.ShapeDtypeStruct((DEPTH, N_DEV, ADA_COLS), F32),
        compiler_params=_params(),
    )(c_all, w_ada, b_cols)


def _ada_bwd(c_all, d_cols):
    def body(c_ref, d_ref, out_ref):
        cv = c_ref[...]
        ca = cv * _sigmoid(cv)
        for l in range(DEPTH):
            out_ref[l] = lax.dot_general(ca, d_ref[l], (((0,), (0,)), ((), ())), precision=HIGHEST,
                                         preferred_element_type=F32)

    return pl.pallas_call(
        body, name="ada_bwd",
        out_shape=jax.ShapeDtypeStruct((DEPTH, D_MODEL, ADA_COLS), F32),
        compiler_params=_params(),
    )(c_all, d_cols)


def _lower_bounds(logits):
    m = jnp.maximum(logits[0:1], logits[1:2])
    e0, e1 = jnp.exp(logits[0:1] - m), jnp.exp(logits[1:2] - m)
    den = e0 + e1
    p0, p1 = e0 / den, e1 / den
    low0 = p0 - p0
    low1 = (p0 + p1) - p0
    return (p0, p1), (low0, low1)


def _lb_fwd(lb_logits):
    def body(lg_ref, out_ref):
        _, (low0, low1) = _lower_bounds(lg_ref[...])
        out_ref[0:1, :] = jnp.clip(low0, 0.0, 1.0)
        out_ref[1:2, :] = jnp.clip(low1, 0.0, 1.0)

    return pl.pallas_call(body, name="lb_fwd", out_shape=jax.ShapeDtypeStruct(lb_logits.shape, F32),
                          compiler_params=_params())(lb_logits)


def _row_spec(cols=D_MODEL):
    return pl.BlockSpec((1, cols), lambda *_: (0, 0))


def _prenorm_fwd(x, g, shift, scale, tm, name, after=None):
    seq = x.shape[0]

    def body(x_ref, g_ref, sh_ref, sc_ref, h_ref):
        xv = x_ref[...]
        rs = lax.rsqrt(jnp.mean(xv * xv, axis=-1, keepdims=True) + NORM_EPS)
        h = (xv * rs * g_ref[...]) * (1.0 + sc_ref[...]) + sh_ref[...]
        h_ref[...] = h.astype(h_ref.dtype)

    tile = pl.BlockSpec((tm, D_MODEL), lambda i: (i, 0))
    return _pallas_after(
        body, 4, after, name=name, grid=(seq // tm,),
        in_specs=[tile, _row_spec(), _row_spec(), _row_spec()], out_specs=tile,
        out_shape=jax.ShapeDtypeStruct((seq, D_MODEL), MXU_DTYPE),
        compiler_params=_params(dimension_semantics=("parallel",)),
    )(x, g, shift, scale)


def _in_proj(h, win_g, tm, name, after=None):
    seq = h.shape[0]

    def body(h_ref, w_ref, z_ref, w_pair):
        @pl.when(pl.program_id(1) == 0)
        def _():
            w_pair[...] = jnp.concatenate([w_ref[0], w_ref[1]], axis=1)

        z_ref[...] = jnp.dot(h_ref[...], w_pair[...], preferred_element_type=F32)

    return _pallas_after(
        body, 2, after, name=name, grid=(N_DEV // 2, seq // tm),
        in_specs=[pl.BlockSpec((tm, D_MODEL), lambda j, i: (i, 0)),
                  pl.BlockSpec((2, D_MODEL, IN_COLS), lambda j, i: (j, 0, 0))],
        out_specs=pl.BlockSpec((tm, 2 * IN_COLS), lambda j, i: (i, j)),
        out_shape=jax.ShapeDtypeStruct((seq, IN_WIDTH), F32),
        scratch_shapes=[pltpu.VMEM((D_MODEL, 2 * IN_COLS), MXU_DTYPE)],
        compiler_params=_params(dimension_semantics=("parallel", "arbitrary")),
    )(h, win_g)


def _shift_down(v, j, pos):
    return jnp.where(pos >= j, pltpu.roll(v, j, 0), 0.0)


def _shift_up(v, j, pos, seq):
    return jnp.where(pos < seq - j, pltpu.roll(v, seq - j, 0), 0.0)


def _select_window(g, candidates):
    out = candidates[-1]
    for i in range(len(candidates) - 2, -1, -1):
        out = jnp.where(g == i, candidates[i], out)
    return out


def _pool_mean_minus_token(u, g, pos):
    sums, acc = [], u
    for j in (1, 2, 4, 8):
        acc = acc + _shift_down(acc, j, pos)
        sums.append(acc)
    wsum = _select_window(g, sums)
    width = jnp.left_shift(2, g).astype(F32)
    count = jnp.minimum(pos.astype(F32) + 1.0, width)
    return wsum / count - u, count


def _pool_fwd(z, pool_w_l, pool_scale_l, name, after=None):
    seq = z.shape[0]

    def body(pv_ref, pg_ref, w_ref, sc_ref, out_ref):
        g = pl.program_id(0)
        pos = lax.broadcasted_iota(jnp.int32, (seq, GROUP_DIM), 0)
        pm, _ = _pool_mean_minus_token(pv_ref[...], g, pos)
        lin = _dot(pm, w_ref[...]) * sc_ref[...]
        pg = pg_ref[...]
        out_ref[...] = (lin * (pg * _sigmoid(pg))).astype(out_ref.dtype)

    return _pallas_after(
        body, 4, after, name=name, grid=(POOL_GROUPS,),
        in_specs=[pl.BlockSpec((seq, GROUP_DIM), lambda g: (0, g)),
                  pl.BlockSpec((seq, GROUP_DIM), lambda g: (0, POOL_GROUPS + g)),
                  pl.BlockSpec((None, GROUP_DIM, GROUP_DIM), lambda g: (g, 0, 0)),
                  pl.BlockSpec((1, GROUP_DIM), lambda g: (0, g))],
        out_specs=pl.BlockSpec((seq, GROUP_DIM), lambda g: (0, g)),
        out_shape=jax.ShapeDtypeStruct((seq, POOL_WIDTH), MXU_DTYPE),
        compiler_params=_params(dimension_semantics=("parallel",)),
    )(z, z, pool_w_l, pool_scale_l)


def _chunk_masks():
    row = lax.broadcasted_iota(jnp.int32, (CHUNK, CHUNK), 0)
    col = lax.broadcasted_iota(jnp.int32, (CHUNK, CHUNK), 1)
    causal = row >= col
    before_sub = col < (row // SUB) * SUB
    suffix = row <= col
    return causal, before_sub, suffix


def _masked_sums(masks, v):
    lhs = jnp.concatenate([m.astype(jnp.bfloat16) for m in masks], axis=0)
    hi = v.astype(jnp.bfloat16)
    rest = v - hi.astype(F32)
    mid = rest.astype(jnp.bfloat16)
    lo = (rest - mid.astype(F32)).astype(jnp.bfloat16)
    out = jnp.dot(lhs, hi, preferred_element_type=F32)
    out += jnp.dot(lhs, mid, preferred_element_type=F32)
    out += jnp.dot(lhs, lo, preferred_element_type=F32)
    return [out[i * CHUNK:(i + 1) * CHUNK] for i in range(len(masks))]


def _gates(zf, lb):
    sg = _sigmoid(zf)
    f = lb + (1.0 - lb) * sg
    logf = jnp.log(jnp.maximum(f, LOG_FLOOR))
    return sg, f, logf


def _intra_blocks(q_h, k_h, cum_h, base_h, causal):
    rel = cum_h - base_h
    out = []
    for i in range(N_SUB):
        rows = slice(i * SUB, (i + 1) * SUB)
        e_q = jnp.exp(rel[rows])
        base_i = jnp.concatenate([base_h[rows]] * N_SUB, axis=0)
        e_k = jnp.exp(jnp.minimum(base_i - cum_h, EXP_CLAMP))
        q_t = (q_h[rows] * e_q).astype(MXU_DTYPE)
        k_t = (k_h * e_k).astype(MXU_DTYPE)
        a_i = jnp.where(causal[rows], _dot_nt(q_t, k_t), 0.0)
        out.append((q_t, k_t, e_q, e_k, a_i))
    return out


def _hgrn_fwd(z, lb_l, gn_l, name, after=None):
    seq = z.shape[0]
    n_chunks = seq // CHUNK
    per_step = min(FWD_STEP_CHUNKS, n_chunks)
    rows_per_step = per_step * CHUNK

    def body(hq_ref, hf_ref, hi_ref, hg_ref, lb_ref, gn_ref, o_ref, bin_ref, st_ref, state):
        @pl.when(pl.program_id(0) == 0)
        def _():
            state[...] = jnp.zeros_like(state)

        causal, before_sub, _ = _chunk_masks()
        for cc in range(per_step):
            rows = slice(cc * CHUNK, (cc + 1) * CHUNK)
            _, f, logf = _gates(hf_ref[rows, :], lb_ref[...])
            kk = 1.0 - f
            hq = hq_ref[rows, :]
            q = hq * _sigmoid(hq)
            cum, base = _masked_sums([causal, before_sub], logf)
            st_ref[cc] = state[...]
            for h in range(HEADS):
                sl = slice(h * HEAD_DIM, (h + 1) * HEAD_DIM)
                q_h, k_h, cum_h = q[:, sl], kk[:, sl], cum[:, sl]
                v_h = hi_ref[rows, sl]
                st_h = state[h]
                blocks = _intra_blocks(q_h, k_h, cum_h, base[:, sl], causal)
                a = jnp.concatenate([b[4] for b in blocks], axis=0)
                o_h = _dot_nt(q_h * jnp.exp(cum_h), st_h) + _dot(a, v_h)
                last = jnp.sum(logf[:, sl], axis=0, keepdims=True)
                state[h] = st_h * jnp.exp(last) + _dot_tn(v_h, k_h * jnp.exp(last - cum_h))
                rs = lax.rsqrt(jnp.mean(o_h * o_h, axis=-1, keepdims=True) + NORM_EPS)
                hg = hg_ref[rows, sl]
                o_ref[rows, sl] = o_h
                bin_ref[rows, sl] = ((o_h * rs * gn_ref[...]) * (hg * _sigmoid(hg))).astype(bin_ref.dtype)

    def col(block):
        return pl.BlockSpec((rows_per_step, D_MODEL), lambda c: (c, block))

    tile = pl.BlockSpec((rows_per_step, D_MODEL), lambda c: (c, 0))
    return _pallas_after(
        body, 6, after, name=name, grid=(n_chunks // per_step,),
        in_specs=[col(COL_HQ), col(COL_HF), col(COL_HI), col(COL_HG), _row_spec(), _row_spec(HEAD_DIM)],
        out_specs=[tile, tile, pl.BlockSpec((per_step, HEADS, HEAD_DIM, HEAD_DIM), lambda c: (c, 0, 0, 0))],
        out_shape=[jax.ShapeDtypeStruct((seq, D_MODEL), F32),
                   jax.ShapeDtypeStruct((seq, D_MODEL), MXU_DTYPE),
                   jax.ShapeDtypeStruct((n_chunks, HEADS, HEAD_DIM, HEAD_DIM), F32)],
        scratch_shapes=[pltpu.VMEM((HEADS, HEAD_DIM, HEAD_DIM), F32)],
        compiler_params=_params(dimension_semantics=("arbitrary",)),
    )(z, z, z, z, lb_l, gn_l)


def _rms_parts(y):
    rs = lax.rsqrt(jnp.mean(y * y, axis=-1, keepdims=True) + NORM_EPS)
    return rs, y * rs


def _merge_fwd(a_in, b_in, z, x, wpo_g, who_g, wout_g, gate, g_post, tm, name, target=None):
    seq = x.shape[0]
    with_loss = target is not None

    def body(*refs):
        a_ref, b_ref, mgp_ref, mgh_ref, x_ref, wpo_ref, who_ref, wout_ref, gate_ref, gp_ref = refs[:10]
        ba_ref, bb_ref, mer_ref, y_ref, last_ref = refs[10 + with_loss:15 + with_loss]
        a = a_ref[...]
        ba = jnp.concatenate([_dot(a, wpo_ref[j]) for j in range(N_DEV)], axis=1)
        bb = _dot(b_ref[...], who_ref[...])
        merged = _sigmoid(mgp_ref[...]) * ba + _sigmoid(mgh_ref[...]) * bb
        y = _dot(merged, wout_ref[...])
        _, yn = _rms_parts(y)
        ba_ref[...] = ba.astype(ba_ref.dtype)
        bb_ref[...] = bb.astype(bb_ref.dtype)
        mer_ref[...] = merged.astype(mer_ref.dtype)
        y_ref[...] = y.astype(y_ref.dtype)
        x_next = x_ref[...] + gate_ref[...] * (yn * gp_ref[...])
        if not with_loss:
            last_ref[...] = x_next
            return
        loss_ref = refs[16]

        @pl.when(pl.program_id(0) == 0)
        def _():
            loss_ref[...] = jnp.zeros_like(loss_ref)

        err = x_next - refs[10][...]
        loss_ref[...] += 0.5 * jnp.sum(jnp.mean(err * err, axis=-1, keepdims=True), axis=0, keepdims=True)
        last_ref[...] = err * (1.0 / D_MODEL)

    def tile(cols=D_MODEL, block=0):
        return pl.BlockSpec((tm, cols), lambda i: (i, block))

    full = pl.BlockSpec((D_MODEL, D_MODEL), lambda i: (0, 0))
    act = jax.ShapeDtypeStruct((seq, D_MODEL), MXU_DTYPE)
    f32 = jax.ShapeDtypeStruct((seq, D_MODEL), F32)
    one = [pl.BlockSpec((1, 1), lambda i: (0, 0))] if with_loss else []
    return pl.pallas_call(
        body, name=name, grid=(seq // tm,),
        in_specs=[tile(POOL_WIDTH), tile(), tile(block=COL_MGP), tile(block=COL_MGH), tile(),
                  pl.BlockSpec((N_DEV, POOL_WIDTH, GROUP_DIM), lambda i: (0, 0, 0)),
                  full, full, _row_spec(), _row_spec()] + ([tile()] if with_loss else []),
        out_specs=[tile(), tile(), tile(), tile(), tile()] + one,
        out_shape=[act, act, act, act, f32] + ([jax.ShapeDtypeStruct((1, 1), F32)] if with_loss else []),
        compiler_params=_params(dimension_semantics=("arbitrary" if with_loss else "parallel",)),
    )(a_in, b_in, z, z, x, wpo_g, who_g, wout_g, gate, g_post, *([target] if with_loss else []))


def _stage_copy(stage, sems, dst, slot, step, where):
    rows, cols = where(step)
    return pltpu.make_async_copy(stage.at[slot], dst.at[rows, cols], sems.at[slot])


def _stage_begin(stage, sems, dst, step, where):
    slot = step % 2

    @pl.when(step >= 2)
    def _():
        _stage_copy(stage, sems, dst, slot, step - 2, where).wait()

    return slot


def _stage_end(stage, sems, dst, step, n_steps, where):
    slot = step % 2
    _stage_copy(stage, sems, dst, slot, step, where).start()

    @pl.when(step == n_steps - 1)
    def _():
        _stage_copy(stage, sems, dst, slot, step, where).wait()
        if n_steps > 1:
            _stage_copy(stage, sems, dst, 1 - slot, step - 1, where).wait()


def _merge_bwd(dx, y, ba, bb, z, wpo_g, who_g, wout_g, gate, g_post, dz, tm, name):
    seq = dx.shape[0]
    n_steps = seq // tm

    def body(dx_ref, y_ref, ba_ref, bb_ref, mgp_ref, mgh_ref, wpo_ref, who_ref, wout_ref, gate_ref, gp_ref, _,
             dy_ref, dba_ref, dbb_ref, da_ref, db_ref, dz_ref, acc_ref, stage, sems):
        step = pl.program_id(0)

        @pl.when(step == 0)
        def _():
            acc_ref[...] = jnp.zeros_like(acc_ref)

        def where(t):
            return pl.ds(t * tm, tm), pl.ds(COL_MGP * D_MODEL, 2 * D_MODEL)

        dmg_ref = stage.at[_stage_begin(stage, sems, dz_ref, step, where)]

        dxv = dx_ref[...]
        rs, yn = _rms_parts(y_ref[...].astype(F32))
        acc_ref[0:1, :] += jnp.sum(dxv * yn * gp_ref[...], axis=0, keepdims=True)
        acc_ref[1:2, :] += jnp.sum(dxv * gate_ref[...] * yn, axis=0, keepdims=True)
        dyn = dxv * (gate_ref[...] * gp_ref[...])
        dy = rs * (dyn - yn * jnp.mean(dyn * yn, axis=-1, keepdims=True))
        dmerged = _dot_nt(dy, wout_ref[...])
        sp, sh = _sigmoid(mgp_ref[...]), _sigmoid(mgh_ref[...])
        dba, dbb = sp * dmerged, sh * dmerged
        dmg_ref[:, 0:D_MODEL] = (dmerged * ba_ref[...].astype(F32) * sp * (1.0 - sp)).astype(dmg_ref.dtype)
        dmg_ref[:, D_MODEL:2 * D_MODEL] = (dmerged * bb_ref[...].astype(F32) * sh * (1.0 - sh)).astype(dmg_ref.dtype)
        da = _dot_nt(dba[:, 0:GROUP_DIM], wpo_ref[0])
        for j in range(1, N_DEV):
            da += _dot_nt(dba[:, j * GROUP_DIM:(j + 1) * GROUP_DIM], wpo_ref[j])
        dy_ref[...] = dy.astype(dy_ref.dtype)
        dba_ref[...] = dba.astype(dba_ref.dtype)
        dbb_ref[...] = dbb.astype(dbb_ref.dtype)
        da_ref[...] = da.astype(da_ref.dtype)
        db_ref[...] = _dot_nt(dbb, who_ref[...]).astype(db_ref.dtype)
        _stage_end(stage, sems, dz_ref, step, n_steps, where)

    def tile(cols=D_MODEL, block=0):
        return pl.BlockSpec((tm, cols), lambda i: (i, block))

    full = pl.BlockSpec((D_MODEL, D_MODEL), lambda i: (0, 0))
    hbm = pl.BlockSpec(memory_space=pl.ANY)
    act = jax.ShapeDtypeStruct((seq, D_MODEL), MXU_DTYPE)
    return pl.pallas_call(
        body, name=name, grid=(n_steps,),
        in_specs=[tile(), tile(), tile(), tile(), tile(block=COL_MGP), tile(block=COL_MGH),
                  pl.BlockSpec((N_DEV, POOL_WIDTH, GROUP_DIM), lambda i: (0, 0, 0)),
                  full, full, _row_spec(), _row_spec(), hbm],
        out_specs=[tile(), tile(), tile(), tile(POOL_WIDTH), tile(), hbm,
                   pl.BlockSpec((8, D_MODEL), lambda i: (0, 0))],
        out_shape=[act, act, act, jax.ShapeDtypeStruct((seq, POOL_WIDTH), MXU_DTYPE), act,
                   jax.ShapeDtypeStruct(dz.shape, dz.dtype),
                   jax.ShapeDtypeStruct((8, D_MODEL), F32)],
        input_output_aliases={11: 5},
        scratch_shapes=[pltpu.VMEM((2, tm, 2 * D_MODEL), MXU_DTYPE), pltpu.SemaphoreType.DMA((2,))],
        compiler_params=_params(dimension_semantics=("arbitrary",)),
    )(dx, y, ba, bb, z, z, wpo_g, who_g, wout_g, gate, g_post, dz)


def _grad_out_weights(merged, dy, b_in, dbb, a_in, dba, name):
    seq = merged.shape[0]
    tn = D_MODEL // 2
    per_step = tn // GROUP_DIM

    def body(mer_ref, dy_ref, b_ref, dbb_ref, a_ref, dba_ref, gout_ref, gho_ref, gpo_ref):
        gout_ref[...] = _dot_tn(mer_ref[...], dy_ref[...]).astype(gout_ref.dtype)
        gho_ref[...] = _dot_tn(b_ref[...], dbb_ref[...]).astype(gho_ref.dtype)
        g_po = _dot_tn(a_ref[...], dba_ref[...])
        for j in range(per_step):
            gpo_ref[j] = g_po[:, j * GROUP_DIM:(j + 1) * GROUP_DIM].astype(gpo_ref.dtype)

    def whole(cols):
        return pl.BlockSpec((seq, cols), lambda j: (0, 0))

    cols = pl.BlockSpec((seq, tn), lambda j: (0, j))
    return pl.pallas_call(
        body, name=name, grid=(D_MODEL // tn,),
        in_specs=[whole(D_MODEL), cols, whole(D_MODEL), cols, whole(POOL_WIDTH), cols],
        out_specs=[pl.BlockSpec((D_MODEL, tn), lambda j: (0, j)), pl.BlockSpec((D_MODEL, tn), lambda j: (0, j)),
                   pl.BlockSpec((per_step, POOL_WIDTH, GROUP_DIM), lambda j: (j, 0, 0))],
        out_shape=[jax.ShapeDtypeStruct((D_MODEL, D_MODEL), WIRE_DTYPE),
                   jax.ShapeDtypeStruct((D_MODEL, D_MODEL), WIRE_DTYPE),
                   jax.ShapeDtypeStruct((N_DEV, POOL_WIDTH, GROUP_DIM), WIRE_DTYPE)],
        compiler_params=_params(dimension_semantics=("parallel",)),
    )(merged, dy, b_in, dbb, a_in, dba)


def _hgrn_bwd(db_in, z, o, states, lb_l, gn_l, dz, name, after=None):
    seq = z.shape[0]
    per_step = min(BWD_STEP_CHUNKS, seq // CHUNK)
    rows_per_step = per_step * CHUNK
    n_steps = seq // rows_per_step
    last_step = n_steps - 1

    def body(db_ref, hq_ref, hf_ref, hi_ref, hg_ref, o_ref, st_ref, lb_ref, gn_ref, _,
             dz_hbm, dlb_ref, dgn_ref, dstate, dq_buf, dk_buf, dg_buf, stage, sems):
        step = pl.program_id(0)

        @pl.when(step == 0)
        def _():
            dstate[...] = jnp.zeros_like(dstate)
            dlb_ref[...] = jnp.zeros_like(dlb_ref)
            dgn_ref[...] = jnp.zeros_like(dgn_ref)

        def one_chunk(cc, *args):
            one_chunk_body((db_ref, hq_ref, hf_ref, hi_ref, hg_ref, o_ref, st_ref, dlb_ref, dgn_ref, dstate,
                            dq_buf, dk_buf, dg_buf), cc, *args)

        def where(t):
            return pl.ds((last_step - t) * rows_per_step, rows_per_step), pl.ds(COL_HQ * D_MODEL, 4 * D_MODEL)

        dz_step = stage.at[_stage_begin(stage, sems, dz_hbm, step, where)]
        causal, before_sub, suffix = _chunk_masks()
        lb = lb_ref[...]
        gn = gn_ref[...]
        for cc in reversed(range(per_step)):
            one_chunk(cc, dz_step, causal, before_sub, suffix, lb, gn)
        _stage_end(stage, sems, dz_hbm, step, n_steps, where)

    def one_chunk_body(refs, cc, dz_step, causal, before_sub, suffix, lb, gn):
        (db_ref, hq_ref, hf_ref, hi_ref, hg_ref, o_ref, st_ref, dlb_ref, dgn_ref, dstate, dq_buf, dk_buf, dg_buf) = refs
        rows = slice(cc * CHUNK, (cc + 1) * CHUNK)
        dz_ref = dz_step.at[rows, :]
        dq_buf, dk_buf, dg_buf = dq_buf.at[cc], dk_buf.at[cc], dg_buf.at[cc]
        sg, f, logf = _gates(hf_ref[rows, :], lb)
        kk = 1.0 - f
        hq = hq_ref[rows, :]
        sq = _sigmoid(hq)
        q = hq * sq
        cum, base = _masked_sums([causal, before_sub], logf)
        dgn = jnp.zeros((1, HEAD_DIM), F32)
        dlast = []
        for h in range(HEADS):
            sl = slice(h * HEAD_DIM, (h + 1) * HEAD_DIM)
            q_h, k_h, cum_h = q[:, sl], kk[:, sl], cum[:, sl]
            v_h = hi_ref[rows, sl]
            st_h = st_ref[cc, h]
            dst_h = dstate[h]
            rs, ohat = _rms_parts(o_ref[rows, sl])
            hg = hg_ref[rows, sl]
            shg = _sigmoid(hg)
            d_bin = db_ref[rows, sl].astype(F32)
            don = d_bin * (hg * shg)
            dgn += jnp.sum(don * ohat, axis=0, keepdims=True)
            dohat = don * gn
            do = rs * (dohat - ohat * jnp.mean(dohat * ohat, axis=-1, keepdims=True))
            dz_ref[:, 3 * D_MODEL + h * HEAD_DIM:3 * D_MODEL + (h + 1) * HEAD_DIM] = (
                d_bin * (ohat * gn) * _dsilu(hg, shg)).astype(dz_ref.dtype)
            last = jnp.sum(logf[:, sl], axis=0, keepdims=True)
            g_in = jnp.exp(cum_h)
            d_out = jnp.exp(last - cum_h)
            q_bar, k_bar = q_h * g_in, k_h * d_out
            blocks = _intra_blocks(q_h, k_h, cum_h, base[:, sl], causal)
            a = jnp.concatenate([b[4] for b in blocks], axis=0)
            da = jnp.where(causal, _dot_nt(do, v_h), 0.0)
            dv = _dot_tn(a, do) + _dot_nt(k_bar, dst_h)
            dq_bar, dk_bar = _dot(do, st_h), _dot(v_h, dst_h)
            dk = dk_bar * d_out
            dq_parts, dg_parts = [], []
            dg_k = k_bar * dk_bar
            dlast.append(jnp.sum(k_bar * dk_bar, axis=0, keepdims=True)
                         + jnp.exp(last) * jnp.sum(st_h * dst_h, axis=0, keepdims=True))
            for i, (q_t, k_t, e_q, e_k, _) in enumerate(blocks):
                da_i = da[i * SUB:(i + 1) * SUB].astype(MXU_DTYPE)
                dq_t = _dot(da_i, k_t)
                dk_t = _dot_tn(da_i, q_t)
                dq_parts.append(dq_t * e_q)
                dk += dk_t * e_k
                dg_parts.append(q_t.astype(F32) * dq_t)
                dg_k += k_t.astype(F32) * dk_t
            dq = dq_bar * g_in + jnp.concatenate(dq_parts, axis=0)
            dg_buf[:, sl] = q_bar * dq_bar + jnp.concatenate(dg_parts, axis=0) - dg_k
            dstate[h] = dst_h * jnp.exp(last) + _dot_tn(do, q_bar)
            dq_buf[:, sl] = dq
            dk_buf[:, sl] = dk
            dz_ref[:, 2 * D_MODEL + h * HEAD_DIM:2 * D_MODEL + (h + 1) * HEAD_DIM] = dv.astype(dz_ref.dtype)
        dgn_ref[...] += dgn
        dq_all, dk_all = dq_buf[...], dk_buf[...]
        dlogf = _masked_sums([suffix], dg_buf[...])[0] + jnp.concatenate(dlast, axis=1)
        df = jnp.where(f > LOG_FLOOR, dlogf / f, 0.0) - dk_all
        dlb_ref[...] += jnp.sum(df * (1.0 - sg), axis=0, keepdims=True)
        dz_ref[:, 0:D_MODEL] = (dq_all * _dsilu(hq, sq)).astype(dz_ref.dtype)
        dz_ref[:, D_MODEL:2 * D_MODEL] = (df * (1.0 - lb) * sg * (1.0 - sg)).astype(dz_ref.dtype)

    def col(block):
        return pl.BlockSpec((rows_per_step, D_MODEL), lambda c: (last_step - c, block))

    hbm = pl.BlockSpec(memory_space=pl.ANY)
    return _pallas_after(
        body, 10, after, name=name, grid=(n_steps,),
        in_specs=[col(0), col(COL_HQ), col(COL_HF), col(COL_HI), col(COL_HG), col(0),
                  pl.BlockSpec((per_step, HEADS, HEAD_DIM, HEAD_DIM), lambda c: (last_step - c, 0, 0, 0)),
                  _row_spec(), _row_spec(HEAD_DIM), hbm],
        out_specs=[hbm, _row_spec(), _row_spec(HEAD_DIM)],
        out_shape=[jax.ShapeDtypeStruct(dz.shape, dz.dtype),
                   jax.ShapeDtypeStruct((1, D_MODEL), F32), jax.ShapeDtypeStruct((1, HEAD_DIM), F32)],
        input_output_aliases={9: 0},
        scratch_shapes=[pltpu.VMEM((HEADS, HEAD_DIM, HEAD_DIM), F32)]
        + [pltpu.VMEM((per_step, CHUNK, D_MODEL), F32)] * 3
        + [pltpu.VMEM((2, rows_per_step, 4 * D_MODEL), MXU_DTYPE), pltpu.SemaphoreType.DMA((2,))],
        compiler_params=_params(dimension_semantics=("arbitrary",)),
    )(db_in, z, z, z, z, o, states, lb_l, gn_l, dz)


def _pool_bwd(da_in, z, pool_w_l, pool_scale_l, dz, name, after=None):
    seq = z.shape[0]

    def body(da_ref, pv_ref, pg_ref, w_ref, sc_ref, _, dz_hbm, dw_ref, dsc_ref, stage_pv, stage_pg, sems_pv, sems_pg):
        g = pl.program_id(0)

        def where_pv(t):
            return pl.ds(0, seq), pl.ds(pl.multiple_of(t * GROUP_DIM, GROUP_DIM), GROUP_DIM)

        def where_pg(t):
            return pl.ds(0, seq), pl.ds(pl.multiple_of(POOL_WIDTH + t * GROUP_DIM, GROUP_DIM), GROUP_DIM)

        dpv_ref = stage_pv.at[_stage_begin(stage_pv, sems_pv, dz_hbm, g, where_pv)]
        dpg_ref = stage_pg.at[_stage_begin(stage_pg, sems_pg, dz_hbm, g, where_pg)]
        pos = lax.broadcasted_iota(jnp.int32, (seq, GROUP_DIM), 0)
        pm, count = _pool_mean_minus_token(pv_ref[...], g, pos)
        lin0 = _dot(pm, w_ref[...])
        pg = pg_ref[...]
        spg = _sigmoid(pg)
        da = da_ref[...].astype(F32)
        dlin = da * (pg * spg)
        dpg_ref[...] = (da * (lin0 * sc_ref[...]) * _dsilu(pg, spg)).astype(dpg_ref.dtype)
        dsc_ref[...] = jnp.sum(dlin * lin0, axis=0, keepdims=True)
        dl0 = dlin * sc_ref[...]
        dw_ref[...] = _dot_tn(pm, dl0)
        dpm = _dot_nt(dl0, w_ref[...])
        sums, acc = [], dpm / count
        for j in (1, 2, 4, 8):
            acc = acc + _shift_up(acc, j, pos, seq)
            sums.append(acc)
        dpv_ref[...] = (_select_window(g, sums) - dpm).astype(dpv_ref.dtype)
        _stage_end(stage_pv, sems_pv, dz_hbm, g, POOL_GROUPS, where_pv)
        _stage_end(stage_pg, sems_pg, dz_hbm, g, POOL_GROUPS, where_pg)

    grp = pl.BlockSpec((seq, GROUP_DIM), lambda g: (0, g))
    hbm = pl.BlockSpec(memory_space=pl.ANY)
    stage = pltpu.VMEM((2, seq, GROUP_DIM), MXU_DTYPE)
    return _pallas_after(
        body, 6, after, name=name, grid=(POOL_GROUPS,),
        in_specs=[grp, grp, pl.BlockSpec((seq, GROUP_DIM), lambda g: (0, POOL_GROUPS + g)),
                  pl.BlockSpec((None, GROUP_DIM, GROUP_DIM), lambda g: (g, 0, 0)),
                  pl.BlockSpec((1, GROUP_DIM), lambda g: (0, g)), hbm],
        out_specs=[hbm, pl.BlockSpec((None, GROUP_DIM, GROUP_DIM), lambda g: (g, 0, 0)),
                   pl.BlockSpec((1, GROUP_DIM), lambda g: (0, g))],
        out_shape=[jax.ShapeDtypeStruct(dz.shape, dz.dtype),
                   jax.ShapeDtypeStruct((POOL_GROUPS, GROUP_DIM, GROUP_DIM), F32),
                   jax.ShapeDtypeStruct((1, POOL_WIDTH), F32)],
        input_output_aliases={5: 0},
        scratch_shapes=[stage, stage, pltpu.SemaphoreType.DMA((2,)), pltpu.SemaphoreType.DMA((2,))],
        compiler_params=_params(dimension_semantics=("arbitrary",)),
    )(da_in, z, z, pool_w_l, pool_scale_l, dz)


def _in_proj_dw(h, dz, name, after=None):
    seq = h.shape[0]

    def body(h_ref, dz_ref, out_ref):
        pair = lax.dot_general(h_ref[...], dz_ref[...], (((0,), (0,)), ((), ())), preferred_element_type=F32)
        out_ref[0] = pair[:, 0:IN_COLS].astype(out_ref.dtype)
        out_ref[1] = pair[:, IN_COLS:].astype(out_ref.dtype)

    return _pallas_after(
        body, 2, after, name=name, grid=(N_DEV // 2,),
        in_specs=[pl.BlockSpec((seq, D_MODEL), lambda j: (0, 0)),
                  pl.BlockSpec((seq, 2 * IN_COLS), lambda j: (0, j))],
        out_specs=pl.BlockSpec((2, D_MODEL, IN_COLS), lambda j: (j, 0, 0)),
        out_shape=jax.ShapeDtypeStruct((N_DEV, D_MODEL, IN_COLS), WIRE_DTYPE),
        compiler_params=_params(dimension_semantics=("parallel",)),
    )(h, dz)


def _in_proj_dh(dz, win_g, tm, name, after=None):
    seq = dz.shape[0]

    def body(dz_ref, w_ref, dh_ref):
        @pl.when(pl.program_id(1) == 0)
        def _():
            dh_ref[...] = jnp.zeros_like(dh_ref)

        w_pair = jnp.concatenate([w_ref[0], w_ref[1]], axis=1)
        dh_ref[...] += lax.dot_general(dz_ref[...], w_pair, (((1,), (1,)), ((), ())), preferred_element_type=F32)

    return _pallas_after(
        body, 2, after, name=name, grid=(seq // tm, N_DEV // 2),
        in_specs=[pl.BlockSpec((tm, 2 * IN_COLS), lambda i, j: (i, j)),
                  pl.BlockSpec((2, D_MODEL, IN_COLS), lambda i, j: (j, 0, 0))],
        out_specs=pl.BlockSpec((tm, D_MODEL), lambda i, j: (i, 0)),
        out_shape=jax.ShapeDtypeStruct((seq, D_MODEL), F32),
        compiler_params=_params(dimension_semantics=("parallel", "arbitrary")),
    )(dz, win_g)


def _prenorm_bwd(x, dh, dx_res, g, scale, tm, name, after=None):
    seq = x.shape[0]

    def body(x_ref, dh_ref, dxr_ref, g_ref, sc_ref, dx_ref, acc_ref):
        @pl.when(pl.program_id(0) == 0)
        def _():
            acc_ref[...] = jnp.zeros_like(acc_ref)

        rs, xn = _rms_parts(x_ref[...])
        dh = dh_ref[...]
        acc_ref[0:1, :] += jnp.sum(dh, axis=0, keepdims=True)
        acc_ref[1:2, :] += jnp.sum(dh * (xn * g_ref[...]), axis=0, keepdims=True)
        dhn = dh * (1.0 + sc_ref[...])
        acc_ref[2:3, :] += jnp.sum(dhn * xn, axis=0, keepdims=True)
        dxn = dhn * g_ref[...]
        dx_ref[...] = rs * (dxn - xn * jnp.mean(dxn * xn, axis=-1, keepdims=True)) + dxr_ref[...]

    tile = pl.BlockSpec((tm, D_MODEL), lambda i: (i, 0))
    return _pallas_after(
        body, 5, after, name=name, grid=(seq // tm,),
        in_specs=[tile, tile, tile, _row_spec(), _row_spec()],
        out_specs=[tile, pl.BlockSpec((8, D_MODEL), lambda i: (0, 0))],
        out_shape=[jax.ShapeDtypeStruct((seq, D_MODEL), F32), jax.ShapeDtypeStruct((8, D_MODEL), F32)],
        compiler_params=_params(dimension_semantics=("arbitrary",)),
    )(x, dh, dx_res, g, scale)


def _adamw_math(w, g, m, v):
    m = ADAM_B1 * m + (1.0 - ADAM_B1) * g
    v = ADAM_B2 * v + (1.0 - ADAM_B2) * (g * g)
    m_hat = m / (1.0 - ADAM_B1 ** ADAM_STEP)
    v_hat = v / (1.0 - ADAM_B2 ** ADAM_STEP)
    delta = -ADAM_LR * (m_hat / (jnp.sqrt(v_hat) + ADAM_EPS) + ADAM_WD * w)
    return delta, m, v


def _adamw_sharded(w, m, v, contrib, tr, name):
    depth, rows, cols = w.shape
    n_parts = contrib.shape[1]

    def body(w_ref, m_ref, v_ref, c_ref, g_ref, d_ref, mo_ref, vo_ref):
        g = c_ref[0].astype(F32)
        for p in range(1, n_parts):
            g += c_ref[p].astype(F32)
        delta, mn, vn = _adamw_math(w_ref[...], g, m_ref[...], v_ref[...])
        g_ref[...] = g
        d_ref[...] = delta
        mo_ref[...] = mn
        vo_ref[...] = vn

    tile = pl.BlockSpec((None, tr, cols), lambda l, i: (l, i, 0))
    shape = jax.ShapeDtypeStruct(w.shape, F32)
    return pl.pallas_call(
        body, name=name, grid=(depth, rows // tr),
        in_specs=[tile, tile, tile, pl.BlockSpec((None, n_parts, tr, cols), lambda l, i: (l, 0, i, 0))],
        out_specs=[tile] * 4, out_shape=[shape] * 4,
        compiler_params=_params(dimension_semantics=("parallel", "parallel")),
    )(w, m, v, contrib)


def _adamw_layer(w, m, v, contribs, l, tr, name, prev=None):
    _, rows, cols = w.shape
    n = len(contribs)

    def body(*refs):
        w_ref, m_ref, v_ref = refs[:3]
        c_refs = refs[3:3 + n]
        g_ref, d_ref, mo_ref, vo_ref = refs[-4:]
        g = c_refs[0][...].astype(F32)
        for c_ref in c_refs[1:]:
            g += c_ref[...].astype(F32)
        delta, mn, vn = _adamw_math(w_ref[...], g, m_ref[...], v_ref[...])
        g_ref[...] = g
        d_ref[...] = delta
        mo_ref[...] = mn
        vo_ref[...] = vn

    tile = pl.BlockSpec((None, tr, cols), lambda i: (l, i, 0))
    in_specs = [tile, tile, tile] + [pl.BlockSpec((None, tr, cols), lambda i, s=slot: (s, i, 0)) for _, slot in contribs]
    operands = [w, m, v] + [arr for arr, _ in contribs]
    aliases = {}
    if prev is not None:
        aliases = {len(operands) + k: k for k in range(4)}
        in_specs += [pl.BlockSpec(memory_space=pl.ANY)] * 4
        operands += list(prev)
    shape = jax.ShapeDtypeStruct(w.shape, F32)
    return pl.pallas_call(
        body, name=name, grid=(rows // tr,), in_specs=in_specs, out_specs=[tile] * 4, out_shape=[shape] * 4,
        input_output_aliases=aliases,
        compiler_params=_params(dimension_semantics=("parallel",)),
    )(*operands)


def _adamw_small(w_pack, m_pack, v_pack, g_late, g_early, shapes):
    pieces, r = {}, 0
    for name, _, n in _SMALL_ROWS:
        pieces.setdefault(name, []).append((r, n))
        r += n
    names = list(pieces)

    def body(w_ref, m_ref, v_ref, gl_ref, ge_ref, *rest):
        outs, packs = rest[:4 * len(names)], rest[4 * len(names):]
        g_l, g_e = gl_ref[0][0:SMALL_LATE_ROWS], ge_ref[0]
        for d in range(1, N_DEV):
            g_l += gl_ref[d][0:SMALL_LATE_ROWS]
            g_e += ge_ref[d]
        g = jnp.concatenate([g_l, g_e], axis=0)
        w = w_ref[...]
        r0, r1, r2 = LB_ROW0, LB_ROW0 + 8, LB_ROW0 + 16
        lg0, lg1 = w[r0:r1], w[r1:r2]
        mx = jnp.maximum(lg0, lg1)
        e0, e1 = jnp.exp(lg0 - mx), jnp.exp(lg1 - mx)
        p0, p1 = e0 / (e0 + e1), e1 / (e0 + e1)
        low = ((p0 - p0), (p0 + p1) - p0)
        dlow = [g_rows * jnp.where((lo > 0.0) & (lo < 1.0), 1.0, jnp.where((lo == 0.0) | (lo == 1.0), 0.5, 0.0))
                for g_rows, lo in ((g[r0:r1], low[0]), (g[r1:r2], low[1]))]
        dp0 = (dlow[0] + dlow[1]) - (dlow[0] + dlow[1])
        dp1 = dlow[1]
        inner = p0 * dp0 + p1 * dp1
        g = jnp.concatenate([g[:r0], p0 * (dp0 - inner), p1 * (dp1 - inner), g[r2:]], axis=0)
        delta, mn, vn = _adamw_math(w, g, m_ref[...], v_ref[...])
        for kind, val in enumerate((g, delta, mn, vn)):
            packs[kind][...] = val
            for j, name in enumerate(names):
                out, at = outs[kind * len(names) + j], 0
                for start, n in pieces[name]:
                    if name in flat:
                        for r in range(n):
                            layer, c = divmod(at + r, flat[name])
                            out[layer:layer + 1, c * 128:(c + 1) * 128] = packs[kind][start + r:start + r + 1, :]
                    else:
                        out[at:at + n, :] = packs[kind][start:start + n, :]
                    at += n

    rows = {name: sum(n for _, n in pieces[name]) for name in names}
    flat = {name: rows[name] // DEPTH for name in names if len(shapes[name]) == 2}
    outs = pl.pallas_call(
        body, name="adamw_small",
        out_shape=[jax.ShapeDtypeStruct(shapes[name] if name in flat else (rows[name], 128), F32)
                   for _ in range(4) for name in names],
        scratch_shapes=[pltpu.VMEM(w_pack.shape, F32)] * 4, compiler_params=_params(),
    )(w_pack, m_pack, v_pack, g_late, g_early)
    return [{name: outs[kind * len(names) + j].reshape(shapes[name]) for j, name in enumerate(names)}
            for kind in range(4)]


def _pack_small(parts, first=0, last=len(_SMALL_ROWS)):
    rows = [(parts[name] if l is None else parts[name][l]).reshape(n, 128) for name, l, n in _SMALL_ROWS[first:last]]
    if last == len(_SMALL_ROWS):
        rows.append(jnp.zeros((SMALL_ROWS_PAD - sum(n for _, _, n in _SMALL_ROWS), 128), F32))
    return jnp.concatenate(rows, axis=0)


def kernel(x, c, w_ada, b_ada, g_pre, g_post, w_in, pool_w, pool_scale, lb_logits, hgrn_norm_g, w_pool_o, w_hgrn_o, w_out, loss_target, m_w_ada, m_b_ada, m_g_pre, m_g_post, m_w_in, m_pool_w, m_pool_scale, m_lb_logits, m_hgrn_norm_g, m_w_pool_o, m_w_hgrn_o, m_w_out, v_w_ada, v_b_ada, v_g_pre, v_g_post, v_w_in, v_pool_w, v_pool_scale, v_lb_logits, v_hgrn_norm_g, v_w_pool_o, v_w_hgrn_o, v_w_out):
    seq = x.shape[1]
    tm = min(512, seq)
    tm_merge = min(256, seq)
    pos = _my_position()
    me = pos[3]

    c_all = _allgather_small(c, "allgather_c").reshape(N_DEV, D_MODEL)
    b_cols = lax.dynamic_slice_in_dim(b_ada, me * ADA_COLS, ADA_COLS, axis=1)
    ada_part = _ada_fwd(c_all, w_ada, b_cols)
    ada_all = _allgather_small(ada_part.reshape(DEPTH * N_DEV, ADA_COLS), "allgather_ada")
    ada = lax.dynamic_index_in_dim(ada_all.reshape(N_DEV, DEPTH, N_DEV, ADA_COLS), me, axis=2, keepdims=False)
    ada = jnp.transpose(ada, (1, 0, 2)).reshape(DEPTH, 3 * D_MODEL)
    shift = [ada[l:l + 1, 0:D_MODEL] for l in range(DEPTH)]
    scale = [ada[l:l + 1, D_MODEL:2 * D_MODEL] for l in range(DEPTH)]
    gate = [ada[l:l + 1, 2 * D_MODEL:] for l in range(DEPTH)]

    big = dict(win=w_in, wpo=w_pool_o, who=w_hgrn_o, wout=w_out)
    units = [["win0"], ["wpo0", "who0", "wout0"], ["win1", "wpo1", "who1", "wout1"]]
    g_streams = [_gather_streams(keys) for keys in units]
    g_state = [None] * len(units)

    def gather_start(u, after):
        bufs = {}
        for k in units[u]:
            arr = big[k[:-1]]
            bufs["s_" + k] = arr[int(k[-1])].astype(WIRE_DTYPE)
            bufs["g_" + k] = _with_own_slot(bufs["s_" + k], me)
        bufs, sems, token = _comm_call(f"gather_start_{u}", bufs, start=list(g_streams[u][:2]), after=after)
        g_state[u] = dict(bufs=bufs, sems=sems)
        return token

    def gather_pass(u, after):
        st = g_state[u]
        to_chips, _, pass_on = g_streams[u]
        st["bufs"], (st["pass_sems"],), _ = _comm_call(f"gather_pass_{u}", st["bufs"], start=[pass_on],
                                                       wait=[(to_chips, st["sems"][0])], after=after)

    def gather_done(u, after=None):
        st = g_state[u]
        _, to_sibling, pass_on = g_streams[u]
        bufs, _, _ = _comm_call(f"gather_done_{u}", st["bufs"], after=after,
                                wait=[(to_sibling, st["sems"][1]), (pass_on, st["pass_sems"])])
        return {k: bufs["g_" + k] for k in units[u]}

    token = gather_start(0, ada_all)

    lb = _lb_fwd(lb_logits)

    gw = {}
    xs, saved = [x[0]], []
    for l in range(DEPTH):
        h = _prenorm_fwd(xs[l], g_pre[l:l + 1], shift[l], scale[l], tm, f"prenorm_fwd_{l}",
                         after=token if l == 0 else None)
        token = None
        if l == 0:
            gather_pass(0, h)
            gw.update(gather_done(0))
            token = gather_start(1, gw["win0"])
        else:
            gw.update(gather_done(2, h))
        z = _in_proj(h, gw[f"win{l}"], min(1024, seq), f"in_proj_{l}", after=token)
        if l == 0:
            gather_pass(1, z)
            token = gather_start(2, g_state[1]["bufs"]["g_wpo0"])
        a_in = _pool_fwd(z, pool_w[l], pool_scale[l:l + 1], f"pool_fwd_{l}", after=token)
        o, b_in, states = _hgrn_fwd(z, lb[l:l + 1], hgrn_norm_g[l:l + 1], f"hgrn_fwd_{l}", after=token)
        if l == 0:
            gw.update(gather_done(1, b_in))
        who_l = gw[f"who{l}"].reshape(D_MODEL, D_MODEL)
        wout_l = gw[f"wout{l}"].reshape(D_MODEL, D_MODEL)
        last = l == DEPTH - 1
        ba, bb, merged, y, *out = _merge_fwd(a_in, b_in, z, xs[l], gw[f"wpo{l}"], who_l, wout_l, gate[l],
                                             g_post[l:l + 1], tm_merge, f"merge_fwd_{l}",
                                             target=loss_target[0] if last else None)
        if last:
            dx, loss_part = out
        else:
            xs.append(out[0])
            gather_pass(2, out[0])
        saved.append((h, z, a_in, o, b_in, states, ba, bb, merged, y, who_l, wout_l))


    chips = _other_chips(pos)
    pair_idx = jnp.stack([_dev_index(cx, cy, pos[2]) for cx, cy in chips] + [me]).astype(jnp.int32)
    pair_rows = dict(win=256, wpo=POOL_WIDTH, who=HEAD_DIM, wout=HEAD_DIM)

    def scatter_pair_start(u, grads):
        keys = list(grads)
        pair, to_chips = _scatter_streams(keys)
        bufs = {}
        for k in keys:
            bufs["g_" + k] = grads[k]
            bufs["st_" + k] = lax.empty((4,) + grads[k].shape[1:], WIRE_DTYPE)
        bufs, (sems,), token = _comm_call(f"scatter_pair_start_{u}", bufs, start=[pair])
        return dict(u=u, keys=keys, pair=pair, to_chips=to_chips, bufs=bufs, sems=sems, token=token)

    def scatter_pair_finish(st, after):
        u, keys = st["u"], st["keys"]
        bufs, _, _ = _comm_call(f"scatter_pair_done_{u}", st["bufs"], wait=[(st["pair"], st["sems"])], after=after)
        bufs2 = {}
        for k in keys:
            bufs2["ps_" + k] = _pair_sum(bufs["g_" + k], bufs["st_" + k], pair_idx, bufs["g_" + k].shape[1],
                                         f"pair_sum_{k}")
            bufs2["ld_" + k] = lax.empty((3,) + bufs["g_" + k].shape[1:], WIRE_DTYPE)
        st.update(bufs=bufs2)

    def scatter_chips_start(st, after=None):
        bufs2, (sems,), token = _comm_call(f"scatter_chips_start_{st['u']}", st["bufs"], start=[st["to_chips"]],
                                           after=after)
        st.update(bufs=bufs2, sems=sems, token=token)

    def scatter_finish(st, after):
        bufs, _, _ = _comm_call(f"scatter_chips_done_{st['u']}", st["bufs"], wait=[(st["to_chips"], st["sems"])],
                                after=after)
        return {k: [(bufs["ps_" + k], 3), (bufs["ld_" + k], 0), (bufs["ld_" + k], 1), (bufs["ld_" + k], 2)]
                for k in st["keys"]}

    moments = dict(win=(m_w_in, v_w_in), wpo=(m_w_pool_o, v_w_pool_o), who=(m_w_hgrn_o, v_w_hgrn_o),
                   wout=(m_w_out, v_w_out))
    big_out = {}

    def finish_unit(unit, after):
        for k, contribs in scatter_finish(scat[unit], after).items():
            wname, l = k[:-1], int(k[-1])
            big_out[wname] = _adamw_layer(big[wname], moments[wname][0], moments[wname][1], contribs, l,
                                          pair_rows[wname], f"adamw_{k}", prev=big_out.get(wname))
            after = big_out[wname][0]
        return after

    d_ada, small, scat = [None] * DEPTH, [None] * DEPTH, {}
    for l in reversed(range(DEPTH)):
        h, z, a_in, o, b_in, states, ba, bb, merged, y, who_l, wout_l = saved[l]
        dy, dba, dbb, da_in, db_in, dz, acc_post = _merge_bwd(
            dx, y, ba, bb, z, gw[f"wpo{l}"], who_l, wout_l, gate[l], g_post[l:l + 1],
            lax.empty((seq, IN_WIDTH), MXU_DTYPE), tm_merge, f"merge_bwd_{l}")
        g_out, g_ho, g_po = _grad_out_weights(merged, dy, b_in, dbb, a_in, dba, f"grad_out_weights_{l}")
        g_small = {f"wout{l}": g_out.reshape(N_DEV, HEAD_DIM, D_MODEL),
                   f"who{l}": g_ho.reshape(N_DEV, HEAD_DIM, D_MODEL), f"wpo{l}": g_po}
        st_small = scat["small0"] = scatter_pair_start("small0", g_small) if l == 0 else None
        dz, dlb, dgn = _hgrn_bwd(db_in, z, o, states, lb[l:l + 1], hgrn_norm_g[l:l + 1], dz, f"hgrn_bwd_{l}",
                                 after=st_small and st_small["token"])
        if l == 0:
            scatter_pair_finish(st_small, dlb)
            scatter_chips_start(st_small)
        dz, dpw, dps = _pool_bwd(da_in, z, pool_w[l], pool_scale[l:l + 1], dz, f"pool_bwd_{l}",
                                 after=st_small and st_small["token"])
        small[l] = dict(g_post=acc_post[1], pool_w=dpw, pool_scale=dps[0], lb_logits=dlb[0], hgrn_norm_g=dgn[0])
        token = None
        if l == 0:
            parts = {name: jnp.stack([small[0][name], small[1][name]]) for name in small[0]}
            parts.update(b_ada=[None, d_ada[1]], g_pre=[None, small[1]["g_pre"]])
            sg_stream = _direct_gather_stream("sg")
            early = _pack_small(parts, 2)
            sg_bufs, (sg_sems,), token = _comm_call(
                "small_grads_start", dict(s_sg=early, g_sg=_with_own_slot(early, me)), start=[sg_stream])
        g_win = {f"win{l}": _in_proj_dw(h, dz, f"grad_w_in_{l}", after=token)}
        st_win = scat[f"win{l}"] = scatter_pair_start(f"win{l}", g_win if l == 0 else {**g_small, **g_win})
        if l > 0:
            dh = _in_proj_dh(dz, gw[f"win{l}"], seq, f"in_proj_dh_{l}", after=st_win["token"])
            scatter_pair_finish(st_win, dh)
            scatter_chips_start(st_win)
        else:
            scatter_pair_finish(st_win, st_win["token"])
            scatter_chips_start(st_win)
            after = st_win["token"]
            for unit in ("win1", "small0"):
                after = finish_unit(unit, after)
            dh = _in_proj_dh(dz, gw[f"win{l}"], seq, f"in_proj_dh_{l}", after=after)
        dx, acc_pre = _prenorm_bwd(xs[l], dh, dx, g_pre[l:l + 1], scale[l], tm, f"prenorm_bwd_{l}",
                                   after=st_win["token"])
        d_ada[l] = jnp.concatenate([acc_pre[0], acc_pre[1], acc_post[0]])
        small[l]["g_pre"] = acc_pre[2]
    grad_x = dx[None]

    parts = dict(b_ada=[d_ada[0]], g_pre=[small[0]["g_pre"]])
    late = jnp.concatenate([_pack_small(parts, 0, 2), jnp.broadcast_to(loss_part, (8, 128))], axis=0)
    g_late = _allgather_small(late, "allgather_late_grads")
    loss = jnp.sum(g_late[:, SMALL_LATE_ROWS, 0])
    sg_bufs, _, _ = _comm_call("small_grads_done", sg_bufs, wait=[(sg_stream, sg_sems)], after=g_late)
    g_early = sg_bufs["g_sg"]
    small_names = list(dict.fromkeys(name for name, _, _ in _SMALL_ROWS))
    weights = dict(b_ada=b_ada, g_pre=g_pre, g_post=g_post, pool_w=pool_w, pool_scale=pool_scale,
                   lb_logits=lb_logits, hgrn_norm_g=hgrn_norm_g)
    m_small = dict(b_ada=m_b_ada, g_pre=m_g_pre, g_post=m_g_post, pool_w=m_pool_w, pool_scale=m_pool_scale,
                   lb_logits=m_lb_logits, hgrn_norm_g=m_hgrn_norm_g)
    v_small = dict(b_ada=v_b_ada, g_pre=v_g_pre, g_post=v_g_post, pool_w=v_pool_w, pool_scale=v_pool_scale,
                   lb_logits=v_lb_logits, hgrn_norm_g=v_hgrn_norm_g)
    shapes = {name: weights[name].shape for name in small_names}
    small_out = _adamw_small(_pack_small(weights), _pack_small(m_small), _pack_small(v_small), g_late, g_early,
                             shapes)

    d_ada_all = jnp.stack([g_late[:, 0:24, :].reshape(N_DEV, 3 * D_MODEL),
                           g_early[:, 0:24, :].reshape(N_DEV, 3 * D_MODEL)], axis=1)
    d_cols = jnp.transpose(lax.dynamic_slice_in_dim(d_ada_all, me * ADA_COLS, ADA_COLS, axis=2), (1, 0, 2))
    g_w_ada = _ada_bwd(c_all, d_cols)
    ada_out = _adamw_sharded(w_ada, m_w_ada, v_w_ada, g_w_ada[:, None], 256, "adamw_w_ada")
    finish_unit("win0", ada_out[1][0, 0:8, 0:128] + small_out[1]["pool_scale"][0:1, 0:128])

    def leaf(kind):
        s = small_out[kind]
        return (ada_out[kind], s["b_ada"], s["g_pre"], s["g_post"], big_out["win"][kind], s["pool_w"], s["pool_scale"],
                s["lb_logits"], s["hgrn_norm_g"], big_out["wpo"][kind], big_out["who"][kind], big_out["wout"][kind])

    return (loss, grad_x) + leaf(0) + leaf(1) + leaf(2) + leaf(3)
```

```python
import jax
import jax.numpy as jnp
from jax import lax
from jax.experimental import pallas as pl
from jax.experimental.pallas import tpu as pltpu

F32 = jnp.float32
MXU_DTYPE = jnp.bfloat16
WIRE_DTYPE = jnp.bfloat16

N_DEV = 8
DEPTH = 2
D_MODEL = 1024
HEADS = 8
HEAD_DIM = 128
POOL_GROUPS = 4
GROUP_DIM = 128
POOL_WIDTH = POOL_GROUPS * GROUP_DIM
IN_WIDTH = 7168
CHUNK = 64
SUB = 16
N_SUB = CHUNK // SUB
FWD_STEP_CHUNKS = 8
BWD_STEP_CHUNKS = 4
EXP_CLAMP = 80.0
NORM_EPS = 1e-6
LOG_FLOOR = 1e-30
ADA_COLS = 3 * D_MODEL // N_DEV
IN_COLS = IN_WIDTH // N_DEV
COL_HQ, COL_HF, COL_HI, COL_HG, COL_MGP, COL_MGH = 1, 2, 3, 4, 5, 6

ADAM_LR = 0.001
ADAM_B1 = 0.9
ADAM_B2 = 0.999
ADAM_EPS = 1e-08
ADAM_WD = 0.01
ADAM_STEP = 10

VMEM_LIMIT = 48 * 1024 * 1024
MESH_ID = pl.DeviceIdType.MESH
HIGHEST = lax.Precision.HIGHEST

_SMALL_ROWS = (("b_ada", 0, 24), ("g_pre", 0, 8), ("b_ada", 1, 24), ("g_pre", 1, 8), ("g_post", None, 16),
               ("pool_w", None, 1024), ("pool_scale", None, 8), ("lb_logits", None, 16), ("hgrn_norm_g", None, 2))
SMALL_LATE_ROWS = 32
SMALL_ROWS_PAD = 1136
LB_ROW0 = 32 + 32 + 16 + 1024 + 8


def _params(**kw):
    return pltpu.CompilerParams(vmem_limit_bytes=VMEM_LIMIT, **kw)


def _sigmoid(v):
    return 1.0 / (1.0 + jnp.exp(-v))


def _dsilu(v, s):
    return s * (1.0 + v * (1.0 - s))


def _dot(a, b):
    return jnp.dot(a.astype(MXU_DTYPE), b.astype(MXU_DTYPE), preferred_element_type=F32)


def _dot_nt(a, b):
    return lax.dot_general(a.astype(MXU_DTYPE), b.astype(MXU_DTYPE), (((1,), (1,)), ((), ())),
                           preferred_element_type=F32)


def _dot_tn(a, b):
    return lax.dot_general(a.astype(MXU_DTYPE), b.astype(MXU_DTYPE), (((0,), (0,)), ((), ())),
                           preferred_element_type=F32)


def _pallas_after(body, n_in, after, *, in_specs, **kw):
    if after is None:
        return pl.pallas_call(body, in_specs=in_specs, **kw)

    def tied(*refs):
        body(*refs[:n_in], *refs[n_in + 1:])

    call = pl.pallas_call(tied, in_specs=list(in_specs) + [pl.BlockSpec(memory_space=pl.ANY)], **kw)
    return lambda *operands: call(*operands, after)


def _my_position():
    mx, my, mc = lax.axis_index("x"), lax.axis_index("y"), lax.axis_index("c")
    return mx, my, mc, 4 * mx + 2 * my + mc


def _peer(mx, my, mc, k):
    px = 1 - mx if (k >> 2) & 1 else mx
    py = 1 - my if (k >> 1) & 1 else my
    pc = 1 - mc if k & 1 else mc
    return (px, py, pc), 4 * px + 2 * py + pc


def _allgather_small(v, name, after=None):
    rows, cols = v.shape

    def body(v_ref, out_ref, send_sems, recv_sems):
        mx, my, mc, me = _my_position()
        out_ref[me] = v_ref[...]
        copies = []
        for k in range(1, N_DEV):
            peer, _ = _peer(mx, my, mc, k)
            cp = pltpu.make_async_remote_copy(
                src_ref=v_ref, dst_ref=out_ref.at[me],
                send_sem=send_sems.at[k - 1], recv_sem=recv_sems.at[k - 1],
                device_id=peer, device_id_type=MESH_ID)
            cp.start()
            copies.append(cp)
        for cp in copies:
            cp.wait()

    return _pallas_after(
        body, 1, after, name=name,
        out_shape=jax.ShapeDtypeStruct((N_DEV, rows, cols), v.dtype),
        in_specs=[pl.BlockSpec(memory_space=pltpu.VMEM)],
        out_specs=pl.BlockSpec(memory_space=pltpu.VMEM),
        scratch_shapes=[pltpu.SemaphoreType.DMA((N_DEV - 1,)), pltpu.SemaphoreType.DMA((N_DEV - 1,))],
        compiler_params=_params(),
    )(v)


class _Stream:
    def __init__(self, n, plan):
        self.n, self.plan = n, plan


def _comm_call(name, bufs, start=(), wait=(), after=None):
    names = list(bufs)

    def body(*refs):
        it = iter(refs)
        buf_refs = {n: next(it) for n in names}
        wait_sems = [(next(it), next(it)) for _ in wait]
        if after is not None:
            next(it)
        start_sems = [(next(it), next(it)) for _ in start]
        for _ in names:
            next(it)
        token = next(it)
        pos = _my_position()

        def descriptors(stream, sems):
            return [pltpu.make_async_remote_copy(src_ref=src, dst_ref=dst, send_sem=sems[0].at[k], recv_sem=sems[1].at[k],
                                                 device_id=dev, device_id_type=MESH_ID)
                    for k, (src, dst, dev) in enumerate(stream.plan(buf_refs, pos))]

        for (stream, _), sems in zip(wait, wait_sems):
            for cp in descriptors(stream, sems):
                cp.wait_send()
                cp.wait_recv()
        for stream, sems in zip(start, start_sems):
            for cp in descriptors(stream, sems):
                cp.start()
        token[...] = jnp.zeros_like(token)

    hbm = pl.BlockSpec(memory_space=pltpu.HBM)
    sem = pl.BlockSpec(memory_space=pltpu.SEMAPHORE)
    operands = [pltpu.with_memory_space_constraint(bufs[n], pltpu.HBM) for n in names]
    in_specs = [hbm] * len(names)
    for _, (send_sems, recv_sems) in wait:
        operands += [send_sems, recv_sems]
        in_specs += [sem, sem]
    if after is not None:
        operands.append(after)
        in_specs.append(pl.BlockSpec(memory_space=pl.ANY))
    out_shape, out_specs = [], []
    for stream in start:
        out_shape += [pltpu.SemaphoreType.DMA((stream.n,)), pltpu.SemaphoreType.DMA((stream.n,))]
        out_specs += [sem, sem]
    n_sem_out = len(out_shape)
    out_shape += [pltpu.HBM(bufs[n].shape, bufs[n].dtype) for n in names]
    out_specs += [hbm] * len(names)
    out_shape.append(jax.ShapeDtypeStruct((8, 128), F32))
    out_specs.append(pl.BlockSpec(memory_space=pltpu.VMEM))
    outs = pl.pallas_call(
        body, name=name, out_shape=out_shape, in_specs=in_specs, out_specs=out_specs,
        input_output_aliases={i: n_sem_out + i for i in range(len(names))},
        compiler_params=pltpu.CompilerParams(has_side_effects=pltpu.SideEffectType.DATAFLOW_SIDE_EFFECTING),
    )(*operands)
    sems = [(outs[2 * i], outs[2 * i + 1]) for i in range(len(start))]
    return dict(zip(names, outs[n_sem_out:n_sem_out + len(names)])), sems, outs[-1]


def _with_own_slot(block, me):
    return lax.dynamic_update_index_in_dim(lax.empty((N_DEV,) + block.shape, block.dtype), block, me, 0)


def _other_chips(pos):
    mx, my, _, _ = pos
    return [(1 - mx if i & 2 else mx, 1 - my if i & 1 else my) for i in (1, 2, 3)]


def _dev_index(px, py, pc):
    return 4 * px + 2 * py + pc


def _gather_streams(keys):
    def to_chips(refs, pos):
        _, _, mc, me = pos
        return [(refs["s_" + k], refs["g_" + k].at[me], (cx, cy, mc)) for k in keys for cx, cy in _other_chips(pos)]

    def to_sibling(refs, pos):
        mx, my, mc, me = pos
        return [(refs["s_" + k], refs["g_" + k].at[me], (mx, my, 1 - mc)) for k in keys]

    def pass_on(refs, pos):
        mx, my, mc, _ = pos
        out = []
        for k in keys:
            for cx, cy in _other_chips(pos):
                slot = refs["g_" + k].at[_dev_index(cx, cy, mc)]
                out.append((slot, slot, (mx, my, 1 - mc)))
        return out

    return _Stream(3 * len(keys), to_chips), _Stream(len(keys), to_sibling), _Stream(3 * len(keys), pass_on)


def _direct_gather_stream(key):
    def plan(refs, pos):
        mx, my, mc, me = pos
        return [(refs["s_" + key], refs["g_" + key].at[me], _peer(mx, my, mc, k)[0]) for k in range(1, N_DEV)]

    return _Stream(N_DEV - 1, plan)


def _scatter_streams(keys):
    def pair(refs, pos):
        mx, my, mc, _ = pos
        sib = (mx, my, 1 - mc)
        out = []
        for k in keys:
            for i, (cx, cy) in enumerate(_other_chips(pos)):
                out.append((refs["g_" + k].at[_dev_index(cx, cy, 1 - mc)], refs["st_" + k].at[i], sib))
            out.append((refs["g_" + k].at[_dev_index(mx, my, 1 - mc)], refs["st_" + k].at[3], sib))
        return out

    def chips(refs, pos):
        mc = pos[2]
        return [(refs["ps_" + k].at[i], refs["ld_" + k].at[i], (cx, cy, mc))
                for k in keys for i, (cx, cy) in enumerate(_other_chips(pos))]

    return _Stream(4 * len(keys), pair), _Stream(3 * len(keys), chips)


def _pair_sum(g, st, idx, tr, name):
    _, rows, cols = g.shape

    def body(idx_ref, g_ref, st_ref, out_ref):
        out_ref[...] = (g_ref[...].astype(F32) + st_ref[...].astype(F32)).astype(out_ref.dtype)

    return pl.pallas_call(
        body, name=name,
        grid_spec=pltpu.PrefetchScalarGridSpec(
            num_scalar_prefetch=1, grid=(4, rows // tr),
            in_specs=[pl.BlockSpec((None, tr, cols), lambda j, i, idx_ref: (idx_ref[j], i, 0)),
                      pl.BlockSpec((None, tr, cols), lambda j, i, idx_ref: (j, i, 0))],
            out_specs=pl.BlockSpec((None, tr, cols), lambda j, i, idx_ref: (j, i, 0))),
        out_shape=jax.ShapeDtypeStruct((4, rows, cols), WIRE_DTYPE),
        compiler_params=_params(dimension_semantics=("parallel", "parallel")),
    )(idx, g, st)


def _ada_fwd(c_all, w_ada, b_cols):
    def body(c_ref, w_ref, b_ref, out_ref):
        cv = c_ref[...]
        ca = cv * _sigmoid(cv)
        for l in range(DEPTH):
            out_ref[l] = jnp.dot(ca, w_ref[l], precision=HIGHEST, preferred_element_type=F32) + b_ref[l:l + 1, :]

    return pl.pallas_call(
        body, name="ada_fwd",
        out_shape=jax.ShapeDtypeStruct((DEPTH, N_DEV, ADA_COLS), F32),
        compiler_params=_params(),
    )(c_all, w_ada, b_cols)


def _ada_bwd(c_all, d_cols):
    def body(c_ref, d_ref, out_ref):
        cv = c_ref[...]
        ca = cv * _sigmoid(cv)
        for l in range(DEPTH):
            out_ref[l] = lax.dot_general(ca, d_ref[l], (((0,), (0,)), ((), ())), precision=HIGHEST,
                                         preferred_element_type=F32)

    return pl.pallas_call(
        body, name="ada_bwd",
        out_shape=jax.ShapeDtypeStruct((DEPTH, D_MODEL, ADA_COLS), F32),
        compiler_params=_params(),
    )(c_all, d_cols)


def _lower_bounds(logits):
    m = jnp.maximum(logits[0:1], logits[1:2])
    e0, e1 = jnp.exp(logits[0:1] - m), jnp.exp(logits[1:2] - m)
    den = e0 + e1
    p0, p1 = e0 / den, e1 / den
    low0 = p0 - p0
    low1 = (p0 + p1) - p0
    return (p0, p1), (low0, low1)


def _lb_fwd(lb_logits):
    def body(lg_ref, out_ref):
        _, (low0, low1) = _lower_bounds(lg_ref[...])
        out_ref[0:1, :] = jnp.clip(low0, 0.0, 1.0)
        out_ref[1:2, :] = jnp.clip(low1, 0.0, 1.0)

    return pl.pallas_call(body, name="lb_fwd", out_shape=jax.ShapeDtypeStruct(lb_logits.shape, F32),
                          compiler_params=_params())(lb_logits)


def _row_spec(cols=D_MODEL):
    return pl.BlockSpec((1, cols), lambda *_: (0, 0))


def _prenorm_fwd(x, g, shift, scale, tm, name, after=None):
    seq = x.shape[0]

    def body(x_ref, g_ref, sh_ref, sc_ref, h_ref):
        xv = x_ref[...]
        rs = lax.rsqrt(jnp.mean(xv * xv, axis=-1, keepdims=True) + NORM_EPS)
        h = (xv * rs * g_ref[...]) * (1.0 + sc_ref[...]) + sh_ref[...]
        h_ref[...] = h.astype(h_ref.dtype)

    tile = pl.BlockSpec((tm, D_MODEL), lambda i: (i, 0))
    return _pallas_after(
        body, 4, after, name=name, grid=(seq // tm,),
        in_specs=[tile, _row_spec(), _row_spec(), _row_spec()], out_specs=tile,
        out_shape=jax.ShapeDtypeStruct((seq, D_MODEL), MXU_DTYPE),
        compiler_params=_params(dimension_semantics=("parallel",)),
    )(x, g, shift, scale)


def _in_proj(h, win_g, tm, name, after=None):
    seq = h.shape[0]

    def body(h_ref, w_ref, z_ref, w_pair):
        @pl.when(pl.program_id(1) == 0)
        def _():
            w_pair[...] = jnp.concatenate([w_ref[0], w_ref[1]], axis=1)

        z_ref[...] = jnp.dot(h_ref[...], w_pair[...], preferred_element_type=F32)

    return _pallas_after(
        body, 2, after, name=name, grid=(N_DEV // 2, seq // tm),
        in_specs=[pl.BlockSpec((tm, D_MODEL), lambda j, i: (i, 0)),
                  pl.BlockSpec((2, D_MODEL, IN_COLS), lambda j, i: (j, 0, 0))],
        out_specs=pl.BlockSpec((tm, 2 * IN_COLS), lambda j, i: (i, j)),
        out_shape=jax.ShapeDtypeStruct((seq, IN_WIDTH), F32),
        scratch_shapes=[pltpu.VMEM((D_MODEL, 2 * IN_COLS), MXU_DTYPE)],
        compiler_params=_params(dimension_semantics=("parallel", "arbitrary")),
    )(h, win_g)


def _shift_down(v, j, pos):
    return jnp.where(pos >= j, pltpu.roll(v, j, 0), 0.0)


def _shift_up(v, j, pos, seq):
    return jnp.where(pos < seq - j, pltpu.roll(v, seq - j, 0), 0.0)


def _select_window(g, candidates):
    out = candidates[-1]
    for i in range(len(candidates) - 2, -1, -1):
        out = jnp.where(g == i, candidates[i], out)
    return out


def _pool_mean_minus_token(u, g, pos):
    sums, acc = [], u
    for j in (1, 2, 4, 8):
        acc = acc + _shift_down(acc, j, pos)
        sums.append(acc)
    wsum = _select_window(g, sums)
    width = jnp.left_shift(2, g).astype(F32)
    count = jnp.minimum(pos.astype(F32) + 1.0, width)
    return wsum / count - u, count


def _pool_fwd(z, pool_w_l, pool_scale_l, name, after=None):
    seq = z.shape[0]

    def body(pv_ref, pg_ref, w_ref, sc_ref, out_ref):
        g = pl.program_id(0)
        pos = lax.broadcasted_iota(jnp.int32, (seq, GROUP_DIM), 0)
        pm, _ = _pool_mean_minus_token(pv_ref[...], g, pos)
        lin = _dot(pm, w_ref[...]) * sc_ref[...]
        pg = pg_ref[...]
        out_ref[...] = (lin * (pg * _sigmoid(pg))).astype(out_ref.dtype)

    return _pallas_after(
        body, 4, after, name=name, grid=(POOL_GROUPS,),
        in_specs=[pl.BlockSpec((seq, GROUP_DIM), lambda g: (0, g)),
                  pl.BlockSpec((seq, GROUP_DIM), lambda g: (0, POOL_GROUPS + g)),
                  pl.BlockSpec((None, GROUP_DIM, GROUP_DIM), lambda g: (g, 0, 0)),
                  pl.BlockSpec((1, GROUP_DIM), lambda g: (0, g))],
        out_specs=pl.BlockSpec((seq, GROUP_DIM), lambda g: (0, g)),
        out_shape=jax.ShapeDtypeStruct((seq, POOL_WIDTH), MXU_DTYPE),
        compiler_params=_params(dimension_semantics=("parallel",)),
    )(z, z, pool_w_l, pool_scale_l)


def _chunk_masks():
    row = lax.broadcasted_iota(jnp.int32, (CHUNK, CHUNK), 0)
    col = lax.broadcasted_iota(jnp.int32, (CHUNK, CHUNK), 1)
    causal = row >= col
    before_sub = col < (row // SUB) * SUB
    suffix = row <= col
    return causal, before_sub, suffix


def _masked_sums(masks, v):
    lhs = jnp.concatenate([m.astype(jnp.bfloat16) for m in masks], axis=0)
    hi = v.astype(jnp.bfloat16)
    rest = v - hi.astype(F32)
    mid = rest.astype(jnp.bfloat16)
    lo = (rest - mid.astype(F32)).astype(jnp.bfloat16)
    out = jnp.dot(lhs, hi, preferred_element_type=F32)
    out += jnp.dot(lhs, mid, preferred_element_type=F32)
    out += jnp.dot(lhs, lo, preferred_element_type=F32)
    return [out[i * CHUNK:(i + 1) * CHUNK] for i in range(len(masks))]


def _gates(zf, lb):
    sg = _sigmoid(zf)
    f = lb + (1.0 - lb) * sg
    logf = jnp.log(jnp.maximum(f, LOG_FLOOR))
    return sg, f, logf


def _intra_blocks(q_h, k_h, cum_h, base_h, causal):
    rel = cum_h - base_h
    out = []
    for i in range(N_SUB):
        rows = slice(i * SUB, (i + 1) * SUB)
        e_q = jnp.exp(rel[rows])
        base_i = jnp.concatenate([base_h[rows]] * N_SUB, axis=0)
        e_k = jnp.exp(jnp.minimum(base_i - cum_h, EXP_CLAMP))
        q_t = (q_h[rows] * e_q).astype(MXU_DTYPE)
        k_t = (k_h * e_k).astype(MXU_DTYPE)
        a_i = jnp.where(causal[rows], _dot_nt(q_t, k_t), 0.0)
        out.append((q_t, k_t, e_q, e_k, a_i))
    return out


def _hgrn_fwd(z, lb_l, gn_l, name, after=None):
    seq = z.shape[0]
    n_chunks = seq // CHUNK
    per_step = min(FWD_STEP_CHUNKS, n_chunks)
    rows_per_step = per_step * CHUNK

    def body(hq_ref, hf_ref, hi_ref, hg_ref, lb_ref, gn_ref, o_ref, bin_ref, st_ref, state):
        @pl.when(pl.program_id(0) == 0)
        def _():
            state[...] = jnp.zeros_like(state)

        causal, before_sub, _ = _chunk_masks()
        for cc in range(per_step):
            rows = slice(cc * CHUNK, (cc + 1) * CHUNK)
            _, f, logf = _gates(hf_ref[rows, :], lb_ref[...])
            kk = 1.0 - f
            hq = hq_ref[rows, :]
            q = hq * _sigmoid(hq)
            cum, base = _masked_sums([causal, before_sub], logf)
            st_ref[cc] = state[...]
            for h in range(HEADS):
                sl = slice(h * HEAD_DIM, (h + 1) * HEAD_DIM)
                q_h, k_h, cum_h = q[:, sl], kk[:, sl], cum[:, sl]
                v_h = hi_ref[rows, sl]
                st_h = state[h]
                blocks = _intra_blocks(q_h, k_h, cum_h, base[:, sl], causal)
                a = jnp.concatenate([b[4] for b in blocks], axis=0)
                o_h = _dot_nt(q_h * jnp.exp(cum_h), st_h) + _dot(a, v_h)
                last = jnp.sum(logf[:, sl], axis=0, keepdims=True)
                state[h] = st_h * jnp.exp(last) + _dot_tn(v_h, k_h * jnp.exp(last - cum_h))
                rs = lax.rsqrt(jnp.mean(o_h * o_h, axis=-1, keepdims=True) + NORM_EPS)
                hg = hg_ref[rows, sl]
                o_ref[rows, sl] = o_h
                bin_ref[rows, sl] = ((o_h * rs * gn_ref[...]) * (hg * _sigmoid(hg))).astype(bin_ref.dtype)

    def col(block):
        return pl.BlockSpec((rows_per_step, D_MODEL), lambda c: (c, block))

    tile = pl.BlockSpec((rows_per_step, D_MODEL), lambda c: (c, 0))
    return _pallas_after(
        body, 6, after, name=name, grid=(n_chunks // per_step,),
        in_specs=[col(COL_HQ), col(COL_HF), col(COL_HI), col(COL_HG), _row_spec(), _row_spec(HEAD_DIM)],
        out_specs=[tile, tile, pl.BlockSpec((per_step, HEADS, HEAD_DIM, HEAD_DIM), lambda c: (c, 0, 0, 0))],
        out_shape=[jax.ShapeDtypeStruct((seq, D_MODEL), F32),
                   jax.ShapeDtypeStruct((seq, D_MODEL), MXU_DTYPE),
                   jax.ShapeDtypeStruct((n_chunks, HEADS, HEAD_DIM, HEAD_DIM), F32)],
        scratch_shapes=[pltpu.VMEM((HEADS, HEAD_DIM, HEAD_DIM), F32)],
        compiler_params=_params(dimension_semantics=("arbitrary",)),
    )(z, z, z, z, lb_l, gn_l)


def _rms_parts(y):
    rs = lax.rsqrt(jnp.mean(y * y, axis=-1, keepdims=True) + NORM_EPS)
    return rs, y * rs


def _merge_fwd(a_in, b_in, z, x, wpo_g, who_g, wout_g, gate, g_post, tm, name, target=None):
    seq = x.shape[0]
    with_loss = target is not None

    def body(*refs):
        a_ref, b_ref, mgp_ref, mgh_ref, x_ref, wpo_ref, who_ref, wout_ref, gate_ref, gp_ref = refs[:10]
        ba_ref, bb_ref, mer_ref, y_ref, last_ref = refs[10 + with_loss:15 + with_loss]
        a = a_ref[...]
        ba = jnp.concatenate([_dot(a, wpo_ref[j]) for j in range(N_DEV)], axis=1)
        bb = _dot(b_ref[...], who_ref[...])
        merged = _sigmoid(mgp_ref[...]) * ba + _sigmoid(mgh_ref[...]) * bb
        y = _dot(merged, wout_ref[...])
        _, yn = _rms_parts(y)
        ba_ref[...] = ba.astype(ba_ref.dtype)
        bb_ref[...] = bb.astype(bb_ref.dtype)
        mer_ref[...] = merged.astype(mer_ref.dtype)
        y_ref[...] = y.astype(y_ref.dtype)
        x_next = x_ref[...] + gate_ref[...] * (yn * gp_ref[...])
        if not with_loss:
            last_ref[...] = x_next
            return
        loss_ref = refs[16]

        @pl.when(pl.program_id(0) == 0)
        def _():
            loss_ref[...] = jnp.zeros_like(loss_ref)

        err = x_next - refs[10][...]
        loss_ref[...] += 0.5 * jnp.sum(jnp.mean(err * err, axis=-1, keepdims=True), axis=0, keepdims=True)
        last_ref[...] = err * (1.0 / D_MODEL)

    def tile(cols=D_MODEL, block=0):
        return pl.BlockSpec((tm, cols), lambda i: (i, block))

    full = pl.BlockSpec((D_MODEL, D_MODEL), lambda i: (0, 0))
    act = jax.ShapeDtypeStruct((seq, D_MODEL), MXU_DTYPE)
    f32 = jax.ShapeDtypeStruct((seq, D_MODEL), F32)
    one = [pl.BlockSpec((1, 1), lambda i: (0, 0))] if with_loss else []
    return pl.pallas_call(
        body, name=name, grid=(seq // tm,),
        in_specs=[tile(POOL_WIDTH), tile(), tile(block=COL_MGP), tile(block=COL_MGH), tile(),
                  pl.BlockSpec((N_DEV, POOL_WIDTH, GROUP_DIM), lambda i: (0, 0, 0)),
                  full, full, _row_spec(), _row_spec()] + ([tile()] if with_loss else []),
        out_specs=[tile(), tile(), tile(), tile(), tile()] + one,
        out_shape=[act, act, act, act, f32] + ([jax.ShapeDtypeStruct((1, 1), F32)] if with_loss else []),
        compiler_params=_params(dimension_semantics=("arbitrary" if with_loss else "parallel",)),
    )(a_in, b_in, z, z, x, wpo_g, who_g, wout_g, gate, g_post, *([target] if with_loss else []))


def _stage_copy(stage, sems, dst, slot, step, where):
    rows, cols = where(step)
    return pltpu.make_async_copy(stage.at[slot], dst.at[rows, cols], sems.at[slot])


def _stage_begin(stage, sems, dst, step, where):
    slot = step % 2

    @pl.when(step >= 2)
    def _():
        _stage_copy(stage, sems, dst, slot, step - 2, where).wait()

    return slot


def _stage_end(stage, sems, dst, step, n_steps, where):
    slot = step % 2
    _stage_copy(stage, sems, dst, slot, step, where).start()

    @pl.when(step == n_steps - 1)
    def _():
        _stage_copy(stage, sems, dst, slot, step, where).wait()
        if n_steps > 1:
            _stage_copy(stage, sems, dst, 1 - slot, step - 1, where).wait()


def _merge_bwd(dx, y, ba, bb, z, wpo_g, who_g, wout_g, gate, g_post, dz, tm, name):
    seq = dx.shape[0]
    n_steps = seq // tm

    def body(dx_ref, y_ref, ba_ref, bb_ref, mgp_ref, mgh_ref, wpo_ref, who_ref, wout_ref, gate_ref, gp_ref, _,
             dy_ref, dba_ref, dbb_ref, da_ref, db_ref, dz_ref, acc_ref, stage, sems):
        step = pl.program_id(0)

        @pl.when(step == 0)
        def _():
            acc_ref[...] = jnp.zeros_like(acc_ref)

        def where(t):
            return pl.ds(t * tm, tm), pl.ds(COL_MGP * D_MODEL, 2 * D_MODEL)

        dmg_ref = stage.at[_stage_begin(stage, sems, dz_ref, step, where)]

        dxv = dx_ref[...]
        rs, yn = _rms_parts(y_ref[...].astype(F32))
        acc_ref[0:1, :] += jnp.sum(dxv * yn * gp_ref[...], axis=0, keepdims=True)
        acc_ref[1:2, :] += jnp.sum(dxv * gate_ref[...] * yn, axis=0, keepdims=True)
        dyn = dxv * (gate_ref[...] * gp_ref[...])
        dy = rs * (dyn - yn * jnp.mean(dyn * yn, axis=-1, keepdims=True))
        dmerged = _dot_nt(dy, wout_ref[...])
        sp, sh = _sigmoid(mgp_ref[...]), _sigmoid(mgh_ref[...])
        dba, dbb = sp * dmerged, sh * dmerged
        dmg_ref[:, 0:D_MODEL] = (dmerged * ba_ref[...].astype(F32) * sp * (1.0 - sp)).astype(dmg_ref.dtype)
        dmg_ref[:, D_MODEL:2 * D_MODEL] = (dmerged * bb_ref[...].astype(F32) * sh * (1.0 - sh)).astype(dmg_ref.dtype)
        da = _dot_nt(dba[:, 0:GROUP_DIM], wpo_ref[0])
        for j in range(1, N_DEV):
            da += _dot_nt(dba[:, j * GROUP_DIM:(j + 1) * GROUP_DIM], wpo_ref[j])
        dy_ref[...] = dy.astype(dy_ref.dtype)
        dba_ref[...] = dba.astype(dba_ref.dtype)
        dbb_ref[...] = dbb.astype(dbb_ref.dtype)
        da_ref[...] = da.astype(da_ref.dtype)
        db_ref[...] = _dot_nt(dbb, who_ref[...]).astype(db_ref.dtype)
        _stage_end(stage, sems, dz_ref, step, n_steps, where)

    def tile(cols=D_MODEL, block=0):
        return pl.BlockSpec((tm, cols), lambda i: (i, block))

    full = pl.BlockSpec((D_MODEL, D_MODEL), lambda i: (0, 0))
    hbm = pl.BlockSpec(memory_space=pl.ANY)
    act = jax.ShapeDtypeStruct((seq, D_MODEL), MXU_DTYPE)
    return pl.pallas_call(
        body, name=name, grid=(n_steps,),
        in_specs=[tile(), tile(), tile(), tile(), tile(block=COL_MGP), tile(block=COL_MGH),
                  pl.BlockSpec((N_DEV, POOL_WIDTH, GROUP_DIM), lambda i: (0, 0, 0)),
                  full, full, _row_spec(), _row_spec(), hbm],
        out_specs=[tile(), tile(), tile(), tile(POOL_WIDTH), tile(), hbm,
                   pl.BlockSpec((8, D_MODEL), lambda i: (0, 0))],
        out_shape=[act, act, act, jax.ShapeDtypeStruct((seq, POOL_WIDTH), MXU_DTYPE), act,
                   jax.ShapeDtypeStruct(dz.shape, dz.dtype),
                   jax.ShapeDtypeStruct((8, D_MODEL), F32)],
        input_output_aliases={11: 5},
        scratch_shapes=[pltpu.VMEM((2, tm, 2 * D_MODEL), MXU_DTYPE), pltpu.SemaphoreType.DMA((2,))],
        compiler_params=_params(dimension_semantics=("arbitrary",)),
    )(dx, y, ba, bb, z, z, wpo_g, who_g, wout_g, gate, g_post, dz)


def _grad_out_weights(merged, dy, b_in, dbb, a_in, dba, name):
    seq = merged.shape[0]
    tn = D_MODEL // 2
    per_step = tn // GROUP_DIM

    def body(mer_ref, dy_ref, b_ref, dbb_ref, a_ref, dba_ref, gout_ref, gho_ref, gpo_ref):
        gout_ref[...] = _dot_tn(mer_ref[...], dy_ref[...]).astype(gout_ref.dtype)
        gho_ref[...] = _dot_tn(b_ref[...], dbb_ref[...]).astype(gho_ref.dtype)
        g_po = _dot_tn(a_ref[...], dba_ref[...])
        for j in range(per_step):
            gpo_ref[j] = g_po[:, j * GROUP_DIM:(j + 1) * GROUP_DIM].astype(gpo_ref.dtype)

    def whole(cols):
        return pl.BlockSpec((seq, cols), lambda j: (0, 0))

    cols = pl.BlockSpec((seq, tn), lambda j: (0, j))
    return pl.pallas_call(
        body, name=name, grid=(D_MODEL // tn,),
        in_specs=[whole(D_MODEL), cols, whole(D_MODEL), cols, whole(POOL_WIDTH), cols],
        out_specs=[pl.BlockSpec((D_MODEL, tn), lambda j: (0, j)), pl.BlockSpec((D_MODEL, tn), lambda j: (0, j)),
                   pl.BlockSpec((per_step, POOL_WIDTH, GROUP_DIM), lambda j: (j, 0, 0))],
        out_shape=[jax.ShapeDtypeStruct((D_MODEL, D_MODEL), WIRE_DTYPE),
                   jax.ShapeDtypeStruct((D_MODEL, D_MODEL), WIRE_DTYPE),
                   jax.ShapeDtypeStruct((N_DEV, POOL_WIDTH, GROUP_DIM), WIRE_DTYPE)],
        compiler_params=_params(dimension_semantics=("parallel",)),
    )(merged, dy, b_in, dbb, a_in, dba)


def _hgrn_bwd(db_in, z, o, states, lb_l, gn_l, dz, name, after=None):
    seq = z.shape[0]
    per_step = min(BWD_STEP_CHUNKS, seq // CHUNK)
    rows_per_step = per_step * CHUNK
    n_steps = seq // rows_per_step
    last_step = n_steps - 1

    def body(db_ref, hq_ref, hf_ref, hi_ref, hg_ref, o_ref, st_ref, lb_ref, gn_ref, _,
             dz_hbm, dlb_ref, dgn_ref, dstate, dq_buf, dk_buf, dg_buf, stage, sems):
        step = pl.program_id(0)

        @pl.when(step == 0)
        def _():
            dstate[...] = jnp.zeros_like(dstate)
            dlb_ref[...] = jnp.zeros_like(dlb_ref)
            dgn_ref[...] = jnp.zeros_like(dgn_ref)

        def one_chunk(cc, *args):
            one_chunk_body((db_ref, hq_ref, hf_ref, hi_ref, hg_ref, o_ref, st_ref, dlb_ref, dgn_ref, dstate,
                            dq_buf, dk_buf, dg_buf), cc, *args)

        def where(t):
            return pl.ds((last_step - t) * rows_per_step, rows_per_step), pl.ds(COL_HQ * D_MODEL, 4 * D_MODEL)

        dz_step = stage.at[_stage_begin(stage, sems, dz_hbm, step, where)]
        causal, before_sub, suffix = _chunk_masks()
        lb = lb_ref[...]
        gn = gn_ref[...]
        for cc in reversed(range(per_step)):
            one_chunk(cc, dz_step, causal, before_sub, suffix, lb, gn)
        _stage_end(stage, sems, dz_hbm, step, n_steps, where)

    def one_chunk_body(refs, cc, dz_step, causal, before_sub, suffix, lb, gn):
        (db_ref, hq_ref, hf_ref, hi_ref, hg_ref, o_ref, st_ref, dlb_ref, dgn_ref, dstate, dq_buf, dk_buf, dg_buf) = refs
        rows = slice(cc * CHUNK, (cc + 1) * CHUNK)
        dz_ref = dz_step.at[rows, :]
        dq_buf, dk_buf, dg_buf = dq_buf.at[cc], dk_buf.at[cc], dg_buf.at[cc]
        sg, f, logf = _gates(hf_ref[rows, :], lb)
        kk = 1.0 - f
        hq = hq_ref[rows, :]
        sq = _sigmoid(hq)
        q = hq * sq
        cum, base = _masked_sums([causal, before_sub], logf)
        dgn = jnp.zeros((1, HEAD_DIM), F32)
        dlast = []
        for h in range(HEADS):
            sl = slice(h * HEAD_DIM, (h + 1) * HEAD_DIM)
            q_h, k_h, cum_h = q[:, sl], kk[:, sl], cum[:, sl]
            v_h = hi_ref[rows, sl]
            st_h = st_ref[cc, h]
            dst_h = dstate[h]
            rs, ohat = _rms_parts(o_ref[rows, sl])
            hg = hg_ref[rows, sl]
            shg = _sigmoid(hg)
            d_bin = db_ref[rows, sl].astype(F32)
            don = d_bin * (hg * shg)
            dgn += jnp.sum(don * ohat, axis=0, keepdims=True)
            dohat = don * gn
            do = rs * (dohat - ohat * jnp.mean(dohat * ohat, axis=-1, keepdims=True))
            dz_ref[:, 3 * D_MODEL + h * HEAD_DIM:3 * D_MODEL + (h + 1) * HEAD_DIM] = (
                d_bin * (ohat * gn) * _dsilu(hg, shg)).astype(dz_ref.dtype)
            last = jnp.sum(logf[:, sl], axis=0, keepdims=True)
            g_in = jnp.exp(cum_h)
            d_out = jnp.exp(last - cum_h)
            q_bar, k_bar = q_h * g_in, k_h * d_out
            blocks = _intra_blocks(q_h, k_h, cum_h, base[:, sl], causal)
            a = jnp.concatenate([b[4] for b in blocks], axis=0)
            da = jnp.where(causal, _dot_nt(do, v_h), 0.0)
            dv = _dot_tn(a, do) + _dot_nt(k_bar, dst_h)
            dq_bar, dk_bar = _dot(do, st_h), _dot(v_h, dst_h)
            dk = dk_bar * d_out
            dq_parts, dg_parts = [], []
            dg_k = k_bar * dk_bar
            dlast.append(jnp.sum(k_bar * dk_bar, axis=0, keepdims=True)
                         + jnp.exp(last) * jnp.sum(st_h * dst_h, axis=0, keepdims=True))
            for i, (q_t, k_t, e_q, e_k, _) in enumerate(blocks):
                da_i = da[i * SUB:(i + 1) * SUB].astype(MXU_DTYPE)
                dq_t = _dot(da_i, k_t)
                dk_t = _dot_tn(da_i, q_t)
                dq_parts.append(dq_t * e_q)
                dk += dk_t * e_k
                dg_parts.append(q_t.astype(F32) * dq_t)
                dg_k += k_t.astype(F32) * dk_t
            dq = dq_bar * g_in + jnp.concatenate(dq_parts, axis=0)
            dg_buf[:, sl] = q_bar * dq_bar + jnp.concatenate(dg_parts, axis=0) - dg_k
            dstate[h] = dst_h * jnp.exp(last) + _dot_tn(do, q_bar)
            dq_buf[:, sl] = dq
            dk_buf[:, sl] = dk
            dz_ref[:, 2 * D_MODEL + h * HEAD_DIM:2 * D_MODEL + (h + 1) * HEAD_DIM] = dv.astype(dz_ref.dtype)
        dgn_ref[...] += dgn
        dq_all, dk_all = dq_buf[...], dk_buf[...]
        dlogf = _masked_sums([suffix], dg_buf[...])[0] + jnp.concatenate(dlast, axis=1)
        df = jnp.where(f > LOG_FLOOR, dlogf / f, 0.0) - dk_all
        dlb_ref[...] += jnp.sum(df * (1.0 - sg), axis=0, keepdims=True)
        dz_ref[:, 0:D_MODEL] = (dq_all * _dsilu(hq, sq)).astype(dz_ref.dtype)
        dz_ref[:, D_MODEL:2 * D_MODEL] = (df * (1.0 - lb) * sg * (1.0 - sg)).astype(dz_ref.dtype)

    def col(block):
        return pl.BlockSpec((rows_per_step, D_MODEL), lambda c: (last_step - c, block))

    hbm = pl.BlockSpec(memory_space=pl.ANY)
    return _pallas_after(
        body, 10, after, name=name, grid=(n_steps,),
        in_specs=[col(0), col(COL_HQ), col(COL_HF), col(COL_HI), col(COL_HG), col(0),
                  pl.BlockSpec((per_step, HEADS, HEAD_DIM, HEAD_DIM), lambda c: (last_step - c, 0, 0, 0)),
                  _row_spec(), _row_spec(HEAD_DIM), hbm],
        out_specs=[hbm, _row_spec(), _row_spec(HEAD_DIM)],
        out_shape=[jax.ShapeDtypeStruct(dz.shape, dz.dtype),
                   jax.ShapeDtypeStruct((1, D_MODEL), F32), jax.ShapeDtypeStruct((1, HEAD_DIM), F32)],
        input_output_aliases={9: 0},
        scratch_shapes=[pltpu.VMEM((HEADS, HEAD_DIM, HEAD_DIM), F32)]
        + [pltpu.VMEM((per_step, CHUNK, D_MODEL), F32)] * 3
        + [pltpu.VMEM((2, rows_per_step, 4 * D_MODEL), MXU_DTYPE), pltpu.SemaphoreType.DMA((2,))],
        compiler_params=_params(dimension_semantics=("arbitrary",)),
    )(db_in, z, z, z, z, o, states, lb_l, gn_l, dz)


def _pool_bwd(da_in, z, pool_w_l, pool_scale_l, dz, name, after=None):
    seq = z.shape[0]

    def body(da_ref, pv_ref, pg_ref, w_ref, sc_ref, _, dz_hbm, dw_ref, dsc_ref, stage_pv, stage_pg, sems_pv, sems_pg):
        g = pl.program_id(0)

        def where_pv(t):
            return pl.ds(0, seq), pl.ds(pl.multiple_of(t * GROUP_DIM, GROUP_DIM), GROUP_DIM)

        def where_pg(t):
            return pl.ds(0, seq), pl.ds(pl.multiple_of(POOL_WIDTH + t * GROUP_DIM, GROUP_DIM), GROUP_DIM)

        dpv_ref = stage_pv.at[_stage_begin(stage_pv, sems_pv, dz_hbm, g, where_pv)]
        dpg_ref = stage_pg.at[_stage_begin(stage_pg, sems_pg, dz_hbm, g, where_pg)]
        pos = lax.broadcasted_iota(jnp.int32, (seq, GROUP_DIM), 0)
        pm, count = _pool_mean_minus_token(pv_ref[...], g, pos)
        lin0 = _dot(pm, w_ref[...])
        pg = pg_ref[...]
        spg = _sigmoid(pg)
        da = da_ref[...].astype(F32)
        dlin = da * (pg * spg)
        dpg_ref[...] = (da * (lin0 * sc_ref[...]) * _dsilu(pg, spg)).astype(dpg_ref.dtype)
        dsc_ref[...] = jnp.sum(dlin * lin0, axis=0, keepdims=True)
        dl0 = dlin * sc_ref[...]
        dw_ref[...] = _dot_tn(pm, dl0)
        dpm = _dot_nt(dl0, w_ref[...])
        sums, acc = [], dpm / count
        for j in (1, 2, 4, 8):
            acc = acc + _shift_up(acc, j, pos, seq)
            sums.append(acc)
        dpv_ref[...] = (_select_window(g, sums) - dpm).astype(dpv_ref.dtype)
        _stage_end(stage_pv, sems_pv, dz_hbm, g, POOL_GROUPS, where_pv)
        _stage_end(stage_pg, sems_pg, dz_hbm, g, POOL_GROUPS, where_pg)

    grp = pl.BlockSpec((seq, GROUP_DIM), lambda g: (0, g))
    hbm = pl.BlockSpec(memory_space=pl.ANY)
    stage = pltpu.VMEM((2, seq, GROUP_DIM), MXU_DTYPE)
    return _pallas_after(
        body, 6, after, name=name, grid=(POOL_GROUPS,),
        in_specs=[grp, grp, pl.BlockSpec((seq, GROUP_DIM), lambda g: (0, POOL_GROUPS + g)),
                  pl.BlockSpec((None, GROUP_DIM, GROUP_DIM), lambda g: (g, 0, 0)),
                  pl.BlockSpec((1, GROUP_DIM), lambda g: (0, g)), hbm],
        out_specs=[hbm, pl.BlockSpec((None, GROUP_DIM, GROUP_DIM), lambda g: (g, 0, 0)),
                   pl.BlockSpec((1, GROUP_DIM), lambda g: (0, g))],
        out_shape=[jax.ShapeDtypeStruct(dz.shape, dz.dtype),
                   jax.ShapeDtypeStruct((POOL_GROUPS, GROUP_DIM, GROUP_DIM), F32),
                   jax.ShapeDtypeStruct((1, POOL_WIDTH), F32)],
        input_output_aliases={5: 0},
        scratch_shapes=[stage, stage, pltpu.SemaphoreType.DMA((2,)), pltpu.SemaphoreType.DMA((2,))],
        compiler_params=_params(dimension_semantics=("arbitrary",)),
    )(da_in, z, z, pool_w_l, pool_scale_l, dz)


def _in_proj_dw(h, dz, name, after=None):
    seq = h.shape[0]

    def body(h_ref, dz_ref, out_ref):
        pair = lax.dot_general(h_ref[...], dz_ref[...], (((0,), (0,)), ((), ())), preferred_element_type=F32)
        out_ref[0] = pair[:, 0:IN_COLS].astype(out_ref.dtype)
        out_ref[1] = pair[:, IN_COLS:].astype(out_ref.dtype)

    return _pallas_after(
        body, 2, after, name=name, grid=(N_DEV // 2,),
        in_specs=[pl.BlockSpec((seq, D_MODEL), lambda j: (0, 0)),
                  pl.BlockSpec((seq, 2 * IN_COLS), lambda j: (0, j))],
        out_specs=pl.BlockSpec((2, D_MODEL, IN_COLS), lambda j: (j, 0, 0)),
        out_shape=jax.ShapeDtypeStruct((N_DEV, D_MODEL, IN_COLS), WIRE_DTYPE),
        compiler_params=_params(dimension_semantics=("parallel",)),
    )(h, dz)


def _in_proj_dh(dz, win_g, tm, name, after=None):
    seq = dz.shape[0]

    def body(dz_ref, w_ref, dh_ref):
        @pl.when(pl.program_id(1) == 0)
        def _():
            dh_ref[...] = jnp.zeros_like(dh_ref)

        w_pair = jnp.concatenate([w_ref[0], w_ref[1]], axis=1)
        dh_ref[...] += lax.dot_general(dz_ref[...], w_pair, (((1,), (1,)), ((), ())), preferred_element_type=F32)

    return _pallas_after(
        body, 2, after, name=name, grid=(seq // tm, N_DEV // 2),
        in_specs=[pl.BlockSpec((tm, 2 * IN_COLS), lambda i, j: (i, j)),
                  pl.BlockSpec((2, D_MODEL, IN_COLS), lambda i, j: (j, 0, 0))],
        out_specs=pl.BlockSpec((tm, D_MODEL), lambda i, j: (i, 0)),
        out_shape=jax.ShapeDtypeStruct((seq, D_MODEL), F32),
        compiler_params=_params(dimension_semantics=("parallel", "arbitrary")),
    )(dz, win_g)


def _prenorm_bwd(x, dh, dx_res, g, scale, tm, name, after=None):
    seq = x.shape[0]

    def body(x_ref, dh_ref, dxr_ref, g_ref, sc_ref, dx_ref, acc_ref):
        @pl.when(pl.program_id(0) == 0)
        def _():
            acc_ref[...] = jnp.zeros_like(acc_ref)

        rs, xn = _rms_parts(x_ref[...])
        dh = dh_ref[...]
        acc_ref[0:1, :] += jnp.sum(dh, axis=0, keepdims=True)
        acc_ref[1:2, :] += jnp.sum(dh * (xn * g_ref[...]), axis=0, keepdims=True)
        dhn = dh * (1.0 + sc_ref[...])
        acc_ref[2:3, :] += jnp.sum(dhn * xn, axis=0, keepdims=True)
        dxn = dhn * g_ref[...]
        dx_ref[...] = rs * (dxn - xn * jnp.mean(dxn * xn, axis=-1, keepdims=True)) + dxr_ref[...]

    tile = pl.BlockSpec((tm, D_MODEL), lambda i: (i, 0))
    return _pallas_after(
        body, 5, after, name=name, grid=(seq // tm,),
        in_specs=[tile, tile, tile, _row_spec(), _row_spec()],
        out_specs=[tile, pl.BlockSpec((8, D_MODEL), lambda i: (0, 0))],
        out_shape=[jax.ShapeDtypeStruct((seq, D_MODEL), F32), jax.ShapeDtypeStruct((8, D_MODEL), F32)],
        compiler_params=_params(dimension_semantics=("arbitrary",)),
    )(x, dh, dx_res, g, scale)


def _adamw_math(w, g, m, v):
    m = ADAM_B1 * m + (1.0 - ADAM_B1) * g
    v = ADAM_B2 * v + (1.0 - ADAM_B2) * (g * g)
    m_hat = m / (1.0 - ADAM_B1 ** ADAM_STEP)
    v_hat = v / (1.0 - ADAM_B2 ** ADAM_STEP)
    delta = -ADAM_LR * (m_hat / (jnp.sqrt(v_hat) + ADAM_EPS) + ADAM_WD * w)
    return delta, m, v


def _adamw_sharded(w, m, v, contrib, tr, name):
    depth, rows, cols = w.shape
    n_parts = contrib.shape[1]

    def body(w_ref, m_ref, v_ref, c_ref, g_ref, d_ref, mo_ref, vo_ref):
        g = c_ref[0].astype(F32)
        for p in range(1, n_parts):
            g += c_ref[p].astype(F32)
        delta, mn, vn = _adamw_math(w_ref[...], g, m_ref[...], v_ref[...])
        g_ref[...] = g
        d_ref[...] = delta
        mo_ref[...] = mn
        vo_ref[...] = vn

    tile = pl.BlockSpec((None, tr, cols), lambda l, i: (l, i, 0))
    shape = jax.ShapeDtypeStruct(w.shape, F32)
    return pl.pallas_call(
        body, name=name, grid=(depth, rows // tr),
        in_specs=[tile, tile, tile, pl.BlockSpec((None, n_parts, tr, cols), lambda l, i: (l, 0, i, 0))],
        out_specs=[tile] * 4, out_shape=[shape] * 4,
        compiler_params=_params(dimension_semantics=("parallel", "parallel")),
    )(w, m, v, contrib)


def _adamw_layer(w, m, v, contribs, l, tr, name, prev=None):
    _, rows, cols = w.shape
    n = len(contribs)

    def body(*refs):
        w_ref, m_ref, v_ref = refs[:3]
        c_refs = refs[3:3 + n]
        g_ref, d_ref, mo_ref, vo_ref = refs[-4:]
        g = c_refs[0][...].astype(F32)
        for c_ref in c_refs[1:]:
            g += c_ref[...].astype(F32)
        delta, mn, vn = _adamw_math(w_ref[...], g, m_ref[...], v_ref[...])
        g_ref[...] = g
        d_ref[...] = delta
        mo_ref[...] = mn
        vo_ref[...] = vn

    tile = pl.BlockSpec((None, tr, cols), lambda i: (l, i, 0))
    in_specs = [tile, tile, tile] + [pl.BlockSpec((None, tr, cols), lambda i, s=slot: (s, i, 0)) for _, slot in contribs]
    operands = [w, m, v] + [arr for arr, _ in contribs]
    aliases = {}
    if prev is not None:
        aliases = {len(operands) + k: k for k in range(4)}
        in_specs += [pl.BlockSpec(memory_space=pl.ANY)] * 4
        operands += list(prev)
    shape = jax.ShapeDtypeStruct(w.shape, F32)
    return pl.pallas_call(
        body, name=name, grid=(rows // tr,), in_specs=in_specs, out_specs=[tile] * 4, out_shape=[shape] * 4,
        input_output_aliases=aliases,
        compiler_params=_params(dimension_semantics=("parallel",)),
    )(*operands)


def _adamw_small(w_pack, m_pack, v_pack, g_late, g_early, shapes):
    pieces, r = {}, 0
    for name, _, n in _SMALL_ROWS:
        pieces.setdefault(name, []).append((r, n))
        r += n
    names = list(pieces)

    def body(w_ref, m_ref, v_ref, gl_ref, ge_ref, *rest):
        outs, packs = rest[:4 * len(names)], rest[4 * len(names):]
        g_l, g_e = gl_ref[0][0:SMALL_LATE_ROWS], ge_ref[0]
        for d in range(1, N_DEV):
            g_l += gl_ref[d][0:SMALL_LATE_ROWS]
            g_e += ge_ref[d]
        g = jnp.concatenate([g_l, g_e], axis=0)
        w = w_ref[...]
        r0, r1, r2 = LB_ROW0, LB_ROW0 + 8, LB_ROW0 + 16
        lg0, lg1 = w[r0:r1], w[r1:r2]
        mx = jnp.maximum(lg0, lg1)
        e0, e1 = jnp.exp(lg0 - mx), jnp.exp(lg1 - mx)
        p0, p1 = e0 / (e0 + e1), e1 / (e0 + e1)
        low = ((p0 - p0), (p0 + p1) - p0)
        dlow = [g_rows * jnp.where((lo > 0.0) & (lo < 1.0), 1.0, jnp.where((lo == 0.0) | (lo == 1.0), 0.5, 0.0))
                for g_rows, lo in ((g[r0:r1], low[0]), (g[r1:r2], low[1]))]
        dp0 = (dlow[0] + dlow[1]) - (dlow[0] + dlow[1])
        dp1 = dlow[1]
        inner = p0 * dp0 + p1 * dp1
        g = jnp.concatenate([g[:r0], p0 * (dp0 - inner), p1 * (dp1 - inner), g[r2:]], axis=0)
        delta, mn, vn = _adamw_math(w, g, m_ref[...], v_ref[...])
        for kind, val in enumerate((g, delta, mn, vn)):
            packs[kind][...] = val
            for j, name in enumerate(names):
                out, at = outs[kind * len(names) + j], 0
                for start, n in pieces[name]:
                    if name in flat:
                        for r in range(n):
                            layer, c = divmod(at + r, flat[name])
                            out[layer:layer + 1, c * 128:(c + 1) * 128] = packs[kind][start + r:start + r + 1, :]
                    else:
                        out[at:at + n, :] = packs[kind][start:start + n, :]
                    at += n

    rows = {name: sum(n for _, n in pieces[name]) for name in names}
    flat = {name: rows[name] // DEPTH for name in names if len(shapes[name]) == 2}
    outs = pl.pallas_call(
        body, name="adamw_small",
        out_shape=[jax.ShapeDtypeStruct(shapes[name] if name in flat else (rows[name], 128), F32)
                   for _ in range(4) for name in names],
        scratch_shapes=[pltpu.VMEM(w_pack.shape, F32)] * 4, compiler_params=_params(),
    )(w_pack, m_pack, v_pack, g_late, g_early)
    return [{name: outs[kind * len(names) + j].reshape(shapes[name]) for j, name in enumerate(names)}
            for kind in range(4)]


def _pack_small(parts, first=0, last=len(_SMALL_ROWS)):
    rows = [(parts[name] if l is None else parts[name][l]).reshape(n, 128) for name, l, n in _SMALL_ROWS[first:last]]
    if last == len(_SMALL_ROWS):
        rows.append(jnp.zeros((SMALL_ROWS_PAD - sum(n for _, _, n in _SMALL_ROWS), 128), F32))
    return jnp.concatenate(rows, axis=0)


def kernel(x, c, w_ada, b_ada, g_pre, g_post, w_in, pool_w, pool_scale, lb_logits, hgrn_norm_g, w_pool_o, w_hgrn_o, w_out, loss_target, m_w_ada, m_b_ada, m_g_pre, m_g_post, m_w_in, m_pool_w, m_pool_scale, m_lb_logits, m_hgrn_norm_g, m_w_pool_o, m_w_hgrn_o, m_w_out, v_w_ada, v_b_ada, v_g_pre, v_g_post, v_w_in, v_pool_w, v_pool_scale, v_lb_logits, v_hgrn_norm_g, v_w_pool_o, v_w_hgrn_o, v_w_out):
    seq = x.shape[1]
    tm = min(512, seq)
    tm_merge = min(256, seq)
    pos = _my_position()
    me = pos[3]

    c_all = _allgather_small(c, "allgather_c").reshape(N_DEV, D_MODEL)
    b_cols = lax.dynamic_slice_in_dim(b_ada, me * ADA_COLS, ADA_COLS, axis=1)
    ada_part = _ada_fwd(c_all, w_ada, b_cols)
    ada_all = _allgather_small(ada_part.reshape(DEPTH * N_DEV, ADA_COLS), "allgather_ada")
    ada = lax.dynamic_index_in_dim(ada_all.reshape(N_DEV, DEPTH, N_DEV, ADA_COLS), me, axis=2, keepdims=False)
    ada = jnp.transpose(ada, (1, 0, 2)).reshape(DEPTH, 3 * D_MODEL)
    shift = [ada[l:l + 1, 0:D_MODEL] for l in range(DEPTH)]
    scale = [ada[l:l + 1, D_MODEL:2 * D_MODEL] for l in range(DEPTH)]
    gate = [ada[l:l + 1, 2 * D_MODEL:] for l in range(DEPTH)]

    big = dict(win=w_in, wpo=w_pool_o, who=w_hgrn_o, wout=w_out)
    units = [["win0"], ["wpo0", "who0", "wout0"], ["win1", "wpo1", "who1", "wout1"]]
    g_streams = [_gather_streams(keys) for keys in units]
    g_state = [None] * len(units)

    def gather_start(us, after):
        bufs = {}
        for k in [k for u in us for k in units[u]]:
            arr = big[k[:-1]]
            bufs["s_" + k] = arr[int(k[-1])].astype(WIRE_DTYPE)
            bufs["g_" + k] = _with_own_slot(bufs["s_" + k], me)
        bufs, sems, token = _comm_call("gather_start_" + "_".join(map(str, us)), bufs,
                                       start=[s for u in us for s in g_streams[u][:2]], after=after)
        for n, u in enumerate(us):
            g_state[u] = dict(bufs={p + k: bufs[p + k] for k in units[u] for p in ("s_", "g_")},
                              sems=sems[2 * n:2 * n + 2])
        return token

    def gather_pass(u, after):
        st = g_state[u]
        to_chips, _, pass_on = g_streams[u]
        st["bufs"], (st["pass_sems"],), _ = _comm_call(f"gather_pass_{u}", st["bufs"], start=[pass_on],
                                                       wait=[(to_chips, st["sems"][0])], after=after)

    def gather_done(u, after=None):
        st = g_state[u]
        _, to_sibling, pass_on = g_streams[u]
        bufs, _, _ = _comm_call(f"gather_done_{u}", st["bufs"], after=after,
                                wait=[(to_sibling, st["sems"][1]), (pass_on, st["pass_sems"])])
        return {k: bufs["g_" + k] for k in units[u]}

    token = gather_start([0], ada_all)

    lb = _lb_fwd(lb_logits)

    gw = {}
    xs, saved = [x[0]], []
    for l in range(DEPTH):
        h = _prenorm_fwd(xs[l], g_pre[l:l + 1], shift[l], scale[l], tm, f"prenorm_fwd_{l}",
                         after=token if l == 0 else None)
        token = None
        if l == 0:
            gather_pass(0, h)
            gw.update(gather_done(0))
            token = gather_start([1, 2], gw["win0"])
        else:
            gw.update(gather_done(2, h))
        z = _in_proj(h, gw[f"win{l}"], min(1024, seq), f"in_proj_{l}", after=token)
        a_in = _pool_fwd(z, pool_w[l], pool_scale[l:l + 1], f"pool_fwd_{l}")
        o, b_in, states = _hgrn_fwd(z, lb[l:l + 1], hgrn_norm_g[l:l + 1], f"hgrn_fwd_{l}")
        if l == 0:
            gather_pass(1, b_in)
            gw.update(gather_done(1))
        who_l = gw[f"who{l}"].reshape(D_MODEL, D_MODEL)
        wout_l = gw[f"wout{l}"].reshape(D_MODEL, D_MODEL)
        last = l == DEPTH - 1
        ba, bb, merged, y, *out = _merge_fwd(a_in, b_in, z, xs[l], gw[f"wpo{l}"], who_l, wout_l, gate[l],
                                             g_post[l:l + 1], tm_merge, f"merge_fwd_{l}",
                                             target=loss_target[0] if last else None)
        if last:
            dx, loss_part = out
        else:
            xs.append(out[0])
            gather_pass(2, out[0])
        saved.append((h, z, a_in, o, b_in, states, ba, bb, merged, y, who_l, wout_l))


    chips = _other_chips(pos)
    pair_idx = jnp.stack([_dev_index(cx, cy, pos[2]) for cx, cy in chips] + [me]).astype(jnp.int32)
    pair_rows = dict(win=256, wpo=POOL_WIDTH, who=HEAD_DIM, wout=HEAD_DIM)

    def scatter_pair_start(u, grads):
        keys = list(grads)
        pair, to_chips = _scatter_streams(keys)
        bufs = {}
        for k in keys:
            bufs["g_" + k] = grads[k]
            bufs["st_" + k] = lax.empty((4,) + grads[k].shape[1:], WIRE_DTYPE)
        bufs, (sems,), token = _comm_call(f"scatter_pair_start_{u}", bufs, start=[pair])
        return dict(u=u, keys=keys, pair=pair, to_chips=to_chips, bufs=bufs, sems=sems, token=token)

    def scatter_pair_finish(st, after):
        u, keys = st["u"], st["keys"]
        bufs, _, _ = _comm_call(f"scatter_pair_done_{u}", st["bufs"], wait=[(st["pair"], st["sems"])], after=after)
        bufs2 = {}
        for k in keys:
            bufs2["ps_" + k] = _pair_sum(bufs["g_" + k], bufs["st_" + k], pair_idx, bufs["g_" + k].shape[1],
                                         f"pair_sum_{k}")
            bufs2["ld_" + k] = lax.empty((3,) + bufs["g_" + k].shape[1:], WIRE_DTYPE)
        st.update(bufs=bufs2)

    def scatter_chips_start(st, after=None):
        bufs2, (sems,), token = _comm_call(f"scatter_chips_start_{st['u']}", st["bufs"], start=[st["to_chips"]],
                                           after=after)
        st.update(bufs=bufs2, sems=sems, token=token)

    def scatter_finish(st, after):
        bufs, _, _ = _comm_call(f"scatter_chips_done_{st['u']}", st["bufs"], wait=[(st["to_chips"], st["sems"])],
                                after=after)
        return {k: [(bufs["ps_" + k], 3), (bufs["ld_" + k], 0), (bufs["ld_" + k], 1), (bufs["ld_" + k], 2)]
                for k in st["keys"]}

    moments = dict(win=(m_w_in, v_w_in), wpo=(m_w_pool_o, v_w_pool_o), who=(m_w_hgrn_o, v_w_hgrn_o),
                   wout=(m_w_out, v_w_out))
    big_out = {}

    def finish_unit(unit, after):
        for k, contribs in scatter_finish(scat[unit], after).items():
            wname, l = k[:-1], int(k[-1])
            big_out[wname] = _adamw_layer(big[wname], moments[wname][0], moments[wname][1], contribs, l,
                                          pair_rows[wname], f"adamw_{k}", prev=big_out.get(wname))
            after = big_out[wname][0]
        return after

    d_ada, small, scat = [None] * DEPTH, [None] * DEPTH, {}
    for l in reversed(range(DEPTH)):
        h, z, a_in, o, b_in, states, ba, bb, merged, y, who_l, wout_l = saved[l]
        dy, dba, dbb, da_in, db_in, dz, acc_post = _merge_bwd(
            dx, y, ba, bb, z, gw[f"wpo{l}"], who_l, wout_l, gate[l], g_post[l:l + 1],
            lax.empty((seq, IN_WIDTH), MXU_DTYPE), tm_merge, f"merge_bwd_{l}")
        g_out, g_ho, g_po = _grad_out_weights(merged, dy, b_in, dbb, a_in, dba, f"grad_out_weights_{l}")
        g_small = {f"wout{l}": g_out.reshape(N_DEV, HEAD_DIM, D_MODEL),
                   f"who{l}": g_ho.reshape(N_DEV, HEAD_DIM, D_MODEL), f"wpo{l}": g_po}
        st_small = scat["small0"] = scatter_pair_start("small0", g_small) if l == 0 else None
        dz, dlb, dgn = _hgrn_bwd(db_in, z, o, states, lb[l:l + 1], hgrn_norm_g[l:l + 1], dz, f"hgrn_bwd_{l}",
                                 after=st_small and st_small["token"])
        if l == 0:
            scatter_pair_finish(st_small, dlb)
            scatter_chips_start(st_small)
        dz, dpw, dps = _pool_bwd(da_in, z, pool_w[l], pool_scale[l:l + 1], dz, f"pool_bwd_{l}",
                                 after=st_small and st_small["token"])
        small[l] = dict(g_post=acc_post[1], pool_w=dpw, pool_scale=dps[0], lb_logits=dlb[0], hgrn_norm_g=dgn[0])
        token = None
        if l == 0:
            parts = {name: jnp.stack([small[0][name], small[1][name]]) for name in small[0]}
            parts.update(b_ada=[None, d_ada[1]], g_pre=[None, small[1]["g_pre"]])
            sg_stream = _direct_gather_stream("sg")
            early = _pack_small(parts, 2)
            sg_bufs, (sg_sems,), token = _comm_call(
                "small_grads_start", dict(s_sg=early, g_sg=_with_own_slot(early, me)), start=[sg_stream])
        g_win = {f"win{l}": _in_proj_dw(h, dz, f"grad_w_in_{l}", after=token)}
        st_win = scat[f"win{l}"] = scatter_pair_start(f"win{l}", g_win if l == 0 else {**g_small, **g_win})
        if l > 0:
            dh = _in_proj_dh(dz, gw[f"win{l}"], seq, f"in_proj_dh_{l}", after=st_win["token"])
            scatter_pair_finish(st_win, dh)
            scatter_chips_start(st_win)
        else:
            scatter_pair_finish(st_win, st_win["token"])
            scatter_chips_start(st_win)
            after = st_win["token"]
            for unit in ("win1", "small0"):
                after = finish_unit(unit, after)
            dh = _in_proj_dh(dz, gw[f"win{l}"], seq, f"in_proj_dh_{l}", after=after)
        dx, acc_pre = _prenorm_bwd(xs[l], dh, dx, g_pre[l:l + 1], scale[l], tm, f"prenorm_bwd_{l}",
                                   after=st_win["token"])
        d_ada[l] = jnp.concatenate([acc_pre[0], acc_pre[1], acc_post[0]])
        small[l]["g_pre"] = acc_pre[2]
    grad_x = dx[None]

    parts = dict(b_ada=[d_ada[0]], g_pre=[small[0]["g_pre"]])
    late = jnp.concatenate([_pack_small(parts, 0, 2), jnp.broadcast_to(loss_part, (8, 128))], axis=0)
    g_late = _allgather_small(late, "allgather_late_grads")
    loss = jnp.sum(g_late[:, SMALL_LATE_ROWS, 0])
    sg_bufs, _, _ = _comm_call("small_grads_done", sg_bufs, wait=[(sg_stream, sg_sems)], after=g_late)
    g_early = sg_bufs["g_sg"]
    small_names = list(dict.fromkeys(name for name, _, _ in _SMALL_ROWS))
    weights = dict(b_ada=b_ada, g_pre=g_pre, g_post=g_post, pool_w=pool_w, pool_scale=pool_scale,
                   lb_logits=lb_logits, hgrn_norm_g=hgrn_norm_g)
    m_small = dict(b_ada=m_b_ada, g_pre=m_g_pre, g_post=m_g_post, pool_w=m_pool_w, pool_scale=m_pool_scale,
                   lb_logits=m_lb_logits, hgrn_norm_g=m_hgrn_norm_g)
    v_small = dict(b_ada=v_b_ada, g_pre=v_g_pre, g_post=v_g_post, pool_w=v_pool_w, pool_scale=v_pool_scale,
                   lb_logits=v_lb_logits, hgrn_norm_g=v_hgrn_norm_g)
    shapes = {name: weights[name].shape for name in small_names}
    small_out = _adamw_small(_pack_small(weights), _pack_small(m_small), _pack_small(v_small), g_late, g_early,
                             shapes)

    d_ada_all = jnp.stack([g_late[:, 0:24, :].reshape(N_DEV, 3 * D_MODEL),
                           g_early[:, 0:24, :].reshape(N_DEV, 3 * D_MODEL)], axis=1)
    d_cols = jnp.transpose(lax.dynamic_slice_in_dim(d_ada_all, me * ADA_COLS, ADA_COLS, axis=2), (1, 0, 2))
    g_w_ada = _ada_bwd(c_all, d_cols)
    ada_out = _adamw_sharded(w_ada, m_w_ada, v_w_ada, g_w_ada[:, None], 256, "adamw_w_ada")
    finish_unit("win0", ada_out[1][0, 0:8, 0:128] + small_out[1]["pool_scale"][0:1, 0:128])

    def leaf(kind):
        s = small_out[kind]
        return (ada_out[kind], s["b_ada"], s["g_pre"], s["g_post"], big_out["win"][kind], s["pool_w"], s["pool_scale"],
                s["lb_logits"], s["hgrn_norm_g"], big_out["wpo"][kind], big_out["who"][kind], big_out["wout"][kind])

    return (loss, grad_x) + leaf(0) + leaf(1) + leaf(2) + leaf(3)
```

```python
import jax
import jax.numpy as jnp
from jax import lax
from jax.experimental import pallas as pl
from jax.experimental.pallas import tpu as pltpu

F32 = jnp.float32
MXU_DTYPE = jnp.bfloat16
WIRE_DTYPE = jnp.bfloat16

N_DEV = 8
DEPTH = 2
D_MODEL = 1024
HEADS = 8
HEAD_DIM = 128
POOL_GROUPS = 4
GROUP_DIM = 128
POOL_WIDTH = POOL_GROUPS * GROUP_DIM
IN_WIDTH = 7168
CHUNK = 64
SUB = 16
N_SUB = CHUNK // SUB
FWD_STEP_CHUNKS = 8
BWD_STEP_CHUNKS = 4
EXP_CLAMP = 80.0
NORM_EPS = 1e-6
LOG_FLOOR = 1e-30
ADA_COLS = 3 * D_MODEL // N_DEV
IN_COLS = IN_WIDTH // N_DEV
COL_HQ, COL_HF, COL_HI, COL_HG, COL_MGP, COL_MGH = 1, 2, 3, 4, 5, 6

ADAM_LR = 0.001
ADAM_B1 = 0.9
ADAM_B2 = 0.999
ADAM_EPS = 1e-08
ADAM_WD = 0.01
ADAM_STEP = 10

VMEM_LIMIT = 48 * 1024 * 1024
MESH_ID = pl.DeviceIdType.MESH
HIGHEST = lax.Precision.HIGHEST

_SMALL_ROWS = (("b_ada", 0, 24), ("g_pre", 0, 8), ("b_ada", 1, 24), ("g_pre", 1, 8), ("g_post", None, 16),
               ("pool_w", None, 1024), ("pool_scale", None, 8), ("lb_logits", None, 16), ("hgrn_norm_g", None, 2))
SMALL_LATE_ROWS = 32
SMALL_ROWS_PAD = 1136
LB_ROW0 = 32 + 32 + 16 + 1024 + 8


def _params(**kw):
    return pltpu.CompilerParams(vmem_limit_bytes=VMEM_LIMIT, **kw)


def _sigmoid(v):
    return 1.0 / (1.0 + jnp.exp(-v))


def _dsilu(v, s):
    return s * (1.0 + v * (1.0 - s))


def _dot(a, b):
    return jnp.dot(a.astype(MXU_DTYPE), b.astype(MXU_DTYPE), preferred_element_type=F32)


def _dot_nt(a, b):
    return lax.dot_general(a.astype(MXU_DTYPE), b.astype(MXU_DTYPE), (((1,), (1,)), ((), ())),
                           preferred_element_type=F32)


def _dot_tn(a, b):
    return lax.dot_general(a.astype(MXU_DTYPE), b.astype(MXU_DTYPE), (((0,), (0,)), ((), ())),
                           preferred_element_type=F32)


def _pallas_after(body, n_in, after, *, in_specs, **kw):
    def wants_vmem(spec):
        return getattr(spec, "memory_space", None) == pltpu.VMEM

    def in_hbm(operands):
        return [o if wants_vmem(spec) else pltpu.with_memory_space_constraint(o, pltpu.HBM)
                for o, spec in zip(operands, in_specs)]

    shapes, specs = kw["out_shape"], kw["out_specs"]
    single = not isinstance(shapes, (list, tuple))
    hbm_shapes = [s if wants_vmem(spec) or not isinstance(s, jax.ShapeDtypeStruct) else pltpu.HBM(s.shape, s.dtype)
                  for s, spec in zip([shapes] if single else shapes, [specs] if single else specs)]
    kw["out_shape"] = hbm_shapes[0] if single else hbm_shapes

    if after is None:
        call = pl.pallas_call(body, in_specs=in_specs, **kw)
        return lambda *operands: call(*in_hbm(operands))

    def tied(*refs):
        body(*refs[:n_in], *refs[n_in + 1:])

    call = pl.pallas_call(tied, in_specs=list(in_specs) + [pl.BlockSpec(memory_space=pl.ANY)], **kw)
    return lambda *operands: call(*in_hbm(operands), after)


def _my_position():
    mx, my, mc = lax.axis_index("x"), lax.axis_index("y"), lax.axis_index("c")
    return mx, my, mc, 4 * mx + 2 * my + mc


def _peer(mx, my, mc, k):
    px = 1 - mx if (k >> 2) & 1 else mx
    py = 1 - my if (k >> 1) & 1 else my
    pc = 1 - mc if k & 1 else mc
    return (px, py, pc), 4 * px + 2 * py + pc


def _allgather_small(v, name, after=None):
    rows, cols = v.shape

    def body(v_ref, out_ref, send_sems, recv_sems):
        mx, my, mc, me = _my_position()
        out_ref[me] = v_ref[...]
        copies = []
        for k in range(1, N_DEV):
            peer, _ = _peer(mx, my, mc, k)
            cp = pltpu.make_async_remote_copy(
                src_ref=v_ref, dst_ref=out_ref.at[me],
                send_sem=send_sems.at[k - 1], recv_sem=recv_sems.at[k - 1],
                device_id=peer, device_id_type=MESH_ID)
            cp.start()
            copies.append(cp)
        for cp in copies:
            cp.wait()

    return _pallas_after(
        body, 1, after, name=name,
        out_shape=jax.ShapeDtypeStruct((N_DEV, rows, cols), v.dtype),
        in_specs=[pl.BlockSpec(memory_space=pltpu.VMEM)],
        out_specs=pl.BlockSpec(memory_space=pltpu.VMEM),
        scratch_shapes=[pltpu.SemaphoreType.DMA((N_DEV - 1,)), pltpu.SemaphoreType.DMA((N_DEV - 1,))],
        compiler_params=_params(),
    )(v)


class _Stream:
    def __init__(self, n, plan):
        self.n, self.plan = n, plan


def _comm_call(name, bufs, start=(), wait=(), after=None):
    names = list(bufs)

    def body(*refs):
        it = iter(refs)
        buf_refs = {n: next(it) for n in names}
        wait_sems = [(next(it), next(it)) for _ in wait]
        if after is not None:
            next(it)
        start_sems = [(next(it), next(it)) for _ in start]
        for _ in names:
            next(it)
        token = next(it)
        pos = _my_position()

        def descriptors(stream, sems):
            return [pltpu.make_async_remote_copy(src_ref=src, dst_ref=dst, send_sem=sems[0].at[k], recv_sem=sems[1].at[k],
                                                 device_id=dev, device_id_type=MESH_ID)
                    for k, (src, dst, dev) in enumerate(stream.plan(buf_refs, pos))]

        for (stream, _), sems in zip(wait, wait_sems):
            for cp in descriptors(stream, sems):
                cp.wait_send()
                cp.wait_recv()
        for stream, sems in zip(start, start_sems):
            for cp in descriptors(stream, sems):
                cp.start()
        token[...] = jnp.zeros_like(token)

    hbm = pl.BlockSpec(memory_space=pltpu.HBM)
    sem = pl.BlockSpec(memory_space=pltpu.SEMAPHORE)
    operands = [pltpu.with_memory_space_constraint(bufs[n], pltpu.HBM) for n in names]
    in_specs = [hbm] * len(names)
    for _, (send_sems, recv_sems) in wait:
        operands += [send_sems, recv_sems]
        in_specs += [sem, sem]
    if after is not None:
        operands.append(after)
        in_specs.append(pl.BlockSpec(memory_space=pl.ANY))
    out_shape, out_specs = [], []
    for stream in start:
        out_shape += [pltpu.SemaphoreType.DMA((stream.n,)), pltpu.SemaphoreType.DMA((stream.n,))]
        out_specs += [sem, sem]
    n_sem_out = len(out_shape)
    out_shape += [pltpu.HBM(bufs[n].shape, bufs[n].dtype) for n in names]
    out_specs += [hbm] * len(names)
    out_shape.append(jax.ShapeDtypeStruct((8, 128), F32))
    out_specs.append(pl.BlockSpec(memory_space=pltpu.VMEM))
    outs = pl.pallas_call(
        body, name=name, out_shape=out_shape, in_specs=in_specs, out_specs=out_specs,
        input_output_aliases={i: n_sem_out + i for i in range(len(names))},
        compiler_params=pltpu.CompilerParams(has_side_effects=pltpu.SideEffectType.DATAFLOW_SIDE_EFFECTING),
    )(*operands)
    sems = [(outs[2 * i], outs[2 * i + 1]) for i in range(len(start))]
    return dict(zip(names, outs[n_sem_out:n_sem_out + len(names)])), sems, outs[-1]


def _with_own_slot(block, me):
    return lax.dynamic_update_index_in_dim(lax.empty((N_DEV,) + block.shape, block.dtype), block, me, 0)


def _other_chips(pos):
    mx, my, _, _ = pos
    return [(1 - mx if i & 2 else mx, 1 - my if i & 1 else my) for i in (1, 2, 3)]


def _dev_index(px, py, pc):
    return 4 * px + 2 * py + pc


def _gather_streams(keys):
    def to_chips(refs, pos):
        _, _, mc, me = pos
        return [(refs["s_" + k], refs["g_" + k].at[me], (cx, cy, mc)) for k in keys for cx, cy in _other_chips(pos)]

    def to_sibling(refs, pos):
        mx, my, mc, me = pos
        return [(refs["s_" + k], refs["g_" + k].at[me], (mx, my, 1 - mc)) for k in keys]

    def pass_on(refs, pos):
        mx, my, mc, _ = pos
        out = []
        for k in keys:
            for cx, cy in _other_chips(pos):
                slot = refs["g_" + k].at[_dev_index(cx, cy, mc)]
                out.append((slot, slot, (mx, my, 1 - mc)))
        return out

    return _Stream(3 * len(keys), to_chips), _Stream(len(keys), to_sibling), _Stream(3 * len(keys), pass_on)


def _direct_gather_stream(key):
    def plan(refs, pos):
        mx, my, mc, me = pos
        return [(refs["s_" + key], refs["g_" + key].at[me], _peer(mx, my, mc, k)[0]) for k in range(1, N_DEV)]

    return _Stream(N_DEV - 1, plan)


def _scatter_streams(keys):
    def pair(refs, pos):
        mx, my, mc, _ = pos
        sib = (mx, my, 1 - mc)
        out = []
        for k in keys:
            for i, (cx, cy) in enumerate(_other_chips(pos)):
                out.append((refs["g_" + k].at[_dev_index(cx, cy, 1 - mc)], refs["st_" + k].at[i], sib))
            out.append((refs["g_" + k].at[_dev_index(mx, my, 1 - mc)], refs["st_" + k].at[3], sib))
        return out

    def chips(refs, pos):
        mc = pos[2]
        return [(refs["ps_" + k].at[i], refs["ld_" + k].at[i], (cx, cy, mc))
                for k in keys for i, (cx, cy) in enumerate(_other_chips(pos))]

    return _Stream(4 * len(keys), pair), _Stream(3 * len(keys), chips)


def _pair_sum(g, st, idx, tr, name):
    _, rows, cols = g.shape

    def body(idx_ref, g_ref, st_ref, out_ref):
        out_ref[...] = (g_ref[...].astype(F32) + st_ref[...].astype(F32)).astype(out_ref.dtype)

    return pl.pallas_call(
        body, name=name,
        grid_spec=pltpu.PrefetchScalarGridSpec(
            num_scalar_prefetch=1, grid=(4, rows // tr),
            in_specs=[pl.BlockSpec((None, tr, cols), lambda j, i, idx_ref: (idx_ref[j], i, 0)),
                      pl.BlockSpec((None, tr, cols), lambda j, i, idx_ref: (j, i, 0))],
            out_specs=pl.BlockSpec((None, tr, cols), lambda j, i, idx_ref: (j, i, 0))),
        out_shape=pltpu.HBM((4, rows, cols), WIRE_DTYPE),
        compiler_params=_params(dimension_semantics=("parallel", "parallel")),
    )(idx, pltpu.with_memory_space_constraint(g, pltpu.HBM), pltpu.with_memory_space_constraint(st, pltpu.HBM))


def _ada_fwd(c_all, w_ada, b_cols):
    def body(c_ref, w_ref, b_ref, out_ref):
        cv = c_ref[...]
        ca = cv * _sigmoid(cv)
        for l in range(DEPTH):
            out_ref[l] = jnp.dot(ca, w_ref[l], precision=HIGHEST, preferred_element_type=F32) + b_ref[l:l + 1, :]

    return pl.pallas_call(
        body, name="ada_fwd",
        out_shape=jax.ShapeDtypeStruct((DEPTH, N_DEV, ADA_COLS), F32),
        compiler_params=_params(),
    )(c_all, w_ada, b_cols)


def _ada_bwd(c_all, d_cols):
    def body(c_ref, d_ref, out_ref):
        cv = c_ref[...]
        ca = cv * _sigmoid(cv)
        for l in range(DEPTH):
            out_ref[l] = lax.dot_general(ca, d_ref[l], (((0,), (0,)), ((), ())), precision=HIGHEST,
                                         preferred_element_type=F32)

    return pl.pallas_call(
        body, name="ada_bwd",
        out_shape=jax.ShapeDtypeStruct((DEPTH, D_MODEL, ADA_COLS), F32),
        compiler_params=_params(),
    )(c_all, d_cols)


def _lower_bounds(logits):
    m = jnp.maximum(logits[0:1], logits[1:2])
    e0, e1 = jnp.exp(logits[0:1] - m), jnp.exp(logits[1:2] - m)
    den = e0 + e1
    p0, p1 = e0 / den, e1 / den
    low0 = p0 - p0
    low1 = (p0 + p1) - p0
    return (p0, p1), (low0, low1)


def _lb_fwd(lb_logits):
    def body(lg_ref, out_ref):
        _, (low0, low1) = _lower_bounds(lg_ref[...])
        out_ref[0:1, :] = jnp.clip(low0, 0.0, 1.0)
        out_ref[1:2, :] = jnp.clip(low1, 0.0, 1.0)

    return pl.pallas_call(body, name="lb_fwd", out_shape=jax.ShapeDtypeStruct(lb_logits.shape, F32),
                          compiler_params=_params())(lb_logits)


def _row_spec(cols=D_MODEL):
    return pl.BlockSpec((1, cols), lambda *_: (0, 0))


def _prenorm_fwd(x, g, shift, scale, tm, name, after=None):
    seq = x.shape[0]

    def body(x_ref, g_ref, sh_ref, sc_ref, h_ref):
        xv = x_ref[...]
        rs = lax.rsqrt(jnp.mean(xv * xv, axis=-1, keepdims=True) + NORM_EPS)
        h = (xv * rs * g_ref[...]) * (1.0 + sc_ref[...]) + sh_ref[...]
        h_ref[...] = h.astype(h_ref.dtype)

    tile = pl.BlockSpec((tm, D_MODEL), lambda i: (i, 0))
    return _pallas_after(
        body, 4, after, name=name, grid=(seq // tm,),
        in_specs=[tile, _row_spec(), _row_spec(), _row_spec()], out_specs=tile,
        out_shape=jax.ShapeDtypeStruct((seq, D_MODEL), MXU_DTYPE),
        compiler_params=_params(dimension_semantics=("parallel",)),
    )(x, g, shift, scale)


def _in_proj(h, win_g, tm, name, after=None):
    seq = h.shape[0]

    def body(h_ref, w_ref, z_ref, w_pair):
        @pl.when(pl.program_id(1) == 0)
        def _():
            w_pair[...] = jnp.concatenate([w_ref[0], w_ref[1]], axis=1)

        z_ref[...] = jnp.dot(h_ref[...], w_pair[...], preferred_element_type=F32)

    return _pallas_after(
        body, 2, after, name=name, grid=(N_DEV // 2, seq // tm),
        in_specs=[pl.BlockSpec((tm, D_MODEL), lambda j, i: (i, 0)),
                  pl.BlockSpec((2, D_MODEL, IN_COLS), lambda j, i: (j, 0, 0))],
        out_specs=pl.BlockSpec((tm, 2 * IN_COLS), lambda j, i: (i, j)),
        out_shape=jax.ShapeDtypeStruct((seq, IN_WIDTH), F32),
        scratch_shapes=[pltpu.VMEM((D_MODEL, 2 * IN_COLS), MXU_DTYPE)],
        compiler_params=_params(dimension_semantics=("parallel", "arbitrary")),
    )(h, win_g)


def _shift_down(v, j, pos):
    return jnp.where(pos >= j, pltpu.roll(v, j, 0), 0.0)


def _shift_up(v, j, pos, seq):
    return jnp.where(pos < seq - j, pltpu.roll(v, seq - j, 0), 0.0)


def _select_window(g, candidates):
    out = candidates[-1]
    for i in range(len(candidates) - 2, -1, -1):
        out = jnp.where(g == i, candidates[i], out)
    return out


def _pool_mean_minus_token(u, g, pos):
    sums, acc = [], u
    for j in (1, 2, 4, 8):
        acc = acc + _shift_down(acc, j, pos)
        sums.append(acc)
    wsum = _select_window(g, sums)
    width = jnp.left_shift(2, g).astype(F32)
    count = jnp.minimum(pos.astype(F32) + 1.0, width)
    return wsum / count - u, count


def _pool_fwd(z, pool_w_l, pool_scale_l, name, after=None):
    seq = z.shape[0]

    def body(pv_ref, pg_ref, w_ref, sc_ref, out_ref):
        g = pl.program_id(0)
        pos = lax.broadcasted_iota(jnp.int32, (seq, GROUP_DIM), 0)
        pm, _ = _pool_mean_minus_token(pv_ref[...], g, pos)
        lin = _dot(pm, w_ref[...]) * sc_ref[...]
        pg = pg_ref[...]
        out_ref[...] = (lin * (pg * _sigmoid(pg))).astype(out_ref.dtype)

    return _pallas_after(
        body, 4, after, name=name, grid=(POOL_GROUPS,),
        in_specs=[pl.BlockSpec((seq, GROUP_DIM), lambda g: (0, g)),
                  pl.BlockSpec((seq, GROUP_DIM), lambda g: (0, POOL_GROUPS + g)),
                  pl.BlockSpec((None, GROUP_DIM, GROUP_DIM), lambda g: (g, 0, 0)),
                  pl.BlockSpec((1, GROUP_DIM), lambda g: (0, g))],
        out_specs=pl.BlockSpec((seq, GROUP_DIM), lambda g: (0, g)),
        out_shape=jax.ShapeDtypeStruct((seq, POOL_WIDTH), MXU_DTYPE),
        compiler_params=_params(dimension_semantics=("parallel",)),
    )(z, z, pool_w_l, pool_scale_l)


def _chunk_masks():
    row = lax.broadcasted_iota(jnp.int32, (CHUNK, CHUNK), 0)
    col = lax.broadcasted_iota(jnp.int32, (CHUNK, CHUNK), 1)
    causal = row >= col
    before_sub = col < (row // SUB) * SUB
    suffix = row <= col
    return causal, before_sub, suffix


def _masked_sums(masks, v):
    lhs = jnp.concatenate([m.astype(jnp.bfloat16) for m in masks], axis=0)
    hi = v.astype(jnp.bfloat16)
    rest = v - hi.astype(F32)
    mid = rest.astype(jnp.bfloat16)
    lo = (rest - mid.astype(F32)).astype(jnp.bfloat16)
    out = jnp.dot(lhs, hi, preferred_element_type=F32)
    out += jnp.dot(lhs, mid, preferred_element_type=F32)
    out += jnp.dot(lhs, lo, preferred_element_type=F32)
    return [out[i * CHUNK:(i + 1) * CHUNK] for i in range(len(masks))]


def _gates(zf, lb):
    sg = _sigmoid(zf)
    f = lb + (1.0 - lb) * sg
    logf = jnp.log(jnp.maximum(f, LOG_FLOOR))
    return sg, f, logf


def _intra_blocks(q_h, k_h, cum_h, base_h, causal):
    rel = cum_h - base_h
    out = []
    for i in range(N_SUB):
        rows = slice(i * SUB, (i + 1) * SUB)
        e_q = jnp.exp(rel[rows])
        base_i = jnp.concatenate([base_h[rows]] * N_SUB, axis=0)
        e_k = jnp.exp(jnp.minimum(base_i - cum_h, EXP_CLAMP))
        q_t = (q_h[rows] * e_q).astype(MXU_DTYPE)
        k_t = (k_h * e_k).astype(MXU_DTYPE)
        a_i = jnp.where(causal[rows], _dot_nt(q_t, k_t), 0.0)
        out.append((q_t, k_t, e_q, e_k, a_i))
    return out


def _hgrn_fwd(z, lb_l, gn_l, name, after=None):
    seq = z.shape[0]
    n_chunks = seq // CHUNK
    per_step = min(FWD_STEP_CHUNKS, n_chunks)
    rows_per_step = per_step * CHUNK

    def body(hq_ref, hf_ref, hi_ref, hg_ref, lb_ref, gn_ref, o_ref, bin_ref, st_ref, state):
        @pl.when(pl.program_id(0) == 0)
        def _():
            state[...] = jnp.zeros_like(state)

        causal, before_sub, _ = _chunk_masks()
        for cc in range(per_step):
            rows = slice(cc * CHUNK, (cc + 1) * CHUNK)
            _, f, logf = _gates(hf_ref[rows, :], lb_ref[...])
            kk = 1.0 - f
            hq = hq_ref[rows, :]
            q = hq * _sigmoid(hq)
            cum, base = _masked_sums([causal, before_sub], logf)
            st_ref[cc] = state[...]
            for h in range(HEADS):
                sl = slice(h * HEAD_DIM, (h + 1) * HEAD_DIM)
                q_h, k_h, cum_h = q[:, sl], kk[:, sl], cum[:, sl]
                v_h = hi_ref[rows, sl]
                st_h = state[h]
                blocks = _intra_blocks(q_h, k_h, cum_h, base[:, sl], causal)
                a = jnp.concatenate([b[4] for b in blocks], axis=0)
                o_h = _dot_nt(q_h * jnp.exp(cum_h), st_h) + _dot(a, v_h)
                last = jnp.sum(logf[:, sl], axis=0, keepdims=True)
                state[h] = st_h * jnp.exp(last) + _dot_tn(v_h, k_h * jnp.exp(last - cum_h))
                rs = lax.rsqrt(jnp.mean(o_h * o_h, axis=-1, keepdims=True) + NORM_EPS)
                hg = hg_ref[rows, sl]
                o_ref[rows, sl] = o_h
                bin_ref[rows, sl] = ((o_h * rs * gn_ref[...]) * (hg * _sigmoid(hg))).astype(bin_ref.dtype)

    def col(block):
        return pl.BlockSpec((rows_per_step, D_MODEL), lambda c: (c, block))

    tile = pl.BlockSpec((rows_per_step, D_MODEL), lambda c: (c, 0))
    return _pallas_after(
        body, 6, after, name=name, grid=(n_chunks // per_step,),
        in_specs=[col(COL_HQ), col(COL_HF), col(COL_HI), col(COL_HG), _row_spec(), _row_spec(HEAD_DIM)],
        out_specs=[tile, tile, pl.BlockSpec((per_step, HEADS, HEAD_DIM, HEAD_DIM), lambda c: (c, 0, 0, 0))],
        out_shape=[jax.ShapeDtypeStruct((seq, D_MODEL), F32),
                   jax.ShapeDtypeStruct((seq, D_MODEL), MXU_DTYPE),
                   jax.ShapeDtypeStruct((n_chunks, HEADS, HEAD_DIM, HEAD_DIM), F32)],
        scratch_shapes=[pltpu.VMEM((HEADS, HEAD_DIM, HEAD_DIM), F32)],
        compiler_params=_params(dimension_semantics=("arbitrary",)),
    )(z, z, z, z, lb_l, gn_l)


def _rms_parts(y):
    rs = lax.rsqrt(jnp.mean(y * y, axis=-1, keepdims=True) + NORM_EPS)
    return rs, y * rs


def _merge_fwd(a_in, b_in, z, x, wpo_g, who_g, wout_g, gate, g_post, tm, name, target=None):
    seq = x.shape[0]
    with_loss = target is not None

    def body(*refs):
        a_ref, b_ref, mgp_ref, mgh_ref, x_ref, wpo_ref, who_ref, wout_ref, gate_ref, gp_ref = refs[:10]
        ba_ref, bb_ref, mer_ref, y_ref, last_ref = refs[10 + with_loss:15 + with_loss]
        a = a_ref[...]
        ba = jnp.concatenate([_dot(a, wpo_ref[j]) for j in range(N_DEV)], axis=1)
        bb = _dot(b_ref[...], who_ref[...])
        merged = _sigmoid(mgp_ref[...]) * ba + _sigmoid(mgh_ref[...]) * bb
        y = _dot(merged, wout_ref[...])
        _, yn = _rms_parts(y)
        ba_ref[...] = ba.astype(ba_ref.dtype)
        bb_ref[...] = bb.astype(bb_ref.dtype)
        mer_ref[...] = merged.astype(mer_ref.dtype)
        y_ref[...] = y.astype(y_ref.dtype)
        x_next = x_ref[...] + gate_ref[...] * (yn * gp_ref[...])
        if not with_loss:
            last_ref[...] = x_next
            return
        loss_ref = refs[16]

        @pl.when(pl.program_id(0) == 0)
        def _():
            loss_ref[...] = jnp.zeros_like(loss_ref)

        err = x_next - refs[10][...]
        loss_ref[...] += 0.5 * jnp.sum(jnp.mean(err * err, axis=-1, keepdims=True), axis=0, keepdims=True)
        last_ref[...] = err * (1.0 / D_MODEL)

    def tile(cols=D_MODEL, block=0):
        return pl.BlockSpec((tm, cols), lambda i: (i, block))

    full = pl.BlockSpec((D_MODEL, D_MODEL), lambda i: (0, 0))
    act = jax.ShapeDtypeStruct((seq, D_MODEL), MXU_DTYPE)
    f32 = jax.ShapeDtypeStruct((seq, D_MODEL), F32)
    one = [pl.BlockSpec((1, 1), lambda i: (0, 0))] if with_loss else []
    return _pallas_after(
        body, 10 + with_loss, None, name=name, grid=(seq // tm,),
        in_specs=[tile(POOL_WIDTH), tile(), tile(block=COL_MGP), tile(block=COL_MGH), tile(),
                  pl.BlockSpec((N_DEV, POOL_WIDTH, GROUP_DIM), lambda i: (0, 0, 0)),
                  full, full, _row_spec(), _row_spec()] + ([tile()] if with_loss else []),
        out_specs=[tile(), tile(), tile(), tile(), tile()] + one,
        out_shape=[act, act, act, act, f32] + ([jax.ShapeDtypeStruct((1, 1), F32)] if with_loss else []),
        compiler_params=_params(dimension_semantics=("arbitrary" if with_loss else "parallel",)),
    )(a_in, b_in, z, z, x, wpo_g, who_g, wout_g, gate, g_post, *([target] if with_loss else []))


def _stage_copy(stage, sems, dst, slot, step, where):
    rows, cols = where(step)
    return pltpu.make_async_copy(stage.at[slot], dst.at[rows, cols], sems.at[slot])


def _stage_begin(stage, sems, dst, step, where):
    slot = step % 2

    @pl.when(step >= 2)
    def _():
        _stage_copy(stage, sems, dst, slot, step - 2, where).wait()

    return slot


def _stage_end(stage, sems, dst, step, n_steps, where):
    slot = step % 2
    _stage_copy(stage, sems, dst, slot, step, where).start()

    @pl.when(step == n_steps - 1)
    def _():
        _stage_copy(stage, sems, dst, slot, step, where).wait()
        if n_steps > 1:
            _stage_copy(stage, sems, dst, 1 - slot, step - 1, where).wait()


def _merge_bwd(dx, y, ba, bb, z, wpo_g, who_g, wout_g, gate, g_post, dz, tm, name):
    seq = dx.shape[0]
    n_steps = seq // tm

    def body(dx_ref, y_ref, ba_ref, bb_ref, mgp_ref, mgh_ref, wpo_ref, who_ref, wout_ref, gate_ref, gp_ref, _,
             dy_ref, dba_ref, dbb_ref, da_ref, db_ref, dz_ref, acc_ref, stage, sems):
        step = pl.program_id(0)

        @pl.when(step == 0)
        def _():
            acc_ref[...] = jnp.zeros_like(acc_ref)

        def where(t):
            return pl.ds(t * tm, tm), pl.ds(COL_MGP * D_MODEL, 2 * D_MODEL)

        dmg_ref = stage.at[_stage_begin(stage, sems, dz_ref, step, where)]

        dxv = dx_ref[...]
        rs, yn = _rms_parts(y_ref[...].astype(F32))
        acc_ref[0:1, :] += jnp.sum(dxv * yn * gp_ref[...], axis=0, keepdims=True)
        acc_ref[1:2, :] += jnp.sum(dxv * gate_ref[...] * yn, axis=0, keepdims=True)
        dyn = dxv * (gate_ref[...] * gp_ref[...])
        dy = rs * (dyn - yn * jnp.mean(dyn * yn, axis=-1, keepdims=True))
        dmerged = _dot_nt(dy, wout_ref[...])
        sp, sh = _sigmoid(mgp_ref[...]), _sigmoid(mgh_ref[...])
        dba, dbb = sp * dmerged, sh * dmerged
        dmg_ref[:, 0:D_MODEL] = (dmerged * ba_ref[...].astype(F32) * sp * (1.0 - sp)).astype(dmg_ref.dtype)
        dmg_ref[:, D_MODEL:2 * D_MODEL] = (dmerged * bb_ref[...].astype(F32) * sh * (1.0 - sh)).astype(dmg_ref.dtype)
        da = _dot_nt(dba[:, 0:GROUP_DIM], wpo_ref[0])
        for j in range(1, N_DEV):
            da += _dot_nt(dba[:, j * GROUP_DIM:(j + 1) * GROUP_DIM], wpo_ref[j])
        dy_ref[...] = dy.astype(dy_ref.dtype)
        dba_ref[...] = dba.astype(dba_ref.dtype)
        dbb_ref[...] = dbb.astype(dbb_ref.dtype)
        da_ref[...] = da.astype(da_ref.dtype)
        db_ref[...] = _dot_nt(dbb, who_ref[...]).astype(db_ref.dtype)
        _stage_end(stage, sems, dz_ref, step, n_steps, where)

    def tile(cols=D_MODEL, block=0):
        return pl.BlockSpec((tm, cols), lambda i: (i, block))

    full = pl.BlockSpec((D_MODEL, D_MODEL), lambda i: (0, 0))
    hbm = pl.BlockSpec(memory_space=pl.ANY)
    act = jax.ShapeDtypeStruct((seq, D_MODEL), MXU_DTYPE)
    return _pallas_after(
        body, 12, None, name=name, grid=(n_steps,),
        in_specs=[tile(), tile(), tile(), tile(), tile(block=COL_MGP), tile(block=COL_MGH),
                  pl.BlockSpec((N_DEV, POOL_WIDTH, GROUP_DIM), lambda i: (0, 0, 0)),
                  full, full, _row_spec(), _row_spec(), hbm],
        out_specs=[tile(), tile(), tile(), tile(POOL_WIDTH), tile(), hbm,
                   pl.BlockSpec((8, D_MODEL), lambda i: (0, 0))],
        out_shape=[act, act, act, jax.ShapeDtypeStruct((seq, POOL_WIDTH), MXU_DTYPE), act,
                   jax.ShapeDtypeStruct(dz.shape, dz.dtype),
                   jax.ShapeDtypeStruct((8, D_MODEL), F32)],
        input_output_aliases={11: 5},
        scratch_shapes=[pltpu.VMEM((2, tm, 2 * D_MODEL), MXU_DTYPE), pltpu.SemaphoreType.DMA((2,))],
        compiler_params=_params(dimension_semantics=("arbitrary",)),
    )(dx, y, ba, bb, z, z, wpo_g, who_g, wout_g, gate, g_post, dz)


def _grad_out_weights(merged, dy, b_in, dbb, a_in, dba, name):
    seq = merged.shape[0]
    tn = D_MODEL // 2
    per_step = tn // GROUP_DIM

    def body(mer_ref, dy_ref, b_ref, dbb_ref, a_ref, dba_ref, gout_ref, gho_ref, gpo_ref):
        gout_ref[...] = _dot_tn(mer_ref[...], dy_ref[...]).astype(gout_ref.dtype)
        gho_ref[...] = _dot_tn(b_ref[...], dbb_ref[...]).astype(gho_ref.dtype)
        g_po = _dot_tn(a_ref[...], dba_ref[...])
        for j in range(per_step):
            gpo_ref[j] = g_po[:, j * GROUP_DIM:(j + 1) * GROUP_DIM].astype(gpo_ref.dtype)

    def whole(cols):
        return pl.BlockSpec((seq, cols), lambda j: (0, 0))

    cols = pl.BlockSpec((seq, tn), lambda j: (0, j))
    return _pallas_after(
        body, 6, None, name=name, grid=(D_MODEL // tn,),
        in_specs=[whole(D_MODEL), cols, whole(D_MODEL), cols, whole(POOL_WIDTH), cols],
        out_specs=[pl.BlockSpec((D_MODEL, tn), lambda j: (0, j)), pl.BlockSpec((D_MODEL, tn), lambda j: (0, j)),
                   pl.BlockSpec((per_step, POOL_WIDTH, GROUP_DIM), lambda j: (j, 0, 0))],
        out_shape=[jax.ShapeDtypeStruct((D_MODEL, D_MODEL), WIRE_DTYPE),
                   jax.ShapeDtypeStruct((D_MODEL, D_MODEL), WIRE_DTYPE),
                   jax.ShapeDtypeStruct((N_DEV, POOL_WIDTH, GROUP_DIM), WIRE_DTYPE)],
        compiler_params=_params(dimension_semantics=("parallel",)),
    )(merged, dy, b_in, dbb, a_in, dba)


def _hgrn_bwd(db_in, z, o, states, lb_l, gn_l, dz, name, after=None):
    seq = z.shape[0]
    per_step = min(BWD_STEP_CHUNKS, seq // CHUNK)
    rows_per_step = per_step * CHUNK
    n_steps = seq // rows_per_step
    last_step = n_steps - 1

    def body(db_ref, hq_ref, hf_ref, hi_ref, hg_ref, o_ref, st_ref, lb_ref, gn_ref, _,
             dz_hbm, dlb_ref, dgn_ref, dstate, dq_buf, dk_buf, dg_buf, stage, sems):
        step = pl.program_id(0)

        @pl.when(step == 0)
        def _():
            dstate[...] = jnp.zeros_like(dstate)
            dlb_ref[...] = jnp.zeros_like(dlb_ref)
            dgn_ref[...] = jnp.zeros_like(dgn_ref)

        def one_chunk(cc, *args):
            one_chunk_body((db_ref, hq_ref, hf_ref, hi_ref, hg_ref, o_ref, st_ref, dlb_ref, dgn_ref, dstate,
                            dq_buf, dk_buf, dg_buf), cc, *args)

        def where(t):
            return pl.ds((last_step - t) * rows_per_step, rows_per_step), pl.ds(COL_HQ * D_MODEL, 4 * D_MODEL)

        dz_step = stage.at[_stage_begin(stage, sems, dz_hbm, step, where)]
        causal, before_sub, suffix = _chunk_masks()
        lb = lb_ref[...]
        gn = gn_ref[...]
        for cc in reversed(range(per_step)):
            one_chunk(cc, dz_step, causal, before_sub, suffix, lb, gn)
        _stage_end(stage, sems, dz_hbm, step, n_steps, where)

    def one_chunk_body(refs, cc, dz_step, causal, before_sub, suffix, lb, gn):
        (db_ref, hq_ref, hf_ref, hi_ref, hg_ref, o_ref, st_ref, dlb_ref, dgn_ref, dstate, dq_buf, dk_buf, dg_buf) = refs
        rows = slice(cc * CHUNK, (cc + 1) * CHUNK)
        dz_ref = dz_step.at[rows, :]
        dq_buf, dk_buf, dg_buf = dq_buf.at[cc], dk_buf.at[cc], dg_buf.at[cc]
        sg, f, logf = _gates(hf_ref[rows, :], lb)
        kk = 1.0 - f
        hq = hq_ref[rows, :]
        sq = _sigmoid(hq)
        q = hq * sq
        cum, base = _masked_sums([causal, before_sub], logf)
        dgn = jnp.zeros((1, HEAD_DIM), F32)
        dlast = []
        for h in range(HEADS):
            sl = slice(h * HEAD_DIM, (h + 1) * HEAD_DIM)
            q_h, k_h, cum_h = q[:, sl], kk[:, sl], cum[:, sl]
            v_h = hi_ref[rows, sl]
            st_h = st_ref[cc, h]
            dst_h = dstate[h]
            rs, ohat = _rms_parts(o_ref[rows, sl])
            hg = hg_ref[rows, sl]
            shg = _sigmoid(hg)
            d_bin = db_ref[rows, sl].astype(F32)
            don = d_bin * (hg * shg)
            dgn += jnp.sum(don * ohat, axis=0, keepdims=True)
            dohat = don * gn
            do = rs * (dohat - ohat * jnp.mean(dohat * ohat, axis=-1, keepdims=True))
            dz_ref[:, 3 * D_MODEL + h * HEAD_DIM:3 * D_MODEL + (h + 1) * HEAD_DIM] = (
                d_bin * (ohat * gn) * _dsilu(hg, shg)).astype(dz_ref.dtype)
            last = jnp.sum(logf[:, sl], axis=0, keepdims=True)
            g_in = jnp.exp(cum_h)
            d_out = jnp.exp(last - cum_h)
            q_bar, k_bar = q_h * g_in, k_h * d_out
            blocks = _intra_blocks(q_h, k_h, cum_h, base[:, sl], causal)
            a = jnp.concatenate([b[4] for b in blocks], axis=0)
            da = jnp.where(causal, _dot_nt(do, v_h), 0.0)
            dv = _dot_tn(a, do) + _dot_nt(k_bar, dst_h)
            dq_bar, dk_bar = _dot(do, st_h), _dot(v_h, dst_h)
            dk = dk_bar * d_out
            dq_parts, dg_parts = [], []
            dg_k = k_bar * dk_bar
            dlast.append(jnp.sum(k_bar * dk_bar, axis=0, keepdims=True)
                         + jnp.exp(last) * jnp.sum(st_h * dst_h, axis=0, keepdims=True))
            for i, (q_t, k_t, e_q, e_k, _) in enumerate(blocks):
                da_i = da[i * SUB:(i + 1) * SUB].astype(MXU_DTYPE)
                dq_t = _dot(da_i, k_t)
                dk_t = _dot_tn(da_i, q_t)
                dq_parts.append(dq_t * e_q)
                dk += dk_t * e_k
                dg_parts.append(q_t.astype(F32) * dq_t)
                dg_k += k_t.astype(F32) * dk_t
            dq = dq_bar * g_in + jnp.concatenate(dq_parts, axis=0)
            dg_buf[:, sl] = q_bar * dq_bar + jnp.concatenate(dg_parts, axis=0) - dg_k
            dstate[h] = dst_h * jnp.exp(last) + _dot_tn(do, q_bar)
            dq_buf[:, sl] = dq
            dk_buf[:, sl] = dk
            dz_ref[:, 2 * D_MODEL + h * HEAD_DIM:2 * D_MODEL + (h + 1) * HEAD_DIM] = dv.astype(dz_ref.dtype)
        dgn_ref[...] += dgn
        dq_all, dk_all = dq_buf[...], dk_buf[...]
        dlogf = _masked_sums([suffix], dg_buf[...])[0] + jnp.concatenate(dlast, axis=1)
        df = jnp.where(f > LOG_FLOOR, dlogf / f, 0.0) - dk_all
        dlb_ref[...] += jnp.sum(df * (1.0 - sg), axis=0, keepdims=True)
        dz_ref[:, 0:D_MODEL] = (dq_all * _dsilu(hq, sq)).astype(dz_ref.dtype)
        dz_ref[:, D_MODEL:2 * D_MODEL] = (df * (1.0 - lb) * sg * (1.0 - sg)).astype(dz_ref.dtype)

    def col(block):
        return pl.BlockSpec((rows_per_step, D_MODEL), lambda c: (last_step - c, block))

    hbm = pl.BlockSpec(memory_space=pl.ANY)
    return _pallas_after(
        body, 10, after, name=name, grid=(n_steps,),
        in_specs=[col(0), col(COL_HQ), col(COL_HF), col(COL_HI), col(COL_HG), col(0),
                  pl.BlockSpec((per_step, HEADS, HEAD_DIM, HEAD_DIM), lambda c: (last_step - c, 0, 0, 0)),
                  _row_spec(), _row_spec(HEAD_DIM), hbm],
        out_specs=[hbm, _row_spec(), _row_spec(HEAD_DIM)],
        out_shape=[jax.ShapeDtypeStruct(dz.shape, dz.dtype),
                   jax.ShapeDtypeStruct((1, D_MODEL), F32), jax.ShapeDtypeStruct((1, HEAD_DIM), F32)],
        input_output_aliases={9: 0},
        scratch_shapes=[pltpu.VMEM((HEADS, HEAD_DIM, HEAD_DIM), F32)]
        + [pltpu.VMEM((per_step, CHUNK, D_MODEL), F32)] * 3
        + [pltpu.VMEM((2, rows_per_step, 4 * D_MODEL), MXU_DTYPE), pltpu.SemaphoreType.DMA((2,))],
        compiler_params=_params(dimension_semantics=("arbitrary",)),
    )(db_in, z, z, z, z, o, states, lb_l, gn_l, dz)


def _pool_bwd(da_in, z, pool_w_l, pool_scale_l, dz, name, after=None):
    seq = z.shape[0]

    def body(da_ref, pv_ref, pg_ref, w_ref, sc_ref, _, dz_hbm, dw_ref, dsc_ref, stage_pv, stage_pg, sems_pv, sems_pg):
        g = pl.program_id(0)

        def where_pv(t):
            return pl.ds(0, seq), pl.ds(pl.multiple_of(t * GROUP_DIM, GROUP_DIM), GROUP_DIM)

        def where_pg(t):
            return pl.ds(0, seq), pl.ds(pl.multiple_of(POOL_WIDTH + t * GROUP_DIM, GROUP_DIM), GROUP_DIM)

        dpv_ref = stage_pv.at[_stage_begin(stage_pv, sems_pv, dz_hbm, g, where_pv)]
        dpg_ref = stage_pg.at[_stage_begin(stage_pg, sems_pg, dz_hbm, g, where_pg)]
        pos = lax.broadcasted_iota(jnp.int32, (seq, GROUP_DIM), 0)
        pm, count = _pool_mean_minus_token(pv_ref[...], g, pos)
        lin0 = _dot(pm, w_ref[...])
        pg = pg_ref[...]
        spg = _sigmoid(pg)
        da = da_ref[...].astype(F32)
        dlin = da * (pg * spg)
        dpg_ref[...] = (da * (lin0 * sc_ref[...]) * _dsilu(pg, spg)).astype(dpg_ref.dtype)
        dsc_ref[...] = jnp.sum(dlin * lin0, axis=0, keepdims=True)
        dl0 = dlin * sc_ref[...]
        dw_ref[...] = _dot_tn(pm, dl0)
        dpm = _dot_nt(dl0, w_ref[...])
        sums, acc = [], dpm / count
        for j in (1, 2, 4, 8):
            acc = acc + _shift_up(acc, j, pos, seq)
            sums.append(acc)
        dpv_ref[...] = (_select_window(g, sums) - dpm).astype(dpv_ref.dtype)
        _stage_end(stage_pv, sems_pv, dz_hbm, g, POOL_GROUPS, where_pv)
        _stage_end(stage_pg, sems_pg, dz_hbm, g, POOL_GROUPS, where_pg)

    grp = pl.BlockSpec((seq, GROUP_DIM), lambda g: (0, g))
    hbm = pl.BlockSpec(memory_space=pl.ANY)
    stage = pltpu.VMEM((2, seq, GROUP_DIM), MXU_DTYPE)
    return _pallas_after(
        body, 6, after, name=name, grid=(POOL_GROUPS,),
        in_specs=[grp, grp, pl.BlockSpec((seq, GROUP_DIM), lambda g: (0, POOL_GROUPS + g)),
                  pl.BlockSpec((None, GROUP_DIM, GROUP_DIM), lambda g: (g, 0, 0)),
                  pl.BlockSpec((1, GROUP_DIM), lambda g: (0, g)), hbm],
        out_specs=[hbm, pl.BlockSpec((None, GROUP_DIM, GROUP_DIM), lambda g: (g, 0, 0)),
                   pl.BlockSpec((1, GROUP_DIM), lambda g: (0, g))],
        out_shape=[jax.ShapeDtypeStruct(dz.shape, dz.dtype),
                   jax.ShapeDtypeStruct((POOL_GROUPS, GROUP_DIM, GROUP_DIM), F32),
                   jax.ShapeDtypeStruct((1, POOL_WIDTH), F32)],
        input_output_aliases={5: 0},
        scratch_shapes=[stage, stage, pltpu.SemaphoreType.DMA((2,)), pltpu.SemaphoreType.DMA((2,))],
        compiler_params=_params(dimension_semantics=("arbitrary",)),
    )(da_in, z, z, pool_w_l, pool_scale_l, dz)


def _in_proj_dw(h, dz, name, after=None):
    seq = h.shape[0]

    def body(h_ref, dz_ref, out_ref):
        pair = lax.dot_general(h_ref[...], dz_ref[...], (((0,), (0,)), ((), ())), preferred_element_type=F32)
        out_ref[0] = pair[:, 0:IN_COLS].astype(out_ref.dtype)
        out_ref[1] = pair[:, IN_COLS:].astype(out_ref.dtype)

    return _pallas_after(
        body, 2, after, name=name, grid=(N_DEV // 2,),
        in_specs=[pl.BlockSpec((seq, D_MODEL), lambda j: (0, 0)),
                  pl.BlockSpec((seq, 2 * IN_COLS), lambda j: (0, j))],
        out_specs=pl.BlockSpec((2, D_MODEL, IN_COLS), lambda j: (j, 0, 0)),
        out_shape=jax.ShapeDtypeStruct((N_DEV, D_MODEL, IN_COLS), WIRE_DTYPE),
        compiler_params=_params(dimension_semantics=("parallel",)),
    )(h, dz)


def _in_proj_dh(dz, win_g, tm, name, after=None):
    seq = dz.shape[0]

    def body(dz_ref, w_ref, dh_ref):
        @pl.when(pl.program_id(1) == 0)
        def _():
            dh_ref[...] = jnp.zeros_like(dh_ref)

        w_pair = jnp.concatenate([w_ref[0], w_ref[1]], axis=1)
        dh_ref[...] += lax.dot_general(dz_ref[...], w_pair, (((1,), (1,)), ((), ())), preferred_element_type=F32)

    return _pallas_after(
        body, 2, after, name=name, grid=(seq // tm, N_DEV // 2),
        in_specs=[pl.BlockSpec((tm, 2 * IN_COLS), lambda i, j: (i, j)),
                  pl.BlockSpec((2, D_MODEL, IN_COLS), lambda i, j: (j, 0, 0))],
        out_specs=pl.BlockSpec((tm, D_MODEL), lambda i, j: (i, 0)),
        out_shape=jax.ShapeDtypeStruct((seq, D_MODEL), F32),
        compiler_params=_params(dimension_semantics=("parallel", "arbitrary")),
    )(dz, win_g)


def _prenorm_bwd(x, dh, dx_res, g, scale, tm, name, after=None):
    seq = x.shape[0]

    def body(x_ref, dh_ref, dxr_ref, g_ref, sc_ref, dx_ref, acc_ref):
        @pl.when(pl.program_id(0) == 0)
        def _():
            acc_ref[...] = jnp.zeros_like(acc_ref)

        rs, xn = _rms_parts(x_ref[...])
        dh = dh_ref[...]
        acc_ref[0:1, :] += jnp.sum(dh, axis=0, keepdims=True)
        acc_ref[1:2, :] += jnp.sum(dh * (xn * g_ref[...]), axis=0, keepdims=True)
        dhn = dh * (1.0 + sc_ref[...])
        acc_ref[2:3, :] += jnp.sum(dhn * xn, axis=0, keepdims=True)
        dxn = dhn * g_ref[...]
        dx_ref[...] = rs * (dxn - xn * jnp.mean(dxn * xn, axis=-1, keepdims=True)) + dxr_ref[...]

    tile = pl.BlockSpec((tm, D_MODEL), lambda i: (i, 0))
    return _pallas_after(
        body, 5, after, name=name, grid=(seq // tm,),
        in_specs=[tile, tile, tile, _row_spec(), _row_spec()],
        out_specs=[tile, pl.BlockSpec((8, D_MODEL), lambda i: (0, 0))],
        out_shape=[jax.ShapeDtypeStruct((seq, D_MODEL), F32), jax.ShapeDtypeStruct((8, D_MODEL), F32)],
        compiler_params=_params(dimension_semantics=("arbitrary",)),
    )(x, dh, dx_res, g, scale)


def _adamw_math(w, g, m, v):
    m = ADAM_B1 * m + (1.0 - ADAM_B1) * g
    v = ADAM_B2 * v + (1.0 - ADAM_B2) * (g * g)
    m_hat = m / (1.0 - ADAM_B1 ** ADAM_STEP)
    v_hat = v / (1.0 - ADAM_B2 ** ADAM_STEP)
    delta = -ADAM_LR * (m_hat / (jnp.sqrt(v_hat) + ADAM_EPS) + ADAM_WD * w)
    return delta, m, v


def _adamw_sharded(w, m, v, contrib, tr, name):
    depth, rows, cols = w.shape
    n_parts = contrib.shape[1]

    def body(w_ref, m_ref, v_ref, c_ref, g_ref, d_ref, mo_ref, vo_ref):
        g = c_ref[0].astype(F32)
        for p in range(1, n_parts):
            g += c_ref[p].astype(F32)
        delta, mn, vn = _adamw_math(w_ref[...], g, m_ref[...], v_ref[...])
        g_ref[...] = g
        d_ref[...] = delta
        mo_ref[...] = mn
        vo_ref[...] = vn

    tile = pl.BlockSpec((None, tr, cols), lambda l, i: (l, i, 0))
    shape = jax.ShapeDtypeStruct(w.shape, F32)
    return _pallas_after(
        body, 4, None, name=name, grid=(depth, rows // tr),
        in_specs=[tile, tile, tile, pl.BlockSpec((None, n_parts, tr, cols), lambda l, i: (l, 0, i, 0))],
        out_specs=[tile] * 4, out_shape=[shape] * 4,
        compiler_params=_params(dimension_semantics=("parallel", "parallel")),
    )(w, m, v, contrib)


def _adamw_layer(w, m, v, contribs, l, tr, name, prev=None):
    _, rows, cols = w.shape
    n = len(contribs)

    def body(*refs):
        w_ref, m_ref, v_ref = refs[:3]
        c_refs = refs[3:3 + n]
        g_ref, d_ref, mo_ref, vo_ref = refs[-4:]
        g = c_refs[0][...].astype(F32)
        for c_ref in c_refs[1:]:
            g += c_ref[...].astype(F32)
        delta, mn, vn = _adamw_math(w_ref[...], g, m_ref[...], v_ref[...])
        g_ref[...] = g
        d_ref[...] = delta
        mo_ref[...] = mn
        vo_ref[...] = vn

    tile = pl.BlockSpec((None, tr, cols), lambda i: (l, i, 0))
    in_specs = [tile, tile, tile] + [pl.BlockSpec((None, tr, cols), lambda i, s=slot: (s, i, 0)) for _, slot in contribs]
    operands = [w, m, v] + [arr for arr, _ in contribs]
    aliases = {}
    if prev is not None:
        aliases = {len(operands) + k: k for k in range(4)}
        in_specs += [pl.BlockSpec(memory_space=pl.ANY)] * 4
        operands += list(prev)
    shape = jax.ShapeDtypeStruct(w.shape, F32)
    return _pallas_after(
        body, len(in_specs), None, name=name, grid=(rows // tr,), in_specs=in_specs, out_specs=[tile] * 4,
        out_shape=[shape] * 4, input_output_aliases=aliases,
        compiler_params=_params(dimension_semantics=("parallel",)),
    )(*operands)


def _adamw_small(w_pack, m_pack, v_pack, g_late, g_early, shapes):
    pieces, r = {}, 0
    for name, _, n in _SMALL_ROWS:
        pieces.setdefault(name, []).append((r, n))
        r += n
    names = list(pieces)

    def body(w_ref, m_ref, v_ref, gl_ref, ge_ref, *rest):
        outs, packs = rest[:4 * len(names)], rest[4 * len(names):]
        g_l, g_e = gl_ref[0][0:SMALL_LATE_ROWS], ge_ref[0]
        for d in range(1, N_DEV):
            g_l += gl_ref[d][0:SMALL_LATE_ROWS]
            g_e += ge_ref[d]
        g = jnp.concatenate([g_l, g_e], axis=0)
        w = w_ref[...]
        r0, r1, r2 = LB_ROW0, LB_ROW0 + 8, LB_ROW0 + 16
        lg0, lg1 = w[r0:r1], w[r1:r2]
        mx = jnp.maximum(lg0, lg1)
        e0, e1 = jnp.exp(lg0 - mx), jnp.exp(lg1 - mx)
        p0, p1 = e0 / (e0 + e1), e1 / (e0 + e1)
        low = ((p0 - p0), (p0 + p1) - p0)
        dlow = [g_rows * jnp.where((lo > 0.0) & (lo < 1.0), 1.0, jnp.where((lo == 0.0) | (lo == 1.0), 0.5, 0.0))
                for g_rows, lo in ((g[r0:r1], low[0]), (g[r1:r2], low[1]))]
        dp0 = (dlow[0] + dlow[1]) - (dlow[0] + dlow[1])
        dp1 = dlow[1]
        inner = p0 * dp0 + p1 * dp1
        g = jnp.concatenate([g[:r0], p0 * (dp0 - inner), p1 * (dp1 - inner), g[r2:]], axis=0)
        delta, mn, vn = _adamw_math(w, g, m_ref[...], v_ref[...])
        for kind, val in enumerate((g, delta, mn, vn)):
            packs[kind][...] = val
            for j, name in enumerate(names):
                out, at = outs[kind * len(names) + j], 0
                for start, n in pieces[name]:
                    if name in flat:
                        for r in range(n):
                            layer, c = divmod(at + r, flat[name])
                            out[layer:layer + 1, c * 128:(c + 1) * 128] = packs[kind][start + r:start + r + 1, :]
                    else:
                        out[at:at + n, :] = packs[kind][start:start + n, :]
                    at += n

    rows = {name: sum(n for _, n in pieces[name]) for name in names}
    flat = {name: rows[name] // DEPTH for name in names if len(shapes[name]) == 2}
    outs = pl.pallas_call(
        body, name="adamw_small",
        out_shape=[jax.ShapeDtypeStruct(shapes[name] if name in flat else (rows[name], 128), F32)
                   for _ in range(4) for name in names],
        scratch_shapes=[pltpu.VMEM(w_pack.shape, F32)] * 4, compiler_params=_params(),
    )(w_pack, m_pack, v_pack, g_late, g_early)
    return [{name: outs[kind * len(names) + j].reshape(shapes[name]) for j, name in enumerate(names)}
            for kind in range(4)]


def _pack_small(parts, first=0, last=len(_SMALL_ROWS)):
    rows = [(parts[name] if l is None else parts[name][l]).reshape(n, 128) for name, l, n in _SMALL_ROWS[first:last]]
    if last == len(_SMALL_ROWS):
        rows.append(jnp.zeros((SMALL_ROWS_PAD - sum(n for _, _, n in _SMALL_ROWS), 128), F32))
    return jnp.concatenate(rows, axis=0)


def kernel(x, c, w_ada, b_ada, g_pre, g_post, w_in, pool_w, pool_scale, lb_logits, hgrn_norm_g, w_pool_o, w_hgrn_o, w_out, loss_target, m_w_ada, m_b_ada, m_g_pre, m_g_post, m_w_in, m_pool_w, m_pool_scale, m_lb_logits, m_hgrn_norm_g, m_w_pool_o, m_w_hgrn_o, m_w_out, v_w_ada, v_b_ada, v_g_pre, v_g_post, v_w_in, v_pool_w, v_pool_scale, v_lb_logits, v_hgrn_norm_g, v_w_pool_o, v_w_hgrn_o, v_w_out):
    seq = x.shape[1]
    tm = min(512, seq)
    tm_merge = min(256, seq)
    pos = _my_position()
    me = pos[3]

    c_all = _allgather_small(c, "allgather_c").reshape(N_DEV, D_MODEL)
    b_cols = lax.dynamic_slice_in_dim(b_ada, me * ADA_COLS, ADA_COLS, axis=1)
    ada_part = _ada_fwd(c_all, w_ada, b_cols)
    ada_all = _allgather_small(ada_part.reshape(DEPTH * N_DEV, ADA_COLS), "allgather_ada")
    ada = lax.dynamic_index_in_dim(ada_all.reshape(N_DEV, DEPTH, N_DEV, ADA_COLS), me, axis=2, keepdims=False)
    ada = jnp.transpose(ada, (1, 0, 2)).reshape(DEPTH, 3 * D_MODEL)
    shift = [ada[l:l + 1, 0:D_MODEL] for l in range(DEPTH)]
    scale = [ada[l:l + 1, D_MODEL:2 * D_MODEL] for l in range(DEPTH)]
    gate = [ada[l:l + 1, 2 * D_MODEL:] for l in range(DEPTH)]

    big = dict(win=w_in, wpo=w_pool_o, who=w_hgrn_o, wout=w_out)
    units = [["win0"], ["wpo0", "who0", "wout0"], ["win1", "wpo1", "who1", "wout1"]]
    g_streams = [_gather_streams(keys) for keys in units]
    g_state = [None] * len(units)

    def gather_start(us, after):
        bufs = {}
        for k in [k for u in us for k in units[u]]:
            arr = big[k[:-1]]
            bufs["s_" + k] = arr[int(k[-1])].astype(WIRE_DTYPE)
            bufs["g_" + k] = _with_own_slot(bufs["s_" + k], me)
        bufs, sems, token = _comm_call("gather_start_" + "_".join(map(str, us)), bufs,
                                       start=[s for u in us for s in g_streams[u][:2]], after=after)
        for n, u in enumerate(us):
            g_state[u] = dict(bufs={p + k: bufs[p + k] for k in units[u] for p in ("s_", "g_")},
                              sems=sems[2 * n:2 * n + 2])
        return token

    def gather_pass(u, after):
        st = g_state[u]
        to_chips, _, pass_on = g_streams[u]
        st["bufs"], (st["pass_sems"],), _ = _comm_call(f"gather_pass_{u}", st["bufs"], start=[pass_on],
                                                       wait=[(to_chips, st["sems"][0])], after=after)

    def gather_done(u, after=None):
        st = g_state[u]
        _, to_sibling, pass_on = g_streams[u]
        bufs, _, _ = _comm_call(f"gather_done_{u}", st["bufs"], after=after,
                                wait=[(to_sibling, st["sems"][1]), (pass_on, st["pass_sems"])])
        return {k: bufs["g_" + k] for k in units[u]}

    token = gather_start([0], ada_all)

    lb = _lb_fwd(lb_logits)

    gw = {}
    xs, saved = [x[0]], []
    for l in range(DEPTH):
        h = _prenorm_fwd(xs[l], g_pre[l:l + 1], shift[l], scale[l], tm, f"prenorm_fwd_{l}",
                         after=token if l == 0 else None)
        token = None
        if l == 0:
            gather_pass(0, h)
            gw.update(gather_done(0))
            token = gather_start([1, 2], gw["win0"])
        else:
            gw.update(gather_done(2, h))
        z = _in_proj(h, gw[f"win{l}"], min(1024, seq), f"in_proj_{l}", after=token)
        a_in = _pool_fwd(z, pool_w[l], pool_scale[l:l + 1], f"pool_fwd_{l}")
        o, b_in, states = _hgrn_fwd(z, lb[l:l + 1], hgrn_norm_g[l:l + 1], f"hgrn_fwd_{l}")
        if l == 0:
            gather_pass(1, b_in)
            gw.update(gather_done(1))
        who_l = gw[f"who{l}"].reshape(D_MODEL, D_MODEL)
        wout_l = gw[f"wout{l}"].reshape(D_MODEL, D_MODEL)
        last = l == DEPTH - 1
        ba, bb, merged, y, *out = _merge_fwd(a_in, b_in, z, xs[l], gw[f"wpo{l}"], who_l, wout_l, gate[l],
                                             g_post[l:l + 1], tm_merge, f"merge_fwd_{l}",
                                             target=loss_target[0] if last else None)
        if last:
            dx, loss_part = out
        else:
            xs.append(out[0])
            gather_pass(2, out[0])
        saved.append((h, z, a_in, o, b_in, states, ba, bb, merged, y, who_l, wout_l))


    chips = _other_chips(pos)
    pair_idx = jnp.stack([_dev_index(cx, cy, pos[2]) for cx, cy in chips] + [me]).astype(jnp.int32)
    pair_rows = dict(win=256, wpo=POOL_WIDTH, who=HEAD_DIM, wout=HEAD_DIM)

    def scatter_pair_start(u, grads):
        keys = list(grads)
        pair, to_chips = _scatter_streams(keys)
        bufs = {}
        for k in keys:
            bufs["g_" + k] = grads[k]
            bufs["st_" + k] = lax.empty((4,) + grads[k].shape[1:], WIRE_DTYPE)
        bufs, (sems,), token = _comm_call(f"scatter_pair_start_{u}", bufs, start=[pair])
        return dict(u=u, keys=keys, pair=pair, to_chips=to_chips, bufs=bufs, sems=sems, token=token)

    def scatter_pair_finish(st, after):
        u, keys = st["u"], st["keys"]
        bufs, _, _ = _comm_call(f"scatter_pair_done_{u}", st["bufs"], wait=[(st["pair"], st["sems"])], after=after)
        bufs2 = {}
        for k in keys:
            bufs2["ps_" + k] = _pair_sum(bufs["g_" + k], bufs["st_" + k], pair_idx, bufs["g_" + k].shape[1],
                                         f"pair_sum_{k}")
            bufs2["ld_" + k] = lax.empty((3,) + bufs["g_" + k].shape[1:], WIRE_DTYPE)
        st.update(bufs=bufs2)

    def scatter_chips_start(st, after=None):
        bufs2, (sems,), token = _comm_call(f"scatter_chips_start_{st['u']}", st["bufs"], start=[st["to_chips"]],
                                           after=after)
        st.update(bufs=bufs2, sems=sems, token=token)

    def scatter_finish(st, after):
        bufs, _, _ = _comm_call(f"scatter_chips_done_{st['u']}", st["bufs"], wait=[(st["to_chips"], st["sems"])],
                                after=after)
        return {k: [(bufs["ps_" + k], 3), (bufs["ld_" + k], 0), (bufs["ld_" + k], 1), (bufs["ld_" + k], 2)]
                for k in st["keys"]}

    moments = dict(win=(m_w_in, v_w_in), wpo=(m_w_pool_o, v_w_pool_o), who=(m_w_hgrn_o, v_w_hgrn_o),
                   wout=(m_w_out, v_w_out))
    big_out = {}

    def finish_unit(unit, after):
        for k, contribs in scatter_finish(scat[unit], after).items():
            wname, l = k[:-1], int(k[-1])
            big_out[wname] = _adamw_layer(big[wname], moments[wname][0], moments[wname][1], contribs, l,
                                          pair_rows[wname], f"adamw_{k}", prev=big_out.get(wname))
            after = big_out[wname][0]
        return after

    d_ada, small, scat = [None] * DEPTH, [None] * DEPTH, {}
    for l in reversed(range(DEPTH)):
        h, z, a_in, o, b_in, states, ba, bb, merged, y, who_l, wout_l = saved[l]
        dy, dba, dbb, da_in, db_in, dz, acc_post = _merge_bwd(
            dx, y, ba, bb, z, gw[f"wpo{l}"], who_l, wout_l, gate[l], g_post[l:l + 1],
            lax.empty((seq, IN_WIDTH), MXU_DTYPE), tm_merge, f"merge_bwd_{l}")
        g_out, g_ho, g_po = _grad_out_weights(merged, dy, b_in, dbb, a_in, dba, f"grad_out_weights_{l}")
        g_small = {f"wout{l}": g_out.reshape(N_DEV, HEAD_DIM, D_MODEL),
                   f"who{l}": g_ho.reshape(N_DEV, HEAD_DIM, D_MODEL), f"wpo{l}": g_po}
        st_small = scat["small0"] = scatter_pair_start("small0", g_small) if l == 0 else None
        dz, dlb, dgn = _hgrn_bwd(db_in, z, o, states, lb[l:l + 1], hgrn_norm_g[l:l + 1], dz, f"hgrn_bwd_{l}",
                                 after=st_small and st_small["token"])
        if l == 0:
            scatter_pair_finish(st_small, dlb)
            scatter_chips_start(st_small)
        dz, dpw, dps = _pool_bwd(da_in, z, pool_w[l], pool_scale[l:l + 1], dz, f"pool_bwd_{l}",
                                 after=st_small and st_small["token"])
        small[l] = dict(g_post=acc_post[1], pool_w=dpw, pool_scale=dps[0], lb_logits=dlb[0], hgrn_norm_g=dgn[0])
        token = None
        if l == 0:
            parts = {name: jnp.stack([small[0][name], small[1][name]]) for name in small[0]}
            parts.update(b_ada=[None, d_ada[1]], g_pre=[None, small[1]["g_pre"]])
            sg_stream = _direct_gather_stream("sg")
            early = _pack_small(parts, 2)
            sg_bufs, (sg_sems,), token = _comm_call(
                "small_grads_start", dict(s_sg=early, g_sg=_with_own_slot(early, me)), start=[sg_stream])
        g_win = {f"win{l}": _in_proj_dw(h, dz, f"grad_w_in_{l}", after=token)}
        st_win = scat[f"win{l}"] = scatter_pair_start(f"win{l}", g_win if l == 0 else {**g_small, **g_win})
        if l > 0:
            dh = _in_proj_dh(dz, gw[f"win{l}"], seq, f"in_proj_dh_{l}", after=st_win["token"])
            scatter_pair_finish(st_win, dh)
            scatter_chips_start(st_win)
        else:
            scatter_pair_finish(st_win, st_win["token"])
            scatter_chips_start(st_win)
            after = st_win["token"]
            for unit in ("win1", "small0"):
                after = finish_unit(unit, after)
            dh = _in_proj_dh(dz, gw[f"win{l}"], seq, f"in_proj_dh_{l}", after=after)
        dx, acc_pre = _prenorm_bwd(xs[l], dh, dx, g_pre[l:l + 1], scale[l], tm, f"prenorm_bwd_{l}",
                                   after=st_win["token"])
        d_ada[l] = jnp.concatenate([acc_pre[0], acc_pre[1], acc_post[0]])
        small[l]["g_pre"] = acc_pre[2]
    grad_x = dx[None]

    parts = dict(b_ada=[d_ada[0]], g_pre=[small[0]["g_pre"]])
    late = jnp.concatenate([_pack_small(parts, 0, 2), jnp.broadcast_to(loss_part, (8, 128))], axis=0)
    g_late = _allgather_small(late, "allgather_late_grads")
    loss = jnp.sum(g_late[:, SMALL_LATE_ROWS, 0])
    sg_bufs, _, _ = _comm_call("small_grads_done", sg_bufs, wait=[(sg_stream, sg_sems)], after=g_late)
    g_early = sg_bufs["g_sg"]
    small_names = list(dict.fromkeys(name for name, _, _ in _SMALL_ROWS))
    weights = dict(b_ada=b_ada, g_pre=g_pre, g_post=g_post, pool_w=pool_w, pool_scale=pool_scale,
                   lb_logits=lb_logits, hgrn_norm_g=hgrn_norm_g)
    m_small = dict(b_ada=m_b_ada, g_pre=m_g_pre, g_post=m_g_post, pool_w=m_pool_w, pool_scale=m_pool_scale,
                   lb_logits=m_lb_logits, hgrn_norm_g=m_hgrn_norm_g)
    v_small = dict(b_ada=v_b_ada, g_pre=v_g_pre, g_post=v_g_post, pool_w=v_pool_w, pool_scale=v_pool_scale,
                   lb_logits=v_lb_logits, hgrn_norm_g=v_hgrn_norm_g)
    shapes = {name: weights[name].shape for name in small_names}
    small_out = _adamw_small(_pack_small(weights), _pack_small(m_small), _pack_small(v_small), g_late, g_early,
                             shapes)

    d_ada_all = jnp.stack([g_late[:, 0:24, :].reshape(N_DEV, 3 * D_MODEL),
                           g_early[:, 0:24, :].reshape(N_DEV, 3 * D_MODEL)], axis=1)
    d_cols = jnp.transpose(lax.dynamic_slice_in_dim(d_ada_all, me * ADA_COLS, ADA_COLS, axis=2), (1, 0, 2))
    g_w_ada = _ada_bwd(c_all, d_cols)
    ada_out = _adamw_sharded(w_ada, m_w_ada, v_w_ada, g_w_ada[:, None], 256, "adamw_w_ada")
    finish_unit("win0", ada_out[1][0, 0:8, 0:128] + small_out[1]["pool_scale"][0:1, 0:128])

    def leaf(kind):
        s = small_out[kind]
        return (ada_out[kind], s["b_ada"], s["g_pre"], s["g_post"], big_out["win"][kind], s["pool_w"], s["pool_scale"],
                s["lb_logits"], s["hgrn_norm_g"], big_out["wpo"][kind], big_out["who"][kind], big_out["wout"][kind])

    return (loss, grad_x) + leaf(0) + leaf(1) + leaf(2) + leaf(3)
```

```python
import jax
import jax.numpy as jnp
from jax import lax
from jax.experimental import pallas as pl
from jax.experimental.pallas import tpu as pltpu

F32 = jnp.float32
MXU_DTYPE = jnp.bfloat16
WIRE_DTYPE = jnp.bfloat16

N_DEV = 8
DEPTH = 2
D_MODEL = 1024
HEADS = 8
HEAD_DIM = 128
POOL_GROUPS = 4
GROUP_DIM = 128
POOL_WIDTH = POOL_GROUPS * GROUP_DIM
IN_WIDTH = 7168
CHUNK = 64
SUB = 16
N_SUB = CHUNK // SUB
FWD_STEP_CHUNKS = 8
BWD_STEP_CHUNKS = 4
EXP_CLAMP = 80.0
NORM_EPS = 1e-6
LOG_FLOOR = 1e-30
ADA_COLS = 3 * D_MODEL // N_DEV
IN_COLS = IN_WIDTH // N_DEV
COL_HQ, COL_HF, COL_HI, COL_HG, COL_MGP, COL_MGH = 1, 2, 3, 4, 5, 6

ADAM_LR = 0.001
ADAM_B1 = 0.9
ADAM_B2 = 0.999
ADAM_EPS = 1e-08
ADAM_WD = 0.01
ADAM_STEP = 10

VMEM_LIMIT = 48 * 1024 * 1024
MESH_ID = pl.DeviceIdType.MESH
HIGHEST = lax.Precision.HIGHEST

_SMALL_ROWS = (("b_ada", 0, 24), ("g_pre", 0, 8), ("b_ada", 1, 24), ("g_pre", 1, 8), ("g_post", None, 16),
               ("pool_w", None, 1024), ("pool_scale", None, 8), ("lb_logits", None, 16), ("hgrn_norm_g", None, 2))
SMALL_LATE_ROWS = 32
SMALL_ROWS_PAD = 1136
LB_ROW0 = 32 + 32 + 16 + 1024 + 8


def _params(**kw):
    return pltpu.CompilerParams(vmem_limit_bytes=VMEM_LIMIT, **kw)


def _sigmoid(v):
    return 1.0 / (1.0 + jnp.exp(-v))


def _dsilu(v, s):
    return s * (1.0 + v * (1.0 - s))


def _dot(a, b):
    return jnp.dot(a.astype(MXU_DTYPE), b.astype(MXU_DTYPE), preferred_element_type=F32)


def _dot_nt(a, b):
    return lax.dot_general(a.astype(MXU_DTYPE), b.astype(MXU_DTYPE), (((1,), (1,)), ((), ())),
                           preferred_element_type=F32)


def _dot_tn(a, b):
    return lax.dot_general(a.astype(MXU_DTYPE), b.astype(MXU_DTYPE), (((0,), (0,)), ((), ())),
                           preferred_element_type=F32)


def _pallas_after(body, n_in, after, *, in_specs, **kw):
    if after is None:
        return pl.pallas_call(body, in_specs=in_specs, **kw)

    def tied(*refs):
        body(*refs[:n_in], *refs[n_in + 1:])

    call = pl.pallas_call(tied, in_specs=list(in_specs) + [pl.BlockSpec(memory_space=pl.ANY)], **kw)
    return lambda *operands: call(*operands, after)


def _my_position():
    mx, my, mc = lax.axis_index("x"), lax.axis_index("y"), lax.axis_index("c")
    return mx, my, mc, 4 * mx + 2 * my + mc


def _peer(mx, my, mc, k):
    px = 1 - mx if (k >> 2) & 1 else mx
    py = 1 - my if (k >> 1) & 1 else my
    pc = 1 - mc if k & 1 else mc
    return (px, py, pc), 4 * px + 2 * py + pc


def _allgather_small(v, name, after=None):
    rows, cols = v.shape

    def body(v_ref, out_ref, send_sems, recv_sems):
        mx, my, mc, me = _my_position()
        out_ref[me] = v_ref[...]
        copies = []
        for k in range(1, N_DEV):
            peer, _ = _peer(mx, my, mc, k)
            cp = pltpu.make_async_remote_copy(
                src_ref=v_ref, dst_ref=out_ref.at[me],
                send_sem=send_sems.at[k - 1], recv_sem=recv_sems.at[k - 1],
                device_id=peer, device_id_type=MESH_ID)
            cp.start()
            copies.append(cp)
        for cp in copies:
            cp.wait()

    return _pallas_after(
        body, 1, after, name=name,
        out_shape=jax.ShapeDtypeStruct((N_DEV, rows, cols), v.dtype),
        in_specs=[pl.BlockSpec(memory_space=pltpu.VMEM)],
        out_specs=pl.BlockSpec(memory_space=pltpu.VMEM),
        scratch_shapes=[pltpu.SemaphoreType.DMA((N_DEV - 1,)), pltpu.SemaphoreType.DMA((N_DEV - 1,))],
        compiler_params=_params(),
    )(v)


class _Stream:
    def __init__(self, n, plan):
        self.n, self.plan = n, plan


def _comm_call(name, bufs, start=(), wait=(), after=None):
    names = list(bufs)

    def body(*refs):
        it = iter(refs)
        buf_refs = {n: next(it) for n in names}
        wait_sems = [(next(it), next(it)) for _ in wait]
        if after is not None:
            next(it)
        start_sems = [(next(it), next(it)) for _ in start]
        for _ in names:
            next(it)
        token = next(it)
        pos = _my_position()

        def descriptors(stream, sems):
            return [pltpu.make_async_remote_copy(src_ref=src, dst_ref=dst, send_sem=sems[0].at[k], recv_sem=sems[1].at[k],
                                                 device_id=dev, device_id_type=MESH_ID)
                    for k, (src, dst, dev) in enumerate(stream.plan(buf_refs, pos))]

        for (stream, _), sems in zip(wait, wait_sems):
            for cp in descriptors(stream, sems):
                cp.wait_send()
                cp.wait_recv()
        for stream, sems in zip(start, start_sems):
            for cp in descriptors(stream, sems):
                cp.start()
        token[...] = jnp.zeros_like(token)

    hbm = pl.BlockSpec(memory_space=pltpu.HBM)
    sem = pl.BlockSpec(memory_space=pltpu.SEMAPHORE)
    operands = [pltpu.with_memory_space_constraint(bufs[n], pltpu.HBM) for n in names]
    in_specs = [hbm] * len(names)
    for _, (send_sems, recv_sems) in wait:
        operands += [send_sems, recv_sems]
        in_specs += [sem, sem]
    if after is not None:
        operands.append(after)
        in_specs.append(pl.BlockSpec(memory_space=pl.ANY))
    out_shape, out_specs = [], []
    for stream in start:
        out_shape += [pltpu.SemaphoreType.DMA((stream.n,)), pltpu.SemaphoreType.DMA((stream.n,))]
        out_specs += [sem, sem]
    n_sem_out = len(out_shape)
    out_shape += [pltpu.HBM(bufs[n].shape, bufs[n].dtype) for n in names]
    out_specs += [hbm] * len(names)
    out_shape.append(jax.ShapeDtypeStruct((8, 128), F32))
    out_specs.append(pl.BlockSpec(memory_space=pltpu.VMEM))
    outs = pl.pallas_call(
        body, name=name, out_shape=out_shape, in_specs=in_specs, out_specs=out_specs,
        input_output_aliases={i: n_sem_out + i for i in range(len(names))},
        compiler_params=pltpu.CompilerParams(has_side_effects=pltpu.SideEffectType.DATAFLOW_SIDE_EFFECTING),
    )(*operands)
    sems = [(outs[2 * i], outs[2 * i + 1]) for i in range(len(start))]
    return dict(zip(names, outs[n_sem_out:n_sem_out + len(names)])), sems, outs[-1]


def _with_own_slot(block, me):
    return lax.dynamic_update_index_in_dim(lax.empty((N_DEV,) + block.shape, block.dtype), block, me, 0)


def _other_chips(pos):
    mx, my, _, _ = pos
    return [(1 - mx if i & 2 else mx, 1 - my if i & 1 else my) for i in (1, 2, 3)]


def _dev_index(px, py, pc):
    return 4 * px + 2 * py + pc


def _gather_streams(keys):
    def to_chips(refs, pos):
        _, _, mc, me = pos
        return [(refs["s_" + k], refs["g_" + k].at[me], (cx, cy, mc)) for k in keys for cx, cy in _other_chips(pos)]

    def to_sibling(refs, pos):
        mx, my, mc, me = pos
        return [(refs["s_" + k], refs["g_" + k].at[me], (mx, my, 1 - mc)) for k in keys]

    def pass_on(refs, pos):
        mx, my, mc, _ = pos
        out = []
        for k in keys:
            for cx, cy in _other_chips(pos):
                slot = refs["g_" + k].at[_dev_index(cx, cy, mc)]
                out.append((slot, slot, (mx, my, 1 - mc)))
        return out

    return _Stream(3 * len(keys), to_chips), _Stream(len(keys), to_sibling), _Stream(3 * len(keys), pass_on)


def _direct_gather_stream(key):
    def plan(refs, pos):
        mx, my, mc, me = pos
        return [(refs["s_" + key], refs["g_" + key].at[me], _peer(mx, my, mc, k)[0]) for k in range(1, N_DEV)]

    return _Stream(N_DEV - 1, plan)


def _scatter_streams(keys):
    def pair(refs, pos):
        mx, my, mc, _ = pos
        sib = (mx, my, 1 - mc)
        out = []
        for k in keys:
            for i, (cx, cy) in enumerate(_other_chips(pos)):
                out.append((refs["g_" + k].at[_dev_index(cx, cy, 1 - mc)], refs["st_" + k].at[i], sib))
            out.append((refs["g_" + k].at[_dev_index(mx, my, 1 - mc)], refs["st_" + k].at[3], sib))
        return out

    def chips(refs, pos):
        mc = pos[2]
        return [(refs["ps_" + k].at[i], refs["ld_" + k].at[i], (cx, cy, mc))
                for k in keys for i, (cx, cy) in enumerate(_other_chips(pos))]

    return _Stream(4 * len(keys), pair), _Stream(3 * len(keys), chips)


def _pair_sum(g, st, idx, tr, name):
    _, rows, cols = g.shape

    def body(idx_ref, g_ref, st_ref, out_ref):
        out_ref[...] = (g_ref[...].astype(F32) + st_ref[...].astype(F32)).astype(out_ref.dtype)

    return pl.pallas_call(
        body, name=name,
        grid_spec=pltpu.PrefetchScalarGridSpec(
            num_scalar_prefetch=1, grid=(4, rows // tr),
            in_specs=[pl.BlockSpec((None, tr, cols), lambda j, i, idx_ref: (idx_ref[j], i, 0)),
                      pl.BlockSpec((None, tr, cols), lambda j, i, idx_ref: (j, i, 0))],
            out_specs=pl.BlockSpec((None, tr, cols), lambda j, i, idx_ref: (j, i, 0))),
        out_shape=jax.ShapeDtypeStruct((4, rows, cols), WIRE_DTYPE),
        compiler_params=_params(dimension_semantics=("parallel", "parallel")),
    )(idx, g, st)


def _ada_fwd(c_all, w_ada, b_cols):
    def body(c_ref, w_ref, b_ref, out_ref):
        cv = c_ref[...]
        ca = cv * _sigmoid(cv)
        for l in range(DEPTH):
            out_ref[l] = jnp.dot(ca, w_ref[l], precision=HIGHEST, preferred_element_type=F32) + b_ref[l:l + 1, :]

    return pl.pallas_call(
        body, name="ada_fwd",
        out_shape=jax.ShapeDtypeStruct((DEPTH, N_DEV, ADA_COLS), F32),
        compiler_params=_params(),
    )(c_all, w_ada, b_cols)


def _ada_bwd(c_all, d_cols):
    def body(c_ref, d_ref, out_ref):
        cv = c_ref[...]
        ca = cv * _sigmoid(cv)
        for l in range(DEPTH):
            out_ref[l] = lax.dot_general(ca, d_ref[l], (((0,), (0,)), ((), ())), precision=HIGHEST,
                                         preferred_element_type=F32)

    return pl.pallas_call(
        body, name="ada_bwd",
        out_shape=jax.ShapeDtypeStruct((DEPTH, D_MODEL, ADA_COLS), F32),
        compiler_params=_params(),
    )(c_all, d_cols)


def _lower_bounds(logits):
    m = jnp.maximum(logits[0:1], logits[1:2])
    e0, e1 = jnp.exp(logits[0:1] - m), jnp.exp(logits[1:2] - m)
    den = e0 + e1
    p0, p1 = e0 / den, e1 / den
    low0 = p0 - p0
    low1 = (p0 + p1) - p0
    return (p0, p1), (low0, low1)


def _lb_fwd(lb_logits):
    def body(lg_ref, out_ref):
        _, (low0, low1) = _lower_bounds(lg_ref[...])
        out_ref[0:1, :] = jnp.clip(low0, 0.0, 1.0)
        out_ref[1:2, :] = jnp.clip(low1, 0.0, 1.0)

    return pl.pallas_call(body, name="lb_fwd", out_shape=jax.ShapeDtypeStruct(lb_logits.shape, F32),
                          compiler_params=_params())(lb_logits)


def _row_spec(cols=D_MODEL):
    return pl.BlockSpec((1, cols), lambda *_: (0, 0))


def _prenorm_fwd(x, g, shift, scale, tm, name, after=None):
    seq = x.shape[0]

    def body(x_ref, g_ref, sh_ref, sc_ref, h_ref):
        xv = x_ref[...]
        rs = lax.rsqrt(jnp.mean(xv * xv, axis=-1, keepdims=True) + NORM_EPS)
        h = (xv * rs * g_ref[...]) * (1.0 + sc_ref[...]) + sh_ref[...]
        h_ref[...] = h.astype(h_ref.dtype)

    tile = pl.BlockSpec((tm, D_MODEL), lambda i: (i, 0))
    return _pallas_after(
        body, 4, after, name=name, grid=(seq // tm,),
        in_specs=[tile, _row_spec(), _row_spec(), _row_spec()], out_specs=tile,
        out_shape=jax.ShapeDtypeStruct((seq, D_MODEL), MXU_DTYPE),
        compiler_params=_params(dimension_semantics=("parallel",)),
    )(x, g, shift, scale)


def _in_proj(h, win_g, tm, name, after=None):
    seq = h.shape[0]

    def body(h_ref, w_ref, z_ref, w_pair):
        @pl.when(pl.program_id(1) == 0)
        def _():
            w_pair[...] = jnp.concatenate([w_ref[0], w_ref[1]], axis=1)

        z_ref[...] = jnp.dot(h_ref[...], w_pair[...], preferred_element_type=F32)

    return _pallas_after(
        body, 2, after, name=name, grid=(N_DEV // 2, seq // tm),
        in_specs=[pl.BlockSpec((tm, D_MODEL), lambda j, i: (i, 0)),
                  pl.BlockSpec((2, D_MODEL, IN_COLS), lambda j, i: (j, 0, 0))],
        out_specs=pl.BlockSpec((tm, 2 * IN_COLS), lambda j, i: (i, j)),
        out_shape=jax.ShapeDtypeStruct((seq, IN_WIDTH), F32),
        scratch_shapes=[pltpu.VMEM((D_MODEL, 2 * IN_COLS), MXU_DTYPE)],
        compiler_params=_params(dimension_semantics=("parallel", "arbitrary")),
    )(h, win_g)


def _shift_down(v, j, pos):
    return jnp.where(pos >= j, pltpu.roll(v, j, 0), 0.0)


def _shift_up(v, j, pos, seq):
    return jnp.where(pos < seq - j, pltpu.roll(v, seq - j, 0), 0.0)


def _select_window(g, candidates):
    out = candidates[-1]
    for i in range(len(candidates) - 2, -1, -1):
        out = jnp.where(g == i, candidates[i], out)
    return out


def _pool_mean_minus_token(u, g, pos):
    sums, acc = [], u
    for j in (1, 2, 4, 8):
        acc = acc + _shift_down(acc, j, pos)
        sums.append(acc)
    wsum = _select_window(g, sums)
    width = jnp.left_shift(2, g).astype(F32)
    count = jnp.minimum(pos.astype(F32) + 1.0, width)
    return wsum / count - u, count


def _pool_fwd(z, pool_w_l, pool_scale_l, name, after=None):
    seq = z.shape[0]

    def body(pv_ref, pg_ref, w_ref, sc_ref, out_ref):
        g = pl.program_id(0)
        pos = lax.broadcasted_iota(jnp.int32, (seq, GROUP_DIM), 0)
        pm, _ = _pool_mean_minus_token(pv_ref[...], g, pos)
        lin = _dot(pm, w_ref[...]) * sc_ref[...]
        pg = pg_ref[...]
        out_ref[...] = (lin * (pg * _sigmoid(pg))).astype(out_ref.dtype)

    return _pallas_after(
        body, 4, after, name=name, grid=(POOL_GROUPS,),
        in_specs=[pl.BlockSpec((seq, GROUP_DIM), lambda g: (0, g)),
                  pl.BlockSpec((seq, GROUP_DIM), lambda g: (0, POOL_GROUPS + g)),
                  pl.BlockSpec((None, GROUP_DIM, GROUP_DIM), lambda g: (g, 0, 0)),
                  pl.BlockSpec((1, GROUP_DIM), lambda g: (0, g))],
        out_specs=pl.BlockSpec((seq, GROUP_DIM), lambda g: (0, g)),
        out_shape=jax.ShapeDtypeStruct((seq, POOL_WIDTH), MXU_DTYPE),
        compiler_params=_params(dimension_semantics=("parallel",)),
    )(z, z, pool_w_l, pool_scale_l)


def _chunk_masks():
    row = lax.broadcasted_iota(jnp.int32, (CHUNK, CHUNK), 0)
    col = lax.broadcasted_iota(jnp.int32, (CHUNK, CHUNK), 1)
    causal = row >= col
    before_sub = col < (row // SUB) * SUB
    suffix = row <= col
    return causal, before_sub, suffix


def _masked_sums(masks, v):
    lhs = jnp.concatenate([m.astype(jnp.bfloat16) for m in masks], axis=0)
    hi = v.astype(jnp.bfloat16)
    rest = v - hi.astype(F32)
    mid = rest.astype(jnp.bfloat16)
    lo = (rest - mid.astype(F32)).astype(jnp.bfloat16)
    out = jnp.dot(lhs, hi, preferred_element_type=F32)
    out += jnp.dot(lhs, mid, preferred_element_type=F32)
    out += jnp.dot(lhs, lo, preferred_element_type=F32)
    return [out[i * CHUNK:(i + 1) * CHUNK] for i in range(len(masks))]


def _gates(zf, lb):
    sg = _sigmoid(zf)
    f = lb + (1.0 - lb) * sg
    logf = jnp.log(jnp.maximum(f, LOG_FLOOR))
    return sg, f, logf


def _intra_blocks(q_h, k_h, cum_h, base_h, causal):
    rel = cum_h - base_h
    out = []
    for i in range(N_SUB):
        rows = slice(i * SUB, (i + 1) * SUB)
        e_q = jnp.exp(rel[rows])
        base_i = jnp.concatenate([base_h[rows]] * N_SUB, axis=0)
        e_k = jnp.exp(jnp.minimum(base_i - cum_h, EXP_CLAMP))
        q_t = (q_h[rows] * e_q).astype(MXU_DTYPE)
        k_t = (k_h * e_k).astype(MXU_DTYPE)
        a_i = jnp.where(causal[rows], _dot_nt(q_t, k_t), 0.0)
        out.append((q_t, k_t, e_q, e_k, a_i))
    return out


def _hgrn_fwd(z, lb_l, gn_l, name, after=None):
    seq = z.shape[0]
    n_chunks = seq // CHUNK
    per_step = min(FWD_STEP_CHUNKS, n_chunks)
    rows_per_step = per_step * CHUNK

    def body(hq_ref, hf_ref, hi_ref, hg_ref, lb_ref, gn_ref, o_ref, bin_ref, st_ref, cb_ref, state):
        @pl.when(pl.program_id(0) == 0)
        def _():
            state[...] = jnp.zeros_like(state)

        causal, before_sub, _ = _chunk_masks()
        for cc in range(per_step):
            rows = slice(cc * CHUNK, (cc + 1) * CHUNK)
            _, f, logf = _gates(hf_ref[rows, :], lb_ref[...])
            kk = 1.0 - f
            hq = hq_ref[rows, :]
            q = hq * _sigmoid(hq)
            cum, base = _masked_sums([causal, before_sub], logf)
            cb_ref[rows, 0:D_MODEL] = cum
            cb_ref[rows, D_MODEL:2 * D_MODEL] = base
            st_ref[cc] = state[...]
            for h in range(HEADS):
                sl = slice(h * HEAD_DIM, (h + 1) * HEAD_DIM)
                q_h, k_h, cum_h = q[:, sl], kk[:, sl], cum[:, sl]
                v_h = hi_ref[rows, sl]
                st_h = state[h]
                blocks = _intra_blocks(q_h, k_h, cum_h, base[:, sl], causal)
                a = jnp.concatenate([b[4] for b in blocks], axis=0)
                o_h = _dot_nt(q_h * jnp.exp(cum_h), st_h) + _dot(a, v_h)
                last = jnp.sum(logf[:, sl], axis=0, keepdims=True)
                state[h] = st_h * jnp.exp(last) + _dot_tn(v_h, k_h * jnp.exp(last - cum_h))
                rs = lax.rsqrt(jnp.mean(o_h * o_h, axis=-1, keepdims=True) + NORM_EPS)
                hg = hg_ref[rows, sl]
                o_ref[rows, sl] = o_h
                bin_ref[rows, sl] = ((o_h * rs * gn_ref[...]) * (hg * _sigmoid(hg))).astype(bin_ref.dtype)

    def col(block):
        return pl.BlockSpec((rows_per_step, D_MODEL), lambda c: (c, block))

    tile = pl.BlockSpec((rows_per_step, D_MODEL), lambda c: (c, 0))
    return _pallas_after(
        body, 6, after, name=name, grid=(n_chunks // per_step,),
        in_specs=[col(COL_HQ), col(COL_HF), col(COL_HI), col(COL_HG), _row_spec(), _row_spec(HEAD_DIM)],
        out_specs=[tile, tile, pl.BlockSpec((per_step, HEADS, HEAD_DIM, HEAD_DIM), lambda c: (c, 0, 0, 0)),
                   pl.BlockSpec((rows_per_step, 2 * D_MODEL), lambda c: (c, 0))],
        out_shape=[jax.ShapeDtypeStruct((seq, D_MODEL), F32),
                   jax.ShapeDtypeStruct((seq, D_MODEL), MXU_DTYPE),
                   jax.ShapeDtypeStruct((n_chunks, HEADS, HEAD_DIM, HEAD_DIM), F32),
                   jax.ShapeDtypeStruct((seq, 2 * D_MODEL), F32)],
        scratch_shapes=[pltpu.VMEM((HEADS, HEAD_DIM, HEAD_DIM), F32)],
        compiler_params=_params(dimension_semantics=("arbitrary",)),
    )(z, z, z, z, lb_l, gn_l)


def _rms_parts(y):
    rs = lax.rsqrt(jnp.mean(y * y, axis=-1, keepdims=True) + NORM_EPS)
    return rs, y * rs


def _merge_fwd(a_in, b_in, z, x, wpo_g, who_g, wout_g, gate, g_post, tm, name, target=None):
    seq = x.shape[0]
    with_loss = target is not None

    def body(*refs):
        a_ref, b_ref, mgp_ref, mgh_ref, x_ref, wpo_ref, who_ref, wout_ref, gate_ref, gp_ref = refs[:10]
        ba_ref, bb_ref, mer_ref, y_ref, last_ref = refs[10 + with_loss:15 + with_loss]
        a = a_ref[...]
        ba = jnp.concatenate([_dot(a, wpo_ref[j]) for j in range(N_DEV)], axis=1)
        bb = _dot(b_ref[...], who_ref[...])
        merged = _sigmoid(mgp_ref[...]) * ba + _sigmoid(mgh_ref[...]) * bb
        y = _dot(merged, wout_ref[...])
        _, yn = _rms_parts(y)
        ba_ref[...] = ba.astype(ba_ref.dtype)
        bb_ref[...] = bb.astype(bb_ref.dtype)
        mer_ref[...] = merged.astype(mer_ref.dtype)
        y_ref[...] = y.astype(y_ref.dtype)
        x_next = x_ref[...] + gate_ref[...] * (yn * gp_ref[...])
        if not with_loss:
            last_ref[...] = x_next
            return
        loss_ref = refs[16]

        @pl.when(pl.program_id(0) == 0)
        def _():
            loss_ref[...] = jnp.zeros_like(loss_ref)

        err = x_next - refs[10][...]
        loss_ref[...] += 0.5 * jnp.sum(jnp.mean(err * err, axis=-1, keepdims=True), axis=0, keepdims=True)
        last_ref[...] = err * (1.0 / D_MODEL)

    def tile(cols=D_MODEL, block=0):
        return pl.BlockSpec((tm, cols), lambda i: (i, block))

    full = pl.BlockSpec((D_MODEL, D_MODEL), lambda i: (0, 0))
    act = jax.ShapeDtypeStruct((seq, D_MODEL), MXU_DTYPE)
    f32 = jax.ShapeDtypeStruct((seq, D_MODEL), F32)
    one = [pl.BlockSpec((1, 1), lambda i: (0, 0))] if with_loss else []
    return pl.pallas_call(
        body, name=name, grid=(seq // tm,),
        in_specs=[tile(POOL_WIDTH), tile(), tile(block=COL_MGP), tile(block=COL_MGH), tile(),
                  pl.BlockSpec((N_DEV, POOL_WIDTH, GROUP_DIM), lambda i: (0, 0, 0)),
                  full, full, _row_spec(), _row_spec()] + ([tile()] if with_loss else []),
        out_specs=[tile(), tile(), tile(), tile(), tile()] + one,
        out_shape=[act, act, act, act, f32] + ([jax.ShapeDtypeStruct((1, 1), F32)] if with_loss else []),
        compiler_params=_params(dimension_semantics=("arbitrary" if with_loss else "parallel",)),
    )(a_in, b_in, z, z, x, wpo_g, who_g, wout_g, gate, g_post, *([target] if with_loss else []))


def _stage_copy(stage, sems, dst, slot, step, where):
    rows, cols = where(step)
    return pltpu.make_async_copy(stage.at[slot], dst.at[rows, cols], sems.at[slot])


def _stage_begin(stage, sems, dst, step, where):
    slot = step % 2

    @pl.when(step >= 2)
    def _():
        _stage_copy(stage, sems, dst, slot, step - 2, where).wait()

    return slot


def _stage_end(stage, sems, dst, step, n_steps, where):
    slot = step % 2
    _stage_copy(stage, sems, dst, slot, step, where).start()

    @pl.when(step == n_steps - 1)
    def _():
        _stage_copy(stage, sems, dst, slot, step, where).wait()
        if n_steps > 1:
            _stage_copy(stage, sems, dst, 1 - slot, step - 1, where).wait()


def _merge_bwd(dx, y, ba, bb, z, wpo_g, who_g, wout_g, gate, g_post, dz, tm, name):
    seq = dx.shape[0]
    n_steps = seq // tm

    def body(dx_ref, y_ref, ba_ref, bb_ref, mgp_ref, mgh_ref, wpo_ref, who_ref, wout_ref, gate_ref, gp_ref, _,
             dy_ref, dba_ref, dbb_ref, da_ref, db_ref, dz_ref, acc_ref, stage, sems):
        step = pl.program_id(0)

        @pl.when(step == 0)
        def _():
            acc_ref[...] = jnp.zeros_like(acc_ref)

        def where(t):
            return pl.ds(t * tm, tm), pl.ds(COL_MGP * D_MODEL, 2 * D_MODEL)

        dmg_ref = stage.at[_stage_begin(stage, sems, dz_ref, step, where)]

        dxv = dx_ref[...]
        rs, yn = _rms_parts(y_ref[...].astype(F32))
        acc_ref[0:1, :] += jnp.sum(dxv * yn * gp_ref[...], axis=0, keepdims=True)
        acc_ref[1:2, :] += jnp.sum(dxv * gate_ref[...] * yn, axis=0, keepdims=True)
        dyn = dxv * (gate_ref[...] * gp_ref[...])
        dy = rs * (dyn - yn * jnp.mean(dyn * yn, axis=-1, keepdims=True))
        dmerged = _dot_nt(dy, wout_ref[...])
        sp, sh = _sigmoid(mgp_ref[...]), _sigmoid(mgh_ref[...])
        dba, dbb = sp * dmerged, sh * dmerged
        dmg_ref[:, 0:D_MODEL] = (dmerged * ba_ref[...].astype(F32) * sp * (1.0 - sp)).astype(dmg_ref.dtype)
        dmg_ref[:, D_MODEL:2 * D_MODEL] = (dmerged * bb_ref[...].astype(F32) * sh * (1.0 - sh)).astype(dmg_ref.dtype)
        da = _dot_nt(dba[:, 0:GROUP_DIM], wpo_ref[0])
        for j in range(1, N_DEV):
            da += _dot_nt(dba[:, j * GROUP_DIM:(j + 1) * GROUP_DIM], wpo_ref[j])
        dy_ref[...] = dy.astype(dy_ref.dtype)
        dba_ref[...] = dba.astype(dba_ref.dtype)
        dbb_ref[...] = dbb.astype(dbb_ref.dtype)
        da_ref[...] = da.astype(da_ref.dtype)
        db_ref[...] = _dot_nt(dbb, who_ref[...]).astype(db_ref.dtype)
        _stage_end(stage, sems, dz_ref, step, n_steps, where)

    def tile(cols=D_MODEL, block=0):
        return pl.BlockSpec((tm, cols), lambda i: (i, block))

    full = pl.BlockSpec((D_MODEL, D_MODEL), lambda i: (0, 0))
    hbm = pl.BlockSpec(memory_space=pl.ANY)
    act = jax.ShapeDtypeStruct((seq, D_MODEL), MXU_DTYPE)
    return pl.pallas_call(
        body, name=name, grid=(n_steps,),
        in_specs=[tile(), tile(), tile(), tile(), tile(block=COL_MGP), tile(block=COL_MGH),
                  pl.BlockSpec((N_DEV, POOL_WIDTH, GROUP_DIM), lambda i: (0, 0, 0)),
                  full, full, _row_spec(), _row_spec(), hbm],
        out_specs=[tile(), tile(), tile(), tile(POOL_WIDTH), tile(), hbm,
                   pl.BlockSpec((8, D_MODEL), lambda i: (0, 0))],
        out_shape=[act, act, act, jax.ShapeDtypeStruct((seq, POOL_WIDTH), MXU_DTYPE), act,
                   jax.ShapeDtypeStruct(dz.shape, dz.dtype),
                   jax.ShapeDtypeStruct((8, D_MODEL), F32)],
        input_output_aliases={11: 5},
        scratch_shapes=[pltpu.VMEM((2, tm, 2 * D_MODEL), MXU_DTYPE), pltpu.SemaphoreType.DMA((2,))],
        compiler_params=_params(dimension_semantics=("arbitrary",)),
    )(dx, y, ba, bb, z, z, wpo_g, who_g, wout_g, gate, g_post, dz)


def _grad_out_weights(merged, dy, b_in, dbb, a_in, dba, name):
    seq = merged.shape[0]
    tn = D_MODEL // 2
    per_step = tn // GROUP_DIM

    def body(mer_ref, dy_ref, b_ref, dbb_ref, a_ref, dba_ref, gout_ref, gho_ref, gpo_ref):
        gout_ref[...] = _dot_tn(mer_ref[...], dy_ref[...]).astype(gout_ref.dtype)
        gho_ref[...] = _dot_tn(b_ref[...], dbb_ref[...]).astype(gho_ref.dtype)
        g_po = _dot_tn(a_ref[...], dba_ref[...])
        for j in range(per_step):
            gpo_ref[j] = g_po[:, j * GROUP_DIM:(j + 1) * GROUP_DIM].astype(gpo_ref.dtype)

    def whole(cols):
        return pl.BlockSpec((seq, cols), lambda j: (0, 0))

    cols = pl.BlockSpec((seq, tn), lambda j: (0, j))
    return pl.pallas_call(
        body, name=name, grid=(D_MODEL // tn,),
        in_specs=[whole(D_MODEL), cols, whole(D_MODEL), cols, whole(POOL_WIDTH), cols],
        out_specs=[pl.BlockSpec((D_MODEL, tn), lambda j: (0, j)), pl.BlockSpec((D_MODEL, tn), lambda j: (0, j)),
                   pl.BlockSpec((per_step, POOL_WIDTH, GROUP_DIM), lambda j: (j, 0, 0))],
        out_shape=[jax.ShapeDtypeStruct((D_MODEL, D_MODEL), WIRE_DTYPE),
                   jax.ShapeDtypeStruct((D_MODEL, D_MODEL), WIRE_DTYPE),
                   jax.ShapeDtypeStruct((N_DEV, POOL_WIDTH, GROUP_DIM), WIRE_DTYPE)],
        compiler_params=_params(dimension_semantics=("parallel",)),
    )(merged, dy, b_in, dbb, a_in, dba)


def _hgrn_bwd(db_in, z, o, states, cum_base, lb_l, gn_l, dz, name, after=None):
    seq = z.shape[0]
    per_step = min(BWD_STEP_CHUNKS, seq // CHUNK)
    rows_per_step = per_step * CHUNK
    n_steps = seq // rows_per_step
    last_step = n_steps - 1

    def body(db_ref, hq_ref, hf_ref, hi_ref, hg_ref, o_ref, st_ref, cb_ref, lb_ref, gn_ref, _,
             dz_hbm, dlb_ref, dgn_ref, dstate, dq_buf, dk_buf, dg_buf, stage, sems):
        step = pl.program_id(0)

        @pl.when(step == 0)
        def _():
            dstate[...] = jnp.zeros_like(dstate)
            dlb_ref[...] = jnp.zeros_like(dlb_ref)
            dgn_ref[...] = jnp.zeros_like(dgn_ref)

        def one_chunk(cc, *args):
            one_chunk_body((db_ref, hq_ref, hf_ref, hi_ref, hg_ref, o_ref, st_ref, cb_ref, dlb_ref, dgn_ref, dstate,
                            dq_buf, dk_buf, dg_buf), cc, *args)

        def where(t):
            return pl.ds((last_step - t) * rows_per_step, rows_per_step), pl.ds(COL_HQ * D_MODEL, 4 * D_MODEL)

        dz_step = stage.at[_stage_begin(stage, sems, dz_hbm, step, where)]
        causal, before_sub, suffix = _chunk_masks()
        lb = lb_ref[...]
        gn = gn_ref[...]
        for cc in reversed(range(per_step)):
            one_chunk(cc, dz_step, causal, before_sub, suffix, lb, gn)
        _stage_end(stage, sems, dz_hbm, step, n_steps, where)

    def one_chunk_body(refs, cc, dz_step, causal, before_sub, suffix, lb, gn):
        (db_ref, hq_ref, hf_ref, hi_ref, hg_ref, o_ref, st_ref, cb_ref, dlb_ref, dgn_ref, dstate,
         dq_buf, dk_buf, dg_buf) = refs
        rows = slice(cc * CHUNK, (cc + 1) * CHUNK)
        dz_ref = dz_step.at[rows, :]
        dq_buf, dk_buf, dg_buf = dq_buf.at[cc], dk_buf.at[cc], dg_buf.at[cc]
        sg, f, logf = _gates(hf_ref[rows, :], lb)
        kk = 1.0 - f
        hq = hq_ref[rows, :]
        sq = _sigmoid(hq)
        q = hq * sq
        cum, base = cb_ref[rows, 0:D_MODEL], cb_ref[rows, D_MODEL:2 * D_MODEL]
        dgn = jnp.zeros((1, HEAD_DIM), F32)
        dlast = []
        for h in range(HEADS):
            sl = slice(h * HEAD_DIM, (h + 1) * HEAD_DIM)
            q_h, k_h, cum_h = q[:, sl], kk[:, sl], cum[:, sl]
            v_h = hi_ref[rows, sl]
            st_h = st_ref[cc, h]
            dst_h = dstate[h]
            rs, ohat = _rms_parts(o_ref[rows, sl])
            hg = hg_ref[rows, sl]
            shg = _sigmoid(hg)
            d_bin = db_ref[rows, sl].astype(F32)
            don = d_bin * (hg * shg)
            dgn += jnp.sum(don * ohat, axis=0, keepdims=True)
            dohat = don * gn
            do = rs * (dohat - ohat * jnp.mean(dohat * ohat, axis=-1, keepdims=True))
            dz_ref[:, 3 * D_MODEL + h * HEAD_DIM:3 * D_MODEL + (h + 1) * HEAD_DIM] = (
                d_bin * (ohat * gn) * _dsilu(hg, shg)).astype(dz_ref.dtype)
            last = cb_ref[(cc + 1) * CHUNK - 1:(cc + 1) * CHUNK, sl]
            g_in = jnp.exp(cum_h)
            d_out = jnp.exp(last - cum_h)
            q_bar, k_bar = q_h * g_in, k_h * d_out
            blocks = _intra_blocks(q_h, k_h, cum_h, base[:, sl], causal)
            a = jnp.concatenate([b[4] for b in blocks], axis=0)
            da = jnp.where(causal, _dot_nt(do, v_h), 0.0)
            dv = _dot_tn(a, do) + _dot_nt(k_bar, dst_h)
            dq_bar, dk_bar = _dot(do, st_h), _dot(v_h, dst_h)
            dk = dk_bar * d_out
            dq_parts, dg_parts = [], []
            dg_k = k_bar * dk_bar
            dlast.append(jnp.sum(k_bar * dk_bar, axis=0, keepdims=True)
                         + jnp.exp(last) * jnp.sum(st_h * dst_h, axis=0, keepdims=True))
            for i, (q_t, k_t, e_q, e_k, _) in enumerate(blocks):
                da_i = da[i * SUB:(i + 1) * SUB].astype(MXU_DTYPE)
                dq_t = _dot(da_i, k_t)
                dk_t = _dot_tn(da_i, q_t)
                dq_parts.append(dq_t * e_q)
                dk += dk_t * e_k
                dg_parts.append(q_t.astype(F32) * dq_t)
                dg_k += k_t.astype(F32) * dk_t
            dq = dq_bar * g_in + jnp.concatenate(dq_parts, axis=0)
            dg_buf[:, sl] = q_bar * dq_bar + jnp.concatenate(dg_parts, axis=0) - dg_k
            dstate[h] = dst_h * jnp.exp(last) + _dot_tn(do, q_bar)
            dq_buf[:, sl] = dq
            dk_buf[:, sl] = dk
            dz_ref[:, 2 * D_MODEL + h * HEAD_DIM:2 * D_MODEL + (h + 1) * HEAD_DIM] = dv.astype(dz_ref.dtype)
        dgn_ref[...] += dgn
        dq_all, dk_all = dq_buf[...], dk_buf[...]
        dlogf = _masked_sums([suffix], dg_buf[...])[0] + jnp.concatenate(dlast, axis=1)
        df = jnp.where(f > LOG_FLOOR, dlogf / f, 0.0) - dk_all
        dlb_ref[...] += jnp.sum(df * (1.0 - sg), axis=0, keepdims=True)
        dz_ref[:, 0:D_MODEL] = (dq_all * _dsilu(hq, sq)).astype(dz_ref.dtype)
        dz_ref[:, D_MODEL:2 * D_MODEL] = (df * (1.0 - lb) * sg * (1.0 - sg)).astype(dz_ref.dtype)

    def col(block):
        return pl.BlockSpec((rows_per_step, D_MODEL), lambda c: (last_step - c, block))

    hbm = pl.BlockSpec(memory_space=pl.ANY)
    return _pallas_after(
        body, 11, after, name=name, grid=(n_steps,),
        in_specs=[col(0), col(COL_HQ), col(COL_HF), col(COL_HI), col(COL_HG), col(0),
                  pl.BlockSpec((per_step, HEADS, HEAD_DIM, HEAD_DIM), lambda c: (last_step - c, 0, 0, 0)),
                  pl.BlockSpec((rows_per_step, 2 * D_MODEL), lambda c: (last_step - c, 0)),
                  _row_spec(), _row_spec(HEAD_DIM), hbm],
        out_specs=[hbm, _row_spec(), _row_spec(HEAD_DIM)],
        out_shape=[jax.ShapeDtypeStruct(dz.shape, dz.dtype),
                   jax.ShapeDtypeStruct((1, D_MODEL), F32), jax.ShapeDtypeStruct((1, HEAD_DIM), F32)],
        input_output_aliases={10: 0},
        scratch_shapes=[pltpu.VMEM((HEADS, HEAD_DIM, HEAD_DIM), F32)]
        + [pltpu.VMEM((per_step, CHUNK, D_MODEL), F32)] * 3
        + [pltpu.VMEM((2, rows_per_step, 4 * D_MODEL), MXU_DTYPE), pltpu.SemaphoreType.DMA((2,))],
        compiler_params=_params(dimension_semantics=("arbitrary",)),
    )(db_in, z, z, z, z, o, states, cum_base, lb_l, gn_l, dz)


def _pool_bwd(da_in, z, pool_w_l, pool_scale_l, dz, name, after=None):
    seq = z.shape[0]

    def body(da_ref, pv_ref, pg_ref, w_ref, sc_ref, _, dz_hbm, dw_ref, dsc_ref, stage_pv, stage_pg, sems_pv, sems_pg):
        g = pl.program_id(0)

        def where_pv(t):
            return pl.ds(0, seq), pl.ds(pl.multiple_of(t * GROUP_DIM, GROUP_DIM), GROUP_DIM)

        def where_pg(t):
            return pl.ds(0, seq), pl.ds(pl.multiple_of(POOL_WIDTH + t * GROUP_DIM, GROUP_DIM), GROUP_DIM)

        dpv_ref = stage_pv.at[_stage_begin(stage_pv, sems_pv, dz_hbm, g, where_pv)]
        dpg_ref = stage_pg.at[_stage_begin(stage_pg, sems_pg, dz_hbm, g, where_pg)]
        pos = lax.broadcasted_iota(jnp.int32, (seq, GROUP_DIM), 0)
        pm, count = _pool_mean_minus_token(pv_ref[...], g, pos)
        lin0 = _dot(pm, w_ref[...])
        pg = pg_ref[...]
        spg = _sigmoid(pg)
        da = da_ref[...].astype(F32)
        dlin = da * (pg * spg)
        dpg_ref[...] = (da * (lin0 * sc_ref[...]) * _dsilu(pg, spg)).astype(dpg_ref.dtype)
        dsc_ref[...] = jnp.sum(dlin * lin0, axis=0, keepdims=True)
        dl0 = dlin * sc_ref[...]
        dw_ref[...] = _dot_tn(pm, dl0)
        dpm = _dot_nt(dl0, w_ref[...])
        sums, acc = [], dpm / count
        for j in (1, 2, 4, 8):
            acc = acc + _shift_up(acc, j, pos, seq)
            sums.append(acc)
        dpv_ref[...] = (_select_window(g, sums) - dpm).astype(dpv_ref.dtype)
        _stage_end(stage_pv, sems_pv, dz_hbm, g, POOL_GROUPS, where_pv)
        _stage_end(stage_pg, sems_pg, dz_hbm, g, POOL_GROUPS, where_pg)

    grp = pl.BlockSpec((seq, GROUP_DIM), lambda g: (0, g))
    hbm = pl.BlockSpec(memory_space=pl.ANY)
    stage = pltpu.VMEM((2, seq, GROUP_DIM), MXU_DTYPE)
    return _pallas_after(
        body, 6, after, name=name, grid=(POOL_GROUPS,),
        in_specs=[grp, grp, pl.BlockSpec((seq, GROUP_DIM), lambda g: (0, POOL_GROUPS + g)),
                  pl.BlockSpec((None, GROUP_DIM, GROUP_DIM), lambda g: (g, 0, 0)),
                  pl.BlockSpec((1, GROUP_DIM), lambda g: (0, g)), hbm],
        out_specs=[hbm, pl.BlockSpec((None, GROUP_DIM, GROUP_DIM), lambda g: (g, 0, 0)),
                   pl.BlockSpec((1, GROUP_DIM), lambda g: (0, g))],
        out_shape=[jax.ShapeDtypeStruct(dz.shape, dz.dtype),
                   jax.ShapeDtypeStruct((POOL_GROUPS, GROUP_DIM, GROUP_DIM), F32),
                   jax.ShapeDtypeStruct((1, POOL_WIDTH), F32)],
        input_output_aliases={5: 0},
        scratch_shapes=[stage, stage, pltpu.SemaphoreType.DMA((2,)), pltpu.SemaphoreType.DMA((2,))],
        compiler_params=_params(dimension_semantics=("arbitrary",)),
    )(da_in, z, z, pool_w_l, pool_scale_l, dz)


def _in_proj_dw(h, dz, name, after=None):
    seq = h.shape[0]

    def body(h_ref, dz_ref, out_ref):
        pair = lax.dot_general(h_ref[...], dz_ref[...], (((0,), (0,)), ((), ())), preferred_element_type=F32)
        out_ref[0] = pair[:, 0:IN_COLS].astype(out_ref.dtype)
        out_ref[1] = pair[:, IN_COLS:].astype(out_ref.dtype)

    return _pallas_after(
        body, 2, after, name=name, grid=(N_DEV // 2,),
        in_specs=[pl.BlockSpec((seq, D_MODEL), lambda j: (0, 0)),
                  pl.BlockSpec((seq, 2 * IN_COLS), lambda j: (0, j))],
        out_specs=pl.BlockSpec((2, D_MODEL, IN_COLS), lambda j: (j, 0, 0)),
        out_shape=jax.ShapeDtypeStruct((N_DEV, D_MODEL, IN_COLS), WIRE_DTYPE),
        compiler_params=_params(dimension_semantics=("parallel",)),
    )(h, dz)


def _in_proj_dh(dz, win_g, tm, name, after=None):
    seq = dz.shape[0]

    def body(dz_ref, w_ref, dh_ref):
        @pl.when(pl.program_id(1) == 0)
        def _():
            dh_ref[...] = jnp.zeros_like(dh_ref)

        w_pair = jnp.concatenate([w_ref[0], w_ref[1]], axis=1)
        dh_ref[...] += lax.dot_general(dz_ref[...], w_pair, (((1,), (1,)), ((), ())), preferred_element_type=F32)

    return _pallas_after(
        body, 2, after, name=name, grid=(seq // tm, N_DEV // 2),
        in_specs=[pl.BlockSpec((tm, 2 * IN_COLS), lambda i, j: (i, j)),
                  pl.BlockSpec((2, D_MODEL, IN_COLS), lambda i, j: (j, 0, 0))],
        out_specs=pl.BlockSpec((tm, D_MODEL), lambda i, j: (i, 0)),
        out_shape=jax.ShapeDtypeStruct((seq, D_MODEL), F32),
        compiler_params=_params(dimension_semantics=("parallel", "arbitrary")),
    )(dz, win_g)


def _prenorm_bwd(x, dh, dx_res, g, scale, tm, name, after=None):
    seq = x.shape[0]

    def body(x_ref, dh_ref, dxr_ref, g_ref, sc_ref, dx_ref, acc_ref):
        @pl.when(pl.program_id(0) == 0)
        def _():
            acc_ref[...] = jnp.zeros_like(acc_ref)

        rs, xn = _rms_parts(x_ref[...])
        dh = dh_ref[...]
        acc_ref[0:1, :] += jnp.sum(dh, axis=0, keepdims=True)
        acc_ref[1:2, :] += jnp.sum(dh * (xn * g_ref[...]), axis=0, keepdims=True)
        dhn = dh * (1.0 + sc_ref[...])
        acc_ref[2:3, :] += jnp.sum(dhn * xn, axis=0, keepdims=True)
        dxn = dhn * g_ref[...]
        dx_ref[...] = rs * (dxn - xn * jnp.mean(dxn * xn, axis=-1, keepdims=True)) + dxr_ref[...]

    tile = pl.BlockSpec((tm, D_MODEL), lambda i: (i, 0))
    return _pallas_after(
        body, 5, after, name=name, grid=(seq // tm,),
        in_specs=[tile, tile, tile, _row_spec(), _row_spec()],
        out_specs=[tile, pl.BlockSpec((8, D_MODEL), lambda i: (0, 0))],
        out_shape=[jax.ShapeDtypeStruct((seq, D_MODEL), F32), jax.ShapeDtypeStruct((8, D_MODEL), F32)],
        compiler_params=_params(dimension_semantics=("arbitrary",)),
    )(x, dh, dx_res, g, scale)


def _adamw_math(w, g, m, v):
    m = ADAM_B1 * m + (1.0 - ADAM_B1) * g
    v = ADAM_B2 * v + (1.0 - ADAM_B2) * (g * g)
    m_hat = m / (1.0 - ADAM_B1 ** ADAM_STEP)
    v_hat = v / (1.0 - ADAM_B2 ** ADAM_STEP)
    delta = -ADAM_LR * (m_hat / (jnp.sqrt(v_hat) + ADAM_EPS) + ADAM_WD * w)
    return delta, m, v


def _adamw_sharded(w, m, v, contrib, tr, name):
    depth, rows, cols = w.shape
    n_parts = contrib.shape[1]

    def body(w_ref, m_ref, v_ref, c_ref, g_ref, d_ref, mo_ref, vo_ref):
        g = c_ref[0].astype(F32)
        for p in range(1, n_parts):
            g += c_ref[p].astype(F32)
        delta, mn, vn = _adamw_math(w_ref[...], g, m_ref[...], v_ref[...])
        g_ref[...] = g
        d_ref[...] = delta
        mo_ref[...] = mn
        vo_ref[...] = vn

    tile = pl.BlockSpec((None, tr, cols), lambda l, i: (l, i, 0))
    shape = jax.ShapeDtypeStruct(w.shape, F32)
    return pl.pallas_call(
        body, name=name, grid=(depth, rows // tr),
        in_specs=[tile, tile, tile, pl.BlockSpec((None, n_parts, tr, cols), lambda l, i: (l, 0, i, 0))],
        out_specs=[tile] * 4, out_shape=[shape] * 4,
        compiler_params=_params(dimension_semantics=("parallel", "parallel")),
    )(w, m, v, contrib)


def _adamw_layer(w, m, v, contribs, l, tr, name, prev=None):
    _, rows, cols = w.shape
    n = len(contribs)

    def body(*refs):
        w_ref, m_ref, v_ref = refs[:3]
        c_refs = refs[3:3 + n]
        g_ref, d_ref, mo_ref, vo_ref = refs[-4:]
        g = c_refs[0][...].astype(F32)
        for c_ref in c_refs[1:]:
            g += c_ref[...].astype(F32)
        delta, mn, vn = _adamw_math(w_ref[...], g, m_ref[...], v_ref[...])
        g_ref[...] = g
        d_ref[...] = delta
        mo_ref[...] = mn
        vo_ref[...] = vn

    tile = pl.BlockSpec((None, tr, cols), lambda i: (l, i, 0))
    in_specs = [tile, tile, tile] + [pl.BlockSpec((None, tr, cols), lambda i, s=slot: (s, i, 0)) for _, slot in contribs]
    operands = [w, m, v] + [arr for arr, _ in contribs]
    aliases = {}
    if prev is not None:
        aliases = {len(operands) + k: k for k in range(4)}
        in_specs += [pl.BlockSpec(memory_space=pl.ANY)] * 4
        operands += list(prev)
    shape = jax.ShapeDtypeStruct(w.shape, F32)
    return pl.pallas_call(
        body, name=name, grid=(rows // tr,), in_specs=in_specs, out_specs=[tile] * 4, out_shape=[shape] * 4,
        input_output_aliases=aliases,
        compiler_params=_params(dimension_semantics=("parallel",)),
    )(*operands)


def _adamw_small(w_pack, m_pack, v_pack, g_late, g_early, shapes):
    pieces, r = {}, 0
    for name, _, n in _SMALL_ROWS:
        pieces.setdefault(name, []).append((r, n))
        r += n
    names = list(pieces)

    def body(w_ref, m_ref, v_ref, gl_ref, ge_ref, *rest):
        outs, packs = rest[:4 * len(names)], rest[4 * len(names):]
        g_l, g_e = gl_ref[0][0:SMALL_LATE_ROWS], ge_ref[0]
        for d in range(1, N_DEV):
            g_l += gl_ref[d][0:SMALL_LATE_ROWS]
            g_e += ge_ref[d]
        g = jnp.concatenate([g_l, g_e], axis=0)
        w = w_ref[...]
        r0, r1, r2 = LB_ROW0, LB_ROW0 + 8, LB_ROW0 + 16
        lg0, lg1 = w[r0:r1], w[r1:r2]
        mx = jnp.maximum(lg0, lg1)
        e0, e1 = jnp.exp(lg0 - mx), jnp.exp(lg1 - mx)
        p0, p1 = e0 / (e0 + e1), e1 / (e0 + e1)
        low = ((p0 - p0), (p0 + p1) - p0)
        dlow = [g_rows * jnp.where((lo > 0.0) & (lo < 1.0), 1.0, jnp.where((lo == 0.0) | (lo == 1.0), 0.5, 0.0))
                for g_rows, lo in ((g[r0:r1], low[0]), (g[r1:r2], low[1]))]
        dp0 = (dlow[0] + dlow[1]) - (dlow[0] + dlow[1])
        dp1 = dlow[1]
        inner = p0 * dp0 + p1 * dp1
        g = jnp.concatenate([g[:r0], p0 * (dp0 - inner), p1 * (dp1 - inner), g[r2:]], axis=0)
        delta, mn, vn = _adamw_math(w, g, m_ref[...], v_ref[...])
        for kind, val in enumerate((g, delta, mn, vn)):
            packs[kind][...] = val
            for j, name in enumerate(names):
                out, at = outs[kind * len(names) + j], 0
                for start, n in pieces[name]:
                    if name in flat:
                        for r in range(n):
                            layer, c = divmod(at + r, flat[name])
                            out[layer:layer + 1, c * 128:(c + 1) * 128] = packs[kind][start + r:start + r + 1, :]
                    else:
                        out[at:at + n, :] = packs[kind][start:start + n, :]
                    at += n

    rows = {name: sum(n for _, n in pieces[name]) for name in names}
    flat = {name: rows[name] // DEPTH for name in names if len(shapes[name]) == 2}
    outs = pl.pallas_call(
        body, name="adamw_small",
        out_shape=[jax.ShapeDtypeStruct(shapes[name] if name in flat else (rows[name], 128), F32)
                   for _ in range(4) for name in names],
        scratch_shapes=[pltpu.VMEM(w_pack.shape, F32)] * 4, compiler_params=_params(),
    )(w_pack, m_pack, v_pack, g_late, g_early)
    return [{name: outs[kind * len(names) + j].reshape(shapes[name]) for j, name in enumerate(names)}
            for kind in range(4)]


def _pack_small(parts, first=0, last=len(_SMALL_ROWS)):
    rows = [(parts[name] if l is None else parts[name][l]).reshape(n, 128) for name, l, n in _SMALL_ROWS[first:last]]
    if last == len(_SMALL_ROWS):
        rows.append(jnp.zeros((SMALL_ROWS_PAD - sum(n for _, _, n in _SMALL_ROWS), 128), F32))
    return jnp.concatenate(rows, axis=0)


def kernel(x, c, w_ada, b_ada, g_pre, g_post, w_in, pool_w, pool_scale, lb_logits, hgrn_norm_g, w_pool_o, w_hgrn_o, w_out, loss_target, m_w_ada, m_b_ada, m_g_pre, m_g_post, m_w_in, m_pool_w, m_pool_scale, m_lb_logits, m_hgrn_norm_g, m_w_pool_o, m_w_hgrn_o, m_w_out, v_w_ada, v_b_ada, v_g_pre, v_g_post, v_w_in, v_pool_w, v_pool_scale, v_lb_logits, v_hgrn_norm_g, v_w_pool_o, v_w_hgrn_o, v_w_out):
    seq = x.shape[1]
    tm = min(512, seq)
    tm_merge = min(256, seq)
    pos = _my_position()
    me = pos[3]

    c_all = _allgather_small(c, "allgather_c").reshape(N_DEV, D_MODEL)
    b_cols = lax.dynamic_slice_in_dim(b_ada, me * ADA_COLS, ADA_COLS, axis=1)
    ada_part = _ada_fwd(c_all, w_ada, b_cols)
    ada_all = _allgather_small(ada_part.reshape(DEPTH * N_DEV, ADA_COLS), "allgather_ada")
    ada = lax.dynamic_index_in_dim(ada_all.reshape(N_DEV, DEPTH, N_DEV, ADA_COLS), me, axis=2, keepdims=False)
    ada = jnp.transpose(ada, (1, 0, 2)).reshape(DEPTH, 3 * D_MODEL)
    shift = [ada[l:l + 1, 0:D_MODEL] for l in range(DEPTH)]
    scale = [ada[l:l + 1, D_MODEL:2 * D_MODEL] for l in range(DEPTH)]
    gate = [ada[l:l + 1, 2 * D_MODEL:] for l in range(DEPTH)]

    big = dict(win=w_in, wpo=w_pool_o, who=w_hgrn_o, wout=w_out)
    units = [["win0"], ["wpo0", "who0", "wout0"], ["win1", "wpo1", "who1", "wout1"]]
    g_streams = [_gather_streams(keys) for keys in units]
    g_state = [None] * len(units)

    def gather_start(us, after):
        bufs = {}
        for k in [k for u in us for k in units[u]]:
            arr = big[k[:-1]]
            bufs["s_" + k] = arr[int(k[-1])].astype(WIRE_DTYPE)
            bufs["g_" + k] = _with_own_slot(bufs["s_" + k], me)
        bufs, sems, token = _comm_call("gather_start_" + "_".join(map(str, us)), bufs,
                                       start=[s for u in us for s in g_streams[u][:2]], after=after)
        for n, u in enumerate(us):
            g_state[u] = dict(bufs={p + k: bufs[p + k] for k in units[u] for p in ("s_", "g_")},
                              sems=sems[2 * n:2 * n + 2])
        return token

    def gather_pass(u, after):
        st = g_state[u]
        to_chips, _, pass_on = g_streams[u]
        st["bufs"], (st["pass_sems"],), _ = _comm_call(f"gather_pass_{u}", st["bufs"], start=[pass_on],
                                                       wait=[(to_chips, st["sems"][0])], after=after)

    def gather_done(u, after=None):
        st = g_state[u]
        _, to_sibling, pass_on = g_streams[u]
        bufs, _, _ = _comm_call(f"gather_done_{u}", st["bufs"], after=after,
                                wait=[(to_sibling, st["sems"][1]), (pass_on, st["pass_sems"])])
        return {k: bufs["g_" + k] for k in units[u]}

    token = gather_start([0], ada_all)

    lb = _lb_fwd(lb_logits)

    gw = {}
    xs, saved = [x[0]], []
    for l in range(DEPTH):
        h = _prenorm_fwd(xs[l], g_pre[l:l + 1], shift[l], scale[l], tm, f"prenorm_fwd_{l}",
                         after=token if l == 0 else None)
        token = None
        if l == 0:
            gather_pass(0, h)
            gw.update(gather_done(0))
            token = gather_start([1, 2], gw["win0"])
        else:
            gw.update(gather_done(2, h))
        z = _in_proj(h, gw[f"win{l}"], min(1024, seq), f"in_proj_{l}", after=token)
        a_in = _pool_fwd(z, pool_w[l], pool_scale[l:l + 1], f"pool_fwd_{l}")
        o, b_in, states, cum_base = _hgrn_fwd(z, lb[l:l + 1], hgrn_norm_g[l:l + 1], f"hgrn_fwd_{l}")
        if l == 0:
            gather_pass(1, b_in)
            gw.update(gather_done(1))
        who_l = gw[f"who{l}"].reshape(D_MODEL, D_MODEL)
        wout_l = gw[f"wout{l}"].reshape(D_MODEL, D_MODEL)
        last = l == DEPTH - 1
        ba, bb, merged, y, *out = _merge_fwd(a_in, b_in, z, xs[l], gw[f"wpo{l}"], who_l, wout_l, gate[l],
                                             g_post[l:l + 1], tm_merge, f"merge_fwd_{l}",
                                             target=loss_target[0] if last else None)
        if last:
            dx, loss_part = out
        else:
            xs.append(out[0])
            gather_pass(2, out[0])
        saved.append((h, z, a_in, o, b_in, states, cum_base, ba, bb, merged, y, who_l, wout_l))


    chips = _other_chips(pos)
    pair_idx = jnp.stack([_dev_index(cx, cy, pos[2]) for cx, cy in chips] + [me]).astype(jnp.int32)
    pair_rows = dict(win=256, wpo=POOL_WIDTH, who=HEAD_DIM, wout=HEAD_DIM)

    def scatter_pair_start(u, grads):
        keys = list(grads)
        pair, to_chips = _scatter_streams(keys)
        bufs = {}
        for k in keys:
            bufs["g_" + k] = grads[k]
            bufs["st_" + k] = lax.empty((4,) + grads[k].shape[1:], WIRE_DTYPE)
        bufs, (sems,), token = _comm_call(f"scatter_pair_start_{u}", bufs, start=[pair])
        return dict(u=u, keys=keys, pair=pair, to_chips=to_chips, bufs=bufs, sems=sems, token=token)

    def scatter_pair_finish(st, after):
        u, keys = st["u"], st["keys"]
        bufs, _, _ = _comm_call(f"scatter_pair_done_{u}", st["bufs"], wait=[(st["pair"], st["sems"])], after=after)
        bufs2 = {}
        for k in keys:
            bufs2["ps_" + k] = _pair_sum(bufs["g_" + k], bufs["st_" + k], pair_idx, bufs["g_" + k].shape[1],
                                         f"pair_sum_{k}")
            bufs2["ld_" + k] = lax.empty((3,) + bufs["g_" + k].shape[1:], WIRE_DTYPE)
        st.update(bufs=bufs2)

    def scatter_chips_start(st, after=None):
        bufs2, (sems,), token = _comm_call(f"scatter_chips_start_{st['u']}", st["bufs"], start=[st["to_chips"]],
                                           after=after)
        st.update(bufs=bufs2, sems=sems, token=token)

    def scatter_finish(st, after):
        bufs, _, _ = _comm_call(f"scatter_chips_done_{st['u']}", st["bufs"], wait=[(st["to_chips"], st["sems"])],
                                after=after)
        return {k: [(bufs["ps_" + k], 3), (bufs["ld_" + k], 0), (bufs["ld_" + k], 1), (bufs["ld_" + k], 2)]
                for k in st["keys"]}

    moments = dict(win=(m_w_in, v_w_in), wpo=(m_w_pool_o, v_w_pool_o), who=(m_w_hgrn_o, v_w_hgrn_o),
                   wout=(m_w_out, v_w_out))
    big_out = {}

    def finish_unit(unit, after):
        for k, contribs in scatter_finish(scat[unit], after).items():
            wname, l = k[:-1], int(k[-1])
            big_out[wname] = _adamw_layer(big[wname], moments[wname][0], moments[wname][1], contribs, l,
                                          pair_rows[wname], f"adamw_{k}", prev=big_out.get(wname))
            after = big_out[wname][0]
        return after

    d_ada, small, scat = [None] * DEPTH, [None] * DEPTH, {}
    for l in reversed(range(DEPTH)):
        h, z, a_in, o, b_in, states, cum_base, ba, bb, merged, y, who_l, wout_l = saved[l]
        dy, dba, dbb, da_in, db_in, dz, acc_post = _merge_bwd(
            dx, y, ba, bb, z, gw[f"wpo{l}"], who_l, wout_l, gate[l], g_post[l:l + 1],
            lax.empty((seq, IN_WIDTH), MXU_DTYPE), tm_merge, f"merge_bwd_{l}")
        g_out, g_ho, g_po = _grad_out_weights(merged, dy, b_in, dbb, a_in, dba, f"grad_out_weights_{l}")
        g_small = {f"wout{l}": g_out.reshape(N_DEV, HEAD_DIM, D_MODEL),
                   f"who{l}": g_ho.reshape(N_DEV, HEAD_DIM, D_MODEL), f"wpo{l}": g_po}
        st_small = scat["small0"] = scatter_pair_start("small0", g_small) if l == 0 else None
        dz, dlb, dgn = _hgrn_bwd(db_in, z, o, states, cum_base, lb[l:l + 1], hgrn_norm_g[l:l + 1], dz, f"hgrn_bwd_{l}",
                                 after=st_small and st_small["token"])
        if l == 0:
            scatter_pair_finish(st_small, dlb)
            scatter_chips_start(st_small)
        dz, dpw, dps = _pool_bwd(da_in, z, pool_w[l], pool_scale[l:l + 1], dz, f"pool_bwd_{l}",
                                 after=st_small and st_small["token"])
        small[l] = dict(g_post=acc_post[1], pool_w=dpw, pool_scale=dps[0], lb_logits=dlb[0], hgrn_norm_g=dgn[0])
        token = None
        if l == 0:
            parts = {name: jnp.stack([small[0][name], small[1][name]]) for name in small[0]}
            parts.update(b_ada=[None, d_ada[1]], g_pre=[None, small[1]["g_pre"]])
            sg_stream = _direct_gather_stream("sg")
            early = _pack_small(parts, 2)
            sg_bufs, (sg_sems,), token = _comm_call(
                "small_grads_start", dict(s_sg=early, g_sg=_with_own_slot(early, me)), start=[sg_stream])
        g_win = {f"win{l}": _in_proj_dw(h, dz, f"grad_w_in_{l}", after=token)}
        st_win = scat[f"win{l}"] = scatter_pair_start(f"win{l}", g_win if l == 0 else {**g_small, **g_win})
        if l > 0:
            dh = _in_proj_dh(dz, gw[f"win{l}"], seq, f"in_proj_dh_{l}", after=st_win["token"])
            scatter_pair_finish(st_win, dh)
            scatter_chips_start(st_win)
        else:
            scatter_pair_finish(st_win, st_win["token"])
            scatter_chips_start(st_win)
            after = st_win["token"]
            for unit in ("win1", "small0"):
                after = finish_unit(unit, after)
            dh = _in_proj_dh(dz, gw[f"win{l}"], seq, f"in_proj_dh_{l}", after=after)
        dx, acc_pre = _prenorm_bwd(xs[l], dh, dx, g_pre[l:l + 1], scale[l], tm, f"prenorm_bwd_{l}",
                                   after=st_win["token"])
        d_ada[l] = jnp.concatenate([acc_pre[0], acc_pre[1], acc_post[0]])
        small[l]["g_pre"] = acc_pre[2]
    grad_x = dx[None]

    parts = dict(b_ada=[d_ada[0]], g_pre=[small[0]["g_pre"]])
    late = jnp.concatenate([_pack_small(parts, 0, 2), jnp.broadcast_to(loss_part, (8, 128))], axis=0)
    g_late = _allgather_small(late, "allgather_late_grads")
    loss = jnp.sum(g_late[:, SMALL_LATE_ROWS, 0])
    sg_bufs, _, _ = _comm_call("small_grads_done", sg_bufs, wait=[(sg_stream, sg_sems)], after=g_late)
    g_early = sg_bufs["g_sg"]
    small_names = list(dict.fromkeys(name for name, _, _ in _SMALL_ROWS))
    weights = dict(b_ada=b_ada, g_pre=g_pre, g_post=g_post, pool_w=pool_w, pool_scale=pool_scale,
                   lb_logits=lb_logits, hgrn_norm_g=hgrn_norm_g)
    m_small = dict(b_ada=m_b_ada, g_pre=m_g_pre, g_post=m_g_post, pool_w=m_pool_w, pool_scale=m_pool_scale,
                   lb_logits=m_lb_logits, hgrn_norm_g=m_hgrn_norm_g)
    v_small = dict(b_ada=v_b_ada, g_pre=v_g_pre, g_post=v_g_post, pool_w=v_pool_w, pool_scale=v_pool_scale,
                   lb_logits=v_lb_logits, hgrn_norm_g=v_hgrn_norm_g)
    shapes = {name: weights[name].shape for name in small_names}
    small_out = _adamw_small(_pack_small(weights), _pack_small(m_small), _pack_small(v_small), g_late, g_early,
                             shapes)

    d_ada_all = jnp.stack([g_late[:, 0:24, :].reshape(N_DEV, 3 * D_MODEL),
                           g_early[:, 0:24, :].reshape(N_DEV, 3 * D_MODEL)], axis=1)
    d_cols = jnp.transpose(lax.dynamic_slice_in_dim(d_ada_all, me * ADA_COLS, ADA_COLS, axis=2), (1, 0, 2))
    g_w_ada = _ada_bwd(c_all, d_cols)
    ada_out = _adamw_sharded(w_ada, m_w_ada, v_w_ada, g_w_ada[:, None], 256, "adamw_w_ada")
    finish_unit("win0", ada_out[1][0, 0:8, 0:128] + small_out[1]["pool_scale"][0:1, 0:128])

    def leaf(kind):
        s = small_out[kind]
        return (ada_out[kind], s["b_ada"], s["g_pre"], s["g_post"], big_out["win"][kind], s["pool_w"], s["pool_scale"],
                s["lb_logits"], s["hgrn_norm_g"], big_out["wpo"][kind], big_out["who"][kind], big_out["wout"][kind])

    return (loss, grad_x) + leaf(0) + leaf(1) + leaf(2) + leaf(3)
```

```python
import jax
import jax.numpy as jnp
from jax import lax
from jax.experimental import pallas as pl
from jax.experimental.pallas import tpu as pltpu

F32 = jnp.float32
MXU_DTYPE = jnp.bfloat16
WIRE_DTYPE = jnp.bfloat16

N_DEV = 8
DEPTH = 2
D_MODEL = 1024
HEADS = 8
HEAD_DIM = 128
POOL_GROUPS = 4
GROUP_DIM = 128
POOL_WIDTH = POOL_GROUPS * GROUP_DIM
IN_WIDTH = 7168
CHUNK = 64
SUB = 16
N_SUB = CHUNK // SUB
FWD_STEP_CHUNKS = 8
BWD_STEP_CHUNKS = 4
EXP_CLAMP = 80.0
NORM_EPS = 1e-6
LOG_FLOOR = 1e-30
ADA_COLS = 3 * D_MODEL // N_DEV
IN_COLS = IN_WIDTH // N_DEV
COL_HQ, COL_HF, COL_HI, COL_HG, COL_MGP, COL_MGH = 1, 2, 3, 4, 5, 6

ADAM_LR = 0.001
ADAM_B1 = 0.9
ADAM_B2 = 0.999
ADAM_EPS = 1e-08
ADAM_WD = 0.01
ADAM_STEP = 10

VMEM_LIMIT = 48 * 1024 * 1024
MESH_ID = pl.DeviceIdType.MESH
HIGHEST = lax.Precision.HIGHEST

_SMALL_ROWS = (("b_ada", 0, 24), ("g_pre", 0, 8), ("b_ada", 1, 24), ("g_pre", 1, 8), ("g_post", None, 16),
               ("pool_w", None, 1024), ("pool_scale", None, 8), ("lb_logits", None, 16), ("hgrn_norm_g", None, 2))
SMALL_LATE_ROWS = 32
SMALL_ROWS_PAD = 1136
LB_ROW0 = 32 + 32 + 16 + 1024 + 8


def _params(**kw):
    return pltpu.CompilerParams(vmem_limit_bytes=VMEM_LIMIT, **kw)


def _sigmoid(v):
    return 1.0 / (1.0 + jnp.exp(-v))


def _dsilu(v, s):
    return s * (1.0 + v * (1.0 - s))


def _dot(a, b):
    return jnp.dot(a.astype(MXU_DTYPE), b.astype(MXU_DTYPE), preferred_element_type=F32)


def _dot_nt(a, b):
    return lax.dot_general(a.astype(MXU_DTYPE), b.astype(MXU_DTYPE), (((1,), (1,)), ((), ())),
                           preferred_element_type=F32)


def _dot_tn(a, b):
    return lax.dot_general(a.astype(MXU_DTYPE), b.astype(MXU_DTYPE), (((0,), (0,)), ((), ())),
                           preferred_element_type=F32)


def _pallas_after(body, n_in, after, *, in_specs, **kw):
    if after is None:
        return pl.pallas_call(body, in_specs=in_specs, **kw)

    def tied(*refs):
        body(*refs[:n_in], *refs[n_in + 1:])

    call = pl.pallas_call(tied, in_specs=list(in_specs) + [pl.BlockSpec(memory_space=pl.ANY)], **kw)
    return lambda *operands: call(*operands, after)


def _my_position():
    mx, my, mc = lax.axis_index("x"), lax.axis_index("y"), lax.axis_index("c")
    return mx, my, mc, 4 * mx + 2 * my + mc


def _peer(mx, my, mc, k):
    px = 1 - mx if (k >> 2) & 1 else mx
    py = 1 - my if (k >> 1) & 1 else my
    pc = 1 - mc if k & 1 else mc
    return (px, py, pc), 4 * px + 2 * py + pc


def _allgather_small(v, name, after=None):
    rows, cols = v.shape

    def body(v_ref, out_ref, send_sems, recv_sems):
        mx, my, mc, me = _my_position()
        out_ref[me] = v_ref[...]
        copies = []
        for k in range(1, N_DEV):
            peer, _ = _peer(mx, my, mc, k)
            cp = pltpu.make_async_remote_copy(
                src_ref=v_ref, dst_ref=out_ref.at[me],
                send_sem=send_sems.at[k - 1], recv_sem=recv_sems.at[k - 1],
                device_id=peer, device_id_type=MESH_ID)
            cp.start()
            copies.append(cp)
        for cp in copies:
            cp.wait()

    return _pallas_after(
        body, 1, after, name=name,
        out_shape=jax.ShapeDtypeStruct((N_DEV, rows, cols), v.dtype),
        in_specs=[pl.BlockSpec(memory_space=pltpu.VMEM)],
        out_specs=pl.BlockSpec(memory_space=pltpu.VMEM),
        scratch_shapes=[pltpu.SemaphoreType.DMA((N_DEV - 1,)), pltpu.SemaphoreType.DMA((N_DEV - 1,))],
        compiler_params=_params(),
    )(v)


class _Stream:
    def __init__(self, n, plan):
        self.n, self.plan = n, plan


def _comm_call(name, bufs, start=(), wait=(), after=None):
    names = list(bufs)

    def body(*refs):
        it = iter(refs)
        buf_refs = {n: next(it) for n in names}
        wait_sems = [(next(it), next(it)) for _ in wait]
        if after is not None:
            next(it)
        start_sems = [(next(it), next(it)) for _ in start]
        for _ in names:
            next(it)
        token = next(it)
        pos = _my_position()

        def descriptors(stream, sems):
            return [pltpu.make_async_remote_copy(src_ref=src, dst_ref=dst, send_sem=sems[0].at[k], recv_sem=sems[1].at[k],
                                                 device_id=dev, device_id_type=MESH_ID)
                    for k, (src, dst, dev) in enumerate(stream.plan(buf_refs, pos))]

        for (stream, _), sems in zip(wait, wait_sems):
            for cp in descriptors(stream, sems):
                cp.wait_send()
                cp.wait_recv()
        for stream, sems in zip(start, start_sems):
            for cp in descriptors(stream, sems):
                cp.start()
        token[...] = jnp.zeros_like(token)

    hbm = pl.BlockSpec(memory_space=pltpu.HBM)
    sem = pl.BlockSpec(memory_space=pltpu.SEMAPHORE)
    operands = [pltpu.with_memory_space_constraint(bufs[n], pltpu.HBM) for n in names]
    in_specs = [hbm] * len(names)
    for _, (send_sems, recv_sems) in wait:
        operands += [send_sems, recv_sems]
        in_specs += [sem, sem]
    if after is not None:
        operands.append(after)
        in_specs.append(pl.BlockSpec(memory_space=pl.ANY))
    out_shape, out_specs = [], []
    for stream in start:
        out_shape += [pltpu.SemaphoreType.DMA((stream.n,)), pltpu.SemaphoreType.DMA((stream.n,))]
        out_specs += [sem, sem]
    n_sem_out = len(out_shape)
    out_shape += [pltpu.HBM(bufs[n].shape, bufs[n].dtype) for n in names]
    out_specs += [hbm] * len(names)
    out_shape.append(jax.ShapeDtypeStruct((8, 128), F32))
    out_specs.append(pl.BlockSpec(memory_space=pltpu.VMEM))
    outs = pl.pallas_call(
        body, name=name, out_shape=out_shape, in_specs=in_specs, out_specs=out_specs,
        input_output_aliases={i: n_sem_out + i for i in range(len(names))},
        compiler_params=pltpu.CompilerParams(has_side_effects=pltpu.SideEffectType.DATAFLOW_SIDE_EFFECTING),
    )(*operands)
    sems = [(outs[2 * i], outs[2 * i + 1]) for i in range(len(start))]
    return dict(zip(names, outs[n_sem_out:n_sem_out + len(names)])), sems, outs[-1]


def _with_own_slot(block, me):
    return lax.dynamic_update_index_in_dim(lax.empty((N_DEV,) + block.shape, block.dtype), block, me, 0)


def _other_chips(pos):
    mx, my, _, _ = pos
    return [(1 - mx if i & 2 else mx, 1 - my if i & 1 else my) for i in (1, 2, 3)]


def _dev_index(px, py, pc):
    return 4 * px + 2 * py + pc


def _gather_streams(keys):
    def to_chips(refs, pos):
        _, _, mc, me = pos
        return [(refs["s_" + k], refs["g_" + k].at[me], (cx, cy, mc)) for k in keys for cx, cy in _other_chips(pos)]

    def to_sibling(refs, pos):
        mx, my, mc, me = pos
        return [(refs["s_" + k], refs["g_" + k].at[me], (mx, my, 1 - mc)) for k in keys]

    def pass_on(refs, pos):
        mx, my, mc, _ = pos
        out = []
        for k in keys:
            for cx, cy in _other_chips(pos):
                slot = refs["g_" + k].at[_dev_index(cx, cy, mc)]
                out.append((slot, slot, (mx, my, 1 - mc)))
        return out

    return _Stream(3 * len(keys), to_chips), _Stream(len(keys), to_sibling), _Stream(3 * len(keys), pass_on)


def _direct_gather_stream(key):
    def plan(refs, pos):
        mx, my, mc, me = pos
        return [(refs["s_" + key], refs["g_" + key].at[me], _peer(mx, my, mc, k)[0]) for k in range(1, N_DEV)]

    return _Stream(N_DEV - 1, plan)


def _scatter_streams(keys):
    def pair(refs, pos):
        mx, my, mc, _ = pos
        sib = (mx, my, 1 - mc)
        out = []
        for k in keys:
            for i, (cx, cy) in enumerate(_other_chips(pos)):
                out.append((refs["g_" + k].at[_dev_index(cx, cy, 1 - mc)], refs["st_" + k].at[i], sib))
            out.append((refs["g_" + k].at[_dev_index(mx, my, 1 - mc)], refs["st_" + k].at[3], sib))
        return out

    def chips(refs, pos):
        mc = pos[2]
        return [(refs["ps_" + k].at[i], refs["ld_" + k].at[i], (cx, cy, mc))
                for k in keys for i, (cx, cy) in enumerate(_other_chips(pos))]

    return _Stream(4 * len(keys), pair), _Stream(3 * len(keys), chips)


def _pair_sum(g, st, idx, tr, name):
    _, rows, cols = g.shape

    def body(idx_ref, g_ref, st_ref, out_ref):
        out_ref[...] = (g_ref[...].astype(F32) + st_ref[...].astype(F32)).astype(out_ref.dtype)

    return pl.pallas_call(
        body, name=name,
        grid_spec=pltpu.PrefetchScalarGridSpec(
            num_scalar_prefetch=1, grid=(4, rows // tr),
            in_specs=[pl.BlockSpec((None, tr, cols), lambda j, i, idx_ref: (idx_ref[j], i, 0)),
                      pl.BlockSpec((None, tr, cols), lambda j, i, idx_ref: (j, i, 0))],
            out_specs=pl.BlockSpec((None, tr, cols), lambda j, i, idx_ref: (j, i, 0))),
        out_shape=jax.ShapeDtypeStruct((4, rows, cols), WIRE_DTYPE),
        compiler_params=_params(dimension_semantics=("parallel", "parallel")),
    )(idx, g, st)


def _ada_fwd(c_all, w_ada, b_cols):
    def body(c_ref, w_ref, b_ref, out_ref):
        cv = c_ref[...]
        ca = cv * _sigmoid(cv)
        for l in range(DEPTH):
            out_ref[l] = jnp.dot(ca, w_ref[l], precision=HIGHEST, preferred_element_type=F32) + b_ref[l:l + 1, :]

    return pl.pallas_call(
        body, name="ada_fwd",
        out_shape=jax.ShapeDtypeStruct((DEPTH, N_DEV, ADA_COLS), F32),
        compiler_params=_params(),
    )(c_all, w_ada, b_cols)


def _ada_bwd(c_all, d_cols):
    def body(c_ref, d_ref, out_ref):
        cv = c_ref[...]
        ca = cv * _sigmoid(cv)
        for l in range(DEPTH):
            out_ref[l] = lax.dot_general(ca, d_ref[l], (((0,), (0,)), ((), ())), precision=HIGHEST,
                                         preferred_element_type=F32)

    return pl.pallas_call(
        body, name="ada_bwd",
        out_shape=jax.ShapeDtypeStruct((DEPTH, D_MODEL, ADA_COLS), F32),
        compiler_params=_params(),
    )(c_all, d_cols)


def _lower_bounds(logits):
    m = jnp.maximum(logits[0:1], logits[1:2])
    e0, e1 = jnp.exp(logits[0:1] - m), jnp.exp(logits[1:2] - m)
    den = e0 + e1
    p0, p1 = e0 / den, e1 / den
    low0 = p0 - p0
    low1 = (p0 + p1) - p0
    return (p0, p1), (low0, low1)


def _lb_fwd(lb_logits):
    def body(lg_ref, out_ref):
        _, (low0, low1) = _lower_bounds(lg_ref[...])
        out_ref[0:1, :] = jnp.clip(low0, 0.0, 1.0)
        out_ref[1:2, :] = jnp.clip(low1, 0.0, 1.0)

    return pl.pallas_call(body, name="lb_fwd", out_shape=jax.ShapeDtypeStruct(lb_logits.shape, F32),
                          compiler_params=_params())(lb_logits)


def _row_spec(cols=D_MODEL):
    return pl.BlockSpec((1, cols), lambda *_: (0, 0))


def _prenorm_fwd(x, g, shift, scale, tm, name, after=None):
    seq = x.shape[0]

    def body(x_ref, g_ref, sh_ref, sc_ref, h_ref):
        xv = x_ref[...]
        rs = lax.rsqrt(jnp.mean(xv * xv, axis=-1, keepdims=True) + NORM_EPS)
        h = (xv * rs * g_ref[...]) * (1.0 + sc_ref[...]) + sh_ref[...]
        h_ref[...] = h.astype(h_ref.dtype)

    tile = pl.BlockSpec((tm, D_MODEL), lambda i: (i, 0))
    return _pallas_after(
        body, 4, after, name=name, grid=(seq // tm,),
        in_specs=[tile, _row_spec(), _row_spec(), _row_spec()], out_specs=tile,
        out_shape=jax.ShapeDtypeStruct((seq, D_MODEL), MXU_DTYPE),
        compiler_params=_params(dimension_semantics=("parallel",)),
    )(x, g, shift, scale)


def _in_proj(h, win_g, tm, name, after=None):
    seq = h.shape[0]

    def body(h_ref, w_ref, z_ref, w_pair):
        @pl.when(pl.program_id(1) == 0)
        def _():
            w_pair[...] = jnp.concatenate([w_ref[0], w_ref[1]], axis=1)

        z_ref[...] = jnp.dot(h_ref[...], w_pair[...], preferred_element_type=F32)

    return _pallas_after(
        body, 2, after, name=name, grid=(N_DEV // 2, seq // tm),
        in_specs=[pl.BlockSpec((tm, D_MODEL), lambda j, i: (i, 0)),
                  pl.BlockSpec((2, D_MODEL, IN_COLS), lambda j, i: (j, 0, 0))],
        out_specs=pl.BlockSpec((tm, 2 * IN_COLS), lambda j, i: (i, j)),
        out_shape=jax.ShapeDtypeStruct((seq, IN_WIDTH), F32),
        scratch_shapes=[pltpu.VMEM((D_MODEL, 2 * IN_COLS), MXU_DTYPE)],
        compiler_params=_params(dimension_semantics=("parallel", "arbitrary")),
    )(h, win_g)


def _shift_down(v, j, pos):
    return jnp.where(pos >= j, pltpu.roll(v, j, 0), 0.0)


def _shift_up(v, j, pos, seq):
    return jnp.where(pos < seq - j, pltpu.roll(v, seq - j, 0), 0.0)


def _select_window(g, candidates):
    out = candidates[-1]
    for i in range(len(candidates) - 2, -1, -1):
        out = jnp.where(g == i, candidates[i], out)
    return out


def _pool_mean_minus_token(u, g, pos):
    sums, acc = [], u
    for j in (1, 2, 4, 8):
        acc = acc + _shift_down(acc, j, pos)
        sums.append(acc)
    wsum = _select_window(g, sums)
    width = jnp.left_shift(2, g).astype(F32)
    count = jnp.minimum(pos.astype(F32) + 1.0, width)
    return wsum / count - u, count


def _pool_fwd(z, pool_w_l, pool_scale_l, name, after=None):
    seq = z.shape[0]

    def body(pv_ref, pg_ref, w_ref, sc_ref, out_ref):
        g = pl.program_id(0)
        pos = lax.broadcasted_iota(jnp.int32, (seq, GROUP_DIM), 0)
        pm, _ = _pool_mean_minus_token(pv_ref[...], g, pos)
        lin = _dot(pm, w_ref[...]) * sc_ref[...]
        pg = pg_ref[...]
        out_ref[...] = (lin * (pg * _sigmoid(pg))).astype(out_ref.dtype)

    return _pallas_after(
        body, 4, after, name=name, grid=(POOL_GROUPS,),
        in_specs=[pl.BlockSpec((seq, GROUP_DIM), lambda g: (0, g)),
                  pl.BlockSpec((seq, GROUP_DIM), lambda g: (0, POOL_GROUPS + g)),
                  pl.BlockSpec((None, GROUP_DIM, GROUP_DIM), lambda g: (g, 0, 0)),
                  pl.BlockSpec((1, GROUP_DIM), lambda g: (0, g))],
        out_specs=pl.BlockSpec((seq, GROUP_DIM), lambda g: (0, g)),
        out_shape=jax.ShapeDtypeStruct((seq, POOL_WIDTH), MXU_DTYPE),
        compiler_params=_params(dimension_semantics=("parallel",)),
    )(z, z, pool_w_l, pool_scale_l)


def _chunk_masks():
    row = lax.broadcasted_iota(jnp.int32, (CHUNK, CHUNK), 0)
    col = lax.broadcasted_iota(jnp.int32, (CHUNK, CHUNK), 1)
    causal = row >= col
    before_sub = col < (row // SUB) * SUB
    suffix = row <= col
    return causal, before_sub, suffix


def _masked_sums(masks, v):
    lhs = jnp.concatenate([m.astype(jnp.bfloat16) for m in masks], axis=0)
    hi = v.astype(jnp.bfloat16)
    rest = v - hi.astype(F32)
    mid = rest.astype(jnp.bfloat16)
    lo = (rest - mid.astype(F32)).astype(jnp.bfloat16)
    out = jnp.dot(lhs, hi, preferred_element_type=F32)
    out += jnp.dot(lhs, mid, preferred_element_type=F32)
    out += jnp.dot(lhs, lo, preferred_element_type=F32)
    return [out[i * CHUNK:(i + 1) * CHUNK] for i in range(len(masks))]


def _gates(zf, lb):
    sg = _sigmoid(zf)
    f = lb + (1.0 - lb) * sg
    logf = jnp.log(jnp.maximum(f, LOG_FLOOR))
    return sg, f, logf


def _intra_blocks(q_h, k_h, cum_h, base_h, causal):
    rel = cum_h - base_h
    out = []
    for i in range(N_SUB):
        rows = slice(i * SUB, (i + 1) * SUB)
        e_q = jnp.exp(rel[rows])
        base_i = jnp.concatenate([base_h[rows]] * N_SUB, axis=0)
        e_k = jnp.exp(jnp.minimum(base_i - cum_h, EXP_CLAMP))
        q_t = (q_h[rows] * e_q).astype(MXU_DTYPE)
        k_t = (k_h * e_k).astype(MXU_DTYPE)
        a_i = jnp.where(causal[rows], _dot_nt(q_t, k_t), 0.0)
        out.append((q_t, k_t, e_q, e_k, a_i))
    return out


def _hgrn_fwd(z, lb_l, gn_l, name, after=None):
    seq = z.shape[0]
    n_chunks = seq // CHUNK
    per_step = min(FWD_STEP_CHUNKS, n_chunks)
    rows_per_step = per_step * CHUNK

    def body(hq_ref, hf_ref, hi_ref, hg_ref, lb_ref, gn_ref, o_ref, bin_ref, st_ref, cb_ref, state):
        @pl.when(pl.program_id(0) == 0)
        def _():
            state[...] = jnp.zeros_like(state)

        causal, before_sub, _ = _chunk_masks()
        for cc in range(per_step):
            rows = slice(cc * CHUNK, (cc + 1) * CHUNK)
            _, f, logf = _gates(hf_ref[rows, :], lb_ref[...])
            kk = 1.0 - f
            hq = hq_ref[rows, :]
            q = hq * _sigmoid(hq)
            cum, base = _masked_sums([causal, before_sub], logf)
            cb_ref[rows, 0:D_MODEL] = cum
            cb_ref[rows, D_MODEL:2 * D_MODEL] = base
            st_ref[cc] = state[...]
            for h in range(HEADS):
                sl = slice(h * HEAD_DIM, (h + 1) * HEAD_DIM)
                q_h, k_h, cum_h = q[:, sl], kk[:, sl], cum[:, sl]
                v_h = hi_ref[rows, sl]
                st_h = state[h]
                blocks = _intra_blocks(q_h, k_h, cum_h, base[:, sl], causal)
                a = jnp.concatenate([b[4] for b in blocks], axis=0)
                o_h = _dot_nt(q_h * jnp.exp(cum_h), st_h) + _dot(a, v_h)
                last = jnp.sum(logf[:, sl], axis=0, keepdims=True)
                state[h] = st_h * jnp.exp(last) + _dot_tn(v_h, k_h * jnp.exp(last - cum_h))
                rs = lax.rsqrt(jnp.mean(o_h * o_h, axis=-1, keepdims=True) + NORM_EPS)
                hg = hg_ref[rows, sl]
                o_ref[rows, sl] = o_h
                bin_ref[rows, sl] = ((o_h * rs * gn_ref[...]) * (hg * _sigmoid(hg))).astype(bin_ref.dtype)

    def col(block):
        return pl.BlockSpec((rows_per_step, D_MODEL), lambda c: (c, block))

    tile = pl.BlockSpec((rows_per_step, D_MODEL), lambda c: (c, 0))
    return _pallas_after(
        body, 6, after, name=name, grid=(n_chunks // per_step,),
        in_specs=[col(COL_HQ), col(COL_HF), col(COL_HI), col(COL_HG), _row_spec(), _row_spec(HEAD_DIM)],
        out_specs=[tile, tile, pl.BlockSpec((per_step, HEADS, HEAD_DIM, HEAD_DIM), lambda c: (c, 0, 0, 0)),
                   pl.BlockSpec((rows_per_step, 2 * D_MODEL), lambda c: (c, 0))],
        out_shape=[jax.ShapeDtypeStruct((seq, D_MODEL), F32),
                   jax.ShapeDtypeStruct((seq, D_MODEL), MXU_DTYPE),
                   jax.ShapeDtypeStruct((n_chunks, HEADS, HEAD_DIM, HEAD_DIM), F32),
                   jax.ShapeDtypeStruct((seq, 2 * D_MODEL), F32)],
        scratch_shapes=[pltpu.VMEM((HEADS, HEAD_DIM, HEAD_DIM), F32)],
        compiler_params=_params(dimension_semantics=("arbitrary",)),
    )(z, z, z, z, lb_l, gn_l)


def _rms_parts(y):
    rs = lax.rsqrt(jnp.mean(y * y, axis=-1, keepdims=True) + NORM_EPS)
    return rs, y * rs


def _merge_fwd(a_in, b_in, z, x, wpo_g, who_g, wout_g, gate, g_post, tm, name, target=None):
    seq = x.shape[0]
    with_loss = target is not None

    def body(*refs):
        a_ref, b_ref, mgp_ref, mgh_ref, x_ref, wpo_ref, who_ref, wout_ref, gate_ref, gp_ref = refs[:10]
        ba_ref, bb_ref, mer_ref, y_ref, last_ref = refs[10 + with_loss:15 + with_loss]
        a = a_ref[...]
        ba = jnp.concatenate([_dot(a, wpo_ref[j]) for j in range(N_DEV)], axis=1)
        bb = _dot(b_ref[...], who_ref[...])
        merged = _sigmoid(mgp_ref[...]) * ba + _sigmoid(mgh_ref[...]) * bb
        y = _dot(merged, wout_ref[...])
        _, yn = _rms_parts(y)
        ba_ref[...] = ba.astype(ba_ref.dtype)
        bb_ref[...] = bb.astype(bb_ref.dtype)
        mer_ref[...] = merged.astype(mer_ref.dtype)
        y_ref[...] = y.astype(y_ref.dtype)
        x_next = x_ref[...] + gate_ref[...] * (yn * gp_ref[...])
        if not with_loss:
            last_ref[...] = x_next
            return
        loss_ref = refs[16]

        @pl.when(pl.program_id(0) == 0)
        def _():
            loss_ref[...] = jnp.zeros_like(loss_ref)

        err = x_next - refs[10][...]
        loss_ref[...] += 0.5 * jnp.sum(jnp.mean(err * err, axis=-1, keepdims=True), axis=0, keepdims=True)
        last_ref[...] = err * (1.0 / D_MODEL)

    def tile(cols=D_MODEL, block=0):
        return pl.BlockSpec((tm, cols), lambda i: (i, block))

    full = pl.BlockSpec((D_MODEL, D_MODEL), lambda i: (0, 0))
    act = jax.ShapeDtypeStruct((seq, D_MODEL), MXU_DTYPE)
    f32 = jax.ShapeDtypeStruct((seq, D_MODEL), F32)
    one = [pl.BlockSpec((1, 1), lambda i: (0, 0))] if with_loss else []
    return pl.pallas_call(
        body, name=name, grid=(seq // tm,),
        in_specs=[tile(POOL_WIDTH), tile(), tile(block=COL_MGP), tile(block=COL_MGH), tile(),
                  pl.BlockSpec((N_DEV, POOL_WIDTH, GROUP_DIM), lambda i: (0, 0, 0)),
                  full, full, _row_spec(), _row_spec()] + ([tile()] if with_loss else []),
        out_specs=[tile(), tile(), tile(), tile(), tile()] + one,
        out_shape=[act, act, act, act, f32] + ([jax.ShapeDtypeStruct((1, 1), F32)] if with_loss else []),
        compiler_params=_params(dimension_semantics=("arbitrary" if with_loss else "parallel",)),
    )(a_in, b_in, z, z, x, wpo_g, who_g, wout_g, gate, g_post, *([target] if with_loss else []))


def _stage_copy(stage, sems, dst, slot, step, where):
    rows, cols = where(step)
    return pltpu.make_async_copy(stage.at[slot], dst.at[rows, cols], sems.at[slot])


def _stage_begin(stage, sems, dst, step, where):
    slot = step % 2

    @pl.when(step >= 2)
    def _():
        _stage_copy(stage, sems, dst, slot, step - 2, where).wait()

    return slot


def _stage_end(stage, sems, dst, step, n_steps, where):
    slot = step % 2
    _stage_copy(stage, sems, dst, slot, step, where).start()

    @pl.when(step == n_steps - 1)
    def _():
        _stage_copy(stage, sems, dst, slot, step, where).wait()
        if n_steps > 1:
            _stage_copy(stage, sems, dst, 1 - slot, step - 1, where).wait()


def _merge_bwd(dx, y, ba, bb, z, wpo_g, who_g, wout_g, gate, g_post, dz, tm, name):
    seq = dx.shape[0]
    n_steps = seq // tm

    def body(dx_ref, y_ref, ba_ref, bb_ref, mgp_ref, mgh_ref, wpo_ref, who_ref, wout_ref, gate_ref, gp_ref, _,
             dy_ref, dba_ref, dbb_ref, da_ref, db_ref, dz_ref, acc_ref, stage, sems):
        step = pl.program_id(0)

        @pl.when(step == 0)
        def _():
            acc_ref[...] = jnp.zeros_like(acc_ref)

        def where(t):
            return pl.ds(t * tm, tm), pl.ds(COL_MGP * D_MODEL, 2 * D_MODEL)

        dmg_ref = stage.at[_stage_begin(stage, sems, dz_ref, step, where)]

        dxv = dx_ref[...]
        rs, yn = _rms_parts(y_ref[...].astype(F32))
        acc_ref[0:1, :] += jnp.sum(dxv * yn * gp_ref[...], axis=0, keepdims=True)
        acc_ref[1:2, :] += jnp.sum(dxv * gate_ref[...] * yn, axis=0, keepdims=True)
        dyn = dxv * (gate_ref[...] * gp_ref[...])
        dy = rs * (dyn - yn * jnp.mean(dyn * yn, axis=-1, keepdims=True))
        dmerged = _dot_nt(dy, wout_ref[...])
        sp, sh = _sigmoid(mgp_ref[...]), _sigmoid(mgh_ref[...])
        dba, dbb = sp * dmerged, sh * dmerged
        dmg_ref[:, 0:D_MODEL] = (dmerged * ba_ref[...].astype(F32) * sp * (1.0 - sp)).astype(dmg_ref.dtype)
        dmg_ref[:, D_MODEL:2 * D_MODEL] = (dmerged * bb_ref[...].astype(F32) * sh * (1.0 - sh)).astype(dmg_ref.dtype)
        da = _dot_nt(dba[:, 0:GROUP_DIM], wpo_ref[0])
        for j in range(1, N_DEV):
            da += _dot_nt(dba[:, j * GROUP_DIM:(j + 1) * GROUP_DIM], wpo_ref[j])
        dy_ref[...] = dy.astype(dy_ref.dtype)
        dba_ref[...] = dba.astype(dba_ref.dtype)
        dbb_ref[...] = dbb.astype(dbb_ref.dtype)
        da_ref[...] = da.astype(da_ref.dtype)
        db_ref[...] = _dot_nt(dbb, who_ref[...]).astype(db_ref.dtype)
        _stage_end(stage, sems, dz_ref, step, n_steps, where)

    def tile(cols=D_MODEL, block=0):
        return pl.BlockSpec((tm, cols), lambda i: (i, block))

    full = pl.BlockSpec((D_MODEL, D_MODEL), lambda i: (0, 0))
    hbm = pl.BlockSpec(memory_space=pl.ANY)
    act = jax.ShapeDtypeStruct((seq, D_MODEL), MXU_DTYPE)
    return pl.pallas_call(
        body, name=name, grid=(n_steps,),
        in_specs=[tile(), tile(), tile(), tile(), tile(block=COL_MGP), tile(block=COL_MGH),
                  pl.BlockSpec((N_DEV, POOL_WIDTH, GROUP_DIM), lambda i: (0, 0, 0)),
                  full, full, _row_spec(), _row_spec(), hbm],
        out_specs=[tile(), tile(), tile(), tile(POOL_WIDTH), tile(), hbm,
                   pl.BlockSpec((8, D_MODEL), lambda i: (0, 0))],
        out_shape=[act, act, act, jax.ShapeDtypeStruct((seq, POOL_WIDTH), MXU_DTYPE), act,
                   jax.ShapeDtypeStruct(dz.shape, dz.dtype),
                   jax.ShapeDtypeStruct((8, D_MODEL), F32)],
        input_output_aliases={11: 5},
        scratch_shapes=[pltpu.VMEM((2, tm, 2 * D_MODEL), MXU_DTYPE), pltpu.SemaphoreType.DMA((2,))],
        compiler_params=_params(dimension_semantics=("arbitrary",)),
    )(dx, y, ba, bb, z, z, wpo_g, who_g, wout_g, gate, g_post, dz)


def _grad_out_weights(merged, dy, b_in, dbb, a_in, dba, name):
    seq = merged.shape[0]
    tn = D_MODEL // 2
    per_step = tn // GROUP_DIM

    def body(mer_ref, dy_ref, b_ref, dbb_ref, a_ref, dba_ref, gout_ref, gho_ref, gpo_ref):
        gout_ref[...] = _dot_tn(mer_ref[...], dy_ref[...]).astype(gout_ref.dtype)
        gho_ref[...] = _dot_tn(b_ref[...], dbb_ref[...]).astype(gho_ref.dtype)
        g_po = _dot_tn(a_ref[...], dba_ref[...])
        for j in range(per_step):
            gpo_ref[j] = g_po[:, j * GROUP_DIM:(j + 1) * GROUP_DIM].astype(gpo_ref.dtype)

    def whole(cols):
        return pl.BlockSpec((seq, cols), lambda j: (0, 0))

    cols = pl.BlockSpec((seq, tn), lambda j: (0, j))
    return pl.pallas_call(
        body, name=name, grid=(D_MODEL // tn,),
        in_specs=[whole(D_MODEL), cols, whole(D_MODEL), cols, whole(POOL_WIDTH), cols],
        out_specs=[pl.BlockSpec((D_MODEL, tn), lambda j: (0, j)), pl.BlockSpec((D_MODEL, tn), lambda j: (0, j)),
                   pl.BlockSpec((per_step, POOL_WIDTH, GROUP_DIM), lambda j: (j, 0, 0))],
        out_shape=[jax.ShapeDtypeStruct((D_MODEL, D_MODEL), WIRE_DTYPE),
                   jax.ShapeDtypeStruct((D_MODEL, D_MODEL), WIRE_DTYPE),
                   jax.ShapeDtypeStruct((N_DEV, POOL_WIDTH, GROUP_DIM), WIRE_DTYPE)],
        compiler_params=_params(dimension_semantics=("parallel",)),
    )(merged, dy, b_in, dbb, a_in, dba)


def _hgrn_bwd(db_in, z, o, states, cum_base, lb_l, gn_l, dz, name, after=None):
    seq = z.shape[0]
    per_step = min(BWD_STEP_CHUNKS, seq // CHUNK)
    rows_per_step = per_step * CHUNK
    n_steps = seq // rows_per_step
    last_step = n_steps - 1

    def body(db_ref, hq_ref, hf_ref, hi_ref, hg_ref, o_ref, st_ref, cb_ref, lb_ref, gn_ref, _,
             dz_hbm, dlb_ref, dgn_ref, dstate, dq_buf, dk_buf, dg_buf, stage, sems):
        step = pl.program_id(0)

        @pl.when(step == 0)
        def _():
            dstate[...] = jnp.zeros_like(dstate)
            dlb_ref[...] = jnp.zeros_like(dlb_ref)
            dgn_ref[...] = jnp.zeros_like(dgn_ref)

        def one_chunk(cc, *args):
            one_chunk_body((db_ref, hq_ref, hf_ref, hi_ref, hg_ref, o_ref, st_ref, cb_ref, dlb_ref, dgn_ref, dstate,
                            dq_buf, dk_buf, dg_buf), cc, *args)

        def where(t):
            return pl.ds((last_step - t) * rows_per_step, rows_per_step), pl.ds(COL_HQ * D_MODEL, 4 * D_MODEL)

        dz_step = stage.at[_stage_begin(stage, sems, dz_hbm, step, where)]
        causal, before_sub, suffix = _chunk_masks()
        lb = lb_ref[...]
        gn = gn_ref[...]
        for cc in reversed(range(per_step)):
            one_chunk(cc, dz_step, causal, before_sub, suffix, lb, gn)
        _stage_end(stage, sems, dz_hbm, step, n_steps, where)

    def one_chunk_body(refs, cc, dz_step, causal, before_sub, suffix, lb, gn):
        (db_ref, hq_ref, hf_ref, hi_ref, hg_ref, o_ref, st_ref, cb_ref, dlb_ref, dgn_ref, dstate,
         dq_buf, dk_buf, dg_buf) = refs
        rows = slice(cc * CHUNK, (cc + 1) * CHUNK)
        dz_ref = dz_step.at[rows, :]
        dq_buf, dk_buf, dg_buf = dq_buf.at[cc], dk_buf.at[cc], dg_buf.at[cc]
        sg, f, logf = _gates(hf_ref[rows, :], lb)
        kk = 1.0 - f
        hq = hq_ref[rows, :]
        sq = _sigmoid(hq)
        q = hq * sq
        cum, base = cb_ref[rows, 0:D_MODEL], cb_ref[rows, D_MODEL:2 * D_MODEL]
        dgn = jnp.zeros((1, HEAD_DIM), F32)
        dlast = []
        for h in range(HEADS):
            sl = slice(h * HEAD_DIM, (h + 1) * HEAD_DIM)
            q_h, k_h, cum_h = q[:, sl], kk[:, sl], cum[:, sl]
            v_h = hi_ref[rows, sl]
            st_h = st_ref[cc, h]
            dst_h = dstate[h]
            rs, ohat = _rms_parts(o_ref[rows, sl])
            hg = hg_ref[rows, sl]
            shg = _sigmoid(hg)
            d_bin = db_ref[rows, sl].astype(F32)
            don = d_bin * (hg * shg)
            dgn += jnp.sum(don * ohat, axis=0, keepdims=True)
            dohat = don * gn
            do = rs * (dohat - ohat * jnp.mean(dohat * ohat, axis=-1, keepdims=True))
            dz_ref[:, 3 * D_MODEL + h * HEAD_DIM:3 * D_MODEL + (h + 1) * HEAD_DIM] = (
                d_bin * (ohat * gn) * _dsilu(hg, shg)).astype(dz_ref.dtype)
            last = cb_ref[(cc + 1) * CHUNK - 1:(cc + 1) * CHUNK, sl]
            g_in = jnp.exp(cum_h)
            d_out = jnp.exp(last - cum_h)
            q_bar, k_bar = q_h * g_in, k_h * d_out
            blocks = _intra_blocks(q_h, k_h, cum_h, base[:, sl], causal)
            a = jnp.concatenate([b[4] for b in blocks], axis=0)
            da = jnp.where(causal, _dot_nt(do, v_h), 0.0)
            dv = _dot_tn(a, do) + _dot_nt(k_bar, dst_h)
            dq_bar, dk_bar = _dot(do, st_h), _dot(v_h, dst_h)
            dk = dk_bar * d_out
            dq_parts, dg_parts = [], []
            dg_k = k_bar * dk_bar
            dlast.append(jnp.sum(k_bar * dk_bar, axis=0, keepdims=True)
                         + jnp.exp(last) * jnp.sum(st_h * dst_h, axis=0, keepdims=True))
            for i, (q_t, k_t, e_q, e_k, _) in enumerate(blocks):
                da_i = da[i * SUB:(i + 1) * SUB].astype(MXU_DTYPE)
                dq_t = _dot(da_i, k_t)
                dk_t = _dot_tn(da_i, q_t)
                dq_parts.append(dq_t * e_q)
                dk += dk_t * e_k
                dg_parts.append(q_t.astype(F32) * dq_t)
                dg_k += k_t.astype(F32) * dk_t
            dq = dq_bar * g_in + jnp.concatenate(dq_parts, axis=0)
            dg_buf[:, sl] = q_bar * dq_bar + jnp.concatenate(dg_parts, axis=0) - dg_k
            dstate[h] = dst_h * jnp.exp(last) + _dot_tn(do, q_bar)
            dq_buf[:, sl] = dq
            dk_buf[:, sl] = dk
            dz_ref[:, 2 * D_MODEL + h * HEAD_DIM:2 * D_MODEL + (h + 1) * HEAD_DIM] = dv.astype(dz_ref.dtype)
        dgn_ref[...] += dgn
        dq_all, dk_all = dq_buf[...], dk_buf[...]
        dlogf = _masked_sums([suffix], dg_buf[...])[0] + jnp.concatenate(dlast, axis=1)
        df = jnp.where(f > LOG_FLOOR, dlogf / f, 0.0) - dk_all
        dlb_ref[...] += jnp.sum(df * (1.0 - sg), axis=0, keepdims=True)
        dz_ref[:, 0:D_MODEL] = (dq_all * _dsilu(hq, sq)).astype(dz_ref.dtype)
        dz_ref[:, D_MODEL:2 * D_MODEL] = (df * (1.0 - lb) * sg * (1.0 - sg)).astype(dz_ref.dtype)

    def col(block):
        return pl.BlockSpec((rows_per_step, D_MODEL), lambda c: (last_step - c, block))

    hbm = pl.BlockSpec(memory_space=pl.ANY)
    return _pallas_after(
        body, 11, after, name=name, grid=(n_steps,),
        in_specs=[col(0), col(COL_HQ), col(COL_HF), col(COL_HI), col(COL_HG), col(0),
                  pl.BlockSpec((per_step, HEADS, HEAD_DIM, HEAD_DIM), lambda c: (last_step - c, 0, 0, 0)),
                  pl.BlockSpec((rows_per_step, 2 * D_MODEL), lambda c: (last_step - c, 0)),
                  _row_spec(), _row_spec(HEAD_DIM), hbm],
        out_specs=[hbm, _row_spec(), _row_spec(HEAD_DIM)],
        out_shape=[jax.ShapeDtypeStruct(dz.shape, dz.dtype),
                   jax.ShapeDtypeStruct((1, D_MODEL), F32), jax.ShapeDtypeStruct((1, HEAD_DIM), F32)],
        input_output_aliases={10: 0},
        scratch_shapes=[pltpu.VMEM((HEADS, HEAD_DIM, HEAD_DIM), F32)]
        + [pltpu.VMEM((per_step, CHUNK, D_MODEL), F32)] * 3
        + [pltpu.VMEM((2, rows_per_step, 4 * D_MODEL), MXU_DTYPE), pltpu.SemaphoreType.DMA((2,))],
        compiler_params=_params(dimension_semantics=("arbitrary",)),
    )(db_in, z, z, z, z, o, states, cum_base, lb_l, gn_l, dz)


def _pool_bwd(da_in, z, pool_w_l, pool_scale_l, dz, name, after=None):
    seq = z.shape[0]

    def body(da_ref, pv_ref, pg_ref, w_ref, sc_ref, _, dz_hbm, dw_ref, dsc_ref, stage_pv, stage_pg, sems_pv, sems_pg):
        g = pl.program_id(0)

        def where_pv(t):
            return pl.ds(0, seq), pl.ds(pl.multiple_of(t * GROUP_DIM, GROUP_DIM), GROUP_DIM)

        def where_pg(t):
            return pl.ds(0, seq), pl.ds(pl.multiple_of(POOL_WIDTH + t * GROUP_DIM, GROUP_DIM), GROUP_DIM)

        dpv_ref = stage_pv.at[_stage_begin(stage_pv, sems_pv, dz_hbm, g, where_pv)]
        dpg_ref = stage_pg.at[_stage_begin(stage_pg, sems_pg, dz_hbm, g, where_pg)]
        pos = lax.broadcasted_iota(jnp.int32, (seq, GROUP_DIM), 0)
        pm, count = _pool_mean_minus_token(pv_ref[...], g, pos)
        lin0 = _dot(pm, w_ref[...])
        pg = pg_ref[...]
        spg = _sigmoid(pg)
        da = da_ref[...].astype(F32)
        dlin = da * (pg * spg)
        dpg_ref[...] = (da * (lin0 * sc_ref[...]) * _dsilu(pg, spg)).astype(dpg_ref.dtype)
        dsc_ref[...] = jnp.sum(dlin * lin0, axis=0, keepdims=True)
        dl0 = dlin * sc_ref[...]
        dw_ref[...] = _dot_tn(pm, dl0)
        dpm = _dot_nt(dl0, w_ref[...])
        sums, acc = [], dpm / count
        for j in (1, 2, 4, 8):
            acc = acc + _shift_up(acc, j, pos, seq)
            sums.append(acc)
        dpv_ref[...] = (_select_window(g, sums) - dpm).astype(dpv_ref.dtype)
        _stage_end(stage_pv, sems_pv, dz_hbm, g, POOL_GROUPS, where_pv)
        _stage_end(stage_pg, sems_pg, dz_hbm, g, POOL_GROUPS, where_pg)

    grp = pl.BlockSpec((seq, GROUP_DIM), lambda g: (0, g))
    hbm = pl.BlockSpec(memory_space=pl.ANY)
    stage = pltpu.VMEM((2, seq, GROUP_DIM), MXU_DTYPE)
    return _pallas_after(
        body, 6, after, name=name, grid=(POOL_GROUPS,),
        in_specs=[grp, grp, pl.BlockSpec((seq, GROUP_DIM), lambda g: (0, POOL_GROUPS + g)),
                  pl.BlockSpec((None, GROUP_DIM, GROUP_DIM), lambda g: (g, 0, 0)),
                  pl.BlockSpec((1, GROUP_DIM), lambda g: (0, g)), hbm],
        out_specs=[hbm, pl.BlockSpec((None, GROUP_DIM, GROUP_DIM), lambda g: (g, 0, 0)),
                   pl.BlockSpec((1, GROUP_DIM), lambda g: (0, g))],
        out_shape=[jax.ShapeDtypeStruct(dz.shape, dz.dtype),
                   jax.ShapeDtypeStruct((POOL_GROUPS, GROUP_DIM, GROUP_DIM), F32),
                   jax.ShapeDtypeStruct((1, POOL_WIDTH), F32)],
        input_output_aliases={5: 0},
        scratch_shapes=[stage, stage, pltpu.SemaphoreType.DMA((2,)), pltpu.SemaphoreType.DMA((2,))],
        compiler_params=_params(dimension_semantics=("arbitrary",)),
    )(da_in, z, z, pool_w_l, pool_scale_l, dz)


def _in_proj_dw(h, dz, name, after=None):
    seq = h.shape[0]

    def body(h_ref, dz_ref, out_ref):
        pair = lax.dot_general(h_ref[...], dz_ref[...], (((0,), (0,)), ((), ())), preferred_element_type=F32)
        out_ref[0] = pair[:, 0:IN_COLS].astype(out_ref.dtype)
        out_ref[1] = pair[:, IN_COLS:].astype(out_ref.dtype)

    return _pallas_after(
        body, 2, after, name=name, grid=(N_DEV // 2,),
        in_specs=[pl.BlockSpec((seq, D_MODEL), lambda j: (0, 0)),
                  pl.BlockSpec((seq, 2 * IN_COLS), lambda j: (0, j))],
        out_specs=pl.BlockSpec((2, D_MODEL, IN_COLS), lambda j: (j, 0, 0)),
        out_shape=jax.ShapeDtypeStruct((N_DEV, D_MODEL, IN_COLS), WIRE_DTYPE),
        compiler_params=_params(dimension_semantics=("parallel",)),
    )(h, dz)


def _in_proj_dh(dz, win_g, tm, name, after=None):
    seq = dz.shape[0]

    def body(dz_ref, w_ref, dh_ref):
        @pl.when(pl.program_id(1) == 0)
        def _():
            dh_ref[...] = jnp.zeros_like(dh_ref)

        w_pair = jnp.concatenate([w_ref[0], w_ref[1]], axis=1)
        dh_ref[...] += lax.dot_general(dz_ref[...], w_pair, (((1,), (1,)), ((), ())), preferred_element_type=F32)

    return _pallas_after(
        body, 2, after, name=name, grid=(seq // tm, N_DEV // 2),
        in_specs=[pl.BlockSpec((tm, 2 * IN_COLS), lambda i, j: (i, j)),
                  pl.BlockSpec((2, D_MODEL, IN_COLS), lambda i, j: (j, 0, 0))],
        out_specs=pl.BlockSpec((tm, D_MODEL), lambda i, j: (i, 0)),
        out_shape=jax.ShapeDtypeStruct((seq, D_MODEL), F32),
        compiler_params=_params(dimension_semantics=("parallel", "arbitrary")),
    )(dz, win_g)


def _prenorm_bwd(x, dh, dx_res, g, scale, tm, name, after=None):
    seq = x.shape[0]

    def body(x_ref, dh_ref, dxr_ref, g_ref, sc_ref, dx_ref, acc_ref):
        @pl.when(pl.program_id(0) == 0)
        def _():
            acc_ref[...] = jnp.zeros_like(acc_ref)

        rs, xn = _rms_parts(x_ref[...])
        dh = dh_ref[...]
        acc_ref[0:1, :] += jnp.sum(dh, axis=0, keepdims=True)
        acc_ref[1:2, :] += jnp.sum(dh * (xn * g_ref[...]), axis=0, keepdims=True)
        dhn = dh * (1.0 + sc_ref[...])
        acc_ref[2:3, :] += jnp.sum(dhn * xn, axis=0, keepdims=True)
        dxn = dhn * g_ref[...]
        dx_ref[...] = rs * (dxn - xn * jnp.mean(dxn * xn, axis=-1, keepdims=True)) + dxr_ref[...]

    tile = pl.BlockSpec((tm, D_MODEL), lambda i: (i, 0))
    return _pallas_after(
        body, 5, after, name=name, grid=(seq // tm,),
        in_specs=[tile, tile, tile, _row_spec(), _row_spec()],
        out_specs=[tile, pl.BlockSpec((8, D_MODEL), lambda i: (0, 0))],
        out_shape=[jax.ShapeDtypeStruct((seq, D_MODEL), F32), jax.ShapeDtypeStruct((8, D_MODEL), F32)],
        compiler_params=_params(dimension_semantics=("arbitrary",)),
    )(x, dh, dx_res, g, scale)


def _adamw_math(w, g, m, v):
    m = ADAM_B1 * m + (1.0 - ADAM_B1) * g
    v = ADAM_B2 * v + (1.0 - ADAM_B2) * (g * g)
    m_hat = m / (1.0 - ADAM_B1 ** ADAM_STEP)
    v_hat = v / (1.0 - ADAM_B2 ** ADAM_STEP)
    delta = -ADAM_LR * (m_hat / (jnp.sqrt(v_hat) + ADAM_EPS) + ADAM_WD * w)
    return delta, m, v


def _adamw_sharded(w, m, v, contrib, tr, name):
    depth, rows, cols = w.shape
    n_parts = contrib.shape[1]

    def body(w_ref, m_ref, v_ref, c_ref, g_ref, d_ref, mo_ref, vo_ref):
        g = c_ref[0].astype(F32)
        for p in range(1, n_parts):
            g += c_ref[p].astype(F32)
        delta, mn, vn = _adamw_math(w_ref[...], g, m_ref[...], v_ref[...])
        g_ref[...] = g
        d_ref[...] = delta
        mo_ref[...] = mn
        vo_ref[...] = vn

    tile = pl.BlockSpec((None, tr, cols), lambda l, i: (l, i, 0))
    shape = jax.ShapeDtypeStruct(w.shape, F32)
    return pl.pallas_call(
        body, name=name, grid=(depth, rows // tr),
        in_specs=[tile, tile, tile, pl.BlockSpec((None, n_parts, tr, cols), lambda l, i: (l, 0, i, 0))],
        out_specs=[tile] * 4, out_shape=[shape] * 4,
        compiler_params=_params(dimension_semantics=("parallel", "parallel")),
    )(w, m, v, contrib)


def _adamw_layer(w, m, v, contribs, l, tr, name, prev=None):
    _, rows, cols = w.shape
    n = len(contribs)

    def body(*refs):
        w_ref, m_ref, v_ref = refs[:3]
        c_refs = refs[3:3 + n]
        g_ref, d_ref, mo_ref, vo_ref = refs[-4:]
        g = c_refs[0][...].astype(F32)
        for c_ref in c_refs[1:]:
            g += c_ref[...].astype(F32)
        delta, mn, vn = _adamw_math(w_ref[...], g, m_ref[...], v_ref[...])
        g_ref[...] = g
        d_ref[...] = delta
        mo_ref[...] = mn
        vo_ref[...] = vn

    tile = pl.BlockSpec((None, tr, cols), lambda i: (l, i, 0))
    in_specs = [tile, tile, tile] + [pl.BlockSpec((None, tr, cols), lambda i, s=slot: (s, i, 0)) for _, slot in contribs]
    operands = [w, m, v] + [arr for arr, _ in contribs]
    aliases = {}
    if prev is not None:
        aliases = {len(operands) + k: k for k in range(4)}
        in_specs += [pl.BlockSpec(memory_space=pl.ANY)] * 4
        operands += list(prev)
    shape = jax.ShapeDtypeStruct(w.shape, F32)
    return pl.pallas_call(
        body, name=name, grid=(rows // tr,), in_specs=in_specs, out_specs=[tile] * 4, out_shape=[shape] * 4,
        input_output_aliases=aliases,
        compiler_params=_params(dimension_semantics=("parallel",)),
    )(*operands)


def _adamw_small(w_pack, m_pack, v_pack, g_late, g_early, shapes):
    pieces, r = {}, 0
    for name, _, n in _SMALL_ROWS:
        pieces.setdefault(name, []).append((r, n))
        r += n
    names = list(pieces)

    def body(w_ref, m_ref, v_ref, gl_ref, ge_ref, *rest):
        outs, packs = rest[:4 * len(names)], rest[4 * len(names):]
        g_l, g_e = gl_ref[0][0:SMALL_LATE_ROWS], ge_ref[0]
        for d in range(1, N_DEV):
            g_l += gl_ref[d][0:SMALL_LATE_ROWS]
            g_e += ge_ref[d]
        g = jnp.concatenate([g_l, g_e], axis=0)
        w = w_ref[...]
        r0, r1, r2 = LB_ROW0, LB_ROW0 + 8, LB_ROW0 + 16
        lg0, lg1 = w[r0:r1], w[r1:r2]
        mx = jnp.maximum(lg0, lg1)
        e0, e1 = jnp.exp(lg0 - mx), jnp.exp(lg1 - mx)
        p0, p1 = e0 / (e0 + e1), e1 / (e0 + e1)
        low = ((p0 - p0), (p0 + p1) - p0)
        dlow = [g_rows * jnp.where((lo > 0.0) & (lo < 1.0), 1.0, jnp.where((lo == 0.0) | (lo == 1.0), 0.5, 0.0))
                for g_rows, lo in ((g[r0:r1], low[0]), (g[r1:r2], low[1]))]
        dp0 = (dlow[0] + dlow[1]) - (dlow[0] + dlow[1])
        dp1 = dlow[1]
        inner = p0 * dp0 + p1 * dp1
        g = jnp.concatenate([g[:r0], p0 * (dp0 - inner), p1 * (dp1 - inner), g[r2:]], axis=0)
        delta, mn, vn = _adamw_math(w, g, m_ref[...], v_ref[...])
        for kind, val in enumerate((g, delta, mn, vn)):
            packs[kind][...] = val
            for j, name in enumerate(names):
                out, at = outs[kind * len(names) + j], 0
                for start, n in pieces[name]:
                    if name in flat:
                        for r in range(n):
                            layer, c = divmod(at + r, flat[name])
                            out[layer:layer + 1, c * 128:(c + 1) * 128] = packs[kind][start + r:start + r + 1, :]
                    else:
                        out[at:at + n, :] = packs[kind][start:start + n, :]
                    at += n

    rows = {name: sum(n for _, n in pieces[name]) for name in names}
    flat = {name: rows[name] // DEPTH for name in names if len(shapes[name]) == 2}
    outs = pl.pallas_call(
        body, name="adamw_small",
        out_shape=[jax.ShapeDtypeStruct(shapes[name] if name in flat else (rows[name], 128), F32)
                   for _ in range(4) for name in names],
        scratch_shapes=[pltpu.VMEM(w_pack.shape, F32)] * 4, compiler_params=_params(),
    )(w_pack, m_pack, v_pack, g_late, g_early)
    return [{name: outs[kind * len(names) + j].reshape(shapes[name]) for j, name in enumerate(names)}
            for kind in range(4)]


def _pack_small(parts, first=0, last=len(_SMALL_ROWS)):
    rows = [(parts[name] if l is None else parts[name][l]).reshape(n, 128) for name, l, n in _SMALL_ROWS[first:last]]
    if last == len(_SMALL_ROWS):
        rows.append(jnp.zeros((SMALL_ROWS_PAD - sum(n for _, _, n in _SMALL_ROWS), 128), F32))
    return jnp.concatenate(rows, axis=0)


def kernel(x, c, w_ada, b_ada, g_pre, g_post, w_in, pool_w, pool_scale, lb_logits, hgrn_norm_g, w_pool_o, w_hgrn_o, w_out, loss_target, m_w_ada, m_b_ada, m_g_pre, m_g_post, m_w_in, m_pool_w, m_pool_scale, m_lb_logits, m_hgrn_norm_g, m_w_pool_o, m_w_hgrn_o, m_w_out, v_w_ada, v_b_ada, v_g_pre, v_g_post, v_w_in, v_pool_w, v_pool_scale, v_lb_logits, v_hgrn_norm_g, v_w_pool_o, v_w_hgrn_o, v_w_out):
    seq = x.shape[1]
    tm = min(512, seq)
    tm_merge = min(512, seq)
    pos = _my_position()
    me = pos[3]

    c_all = _allgather_small(c, "allgather_c").reshape(N_DEV, D_MODEL)
    b_cols = lax.dynamic_slice_in_dim(b_ada, me * ADA_COLS, ADA_COLS, axis=1)
    ada_part = _ada_fwd(c_all, w_ada, b_cols)
    ada_all = _allgather_small(ada_part.reshape(DEPTH * N_DEV, ADA_COLS), "allgather_ada")
    ada = lax.dynamic_index_in_dim(ada_all.reshape(N_DEV, DEPTH, N_DEV, ADA_COLS), me, axis=2, keepdims=False)
    ada = jnp.transpose(ada, (1, 0, 2)).reshape(DEPTH, 3 * D_MODEL)
    shift = [ada[l:l + 1, 0:D_MODEL] for l in range(DEPTH)]
    scale = [ada[l:l + 1, D_MODEL:2 * D_MODEL] for l in range(DEPTH)]
    gate = [ada[l:l + 1, 2 * D_MODEL:] for l in range(DEPTH)]

    big = dict(win=w_in, wpo=w_pool_o, who=w_hgrn_o, wout=w_out)
    units = [["win0"], ["wpo0", "who0", "wout0"], ["win1", "wpo1", "who1", "wout1"]]
    g_streams = [_gather_streams(keys) for keys in units]
    g_state = [None] * len(units)

    def gather_start(us, after):
        bufs = {}
        for k in [k for u in us for k in units[u]]:
            arr = big[k[:-1]]
            bufs["s_" + k] = arr[int(k[-1])].astype(WIRE_DTYPE)
            bufs["g_" + k] = _with_own_slot(bufs["s_" + k], me)
        bufs, sems, token = _comm_call("gather_start_" + "_".join(map(str, us)), bufs,
                                       start=[s for u in us for s in g_streams[u][:2]], after=after)
        for n, u in enumerate(us):
            g_state[u] = dict(bufs={p + k: bufs[p + k] for k in units[u] for p in ("s_", "g_")},
                              sems=sems[2 * n:2 * n + 2])
        return token

    def gather_pass(u, after):
        st = g_state[u]
        to_chips, _, pass_on = g_streams[u]
        st["bufs"], (st["pass_sems"],), _ = _comm_call(f"gather_pass_{u}", st["bufs"], start=[pass_on],
                                                       wait=[(to_chips, st["sems"][0])], after=after)

    def gather_done(u, after=None):
        st = g_state[u]
        _, to_sibling, pass_on = g_streams[u]
        bufs, _, _ = _comm_call(f"gather_done_{u}", st["bufs"], after=after,
                                wait=[(to_sibling, st["sems"][1]), (pass_on, st["pass_sems"])])
        return {k: bufs["g_" + k] for k in units[u]}

    token = gather_start([0], ada_all)

    lb = _lb_fwd(lb_logits)

    gw = {}
    xs, saved = [x[0]], []
    for l in range(DEPTH):
        h = _prenorm_fwd(xs[l], g_pre[l:l + 1], shift[l], scale[l], tm, f"prenorm_fwd_{l}",
                         after=token if l == 0 else None)
        token = None
        if l == 0:
            gather_pass(0, h)
            gw.update(gather_done(0))
            token = gather_start([1, 2], gw["win0"])
        else:
            gw.update(gather_done(2, h))
        z = _in_proj(h, gw[f"win{l}"], min(1024, seq), f"in_proj_{l}", after=token)
        a_in = _pool_fwd(z, pool_w[l], pool_scale[l:l + 1], f"pool_fwd_{l}")
        o, b_in, states, cum_base = _hgrn_fwd(z, lb[l:l + 1], hgrn_norm_g[l:l + 1], f"hgrn_fwd_{l}")
        if l == 0:
            gather_pass(1, b_in)
            gw.update(gather_done(1))
        who_l = gw[f"who{l}"].reshape(D_MODEL, D_MODEL)
        wout_l = gw[f"wout{l}"].reshape(D_MODEL, D_MODEL)
        last = l == DEPTH - 1
        ba, bb, merged, y, *out = _merge_fwd(a_in, b_in, z, xs[l], gw[f"wpo{l}"], who_l, wout_l, gate[l],
                                             g_post[l:l + 1], tm_merge, f"merge_fwd_{l}",
                                             target=loss_target[0] if last else None)
        if last:
            dx, loss_part = out
        else:
            xs.append(out[0])
            gather_pass(2, out[0])
        saved.append((h, z, a_in, o, b_in, states, cum_base, ba, bb, merged, y, who_l, wout_l))


    chips = _other_chips(pos)
    pair_idx = jnp.stack([_dev_index(cx, cy, pos[2]) for cx, cy in chips] + [me]).astype(jnp.int32)
    pair_rows = dict(win=256, wpo=POOL_WIDTH, who=HEAD_DIM, wout=HEAD_DIM)

    def scatter_pair_start(u, grads):
        keys = list(grads)
        pair, to_chips = _scatter_streams(keys)
        bufs = {}
        for k in keys:
            bufs["g_" + k] = grads[k]
            bufs["st_" + k] = lax.empty((4,) + grads[k].shape[1:], WIRE_DTYPE)
        bufs, (sems,), token = _comm_call(f"scatter_pair_start_{u}", bufs, start=[pair])
        return dict(u=u, keys=keys, pair=pair, to_chips=to_chips, bufs=bufs, sems=sems, token=token)

    def scatter_pair_finish(st, after):
        u, keys = st["u"], st["keys"]
        bufs, _, _ = _comm_call(f"scatter_pair_done_{u}", st["bufs"], wait=[(st["pair"], st["sems"])], after=after)
        bufs2 = {}
        for k in keys:
            bufs2["ps_" + k] = _pair_sum(bufs["g_" + k], bufs["st_" + k], pair_idx, bufs["g_" + k].shape[1],
                                         f"pair_sum_{k}")
            bufs2["ld_" + k] = lax.empty((3,) + bufs["g_" + k].shape[1:], WIRE_DTYPE)
        st.update(bufs=bufs2)

    def scatter_chips_start(st, after=None):
        bufs2, (sems,), token = _comm_call(f"scatter_chips_start_{st['u']}", st["bufs"], start=[st["to_chips"]],
                                           after=after)
        st.update(bufs=bufs2, sems=sems, token=token)

    def scatter_finish(st, after):
        bufs, _, _ = _comm_call(f"scatter_chips_done_{st['u']}", st["bufs"], wait=[(st["to_chips"], st["sems"])],
                                after=after)
        return {k: [(bufs["ps_" + k], 3), (bufs["ld_" + k], 0), (bufs["ld_" + k], 1), (bufs["ld_" + k], 2)]
                for k in st["keys"]}

    moments = dict(win=(m_w_in, v_w_in), wpo=(m_w_pool_o, v_w_pool_o), who=(m_w_hgrn_o, v_w_hgrn_o),
                   wout=(m_w_out, v_w_out))
    big_out = {}

    def finish_unit(unit, after):
        for k, contribs in scatter_finish(scat[unit], after).items():
            wname, l = k[:-1], int(k[-1])
            big_out[wname] = _adamw_layer(big[wname], moments[wname][0], moments[wname][1], contribs, l,
                                          pair_rows[wname], f"adamw_{k}", prev=big_out.get(wname))
            after = big_out[wname][0]
        return after

    d_ada, small, scat = [None] * DEPTH, [None] * DEPTH, {}
    for l in reversed(range(DEPTH)):
        h, z, a_in, o, b_in, states, cum_base, ba, bb, merged, y, who_l, wout_l = saved[l]
        dy, dba, dbb, da_in, db_in, dz, acc_post = _merge_bwd(
            dx, y, ba, bb, z, gw[f"wpo{l}"], who_l, wout_l, gate[l], g_post[l:l + 1],
            lax.empty((seq, IN_WIDTH), MXU_DTYPE), tm_merge, f"merge_bwd_{l}")
        g_out, g_ho, g_po = _grad_out_weights(merged, dy, b_in, dbb, a_in, dba, f"grad_out_weights_{l}")
        g_small = {f"wout{l}": g_out.reshape(N_DEV, HEAD_DIM, D_MODEL),
                   f"who{l}": g_ho.reshape(N_DEV, HEAD_DIM, D_MODEL), f"wpo{l}": g_po}
        st_small = scat["small0"] = scatter_pair_start("small0", g_small) if l == 0 else None
        dz, dlb, dgn = _hgrn_bwd(db_in, z, o, states, cum_base, lb[l:l + 1], hgrn_norm_g[l:l + 1], dz, f"hgrn_bwd_{l}",
                                 after=st_small and st_small["token"])
        if l == 0:
            scatter_pair_finish(st_small, dlb)
            scatter_chips_start(st_small)
        dz, dpw, dps = _pool_bwd(da_in, z, pool_w[l], pool_scale[l:l + 1], dz, f"pool_bwd_{l}",
                                 after=st_small and st_small["token"])
        small[l] = dict(g_post=acc_post[1], pool_w=dpw, pool_scale=dps[0], lb_logits=dlb[0], hgrn_norm_g=dgn[0])
        token = None
        if l == 0:
            parts = {name: jnp.stack([small[0][name], small[1][name]]) for name in small[0]}
            parts.update(b_ada=[None, d_ada[1]], g_pre=[None, small[1]["g_pre"]])
            sg_stream = _direct_gather_stream("sg")
            early = _pack_small(parts, 2)
            sg_bufs, (sg_sems,), token = _comm_call(
                "small_grads_start", dict(s_sg=early, g_sg=_with_own_slot(early, me)), start=[sg_stream])
        g_win = {f"win{l}": _in_proj_dw(h, dz, f"grad_w_in_{l}", after=token)}
        st_win = scat[f"win{l}"] = scatter_pair_start(f"win{l}", g_win if l == 0 else {**g_small, **g_win})
        if l > 0:
            dh = _in_proj_dh(dz, gw[f"win{l}"], seq, f"in_proj_dh_{l}", after=st_win["token"])
            scatter_pair_finish(st_win, dh)
            scatter_chips_start(st_win)
        else:
            scatter_pair_finish(st_win, st_win["token"])
            scatter_chips_start(st_win)
            after = st_win["token"]
            for unit in ("win1", "small0"):
                after = finish_unit(unit, after)
            dh = _in_proj_dh(dz, gw[f"win{l}"], seq, f"in_proj_dh_{l}", after=after)
        dx, acc_pre = _prenorm_bwd(xs[l], dh, dx, g_pre[l:l + 1], scale[l], tm, f"prenorm_bwd_{l}",
                                   after=st_win["token"])
        d_ada[l] = jnp.concatenate([acc_pre[0], acc_pre[1], acc_post[0]])
        small[l]["g_pre"] = acc_pre[2]
    grad_x = dx[None]

    parts = dict(b_ada=[d_ada[0]], g_pre=[small[0]["g_pre"]])
    late = jnp.concatenate([_pack_small(parts, 0, 2), jnp.broadcast_to(loss_part, (8, 128))], axis=0)
    g_late = _allgather_small(late, "allgather_late_grads")
    loss = jnp.sum(g_late[:, SMALL_LATE_ROWS, 0])
    sg_bufs, _, _ = _comm_call("small_grads_done", sg_bufs, wait=[(sg_stream, sg_sems)], after=g_late)
    g_early = sg_bufs["g_sg"]
    small_names = list(dict.fromkeys(name for name, _, _ in _SMALL_ROWS))
    weights = dict(b_ada=b_ada, g_pre=g_pre, g_post=g_post, pool_w=pool_w, pool_scale=pool_scale,
                   lb_logits=lb_logits, hgrn_norm_g=hgrn_norm_g)
    m_small = dict(b_ada=m_b_ada, g_pre=m_g_pre, g_post=m_g_post, pool_w=m_pool_w, pool_scale=m_pool_scale,
                   lb_logits=m_lb_logits, hgrn_norm_g=m_hgrn_norm_g)
    v_small = dict(b_ada=v_b_ada, g_pre=v_g_pre, g_post=v_g_post, pool_w=v_pool_w, pool_scale=v_pool_scale,
                   lb_logits=v_lb_logits, hgrn_norm_g=v_hgrn_norm_g)
    shapes = {name: weights[name].shape for name in small_names}
    small_out = _adamw_small(_pack_small(weights), _pack_small(m_small), _pack_small(v_small), g_late, g_early,
                             shapes)

    d_ada_all = jnp.stack([g_late[:, 0:24, :].reshape(N_DEV, 3 * D_MODEL),
                           g_early[:, 0:24, :].reshape(N_DEV, 3 * D_MODEL)], axis=1)
    d_cols = jnp.transpose(lax.dynamic_slice_in_dim(d_ada_all, me * ADA_COLS, ADA_COLS, axis=2), (1, 0, 2))
    g_w_ada = _ada_bwd(c_all, d_cols)
    ada_out = _adamw_sharded(w_ada, m_w_ada, v_w_ada, g_w_ada[:, None], 256, "adamw_w_ada")
    finish_unit("win0", ada_out[1][0, 0:8, 0:128] + small_out[1]["pool_scale"][0:1, 0:128])

    def leaf(kind):
        s = small_out[kind]
        return (ada_out[kind], s["b_ada"], s["g_pre"], s["g_post"], big_out["win"][kind], s["pool_w"], s["pool_scale"],
                s["lb_logits"], s["hgrn_norm_g"], big_out["wpo"][kind], big_out["who"][kind], big_out["wout"][kind])

    return (loss, grad_x) + leaf(0) + leaf(1) + leaf(2) + leaf(3)
```

```python
import jax
import jax.numpy as jnp
from jax import lax
from jax.experimental import pallas as pl
from jax.experimental.pallas import tpu as pltpu

F32 = jnp.float32
MXU_DTYPE = jnp.bfloat16
WIRE_DTYPE = jnp.bfloat16

N_DEV = 8
DEPTH = 2
D_MODEL = 1024
HEADS = 8
HEAD_DIM = 128
POOL_GROUPS = 4
GROUP_DIM = 128
POOL_WIDTH = POOL_GROUPS * GROUP_DIM
IN_WIDTH = 7168
CHUNK = 64
SUB = 16
N_SUB = CHUNK // SUB
FWD_STEP_CHUNKS = 8
BWD_STEP_CHUNKS = 4
EXP_CLAMP = 80.0
NORM_EPS = 1e-6
LOG_FLOOR = 1e-30
ADA_COLS = 3 * D_MODEL // N_DEV
IN_COLS = IN_WIDTH // N_DEV
COL_HQ, COL_HF, COL_HI, COL_HG, COL_MGP, COL_MGH = 1, 2, 3, 4, 5, 6

ADAM_LR = 0.001
ADAM_B1 = 0.9
ADAM_B2 = 0.999
ADAM_EPS = 1e-08
ADAM_WD = 0.01
ADAM_STEP = 10

VMEM_LIMIT = 48 * 1024 * 1024
MESH_ID = pl.DeviceIdType.MESH
HIGHEST = lax.Precision.HIGHEST

_SMALL_ROWS = (("b_ada", 0, 24), ("g_pre", 0, 8), ("b_ada", 1, 24), ("g_pre", 1, 8), ("g_post", None, 16),
               ("pool_w", None, 1024), ("pool_scale", None, 8), ("lb_logits", None, 16), ("hgrn_norm_g", None, 2))
SMALL_LATE_ROWS = 32
SMALL_ROWS_PAD = 1136
LB_ROW0 = 32 + 32 + 16 + 1024 + 8


def _params(**kw):
    return pltpu.CompilerParams(vmem_limit_bytes=VMEM_LIMIT, **kw)


def _sigmoid(v):
    return 1.0 / (1.0 + jnp.exp(-v))


def _dsilu(v, s):
    return s * (1.0 + v * (1.0 - s))


def _dot(a, b):
    return jnp.dot(a.astype(MXU_DTYPE), b.astype(MXU_DTYPE), preferred_element_type=F32)


def _dot_nt(a, b):
    return lax.dot_general(a.astype(MXU_DTYPE), b.astype(MXU_DTYPE), (((1,), (1,)), ((), ())),
                           preferred_element_type=F32)


def _dot_tn(a, b):
    return lax.dot_general(a.astype(MXU_DTYPE), b.astype(MXU_DTYPE), (((0,), (0,)), ((), ())),
                           preferred_element_type=F32)


def _pallas_after(body, n_in, after, *, in_specs, **kw):
    if after is None:
        return pl.pallas_call(body, in_specs=in_specs, **kw)

    def tied(*refs):
        body(*refs[:n_in], *refs[n_in + 1:])

    call = pl.pallas_call(tied, in_specs=list(in_specs) + [pl.BlockSpec(memory_space=pl.ANY)], **kw)
    return lambda *operands: call(*operands, after)


def _my_position():
    mx, my, mc = lax.axis_index("x"), lax.axis_index("y"), lax.axis_index("c")
    return mx, my, mc, 4 * mx + 2 * my + mc


def _peer(mx, my, mc, k):
    px = 1 - mx if (k >> 2) & 1 else mx
    py = 1 - my if (k >> 1) & 1 else my
    pc = 1 - mc if k & 1 else mc
    return (px, py, pc), 4 * px + 2 * py + pc


def _allgather_small(v, name, after=None):
    rows, cols = v.shape

    def body(v_ref, out_ref, send_sems, recv_sems):
        mx, my, mc, me = _my_position()
        out_ref[me] = v_ref[...]
        copies = []
        for k in range(1, N_DEV):
            peer, _ = _peer(mx, my, mc, k)
            cp = pltpu.make_async_remote_copy(
                src_ref=v_ref, dst_ref=out_ref.at[me],
                send_sem=send_sems.at[k - 1], recv_sem=recv_sems.at[k - 1],
                device_id=peer, device_id_type=MESH_ID)
            cp.start()
            copies.append(cp)
        for cp in copies:
            cp.wait()

    return _pallas_after(
        body, 1, after, name=name,
        out_shape=jax.ShapeDtypeStruct((N_DEV, rows, cols), v.dtype),
        in_specs=[pl.BlockSpec(memory_space=pltpu.VMEM)],
        out_specs=pl.BlockSpec(memory_space=pltpu.VMEM),
        scratch_shapes=[pltpu.SemaphoreType.DMA((N_DEV - 1,)), pltpu.SemaphoreType.DMA((N_DEV - 1,))],
        compiler_params=_params(),
    )(v)


class _Stream:
    def __init__(self, n, plan):
        self.n, self.plan = n, plan


def _comm_call(name, bufs, start=(), wait=(), after=None):
    names = list(bufs)

    def body(*refs):
        it = iter(refs)
        buf_refs = {n: next(it) for n in names}
        wait_sems = [(next(it), next(it)) for _ in wait]
        if after is not None:
            next(it)
        start_sems = [(next(it), next(it)) for _ in start]
        for _ in names:
            next(it)
        token = next(it)
        pos = _my_position()

        def descriptors(stream, sems):
            return [pltpu.make_async_remote_copy(src_ref=src, dst_ref=dst, send_sem=sems[0].at[k], recv_sem=sems[1].at[k],
                                                 device_id=dev, device_id_type=MESH_ID)
                    for k, (src, dst, dev) in enumerate(stream.plan(buf_refs, pos))]

        for (stream, _), sems in zip(wait, wait_sems):
            for cp in descriptors(stream, sems):
                cp.wait_send()
                cp.wait_recv()
        for stream, sems in zip(start, start_sems):
            for cp in descriptors(stream, sems):
                cp.start()
        token[...] = jnp.zeros_like(token)

    hbm = pl.BlockSpec(memory_space=pltpu.HBM)
    sem = pl.BlockSpec(memory_space=pltpu.SEMAPHORE)
    operands = [pltpu.with_memory_space_constraint(bufs[n], pltpu.HBM) for n in names]
    in_specs = [hbm] * len(names)
    for _, (send_sems, recv_sems) in wait:
        operands += [send_sems, recv_sems]
        in_specs += [sem, sem]
    if after is not None:
        operands.append(after)
        in_specs.append(pl.BlockSpec(memory_space=pl.ANY))
    out_shape, out_specs = [], []
    for stream in start:
        out_shape += [pltpu.SemaphoreType.DMA((stream.n,)), pltpu.SemaphoreType.DMA((stream.n,))]
        out_specs += [sem, sem]
    n_sem_out = len(out_shape)
    out_shape += [pltpu.HBM(bufs[n].shape, bufs[n].dtype) for n in names]
    out_specs += [hbm] * len(names)
    out_shape.append(jax.ShapeDtypeStruct((8, 128), F32))
    out_specs.append(pl.BlockSpec(memory_space=pltpu.VMEM))
    outs = pl.pallas_call(
        body, name=name, out_shape=out_shape, in_specs=in_specs, out_specs=out_specs,
        input_output_aliases={i: n_sem_out + i for i in range(len(names))},
        compiler_params=pltpu.CompilerParams(has_side_effects=pltpu.SideEffectType.DATAFLOW_SIDE_EFFECTING),
    )(*operands)
    sems = [(outs[2 * i], outs[2 * i + 1]) for i in range(len(start))]
    return dict(zip(names, outs[n_sem_out:n_sem_out + len(names)])), sems, outs[-1]


def _with_own_slot(block, me):
    return lax.dynamic_update_index_in_dim(lax.empty((N_DEV,) + block.shape, block.dtype), block, me, 0)


def _other_chips(pos):
    mx, my, _, _ = pos
    return [(1 - mx if i & 2 else mx, 1 - my if i & 1 else my) for i in (1, 2, 3)]


def _dev_index(px, py, pc):
    return 4 * px + 2 * py + pc


def _gather_streams(keys):
    def to_chips(refs, pos):
        _, _, mc, me = pos
        return [(refs["s_" + k], refs["g_" + k].at[me], (cx, cy, mc)) for k in keys for cx, cy in _other_chips(pos)]

    def to_sibling(refs, pos):
        mx, my, mc, me = pos
        return [(refs["s_" + k], refs["g_" + k].at[me], (mx, my, 1 - mc)) for k in keys]

    def pass_on(refs, pos):
        mx, my, mc, _ = pos
        out = []
        for k in keys:
            for cx, cy in _other_chips(pos):
                slot = refs["g_" + k].at[_dev_index(cx, cy, mc)]
                out.append((slot, slot, (mx, my, 1 - mc)))
        return out

    return _Stream(3 * len(keys), to_chips), _Stream(len(keys), to_sibling), _Stream(3 * len(keys), pass_on)


def _direct_gather_stream(key):
    def plan(refs, pos):
        mx, my, mc, me = pos
        return [(refs["s_" + key], refs["g_" + key].at[me], _peer(mx, my, mc, k)[0]) for k in range(1, N_DEV)]

    return _Stream(N_DEV - 1, plan)


def _scatter_streams(keys):
    def pair(refs, pos):
        mx, my, mc, _ = pos
        sib = (mx, my, 1 - mc)
        out = []
        for k in keys:
            for i, (cx, cy) in enumerate(_other_chips(pos)):
                out.append((refs["g_" + k].at[_dev_index(cx, cy, 1 - mc)], refs["st_" + k].at[i], sib))
            out.append((refs["g_" + k].at[_dev_index(mx, my, 1 - mc)], refs["st_" + k].at[3], sib))
        return out

    def chips(refs, pos):
        mc = pos[2]
        return [(refs["ps_" + k].at[i], refs["ld_" + k].at[i], (cx, cy, mc))
                for k in keys for i, (cx, cy) in enumerate(_other_chips(pos))]

    return _Stream(4 * len(keys), pair), _Stream(3 * len(keys), chips)


def _pair_sum(g, st, idx, tr, name):
    _, rows, cols = g.shape

    def body(idx_ref, g_ref, st_ref, out_ref):
        out_ref[...] = (g_ref[...].astype(F32) + st_ref[...].astype(F32)).astype(out_ref.dtype)

    return pl.pallas_call(
        body, name=name,
        grid_spec=pltpu.PrefetchScalarGridSpec(
            num_scalar_prefetch=1, grid=(4, rows // tr),
            in_specs=[pl.BlockSpec((None, tr, cols), lambda j, i, idx_ref: (idx_ref[j], i, 0)),
                      pl.BlockSpec((None, tr, cols), lambda j, i, idx_ref: (j, i, 0))],
            out_specs=pl.BlockSpec((None, tr, cols), lambda j, i, idx_ref: (j, i, 0))),
        out_shape=jax.ShapeDtypeStruct((4, rows, cols), WIRE_DTYPE),
        compiler_params=_params(dimension_semantics=("parallel", "parallel")),
    )(idx, g, st)


def _ada_fwd(c_all, w_ada, b_cols):
    def body(c_ref, w_ref, b_ref, out_ref):
        cv = c_ref[...]
        ca = cv * _sigmoid(cv)
        for l in range(DEPTH):
            out_ref[l] = jnp.dot(ca, w_ref[l], precision=HIGHEST, preferred_element_type=F32) + b_ref[l:l + 1, :]

    return pl.pallas_call(
        body, name="ada_fwd",
        out_shape=jax.ShapeDtypeStruct((DEPTH, N_DEV, ADA_COLS), F32),
        compiler_params=_params(),
    )(c_all, w_ada, b_cols)


def _ada_bwd(c_all, d_cols):
    def body(c_ref, d_ref, out_ref):
        cv = c_ref[...]
        ca = cv * _sigmoid(cv)
        for l in range(DEPTH):
            out_ref[l] = lax.dot_general(ca, d_ref[l], (((0,), (0,)), ((), ())), precision=HIGHEST,
                                         preferred_element_type=F32)

    return pl.pallas_call(
        body, name="ada_bwd",
        out_shape=jax.ShapeDtypeStruct((DEPTH, D_MODEL, ADA_COLS), F32),
        compiler_params=_params(),
    )(c_all, d_cols)


def _lower_bounds(logits):
    m = jnp.maximum(logits[0:1], logits[1:2])
    e0, e1 = jnp.exp(logits[0:1] - m), jnp.exp(logits[1:2] - m)
    den = e0 + e1
    p0, p1 = e0 / den, e1 / den
    low0 = p0 - p0
    low1 = (p0 + p1) - p0
    return (p0, p1), (low0, low1)


def _lb_fwd(lb_logits):
    def body(lg_ref, out_ref):
        _, (low0, low1) = _lower_bounds(lg_ref[...])
        out_ref[0:1, :] = jnp.clip(low0, 0.0, 1.0)
        out_ref[1:2, :] = jnp.clip(low1, 0.0, 1.0)

    return pl.pallas_call(body, name="lb_fwd", out_shape=jax.ShapeDtypeStruct(lb_logits.shape, F32),
                          compiler_params=_params())(lb_logits)


def _row_spec(cols=D_MODEL):
    return pl.BlockSpec((1, cols), lambda *_: (0, 0))


def _prenorm_fwd(x, g, shift, scale, tm, name, after=None):
    seq = x.shape[0]

    def body(x_ref, g_ref, sh_ref, sc_ref, h_ref):
        xv = x_ref[...]
        rs = lax.rsqrt(jnp.mean(xv * xv, axis=-1, keepdims=True) + NORM_EPS)
        h = (xv * rs * g_ref[...]) * (1.0 + sc_ref[...]) + sh_ref[...]
        h_ref[...] = h.astype(h_ref.dtype)

    tile = pl.BlockSpec((tm, D_MODEL), lambda i: (i, 0))
    return _pallas_after(
        body, 4, after, name=name, grid=(seq // tm,),
        in_specs=[tile, _row_spec(), _row_spec(), _row_spec()], out_specs=tile,
        out_shape=jax.ShapeDtypeStruct((seq, D_MODEL), MXU_DTYPE),
        compiler_params=_params(dimension_semantics=("parallel",)),
    )(x, g, shift, scale)


def _in_proj(h, win_g, tm, name, after=None):
    seq = h.shape[0]

    def body(h_ref, w_ref, z_ref, w_pair):
        @pl.when(pl.program_id(1) == 0)
        def _():
            w_pair[...] = jnp.concatenate([w_ref[0], w_ref[1]], axis=1)

        z_ref[...] = jnp.dot(h_ref[...], w_pair[...], preferred_element_type=F32)

    return _pallas_after(
        body, 2, after, name=name, grid=(N_DEV // 2, seq // tm),
        in_specs=[pl.BlockSpec((tm, D_MODEL), lambda j, i: (i, 0)),
                  pl.BlockSpec((2, D_MODEL, IN_COLS), lambda j, i: (j, 0, 0))],
        out_specs=pl.BlockSpec((tm, 2 * IN_COLS), lambda j, i: (i, j)),
        out_shape=jax.ShapeDtypeStruct((seq, IN_WIDTH), F32),
        scratch_shapes=[pltpu.VMEM((D_MODEL, 2 * IN_COLS), MXU_DTYPE)],
        compiler_params=_params(dimension_semantics=("parallel", "arbitrary")),
    )(h, win_g)


def _shift_down(v, j, pos):
    return jnp.where(pos >= j, pltpu.roll(v, j, 0), 0.0)


def _shift_up(v, j, pos, seq):
    return jnp.where(pos < seq - j, pltpu.roll(v, seq - j, 0), 0.0)


def _select_window(g, candidates):
    out = candidates[-1]
    for i in range(len(candidates) - 2, -1, -1):
        out = jnp.where(g == i, candidates[i], out)
    return out


def _pool_mean_minus_token(u, g, pos):
    sums, acc = [], u
    for j in (1, 2, 4, 8):
        acc = acc + _shift_down(acc, j, pos)
        sums.append(acc)
    wsum = _select_window(g, sums)
    width = jnp.left_shift(2, g).astype(F32)
    count = jnp.minimum(pos.astype(F32) + 1.0, width)
    return wsum / count - u, count


def _pool_fwd(z, pool_w_l, pool_scale_l, name, after=None):
    seq = z.shape[0]

    def body(pv_ref, pg_ref, w_ref, sc_ref, out_ref):
        g = pl.program_id(0)
        pos = lax.broadcasted_iota(jnp.int32, (seq, GROUP_DIM), 0)
        pm, _ = _pool_mean_minus_token(pv_ref[...], g, pos)
        lin = _dot(pm, w_ref[...]) * sc_ref[...]
        pg = pg_ref[...]
        out_ref[...] = (lin * (pg * _sigmoid(pg))).astype(out_ref.dtype)

    return _pallas_after(
        body, 4, after, name=name, grid=(POOL_GROUPS,),
        in_specs=[pl.BlockSpec((seq, GROUP_DIM), lambda g: (0, g)),
                  pl.BlockSpec((seq, GROUP_DIM), lambda g: (0, POOL_GROUPS + g)),
                  pl.BlockSpec((None, GROUP_DIM, GROUP_DIM), lambda g: (g, 0, 0)),
                  pl.BlockSpec((1, GROUP_DIM), lambda g: (0, g))],
        out_specs=pl.BlockSpec((seq, GROUP_DIM), lambda g: (0, g)),
        out_shape=jax.ShapeDtypeStruct((seq, POOL_WIDTH), MXU_DTYPE),
        compiler_params=_params(dimension_semantics=("parallel",)),
    )(z, z, pool_w_l, pool_scale_l)


def _chunk_masks():
    row = lax.broadcasted_iota(jnp.int32, (CHUNK, CHUNK), 0)
    col = lax.broadcasted_iota(jnp.int32, (CHUNK, CHUNK), 1)
    causal = row >= col
    before_sub = col < (row // SUB) * SUB
    suffix = row <= col
    return causal, before_sub, suffix


def _masked_sums(masks, v):
    lhs = jnp.concatenate([m.astype(jnp.bfloat16) for m in masks], axis=0)
    hi = v.astype(jnp.bfloat16)
    rest = v - hi.astype(F32)
    mid = rest.astype(jnp.bfloat16)
    lo = (rest - mid.astype(F32)).astype(jnp.bfloat16)
    out = jnp.dot(lhs, hi, preferred_element_type=F32)
    out += jnp.dot(lhs, mid, preferred_element_type=F32)
    out += jnp.dot(lhs, lo, preferred_element_type=F32)
    return [out[i * CHUNK:(i + 1) * CHUNK] for i in range(len(masks))]


def _gates(zf, lb):
    sg = _sigmoid(zf)
    f = lb + (1.0 - lb) * sg
    logf = jnp.log(jnp.maximum(f, LOG_FLOOR))
    return sg, f, logf


def _intra_blocks(q_h, k_h, cum_h, base_h, causal):
    rel = cum_h - base_h
    out = []
    for i in range(N_SUB):
        rows = slice(i * SUB, (i + 1) * SUB)
        e_q = jnp.exp(rel[rows])
        base_i = jnp.concatenate([base_h[rows]] * N_SUB, axis=0)
        e_k = jnp.exp(jnp.minimum(base_i - cum_h, EXP_CLAMP))
        q_t = (q_h[rows] * e_q).astype(MXU_DTYPE)
        k_t = (k_h * e_k).astype(MXU_DTYPE)
        a_i = jnp.where(causal[rows], _dot_nt(q_t, k_t), 0.0)
        out.append((q_t, k_t, e_q, e_k, a_i))
    return out


def _hgrn_fwd(z, lb_l, gn_l, name, after=None):
    seq = z.shape[0]
    n_chunks = seq // CHUNK
    per_step = min(FWD_STEP_CHUNKS, n_chunks)
    rows_per_step = per_step * CHUNK

    def body(hq_ref, hf_ref, hi_ref, hg_ref, lb_ref, gn_ref, o_ref, bin_ref, st_ref, cb_ref, state):
        @pl.when(pl.program_id(0) == 0)
        def _():
            state[...] = jnp.zeros_like(state)

        causal, before_sub, _ = _chunk_masks()
        for cc in range(per_step):
            rows = slice(cc * CHUNK, (cc + 1) * CHUNK)
            _, f, logf = _gates(hf_ref[rows, :], lb_ref[...])
            kk = 1.0 - f
            hq = hq_ref[rows, :]
            q = hq * _sigmoid(hq)
            cum, base = _masked_sums([causal, before_sub], logf)
            cb_ref[rows, 0:D_MODEL] = cum
            cb_ref[rows, D_MODEL:2 * D_MODEL] = base
            st_ref[cc] = state[...]
            for h in range(HEADS):
                sl = slice(h * HEAD_DIM, (h + 1) * HEAD_DIM)
                q_h, k_h, cum_h = q[:, sl], kk[:, sl], cum[:, sl]
                v_h = hi_ref[rows, sl]
                st_h = state[h]
                blocks = _intra_blocks(q_h, k_h, cum_h, base[:, sl], causal)
                a = jnp.concatenate([b[4] for b in blocks], axis=0)
                o_h = _dot_nt(q_h * jnp.exp(cum_h), st_h) + _dot(a, v_h)
                last = jnp.sum(logf[:, sl], axis=0, keepdims=True)
                state[h] = st_h * jnp.exp(last) + _dot_tn(v_h, k_h * jnp.exp(last - cum_h))
                rs = lax.rsqrt(jnp.mean(o_h * o_h, axis=-1, keepdims=True) + NORM_EPS)
                hg = hg_ref[rows, sl]
                o_ref[rows, sl] = o_h
                bin_ref[rows, sl] = ((o_h * rs * gn_ref[...]) * (hg * _sigmoid(hg))).astype(bin_ref.dtype)

    def col(block):
        return pl.BlockSpec((rows_per_step, D_MODEL), lambda c: (c, block))

    tile = pl.BlockSpec((rows_per_step, D_MODEL), lambda c: (c, 0))
    return _pallas_after(
        body, 6, after, name=name, grid=(n_chunks // per_step,),
        in_specs=[col(COL_HQ), col(COL_HF), col(COL_HI), col(COL_HG), _row_spec(), _row_spec(HEAD_DIM)],
        out_specs=[tile, tile, pl.BlockSpec((per_step, HEADS, HEAD_DIM, HEAD_DIM), lambda c: (c, 0, 0, 0)),
                   pl.BlockSpec((rows_per_step, 2 * D_MODEL), lambda c: (c, 0))],
        out_shape=[jax.ShapeDtypeStruct((seq, D_MODEL), F32),
                   jax.ShapeDtypeStruct((seq, D_MODEL), MXU_DTYPE),
                   jax.ShapeDtypeStruct((n_chunks, HEADS, HEAD_DIM, HEAD_DIM), F32),
                   jax.ShapeDtypeStruct((seq, 2 * D_MODEL), F32)],
        scratch_shapes=[pltpu.VMEM((HEADS, HEAD_DIM, HEAD_DIM), F32)],
        compiler_params=_params(dimension_semantics=("arbitrary",)),
    )(z, z, z, z, lb_l, gn_l)


def _rms_parts(y):
    rs = lax.rsqrt(jnp.mean(y * y, axis=-1, keepdims=True) + NORM_EPS)
    return rs, y * rs


def _merge_fwd(a_in, b_in, z, x, wpo_g, who_g, wout_g, gate, g_post, tm, name, target=None):
    seq = x.shape[0]
    with_loss = target is not None

    def body(*refs):
        a_ref, b_ref, mgp_ref, mgh_ref, x_ref, wpo_ref, who_ref, wout_ref, gate_ref, gp_ref = refs[:10]
        ba_ref, bb_ref, mer_ref, y_ref, last_ref = refs[10 + with_loss:15 + with_loss]
        a = a_ref[...]
        ba = _dot(a, jnp.concatenate([wpo_ref[j] for j in range(N_DEV)], axis=1))
        bb = _dot(b_ref[...], who_ref[...])
        merged = _sigmoid(mgp_ref[...]) * ba + _sigmoid(mgh_ref[...]) * bb
        y = _dot(merged, wout_ref[...])
        _, yn = _rms_parts(y)
        ba_ref[...] = ba.astype(ba_ref.dtype)
        bb_ref[...] = bb.astype(bb_ref.dtype)
        mer_ref[...] = merged.astype(mer_ref.dtype)
        y_ref[...] = y.astype(y_ref.dtype)
        x_next = x_ref[...] + gate_ref[...] * (yn * gp_ref[...])
        if not with_loss:
            last_ref[...] = x_next
            return
        loss_ref = refs[16]

        @pl.when(pl.program_id(0) == 0)
        def _():
            loss_ref[...] = jnp.zeros_like(loss_ref)

        err = x_next - refs[10][...]
        loss_ref[...] += 0.5 * jnp.sum(jnp.mean(err * err, axis=-1, keepdims=True), axis=0, keepdims=True)
        last_ref[...] = err * (1.0 / D_MODEL)

    def tile(cols=D_MODEL, block=0):
        return pl.BlockSpec((tm, cols), lambda i: (i, block))

    full = pl.BlockSpec((D_MODEL, D_MODEL), lambda i: (0, 0))
    act = jax.ShapeDtypeStruct((seq, D_MODEL), MXU_DTYPE)
    f32 = jax.ShapeDtypeStruct((seq, D_MODEL), F32)
    one = [pl.BlockSpec((1, 1), lambda i: (0, 0))] if with_loss else []
    return pl.pallas_call(
        body, name=name, grid=(seq // tm,),
        in_specs=[tile(POOL_WIDTH), tile(), tile(block=COL_MGP), tile(block=COL_MGH), tile(),
                  pl.BlockSpec((N_DEV, POOL_WIDTH, GROUP_DIM), lambda i: (0, 0, 0)),
                  full, full, _row_spec(), _row_spec()] + ([tile()] if with_loss else []),
        out_specs=[tile(), tile(), tile(), tile(), tile()] + one,
        out_shape=[act, act, act, act, f32] + ([jax.ShapeDtypeStruct((1, 1), F32)] if with_loss else []),
        compiler_params=_params(dimension_semantics=("arbitrary" if with_loss else "parallel",)),
    )(a_in, b_in, z, z, x, wpo_g, who_g, wout_g, gate, g_post, *([target] if with_loss else []))


def _stage_copy(stage, sems, dst, slot, step, where):
    rows, cols = where(step)
    return pltpu.make_async_copy(stage.at[slot], dst.at[rows, cols], sems.at[slot])


def _stage_begin(stage, sems, dst, step, where):
    slot = step % 2

    @pl.when(step >= 2)
    def _():
        _stage_copy(stage, sems, dst, slot, step - 2, where).wait()

    return slot


def _stage_end(stage, sems, dst, step, n_steps, where):
    slot = step % 2
    _stage_copy(stage, sems, dst, slot, step, where).start()

    @pl.when(step == n_steps - 1)
    def _():
        _stage_copy(stage, sems, dst, slot, step, where).wait()
        if n_steps > 1:
            _stage_copy(stage, sems, dst, 1 - slot, step - 1, where).wait()


def _merge_bwd(dx, y, ba, bb, z, wpo_g, who_g, wout_g, gate, g_post, dz, tm, name):
    seq = dx.shape[0]
    n_steps = seq // tm

    def body(dx_ref, y_ref, ba_ref, bb_ref, mgp_ref, mgh_ref, wpo_ref, who_ref, wout_ref, gate_ref, gp_ref, _,
             dy_ref, dba_ref, dbb_ref, da_ref, db_ref, dz_ref, acc_ref, stage, sems):
        step = pl.program_id(0)

        @pl.when(step == 0)
        def _():
            acc_ref[...] = jnp.zeros_like(acc_ref)

        def where(t):
            return pl.ds(t * tm, tm), pl.ds(COL_MGP * D_MODEL, 2 * D_MODEL)

        dmg_ref = stage.at[_stage_begin(stage, sems, dz_ref, step, where)]

        dxv = dx_ref[...]
        rs, yn = _rms_parts(y_ref[...].astype(F32))
        acc_ref[0:1, :] += jnp.sum(dxv * yn * gp_ref[...], axis=0, keepdims=True)
        acc_ref[1:2, :] += jnp.sum(dxv * gate_ref[...] * yn, axis=0, keepdims=True)
        dyn = dxv * (gate_ref[...] * gp_ref[...])
        dy = rs * (dyn - yn * jnp.mean(dyn * yn, axis=-1, keepdims=True))
        dmerged = _dot_nt(dy, wout_ref[...])
        sp, sh = _sigmoid(mgp_ref[...]), _sigmoid(mgh_ref[...])
        dba, dbb = sp * dmerged, sh * dmerged
        dmg_ref[:, 0:D_MODEL] = (dmerged * ba_ref[...].astype(F32) * sp * (1.0 - sp)).astype(dmg_ref.dtype)
        dmg_ref[:, D_MODEL:2 * D_MODEL] = (dmerged * bb_ref[...].astype(F32) * sh * (1.0 - sh)).astype(dmg_ref.dtype)
        da = _dot_nt(dba, jnp.concatenate([wpo_ref[j] for j in range(N_DEV)], axis=1))
        dy_ref[...] = dy.astype(dy_ref.dtype)
        dba_ref[...] = dba.astype(dba_ref.dtype)
        dbb_ref[...] = dbb.astype(dbb_ref.dtype)
        da_ref[...] = da.astype(da_ref.dtype)
        db_ref[...] = _dot_nt(dbb, who_ref[...]).astype(db_ref.dtype)
        _stage_end(stage, sems, dz_ref, step, n_steps, where)

    def tile(cols=D_MODEL, block=0):
        return pl.BlockSpec((tm, cols), lambda i: (i, block))

    full = pl.BlockSpec((D_MODEL, D_MODEL), lambda i: (0, 0))
    hbm = pl.BlockSpec(memory_space=pl.ANY)
    act = jax.ShapeDtypeStruct((seq, D_MODEL), MXU_DTYPE)
    return pl.pallas_call(
        body, name=name, grid=(n_steps,),
        in_specs=[tile(), tile(), tile(), tile(), tile(block=COL_MGP), tile(block=COL_MGH),
                  pl.BlockSpec((N_DEV, POOL_WIDTH, GROUP_DIM), lambda i: (0, 0, 0)),
                  full, full, _row_spec(), _row_spec(), hbm],
        out_specs=[tile(), tile(), tile(), tile(POOL_WIDTH), tile(), hbm,
                   pl.BlockSpec((8, D_MODEL), lambda i: (0, 0))],
        out_shape=[act, act, act, jax.ShapeDtypeStruct((seq, POOL_WIDTH), MXU_DTYPE), act,
                   jax.ShapeDtypeStruct(dz.shape, dz.dtype),
                   jax.ShapeDtypeStruct((8, D_MODEL), F32)],
        input_output_aliases={11: 5},
        scratch_shapes=[pltpu.VMEM((2, tm, 2 * D_MODEL), MXU_DTYPE), pltpu.SemaphoreType.DMA((2,))],
        compiler_params=_params(dimension_semantics=("arbitrary",)),
    )(dx, y, ba, bb, z, z, wpo_g, who_g, wout_g, gate, g_post, dz)


def _grad_out_weights(merged, dy, b_in, dbb, a_in, dba, name):
    seq = merged.shape[0]
    tn = D_MODEL // 2
    per_step = tn // GROUP_DIM

    def body(mer_ref, dy_ref, b_ref, dbb_ref, a_ref, dba_ref, gout_ref, gho_ref, gpo_ref):
        gout_ref[...] = _dot_tn(mer_ref[...], dy_ref[...]).astype(gout_ref.dtype)
        gho_ref[...] = _dot_tn(b_ref[...], dbb_ref[...]).astype(gho_ref.dtype)
        g_po = _dot_tn(a_ref[...], dba_ref[...])
        for j in range(per_step):
            gpo_ref[j] = g_po[:, j * GROUP_DIM:(j + 1) * GROUP_DIM].astype(gpo_ref.dtype)

    def whole(cols):
        return pl.BlockSpec((seq, cols), lambda j: (0, 0))

    cols = pl.BlockSpec((seq, tn), lambda j: (0, j))
    return pl.pallas_call(
        body, name=name, grid=(D_MODEL // tn,),
        in_specs=[whole(D_MODEL), cols, whole(D_MODEL), cols, whole(POOL_WIDTH), cols],
        out_specs=[pl.BlockSpec((D_MODEL, tn), lambda j: (0, j)), pl.BlockSpec((D_MODEL, tn), lambda j: (0, j)),
                   pl.BlockSpec((per_step, POOL_WIDTH, GROUP_DIM), lambda j: (j, 0, 0))],
        out_shape=[jax.ShapeDtypeStruct((D_MODEL, D_MODEL), WIRE_DTYPE),
                   jax.ShapeDtypeStruct((D_MODEL, D_MODEL), WIRE_DTYPE),
                   jax.ShapeDtypeStruct((N_DEV, POOL_WIDTH, GROUP_DIM), WIRE_DTYPE)],
        compiler_params=_params(dimension_semantics=("parallel",)),
    )(merged, dy, b_in, dbb, a_in, dba)


def _hgrn_bwd(db_in, z, o, states, cum_base, lb_l, gn_l, dz, name, after=None):
    seq = z.shape[0]
    per_step = min(BWD_STEP_CHUNKS, seq // CHUNK)
    rows_per_step = per_step * CHUNK
    n_steps = seq // rows_per_step
    last_step = n_steps - 1

    def body(db_ref, hq_ref, hf_ref, hi_ref, hg_ref, o_ref, st_ref, cb_ref, lb_ref, gn_ref, _,
             dz_hbm, dlb_ref, dgn_ref, dstate, dq_buf, dk_buf, dg_buf, stage, sems):
        step = pl.program_id(0)

        @pl.when(step == 0)
        def _():
            dstate[...] = jnp.zeros_like(dstate)
            dlb_ref[...] = jnp.zeros_like(dlb_ref)
            dgn_ref[...] = jnp.zeros_like(dgn_ref)

        def one_chunk(cc, *args):
            one_chunk_body((db_ref, hq_ref, hf_ref, hi_ref, hg_ref, o_ref, st_ref, cb_ref, dlb_ref, dgn_ref, dstate,
                            dq_buf, dk_buf, dg_buf), cc, *args)

        def where(t):
            return pl.ds((last_step - t) * rows_per_step, rows_per_step), pl.ds(COL_HQ * D_MODEL, 4 * D_MODEL)

        dz_step = stage.at[_stage_begin(stage, sems, dz_hbm, step, where)]
        causal, before_sub, suffix = _chunk_masks()
        lb = lb_ref[...]
        gn = gn_ref[...]
        for cc in reversed(range(per_step)):
            one_chunk(cc, dz_step, causal, before_sub, suffix, lb, gn)
        _stage_end(stage, sems, dz_hbm, step, n_steps, where)

    def one_chunk_body(refs, cc, dz_step, causal, before_sub, suffix, lb, gn):
        (db_ref, hq_ref, hf_ref, hi_ref, hg_ref, o_ref, st_ref, cb_ref, dlb_ref, dgn_ref, dstate,
         dq_buf, dk_buf, dg_buf) = refs
        rows = slice(cc * CHUNK, (cc + 1) * CHUNK)
        dz_ref = dz_step.at[rows, :]
        dq_buf, dk_buf, dg_buf = dq_buf.at[cc], dk_buf.at[cc], dg_buf.at[cc]
        sg, f, logf = _gates(hf_ref[rows, :], lb)
        kk = 1.0 - f
        hq = hq_ref[rows, :]
        sq = _sigmoid(hq)
        q = hq * sq
        cum, base = cb_ref[rows, 0:D_MODEL], cb_ref[rows, D_MODEL:2 * D_MODEL]
        dgn = jnp.zeros((1, HEAD_DIM), F32)
        dlast = []
        for h in range(HEADS):
            sl = slice(h * HEAD_DIM, (h + 1) * HEAD_DIM)
            q_h, k_h, cum_h = q[:, sl], kk[:, sl], cum[:, sl]
            v_h = hi_ref[rows, sl]
            st_h = st_ref[cc, h]
            dst_h = dstate[h]
            rs, ohat = _rms_parts(o_ref[rows, sl])
            hg = hg_ref[rows, sl]
            shg = _sigmoid(hg)
            d_bin = db_ref[rows, sl].astype(F32)
            don = d_bin * (hg * shg)
            dgn += jnp.sum(don * ohat, axis=0, keepdims=True)
            dohat = don * gn
            do = rs * (dohat - ohat * jnp.mean(dohat * ohat, axis=-1, keepdims=True))
            dz_ref[:, 3 * D_MODEL + h * HEAD_DIM:3 * D_MODEL + (h + 1) * HEAD_DIM] = (
                d_bin * (ohat * gn) * _dsilu(hg, shg)).astype(dz_ref.dtype)
            last = cb_ref[(cc + 1) * CHUNK - 1:(cc + 1) * CHUNK, sl]
            g_in = jnp.exp(cum_h)
            d_out = jnp.exp(last - cum_h)
            q_bar, k_bar = q_h * g_in, k_h * d_out
            blocks = _intra_blocks(q_h, k_h, cum_h, base[:, sl], causal)
            a = jnp.concatenate([b[4] for b in blocks], axis=0)
            da = jnp.where(causal, _dot_nt(do, v_h), 0.0)
            dv = _dot_tn(a, do) + _dot_nt(k_bar, dst_h)
            dq_bar, dk_bar = _dot(do, st_h), _dot(v_h, dst_h)
            dk = dk_bar * d_out
            dq_parts, dg_parts = [], []
            dg_k = k_bar * dk_bar
            dlast.append(jnp.sum(k_bar * dk_bar, axis=0, keepdims=True)
                         + jnp.exp(last) * jnp.sum(st_h * dst_h, axis=0, keepdims=True))
            for i, (q_t, k_t, e_q, e_k, _) in enumerate(blocks):
                da_i = da[i * SUB:(i + 1) * SUB].astype(MXU_DTYPE)
                dq_t = _dot(da_i, k_t)
                dk_t = _dot_tn(da_i, q_t)
                dq_parts.append(dq_t * e_q)
                dk += dk_t * e_k
                dg_parts.append(q_t.astype(F32) * dq_t)
                dg_k += k_t.astype(F32) * dk_t
            dq = dq_bar * g_in + jnp.concatenate(dq_parts, axis=0)
            dg_buf[:, sl] = q_bar * dq_bar + jnp.concatenate(dg_parts, axis=0) - dg_k
            dstate[h] = dst_h * jnp.exp(last) + _dot_tn(do, q_bar)
            dq_buf[:, sl] = dq
            dk_buf[:, sl] = dk
            dz_ref[:, 2 * D_MODEL + h * HEAD_DIM:2 * D_MODEL + (h + 1) * HEAD_DIM] = dv.astype(dz_ref.dtype)
        dgn_ref[...] += dgn
        dq_all, dk_all = dq_buf[...], dk_buf[...]
        dlogf = _masked_sums([suffix], dg_buf[...])[0] + jnp.concatenate(dlast, axis=1)
        df = jnp.where(f > LOG_FLOOR, dlogf / f, 0.0) - dk_all
        dlb_ref[...] += jnp.sum(df * (1.0 - sg), axis=0, keepdims=True)
        dz_ref[:, 0:D_MODEL] = (dq_all * _dsilu(hq, sq)).astype(dz_ref.dtype)
        dz_ref[:, D_MODEL:2 * D_MODEL] = (df * (1.0 - lb) * sg * (1.0 - sg)).astype(dz_ref.dtype)

    def col(block):
        return pl.BlockSpec((rows_per_step, D_MODEL), lambda c: (last_step - c, block))

    hbm = pl.BlockSpec(memory_space=pl.ANY)
    return _pallas_after(
        body, 11, after, name=name, grid=(n_steps,),
        in_specs=[col(0), col(COL_HQ), col(COL_HF), col(COL_HI), col(COL_HG), col(0),
                  pl.BlockSpec((per_step, HEADS, HEAD_DIM, HEAD_DIM), lambda c: (last_step - c, 0, 0, 0)),
                  pl.BlockSpec((rows_per_step, 2 * D_MODEL), lambda c: (last_step - c, 0)),
                  _row_spec(), _row_spec(HEAD_DIM), hbm],
        out_specs=[hbm, _row_spec(), _row_spec(HEAD_DIM)],
        out_shape=[jax.ShapeDtypeStruct(dz.shape, dz.dtype),
                   jax.ShapeDtypeStruct((1, D_MODEL), F32), jax.ShapeDtypeStruct((1, HEAD_DIM), F32)],
        input_output_aliases={10: 0},
        scratch_shapes=[pltpu.VMEM((HEADS, HEAD_DIM, HEAD_DIM), F32)]
        + [pltpu.VMEM((per_step, CHUNK, D_MODEL), F32)] * 3
        + [pltpu.VMEM((2, rows_per_step, 4 * D_MODEL), MXU_DTYPE), pltpu.SemaphoreType.DMA((2,))],
        compiler_params=_params(dimension_semantics=("arbitrary",)),
    )(db_in, z, z, z, z, o, states, cum_base, lb_l, gn_l, dz)


def _pool_bwd(da_in, z, pool_w_l, pool_scale_l, dz, name, after=None):
    seq = z.shape[0]

    def body(da_ref, pv_ref, pg_ref, w_ref, sc_ref, _, dz_hbm, dw_ref, dsc_ref, stage_pv, stage_pg, sems_pv, sems_pg):
        g = pl.program_id(0)

        def where_pv(t):
            return pl.ds(0, seq), pl.ds(pl.multiple_of(t * GROUP_DIM, GROUP_DIM), GROUP_DIM)

        def where_pg(t):
            return pl.ds(0, seq), pl.ds(pl.multiple_of(POOL_WIDTH + t * GROUP_DIM, GROUP_DIM), GROUP_DIM)

        dpv_ref = stage_pv.at[_stage_begin(stage_pv, sems_pv, dz_hbm, g, where_pv)]
        dpg_ref = stage_pg.at[_stage_begin(stage_pg, sems_pg, dz_hbm, g, where_pg)]
        pos = lax.broadcasted_iota(jnp.int32, (seq, GROUP_DIM), 0)
        pm, count = _pool_mean_minus_token(pv_ref[...], g, pos)
        lin0 = _dot(pm, w_ref[...])
        pg = pg_ref[...]
        spg = _sigmoid(pg)
        da = da_ref[...].astype(F32)
        dlin = da * (pg * spg)
        dpg_ref[...] = (da * (lin0 * sc_ref[...]) * _dsilu(pg, spg)).astype(dpg_ref.dtype)
        dsc_ref[...] = jnp.sum(dlin * lin0, axis=0, keepdims=True)
        dl0 = dlin * sc_ref[...]
        dw_ref[...] = _dot_tn(pm, dl0)
        dpm = _dot_nt(dl0, w_ref[...])
        sums, acc = [], dpm / count
        for j in (1, 2, 4, 8):
            acc = acc + _shift_up(acc, j, pos, seq)
            sums.append(acc)
        dpv_ref[...] = (_select_window(g, sums) - dpm).astype(dpv_ref.dtype)
        _stage_end(stage_pv, sems_pv, dz_hbm, g, POOL_GROUPS, where_pv)
        _stage_end(stage_pg, sems_pg, dz_hbm, g, POOL_GROUPS, where_pg)

    grp = pl.BlockSpec((seq, GROUP_DIM), lambda g: (0, g))
    hbm = pl.BlockSpec(memory_space=pl.ANY)
    stage = pltpu.VMEM((2, seq, GROUP_DIM), MXU_DTYPE)
    return _pallas_after(
        body, 6, after, name=name, grid=(POOL_GROUPS,),
        in_specs=[grp, grp, pl.BlockSpec((seq, GROUP_DIM), lambda g: (0, POOL_GROUPS + g)),
                  pl.BlockSpec((None, GROUP_DIM, GROUP_DIM), lambda g: (g, 0, 0)),
                  pl.BlockSpec((1, GROUP_DIM), lambda g: (0, g)), hbm],
        out_specs=[hbm, pl.BlockSpec((None, GROUP_DIM, GROUP_DIM), lambda g: (g, 0, 0)),
                   pl.BlockSpec((1, GROUP_DIM), lambda g: (0, g))],
        out_shape=[jax.ShapeDtypeStruct(dz.shape, dz.dtype),
                   jax.ShapeDtypeStruct((POOL_GROUPS, GROUP_DIM, GROUP_DIM), F32),
                   jax.ShapeDtypeStruct((1, POOL_WIDTH), F32)],
        input_output_aliases={5: 0},
        scratch_shapes=[stage, stage, pltpu.SemaphoreType.DMA((2,)), pltpu.SemaphoreType.DMA((2,))],
        compiler_params=_params(dimension_semantics=("arbitrary",)),
    )(da_in, z, z, pool_w_l, pool_scale_l, dz)


def _in_proj_dw(h, dz, name, after=None):
    seq = h.shape[0]

    def body(h_ref, dz_ref, out_ref):
        pair = lax.dot_general(h_ref[...], dz_ref[...], (((0,), (0,)), ((), ())), preferred_element_type=F32)
        out_ref[0] = pair[:, 0:IN_COLS].astype(out_ref.dtype)
        out_ref[1] = pair[:, IN_COLS:].astype(out_ref.dtype)

    return _pallas_after(
        body, 2, after, name=name, grid=(N_DEV // 2,),
        in_specs=[pl.BlockSpec((seq, D_MODEL), lambda j: (0, 0)),
                  pl.BlockSpec((seq, 2 * IN_COLS), lambda j: (0, j))],
        out_specs=pl.BlockSpec((2, D_MODEL, IN_COLS), lambda j: (j, 0, 0)),
        out_shape=jax.ShapeDtypeStruct((N_DEV, D_MODEL, IN_COLS), WIRE_DTYPE),
        compiler_params=_params(dimension_semantics=("parallel",)),
    )(h, dz)


def _in_proj_dh(dz, win_g, tm, name, after=None):
    seq = dz.shape[0]

    def body(dz_ref, w_ref, dh_ref):
        @pl.when(pl.program_id(1) == 0)
        def _():
            dh_ref[...] = jnp.zeros_like(dh_ref)

        w_pair = jnp.concatenate([w_ref[0], w_ref[1]], axis=1)
        dh_ref[...] += lax.dot_general(dz_ref[...], w_pair, (((1,), (1,)), ((), ())), preferred_element_type=F32)

    return _pallas_after(
        body, 2, after, name=name, grid=(seq // tm, N_DEV // 2),
        in_specs=[pl.BlockSpec((tm, 2 * IN_COLS), lambda i, j: (i, j)),
                  pl.BlockSpec((2, D_MODEL, IN_COLS), lambda i, j: (j, 0, 0))],
        out_specs=pl.BlockSpec((tm, D_MODEL), lambda i, j: (i, 0)),
        out_shape=jax.ShapeDtypeStruct((seq, D_MODEL), F32),
        compiler_params=_params(dimension_semantics=("parallel", "arbitrary")),
    )(dz, win_g)


def _prenorm_bwd(x, dh, dx_res, g, scale, tm, name, after=None):
    seq = x.shape[0]

    def body(x_ref, dh_ref, dxr_ref, g_ref, sc_ref, dx_ref, acc_ref):
        @pl.when(pl.program_id(0) == 0)
        def _():
            acc_ref[...] = jnp.zeros_like(acc_ref)

        rs, xn = _rms_parts(x_ref[...])
        dh = dh_ref[...]
        acc_ref[0:1, :] += jnp.sum(dh, axis=0, keepdims=True)
        acc_ref[1:2, :] += jnp.sum(dh * (xn * g_ref[...]), axis=0, keepdims=True)
        dhn = dh * (1.0 + sc_ref[...])
        acc_ref[2:3, :] += jnp.sum(dhn * xn, axis=0, keepdims=True)
        dxn = dhn * g_ref[...]
        dx_ref[...] = rs * (dxn - xn * jnp.mean(dxn * xn, axis=-1, keepdims=True)) + dxr_ref[...]

    tile = pl.BlockSpec((tm, D_MODEL), lambda i: (i, 0))
    return _pallas_after(
        body, 5, after, name=name, grid=(seq // tm,),
        in_specs=[tile, tile, tile, _row_spec(), _row_spec()],
        out_specs=[tile, pl.BlockSpec((8, D_MODEL), lambda i: (0, 0))],
        out_shape=[jax.ShapeDtypeStruct((seq, D_MODEL), F32), jax.ShapeDtypeStruct((8, D_MODEL), F32)],
        compiler_params=_params(dimension_semantics=("arbitrary",)),
    )(x, dh, dx_res, g, scale)


def _adamw_math(w, g, m, v):
    m = ADAM_B1 * m + (1.0 - ADAM_B1) * g
    v = ADAM_B2 * v + (1.0 - ADAM_B2) * (g * g)
    m_hat = m / (1.0 - ADAM_B1 ** ADAM_STEP)
    v_hat = v / (1.0 - ADAM_B2 ** ADAM_STEP)
    delta = -ADAM_LR * (m_hat / (jnp.sqrt(v_hat) + ADAM_EPS) + ADAM_WD * w)
    return delta, m, v


def _adamw_sharded(w, m, v, contrib, tr, name):
    depth, rows, cols = w.shape
    n_parts = contrib.shape[1]

    def body(w_ref, m_ref, v_ref, c_ref, g_ref, d_ref, mo_ref, vo_ref):
        g = c_ref[0].astype(F32)
        for p in range(1, n_parts):
            g += c_ref[p].astype(F32)
        delta, mn, vn = _adamw_math(w_ref[...], g, m_ref[...], v_ref[...])
        g_ref[...] = g
        d_ref[...] = delta
        mo_ref[...] = mn
        vo_ref[...] = vn

    tile = pl.BlockSpec((None, tr, cols), lambda l, i: (l, i, 0))
    shape = jax.ShapeDtypeStruct(w.shape, F32)
    return pl.pallas_call(
        body, name=name, grid=(depth, rows // tr),
        in_specs=[tile, tile, tile, pl.BlockSpec((None, n_parts, tr, cols), lambda l, i: (l, 0, i, 0))],
        out_specs=[tile] * 4, out_shape=[shape] * 4,
        compiler_params=_params(dimension_semantics=("parallel", "parallel")),
    )(w, m, v, contrib)


def _adamw_layer(w, m, v, contribs, l, tr, name, prev=None):
    _, rows, cols = w.shape
    n = len(contribs)

    def body(*refs):
        w_ref, m_ref, v_ref = refs[:3]
        c_refs = refs[3:3 + n]
        g_ref, d_ref, mo_ref, vo_ref = refs[-4:]
        g = c_refs[0][...].astype(F32)
        for c_ref in c_refs[1:]:
            g += c_ref[...].astype(F32)
        delta, mn, vn = _adamw_math(w_ref[...], g, m_ref[...], v_ref[...])
        g_ref[...] = g
        d_ref[...] = delta
        mo_ref[...] = mn
        vo_ref[...] = vn

    tile = pl.BlockSpec((None, tr, cols), lambda i: (l, i, 0))
    in_specs = [tile, tile, tile] + [pl.BlockSpec((None, tr, cols), lambda i, s=slot: (s, i, 0)) for _, slot in contribs]
    operands = [w, m, v] + [arr for arr, _ in contribs]
    aliases = {}
    if prev is not None:
        aliases = {len(operands) + k: k for k in range(4)}
        in_specs += [pl.BlockSpec(memory_space=pl.ANY)] * 4
        operands += list(prev)
    shape = jax.ShapeDtypeStruct(w.shape, F32)
    return pl.pallas_call(
        body, name=name, grid=(rows // tr,), in_specs=in_specs, out_specs=[tile] * 4, out_shape=[shape] * 4,
        input_output_aliases=aliases,
        compiler_params=_params(dimension_semantics=("parallel",)),
    )(*operands)


def _adamw_small(w_pack, m_pack, v_pack, g_late, g_early, shapes):
    pieces, r = {}, 0
    for name, _, n in _SMALL_ROWS:
        pieces.setdefault(name, []).append((r, n))
        r += n
    names = list(pieces)

    def body(w_ref, m_ref, v_ref, gl_ref, ge_ref, *rest):
        outs, packs = rest[:4 * len(names)], rest[4 * len(names):]
        g_l, g_e = gl_ref[0][0:SMALL_LATE_ROWS], ge_ref[0]
        for d in range(1, N_DEV):
            g_l += gl_ref[d][0:SMALL_LATE_ROWS]
            g_e += ge_ref[d]
        g = jnp.concatenate([g_l, g_e], axis=0)
        w = w_ref[...]
        r0, r1, r2 = LB_ROW0, LB_ROW0 + 8, LB_ROW0 + 16
        lg0, lg1 = w[r0:r1], w[r1:r2]
        mx = jnp.maximum(lg0, lg1)
        e0, e1 = jnp.exp(lg0 - mx), jnp.exp(lg1 - mx)
        p0, p1 = e0 / (e0 + e1), e1 / (e0 + e1)
        low = ((p0 - p0), (p0 + p1) - p0)
        dlow = [g_rows * jnp.where((lo > 0.0) & (lo < 1.0), 1.0, jnp.where((lo == 0.0) | (lo == 1.0), 0.5, 0.0))
                for g_rows, lo in ((g[r0:r1], low[0]), (g[r1:r2], low[1]))]
        dp0 = (dlow[0] + dlow[1]) - (dlow[0] + dlow[1])
        dp1 = dlow[1]
        inner = p0 * dp0 + p1 * dp1
        g = jnp.concatenate([g[:r0], p0 * (dp0 - inner), p1 * (dp1 - inner), g[r2:]], axis=0)
        delta, mn, vn = _adamw_math(w, g, m_ref[...], v_ref[...])
        for kind, val in enumerate((g, delta, mn, vn)):
            packs[kind][...] = val
            for j, name in enumerate(names):
                out, at = outs[kind * len(names) + j], 0
                for start, n in pieces[name]:
                    if name in flat:
                        for r in range(n):
                            layer, c = divmod(at + r, flat[name])
                            out[layer:layer + 1, c * 128:(c + 1) * 128] = packs[kind][start + r:start + r + 1, :]
                    else:
                        out[at:at + n, :] = packs[kind][start:start + n, :]
                    at += n

    rows = {name: sum(n for _, n in pieces[name]) for name in names}
    flat = {name: rows[name] // DEPTH for name in names if len(shapes[name]) == 2}
    outs = pl.pallas_call(
        body, name="adamw_small",
        out_shape=[jax.ShapeDtypeStruct(shapes[name] if name in flat else (rows[name], 128), F32)
                   for _ in range(4) for name in names],
        scratch_shapes=[pltpu.VMEM(w_pack.shape, F32)] * 4, compiler_params=_params(),
    )(w_pack, m_pack, v_pack, g_late, g_early)
    return [{name: outs[kind * len(names) + j].reshape(shapes[name]) for j, name in enumerate(names)}
            for kind in range(4)]


def _pack_small(parts, first=0, last=len(_SMALL_ROWS)):
    rows = [(parts[name] if l is None else parts[name][l]).reshape(n, 128) for name, l, n in _SMALL_ROWS[first:last]]
    if last == len(_SMALL_ROWS):
        rows.append(jnp.zeros((SMALL_ROWS_PAD - sum(n for _, _, n in _SMALL_ROWS), 128), F32))
    return jnp.concatenate(rows, axis=0)


def kernel(x, c, w_ada, b_ada, g_pre, g_post, w_in, pool_w, pool_scale, lb_logits, hgrn_norm_g, w_pool_o, w_hgrn_o, w_out, loss_target, m_w_ada, m_b_ada, m_g_pre, m_g_post, m_w_in, m_pool_w, m_pool_scale, m_lb_logits, m_hgrn_norm_g, m_w_pool_o, m_w_hgrn_o, m_w_out, v_w_ada, v_b_ada, v_g_pre, v_g_post, v_w_in, v_pool_w, v_pool_scale, v_lb_logits, v_hgrn_norm_g, v_w_pool_o, v_w_hgrn_o, v_w_out):
    seq = x.shape[1]
    tm = min(1024, seq)
    tm_merge = min(512, seq)
    pos = _my_position()
    me = pos[3]

    c_all = _allgather_small(c, "allgather_c").reshape(N_DEV, D_MODEL)
    b_cols = lax.dynamic_slice_in_dim(b_ada, me * ADA_COLS, ADA_COLS, axis=1)
    ada_part = _ada_fwd(c_all, w_ada, b_cols)
    ada_all = _allgather_small(ada_part.reshape(DEPTH * N_DEV, ADA_COLS), "allgather_ada")
    ada = lax.dynamic_index_in_dim(ada_all.reshape(N_DEV, DEPTH, N_DEV, ADA_COLS), me, axis=2, keepdims=False)
    ada = jnp.transpose(ada, (1, 0, 2)).reshape(DEPTH, 3 * D_MODEL)
    shift = [ada[l:l + 1, 0:D_MODEL] for l in range(DEPTH)]
    scale = [ada[l:l + 1, D_MODEL:2 * D_MODEL] for l in range(DEPTH)]
    gate = [ada[l:l + 1, 2 * D_MODEL:] for l in range(DEPTH)]

    big = dict(win=w_in, wpo=w_pool_o, who=w_hgrn_o, wout=w_out)
    units = [["win0"], ["wpo0", "who0", "wout0"], ["win1", "wpo1", "who1", "wout1"]]
    g_streams = [_gather_streams(keys) for keys in units]
    g_state = [None] * len(units)

    def gather_start(us, after):
        bufs = {}
        for k in [k for u in us for k in units[u]]:
            arr = big[k[:-1]]
            bufs["s_" + k] = arr[int(k[-1])].astype(WIRE_DTYPE)
            bufs["g_" + k] = _with_own_slot(bufs["s_" + k], me)
        bufs, sems, token = _comm_call("gather_start_" + "_".join(map(str, us)), bufs,
                                       start=[s for u in us for s in g_streams[u][:2]], after=after)
        for n, u in enumerate(us):
            g_state[u] = dict(bufs={p + k: bufs[p + k] for k in units[u] for p in ("s_", "g_")},
                              sems=sems[2 * n:2 * n + 2])
        return token

    def gather_pass(u, after):
        st = g_state[u]
        to_chips, _, pass_on = g_streams[u]
        st["bufs"], (st["pass_sems"],), _ = _comm_call(f"gather_pass_{u}", st["bufs"], start=[pass_on],
                                                       wait=[(to_chips, st["sems"][0])], after=after)

    def gather_done(u, after=None):
        st = g_state[u]
        _, to_sibling, pass_on = g_streams[u]
        bufs, _, _ = _comm_call(f"gather_done_{u}", st["bufs"], after=after,
                                wait=[(to_sibling, st["sems"][1]), (pass_on, st["pass_sems"])])
        return {k: bufs["g_" + k] for k in units[u]}

    token = gather_start([0], ada_all)

    lb = _lb_fwd(lb_logits)

    gw = {}
    xs, saved = [x[0]], []
    for l in range(DEPTH):
        h = _prenorm_fwd(xs[l], g_pre[l:l + 1], shift[l], scale[l], tm, f"prenorm_fwd_{l}",
                         after=token if l == 0 else None)
        token = None
        if l == 0:
            gather_pass(0, h)
            gw.update(gather_done(0))
            token = gather_start([1, 2], gw["win0"])
        else:
            gw.update(gather_done(2, h))
        z = _in_proj(h, gw[f"win{l}"], min(1024, seq), f"in_proj_{l}", after=token)
        a_in = _pool_fwd(z, pool_w[l], pool_scale[l:l + 1], f"pool_fwd_{l}")
        o, b_in, states, cum_base = _hgrn_fwd(z, lb[l:l + 1], hgrn_norm_g[l:l + 1], f"hgrn_fwd_{l}")
        if l == 0:
            gather_pass(1, b_in)
            gw.update(gather_done(1))
        who_l = gw[f"who{l}"].reshape(D_MODEL, D_MODEL)
        wout_l = gw[f"wout{l}"].reshape(D_MODEL, D_MODEL)
        last = l == DEPTH - 1
        ba, bb, merged, y, *out = _merge_fwd(a_in, b_in, z, xs[l], gw[f"wpo{l}"], who_l, wout_l, gate[l],
                                             g_post[l:l + 1], tm_merge, f"merge_fwd_{l}",
                                             target=loss_target[0] if last else None)
        if last:
            dx, loss_part = out
        else:
            xs.append(out[0])
            gather_pass(2, out[0])
        saved.append((h, z, a_in, o, b_in, states, cum_base, ba, bb, merged, y, who_l, wout_l))


    chips = _other_chips(pos)
    pair_idx = jnp.stack([_dev_index(cx, cy, pos[2]) for cx, cy in chips] + [me]).astype(jnp.int32)
    pair_rows = dict(win=512, wpo=POOL_WIDTH, who=HEAD_DIM, wout=HEAD_DIM)

    def scatter_pair_start(u, grads):
        keys = list(grads)
        pair, to_chips = _scatter_streams(keys)
        bufs = {}
        for k in keys:
            bufs["g_" + k] = grads[k]
            bufs["st_" + k] = lax.empty((4,) + grads[k].shape[1:], WIRE_DTYPE)
        bufs, (sems,), token = _comm_call(f"scatter_pair_start_{u}", bufs, start=[pair])
        return dict(u=u, keys=keys, pair=pair, to_chips=to_chips, bufs=bufs, sems=sems, token=token)

    def scatter_pair_finish(st, after):
        u, keys = st["u"], st["keys"]
        bufs, _, _ = _comm_call(f"scatter_pair_done_{u}", st["bufs"], wait=[(st["pair"], st["sems"])], after=after)
        bufs2 = {}
        for k in keys:
            bufs2["ps_" + k] = _pair_sum(bufs["g_" + k], bufs["st_" + k], pair_idx, bufs["g_" + k].shape[1],
                                         f"pair_sum_{k}")
            bufs2["ld_" + k] = lax.empty((3,) + bufs["g_" + k].shape[1:], WIRE_DTYPE)
        st.update(bufs=bufs2)

    def scatter_chips_start(st, after=None):
        bufs2, (sems,), token = _comm_call(f"scatter_chips_start_{st['u']}", st["bufs"], start=[st["to_chips"]],
                                           after=after)
        st.update(bufs=bufs2, sems=sems, token=token)

    def scatter_finish(st, after):
        bufs, _, _ = _comm_call(f"scatter_chips_done_{st['u']}", st["bufs"], wait=[(st["to_chips"], st["sems"])],
                                after=after)
        return {k: [(bufs["ps_" + k], 3), (bufs["ld_" + k], 0), (bufs["ld_" + k], 1), (bufs["ld_" + k], 2)]
                for k in st["keys"]}

    moments = dict(win=(m_w_in, v_w_in), wpo=(m_w_pool_o, v_w_pool_o), who=(m_w_hgrn_o, v_w_hgrn_o),
                   wout=(m_w_out, v_w_out))
    big_out = {}

    def finish_unit(unit, after):
        for k, contribs in scatter_finish(scat[unit], after).items():
            wname, l = k[:-1], int(k[-1])
            big_out[wname] = _adamw_layer(big[wname], moments[wname][0], moments[wname][1], contribs, l,
                                          pair_rows[wname], f"adamw_{k}", prev=big_out.get(wname))
            after = big_out[wname][0]
        return after

    d_ada, small, scat = [None] * DEPTH, [None] * DEPTH, {}
    for l in reversed(range(DEPTH)):
        h, z, a_in, o, b_in, states, cum_base, ba, bb, merged, y, who_l, wout_l = saved[l]
        dy, dba, dbb, da_in, db_in, dz, acc_post = _merge_bwd(
            dx, y, ba, bb, z, gw[f"wpo{l}"], who_l, wout_l, gate[l], g_post[l:l + 1],
            lax.empty((seq, IN_WIDTH), MXU_DTYPE), tm_merge, f"merge_bwd_{l}")
        g_out, g_ho, g_po = _grad_out_weights(merged, dy, b_in, dbb, a_in, dba, f"grad_out_weights_{l}")
        g_small = {f"wout{l}": g_out.reshape(N_DEV, HEAD_DIM, D_MODEL),
                   f"who{l}": g_ho.reshape(N_DEV, HEAD_DIM, D_MODEL), f"wpo{l}": g_po}
        st_small = scat["small0"] = scatter_pair_start("small0", g_small) if l == 0 else None
        dz, dlb, dgn = _hgrn_bwd(db_in, z, o, states, cum_base, lb[l:l + 1], hgrn_norm_g[l:l + 1], dz, f"hgrn_bwd_{l}",
                                 after=st_small and st_small["token"])
        if l == 0:
            scatter_pair_finish(st_small, dlb)
            scatter_chips_start(st_small)
        dz, dpw, dps = _pool_bwd(da_in, z, pool_w[l], pool_scale[l:l + 1], dz, f"pool_bwd_{l}",
                                 after=st_small and st_small["token"])
        small[l] = dict(g_post=acc_post[1], pool_w=dpw, pool_scale=dps[0], lb_logits=dlb[0], hgrn_norm_g=dgn[0])
        token = None
        if l == 0:
            parts = {name: jnp.stack([small[0][name], small[1][name]]) for name in small[0]}
            parts.update(b_ada=[None, d_ada[1]], g_pre=[None, small[1]["g_pre"]])
            sg_stream = _direct_gather_stream("sg")
            early = _pack_small(parts, 2)
            sg_bufs, (sg_sems,), token = _comm_call(
                "small_grads_start", dict(s_sg=early, g_sg=_with_own_slot(early, me)), start=[sg_stream])
        g_win = {f"win{l}": _in_proj_dw(h, dz, f"grad_w_in_{l}", after=token)}
        st_win = scat[f"win{l}"] = scatter_pair_start(f"win{l}", g_win if l == 0 else {**g_small, **g_win})
        if l > 0:
            dh = _in_proj_dh(dz, gw[f"win{l}"], seq, f"in_proj_dh_{l}", after=st_win["token"])
            scatter_pair_finish(st_win, dh)
            scatter_chips_start(st_win)
        else:
            scatter_pair_finish(st_win, st_win["token"])
            scatter_chips_start(st_win)
            after = st_win["token"]
            for unit in ("win1", "small0"):
                after = finish_unit(unit, after)
            dh = _in_proj_dh(dz, gw[f"win{l}"], seq, f"in_proj_dh_{l}", after=after)
        dx, acc_pre = _prenorm_bwd(xs[l], dh, dx, g_pre[l:l + 1], scale[l], tm, f"prenorm_bwd_{l}",
                                   after=st_win["token"])
        d_ada[l] = jnp.concatenate([acc_pre[0], acc_pre[1], acc_post[0]])
        small[l]["g_pre"] = acc_pre[2]
    grad_x = dx[None]

    parts = dict(b_ada=[d_ada[0]], g_pre=[small[0]["g_pre"]])
    late = jnp.concatenate([_pack_small(parts, 0, 2), jnp.broadcast_to(loss_part, (8, 128))], axis=0)
    g_late = _allgather_small(late, "allgather_late_grads")
    loss = jnp.sum(g_late[:, SMALL_LATE_ROWS, 0])
    sg_bufs, _, _ = _comm_call("small_grads_done", sg_bufs, wait=[(sg_stream, sg_sems)], after=g_late)
    g_early = sg_bufs["g_sg"]
    small_names = list(dict.fromkeys(name for name, _, _ in _SMALL_ROWS))
    weights = dict(b_ada=b_ada, g_pre=g_pre, g_post=g_post, pool_w=pool_w, pool_scale=pool_scale,
                   lb_logits=lb_logits, hgrn_norm_g=hgrn_norm_g)
    m_small = dict(b_ada=m_b_ada, g_pre=m_g_pre, g_post=m_g_post, pool_w=m_pool_w, pool_scale=m_pool_scale,
                   lb_logits=m_lb_logits, hgrn_norm_g=m_hgrn_norm_g)
    v_small = dict(b_ada=v_b_ada, g_pre=v_g_pre, g_post=v_g_post, pool_w=v_pool_w, pool_scale=v_pool_scale,
                   lb_logits=v_lb_logits, hgrn_norm_g=v_hgrn_norm_g)
    shapes = {name: weights[name].shape for name in small_names}
    small_out = _adamw_small(_pack_small(weights), _pack_small(m_small), _pack_small(v_small), g_late, g_early,
                             shapes)

    d_ada_all = jnp.stack([g_late[:, 0:24, :].reshape(N_DEV, 3 * D_MODEL),
                           g_early[:, 0:24, :].reshape(N_DEV, 3 * D_MODEL)], axis=1)
    d_cols = jnp.transpose(lax.dynamic_slice_in_dim(d_ada_all, me * ADA_COLS, ADA_COLS, axis=2), (1, 0, 2))
    g_w_ada = _ada_bwd(c_all, d_cols)
    ada_out = _adamw_sharded(w_ada, m_w_ada, v_w_ada, g_w_ada[:, None], 256, "adamw_w_ada")
    finish_unit("win0", ada_out[1][0, 0:8, 0:128] + small_out[1]["pool_scale"][0:1, 0:128])

    def leaf(kind):
        s = small_out[kind]
        return (ada_out[kind], s["b_ada"], s["g_pre"], s["g_post"], big_out["win"][kind], s["pool_w"], s["pool_scale"],
                s["lb_logits"], s["hgrn_norm_g"], big_out["wpo"][kind], big_out["who"][kind], big_out["wout"][kind])

    return (loss, grad_x) + leaf(0) + leaf(1) + leaf(2) + leaf(3)
```

```python
import jax
import jax.numpy as jnp
from jax import lax
from jax.experimental import pallas as pl
from jax.experimental.pallas import tpu as pltpu

F32 = jnp.float32
MXU_DTYPE = jnp.bfloat16
WIRE_DTYPE = jnp.bfloat16

N_DEV = 8
DEPTH = 2
D_MODEL = 1024
HEADS = 8
HEAD_DIM = 128
POOL_GROUPS = 4
GROUP_DIM = 128
POOL_WIDTH = POOL_GROUPS * GROUP_DIM
IN_WIDTH = 7168
CHUNK = 64
SUB = 16
N_SUB = CHUNK // SUB
FWD_STEP_CHUNKS = 8
BWD_STEP_CHUNKS = 4
EXP_CLAMP = 80.0
NORM_EPS = 1e-6
LOG_FLOOR = 1e-30
ADA_COLS = 3 * D_MODEL // N_DEV
IN_COLS = IN_WIDTH // N_DEV
COL_HQ, COL_HF, COL_HI, COL_HG, COL_MGP, COL_MGH = 1, 2, 3, 4, 5, 6

ADAM_LR = 0.001
ADAM_B1 = 0.9
ADAM_B2 = 0.999
ADAM_EPS = 1e-08
ADAM_WD = 0.01
ADAM_STEP = 10

VMEM_LIMIT = 48 * 1024 * 1024
MESH_ID = pl.DeviceIdType.MESH
HIGHEST = lax.Precision.HIGHEST

_SMALL_ROWS = (("b_ada", 0, 24), ("g_pre", 0, 8), ("b_ada", 1, 24), ("g_pre", 1, 8), ("g_post", None, 16),
               ("pool_w", None, 1024), ("pool_scale", None, 8), ("lb_logits", None, 16), ("hgrn_norm_g", None, 2))
SMALL_LATE_ROWS = 32
SMALL_ROWS_PAD = 1136
LB_ROW0 = 32 + 32 + 16 + 1024 + 8


def _params(**kw):
    return pltpu.CompilerParams(vmem_limit_bytes=VMEM_LIMIT, **kw)


def _sigmoid(v):
    return 1.0 / (1.0 + jnp.exp(-v))


def _dsilu(v, s):
    return s * (1.0 + v * (1.0 - s))


def _dot(a, b):
    return jnp.dot(a.astype(MXU_DTYPE), b.astype(MXU_DTYPE), preferred_element_type=F32)


def _dot_nt(a, b):
    return lax.dot_general(a.astype(MXU_DTYPE), b.astype(MXU_DTYPE), (((1,), (1,)), ((), ())),
                           preferred_element_type=F32)


def _dot_tn(a, b):
    return lax.dot_general(a.astype(MXU_DTYPE), b.astype(MXU_DTYPE), (((0,), (0,)), ((), ())),
                           preferred_element_type=F32)


def _pallas_after(body, n_in, after, *, in_specs, **kw):
    if after is None:
        return pl.pallas_call(body, in_specs=in_specs, **kw)

    def tied(*refs):
        body(*refs[:n_in], *refs[n_in + 1:])

    call = pl.pallas_call(tied, in_specs=list(in_specs) + [pl.BlockSpec(memory_space=pl.ANY)], **kw)
    return lambda *operands: call(*operands, after)


def _my_position():
    mx, my, mc = lax.axis_index("x"), lax.axis_index("y"), lax.axis_index("c")
    return mx, my, mc, 4 * mx + 2 * my + mc


def _peer(mx, my, mc, k):
    px = 1 - mx if (k >> 2) & 1 else mx
    py = 1 - my if (k >> 1) & 1 else my
    pc = 1 - mc if k & 1 else mc
    return (px, py, pc), 4 * px + 2 * py + pc


def _allgather_small(v, name, after=None):
    rows, cols = v.shape

    def body(v_ref, out_ref, send_sems, recv_sems):
        mx, my, mc, me = _my_position()
        out_ref[me] = v_ref[...]
        copies = []
        for k in range(1, N_DEV):
            peer, _ = _peer(mx, my, mc, k)
            cp = pltpu.make_async_remote_copy(
                src_ref=v_ref, dst_ref=out_ref.at[me],
                send_sem=send_sems.at[k - 1], recv_sem=recv_sems.at[k - 1],
                device_id=peer, device_id_type=MESH_ID)
            cp.start()
            copies.append(cp)
        for cp in copies:
            cp.wait()

    return _pallas_after(
        body, 1, after, name=name,
        out_shape=jax.ShapeDtypeStruct((N_DEV, rows, cols), v.dtype),
        in_specs=[pl.BlockSpec(memory_space=pltpu.VMEM)],
        out_specs=pl.BlockSpec(memory_space=pltpu.VMEM),
        scratch_shapes=[pltpu.SemaphoreType.DMA((N_DEV - 1,)), pltpu.SemaphoreType.DMA((N_DEV - 1,))],
        compiler_params=_params(),
    )(v)


class _Stream:
    def __init__(self, n, plan):
        self.n, self.plan = n, plan


def _comm_call(name, bufs, start=(), wait=(), after=None):
    names = list(bufs)

    def body(*refs):
        it = iter(refs)
        buf_refs = {n: next(it) for n in names}
        wait_sems = [(next(it), next(it)) for _ in wait]
        if after is not None:
            next(it)
        start_sems = [(next(it), next(it)) for _ in start]
        for _ in names:
            next(it)
        token = next(it)
        pos = _my_position()

        def descriptors(stream, sems):
            return [pltpu.make_async_remote_copy(src_ref=src, dst_ref=dst, send_sem=sems[0].at[k], recv_sem=sems[1].at[k],
                                                 device_id=dev, device_id_type=MESH_ID)
                    for k, (src, dst, dev) in enumerate(stream.plan(buf_refs, pos))]

        for (stream, _), sems in zip(wait, wait_sems):
            for cp in descriptors(stream, sems):
                cp.wait_send()
                cp.wait_recv()
        for stream, sems in zip(start, start_sems):
            for cp in descriptors(stream, sems):
                cp.start()
        token[...] = jnp.zeros_like(token)

    hbm = pl.BlockSpec(memory_space=pltpu.HBM)
    sem = pl.BlockSpec(memory_space=pltpu.SEMAPHORE)
    operands = [pltpu.with_memory_space_constraint(bufs[n], pltpu.HBM) for n in names]
    in_specs = [hbm] * len(names)
    for _, (send_sems, recv_sems) in wait:
        operands += [send_sems, recv_sems]
        in_specs += [sem, sem]
    if after is not None:
        operands.append(after)
        in_specs.append(pl.BlockSpec(memory_space=pl.ANY))
    out_shape, out_specs = [], []
    for stream in start:
        out_shape += [pltpu.SemaphoreType.DMA((stream.n,)), pltpu.SemaphoreType.DMA((stream.n,))]
        out_specs += [sem, sem]
    n_sem_out = len(out_shape)
    out_shape += [pltpu.HBM(bufs[n].shape, bufs[n].dtype) for n in names]
    out_specs += [hbm] * len(names)
    out_shape.append(jax.ShapeDtypeStruct((8, 128), F32))
    out_specs.append(pl.BlockSpec(memory_space=pltpu.VMEM))
    outs = pl.pallas_call(
        body, name=name, out_shape=out_shape, in_specs=in_specs, out_specs=out_specs,
        input_output_aliases={i: n_sem_out + i for i in range(len(names))},
        compiler_params=pltpu.CompilerParams(has_side_effects=pltpu.SideEffectType.DATAFLOW_SIDE_EFFECTING),
    )(*operands)
    sems = [(outs[2 * i], outs[2 * i + 1]) for i in range(len(start))]
    return dict(zip(names, outs[n_sem_out:n_sem_out + len(names)])), sems, outs[-1]


def _with_own_slot(block, me):
    return lax.dynamic_update_index_in_dim(lax.empty((N_DEV,) + block.shape, block.dtype), block, me, 0)


def _other_chips(pos):
    mx, my, _, _ = pos
    return [(1 - mx if i & 2 else mx, 1 - my if i & 1 else my) for i in (1, 2, 3)]


def _dev_index(px, py, pc):
    return 4 * px + 2 * py + pc


def _gather_streams(keys):
    def to_chips(refs, pos):
        _, _, mc, me = pos
        return [(refs["s_" + k], refs["g_" + k].at[me], (cx, cy, mc)) for k in keys for cx, cy in _other_chips(pos)]

    def to_sibling(refs, pos):
        mx, my, mc, me = pos
        return [(refs["s_" + k], refs["g_" + k].at[me], (mx, my, 1 - mc)) for k in keys]

    def pass_on(refs, pos):
        mx, my, mc, _ = pos
        out = []
        for k in keys:
            for cx, cy in _other_chips(pos):
                slot = refs["g_" + k].at[_dev_index(cx, cy, mc)]
                out.append((slot, slot, (mx, my, 1 - mc)))
        return out

    return _Stream(3 * len(keys), to_chips), _Stream(len(keys), to_sibling), _Stream(3 * len(keys), pass_on)


def _direct_gather_stream(key):
    def plan(refs, pos):
        mx, my, mc, me = pos
        return [(refs["s_" + key], refs["g_" + key].at[me], _peer(mx, my, mc, k)[0]) for k in range(1, N_DEV)]

    return _Stream(N_DEV - 1, plan)


def _scatter_streams(keys):
    def pair(refs, pos):
        mx, my, mc, _ = pos
        sib = (mx, my, 1 - mc)
        out = []
        for k in keys:
            for i, (cx, cy) in enumerate(_other_chips(pos)):
                out.append((refs["g_" + k].at[_dev_index(cx, cy, 1 - mc)], refs["st_" + k].at[i], sib))
            out.append((refs["g_" + k].at[_dev_index(mx, my, 1 - mc)], refs["st_" + k].at[3], sib))
        return out

    def chips(refs, pos):
        mc = pos[2]
        return [(refs["ps_" + k].at[i], refs["ld_" + k].at[i], (cx, cy, mc))
                for k in keys for i, (cx, cy) in enumerate(_other_chips(pos))]

    return _Stream(4 * len(keys), pair), _Stream(3 * len(keys), chips)


def _pair_sum(g, st, idx, tr, name):
    _, rows, cols = g.shape

    def body(idx_ref, g_ref, st_ref, out_ref):
        out_ref[...] = (g_ref[...].astype(F32) + st_ref[...].astype(F32)).astype(out_ref.dtype)

    return pl.pallas_call(
        body, name=name,
        grid_spec=pltpu.PrefetchScalarGridSpec(
            num_scalar_prefetch=1, grid=(4, rows // tr),
            in_specs=[pl.BlockSpec((None, tr, cols), lambda j, i, idx_ref: (idx_ref[j], i, 0)),
                      pl.BlockSpec((None, tr, cols), lambda j, i, idx_ref: (j, i, 0))],
            out_specs=pl.BlockSpec((None, tr, cols), lambda j, i, idx_ref: (j, i, 0))),
        out_shape=jax.ShapeDtypeStruct((4, rows, cols), WIRE_DTYPE),
        compiler_params=_params(dimension_semantics=("parallel", "parallel")),
    )(idx, g, st)


def _ada_fwd(c_all, w_ada, b_cols):
    def body(c_ref, w_ref, b_ref, out_ref):
        cv = c_ref[...]
        ca = cv * _sigmoid(cv)
        for l in range(DEPTH):
            out_ref[l] = jnp.dot(ca, w_ref[l], precision=HIGHEST, preferred_element_type=F32) + b_ref[l:l + 1, :]

    return pl.pallas_call(
        body, name="ada_fwd",
        out_shape=jax.ShapeDtypeStruct((DEPTH, N_DEV, ADA_COLS), F32),
        compiler_params=_params(),
    )(c_all, w_ada, b_cols)


def _ada_bwd(c_all, d_cols):
    def body(c_ref, d_ref, out_ref):
        cv = c_ref[...]
        ca = cv * _sigmoid(cv)
        for l in range(DEPTH):
            out_ref[l] = lax.dot_general(ca, d_ref[l], (((0,), (0,)), ((), ())), precision=HIGHEST,
                                         preferred_element_type=F32)

    return pl.pallas_call(
        body, name="ada_bwd",
        out_shape=jax.ShapeDtypeStruct((DEPTH, D_MODEL, ADA_COLS), F32),
        compiler_params=_params(),
    )(c_all, d_cols)


def _lower_bounds(logits):
    m = jnp.maximum(logits[0:1], logits[1:2])
    e0, e1 = jnp.exp(logits[0:1] - m), jnp.exp(logits[1:2] - m)
    den = e0 + e1
    p0, p1 = e0 / den, e1 / den
    low0 = p0 - p0
    low1 = (p0 + p1) - p0
    return (p0, p1), (low0, low1)


def _lb_fwd(lb_logits):
    def body(lg_ref, out_ref):
        _, (low0, low1) = _lower_bounds(lg_ref[...])
        out_ref[0:1, :] = jnp.clip(low0, 0.0, 1.0)
        out_ref[1:2, :] = jnp.clip(low1, 0.0, 1.0)

    return pl.pallas_call(body, name="lb_fwd", out_shape=jax.ShapeDtypeStruct(lb_logits.shape, F32),
                          compiler_params=_params())(lb_logits)


def _row_spec(cols=D_MODEL):
    return pl.BlockSpec((1, cols), lambda *_: (0, 0))


def _prenorm_fwd(x, g, shift, scale, tm, name, after=None):
    seq = x.shape[0]

    def body(x_ref, g_ref, sh_ref, sc_ref, h_ref):
        xv = x_ref[...]
        rs = lax.rsqrt(jnp.mean(xv * xv, axis=-1, keepdims=True) + NORM_EPS)
        h = (xv * rs * g_ref[...]) * (1.0 + sc_ref[...]) + sh_ref[...]
        h_ref[...] = h.astype(h_ref.dtype)

    tile = pl.BlockSpec((tm, D_MODEL), lambda i: (i, 0))
    return _pallas_after(
        body, 4, after, name=name, grid=(seq // tm,),
        in_specs=[tile, _row_spec(), _row_spec(), _row_spec()], out_specs=tile,
        out_shape=jax.ShapeDtypeStruct((seq, D_MODEL), MXU_DTYPE),
        compiler_params=_params(dimension_semantics=("parallel",)),
    )(x, g, shift, scale)


def _in_proj(h, win_g, tm, name, after=None):
    seq = h.shape[0]

    def body(h_ref, w_ref, z_ref, w_pair):
        @pl.when(pl.program_id(1) == 0)
        def _():
            w_pair[...] = jnp.concatenate([w_ref[0], w_ref[1]], axis=1)

        z_ref[...] = jnp.dot(h_ref[...], w_pair[...], preferred_element_type=F32)

    return _pallas_after(
        body, 2, after, name=name, grid=(N_DEV // 2, seq // tm),
        in_specs=[pl.BlockSpec((tm, D_MODEL), lambda j, i: (i, 0)),
                  pl.BlockSpec((2, D_MODEL, IN_COLS), lambda j, i: (j, 0, 0))],
        out_specs=pl.BlockSpec((tm, 2 * IN_COLS), lambda j, i: (i, j)),
        out_shape=jax.ShapeDtypeStruct((seq, IN_WIDTH), F32),
        scratch_shapes=[pltpu.VMEM((D_MODEL, 2 * IN_COLS), MXU_DTYPE)],
        compiler_params=_params(dimension_semantics=("parallel", "arbitrary")),
    )(h, win_g)


def _shift_down(v, j, pos):
    return jnp.where(pos >= j, pltpu.roll(v, j, 0), 0.0)


def _shift_up(v, j, pos, seq):
    return jnp.where(pos < seq - j, pltpu.roll(v, seq - j, 0), 0.0)


def _select_window(g, candidates):
    out = candidates[-1]
    for i in range(len(candidates) - 2, -1, -1):
        out = jnp.where(g == i, candidates[i], out)
    return out


def _pool_mean_minus_token(u, g, pos):
    sums, acc = [], u
    for j in (1, 2, 4, 8):
        acc = acc + _shift_down(acc, j, pos)
        sums.append(acc)
    wsum = _select_window(g, sums)
    width = jnp.left_shift(2, g).astype(F32)
    count = jnp.minimum(pos.astype(F32) + 1.0, width)
    return wsum / count - u, count


def _pool_fwd(z, pool_w_l, pool_scale_l, name, after=None):
    seq = z.shape[0]

    def body(pv_ref, pg_ref, w_ref, sc_ref, out_ref):
        g = pl.program_id(0)
        pos = lax.broadcasted_iota(jnp.int32, (seq, GROUP_DIM), 0)
        pm, _ = _pool_mean_minus_token(pv_ref[...], g, pos)
        lin = _dot(pm, w_ref[...]) * sc_ref[...]
        pg = pg_ref[...]
        out_ref[...] = (lin * (pg * _sigmoid(pg))).astype(out_ref.dtype)

    return _pallas_after(
        body, 4, after, name=name, grid=(POOL_GROUPS,),
        in_specs=[pl.BlockSpec((seq, GROUP_DIM), lambda g: (0, g)),
                  pl.BlockSpec((seq, GROUP_DIM), lambda g: (0, POOL_GROUPS + g)),
                  pl.BlockSpec((None, GROUP_DIM, GROUP_DIM), lambda g: (g, 0, 0)),
                  pl.BlockSpec((1, GROUP_DIM), lambda g: (0, g))],
        out_specs=pl.BlockSpec((seq, GROUP_DIM), lambda g: (0, g)),
        out_shape=jax.ShapeDtypeStruct((seq, POOL_WIDTH), MXU_DTYPE),
        compiler_params=_params(dimension_semantics=("parallel",)),
    )(z, z, pool_w_l, pool_scale_l)


def _chunk_masks():
    row = lax.broadcasted_iota(jnp.int32, (CHUNK, CHUNK), 0)
    col = lax.broadcasted_iota(jnp.int32, (CHUNK, CHUNK), 1)
    causal = row >= col
    before_sub = col < (row // SUB) * SUB
    suffix = row <= col
    return causal, before_sub, suffix


def _masked_sums(masks, v):
    lhs = jnp.concatenate([m.astype(jnp.bfloat16) for m in masks], axis=0)
    hi = v.astype(jnp.bfloat16)
    rest = v - hi.astype(F32)
    mid = rest.astype(jnp.bfloat16)
    lo = (rest - mid.astype(F32)).astype(jnp.bfloat16)
    out = jnp.dot(lhs, hi, preferred_element_type=F32)
    out += jnp.dot(lhs, mid, preferred_element_type=F32)
    out += jnp.dot(lhs, lo, preferred_element_type=F32)
    return [out[i * CHUNK:(i + 1) * CHUNK] for i in range(len(masks))]


def _gates(zf, lb):
    sg = _sigmoid(zf)
    f = lb + (1.0 - lb) * sg
    logf = jnp.log(jnp.maximum(f, LOG_FLOOR))
    return sg, f, logf


def _intra_blocks(q_h, k_h, cum_h, base_h, causal):
    rel = cum_h - base_h
    out = []
    for i in range(N_SUB):
        rows = slice(i * SUB, (i + 1) * SUB)
        e_q = jnp.exp(rel[rows])
        base_i = jnp.concatenate([base_h[rows]] * N_SUB, axis=0)
        e_k = jnp.exp(jnp.minimum(base_i - cum_h, EXP_CLAMP))
        q_t = (q_h[rows] * e_q).astype(MXU_DTYPE)
        k_t = (k_h * e_k).astype(MXU_DTYPE)
        a_i = jnp.where(causal[rows], _dot_nt(q_t, k_t), 0.0)
        out.append((q_t, k_t, e_q, e_k, a_i))
    return out


def _hgrn_fwd(z, lb_l, gn_l, name, after=None):
    seq = z.shape[0]
    n_chunks = seq // CHUNK
    per_step = min(FWD_STEP_CHUNKS, n_chunks)
    rows_per_step = per_step * CHUNK

    def body(hq_ref, hf_ref, hi_ref, hg_ref, lb_ref, gn_ref, o_ref, bin_ref, st_ref, cb_ref, state):
        @pl.when(pl.program_id(0) == 0)
        def _():
            state[...] = jnp.zeros_like(state)

        causal, before_sub, _ = _chunk_masks()
        for cc in range(per_step):
            rows = slice(cc * CHUNK, (cc + 1) * CHUNK)
            _, f, logf = _gates(hf_ref[rows, :], lb_ref[...])
            kk = 1.0 - f
            hq = hq_ref[rows, :]
            q = hq * _sigmoid(hq)
            cum, base = _masked_sums([causal, before_sub], logf)
            cb_ref[rows, 0:D_MODEL] = cum
            cb_ref[rows, D_MODEL:2 * D_MODEL] = base
            st_ref[cc] = state[...]
            for h in range(HEADS):
                sl = slice(h * HEAD_DIM, (h + 1) * HEAD_DIM)
                q_h, k_h, cum_h = q[:, sl], kk[:, sl], cum[:, sl]
                v_h = hi_ref[rows, sl]
                st_h = state[h]
                blocks = _intra_blocks(q_h, k_h, cum_h, base[:, sl], causal)
                a = jnp.concatenate([b[4] for b in blocks], axis=0)
                o_h = _dot_nt(q_h * jnp.exp(cum_h), st_h) + _dot(a, v_h)
                last = jnp.sum(logf[:, sl], axis=0, keepdims=True)
                state[h] = st_h * jnp.exp(last) + _dot_tn(v_h, k_h * jnp.exp(last - cum_h))
                rs = lax.rsqrt(jnp.mean(o_h * o_h, axis=-1, keepdims=True) + NORM_EPS)
                hg = hg_ref[rows, sl]
                o_ref[rows, sl] = o_h
                bin_ref[rows, sl] = ((o_h * rs * gn_ref[...]) * (hg * _sigmoid(hg))).astype(bin_ref.dtype)

    def col(block):
        return pl.BlockSpec((rows_per_step, D_MODEL), lambda c: (c, block))

    tile = pl.BlockSpec((rows_per_step, D_MODEL), lambda c: (c, 0))
    return _pallas_after(
        body, 6, after, name=name, grid=(n_chunks // per_step,),
        in_specs=[col(COL_HQ), col(COL_HF), col(COL_HI), col(COL_HG), _row_spec(), _row_spec(HEAD_DIM)],
        out_specs=[tile, tile, pl.BlockSpec((per_step, HEADS, HEAD_DIM, HEAD_DIM), lambda c: (c, 0, 0, 0)),
                   pl.BlockSpec((rows_per_step, 2 * D_MODEL), lambda c: (c, 0))],
        out_shape=[jax.ShapeDtypeStruct((seq, D_MODEL), F32),
                   jax.ShapeDtypeStruct((seq, D_MODEL), MXU_DTYPE),
                   jax.ShapeDtypeStruct((n_chunks, HEADS, HEAD_DIM, HEAD_DIM), F32),
                   jax.ShapeDtypeStruct((seq, 2 * D_MODEL), F32)],
        scratch_shapes=[pltpu.VMEM((HEADS, HEAD_DIM, HEAD_DIM), F32)],
        compiler_params=_params(dimension_semantics=("arbitrary",)),
    )(z, z, z, z, lb_l, gn_l)


def _rms_parts(y):
    rs = lax.rsqrt(jnp.mean(y * y, axis=-1, keepdims=True) + NORM_EPS)
    return rs, y * rs


def _merge_fwd(a_in, b_in, z, x, wpo_g, who_g, wout_g, gate, g_post, tm, name, target=None):
    seq = x.shape[0]
    with_loss = target is not None

    def body(*refs):
        a_ref, b_ref, mgp_ref, mgh_ref, x_ref, wpo_ref, who_ref, wout_ref, gate_ref, gp_ref = refs[:10]
        ba_ref, bb_ref, mer_ref, y_ref, last_ref = refs[10 + with_loss:15 + with_loss]
        a = a_ref[...]
        ba = _dot(a, jnp.concatenate([wpo_ref[j] for j in range(N_DEV)], axis=1))
        bb = _dot(b_ref[...], who_ref[...])
        merged = _sigmoid(mgp_ref[...]) * ba + _sigmoid(mgh_ref[...]) * bb
        y = _dot(merged, wout_ref[...])
        _, yn = _rms_parts(y)
        ba_ref[...] = ba.astype(ba_ref.dtype)
        bb_ref[...] = bb.astype(bb_ref.dtype)
        mer_ref[...] = merged.astype(mer_ref.dtype)
        y_ref[...] = y.astype(y_ref.dtype)
        x_next = x_ref[...] + gate_ref[...] * (yn * gp_ref[...])
        if not with_loss:
            last_ref[...] = x_next
            return
        loss_ref = refs[16]

        @pl.when(pl.program_id(0) == 0)
        def _():
            loss_ref[...] = jnp.zeros_like(loss_ref)

        err = x_next - refs[10][...]
        loss_ref[...] += 0.5 * jnp.sum(jnp.mean(err * err, axis=-1, keepdims=True), axis=0, keepdims=True)
        last_ref[...] = err * (1.0 / D_MODEL)

    def tile(cols=D_MODEL, block=0):
        return pl.BlockSpec((tm, cols), lambda i: (i, block))

    full = pl.BlockSpec((D_MODEL, D_MODEL), lambda i: (0, 0))
    act = jax.ShapeDtypeStruct((seq, D_MODEL), MXU_DTYPE)
    f32 = jax.ShapeDtypeStruct((seq, D_MODEL), F32)
    one = [pl.BlockSpec((1, 1), lambda i: (0, 0))] if with_loss else []
    return pl.pallas_call(
        body, name=name, grid=(seq // tm,),
        in_specs=[tile(POOL_WIDTH), tile(), tile(block=COL_MGP), tile(block=COL_MGH), tile(),
                  pl.BlockSpec((N_DEV, POOL_WIDTH, GROUP_DIM), lambda i: (0, 0, 0)),
                  full, full, _row_spec(), _row_spec()] + ([tile()] if with_loss else []),
        out_specs=[tile(), tile(), tile(), tile(), tile()] + one,
        out_shape=[act, act, act, act, f32] + ([jax.ShapeDtypeStruct((1, 1), F32)] if with_loss else []),
        compiler_params=_params(dimension_semantics=("arbitrary" if with_loss else "parallel",)),
    )(a_in, b_in, z, z, x, wpo_g, who_g, wout_g, gate, g_post, *([target] if with_loss else []))


def _stage_copy(stage, sems, dst, slot, step, where):
    rows, cols = where(step)
    return pltpu.make_async_copy(stage.at[slot], dst.at[rows, cols], sems.at[slot])


def _stage_begin(stage, sems, dst, step, where):
    slot = step % 2

    @pl.when(step >= 2)
    def _():
        _stage_copy(stage, sems, dst, slot, step - 2, where).wait()

    return slot


def _stage_end(stage, sems, dst, step, n_steps, where):
    slot = step % 2
    _stage_copy(stage, sems, dst, slot, step, where).start()

    @pl.when(step == n_steps - 1)
    def _():
        _stage_copy(stage, sems, dst, slot, step, where).wait()
        if n_steps > 1:
            _stage_copy(stage, sems, dst, 1 - slot, step - 1, where).wait()


def _merge_bwd(dx, y, ba, bb, z, wpo_g, who_g, wout_g, gate, g_post, dz, tm, name):
    seq = dx.shape[0]
    n_steps = seq // tm

    def body(dx_ref, y_ref, ba_ref, bb_ref, mgp_ref, mgh_ref, wpo_ref, who_ref, wout_ref, gate_ref, gp_ref, _,
             dy_ref, dba_ref, dbb_ref, da_ref, db_ref, dz_ref, acc_ref, stage, sems):
        step = pl.program_id(0)

        @pl.when(step == 0)
        def _():
            acc_ref[...] = jnp.zeros_like(acc_ref)

        def where(t):
            return pl.ds(t * tm, tm), pl.ds(COL_MGP * D_MODEL, 2 * D_MODEL)

        dmg_ref = stage.at[_stage_begin(stage, sems, dz_ref, step, where)]

        dxv = dx_ref[...]
        rs, yn = _rms_parts(y_ref[...].astype(F32))
        acc_ref[0:1, :] += jnp.sum(dxv * yn * gp_ref[...], axis=0, keepdims=True)
        acc_ref[1:2, :] += jnp.sum(dxv * gate_ref[...] * yn, axis=0, keepdims=True)
        dyn = dxv * (gate_ref[...] * gp_ref[...])
        dy = rs * (dyn - yn * jnp.mean(dyn * yn, axis=-1, keepdims=True))
        dmerged = _dot_nt(dy, wout_ref[...])
        sp, sh = _sigmoid(mgp_ref[...]), _sigmoid(mgh_ref[...])
        dba, dbb = sp * dmerged, sh * dmerged
        dmg_ref[:, 0:D_MODEL] = (dmerged * ba_ref[...].astype(F32) * sp * (1.0 - sp)).astype(dmg_ref.dtype)
        dmg_ref[:, D_MODEL:2 * D_MODEL] = (dmerged * bb_ref[...].astype(F32) * sh * (1.0 - sh)).astype(dmg_ref.dtype)
        da = _dot_nt(dba, jnp.concatenate([wpo_ref[j] for j in range(N_DEV)], axis=1))
        dy_ref[...] = dy.astype(dy_ref.dtype)
        dba_ref[...] = dba.astype(dba_ref.dtype)
        dbb_ref[...] = dbb.astype(dbb_ref.dtype)
        da_ref[...] = da.astype(da_ref.dtype)
        db_ref[...] = _dot_nt(dbb, who_ref[...]).astype(db_ref.dtype)
        _stage_end(stage, sems, dz_ref, step, n_steps, where)

    def tile(cols=D_MODEL, block=0):
        return pl.BlockSpec((tm, cols), lambda i: (i, block))

    full = pl.BlockSpec((D_MODEL, D_MODEL), lambda i: (0, 0))
    hbm = pl.BlockSpec(memory_space=pl.ANY)
    act = jax.ShapeDtypeStruct((seq, D_MODEL), MXU_DTYPE)
    return pl.pallas_call(
        body, name=name, grid=(n_steps,),
        in_specs=[tile(), tile(), tile(), tile(), tile(block=COL_MGP), tile(block=COL_MGH),
                  pl.BlockSpec((N_DEV, POOL_WIDTH, GROUP_DIM), lambda i: (0, 0, 0)),
                  full, full, _row_spec(), _row_spec(), hbm],
        out_specs=[tile(), tile(), tile(), tile(POOL_WIDTH), tile(), hbm,
                   pl.BlockSpec((8, D_MODEL), lambda i: (0, 0))],
        out_shape=[act, act, act, jax.ShapeDtypeStruct((seq, POOL_WIDTH), MXU_DTYPE), act,
                   jax.ShapeDtypeStruct(dz.shape, dz.dtype),
                   jax.ShapeDtypeStruct((8, D_MODEL), F32)],
        input_output_aliases={11: 5},
        scratch_shapes=[pltpu.VMEM((2, tm, 2 * D_MODEL), MXU_DTYPE), pltpu.SemaphoreType.DMA((2,))],
        compiler_params=_params(dimension_semantics=("arbitrary",)),
    )(dx, y, ba, bb, z, z, wpo_g, who_g, wout_g, gate, g_post, dz)


def _grad_out_weights(merged, dy, b_in, dbb, a_in, dba, name):
    seq = merged.shape[0]
    tn = D_MODEL // 2
    per_step = tn // GROUP_DIM

    def body(mer_ref, dy_ref, b_ref, dbb_ref, a_ref, dba_ref, gout_ref, gho_ref, gpo_ref):
        gout_ref[...] = _dot_tn(mer_ref[...], dy_ref[...]).astype(gout_ref.dtype)
        gho_ref[...] = _dot_tn(b_ref[...], dbb_ref[...]).astype(gho_ref.dtype)
        g_po = _dot_tn(a_ref[...], dba_ref[...])
        for j in range(per_step):
            gpo_ref[j] = g_po[:, j * GROUP_DIM:(j + 1) * GROUP_DIM].astype(gpo_ref.dtype)

    def whole(cols):
        return pl.BlockSpec((seq, cols), lambda j: (0, 0))

    cols = pl.BlockSpec((seq, tn), lambda j: (0, j))
    return pl.pallas_call(
        body, name=name, grid=(D_MODEL // tn,),
        in_specs=[whole(D_MODEL), cols, whole(D_MODEL), cols, whole(POOL_WIDTH), cols],
        out_specs=[pl.BlockSpec((D_MODEL, tn), lambda j: (0, j)), pl.BlockSpec((D_MODEL, tn), lambda j: (0, j)),
                   pl.BlockSpec((per_step, POOL_WIDTH, GROUP_DIM), lambda j: (j, 0, 0))],
        out_shape=[jax.ShapeDtypeStruct((D_MODEL, D_MODEL), WIRE_DTYPE),
                   jax.ShapeDtypeStruct((D_MODEL, D_MODEL), WIRE_DTYPE),
                   jax.ShapeDtypeStruct((N_DEV, POOL_WIDTH, GROUP_DIM), WIRE_DTYPE)],
        compiler_params=_params(dimension_semantics=("parallel",)),
    )(merged, dy, b_in, dbb, a_in, dba)


def _hgrn_bwd(db_in, z, o, states, cum_base, lb_l, gn_l, dz, name, after=None):
    seq = z.shape[0]
    per_step = min(BWD_STEP_CHUNKS, seq // CHUNK)
    rows_per_step = per_step * CHUNK
    n_steps = seq // rows_per_step
    last_step = n_steps - 1

    def body(db_ref, hq_ref, hf_ref, hi_ref, hg_ref, o_ref, st_ref, cb_ref, lb_ref, gn_ref, _,
             dz_hbm, dlb_ref, dgn_ref, dstate, dq_buf, dk_buf, dg_buf, stage, sems):
        step = pl.program_id(0)

        @pl.when(step == 0)
        def _():
            dstate[...] = jnp.zeros_like(dstate)
            dlb_ref[...] = jnp.zeros_like(dlb_ref)
            dgn_ref[...] = jnp.zeros_like(dgn_ref)

        def one_chunk(cc, *args):
            one_chunk_body((db_ref, hq_ref, hf_ref, hi_ref, hg_ref, o_ref, st_ref, cb_ref, dlb_ref, dgn_ref, dstate,
                            dq_buf, dk_buf, dg_buf), cc, *args)

        def where(t):
            return pl.ds((last_step - t) * rows_per_step, rows_per_step), pl.ds(COL_HQ * D_MODEL, 4 * D_MODEL)

        dz_step = stage.at[_stage_begin(stage, sems, dz_hbm, step, where)]
        causal, before_sub, suffix = _chunk_masks()
        lb = lb_ref[...]
        gn = gn_ref[...]
        for cc in reversed(range(per_step)):
            one_chunk(cc, dz_step, causal, before_sub, suffix, lb, gn)
        _stage_end(stage, sems, dz_hbm, step, n_steps, where)

    def one_chunk_body(refs, cc, dz_step, causal, before_sub, suffix, lb, gn):
        (db_ref, hq_ref, hf_ref, hi_ref, hg_ref, o_ref, st_ref, cb_ref, dlb_ref, dgn_ref, dstate,
         dq_buf, dk_buf, dg_buf) = refs
        rows = slice(cc * CHUNK, (cc + 1) * CHUNK)
        dz_ref = dz_step.at[rows, :]
        dq_buf, dk_buf, dg_buf = dq_buf.at[cc], dk_buf.at[cc], dg_buf.at[cc]
        sg, f, logf = _gates(hf_ref[rows, :], lb)
        kk = 1.0 - f
        hq = hq_ref[rows, :]
        sq = _sigmoid(hq)
        q = hq * sq
        cum, base = cb_ref[rows, 0:D_MODEL], cb_ref[rows, D_MODEL:2 * D_MODEL]
        dgn = jnp.zeros((1, HEAD_DIM), F32)
        dlast = []
        for h in range(HEADS):
            sl = slice(h * HEAD_DIM, (h + 1) * HEAD_DIM)
            q_h, k_h, cum_h = q[:, sl], kk[:, sl], cum[:, sl]
            v_h = hi_ref[rows, sl]
            st_h = st_ref[cc, h]
            dst_h = dstate[h]
            rs, ohat = _rms_parts(o_ref[rows, sl])
            hg = hg_ref[rows, sl]
            shg = _sigmoid(hg)
            d_bin = db_ref[rows, sl].astype(F32)
            don = d_bin * (hg * shg)
            dgn += jnp.sum(don * ohat, axis=0, keepdims=True)
            dohat = don * gn
            do = rs * (dohat - ohat * jnp.mean(dohat * ohat, axis=-1, keepdims=True))
            dz_ref[:, 3 * D_MODEL + h * HEAD_DIM:3 * D_MODEL + (h + 1) * HEAD_DIM] = (
                d_bin * (ohat * gn) * _dsilu(hg, shg)).astype(dz_ref.dtype)
            last = cb_ref[(cc + 1) * CHUNK - 1:(cc + 1) * CHUNK, sl]
            g_in = jnp.exp(cum_h)
            d_out = jnp.exp(last - cum_h)
            q_bar, k_bar = q_h * g_in, k_h * d_out
            blocks = _intra_blocks(q_h, k_h, cum_h, base[:, sl], causal)
            a = jnp.concatenate([b[4] for b in blocks], axis=0)
            da = jnp.where(causal, _dot_nt(do, v_h), 0.0)
            dv = _dot_tn(a, do) + _dot_nt(k_bar, dst_h)
            dq_bar, dk_bar = _dot(do, st_h), _dot(v_h, dst_h)
            dk = dk_bar * d_out
            dq_parts, dg_parts = [], []
            dg_k = k_bar * dk_bar
            dlast.append(jnp.sum(k_bar * dk_bar, axis=0, keepdims=True)
                         + jnp.exp(last) * jnp.sum(st_h * dst_h, axis=0, keepdims=True))
            for i, (q_t, k_t, e_q, e_k, _) in enumerate(blocks):
                da_i = da[i * SUB:(i + 1) * SUB].astype(MXU_DTYPE)
                dq_t = _dot(da_i, k_t)
                dk_t = _dot_tn(da_i, q_t)
                dq_parts.append(dq_t * e_q)
                dk += dk_t * e_k
                dg_parts.append(q_t.astype(F32) * dq_t)
                dg_k += k_t.astype(F32) * dk_t
            dq = dq_bar * g_in + jnp.concatenate(dq_parts, axis=0)
            dg_buf[:, sl] = q_bar * dq_bar + jnp.concatenate(dg_parts, axis=0) - dg_k
            dstate[h] = dst_h * jnp.exp(last) + _dot_tn(do, q_bar)
            dq_buf[:, sl] = dq
            dk_buf[:, sl] = dk
            dz_ref[:, 2 * D_MODEL + h * HEAD_DIM:2 * D_MODEL + (h + 1) * HEAD_DIM] = dv.astype(dz_ref.dtype)
        dgn_ref[...] += dgn
        dq_all, dk_all = dq_buf[...], dk_buf[...]
        dlogf = _masked_sums([suffix], dg_buf[...])[0] + jnp.concatenate(dlast, axis=1)
        df = jnp.where(f > LOG_FLOOR, dlogf / f, 0.0) - dk_all
        dlb_ref[...] += jnp.sum(df * (1.0 - sg), axis=0, keepdims=True)
        dz_ref[:, 0:D_MODEL] = (dq_all * _dsilu(hq, sq)).astype(dz_ref.dtype)
        dz_ref[:, D_MODEL:2 * D_MODEL] = (df * (1.0 - lb) * sg * (1.0 - sg)).astype(dz_ref.dtype)

    def col(block):
        return pl.BlockSpec((rows_per_step, D_MODEL), lambda c: (last_step - c, block))

    hbm = pl.BlockSpec(memory_space=pl.ANY)
    return _pallas_after(
        body, 11, after, name=name, grid=(n_steps,),
        in_specs=[col(0), col(COL_HQ), col(COL_HF), col(COL_HI), col(COL_HG), col(0),
                  pl.BlockSpec((per_step, HEADS, HEAD_DIM, HEAD_DIM), lambda c: (last_step - c, 0, 0, 0)),
                  pl.BlockSpec((rows_per_step, 2 * D_MODEL), lambda c: (last_step - c, 0)),
                  _row_spec(), _row_spec(HEAD_DIM), hbm],
        out_specs=[hbm, _row_spec(), _row_spec(HEAD_DIM)],
        out_shape=[jax.ShapeDtypeStruct(dz.shape, dz.dtype),
                   jax.ShapeDtypeStruct((1, D_MODEL), F32), jax.ShapeDtypeStruct((1, HEAD_DIM), F32)],
        input_output_aliases={10: 0},
        scratch_shapes=[pltpu.VMEM((HEADS, HEAD_DIM, HEAD_DIM), F32)]
        + [pltpu.VMEM((per_step, CHUNK, D_MODEL), F32)] * 3
        + [pltpu.VMEM((2, rows_per_step, 4 * D_MODEL), MXU_DTYPE), pltpu.SemaphoreType.DMA((2,))],
        compiler_params=_params(dimension_semantics=("arbitrary",)),
    )(db_in, z, z, z, z, o, states, cum_base, lb_l, gn_l, dz)


def _pool_bwd(da_in, z, pool_w_l, pool_scale_l, dz, name, after=None):
    seq = z.shape[0]

    def body(da_ref, pv_ref, pg_ref, w_ref, sc_ref, _, dz_hbm, dw_ref, dsc_ref, stage_pv, stage_pg, sems_pv, sems_pg):
        g = pl.program_id(0)

        def where_pv(t):
            return pl.ds(0, seq), pl.ds(pl.multiple_of(t * GROUP_DIM, GROUP_DIM), GROUP_DIM)

        def where_pg(t):
            return pl.ds(0, seq), pl.ds(pl.multiple_of(POOL_WIDTH + t * GROUP_DIM, GROUP_DIM), GROUP_DIM)

        dpv_ref = stage_pv.at[_stage_begin(stage_pv, sems_pv, dz_hbm, g, where_pv)]
        dpg_ref = stage_pg.at[_stage_begin(stage_pg, sems_pg, dz_hbm, g, where_pg)]
        pos = lax.broadcasted_iota(jnp.int32, (seq, GROUP_DIM), 0)
        pm, count = _pool_mean_minus_token(pv_ref[...], g, pos)
        lin0 = _dot(pm, w_ref[...])
        pg = pg_ref[...]
        spg = _sigmoid(pg)
        da = da_ref[...].astype(F32)
        dlin = da * (pg * spg)
        dpg_ref[...] = (da * (lin0 * sc_ref[...]) * _dsilu(pg, spg)).astype(dpg_ref.dtype)
        dsc_ref[...] = jnp.sum(dlin * lin0, axis=0, keepdims=True)
        dl0 = dlin * sc_ref[...]
        dw_ref[...] = _dot_tn(pm, dl0)
        dpm = _dot_nt(dl0, w_ref[...])
        sums, acc = [], dpm / count
        for j in (1, 2, 4, 8):
            acc = acc + _shift_up(acc, j, pos, seq)
            sums.append(acc)
        dpv_ref[...] = (_select_window(g, sums) - dpm).astype(dpv_ref.dtype)
        _stage_end(stage_pv, sems_pv, dz_hbm, g, POOL_GROUPS, where_pv)
        _stage_end(stage_pg, sems_pg, dz_hbm, g, POOL_GROUPS, where_pg)

    grp = pl.BlockSpec((seq, GROUP_DIM), lambda g: (0, g))
    hbm = pl.BlockSpec(memory_space=pl.ANY)
    stage = pltpu.VMEM((2, seq, GROUP_DIM), MXU_DTYPE)
    return _pallas_after(
        body, 6, after, name=name, grid=(POOL_GROUPS,),
        in_specs=[grp, grp, pl.BlockSpec((seq, GROUP_DIM), lambda g: (0, POOL_GROUPS + g)),
                  pl.BlockSpec((None, GROUP_DIM, GROUP_DIM), lambda g: (g, 0, 0)),
                  pl.BlockSpec((1, GROUP_DIM), lambda g: (0, g)), hbm],
        out_specs=[hbm, pl.BlockSpec((None, GROUP_DIM, GROUP_DIM), lambda g: (g, 0, 0)),
                   pl.BlockSpec((1, GROUP_DIM), lambda g: (0, g))],
        out_shape=[jax.ShapeDtypeStruct(dz.shape, dz.dtype),
                   jax.ShapeDtypeStruct((POOL_GROUPS, GROUP_DIM, GROUP_DIM), F32),
                   jax.ShapeDtypeStruct((1, POOL_WIDTH), F32)],
        input_output_aliases={5: 0},
        scratch_shapes=[stage, stage, pltpu.SemaphoreType.DMA((2,)), pltpu.SemaphoreType.DMA((2,))],
        compiler_params=_params(dimension_semantics=("arbitrary",)),
    )(da_in, z, z, pool_w_l, pool_scale_l, dz)


def _in_proj_dw(h, dz, name, after=None):
    seq = h.shape[0]

    def body(h_ref, dz_ref, out_ref):
        pair = lax.dot_general(h_ref[...], dz_ref[...], (((0,), (0,)), ((), ())), preferred_element_type=F32)
        out_ref[0] = pair[:, 0:IN_COLS].astype(out_ref.dtype)
        out_ref[1] = pair[:, IN_COLS:].astype(out_ref.dtype)

    return _pallas_after(
        body, 2, after, name=name, grid=(N_DEV // 2,),
        in_specs=[pl.BlockSpec((seq, D_MODEL), lambda j: (0, 0)),
                  pl.BlockSpec((seq, 2 * IN_COLS), lambda j: (0, j))],
        out_specs=pl.BlockSpec((2, D_MODEL, IN_COLS), lambda j: (j, 0, 0)),
        out_shape=jax.ShapeDtypeStruct((N_DEV, D_MODEL, IN_COLS), WIRE_DTYPE),
        compiler_params=_params(dimension_semantics=("parallel",)),
    )(h, dz)


def _in_proj_dh(dz, win_g, tm, name, after=None):
    seq = dz.shape[0]

    def body(dz_ref, w_ref, dh_ref):
        @pl.when(pl.program_id(1) == 0)
        def _():
            dh_ref[...] = jnp.zeros_like(dh_ref)

        w_pair = jnp.concatenate([w_ref[0], w_ref[1]], axis=1)
        dh_ref[...] += lax.dot_general(dz_ref[...], w_pair, (((1,), (1,)), ((), ())), preferred_element_type=F32)

    return _pallas_after(
        body, 2, after, name=name, grid=(seq // tm, N_DEV // 2),
        in_specs=[pl.BlockSpec((tm, 2 * IN_COLS), lambda i, j: (i, j)),
                  pl.BlockSpec((2, D_MODEL, IN_COLS), lambda i, j: (j, 0, 0))],
        out_specs=pl.BlockSpec((tm, D_MODEL), lambda i, j: (i, 0)),
        out_shape=jax.ShapeDtypeStruct((seq, D_MODEL), F32),
        compiler_params=_params(dimension_semantics=("parallel", "arbitrary")),
    )(dz, win_g)


def _prenorm_bwd(x, dh, dx_res, g, scale, tm, name, after=None):
    seq = x.shape[0]

    def body(x_ref, dh_ref, dxr_ref, g_ref, sc_ref, dx_ref, acc_ref):
        @pl.when(pl.program_id(0) == 0)
        def _():
            acc_ref[...] = jnp.zeros_like(acc_ref)

        rs, xn = _rms_parts(x_ref[...])
        dh = dh_ref[...]
        acc_ref[0:1, :] += jnp.sum(dh, axis=0, keepdims=True)
        acc_ref[1:2, :] += jnp.sum(dh * (xn * g_ref[...]), axis=0, keepdims=True)
        dhn = dh * (1.0 + sc_ref[...])
        acc_ref[2:3, :] += jnp.sum(dhn * xn, axis=0, keepdims=True)
        dxn = dhn * g_ref[...]
        dx_ref[...] = rs * (dxn - xn * jnp.mean(dxn * xn, axis=-1, keepdims=True)) + dxr_ref[...]

    tile = pl.BlockSpec((tm, D_MODEL), lambda i: (i, 0))
    return _pallas_after(
        body, 5, after, name=name, grid=(seq // tm,),
        in_specs=[tile, tile, tile, _row_spec(), _row_spec()],
        out_specs=[tile, pl.BlockSpec((8, D_MODEL), lambda i: (0, 0))],
        out_shape=[jax.ShapeDtypeStruct((seq, D_MODEL), F32), jax.ShapeDtypeStruct((8, D_MODEL), F32)],
        compiler_params=_params(dimension_semantics=("arbitrary",)),
    )(x, dh, dx_res, g, scale)


def _adamw_math(w, g, m, v):
    m = ADAM_B1 * m + (1.0 - ADAM_B1) * g
    v = ADAM_B2 * v + (1.0 - ADAM_B2) * (g * g)
    m_hat = m / (1.0 - ADAM_B1 ** ADAM_STEP)
    v_hat = v / (1.0 - ADAM_B2 ** ADAM_STEP)
    delta = -ADAM_LR * (m_hat / (jnp.sqrt(v_hat) + ADAM_EPS) + ADAM_WD * w)
    return delta, m, v


def _adamw_sharded(w, m, v, contrib, tr, name):
    depth, rows, cols = w.shape
    n_parts = contrib.shape[1]

    def body(w_ref, m_ref, v_ref, c_ref, g_ref, d_ref, mo_ref, vo_ref):
        g = c_ref[0].astype(F32)
        for p in range(1, n_parts):
            g += c_ref[p].astype(F32)
        delta, mn, vn = _adamw_math(w_ref[...], g, m_ref[...], v_ref[...])
        g_ref[...] = g
        d_ref[...] = delta
        mo_ref[...] = mn
        vo_ref[...] = vn

    tile = pl.BlockSpec((None, tr, cols), lambda l, i: (l, i, 0))
    shape = jax.ShapeDtypeStruct(w.shape, F32)
    return pl.pallas_call(
        body, name=name, grid=(depth, rows // tr),
        in_specs=[tile, tile, tile, pl.BlockSpec((None, n_parts, tr, cols), lambda l, i: (l, 0, i, 0))],
        out_specs=[tile] * 4, out_shape=[shape] * 4,
        compiler_params=_params(dimension_semantics=("parallel", "parallel")),
    )(w, m, v, contrib)


def _adamw_layer(w, m, v, contribs, l, tr, name, prev=None):
    _, rows, cols = w.shape
    n = len(contribs)

    def body(*refs):
        w_ref, m_ref, v_ref = refs[:3]
        c_refs = refs[3:3 + n]
        g_ref, d_ref, mo_ref, vo_ref = refs[-4:]
        g = c_refs[0][...].astype(F32)
        for c_ref in c_refs[1:]:
            g += c_ref[...].astype(F32)
        delta, mn, vn = _adamw_math(w_ref[...], g, m_ref[...], v_ref[...])
        g_ref[...] = g
        d_ref[...] = delta
        mo_ref[...] = mn
        vo_ref[...] = vn

    tile = pl.BlockSpec((None, tr, cols), lambda i: (l, i, 0))
    in_specs = [tile, tile, tile] + [pl.BlockSpec((None, tr, cols), lambda i, s=slot: (s, i, 0)) for _, slot in contribs]
    operands = [w, m, v] + [arr for arr, _ in contribs]
    aliases = {}
    if prev is not None:
        aliases = {len(operands) + k: k for k in range(4)}
        in_specs += [pl.BlockSpec(memory_space=pl.ANY)] * 4
        operands += list(prev)
    shape = jax.ShapeDtypeStruct(w.shape, F32)
    return pl.pallas_call(
        body, name=name, grid=(rows // tr,), in_specs=in_specs, out_specs=[tile] * 4, out_shape=[shape] * 4,
        input_output_aliases=aliases,
        compiler_params=_params(dimension_semantics=("parallel",)),
    )(*operands)


def _adamw_small(w_pack, m_pack, v_pack, g_late, g_early, shapes):
    pieces, r = {}, 0
    for name, _, n in _SMALL_ROWS:
        pieces.setdefault(name, []).append((r, n))
        r += n
    names = list(pieces)

    def body(w_ref, m_ref, v_ref, gl_ref, ge_ref, *rest):
        outs, packs = rest[:4 * len(names)], rest[4 * len(names):]
        g_l, g_e = gl_ref[0][0:SMALL_LATE_ROWS], ge_ref[0]
        for d in range(1, N_DEV):
            g_l += gl_ref[d][0:SMALL_LATE_ROWS]
            g_e += ge_ref[d]
        g = jnp.concatenate([g_l, g_e], axis=0)
        w = w_ref[...]
        r0, r1, r2 = LB_ROW0, LB_ROW0 + 8, LB_ROW0 + 16
        lg0, lg1 = w[r0:r1], w[r1:r2]
        mx = jnp.maximum(lg0, lg1)
        e0, e1 = jnp.exp(lg0 - mx), jnp.exp(lg1 - mx)
        p0, p1 = e0 / (e0 + e1), e1 / (e0 + e1)
        low = ((p0 - p0), (p0 + p1) - p0)
        dlow = [g_rows * jnp.where((lo > 0.0) & (lo < 1.0), 1.0, jnp.where((lo == 0.0) | (lo == 1.0), 0.5, 0.0))
                for g_rows, lo in ((g[r0:r1], low[0]), (g[r1:r2], low[1]))]
        dp0 = (dlow[0] + dlow[1]) - (dlow[0] + dlow[1])
        dp1 = dlow[1]
        inner = p0 * dp0 + p1 * dp1
        g = jnp.concatenate([g[:r0], p0 * (dp0 - inner), p1 * (dp1 - inner), g[r2:]], axis=0)
        delta, mn, vn = _adamw_math(w, g, m_ref[...], v_ref[...])
        for kind, val in enumerate((g, delta, mn, vn)):
            packs[kind][...] = val
            for j, name in enumerate(names):
                out, at = outs[kind * len(names) + j], 0
                for start, n in pieces[name]:
                    if name in flat:
                        for r in range(n):
                            layer, c = divmod(at + r, flat[name])
                            out[layer:layer + 1, c * 128:(c + 1) * 128] = packs[kind][start + r:start + r + 1, :]
                    else:
                        out[at:at + n, :] = packs[kind][start:start + n, :]
                    at += n

    rows = {name: sum(n for _, n in pieces[name]) for name in names}
    flat = {name: rows[name] // DEPTH for name in names if len(shapes[name]) == 2}
    outs = pl.pallas_call(
        body, name="adamw_small",
        out_shape=[jax.ShapeDtypeStruct(shapes[name] if name in flat else (rows[name], 128), F32)
                   for _ in range(4) for name in names],
        scratch_shapes=[pltpu.VMEM(w_pack.shape, F32)] * 4, compiler_params=_params(),
    )(w_pack, m_pack, v_pack, g_late, g_early)
    return [{name: outs[kind * len(names) + j].reshape(shapes[name]) for j, name in enumerate(names)}
            for kind in range(4)]


def _pack_small(parts, first=0, last=len(_SMALL_ROWS)):
    rows = [(parts[name] if l is None else parts[name][l]).reshape(n, 128) for name, l, n in _SMALL_ROWS[first:last]]
    if last == len(_SMALL_ROWS):
        rows.append(jnp.zeros((SMALL_ROWS_PAD - sum(n for _, _, n in _SMALL_ROWS), 128), F32))
    return jnp.concatenate(rows, axis=0)


def kernel(x, c, w_ada, b_ada, g_pre, g_post, w_in, pool_w, pool_scale, lb_logits, hgrn_norm_g, w_pool_o, w_hgrn_o, w_out, loss_target, m_w_ada, m_b_ada, m_g_pre, m_g_post, m_w_in, m_pool_w, m_pool_scale, m_lb_logits, m_hgrn_norm_g, m_w_pool_o, m_w_hgrn_o, m_w_out, v_w_ada, v_b_ada, v_g_pre, v_g_post, v_w_in, v_pool_w, v_pool_scale, v_lb_logits, v_hgrn_norm_g, v_w_pool_o, v_w_hgrn_o, v_w_out):
    seq = x.shape[1]
    tm = min(1024, seq)
    tm_merge = min(512, seq)
    pos = _my_position()
    me = pos[3]

    big = dict(win=w_in, wpo=w_pool_o, who=w_hgrn_o, wout=w_out)
    units = [["win0"], ["wpo0", "who0", "wout0"], ["win1", "wpo1", "who1", "wout1"]]
    g_streams = [_gather_streams(keys) for keys in units]
    g_state = [None] * len(units)

    def gather_start(us, after, first=None):
        bufs = dict(first[0]) if first else {}
        for k in [k for u in us for k in units[u]]:
            arr = big[k[:-1]]
            bufs["s_" + k] = arr[int(k[-1])].astype(WIRE_DTYPE)
            bufs["g_" + k] = _with_own_slot(bufs["s_" + k], me)
        streams = ([first[1]] if first else []) + [s for u in us for s in g_streams[u][:2]]
        bufs, sems, token = _comm_call("gather_start_" + "_".join(map(str, us)), bufs, start=streams, after=after)
        if first:
            first_out, sems = ({k: bufs[k] for k in first[0]}, sems[0]), sems[1:]
        for n, u in enumerate(us):
            g_state[u] = dict(bufs={p + k: bufs[p + k] for k in units[u] for p in ("s_", "g_")},
                              sems=sems[2 * n:2 * n + 2])
        return (token, first_out) if first else token

    def gather_pass(u, after):
        st = g_state[u]
        to_chips, _, pass_on = g_streams[u]
        st["bufs"], (st["pass_sems"],), _ = _comm_call(f"gather_pass_{u}", st["bufs"], start=[pass_on],
                                                       wait=[(to_chips, st["sems"][0])], after=after)

    def gather_done(u, after=None):
        st = g_state[u]
        _, to_sibling, pass_on = g_streams[u]
        bufs, _, _ = _comm_call(f"gather_done_{u}", st["bufs"], after=after,
                                wait=[(to_sibling, st["sems"][1]), (pass_on, st["pass_sems"])])
        return {k: bufs["g_" + k] for k in units[u]}

    c_stream = _direct_gather_stream("c")
    token, (c_bufs, c_sems) = gather_start([0], None, first=(dict(s_c=c, g_c=_with_own_slot(c, me)), c_stream))
    c_bufs, _, _ = _comm_call("gather_c_done", c_bufs, wait=[(c_stream, c_sems)])
    c_all = c_bufs["g_c"].reshape(N_DEV, D_MODEL)
    b_cols = lax.dynamic_slice_in_dim(b_ada, me * ADA_COLS, ADA_COLS, axis=1)
    ada_part = _ada_fwd(c_all, w_ada, b_cols)
    gather_pass(0, ada_part)
    ada_all = _allgather_small(ada_part.reshape(DEPTH * N_DEV, ADA_COLS), "allgather_ada",
                               after=g_state[0]["bufs"]["g_win0"])
    ada = lax.dynamic_index_in_dim(ada_all.reshape(N_DEV, DEPTH, N_DEV, ADA_COLS), me, axis=2, keepdims=False)
    ada = jnp.transpose(ada, (1, 0, 2)).reshape(DEPTH, 3 * D_MODEL)
    shift = [ada[l:l + 1, 0:D_MODEL] for l in range(DEPTH)]
    scale = [ada[l:l + 1, D_MODEL:2 * D_MODEL] for l in range(DEPTH)]
    gate = [ada[l:l + 1, 2 * D_MODEL:] for l in range(DEPTH)]

    lb = _lb_fwd(lb_logits)

    gw = {}
    xs, saved = [x[0]], []
    for l in range(DEPTH):
        h = _prenorm_fwd(xs[l], g_pre[l:l + 1], shift[l], scale[l], tm, f"prenorm_fwd_{l}",
                         after=token if l == 0 else None)
        token = None
        if l == 0:
            gw.update(gather_done(0, h))
            token = gather_start([1, 2], gw["win0"])
        else:
            gw.update(gather_done(2, h))
        z = _in_proj(h, gw[f"win{l}"], min(1024, seq), f"in_proj_{l}", after=token)
        a_in = _pool_fwd(z, pool_w[l], pool_scale[l:l + 1], f"pool_fwd_{l}")
        o, b_in, states, cum_base = _hgrn_fwd(z, lb[l:l + 1], hgrn_norm_g[l:l + 1], f"hgrn_fwd_{l}")
        if l == 0:
            gather_pass(1, b_in)
            gw.update(gather_done(1))
        who_l = gw[f"who{l}"].reshape(D_MODEL, D_MODEL)
        wout_l = gw[f"wout{l}"].reshape(D_MODEL, D_MODEL)
        last = l == DEPTH - 1
        ba, bb, merged, y, *out = _merge_fwd(a_in, b_in, z, xs[l], gw[f"wpo{l}"], who_l, wout_l, gate[l],
                                             g_post[l:l + 1], tm_merge, f"merge_fwd_{l}",
                                             target=loss_target[0] if last else None)
        if last:
            dx, loss_part = out
        else:
            xs.append(out[0])
            gather_pass(2, out[0])
        saved.append((h, z, a_in, o, b_in, states, cum_base, ba, bb, merged, y, who_l, wout_l))


    chips = _other_chips(pos)
    pair_idx = jnp.stack([_dev_index(cx, cy, pos[2]) for cx, cy in chips] + [me]).astype(jnp.int32)
    pair_rows = dict(win=512, wpo=POOL_WIDTH, who=HEAD_DIM, wout=HEAD_DIM)

    def scatter_pair_start(u, grads):
        keys = list(grads)
        pair, to_chips = _scatter_streams(keys)
        bufs = {}
        for k in keys:
            bufs["g_" + k] = grads[k]
            bufs["st_" + k] = lax.empty((4,) + grads[k].shape[1:], WIRE_DTYPE)
        bufs, (sems,), token = _comm_call(f"scatter_pair_start_{u}", bufs, start=[pair])
        return dict(u=u, keys=keys, pair=pair, to_chips=to_chips, bufs=bufs, sems=sems, token=token)

    def scatter_pair_finish(st, after):
        u, keys = st["u"], st["keys"]
        bufs, _, _ = _comm_call(f"scatter_pair_done_{u}", st["bufs"], wait=[(st["pair"], st["sems"])], after=after)
        bufs2 = {}
        for k in keys:
            bufs2["ps_" + k] = _pair_sum(bufs["g_" + k], bufs["st_" + k], pair_idx, bufs["g_" + k].shape[1],
                                         f"pair_sum_{k}")
            bufs2["ld_" + k] = lax.empty((3,) + bufs["g_" + k].shape[1:], WIRE_DTYPE)
        st.update(bufs=bufs2)

    def scatter_chips_start(st, after=None):
        bufs2, (sems,), token = _comm_call(f"scatter_chips_start_{st['u']}", st["bufs"], start=[st["to_chips"]],
                                           after=after)
        st.update(bufs=bufs2, sems=sems, token=token)

    def scatter_finish(st, after):
        bufs, _, _ = _comm_call(f"scatter_chips_done_{st['u']}", st["bufs"], wait=[(st["to_chips"], st["sems"])],
                                after=after)
        return {k: [(bufs["ps_" + k], 3), (bufs["ld_" + k], 0), (bufs["ld_" + k], 1), (bufs["ld_" + k], 2)]
                for k in st["keys"]}

    moments = dict(win=(m_w_in, v_w_in), wpo=(m_w_pool_o, v_w_pool_o), who=(m_w_hgrn_o, v_w_hgrn_o),
                   wout=(m_w_out, v_w_out))
    big_out = {}

    def finish_unit(unit, after):
        for k, contribs in scatter_finish(scat[unit], after).items():
            wname, l = k[:-1], int(k[-1])
            big_out[wname] = _adamw_layer(big[wname], moments[wname][0], moments[wname][1], contribs, l,
                                          pair_rows[wname], f"adamw_{k}", prev=big_out.get(wname))
            after = big_out[wname][0]
        return after

    d_ada, small, scat = [None] * DEPTH, [None] * DEPTH, {}
    for l in reversed(range(DEPTH)):
        h, z, a_in, o, b_in, states, cum_base, ba, bb, merged, y, who_l, wout_l = saved[l]
        dy, dba, dbb, da_in, db_in, dz, acc_post = _merge_bwd(
            dx, y, ba, bb, z, gw[f"wpo{l}"], who_l, wout_l, gate[l], g_post[l:l + 1],
            lax.empty((seq, IN_WIDTH), MXU_DTYPE), tm_merge, f"merge_bwd_{l}")
        g_out, g_ho, g_po = _grad_out_weights(merged, dy, b_in, dbb, a_in, dba, f"grad_out_weights_{l}")
        g_small = {f"wout{l}": g_out.reshape(N_DEV, HEAD_DIM, D_MODEL),
                   f"who{l}": g_ho.reshape(N_DEV, HEAD_DIM, D_MODEL), f"wpo{l}": g_po}
        st_small = scat["small0"] = scatter_pair_start("small0", g_small) if l == 0 else None
        dz, dlb, dgn = _hgrn_bwd(db_in, z, o, states, cum_base, lb[l:l + 1], hgrn_norm_g[l:l + 1], dz, f"hgrn_bwd_{l}",
                                 after=st_small and st_small["token"])
        if l == 0:
            scatter_pair_finish(st_small, dlb)
            scatter_chips_start(st_small)
        dz, dpw, dps = _pool_bwd(da_in, z, pool_w[l], pool_scale[l:l + 1], dz, f"pool_bwd_{l}",
                                 after=st_small and st_small["token"])
        small[l] = dict(g_post=acc_post[1], pool_w=dpw, pool_scale=dps[0], lb_logits=dlb[0], hgrn_norm_g=dgn[0])
        token = None
        if l == 0:
            parts = {name: jnp.stack([small[0][name], small[1][name]]) for name in small[0]}
            parts.update(b_ada=[None, d_ada[1]], g_pre=[None, small[1]["g_pre"]])
            sg_stream = _direct_gather_stream("sg")
            early = _pack_small(parts, 2)
            sg_bufs, (sg_sems,), token = _comm_call(
                "small_grads_start", dict(s_sg=early, g_sg=_with_own_slot(early, me)), start=[sg_stream])
        g_win = {f"win{l}": _in_proj_dw(h, dz, f"grad_w_in_{l}", after=token)}
        st_win = scat[f"win{l}"] = scatter_pair_start(f"win{l}", g_win if l == 0 else {**g_small, **g_win})
        if l > 0:
            dh = _in_proj_dh(dz, gw[f"win{l}"], seq, f"in_proj_dh_{l}", after=st_win["token"])
            scatter_pair_finish(st_win, dh)
            scatter_chips_start(st_win)
        else:
            scatter_pair_finish(st_win, st_win["token"])
            scatter_chips_start(st_win)
            after = st_win["token"]
            for unit in ("win1", "small0"):
                after = finish_unit(unit, after)
            dh = _in_proj_dh(dz, gw[f"win{l}"], seq, f"in_proj_dh_{l}", after=after)
        dx, acc_pre = _prenorm_bwd(xs[l], dh, dx, g_pre[l:l + 1], scale[l], tm, f"prenorm_bwd_{l}",
                                   after=st_win["token"])
        d_ada[l] = jnp.concatenate([acc_pre[0], acc_pre[1], acc_post[0]])
        small[l]["g_pre"] = acc_pre[2]
    grad_x = dx[None]

    parts = dict(b_ada=[d_ada[0]], g_pre=[small[0]["g_pre"]])
    late = jnp.concatenate([_pack_small(parts, 0, 2), jnp.broadcast_to(loss_part, (8, 128))], axis=0)
    g_late = _allgather_small(late, "allgather_late_grads")
    loss = jnp.sum(g_late[:, SMALL_LATE_ROWS, 0])
    sg_bufs, _, _ = _comm_call("small_grads_done", sg_bufs, wait=[(sg_stream, sg_sems)], after=g_late)
    g_early = sg_bufs["g_sg"]
    small_names = list(dict.fromkeys(name for name, _, _ in _SMALL_ROWS))
    weights = dict(b_ada=b_ada, g_pre=g_pre, g_post=g_post, pool_w=pool_w, pool_scale=pool_scale,
                   lb_logits=lb_logits, hgrn_norm_g=hgrn_norm_g)
    m_small = dict(b_ada=m_b_ada, g_pre=m_g_pre, g_post=m_g_post, pool_w=m_pool_w, pool_scale=m_pool_scale,
                   lb_logits=m_lb_logits, hgrn_norm_g=m_hgrn_norm_g)
    v_small = dict(b_ada=v_b_ada, g_pre=v_g_pre, g_post=v_g_post, pool_w=v_pool_w, pool_scale=v_pool_scale,
                   lb_logits=v_lb_logits, hgrn_norm_g=v_hgrn_norm_g)
    shapes = {name: weights[name].shape for name in small_names}
    small_out = _adamw_small(_pack_small(weights), _pack_small(m_small), _pack_small(v_small), g_late, g_early,
                             shapes)

    d_ada_all = jnp.stack([g_late[:, 0:24, :].reshape(N_DEV, 3 * D_MODEL),
                           g_early[:, 0:24, :].reshape(N_DEV, 3 * D_MODEL)], axis=1)
    d_cols = jnp.transpose(lax.dynamic_slice_in_dim(d_ada_all, me * ADA_COLS, ADA_COLS, axis=2), (1, 0, 2))
    g_w_ada = _ada_bwd(c_all, d_cols)
    ada_out = _adamw_sharded(w_ada, m_w_ada, v_w_ada, g_w_ada[:, None], 256, "adamw_w_ada")
    finish_unit("win0", ada_out[1][0, 0:8, 0:128] + small_out[1]["pool_scale"][0:1, 0:128])

    def leaf(kind):
        s = small_out[kind]
        return (ada_out[kind], s["b_ada"], s["g_pre"], s["g_post"], big_out["win"][kind], s["pool_w"], s["pool_scale"],
                s["lb_logits"], s["hgrn_norm_g"], big_out["wpo"][kind], big_out["who"][kind], big_out["wout"][kind])

    return (loss, grad_x) + leaf(0) + leaf(1) + leaf(2) + leaf(3)
```

```python
import jax
import jax.numpy as jnp
from jax import lax
from jax.experimental import pallas as pl
from jax.experimental.pallas import tpu as pltpu

F32 = jnp.float32
MXU_DTYPE = jnp.bfloat16
WIRE_DTYPE = jnp.bfloat16

N_DEV = 8
DEPTH = 2
D_MODEL = 1024
HEADS = 8
HEAD_DIM = 128
POOL_GROUPS = 4
GROUP_DIM = 128
POOL_WIDTH = POOL_GROUPS * GROUP_DIM
IN_WIDTH = 7168
CHUNK = 64
SUB = 16
N_SUB = CHUNK // SUB
FWD_STEP_CHUNKS = 8
BWD_STEP_CHUNKS = 4
EXP_CLAMP = 80.0
NORM_EPS = 1e-6
LOG_FLOOR = 1e-30
ADA_COLS = 3 * D_MODEL // N_DEV
IN_COLS = IN_WIDTH // N_DEV
COL_HQ, COL_HF, COL_HI, COL_HG, COL_MGP, COL_MGH = 1, 2, 3, 4, 5, 6

ADAM_LR = 0.001
ADAM_B1 = 0.9
ADAM_B2 = 0.999
ADAM_EPS = 1e-08
ADAM_WD = 0.01
ADAM_STEP = 10

VMEM_LIMIT = 48 * 1024 * 1024
MESH_ID = pl.DeviceIdType.MESH
HIGHEST = lax.Precision.HIGHEST

_SMALL_ROWS = (("b_ada", 0, 24), ("g_pre", 0, 8), ("b_ada", 1, 24), ("g_pre", 1, 8), ("g_post", None, 16),
               ("pool_w", None, 1024), ("pool_scale", None, 8), ("lb_logits", None, 16), ("hgrn_norm_g", None, 2))
SMALL_LATE_ROWS = 32
SMALL_ROWS_PAD = 1136
LB_ROW0 = 32 + 32 + 16 + 1024 + 8


def _params(**kw):
    return pltpu.CompilerParams(vmem_limit_bytes=VMEM_LIMIT, **kw)


def _sigmoid(v):
    return 1.0 / (1.0 + jnp.exp(-v))


def _dsilu(v, s):
    return s * (1.0 + v * (1.0 - s))


def _dot(a, b):
    return jnp.dot(a.astype(MXU_DTYPE), b.astype(MXU_DTYPE), preferred_element_type=F32)


def _dot_nt(a, b):
    return lax.dot_general(a.astype(MXU_DTYPE), b.astype(MXU_DTYPE), (((1,), (1,)), ((), ())),
                           preferred_element_type=F32)


def _dot_tn(a, b):
    return lax.dot_general(a.astype(MXU_DTYPE), b.astype(MXU_DTYPE), (((0,), (0,)), ((), ())),
                           preferred_element_type=F32)


def _pallas_after(body, n_in, after, *, in_specs, **kw):
    if after is None:
        return pl.pallas_call(body, in_specs=in_specs, **kw)

    def tied(*refs):
        body(*refs[:n_in], *refs[n_in + 1:])

    call = pl.pallas_call(tied, in_specs=list(in_specs) + [pl.BlockSpec(memory_space=pl.ANY)], **kw)
    return lambda *operands: call(*operands, after)


def _my_position():
    mx, my, mc = lax.axis_index("x"), lax.axis_index("y"), lax.axis_index("c")
    return mx, my, mc, 4 * mx + 2 * my + mc


def _peer(mx, my, mc, k):
    px = 1 - mx if (k >> 2) & 1 else mx
    py = 1 - my if (k >> 1) & 1 else my
    pc = 1 - mc if k & 1 else mc
    return (px, py, pc), 4 * px + 2 * py + pc


def _allgather_small(v, name, after=None):
    rows, cols = v.shape

    def body(v_ref, out_ref, send_sems, recv_sems):
        mx, my, mc, me = _my_position()
        out_ref[me] = v_ref[...]
        copies = []
        for k in range(1, N_DEV):
            peer, _ = _peer(mx, my, mc, k)
            cp = pltpu.make_async_remote_copy(
                src_ref=v_ref, dst_ref=out_ref.at[me],
                send_sem=send_sems.at[k - 1], recv_sem=recv_sems.at[k - 1],
                device_id=peer, device_id_type=MESH_ID)
            cp.start()
            copies.append(cp)
        for cp in copies:
            cp.wait()

    return _pallas_after(
        body, 1, after, name=name,
        out_shape=jax.ShapeDtypeStruct((N_DEV, rows, cols), v.dtype),
        in_specs=[pl.BlockSpec(memory_space=pltpu.VMEM)],
        out_specs=pl.BlockSpec(memory_space=pltpu.VMEM),
        scratch_shapes=[pltpu.SemaphoreType.DMA((N_DEV - 1,)), pltpu.SemaphoreType.DMA((N_DEV - 1,))],
        compiler_params=_params(),
    )(v)


class _Stream:
    def __init__(self, n, plan):
        self.n, self.plan = n, plan


def _comm_call(name, bufs, start=(), wait=(), after=None):
    names = list(bufs)

    def body(*refs):
        it = iter(refs)
        buf_refs = {n: next(it) for n in names}
        wait_sems = [(next(it), next(it)) for _ in wait]
        if after is not None:
            next(it)
        start_sems = [(next(it), next(it)) for _ in start]
        for _ in names:
            next(it)
        token = next(it)
        pos = _my_position()

        def descriptors(stream, sems):
            return [pltpu.make_async_remote_copy(src_ref=src, dst_ref=dst, send_sem=sems[0].at[k], recv_sem=sems[1].at[k],
                                                 device_id=dev, device_id_type=MESH_ID)
                    for k, (src, dst, dev) in enumerate(stream.plan(buf_refs, pos))]

        for (stream, _), sems in zip(wait, wait_sems):
            for cp in descriptors(stream, sems):
                cp.wait_send()
                cp.wait_recv()
        for stream, sems in zip(start, start_sems):
            for cp in descriptors(stream, sems):
                cp.start()
        token[...] = jnp.zeros_like(token)

    hbm = pl.BlockSpec(memory_space=pltpu.HBM)
    sem = pl.BlockSpec(memory_space=pltpu.SEMAPHORE)
    operands = [pltpu.with_memory_space_constraint(bufs[n], pltpu.HBM) for n in names]
    in_specs = [hbm] * len(names)
    for _, (send_sems, recv_sems) in wait:
        operands += [send_sems, recv_sems]
        in_specs += [sem, sem]
    if after is not None:
        operands.append(after)
        in_specs.append(pl.BlockSpec(memory_space=pl.ANY))
    out_shape, out_specs = [], []
    for stream in start:
        out_shape += [pltpu.SemaphoreType.DMA((stream.n,)), pltpu.SemaphoreType.DMA((stream.n,))]
        out_specs += [sem, sem]
    n_sem_out = len(out_shape)
    out_shape += [pltpu.HBM(bufs[n].shape, bufs[n].dtype) for n in names]
    out_specs += [hbm] * len(names)
    out_shape.append(jax.ShapeDtypeStruct((8, 128), F32))
    out_specs.append(pl.BlockSpec(memory_space=pltpu.VMEM))
    outs = pl.pallas_call(
        body, name=name, out_shape=out_shape, in_specs=in_specs, out_specs=out_specs,
        input_output_aliases={i: n_sem_out + i for i in range(len(names))},
        compiler_params=pltpu.CompilerParams(has_side_effects=pltpu.SideEffectType.DATAFLOW_SIDE_EFFECTING),
    )(*operands)
    sems = [(outs[2 * i], outs[2 * i + 1]) for i in range(len(start))]
    return dict(zip(names, outs[n_sem_out:n_sem_out + len(names)])), sems, outs[-1]


def _with_own_slot(block, me):
    return lax.dynamic_update_index_in_dim(lax.empty((N_DEV,) + block.shape, block.dtype), block, me, 0)


def _other_chips(pos):
    mx, my, _, _ = pos
    return [(1 - mx if i & 2 else mx, 1 - my if i & 1 else my) for i in (1, 2, 3)]


def _dev_index(px, py, pc):
    return 4 * px + 2 * py + pc


def _gather_streams(keys):
    def to_chips(refs, pos):
        _, _, mc, me = pos
        return [(refs["g_" + k].at[me], refs["g_" + k].at[me], (cx, cy, mc))
                for k in keys for cx, cy in _other_chips(pos)]

    def to_sibling(refs, pos):
        mx, my, mc, me = pos
        return [(refs["g_" + k].at[me], refs["g_" + k].at[me], (mx, my, 1 - mc)) for k in keys]

    def pass_on(refs, pos):
        mx, my, mc, _ = pos
        out = []
        for k in keys:
            for cx, cy in _other_chips(pos):
                slot = refs["g_" + k].at[_dev_index(cx, cy, mc)]
                out.append((slot, slot, (mx, my, 1 - mc)))
        return out

    return _Stream(3 * len(keys), to_chips), _Stream(len(keys), to_sibling), _Stream(3 * len(keys), pass_on)


def _direct_gather_stream(key):
    def plan(refs, pos):
        mx, my, mc, me = pos
        return [(refs["s_" + key], refs["g_" + key].at[me], _peer(mx, my, mc, k)[0]) for k in range(1, N_DEV)]

    return _Stream(N_DEV - 1, plan)


def _scatter_streams(keys):
    def pair(refs, pos):
        mx, my, mc, _ = pos
        sib = (mx, my, 1 - mc)
        out = []
        for k in keys:
            for i, (cx, cy) in enumerate(_other_chips(pos)):
                out.append((refs["g_" + k].at[_dev_index(cx, cy, 1 - mc)], refs["st_" + k].at[i], sib))
            out.append((refs["g_" + k].at[_dev_index(mx, my, 1 - mc)], refs["st_" + k].at[3], sib))
        return out

    def chips(refs, pos):
        mc = pos[2]
        return [(refs["ps_" + k].at[i], refs["ld_" + k].at[i], (cx, cy, mc))
                for k in keys for i, (cx, cy) in enumerate(_other_chips(pos))]

    return _Stream(4 * len(keys), pair), _Stream(3 * len(keys), chips)


def _pair_sum(g, st, idx, tr, name):
    _, rows, cols = g.shape

    def body(idx_ref, g_ref, st_ref, out_ref):
        out_ref[...] = (g_ref[...].astype(F32) + st_ref[...].astype(F32)).astype(out_ref.dtype)

    return pl.pallas_call(
        body, name=name,
        grid_spec=pltpu.PrefetchScalarGridSpec(
            num_scalar_prefetch=1, grid=(4, rows // tr),
            in_specs=[pl.BlockSpec((None, tr, cols), lambda j, i, idx_ref: (idx_ref[j], i, 0)),
                      pl.BlockSpec((None, tr, cols), lambda j, i, idx_ref: (j, i, 0))],
            out_specs=pl.BlockSpec((None, tr, cols), lambda j, i, idx_ref: (j, i, 0))),
        out_shape=jax.ShapeDtypeStruct((4, rows, cols), WIRE_DTYPE),
        compiler_params=_params(dimension_semantics=("parallel", "parallel")),
    )(idx, g, st)


def _ada_fwd(c_all, w_ada, b_cols):
    def body(c_ref, w_ref, b_ref, out_ref):
        cv = c_ref[...]
        ca = cv * _sigmoid(cv)
        for l in range(DEPTH):
            out_ref[l] = jnp.dot(ca, w_ref[l], precision=HIGHEST, preferred_element_type=F32) + b_ref[l:l + 1, :]

    return pl.pallas_call(
        body, name="ada_fwd",
        out_shape=jax.ShapeDtypeStruct((DEPTH, N_DEV, ADA_COLS), F32),
        compiler_params=_params(),
    )(c_all, w_ada, b_cols)


def _ada_bwd(c_all, d_cols):
    def body(c_ref, d_ref, out_ref):
        cv = c_ref[...]
        ca = cv * _sigmoid(cv)
        for l in range(DEPTH):
            out_ref[l] = lax.dot_general(ca, d_ref[l], (((0,), (0,)), ((), ())), precision=HIGHEST,
                                         preferred_element_type=F32)

    return pl.pallas_call(
        body, name="ada_bwd",
        out_shape=jax.ShapeDtypeStruct((DEPTH, D_MODEL, ADA_COLS), F32),
        compiler_params=_params(),
    )(c_all, d_cols)


def _lower_bounds(logits):
    m = jnp.maximum(logits[0:1], logits[1:2])
    e0, e1 = jnp.exp(logits[0:1] - m), jnp.exp(logits[1:2] - m)
    den = e0 + e1
    p0, p1 = e0 / den, e1 / den
    low0 = p0 - p0
    low1 = (p0 + p1) - p0
    return (p0, p1), (low0, low1)


def _lb_fwd(lb_logits):
    def body(lg_ref, out_ref):
        _, (low0, low1) = _lower_bounds(lg_ref[...])
        out_ref[0:1, :] = jnp.clip(low0, 0.0, 1.0)
        out_ref[1:2, :] = jnp.clip(low1, 0.0, 1.0)

    return pl.pallas_call(body, name="lb_fwd", out_shape=jax.ShapeDtypeStruct(lb_logits.shape, F32),
                          compiler_params=_params())(lb_logits)


def _row_spec(cols=D_MODEL):
    return pl.BlockSpec((1, cols), lambda *_: (0, 0))


def _prenorm_fwd(x, g, shift, scale, tm, name, after=None):
    seq = x.shape[0]

    def body(x_ref, g_ref, sh_ref, sc_ref, h_ref):
        xv = x_ref[...]
        rs = lax.rsqrt(jnp.mean(xv * xv, axis=-1, keepdims=True) + NORM_EPS)
        h = (xv * rs * g_ref[...]) * (1.0 + sc_ref[...]) + sh_ref[...]
        h_ref[...] = h.astype(h_ref.dtype)

    tile = pl.BlockSpec((tm, D_MODEL), lambda i: (i, 0))
    return _pallas_after(
        body, 4, after, name=name, grid=(seq // tm,),
        in_specs=[tile, _row_spec(), _row_spec(), _row_spec()], out_specs=tile,
        out_shape=jax.ShapeDtypeStruct((seq, D_MODEL), MXU_DTYPE),
        compiler_params=_params(dimension_semantics=("parallel",)),
    )(x, g, shift, scale)


def _in_proj(h, win_g, tm, name, after=None):
    seq = h.shape[0]

    def body(h_ref, w_ref, z_ref, w_pair):
        @pl.when(pl.program_id(1) == 0)
        def _():
            w_pair[...] = jnp.concatenate([w_ref[0], w_ref[1]], axis=1)

        z_ref[...] = jnp.dot(h_ref[...], w_pair[...], preferred_element_type=F32)

    return _pallas_after(
        body, 2, after, name=name, grid=(N_DEV // 2, seq // tm),
        in_specs=[pl.BlockSpec((tm, D_MODEL), lambda j, i: (i, 0)),
                  pl.BlockSpec((2, D_MODEL, IN_COLS), lambda j, i: (j, 0, 0))],
        out_specs=pl.BlockSpec((tm, 2 * IN_COLS), lambda j, i: (i, j)),
        out_shape=jax.ShapeDtypeStruct((seq, IN_WIDTH), F32),
        scratch_shapes=[pltpu.VMEM((D_MODEL, 2 * IN_COLS), MXU_DTYPE)],
        compiler_params=_params(dimension_semantics=("parallel", "arbitrary")),
    )(h, win_g)


def _shift_down(v, j, pos):
    return jnp.where(pos >= j, pltpu.roll(v, j, 0), 0.0)


def _shift_up(v, j, pos, seq):
    return jnp.where(pos < seq - j, pltpu.roll(v, seq - j, 0), 0.0)


def _select_window(g, candidates):
    out = candidates[-1]
    for i in range(len(candidates) - 2, -1, -1):
        out = jnp.where(g == i, candidates[i], out)
    return out


def _pool_mean_minus_token(u, g, pos):
    sums, acc = [], u
    for j in (1, 2, 4, 8):
        acc = acc + _shift_down(acc, j, pos)
        sums.append(acc)
    wsum = _select_window(g, sums)
    width = jnp.left_shift(2, g).astype(F32)
    count = jnp.minimum(pos.astype(F32) + 1.0, width)
    return wsum / count - u, count


def _pool_fwd(z, pool_w_l, pool_scale_l, name, after=None):
    seq = z.shape[0]

    def body(pv_ref, pg_ref, w_ref, sc_ref, out_ref):
        g = pl.program_id(0)
        pos = lax.broadcasted_iota(jnp.int32, (seq, GROUP_DIM), 0)
        pm, _ = _pool_mean_minus_token(pv_ref[...], g, pos)
        lin = _dot(pm, w_ref[...]) * sc_ref[...]
        pg = pg_ref[...]
        out_ref[...] = (lin * (pg * _sigmoid(pg))).astype(out_ref.dtype)

    return _pallas_after(
        body, 4, after, name=name, grid=(POOL_GROUPS,),
        in_specs=[pl.BlockSpec((seq, GROUP_DIM), lambda g: (0, g)),
                  pl.BlockSpec((seq, GROUP_DIM), lambda g: (0, POOL_GROUPS + g)),
                  pl.BlockSpec((None, GROUP_DIM, GROUP_DIM), lambda g: (g, 0, 0)),
                  pl.BlockSpec((1, GROUP_DIM), lambda g: (0, g))],
        out_specs=pl.BlockSpec((seq, GROUP_DIM), lambda g: (0, g)),
        out_shape=jax.ShapeDtypeStruct((seq, POOL_WIDTH), MXU_DTYPE),
        compiler_params=_params(dimension_semantics=("parallel",)),
    )(z, z, pool_w_l, pool_scale_l)


def _chunk_masks():
    row = lax.broadcasted_iota(jnp.int32, (CHUNK, CHUNK), 0)
    col = lax.broadcasted_iota(jnp.int32, (CHUNK, CHUNK), 1)
    causal = row >= col
    before_sub = col < (row // SUB) * SUB
    suffix = row <= col
    return causal, before_sub, suffix


def _masked_sums(masks, v):
    lhs = jnp.concatenate([m.astype(jnp.bfloat16) for m in masks], axis=0)
    hi = v.astype(jnp.bfloat16)
    rest = v - hi.astype(F32)
    mid = rest.astype(jnp.bfloat16)
    lo = (rest - mid.astype(F32)).astype(jnp.bfloat16)
    out = jnp.dot(lhs, hi, preferred_element_type=F32)
    out += jnp.dot(lhs, mid, preferred_element_type=F32)
    out += jnp.dot(lhs, lo, preferred_element_type=F32)
    return [out[i * CHUNK:(i + 1) * CHUNK] for i in range(len(masks))]


def _gates(zf, lb):
    sg = _sigmoid(zf)
    f = lb + (1.0 - lb) * sg
    logf = jnp.log(jnp.maximum(f, LOG_FLOOR))
    return sg, f, logf


def _intra_blocks(q_h, k_h, cum_h, base_h, causal):
    rel = cum_h - base_h
    out = []
    for i in range(N_SUB):
        rows = slice(i * SUB, (i + 1) * SUB)
        e_q = jnp.exp(rel[rows])
        base_i = jnp.concatenate([base_h[rows]] * N_SUB, axis=0)
        e_k = jnp.exp(jnp.minimum(base_i - cum_h, EXP_CLAMP))
        q_t = (q_h[rows] * e_q).astype(MXU_DTYPE)
        k_t = (k_h * e_k).astype(MXU_DTYPE)
        a_i = jnp.where(causal[rows], _dot_nt(q_t, k_t), 0.0)
        out.append((q_t, k_t, e_q, e_k, a_i))
    return out


def _hgrn_fwd(z, lb_l, gn_l, name, after=None):
    seq = z.shape[0]
    n_chunks = seq // CHUNK
    per_step = min(FWD_STEP_CHUNKS, n_chunks)
    rows_per_step = per_step * CHUNK

    def body(hq_ref, hf_ref, hi_ref, hg_ref, lb_ref, gn_ref, o_ref, bin_ref, st_ref, cb_ref, state):
        @pl.when(pl.program_id(0) == 0)
        def _():
            state[...] = jnp.zeros_like(state)

        causal, before_sub, _ = _chunk_masks()
        for cc in range(per_step):
            rows = slice(cc * CHUNK, (cc + 1) * CHUNK)
            _, f, logf = _gates(hf_ref[rows, :], lb_ref[...])
            kk = 1.0 - f
            hq = hq_ref[rows, :]
            q = hq * _sigmoid(hq)
            cum, base = _masked_sums([causal, before_sub], logf)
            cb_ref[rows, 0:D_MODEL] = cum
            cb_ref[rows, D_MODEL:2 * D_MODEL] = base
            st_ref[cc] = state[...]
            for h in range(HEADS):
                sl = slice(h * HEAD_DIM, (h + 1) * HEAD_DIM)
                q_h, k_h, cum_h = q[:, sl], kk[:, sl], cum[:, sl]
                v_h = hi_ref[rows, sl]
                st_h = state[h]
                blocks = _intra_blocks(q_h, k_h, cum_h, base[:, sl], causal)
                a = jnp.concatenate([b[4] for b in blocks], axis=0)
                o_h = _dot_nt(q_h * jnp.exp(cum_h), st_h) + _dot(a, v_h)
                last = jnp.sum(logf[:, sl], axis=0, keepdims=True)
                state[h] = st_h * jnp.exp(last) + _dot_tn(v_h, k_h * jnp.exp(last - cum_h))
                rs = lax.rsqrt(jnp.mean(o_h * o_h, axis=-1, keepdims=True) + NORM_EPS)
                hg = hg_ref[rows, sl]
                o_ref[rows, sl] = o_h
                bin_ref[rows, sl] = ((o_h * rs * gn_ref[...]) * (hg * _sigmoid(hg))).astype(bin_ref.dtype)

    def col(block):
        return pl.BlockSpec((rows_per_step, D_MODEL), lambda c: (c, block))

    tile = pl.BlockSpec((rows_per_step, D_MODEL), lambda c: (c, 0))
    return _pallas_after(
        body, 6, after, name=name, grid=(n_chunks // per_step,),
        in_specs=[col(COL_HQ), col(COL_HF), col(COL_HI), col(COL_HG), _row_spec(), _row_spec(HEAD_DIM)],
        out_specs=[tile, tile, pl.BlockSpec((per_step, HEADS, HEAD_DIM, HEAD_DIM), lambda c: (c, 0, 0, 0)),
                   pl.BlockSpec((rows_per_step, 2 * D_MODEL), lambda c: (c, 0))],
        out_shape=[jax.ShapeDtypeStruct((seq, D_MODEL), F32),
                   jax.ShapeDtypeStruct((seq, D_MODEL), MXU_DTYPE),
                   jax.ShapeDtypeStruct((n_chunks, HEADS, HEAD_DIM, HEAD_DIM), F32),
                   jax.ShapeDtypeStruct((seq, 2 * D_MODEL), F32)],
        scratch_shapes=[pltpu.VMEM((HEADS, HEAD_DIM, HEAD_DIM), F32)],
        compiler_params=_params(dimension_semantics=("arbitrary",)),
    )(z, z, z, z, lb_l, gn_l)


def _rms_parts(y):
    rs = lax.rsqrt(jnp.mean(y * y, axis=-1, keepdims=True) + NORM_EPS)
    return rs, y * rs


def _merge_fwd(a_in, b_in, z, x, wpo_g, who_g, wout_g, gate, g_post, tm, name, target=None):
    seq = x.shape[0]
    with_loss = target is not None

    def body(*refs):
        a_ref, b_ref, mgp_ref, mgh_ref, x_ref, wpo_ref, who_ref, wout_ref, gate_ref, gp_ref = refs[:10]
        ba_ref, bb_ref, mer_ref, y_ref, last_ref = refs[10 + with_loss:15 + with_loss]
        a = a_ref[...]
        ba = _dot(a, jnp.concatenate([wpo_ref[j] for j in range(N_DEV)], axis=1))
        bb = _dot(b_ref[...], who_ref[...])
        merged = _sigmoid(mgp_ref[...]) * ba + _sigmoid(mgh_ref[...]) * bb
        y = _dot(merged, wout_ref[...])
        _, yn = _rms_parts(y)
        ba_ref[...] = ba.astype(ba_ref.dtype)
        bb_ref[...] = bb.astype(bb_ref.dtype)
        mer_ref[...] = merged.astype(mer_ref.dtype)
        y_ref[...] = y.astype(y_ref.dtype)
        x_next = x_ref[...] + gate_ref[...] * (yn * gp_ref[...])
        if not with_loss:
            last_ref[...] = x_next
            return
        loss_ref = refs[16]

        @pl.when(pl.program_id(0) == 0)
        def _():
            loss_ref[...] = jnp.zeros_like(loss_ref)

        err = x_next - refs[10][...]
        loss_ref[...] += 0.5 * jnp.sum(jnp.mean(err * err, axis=-1, keepdims=True), axis=0, keepdims=True)
        last_ref[...] = err * (1.0 / D_MODEL)

    def tile(cols=D_MODEL, block=0):
        return pl.BlockSpec((tm, cols), lambda i: (i, block))

    full = pl.BlockSpec((D_MODEL, D_MODEL), lambda i: (0, 0))
    act = jax.ShapeDtypeStruct((seq, D_MODEL), MXU_DTYPE)
    f32 = jax.ShapeDtypeStruct((seq, D_MODEL), F32)
    one = [pl.BlockSpec((1, 1), lambda i: (0, 0))] if with_loss else []
    return pl.pallas_call(
        body, name=name, grid=(seq // tm,),
        in_specs=[tile(POOL_WIDTH), tile(), tile(block=COL_MGP), tile(block=COL_MGH), tile(),
                  pl.BlockSpec((N_DEV, POOL_WIDTH, GROUP_DIM), lambda i: (0, 0, 0)),
                  full, full, _row_spec(), _row_spec()] + ([tile()] if with_loss else []),
        out_specs=[tile(), tile(), tile(), tile(), tile()] + one,
        out_shape=[act, act, act, act, f32] + ([jax.ShapeDtypeStruct((1, 1), F32)] if with_loss else []),
        compiler_params=_params(dimension_semantics=("arbitrary" if with_loss else "parallel",)),
    )(a_in, b_in, z, z, x, wpo_g, who_g, wout_g, gate, g_post, *([target] if with_loss else []))


def _stage_copy(stage, sems, dst, slot, step, where):
    rows, cols = where(step)
    return pltpu.make_async_copy(stage.at[slot], dst.at[rows, cols], sems.at[slot])


def _stage_begin(stage, sems, dst, step, where):
    slot = step % 2

    @pl.when(step >= 2)
    def _():
        _stage_copy(stage, sems, dst, slot, step - 2, where).wait()

    return slot


def _stage_end(stage, sems, dst, step, n_steps, where):
    slot = step % 2
    _stage_copy(stage, sems, dst, slot, step, where).start()

    @pl.when(step == n_steps - 1)
    def _():
        _stage_copy(stage, sems, dst, slot, step, where).wait()
        if n_steps > 1:
            _stage_copy(stage, sems, dst, 1 - slot, step - 1, where).wait()


def _merge_bwd(dx, y, ba, bb, z, wpo_g, who_g, wout_g, gate, g_post, dz, tm, name):
    seq = dx.shape[0]
    n_steps = seq // tm

    def body(dx_ref, y_ref, ba_ref, bb_ref, mgp_ref, mgh_ref, wpo_ref, who_ref, wout_ref, gate_ref, gp_ref, _,
             dy_ref, dba_ref, dbb_ref, da_ref, db_ref, dz_ref, acc_ref, stage, sems):
        step = pl.program_id(0)

        @pl.when(step == 0)
        def _():
            acc_ref[...] = jnp.zeros_like(acc_ref)

        def where(t):
            return pl.ds(t * tm, tm), pl.ds(COL_MGP * D_MODEL, 2 * D_MODEL)

        dmg_ref = stage.at[_stage_begin(stage, sems, dz_ref, step, where)]

        dxv = dx_ref[...]
        rs, yn = _rms_parts(y_ref[...].astype(F32))
        acc_ref[0:1, :] += jnp.sum(dxv * yn * gp_ref[...], axis=0, keepdims=True)
        acc_ref[1:2, :] += jnp.sum(dxv * gate_ref[...] * yn, axis=0, keepdims=True)
        dyn = dxv * (gate_ref[...] * gp_ref[...])
        dy = rs * (dyn - yn * jnp.mean(dyn * yn, axis=-1, keepdims=True))
        dmerged = _dot_nt(dy, wout_ref[...])
        sp, sh = _sigmoid(mgp_ref[...]), _sigmoid(mgh_ref[...])
        dba, dbb = sp * dmerged, sh * dmerged
        dmg_ref[:, 0:D_MODEL] = (dmerged * ba_ref[...].astype(F32) * sp * (1.0 - sp)).astype(dmg_ref.dtype)
        dmg_ref[:, D_MODEL:2 * D_MODEL] = (dmerged * bb_ref[...].astype(F32) * sh * (1.0 - sh)).astype(dmg_ref.dtype)
        da = _dot_nt(dba, jnp.concatenate([wpo_ref[j] for j in range(N_DEV)], axis=1))
        dy_ref[...] = dy.astype(dy_ref.dtype)
        dba_ref[...] = dba.astype(dba_ref.dtype)
        dbb_ref[...] = dbb.astype(dbb_ref.dtype)
        da_ref[...] = da.astype(da_ref.dtype)
        db_ref[...] = _dot_nt(dbb, who_ref[...]).astype(db_ref.dtype)
        _stage_end(stage, sems, dz_ref, step, n_steps, where)

    def tile(cols=D_MODEL, block=0):
        return pl.BlockSpec((tm, cols), lambda i: (i, block))

    full = pl.BlockSpec((D_MODEL, D_MODEL), lambda i: (0, 0))
    hbm = pl.BlockSpec(memory_space=pl.ANY)
    act = jax.ShapeDtypeStruct((seq, D_MODEL), MXU_DTYPE)
    return pl.pallas_call(
        body, name=name, grid=(n_steps,),
        in_specs=[tile(), tile(), tile(), tile(), tile(block=COL_MGP), tile(block=COL_MGH),
                  pl.BlockSpec((N_DEV, POOL_WIDTH, GROUP_DIM), lambda i: (0, 0, 0)),
                  full, full, _row_spec(), _row_spec(), hbm],
        out_specs=[tile(), tile(), tile(), tile(POOL_WIDTH), tile(), hbm,
                   pl.BlockSpec((8, D_MODEL), lambda i: (0, 0))],
        out_shape=[act, act, act, jax.ShapeDtypeStruct((seq, POOL_WIDTH), MXU_DTYPE), act,
                   jax.ShapeDtypeStruct(dz.shape, dz.dtype),
                   jax.ShapeDtypeStruct((8, D_MODEL), F32)],
        input_output_aliases={11: 5},
        scratch_shapes=[pltpu.VMEM((2, tm, 2 * D_MODEL), MXU_DTYPE), pltpu.SemaphoreType.DMA((2,))],
        compiler_params=_params(dimension_semantics=("arbitrary",)),
    )(dx, y, ba, bb, z, z, wpo_g, who_g, wout_g, gate, g_post, dz)


def _grad_out_weights(merged, dy, b_in, dbb, a_in, dba, name):
    seq = merged.shape[0]
    tn = D_MODEL // 2
    per_step = tn // GROUP_DIM

    def body(mer_ref, dy_ref, b_ref, dbb_ref, a_ref, dba_ref, gout_ref, gho_ref, gpo_ref):
        gout_ref[...] = _dot_tn(mer_ref[...], dy_ref[...]).astype(gout_ref.dtype)
        gho_ref[...] = _dot_tn(b_ref[...], dbb_ref[...]).astype(gho_ref.dtype)
        g_po = _dot_tn(a_ref[...], dba_ref[...])
        for j in range(per_step):
            gpo_ref[j] = g_po[:, j * GROUP_DIM:(j + 1) * GROUP_DIM].astype(gpo_ref.dtype)

    def whole(cols):
        return pl.BlockSpec((seq, cols), lambda j: (0, 0))

    cols = pl.BlockSpec((seq, tn), lambda j: (0, j))
    return pl.pallas_call(
        body, name=name, grid=(D_MODEL // tn,),
        in_specs=[whole(D_MODEL), cols, whole(D_MODEL), cols, whole(POOL_WIDTH), cols],
        out_specs=[pl.BlockSpec((D_MODEL, tn), lambda j: (0, j)), pl.BlockSpec((D_MODEL, tn), lambda j: (0, j)),
                   pl.BlockSpec((per_step, POOL_WIDTH, GROUP_DIM), lambda j: (j, 0, 0))],
        out_shape=[jax.ShapeDtypeStruct((D_MODEL, D_MODEL), WIRE_DTYPE),
                   jax.ShapeDtypeStruct((D_MODEL, D_MODEL), WIRE_DTYPE),
                   jax.ShapeDtypeStruct((N_DEV, POOL_WIDTH, GROUP_DIM), WIRE_DTYPE)],
        compiler_params=_params(dimension_semantics=("parallel",)),
    )(merged, dy, b_in, dbb, a_in, dba)


def _hgrn_bwd(db_in, z, o, states, cum_base, lb_l, gn_l, dz, name, after=None):
    seq = z.shape[0]
    per_step = min(BWD_STEP_CHUNKS, seq // CHUNK)
    rows_per_step = per_step * CHUNK
    n_steps = seq // rows_per_step
    last_step = n_steps - 1

    def body(db_ref, hq_ref, hf_ref, hi_ref, hg_ref, o_ref, st_ref, cb_ref, lb_ref, gn_ref, _,
             dz_hbm, dlb_ref, dgn_ref, dstate, dq_buf, dk_buf, dg_buf, stage, sems):
        step = pl.program_id(0)

        @pl.when(step == 0)
        def _():
            dstate[...] = jnp.zeros_like(dstate)
            dlb_ref[...] = jnp.zeros_like(dlb_ref)
            dgn_ref[...] = jnp.zeros_like(dgn_ref)

        def one_chunk(cc, *args):
            one_chunk_body((db_ref, hq_ref, hf_ref, hi_ref, hg_ref, o_ref, st_ref, cb_ref, dlb_ref, dgn_ref, dstate,
                            dq_buf, dk_buf, dg_buf), cc, *args)

        def where(t):
            return pl.ds((last_step - t) * rows_per_step, rows_per_step), pl.ds(COL_HQ * D_MODEL, 4 * D_MODEL)

        dz_step = stage.at[_stage_begin(stage, sems, dz_hbm, step, where)]
        causal, before_sub, suffix = _chunk_masks()
        lb = lb_ref[...]
        gn = gn_ref[...]
        for cc in reversed(range(per_step)):
            one_chunk(cc, dz_step, causal, before_sub, suffix, lb, gn)
        _stage_end(stage, sems, dz_hbm, step, n_steps, where)

    def one_chunk_body(refs, cc, dz_step, causal, before_sub, suffix, lb, gn):
        (db_ref, hq_ref, hf_ref, hi_ref, hg_ref, o_ref, st_ref, cb_ref, dlb_ref, dgn_ref, dstate,
         dq_buf, dk_buf, dg_buf) = refs
        rows = slice(cc * CHUNK, (cc + 1) * CHUNK)
        dz_ref = dz_step.at[rows, :]
        dq_buf, dk_buf, dg_buf = dq_buf.at[cc], dk_buf.at[cc], dg_buf.at[cc]
        sg, f, logf = _gates(hf_ref[rows, :], lb)
        kk = 1.0 - f
        hq = hq_ref[rows, :]
        sq = _sigmoid(hq)
        q = hq * sq
        cum, base = cb_ref[rows, 0:D_MODEL], cb_ref[rows, D_MODEL:2 * D_MODEL]
        dgn = jnp.zeros((1, HEAD_DIM), F32)
        dlast = []
        for h in range(HEADS):
            sl = slice(h * HEAD_DIM, (h + 1) * HEAD_DIM)
            q_h, k_h, cum_h = q[:, sl], kk[:, sl], cum[:, sl]
            v_h = hi_ref[rows, sl]
            st_h = st_ref[cc, h]
            dst_h = dstate[h]
            rs, ohat = _rms_parts(o_ref[rows, sl])
            hg = hg_ref[rows, sl]
            shg = _sigmoid(hg)
            d_bin = db_ref[rows, sl].astype(F32)
            don = d_bin * (hg * shg)
            dgn += jnp.sum(don * ohat, axis=0, keepdims=True)
            dohat = don * gn
            do = rs * (dohat - ohat * jnp.mean(dohat * ohat, axis=-1, keepdims=True))
            dz_ref[:, 3 * D_MODEL + h * HEAD_DIM:3 * D_MODEL + (h + 1) * HEAD_DIM] = (
                d_bin * (ohat * gn) * _dsilu(hg, shg)).astype(dz_ref.dtype)
            last = cb_ref[(cc + 1) * CHUNK - 1:(cc + 1) * CHUNK, sl]
            g_in = jnp.exp(cum_h)
            d_out = jnp.exp(last - cum_h)
            q_bar, k_bar = q_h * g_in, k_h * d_out
            blocks = _intra_blocks(q_h, k_h, cum_h, base[:, sl], causal)
            a = jnp.concatenate([b[4] for b in blocks], axis=0)
            da = jnp.where(causal, _dot_nt(do, v_h), 0.0)
            dv = _dot_tn(a, do) + _dot_nt(k_bar, dst_h)
            dq_bar, dk_bar = _dot(do, st_h), _dot(v_h, dst_h)
            dk = dk_bar * d_out
            dq_parts, dg_parts = [], []
            dg_k = k_bar * dk_bar
            dlast.append(jnp.sum(k_bar * dk_bar, axis=0, keepdims=True)
                         + jnp.exp(last) * jnp.sum(st_h * dst_h, axis=0, keepdims=True))
            for i, (q_t, k_t, e_q, e_k, _) in enumerate(blocks):
                da_i = da[i * SUB:(i + 1) * SUB].astype(MXU_DTYPE)
                dq_t = _dot(da_i, k_t)
                dk_t = _dot_tn(da_i, q_t)
                dq_parts.append(dq_t * e_q)
                dk += dk_t * e_k
                dg_parts.append(q_t.astype(F32) * dq_t)
                dg_k += k_t.astype(F32) * dk_t
            dq = dq_bar * g_in + jnp.concatenate(dq_parts, axis=0)
            dg_buf[:, sl] = q_bar * dq_bar + jnp.concatenate(dg_parts, axis=0) - dg_k
            dstate[h] = dst_h * jnp.exp(last) + _dot_tn(do, q_bar)
            dq_buf[:, sl] = dq
            dk_buf[:, sl] = dk
            dz_ref[:, 2 * D_MODEL + h * HEAD_DIM:2 * D_MODEL + (h + 1) * HEAD_DIM] = dv.astype(dz_ref.dtype)
        dgn_ref[...] += dgn
        dq_all, dk_all = dq_buf[...], dk_buf[...]
        dlogf = _masked_sums([suffix], dg_buf[...])[0] + jnp.concatenate(dlast, axis=1)
        df = jnp.where(f > LOG_FLOOR, dlogf / f, 0.0) - dk_all
        dlb_ref[...] += jnp.sum(df * (1.0 - sg), axis=0, keepdims=True)
        dz_ref[:, 0:D_MODEL] = (dq_all * _dsilu(hq, sq)).astype(dz_ref.dtype)
        dz_ref[:, D_MODEL:2 * D_MODEL] = (df * (1.0 - lb) * sg * (1.0 - sg)).astype(dz_ref.dtype)

    def col(block):
        return pl.BlockSpec((rows_per_step, D_MODEL), lambda c: (last_step - c, block))

    hbm = pl.BlockSpec(memory_space=pl.ANY)
    return _pallas_after(
        body, 11, after, name=name, grid=(n_steps,),
        in_specs=[col(0), col(COL_HQ), col(COL_HF), col(COL_HI), col(COL_HG), col(0),
                  pl.BlockSpec((per_step, HEADS, HEAD_DIM, HEAD_DIM), lambda c: (last_step - c, 0, 0, 0)),
                  pl.BlockSpec((rows_per_step, 2 * D_MODEL), lambda c: (last_step - c, 0)),
                  _row_spec(), _row_spec(HEAD_DIM), hbm],
        out_specs=[hbm, _row_spec(), _row_spec(HEAD_DIM)],
        out_shape=[jax.ShapeDtypeStruct(dz.shape, dz.dtype),
                   jax.ShapeDtypeStruct((1, D_MODEL), F32), jax.ShapeDtypeStruct((1, HEAD_DIM), F32)],
        input_output_aliases={10: 0},
        scratch_shapes=[pltpu.VMEM((HEADS, HEAD_DIM, HEAD_DIM), F32)]
        + [pltpu.VMEM((per_step, CHUNK, D_MODEL), F32)] * 3
        + [pltpu.VMEM((2, rows_per_step, 4 * D_MODEL), MXU_DTYPE), pltpu.SemaphoreType.DMA((2,))],
        compiler_params=_params(dimension_semantics=("arbitrary",)),
    )(db_in, z, z, z, z, o, states, cum_base, lb_l, gn_l, dz)


def _pool_bwd(da_in, z, pool_w_l, pool_scale_l, dz, name, after=None):
    seq = z.shape[0]

    def body(da_ref, pv_ref, pg_ref, w_ref, sc_ref, _, dz_hbm, dw_ref, dsc_ref, stage_pv, stage_pg, sems_pv, sems_pg):
        g = pl.program_id(0)

        def where_pv(t):
            return pl.ds(0, seq), pl.ds(pl.multiple_of(t * GROUP_DIM, GROUP_DIM), GROUP_DIM)

        def where_pg(t):
            return pl.ds(0, seq), pl.ds(pl.multiple_of(POOL_WIDTH + t * GROUP_DIM, GROUP_DIM), GROUP_DIM)

        dpv_ref = stage_pv.at[_stage_begin(stage_pv, sems_pv, dz_hbm, g, where_pv)]
        dpg_ref = stage_pg.at[_stage_begin(stage_pg, sems_pg, dz_hbm, g, where_pg)]
        pos = lax.broadcasted_iota(jnp.int32, (seq, GROUP_DIM), 0)
        pm, count = _pool_mean_minus_token(pv_ref[...], g, pos)
        lin0 = _dot(pm, w_ref[...])
        pg = pg_ref[...]
        spg = _sigmoid(pg)
        da = da_ref[...].astype(F32)
        dlin = da * (pg * spg)
        dpg_ref[...] = (da * (lin0 * sc_ref[...]) * _dsilu(pg, spg)).astype(dpg_ref.dtype)
        dsc_ref[...] = jnp.sum(dlin * lin0, axis=0, keepdims=True)
        dl0 = dlin * sc_ref[...]
        dw_ref[...] = _dot_tn(pm, dl0)
        dpm = _dot_nt(dl0, w_ref[...])
        sums, acc = [], dpm / count
        for j in (1, 2, 4, 8):
            acc = acc + _shift_up(acc, j, pos, seq)
            sums.append(acc)
        dpv_ref[...] = (_select_window(g, sums) - dpm).astype(dpv_ref.dtype)
        _stage_end(stage_pv, sems_pv, dz_hbm, g, POOL_GROUPS, where_pv)
        _stage_end(stage_pg, sems_pg, dz_hbm, g, POOL_GROUPS, where_pg)

    grp = pl.BlockSpec((seq, GROUP_DIM), lambda g: (0, g))
    hbm = pl.BlockSpec(memory_space=pl.ANY)
    stage = pltpu.VMEM((2, seq, GROUP_DIM), MXU_DTYPE)
    return _pallas_after(
        body, 6, after, name=name, grid=(POOL_GROUPS,),
        in_specs=[grp, grp, pl.BlockSpec((seq, GROUP_DIM), lambda g: (0, POOL_GROUPS + g)),
                  pl.BlockSpec((None, GROUP_DIM, GROUP_DIM), lambda g: (g, 0, 0)),
                  pl.BlockSpec((1, GROUP_DIM), lambda g: (0, g)), hbm],
        out_specs=[hbm, pl.BlockSpec((None, GROUP_DIM, GROUP_DIM), lambda g: (g, 0, 0)),
                   pl.BlockSpec((1, GROUP_DIM), lambda g: (0, g))],
        out_shape=[jax.ShapeDtypeStruct(dz.shape, dz.dtype),
                   jax.ShapeDtypeStruct((POOL_GROUPS, GROUP_DIM, GROUP_DIM), F32),
                   jax.ShapeDtypeStruct((1, POOL_WIDTH), F32)],
        input_output_aliases={5: 0},
        scratch_shapes=[stage, stage, pltpu.SemaphoreType.DMA((2,)), pltpu.SemaphoreType.DMA((2,))],
        compiler_params=_params(dimension_semantics=("arbitrary",)),
    )(da_in, z, z, pool_w_l, pool_scale_l, dz)


def _in_proj_dw(h, dz, name, after=None):
    seq = h.shape[0]

    def body(h_ref, dz_ref, out_ref):
        pair = lax.dot_general(h_ref[...], dz_ref[...], (((0,), (0,)), ((), ())), preferred_element_type=F32)
        out_ref[0] = pair[:, 0:IN_COLS].astype(out_ref.dtype)
        out_ref[1] = pair[:, IN_COLS:].astype(out_ref.dtype)

    return _pallas_after(
        body, 2, after, name=name, grid=(N_DEV // 2,),
        in_specs=[pl.BlockSpec((seq, D_MODEL), lambda j: (0, 0)),
                  pl.BlockSpec((seq, 2 * IN_COLS), lambda j: (0, j))],
        out_specs=pl.BlockSpec((2, D_MODEL, IN_COLS), lambda j: (j, 0, 0)),
        out_shape=jax.ShapeDtypeStruct((N_DEV, D_MODEL, IN_COLS), WIRE_DTYPE),
        compiler_params=_params(dimension_semantics=("parallel",)),
    )(h, dz)


def _in_proj_dh(dz, win_g, tm, name, after=None):
    seq = dz.shape[0]

    def body(dz_ref, w_ref, dh_ref):
        @pl.when(pl.program_id(1) == 0)
        def _():
            dh_ref[...] = jnp.zeros_like(dh_ref)

        w_pair = jnp.concatenate([w_ref[0], w_ref[1]], axis=1)
        dh_ref[...] += lax.dot_general(dz_ref[...], w_pair, (((1,), (1,)), ((), ())), preferred_element_type=F32)

    return _pallas_after(
        body, 2, after, name=name, grid=(seq // tm, N_DEV // 2),
        in_specs=[pl.BlockSpec((tm, 2 * IN_COLS), lambda i, j: (i, j)),
                  pl.BlockSpec((2, D_MODEL, IN_COLS), lambda i, j: (j, 0, 0))],
        out_specs=pl.BlockSpec((tm, D_MODEL), lambda i, j: (i, 0)),
        out_shape=jax.ShapeDtypeStruct((seq, D_MODEL), F32),
        compiler_params=_params(dimension_semantics=("parallel", "arbitrary")),
    )(dz, win_g)


def _prenorm_bwd(x, dh, dx_res, g, scale, tm, name, after=None):
    seq = x.shape[0]

    def body(x_ref, dh_ref, dxr_ref, g_ref, sc_ref, dx_ref, acc_ref):
        @pl.when(pl.program_id(0) == 0)
        def _():
            acc_ref[...] = jnp.zeros_like(acc_ref)

        rs, xn = _rms_parts(x_ref[...])
        dh = dh_ref[...]
        acc_ref[0:1, :] += jnp.sum(dh, axis=0, keepdims=True)
        acc_ref[1:2, :] += jnp.sum(dh * (xn * g_ref[...]), axis=0, keepdims=True)
        dhn = dh * (1.0 + sc_ref[...])
        acc_ref[2:3, :] += jnp.sum(dhn * xn, axis=0, keepdims=True)
        dxn = dhn * g_ref[...]
        dx_ref[...] = rs * (dxn - xn * jnp.mean(dxn * xn, axis=-1, keepdims=True)) + dxr_ref[...]

    tile = pl.BlockSpec((tm, D_MODEL), lambda i: (i, 0))
    return _pallas_after(
        body, 5, after, name=name, grid=(seq // tm,),
        in_specs=[tile, tile, tile, _row_spec(), _row_spec()],
        out_specs=[tile, pl.BlockSpec((8, D_MODEL), lambda i: (0, 0))],
        out_shape=[jax.ShapeDtypeStruct((seq, D_MODEL), F32), jax.ShapeDtypeStruct((8, D_MODEL), F32)],
        compiler_params=_params(dimension_semantics=("arbitrary",)),
    )(x, dh, dx_res, g, scale)


def _adamw_math(w, g, m, v):
    m = ADAM_B1 * m + (1.0 - ADAM_B1) * g
    v = ADAM_B2 * v + (1.0 - ADAM_B2) * (g * g)
    m_hat = m / (1.0 - ADAM_B1 ** ADAM_STEP)
    v_hat = v / (1.0 - ADAM_B2 ** ADAM_STEP)
    delta = -ADAM_LR * (m_hat / (jnp.sqrt(v_hat) + ADAM_EPS) + ADAM_WD * w)
    return delta, m, v


def _adamw_sharded(w, m, v, contrib, tr, name):
    depth, rows, cols = w.shape
    n_parts = contrib.shape[1]

    def body(w_ref, m_ref, v_ref, c_ref, g_ref, d_ref, mo_ref, vo_ref):
        g = c_ref[0].astype(F32)
        for p in range(1, n_parts):
            g += c_ref[p].astype(F32)
        delta, mn, vn = _adamw_math(w_ref[...], g, m_ref[...], v_ref[...])
        g_ref[...] = g
        d_ref[...] = delta
        mo_ref[...] = mn
        vo_ref[...] = vn

    tile = pl.BlockSpec((None, tr, cols), lambda l, i: (l, i, 0))
    shape = jax.ShapeDtypeStruct(w.shape, F32)
    return pl.pallas_call(
        body, name=name, grid=(depth, rows // tr),
        in_specs=[tile, tile, tile, pl.BlockSpec((None, n_parts, tr, cols), lambda l, i: (l, 0, i, 0))],
        out_specs=[tile] * 4, out_shape=[shape] * 4,
        compiler_params=_params(dimension_semantics=("parallel", "parallel")),
    )(w, m, v, contrib)


def _adamw_layer(w, m, v, contribs, l, tr, name, prev=None):
    _, rows, cols = w.shape
    n = len(contribs)

    def body(*refs):
        w_ref, m_ref, v_ref = refs[:3]
        c_refs = refs[3:3 + n]
        g_ref, d_ref, mo_ref, vo_ref = refs[-4:]
        g = c_refs[0][...].astype(F32)
        for c_ref in c_refs[1:]:
            g += c_ref[...].astype(F32)
        delta, mn, vn = _adamw_math(w_ref[...], g, m_ref[...], v_ref[...])
        g_ref[...] = g
        d_ref[...] = delta
        mo_ref[...] = mn
        vo_ref[...] = vn

    tile = pl.BlockSpec((None, tr, cols), lambda i: (l, i, 0))
    in_specs = [tile, tile, tile] + [pl.BlockSpec((None, tr, cols), lambda i, s=slot: (s, i, 0)) for _, slot in contribs]
    operands = [w, m, v] + [arr for arr, _ in contribs]
    aliases = {}
    if prev is not None:
        aliases = {len(operands) + k: k for k in range(4)}
        in_specs += [pl.BlockSpec(memory_space=pl.ANY)] * 4
        operands += list(prev)
    shape = jax.ShapeDtypeStruct(w.shape, F32)
    return pl.pallas_call(
        body, name=name, grid=(rows // tr,), in_specs=in_specs, out_specs=[tile] * 4, out_shape=[shape] * 4,
        input_output_aliases=aliases,
        compiler_params=_params(dimension_semantics=("parallel",)),
    )(*operands)


def _adamw_small(w_pack, m_pack, v_pack, g_late, g_early, shapes):
    pieces, r = {}, 0
    for name, _, n in _SMALL_ROWS:
        pieces.setdefault(name, []).append((r, n))
        r += n
    names = list(pieces)

    def body(w_ref, m_ref, v_ref, gl_ref, ge_ref, *rest):
        outs, packs = rest[:4 * len(names)], rest[4 * len(names):]
        g_l, g_e = gl_ref[0][0:SMALL_LATE_ROWS], ge_ref[0]
        for d in range(1, N_DEV):
            g_l += gl_ref[d][0:SMALL_LATE_ROWS]
            g_e += ge_ref[d]
        g = jnp.concatenate([g_l, g_e], axis=0)
        w = w_ref[...]
        r0, r1, r2 = LB_ROW0, LB_ROW0 + 8, LB_ROW0 + 16
        lg0, lg1 = w[r0:r1], w[r1:r2]
        mx = jnp.maximum(lg0, lg1)
        e0, e1 = jnp.exp(lg0 - mx), jnp.exp(lg1 - mx)
        p0, p1 = e0 / (e0 + e1), e1 / (e0 + e1)
        low = ((p0 - p0), (p0 + p1) - p0)
        dlow = [g_rows * jnp.where((lo > 0.0) & (lo < 1.0), 1.0, jnp.where((lo == 0.0) | (lo == 1.0), 0.5, 0.0))
                for g_rows, lo in ((g[r0:r1], low[0]), (g[r1:r2], low[1]))]
        dp0 = (dlow[0] + dlow[1]) - (dlow[0] + dlow[1])
        dp1 = dlow[1]
        inner = p0 * dp0 + p1 * dp1
        g = jnp.concatenate([g[:r0], p0 * (dp0 - inner), p1 * (dp1 - inner), g[r2:]], axis=0)
        delta, mn, vn = _adamw_math(w, g, m_ref[...], v_ref[...])
        for kind, val in enumerate((g, delta, mn, vn)):
            packs[kind][...] = val
            for j, name in enumerate(names):
                out, at = outs[kind * len(names) + j], 0
                for start, n in pieces[name]:
                    if name in flat:
                        for r in range(n):
                            layer, c = divmod(at + r, flat[name])
                            out[layer:layer + 1, c * 128:(c + 1) * 128] = packs[kind][start + r:start + r + 1, :]
                    else:
                        out[at:at + n, :] = packs[kind][start:start + n, :]
                    at += n

    rows = {name: sum(n for _, n in pieces[name]) for name in names}
    flat = {name: rows[name] // DEPTH for name in names if len(shapes[name]) == 2}
    outs = pl.pallas_call(
        body, name="adamw_small",
        out_shape=[jax.ShapeDtypeStruct(shapes[name] if name in flat else (rows[name], 128), F32)
                   for _ in range(4) for name in names],
        scratch_shapes=[pltpu.VMEM(w_pack.shape, F32)] * 4, compiler_params=_params(),
    )(w_pack, m_pack, v_pack, g_late, g_early)
    return [{name: outs[kind * len(names) + j].reshape(shapes[name]) for j, name in enumerate(names)}
            for kind in range(4)]


def _pack_small(parts, first=0, last=len(_SMALL_ROWS)):
    rows = [(parts[name] if l is None else parts[name][l]).reshape(n, 128) for name, l, n in _SMALL_ROWS[first:last]]
    if last == len(_SMALL_ROWS):
        rows.append(jnp.zeros((SMALL_ROWS_PAD - sum(n for _, _, n in _SMALL_ROWS), 128), F32))
    return jnp.concatenate(rows, axis=0)


def kernel(x, c, w_ada, b_ada, g_pre, g_post, w_in, pool_w, pool_scale, lb_logits, hgrn_norm_g, w_pool_o, w_hgrn_o, w_out, loss_target, m_w_ada, m_b_ada, m_g_pre, m_g_post, m_w_in, m_pool_w, m_pool_scale, m_lb_logits, m_hgrn_norm_g, m_w_pool_o, m_w_hgrn_o, m_w_out, v_w_ada, v_b_ada, v_g_pre, v_g_post, v_w_in, v_pool_w, v_pool_scale, v_lb_logits, v_hgrn_norm_g, v_w_pool_o, v_w_hgrn_o, v_w_out):
    seq = x.shape[1]
    tm = min(1024, seq)
    tm_merge = min(512, seq)
    pos = _my_position()
    me = pos[3]

    big = dict(win=w_in, wpo=w_pool_o, who=w_hgrn_o, wout=w_out)
    units = [["win0"], ["wpo0", "who0", "wout0"], ["win1", "wpo1", "who1", "wout1"]]
    g_streams = [_gather_streams(keys) for keys in units]
    g_state = [None] * len(units)

    def gather_start(us, after, first=None):
        bufs = dict(first[0]) if first else {}
        for k in [k for u in us for k in units[u]]:
            arr = big[k[:-1]]
            bufs["g_" + k] = _with_own_slot(arr[int(k[-1])].astype(WIRE_DTYPE), me)
        streams = ([first[1]] if first else []) + [s for u in us for s in g_streams[u][:2]]
        bufs, sems, token = _comm_call("gather_start_" + "_".join(map(str, us)), bufs, start=streams, after=after)
        if first:
            first_out, sems = ({k: bufs[k] for k in first[0]}, sems[0]), sems[1:]
        for n, u in enumerate(us):
            g_state[u] = dict(bufs={"g_" + k: bufs["g_" + k] for k in units[u]},
                              sems=sems[2 * n:2 * n + 2])
        return (token, first_out) if first else token

    def gather_pass(u, after):
        st = g_state[u]
        to_chips, _, pass_on = g_streams[u]
        st["bufs"], (st["pass_sems"],), _ = _comm_call(f"gather_pass_{u}", st["bufs"], start=[pass_on],
                                                       wait=[(to_chips, st["sems"][0])], after=after)

    def gather_done(u, after=None):
        st = g_state[u]
        _, to_sibling, pass_on = g_streams[u]
        bufs, _, _ = _comm_call(f"gather_done_{u}", st["bufs"], after=after,
                                wait=[(to_sibling, st["sems"][1]), (pass_on, st["pass_sems"])])
        return {k: bufs["g_" + k] for k in units[u]}

    c_stream = _direct_gather_stream("c")
    token, (c_bufs, c_sems) = gather_start([0], None, first=(dict(s_c=c, g_c=_with_own_slot(c, me)), c_stream))
    c_bufs, _, _ = _comm_call("gather_c_done", c_bufs, wait=[(c_stream, c_sems)])
    c_all = c_bufs["g_c"].reshape(N_DEV, D_MODEL)
    b_cols = lax.dynamic_slice_in_dim(b_ada, me * ADA_COLS, ADA_COLS, axis=1)
    ada_part = _ada_fwd(c_all, w_ada, b_cols)
    gather_pass(0, ada_part)
    ada_all = _allgather_small(ada_part.reshape(DEPTH * N_DEV, ADA_COLS), "allgather_ada",
                               after=g_state[0]["bufs"]["g_win0"])
    ada = lax.dynamic_index_in_dim(ada_all.reshape(N_DEV, DEPTH, N_DEV, ADA_COLS), me, axis=2, keepdims=False)
    ada = jnp.transpose(ada, (1, 0, 2)).reshape(DEPTH, 3 * D_MODEL)
    shift = [ada[l:l + 1, 0:D_MODEL] for l in range(DEPTH)]
    scale = [ada[l:l + 1, D_MODEL:2 * D_MODEL] for l in range(DEPTH)]
    gate = [ada[l:l + 1, 2 * D_MODEL:] for l in range(DEPTH)]

    lb = _lb_fwd(lb_logits)

    gw = {}
    xs, saved = [x[0]], []
    for l in range(DEPTH):
        h = _prenorm_fwd(xs[l], g_pre[l:l + 1], shift[l], scale[l], tm, f"prenorm_fwd_{l}",
                         after=token if l == 0 else None)
        token = None
        if l == 0:
            gw.update(gather_done(0, h))
            token = gather_start([1, 2], gw["win0"])
        else:
            gw.update(gather_done(2, h))
        z = _in_proj(h, gw[f"win{l}"], min(1024, seq), f"in_proj_{l}", after=token)
        a_in = _pool_fwd(z, pool_w[l], pool_scale[l:l + 1], f"pool_fwd_{l}")
        o, b_in, states, cum_base = _hgrn_fwd(z, lb[l:l + 1], hgrn_norm_g[l:l + 1], f"hgrn_fwd_{l}")
        if l == 0:
            gather_pass(1, b_in)
            gw.update(gather_done(1))
        who_l = gw[f"who{l}"].reshape(D_MODEL, D_MODEL)
        wout_l = gw[f"wout{l}"].reshape(D_MODEL, D_MODEL)
        last = l == DEPTH - 1
        ba, bb, merged, y, *out = _merge_fwd(a_in, b_in, z, xs[l], gw[f"wpo{l}"], who_l, wout_l, gate[l],
                                             g_post[l:l + 1], tm_merge, f"merge_fwd_{l}",
                                             target=loss_target[0] if last else None)
        if last:
            dx, loss_part = out
        else:
            xs.append(out[0])
            gather_pass(2, out[0])
        saved.append((h, z, a_in, o, b_in, states, cum_base, ba, bb, merged, y, who_l, wout_l))


    chips = _other_chips(pos)
    pair_idx = jnp.stack([_dev_index(cx, cy, pos[2]) for cx, cy in chips] + [me]).astype(jnp.int32)
    pair_rows = dict(win=512, wpo=POOL_WIDTH, who=HEAD_DIM, wout=HEAD_DIM)

    def scatter_pair_start(u, grads):
        keys = list(grads)
        pair, to_chips = _scatter_streams(keys)
        bufs = {}
        for k in keys:
            bufs["g_" + k] = grads[k]
            bufs["st_" + k] = lax.empty((4,) + grads[k].shape[1:], WIRE_DTYPE)
        bufs, (sems,), token = _comm_call(f"scatter_pair_start_{u}", bufs, start=[pair])
        return dict(u=u, keys=keys, pair=pair, to_chips=to_chips, bufs=bufs, sems=sems, token=token)

    def scatter_pair_finish(st, after):
        u, keys = st["u"], st["keys"]
        bufs, _, _ = _comm_call(f"scatter_pair_done_{u}", st["bufs"], wait=[(st["pair"], st["sems"])], after=after)
        bufs2 = {}
        for k in keys:
            bufs2["ps_" + k] = _pair_sum(bufs["g_" + k], bufs["st_" + k], pair_idx, bufs["g_" + k].shape[1],
                                         f"pair_sum_{k}")
            bufs2["ld_" + k] = lax.empty((3,) + bufs["g_" + k].shape[1:], WIRE_DTYPE)
        st.update(bufs=bufs2)

    def scatter_chips_start(st, after=None):
        bufs2, (sems,), token = _comm_call(f"scatter_chips_start_{st['u']}", st["bufs"], start=[st["to_chips"]],
                                           after=after)
        st.update(bufs=bufs2, sems=sems, token=token)

    def scatter_finish(st, after):
        bufs, _, _ = _comm_call(f"scatter_chips_done_{st['u']}", st["bufs"], wait=[(st["to_chips"], st["sems"])],
                                after=after)
        return {k: [(bufs["ps_" + k], 3), (bufs["ld_" + k], 0), (bufs["ld_" + k], 1), (bufs["ld_" + k], 2)]
                for k in st["keys"]}

    moments = dict(win=(m_w_in, v_w_in), wpo=(m_w_pool_o, v_w_pool_o), who=(m_w_hgrn_o, v_w_hgrn_o),
                   wout=(m_w_out, v_w_out))
    big_out = {}

    def finish_unit(unit, after):
        for k, contribs in scatter_finish(scat[unit], after).items():
            wname, l = k[:-1], int(k[-1])
            big_out[wname] = _adamw_layer(big[wname], moments[wname][0], moments[wname][1], contribs, l,
                                          pair_rows[wname], f"adamw_{k}", prev=big_out.get(wname))
            after = big_out[wname][0]
        return after

    d_ada, small, scat = [None] * DEPTH, [None] * DEPTH, {}
    for l in reversed(range(DEPTH)):
        h, z, a_in, o, b_in, states, cum_base, ba, bb, merged, y, who_l, wout_l = saved[l]
        dy, dba, dbb, da_in, db_in, dz, acc_post = _merge_bwd(
            dx, y, ba, bb, z, gw[f"wpo{l}"], who_l, wout_l, gate[l], g_post[l:l + 1],
            lax.empty((seq, IN_WIDTH), MXU_DTYPE), tm_merge, f"merge_bwd_{l}")
        g_out, g_ho, g_po = _grad_out_weights(merged, dy, b_in, dbb, a_in, dba, f"grad_out_weights_{l}")
        g_small = {f"wout{l}": g_out.reshape(N_DEV, HEAD_DIM, D_MODEL),
                   f"who{l}": g_ho.reshape(N_DEV, HEAD_DIM, D_MODEL), f"wpo{l}": g_po}
        st_small = scat["small0"] = scatter_pair_start("small0", g_small) if l == 0 else None
        dz, dlb, dgn = _hgrn_bwd(db_in, z, o, states, cum_base, lb[l:l + 1], hgrn_norm_g[l:l + 1], dz, f"hgrn_bwd_{l}",
                                 after=st_small and st_small["token"])
        if l == 0:
            scatter_pair_finish(st_small, dlb)
            scatter_chips_start(st_small)
        dz, dpw, dps = _pool_bwd(da_in, z, pool_w[l], pool_scale[l:l + 1], dz, f"pool_bwd_{l}",
                                 after=st_small and st_small["token"])
        small[l] = dict(g_post=acc_post[1], pool_w=dpw, pool_scale=dps[0], lb_logits=dlb[0], hgrn_norm_g=dgn[0])
        token = None
        if l == 0:
            parts = {name: jnp.stack([small[0][name], small[1][name]]) for name in small[0]}
            parts.update(b_ada=[None, d_ada[1]], g_pre=[None, small[1]["g_pre"]])
            sg_stream = _direct_gather_stream("sg")
            early = _pack_small(parts, 2)
            sg_bufs, (sg_sems,), token = _comm_call(
                "small_grads_start", dict(s_sg=early, g_sg=_with_own_slot(early, me)), start=[sg_stream])
        g_win = {f"win{l}": _in_proj_dw(h, dz, f"grad_w_in_{l}", after=token)}
        st_win = scat[f"win{l}"] = scatter_pair_start(f"win{l}", g_win if l == 0 else {**g_small, **g_win})
        if l > 0:
            dh = _in_proj_dh(dz, gw[f"win{l}"], seq, f"in_proj_dh_{l}", after=st_win["token"])
            scatter_pair_finish(st_win, dh)
            scatter_chips_start(st_win)
        else:
            after = finish_unit("win1", st_win["token"])
            scatter_pair_finish(st_win, after)
            scatter_chips_start(st_win)
            after = finish_unit("small0", st_win["token"])
            dh = _in_proj_dh(dz, gw[f"win{l}"], seq, f"in_proj_dh_{l}", after=after)
        dx, acc_pre = _prenorm_bwd(xs[l], dh, dx, g_pre[l:l + 1], scale[l], tm, f"prenorm_bwd_{l}",
                                   after=st_win["token"])
        d_ada[l] = jnp.concatenate([acc_pre[0], acc_pre[1], acc_post[0]])
        small[l]["g_pre"] = acc_pre[2]
    grad_x = dx[None]

    parts = dict(b_ada=[d_ada[0]], g_pre=[small[0]["g_pre"]])
    late = jnp.concatenate([_pack_small(parts, 0, 2), jnp.broadcast_to(loss_part, (8, 128))], axis=0)
    g_late = _allgather_small(late, "allgather_late_grads")
    loss = jnp.sum(g_late[:, SMALL_LATE_ROWS, 0])
    sg_bufs, _, _ = _comm_call("small_grads_done", sg_bufs, wait=[(sg_stream, sg_sems)], after=g_late)
    g_early = sg_bufs["g_sg"]
    small_names = list(dict.fromkeys(name for name, _, _ in _SMALL_ROWS))
    weights = dict(b_ada=b_ada, g_pre=g_pre, g_post=g_post, pool_w=pool_w, pool_scale=pool_scale,
                   lb_logits=lb_logits, hgrn_norm_g=hgrn_norm_g)
    m_small = dict(b_ada=m_b_ada, g_pre=m_g_pre, g_post=m_g_post, pool_w=m_pool_w, pool_scale=m_pool_scale,
                   lb_logits=m_lb_logits, hgrn_norm_g=m_hgrn_norm_g)
    v_small = dict(b_ada=v_b_ada, g_pre=v_g_pre, g_post=v_g_post, pool_w=v_pool_w, pool_scale=v_pool_scale,
                   lb_logits=v_lb_logits, hgrn_norm_g=v_hgrn_norm_g)
    shapes = {name: weights[name].shape for name in small_names}
    small_out = _adamw_small(_pack_small(weights), _pack_small(m_small), _pack_small(v_small), g_late, g_early,
                             shapes)

    d_ada_all = jnp.stack([g_late[:, 0:24, :].reshape(N_DEV, 3 * D_MODEL),
                           g_early[:, 0:24, :].reshape(N_DEV, 3 * D_MODEL)], axis=1)
    d_cols = jnp.transpose(lax.dynamic_slice_in_dim(d_ada_all, me * ADA_COLS, ADA_COLS, axis=2), (1, 0, 2))
    g_w_ada = _ada_bwd(c_all, d_cols)
    ada_out = _adamw_sharded(w_ada, m_w_ada, v_w_ada, g_w_ada[:, None], 256, "adamw_w_ada")
    finish_unit("win0", ada_out[1][0, 0:8, 0:128] + small_out[1]["pool_scale"][0:1, 0:128])

    def leaf(kind):
        s = small_out[kind]
        return (ada_out[kind], s["b_ada"], s["g_pre"], s["g_post"], big_out["win"][kind], s["pool_w"], s["pool_scale"],
                s["lb_logits"], s["hgrn_norm_g"], big_out["wpo"][kind], big_out["who"][kind], big_out["wout"][kind])

    return (loss, grad_x) + leaf(0) + leaf(1) + leaf(2) + leaf(3)
```

```python
import jax
import jax.numpy as jnp
from jax import lax
from jax.experimental import pallas as pl
from jax.experimental.pallas import tpu as pltpu

F32 = jnp.float32
MXU_DTYPE = jnp.bfloat16
WIRE_DTYPE = jnp.bfloat16

N_DEV = 8
DEPTH = 2
D_MODEL = 1024
HEADS = 8
HEAD_DIM = 128
POOL_GROUPS = 4
GROUP_DIM = 128
POOL_WIDTH = POOL_GROUPS * GROUP_DIM
IN_WIDTH = 7168
CHUNK = 64
SUB = 16
N_SUB = CHUNK // SUB
FWD_STEP_CHUNKS = 8
BWD_STEP_CHUNKS = 4
EXP_CLAMP = 80.0
NORM_EPS = 1e-6
LOG_FLOOR = 1e-30
ADA_COLS = 3 * D_MODEL // N_DEV
IN_COLS = IN_WIDTH // N_DEV
COL_HQ, COL_HF, COL_HI, COL_HG, COL_MGP, COL_MGH = 1, 2, 3, 4, 5, 6

ADAM_LR = 0.001
ADAM_B1 = 0.9
ADAM_B2 = 0.999
ADAM_EPS = 1e-08
ADAM_WD = 0.01
ADAM_STEP = 10

VMEM_LIMIT = 48 * 1024 * 1024
MESH_ID = pl.DeviceIdType.MESH
HIGHEST = lax.Precision.HIGHEST

_SMALL_ROWS = (("b_ada", 0, 24), ("g_pre", 0, 8), ("b_ada", 1, 24), ("g_pre", 1, 8), ("g_post", None, 16),
               ("pool_w", None, 1024), ("pool_scale", None, 8), ("lb_logits", None, 16), ("hgrn_norm_g", None, 2))
SMALL_LATE_ROWS = 32
SMALL_ROWS_PAD = 1136
LB_ROW0 = 32 + 32 + 16 + 1024 + 8


def _params(**kw):
    return pltpu.CompilerParams(vmem_limit_bytes=VMEM_LIMIT, **kw)


def _sigmoid(v):
    return 1.0 / (1.0 + jnp.exp(-v))


def _dsilu(v, s):
    return s * (1.0 + v * (1.0 - s))


def _dot(a, b):
    return jnp.dot(a.astype(MXU_DTYPE), b.astype(MXU_DTYPE), preferred_element_type=F32)


def _dot_nt(a, b):
    return lax.dot_general(a.astype(MXU_DTYPE), b.astype(MXU_DTYPE), (((1,), (1,)), ((), ())),
                           preferred_element_type=F32)


def _dot_tn(a, b):
    return lax.dot_general(a.astype(MXU_DTYPE), b.astype(MXU_DTYPE), (((0,), (0,)), ((), ())),
                           preferred_element_type=F32)


def _pallas_after(body, n_in, after, *, in_specs, **kw):
    if after is None:
        return pl.pallas_call(body, in_specs=in_specs, **kw)

    def tied(*refs):
        body(*refs[:n_in], *refs[n_in + 1:])

    call = pl.pallas_call(tied, in_specs=list(in_specs) + [pl.BlockSpec(memory_space=pl.ANY)], **kw)
    return lambda *operands: call(*operands, after)


def _my_position():
    mx, my, mc = lax.axis_index("x"), lax.axis_index("y"), lax.axis_index("c")
    return mx, my, mc, 4 * mx + 2 * my + mc


def _peer(mx, my, mc, k):
    px = 1 - mx if (k >> 2) & 1 else mx
    py = 1 - my if (k >> 1) & 1 else my
    pc = 1 - mc if k & 1 else mc
    return (px, py, pc), 4 * px + 2 * py + pc


def _allgather_small(v, name, after=None):
    rows, cols = v.shape

    def body(v_ref, out_ref, send_sems, recv_sems):
        mx, my, mc, me = _my_position()
        out_ref[me] = v_ref[...]
        copies = []
        for k in range(1, N_DEV):
            peer, _ = _peer(mx, my, mc, k)
            cp = pltpu.make_async_remote_copy(
                src_ref=v_ref, dst_ref=out_ref.at[me],
                send_sem=send_sems.at[k - 1], recv_sem=recv_sems.at[k - 1],
                device_id=peer, device_id_type=MESH_ID)
            cp.start()
            copies.append(cp)
        for cp in copies:
            cp.wait()

    return _pallas_after(
        body, 1, after, name=name,
        out_shape=jax.ShapeDtypeStruct((N_DEV, rows, cols), v.dtype),
        in_specs=[pl.BlockSpec(memory_space=pltpu.VMEM)],
        out_specs=pl.BlockSpec(memory_space=pltpu.VMEM),
        scratch_shapes=[pltpu.SemaphoreType.DMA((N_DEV - 1,)), pltpu.SemaphoreType.DMA((N_DEV - 1,))],
        compiler_params=_params(),
    )(v)


class _Stream:
    def __init__(self, n, plan):
        self.n, self.plan = n, plan


def _comm_call(name, bufs, start=(), wait=(), after=None):
    names = list(bufs)

    def body(*refs):
        it = iter(refs)
        buf_refs = {n: next(it) for n in names}
        wait_sems = [(next(it), next(it)) for _ in wait]
        if after is not None:
            next(it)
        start_sems = [(next(it), next(it)) for _ in start]
        for _ in names:
            next(it)
        token = next(it)
        pos = _my_position()

        def descriptors(stream, sems):
            return [pltpu.make_async_remote_copy(src_ref=src, dst_ref=dst, send_sem=sems[0].at[k], recv_sem=sems[1].at[k],
                                                 device_id=dev, device_id_type=MESH_ID)
                    for k, (src, dst, dev) in enumerate(stream.plan(buf_refs, pos))]

        for (stream, _), sems in zip(wait, wait_sems):
            for cp in descriptors(stream, sems):
                cp.wait_send()
                cp.wait_recv()
        for stream, sems in zip(start, start_sems):
            for cp in descriptors(stream, sems):
                cp.start()
        token[...] = jnp.zeros_like(token)

    hbm = pl.BlockSpec(memory_space=pltpu.HBM)
    sem = pl.BlockSpec(memory_space=pltpu.SEMAPHORE)
    operands = [pltpu.with_memory_space_constraint(bufs[n], pltpu.HBM) for n in names]
    in_specs = [hbm] * len(names)
    for _, (send_sems, recv_sems) in wait:
        operands += [send_sems, recv_sems]
        in_specs += [sem, sem]
    if after is not None:
        operands.append(after)
        in_specs.append(pl.BlockSpec(memory_space=pl.ANY))
    out_shape, out_specs = [], []
    for stream in start:
        out_shape += [pltpu.SemaphoreType.DMA((stream.n,)), pltpu.SemaphoreType.DMA((stream.n,))]
        out_specs += [sem, sem]
    n_sem_out = len(out_shape)
    out_shape += [pltpu.HBM(bufs[n].shape, bufs[n].dtype) for n in names]
    out_specs += [hbm] * len(names)
    out_shape.append(jax.ShapeDtypeStruct((8, 128), F32))
    out_specs.append(pl.BlockSpec(memory_space=pltpu.VMEM))
    outs = pl.pallas_call(
        body, name=name, out_shape=out_shape, in_specs=in_specs, out_specs=out_specs,
        input_output_aliases={i: n_sem_out + i for i in range(len(names))},
        compiler_params=pltpu.CompilerParams(has_side_effects=pltpu.SideEffectType.DATAFLOW_SIDE_EFFECTING),
    )(*operands)
    sems = [(outs[2 * i], outs[2 * i + 1]) for i in range(len(start))]
    return dict(zip(names, outs[n_sem_out:n_sem_out + len(names)])), sems, outs[-1]


def _with_own_slot(block, me):
    return lax.dynamic_update_index_in_dim(lax.empty((N_DEV,) + block.shape, block.dtype), block, me, 0)


def _other_chips(pos):
    mx, my, _, _ = pos
    return [(1 - mx if i & 2 else mx, 1 - my if i & 1 else my) for i in (1, 2, 3)]


def _dev_index(px, py, pc):
    return 4 * px + 2 * py + pc


def _gather_streams(keys):
    def to_chips(refs, pos):
        _, _, mc, me = pos
        return [(refs["g_" + k].at[me], refs["g_" + k].at[me], (cx, cy, mc))
                for k in keys for cx, cy in _other_chips(pos)]

    def to_sibling(refs, pos):
        mx, my, mc, me = pos
        return [(refs["g_" + k].at[me], refs["g_" + k].at[me], (mx, my, 1 - mc)) for k in keys]

    def pass_on(refs, pos):
        mx, my, mc, _ = pos
        out = []
        for k in keys:
            for cx, cy in _other_chips(pos):
                slot = refs["g_" + k].at[_dev_index(cx, cy, mc)]
                out.append((slot, slot, (mx, my, 1 - mc)))
        return out

    return _Stream(3 * len(keys), to_chips), _Stream(len(keys), to_sibling), _Stream(3 * len(keys), pass_on)


def _direct_gather_stream(key):
    def plan(refs, pos):
        mx, my, mc, me = pos
        return [(refs["s_" + key], refs["g_" + key].at[me], _peer(mx, my, mc, k)[0]) for k in range(1, N_DEV)]

    return _Stream(N_DEV - 1, plan)


def _scatter_streams(keys):
    def pair(refs, pos):
        mx, my, mc, _ = pos
        sib = (mx, my, 1 - mc)
        out = []
        for k in keys:
            for i, (cx, cy) in enumerate(_other_chips(pos)):
                out.append((refs["g_" + k].at[_dev_index(cx, cy, 1 - mc)], refs["st_" + k].at[i], sib))
            out.append((refs["g_" + k].at[_dev_index(mx, my, 1 - mc)], refs["st_" + k].at[3], sib))
        return out

    def chips(refs, pos):
        mc = pos[2]
        return [(refs["ps_" + k].at[i], refs["ld_" + k].at[i], (cx, cy, mc))
                for k in keys for i, (cx, cy) in enumerate(_other_chips(pos))]

    return _Stream(4 * len(keys), pair), _Stream(3 * len(keys), chips)


def _pair_sum(g, st, idx, tr, name):
    _, rows, cols = g.shape

    def body(idx_ref, g_ref, st_ref, out_ref):
        out_ref[...] = (g_ref[...].astype(F32) + st_ref[...].astype(F32)).astype(out_ref.dtype)

    return pl.pallas_call(
        body, name=name,
        grid_spec=pltpu.PrefetchScalarGridSpec(
            num_scalar_prefetch=1, grid=(4, rows // tr),
            in_specs=[pl.BlockSpec((None, tr, cols), lambda j, i, idx_ref: (idx_ref[j], i, 0)),
                      pl.BlockSpec((None, tr, cols), lambda j, i, idx_ref: (j, i, 0))],
            out_specs=pl.BlockSpec((None, tr, cols), lambda j, i, idx_ref: (j, i, 0))),
        out_shape=jax.ShapeDtypeStruct((4, rows, cols), WIRE_DTYPE),
        compiler_params=_params(dimension_semantics=("parallel", "parallel")),
    )(idx, g, st)


def _ada_fwd(c_all, w_ada, b_cols):
    def body(c_ref, w_ref, b_ref, out_ref):
        cv = c_ref[...]
        ca = cv * _sigmoid(cv)
        for l in range(DEPTH):
            out_ref[l] = jnp.dot(ca, w_ref[l], precision=HIGHEST, preferred_element_type=F32) + b_ref[l:l + 1, :]

    return pl.pallas_call(
        body, name="ada_fwd",
        out_shape=jax.ShapeDtypeStruct((DEPTH, N_DEV, ADA_COLS), F32),
        compiler_params=_params(),
    )(c_all, w_ada, b_cols)


def _ada_bwd(c_all, d_cols, name):
    def body(c_ref, d_ref, out_ref):
        cv = c_ref[...]
        ca = cv * _sigmoid(cv)
        out_ref[0] = lax.dot_general(ca, d_ref[...], (((0,), (0,)), ((), ())), precision=HIGHEST,
                                     preferred_element_type=F32)

    return pl.pallas_call(
        body, name=name,
        out_shape=jax.ShapeDtypeStruct((1, D_MODEL, ADA_COLS), F32),
        compiler_params=_params(),
    )(c_all, d_cols)


def _lower_bounds(logits):
    m = jnp.maximum(logits[0:1], logits[1:2])
    e0, e1 = jnp.exp(logits[0:1] - m), jnp.exp(logits[1:2] - m)
    den = e0 + e1
    p0, p1 = e0 / den, e1 / den
    low0 = p0 - p0
    low1 = (p0 + p1) - p0
    return (p0, p1), (low0, low1)


def _lb_fwd(lb_logits):
    def body(lg_ref, out_ref):
        _, (low0, low1) = _lower_bounds(lg_ref[...])
        out_ref[0:1, :] = jnp.clip(low0, 0.0, 1.0)
        out_ref[1:2, :] = jnp.clip(low1, 0.0, 1.0)

    return pl.pallas_call(body, name="lb_fwd", out_shape=jax.ShapeDtypeStruct(lb_logits.shape, F32),
                          compiler_params=_params())(lb_logits)


def _row_spec(cols=D_MODEL):
    return pl.BlockSpec((1, cols), lambda *_: (0, 0))


def _prenorm_fwd(x, g, shift, scale, tm, name, after=None):
    seq = x.shape[0]

    def body(x_ref, g_ref, sh_ref, sc_ref, h_ref):
        xv = x_ref[...]
        rs = lax.rsqrt(jnp.mean(xv * xv, axis=-1, keepdims=True) + NORM_EPS)
        h = (xv * rs * g_ref[...]) * (1.0 + sc_ref[...]) + sh_ref[...]
        h_ref[...] = h.astype(h_ref.dtype)

    tile = pl.BlockSpec((tm, D_MODEL), lambda i: (i, 0))
    return _pallas_after(
        body, 4, after, name=name, grid=(seq // tm,),
        in_specs=[tile, _row_spec(), _row_spec(), _row_spec()], out_specs=tile,
        out_shape=jax.ShapeDtypeStruct((seq, D_MODEL), MXU_DTYPE),
        compiler_params=_params(dimension_semantics=("parallel",)),
    )(x, g, shift, scale)


def _in_proj(h, win_g, tm, name, after=None):
    seq = h.shape[0]

    def body(h_ref, w_ref, z_ref, w_pair):
        @pl.when(pl.program_id(1) == 0)
        def _():
            w_pair[...] = jnp.concatenate([w_ref[0], w_ref[1]], axis=1)

        z_ref[...] = jnp.dot(h_ref[...], w_pair[...], preferred_element_type=F32)

    return _pallas_after(
        body, 2, after, name=name, grid=(N_DEV // 2, seq // tm),
        in_specs=[pl.BlockSpec((tm, D_MODEL), lambda j, i: (i, 0)),
                  pl.BlockSpec((2, D_MODEL, IN_COLS), lambda j, i: (j, 0, 0))],
        out_specs=pl.BlockSpec((tm, 2 * IN_COLS), lambda j, i: (i, j)),
        out_shape=jax.ShapeDtypeStruct((seq, IN_WIDTH), F32),
        scratch_shapes=[pltpu.VMEM((D_MODEL, 2 * IN_COLS), MXU_DTYPE)],
        compiler_params=_params(dimension_semantics=("parallel", "arbitrary")),
    )(h, win_g)


def _shift_down(v, j, pos):
    return jnp.where(pos >= j, pltpu.roll(v, j, 0), 0.0)


def _shift_up(v, j, pos, seq):
    return jnp.where(pos < seq - j, pltpu.roll(v, seq - j, 0), 0.0)


def _select_window(g, candidates):
    out = candidates[-1]
    for i in range(len(candidates) - 2, -1, -1):
        out = jnp.where(g == i, candidates[i], out)
    return out


def _pool_mean_minus_token(u, g, pos):
    sums, acc = [], u
    for j in (1, 2, 4, 8):
        acc = acc + _shift_down(acc, j, pos)
        sums.append(acc)
    wsum = _select_window(g, sums)
    width = jnp.left_shift(2, g).astype(F32)
    count = jnp.minimum(pos.astype(F32) + 1.0, width)
    return wsum / count - u, count


def _pool_fwd(z, pool_w_l, pool_scale_l, name, after=None):
    seq = z.shape[0]

    def body(pv_ref, pg_ref, w_ref, sc_ref, out_ref):
        g = pl.program_id(0)
        pos = lax.broadcasted_iota(jnp.int32, (seq, GROUP_DIM), 0)
        pm, _ = _pool_mean_minus_token(pv_ref[...], g, pos)
        lin = _dot(pm, w_ref[...]) * sc_ref[...]
        pg = pg_ref[...]
        out_ref[...] = (lin * (pg * _sigmoid(pg))).astype(out_ref.dtype)

    return _pallas_after(
        body, 4, after, name=name, grid=(POOL_GROUPS,),
        in_specs=[pl.BlockSpec((seq, GROUP_DIM), lambda g: (0, g)),
                  pl.BlockSpec((seq, GROUP_DIM), lambda g: (0, POOL_GROUPS + g)),
                  pl.BlockSpec((None, GROUP_DIM, GROUP_DIM), lambda g: (g, 0, 0)),
                  pl.BlockSpec((1, GROUP_DIM), lambda g: (0, g))],
        out_specs=pl.BlockSpec((seq, GROUP_DIM), lambda g: (0, g)),
        out_shape=jax.ShapeDtypeStruct((seq, POOL_WIDTH), MXU_DTYPE),
        compiler_params=_params(dimension_semantics=("parallel",)),
    )(z, z, pool_w_l, pool_scale_l)


def _chunk_masks():
    row = lax.broadcasted_iota(jnp.int32, (CHUNK, CHUNK), 0)
    col = lax.broadcasted_iota(jnp.int32, (CHUNK, CHUNK), 1)
    causal = row >= col
    before_sub = col < (row // SUB) * SUB
    suffix = row <= col
    return causal, before_sub, suffix


def _masked_sums(masks, v):
    lhs = jnp.concatenate([m.astype(jnp.bfloat16) for m in masks], axis=0)
    hi = v.astype(jnp.bfloat16)
    rest = v - hi.astype(F32)
    mid = rest.astype(jnp.bfloat16)
    lo = (rest - mid.astype(F32)).astype(jnp.bfloat16)
    out = jnp.dot(lhs, hi, preferred_element_type=F32)
    out += jnp.dot(lhs, mid, preferred_element_type=F32)
    out += jnp.dot(lhs, lo, preferred_element_type=F32)
    return [out[i * CHUNK:(i + 1) * CHUNK] for i in range(len(masks))]


def _gates(zf, lb):
    sg = _sigmoid(zf)
    f = lb + (1.0 - lb) * sg
    logf = jnp.log(jnp.maximum(f, LOG_FLOOR))
    return sg, f, logf


def _intra_blocks(q_h, k_h, cum_h, base_h, causal):
    rel = cum_h - base_h
    out = []
    for i in range(N_SUB):
        rows = slice(i * SUB, (i + 1) * SUB)
        e_q = jnp.exp(rel[rows])
        base_i = jnp.concatenate([base_h[rows]] * N_SUB, axis=0)
        e_k = jnp.exp(jnp.minimum(base_i - cum_h, EXP_CLAMP))
        q_t = (q_h[rows] * e_q).astype(MXU_DTYPE)
        k_t = (k_h * e_k).astype(MXU_DTYPE)
        a_i = jnp.where(causal[rows], _dot_nt(q_t, k_t), 0.0)
        out.append((q_t, k_t, e_q, e_k, a_i))
    return out


def _hgrn_fwd(z, lb_l, gn_l, name, after=None):
    seq = z.shape[0]
    n_chunks = seq // CHUNK
    per_step = min(FWD_STEP_CHUNKS, n_chunks)
    rows_per_step = per_step * CHUNK

    def body(hq_ref, hf_ref, hi_ref, hg_ref, lb_ref, gn_ref, o_ref, bin_ref, st_ref, cb_ref, state):
        @pl.when(pl.program_id(0) == 0)
        def _():
            state[...] = jnp.zeros_like(state)

        causal, before_sub, _ = _chunk_masks()
        for cc in range(per_step):
            rows = slice(cc * CHUNK, (cc + 1) * CHUNK)
            _, f, logf = _gates(hf_ref[rows, :], lb_ref[...])
            kk = 1.0 - f
            hq = hq_ref[rows, :]
            q = hq * _sigmoid(hq)
            cum, base = _masked_sums([causal, before_sub], logf)
            cb_ref[rows, 0:D_MODEL] = cum
            cb_ref[rows, D_MODEL:2 * D_MODEL] = base
            st_ref[cc] = state[...]
            for h in range(HEADS):
                sl = slice(h * HEAD_DIM, (h + 1) * HEAD_DIM)
                q_h, k_h, cum_h = q[:, sl], kk[:, sl], cum[:, sl]
                v_h = hi_ref[rows, sl]
                st_h = state[h]
                blocks = _intra_blocks(q_h, k_h, cum_h, base[:, sl], causal)
                a = jnp.concatenate([b[4] for b in blocks], axis=0)
                o_h = _dot_nt(q_h * jnp.exp(cum_h), st_h) + _dot(a, v_h)
                last = jnp.sum(logf[:, sl], axis=0, keepdims=True)
                state[h] = st_h * jnp.exp(last) + _dot_tn(v_h, k_h * jnp.exp(last - cum_h))
                rs = lax.rsqrt(jnp.mean(o_h * o_h, axis=-1, keepdims=True) + NORM_EPS)
                hg = hg_ref[rows, sl]
                o_ref[rows, sl] = o_h
                bin_ref[rows, sl] = ((o_h * rs * gn_ref[...]) * (hg * _sigmoid(hg))).astype(bin_ref.dtype)

    def col(block):
        return pl.BlockSpec((rows_per_step, D_MODEL), lambda c: (c, block))

    tile = pl.BlockSpec((rows_per_step, D_MODEL), lambda c: (c, 0))
    return _pallas_after(
        body, 6, after, name=name, grid=(n_chunks // per_step,),
        in_specs=[col(COL_HQ), col(COL_HF), col(COL_HI), col(COL_HG), _row_spec(), _row_spec(HEAD_DIM)],
        out_specs=[tile, tile, pl.BlockSpec((per_step, HEADS, HEAD_DIM, HEAD_DIM), lambda c: (c, 0, 0, 0)),
                   pl.BlockSpec((rows_per_step, 2 * D_MODEL), lambda c: (c, 0))],
        out_shape=[jax.ShapeDtypeStruct((seq, D_MODEL), F32),
                   jax.ShapeDtypeStruct((seq, D_MODEL), MXU_DTYPE),
                   jax.ShapeDtypeStruct((n_chunks, HEADS, HEAD_DIM, HEAD_DIM), F32),
                   jax.ShapeDtypeStruct((seq, 2 * D_MODEL), F32)],
        scratch_shapes=[pltpu.VMEM((HEADS, HEAD_DIM, HEAD_DIM), F32)],
        compiler_params=_params(dimension_semantics=("arbitrary",)),
    )(z, z, z, z, lb_l, gn_l)


def _rms_parts(y):
    rs = lax.rsqrt(jnp.mean(y * y, axis=-1, keepdims=True) + NORM_EPS)
    return rs, y * rs


def _merge_fwd(a_in, b_in, z, x, wpo_g, who_g, wout_g, gate, g_post, tm, name, target=None):
    seq = x.shape[0]
    with_loss = target is not None

    def body(*refs):
        a_ref, b_ref, mgp_ref, mgh_ref, x_ref, wpo_ref, who_ref, wout_ref, gate_ref, gp_ref = refs[:10]
        ba_ref, bb_ref, mer_ref, y_ref, last_ref = refs[10 + with_loss:15 + with_loss]
        a = a_ref[...]
        ba = _dot(a, jnp.concatenate([wpo_ref[j] for j in range(N_DEV)], axis=1))
        bb = _dot(b_ref[...], who_ref[...])
        merged = _sigmoid(mgp_ref[...]) * ba + _sigmoid(mgh_ref[...]) * bb
        y = _dot(merged, wout_ref[...])
        _, yn = _rms_parts(y)
        ba_ref[...] = ba.astype(ba_ref.dtype)
        bb_ref[...] = bb.astype(bb_ref.dtype)
        mer_ref[...] = merged.astype(mer_ref.dtype)
        y_ref[...] = y.astype(y_ref.dtype)
        x_next = x_ref[...] + gate_ref[...] * (yn * gp_ref[...])
        if not with_loss:
            last_ref[...] = x_next
            return
        loss_ref = refs[16]

        @pl.when(pl.program_id(0) == 0)
        def _():
            loss_ref[...] = jnp.zeros_like(loss_ref)

        err = x_next - refs[10][...]
        loss_ref[...] += 0.5 * jnp.sum(jnp.mean(err * err, axis=-1, keepdims=True), axis=0, keepdims=True)
        last_ref[...] = err * (1.0 / D_MODEL)

    def tile(cols=D_MODEL, block=0):
        return pl.BlockSpec((tm, cols), lambda i: (i, block))

    full = pl.BlockSpec((D_MODEL, D_MODEL), lambda i: (0, 0))
    act = jax.ShapeDtypeStruct((seq, D_MODEL), MXU_DTYPE)
    f32 = jax.ShapeDtypeStruct((seq, D_MODEL), F32)
    one = [pl.BlockSpec((1, 1), lambda i: (0, 0))] if with_loss else []
    return pl.pallas_call(
        body, name=name, grid=(seq // tm,),
        in_specs=[tile(POOL_WIDTH), tile(), tile(block=COL_MGP), tile(block=COL_MGH), tile(),
                  pl.BlockSpec((N_DEV, POOL_WIDTH, GROUP_DIM), lambda i: (0, 0, 0)),
                  full, full, _row_spec(), _row_spec()] + ([tile()] if with_loss else []),
        out_specs=[tile(), tile(), tile(), tile(), tile()] + one,
        out_shape=[act, act, act, act, f32] + ([jax.ShapeDtypeStruct((1, 1), F32)] if with_loss else []),
        compiler_params=_params(dimension_semantics=("arbitrary" if with_loss else "parallel",)),
    )(a_in, b_in, z, z, x, wpo_g, who_g, wout_g, gate, g_post, *([target] if with_loss else []))


def _stage_copy(stage, sems, dst, slot, step, where):
    rows, cols = where(step)
    return pltpu.make_async_copy(stage.at[slot], dst.at[rows, cols], sems.at[slot])


def _stage_begin(stage, sems, dst, step, where):
    slot = step % 2

    @pl.when(step >= 2)
    def _():
        _stage_copy(stage, sems, dst, slot, step - 2, where).wait()

    return slot


def _stage_end(stage, sems, dst, step, n_steps, where):
    slot = step % 2
    _stage_copy(stage, sems, dst, slot, step, where).start()

    @pl.when(step == n_steps - 1)
    def _():
        _stage_copy(stage, sems, dst, slot, step, where).wait()
        if n_steps > 1:
            _stage_copy(stage, sems, dst, 1 - slot, step - 1, where).wait()


def _merge_bwd(dx, y, ba, bb, z, wpo_g, who_g, wout_g, gate, g_post, dz, tm, name):
    seq = dx.shape[0]
    n_steps = seq // tm

    def body(dx_ref, y_ref, ba_ref, bb_ref, mgp_ref, mgh_ref, wpo_ref, who_ref, wout_ref, gate_ref, gp_ref, _,
             dy_ref, dba_ref, dbb_ref, da_ref, db_ref, dz_ref, acc_ref, stage, sems):
        step = pl.program_id(0)

        @pl.when(step == 0)
        def _():
            acc_ref[...] = jnp.zeros_like(acc_ref)

        def where(t):
            return pl.ds(t * tm, tm), pl.ds(COL_MGP * D_MODEL, 2 * D_MODEL)

        dmg_ref = stage.at[_stage_begin(stage, sems, dz_ref, step, where)]

        dxv = dx_ref[...]
        rs, yn = _rms_parts(y_ref[...].astype(F32))
        acc_ref[0:1, :] += jnp.sum(dxv * yn * gp_ref[...], axis=0, keepdims=True)
        acc_ref[1:2, :] += jnp.sum(dxv * gate_ref[...] * yn, axis=0, keepdims=True)
        dyn = dxv * (gate_ref[...] * gp_ref[...])
        dy = rs * (dyn - yn * jnp.mean(dyn * yn, axis=-1, keepdims=True))
        dmerged = _dot_nt(dy, wout_ref[...])
        sp, sh = _sigmoid(mgp_ref[...]), _sigmoid(mgh_ref[...])
        dba, dbb = sp * dmerged, sh * dmerged
        dmg_ref[:, 0:D_MODEL] = (dmerged * ba_ref[...].astype(F32) * sp * (1.0 - sp)).astype(dmg_ref.dtype)
        dmg_ref[:, D_MODEL:2 * D_MODEL] = (dmerged * bb_ref[...].astype(F32) * sh * (1.0 - sh)).astype(dmg_ref.dtype)
        da = _dot_nt(dba, jnp.concatenate([wpo_ref[j] for j in range(N_DEV)], axis=1))
        dy_ref[...] = dy.astype(dy_ref.dtype)
        dba_ref[...] = dba.astype(dba_ref.dtype)
        dbb_ref[...] = dbb.astype(dbb_ref.dtype)
        da_ref[...] = da.astype(da_ref.dtype)
        db_ref[...] = _dot_nt(dbb, who_ref[...]).astype(db_ref.dtype)
        _stage_end(stage, sems, dz_ref, step, n_steps, where)

    def tile(cols=D_MODEL, block=0):
        return pl.BlockSpec((tm, cols), lambda i: (i, block))

    full = pl.BlockSpec((D_MODEL, D_MODEL), lambda i: (0, 0))
    hbm = pl.BlockSpec(memory_space=pl.ANY)
    act = jax.ShapeDtypeStruct((seq, D_MODEL), MXU_DTYPE)
    return pl.pallas_call(
        body, name=name, grid=(n_steps,),
        in_specs=[tile(), tile(), tile(), tile(), tile(block=COL_MGP), tile(block=COL_MGH),
                  pl.BlockSpec((N_DEV, POOL_WIDTH, GROUP_DIM), lambda i: (0, 0, 0)),
                  full, full, _row_spec(), _row_spec(), hbm],
        out_specs=[tile(), tile(), tile(), tile(POOL_WIDTH), tile(), hbm,
                   pl.BlockSpec((8, D_MODEL), lambda i: (0, 0))],
        out_shape=[act, act, act, jax.ShapeDtypeStruct((seq, POOL_WIDTH), MXU_DTYPE), act,
                   jax.ShapeDtypeStruct(dz.shape, dz.dtype),
                   jax.ShapeDtypeStruct((8, D_MODEL), F32)],
        input_output_aliases={11: 5},
        scratch_shapes=[pltpu.VMEM((2, tm, 2 * D_MODEL), MXU_DTYPE), pltpu.SemaphoreType.DMA((2,))],
        compiler_params=_params(dimension_semantics=("arbitrary",)),
    )(dx, y, ba, bb, z, z, wpo_g, who_g, wout_g, gate, g_post, dz)


def _grad_out_weights(merged, dy, b_in, dbb, a_in, dba, name):
    seq = merged.shape[0]
    tn = D_MODEL // 2
    per_step = tn // GROUP_DIM

    def body(mer_ref, dy_ref, b_ref, dbb_ref, a_ref, dba_ref, gout_ref, gho_ref, gpo_ref):
        gout_ref[...] = _dot_tn(mer_ref[...], dy_ref[...]).astype(gout_ref.dtype)
        gho_ref[...] = _dot_tn(b_ref[...], dbb_ref[...]).astype(gho_ref.dtype)
        g_po = _dot_tn(a_ref[...], dba_ref[...])
        for j in range(per_step):
            gpo_ref[j] = g_po[:, j * GROUP_DIM:(j + 1) * GROUP_DIM].astype(gpo_ref.dtype)

    def whole(cols):
        return pl.BlockSpec((seq, cols), lambda j: (0, 0))

    cols = pl.BlockSpec((seq, tn), lambda j: (0, j))
    return pl.pallas_call(
        body, name=name, grid=(D_MODEL // tn,),
        in_specs=[whole(D_MODEL), cols, whole(D_MODEL), cols, whole(POOL_WIDTH), cols],
        out_specs=[pl.BlockSpec((D_MODEL, tn), lambda j: (0, j)), pl.BlockSpec((D_MODEL, tn), lambda j: (0, j)),
                   pl.BlockSpec((per_step, POOL_WIDTH, GROUP_DIM), lambda j: (j, 0, 0))],
        out_shape=[jax.ShapeDtypeStruct((D_MODEL, D_MODEL), WIRE_DTYPE),
                   jax.ShapeDtypeStruct((D_MODEL, D_MODEL), WIRE_DTYPE),
                   jax.ShapeDtypeStruct((N_DEV, POOL_WIDTH, GROUP_DIM), WIRE_DTYPE)],
        compiler_params=_params(dimension_semantics=("parallel",)),
    )(merged, dy, b_in, dbb, a_in, dba)


def _hgrn_bwd(db_in, z, o, states, cum_base, lb_l, gn_l, dz, name, after=None):
    seq = z.shape[0]
    per_step = min(BWD_STEP_CHUNKS, seq // CHUNK)
    rows_per_step = per_step * CHUNK
    n_steps = seq // rows_per_step
    last_step = n_steps - 1

    def body(db_ref, hq_ref, hf_ref, hi_ref, hg_ref, o_ref, st_ref, cb_ref, lb_ref, gn_ref, _,
             dz_hbm, dlb_ref, dgn_ref, dstate, dq_buf, dk_buf, dg_buf, stage, sems):
        step = pl.program_id(0)

        @pl.when(step == 0)
        def _():
            dstate[...] = jnp.zeros_like(dstate)
            dlb_ref[...] = jnp.zeros_like(dlb_ref)
            dgn_ref[...] = jnp.zeros_like(dgn_ref)

        def one_chunk(cc, *args):
            one_chunk_body((db_ref, hq_ref, hf_ref, hi_ref, hg_ref, o_ref, st_ref, cb_ref, dlb_ref, dgn_ref, dstate,
                            dq_buf, dk_buf, dg_buf), cc, *args)

        def where(t):
            return pl.ds((last_step - t) * rows_per_step, rows_per_step), pl.ds(COL_HQ * D_MODEL, 4 * D_MODEL)

        dz_step = stage.at[_stage_begin(stage, sems, dz_hbm, step, where)]
        causal, before_sub, suffix = _chunk_masks()
        lb = lb_ref[...]
        gn = gn_ref[...]
        for cc in reversed(range(per_step)):
            one_chunk(cc, dz_step, causal, before_sub, suffix, lb, gn)
        _stage_end(stage, sems, dz_hbm, step, n_steps, where)

    def one_chunk_body(refs, cc, dz_step, causal, before_sub, suffix, lb, gn):
        (db_ref, hq_ref, hf_ref, hi_ref, hg_ref, o_ref, st_ref, cb_ref, dlb_ref, dgn_ref, dstate,
         dq_buf, dk_buf, dg_buf) = refs
        rows = slice(cc * CHUNK, (cc + 1) * CHUNK)
        dz_ref = dz_step.at[rows, :]
        dq_buf, dk_buf, dg_buf = dq_buf.at[cc], dk_buf.at[cc], dg_buf.at[cc]
        sg, f, logf = _gates(hf_ref[rows, :], lb)
        kk = 1.0 - f
        hq = hq_ref[rows, :]
        sq = _sigmoid(hq)
        q = hq * sq
        cum, base = cb_ref[rows, 0:D_MODEL], cb_ref[rows, D_MODEL:2 * D_MODEL]
        dgn = jnp.zeros((1, HEAD_DIM), F32)
        dlast = []
        for h in range(HEADS):
            sl = slice(h * HEAD_DIM, (h + 1) * HEAD_DIM)
            q_h, k_h, cum_h = q[:, sl], kk[:, sl], cum[:, sl]
            v_h = hi_ref[rows, sl]
            st_h = st_ref[cc, h]
            dst_h = dstate[h]
            rs, ohat = _rms_parts(o_ref[rows, sl])
            hg = hg_ref[rows, sl]
            shg = _sigmoid(hg)
            d_bin = db_ref[rows, sl].astype(F32)
            don = d_bin * (hg * shg)
            dgn += jnp.sum(don * ohat, axis=0, keepdims=True)
            dohat = don * gn
            do = rs * (dohat - ohat * jnp.mean(dohat * ohat, axis=-1, keepdims=True))
            dz_ref[:, 3 * D_MODEL + h * HEAD_DIM:3 * D_MODEL + (h + 1) * HEAD_DIM] = (
                d_bin * (ohat * gn) * _dsilu(hg, shg)).astype(dz_ref.dtype)
            last = cb_ref[(cc + 1) * CHUNK - 1:(cc + 1) * CHUNK, sl]
            g_in = jnp.exp(cum_h)
            d_out = jnp.exp(last - cum_h)
            q_bar, k_bar = q_h * g_in, k_h * d_out
            blocks = _intra_blocks(q_h, k_h, cum_h, base[:, sl], causal)
            a = jnp.concatenate([b[4] for b in blocks], axis=0)
            da = jnp.where(causal, _dot_nt(do, v_h), 0.0)
            dv = _dot_tn(a, do) + _dot_nt(k_bar, dst_h)
            dq_bar, dk_bar = _dot(do, st_h), _dot(v_h, dst_h)
            dk = dk_bar * d_out
            dq_parts, dg_parts = [], []
            dg_k = k_bar * dk_bar
            dlast.append(jnp.sum(k_bar * dk_bar, axis=0, keepdims=True)
                         + jnp.exp(last) * jnp.sum(st_h * dst_h, axis=0, keepdims=True))
            for i, (q_t, k_t, e_q, e_k, _) in enumerate(blocks):
                da_i = da[i * SUB:(i + 1) * SUB].astype(MXU_DTYPE)
                dq_t = _dot(da_i, k_t)
                dk_t = _dot_tn(da_i, q_t)
                dq_parts.append(dq_t * e_q)
                dk += dk_t * e_k
                dg_parts.append(q_t.astype(F32) * dq_t)
                dg_k += k_t.astype(F32) * dk_t
            dq = dq_bar * g_in + jnp.concatenate(dq_parts, axis=0)
            dg_buf[:, sl] = q_bar * dq_bar + jnp.concatenate(dg_parts, axis=0) - dg_k
            dstate[h] = dst_h * jnp.exp(last) + _dot_tn(do, q_bar)
            dq_buf[:, sl] = dq
            dk_buf[:, sl] = dk
            dz_ref[:, 2 * D_MODEL + h * HEAD_DIM:2 * D_MODEL + (h + 1) * HEAD_DIM] = dv.astype(dz_ref.dtype)
        dgn_ref[...] += dgn
        dq_all, dk_all = dq_buf[...], dk_buf[...]
        dlogf = _masked_sums([suffix], dg_buf[...])[0] + jnp.concatenate(dlast, axis=1)
        df = jnp.where(f > LOG_FLOOR, dlogf / f, 0.0) - dk_all
        dlb_ref[...] += jnp.sum(df * (1.0 - sg), axis=0, keepdims=True)
        dz_ref[:, 0:D_MODEL] = (dq_all * _dsilu(hq, sq)).astype(dz_ref.dtype)
        dz_ref[:, D_MODEL:2 * D_MODEL] = (df * (1.0 - lb) * sg * (1.0 - sg)).astype(dz_ref.dtype)

    def col(block):
        return pl.BlockSpec((rows_per_step, D_MODEL), lambda c: (last_step - c, block))

    hbm = pl.BlockSpec(memory_space=pl.ANY)
    return _pallas_after(
        body, 11, after, name=name, grid=(n_steps,),
        in_specs=[col(0), col(COL_HQ), col(COL_HF), col(COL_HI), col(COL_HG), col(0),
                  pl.BlockSpec((per_step, HEADS, HEAD_DIM, HEAD_DIM), lambda c: (last_step - c, 0, 0, 0)),
                  pl.BlockSpec((rows_per_step, 2 * D_MODEL), lambda c: (last_step - c, 0)),
                  _row_spec(), _row_spec(HEAD_DIM), hbm],
        out_specs=[hbm, _row_spec(), _row_spec(HEAD_DIM)],
        out_shape=[jax.ShapeDtypeStruct(dz.shape, dz.dtype),
                   jax.ShapeDtypeStruct((1, D_MODEL), F32), jax.ShapeDtypeStruct((1, HEAD_DIM), F32)],
        input_output_aliases={10: 0},
        scratch_shapes=[pltpu.VMEM((HEADS, HEAD_DIM, HEAD_DIM), F32)]
        + [pltpu.VMEM((per_step, CHUNK, D_MODEL), F32)] * 3
        + [pltpu.VMEM((2, rows_per_step, 4 * D_MODEL), MXU_DTYPE), pltpu.SemaphoreType.DMA((2,))],
        compiler_params=_params(dimension_semantics=("arbitrary",)),
    )(db_in, z, z, z, z, o, states, cum_base, lb_l, gn_l, dz)


def _pool_bwd(da_in, z, pool_w_l, pool_scale_l, dz, name, after=None):
    seq = z.shape[0]

    def body(da_ref, pv_ref, pg_ref, w_ref, sc_ref, _, dz_hbm, dw_ref, dsc_ref, stage_pv, stage_pg, sems_pv, sems_pg):
        g = pl.program_id(0)

        def where_pv(t):
            return pl.ds(0, seq), pl.ds(pl.multiple_of(t * GROUP_DIM, GROUP_DIM), GROUP_DIM)

        def where_pg(t):
            return pl.ds(0, seq), pl.ds(pl.multiple_of(POOL_WIDTH + t * GROUP_DIM, GROUP_DIM), GROUP_DIM)

        dpv_ref = stage_pv.at[_stage_begin(stage_pv, sems_pv, dz_hbm, g, where_pv)]
        dpg_ref = stage_pg.at[_stage_begin(stage_pg, sems_pg, dz_hbm, g, where_pg)]
        pos = lax.broadcasted_iota(jnp.int32, (seq, GROUP_DIM), 0)
        pm, count = _pool_mean_minus_token(pv_ref[...], g, pos)
        lin0 = _dot(pm, w_ref[...])
        pg = pg_ref[...]
        spg = _sigmoid(pg)
        da = da_ref[...].astype(F32)
        dlin = da * (pg * spg)
        dpg_ref[...] = (da * (lin0 * sc_ref[...]) * _dsilu(pg, spg)).astype(dpg_ref.dtype)
        dsc_ref[...] = jnp.sum(dlin * lin0, axis=0, keepdims=True)
        dl0 = dlin * sc_ref[...]
        dw_ref[...] = _dot_tn(pm, dl0)
        dpm = _dot_nt(dl0, w_ref[...])
        sums, acc = [], dpm / count
        for j in (1, 2, 4, 8):
            acc = acc + _shift_up(acc, j, pos, seq)
            sums.append(acc)
        dpv_ref[...] = (_select_window(g, sums) - dpm).astype(dpv_ref.dtype)
        _stage_end(stage_pv, sems_pv, dz_hbm, g, POOL_GROUPS, where_pv)
        _stage_end(stage_pg, sems_pg, dz_hbm, g, POOL_GROUPS, where_pg)

    grp = pl.BlockSpec((seq, GROUP_DIM), lambda g: (0, g))
    hbm = pl.BlockSpec(memory_space=pl.ANY)
    stage = pltpu.VMEM((2, seq, GROUP_DIM), MXU_DTYPE)
    return _pallas_after(
        body, 6, after, name=name, grid=(POOL_GROUPS,),
        in_specs=[grp, grp, pl.BlockSpec((seq, GROUP_DIM), lambda g: (0, POOL_GROUPS + g)),
                  pl.BlockSpec((None, GROUP_DIM, GROUP_DIM), lambda g: (g, 0, 0)),
                  pl.BlockSpec((1, GROUP_DIM), lambda g: (0, g)), hbm],
        out_specs=[hbm, pl.BlockSpec((None, GROUP_DIM, GROUP_DIM), lambda g: (g, 0, 0)),
                   pl.BlockSpec((1, GROUP_DIM), lambda g: (0, g))],
        out_shape=[jax.ShapeDtypeStruct(dz.shape, dz.dtype),
                   jax.ShapeDtypeStruct((POOL_GROUPS, GROUP_DIM, GROUP_DIM), F32),
                   jax.ShapeDtypeStruct((1, POOL_WIDTH), F32)],
        input_output_aliases={5: 0},
        scratch_shapes=[stage, stage, pltpu.SemaphoreType.DMA((2,)), pltpu.SemaphoreType.DMA((2,))],
        compiler_params=_params(dimension_semantics=("arbitrary",)),
    )(da_in, z, z, pool_w_l, pool_scale_l, dz)


def _in_proj_dw(h, dz, name, after=None):
    seq = h.shape[0]

    def body(h_ref, dz_ref, out_ref):
        pair = lax.dot_general(h_ref[...], dz_ref[...], (((0,), (0,)), ((), ())), preferred_element_type=F32)
        out_ref[0] = pair[:, 0:IN_COLS].astype(out_ref.dtype)
        out_ref[1] = pair[:, IN_COLS:].astype(out_ref.dtype)

    return _pallas_after(
        body, 2, after, name=name, grid=(N_DEV // 2,),
        in_specs=[pl.BlockSpec((seq, D_MODEL), lambda j: (0, 0)),
                  pl.BlockSpec((seq, 2 * IN_COLS), lambda j: (0, j))],
        out_specs=pl.BlockSpec((2, D_MODEL, IN_COLS), lambda j: (j, 0, 0)),
        out_shape=jax.ShapeDtypeStruct((N_DEV, D_MODEL, IN_COLS), WIRE_DTYPE),
        compiler_params=_params(dimension_semantics=("parallel",)),
    )(h, dz)


def _in_proj_dh(dz, win_g, tm, name, after=None):
    seq = dz.shape[0]

    def body(dz_ref, w_ref, dh_ref):
        @pl.when(pl.program_id(1) == 0)
        def _():
            dh_ref[...] = jnp.zeros_like(dh_ref)

        w_pair = jnp.concatenate([w_ref[0], w_ref[1]], axis=1)
        dh_ref[...] += lax.dot_general(dz_ref[...], w_pair, (((1,), (1,)), ((), ())), preferred_element_type=F32)

    return _pallas_after(
        body, 2, after, name=name, grid=(seq // tm, N_DEV // 2),
        in_specs=[pl.BlockSpec((tm, 2 * IN_COLS), lambda i, j: (i, j)),
                  pl.BlockSpec((2, D_MODEL, IN_COLS), lambda i, j: (j, 0, 0))],
        out_specs=pl.BlockSpec((tm, D_MODEL), lambda i, j: (i, 0)),
        out_shape=jax.ShapeDtypeStruct((seq, D_MODEL), F32),
        compiler_params=_params(dimension_semantics=("parallel", "arbitrary")),
    )(dz, win_g)


def _prenorm_bwd(x, dh, dx_res, g, scale, tm, name, after=None):
    seq = x.shape[0]

    def body(x_ref, dh_ref, dxr_ref, g_ref, sc_ref, dx_ref, acc_ref):
        @pl.when(pl.program_id(0) == 0)
        def _():
            acc_ref[...] = jnp.zeros_like(acc_ref)

        rs, xn = _rms_parts(x_ref[...])
        dh = dh_ref[...]
        acc_ref[0:1, :] += jnp.sum(dh, axis=0, keepdims=True)
        acc_ref[1:2, :] += jnp.sum(dh * (xn * g_ref[...]), axis=0, keepdims=True)
        dhn = dh * (1.0 + sc_ref[...])
        acc_ref[2:3, :] += jnp.sum(dhn * xn, axis=0, keepdims=True)
        dxn = dhn * g_ref[...]
        dx_ref[...] = rs * (dxn - xn * jnp.mean(dxn * xn, axis=-1, keepdims=True)) + dxr_ref[...]

    tile = pl.BlockSpec((tm, D_MODEL), lambda i: (i, 0))
    return _pallas_after(
        body, 5, after, name=name, grid=(seq // tm,),
        in_specs=[tile, tile, tile, _row_spec(), _row_spec()],
        out_specs=[tile, pl.BlockSpec((8, D_MODEL), lambda i: (0, 0))],
        out_shape=[jax.ShapeDtypeStruct((seq, D_MODEL), F32), jax.ShapeDtypeStruct((8, D_MODEL), F32)],
        compiler_params=_params(dimension_semantics=("arbitrary",)),
    )(x, dh, dx_res, g, scale)


def _adamw_math(w, g, m, v):
    m = ADAM_B1 * m + (1.0 - ADAM_B1) * g
    v = ADAM_B2 * v + (1.0 - ADAM_B2) * (g * g)
    m_hat = m / (1.0 - ADAM_B1 ** ADAM_STEP)
    v_hat = v / (1.0 - ADAM_B2 ** ADAM_STEP)
    delta = -ADAM_LR * (m_hat / (jnp.sqrt(v_hat) + ADAM_EPS) + ADAM_WD * w)
    return delta, m, v


def _adamw_layer(w, m, v, contribs, l, tr, name, prev=None):
    _, rows, cols = w.shape
    n = len(contribs)

    def body(*refs):
        w_ref, m_ref, v_ref = refs[:3]
        c_refs = refs[3:3 + n]
        g_ref, d_ref, mo_ref, vo_ref = refs[-4:]
        g = c_refs[0][...].astype(F32)
        for c_ref in c_refs[1:]:
            g += c_ref[...].astype(F32)
        delta, mn, vn = _adamw_math(w_ref[...], g, m_ref[...], v_ref[...])
        g_ref[...] = g
        d_ref[...] = delta
        mo_ref[...] = mn
        vo_ref[...] = vn

    tile = pl.BlockSpec((None, tr, cols), lambda i: (l, i, 0))
    in_specs = [tile, tile, tile] + [pl.BlockSpec((None, tr, cols), lambda i, s=slot: (s, i, 0)) for _, slot in contribs]
    operands = [w, m, v] + [arr for arr, _ in contribs]
    aliases = {}
    if prev is not None:
        aliases = {len(operands) + k: k for k in range(4)}
        in_specs += [pl.BlockSpec(memory_space=pl.ANY)] * 4
        operands += list(prev)
    shape = jax.ShapeDtypeStruct(w.shape, F32)
    return pl.pallas_call(
        body, name=name, grid=(rows // tr,), in_specs=in_specs, out_specs=[tile] * 4, out_shape=[shape] * 4,
        input_output_aliases=aliases,
        compiler_params=_params(dimension_semantics=("parallel",)),
    )(*operands)


def _adamw_small(w_pack, m_pack, v_pack, g_late, g_early, shapes):
    pieces, r = {}, 0
    for name, _, n in _SMALL_ROWS:
        pieces.setdefault(name, []).append((r, n))
        r += n
    names = list(pieces)

    def body(w_ref, m_ref, v_ref, gl_ref, ge_ref, *rest):
        outs, packs = rest[:4 * len(names)], rest[4 * len(names):]
        g_l, g_e = gl_ref[0][0:SMALL_LATE_ROWS], ge_ref[0]
        for d in range(1, N_DEV):
            g_l += gl_ref[d][0:SMALL_LATE_ROWS]
            g_e += ge_ref[d]
        g = jnp.concatenate([g_l, g_e], axis=0)
        w = w_ref[...]
        r0, r1, r2 = LB_ROW0, LB_ROW0 + 8, LB_ROW0 + 16
        lg0, lg1 = w[r0:r1], w[r1:r2]
        mx = jnp.maximum(lg0, lg1)
        e0, e1 = jnp.exp(lg0 - mx), jnp.exp(lg1 - mx)
        p0, p1 = e0 / (e0 + e1), e1 / (e0 + e1)
        low = ((p0 - p0), (p0 + p1) - p0)
        dlow = [g_rows * jnp.where((lo > 0.0) & (lo < 1.0), 1.0, jnp.where((lo == 0.0) | (lo == 1.0), 0.5, 0.0))
                for g_rows, lo in ((g[r0:r1], low[0]), (g[r1:r2], low[1]))]
        dp0 = (dlow[0] + dlow[1]) - (dlow[0] + dlow[1])
        dp1 = dlow[1]
        inner = p0 * dp0 + p1 * dp1
        g = jnp.concatenate([g[:r0], p0 * (dp0 - inner), p1 * (dp1 - inner), g[r2:]], axis=0)
        delta, mn, vn = _adamw_math(w, g, m_ref[...], v_ref[...])
        for kind, val in enumerate((g, delta, mn, vn)):
            packs[kind][...] = val
            for j, name in enumerate(names):
                out, at = outs[kind * len(names) + j], 0
                for start, n in pieces[name]:
                    if name in flat:
                        for r in range(n):
                            layer, c = divmod(at + r, flat[name])
                            out[layer:layer + 1, c * 128:(c + 1) * 128] = packs[kind][start + r:start + r + 1, :]
                    else:
                        out[at:at + n, :] = packs[kind][start:start + n, :]
                    at += n

    rows = {name: sum(n for _, n in pieces[name]) for name in names}
    flat = {name: rows[name] // DEPTH for name in names if len(shapes[name]) == 2}
    outs = pl.pallas_call(
        body, name="adamw_small",
        out_shape=[jax.ShapeDtypeStruct(shapes[name] if name in flat else (rows[name], 128), F32)
                   for _ in range(4) for name in names],
        scratch_shapes=[pltpu.VMEM(w_pack.shape, F32)] * 4, compiler_params=_params(),
    )(w_pack, m_pack, v_pack, g_late, g_early)
    return [{name: outs[kind * len(names) + j].reshape(shapes[name]) for j, name in enumerate(names)}
            for kind in range(4)]


def _pack_small(parts, first=0, last=len(_SMALL_ROWS)):
    rows = [(parts[name] if l is None else parts[name][l]).reshape(n, 128) for name, l, n in _SMALL_ROWS[first:last]]
    if last == len(_SMALL_ROWS):
        rows.append(jnp.zeros((SMALL_ROWS_PAD - sum(n for _, _, n in _SMALL_ROWS), 128), F32))
    return jnp.concatenate(rows, axis=0)


def kernel(x, c, w_ada, b_ada, g_pre, g_post, w_in, pool_w, pool_scale, lb_logits, hgrn_norm_g, w_pool_o, w_hgrn_o, w_out, loss_target, m_w_ada, m_b_ada, m_g_pre, m_g_post, m_w_in, m_pool_w, m_pool_scale, m_lb_logits, m_hgrn_norm_g, m_w_pool_o, m_w_hgrn_o, m_w_out, v_w_ada, v_b_ada, v_g_pre, v_g_post, v_w_in, v_pool_w, v_pool_scale, v_lb_logits, v_hgrn_norm_g, v_w_pool_o, v_w_hgrn_o, v_w_out):
    seq = x.shape[1]
    tm = min(1024, seq)
    tm_merge = min(512, seq)
    pos = _my_position()
    me = pos[3]

    big = dict(win=w_in, wpo=w_pool_o, who=w_hgrn_o, wout=w_out)
    units = [["win0"], ["wpo0", "who0", "wout0"], ["win1", "wpo1", "who1", "wout1"]]
    g_streams = [_gather_streams(keys) for keys in units]
    g_state = [None] * len(units)

    def gather_start(us, after, first=None):
        bufs = dict(first[0]) if first else {}
        for k in [k for u in us for k in units[u]]:
            arr = big[k[:-1]]
            bufs["g_" + k] = _with_own_slot(arr[int(k[-1])].astype(WIRE_DTYPE), me)
        streams = ([first[1]] if first else []) + [s for u in us for s in g_streams[u][:2]]
        bufs, sems, token = _comm_call("gather_start_" + "_".join(map(str, us)), bufs, start=streams, after=after)
        if first:
            first_out, sems = ({k: bufs[k] for k in first[0]}, sems[0]), sems[1:]
        for n, u in enumerate(us):
            g_state[u] = dict(bufs={"g_" + k: bufs["g_" + k] for k in units[u]},
                              sems=sems[2 * n:2 * n + 2])
        return (token, first_out) if first else token

    def gather_pass(u, after):
        st = g_state[u]
        to_chips, _, pass_on = g_streams[u]
        st["bufs"], (st["pass_sems"],), _ = _comm_call(f"gather_pass_{u}", st["bufs"], start=[pass_on],
                                                       wait=[(to_chips, st["sems"][0])], after=after)

    def gather_done(u, after=None):
        st = g_state[u]
        _, to_sibling, pass_on = g_streams[u]
        bufs, _, _ = _comm_call(f"gather_done_{u}", st["bufs"], after=after,
                                wait=[(to_sibling, st["sems"][1]), (pass_on, st["pass_sems"])])
        return {k: bufs["g_" + k] for k in units[u]}

    c_stream = _direct_gather_stream("c")
    token, (c_bufs, c_sems) = gather_start([0], None, first=(dict(s_c=c, g_c=_with_own_slot(c, me)), c_stream))
    c_bufs, _, _ = _comm_call("gather_c_done", c_bufs, wait=[(c_stream, c_sems)])
    c_all = c_bufs["g_c"].reshape(N_DEV, D_MODEL)
    b_cols = lax.dynamic_slice_in_dim(b_ada, me * ADA_COLS, ADA_COLS, axis=1)
    ada_part = _ada_fwd(c_all, w_ada, b_cols)
    gather_pass(0, ada_part)
    ada_all = _allgather_small(ada_part.reshape(DEPTH * N_DEV, ADA_COLS), "allgather_ada",
                               after=g_state[0]["bufs"]["g_win0"])
    ada = lax.dynamic_index_in_dim(ada_all.reshape(N_DEV, DEPTH, N_DEV, ADA_COLS), me, axis=2, keepdims=False)
    ada = jnp.transpose(ada, (1, 0, 2)).reshape(DEPTH, 3 * D_MODEL)
    shift = [ada[l:l + 1, 0:D_MODEL] for l in range(DEPTH)]
    scale = [ada[l:l + 1, D_MODEL:2 * D_MODEL] for l in range(DEPTH)]
    gate = [ada[l:l + 1, 2 * D_MODEL:] for l in range(DEPTH)]

    lb = _lb_fwd(lb_logits)

    gw = {}
    xs, saved = [x[0]], []
    for l in range(DEPTH):
        h = _prenorm_fwd(xs[l], g_pre[l:l + 1], shift[l], scale[l], tm, f"prenorm_fwd_{l}",
                         after=token if l == 0 else None)
        token = None
        if l == 0:
            gw.update(gather_done(0, h))
            token = gather_start([1, 2], gw["win0"])
        else:
            gw.update(gather_done(2, h))
        z = _in_proj(h, gw[f"win{l}"], min(1024, seq), f"in_proj_{l}", after=token)
        a_in = _pool_fwd(z, pool_w[l], pool_scale[l:l + 1], f"pool_fwd_{l}")
        o, b_in, states, cum_base = _hgrn_fwd(z, lb[l:l + 1], hgrn_norm_g[l:l + 1], f"hgrn_fwd_{l}")
        if l == 0:
            gather_pass(1, b_in)
            gw.update(gather_done(1))
        who_l = gw[f"who{l}"].reshape(D_MODEL, D_MODEL)
        wout_l = gw[f"wout{l}"].reshape(D_MODEL, D_MODEL)
        last = l == DEPTH - 1
        ba, bb, merged, y, *out = _merge_fwd(a_in, b_in, z, xs[l], gw[f"wpo{l}"], who_l, wout_l, gate[l],
                                             g_post[l:l + 1], tm_merge, f"merge_fwd_{l}",
                                             target=loss_target[0] if last else None)
        if last:
            dx, loss_part = out
        else:
            xs.append(out[0])
            gather_pass(2, out[0])
        saved.append((h, z, a_in, o, b_in, states, cum_base, ba, bb, merged, y, who_l, wout_l))


    chips = _other_chips(pos)
    pair_idx = jnp.stack([_dev_index(cx, cy, pos[2]) for cx, cy in chips] + [me]).astype(jnp.int32)
    pair_rows = dict(win=512, wpo=POOL_WIDTH, who=HEAD_DIM, wout=HEAD_DIM)

    def scatter_pair_start(u, grads):
        keys = list(grads)
        pair, to_chips = _scatter_streams(keys)
        bufs = {}
        for k in keys:
            bufs["g_" + k] = grads[k]
            bufs["st_" + k] = lax.empty((4,) + grads[k].shape[1:], WIRE_DTYPE)
        bufs, (sems,), token = _comm_call(f"scatter_pair_start_{u}", bufs, start=[pair])
        return dict(u=u, keys=keys, pair=pair, to_chips=to_chips, bufs=bufs, sems=sems, token=token)

    def scatter_pair_finish(st, after):
        u, keys = st["u"], st["keys"]
        bufs, _, _ = _comm_call(f"scatter_pair_done_{u}", st["bufs"], wait=[(st["pair"], st["sems"])], after=after)
        bufs2 = {}
        for k in keys:
            bufs2["ps_" + k] = _pair_sum(bufs["g_" + k], bufs["st_" + k], pair_idx, bufs["g_" + k].shape[1],
                                         f"pair_sum_{k}")
            bufs2["ld_" + k] = lax.empty((3,) + bufs["g_" + k].shape[1:], WIRE_DTYPE)
        st.update(bufs=bufs2)

    def scatter_chips_start(st, after=None):
        bufs2, (sems,), token = _comm_call(f"scatter_chips_start_{st['u']}", st["bufs"], start=[st["to_chips"]],
                                           after=after)
        st.update(bufs=bufs2, sems=sems, token=token)

    def scatter_finish(st, after):
        bufs, _, _ = _comm_call(f"scatter_chips_done_{st['u']}", st["bufs"], wait=[(st["to_chips"], st["sems"])],
                                after=after)
        return {k: [(bufs["ps_" + k], 3), (bufs["ld_" + k], 0), (bufs["ld_" + k], 1), (bufs["ld_" + k], 2)]
                for k in st["keys"]}

    moments = dict(win=(m_w_in, v_w_in), wpo=(m_w_pool_o, v_w_pool_o), who=(m_w_hgrn_o, v_w_hgrn_o),
                   wout=(m_w_out, v_w_out))
    big_out = {}

    def finish_unit(unit, after):
        for k, contribs in scatter_finish(scat[unit], after).items():
            wname, l = k[:-1], int(k[-1])
            big_out[wname] = _adamw_layer(big[wname], moments[wname][0], moments[wname][1], contribs, l,
                                          pair_rows[wname], f"adamw_{k}", prev=big_out.get(wname))
            after = big_out[wname][0]
        return after

    d_ada, small, scat = [None] * DEPTH, [None] * DEPTH, {}
    for l in reversed(range(DEPTH)):
        h, z, a_in, o, b_in, states, cum_base, ba, bb, merged, y, who_l, wout_l = saved[l]
        dy, dba, dbb, da_in, db_in, dz, acc_post = _merge_bwd(
            dx, y, ba, bb, z, gw[f"wpo{l}"], who_l, wout_l, gate[l], g_post[l:l + 1],
            lax.empty((seq, IN_WIDTH), MXU_DTYPE), tm_merge, f"merge_bwd_{l}")
        g_out, g_ho, g_po = _grad_out_weights(merged, dy, b_in, dbb, a_in, dba, f"grad_out_weights_{l}")
        g_small = {f"wout{l}": g_out.reshape(N_DEV, HEAD_DIM, D_MODEL),
                   f"who{l}": g_ho.reshape(N_DEV, HEAD_DIM, D_MODEL), f"wpo{l}": g_po}
        st_small = scat["small0"] = scatter_pair_start("small0", g_small) if l == 0 else None
        dz, dlb, dgn = _hgrn_bwd(db_in, z, o, states, cum_base, lb[l:l + 1], hgrn_norm_g[l:l + 1], dz, f"hgrn_bwd_{l}",
                                 after=st_small and st_small["token"])
        if l == 0:
            scatter_pair_finish(st_small, dlb)
            scatter_chips_start(st_small)
        dz, dpw, dps = _pool_bwd(da_in, z, pool_w[l], pool_scale[l:l + 1], dz, f"pool_bwd_{l}",
                                 after=st_small and st_small["token"])
        small[l] = dict(g_post=acc_post[1], pool_w=dpw, pool_scale=dps[0], lb_logits=dlb[0], hgrn_norm_g=dgn[0])
        token = None
        if l == 0:
            parts = {name: jnp.stack([small[0][name], small[1][name]]) for name in small[0]}
            parts.update(b_ada=[None, d_ada[1]], g_pre=[None, small[1]["g_pre"]])
            sg_stream = _direct_gather_stream("sg")
            early = _pack_small(parts, 2)
            sg_bufs, (sg_sems,), token = _comm_call(
                "small_grads_start", dict(s_sg=early, g_sg=_with_own_slot(early, me)), start=[sg_stream])
        g_win = {f"win{l}": _in_proj_dw(h, dz, f"grad_w_in_{l}", after=token)}
        st_win = scat[f"win{l}"] = scatter_pair_start(f"win{l}", g_win if l == 0 else {**g_small, **g_win})
        if l > 0:
            dh = _in_proj_dh(dz, gw[f"win{l}"], seq, f"in_proj_dh_{l}", after=st_win["token"])
            scatter_pair_finish(st_win, dh)
            scatter_chips_start(st_win)
        else:
            after = finish_unit("win1", st_win["token"])
            scatter_pair_finish(st_win, after)
            scatter_chips_start(st_win)
            after = finish_unit("small0", st_win["token"])
            dh = _in_proj_dh(dz, gw[f"win{l}"], seq, f"in_proj_dh_{l}", after=after)
        dx, acc_pre = _prenorm_bwd(xs[l], dh, dx, g_pre[l:l + 1], scale[l], tm, f"prenorm_bwd_{l}",
                                   after=st_win["token"])
        d_ada[l] = jnp.concatenate([acc_pre[0], acc_pre[1], acc_post[0]])
        small[l]["g_pre"] = acc_pre[2]
    grad_x = dx[None]

    sg_bufs, _, _ = _comm_call("small_grads_done", sg_bufs, wait=[(sg_stream, sg_sems)], after=dx)
    g_early = sg_bufs["g_sg"]

    def w_ada_layer(l, d_rows, prev):
        d_cols = lax.dynamic_slice_in_dim(d_rows.reshape(N_DEV, 3 * D_MODEL), me * ADA_COLS, ADA_COLS, axis=1)
        return _adamw_layer(w_ada, m_w_ada, v_w_ada, [(_ada_bwd(c_all, d_cols, f"ada_bwd_{l}"), 0)], l, 512,
                            f"adamw_w_ada{l}", prev=prev)

    ada_out = w_ada_layer(1, g_early[:, 0:24, :], None)

    parts = dict(b_ada=[d_ada[0]], g_pre=[small[0]["g_pre"]])
    late = jnp.concatenate([_pack_small(parts, 0, 2), jnp.broadcast_to(loss_part, (8, 128))], axis=0)
    g_late = _allgather_small(late, "allgather_late_grads", after=ada_out[0])
    loss = jnp.sum(g_late[:, SMALL_LATE_ROWS, 0])
    small_names = list(dict.fromkeys(name for name, _, _ in _SMALL_ROWS))
    weights = dict(b_ada=b_ada, g_pre=g_pre, g_post=g_post, pool_w=pool_w, pool_scale=pool_scale,
                   lb_logits=lb_logits, hgrn_norm_g=hgrn_norm_g)
    m_small = dict(b_ada=m_b_ada, g_pre=m_g_pre, g_post=m_g_post, pool_w=m_pool_w, pool_scale=m_pool_scale,
                   lb_logits=m_lb_logits, hgrn_norm_g=m_hgrn_norm_g)
    v_small = dict(b_ada=v_b_ada, g_pre=v_g_pre, g_post=v_g_post, pool_w=v_pool_w, pool_scale=v_pool_scale,
                   lb_logits=v_lb_logits, hgrn_norm_g=v_hgrn_norm_g)
    shapes = {name: weights[name].shape for name in small_names}
    small_out = _adamw_small(_pack_small(weights), _pack_small(m_small), _pack_small(v_small), g_late, g_early,
                             shapes)

    ada_out = w_ada_layer(0, g_late[:, 0:24, :], ada_out)
    finish_unit("win0", ada_out[1][0, 0:8, 0:128] + small_out[1]["pool_scale"][0:1, 0:128])

    def leaf(kind):
        s = small_out[kind]
        return (ada_out[kind], s["b_ada"], s["g_pre"], s["g_post"], big_out["win"][kind], s["pool_w"], s["pool_scale"],
                s["lb_logits"], s["hgrn_norm_g"], big_out["wpo"][kind], big_out["who"][kind], big_out["wout"][kind])

    return (loss, grad_x) + leaf(0) + leaf(1) + leaf(2) + leaf(3)
```

```python
import jax
import jax.numpy as jnp
from jax import lax
from jax.experimental import pallas as pl
from jax.experimental.pallas import tpu as pltpu

F32 = jnp.float32
MXU_DTYPE = jnp.bfloat16
WIRE_DTYPE = jnp.bfloat16

N_DEV = 8
DEPTH = 2
D_MODEL = 1024
HEADS = 8
HEAD_DIM = 128
POOL_GROUPS = 4
GROUP_DIM = 128
POOL_WIDTH = POOL_GROUPS * GROUP_DIM
IN_WIDTH = 7168
CHUNK = 64
SUB = 16
N_SUB = CHUNK // SUB
FWD_STEP_CHUNKS = 8
BWD_STEP_CHUNKS = 4
EXP_CLAMP = 80.0
NORM_EPS = 1e-6
LOG_FLOOR = 1e-30
ADA_COLS = 3 * D_MODEL // N_DEV
IN_COLS = IN_WIDTH // N_DEV
COL_HQ, COL_HF, COL_HI, COL_HG, COL_MGP, COL_MGH = 1, 2, 3, 4, 5, 6

ADAM_LR = 0.001
ADAM_B1 = 0.9
ADAM_B2 = 0.999
ADAM_EPS = 1e-08
ADAM_WD = 0.01
ADAM_STEP = 10

VMEM_LIMIT = 48 * 1024 * 1024
MESH_ID = pl.DeviceIdType.MESH
HIGHEST = lax.Precision.HIGHEST

_SMALL_ROWS = (("b_ada", 0, 24), ("g_pre", 0, 8), ("b_ada", 1, 24), ("g_pre", 1, 8), ("g_post", None, 16),
               ("pool_w", None, 1024), ("pool_scale", None, 8), ("lb_logits", None, 16), ("hgrn_norm_g", None, 2))
SMALL_LATE_ROWS = 32
SMALL_ROWS_PAD = 1136
LB_ROW0 = 32 + 32 + 16 + 1024 + 8


def _params(**kw):
    return pltpu.CompilerParams(vmem_limit_bytes=VMEM_LIMIT, **kw)


def _sigmoid(v):
    return 1.0 / (1.0 + jnp.exp(-v))


def _dsilu(v, s):
    return s * (1.0 + v * (1.0 - s))


def _dot(a, b):
    return jnp.dot(a.astype(MXU_DTYPE), b.astype(MXU_DTYPE), preferred_element_type=F32)


def _dot_nt(a, b):
    return lax.dot_general(a.astype(MXU_DTYPE), b.astype(MXU_DTYPE), (((1,), (1,)), ((), ())),
                           preferred_element_type=F32)


def _dot_tn(a, b):
    return lax.dot_general(a.astype(MXU_DTYPE), b.astype(MXU_DTYPE), (((0,), (0,)), ((), ())),
                           preferred_element_type=F32)


def _pallas_after(body, n_in, after, *, in_specs, **kw):
    if after is None:
        return pl.pallas_call(body, in_specs=in_specs, **kw)

    def tied(*refs):
        body(*refs[:n_in], *refs[n_in + 1:])

    call = pl.pallas_call(tied, in_specs=list(in_specs) + [pl.BlockSpec(memory_space=pl.ANY)], **kw)
    return lambda *operands: call(*operands, after)


def _my_position():
    mx, my, mc = lax.axis_index("x"), lax.axis_index("y"), lax.axis_index("c")
    return mx, my, mc, 4 * mx + 2 * my + mc


def _peer(mx, my, mc, k):
    px = 1 - mx if (k >> 2) & 1 else mx
    py = 1 - my if (k >> 1) & 1 else my
    pc = 1 - mc if k & 1 else mc
    return (px, py, pc), 4 * px + 2 * py + pc


def _allgather_small(v, name, after=None):
    rows, cols = v.shape

    def body(v_ref, out_ref, send_sems, recv_sems):
        mx, my, mc, me = _my_position()
        out_ref[me] = v_ref[...]
        copies = []
        for k in range(1, N_DEV):
            peer, _ = _peer(mx, my, mc, k)
            cp = pltpu.make_async_remote_copy(
                src_ref=v_ref, dst_ref=out_ref.at[me],
                send_sem=send_sems.at[k - 1], recv_sem=recv_sems.at[k - 1],
                device_id=peer, device_id_type=MESH_ID)
            cp.start()
            copies.append(cp)
        for cp in copies:
            cp.wait()

    return _pallas_after(
        body, 1, after, name=name,
        out_shape=jax.ShapeDtypeStruct((N_DEV, rows, cols), v.dtype),
        in_specs=[pl.BlockSpec(memory_space=pltpu.VMEM)],
        out_specs=pl.BlockSpec(memory_space=pltpu.VMEM),
        scratch_shapes=[pltpu.SemaphoreType.DMA((N_DEV - 1,)), pltpu.SemaphoreType.DMA((N_DEV - 1,))],
        compiler_params=_params(),
    )(v)


class _Stream:
    def __init__(self, n, plan):
        self.n, self.plan = n, plan


def _comm_call(name, bufs, start=(), wait=(), after=None):
    names = list(bufs)

    def body(*refs):
        it = iter(refs)
        buf_refs = {n: next(it) for n in names}
        wait_sems = [(next(it), next(it)) for _ in wait]
        if after is not None:
            next(it)
        start_sems = [(next(it), next(it)) for _ in start]
        for _ in names:
            next(it)
        token = next(it)
        pos = _my_position()

        def descriptors(stream, sems):
            return [pltpu.make_async_remote_copy(src_ref=src, dst_ref=dst, send_sem=sems[0].at[k], recv_sem=sems[1].at[k],
                                                 device_id=dev, device_id_type=MESH_ID)
                    for k, (src, dst, dev) in enumerate(stream.plan(buf_refs, pos))]

        for (stream, _), sems in zip(wait, wait_sems):
            for cp in descriptors(stream, sems):
                cp.wait_send()
                cp.wait_recv()
        for stream, sems in zip(start, start_sems):
            for cp in descriptors(stream, sems):
                cp.start()
        token[...] = jnp.zeros_like(token)

    hbm = pl.BlockSpec(memory_space=pltpu.HBM)
    sem = pl.BlockSpec(memory_space=pltpu.SEMAPHORE)
    operands = [pltpu.with_memory_space_constraint(bufs[n], pltpu.HBM) for n in names]
    in_specs = [hbm] * len(names)
    for _, (send_sems, recv_sems) in wait:
        operands += [send_sems, recv_sems]
        in_specs += [sem, sem]
    if after is not None:
        operands.append(after)
        in_specs.append(pl.BlockSpec(memory_space=pl.ANY))
    out_shape, out_specs = [], []
    for stream in start:
        out_shape += [pltpu.SemaphoreType.DMA((stream.n,)), pltpu.SemaphoreType.DMA((stream.n,))]
        out_specs += [sem, sem]
    n_sem_out = len(out_shape)
    out_shape += [pltpu.HBM(bufs[n].shape, bufs[n].dtype) for n in names]
    out_specs += [hbm] * len(names)
    out_shape.append(jax.ShapeDtypeStruct((8, 128), F32))
    out_specs.append(pl.BlockSpec(memory_space=pltpu.VMEM))
    outs = pl.pallas_call(
        body, name=name, out_shape=out_shape, in_specs=in_specs, out_specs=out_specs,
        input_output_aliases={i: n_sem_out + i for i in range(len(names))},
        compiler_params=pltpu.CompilerParams(has_side_effects=pltpu.SideEffectType.DATAFLOW_SIDE_EFFECTING),
    )(*operands)
    sems = [(outs[2 * i], outs[2 * i + 1]) for i in range(len(start))]
    return dict(zip(names, outs[n_sem_out:n_sem_out + len(names)])), sems, outs[-1]


def _with_own_slot(block, me):
    return lax.dynamic_update_index_in_dim(lax.empty((N_DEV,) + block.shape, block.dtype), block, me, 0)


def _other_chips(pos):
    mx, my, _, _ = pos
    return [(1 - mx if i & 2 else mx, 1 - my if i & 1 else my) for i in (1, 2, 3)]


def _dev_index(px, py, pc):
    return 4 * px + 2 * py + pc


def _gather_streams(keys):
    def to_chips(refs, pos):
        _, _, mc, me = pos
        return [(refs["g_" + k].at[me], refs["g_" + k].at[me], (cx, cy, mc))
                for k in keys for cx, cy in _other_chips(pos)]

    def to_sibling(refs, pos):
        mx, my, mc, me = pos
        return [(refs["g_" + k].at[me], refs["g_" + k].at[me], (mx, my, 1 - mc)) for k in keys]

    def pass_on(refs, pos):
        mx, my, mc, _ = pos
        out = []
        for k in keys:
            for cx, cy in _other_chips(pos):
                slot = refs["g_" + k].at[_dev_index(cx, cy, mc)]
                out.append((slot, slot, (mx, my, 1 - mc)))
        return out

    return _Stream(3 * len(keys), to_chips), _Stream(len(keys), to_sibling), _Stream(3 * len(keys), pass_on)


def _direct_gather_stream(key):
    def plan(refs, pos):
        mx, my, mc, me = pos
        return [(refs["s_" + key], refs["g_" + key].at[me], _peer(mx, my, mc, k)[0]) for k in range(1, N_DEV)]

    return _Stream(N_DEV - 1, plan)


def _scatter_streams(keys):
    def pair(refs, pos):
        mx, my, mc, _ = pos
        sib = (mx, my, 1 - mc)
        out = []
        for k in keys:
            for i, (cx, cy) in enumerate(_other_chips(pos)):
                out.append((refs["g_" + k].at[_dev_index(cx, cy, 1 - mc)], refs["st_" + k].at[i], sib))
            out.append((refs["g_" + k].at[_dev_index(mx, my, 1 - mc)], refs["st_" + k].at[3], sib))
        return out

    def chips(refs, pos):
        mc = pos[2]
        return [(refs["ps_" + k].at[i], refs["ld_" + k].at[i], (cx, cy, mc))
                for k in keys for i, (cx, cy) in enumerate(_other_chips(pos))]

    return _Stream(4 * len(keys), pair), _Stream(3 * len(keys), chips)


def _pair_sum(g, st, idx, tr, name):
    _, rows, cols = g.shape

    def body(idx_ref, g_ref, st_ref, out_ref):
        out_ref[...] = (g_ref[...].astype(F32) + st_ref[...].astype(F32)).astype(out_ref.dtype)

    return pl.pallas_call(
        body, name=name,
        grid_spec=pltpu.PrefetchScalarGridSpec(
            num_scalar_prefetch=1, grid=(4, rows // tr),
            in_specs=[pl.BlockSpec((None, tr, cols), lambda j, i, idx_ref: (idx_ref[j], i, 0)),
                      pl.BlockSpec((None, tr, cols), lambda j, i, idx_ref: (j, i, 0))],
            out_specs=pl.BlockSpec((None, tr, cols), lambda j, i, idx_ref: (j, i, 0))),
        out_shape=jax.ShapeDtypeStruct((4, rows, cols), WIRE_DTYPE),
        compiler_params=_params(dimension_semantics=("parallel", "parallel")),
    )(idx, g, st)


def _ada_fwd(c_all, w_ada, b_cols):
    def body(c_ref, w_ref, b_ref, out_ref):
        cv = c_ref[...]
        ca = cv * _sigmoid(cv)
        for l in range(DEPTH):
            out_ref[l] = jnp.dot(ca, w_ref[l], precision=HIGHEST, preferred_element_type=F32) + b_ref[l:l + 1, :]

    return pl.pallas_call(
        body, name="ada_fwd",
        out_shape=jax.ShapeDtypeStruct((DEPTH, N_DEV, ADA_COLS), F32),
        compiler_params=_params(),
    )(c_all, w_ada, b_cols)


def _ada_bwd(c_all, d_cols, name):
    def body(c_ref, d_ref, out_ref):
        cv = c_ref[...]
        ca = cv * _sigmoid(cv)
        out_ref[0] = lax.dot_general(ca, d_ref[...], (((0,), (0,)), ((), ())), precision=HIGHEST,
                                     preferred_element_type=F32)

    return pl.pallas_call(
        body, name=name,
        out_shape=jax.ShapeDtypeStruct((1, D_MODEL, ADA_COLS), F32),
        compiler_params=_params(),
    )(c_all, d_cols)


def _lower_bounds(logits):
    m = jnp.maximum(logits[0:1], logits[1:2])
    e0, e1 = jnp.exp(logits[0:1] - m), jnp.exp(logits[1:2] - m)
    den = e0 + e1
    p0, p1 = e0 / den, e1 / den
    low0 = p0 - p0
    low1 = (p0 + p1) - p0
    return (p0, p1), (low0, low1)


def _lb_fwd(lb_logits):
    def body(lg_ref, out_ref):
        _, (low0, low1) = _lower_bounds(lg_ref[...])
        out_ref[0:1, :] = jnp.clip(low0, 0.0, 1.0)
        out_ref[1:2, :] = jnp.clip(low1, 0.0, 1.0)

    return pl.pallas_call(body, name="lb_fwd", out_shape=jax.ShapeDtypeStruct(lb_logits.shape, F32),
                          compiler_params=_params())(lb_logits)


def _row_spec(cols=D_MODEL):
    return pl.BlockSpec((1, cols), lambda *_: (0, 0))


def _prenorm_fwd(x, g, shift, scale, tm, name, after=None):
    seq = x.shape[0]

    def body(x_ref, g_ref, sh_ref, sc_ref, h_ref):
        xv = x_ref[...]
        rs = lax.rsqrt(jnp.mean(xv * xv, axis=-1, keepdims=True) + NORM_EPS)
        h = (xv * rs * g_ref[...]) * (1.0 + sc_ref[...]) + sh_ref[...]
        h_ref[...] = h.astype(h_ref.dtype)

    tile = pl.BlockSpec((tm, D_MODEL), lambda i: (i, 0))
    return _pallas_after(
        body, 4, after, name=name, grid=(seq // tm,),
        in_specs=[tile, _row_spec(), _row_spec(), _row_spec()], out_specs=tile,
        out_shape=jax.ShapeDtypeStruct((seq, D_MODEL), MXU_DTYPE),
        compiler_params=_params(dimension_semantics=("parallel",)),
    )(x, g, shift, scale)


def _in_proj(h, win_g, tm, name, after=None):
    seq = h.shape[0]

    def body(h_ref, w_ref, z_ref, w_pair):
        @pl.when(pl.program_id(1) == 0)
        def _():
            w_pair[...] = jnp.concatenate([w_ref[0], w_ref[1]], axis=1)

        z_ref[...] = jnp.dot(h_ref[...], w_pair[...], preferred_element_type=F32)

    return _pallas_after(
        body, 2, after, name=name, grid=(N_DEV // 2, seq // tm),
        in_specs=[pl.BlockSpec((tm, D_MODEL), lambda j, i: (i, 0)),
                  pl.BlockSpec((2, D_MODEL, IN_COLS), lambda j, i: (j, 0, 0))],
        out_specs=pl.BlockSpec((tm, 2 * IN_COLS), lambda j, i: (i, j)),
        out_shape=jax.ShapeDtypeStruct((seq, IN_WIDTH), F32),
        scratch_shapes=[pltpu.VMEM((D_MODEL, 2 * IN_COLS), MXU_DTYPE)],
        compiler_params=_params(dimension_semantics=("parallel", "arbitrary")),
    )(h, win_g)


def _shift_down(v, j, pos):
    return jnp.where(pos >= j, pltpu.roll(v, j, 0), 0.0)


def _shift_up(v, j, pos, seq):
    return jnp.where(pos < seq - j, pltpu.roll(v, seq - j, 0), 0.0)


def _select_window(g, candidates):
    out = candidates[-1]
    for i in range(len(candidates) - 2, -1, -1):
        out = jnp.where(g == i, candidates[i], out)
    return out


def _pool_mean_minus_token(u, g, pos):
    sums, acc = [], u
    for j in (1, 2, 4, 8):
        acc = acc + _shift_down(acc, j, pos)
        sums.append(acc)
    wsum = _select_window(g, sums)
    width = jnp.left_shift(2, g).astype(F32)
    count = jnp.minimum(pos.astype(F32) + 1.0, width)
    return wsum / count - u, count


def _pool_fwd(z, pool_w_l, pool_scale_l, name, after=None):
    seq = z.shape[0]

    def body(pv_ref, pg_ref, w_ref, sc_ref, out_ref):
        g = pl.program_id(0)
        pos = lax.broadcasted_iota(jnp.int32, (seq, GROUP_DIM), 0)
        pm, _ = _pool_mean_minus_token(pv_ref[...], g, pos)
        lin = _dot(pm, w_ref[...]) * sc_ref[...]
        pg = pg_ref[...]
        out_ref[...] = (lin * (pg * _sigmoid(pg))).astype(out_ref.dtype)

    return _pallas_after(
        body, 4, after, name=name, grid=(POOL_GROUPS,),
        in_specs=[pl.BlockSpec((seq, GROUP_DIM), lambda g: (0, g)),
                  pl.BlockSpec((seq, GROUP_DIM), lambda g: (0, POOL_GROUPS + g)),
                  pl.BlockSpec((None, GROUP_DIM, GROUP_DIM), lambda g: (g, 0, 0)),
                  pl.BlockSpec((1, GROUP_DIM), lambda g: (0, g))],
        out_specs=pl.BlockSpec((seq, GROUP_DIM), lambda g: (0, g)),
        out_shape=jax.ShapeDtypeStruct((seq, POOL_WIDTH), MXU_DTYPE),
        compiler_params=_params(dimension_semantics=("parallel",)),
    )(z, z, pool_w_l, pool_scale_l)


def _chunk_masks():
    row = lax.broadcasted_iota(jnp.int32, (CHUNK, CHUNK), 0)
    col = lax.broadcasted_iota(jnp.int32, (CHUNK, CHUNK), 1)
    causal = row >= col
    before_sub = col < (row // SUB) * SUB
    suffix = row <= col
    return causal, before_sub, suffix


def _masked_sums(masks, v):
    lhs = jnp.concatenate([m.astype(jnp.bfloat16) for m in masks], axis=0)
    hi = v.astype(jnp.bfloat16)
    rest = v - hi.astype(F32)
    mid = rest.astype(jnp.bfloat16)
    lo = (rest - mid.astype(F32)).astype(jnp.bfloat16)
    out = jnp.dot(lhs, hi, preferred_element_type=F32)
    out += jnp.dot(lhs, mid, preferred_element_type=F32)
    out += jnp.dot(lhs, lo, preferred_element_type=F32)
    return [out[i * CHUNK:(i + 1) * CHUNK] for i in range(len(masks))]


def _gates(zf, lb):
    sg = _sigmoid(zf)
    f = lb + (1.0 - lb) * sg
    logf = jnp.log(jnp.maximum(f, LOG_FLOOR))
    return sg, f, logf


def _intra_blocks(q_h, k_h, cum_h, base_h, causal):
    rel = cum_h - base_h
    out = []
    for i in range(N_SUB):
        rows = slice(i * SUB, (i + 1) * SUB)
        e_q = jnp.exp(rel[rows])
        base_i = jnp.concatenate([base_h[rows]] * N_SUB, axis=0)
        e_k = jnp.exp(jnp.minimum(base_i - cum_h, EXP_CLAMP))
        q_t = (q_h[rows] * e_q).astype(MXU_DTYPE)
        k_t = (k_h * e_k).astype(MXU_DTYPE)
        a_i = jnp.where(causal[rows], _dot_nt(q_t, k_t), 0.0)
        out.append((q_t, k_t, e_q, e_k, a_i))
    return out


def _hgrn_fwd(z, lb_l, gn_l, name, after=None):
    seq = z.shape[0]
    n_chunks = seq // CHUNK
    per_step = min(FWD_STEP_CHUNKS, n_chunks)
    rows_per_step = per_step * CHUNK

    def body(hq_ref, hf_ref, hi_ref, hg_ref, lb_ref, gn_ref, o_ref, bin_ref, st_ref, cb_ref, state):
        @pl.when(pl.program_id(0) == 0)
        def _():
            state[...] = jnp.zeros_like(state)

        causal, before_sub, _ = _chunk_masks()
        for cc in range(per_step):
            rows = slice(cc * CHUNK, (cc + 1) * CHUNK)
            _, f, logf = _gates(hf_ref[rows, :], lb_ref[...])
            kk = 1.0 - f
            hq = hq_ref[rows, :]
            q = hq * _sigmoid(hq)
            cum, base = _masked_sums([causal, before_sub], logf)
            cb_ref[rows, 0:D_MODEL] = cum
            cb_ref[rows, D_MODEL:2 * D_MODEL] = base
            st_ref[cc] = state[...]
            for h in range(HEADS):
                sl = slice(h * HEAD_DIM, (h + 1) * HEAD_DIM)
                q_h, k_h, cum_h = q[:, sl], kk[:, sl], cum[:, sl]
                v_h = hi_ref[rows, sl]
                st_h = state[h]
                blocks = _intra_blocks(q_h, k_h, cum_h, base[:, sl], causal)
                a = jnp.concatenate([b[4] for b in blocks], axis=0)
                o_h = _dot_nt(q_h * jnp.exp(cum_h), st_h) + _dot(a, v_h)
                last = jnp.sum(logf[:, sl], axis=0, keepdims=True)
                state[h] = st_h * jnp.exp(last) + _dot_tn(v_h, k_h * jnp.exp(last - cum_h))
                rs = lax.rsqrt(jnp.mean(o_h * o_h, axis=-1, keepdims=True) + NORM_EPS)
                hg = hg_ref[rows, sl]
                o_ref[rows, sl] = o_h
                bin_ref[rows, sl] = ((o_h * rs * gn_ref[...]) * (hg * _sigmoid(hg))).astype(bin_ref.dtype)

    def col(block):
        return pl.BlockSpec((rows_per_step, D_MODEL), lambda c: (c, block))

    tile = pl.BlockSpec((rows_per_step, D_MODEL), lambda c: (c, 0))
    return _pallas_after(
        body, 6, after, name=name, grid=(n_chunks // per_step,),
        in_specs=[col(COL_HQ), col(COL_HF), col(COL_HI), col(COL_HG), _row_spec(), _row_spec(HEAD_DIM)],
        out_specs=[tile, tile, pl.BlockSpec((per_step, HEADS, HEAD_DIM, HEAD_DIM), lambda c: (c, 0, 0, 0)),
                   pl.BlockSpec((rows_per_step, 2 * D_MODEL), lambda c: (c, 0))],
        out_shape=[jax.ShapeDtypeStruct((seq, D_MODEL), F32),
                   jax.ShapeDtypeStruct((seq, D_MODEL), MXU_DTYPE),
                   jax.ShapeDtypeStruct((n_chunks, HEADS, HEAD_DIM, HEAD_DIM), F32),
                   jax.ShapeDtypeStruct((seq, 2 * D_MODEL), F32)],
        scratch_shapes=[pltpu.VMEM((HEADS, HEAD_DIM, HEAD_DIM), F32)],
        compiler_params=_params(dimension_semantics=("arbitrary",)),
    )(z, z, z, z, lb_l, gn_l)


def _rms_parts(y):
    rs = lax.rsqrt(jnp.mean(y * y, axis=-1, keepdims=True) + NORM_EPS)
    return rs, y * rs


def _merge_fwd(a_in, b_in, z, x, wpo_g, who_g, wout_g, gate, g_post, tm, name, target=None):
    seq = x.shape[0]
    with_loss = target is not None

    def body(*refs):
        a_ref, b_ref, mgp_ref, mgh_ref, x_ref, wpo_ref, who_ref, wout_ref, gate_ref, gp_ref = refs[:10]
        ba_ref, bb_ref, mer_ref, y_ref, last_ref = refs[10 + with_loss:15 + with_loss]
        a = a_ref[...]
        ba = _dot(a, jnp.concatenate([wpo_ref[j] for j in range(N_DEV)], axis=1))
        bb = _dot(b_ref[...], who_ref[...])
        merged = _sigmoid(mgp_ref[...]) * ba + _sigmoid(mgh_ref[...]) * bb
        y = _dot(merged, wout_ref[...])
        _, yn = _rms_parts(y)
        ba_ref[...] = ba.astype(ba_ref.dtype)
        bb_ref[...] = bb.astype(bb_ref.dtype)
        mer_ref[...] = merged.astype(mer_ref.dtype)
        y_ref[...] = y.astype(y_ref.dtype)
        x_next = x_ref[...] + gate_ref[...] * (yn * gp_ref[...])
        if not with_loss:
            last_ref[...] = x_next
            return
        loss_ref = refs[16]

        @pl.when(pl.program_id(0) == 0)
        def _():
            loss_ref[...] = jnp.zeros_like(loss_ref)

        err = x_next - refs[10][...]
        loss_ref[...] += 0.5 * jnp.sum(jnp.mean(err * err, axis=-1, keepdims=True), axis=0, keepdims=True)
        last_ref[...] = err * (1.0 / D_MODEL)

    def tile(cols=D_MODEL, block=0):
        return pl.BlockSpec((tm, cols), lambda i: (i, block))

    full = pl.BlockSpec((D_MODEL, D_MODEL), lambda i: (0, 0))
    act = jax.ShapeDtypeStruct((seq, D_MODEL), MXU_DTYPE)
    f32 = jax.ShapeDtypeStruct((seq, D_MODEL), F32)
    one = [pl.BlockSpec((1, 1), lambda i: (0, 0))] if with_loss else []
    return pl.pallas_call(
        body, name=name, grid=(seq // tm,),
        in_specs=[tile(POOL_WIDTH), tile(), tile(block=COL_MGP), tile(block=COL_MGH), tile(),
                  pl.BlockSpec((N_DEV, POOL_WIDTH, GROUP_DIM), lambda i: (0, 0, 0)),
                  full, full, _row_spec(), _row_spec()] + ([tile()] if with_loss else []),
        out_specs=[tile(), tile(), tile(), tile(), tile()] + one,
        out_shape=[act, act, act, act, f32] + ([jax.ShapeDtypeStruct((1, 1), F32)] if with_loss else []),
        compiler_params=_params(dimension_semantics=("arbitrary" if with_loss else "parallel",)),
    )(a_in, b_in, z, z, x, wpo_g, who_g, wout_g, gate, g_post, *([target] if with_loss else []))


def _stage_copy(stage, sems, dst, slot, step, where):
    rows, cols = where(step)
    return pltpu.make_async_copy(stage.at[slot], dst.at[rows, cols], sems.at[slot])


def _stage_begin(stage, sems, dst, step, where):
    slot = step % 2

    @pl.when(step >= 2)
    def _():
        _stage_copy(stage, sems, dst, slot, step - 2, where).wait()

    return slot


def _stage_end(stage, sems, dst, step, n_steps, where):
    slot = step % 2
    _stage_copy(stage, sems, dst, slot, step, where).start()

    @pl.when(step == n_steps - 1)
    def _():
        _stage_copy(stage, sems, dst, slot, step, where).wait()
        if n_steps > 1:
            _stage_copy(stage, sems, dst, 1 - slot, step - 1, where).wait()


def _merge_bwd(dx, y, ba, bb, z, wpo_g, who_g, wout_g, gate, g_post, dz, tm, name):
    seq = dx.shape[0]
    n_steps = seq // tm

    def body(dx_ref, y_ref, ba_ref, bb_ref, mgp_ref, mgh_ref, wpo_ref, who_ref, wout_ref, gate_ref, gp_ref, _,
             dy_ref, dba_ref, dbb_ref, da_ref, db_ref, dz_ref, acc_ref, stage, sems):
        step = pl.program_id(0)

        @pl.when(step == 0)
        def _():
            acc_ref[...] = jnp.zeros_like(acc_ref)

        def where(t):
            return pl.ds(t * tm, tm), pl.ds(COL_MGP * D_MODEL, 2 * D_MODEL)

        dmg_ref = stage.at[_stage_begin(stage, sems, dz_ref, step, where)]

        dxv = dx_ref[...]
        rs, yn = _rms_parts(y_ref[...].astype(F32))
        acc_ref[0:1, :] += jnp.sum(dxv * yn * gp_ref[...], axis=0, keepdims=True)
        acc_ref[1:2, :] += jnp.sum(dxv * gate_ref[...] * yn, axis=0, keepdims=True)
        dyn = dxv * (gate_ref[...] * gp_ref[...])
        dy = rs * (dyn - yn * jnp.mean(dyn * yn, axis=-1, keepdims=True))
        dmerged = _dot_nt(dy, wout_ref[...])
        sp, sh = _sigmoid(mgp_ref[...]), _sigmoid(mgh_ref[...])
        dba, dbb = sp * dmerged, sh * dmerged
        dmg_ref[:, 0:D_MODEL] = (dmerged * ba_ref[...].astype(F32) * sp * (1.0 - sp)).astype(dmg_ref.dtype)
        dmg_ref[:, D_MODEL:2 * D_MODEL] = (dmerged * bb_ref[...].astype(F32) * sh * (1.0 - sh)).astype(dmg_ref.dtype)
        da = _dot_nt(dba, jnp.concatenate([wpo_ref[j] for j in range(N_DEV)], axis=1))
        dy_ref[...] = dy.astype(dy_ref.dtype)
        dba_ref[...] = dba.astype(dba_ref.dtype)
        dbb_ref[...] = dbb.astype(dbb_ref.dtype)
        da_ref[...] = da.astype(da_ref.dtype)
        db_ref[...] = _dot_nt(dbb, who_ref[...]).astype(db_ref.dtype)
        _stage_end(stage, sems, dz_ref, step, n_steps, where)

    def tile(cols=D_MODEL, block=0):
        return pl.BlockSpec((tm, cols), lambda i: (i, block))

    full = pl.BlockSpec((D_MODEL, D_MODEL), lambda i: (0, 0))
    hbm = pl.BlockSpec(memory_space=pl.ANY)
    act = jax.ShapeDtypeStruct((seq, D_MODEL), MXU_DTYPE)
    return pl.pallas_call(
        body, name=name, grid=(n_steps,),
        in_specs=[tile(), tile(), tile(), tile(), tile(block=COL_MGP), tile(block=COL_MGH),
                  pl.BlockSpec((N_DEV, POOL_WIDTH, GROUP_DIM), lambda i: (0, 0, 0)),
                  full, full, _row_spec(), _row_spec(), hbm],
        out_specs=[tile(), tile(), tile(), tile(POOL_WIDTH), tile(), hbm,
                   pl.BlockSpec((8, D_MODEL), lambda i: (0, 0))],
        out_shape=[act, act, act, jax.ShapeDtypeStruct((seq, POOL_WIDTH), MXU_DTYPE), act,
                   jax.ShapeDtypeStruct(dz.shape, dz.dtype),
                   jax.ShapeDtypeStruct((8, D_MODEL), F32)],
        input_output_aliases={11: 5},
        scratch_shapes=[pltpu.VMEM((2, tm, 2 * D_MODEL), MXU_DTYPE), pltpu.SemaphoreType.DMA((2,))],
        compiler_params=_params(dimension_semantics=("arbitrary",)),
    )(dx, y, ba, bb, z, z, wpo_g, who_g, wout_g, gate, g_post, dz)


def _grad_out_weights(merged, dy, b_in, dbb, a_in, dba, name):
    seq = merged.shape[0]
    tn = D_MODEL // 2
    per_step = tn // GROUP_DIM

    def body(mer_ref, dy_ref, b_ref, dbb_ref, a_ref, dba_ref, gout_ref, gho_ref, gpo_ref):
        gout_ref[...] = _dot_tn(mer_ref[...], dy_ref[...]).astype(gout_ref.dtype)
        gho_ref[...] = _dot_tn(b_ref[...], dbb_ref[...]).astype(gho_ref.dtype)
        g_po = _dot_tn(a_ref[...], dba_ref[...])
        for j in range(per_step):
            gpo_ref[j] = g_po[:, j * GROUP_DIM:(j + 1) * GROUP_DIM].astype(gpo_ref.dtype)

    def whole(cols):
        return pl.BlockSpec((seq, cols), lambda j: (0, 0))

    cols = pl.BlockSpec((seq, tn), lambda j: (0, j))
    return pl.pallas_call(
        body, name=name, grid=(D_MODEL // tn,),
        in_specs=[whole(D_MODEL), cols, whole(D_MODEL), cols, whole(POOL_WIDTH), cols],
        out_specs=[pl.BlockSpec((D_MODEL, tn), lambda j: (0, j)), pl.BlockSpec((D_MODEL, tn), lambda j: (0, j)),
                   pl.BlockSpec((per_step, POOL_WIDTH, GROUP_DIM), lambda j: (j, 0, 0))],
        out_shape=[jax.ShapeDtypeStruct((D_MODEL, D_MODEL), WIRE_DTYPE),
                   jax.ShapeDtypeStruct((D_MODEL, D_MODEL), WIRE_DTYPE),
                   jax.ShapeDtypeStruct((N_DEV, POOL_WIDTH, GROUP_DIM), WIRE_DTYPE)],
        compiler_params=_params(dimension_semantics=("parallel",)),
    )(merged, dy, b_in, dbb, a_in, dba)


def _hgrn_bwd(db_in, z, o, states, cum_base, lb_l, gn_l, dz, name, after=None):
    seq = z.shape[0]
    per_step = min(BWD_STEP_CHUNKS, seq // CHUNK)
    rows_per_step = per_step * CHUNK
    n_steps = seq // rows_per_step
    last_step = n_steps - 1

    def body(db_ref, hq_ref, hf_ref, hi_ref, hg_ref, o_ref, st_ref, cb_ref, lb_ref, gn_ref, _,
             dz_hbm, dlb_ref, dgn_ref, dstate, dq_buf, dk_buf, dg_buf, stage, sems):
        step = pl.program_id(0)

        @pl.when(step == 0)
        def _():
            dstate[...] = jnp.zeros_like(dstate)
            dlb_ref[...] = jnp.zeros_like(dlb_ref)
            dgn_ref[...] = jnp.zeros_like(dgn_ref)

        def one_chunk(cc, *args):
            one_chunk_body((db_ref, hq_ref, hf_ref, hi_ref, hg_ref, o_ref, st_ref, cb_ref, dlb_ref, dgn_ref, dstate,
                            dq_buf, dk_buf, dg_buf), cc, *args)

        def where(t):
            return pl.ds((last_step - t) * rows_per_step, rows_per_step), pl.ds(COL_HQ * D_MODEL, 4 * D_MODEL)

        dz_step = stage.at[_stage_begin(stage, sems, dz_hbm, step, where)]
        causal, before_sub, suffix = _chunk_masks()
        lb = lb_ref[...]
        gn = gn_ref[...]
        for cc in reversed(range(per_step)):
            one_chunk(cc, dz_step, causal, before_sub, suffix, lb, gn)
        _stage_end(stage, sems, dz_hbm, step, n_steps, where)

    def one_chunk_body(refs, cc, dz_step, causal, before_sub, suffix, lb, gn):
        (db_ref, hq_ref, hf_ref, hi_ref, hg_ref, o_ref, st_ref, cb_ref, dlb_ref, dgn_ref, dstate,
         dq_buf, dk_buf, dg_buf) = refs
        rows = slice(cc * CHUNK, (cc + 1) * CHUNK)
        dz_ref = dz_step.at[rows, :]
        dq_buf, dk_buf, dg_buf = dq_buf.at[cc], dk_buf.at[cc], dg_buf.at[cc]
        sg, f, logf = _gates(hf_ref[rows, :], lb)
        kk = 1.0 - f
        hq = hq_ref[rows, :]
        sq = _sigmoid(hq)
        q = hq * sq
        cum, base = cb_ref[rows, 0:D_MODEL], cb_ref[rows, D_MODEL:2 * D_MODEL]
        dgn = jnp.zeros((1, HEAD_DIM), F32)
        dlast = []
        for h in range(HEADS):
            sl = slice(h * HEAD_DIM, (h + 1) * HEAD_DIM)
            q_h, k_h, cum_h = q[:, sl], kk[:, sl], cum[:, sl]
            v_h = hi_ref[rows, sl]
            st_h = st_ref[cc, h]
            dst_h = dstate[h]
            rs, ohat = _rms_parts(o_ref[rows, sl])
            hg = hg_ref[rows, sl]
            shg = _sigmoid(hg)
            d_bin = db_ref[rows, sl].astype(F32)
            don = d_bin * (hg * shg)
            dgn += jnp.sum(don * ohat, axis=0, keepdims=True)
            dohat = don * gn
            do = rs * (dohat - ohat * jnp.mean(dohat * ohat, axis=-1, keepdims=True))
            dz_ref[:, 3 * D_MODEL + h * HEAD_DIM:3 * D_MODEL + (h + 1) * HEAD_DIM] = (
                d_bin * (ohat * gn) * _dsilu(hg, shg)).astype(dz_ref.dtype)
            last = cb_ref[(cc + 1) * CHUNK - 1:(cc + 1) * CHUNK, sl]
            g_in = jnp.exp(cum_h)
            d_out = jnp.exp(last - cum_h)
            q_bar, k_bar = q_h * g_in, k_h * d_out
            blocks = _intra_blocks(q_h, k_h, cum_h, base[:, sl], causal)
            a = jnp.concatenate([b[4] for b in blocks], axis=0)
            da = jnp.where(causal, _dot_nt(do, v_h), 0.0)
            dv = _dot_tn(a, do) + _dot_nt(k_bar, dst_h)
            dq_bar, dk_bar = _dot(do, st_h), _dot(v_h, dst_h)
            dk = dk_bar * d_out
            dq_parts, dg_parts = [], []
            dg_k = k_bar * dk_bar
            dlast.append(jnp.sum(k_bar * dk_bar, axis=0, keepdims=True)
                         + jnp.exp(last) * jnp.sum(st_h * dst_h, axis=0, keepdims=True))
            for i, (q_t, k_t, e_q, e_k, _) in enumerate(blocks):
                da_i = da[i * SUB:(i + 1) * SUB].astype(MXU_DTYPE)
                dq_t = _dot(da_i, k_t)
                dk_t = _dot_tn(da_i, q_t)
                dq_parts.append(dq_t * e_q)
                dk += dk_t * e_k
                dg_parts.append(q_t.astype(F32) * dq_t)
                dg_k += k_t.astype(F32) * dk_t
            dq = dq_bar * g_in + jnp.concatenate(dq_parts, axis=0)
            dg_buf[:, sl] = q_bar * dq_bar + jnp.concatenate(dg_parts, axis=0) - dg_k
            dstate[h] = dst_h * jnp.exp(last) + _dot_tn(do, q_bar)
            dq_buf[:, sl] = dq
            dk_buf[:, sl] = dk
            dz_ref[:, 2 * D_MODEL + h * HEAD_DIM:2 * D_MODEL + (h + 1) * HEAD_DIM] = dv.astype(dz_ref.dtype)
        dgn_ref[...] += dgn
        dq_all, dk_all = dq_buf[...], dk_buf[...]
        dlogf = _masked_sums([suffix], dg_buf[...])[0] + jnp.concatenate(dlast, axis=1)
        df = jnp.where(f > LOG_FLOOR, dlogf / f, 0.0) - dk_all
        dlb_ref[...] += jnp.sum(df * (1.0 - sg), axis=0, keepdims=True)
        dz_ref[:, 0:D_MODEL] = (dq_all * _dsilu(hq, sq)).astype(dz_ref.dtype)
        dz_ref[:, D_MODEL:2 * D_MODEL] = (df * (1.0 - lb) * sg * (1.0 - sg)).astype(dz_ref.dtype)

    def col(block):
        return pl.BlockSpec((rows_per_step, D_MODEL), lambda c: (last_step - c, block))

    hbm = pl.BlockSpec(memory_space=pl.ANY)
    return _pallas_after(
        body, 11, after, name=name, grid=(n_steps,),
        in_specs=[col(0), col(COL_HQ), col(COL_HF), col(COL_HI), col(COL_HG), col(0),
                  pl.BlockSpec((per_step, HEADS, HEAD_DIM, HEAD_DIM), lambda c: (last_step - c, 0, 0, 0)),
                  pl.BlockSpec((rows_per_step, 2 * D_MODEL), lambda c: (last_step - c, 0)),
                  _row_spec(), _row_spec(HEAD_DIM), hbm],
        out_specs=[hbm, _row_spec(), _row_spec(HEAD_DIM)],
        out_shape=[jax.ShapeDtypeStruct(dz.shape, dz.dtype),
                   jax.ShapeDtypeStruct((1, D_MODEL), F32), jax.ShapeDtypeStruct((1, HEAD_DIM), F32)],
        input_output_aliases={10: 0},
        scratch_shapes=[pltpu.VMEM((HEADS, HEAD_DIM, HEAD_DIM), F32)]
        + [pltpu.VMEM((per_step, CHUNK, D_MODEL), F32)] * 3
        + [pltpu.VMEM((2, rows_per_step, 4 * D_MODEL), MXU_DTYPE), pltpu.SemaphoreType.DMA((2,))],
        compiler_params=_params(dimension_semantics=("arbitrary",)),
    )(db_in, z, z, z, z, o, states, cum_base, lb_l, gn_l, dz)


def _pool_bwd(da_in, z, pool_w_l, pool_scale_l, dz, name, after=None):
    seq = z.shape[0]

    def body(da_ref, pv_ref, pg_ref, w_ref, sc_ref, _, dz_hbm, dw_ref, dsc_ref, stage_pv, stage_pg, sems_pv, sems_pg):
        g = pl.program_id(0)

        def where_pv(t):
            return pl.ds(0, seq), pl.ds(pl.multiple_of(t * GROUP_DIM, GROUP_DIM), GROUP_DIM)

        def where_pg(t):
            return pl.ds(0, seq), pl.ds(pl.multiple_of(POOL_WIDTH + t * GROUP_DIM, GROUP_DIM), GROUP_DIM)

        dpv_ref = stage_pv.at[_stage_begin(stage_pv, sems_pv, dz_hbm, g, where_pv)]
        dpg_ref = stage_pg.at[_stage_begin(stage_pg, sems_pg, dz_hbm, g, where_pg)]
        pos = lax.broadcasted_iota(jnp.int32, (seq, GROUP_DIM), 0)
        pm, count = _pool_mean_minus_token(pv_ref[...], g, pos)
        lin0 = _dot(pm, w_ref[...])
        pg = pg_ref[...]
        spg = _sigmoid(pg)
        da = da_ref[...].astype(F32)
        dlin = da * (pg * spg)
        dpg_ref[...] = (da * (lin0 * sc_ref[...]) * _dsilu(pg, spg)).astype(dpg_ref.dtype)
        dsc_ref[...] = jnp.sum(dlin * lin0, axis=0, keepdims=True)
        dl0 = dlin * sc_ref[...]
        dw_ref[...] = _dot_tn(pm, dl0)
        dpm = _dot_nt(dl0, w_ref[...])
        sums, acc = [], dpm / count
        for j in (1, 2, 4, 8):
            acc = acc + _shift_up(acc, j, pos, seq)
            sums.append(acc)
        dpv_ref[...] = (_select_window(g, sums) - dpm).astype(dpv_ref.dtype)
        _stage_end(stage_pv, sems_pv, dz_hbm, g, POOL_GROUPS, where_pv)
        _stage_end(stage_pg, sems_pg, dz_hbm, g, POOL_GROUPS, where_pg)

    grp = pl.BlockSpec((seq, GROUP_DIM), lambda g: (0, g))
    hbm = pl.BlockSpec(memory_space=pl.ANY)
    stage = pltpu.VMEM((2, seq, GROUP_DIM), MXU_DTYPE)
    return _pallas_after(
        body, 6, after, name=name, grid=(POOL_GROUPS,),
        in_specs=[grp, grp, pl.BlockSpec((seq, GROUP_DIM), lambda g: (0, POOL_GROUPS + g)),
                  pl.BlockSpec((None, GROUP_DIM, GROUP_DIM), lambda g: (g, 0, 0)),
                  pl.BlockSpec((1, GROUP_DIM), lambda g: (0, g)), hbm],
        out_specs=[hbm, pl.BlockSpec((None, GROUP_DIM, GROUP_DIM), lambda g: (g, 0, 0)),
                   pl.BlockSpec((1, GROUP_DIM), lambda g: (0, g))],
        out_shape=[jax.ShapeDtypeStruct(dz.shape, dz.dtype),
                   jax.ShapeDtypeStruct((POOL_GROUPS, GROUP_DIM, GROUP_DIM), F32),
                   jax.ShapeDtypeStruct((1, POOL_WIDTH), F32)],
        input_output_aliases={5: 0},
        scratch_shapes=[stage, stage, pltpu.SemaphoreType.DMA((2,)), pltpu.SemaphoreType.DMA((2,))],
        compiler_params=_params(dimension_semantics=("arbitrary",)),
    )(da_in, z, z, pool_w_l, pool_scale_l, dz)


def _in_proj_dw(h, dz, name, after=None):
    seq = h.shape[0]

    def body(h_ref, dz_ref, out_ref):
        pair = lax.dot_general(h_ref[...], dz_ref[...], (((0,), (0,)), ((), ())), preferred_element_type=F32)
        out_ref[0] = pair[:, 0:IN_COLS].astype(out_ref.dtype)
        out_ref[1] = pair[:, IN_COLS:].astype(out_ref.dtype)

    return _pallas_after(
        body, 2, after, name=name, grid=(N_DEV // 2,),
        in_specs=[pl.BlockSpec((seq, D_MODEL), lambda j: (0, 0)),
                  pl.BlockSpec((seq, 2 * IN_COLS), lambda j: (0, j))],
        out_specs=pl.BlockSpec((2, D_MODEL, IN_COLS), lambda j: (j, 0, 0)),
        out_shape=jax.ShapeDtypeStruct((N_DEV, D_MODEL, IN_COLS), WIRE_DTYPE),
        compiler_params=_params(dimension_semantics=("parallel",)),
    )(h, dz)


def _in_proj_dh(dz, win_g, tm, name, after=None):
    seq = dz.shape[0]

    def body(dz_ref, w_ref, dh_ref):
        @pl.when(pl.program_id(1) == 0)
        def _():
            dh_ref[...] = jnp.zeros_like(dh_ref)

        w_pair = jnp.concatenate([w_ref[0], w_ref[1]], axis=1)
        dh_ref[...] += lax.dot_general(dz_ref[...], w_pair, (((1,), (1,)), ((), ())), preferred_element_type=F32)

    return _pallas_after(
        body, 2, after, name=name, grid=(seq // tm, N_DEV // 2),
        in_specs=[pl.BlockSpec((tm, 2 * IN_COLS), lambda i, j: (i, j)),
                  pl.BlockSpec((2, D_MODEL, IN_COLS), lambda i, j: (j, 0, 0))],
        out_specs=pl.BlockSpec((tm, D_MODEL), lambda i, j: (i, 0)),
        out_shape=jax.ShapeDtypeStruct((seq, D_MODEL), F32),
        compiler_params=_params(dimension_semantics=("parallel", "arbitrary")),
    )(dz, win_g)


def _prenorm_bwd(x, dh, dx_res, g, scale, tm, name, after=None):
    seq = x.shape[0]

    def body(x_ref, dh_ref, dxr_ref, g_ref, sc_ref, dx_ref, acc_ref):
        @pl.when(pl.program_id(0) == 0)
        def _():
            acc_ref[...] = jnp.zeros_like(acc_ref)

        rs, xn = _rms_parts(x_ref[...])
        dh = dh_ref[...]
        acc_ref[0:1, :] += jnp.sum(dh, axis=0, keepdims=True)
        acc_ref[1:2, :] += jnp.sum(dh * (xn * g_ref[...]), axis=0, keepdims=True)
        dhn = dh * (1.0 + sc_ref[...])
        acc_ref[2:3, :] += jnp.sum(dhn * xn, axis=0, keepdims=True)
        dxn = dhn * g_ref[...]
        dx_ref[...] = rs * (dxn - xn * jnp.mean(dxn * xn, axis=-1, keepdims=True)) + dxr_ref[...]

    tile = pl.BlockSpec((tm, D_MODEL), lambda i: (i, 0))
    return _pallas_after(
        body, 5, after, name=name, grid=(seq // tm,),
        in_specs=[tile, tile, tile, _row_spec(), _row_spec()],
        out_specs=[tile, pl.BlockSpec((8, D_MODEL), lambda i: (0, 0))],
        out_shape=[jax.ShapeDtypeStruct((seq, D_MODEL), F32), jax.ShapeDtypeStruct((8, D_MODEL), F32)],
        compiler_params=_params(dimension_semantics=("arbitrary",)),
    )(x, dh, dx_res, g, scale)


def _adamw_math(w, g, m, v):
    m = ADAM_B1 * m + (1.0 - ADAM_B1) * g
    v = ADAM_B2 * v + (1.0 - ADAM_B2) * (g * g)
    m_hat = m / (1.0 - ADAM_B1 ** ADAM_STEP)
    v_hat = v / (1.0 - ADAM_B2 ** ADAM_STEP)
    delta = -ADAM_LR * (m_hat / (jnp.sqrt(v_hat) + ADAM_EPS) + ADAM_WD * w)
    return delta, m, v


def _adamw_layer(w, m, v, contribs, l, tr, name, prev=None):
    _, rows, cols = w.shape
    n = len(contribs)

    def body(*refs):
        w_ref, m_ref, v_ref = refs[:3]
        c_refs = refs[3:3 + n]
        g_ref, d_ref, mo_ref, vo_ref = refs[-4:]
        g = c_refs[0][...].astype(F32)
        for c_ref in c_refs[1:]:
            g += c_ref[...].astype(F32)
        delta, mn, vn = _adamw_math(w_ref[...], g, m_ref[...], v_ref[...])
        g_ref[...] = g
        d_ref[...] = delta
        mo_ref[...] = mn
        vo_ref[...] = vn

    tile = pl.BlockSpec((None, tr, cols), lambda i: (l, i, 0))
    in_specs = [tile, tile, tile] + [pl.BlockSpec((None, tr, cols), lambda i, s=slot: (s, i, 0)) for _, slot in contribs]
    operands = [w, m, v] + [arr for arr, _ in contribs]
    aliases = {}
    if prev is not None:
        aliases = {len(operands) + k: k for k in range(4)}
        in_specs += [pl.BlockSpec(memory_space=pl.ANY)] * 4
        operands += list(prev)
    shape = jax.ShapeDtypeStruct(w.shape, F32)
    return pl.pallas_call(
        body, name=name, grid=(rows // tr,), in_specs=in_specs, out_specs=[tile] * 4, out_shape=[shape] * 4,
        input_output_aliases=aliases,
        compiler_params=_params(dimension_semantics=("parallel",)),
    )(*operands)


def _adamw_small(w_pack, m_pack, v_pack, g_late, g_early, shapes):
    pieces, r = {}, 0
    for name, _, n in _SMALL_ROWS:
        pieces.setdefault(name, []).append((r, n))
        r += n
    names = list(pieces)

    def body(w_ref, m_ref, v_ref, gl_ref, ge_ref, *rest):
        outs, packs = rest[:4 * len(names)], rest[4 * len(names):]
        g_l, g_e = gl_ref[0][0:SMALL_LATE_ROWS], ge_ref[0]
        for d in range(1, N_DEV):
            g_l += gl_ref[d][0:SMALL_LATE_ROWS]
            g_e += ge_ref[d]
        g = jnp.concatenate([g_l, g_e], axis=0)
        w = w_ref[...]
        r0, r1, r2 = LB_ROW0, LB_ROW0 + 8, LB_ROW0 + 16
        lg0, lg1 = w[r0:r1], w[r1:r2]
        mx = jnp.maximum(lg0, lg1)
        e0, e1 = jnp.exp(lg0 - mx), jnp.exp(lg1 - mx)
        p0, p1 = e0 / (e0 + e1), e1 / (e0 + e1)
        low = ((p0 - p0), (p0 + p1) - p0)
        dlow = [g_rows * jnp.where((lo > 0.0) & (lo < 1.0), 1.0, jnp.where((lo == 0.0) | (lo == 1.0), 0.5, 0.0))
                for g_rows, lo in ((g[r0:r1], low[0]), (g[r1:r2], low[1]))]
        dp0 = (dlow[0] + dlow[1]) - (dlow[0] + dlow[1])
        dp1 = dlow[1]
        inner = p0 * dp0 + p1 * dp1
        g = jnp.concatenate([g[:r0], p0 * (dp0 - inner), p1 * (dp1 - inner), g[r2:]], axis=0)
        delta, mn, vn = _adamw_math(w, g, m_ref[...], v_ref[...])
        for kind, val in enumerate((g, delta, mn, vn)):
            packs[kind][...] = val
            for j, name in enumerate(names):
                out, at = outs[kind * len(names) + j], 0
                for start, n in pieces[name]:
                    if name in flat:
                        for r in range(n):
                            layer, c = divmod(at + r, flat[name])
                            out[layer:layer + 1, c * 128:(c + 1) * 128] = packs[kind][start + r:start + r + 1, :]
                    else:
                        out[at:at + n, :] = packs[kind][start:start + n, :]
                    at += n

    rows = {name: sum(n for _, n in pieces[name]) for name in names}
    flat = {name: rows[name] // DEPTH for name in names if len(shapes[name]) == 2}
    outs = pl.pallas_call(
        body, name="adamw_small",
        out_shape=[jax.ShapeDtypeStruct(shapes[name] if name in flat else (rows[name], 128), F32)
                   for _ in range(4) for name in names],
        scratch_shapes=[pltpu.VMEM(w_pack.shape, F32)] * 4, compiler_params=_params(),
    )(w_pack, m_pack, v_pack, g_late, g_early)
    return [{name: outs[kind * len(names) + j].reshape(shapes[name]) for j, name in enumerate(names)}
            for kind in range(4)]


def _pack_small(parts, first=0, last=len(_SMALL_ROWS)):
    rows = [(parts[name] if l is None else parts[name][l]).reshape(n, 128) for name, l, n in _SMALL_ROWS[first:last]]
    if last == len(_SMALL_ROWS):
        rows.append(jnp.zeros((SMALL_ROWS_PAD - sum(n for _, _, n in _SMALL_ROWS), 128), F32))
    return jnp.concatenate(rows, axis=0)


def kernel(x, c, w_ada, b_ada, g_pre, g_post, w_in, pool_w, pool_scale, lb_logits, hgrn_norm_g, w_pool_o, w_hgrn_o, w_out, loss_target, m_w_ada, m_b_ada, m_g_pre, m_g_post, m_w_in, m_pool_w, m_pool_scale, m_lb_logits, m_hgrn_norm_g, m_w_pool_o, m_w_hgrn_o, m_w_out, v_w_ada, v_b_ada, v_g_pre, v_g_post, v_w_in, v_pool_w, v_pool_scale, v_lb_logits, v_hgrn_norm_g, v_w_pool_o, v_w_hgrn_o, v_w_out):
    seq = x.shape[1]
    tm = min(1024, seq)
    tm_merge = min(512, seq)
    pos = _my_position()
    me = pos[3]

    big = dict(win=w_in, wpo=w_pool_o, who=w_hgrn_o, wout=w_out)
    units = [["win0"], ["wpo0", "who0", "wout0"], ["win1", "wpo1", "who1", "wout1"]]
    g_streams = [_gather_streams(keys) for keys in units]
    g_state = [None] * len(units)

    def gather_start(us, after, first=None):
        bufs = dict(first[0]) if first else {}
        for k in [k for u in us for k in units[u]]:
            arr = big[k[:-1]]
            bufs["g_" + k] = _with_own_slot(arr[int(k[-1])].astype(WIRE_DTYPE), me)
        streams = ([first[1]] if first else []) + [s for u in us for s in g_streams[u][:2]]
        bufs, sems, token = _comm_call("gather_start_" + "_".join(map(str, us)), bufs, start=streams, after=after)
        if first:
            first_out, sems = ({k: bufs[k] for k in first[0]}, sems[0]), sems[1:]
        for n, u in enumerate(us):
            g_state[u] = dict(bufs={"g_" + k: bufs["g_" + k] for k in units[u]},
                              sems=sems[2 * n:2 * n + 2])
        return (token, first_out) if first else token

    def gather_pass(u, after):
        st = g_state[u]
        to_chips, _, pass_on = g_streams[u]
        st["bufs"], (st["pass_sems"],), token = _comm_call(f"gather_pass_{u}", st["bufs"], start=[pass_on],
                                                           wait=[(to_chips, st["sems"][0])], after=after)
        return token

    def gather_done(u, after=None):
        st = g_state[u]
        _, to_sibling, pass_on = g_streams[u]
        bufs, _, _ = _comm_call(f"gather_done_{u}", st["bufs"], after=after,
                                wait=[(to_sibling, st["sems"][1]), (pass_on, st["pass_sems"])])
        return {k: bufs["g_" + k] for k in units[u]}

    c_stream = _direct_gather_stream("c")
    token, (c_bufs, c_sems) = gather_start([0], None, first=(dict(s_c=c, g_c=_with_own_slot(c, me)), c_stream))
    c_bufs, _, _ = _comm_call("gather_c_done", c_bufs, wait=[(c_stream, c_sems)])
    c_all = c_bufs["g_c"].reshape(N_DEV, D_MODEL)
    b_cols = lax.dynamic_slice_in_dim(b_ada, me * ADA_COLS, ADA_COLS, axis=1)
    ada_part = _ada_fwd(c_all, w_ada, b_cols)
    gather_pass(0, ada_part)
    ada_all = _allgather_small(ada_part.reshape(DEPTH * N_DEV, ADA_COLS), "allgather_ada",
                               after=g_state[0]["bufs"]["g_win0"])
    ada = lax.dynamic_index_in_dim(ada_all.reshape(N_DEV, DEPTH, N_DEV, ADA_COLS), me, axis=2, keepdims=False)
    ada = jnp.transpose(ada, (1, 0, 2)).reshape(DEPTH, 3 * D_MODEL)
    shift = [ada[l:l + 1, 0:D_MODEL] for l in range(DEPTH)]
    scale = [ada[l:l + 1, D_MODEL:2 * D_MODEL] for l in range(DEPTH)]
    gate = [ada[l:l + 1, 2 * D_MODEL:] for l in range(DEPTH)]

    lb = _lb_fwd(lb_logits)

    gw = {}
    xs, saved = [x[0]], []
    for l in range(DEPTH):
        h = _prenorm_fwd(xs[l], g_pre[l:l + 1], shift[l], scale[l], tm, f"prenorm_fwd_{l}",
                         after=token if l == 0 else None)
        token = None
        if l == 0:
            gw.update(gather_done(0, h))
            token = gather_start([1, 2], gw["win0"])
        else:
            gw.update(gather_done(2, h))
        z = _in_proj(h, gw[f"win{l}"], min(1024, seq), f"in_proj_{l}", after=token)
        o, b_in, states, cum_base = _hgrn_fwd(z, lb[l:l + 1], hgrn_norm_g[l:l + 1], f"hgrn_fwd_{l}")
        token = gather_pass(1, b_in) if l == 0 else None
        a_in = _pool_fwd(z, pool_w[l], pool_scale[l:l + 1], f"pool_fwd_{l}", after=token)
        if l == 0:
            gw.update(gather_done(1, a_in))
        who_l = gw[f"who{l}"].reshape(D_MODEL, D_MODEL)
        wout_l = gw[f"wout{l}"].reshape(D_MODEL, D_MODEL)
        last = l == DEPTH - 1
        ba, bb, merged, y, *out = _merge_fwd(a_in, b_in, z, xs[l], gw[f"wpo{l}"], who_l, wout_l, gate[l],
                                             g_post[l:l + 1], tm_merge, f"merge_fwd_{l}",
                                             target=loss_target[0] if last else None)
        if last:
            dx, loss_part = out
        else:
            xs.append(out[0])
            gather_pass(2, out[0])
        saved.append((h, z, a_in, o, b_in, states, cum_base, ba, bb, merged, y, who_l, wout_l))


    chips = _other_chips(pos)
    pair_idx = jnp.stack([_dev_index(cx, cy, pos[2]) for cx, cy in chips] + [me]).astype(jnp.int32)
    pair_rows = dict(win=512, wpo=POOL_WIDTH, who=HEAD_DIM, wout=HEAD_DIM)

    def scatter_pair_start(u, grads):
        keys = list(grads)
        pair, to_chips = _scatter_streams(keys)
        bufs = {}
        for k in keys:
            bufs["g_" + k] = grads[k]
            bufs["st_" + k] = lax.empty((4,) + grads[k].shape[1:], WIRE_DTYPE)
        bufs, (sems,), token = _comm_call(f"scatter_pair_start_{u}", bufs, start=[pair])
        return dict(u=u, keys=keys, pair=pair, to_chips=to_chips, bufs=bufs, sems=sems, token=token)

    def scatter_pair_finish(st, after):
        u, keys = st["u"], st["keys"]
        bufs, _, _ = _comm_call(f"scatter_pair_done_{u}", st["bufs"], wait=[(st["pair"], st["sems"])], after=after)
        bufs2 = {}
        for k in keys:
            bufs2["ps_" + k] = _pair_sum(bufs["g_" + k], bufs["st_" + k], pair_idx, bufs["g_" + k].shape[1],
                                         f"pair_sum_{k}")
            bufs2["ld_" + k] = lax.empty((3,) + bufs["g_" + k].shape[1:], WIRE_DTYPE)
        st.update(bufs=bufs2)

    def scatter_chips_start(st, after=None):
        bufs2, (sems,), token = _comm_call(f"scatter_chips_start_{st['u']}", st["bufs"], start=[st["to_chips"]],
                                           after=after)
        st.update(bufs=bufs2, sems=sems, token=token)

    def scatter_finish(st, after):
        bufs, _, _ = _comm_call(f"scatter_chips_done_{st['u']}", st["bufs"], wait=[(st["to_chips"], st["sems"])],
                                after=after)
        return {k: [(bufs["ps_" + k], 3), (bufs["ld_" + k], 0), (bufs["ld_" + k], 1), (bufs["ld_" + k], 2)]
                for k in st["keys"]}

    moments = dict(win=(m_w_in, v_w_in), wpo=(m_w_pool_o, v_w_pool_o), who=(m_w_hgrn_o, v_w_hgrn_o),
                   wout=(m_w_out, v_w_out))
    big_out = {}

    def finish_unit(unit, after):
        for k, contribs in scatter_finish(scat[unit], after).items():
            wname, l = k[:-1], int(k[-1])
            big_out[wname] = _adamw_layer(big[wname], moments[wname][0], moments[wname][1], contribs, l,
                                          pair_rows[wname], f"adamw_{k}", prev=big_out.get(wname))
            after = big_out[wname][0]
        return after

    d_ada, small, scat = [None] * DEPTH, [None] * DEPTH, {}
    for l in reversed(range(DEPTH)):
        h, z, a_in, o, b_in, states, cum_base, ba, bb, merged, y, who_l, wout_l = saved[l]
        dy, dba, dbb, da_in, db_in, dz, acc_post = _merge_bwd(
            dx, y, ba, bb, z, gw[f"wpo{l}"], who_l, wout_l, gate[l], g_post[l:l + 1],
            lax.empty((seq, IN_WIDTH), MXU_DTYPE), tm_merge, f"merge_bwd_{l}")
        g_out, g_ho, g_po = _grad_out_weights(merged, dy, b_in, dbb, a_in, dba, f"grad_out_weights_{l}")
        g_small = {f"wout{l}": g_out.reshape(N_DEV, HEAD_DIM, D_MODEL),
                   f"who{l}": g_ho.reshape(N_DEV, HEAD_DIM, D_MODEL), f"wpo{l}": g_po}
        st_small = scat["small0"] = scatter_pair_start("small0", g_small) if l == 0 else None
        dz, dlb, dgn = _hgrn_bwd(db_in, z, o, states, cum_base, lb[l:l + 1], hgrn_norm_g[l:l + 1], dz, f"hgrn_bwd_{l}",
                                 after=st_small and st_small["token"])
        if l == 0:
            scatter_pair_finish(st_small, dlb)
            scatter_chips_start(st_small)
        dz, dpw, dps = _pool_bwd(da_in, z, pool_w[l], pool_scale[l:l + 1], dz, f"pool_bwd_{l}",
                                 after=st_small and st_small["token"])
        small[l] = dict(g_post=acc_post[1], pool_w=dpw, pool_scale=dps[0], lb_logits=dlb[0], hgrn_norm_g=dgn[0])
        token = None
        if l == 0:
            parts = {name: jnp.stack([small[0][name], small[1][name]]) for name in small[0]}
            parts.update(b_ada=[None, d_ada[1]], g_pre=[None, small[1]["g_pre"]])
            sg_stream = _direct_gather_stream("sg")
            early = _pack_small(parts, 2)
            sg_bufs, (sg_sems,), token = _comm_call(
                "small_grads_start", dict(s_sg=early, g_sg=_with_own_slot(early, me)), start=[sg_stream])
        g_win = {f"win{l}": _in_proj_dw(h, dz, f"grad_w_in_{l}", after=token)}
        st_win = scat[f"win{l}"] = scatter_pair_start(f"win{l}", g_win if l == 0 else {**g_small, **g_win})
        if l > 0:
            dh = _in_proj_dh(dz, gw[f"win{l}"], seq, f"in_proj_dh_{l}", after=st_win["token"])
            scatter_pair_finish(st_win, dh)
            scatter_chips_start(st_win)
        else:
            after = finish_unit("win1", st_win["token"])
            scatter_pair_finish(st_win, after)
            scatter_chips_start(st_win)
            after = finish_unit("small0", st_win["token"])
            dh = _in_proj_dh(dz, gw[f"win{l}"], seq, f"in_proj_dh_{l}", after=after)
        dx, acc_pre = _prenorm_bwd(xs[l], dh, dx, g_pre[l:l + 1], scale[l], tm, f"prenorm_bwd_{l}",
                                   after=st_win["token"])
        d_ada[l] = jnp.concatenate([acc_pre[0], acc_pre[1], acc_post[0]])
        small[l]["g_pre"] = acc_pre[2]
    grad_x = dx[None]

    sg_bufs, _, _ = _comm_call("small_grads_done", sg_bufs, wait=[(sg_stream, sg_sems)], after=dx)
    g_early = sg_bufs["g_sg"]

    def w_ada_layer(l, d_rows, prev):
        d_cols = lax.dynamic_slice_in_dim(d_rows.reshape(N_DEV, 3 * D_MODEL), me * ADA_COLS, ADA_COLS, axis=1)
        return _adamw_layer(w_ada, m_w_ada, v_w_ada, [(_ada_bwd(c_all, d_cols, f"ada_bwd_{l}"), 0)], l, 512,
                            f"adamw_w_ada{l}", prev=prev)

    ada_out = w_ada_layer(1, g_early[:, 0:24, :], None)

    parts = dict(b_ada=[d_ada[0]], g_pre=[small[0]["g_pre"]])
    late = jnp.concatenate([_pack_small(parts, 0, 2), jnp.broadcast_to(loss_part, (8, 128))], axis=0)
    g_late = _allgather_small(late, "allgather_late_grads", after=ada_out[0])
    loss = jnp.sum(g_late[:, SMALL_LATE_ROWS, 0])
    small_names = list(dict.fromkeys(name for name, _, _ in _SMALL_ROWS))
    weights = dict(b_ada=b_ada, g_pre=g_pre, g_post=g_post, pool_w=pool_w, pool_scale=pool_scale,
                   lb_logits=lb_logits, hgrn_norm_g=hgrn_norm_g)
    m_small = dict(b_ada=m_b_ada, g_pre=m_g_pre, g_post=m_g_post, pool_w=m_pool_w, pool_scale=m_pool_scale,
                   lb_logits=m_lb_logits, hgrn_norm_g=m_hgrn_norm_g)
    v_small = dict(b_ada=v_b_ada, g_pre=v_g_pre, g_post=v_g_post, pool_w=v_pool_w, pool_scale=v_pool_scale,
                   lb_logits=v_lb_logits, hgrn_norm_g=v_hgrn_norm_g)
    shapes = {name: weights[name].shape for name in small_names}
    small_out = _adamw_small(_pack_small(weights), _pack_small(m_small), _pack_small(v_small), g_late, g_early,
                             shapes)

    ada_out = w_ada_layer(0, g_late[:, 0:24, :], ada_out)
    finish_unit("win0", ada_out[1][0, 0:8, 0:128] + small_out[1]["pool_scale"][0:1, 0:128])

    def leaf(kind):
        s = small_out[kind]
        return (ada_out[kind], s["b_ada"], s["g_pre"], s["g_post"], big_out["win"][kind], s["pool_w"], s["pool_scale"],
                s["lb_logits"], s["hgrn_norm_g"], big_out["wpo"][kind], big_out["who"][kind], big_out["wout"][kind])

    return (loss, grad_x) + leaf(0) + leaf(1) + leaf(2) + leaf(3)
```

```python
import jax
import jax.numpy as jnp
from jax import lax
from jax.experimental import pallas as pl
from jax.experimental.pallas import tpu as pltpu

F32 = jnp.float32
MXU_DTYPE = jnp.bfloat16
WIRE_DTYPE = jnp.bfloat16

N_DEV = 8
DEPTH = 2
D_MODEL = 1024
HEADS = 8
HEAD_DIM = 128
POOL_GROUPS = 4
GROUP_DIM = 128
POOL_WIDTH = POOL_GROUPS * GROUP_DIM
IN_WIDTH = 7168
CHUNK = 64
SUB = 16
N_SUB = CHUNK // SUB
FWD_STEP_CHUNKS = 8
BWD_STEP_CHUNKS = 4
EXP_CLAMP = 80.0
NORM_EPS = 1e-6
LOG_FLOOR = 1e-30
ADA_COLS = 3 * D_MODEL // N_DEV
IN_COLS = IN_WIDTH // N_DEV
COL_HQ, COL_HF, COL_HI, COL_HG, COL_MGP, COL_MGH = 1, 2, 3, 4, 5, 6

ADAM_LR = 0.001
ADAM_B1 = 0.9
ADAM_B2 = 0.999
ADAM_EPS = 1e-08
ADAM_WD = 0.01
ADAM_STEP = 10

VMEM_LIMIT = 48 * 1024 * 1024
MESH_ID = pl.DeviceIdType.MESH
HIGHEST = lax.Precision.HIGHEST

_SMALL_ROWS = (("b_ada", 0, 24), ("g_pre", 0, 8), ("b_ada", 1, 24), ("g_pre", 1, 8), ("g_post", None, 16),
               ("pool_w", None, 1024), ("pool_scale", None, 8), ("lb_logits", None, 16), ("hgrn_norm_g", None, 2))
SMALL_LATE_ROWS = 32
SMALL_ROWS_PAD = 1136
LB_ROW0 = 32 + 32 + 16 + 1024 + 8


def _params(**kw):
    return pltpu.CompilerParams(vmem_limit_bytes=VMEM_LIMIT, **kw)


def _sigmoid(v):
    return 1.0 / (1.0 + jnp.exp(-v))


def _dsilu(v, s):
    return s * (1.0 + v * (1.0 - s))


def _dot(a, b):
    return jnp.dot(a.astype(MXU_DTYPE), b.astype(MXU_DTYPE), preferred_element_type=F32)


def _dot_nt(a, b):
    return lax.dot_general(a.astype(MXU_DTYPE), b.astype(MXU_DTYPE), (((1,), (1,)), ((), ())),
                           preferred_element_type=F32)


def _dot_tn(a, b):
    return lax.dot_general(a.astype(MXU_DTYPE), b.astype(MXU_DTYPE), (((0,), (0,)), ((), ())),
                           preferred_element_type=F32)


def _pallas_after(body, n_in, after, *, in_specs, **kw):
    if after is None:
        return pl.pallas_call(body, in_specs=in_specs, **kw)

    def tied(*refs):
        body(*refs[:n_in], *refs[n_in + 1:])

    call = pl.pallas_call(tied, in_specs=list(in_specs) + [pl.BlockSpec(memory_space=pl.ANY)], **kw)
    return lambda *operands: call(*operands, after)


def _my_position():
    mx, my, mc = lax.axis_index("x"), lax.axis_index("y"), lax.axis_index("c")
    return mx, my, mc, 4 * mx + 2 * my + mc


def _peer(mx, my, mc, k):
    px = 1 - mx if (k >> 2) & 1 else mx
    py = 1 - my if (k >> 1) & 1 else my
    pc = 1 - mc if k & 1 else mc
    return (px, py, pc), 4 * px + 2 * py + pc


def _allgather_small(v, name, after=None):
    rows, cols = v.shape

    def body(v_ref, out_ref, send_sems, recv_sems):
        mx, my, mc, me = _my_position()
        out_ref[me] = v_ref[...]
        copies = []
        for k in range(1, N_DEV):
            peer, _ = _peer(mx, my, mc, k)
            cp = pltpu.make_async_remote_copy(
                src_ref=v_ref, dst_ref=out_ref.at[me],
                send_sem=send_sems.at[k - 1], recv_sem=recv_sems.at[k - 1],
                device_id=peer, device_id_type=MESH_ID)
            cp.start()
            copies.append(cp)
        for cp in copies:
            cp.wait()

    return _pallas_after(
        body, 1, after, name=name,
        out_shape=jax.ShapeDtypeStruct((N_DEV, rows, cols), v.dtype),
        in_specs=[pl.BlockSpec(memory_space=pltpu.VMEM)],
        out_specs=pl.BlockSpec(memory_space=pltpu.VMEM),
        scratch_shapes=[pltpu.SemaphoreType.DMA((N_DEV - 1,)), pltpu.SemaphoreType.DMA((N_DEV - 1,))],
        compiler_params=_params(),
    )(v)


class _Stream:
    def __init__(self, n, plan):
        self.n, self.plan = n, plan


def _comm_call(name, bufs, start=(), wait=(), after=None):
    names = list(bufs)

    def body(*refs):
        it = iter(refs)
        buf_refs = {n: next(it) for n in names}
        wait_sems = [(next(it), next(it)) for _ in wait]
        if after is not None:
            next(it)
        start_sems = [(next(it), next(it)) for _ in start]
        for _ in names:
            next(it)
        token = next(it)
        pos = _my_position()

        def descriptors(stream, sems):
            return [pltpu.make_async_remote_copy(src_ref=src, dst_ref=dst, send_sem=sems[0].at[k], recv_sem=sems[1].at[k],
                                                 device_id=dev, device_id_type=MESH_ID)
                    for k, (src, dst, dev) in enumerate(stream.plan(buf_refs, pos))]

        for (stream, _), sems in zip(wait, wait_sems):
            for cp in descriptors(stream, sems):
                cp.wait_send()
                cp.wait_recv()
        for stream, sems in zip(start, start_sems):
            for cp in descriptors(stream, sems):
                cp.start()
        token[...] = jnp.zeros_like(token)

    hbm = pl.BlockSpec(memory_space=pltpu.HBM)
    sem = pl.BlockSpec(memory_space=pltpu.SEMAPHORE)
    operands = [pltpu.with_memory_space_constraint(bufs[n], pltpu.HBM) for n in names]
    in_specs = [hbm] * len(names)
    for _, (send_sems, recv_sems) in wait:
        operands += [send_sems, recv_sems]
        in_specs += [sem, sem]
    if after is not None:
        operands.append(after)
        in_specs.append(pl.BlockSpec(memory_space=pl.ANY))
    out_shape, out_specs = [], []
    for stream in start:
        out_shape += [pltpu.SemaphoreType.DMA((stream.n,)), pltpu.SemaphoreType.DMA((stream.n,))]
        out_specs += [sem, sem]
    n_sem_out = len(out_shape)
    out_shape += [pltpu.HBM(bufs[n].shape, bufs[n].dtype) for n in names]
    out_specs += [hbm] * len(names)
    out_shape.append(jax.ShapeDtypeStruct((8, 128), F32))
    out_specs.append(pl.BlockSpec(memory_space=pltpu.VMEM))
    outs = pl.pallas_call(
        body, name=name, out_shape=out_shape, in_specs=in_specs, out_specs=out_specs,
        input_output_aliases={i: n_sem_out + i for i in range(len(names))},
        compiler_params=pltpu.CompilerParams(has_side_effects=pltpu.SideEffectType.DATAFLOW_SIDE_EFFECTING),
    )(*operands)
    sems = [(outs[2 * i], outs[2 * i + 1]) for i in range(len(start))]
    return dict(zip(names, outs[n_sem_out:n_sem_out + len(names)])), sems, outs[-1]


def _with_own_slot(block, me):
    return lax.dynamic_update_index_in_dim(lax.empty((N_DEV,) + block.shape, block.dtype), block, me, 0)


def _other_chips(pos):
    mx, my, _, _ = pos
    return [(1 - mx if i & 2 else mx, 1 - my if i & 1 else my) for i in (1, 2, 3)]


def _dev_index(px, py, pc):
    return 4 * px + 2 * py + pc


def _gather_streams(keys):
    def to_chips(refs, pos):
        _, _, mc, me = pos
        return [(refs["g_" + k].at[me], refs["g_" + k].at[me], (cx, cy, mc))
                for k in keys for cx, cy in _other_chips(pos)]

    def to_sibling(refs, pos):
        mx, my, mc, me = pos
        return [(refs["g_" + k].at[me], refs["g_" + k].at[me], (mx, my, 1 - mc)) for k in keys]

    def pass_on(refs, pos):
        mx, my, mc, _ = pos
        out = []
        for k in keys:
            for cx, cy in _other_chips(pos):
                slot = refs["g_" + k].at[_dev_index(cx, cy, mc)]
                out.append((slot, slot, (mx, my, 1 - mc)))
        return out

    return _Stream(3 * len(keys), to_chips), _Stream(len(keys), to_sibling), _Stream(3 * len(keys), pass_on)


def _direct_gather_stream(key):
    def plan(refs, pos):
        mx, my, mc, me = pos
        return [(refs["s_" + key], refs["g_" + key].at[me], _peer(mx, my, mc, k)[0]) for k in range(1, N_DEV)]

    return _Stream(N_DEV - 1, plan)


def _scatter_streams(keys):
    def pair(refs, pos):
        mx, my, mc, _ = pos
        sib = (mx, my, 1 - mc)
        out = []
        for k in keys:
            for i, (cx, cy) in enumerate(_other_chips(pos)):
                out.append((refs["g_" + k].at[_dev_index(cx, cy, 1 - mc)], refs["st_" + k].at[i], sib))
            out.append((refs["g_" + k].at[_dev_index(mx, my, 1 - mc)], refs["st_" + k].at[3], sib))
        return out

    def chips(refs, pos):
        mc = pos[2]
        return [(refs["ps_" + k].at[i], refs["ld_" + k].at[i], (cx, cy, mc))
                for k in keys for i, (cx, cy) in enumerate(_other_chips(pos))]

    return _Stream(4 * len(keys), pair), _Stream(3 * len(keys), chips)


def _pair_sum(g, st, idx, tr, name):
    _, rows, cols = g.shape

    def body(idx_ref, g_ref, st_ref, out_ref):
        out_ref[...] = (g_ref[...].astype(F32) + st_ref[...].astype(F32)).astype(out_ref.dtype)

    return pl.pallas_call(
        body, name=name,
        grid_spec=pltpu.PrefetchScalarGridSpec(
            num_scalar_prefetch=1, grid=(4, rows // tr),
            in_specs=[pl.BlockSpec((None, tr, cols), lambda j, i, idx_ref: (idx_ref[j], i, 0)),
                      pl.BlockSpec((None, tr, cols), lambda j, i, idx_ref: (j, i, 0))],
            out_specs=pl.BlockSpec((None, tr, cols), lambda j, i, idx_ref: (j, i, 0))),
        out_shape=jax.ShapeDtypeStruct((4, rows, cols), WIRE_DTYPE),
        compiler_params=_params(dimension_semantics=("parallel", "parallel")),
    )(idx, g, st)


def _ada_fwd(c_all, w_ada, b_cols):
    def body(c_ref, w_ref, b_ref, out_ref):
        cv = c_ref[...]
        ca = cv * _sigmoid(cv)
        for l in range(DEPTH):
            out_ref[l] = jnp.dot(ca, w_ref[l], precision=HIGHEST, preferred_element_type=F32) + b_ref[l:l + 1, :]

    return pl.pallas_call(
        body, name="ada_fwd",
        out_shape=jax.ShapeDtypeStruct((DEPTH, N_DEV, ADA_COLS), F32),
        compiler_params=_params(),
    )(c_all, w_ada, b_cols)


def _ada_bwd(c_all, d_cols, name):
    def body(c_ref, d_ref, out_ref):
        cv = c_ref[...]
        ca = cv * _sigmoid(cv)
        out_ref[0] = lax.dot_general(ca, d_ref[...], (((0,), (0,)), ((), ())), precision=HIGHEST,
                                     preferred_element_type=F32)

    return pl.pallas_call(
        body, name=name,
        out_shape=jax.ShapeDtypeStruct((1, D_MODEL, ADA_COLS), F32),
        compiler_params=_params(),
    )(c_all, d_cols)


def _lower_bounds(logits):
    m = jnp.maximum(logits[0:1], logits[1:2])
    e0, e1 = jnp.exp(logits[0:1] - m), jnp.exp(logits[1:2] - m)
    den = e0 + e1
    p0, p1 = e0 / den, e1 / den
    low0 = p0 - p0
    low1 = (p0 + p1) - p0
    return (p0, p1), (low0, low1)


def _lb_fwd(lb_logits):
    def body(lg_ref, out_ref):
        _, (low0, low1) = _lower_bounds(lg_ref[...])
        out_ref[0:1, :] = jnp.clip(low0, 0.0, 1.0)
        out_ref[1:2, :] = jnp.clip(low1, 0.0, 1.0)

    return pl.pallas_call(body, name="lb_fwd", out_shape=jax.ShapeDtypeStruct(lb_logits.shape, F32),
                          compiler_params=_params())(lb_logits)


def _row_spec(cols=D_MODEL):
    return pl.BlockSpec((1, cols), lambda *_: (0, 0))


def _prenorm_fwd(x, g, shift, scale, tm, name, after=None):
    seq = x.shape[0]

    def body(x_ref, g_ref, sh_ref, sc_ref, h_ref):
        xv = x_ref[...]
        rs = lax.rsqrt(jnp.mean(xv * xv, axis=-1, keepdims=True) + NORM_EPS)
        h = (xv * rs * g_ref[...]) * (1.0 + sc_ref[...]) + sh_ref[...]
        h_ref[...] = h.astype(h_ref.dtype)

    tile = pl.BlockSpec((tm, D_MODEL), lambda i: (i, 0))
    return _pallas_after(
        body, 4, after, name=name, grid=(seq // tm,),
        in_specs=[tile, _row_spec(), _row_spec(), _row_spec()], out_specs=tile,
        out_shape=jax.ShapeDtypeStruct((seq, D_MODEL), MXU_DTYPE),
        compiler_params=_params(dimension_semantics=("parallel",)),
    )(x, g, shift, scale)


def _in_proj(h, win_g, tm, name, after=None):
    seq = h.shape[0]

    def body(h_ref, w_ref, z_ref, w_pair):
        @pl.when(pl.program_id(1) == 0)
        def _():
            w_pair[...] = jnp.concatenate([w_ref[0], w_ref[1]], axis=1)

        z_ref[...] = jnp.dot(h_ref[...], w_pair[...], preferred_element_type=F32)

    return _pallas_after(
        body, 2, after, name=name, grid=(N_DEV // 2, seq // tm),
        in_specs=[pl.BlockSpec((tm, D_MODEL), lambda j, i: (i, 0)),
                  pl.BlockSpec((2, D_MODEL, IN_COLS), lambda j, i: (j, 0, 0))],
        out_specs=pl.BlockSpec((tm, 2 * IN_COLS), lambda j, i: (i, j)),
        out_shape=jax.ShapeDtypeStruct((seq, IN_WIDTH), F32),
        scratch_shapes=[pltpu.VMEM((D_MODEL, 2 * IN_COLS), MXU_DTYPE)],
        compiler_params=_params(dimension_semantics=("parallel", "arbitrary")),
    )(h, win_g)


def _shift_down(v, j, pos):
    return jnp.where(pos >= j, pltpu.roll(v, j, 0), 0.0)


def _shift_up(v, j, pos, seq):
    return jnp.where(pos < seq - j, pltpu.roll(v, seq - j, 0), 0.0)


def _select_window(g, candidates):
    out = candidates[-1]
    for i in range(len(candidates) - 2, -1, -1):
        out = jnp.where(g == i, candidates[i], out)
    return out


def _pool_mean_minus_token(u, g, pos):
    sums, acc = [], u
    for j in (1, 2, 4, 8):
        acc = acc + _shift_down(acc, j, pos)
        sums.append(acc)
    wsum = _select_window(g, sums)
    width = jnp.left_shift(2, g).astype(F32)
    count = jnp.minimum(pos.astype(F32) + 1.0, width)
    return wsum / count - u, count


def _pool_mean_minus_token_static(u, i, pos):
    acc = u
    for j in (1, 2, 4, 8)[:i + 1]:
        acc = acc + _shift_down(acc, j, pos)
    count = jnp.minimum(pos.astype(F32) + 1.0, float(2 << i))
    return acc / count - u, count


def _per_group(fn):
    for i in range(POOL_GROUPS):
        pl.when(pl.program_id(0) == i)(lambda i=i: fn(i))


def _pool_fwd(z, pool_w_l, pool_scale_l, name, after=None):
    seq = z.shape[0]

    def body(pv_ref, pg_ref, w_ref, sc_ref, out_ref):
        def one_group(i):
            pos = lax.broadcasted_iota(jnp.int32, (seq, GROUP_DIM), 0)
            pm, _ = _pool_mean_minus_token_static(pv_ref[...], i, pos)
            lin = _dot(pm, w_ref[...]) * sc_ref[...]
            pg = pg_ref[...]
            out_ref[...] = (lin * (pg * _sigmoid(pg))).astype(out_ref.dtype)

        _per_group(one_group)

    return _pallas_after(
        body, 4, after, name=name, grid=(POOL_GROUPS,),
        in_specs=[pl.BlockSpec((seq, GROUP_DIM), lambda g: (0, g)),
                  pl.BlockSpec((seq, GROUP_DIM), lambda g: (0, POOL_GROUPS + g)),
                  pl.BlockSpec((None, GROUP_DIM, GROUP_DIM), lambda g: (g, 0, 0)),
                  pl.BlockSpec((1, GROUP_DIM), lambda g: (0, g))],
        out_specs=pl.BlockSpec((seq, GROUP_DIM), lambda g: (0, g)),
        out_shape=jax.ShapeDtypeStruct((seq, POOL_WIDTH), MXU_DTYPE),
        compiler_params=_params(dimension_semantics=("parallel",)),
    )(z, z, pool_w_l, pool_scale_l)


def _chunk_masks():
    row = lax.broadcasted_iota(jnp.int32, (CHUNK, CHUNK), 0)
    col = lax.broadcasted_iota(jnp.int32, (CHUNK, CHUNK), 1)
    causal = row >= col
    before_sub = col < (row // SUB) * SUB
    suffix = row <= col
    return causal, before_sub, suffix


def _masked_sums(masks, v):
    lhs = jnp.concatenate([m.astype(jnp.bfloat16) for m in masks], axis=0)
    hi = v.astype(jnp.bfloat16)
    rest = v - hi.astype(F32)
    mid = rest.astype(jnp.bfloat16)
    lo = (rest - mid.astype(F32)).astype(jnp.bfloat16)
    out = jnp.dot(lhs, hi, preferred_element_type=F32)
    out += jnp.dot(lhs, mid, preferred_element_type=F32)
    out += jnp.dot(lhs, lo, preferred_element_type=F32)
    return [out[i * CHUNK:(i + 1) * CHUNK] for i in range(len(masks))]


def _gates(zf, lb):
    sg = _sigmoid(zf)
    f = lb + (1.0 - lb) * sg
    logf = jnp.log(jnp.maximum(f, LOG_FLOOR))
    return sg, f, logf


def _intra_blocks(q_h, k_h, cum_h, base_h, causal):
    rel = cum_h - base_h
    out = []
    for i in range(N_SUB):
        rows = slice(i * SUB, (i + 1) * SUB)
        e_q = jnp.exp(rel[rows])
        base_i = jnp.concatenate([base_h[rows]] * N_SUB, axis=0)
        e_k = jnp.exp(jnp.minimum(base_i - cum_h, EXP_CLAMP))
        q_t = (q_h[rows] * e_q).astype(MXU_DTYPE)
        k_t = (k_h * e_k).astype(MXU_DTYPE)
        a_i = jnp.where(causal[rows], _dot_nt(q_t, k_t), 0.0)
        out.append((q_t, k_t, e_q, e_k, a_i))
    return out


def _hgrn_fwd(z, lb_l, gn_l, name, after=None):
    seq = z.shape[0]
    n_chunks = seq // CHUNK
    per_step = min(FWD_STEP_CHUNKS, n_chunks)
    rows_per_step = per_step * CHUNK

    def body(hq_ref, hf_ref, hi_ref, hg_ref, lb_ref, gn_ref, o_ref, bin_ref, st_ref, cb_ref, state):
        @pl.when(pl.program_id(0) == 0)
        def _():
            state[...] = jnp.zeros_like(state)

        causal, before_sub, _ = _chunk_masks()
        for cc in range(per_step):
            rows = slice(cc * CHUNK, (cc + 1) * CHUNK)
            _, f, logf = _gates(hf_ref[rows, :], lb_ref[...])
            kk = 1.0 - f
            hq = hq_ref[rows, :]
            q = hq * _sigmoid(hq)
            cum, base = _masked_sums([causal, before_sub], logf)
            cb_ref[rows, 0:D_MODEL] = cum
            cb_ref[rows, D_MODEL:2 * D_MODEL] = base
            st_ref[cc] = state[...]
            for h in range(HEADS):
                sl = slice(h * HEAD_DIM, (h + 1) * HEAD_DIM)
                q_h, k_h, cum_h = q[:, sl], kk[:, sl], cum[:, sl]
                v_h = hi_ref[rows, sl]
                st_h = state[h]
                blocks = _intra_blocks(q_h, k_h, cum_h, base[:, sl], causal)
                a = jnp.concatenate([b[4] for b in blocks], axis=0)
                o_h = _dot_nt(q_h * jnp.exp(cum_h), st_h) + _dot(a, v_h)
                last = jnp.sum(logf[:, sl], axis=0, keepdims=True)
                state[h] = st_h * jnp.exp(last) + _dot_tn(v_h, k_h * jnp.exp(last - cum_h))
                rs = lax.rsqrt(jnp.mean(o_h * o_h, axis=-1, keepdims=True) + NORM_EPS)
                hg = hg_ref[rows, sl]
                o_ref[rows, sl] = o_h
                bin_ref[rows, sl] = ((o_h * rs * gn_ref[...]) * (hg * _sigmoid(hg))).astype(bin_ref.dtype)

    def col(block):
        return pl.BlockSpec((rows_per_step, D_MODEL), lambda c: (c, block))

    tile = pl.BlockSpec((rows_per_step, D_MODEL), lambda c: (c, 0))
    return _pallas_after(
        body, 6, after, name=name, grid=(n_chunks // per_step,),
        in_specs=[col(COL_HQ), col(COL_HF), col(COL_HI), col(COL_HG), _row_spec(), _row_spec(HEAD_DIM)],
        out_specs=[tile, tile, pl.BlockSpec((per_step, HEADS, HEAD_DIM, HEAD_DIM), lambda c: (c, 0, 0, 0)),
                   pl.BlockSpec((rows_per_step, 2 * D_MODEL), lambda c: (c, 0))],
        out_shape=[jax.ShapeDtypeStruct((seq, D_MODEL), F32),
                   jax.ShapeDtypeStruct((seq, D_MODEL), MXU_DTYPE),
                   jax.ShapeDtypeStruct((n_chunks, HEADS, HEAD_DIM, HEAD_DIM), F32),
                   jax.ShapeDtypeStruct((seq, 2 * D_MODEL), F32)],
        scratch_shapes=[pltpu.VMEM((HEADS, HEAD_DIM, HEAD_DIM), F32)],
        compiler_params=_params(dimension_semantics=("arbitrary",)),
    )(z, z, z, z, lb_l, gn_l)


def _rms_parts(y):
    rs = lax.rsqrt(jnp.mean(y * y, axis=-1, keepdims=True) + NORM_EPS)
    return rs, y * rs


def _merge_fwd(a_in, b_in, z, x, wpo_g, who_g, wout_g, gate, g_post, tm, name, target=None):
    seq = x.shape[0]
    with_loss = target is not None

    def body(*refs):
        a_ref, b_ref, mgp_ref, mgh_ref, x_ref, wpo_ref, who_ref, wout_ref, gate_ref, gp_ref = refs[:10]
        ba_ref, bb_ref, mer_ref, y_ref, last_ref = refs[10 + with_loss:15 + with_loss]
        a = a_ref[...]
        ba = _dot(a, jnp.concatenate([wpo_ref[j] for j in range(N_DEV)], axis=1))
        bb = _dot(b_ref[...], who_ref[...])
        merged = _sigmoid(mgp_ref[...]) * ba + _sigmoid(mgh_ref[...]) * bb
        y = _dot(merged, wout_ref[...])
        _, yn = _rms_parts(y)
        ba_ref[...] = ba.astype(ba_ref.dtype)
        bb_ref[...] = bb.astype(bb_ref.dtype)
        mer_ref[...] = merged.astype(mer_ref.dtype)
        y_ref[...] = y.astype(y_ref.dtype)
        x_next = x_ref[...] + gate_ref[...] * (yn * gp_ref[...])
        if not with_loss:
            last_ref[...] = x_next
            return
        loss_ref = refs[16]

        @pl.when(pl.program_id(0) == 0)
        def _():
            loss_ref[...] = jnp.zeros_like(loss_ref)

        err = x_next - refs[10][...]
        loss_ref[...] += 0.5 * jnp.sum(jnp.mean(err * err, axis=-1, keepdims=True), axis=0, keepdims=True)
        last_ref[...] = err * (1.0 / D_MODEL)

    def tile(cols=D_MODEL, block=0):
        return pl.BlockSpec((tm, cols), lambda i: (i, block))

    full = pl.BlockSpec((D_MODEL, D_MODEL), lambda i: (0, 0))
    act = jax.ShapeDtypeStruct((seq, D_MODEL), MXU_DTYPE)
    f32 = jax.ShapeDtypeStruct((seq, D_MODEL), F32)
    one = [pl.BlockSpec((1, 1), lambda i: (0, 0))] if with_loss else []
    return pl.pallas_call(
        body, name=name, grid=(seq // tm,),
        in_specs=[tile(POOL_WIDTH), tile(), tile(block=COL_MGP), tile(block=COL_MGH), tile(),
                  pl.BlockSpec((N_DEV, POOL_WIDTH, GROUP_DIM), lambda i: (0, 0, 0)),
                  full, full, _row_spec(), _row_spec()] + ([tile()] if with_loss else []),
        out_specs=[tile(), tile(), tile(), tile(), tile()] + one,
        out_shape=[act, act, act, act, f32] + ([jax.ShapeDtypeStruct((1, 1), F32)] if with_loss else []),
        compiler_params=_params(dimension_semantics=("arbitrary" if with_loss else "parallel",)),
    )(a_in, b_in, z, z, x, wpo_g, who_g, wout_g, gate, g_post, *([target] if with_loss else []))


def _stage_copy(stage, sems, dst, slot, step, where):
    rows, cols = where(step)
    return pltpu.make_async_copy(stage.at[slot], dst.at[rows, cols], sems.at[slot])


def _stage_begin(stage, sems, dst, step, where):
    slot = step % 2

    @pl.when(step >= 2)
    def _():
        _stage_copy(stage, sems, dst, slot, step - 2, where).wait()

    return slot


def _stage_end(stage, sems, dst, step, n_steps, where):
    slot = step % 2
    _stage_copy(stage, sems, dst, slot, step, where).start()

    @pl.when(step == n_steps - 1)
    def _():
        _stage_copy(stage, sems, dst, slot, step, where).wait()
        if n_steps > 1:
            _stage_copy(stage, sems, dst, 1 - slot, step - 1, where).wait()


def _merge_bwd(dx, y, ba, bb, z, wpo_g, who_g, wout_g, gate, g_post, dz, tm, name):
    seq = dx.shape[0]
    n_steps = seq // tm

    def body(dx_ref, y_ref, ba_ref, bb_ref, mgp_ref, mgh_ref, wpo_ref, who_ref, wout_ref, gate_ref, gp_ref, _,
             dy_ref, dba_ref, dbb_ref, da_ref, db_ref, dz_ref, acc_ref, stage, sems):
        step = pl.program_id(0)

        @pl.when(step == 0)
        def _():
            acc_ref[...] = jnp.zeros_like(acc_ref)

        def where(t):
            return pl.ds(t * tm, tm), pl.ds(COL_MGP * D_MODEL, 2 * D_MODEL)

        dmg_ref = stage.at[_stage_begin(stage, sems, dz_ref, step, where)]

        dxv = dx_ref[...]
        rs, yn = _rms_parts(y_ref[...].astype(F32))
        acc_ref[0:1, :] += jnp.sum(dxv * yn * gp_ref[...], axis=0, keepdims=True)
        acc_ref[1:2, :] += jnp.sum(dxv * gate_ref[...] * yn, axis=0, keepdims=True)
        dyn = dxv * (gate_ref[...] * gp_ref[...])
        dy = rs * (dyn - yn * jnp.mean(dyn * yn, axis=-1, keepdims=True))
        dmerged = _dot_nt(dy, wout_ref[...])
        sp, sh = _sigmoid(mgp_ref[...]), _sigmoid(mgh_ref[...])
        dba, dbb = sp * dmerged, sh * dmerged
        dmg_ref[:, 0:D_MODEL] = (dmerged * ba_ref[...].astype(F32) * sp * (1.0 - sp)).astype(dmg_ref.dtype)
        dmg_ref[:, D_MODEL:2 * D_MODEL] = (dmerged * bb_ref[...].astype(F32) * sh * (1.0 - sh)).astype(dmg_ref.dtype)
        da = _dot_nt(dba, jnp.concatenate([wpo_ref[j] for j in range(N_DEV)], axis=1))
        dy_ref[...] = dy.astype(dy_ref.dtype)
        dba_ref[...] = dba.astype(dba_ref.dtype)
        dbb_ref[...] = dbb.astype(dbb_ref.dtype)
        da_ref[...] = da.astype(da_ref.dtype)
        db_ref[...] = _dot_nt(dbb, who_ref[...]).astype(db_ref.dtype)
        _stage_end(stage, sems, dz_ref, step, n_steps, where)

    def tile(cols=D_MODEL, block=0):
        return pl.BlockSpec((tm, cols), lambda i: (i, block))

    full = pl.BlockSpec((D_MODEL, D_MODEL), lambda i: (0, 0))
    hbm = pl.BlockSpec(memory_space=pl.ANY)
    act = jax.ShapeDtypeStruct((seq, D_MODEL), MXU_DTYPE)
    return pl.pallas_call(
        body, name=name, grid=(n_steps,),
        in_specs=[tile(), tile(), tile(), tile(), tile(block=COL_MGP), tile(block=COL_MGH),
                  pl.BlockSpec((N_DEV, POOL_WIDTH, GROUP_DIM), lambda i: (0, 0, 0)),
                  full, full, _row_spec(), _row_spec(), hbm],
        out_specs=[tile(), tile(), tile(), tile(POOL_WIDTH), tile(), hbm,
                   pl.BlockSpec((8, D_MODEL), lambda i: (0, 0))],
        out_shape=[act, act, act, jax.ShapeDtypeStruct((seq, POOL_WIDTH), MXU_DTYPE), act,
                   jax.ShapeDtypeStruct(dz.shape, dz.dtype),
                   jax.ShapeDtypeStruct((8, D_MODEL), F32)],
        input_output_aliases={11: 5},
        scratch_shapes=[pltpu.VMEM((2, tm, 2 * D_MODEL), MXU_DTYPE), pltpu.SemaphoreType.DMA((2,))],
        compiler_params=_params(dimension_semantics=("arbitrary",)),
    )(dx, y, ba, bb, z, z, wpo_g, who_g, wout_g, gate, g_post, dz)


def _grad_out_weights(merged, dy, b_in, dbb, a_in, dba, name):
    seq = merged.shape[0]
    tn = D_MODEL // 2
    per_step = tn // GROUP_DIM

    def body(mer_ref, dy_ref, b_ref, dbb_ref, a_ref, dba_ref, gout_ref, gho_ref, gpo_ref):
        gout_ref[...] = _dot_tn(mer_ref[...], dy_ref[...]).astype(gout_ref.dtype)
        gho_ref[...] = _dot_tn(b_ref[...], dbb_ref[...]).astype(gho_ref.dtype)
        g_po = _dot_tn(a_ref[...], dba_ref[...])
        for j in range(per_step):
            gpo_ref[j] = g_po[:, j * GROUP_DIM:(j + 1) * GROUP_DIM].astype(gpo_ref.dtype)

    def whole(cols):
        return pl.BlockSpec((seq, cols), lambda j: (0, 0))

    cols = pl.BlockSpec((seq, tn), lambda j: (0, j))
    return pl.pallas_call(
        body, name=name, grid=(D_MODEL // tn,),
        in_specs=[whole(D_MODEL), cols, whole(D_MODEL), cols, whole(POOL_WIDTH), cols],
        out_specs=[pl.BlockSpec((D_MODEL, tn), lambda j: (0, j)), pl.BlockSpec((D_MODEL, tn), lambda j: (0, j)),
                   pl.BlockSpec((per_step, POOL_WIDTH, GROUP_DIM), lambda j: (j, 0, 0))],
        out_shape=[jax.ShapeDtypeStruct((D_MODEL, D_MODEL), WIRE_DTYPE),
                   jax.ShapeDtypeStruct((D_MODEL, D_MODEL), WIRE_DTYPE),
                   jax.ShapeDtypeStruct((N_DEV, POOL_WIDTH, GROUP_DIM), WIRE_DTYPE)],
        compiler_params=_params(dimension_semantics=("parallel",)),
    )(merged, dy, b_in, dbb, a_in, dba)


def _hgrn_bwd(db_in, z, o, states, cum_base, lb_l, gn_l, dz, name, after=None):
    seq = z.shape[0]
    per_step = min(BWD_STEP_CHUNKS, seq // CHUNK)
    rows_per_step = per_step * CHUNK
    n_steps = seq // rows_per_step
    last_step = n_steps - 1

    def body(db_ref, hq_ref, hf_ref, hi_ref, hg_ref, o_ref, st_ref, cb_ref, lb_ref, gn_ref, _,
             dz_hbm, dlb_ref, dgn_ref, dstate, dq_buf, dk_buf, dg_buf, stage, sems):
        step = pl.program_id(0)

        @pl.when(step == 0)
        def _():
            dstate[...] = jnp.zeros_like(dstate)
            dlb_ref[...] = jnp.zeros_like(dlb_ref)
            dgn_ref[...] = jnp.zeros_like(dgn_ref)

        def one_chunk(cc, *args):
            one_chunk_body((db_ref, hq_ref, hf_ref, hi_ref, hg_ref, o_ref, st_ref, cb_ref, dlb_ref, dgn_ref, dstate,
                            dq_buf, dk_buf, dg_buf), cc, *args)

        def where(t):
            return pl.ds((last_step - t) * rows_per_step, rows_per_step), pl.ds(COL_HQ * D_MODEL, 4 * D_MODEL)

        dz_step = stage.at[_stage_begin(stage, sems, dz_hbm, step, where)]
        causal, before_sub, suffix = _chunk_masks()
        lb = lb_ref[...]
        gn = gn_ref[...]
        for cc in reversed(range(per_step)):
            one_chunk(cc, dz_step, causal, before_sub, suffix, lb, gn)
        _stage_end(stage, sems, dz_hbm, step, n_steps, where)

    def one_chunk_body(refs, cc, dz_step, causal, before_sub, suffix, lb, gn):
        (db_ref, hq_ref, hf_ref, hi_ref, hg_ref, o_ref, st_ref, cb_ref, dlb_ref, dgn_ref, dstate,
         dq_buf, dk_buf, dg_buf) = refs
        rows = slice(cc * CHUNK, (cc + 1) * CHUNK)
        dz_ref = dz_step.at[rows, :]
        dq_buf, dk_buf, dg_buf = dq_buf.at[cc], dk_buf.at[cc], dg_buf.at[cc]
        sg, f, logf = _gates(hf_ref[rows, :], lb)
        kk = 1.0 - f
        hq = hq_ref[rows, :]
        sq = _sigmoid(hq)
        q = hq * sq
        cum, base = cb_ref[rows, 0:D_MODEL], cb_ref[rows, D_MODEL:2 * D_MODEL]
        dgn = jnp.zeros((1, HEAD_DIM), F32)
        dlast = []
        for h in range(HEADS):
            sl = slice(h * HEAD_DIM, (h + 1) * HEAD_DIM)
            q_h, k_h, cum_h = q[:, sl], kk[:, sl], cum[:, sl]
            v_h = hi_ref[rows, sl]
            st_h = st_ref[cc, h]
            dst_h = dstate[h]
            rs, ohat = _rms_parts(o_ref[rows, sl])
            hg = hg_ref[rows, sl]
            shg = _sigmoid(hg)
            d_bin = db_ref[rows, sl].astype(F32)
            don = d_bin * (hg * shg)
            dgn += jnp.sum(don * ohat, axis=0, keepdims=True)
            dohat = don * gn
            do = rs * (dohat - ohat * jnp.mean(dohat * ohat, axis=-1, keepdims=True))
            dz_ref[:, 3 * D_MODEL + h * HEAD_DIM:3 * D_MODEL + (h + 1) * HEAD_DIM] = (
                d_bin * (ohat * gn) * _dsilu(hg, shg)).astype(dz_ref.dtype)
            last = cb_ref[(cc + 1) * CHUNK - 1:(cc + 1) * CHUNK, sl]
            g_in = jnp.exp(cum_h)
            d_out = jnp.exp(last - cum_h)
            q_bar, k_bar = q_h * g_in, k_h * d_out
            blocks = _intra_blocks(q_h, k_h, cum_h, base[:, sl], causal)
            a = jnp.concatenate([b[4] for b in blocks], axis=0)
            da = jnp.where(causal, _dot_nt(do, v_h), 0.0)
            dv = _dot_tn(a, do) + _dot_nt(k_bar, dst_h)
            dq_bar, dk_bar = _dot(do, st_h), _dot(v_h, dst_h)
            dk = dk_bar * d_out
            dq_parts, dg_parts = [], []
            dg_k = k_bar * dk_bar
            dlast.append(jnp.sum(k_bar * dk_bar, axis=0, keepdims=True)
                         + jnp.exp(last) * jnp.sum(st_h * dst_h, axis=0, keepdims=True))
            for i, (q_t, k_t, e_q, e_k, _) in enumerate(blocks):
                da_i = da[i * SUB:(i + 1) * SUB].astype(MXU_DTYPE)
                dq_t = _dot(da_i, k_t)
                dk_t = _dot_tn(da_i, q_t)
                dq_parts.append(dq_t * e_q)
                dk += dk_t * e_k
                dg_parts.append(q_t.astype(F32) * dq_t)
                dg_k += k_t.astype(F32) * dk_t
            dq = dq_bar * g_in + jnp.concatenate(dq_parts, axis=0)
            dg_buf[:, sl] = q_bar * dq_bar + jnp.concatenate(dg_parts, axis=0) - dg_k
            dstate[h] = dst_h * jnp.exp(last) + _dot_tn(do, q_bar)
            dq_buf[:, sl] = dq
            dk_buf[:, sl] = dk
            dz_ref[:, 2 * D_MODEL + h * HEAD_DIM:2 * D_MODEL + (h + 1) * HEAD_DIM] = dv.astype(dz_ref.dtype)
        dgn_ref[...] += dgn
        dq_all, dk_all = dq_buf[...], dk_buf[...]
        dlogf = _masked_sums([suffix], dg_buf[...])[0] + jnp.concatenate(dlast, axis=1)
        df = jnp.where(f > LOG_FLOOR, dlogf / f, 0.0) - dk_all
        dlb_ref[...] += jnp.sum(df * (1.0 - sg), axis=0, keepdims=True)
        dz_ref[:, 0:D_MODEL] = (dq_all * _dsilu(hq, sq)).astype(dz_ref.dtype)
        dz_ref[:, D_MODEL:2 * D_MODEL] = (df * (1.0 - lb) * sg * (1.0 - sg)).astype(dz_ref.dtype)

    def col(block):
        return pl.BlockSpec((rows_per_step, D_MODEL), lambda c: (last_step - c, block))

    hbm = pl.BlockSpec(memory_space=pl.ANY)
    return _pallas_after(
        body, 11, after, name=name, grid=(n_steps,),
        in_specs=[col(0), col(COL_HQ), col(COL_HF), col(COL_HI), col(COL_HG), col(0),
                  pl.BlockSpec((per_step, HEADS, HEAD_DIM, HEAD_DIM), lambda c: (last_step - c, 0, 0, 0)),
                  pl.BlockSpec((rows_per_step, 2 * D_MODEL), lambda c: (last_step - c, 0)),
                  _row_spec(), _row_spec(HEAD_DIM), hbm],
        out_specs=[hbm, _row_spec(), _row_spec(HEAD_DIM)],
        out_shape=[jax.ShapeDtypeStruct(dz.shape, dz.dtype),
                   jax.ShapeDtypeStruct((1, D_MODEL), F32), jax.ShapeDtypeStruct((1, HEAD_DIM), F32)],
        input_output_aliases={10: 0},
        scratch_shapes=[pltpu.VMEM((HEADS, HEAD_DIM, HEAD_DIM), F32)]
        + [pltpu.VMEM((per_step, CHUNK, D_MODEL), F32)] * 3
        + [pltpu.VMEM((2, rows_per_step, 4 * D_MODEL), MXU_DTYPE), pltpu.SemaphoreType.DMA((2,))],
        compiler_params=_params(dimension_semantics=("arbitrary",)),
    )(db_in, z, z, z, z, o, states, cum_base, lb_l, gn_l, dz)


def _pool_bwd(da_in, z, pool_w_l, pool_scale_l, dz, name, after=None):
    seq = z.shape[0]

    def body(da_ref, pv_ref, pg_ref, w_ref, sc_ref, _, dz_hbm, dw_ref, dsc_ref, stage_pv, stage_pg, sems_pv, sems_pg):
        g = pl.program_id(0)

        def where_pv(t):
            return pl.ds(0, seq), pl.ds(pl.multiple_of(t * GROUP_DIM, GROUP_DIM), GROUP_DIM)

        def where_pg(t):
            return pl.ds(0, seq), pl.ds(pl.multiple_of(POOL_WIDTH + t * GROUP_DIM, GROUP_DIM), GROUP_DIM)

        dpv_ref = stage_pv.at[_stage_begin(stage_pv, sems_pv, dz_hbm, g, where_pv)]
        dpg_ref = stage_pg.at[_stage_begin(stage_pg, sems_pg, dz_hbm, g, where_pg)]

        def one_group(i):
            pos = lax.broadcasted_iota(jnp.int32, (seq, GROUP_DIM), 0)
            pm, count = _pool_mean_minus_token_static(pv_ref[...], i, pos)
            lin0 = _dot(pm, w_ref[...])
            pg = pg_ref[...]
            spg = _sigmoid(pg)
            da = da_ref[...].astype(F32)
            dlin = da * (pg * spg)
            dpg_ref[...] = (da * (lin0 * sc_ref[...]) * _dsilu(pg, spg)).astype(dpg_ref.dtype)
            dsc_ref[...] = jnp.sum(dlin * lin0, axis=0, keepdims=True)
            dl0 = dlin * sc_ref[...]
            dw_ref[...] = _dot_tn(pm, dl0)
            dpm = _dot_nt(dl0, w_ref[...])
            acc = dpm / count
            for j in (1, 2, 4, 8)[:i + 1]:
                acc = acc + _shift_up(acc, j, pos, seq)
            dpv_ref[...] = (acc - dpm).astype(dpv_ref.dtype)

        _per_group(one_group)
        _stage_end(stage_pv, sems_pv, dz_hbm, g, POOL_GROUPS, where_pv)
        _stage_end(stage_pg, sems_pg, dz_hbm, g, POOL_GROUPS, where_pg)

    grp = pl.BlockSpec((seq, GROUP_DIM), lambda g: (0, g))
    hbm = pl.BlockSpec(memory_space=pl.ANY)
    stage = pltpu.VMEM((2, seq, GROUP_DIM), MXU_DTYPE)
    return _pallas_after(
        body, 6, after, name=name, grid=(POOL_GROUPS,),
        in_specs=[grp, grp, pl.BlockSpec((seq, GROUP_DIM), lambda g: (0, POOL_GROUPS + g)),
                  pl.BlockSpec((None, GROUP_DIM, GROUP_DIM), lambda g: (g, 0, 0)),
                  pl.BlockSpec((1, GROUP_DIM), lambda g: (0, g)), hbm],
        out_specs=[hbm, pl.BlockSpec((None, GROUP_DIM, GROUP_DIM), lambda g: (g, 0, 0)),
                   pl.BlockSpec((1, GROUP_DIM), lambda g: (0, g))],
        out_shape=[jax.ShapeDtypeStruct(dz.shape, dz.dtype),
                   jax.ShapeDtypeStruct((POOL_GROUPS, GROUP_DIM, GROUP_DIM), F32),
                   jax.ShapeDtypeStruct((1, POOL_WIDTH), F32)],
        input_output_aliases={5: 0},
        scratch_shapes=[stage, stage, pltpu.SemaphoreType.DMA((2,)), pltpu.SemaphoreType.DMA((2,))],
        compiler_params=_params(dimension_semantics=("arbitrary",)),
    )(da_in, z, z, pool_w_l, pool_scale_l, dz)


def _in_proj_dw(h, dz, name, after=None):
    seq = h.shape[0]

    def body(h_ref, dz_ref, out_ref):
        pair = lax.dot_general(h_ref[...], dz_ref[...], (((0,), (0,)), ((), ())), preferred_element_type=F32)
        out_ref[0] = pair[:, 0:IN_COLS].astype(out_ref.dtype)
        out_ref[1] = pair[:, IN_COLS:].astype(out_ref.dtype)

    return _pallas_after(
        body, 2, after, name=name, grid=(N_DEV // 2,),
        in_specs=[pl.BlockSpec((seq, D_MODEL), lambda j: (0, 0)),
                  pl.BlockSpec((seq, 2 * IN_COLS), lambda j: (0, j))],
        out_specs=pl.BlockSpec((2, D_MODEL, IN_COLS), lambda j: (j, 0, 0)),
        out_shape=jax.ShapeDtypeStruct((N_DEV, D_MODEL, IN_COLS), WIRE_DTYPE),
        compiler_params=_params(dimension_semantics=("parallel",)),
    )(h, dz)


def _in_proj_dh(dz, win_g, tm, name, after=None):
    seq = dz.shape[0]

    def body(dz_ref, w_ref, dh_ref):
        @pl.when(pl.program_id(1) == 0)
        def _():
            dh_ref[...] = jnp.zeros_like(dh_ref)

        w_pair = jnp.concatenate([w_ref[0], w_ref[1]], axis=1)
        dh_ref[...] += lax.dot_general(dz_ref[...], w_pair, (((1,), (1,)), ((), ())), preferred_element_type=F32)

    return _pallas_after(
        body, 2, after, name=name, grid=(seq // tm, N_DEV // 2),
        in_specs=[pl.BlockSpec((tm, 2 * IN_COLS), lambda i, j: (i, j)),
                  pl.BlockSpec((2, D_MODEL, IN_COLS), lambda i, j: (j, 0, 0))],
        out_specs=pl.BlockSpec((tm, D_MODEL), lambda i, j: (i, 0)),
        out_shape=jax.ShapeDtypeStruct((seq, D_MODEL), F32),
        compiler_params=_params(dimension_semantics=("parallel", "arbitrary")),
    )(dz, win_g)


def _prenorm_bwd(x, dh, dx_res, g, scale, tm, name, after=None):
    seq = x.shape[0]

    def body(x_ref, dh_ref, dxr_ref, g_ref, sc_ref, dx_ref, acc_ref):
        @pl.when(pl.program_id(0) == 0)
        def _():
            acc_ref[...] = jnp.zeros_like(acc_ref)

        rs, xn = _rms_parts(x_ref[...])
        dh = dh_ref[...]
        acc_ref[0:1, :] += jnp.sum(dh, axis=0, keepdims=True)
        acc_ref[1:2, :] += jnp.sum(dh * (xn * g_ref[...]), axis=0, keepdims=True)
        dhn = dh * (1.0 + sc_ref[...])
        acc_ref[2:3, :] += jnp.sum(dhn * xn, axis=0, keepdims=True)
        dxn = dhn * g_ref[...]
        dx_ref[...] = rs * (dxn - xn * jnp.mean(dxn * xn, axis=-1, keepdims=True)) + dxr_ref[...]

    tile = pl.BlockSpec((tm, D_MODEL), lambda i: (i, 0))
    return _pallas_after(
        body, 5, after, name=name, grid=(seq // tm,),
        in_specs=[tile, tile, tile, _row_spec(), _row_spec()],
        out_specs=[tile, pl.BlockSpec((8, D_MODEL), lambda i: (0, 0))],
        out_shape=[jax.ShapeDtypeStruct((seq, D_MODEL), F32), jax.ShapeDtypeStruct((8, D_MODEL), F32)],
        compiler_params=_params(dimension_semantics=("arbitrary",)),
    )(x, dh, dx_res, g, scale)


def _adamw_math(w, g, m, v):
    m = ADAM_B1 * m + (1.0 - ADAM_B1) * g
    v = ADAM_B2 * v + (1.0 - ADAM_B2) * (g * g)
    m_hat = m / (1.0 - ADAM_B1 ** ADAM_STEP)
    v_hat = v / (1.0 - ADAM_B2 ** ADAM_STEP)
    delta = -ADAM_LR * (m_hat / (jnp.sqrt(v_hat) + ADAM_EPS) + ADAM_WD * w)
    return delta, m, v


def _adamw_layer(w, m, v, contribs, l, tr, name, prev=None):
    _, rows, cols = w.shape
    n = len(contribs)

    def body(*refs):
        w_ref, m_ref, v_ref = refs[:3]
        c_refs = refs[3:3 + n]
        g_ref, d_ref, mo_ref, vo_ref = refs[-4:]
        g = c_refs[0][...].astype(F32)
        for c_ref in c_refs[1:]:
            g += c_ref[...].astype(F32)
        delta, mn, vn = _adamw_math(w_ref[...], g, m_ref[...], v_ref[...])
        g_ref[...] = g
        d_ref[...] = delta
        mo_ref[...] = mn
        vo_ref[...] = vn

    tile = pl.BlockSpec((None, tr, cols), lambda i: (l, i, 0))
    in_specs = [tile, tile, tile] + [pl.BlockSpec((None, tr, cols), lambda i, s=slot: (s, i, 0)) for _, slot in contribs]
    operands = [w, m, v] + [arr for arr, _ in contribs]
    aliases = {}
    if prev is not None:
        aliases = {len(operands) + k: k for k in range(4)}
        in_specs += [pl.BlockSpec(memory_space=pl.ANY)] * 4
        operands += list(prev)
    shape = jax.ShapeDtypeStruct(w.shape, F32)
    return pl.pallas_call(
        body, name=name, grid=(rows // tr,), in_specs=in_specs, out_specs=[tile] * 4, out_shape=[shape] * 4,
        input_output_aliases=aliases,
        compiler_params=_params(dimension_semantics=("parallel",)),
    )(*operands)


def _adamw_small(w_pack, m_pack, v_pack, g_late, g_early, shapes):
    pieces, r = {}, 0
    for name, _, n in _SMALL_ROWS:
        pieces.setdefault(name, []).append((r, n))
        r += n
    names = list(pieces)

    def body(w_ref, m_ref, v_ref, gl_ref, ge_ref, *rest):
        outs, packs = rest[:4 * len(names)], rest[4 * len(names):]
        g_l, g_e = gl_ref[0][0:SMALL_LATE_ROWS], ge_ref[0]
        for d in range(1, N_DEV):
            g_l += gl_ref[d][0:SMALL_LATE_ROWS]
            g_e += ge_ref[d]
        g = jnp.concatenate([g_l, g_e], axis=0)
        w = w_ref[...]
        r0, r1, r2 = LB_ROW0, LB_ROW0 + 8, LB_ROW0 + 16
        lg0, lg1 = w[r0:r1], w[r1:r2]
        mx = jnp.maximum(lg0, lg1)
        e0, e1 = jnp.exp(lg0 - mx), jnp.exp(lg1 - mx)
        p0, p1 = e0 / (e0 + e1), e1 / (e0 + e1)
        low = ((p0 - p0), (p0 + p1) - p0)
        dlow = [g_rows * jnp.where((lo > 0.0) & (lo < 1.0), 1.0, jnp.where((lo == 0.0) | (lo == 1.0), 0.5, 0.0))
                for g_rows, lo in ((g[r0:r1], low[0]), (g[r1:r2], low[1]))]
        dp0 = (dlow[0] + dlow[1]) - (dlow[0] + dlow[1])
        dp1 = dlow[1]
        inner = p0 * dp0 + p1 * dp1
        g = jnp.concatenate([g[:r0], p0 * (dp0 - inner), p1 * (dp1 - inner), g[r2:]], axis=0)
        delta, mn, vn = _adamw_math(w, g, m_ref[...], v_ref[...])
        for kind, val in enumerate((g, delta, mn, vn)):
            packs[kind][...] = val
            for j, name in enumerate(names):
                out, at = outs[kind * len(names) + j], 0
                for start, n in pieces[name]:
                    if name in flat:
                        for r in range(n):
                            layer, c = divmod(at + r, flat[name])
                            out[layer:layer + 1, c * 128:(c + 1) * 128] = packs[kind][start + r:start + r + 1, :]
                    else:
                        out[at:at + n, :] = packs[kind][start:start + n, :]
                    at += n

    rows = {name: sum(n for _, n in pieces[name]) for name in names}
    flat = {name: rows[name] // DEPTH for name in names if len(shapes[name]) == 2}
    outs = pl.pallas_call(
        body, name="adamw_small",
        out_shape=[jax.ShapeDtypeStruct(shapes[name] if name in flat else (rows[name], 128), F32)
                   for _ in range(4) for name in names],
        scratch_shapes=[pltpu.VMEM(w_pack.shape, F32)] * 4, compiler_params=_params(),
    )(w_pack, m_pack, v_pack, g_late, g_early)
    return [{name: outs[kind * len(names) + j].reshape(shapes[name]) for j, name in enumerate(names)}
            for kind in range(4)]


def _pack_small(parts, first=0, last=len(_SMALL_ROWS)):
    rows = [(parts[name] if l is None else parts[name][l]).reshape(n, 128) for name, l, n in _SMALL_ROWS[first:last]]
    if last == len(_SMALL_ROWS):
        rows.append(jnp.zeros((SMALL_ROWS_PAD - sum(n for _, _, n in _SMALL_ROWS), 128), F32))
    return jnp.concatenate(rows, axis=0)


def kernel(x, c, w_ada, b_ada, g_pre, g_post, w_in, pool_w, pool_scale, lb_logits, hgrn_norm_g, w_pool_o, w_hgrn_o, w_out, loss_target, m_w_ada, m_b_ada, m_g_pre, m_g_post, m_w_in, m_pool_w, m_pool_scale, m_lb_logits, m_hgrn_norm_g, m_w_pool_o, m_w_hgrn_o, m_w_out, v_w_ada, v_b_ada, v_g_pre, v_g_post, v_w_in, v_pool_w, v_pool_scale, v_lb_logits, v_hgrn_norm_g, v_w_pool_o, v_w_hgrn_o, v_w_out):
    seq = x.shape[1]
    tm = min(1024, seq)
    tm_merge = min(512, seq)
    pos = _my_position()
    me = pos[3]

    big = dict(win=w_in, wpo=w_pool_o, who=w_hgrn_o, wout=w_out)
    units = [["win0"], ["wpo0", "who0", "wout0"], ["win1", "wpo1", "who1", "wout1"]]
    g_streams = [_gather_streams(keys) for keys in units]
    g_state = [None] * len(units)

    def gather_start(us, after, first=None):
        bufs = dict(first[0]) if first else {}
        for k in [k for u in us for k in units[u]]:
            arr = big[k[:-1]]
            bufs["g_" + k] = _with_own_slot(arr[int(k[-1])].astype(WIRE_DTYPE), me)
        streams = ([first[1]] if first else []) + [s for u in us for s in g_streams[u][:2]]
        bufs, sems, token = _comm_call("gather_start_" + "_".join(map(str, us)), bufs, start=streams, after=after)
        if first:
            first_out, sems = ({k: bufs[k] for k in first[0]}, sems[0]), sems[1:]
        for n, u in enumerate(us):
            g_state[u] = dict(bufs={"g_" + k: bufs["g_" + k] for k in units[u]},
                              sems=sems[2 * n:2 * n + 2])
        return (token, first_out) if first else token

    def gather_pass(u, after):
        st = g_state[u]
        to_chips, _, pass_on = g_streams[u]
        st["bufs"], (st["pass_sems"],), _ = _comm_call(f"gather_pass_{u}", st["bufs"], start=[pass_on],
                                                       wait=[(to_chips, st["sems"][0])], after=after)

    def gather_done(u, after=None):
        st = g_state[u]
        _, to_sibling, pass_on = g_streams[u]
        bufs, _, _ = _comm_call(f"gather_done_{u}", st["bufs"], after=after,
                                wait=[(to_sibling, st["sems"][1]), (pass_on, st["pass_sems"])])
        return {k: bufs["g_" + k] for k in units[u]}

    c_stream = _direct_gather_stream("c")
    token, (c_bufs, c_sems) = gather_start([0], None, first=(dict(s_c=c, g_c=_with_own_slot(c, me)), c_stream))
    c_bufs, _, _ = _comm_call("gather_c_done", c_bufs, wait=[(c_stream, c_sems)])
    c_all = c_bufs["g_c"].reshape(N_DEV, D_MODEL)
    b_cols = lax.dynamic_slice_in_dim(b_ada, me * ADA_COLS, ADA_COLS, axis=1)
    ada_part = _ada_fwd(c_all, w_ada, b_cols)
    gather_pass(0, ada_part)
    ada_all = _allgather_small(ada_part.reshape(DEPTH * N_DEV, ADA_COLS), "allgather_ada",
                               after=g_state[0]["bufs"]["g_win0"])
    ada = lax.dynamic_index_in_dim(ada_all.reshape(N_DEV, DEPTH, N_DEV, ADA_COLS), me, axis=2, keepdims=False)
    ada = jnp.transpose(ada, (1, 0, 2)).reshape(DEPTH, 3 * D_MODEL)
    shift = [ada[l:l + 1, 0:D_MODEL] for l in range(DEPTH)]
    scale = [ada[l:l + 1, D_MODEL:2 * D_MODEL] for l in range(DEPTH)]
    gate = [ada[l:l + 1, 2 * D_MODEL:] for l in range(DEPTH)]

    lb = _lb_fwd(lb_logits)

    gw = {}
    xs, saved = [x[0]], []
    for l in range(DEPTH):
        h = _prenorm_fwd(xs[l], g_pre[l:l + 1], shift[l], scale[l], tm, f"prenorm_fwd_{l}",
                         after=token if l == 0 else None)
        token = None
        if l == 0:
            gw.update(gather_done(0, h))
            token = gather_start([1, 2], gw["win0"])
        else:
            gw.update(gather_done(2, h))
        z = _in_proj(h, gw[f"win{l}"], min(1024, seq), f"in_proj_{l}", after=token)
        a_in = _pool_fwd(z, pool_w[l], pool_scale[l:l + 1], f"pool_fwd_{l}")
        o, b_in, states, cum_base = _hgrn_fwd(z, lb[l:l + 1], hgrn_norm_g[l:l + 1], f"hgrn_fwd_{l}")
        if l == 0:
            gather_pass(1, b_in)
            gw.update(gather_done(1))
        who_l = gw[f"who{l}"].reshape(D_MODEL, D_MODEL)
        wout_l = gw[f"wout{l}"].reshape(D_MODEL, D_MODEL)
        last = l == DEPTH - 1
        ba, bb, merged, y, *out = _merge_fwd(a_in, b_in, z, xs[l], gw[f"wpo{l}"], who_l, wout_l, gate[l],
                                             g_post[l:l + 1], tm_merge, f"merge_fwd_{l}",
                                             target=loss_target[0] if last else None)
        if last:
            dx, loss_part = out
        else:
            xs.append(out[0])
            gather_pass(2, out[0])
        saved.append((h, z, a_in, o, b_in, states, cum_base, ba, bb, merged, y, who_l, wout_l))


    chips = _other_chips(pos)
    pair_idx = jnp.stack([_dev_index(cx, cy, pos[2]) for cx, cy in chips] + [me]).astype(jnp.int32)
    pair_rows = dict(win=512, wpo=POOL_WIDTH, who=HEAD_DIM, wout=HEAD_DIM)

    def scatter_pair_start(u, grads):
        keys = list(grads)
        pair, to_chips = _scatter_streams(keys)
        bufs = {}
        for k in keys:
            bufs["g_" + k] = grads[k]
            bufs["st_" + k] = lax.empty((4,) + grads[k].shape[1:], WIRE_DTYPE)
        bufs, (sems,), token = _comm_call(f"scatter_pair_start_{u}", bufs, start=[pair])
        return dict(u=u, keys=keys, pair=pair, to_chips=to_chips, bufs=bufs, sems=sems, token=token)

    def scatter_pair_finish(st, after):
        u, keys = st["u"], st["keys"]
        bufs, _, _ = _comm_call(f"scatter_pair_done_{u}", st["bufs"], wait=[(st["pair"], st["sems"])], after=after)
        bufs2 = {}
        for k in keys:
            bufs2["ps_" + k] = _pair_sum(bufs["g_" + k], bufs["st_" + k], pair_idx, bufs["g_" + k].shape[1],
                                         f"pair_sum_{k}")
            bufs2["ld_" + k] = lax.empty((3,) + bufs["g_" + k].shape[1:], WIRE_DTYPE)
        st.update(bufs=bufs2)

    def scatter_chips_start(st, after=None):
        bufs2, (sems,), token = _comm_call(f"scatter_chips_start_{st['u']}", st["bufs"], start=[st["to_chips"]],
                                           after=after)
        st.update(bufs=bufs2, sems=sems, token=token)

    def scatter_finish(st, after):
        bufs, _, _ = _comm_call(f"scatter_chips_done_{st['u']}", st["bufs"], wait=[(st["to_chips"], st["sems"])],
                                after=after)
        return {k: [(bufs["ps_" + k], 3), (bufs["ld_" + k], 0), (bufs["ld_" + k], 1), (bufs["ld_" + k], 2)]
                for k in st["keys"]}

    moments = dict(win=(m_w_in, v_w_in), wpo=(m_w_pool_o, v_w_pool_o), who=(m_w_hgrn_o, v_w_hgrn_o),
                   wout=(m_w_out, v_w_out))
    big_out = {}

    def finish_unit(unit, after):
        for k, contribs in scatter_finish(scat[unit], after).items():
            wname, l = k[:-1], int(k[-1])
            big_out[wname] = _adamw_layer(big[wname], moments[wname][0], moments[wname][1], contribs, l,
                                          pair_rows[wname], f"adamw_{k}", prev=big_out.get(wname))
            after = big_out[wname][0]
        return after

    d_ada, small, scat = [None] * DEPTH, [None] * DEPTH, {}
    for l in reversed(range(DEPTH)):
        h, z, a_in, o, b_in, states, cum_base, ba, bb, merged, y, who_l, wout_l = saved[l]
        dy, dba, dbb, da_in, db_in, dz, acc_post = _merge_bwd(
            dx, y, ba, bb, z, gw[f"wpo{l}"], who_l, wout_l, gate[l], g_post[l:l + 1],
            lax.empty((seq, IN_WIDTH), MXU_DTYPE), tm_merge, f"merge_bwd_{l}")
        g_out, g_ho, g_po = _grad_out_weights(merged, dy, b_in, dbb, a_in, dba, f"grad_out_weights_{l}")
        g_small = {f"wout{l}": g_out.reshape(N_DEV, HEAD_DIM, D_MODEL),
                   f"who{l}": g_ho.reshape(N_DEV, HEAD_DIM, D_MODEL), f"wpo{l}": g_po}
        st_small = scat["small0"] = scatter_pair_start("small0", g_small) if l == 0 else None
        dz, dlb, dgn = _hgrn_bwd(db_in, z, o, states, cum_base, lb[l:l + 1], hgrn_norm_g[l:l + 1], dz, f"hgrn_bwd_{l}",
                                 after=st_small and st_small["token"])
        if l == 0:
            scatter_pair_finish(st_small, dlb)
            scatter_chips_start(st_small)
        dz, dpw, dps = _pool_bwd(da_in, z, pool_w[l], pool_scale[l:l + 1], dz, f"pool_bwd_{l}",
                                 after=st_small and st_small["token"])
        small[l] = dict(g_post=acc_post[1], pool_w=dpw, pool_scale=dps[0], lb_logits=dlb[0], hgrn_norm_g=dgn[0])
        token = None
        if l == 0:
            parts = {name: jnp.stack([small[0][name], small[1][name]]) for name in small[0]}
            parts.update(b_ada=[None, d_ada[1]], g_pre=[None, small[1]["g_pre"]])
            sg_stream = _direct_gather_stream("sg")
            early = _pack_small(parts, 2)
            sg_bufs, (sg_sems,), token = _comm_call(
                "small_grads_start", dict(s_sg=early, g_sg=_with_own_slot(early, me)), start=[sg_stream])
        g_win = {f"win{l}": _in_proj_dw(h, dz, f"grad_w_in_{l}", after=token)}
        st_win = scat[f"win{l}"] = scatter_pair_start(f"win{l}", g_win if l == 0 else {**g_small, **g_win})
        if l > 0:
            dh = _in_proj_dh(dz, gw[f"win{l}"], seq, f"in_proj_dh_{l}", after=st_win["token"])
            scatter_pair_finish(st_win, dh)
            scatter_chips_start(st_win)
        else:
            after = finish_unit("win1", st_win["token"])
            scatter_pair_finish(st_win, after)
            scatter_chips_start(st_win)
            after = finish_unit("small0", st_win["token"])
            dh = _in_proj_dh(dz, gw[f"win{l}"], seq, f"in_proj_dh_{l}", after=after)
        dx, acc_pre = _prenorm_bwd(xs[l], dh, dx, g_pre[l:l + 1], scale[l], tm, f"prenorm_bwd_{l}",
                                   after=st_win["token"])
        d_ada[l] = jnp.concatenate([acc_pre[0], acc_pre[1], acc_post[0]])
        small[l]["g_pre"] = acc_pre[2]
    grad_x = dx[None]

    sg_bufs, _, _ = _comm_call("small_grads_done", sg_bufs, wait=[(sg_stream, sg_sems)], after=dx)
    g_early = sg_bufs["g_sg"]

    def w_ada_layer(l, d_rows, prev):
        d_cols = lax.dynamic_slice_in_dim(d_rows.reshape(N_DEV, 3 * D_MODEL), me * ADA_COLS, ADA_COLS, axis=1)
        return _adamw_layer(w_ada, m_w_ada, v_w_ada, [(_ada_bwd(c_all, d_cols, f"ada_bwd_{l}"), 0)], l, 512,
                            f"adamw_w_ada{l}", prev=prev)

    ada_out = w_ada_layer(1, g_early[:, 0:24, :], None)

    parts = dict(b_ada=[d_ada[0]], g_pre=[small[0]["g_pre"]])
    late = jnp.concatenate([_pack_small(parts, 0, 2), jnp.broadcast_to(loss_part, (8, 128))], axis=0)
    g_late = _allgather_small(late, "allgather_late_grads", after=ada_out[0])
    loss = jnp.sum(g_late[:, SMALL_LATE_ROWS, 0])
    small_names = list(dict.fromkeys(name for name, _, _ in _SMALL_ROWS))
    weights = dict(b_ada=b_ada, g_pre=g_pre, g_post=g_post, pool_w=pool_w, pool_scale=pool_scale,
                   lb_logits=lb_logits, hgrn_norm_g=hgrn_norm_g)
    m_small = dict(b_ada=m_b_ada, g_pre=m_g_pre, g_post=m_g_post, pool_w=m_pool_w, pool_scale=m_pool_scale,
                   lb_logits=m_lb_logits, hgrn_norm_g=m_hgrn_norm_g)
    v_small = dict(b_ada=v_b_ada, g_pre=v_g_pre, g_post=v_g_post, pool_w=v_pool_w, pool_scale=v_pool_scale,
                   lb_logits=v_lb_logits, hgrn_norm_g=v_hgrn_norm_g)
    shapes = {name: weights[name].shape for name in small_names}
    small_out = _adamw_small(_pack_small(weights), _pack_small(m_small), _pack_small(v_small), g_late, g_early,
                             shapes)

    ada_out = w_ada_layer(0, g_late[:, 0:24, :], ada_out)
    finish_unit("win0", ada_out[1][0, 0:8, 0:128] + small_out[1]["pool_scale"][0:1, 0:128])

    def leaf(kind):
        s = small_out[kind]
        return (ada_out[kind], s["b_ada"], s["g_pre"], s["g_post"], big_out["win"][kind], s["pool_w"], s["pool_scale"],
                s["lb_logits"], s["hgrn_norm_g"], big_out["wpo"][kind], big_out["who"][kind], big_out["wout"][kind])

    return (loss, grad_x) + leaf(0) + leaf(1) + leaf(2) + leaf(3)
```

```python
import jax
import jax.numpy as jnp
from jax import lax
from jax.experimental import pallas as pl
from jax.experimental.pallas import tpu as pltpu

F32 = jnp.float32
MXU_DTYPE = jnp.bfloat16
WIRE_DTYPE = jnp.bfloat16

N_DEV = 8
DEPTH = 2
D_MODEL = 1024
HEADS = 8
HEAD_DIM = 128
POOL_GROUPS = 4
GROUP_DIM = 128
POOL_WIDTH = POOL_GROUPS * GROUP_DIM
IN_WIDTH = 7168
CHUNK = 64
SUB = 16
N_SUB = CHUNK // SUB
FWD_STEP_CHUNKS = 8
BWD_STEP_CHUNKS = 4
EXP_CLAMP = 80.0
NORM_EPS = 1e-6
LOG_FLOOR = 1e-30
ADA_COLS = 3 * D_MODEL // N_DEV
IN_COLS = IN_WIDTH // N_DEV
COL_HQ, COL_HF, COL_HI, COL_HG, COL_MGP, COL_MGH = 1, 2, 3, 4, 5, 6

ADAM_LR = 0.001
ADAM_B1 = 0.9
ADAM_B2 = 0.999
ADAM_EPS = 1e-08
ADAM_WD = 0.01
ADAM_STEP = 10

VMEM_LIMIT = 48 * 1024 * 1024
MESH_ID = pl.DeviceIdType.MESH
HIGHEST = lax.Precision.HIGHEST

_SMALL_ROWS = (("b_ada", 0, 24), ("g_pre", 0, 8), ("b_ada", 1, 24), ("g_pre", 1, 8), ("g_post", None, 16),
               ("pool_w", None, 1024), ("pool_scale", None, 8), ("lb_logits", None, 16), ("hgrn_norm_g", None, 2))
SMALL_LATE_ROWS = 32
SMALL_ROWS_PAD = 1136
LB_ROW0 = 32 + 32 + 16 + 1024 + 8


def _params(**kw):
    return pltpu.CompilerParams(vmem_limit_bytes=VMEM_LIMIT, **kw)


def _sigmoid(v):
    return 1.0 / (1.0 + jnp.exp(-v))


def _dsilu(v, s):
    return s * (1.0 + v * (1.0 - s))


def _dot(a, b):
    return jnp.dot(a.astype(MXU_DTYPE), b.astype(MXU_DTYPE), preferred_element_type=F32)


def _dot_nt(a, b):
    return lax.dot_general(a.astype(MXU_DTYPE), b.astype(MXU_DTYPE), (((1,), (1,)), ((), ())),
                           preferred_element_type=F32)


def _dot_tn(a, b):
    return lax.dot_general(a.astype(MXU_DTYPE), b.astype(MXU_DTYPE), (((0,), (0,)), ((), ())),
                           preferred_element_type=F32)


def _pallas_after(body, n_in, after, *, in_specs, **kw):
    if after is None:
        return pl.pallas_call(body, in_specs=in_specs, **kw)

    def tied(*refs):
        body(*refs[:n_in], *refs[n_in + 1:])

    call = pl.pallas_call(tied, in_specs=list(in_specs) + [pl.BlockSpec(memory_space=pl.ANY)], **kw)
    return lambda *operands: call(*operands, after)


def _my_position():
    mx, my, mc = lax.axis_index("x"), lax.axis_index("y"), lax.axis_index("c")
    return mx, my, mc, 4 * mx + 2 * my + mc


def _peer(mx, my, mc, k):
    px = 1 - mx if (k >> 2) & 1 else mx
    py = 1 - my if (k >> 1) & 1 else my
    pc = 1 - mc if k & 1 else mc
    return (px, py, pc), 4 * px + 2 * py + pc


def _allgather_small(v, name, after=None):
    rows, cols = v.shape

    def body(v_ref, out_ref, send_sems, recv_sems):
        mx, my, mc, me = _my_position()
        out_ref[me] = v_ref[...]
        copies = []
        for k in range(1, N_DEV):
            peer, _ = _peer(mx, my, mc, k)
            cp = pltpu.make_async_remote_copy(
                src_ref=v_ref, dst_ref=out_ref.at[me],
                send_sem=send_sems.at[k - 1], recv_sem=recv_sems.at[k - 1],
                device_id=peer, device_id_type=MESH_ID)
            cp.start()
            copies.append(cp)
        for cp in copies:
            cp.wait()

    return _pallas_after(
        body, 1, after, name=name,
        out_shape=jax.ShapeDtypeStruct((N_DEV, rows, cols), v.dtype),
        in_specs=[pl.BlockSpec(memory_space=pltpu.VMEM)],
        out_specs=pl.BlockSpec(memory_space=pltpu.VMEM),
        scratch_shapes=[pltpu.SemaphoreType.DMA((N_DEV - 1,)), pltpu.SemaphoreType.DMA((N_DEV - 1,))],
        compiler_params=_params(),
    )(v)


class _Stream:
    def __init__(self, n, plan):
        self.n, self.plan = n, plan


def _comm_call(name, bufs, start=(), wait=(), after=None):
    names = list(bufs)

    def body(*refs):
        it = iter(refs)
        buf_refs = {n: next(it) for n in names}
        wait_sems = [(next(it), next(it)) for _ in wait]
        if after is not None:
            next(it)
        start_sems = [(next(it), next(it)) for _ in start]
        for _ in names:
            next(it)
        token = next(it)
        pos = _my_position()

        def descriptors(stream, sems):
            return [pltpu.make_async_remote_copy(src_ref=src, dst_ref=dst, send_sem=sems[0].at[k], recv_sem=sems[1].at[k],
                                                 device_id=dev, device_id_type=MESH_ID)
                    for k, (src, dst, dev) in enumerate(stream.plan(buf_refs, pos))]

        for (stream, _), sems in zip(wait, wait_sems):
            for cp in descriptors(stream, sems):
                cp.wait_send()
                cp.wait_recv()
        for stream, sems in zip(start, start_sems):
            for cp in descriptors(stream, sems):
                cp.start()
        token[...] = jnp.zeros_like(token)

    hbm = pl.BlockSpec(memory_space=pltpu.HBM)
    sem = pl.BlockSpec(memory_space=pltpu.SEMAPHORE)
    operands = [pltpu.with_memory_space_constraint(bufs[n], pltpu.HBM) for n in names]
    in_specs = [hbm] * len(names)
    for _, (send_sems, recv_sems) in wait:
        operands += [send_sems, recv_sems]
        in_specs += [sem, sem]
    if after is not None:
        operands.append(after)
        in_specs.append(pl.BlockSpec(memory_space=pl.ANY))
    out_shape, out_specs = [], []
    for stream in start:
        out_shape += [pltpu.SemaphoreType.DMA((stream.n,)), pltpu.SemaphoreType.DMA((stream.n,))]
        out_specs += [sem, sem]
    n_sem_out = len(out_shape)
    out_shape += [pltpu.HBM(bufs[n].shape, bufs[n].dtype) for n in names]
    out_specs += [hbm] * len(names)
    out_shape.append(jax.ShapeDtypeStruct((8, 128), F32))
    out_specs.append(pl.BlockSpec(memory_space=pltpu.VMEM))
    outs = pl.pallas_call(
        body, name=name, out_shape=out_shape, in_specs=in_specs, out_specs=out_specs,
        input_output_aliases={i: n_sem_out + i for i in range(len(names))},
        compiler_params=pltpu.CompilerParams(has_side_effects=pltpu.SideEffectType.DATAFLOW_SIDE_EFFECTING),
    )(*operands)
    sems = [(outs[2 * i], outs[2 * i + 1]) for i in range(len(start))]
    return dict(zip(names, outs[n_sem_out:n_sem_out + len(names)])), sems, outs[-1]


def _with_own_slot(block, me):
    return lax.dynamic_update_index_in_dim(lax.empty((N_DEV,) + block.shape, block.dtype), block, me, 0)


def _other_chips(pos):
    mx, my, _, _ = pos
    return [(1 - mx if i & 2 else mx, 1 - my if i & 1 else my) for i in (1, 2, 3)]


def _dev_index(px, py, pc):
    return 4 * px + 2 * py + pc


def _gather_streams(keys):
    def to_chips(refs, pos):
        _, _, mc, me = pos
        return [(refs["g_" + k].at[me], refs["g_" + k].at[me], (cx, cy, mc))
                for k in keys for cx, cy in _other_chips(pos)]

    def to_sibling(refs, pos):
        mx, my, mc, me = pos
        return [(refs["g_" + k].at[me], refs["g_" + k].at[me], (mx, my, 1 - mc)) for k in keys]

    def pass_on(refs, pos):
        mx, my, mc, _ = pos
        out = []
        for k in keys:
            for cx, cy in _other_chips(pos):
                slot = refs["g_" + k].at[_dev_index(cx, cy, mc)]
                out.append((slot, slot, (mx, my, 1 - mc)))
        return out

    return _Stream(3 * len(keys), to_chips), _Stream(len(keys), to_sibling), _Stream(3 * len(keys), pass_on)


def _direct_gather_stream(key):
    def plan(refs, pos):
        mx, my, mc, me = pos
        return [(refs["s_" + key], refs["g_" + key].at[me], _peer(mx, my, mc, k)[0]) for k in range(1, N_DEV)]

    return _Stream(N_DEV - 1, plan)


def _scatter_streams(keys):
    def pair(refs, pos):
        mx, my, mc, _ = pos
        sib = (mx, my, 1 - mc)
        out = []
        for k in keys:
            for i, (cx, cy) in enumerate(_other_chips(pos)):
                out.append((refs["g_" + k].at[_dev_index(cx, cy, 1 - mc)], refs["st_" + k].at[i], sib))
            out.append((refs["g_" + k].at[_dev_index(mx, my, 1 - mc)], refs["st_" + k].at[3], sib))
        return out

    def chips(refs, pos):
        mc = pos[2]
        return [(refs["ps_" + k].at[i], refs["ld_" + k].at[i], (cx, cy, mc))
                for k in keys for i, (cx, cy) in enumerate(_other_chips(pos))]

    return _Stream(4 * len(keys), pair), _Stream(3 * len(keys), chips)


def _pair_sum(g, st, idx, tr, name):
    _, rows, cols = g.shape

    def body(idx_ref, g_ref, st_ref, out_ref):
        out_ref[...] = (g_ref[...].astype(F32) + st_ref[...].astype(F32)).astype(out_ref.dtype)

    return pl.pallas_call(
        body, name=name,
        grid_spec=pltpu.PrefetchScalarGridSpec(
            num_scalar_prefetch=1, grid=(4, rows // tr),
            in_specs=[pl.BlockSpec((None, tr, cols), lambda j, i, idx_ref: (idx_ref[j], i, 0)),
                      pl.BlockSpec((None, tr, cols), lambda j, i, idx_ref: (j, i, 0))],
            out_specs=pl.BlockSpec((None, tr, cols), lambda j, i, idx_ref: (j, i, 0))),
        out_shape=jax.ShapeDtypeStruct((4, rows, cols), WIRE_DTYPE),
        compiler_params=_params(dimension_semantics=("parallel", "parallel")),
    )(idx, g, st)


def _ada_fwd(c_all, w_ada, b_cols):
    def body(c_ref, w_ref, b_ref, out_ref):
        cv = c_ref[...]
        ca = cv * _sigmoid(cv)
        for l in range(DEPTH):
            out_ref[l] = jnp.dot(ca, w_ref[l], precision=HIGHEST, preferred_element_type=F32) + b_ref[l:l + 1, :]

    return pl.pallas_call(
        body, name="ada_fwd",
        out_shape=jax.ShapeDtypeStruct((DEPTH, N_DEV, ADA_COLS), F32),
        compiler_params=_params(),
    )(c_all, w_ada, b_cols)


def _ada_bwd(c_all, d_cols, name):
    def body(c_ref, d_ref, out_ref):
        cv = c_ref[...]
        ca = cv * _sigmoid(cv)
        out_ref[0] = lax.dot_general(ca, d_ref[...], (((0,), (0,)), ((), ())), precision=HIGHEST,
                                     preferred_element_type=F32)

    return pl.pallas_call(
        body, name=name,
        out_shape=jax.ShapeDtypeStruct((1, D_MODEL, ADA_COLS), F32),
        compiler_params=_params(),
    )(c_all, d_cols)


def _lower_bounds(logits):
    m = jnp.maximum(logits[0:1], logits[1:2])
    e0, e1 = jnp.exp(logits[0:1] - m), jnp.exp(logits[1:2] - m)
    den = e0 + e1
    p0, p1 = e0 / den, e1 / den
    low0 = p0 - p0
    low1 = (p0 + p1) - p0
    return (p0, p1), (low0, low1)


def _lb_fwd(lb_logits):
    def body(lg_ref, out_ref):
        _, (low0, low1) = _lower_bounds(lg_ref[...])
        out_ref[0:1, :] = jnp.clip(low0, 0.0, 1.0)
        out_ref[1:2, :] = jnp.clip(low1, 0.0, 1.0)

    return pl.pallas_call(body, name="lb_fwd", out_shape=jax.ShapeDtypeStruct(lb_logits.shape, F32),
                          compiler_params=_params())(lb_logits)


def _row_spec(cols=D_MODEL):
    return pl.BlockSpec((1, cols), lambda *_: (0, 0))


def _prenorm_fwd(x, g, shift, scale, tm, name, after=None):
    seq = x.shape[0]

    def body(x_ref, g_ref, sh_ref, sc_ref, h_ref):
        xv = x_ref[...]
        rs = lax.rsqrt(jnp.mean(xv * xv, axis=-1, keepdims=True) + NORM_EPS)
        h = (xv * rs * g_ref[...]) * (1.0 + sc_ref[...]) + sh_ref[...]
        h_ref[...] = h.astype(h_ref.dtype)

    tile = pl.BlockSpec((tm, D_MODEL), lambda i: (i, 0))
    return _pallas_after(
        body, 4, after, name=name, grid=(seq // tm,),
        in_specs=[tile, _row_spec(), _row_spec(), _row_spec()], out_specs=tile,
        out_shape=jax.ShapeDtypeStruct((seq, D_MODEL), MXU_DTYPE),
        compiler_params=_params(dimension_semantics=("parallel",)),
    )(x, g, shift, scale)


def _in_proj(h, win_g, tm, name, after=None):
    seq = h.shape[0]

    def body(h_ref, w_ref, z_ref, w_pair):
        @pl.when(pl.program_id(1) == 0)
        def _():
            w_pair[...] = jnp.concatenate([w_ref[0], w_ref[1]], axis=1)

        z_ref[...] = jnp.dot(h_ref[...], w_pair[...], preferred_element_type=F32)

    return _pallas_after(
        body, 2, after, name=name, grid=(N_DEV // 2, seq // tm),
        in_specs=[pl.BlockSpec((tm, D_MODEL), lambda j, i: (i, 0)),
                  pl.BlockSpec((2, D_MODEL, IN_COLS), lambda j, i: (j, 0, 0))],
        out_specs=pl.BlockSpec((tm, 2 * IN_COLS), lambda j, i: (i, j)),
        out_shape=jax.ShapeDtypeStruct((seq, IN_WIDTH), F32),
        scratch_shapes=[pltpu.VMEM((D_MODEL, 2 * IN_COLS), MXU_DTYPE)],
        compiler_params=_params(dimension_semantics=("parallel", "arbitrary")),
    )(h, win_g)


def _shift_down(v, j, pos):
    return jnp.where(pos >= j, pltpu.roll(v, j, 0), 0.0)


def _shift_up(v, j, pos, seq):
    return jnp.where(pos < seq - j, pltpu.roll(v, seq - j, 0), 0.0)


def _select_window(g, candidates):
    out = candidates[-1]
    for i in range(len(candidates) - 2, -1, -1):
        out = jnp.where(g == i, candidates[i], out)
    return out


def _pool_mean_minus_token(u, g, pos):
    sums, acc = [], u
    for j in (1, 2, 4, 8):
        acc = acc + _shift_down(acc, j, pos)
        sums.append(acc)
    wsum = _select_window(g, sums)
    width = jnp.left_shift(2, g).astype(F32)
    count = jnp.minimum(pos.astype(F32) + 1.0, width)
    return wsum / count - u, count


def _pool_mean_minus_token_static(u, i, pos):
    acc = u
    for j in (1, 2, 4, 8)[:i + 1]:
        acc = acc + _shift_down(acc, j, pos)
    inv = _inv_count(i, u.shape[0])
    return acc * inv - u, inv


def _inv_count(i, seq):
    head = lax.broadcasted_iota(jnp.int32, (SUB, GROUP_DIM), 0).astype(F32) + 1.0
    head = 1.0 / jnp.minimum(head, float(2 << i))
    tail = jnp.full((seq - SUB, GROUP_DIM), 1.0 / float(2 << i), F32)
    return jnp.concatenate([head, tail], axis=0)


def _per_group(fn):
    for i in range(POOL_GROUPS):
        pl.when(pl.program_id(0) == i)(lambda i=i: fn(i))


def _pool_fwd(z, pool_w_l, pool_scale_l, name, after=None):
    seq = z.shape[0]

    def body(pv_ref, pg_ref, w_ref, sc_ref, out_ref):
        def one_group(i):
            pos = lax.broadcasted_iota(jnp.int32, (seq, GROUP_DIM), 0)
            pm, _ = _pool_mean_minus_token_static(pv_ref[...], i, pos)
            lin = _dot(pm, w_ref[...]) * sc_ref[...]
            pg = pg_ref[...]
            out_ref[...] = (lin * (pg * _sigmoid(pg))).astype(out_ref.dtype)

        _per_group(one_group)

    return _pallas_after(
        body, 4, after, name=name, grid=(POOL_GROUPS,),
        in_specs=[pl.BlockSpec((seq, GROUP_DIM), lambda g: (0, g)),
                  pl.BlockSpec((seq, GROUP_DIM), lambda g: (0, POOL_GROUPS + g)),
                  pl.BlockSpec((None, GROUP_DIM, GROUP_DIM), lambda g: (g, 0, 0)),
                  pl.BlockSpec((1, GROUP_DIM), lambda g: (0, g))],
        out_specs=pl.BlockSpec((seq, GROUP_DIM), lambda g: (0, g)),
        out_shape=jax.ShapeDtypeStruct((seq, POOL_WIDTH), MXU_DTYPE),
        compiler_params=_params(dimension_semantics=("parallel",)),
    )(z, z, pool_w_l, pool_scale_l)


def _chunk_masks():
    row = lax.broadcasted_iota(jnp.int32, (CHUNK, CHUNK), 0)
    col = lax.broadcasted_iota(jnp.int32, (CHUNK, CHUNK), 1)
    causal = row >= col
    before_sub = col < (row // SUB) * SUB
    suffix = row <= col
    return causal, before_sub, suffix


def _masked_sums(masks, v):
    lhs = jnp.concatenate([m.astype(jnp.bfloat16) for m in masks], axis=0)
    hi = v.astype(jnp.bfloat16)
    rest = v - hi.astype(F32)
    mid = rest.astype(jnp.bfloat16)
    lo = (rest - mid.astype(F32)).astype(jnp.bfloat16)
    out = jnp.dot(lhs, hi, preferred_element_type=F32)
    out += jnp.dot(lhs, mid, preferred_element_type=F32)
    out += jnp.dot(lhs, lo, preferred_element_type=F32)
    return [out[i * CHUNK:(i + 1) * CHUNK] for i in range(len(masks))]


def _gates(zf, lb):
    sg = _sigmoid(zf)
    f = lb + (1.0 - lb) * sg
    logf = jnp.log(jnp.maximum(f, LOG_FLOOR))
    return sg, f, logf


def _intra_blocks(q_h, k_h, cum_h, base_h, causal):
    rel = cum_h - base_h
    out = []
    for i in range(N_SUB):
        rows = slice(i * SUB, (i + 1) * SUB)
        e_q = jnp.exp(rel[rows])
        base_i = jnp.concatenate([base_h[rows]] * N_SUB, axis=0)
        e_k = jnp.exp(jnp.minimum(base_i - cum_h, EXP_CLAMP))
        q_t = (q_h[rows] * e_q).astype(MXU_DTYPE)
        k_t = (k_h * e_k).astype(MXU_DTYPE)
        a_i = jnp.where(causal[rows], _dot_nt(q_t, k_t), 0.0)
        out.append((q_t, k_t, e_q, e_k, a_i))
    return out


def _hgrn_fwd(z, lb_l, gn_l, name, after=None):
    seq = z.shape[0]
    n_chunks = seq // CHUNK
    per_step = min(FWD_STEP_CHUNKS, n_chunks)
    rows_per_step = per_step * CHUNK

    def body(hq_ref, hf_ref, hi_ref, hg_ref, lb_ref, gn_ref, o_ref, bin_ref, st_ref, cb_ref, state):
        @pl.when(pl.program_id(0) == 0)
        def _():
            state[...] = jnp.zeros_like(state)

        causal, before_sub, _ = _chunk_masks()
        for cc in range(per_step):
            rows = slice(cc * CHUNK, (cc + 1) * CHUNK)
            _, f, logf = _gates(hf_ref[rows, :], lb_ref[...])
            kk = 1.0 - f
            hq = hq_ref[rows, :]
            q = hq * _sigmoid(hq)
            cum, base = _masked_sums([causal, before_sub], logf)
            cb_ref[rows, 0:D_MODEL] = cum
            cb_ref[rows, D_MODEL:2 * D_MODEL] = base
            st_ref[cc] = state[...]
            for h in range(HEADS):
                sl = slice(h * HEAD_DIM, (h + 1) * HEAD_DIM)
                q_h, k_h, cum_h = q[:, sl], kk[:, sl], cum[:, sl]
                v_h = hi_ref[rows, sl]
                st_h = state[h]
                blocks = _intra_blocks(q_h, k_h, cum_h, base[:, sl], causal)
                a = jnp.concatenate([b[4] for b in blocks], axis=0)
                o_h = _dot_nt(q_h * jnp.exp(cum_h), st_h) + _dot(a, v_h)
                last = jnp.sum(logf[:, sl], axis=0, keepdims=True)
                state[h] = st_h * jnp.exp(last) + _dot_tn(v_h, k_h * jnp.exp(last - cum_h))
                rs = lax.rsqrt(jnp.mean(o_h * o_h, axis=-1, keepdims=True) + NORM_EPS)
                hg = hg_ref[rows, sl]
                o_ref[rows, sl] = o_h
                bin_ref[rows, sl] = ((o_h * rs * gn_ref[...]) * (hg * _sigmoid(hg))).astype(bin_ref.dtype)

    def col(block):
        return pl.BlockSpec((rows_per_step, D_MODEL), lambda c: (c, block))

    tile = pl.BlockSpec((rows_per_step, D_MODEL), lambda c: (c, 0))
    return _pallas_after(
        body, 6, after, name=name, grid=(n_chunks // per_step,),
        in_specs=[col(COL_HQ), col(COL_HF), col(COL_HI), col(COL_HG), _row_spec(), _row_spec(HEAD_DIM)],
        out_specs=[tile, tile, pl.BlockSpec((per_step, HEADS, HEAD_DIM, HEAD_DIM), lambda c: (c, 0, 0, 0)),
                   pl.BlockSpec((rows_per_step, 2 * D_MODEL), lambda c: (c, 0))],
        out_shape=[jax.ShapeDtypeStruct((seq, D_MODEL), F32),
                   jax.ShapeDtypeStruct((seq, D_MODEL), MXU_DTYPE),
                   jax.ShapeDtypeStruct((n_chunks, HEADS, HEAD_DIM, HEAD_DIM), F32),
                   jax.ShapeDtypeStruct((seq, 2 * D_MODEL), F32)],
        scratch_shapes=[pltpu.VMEM((HEADS, HEAD_DIM, HEAD_DIM), F32)],
        compiler_params=_params(dimension_semantics=("arbitrary",)),
    )(z, z, z, z, lb_l, gn_l)


def _rms_parts(y):
    rs = lax.rsqrt(jnp.mean(y * y, axis=-1, keepdims=True) + NORM_EPS)
    return rs, y * rs


def _merge_fwd(a_in, b_in, z, x, wpo_g, who_g, wout_g, gate, g_post, tm, name, target=None):
    seq = x.shape[0]
    with_loss = target is not None

    def body(*refs):
        a_ref, b_ref, mgp_ref, mgh_ref, x_ref, wpo_ref, who_ref, wout_ref, gate_ref, gp_ref = refs[:10]
        ba_ref, bb_ref, mer_ref, y_ref, last_ref = refs[10 + with_loss:15 + with_loss]
        a = a_ref[...]
        ba = _dot(a, jnp.concatenate([wpo_ref[j] for j in range(N_DEV)], axis=1))
        bb = _dot(b_ref[...], who_ref[...])
        merged = _sigmoid(mgp_ref[...]) * ba + _sigmoid(mgh_ref[...]) * bb
        y = _dot(merged, wout_ref[...])
        _, yn = _rms_parts(y)
        ba_ref[...] = ba.astype(ba_ref.dtype)
        bb_ref[...] = bb.astype(bb_ref.dtype)
        mer_ref[...] = merged.astype(mer_ref.dtype)
        y_ref[...] = y.astype(y_ref.dtype)
        x_next = x_ref[...] + gate_ref[...] * (yn * gp_ref[...])
        if not with_loss:
            last_ref[...] = x_next
            return
        loss_ref = refs[16]

        @pl.when(pl.program_id(0) == 0)
        def _():
            loss_ref[...] = jnp.zeros_like(loss_ref)

        err = x_next - refs[10][...]
        loss_ref[...] += 0.5 * jnp.sum(jnp.mean(err * err, axis=-1, keepdims=True), axis=0, keepdims=True)
        last_ref[...] = err * (1.0 / D_MODEL)

    def tile(cols=D_MODEL, block=0):
        return pl.BlockSpec((tm, cols), lambda i: (i, block))

    full = pl.BlockSpec((D_MODEL, D_MODEL), lambda i: (0, 0))
    act = jax.ShapeDtypeStruct((seq, D_MODEL), MXU_DTYPE)
    f32 = jax.ShapeDtypeStruct((seq, D_MODEL), F32)
    one = [pl.BlockSpec((1, 1), lambda i: (0, 0))] if with_loss else []
    return pl.pallas_call(
        body, name=name, grid=(seq // tm,),
        in_specs=[tile(POOL_WIDTH), tile(), tile(block=COL_MGP), tile(block=COL_MGH), tile(),
                  pl.BlockSpec((N_DEV, POOL_WIDTH, GROUP_DIM), lambda i: (0, 0, 0)),
                  full, full, _row_spec(), _row_spec()] + ([tile()] if with_loss else []),
        out_specs=[tile(), tile(), tile(), tile(), tile()] + one,
        out_shape=[act, act, act, act, f32] + ([jax.ShapeDtypeStruct((1, 1), F32)] if with_loss else []),
        compiler_params=_params(dimension_semantics=("arbitrary" if with_loss else "parallel",)),
    )(a_in, b_in, z, z, x, wpo_g, who_g, wout_g, gate, g_post, *([target] if with_loss else []))


def _stage_copy(stage, sems, dst, slot, step, where):
    rows, cols = where(step)
    return pltpu.make_async_copy(stage.at[slot], dst.at[rows, cols], sems.at[slot])


def _stage_begin(stage, sems, dst, step, where):
    slot = step % 2

    @pl.when(step >= 2)
    def _():
        _stage_copy(stage, sems, dst, slot, step - 2, where).wait()

    return slot


def _stage_end(stage, sems, dst, step, n_steps, where):
    slot = step % 2
    _stage_copy(stage, sems, dst, slot, step, where).start()

    @pl.when(step == n_steps - 1)
    def _():
        _stage_copy(stage, sems, dst, slot, step, where).wait()
        if n_steps > 1:
            _stage_copy(stage, sems, dst, 1 - slot, step - 1, where).wait()


def _merge_bwd(dx, y, ba, bb, z, wpo_g, who_g, wout_g, gate, g_post, dz, tm, name):
    seq = dx.shape[0]
    n_steps = seq // tm

    def body(dx_ref, y_ref, ba_ref, bb_ref, mgp_ref, mgh_ref, wpo_ref, who_ref, wout_ref, gate_ref, gp_ref, _,
             dy_ref, dba_ref, dbb_ref, da_ref, db_ref, dz_ref, acc_ref, stage, sems):
        step = pl.program_id(0)

        @pl.when(step == 0)
        def _():
            acc_ref[...] = jnp.zeros_like(acc_ref)

        def where(t):
            return pl.ds(t * tm, tm), pl.ds(COL_MGP * D_MODEL, 2 * D_MODEL)

        dmg_ref = stage.at[_stage_begin(stage, sems, dz_ref, step, where)]

        dxv = dx_ref[...]
        rs, yn = _rms_parts(y_ref[...].astype(F32))
        acc_ref[0:1, :] += jnp.sum(dxv * yn * gp_ref[...], axis=0, keepdims=True)
        acc_ref[1:2, :] += jnp.sum(dxv * gate_ref[...] * yn, axis=0, keepdims=True)
        dyn = dxv * (gate_ref[...] * gp_ref[...])
        dy = rs * (dyn - yn * jnp.mean(dyn * yn, axis=-1, keepdims=True))
        dmerged = _dot_nt(dy, wout_ref[...])
        sp, sh = _sigmoid(mgp_ref[...]), _sigmoid(mgh_ref[...])
        dba, dbb = sp * dmerged, sh * dmerged
        dmg_ref[:, 0:D_MODEL] = (dmerged * ba_ref[...].astype(F32) * sp * (1.0 - sp)).astype(dmg_ref.dtype)
        dmg_ref[:, D_MODEL:2 * D_MODEL] = (dmerged * bb_ref[...].astype(F32) * sh * (1.0 - sh)).astype(dmg_ref.dtype)
        da = _dot_nt(dba, jnp.concatenate([wpo_ref[j] for j in range(N_DEV)], axis=1))
        dy_ref[...] = dy.astype(dy_ref.dtype)
        dba_ref[...] = dba.astype(dba_ref.dtype)
        dbb_ref[...] = dbb.astype(dbb_ref.dtype)
        da_ref[...] = da.astype(da_ref.dtype)
        db_ref[...] = _dot_nt(dbb, who_ref[...]).astype(db_ref.dtype)
        _stage_end(stage, sems, dz_ref, step, n_steps, where)

    def tile(cols=D_MODEL, block=0):
        return pl.BlockSpec((tm, cols), lambda i: (i, block))

    full = pl.BlockSpec((D_MODEL, D_MODEL), lambda i: (0, 0))
    hbm = pl.BlockSpec(memory_space=pl.ANY)
    act = jax.ShapeDtypeStruct((seq, D_MODEL), MXU_DTYPE)
    return pl.pallas_call(
        body, name=name, grid=(n_steps,),
        in_specs=[tile(), tile(), tile(), tile(), tile(block=COL_MGP), tile(block=COL_MGH),
                  pl.BlockSpec((N_DEV, POOL_WIDTH, GROUP_DIM), lambda i: (0, 0, 0)),
                  full, full, _row_spec(), _row_spec(), hbm],
        out_specs=[tile(), tile(), tile(), tile(POOL_WIDTH), tile(), hbm,
                   pl.BlockSpec((8, D_MODEL), lambda i: (0, 0))],
        out_shape=[act, act, act, jax.ShapeDtypeStruct((seq, POOL_WIDTH), MXU_DTYPE), act,
                   jax.ShapeDtypeStruct(dz.shape, dz.dtype),
                   jax.ShapeDtypeStruct((8, D_MODEL), F32)],
        input_output_aliases={11: 5},
        scratch_shapes=[pltpu.VMEM((2, tm, 2 * D_MODEL), MXU_DTYPE), pltpu.SemaphoreType.DMA((2,))],
        compiler_params=_params(dimension_semantics=("arbitrary",)),
    )(dx, y, ba, bb, z, z, wpo_g, who_g, wout_g, gate, g_post, dz)


def _grad_out_weights(merged, dy, b_in, dbb, a_in, dba, name):
    seq = merged.shape[0]
    tn = D_MODEL // 2
    per_step = tn // GROUP_DIM

    def body(mer_ref, dy_ref, b_ref, dbb_ref, a_ref, dba_ref, gout_ref, gho_ref, gpo_ref):
        gout_ref[...] = _dot_tn(mer_ref[...], dy_ref[...]).astype(gout_ref.dtype)
        gho_ref[...] = _dot_tn(b_ref[...], dbb_ref[...]).astype(gho_ref.dtype)
        g_po = _dot_tn(a_ref[...], dba_ref[...])
        for j in range(per_step):
            gpo_ref[j] = g_po[:, j * GROUP_DIM:(j + 1) * GROUP_DIM].astype(gpo_ref.dtype)

    def whole(cols):
        return pl.BlockSpec((seq, cols), lambda j: (0, 0))

    cols = pl.BlockSpec((seq, tn), lambda j: (0, j))
    return pl.pallas_call(
        body, name=name, grid=(D_MODEL // tn,),
        in_specs=[whole(D_MODEL), cols, whole(D_MODEL), cols, whole(POOL_WIDTH), cols],
        out_specs=[pl.BlockSpec((D_MODEL, tn), lambda j: (0, j)), pl.BlockSpec((D_MODEL, tn), lambda j: (0, j)),
                   pl.BlockSpec((per_step, POOL_WIDTH, GROUP_DIM), lambda j: (j, 0, 0))],
        out_shape=[jax.ShapeDtypeStruct((D_MODEL, D_MODEL), WIRE_DTYPE),
                   jax.ShapeDtypeStruct((D_MODEL, D_MODEL), WIRE_DTYPE),
                   jax.ShapeDtypeStruct((N_DEV, POOL_WIDTH, GROUP_DIM), WIRE_DTYPE)],
        compiler_params=_params(dimension_semantics=("parallel",)),
    )(merged, dy, b_in, dbb, a_in, dba)


def _hgrn_bwd(db_in, z, o, states, cum_base, lb_l, gn_l, dz, name, after=None):
    seq = z.shape[0]
    per_step = min(BWD_STEP_CHUNKS, seq // CHUNK)
    rows_per_step = per_step * CHUNK
    n_steps = seq // rows_per_step
    last_step = n_steps - 1

    def body(db_ref, hq_ref, hf_ref, hi_ref, hg_ref, o_ref, st_ref, cb_ref, lb_ref, gn_ref, _,
             dz_hbm, dlb_ref, dgn_ref, dstate, dq_buf, dk_buf, dg_buf, stage, sems):
        step = pl.program_id(0)

        @pl.when(step == 0)
        def _():
            dstate[...] = jnp.zeros_like(dstate)
            dlb_ref[...] = jnp.zeros_like(dlb_ref)
            dgn_ref[...] = jnp.zeros_like(dgn_ref)

        def one_chunk(cc, *args):
            one_chunk_body((db_ref, hq_ref, hf_ref, hi_ref, hg_ref, o_ref, st_ref, cb_ref, dlb_ref, dgn_ref, dstate,
                            dq_buf, dk_buf, dg_buf), cc, *args)

        def where(t):
            return pl.ds((last_step - t) * rows_per_step, rows_per_step), pl.ds(COL_HQ * D_MODEL, 4 * D_MODEL)

        dz_step = stage.at[_stage_begin(stage, sems, dz_hbm, step, where)]
        causal, before_sub, suffix = _chunk_masks()
        lb = lb_ref[...]
        gn = gn_ref[...]
        for cc in reversed(range(per_step)):
            one_chunk(cc, dz_step, causal, before_sub, suffix, lb, gn)
        _stage_end(stage, sems, dz_hbm, step, n_steps, where)

    def one_chunk_body(refs, cc, dz_step, causal, before_sub, suffix, lb, gn):
        (db_ref, hq_ref, hf_ref, hi_ref, hg_ref, o_ref, st_ref, cb_ref, dlb_ref, dgn_ref, dstate,
         dq_buf, dk_buf, dg_buf) = refs
        rows = slice(cc * CHUNK, (cc + 1) * CHUNK)
        dz_ref = dz_step.at[rows, :]
        dq_buf, dk_buf, dg_buf = dq_buf.at[cc], dk_buf.at[cc], dg_buf.at[cc]
        sg, f, logf = _gates(hf_ref[rows, :], lb)
        kk = 1.0 - f
        hq = hq_ref[rows, :]
        sq = _sigmoid(hq)
        q = hq * sq
        cum, base = cb_ref[rows, 0:D_MODEL], cb_ref[rows, D_MODEL:2 * D_MODEL]
        dgn = jnp.zeros((1, HEAD_DIM), F32)
        dlast = []
        for h in range(HEADS):
            sl = slice(h * HEAD_DIM, (h + 1) * HEAD_DIM)
            q_h, k_h, cum_h = q[:, sl], kk[:, sl], cum[:, sl]
            v_h = hi_ref[rows, sl]
            st_h = st_ref[cc, h]
            dst_h = dstate[h]
            rs, ohat = _rms_parts(o_ref[rows, sl])
            hg = hg_ref[rows, sl]
            shg = _sigmoid(hg)
            d_bin = db_ref[rows, sl].astype(F32)
            don = d_bin * (hg * shg)
            dgn += jnp.sum(don * ohat, axis=0, keepdims=True)
            dohat = don * gn
            do = rs * (dohat - ohat * jnp.mean(dohat * ohat, axis=-1, keepdims=True))
            dz_ref[:, 3 * D_MODEL + h * HEAD_DIM:3 * D_MODEL + (h + 1) * HEAD_DIM] = (
                d_bin * (ohat * gn) * _dsilu(hg, shg)).astype(dz_ref.dtype)
            last = cb_ref[(cc + 1) * CHUNK - 1:(cc + 1) * CHUNK, sl]
            g_in = jnp.exp(cum_h)
            d_out = jnp.exp(last - cum_h)
            q_bar, k_bar = q_h * g_in, k_h * d_out
            blocks = _intra_blocks(q_h, k_h, cum_h, base[:, sl], causal)
            a = jnp.concatenate([b[4] for b in blocks], axis=0)
            da = jnp.where(causal, _dot_nt(do, v_h), 0.0)
            dv = _dot_tn(a, do) + _dot_nt(k_bar, dst_h)
            dq_bar, dk_bar = _dot(do, st_h), _dot(v_h, dst_h)
            dk = dk_bar * d_out
            dq_parts, dg_parts = [], []
            dg_k = k_bar * dk_bar
            dlast.append(jnp.sum(k_bar * dk_bar, axis=0, keepdims=True)
                         + jnp.exp(last) * jnp.sum(st_h * dst_h, axis=0, keepdims=True))
            for i, (q_t, k_t, e_q, e_k, _) in enumerate(blocks):
                da_i = da[i * SUB:(i + 1) * SUB].astype(MXU_DTYPE)
                dq_t = _dot(da_i, k_t)
                dk_t = _dot_tn(da_i, q_t)
                dq_parts.append(dq_t * e_q)
                dk += dk_t * e_k
                dg_parts.append(q_t.astype(F32) * dq_t)
                dg_k += k_t.astype(F32) * dk_t
            dq = dq_bar * g_in + jnp.concatenate(dq_parts, axis=0)
            dg_buf[:, sl] = q_bar * dq_bar + jnp.concatenate(dg_parts, axis=0) - dg_k
            dstate[h] = dst_h * jnp.exp(last) + _dot_tn(do, q_bar)
            dq_buf[:, sl] = dq
            dk_buf[:, sl] = dk
            dz_ref[:, 2 * D_MODEL + h * HEAD_DIM:2 * D_MODEL + (h + 1) * HEAD_DIM] = dv.astype(dz_ref.dtype)
        dgn_ref[...] += dgn
        dq_all, dk_all = dq_buf[...], dk_buf[...]
        dlogf = _masked_sums([suffix], dg_buf[...])[0] + jnp.concatenate(dlast, axis=1)
        df = jnp.where(f > LOG_FLOOR, dlogf / f, 0.0) - dk_all
        dlb_ref[...] += jnp.sum(df * (1.0 - sg), axis=0, keepdims=True)
        dz_ref[:, 0:D_MODEL] = (dq_all * _dsilu(hq, sq)).astype(dz_ref.dtype)
        dz_ref[:, D_MODEL:2 * D_MODEL] = (df * (1.0 - lb) * sg * (1.0 - sg)).astype(dz_ref.dtype)

    def col(block):
        return pl.BlockSpec((rows_per_step, D_MODEL), lambda c: (last_step - c, block))

    hbm = pl.BlockSpec(memory_space=pl.ANY)
    return _pallas_after(
        body, 11, after, name=name, grid=(n_steps,),
        in_specs=[col(0), col(COL_HQ), col(COL_HF), col(COL_HI), col(COL_HG), col(0),
                  pl.BlockSpec((per_step, HEADS, HEAD_DIM, HEAD_DIM), lambda c: (last_step - c, 0, 0, 0)),
                  pl.BlockSpec((rows_per_step, 2 * D_MODEL), lambda c: (last_step - c, 0)),
                  _row_spec(), _row_spec(HEAD_DIM), hbm],
        out_specs=[hbm, _row_spec(), _row_spec(HEAD_DIM)],
        out_shape=[jax.ShapeDtypeStruct(dz.shape, dz.dtype),
                   jax.ShapeDtypeStruct((1, D_MODEL), F32), jax.ShapeDtypeStruct((1, HEAD_DIM), F32)],
        input_output_aliases={10: 0},
        scratch_shapes=[pltpu.VMEM((HEADS, HEAD_DIM, HEAD_DIM), F32)]
        + [pltpu.VMEM((per_step, CHUNK, D_MODEL), F32)] * 3
        + [pltpu.VMEM((2, rows_per_step, 4 * D_MODEL), MXU_DTYPE), pltpu.SemaphoreType.DMA((2,))],
        compiler_params=_params(dimension_semantics=("arbitrary",)),
    )(db_in, z, z, z, z, o, states, cum_base, lb_l, gn_l, dz)


def _pool_bwd(da_in, z, pool_w_l, pool_scale_l, dz, name, after=None):
    seq = z.shape[0]

    def body(da_ref, pv_ref, pg_ref, w_ref, sc_ref, _, dz_hbm, dw_ref, dsc_ref, stage_pv, stage_pg, sems_pv, sems_pg):
        g = pl.program_id(0)

        def where_pv(t):
            return pl.ds(0, seq), pl.ds(pl.multiple_of(t * GROUP_DIM, GROUP_DIM), GROUP_DIM)

        def where_pg(t):
            return pl.ds(0, seq), pl.ds(pl.multiple_of(POOL_WIDTH + t * GROUP_DIM, GROUP_DIM), GROUP_DIM)

        dpv_ref = stage_pv.at[_stage_begin(stage_pv, sems_pv, dz_hbm, g, where_pv)]
        dpg_ref = stage_pg.at[_stage_begin(stage_pg, sems_pg, dz_hbm, g, where_pg)]

        def one_group(i):
            pos = lax.broadcasted_iota(jnp.int32, (seq, GROUP_DIM), 0)
            pm, inv = _pool_mean_minus_token_static(pv_ref[...], i, pos)
            lin0 = _dot(pm, w_ref[...])
            pg = pg_ref[...]
            spg = _sigmoid(pg)
            da = da_ref[...].astype(F32)
            dlin = da * (pg * spg)
            dpg_ref[...] = (da * (lin0 * sc_ref[...]) * _dsilu(pg, spg)).astype(dpg_ref.dtype)
            dsc_ref[...] = jnp.sum(dlin * lin0, axis=0, keepdims=True)
            dl0 = dlin * sc_ref[...]
            dw_ref[...] = _dot_tn(pm, dl0)
            dpm = _dot_nt(dl0, w_ref[...])
            acc = dpm * inv
            for j in (1, 2, 4, 8)[:i + 1]:
                acc = acc + _shift_up(acc, j, pos, seq)
            dpv_ref[...] = (acc - dpm).astype(dpv_ref.dtype)

        _per_group(one_group)
        _stage_end(stage_pv, sems_pv, dz_hbm, g, POOL_GROUPS, where_pv)
        _stage_end(stage_pg, sems_pg, dz_hbm, g, POOL_GROUPS, where_pg)

    grp = pl.BlockSpec((seq, GROUP_DIM), lambda g: (0, g))
    hbm = pl.BlockSpec(memory_space=pl.ANY)
    stage = pltpu.VMEM((2, seq, GROUP_DIM), MXU_DTYPE)
    return _pallas_after(
        body, 6, after, name=name, grid=(POOL_GROUPS,),
        in_specs=[grp, grp, pl.BlockSpec((seq, GROUP_DIM), lambda g: (0, POOL_GROUPS + g)),
                  pl.BlockSpec((None, GROUP_DIM, GROUP_DIM), lambda g: (g, 0, 0)),
                  pl.BlockSpec((1, GROUP_DIM), lambda g: (0, g)), hbm],
        out_specs=[hbm, pl.BlockSpec((None, GROUP_DIM, GROUP_DIM), lambda g: (g, 0, 0)),
                   pl.BlockSpec((1, GROUP_DIM), lambda g: (0, g))],
        out_shape=[jax.ShapeDtypeStruct(dz.shape, dz.dtype),
                   jax.ShapeDtypeStruct((POOL_GROUPS, GROUP_DIM, GROUP_DIM), F32),
                   jax.ShapeDtypeStruct((1, POOL_WIDTH), F32)],
        input_output_aliases={5: 0},
        scratch_shapes=[stage, stage, pltpu.SemaphoreType.DMA((2,)), pltpu.SemaphoreType.DMA((2,))],
        compiler_params=_params(dimension_semantics=("arbitrary",)),
    )(da_in, z, z, pool_w_l, pool_scale_l, dz)


def _in_proj_dw(h, dz, name, after=None):
    seq = h.shape[0]

    def body(h_ref, dz_ref, out_ref):
        pair = lax.dot_general(h_ref[...], dz_ref[...], (((0,), (0,)), ((), ())), preferred_element_type=F32)
        out_ref[0] = pair[:, 0:IN_COLS].astype(out_ref.dtype)
        out_ref[1] = pair[:, IN_COLS:].astype(out_ref.dtype)

    return _pallas_after(
        body, 2, after, name=name, grid=(N_DEV // 2,),
        in_specs=[pl.BlockSpec((seq, D_MODEL), lambda j: (0, 0)),
                  pl.BlockSpec((seq, 2 * IN_COLS), lambda j: (0, j))],
        out_specs=pl.BlockSpec((2, D_MODEL, IN_COLS), lambda j: (j, 0, 0)),
        out_shape=jax.ShapeDtypeStruct((N_DEV, D_MODEL, IN_COLS), WIRE_DTYPE),
        compiler_params=_params(dimension_semantics=("parallel",)),
    )(h, dz)


def _in_proj_dh(dz, win_g, tm, name, after=None):
    seq = dz.shape[0]

    def body(dz_ref, w_ref, dh_ref):
        @pl.when(pl.program_id(1) == 0)
        def _():
            dh_ref[...] = jnp.zeros_like(dh_ref)

        w_pair = jnp.concatenate([w_ref[0], w_ref[1]], axis=1)
        dh_ref[...] += lax.dot_general(dz_ref[...], w_pair, (((1,), (1,)), ((), ())), preferred_element_type=F32)

    return _pallas_after(
        body, 2, after, name=name, grid=(seq // tm, N_DEV // 2),
        in_specs=[pl.BlockSpec((tm, 2 * IN_COLS), lambda i, j: (i, j)),
                  pl.BlockSpec((2, D_MODEL, IN_COLS), lambda i, j: (j, 0, 0))],
        out_specs=pl.BlockSpec((tm, D_MODEL), lambda i, j: (i, 0)),
        out_shape=jax.ShapeDtypeStruct((seq, D_MODEL), F32),
        compiler_params=_params(dimension_semantics=("parallel", "arbitrary")),
    )(dz, win_g)


def _prenorm_bwd(x, dh, dx_res, g, scale, tm, name, after=None):
    seq = x.shape[0]

    def body(x_ref, dh_ref, dxr_ref, g_ref, sc_ref, dx_ref, acc_ref):
        @pl.when(pl.program_id(0) == 0)
        def _():
            acc_ref[...] = jnp.zeros_like(acc_ref)

        rs, xn = _rms_parts(x_ref[...])
        dh = dh_ref[...]
        acc_ref[0:1, :] += jnp.sum(dh, axis=0, keepdims=True)
        acc_ref[1:2, :] += jnp.sum(dh * (xn * g_ref[...]), axis=0, keepdims=True)
        dhn = dh * (1.0 + sc_ref[...])
        acc_ref[2:3, :] += jnp.sum(dhn * xn, axis=0, keepdims=True)
        dxn = dhn * g_ref[...]
        dx_ref[...] = rs * (dxn - xn * jnp.mean(dxn * xn, axis=-1, keepdims=True)) + dxr_ref[...]

    tile = pl.BlockSpec((tm, D_MODEL), lambda i: (i, 0))
    return _pallas_after(
        body, 5, after, name=name, grid=(seq // tm,),
        in_specs=[tile, tile, tile, _row_spec(), _row_spec()],
        out_specs=[tile, pl.BlockSpec((8, D_MODEL), lambda i: (0, 0))],
        out_shape=[jax.ShapeDtypeStruct((seq, D_MODEL), F32), jax.ShapeDtypeStruct((8, D_MODEL), F32)],
        compiler_params=_params(dimension_semantics=("arbitrary",)),
    )(x, dh, dx_res, g, scale)


def _adamw_math(w, g, m, v):
    m = ADAM_B1 * m + (1.0 - ADAM_B1) * g
    v = ADAM_B2 * v + (1.0 - ADAM_B2) * (g * g)
    m_hat = m / (1.0 - ADAM_B1 ** ADAM_STEP)
    v_hat = v / (1.0 - ADAM_B2 ** ADAM_STEP)
    delta = -ADAM_LR * (m_hat / (jnp.sqrt(v_hat) + ADAM_EPS) + ADAM_WD * w)
    return delta, m, v


def _adamw_layer(w, m, v, contribs, l, tr, name, prev=None):
    _, rows, cols = w.shape
    n = len(contribs)

    def body(*refs):
        w_ref, m_ref, v_ref = refs[:3]
        c_refs = refs[3:3 + n]
        g_ref, d_ref, mo_ref, vo_ref = refs[-4:]
        g = c_refs[0][...].astype(F32)
        for c_ref in c_refs[1:]:
            g += c_ref[...].astype(F32)
        delta, mn, vn = _adamw_math(w_ref[...], g, m_ref[...], v_ref[...])
        g_ref[...] = g
        d_ref[...] = delta
        mo_ref[...] = mn
        vo_ref[...] = vn

    tile = pl.BlockSpec((None, tr, cols), lambda i: (l, i, 0))
    in_specs = [tile, tile, tile] + [pl.BlockSpec((None, tr, cols), lambda i, s=slot: (s, i, 0)) for _, slot in contribs]
    operands = [w, m, v] + [arr for arr, _ in contribs]
    aliases = {}
    if prev is not None:
        aliases = {len(operands) + k: k for k in range(4)}
        in_specs += [pl.BlockSpec(memory_space=pl.ANY)] * 4
        operands += list(prev)
    shape = jax.ShapeDtypeStruct(w.shape, F32)
    return pl.pallas_call(
        body, name=name, grid=(rows // tr,), in_specs=in_specs, out_specs=[tile] * 4, out_shape=[shape] * 4,
        input_output_aliases=aliases,
        compiler_params=_params(dimension_semantics=("parallel",)),
    )(*operands)


def _adamw_small(w_pack, m_pack, v_pack, g_late, g_early, shapes):
    pieces, r = {}, 0
    for name, _, n in _SMALL_ROWS:
        pieces.setdefault(name, []).append((r, n))
        r += n
    names = list(pieces)

    def body(w_ref, m_ref, v_ref, gl_ref, ge_ref, *rest):
        outs, packs = rest[:4 * len(names)], rest[4 * len(names):]
        g_l, g_e = gl_ref[0][0:SMALL_LATE_ROWS], ge_ref[0]
        for d in range(1, N_DEV):
            g_l += gl_ref[d][0:SMALL_LATE_ROWS]
            g_e += ge_ref[d]
        g = jnp.concatenate([g_l, g_e], axis=0)
        w = w_ref[...]
        r0, r1, r2 = LB_ROW0, LB_ROW0 + 8, LB_ROW0 + 16
        lg0, lg1 = w[r0:r1], w[r1:r2]
        mx = jnp.maximum(lg0, lg1)
        e0, e1 = jnp.exp(lg0 - mx), jnp.exp(lg1 - mx)
        p0, p1 = e0 / (e0 + e1), e1 / (e0 + e1)
        low = ((p0 - p0), (p0 + p1) - p0)
        dlow = [g_rows * jnp.where((lo > 0.0) & (lo < 1.0), 1.0, jnp.where((lo == 0.0) | (lo == 1.0), 0.5, 0.0))
                for g_rows, lo in ((g[r0:r1], low[0]), (g[r1:r2], low[1]))]
        dp0 = (dlow[0] + dlow[1]) - (dlow[0] + dlow[1])
        dp1 = dlow[1]
        inner = p0 * dp0 + p1 * dp1
        g = jnp.concatenate([g[:r0], p0 * (dp0 - inner), p1 * (dp1 - inner), g[r2:]], axis=0)
        delta, mn, vn = _adamw_math(w, g, m_ref[...], v_ref[...])
        for kind, val in enumerate((g, delta, mn, vn)):
            packs[kind][...] = val
            for j, name in enumerate(names):
                out, at = outs[kind * len(names) + j], 0
                for start, n in pieces[name]:
                    if name in flat:
                        for r in range(n):
                            layer, c = divmod(at + r, flat[name])
                            out[layer:layer + 1, c * 128:(c + 1) * 128] = packs[kind][start + r:start + r + 1, :]
                    else:
                        out[at:at + n, :] = packs[kind][start:start + n, :]
                    at += n

    rows = {name: sum(n for _, n in pieces[name]) for name in names}
    flat = {name: rows[name] // DEPTH for name in names if len(shapes[name]) == 2}
    outs = pl.pallas_call(
        body, name="adamw_small",
        out_shape=[jax.ShapeDtypeStruct(shapes[name] if name in flat else (rows[name], 128), F32)
                   for _ in range(4) for name in names],
        scratch_shapes=[pltpu.VMEM(w_pack.shape, F32)] * 4, compiler_params=_params(),
    )(w_pack, m_pack, v_pack, g_late, g_early)
    return [{name: outs[kind * len(names) + j].reshape(shapes[name]) for j, name in enumerate(names)}
            for kind in range(4)]


def _pack_small(parts, first=0, last=len(_SMALL_ROWS)):
    rows = [(parts[name] if l is None else parts[name][l]).reshape(n, 128) for name, l, n in _SMALL_ROWS[first:last]]
    if last == len(_SMALL_ROWS):
        rows.append(jnp.zeros((SMALL_ROWS_PAD - sum(n for _, _, n in _SMALL_ROWS), 128), F32))
    return jnp.concatenate(rows, axis=0)


def kernel(x, c, w_ada, b_ada, g_pre, g_post, w_in, pool_w, pool_scale, lb_logits, hgrn_norm_g, w_pool_o, w_hgrn_o, w_out, loss_target, m_w_ada, m_b_ada, m_g_pre, m_g_post, m_w_in, m_pool_w, m_pool_scale, m_lb_logits, m_hgrn_norm_g, m_w_pool_o, m_w_hgrn_o, m_w_out, v_w_ada, v_b_ada, v_g_pre, v_g_post, v_w_in, v_pool_w, v_pool_scale, v_lb_logits, v_hgrn_norm_g, v_w_pool_o, v_w_hgrn_o, v_w_out):
    seq = x.shape[1]
    tm = min(1024, seq)
    tm_merge = min(512, seq)
    pos = _my_position()
    me = pos[3]

    big = dict(win=w_in, wpo=w_pool_o, who=w_hgrn_o, wout=w_out)
    units = [["win0"], ["wpo0", "who0", "wout0"], ["win1", "wpo1", "who1", "wout1"]]
    g_streams = [_gather_streams(keys) for keys in units]
    g_state = [None] * len(units)

    def gather_start(us, after, first=None):
        bufs = dict(first[0]) if first else {}
        for k in [k for u in us for k in units[u]]:
            arr = big[k[:-1]]
            bufs["g_" + k] = _with_own_slot(arr[int(k[-1])].astype(WIRE_DTYPE), me)
        streams = ([first[1]] if first else []) + [s for u in us for s in g_streams[u][:2]]
        bufs, sems, token = _comm_call("gather_start_" + "_".join(map(str, us)), bufs, start=streams, after=after)
        if first:
            first_out, sems = ({k: bufs[k] for k in first[0]}, sems[0]), sems[1:]
        for n, u in enumerate(us):
            g_state[u] = dict(bufs={"g_" + k: bufs["g_" + k] for k in units[u]},
                              sems=sems[2 * n:2 * n + 2])
        return (token, first_out) if first else token

    def gather_pass(u, after):
        st = g_state[u]
        to_chips, _, pass_on = g_streams[u]
        st["bufs"], (st["pass_sems"],), _ = _comm_call(f"gather_pass_{u}", st["bufs"], start=[pass_on],
                                                       wait=[(to_chips, st["sems"][0])], after=after)

    def gather_done(u, after=None):
        st = g_state[u]
        _, to_sibling, pass_on = g_streams[u]
        bufs, _, _ = _comm_call(f"gather_done_{u}", st["bufs"], after=after,
                                wait=[(to_sibling, st["sems"][1]), (pass_on, st["pass_sems"])])
        return {k: bufs["g_" + k] for k in units[u]}

    c_stream = _direct_gather_stream("c")
    token, (c_bufs, c_sems) = gather_start([0], None, first=(dict(s_c=c, g_c=_with_own_slot(c, me)), c_stream))
    c_bufs, _, _ = _comm_call("gather_c_done", c_bufs, wait=[(c_stream, c_sems)])
    c_all = c_bufs["g_c"].reshape(N_DEV, D_MODEL)
    b_cols = lax.dynamic_slice_in_dim(b_ada, me * ADA_COLS, ADA_COLS, axis=1)
    ada_part = _ada_fwd(c_all, w_ada, b_cols)
    gather_pass(0, ada_part)
    ada_all = _allgather_small(ada_part.reshape(DEPTH * N_DEV, ADA_COLS), "allgather_ada",
                               after=g_state[0]["bufs"]["g_win0"])
    ada = lax.dynamic_index_in_dim(ada_all.reshape(N_DEV, DEPTH, N_DEV, ADA_COLS), me, axis=2, keepdims=False)
    ada = jnp.transpose(ada, (1, 0, 2)).reshape(DEPTH, 3 * D_MODEL)
    shift = [ada[l:l + 1, 0:D_MODEL] for l in range(DEPTH)]
    scale = [ada[l:l + 1, D_MODEL:2 * D_MODEL] for l in range(DEPTH)]
    gate = [ada[l:l + 1, 2 * D_MODEL:] for l in range(DEPTH)]

    lb = _lb_fwd(lb_logits)

    gw = {}
    xs, saved = [x[0]], []
    for l in range(DEPTH):
        h = _prenorm_fwd(xs[l], g_pre[l:l + 1], shift[l], scale[l], tm, f"prenorm_fwd_{l}",
                         after=token if l == 0 else None)
        token = None
        if l == 0:
            gw.update(gather_done(0, h))
            token = gather_start([1, 2], gw["win0"])
        else:
            gw.update(gather_done(2, h))
        z = _in_proj(h, gw[f"win{l}"], min(1024, seq), f"in_proj_{l}", after=token)
        a_in = _pool_fwd(z, pool_w[l], pool_scale[l:l + 1], f"pool_fwd_{l}")
        o, b_in, states, cum_base = _hgrn_fwd(z, lb[l:l + 1], hgrn_norm_g[l:l + 1], f"hgrn_fwd_{l}")
        if l == 0:
            gather_pass(1, b_in)
            gw.update(gather_done(1))
        who_l = gw[f"who{l}"].reshape(D_MODEL, D_MODEL)
        wout_l = gw[f"wout{l}"].reshape(D_MODEL, D_MODEL)
        last = l == DEPTH - 1
        ba, bb, merged, y, *out = _merge_fwd(a_in, b_in, z, xs[l], gw[f"wpo{l}"], who_l, wout_l, gate[l],
                                             g_post[l:l + 1], tm_merge, f"merge_fwd_{l}",
                                             target=loss_target[0] if last else None)
        if last:
            dx, loss_part = out
        else:
            xs.append(out[0])
            gather_pass(2, out[0])
        saved.append((h, z, a_in, o, b_in, states, cum_base, ba, bb, merged, y, who_l, wout_l))


    chips = _other_chips(pos)
    pair_idx = jnp.stack([_dev_index(cx, cy, pos[2]) for cx, cy in chips] + [me]).astype(jnp.int32)
    pair_rows = dict(win=512, wpo=POOL_WIDTH, who=HEAD_DIM, wout=HEAD_DIM)

    def scatter_pair_start(u, grads):
        keys = list(grads)
        pair, to_chips = _scatter_streams(keys)
        bufs = {}
        for k in keys:
            bufs["g_" + k] = grads[k]
            bufs["st_" + k] = lax.empty((4,) + grads[k].shape[1:], WIRE_DTYPE)
        bufs, (sems,), token = _comm_call(f"scatter_pair_start_{u}", bufs, start=[pair])
        return dict(u=u, keys=keys, pair=pair, to_chips=to_chips, bufs=bufs, sems=sems, token=token)

    def scatter_pair_finish(st, after):
        u, keys = st["u"], st["keys"]
        bufs, _, _ = _comm_call(f"scatter_pair_done_{u}", st["bufs"], wait=[(st["pair"], st["sems"])], after=after)
        bufs2 = {}
        for k in keys:
            bufs2["ps_" + k] = _pair_sum(bufs["g_" + k], bufs["st_" + k], pair_idx, bufs["g_" + k].shape[1],
                                         f"pair_sum_{k}")
            bufs2["ld_" + k] = lax.empty((3,) + bufs["g_" + k].shape[1:], WIRE_DTYPE)
        st.update(bufs=bufs2)

    def scatter_chips_start(st, after=None):
        bufs2, (sems,), token = _comm_call(f"scatter_chips_start_{st['u']}", st["bufs"], start=[st["to_chips"]],
                                           after=after)
        st.update(bufs=bufs2, sems=sems, token=token)

    def scatter_finish(st, after):
        bufs, _, _ = _comm_call(f"scatter_chips_done_{st['u']}", st["bufs"], wait=[(st["to_chips"], st["sems"])],
                                after=after)
        return {k: [(bufs["ps_" + k], 3), (bufs["ld_" + k], 0), (bufs["ld_" + k], 1), (bufs["ld_" + k], 2)]
                for k in st["keys"]}

    moments = dict(win=(m_w_in, v_w_in), wpo=(m_w_pool_o, v_w_pool_o), who=(m_w_hgrn_o, v_w_hgrn_o),
                   wout=(m_w_out, v_w_out))
    big_out = {}

    def finish_unit(unit, after):
        for k, contribs in scatter_finish(scat[unit], after).items():
            wname, l = k[:-1], int(k[-1])
            big_out[wname] = _adamw_layer(big[wname], moments[wname][0], moments[wname][1], contribs, l,
                                          pair_rows[wname], f"adamw_{k}", prev=big_out.get(wname))
            after = big_out[wname][0]
        return after

    d_ada, small, scat = [None] * DEPTH, [None] * DEPTH, {}
    for l in reversed(range(DEPTH)):
        h, z, a_in, o, b_in, states, cum_base, ba, bb, merged, y, who_l, wout_l = saved[l]
        dy, dba, dbb, da_in, db_in, dz, acc_post = _merge_bwd(
            dx, y, ba, bb, z, gw[f"wpo{l}"], who_l, wout_l, gate[l], g_post[l:l + 1],
            lax.empty((seq, IN_WIDTH), MXU_DTYPE), tm_merge, f"merge_bwd_{l}")
        g_out, g_ho, g_po = _grad_out_weights(merged, dy, b_in, dbb, a_in, dba, f"grad_out_weights_{l}")
        g_small = {f"wout{l}": g_out.reshape(N_DEV, HEAD_DIM, D_MODEL),
                   f"who{l}": g_ho.reshape(N_DEV, HEAD_DIM, D_MODEL), f"wpo{l}": g_po}
        st_small = scat["small0"] = scatter_pair_start("small0", g_small) if l == 0 else None
        dz, dlb, dgn = _hgrn_bwd(db_in, z, o, states, cum_base, lb[l:l + 1], hgrn_norm_g[l:l + 1], dz, f"hgrn_bwd_{l}",
                                 after=st_small and st_small["token"])
        if l == 0:
            scatter_pair_finish(st_small, dlb)
            scatter_chips_start(st_small)
        dz, dpw, dps = _pool_bwd(da_in, z, pool_w[l], pool_scale[l:l + 1], dz, f"pool_bwd_{l}",
                                 after=st_small and st_small["token"])
        small[l] = dict(g_post=acc_post[1], pool_w=dpw, pool_scale=dps[0], lb_logits=dlb[0], hgrn_norm_g=dgn[0])
        token = None
        if l == 0:
            parts = {name: jnp.stack([small[0][name], small[1][name]]) for name in small[0]}
            parts.update(b_ada=[None, d_ada[1]], g_pre=[None, small[1]["g_pre"]])
            sg_stream = _direct_gather_stream("sg")
            early = _pack_small(parts, 2)
            sg_bufs, (sg_sems,), token = _comm_call(
                "small_grads_start", dict(s_sg=early, g_sg=_with_own_slot(early, me)), start=[sg_stream])
        g_win = {f"win{l}": _in_proj_dw(h, dz, f"grad_w_in_{l}", after=token)}
        st_win = scat[f"win{l}"] = scatter_pair_start(f"win{l}", g_win if l == 0 else {**g_small, **g_win})
        if l > 0:
            dh = _in_proj_dh(dz, gw[f"win{l}"], seq, f"in_proj_dh_{l}", after=st_win["token"])
            scatter_pair_finish(st_win, dh)
            scatter_chips_start(st_win)
        else:
            after = finish_unit("win1", st_win["token"])
            scatter_pair_finish(st_win, after)
            scatter_chips_start(st_win)
            after = finish_unit("small0", st_win["token"])
            dh = _in_proj_dh(dz, gw[f"win{l}"], seq, f"in_proj_dh_{l}", after=after)
        dx, acc_pre = _prenorm_bwd(xs[l], dh, dx, g_pre[l:l + 1], scale[l], tm, f"prenorm_bwd_{l}",
                                   after=st_win["token"])
        d_ada[l] = jnp.concatenate([acc_pre[0], acc_pre[1], acc_post[0]])
        small[l]["g_pre"] = acc_pre[2]
    grad_x = dx[None]

    sg_bufs, _, _ = _comm_call("small_grads_done", sg_bufs, wait=[(sg_stream, sg_sems)], after=dx)
    g_early = sg_bufs["g_sg"]

    def w_ada_layer(l, d_rows, prev):
        d_cols = lax.dynamic_slice_in_dim(d_rows.reshape(N_DEV, 3 * D_MODEL), me * ADA_COLS, ADA_COLS, axis=1)
        return _adamw_layer(w_ada, m_w_ada, v_w_ada, [(_ada_bwd(c_all, d_cols, f"ada_bwd_{l}"), 0)], l, 512,
                            f"adamw_w_ada{l}", prev=prev)

    ada_out = w_ada_layer(1, g_early[:, 0:24, :], None)

    parts = dict(b_ada=[d_ada[0]], g_pre=[small[0]["g_pre"]])
    late = jnp.concatenate([_pack_small(parts, 0, 2), jnp.broadcast_to(loss_part, (8, 128))], axis=0)
    g_late = _allgather_small(late, "allgather_late_grads", after=ada_out[0])
    loss = jnp.sum(g_late[:, SMALL_LATE_ROWS, 0])
    small_names = list(dict.fromkeys(name for name, _, _ in _SMALL_ROWS))
    weights = dict(b_ada=b_ada, g_pre=g_pre, g_post=g_post, pool_w=pool_w, pool_scale=pool_scale,
                   lb_logits=lb_logits, hgrn_norm_g=hgrn_norm_g)
    m_small = dict(b_ada=m_b_ada, g_pre=m_g_pre, g_post=m_g_post, pool_w=m_pool_w, pool_scale=m_pool_scale,
                   lb_logits=m_lb_logits, hgrn_norm_g=m_hgrn_norm_g)
    v_small = dict(b_ada=v_b_ada, g_pre=v_g_pre, g_post=v_g_post, pool_w=v_pool_w, pool_scale=v_pool_scale,
                   lb_logits=v_lb_logits, hgrn_norm_g=v_hgrn_norm_g)
    shapes = {name: weights[name].shape for name in small_names}
    small_out = _adamw_small(_pack_small(weights), _pack_small(m_small), _pack_small(v_small), g_late, g_early,
                             shapes)

    ada_out = w_ada_layer(0, g_late[:, 0:24, :], ada_out)
    finish_unit("win0", ada_out[1][0, 0:8, 0:128] + small_out[1]["pool_scale"][0:1, 0:128])

    def leaf(kind):
        s = small_out[kind]
        return (ada_out[kind], s["b_ada"], s["g_pre"], s["g_post"], big_out["win"][kind], s["pool_w"], s["pool_scale"],
                s["lb_logits"], s["hgrn_norm_g"], big_out["wpo"][kind], big_out["who"][kind], big_out["wout"][kind])

    return (loss, grad_x) + leaf(0) + leaf(1) + leaf(2) + leaf(3)
```
